```python
import math
import jax
import jax.numpy as jnp
from jax import lax
import numpy as np

D_MODEL = 2048
BATCH = 8
SEQ = 2048
DEPTH = 2

FOX_HEADS = 8
FOX_HEAD_DIM = 128
FOX_WIDTH = FOX_HEADS * FOX_HEAD_DIM
SSM_WIDTH = D_MODEL - FOX_WIDTH
SSM_GROUP = 16
SSM_GROUPS = SSM_WIDTH // SSM_GROUP
SSM_STATE = 64
EVEN_IN = 3 * FOX_WIDTH + FOX_HEADS + SSM_WIDTH
FORGET_BIAS_INIT = 2.0
DT_MIN = 1e-3
DT_MAX = 1e-1
SWA_HEADS = 32
SWA_KV_HEADS = 4
SWA_HEAD_DIM = 64
SWA_GROUPS = SWA_HEADS // SWA_KV_HEADS
SWA_WINDOW = 128
ODD_IN = (SWA_HEADS + 2 * SWA_KV_HEADS) * SWA_HEAD_DIM
ROPE_DIM = SWA_HEAD_DIM // 4
ROPE_THETA = 500000.0
Q_BLOCK = 128
D_FF = 5504
CONV_WIDTH = 3
LN_EPS = 1e-5
DEEPNORM_ALPHA = (2.0 * DEPTH) ** 0.25
DEEPNORM_BETA = (8.0 * DEPTH) ** -0.25
N_EVEN = (DEPTH + 1) // 2
N_ODD = DEPTH // 2

kernel_name = 'hybrid_fox_s5_swa_deepnorm'


def _layer_norm(x, g, b):
    x32 = x.astype(jnp.float32)
    mu = jnp.mean(x32, axis=-1, keepdims=True)
    var = jnp.mean(jnp.square(x32 - mu), axis=-1, keepdims=True)
    y = (x32 - mu) * lax.rsqrt(var + LN_EPS)
    return (y * g.astype(jnp.float32) + b.astype(jnp.float32)).astype(x.dtype)


def _forgetting_attention(q, k, v, f_logit):
    s_len = q.shape[1]
    dh = q.shape[-1]
    scale = 1.0 / math.sqrt(dh)
    log_f = jax.nn.log_sigmoid(f_logit.astype(jnp.float32))
    c = jnp.cumsum(log_f, axis=1).transpose(0, 2, 1)
    outs = []
    for start in range(0, s_len, Q_BLOCK):
        end = start + Q_BLOCK
        s = jnp.einsum('bqhd,bkhd->bhqk', q[:, start:end], k[:, :end]).astype(jnp.float32) * scale
        s = s + c[:, :, start:end, None] - c[:, :, None, :end]
        causal = jnp.arange(start, end)[:, None] >= jnp.arange(end)[None, :]
        s = jnp.where(causal, s, -jnp.inf)
        p = jax.nn.softmax(s, axis=-1).astype(v.dtype)
        outs.append(jnp.einsum('bhqk,bkhd->bqhd', p, v[:, :end]))
    return jnp.concatenate(outs, axis=1)


def _s5_scan(u, lam_re, lam_im, log_step, b_re, b_im, c_re, c_im, d_skip):
    u32 = u.astype(jnp.float32)
    lr = lam_re.astype(jnp.float32)
    li = lam_im.astype(jnp.float32)
    dt = jnp.exp(log_step.astype(jnp.float32))[:, None]
    mag = jnp.exp(lr * dt)
    a_re = mag * jnp.cos(li * dt)
    a_im = mag * jnp.sin(li * dt)
    den = lr * lr + li * li
    xr = a_re - 1.0
    xi = a_im
    g_re = (xr * lr + xi * li) / den
    g_im = (xi * lr - xr * li) / den
    br = b_re.astype(jnp.float32)
    bi = b_im.astype(jnp.float32)
    bb_re = g_re[..., None] * br - g_im[..., None] * bi
    bb_im = g_re[..., None] * bi + g_im[..., None] * br
    bu_re = jnp.einsum('gpc,bsgc->bsgp', bb_re, u32)
    bu_im = jnp.einsum('gpc,bsgc->bsgp', bb_im, u32)

    def combine(e1, e2):
        a1r, a1i, b1r, b1i = e1
        a2r, a2i, b2r, b2i = e2
        return (a2r * a1r - a2i * a1i,
                a2r * a1i + a2i * a1r,
                a2r * b1r - a2i * b1i + b2r,
                a2r * b1i + a2i * b1r + b2i)

    s_len = u.shape[1]
    a_re_t = jnp.broadcast_to(a_re[None, None], (1, s_len) + a_re.shape)
    a_im_t = jnp.broadcast_to(a_im[None, None], (1, s_len) + a_im.shape)
    _, _, h_re, h_im = lax.associative_scan(combine, (a_re_t, a_im_t, bu_re, bu_im), axis=1)
    y = (jnp.einsum('gcp,bsgp->bsgc', c_re.astype(jnp.float32), h_re)
         - jnp.einsum('gcp,bsgp->bsgc', c_im.astype(jnp.float32), h_im)
         + d_skip.astype(jnp.float32) * u32)
    return y.astype(u.dtype)


def _partial_rope(x, positions):
    half = ROPE_DIM // 2
    inv_freq = ROPE_THETA ** (-jnp.arange(half, dtype=jnp.float32) / half)
    ang = positions.astype(jnp.float32)[..., None] * inv_freq
    cos = jnp.cos(ang)[:, :, None, :]
    sin = jnp.sin(ang)[:, :, None, :]
    xr = x[..., :ROPE_DIM].astype(jnp.float32)
    x1 = xr[..., :half]
    x2 = xr[..., half:]
    rot = jnp.concatenate([x1 * cos - x2 * sin, x2 * cos + x1 * sin], axis=-1).astype(x.dtype)
    return jnp.concatenate([rot, x[..., ROPE_DIM:]], axis=-1)


def _sliding_window_attention(q, k, v, sinks):
    bsz, s_len, _, dh = q.shape
    nb = s_len // Q_BLOCK
    scale = 1.0 / math.sqrt(dh)
    qb = q.reshape(bsz, nb, Q_BLOCK, SWA_KV_HEADS, SWA_GROUPS, dh)
    kb = k.reshape(bsz, nb, Q_BLOCK, SWA_KV_HEADS, dh)
    vb = v.reshape(bsz, nb, Q_BLOCK, SWA_KV_HEADS, dh)
    pad = ((0, 0), (1, 0), (0, 0), (0, 0), (0, 0))
    kk = jnp.concatenate([jnp.pad(kb, pad)[:, :-1], kb], axis=2)
    vv = jnp.concatenate([jnp.pad(vb, pad)[:, :-1], vb], axis=2)
    s = jnp.einsum('bnqhgd,bnkhd->bnhgqk', qb, kk).astype(jnp.float32) * scale
    qi = jnp.arange(Q_BLOCK)[:, None]
    kj = jnp.arange(2 * Q_BLOCK)[None, :]
    rel = Q_BLOCK + qi - kj
    band = (rel >= 0) & (rel < SWA_WINDOW)
    exists = (jnp.arange(nb)[:, None, None] > 0) | (kj[None] >= Q_BLOCK)
    valid = band[None] & exists
    s = jnp.where(valid[None, :, None, None], s, -jnp.inf)
    sink = jnp.broadcast_to(
        sinks.astype(jnp.float32).reshape(SWA_KV_HEADS, SWA_GROUPS)[None, None, :, :, None, None],
        s.shape[:-1] + (1,))
    p = jax.nn.softmax(jnp.concatenate([s, sink], axis=-1), axis=-1)[..., :-1]
    o = jnp.einsum('bnhgqk,bnkhd->bnqhgd', p.astype(v.dtype), vv)
    return o.reshape(bsz, s_len, SWA_HEADS * dh)


def _even_mixer(x, w_in, b_f, lam_re, lam_im, log_step, b_re, b_im, c_re, c_im, d_skip, w_glu, w_out):
    bsz, s_len, _ = x.shape
    proj = jnp.einsum('bsd,de->bse', x, w_in)
    q, k, v, f_logit, u = jnp.split(
        proj, [FOX_WIDTH, 2 * FOX_WIDTH, 3 * FOX_WIDTH, 3 * FOX_WIDTH + FOX_HEADS], axis=-1)
    hs = (bsz, s_len, FOX_HEADS, FOX_HEAD_DIM)
    fox = _forgetting_attention(q.reshape(hs), k.reshape(hs), v.reshape(hs), f_logit + b_f)
    fox = fox.reshape(bsz, s_len, FOX_WIDTH)
    y = _s5_scan(u.reshape(bsz, s_len, SSM_GROUPS, SSM_GROUP),
                 lam_re, lam_im, log_step, b_re, b_im, c_re, c_im, d_skip)
    z = jnp.einsum('bsc,ce->bse', jax.nn.gelu(y.reshape(bsz, s_len, SSM_WIDTH)), w_glu)
    ssm = z[..., :SSM_WIDTH] * jax.nn.sigmoid(z[..., SSM_WIDTH:])
    return jnp.einsum('bsc,cd->bsd', jnp.concatenate([fox, ssm], axis=-1), w_out)


def _odd_mixer(x, positions, w_in, sinks, w_out):
    bsz, s_len, _ = x.shape
    proj = jnp.einsum('bsd,de->bse', x, w_in)
    qw = SWA_HEADS * SWA_HEAD_DIM
    kw = SWA_KV_HEADS * SWA_HEAD_DIM
    q, k, v = jnp.split(proj, [qw, qw + kw], axis=-1)
    q = _partial_rope(q.reshape(bsz, s_len, SWA_HEADS, SWA_HEAD_DIM), positions)
    k = _partial_rope(k.reshape(bsz, s_len, SWA_KV_HEADS, SWA_HEAD_DIM), positions)
    v = v.reshape(bsz, s_len, SWA_KV_HEADS, SWA_HEAD_DIM)
    o = _sliding_window_attention(q, k, v, sinks)
    return jnp.einsum('bsc,cd->bsd', o, w_out)


def _conv_ffn(x, w_up, conv_w, conv_b, w_down):
    s_len = x.shape[1]
    h = jnp.einsum('bsd,df->bsf', x, w_up)
    hp = jnp.pad(h, ((0, 0), (CONV_WIDTH - 1, 0), (0, 0)))
    h = conv_b + sum(conv_w[t] * hp[:, t:t + s_len] for t in range(CONV_WIDTH))
    gate = h[..., :D_FF]
    val = h[..., D_FF:]
    return jnp.einsum('bsf,fd->bsd', jax.nn.silu(gate) * val, w_down)


def _fwd_setup_inputs(seed: int = 0) -> dict:
    key = jax.random.key(seed)
    ks = jax.random.split(key, 26)
    f32 = jnp.float32

    def nrm(k, shape, scale):
        return jax.random.normal(k, shape, f32) * scale

    x = nrm(ks[0], (BATCH, SEQ, D_MODEL), 1.0)
    offs = jax.random.randint(ks[1], (BATCH, 1), 0, 1024, dtype=jnp.int32)
    positions = (offs + jnp.arange(SEQ, dtype=jnp.int32)[None, :]).astype(jnp.int32)

    ev_w_in = nrm(ks[2], (N_EVEN, D_MODEL, EVEN_IN), D_MODEL ** -0.5)
    ev_w_in = ev_w_in.at[:, :, 2 * FOX_WIDTH:3 * FOX_WIDTH].multiply(DEEPNORM_BETA)
    ev_b_f = FORGET_BIAS_INIT + nrm(ks[3], (N_EVEN, FOX_HEADS), 0.1)
    ev_lambda_re = -0.5 + nrm(ks[4], (N_EVEN, SSM_GROUPS, SSM_STATE), 0.01)
    ev_lambda_im = (math.pi * jnp.arange(SSM_STATE, dtype=f32))[None, None, :] + nrm(
        ks[5], (N_EVEN, SSM_GROUPS, SSM_STATE), 0.01)
    ev_log_step = jax.random.uniform(ks[6], (N_EVEN, SSM_GROUPS), f32,
                                     minval=math.log(DT_MIN), maxval=math.log(DT_MAX))
    ev_ssm_b_re = nrm(ks[7], (N_EVEN, SSM_GROUPS, SSM_STATE, SSM_GROUP), (2 * SSM_GROUP) ** -0.5)
    ev_ssm_b_im = nrm(ks[8], (N_EVEN, SSM_GROUPS, SSM_STATE, SSM_GROUP), (2 * SSM_GROUP) ** -0.5)
    ev_ssm_c_re = nrm(ks[9], (N_EVEN, SSM_GROUPS, SSM_GROUP, SSM_STATE), (2 * SSM_STATE) ** -0.5)
    ev_ssm_c_im = nrm(ks[10], (N_EVEN, SSM_GROUPS, SSM_GROUP, SSM_STATE), (2 * SSM_STATE) ** -0.5)
    ev_ssm_d = nrm(ks[11], (N_EVEN, SSM_GROUPS, SSM_GROUP), 1.0)
    ev_w_glu = nrm(ks[12], (N_EVEN, SSM_WIDTH, 2 * SSM_WIDTH), SSM_WIDTH ** -0.5)
    ev_w_out = nrm(ks[13], (N_EVEN, D_MODEL, D_MODEL), D_MODEL ** -0.5 * DEEPNORM_BETA)

    od_w_in = nrm(ks[14], (N_ODD, D_MODEL, ODD_IN), D_MODEL ** -0.5)
    v_start = (SWA_HEADS + SWA_KV_HEADS) * SWA_HEAD_DIM
    od_w_in = od_w_in.at[:, :, v_start:].multiply(DEEPNORM_BETA)
    od_sinks = nrm(ks[15], (N_ODD, SWA_HEADS), 0.1)
    od_w_out = nrm(ks[16], (N_ODD, SWA_HEADS * SWA_HEAD_DIM, D_MODEL),
                   (SWA_HEADS * SWA_HEAD_DIM) ** -0.5 * DEEPNORM_BETA)

    ln_mix_g = 1.0 + nrm(ks[17], (DEPTH, D_MODEL), 0.02)
    ln_mix_b = nrm(ks[18], (DEPTH, D_MODEL), 0.02)
    ffn_w_up = nrm(ks[19], (DEPTH, D_MODEL, 2 * D_FF), D_MODEL ** -0.5)
    ffn_conv_w = nrm(ks[20], (DEPTH, CONV_WIDTH, 2 * D_FF), CONV_WIDTH ** -0.5)
    ffn_conv_b = nrm(ks[21], (DEPTH, 2 * D_FF), 0.02)
    ffn_w_down = nrm(ks[22], (DEPTH, D_FF, D_MODEL), D_FF ** -0.5 * DEEPNORM_BETA)
    ln_ffn_g = 1.0 + nrm(ks[23], (DEPTH, D_MODEL), 0.02)
    ln_ffn_b = nrm(ks[24], (DEPTH, D_MODEL), 0.02)

    return {'x': x, 'positions': positions,
            'ev_w_in': ev_w_in, 'ev_b_f': ev_b_f,
            'ev_lambda_re': ev_lambda_re, 'ev_lambda_im': ev_lambda_im, 'ev_log_step': ev_log_step,
            'ev_ssm_b_re': ev_ssm_b_re, 'ev_ssm_b_im': ev_ssm_b_im,
            'ev_ssm_c_re': ev_ssm_c_re, 'ev_ssm_c_im': ev_ssm_c_im, 'ev_ssm_d': ev_ssm_d,
            'ev_w_glu': ev_w_glu, 'ev_w_out': ev_w_out,
            'od_w_in': od_w_in, 'od_sinks': od_sinks, 'od_w_out': od_w_out,
            'ln_mix_g': ln_mix_g, 'ln_mix_b': ln_mix_b,
            'ffn_w_up': ffn_w_up, 'ffn_conv_w': ffn_conv_w, 'ffn_conv_b': ffn_conv_b,
            'ffn_w_down': ffn_w_down, 'ln_ffn_g': ln_ffn_g, 'ln_ffn_b': ln_ffn_b}


def _fwd_reference(x, positions, ev_w_in, ev_b_f, ev_lambda_re, ev_lambda_im, ev_log_step,
              ev_ssm_b_re, ev_ssm_b_im, ev_ssm_c_re, ev_ssm_c_im, ev_ssm_d, ev_w_glu, ev_w_out,
              od_w_in, od_sinks, od_w_out, ln_mix_g, ln_mix_b,
              ffn_w_up, ffn_conv_w, ffn_conv_b, ffn_w_down, ln_ffn_g, ln_ffn_b):
    for i in range(DEPTH):
        j = i // 2
        if i % 2 == 0:
            mix = _even_mixer(x, ev_w_in[j], ev_b_f[j], ev_lambda_re[j], ev_lambda_im[j],
                              ev_log_step[j], ev_ssm_b_re[j], ev_ssm_b_im[j], ev_ssm_c_re[j],
                              ev_ssm_c_im[j], ev_ssm_d[j], ev_w_glu[j], ev_w_out[j])
        else:
            mix = _odd_mixer(x, positions, od_w_in[j], od_sinks[j], od_w_out[j])
        x = _layer_norm(DEEPNORM_ALPHA * x + mix, ln_mix_g[i], ln_mix_b[i])
        ffn = _conv_ffn(x, ffn_w_up[i], ffn_conv_w[i], ffn_conv_b[i], ffn_w_down[i])
        x = _layer_norm(DEEPNORM_ALPHA * x + ffn, ln_ffn_g[i], ln_ffn_b[i])
    return x


import jax as _jax
import jax.numpy as _jnp

TWIN_FORMAT = 'train_step'
FWD_PARAMS = ['x', 'positions', 'ev_w_in', 'ev_b_f', 'ev_lambda_re', 'ev_lambda_im', 'ev_log_step', 'ev_ssm_b_re', 'ev_ssm_b_im', 'ev_ssm_c_re', 'ev_ssm_c_im', 'ev_ssm_d', 'ev_w_glu', 'ev_w_out', 'od_w_in', 'od_sinks', 'od_w_out', 'ln_mix_g', 'ln_mix_b', 'ffn_w_up', 'ffn_conv_w', 'ffn_conv_b', 'ffn_w_down', 'ln_ffn_g', 'ln_ffn_b']
TWIN_WEIGHTS = ['ev_w_in', 'ev_b_f', 'ev_lambda_re', 'ev_lambda_im', 'ev_log_step', 'ev_ssm_b_re', 'ev_ssm_b_im', 'ev_ssm_c_re', 'ev_ssm_c_im', 'ev_ssm_d', 'ev_w_glu', 'ev_w_out', 'od_w_in', 'od_sinks', 'od_w_out', 'ln_mix_g', 'ln_mix_b', 'ffn_w_up', 'ffn_conv_w', 'ffn_conv_b', 'ffn_w_down', 'ln_ffn_g', 'ln_ffn_b']
TWIN_DIFF_INPUT = 'x'
TWIN_INPUTS = ['x', 'positions', 'ev_w_in', 'ev_b_f', 'ev_lambda_re', 'ev_lambda_im', 'ev_log_step', 'ev_ssm_b_re', 'ev_ssm_b_im', 'ev_ssm_c_re', 'ev_ssm_c_im', 'ev_ssm_d', 'ev_w_glu', 'ev_w_out', 'od_w_in', 'od_sinks', 'od_w_out', 'ln_mix_g', 'ln_mix_b', 'ffn_w_up', 'ffn_conv_w', 'ffn_conv_b', 'ffn_w_down', 'ln_ffn_g', 'ln_ffn_b', 'loss_target', 'm_ev_w_in', 'm_ev_b_f', 'm_ev_lambda_re', 'm_ev_lambda_im', 'm_ev_log_step', 'm_ev_ssm_b_re', 'm_ev_ssm_b_im', 'm_ev_ssm_c_re', 'm_ev_ssm_c_im', 'm_ev_ssm_d', 'm_ev_w_glu', 'm_ev_w_out', 'm_od_w_in', 'm_od_sinks', 'm_od_w_out', 'm_ln_mix_g', 'm_ln_mix_b', 'm_ffn_w_up', 'm_ffn_conv_w', 'm_ffn_conv_b', 'm_ffn_w_down', 'm_ln_ffn_g', 'm_ln_ffn_b', 'v_ev_w_in', 'v_ev_b_f', 'v_ev_lambda_re', 'v_ev_lambda_im', 'v_ev_log_step', 'v_ev_ssm_b_re', 'v_ev_ssm_b_im', 'v_ev_ssm_c_re', 'v_ev_ssm_c_im', 'v_ev_ssm_d', 'v_ev_w_glu', 'v_ev_w_out', 'v_od_w_in', 'v_od_sinks', 'v_od_w_out', 'v_ln_mix_g', 'v_ln_mix_b', 'v_ffn_w_up', 'v_ffn_conv_w', 'v_ffn_conv_b', 'v_ffn_w_down', 'v_ln_ffn_g', 'v_ln_ffn_b']
TWIN_OUTPUTS = ['loss', 'grad_x', 'grad_ev_w_in', 'grad_ev_b_f', 'grad_ev_lambda_re', 'grad_ev_lambda_im', 'grad_ev_log_step', 'grad_ev_ssm_b_re', 'grad_ev_ssm_b_im', 'grad_ev_ssm_c_re', 'grad_ev_ssm_c_im', 'grad_ev_ssm_d', 'grad_ev_w_glu', 'grad_ev_w_out', 'grad_od_w_in', 'grad_od_sinks', 'grad_od_w_out', 'grad_ln_mix_g', 'grad_ln_mix_b', 'grad_ffn_w_up', 'grad_ffn_conv_w', 'grad_ffn_conv_b', 'grad_ffn_w_down', 'grad_ln_ffn_g', 'grad_ln_ffn_b', 'delta_ev_w_in', 'delta_ev_b_f', 'delta_ev_lambda_re', 'delta_ev_lambda_im', 'delta_ev_log_step', 'delta_ev_ssm_b_re', 'delta_ev_ssm_b_im', 'delta_ev_ssm_c_re', 'delta_ev_ssm_c_im', 'delta_ev_ssm_d', 'delta_ev_w_glu', 'delta_ev_w_out', 'delta_od_w_in', 'delta_od_sinks', 'delta_od_w_out', 'delta_ln_mix_g', 'delta_ln_mix_b', 'delta_ffn_w_up', 'delta_ffn_conv_w', 'delta_ffn_conv_b', 'delta_ffn_w_down', 'delta_ln_ffn_g', 'delta_ln_ffn_b', 'new_m_ev_w_in', 'new_m_ev_b_f', 'new_m_ev_lambda_re', 'new_m_ev_lambda_im', 'new_m_ev_log_step', 'new_m_ev_ssm_b_re', 'new_m_ev_ssm_b_im', 'new_m_ev_ssm_c_re', 'new_m_ev_ssm_c_im', 'new_m_ev_ssm_d', 'new_m_ev_w_glu', 'new_m_ev_w_out', 'new_m_od_w_in', 'new_m_od_sinks', 'new_m_od_w_out', 'new_m_ln_mix_g', 'new_m_ln_mix_b', 'new_m_ffn_w_up', 'new_m_ffn_conv_w', 'new_m_ffn_conv_b', 'new_m_ffn_w_down', 'new_m_ln_ffn_g', 'new_m_ln_ffn_b', 'new_v_ev_w_in', 'new_v_ev_b_f', 'new_v_ev_lambda_re', 'new_v_ev_lambda_im', 'new_v_ev_log_step', 'new_v_ev_ssm_b_re', 'new_v_ev_ssm_b_im', 'new_v_ev_ssm_c_re', 'new_v_ev_ssm_c_im', 'new_v_ev_ssm_d', 'new_v_ev_w_glu', 'new_v_ev_w_out', 'new_v_od_w_in', 'new_v_od_sinks', 'new_v_od_w_out', 'new_v_ln_mix_g', 'new_v_ln_mix_b', 'new_v_ffn_w_up', 'new_v_ffn_conv_w', 'new_v_ffn_conv_b', 'new_v_ffn_w_down', 'new_v_ln_ffn_g', 'new_v_ln_ffn_b']
TWIN_LEAF_KINDS = {'loss': 'loss', 'grad_x': 'grad_x', 'grad_ev_w_in': 'grad_w', 'grad_ev_b_f': 'grad_w', 'grad_ev_lambda_re': 'grad_w', 'grad_ev_lambda_im': 'grad_w', 'grad_ev_log_step': 'grad_w', 'grad_ev_ssm_b_re': 'grad_w', 'grad_ev_ssm_b_im': 'grad_w', 'grad_ev_ssm_c_re': 'grad_w', 'grad_ev_ssm_c_im': 'grad_w', 'grad_ev_ssm_d': 'grad_w', 'grad_ev_w_glu': 'grad_w', 'grad_ev_w_out': 'grad_w', 'grad_od_w_in': 'grad_w', 'grad_od_sinks': 'grad_w', 'grad_od_w_out': 'grad_w', 'grad_ln_mix_g': 'grad_w', 'grad_ln_mix_b': 'grad_w', 'grad_ffn_w_up': 'grad_w', 'grad_ffn_conv_w': 'grad_w', 'grad_ffn_conv_b': 'grad_w', 'grad_ffn_w_down': 'grad_w', 'grad_ln_ffn_g': 'grad_w', 'grad_ln_ffn_b': 'grad_w', 'delta_ev_w_in': 'delta_w', 'delta_ev_b_f': 'delta_w', 'delta_ev_lambda_re': 'delta_w', 'delta_ev_lambda_im': 'delta_w', 'delta_ev_log_step': 'delta_w', 'delta_ev_ssm_b_re': 'delta_w', 'delta_ev_ssm_b_im': 'delta_w', 'delta_ev_ssm_c_re': 'delta_w', 'delta_ev_ssm_c_im': 'delta_w', 'delta_ev_ssm_d': 'delta_w', 'delta_ev_w_glu': 'delta_w', 'delta_ev_w_out': 'delta_w', 'delta_od_w_in': 'delta_w', 'delta_od_sinks': 'delta_w', 'delta_od_w_out': 'delta_w', 'delta_ln_mix_g': 'delta_w', 'delta_ln_mix_b': 'delta_w', 'delta_ffn_w_up': 'delta_w', 'delta_ffn_conv_w': 'delta_w', 'delta_ffn_conv_b': 'delta_w', 'delta_ffn_w_down': 'delta_w', 'delta_ln_ffn_g': 'delta_w', 'delta_ln_ffn_b': 'delta_w', 'new_m_ev_w_in': 'new_m', 'new_m_ev_b_f': 'new_m', 'new_m_ev_lambda_re': 'new_m', 'new_m_ev_lambda_im': 'new_m', 'new_m_ev_log_step': 'new_m', 'new_m_ev_ssm_b_re': 'new_m', 'new_m_ev_ssm_b_im': 'new_m', 'new_m_ev_ssm_c_re': 'new_m', 'new_m_ev_ssm_c_im': 'new_m', 'new_m_ev_ssm_d': 'new_m', 'new_m_ev_w_glu': 'new_m', 'new_m_ev_w_out': 'new_m', 'new_m_od_w_in': 'new_m', 'new_m_od_sinks': 'new_m', 'new_m_od_w_out': 'new_m', 'new_m_ln_mix_g': 'new_m', 'new_m_ln_mix_b': 'new_m', 'new_m_ffn_w_up': 'new_m', 'new_m_ffn_conv_w': 'new_m', 'new_m_ffn_conv_b': 'new_m', 'new_m_ffn_w_down': 'new_m', 'new_m_ln_ffn_g': 'new_m', 'new_m_ln_ffn_b': 'new_m', 'new_v_ev_w_in': 'new_v', 'new_v_ev_b_f': 'new_v', 'new_v_ev_lambda_re': 'new_v', 'new_v_ev_lambda_im': 'new_v', 'new_v_ev_log_step': 'new_v', 'new_v_ev_ssm_b_re': 'new_v', 'new_v_ev_ssm_b_im': 'new_v', 'new_v_ev_ssm_c_re': 'new_v', 'new_v_ev_ssm_c_im': 'new_v', 'new_v_ev_ssm_d': 'new_v', 'new_v_ev_w_glu': 'new_v', 'new_v_ev_w_out': 'new_v', 'new_v_od_w_in': 'new_v', 'new_v_od_sinks': 'new_v', 'new_v_od_w_out': 'new_v', 'new_v_ln_mix_g': 'new_v', 'new_v_ln_mix_b': 'new_v', 'new_v_ffn_w_up': 'new_v', 'new_v_ffn_conv_w': 'new_v', 'new_v_ffn_conv_b': 'new_v', 'new_v_ffn_w_down': 'new_v', 'new_v_ln_ffn_g': 'new_v', 'new_v_ln_ffn_b': 'new_v'}


def _forward(args):
    return _fwd_reference(*[args[k] for k in FWD_PARAMS])


def _output_shape():
    out = _jax.eval_shape(lambda: _forward(_fwd_setup_inputs(0)))
    return out.shape, out.dtype

N_MICROBATCH = 1
ADAM_LR = 0.001
ADAM_B1 = 0.9
ADAM_B2 = 0.999
ADAM_EPS = 1e-08
ADAM_WD = 0.01
ADAM_STEP = 10
PER_EXAMPLE_BATCH_AXIS = {'x': 0, 'positions': 0, 'loss_target': 0}
SHARED_INPUTS = []
_WEIGHT_DTYPES = {'ev_w_in': _jnp.float32, 'ev_b_f': _jnp.float32, 'ev_lambda_re': _jnp.float32, 'ev_lambda_im': _jnp.float32, 'ev_log_step': _jnp.float32, 'ev_ssm_b_re': _jnp.float32, 'ev_ssm_b_im': _jnp.float32, 'ev_ssm_c_re': _jnp.float32, 'ev_ssm_c_im': _jnp.float32, 'ev_ssm_d': _jnp.float32, 'ev_w_glu': _jnp.float32, 'ev_w_out': _jnp.float32, 'od_w_in': _jnp.float32, 'od_sinks': _jnp.float32, 'od_w_out': _jnp.float32, 'ln_mix_g': _jnp.float32, 'ln_mix_b': _jnp.float32, 'ffn_w_up': _jnp.float32, 'ffn_conv_w': _jnp.float32, 'ffn_conv_b': _jnp.float32, 'ffn_w_down': _jnp.float32, 'ln_ffn_g': _jnp.float32, 'ln_ffn_b': _jnp.float32}
MOMENT_SCALE = {'ev_w_in': 6.981455e-03, 'ev_b_f': 5.649912e-02, 'ev_lambda_re': 4.340860e-04, 'ev_lambda_im': 4.366089e-04, 'ev_log_step': 2.168283e-01, 'ev_ssm_b_re': 2.755807e-04, 'ev_ssm_b_im': 2.801556e-04, 'ev_ssm_c_re': 5.532127e-04, 'ev_ssm_c_im': 5.588156e-04, 'ev_ssm_d': 1.130579e-02, 'ev_w_glu': 7.300751e-03, 'ev_w_out': 1.578715e-02, 'od_w_in': 5.014232e-03, 'od_sinks': 1.789401e-03, 'od_w_out': 4.825676e-03, 'ln_mix_g': 2.751255e-01, 'ln_mix_b': 1.383743e-01, 'ffn_w_up': 8.368413e-03, 'ffn_conv_w': 8.499670e-03, 'ffn_conv_b': 8.980597e-03, 'ffn_w_down': 2.702207e-02, 'ln_ffn_g': 5.668640e+00, 'ln_ffn_b': 3.029945e-01}


def _to_microbatches(a, axis):
    t = _jnp.moveaxis(a, axis, 0)
    t = t.reshape((N_MICROBATCH, t.shape[0] // N_MICROBATCH) + t.shape[1:])
    return _jnp.moveaxis(t, 1, axis + 1)


def setup_inputs(seed: int = 0) -> dict:
    inp = _fwd_setup_inputs(seed)
    key = _jax.random.fold_in(_jax.random.key(seed), 7919)
    shape, _ = _output_shape()
    out = dict(inp)
    out["loss_target"] = _jax.random.normal(_jax.random.fold_in(key, 0), shape, _jnp.float32)
    for i, name in enumerate(TWIN_WEIGHTS):
        w = inp[name].astype(_jnp.float32)
        if MOMENT_SCALE is None:
            s = _jnp.sqrt(_jnp.mean(_jnp.square(w)) + 1e-30)
        else:
            s = MOMENT_SCALE[name]
        km, kv = _jax.random.split(_jax.random.fold_in(key, i + 1))
        out[name] = w
        out["m_" + name] = s * _jax.random.normal(km, w.shape, _jnp.float32)
        out["v_" + name] = (s * s) * _jax.random.uniform(kv, w.shape, _jnp.float32, 0.5, 1.5)
    if N_MICROBATCH > 1:
        for name, axis in PER_EXAMPLE_BATCH_AXIS.items():
            out[name] = _to_microbatches(out[name], axis)
    return {'x': out['x'], 'positions': out['positions'], 'ev_w_in': out['ev_w_in'], 'ev_b_f': out['ev_b_f'], 'ev_lambda_re': out['ev_lambda_re'], 'ev_lambda_im': out['ev_lambda_im'], 'ev_log_step': out['ev_log_step'], 'ev_ssm_b_re': out['ev_ssm_b_re'], 'ev_ssm_b_im': out['ev_ssm_b_im'], 'ev_ssm_c_re': out['ev_ssm_c_re'], 'ev_ssm_c_im': out['ev_ssm_c_im'], 'ev_ssm_d': out['ev_ssm_d'], 'ev_w_glu': out['ev_w_glu'], 'ev_w_out': out['ev_w_out'], 'od_w_in': out['od_w_in'], 'od_sinks': out['od_sinks'], 'od_w_out': out['od_w_out'], 'ln_mix_g': out['ln_mix_g'], 'ln_mix_b': out['ln_mix_b'], 'ffn_w_up': out['ffn_w_up'], 'ffn_conv_w': out['ffn_conv_w'], 'ffn_conv_b': out['ffn_conv_b'], 'ffn_w_down': out['ffn_w_down'], 'ln_ffn_g': out['ln_ffn_g'], 'ln_ffn_b': out['ln_ffn_b'], 'loss_target': out['loss_target'], 'm_ev_w_in': out['m_ev_w_in'], 'm_ev_b_f': out['m_ev_b_f'], 'm_ev_lambda_re': out['m_ev_lambda_re'], 'm_ev_lambda_im': out['m_ev_lambda_im'], 'm_ev_log_step': out['m_ev_log_step'], 'm_ev_ssm_b_re': out['m_ev_ssm_b_re'], 'm_ev_ssm_b_im': out['m_ev_ssm_b_im'], 'm_ev_ssm_c_re': out['m_ev_ssm_c_re'], 'm_ev_ssm_c_im': out['m_ev_ssm_c_im'], 'm_ev_ssm_d': out['m_ev_ssm_d'], 'm_ev_w_glu': out['m_ev_w_glu'], 'm_ev_w_out': out['m_ev_w_out'], 'm_od_w_in': out['m_od_w_in'], 'm_od_sinks': out['m_od_sinks'], 'm_od_w_out': out['m_od_w_out'], 'm_ln_mix_g': out['m_ln_mix_g'], 'm_ln_mix_b': out['m_ln_mix_b'], 'm_ffn_w_up': out['m_ffn_w_up'], 'm_ffn_conv_w': out['m_ffn_conv_w'], 'm_ffn_conv_b': out['m_ffn_conv_b'], 'm_ffn_w_down': out['m_ffn_w_down'], 'm_ln_ffn_g': out['m_ln_ffn_g'], 'm_ln_ffn_b': out['m_ln_ffn_b'], 'v_ev_w_in': out['v_ev_w_in'], 'v_ev_b_f': out['v_ev_b_f'], 'v_ev_lambda_re': out['v_ev_lambda_re'], 'v_ev_lambda_im': out['v_ev_lambda_im'], 'v_ev_log_step': out['v_ev_log_step'], 'v_ev_ssm_b_re': out['v_ev_ssm_b_re'], 'v_ev_ssm_b_im': out['v_ev_ssm_b_im'], 'v_ev_ssm_c_re': out['v_ev_ssm_c_re'], 'v_ev_ssm_c_im': out['v_ev_ssm_c_im'], 'v_ev_ssm_d': out['v_ev_ssm_d'], 'v_ev_w_glu': out['v_ev_w_glu'], 'v_ev_w_out': out['v_ev_w_out'], 'v_od_w_in': out['v_od_w_in'], 'v_od_sinks': out['v_od_sinks'], 'v_od_w_out': out['v_od_w_out'], 'v_ln_mix_g': out['v_ln_mix_g'], 'v_ln_mix_b': out['v_ln_mix_b'], 'v_ffn_w_up': out['v_ffn_w_up'], 'v_ffn_conv_w': out['v_ffn_conv_w'], 'v_ffn_conv_b': out['v_ffn_conv_b'], 'v_ffn_w_down': out['v_ffn_w_down'], 'v_ln_ffn_g': out['v_ln_ffn_g'], 'v_ln_ffn_b': out['v_ln_ffn_b']}


def _loss(weights, diff, rest, loss_target):
    with _jax.named_scope("forward"):
        args = {**rest, TWIN_DIFF_INPUT: diff, **{k: w.astype(_WEIGHT_DTYPES[k]) for k, w in weights.items()}}
        y = _forward(args)
    with _jax.named_scope("loss_head"):
        err = _jnp.square(y.astype(_jnp.float32) - loss_target)
        return 0.5 * _jnp.sum(_jnp.mean(err, axis=-1)) if err.ndim else 0.5 * err


def _adamw(w, g, m, v):
    m = ADAM_B1 * m + (1.0 - ADAM_B1) * g
    v = ADAM_B2 * v + (1.0 - ADAM_B2) * _jnp.square(g)
    m_hat = m / (1.0 - ADAM_B1 ** ADAM_STEP)
    v_hat = v / (1.0 - ADAM_B2 ** ADAM_STEP)
    delta = -ADAM_LR * (m_hat / (_jnp.sqrt(v_hat) + ADAM_EPS) + ADAM_WD * w)
    return delta, m, v


def reference(x, positions, ev_w_in, ev_b_f, ev_lambda_re, ev_lambda_im, ev_log_step, ev_ssm_b_re, ev_ssm_b_im, ev_ssm_c_re, ev_ssm_c_im, ev_ssm_d, ev_w_glu, ev_w_out, od_w_in, od_sinks, od_w_out, ln_mix_g, ln_mix_b, ffn_w_up, ffn_conv_w, ffn_conv_b, ffn_w_down, ln_ffn_g, ln_ffn_b, loss_target, m_ev_w_in, m_ev_b_f, m_ev_lambda_re, m_ev_lambda_im, m_ev_log_step, m_ev_ssm_b_re, m_ev_ssm_b_im, m_ev_ssm_c_re, m_ev_ssm_c_im, m_ev_ssm_d, m_ev_w_glu, m_ev_w_out, m_od_w_in, m_od_sinks, m_od_w_out, m_ln_mix_g, m_ln_mix_b, m_ffn_w_up, m_ffn_conv_w, m_ffn_conv_b, m_ffn_w_down, m_ln_ffn_g, m_ln_ffn_b, v_ev_w_in, v_ev_b_f, v_ev_lambda_re, v_ev_lambda_im, v_ev_log_step, v_ev_ssm_b_re, v_ev_ssm_b_im, v_ev_ssm_c_re, v_ev_ssm_c_im, v_ev_ssm_d, v_ev_w_glu, v_ev_w_out, v_od_w_in, v_od_sinks, v_od_w_out, v_ln_mix_g, v_ln_mix_b, v_ffn_w_up, v_ffn_conv_w, v_ffn_conv_b, v_ffn_w_down, v_ln_ffn_g, v_ln_ffn_b):
    given = dict(x=x, positions=positions, ev_w_in=ev_w_in, ev_b_f=ev_b_f, ev_lambda_re=ev_lambda_re, ev_lambda_im=ev_lambda_im, ev_log_step=ev_log_step, ev_ssm_b_re=ev_ssm_b_re, ev_ssm_b_im=ev_ssm_b_im, ev_ssm_c_re=ev_ssm_c_re, ev_ssm_c_im=ev_ssm_c_im, ev_ssm_d=ev_ssm_d, ev_w_glu=ev_w_glu, ev_w_out=ev_w_out, od_w_in=od_w_in, od_sinks=od_sinks, od_w_out=od_w_out, ln_mix_g=ln_mix_g, ln_mix_b=ln_mix_b, ffn_w_up=ffn_w_up, ffn_conv_w=ffn_conv_w, ffn_conv_b=ffn_conv_b, ffn_w_down=ffn_w_down, ln_ffn_g=ln_ffn_g, ln_ffn_b=ln_ffn_b, loss_target=loss_target, m_ev_w_in=m_ev_w_in, m_ev_b_f=m_ev_b_f, m_ev_lambda_re=m_ev_lambda_re, m_ev_lambda_im=m_ev_lambda_im, m_ev_log_step=m_ev_log_step, m_ev_ssm_b_re=m_ev_ssm_b_re, m_ev_ssm_b_im=m_ev_ssm_b_im, m_ev_ssm_c_re=m_ev_ssm_c_re, m_ev_ssm_c_im=m_ev_ssm_c_im, m_ev_ssm_d=m_ev_ssm_d, m_ev_w_glu=m_ev_w_glu, m_ev_w_out=m_ev_w_out, m_od_w_in=m_od_w_in, m_od_sinks=m_od_sinks, m_od_w_out=m_od_w_out, m_ln_mix_g=m_ln_mix_g, m_ln_mix_b=m_ln_mix_b, m_ffn_w_up=m_ffn_w_up, m_ffn_conv_w=m_ffn_conv_w, m_ffn_conv_b=m_ffn_conv_b, m_ffn_w_down=m_ffn_w_down, m_ln_ffn_g=m_ln_ffn_g, m_ln_ffn_b=m_ln_ffn_b, v_ev_w_in=v_ev_w_in, v_ev_b_f=v_ev_b_f, v_ev_lambda_re=v_ev_lambda_re, v_ev_lambda_im=v_ev_lambda_im, v_ev_log_step=v_ev_log_step, v_ev_ssm_b_re=v_ev_ssm_b_re, v_ev_ssm_b_im=v_ev_ssm_b_im, v_ev_ssm_c_re=v_ev_ssm_c_re, v_ev_ssm_c_im=v_ev_ssm_c_im, v_ev_ssm_d=v_ev_ssm_d, v_ev_w_glu=v_ev_w_glu, v_ev_w_out=v_ev_w_out, v_od_w_in=v_od_w_in, v_od_sinks=v_od_sinks, v_od_w_out=v_od_w_out, v_ln_mix_g=v_ln_mix_g, v_ln_mix_b=v_ln_mix_b, v_ffn_w_up=v_ffn_w_up, v_ffn_conv_w=v_ffn_conv_w, v_ffn_conv_b=v_ffn_conv_b, v_ffn_w_down=v_ffn_w_down, v_ln_ffn_g=v_ln_ffn_g, v_ln_ffn_b=v_ln_ffn_b)
    weights = {n: given[n] for n in TWIN_WEIGHTS}
    shared = {n: given[n] for n in SHARED_INPUTS}
    per_example = {n: given[n] for n in ['x', 'positions']}
    grad_fn = _jax.value_and_grad(_loss, argnums=(0, 1))

    def one_microbatch(ex, loss_target):
        ex = dict(ex)
        diff = ex.pop(TWIN_DIFF_INPUT)
        return grad_fn(weights, diff, {**shared, **ex}, loss_target)

    if N_MICROBATCH == 1:
        loss, (grad_w, grad_x) = one_microbatch(per_example, given["loss_target"])
    else:
        def body(carry, xs):
            loss_sum, grad_sum = carry
            l_k, (gw_k, gx_k) = one_microbatch(xs[0], xs[1])
            with _jax.named_scope("update"):
                return (loss_sum + l_k, _jax.tree.map(_jnp.add, grad_sum, gw_k)), gx_k

        init = (_jnp.zeros((), _jnp.float32), _jax.tree.map(_jnp.zeros_like, weights))
        (loss, grad_w), grad_x = _jax.lax.scan(body, init, (per_example, given["loss_target"]))
    with _jax.named_scope("update"):
        delta_w, new_m, new_v = {}, {}, {}
        for n in TWIN_WEIGHTS:
            delta_w[n], new_m[n], new_v[n] = _adamw(weights[n], grad_w[n], given["m_" + n], given["v_" + n])
    return (loss, grad_x, *[grad_w[n] for n in TWIN_WEIGHTS], *[delta_w[n] for n in TWIN_WEIGHTS],
            *[new_m[n] for n in TWIN_WEIGHTS], *[new_v[n] for n in TWIN_WEIGHTS])
```

```python
import functools
import math

import numpy as np
import jax
import jax.numpy as jnp
from jax import lax
from jax.experimental import pallas as pl
from jax.experimental.pallas import tpu as pltpu

F32 = jnp.float32
BF16 = jnp.bfloat16
MESH = pl.DeviceIdType.MESH
ANY = pl.BlockSpec(memory_space=pl.ANY)

D_MODEL = 2048
FOX_HEADS = 8
FOX_HEAD_DIM = 128
FOX_WIDTH = 1024
SSM_WIDTH = 1024
SSM_GROUP = 16
SSM_GROUPS = 64
SSM_STATE = 64
SWA_HEADS = 32
SWA_KV_HEADS = 4
SWA_HEAD_DIM = 64
SWA_GROUPS = 8
SWA_WINDOW = 128
ROPE_DIM = 16
ROPE_THETA = 500000.0
LN_EPS = 1e-5
DEPTH = 2
ALPHA = (2.0 * DEPTH) ** 0.25
ADAM_LR = 0.001
ADAM_B1 = 0.9
ADAM_B2 = 0.999
ADAM_EPS = 1e-08
ADAM_WD = 0.01
ADAM_STEP = 10
N_CHIPS = 4

VMEM_LIMIT = 56 * 1024 * 1024
LANE = 128


def _call(body, **kw):
    return pl.pallas_call(body, **kw)


def _cparams(sem):
    return pltpu.CompilerParams(dimension_semantics=sem, vmem_limit_bytes=VMEM_LIMIT)


def _rup(n, m):
    return (n + m - 1) // m * m


def _pick(n, pref):
    if n <= pref:
        return n
    t = pref
    while n % t:
        t //= 2
    return t


def _mm(a, b, mode, *, name, tm=512, tn=1024, tk=2048, bmode=None, out_dtype=F32):
    a3 = a if a.ndim == 3 else a[None]
    b3 = b if b.ndim == 3 else b[None]
    if mode == 'tn':
        K, M = a3.shape[1:]
    else:
        M, K = a3.shape[1:]
    N = b3.shape[1] if mode == 'nt' else b3.shape[2]
    tm, tn, tk = _pick(M, tm), _pick(N, tn), _pick(K, tk)
    nb = max(a3.shape[0], b3.shape[0])
    nbo, nbr = (1, nb) if bmode == 'abr' else (nb, 1)
    nk = K // tk
    nred = nbr * nk
    a_b = bmode in ('ao', 'abr')
    b_b = bmode in ('bo', 'abr')
    o_b = bmode in ('bo', 'ao')

    def bsel(flag, bo, br):
        return (bo + br) if flag else 0

    if mode == 'tn':
        a_spec = pl.BlockSpec((None, tk, tm), lambda bo, i, j, br, k: (bsel(a_b, bo, br), k, i))
    else:
        a_spec = pl.BlockSpec((None, tm, tk), lambda bo, i, j, br, k: (bsel(a_b, bo, br), i, k))
    if mode == 'nt':
        b_spec = pl.BlockSpec((None, tn, tk), lambda bo, i, j, br, k: (bsel(b_b, bo, br), j, k))
    else:
        b_spec = pl.BlockSpec((None, tk, tn), lambda bo, i, j, br, k: (bsel(b_b, bo, br), k, j))
    o_spec = pl.BlockSpec((None, tm, tn), lambda bo, i, j, br, k: (bsel(o_b, bo, br), i, j))
    dn = {'nn': (((1,), (0,)), ((), ())), 'nt': (((1,), (1,)), ((), ())), 'tn': (((0,), (0,)), ((), ()))}[mode]

    def body(a_ref, b_ref, o_ref, *scratch):
        r = lax.dot_general(a_ref[...].astype(BF16), b_ref[...].astype(BF16), dn, preferred_element_type=F32)
        if nred == 1:
            o_ref[...] = r.astype(out_dtype)
        else:
            acc = scratch[0]
            step = pl.program_id(3) * nk + pl.program_id(4)

            @pl.when(step == 0)
            def _():
                acc[...] = r

            @pl.when(step > 0)
            def _():
                acc[...] += r

            @pl.when(step == nred - 1)
            def _():
                o_ref[...] = acc[...].astype(out_dtype)

    out = _call(
        body, name=name,
        grid=(nbo, M // tm, N // tn, nbr, nk),
        in_specs=[a_spec, b_spec], out_specs=o_spec,
        out_shape=jax.ShapeDtypeStruct((nbo if o_b else 1, M, N), out_dtype),
        scratch_shapes=[] if nred == 1 else [pltpu.VMEM((tm, tn), F32)],
        compiler_params=_cparams(("parallel", "parallel", "parallel", "arbitrary", "arbitrary")),
    )(a3, b3)
    return out if o_b else out[0]


def _add_ln_fwd(x, r, g, b, *, name):
    S, D = x.shape
    tr = _pick(S, 256)

    def body(x_ref, r_ref, g_ref, b_ref, o_ref, xh_ref, rs_ref):
        z = ALPHA * x_ref[...] + r_ref[...]
        mu = jnp.mean(z, axis=-1, keepdims=True)
        zc = z - mu
        var = jnp.mean(zc * zc, axis=-1, keepdims=True)
        rstd = lax.rsqrt(var + LN_EPS)
        xh = zc * rstd
        xh_ref[...] = xh
        rs_ref[...] = rstd
        o_ref[...] = xh * g_ref[...] + b_ref[...]

    row = pl.BlockSpec((tr, D), lambda i: (i, 0))
    vec = pl.BlockSpec((1, D), lambda i: (0, 0))
    return _call(
        body, name=name, grid=(S // tr,),
        in_specs=[row, row, vec, vec],
        out_specs=[row, row, pl.BlockSpec((tr, 1), lambda i: (i, 0))],
        out_shape=[jax.ShapeDtypeStruct((S, D), F32), jax.ShapeDtypeStruct((S, D), F32),
                   jax.ShapeDtypeStruct((S, 1), F32)],
        compiler_params=_cparams(("parallel",)),
    )(x, r, g.reshape(1, D), b.reshape(1, D))


def _ln_bwd(da, db, xhat, rstd, g, *, name):
    S, D = xhat.shape
    tr = _pick(S, 256)
    two = db is not None

    def body(*refs):
        if two:
            da_ref, db_ref, xh_ref, rs_ref, g_ref, dz_ref, dg_ref, dbt_ref = refs
            dy = ALPHA * da_ref[...] + db_ref[...]
        else:
            da_ref, xh_ref, rs_ref, g_ref, dz_ref, dg_ref, dbt_ref = refs
            dy = da_ref[...]
        xh = xh_ref[...]
        dxh = dy * g_ref[...]
        m1 = jnp.mean(dxh, axis=-1, keepdims=True)
        m2 = jnp.mean(dxh * xh, axis=-1, keepdims=True)
        dz_ref[...] = rs_ref[...] * (dxh - m1 - xh * m2)
        pg = jnp.sum(dy * xh, axis=0, keepdims=True)
        pb = jnp.sum(dy, axis=0, keepdims=True)

        @pl.when(pl.program_id(0) == 0)
        def _():
            dg_ref[...] = pg
            dbt_ref[...] = pb

        @pl.when(pl.program_id(0) > 0)
        def _():
            dg_ref[...] += pg
            dbt_ref[...] += pb

    row = pl.BlockSpec((tr, D), lambda i: (i, 0))
    vec = pl.BlockSpec((1, D), lambda i: (0, 0))
    ins = [da] + ([db] if two else []) + [xhat, rstd, g.reshape(1, D)]
    in_specs = [row] + ([row] if two else []) + [row, pl.BlockSpec((tr, 1), lambda i: (i, 0)), vec]
    return _call(
        body, name=name, grid=(S // tr,),
        in_specs=in_specs, out_specs=[row, vec, vec],
        out_shape=[jax.ShapeDtypeStruct((S, D), F32), jax.ShapeDtypeStruct((1, D), F32),
                   jax.ShapeDtypeStruct((1, D), F32)],
        compiler_params=_cparams(("arbitrary",)),
    )(*ins)


def _loss_grad(y, t, *, name):
    S, D = y.shape
    tr = _pick(S, 256)

    def body(y_ref, t_ref, dy_ref, l_ref):
        e = y_ref[...] - t_ref[...]
        dy_ref[...] = e * (1.0 / D)
        part = 0.5 * jnp.sum(jnp.sum(e * e, axis=-1, keepdims=True) * (1.0 / D), axis=0, keepdims=True)

        @pl.when(pl.program_id(0) == 0)
        def _():
            l_ref[...] = part

        @pl.when(pl.program_id(0) > 0)
        def _():
            l_ref[...] += part

    row = pl.BlockSpec((tr, D), lambda i: (i, 0))
    return _call(
        body, name=name, grid=(S // tr,), in_specs=[row, row],
        out_specs=[row, pl.BlockSpec((1, 1), lambda i: (0, 0))],
        out_shape=[jax.ShapeDtypeStruct((S, D), F32), jax.ShapeDtypeStruct((1, 1), F32)],
        compiler_params=_cparams(("arbitrary",)),
    )(y, t)


def _combine(terms, scales, *, name, out_dtype=F32):
    S, D = terms[0].shape
    tr = _pick(S, 256)
    n = len(terms)

    def body(*refs):
        acc = scales[0] * refs[0][...].astype(F32)
        for i in range(1, n):
            acc = acc + scales[i] * refs[i][...].astype(F32)
        refs[n][...] = acc.astype(out_dtype)

    row = pl.BlockSpec((tr, D), lambda i: (i, 0))
    return _call(
        body, name=name, grid=(S // tr,), in_specs=[row] * n, out_specs=row,
        out_shape=jax.ShapeDtypeStruct((S, D), out_dtype),
        compiler_params=_cparams(("parallel",)),
    )(*terms)


def _split3(x):
    h = x.astype(BF16)
    r = x - h.astype(F32)
    m = r.astype(BF16)
    l = (r - m.astype(F32)).astype(BF16)
    return h, m, l


def _tri_matmul(tri_bf, x):
    h, m, l = _split3(x)
    dn = (((1,), (0,)), ((), ()))
    return (lax.dot_general(tri_bf, l, dn, preferred_element_type=F32)
            + lax.dot_general(tri_bf, m, dn, preferred_element_type=F32)
            + lax.dot_general(tri_bf, h, dn, preferred_element_type=F32))


def _gate_fwd(fl, bf, *, name):
    S = fl.shape[0]
    tc = _pick(S, 256)
    nchunk = S // tc

    def body(fl_ref, bf_ref, c_ref, sg_ref):
        r = lax.broadcasted_iota(jnp.int32, (tc, tc), 0)
        cidx = lax.broadcasted_iota(jnp.int32, (tc, tc), 1)
        tri = (r >= cidx).astype(BF16)
        carry = jnp.zeros((1, LANE), F32)
        for ch in range(nchunk):
            x = fl_ref[pl.ds(ch * tc, tc), :] + bf_ref[...]
            lf = jnp.minimum(x, 0.0) - jnp.log(1.0 + jnp.exp(-jnp.abs(x)))
            sg_ref[pl.ds(ch * tc, tc), :] = jax.nn.sigmoid(-x)
            c_ref[pl.ds(ch * tc, tc), :] = _tri_matmul(tri, lf) + carry
            carry = carry + jnp.sum(lf, axis=0, keepdims=True)

    full = pl.BlockSpec((S, LANE), lambda: (0, 0))
    return _call(
        body, name=name, in_specs=[full, pl.BlockSpec((1, LANE), lambda: (0, 0))], out_specs=[full, full],
        out_shape=[jax.ShapeDtypeStruct((S, LANE), F32)] * 2,
        compiler_params=pltpu.CompilerParams(vmem_limit_bytes=VMEM_LIMIT),
    )(fl, bf)


def _gate_bwd(dc, sg, *, name):
    S = dc.shape[0]
    tc = _pick(S, 256)
    nchunk = S // tc

    def body(dc_ref, sg_ref, dfl_ref, db_ref):
        r = lax.broadcasted_iota(jnp.int32, (tc, tc), 0)
        cidx = lax.broadcasted_iota(jnp.int32, (tc, tc), 1)
        tri = (r <= cidx).astype(BF16)
        carry = jnp.zeros((1, LANE), F32)
        dbacc = jnp.zeros((1, LANE), F32)
        for ch in reversed(range(nchunk)):
            d = dc_ref[pl.ds(ch * tc, tc), :]
            dfl = (_tri_matmul(tri, d) + carry) * sg_ref[pl.ds(ch * tc, tc), :]
            dfl_ref[pl.ds(ch * tc, tc), :] = dfl
            dbacc = dbacc + jnp.sum(dfl, axis=0, keepdims=True)
            carry = carry + jnp.sum(d, axis=0, keepdims=True)
        db_ref[...] = dbacc

    full = pl.BlockSpec((S, LANE), lambda: (0, 0))
    return _call(
        body, name=name, in_specs=[full, full], out_specs=[full, pl.BlockSpec((1, LANE), lambda: (0, 0))],
        out_shape=[jax.ShapeDtypeStruct((S, LANE), F32), jax.ShapeDtypeStruct((1, LANE), F32)],
        compiler_params=pltpu.CompilerParams(vmem_limit_bytes=VMEM_LIMIT),
    )(dc, sg)


def _fox_scores(q_ref, k_ref, cc_ref, cr_ref, qi, tq, S):
    scale = 1.0 / math.sqrt(FOX_HEAD_DIM)
    s = lax.dot_general(q_ref[...].astype(BF16), k_ref[...].astype(BF16), (((1,), (1,)), ((), ())),
                        preferred_element_type=F32) * scale
    s = s + cc_ref[...] - cr_ref[...]
    row = lax.broadcasted_iota(jnp.int32, (tq, S), 0) + qi * tq
    col = lax.broadcasted_iota(jnp.int32, (tq, S), 1)
    return s, row >= col


def _fox_fwd(P, ccol, crow, *, name):
    S = P.shape[0]
    tq = _pick(S, 256)
    H = FOX_HEADS

    def body(q_ref, k_ref, v_ref, cc_ref, cr_ref, o_ref, l_ref):
        s, causal = _fox_scores(q_ref, k_ref, cc_ref, cr_ref, pl.program_id(1), tq, S)
        s = jnp.where(causal, s, -1e30)
        m = jnp.max(s, axis=-1, keepdims=True)
        e = jnp.exp(s - m)
        den = jnp.sum(e, axis=-1, keepdims=True)
        p = e / den
        o_ref[...] = jnp.dot(p.astype(BF16), v_ref[...].astype(BF16), preferred_element_type=F32)
        l_ref[...] = m + jnp.log(den)

    return _call(
        body, name=name, grid=(H, S // tq),
        in_specs=[pl.BlockSpec((tq, 128), lambda h, i: (i, h)),
                  pl.BlockSpec((S, 128), lambda h, i: (0, H + h)),
                  pl.BlockSpec((S, 128), lambda h, i: (0, 2 * H + h)),
                  pl.BlockSpec((None, tq, 1), lambda h, i: (h, i, 0)),
                  pl.BlockSpec((None, 1, S), lambda h, i: (h, 0, 0))],
        out_specs=[pl.BlockSpec((tq, 128), lambda h, i: (i, h)),
                   pl.BlockSpec((None, tq, 1), lambda h, i: (h, i, 0))],
        out_shape=[jax.ShapeDtypeStruct((S, FOX_WIDTH), F32), jax.ShapeDtypeStruct((H, S, 1), F32)],
        compiler_params=_cparams(("parallel", "parallel")),
    )(P, P, P, ccol, crow)


def _fox_bwd(P, ccol, crow, o, lse, dcat, *, name):
    S = P.shape[0]
    tq = _pick(S, 256)
    H = FOX_HEADS
    nq = S // tq
    scale = 1.0 / math.sqrt(FOX_HEAD_DIM)

    def body(q_ref, k_ref, v_ref, cc_ref, cr_ref, o_ref, l_ref, do_ref,
             dq_ref, dk_ref, dv_ref, dcc_ref, dcr_ref, dk_acc, dv_acc):
        qi = pl.program_id(1)
        s, causal = _fox_scores(q_ref, k_ref, cc_ref, cr_ref, qi, tq, S)
        p = jnp.where(causal, jnp.exp(s - l_ref[...]), 0.0)
        do = do_ref[...]
        do_bf = do.astype(BF16)
        dp = lax.dot_general(do_bf, v_ref[...].astype(BF16), (((1,), (1,)), ((), ())), preferred_element_type=F32)
        delta = jnp.sum(do * o_ref[...], axis=-1, keepdims=True)
        ds = p * (dp - delta)
        ds_bf = ds.astype(BF16)
        dq_ref[...] = (jnp.dot(ds_bf, k_ref[...].astype(BF16), preferred_element_type=F32) * scale).astype(BF16)
        dkp = lax.dot_general(ds_bf, q_ref[...].astype(BF16), (((0,), (0,)), ((), ())),
                              preferred_element_type=F32) * scale
        dvp = lax.dot_general(p.astype(BF16), do_bf, (((0,), (0,)), ((), ())), preferred_element_type=F32)
        dcc_ref[...] = jnp.sum(ds, axis=-1, keepdims=True)
        dcr = jnp.sum(ds, axis=0, keepdims=True)

        @pl.when(qi == 0)
        def _():
            dk_acc[...] = dkp
            dv_acc[...] = dvp
            dcr_ref[...] = dcr

        @pl.when(qi > 0)
        def _():
            dk_acc[...] += dkp
            dv_acc[...] += dvp
            dcr_ref[...] += dcr

        @pl.when(qi == nq - 1)
        def _():
            dk_ref[...] = dk_acc[...].astype(BF16)
            dv_ref[...] = dv_acc[...].astype(BF16)

    qblk = pl.BlockSpec((tq, 128), lambda h, i: (i, h))
    kvo = pl.BlockSpec((S, 128), lambda h, i: (0, h))
    col = pl.BlockSpec((None, tq, 1), lambda h, i: (h, i, 0))
    rowv = pl.BlockSpec((None, 1, S), lambda h, i: (h, 0, 0))
    return _call(
        body, name=name, grid=(H, nq),
        in_specs=[qblk,
                  pl.BlockSpec((S, 128), lambda h, i: (0, H + h)),
                  pl.BlockSpec((S, 128), lambda h, i: (0, 2 * H + h)),
                  col, rowv, qblk, col, qblk],
        out_specs=[qblk, kvo, kvo, col, rowv],
        out_shape=[jax.ShapeDtypeStruct((S, FOX_WIDTH), BF16)] * 3
        + [jax.ShapeDtypeStruct((H, S, 1), F32), jax.ShapeDtypeStruct((H, 1, S), F32)],
        scratch_shapes=[pltpu.VMEM((S, 128), F32), pltpu.VMEM((S, 128), F32)],
        compiler_params=_cparams(("parallel", "arbitrary")),
    )(P, P, P, ccol, crow, o, lse, dcat)


def _s5_disc_fwd(lr, li, ls, *, name):
    G, Pn = lr.shape

    def body(lr_ref, li_ref, ls_ref, ar_ref, ai_ref, gr_ref, gi_ref):
        lr_, li_ = lr_ref[...], li_ref[...]
        dt = jnp.exp(ls_ref[...])
        mag = jnp.exp(lr_ * dt)
        th = li_ * dt
        ar = mag * jnp.cos(th)
        ai = mag * jnp.sin(th)
        den = lr_ * lr_ + li_ * li_
        xr = ar - 1.0
        ar_ref[...] = ar
        ai_ref[...] = ai
        gr_ref[...] = (xr * lr_ + ai * li_) / den
        gi_ref[...] = (ai * lr_ - xr * li_) / den

    sq = pl.BlockSpec((G, Pn), lambda: (0, 0))
    return _call(
        body, name=name, in_specs=[sq, sq, pl.BlockSpec((G, 1), lambda: (0, 0))], out_specs=[sq] * 4,
        out_shape=[jax.ShapeDtypeStruct((G, Pn), F32)] * 4,
    )(lr, li, ls)


def _s5_disc_bwd(lr, li, ls, dar, dai, dgr, dgi, *, name):
    G, Pn = lr.shape

    def body(lr_ref, li_ref, ls_ref, dar_ref, dai_ref, dgr_ref, dgi_ref, dlr_ref, dli_ref, dls_ref):
        lr_, li_ = lr_ref[...], li_ref[...]
        dt = jnp.exp(ls_ref[...])
        mag = jnp.exp(lr_ * dt)
        th = li_ * dt
        ar = mag * jnp.cos(th)
        ai = mag * jnp.sin(th)
        den = lr_ * lr_ + li_ * li_
        xr = ar - 1.0
        xi = ai
        g_re = (xr * lr_ + xi * li_) / den
        g_im = (xi * lr_ - xr * li_) / den
        dgr_, dgi_ = dgr_ref[...], dgi_ref[...]
        dxr = (dgr_ * lr_ - dgi_ * li_) / den
        dxi = (dgr_ * li_ + dgi_ * lr_) / den
        dden = -(dgr_ * g_re + dgi_ * g_im) / den
        dlr = (dgr_ * xr + dgi_ * xi) / den + 2.0 * dden * lr_
        dli = (dgr_ * xi - dgi_ * xr) / den + 2.0 * dden * li_
        da_r = dar_ref[...] + dxr
        da_i = dai_ref[...] + dxi
        dmag_mag = da_r * ar + da_i * ai
        dth = da_i * ar - da_r * ai
        dlr_ref[...] = dlr + dmag_mag * dt
        dli_ref[...] = dli + dth * dt
        ddt = jnp.sum(dmag_mag * lr_ + dth * li_, axis=-1, keepdims=True)
        dls_ref[...] = ddt * dt

    sq = pl.BlockSpec((G, Pn), lambda: (0, 0))
    c1 = pl.BlockSpec((G, 1), lambda: (0, 0))
    return _call(
        body, name=name, in_specs=[sq, sq, c1, sq, sq, sq, sq], out_specs=[sq, sq, c1],
        out_shape=[jax.ShapeDtypeStruct((G, Pn), F32)] * 2 + [jax.ShapeDtypeStruct((G, 1), F32)],
    )(lr, li, ls, dar, dai, dgr, dgi)


def _s5_bb_fwd(gr, gi, br, bi, *, name):
    R, C = br.shape

    def body(gr_ref, gi_ref, br_ref, bi_ref, or_ref, oi_ref):
        g_r, g_i, b_r, b_i = gr_ref[...], gi_ref[...], br_ref[...], bi_ref[...]
        or_ref[...] = g_r * b_r - g_i * b_i
        oi_ref[...] = g_r * b_i + g_i * b_r

    w = pl.BlockSpec((R, C), lambda: (0, 0))
    c1 = pl.BlockSpec((R, 1), lambda: (0, 0))
    return _call(body, name=name, in_specs=[c1, c1, w, w], out_specs=[w, w],
                 out_shape=[jax.ShapeDtypeStruct((R, C), F32)] * 2)(gr, gi, br, bi)


def _s5_bb_bwd(gr, gi, br, bi, dbbr, dbbi, *, name):
    R, C = br.shape

    def body(gr_ref, gi_ref, br_ref, bi_ref, dr_ref, di_ref, dbr_ref, dbi_ref, dgr_ref, dgi_ref):
        g_r, g_i, b_r, b_i = gr_ref[...], gi_ref[...], br_ref[...], bi_ref[...]
        d_r, d_i = dr_ref[...], di_ref[...]
        dbr_ref[...] = g_r * d_r + g_i * d_i
        dbi_ref[...] = g_r * d_i - g_i * d_r
        dgr_ref[...] = jnp.sum(d_r * b_r + d_i * b_i, axis=-1, keepdims=True)
        dgi_ref[...] = jnp.sum(d_i * b_r - d_r * b_i, axis=-1, keepdims=True)

    w = pl.BlockSpec((R, C), lambda: (0, 0))
    c1 = pl.BlockSpec((R, 1), lambda: (0, 0))
    return _call(body, name=name, in_specs=[c1, c1, w, w, w, w], out_specs=[w, w, c1, c1],
                 out_shape=[jax.ShapeDtypeStruct((R, C), F32)] * 2 + [jax.ShapeDtypeStruct((R, 1), F32)] * 2,
                 )(gr, gi, br, bi, dbbr, dbbi)


def _s5_scan_fwd(bu, a, *, name):
    _, S, N = bu.shape
    tc = 512
    nt = N // tc

    def body(a_ref, b_ref, h_ref):
        ar, ai = a_ref[0], a_ref[1]

        def step(t, carry):
            hr, hi = carry
            nr = ar * hr - ai * hi + b_ref[0, pl.ds(t, 1), :]
            ni = ar * hi + ai * hr + b_ref[1, pl.ds(t, 1), :]
            h_ref[0, pl.ds(t, 1), :] = nr
            h_ref[1, pl.ds(t, 1), :] = ni
            return nr, ni

        z = jnp.zeros((1, tc), F32)
        lax.fori_loop(0, S, step, (z, z), unroll=8)

    vec = pl.BlockSpec((2, 1, tc), lambda j: (0, 0, j))
    mat = pl.BlockSpec((2, S, tc), lambda j: (0, 0, j))
    return _call(
        body, name=name, grid=(nt,), in_specs=[vec, mat], out_specs=mat,
        out_shape=jax.ShapeDtypeStruct((2, S, N), F32),
        compiler_params=_cparams(("parallel",)),
    )(a, bu)


def _s5_scan_bwd(g, h, a, *, name):
    _, S, N = g.shape
    tc = 256
    nt = N // tc

    def body(a_ref, g_ref, h_ref, l_ref, da_ref):
        ar, ai = a_ref[0], a_ref[1]

        def step(i, carry):
            t = S - 1 - i
            lr, li, dar, dai = carry
            nr = g_ref[0, pl.ds(t, 1), :] + ar * lr + ai * li
            ni = g_ref[1, pl.ds(t, 1), :] + ar * li - ai * lr
            l_ref[0, pl.ds(t, 1), :] = nr
            l_ref[1, pl.ds(t, 1), :] = ni
            tp = jnp.maximum(t - 1, 0)
            keep = jnp.where(t > 0, 1.0, 0.0).astype(F32)
            hpr = h_ref[0, pl.ds(tp, 1), :] * keep
            hpi = h_ref[1, pl.ds(tp, 1), :] * keep
            return nr, ni, dar + nr * hpr + ni * hpi, dai + ni * hpr - nr * hpi

        z = jnp.zeros((1, tc), F32)
        _, _, dar, dai = lax.fori_loop(0, S, step, (z, z, z, z), unroll=8)
        da_ref[0] = dar
        da_ref[1] = dai

    vec = pl.BlockSpec((2, 1, tc), lambda j: (0, 0, j))
    mat = pl.BlockSpec((2, S, tc), lambda j: (0, 0, j))
    return _call(
        body, name=name, grid=(nt,), in_specs=[vec, mat, mat], out_specs=[mat, vec],
        out_shape=[jax.ShapeDtypeStruct((2, S, N), F32), jax.ShapeDtypeStruct((2, 1, N), F32)],
        compiler_params=_cparams(("parallel",)),
    )(a, g, h)


_GELU_C = math.sqrt(2.0 / math.pi)


def _s5_out_fwd(yc, P, dskip, *, name):
    S, W = yc.shape
    tr = _pick(S, 256)
    ub = 3 * FOX_WIDTH // W

    def body(yc_ref, u_ref, d_ref, y_ref, yg_ref):
        y = yc_ref[...] + d_ref[...] * u_ref[...]
        y_ref[...] = y
        t = jnp.tanh(_GELU_C * (y + 0.044715 * y * y * y))
        yg_ref[...] = (0.5 * y * (1.0 + t)).astype(BF16)

    row = pl.BlockSpec((tr, W), lambda i: (i, 0))
    return _call(
        body, name=name, grid=(S // tr,),
        in_specs=[row, pl.BlockSpec((tr, W), lambda i: (i, ub)), pl.BlockSpec((1, W), lambda i: (0, 0))],
        out_specs=[row, row],
        out_shape=[jax.ShapeDtypeStruct((S, W), F32), jax.ShapeDtypeStruct((S, W), BF16)],
        compiler_params=_cparams(("parallel",)),
    )(yc, P, dskip)


def _s5_out_bwd(dyg, y, P, dskip, *, name):
    S, W = y.shape
    tr = _pick(S, 256)
    ub = 3 * FOX_WIDTH // W

    def body(dyg_ref, y_ref, u_ref, d_ref, dy_ref, du_ref, dd_ref):
        y_ = y_ref[...]
        inner = _GELU_C * (y_ + 0.044715 * y_ * y_ * y_)
        t = jnp.tanh(inner)
        dgelu = 0.5 * (1.0 + t) + 0.5 * y_ * (1.0 - t * t) * _GELU_C * (1.0 + 3.0 * 0.044715 * y_ * y_)
        dy = dyg_ref[...] * dgelu
        dy_ref[...] = dy.astype(BF16)
        du_ref[...] = d_ref[...] * dy
        part = jnp.sum(dy * u_ref[...], axis=0, keepdims=True)

        @pl.when(pl.program_id(0) == 0)
        def _():
            dd_ref[...] = part

        @pl.when(pl.program_id(0) > 0)
        def _():
            dd_ref[...] += part

    row = pl.BlockSpec((tr, W), lambda i: (i, 0))
    vec = pl.BlockSpec((1, W), lambda i: (0, 0))
    return _call(
        body, name=name, grid=(S // tr,),
        in_specs=[row, row, pl.BlockSpec((tr, W), lambda i: (i, ub)), vec],
        out_specs=[row, row, vec],
        out_shape=[jax.ShapeDtypeStruct((S, W), BF16), jax.ShapeDtypeStruct((S, W), F32),
                   jax.ShapeDtypeStruct((1, W), F32)],
        compiler_params=_cparams(("arbitrary",)),
    )(dyg, y, P, dskip)


def _glu_fwd(z, *, name):
    S, W2 = z.shape
    W = W2 // 2
    tr = _pick(S, 256)

    def body(z1_ref, z2_ref, o_ref):
        o_ref[...] = (z1_ref[...] * jax.nn.sigmoid(z2_ref[...])).astype(BF16)

    return _call(
        body, name=name, grid=(S // tr,),
        in_specs=[pl.BlockSpec((tr, W), lambda i: (i, 0)), pl.BlockSpec((tr, W), lambda i: (i, 1))],
        out_specs=pl.BlockSpec((tr, W), lambda i: (i, 0)),
        out_shape=jax.ShapeDtypeStruct((S, W), BF16),
        compiler_params=_cparams(("parallel",)),
    )(z, z)


def _glu_bwd(z, dcat, *, name):
    S, W2 = z.shape
    W = W2 // 2
    tr = _pick(S, 256)

    def body(z1_ref, z2_ref, d_ref, dz1_ref, dz2_ref):
        sg = jax.nn.sigmoid(z2_ref[...])
        d = d_ref[...]
        dz1_ref[...] = (d * sg).astype(BF16)
        dz2_ref[...] = (d * z1_ref[...] * sg * (1.0 - sg)).astype(BF16)

    lo = pl.BlockSpec((tr, W), lambda i: (i, 0))
    hi = pl.BlockSpec((tr, W), lambda i: (i, 1))
    dz1, dz2 = _call(
        body, name=name, grid=(S // tr,), in_specs=[lo, hi, hi], out_specs=[lo, lo],
        out_shape=[jax.ShapeDtypeStruct((S, W), BF16)] * 2,
        compiler_params=_cparams(("parallel",)),
    )(z, z, dcat)
    return jnp.concatenate([dz1, dz2], axis=1)


def _act_fwd(h, cw, cb, *, name):
    _, S, FP = h.shape
    tr = _pick(S, 256)
    hb = tr // 8

    def conv(x_ref, halo_ref, w_ref, b_ref, ext, first):
        ext[pl.ds(0, 8), :] = jnp.where(first, 0.0, halo_ref[...])
        ext[pl.ds(8, tr), :] = x_ref[...]
        return (b_ref[...] + w_ref[pl.ds(2, 1), :] * ext[pl.ds(8, tr), :]
                + w_ref[pl.ds(1, 1), :] * ext[pl.ds(7, tr), :] + w_ref[pl.ds(0, 1), :] * ext[pl.ds(6, tr), :])

    def body(g_ref, gh_ref, v_ref, vh_ref, wg_ref, wv_ref, bg_ref, bv_ref, a_ref, ext):
        first = pl.program_id(1) == 0
        cg = conv(g_ref, gh_ref, wg_ref, bg_ref, ext, first)
        cv = conv(v_ref, vh_ref, wv_ref, bv_ref, ext, first)
        a_ref[...] = (cg * jax.nn.sigmoid(cg) * cv).astype(BF16)

    def main(off):
        return pl.BlockSpec((None, tr, FP), lambda j, i: (j + off, i, 0))

    def halo(off):
        return pl.BlockSpec((None, 8, FP), lambda j, i: (j + off, jnp.maximum(i * hb - 1, 0), 0))

    def wspec(off):
        return pl.BlockSpec((None, 3, FP), lambda j, i: (j + off, 0, 0))

    def bspec(off):
        return pl.BlockSpec((None, 1, FP), lambda j, i: (j + off, 0, 0))

    cb3 = cb.reshape(4, 1, FP)
    return _call(
        body, name=name, grid=(2, S // tr),
        in_specs=[main(0), halo(0), main(2), halo(2), wspec(0), wspec(2), bspec(0), bspec(2)],
        out_specs=pl.BlockSpec((None, tr, FP), lambda j, i: (j, i, 0)),
        out_shape=jax.ShapeDtypeStruct((2, S, FP), BF16),
        scratch_shapes=[pltpu.VMEM((tr + 8, FP), F32)],
        compiler_params=_cparams(("parallel", "arbitrary")),
    )(h, h, h, h, cw, cw, cb3, cb3)


def _act_bwd(h, da, cw, cb, *, name):
    _, S, FP = h.shape
    tr = _pick(S, 128)
    hb = tr // 8
    nr = S // tr

    def fill(ext, x_ref, prev_ref, next_ref, first, last):
        ext[pl.ds(0, 8), :] = jnp.where(first, 0.0, prev_ref[...])
        ext[pl.ds(8, tr), :] = x_ref[...]
        ext[pl.ds(8 + tr, 8), :] = jnp.where(last, 0.0, next_ref[...])

    def convo(ext, w, b, base, n):
        return (b + w[2] * ext[pl.ds(base, n), :] + w[1] * ext[pl.ds(base - 1, n), :]
                + w[0] * ext[pl.ds(base - 2, n), :])

    def body(g_ref, gp_ref, gn_ref, v_ref, vp_ref, vn_ref, da_ref, dan_ref,
             wg_ref, wv_ref, bg_ref, bv_ref,
             dg_ref, dv_ref, dwg_ref, dwv_ref, dbg_ref, dbv_ref, eg, ev, ed, dcg, dcv):
        i = pl.program_id(1)
        first = i == 0
        last = i == nr - 1
        fill(eg, g_ref, gp_ref, gn_ref, first, last)
        fill(ev, v_ref, vp_ref, vn_ref, first, last)
        ed[pl.ds(0, tr), :] = da_ref[...]
        ed[pl.ds(tr, 8), :] = jnp.where(last, 0.0, dan_ref[...])
        wg = [wg_ref[pl.ds(k, 1), :] for k in range(3)]
        wv = [wv_ref[pl.ds(k, 1), :] for k in range(3)]
        n = tr + 8
        cg = convo(eg, wg, bg_ref[...], 8, n)
        cv = convo(ev, wv, bv_ref[...], 8, n)
        sg = jax.nn.sigmoid(cg)
        d = ed[...]
        dcg[...] = d * cv * sg * (1.0 + cg * (1.0 - sg))
        dcv[...] = d * cg * sg
        for (dc, w, e, dh_ref, dw_ref, db_ref) in ((dcg, wg, eg, dg_ref, dwg_ref, dbg_ref),
                                                  (dcv, wv, ev, dv_ref, dwv_ref, dbv_ref)):
            d0 = dc[pl.ds(0, tr), :]
            dh_ref[...] = (w[2] * d0 + w[1] * dc[pl.ds(1, tr), :] + w[0] * dc[pl.ds(2, tr), :]).astype(BF16)
            pw = [jnp.sum(d0 * e[pl.ds(6 + k, tr), :], axis=0, keepdims=True) for k in range(3)]
            pb = jnp.sum(d0, axis=0, keepdims=True)

            @pl.when(first)
            def _():
                for k in range(3):
                    dw_ref[pl.ds(k, 1), :] = pw[k]
                db_ref[...] = pb

            @pl.when(jnp.logical_not(first))
            def _():
                for k in range(3):
                    dw_ref[pl.ds(k, 1), :] += pw[k]
                db_ref[...] += pb

    def main(off):
        return pl.BlockSpec((None, tr, FP), lambda j, i: (j + off, i, 0))

    def prev(off):
        return pl.BlockSpec((None, 8, FP), lambda j, i: (j + off, jnp.maximum(i * hb - 1, 0), 0))

    def nxt(off):
        return pl.BlockSpec((None, 8, FP), lambda j, i: (j + off, jnp.minimum((i + 1) * hb, S // 8 - 1), 0))

    def wspec(off):
        return pl.BlockSpec((None, 3, FP), lambda j, i: (j + off, 0, 0))

    def bspec(off):
        return pl.BlockSpec((None, 1, FP), lambda j, i: (j + off, 0, 0))

    cb3 = cb.reshape(4, 1, FP)
    dg, dv, dwg, dwv, dbg, dbv = _call(
        body, name=name, grid=(2, nr),
        in_specs=[main(0), prev(0), nxt(0), main(2), prev(2), nxt(2), main(0), nxt(0),
                  wspec(0), wspec(2), bspec(0), bspec(2)],
        out_specs=[main(0), main(0), wspec(0), wspec(0), bspec(0), bspec(0)],
        out_shape=[jax.ShapeDtypeStruct((2, S, FP), BF16)] * 2
        + [jax.ShapeDtypeStruct((2, 3, FP), F32)] * 2 + [jax.ShapeDtypeStruct((2, 1, FP), F32)] * 2,
        scratch_shapes=[pltpu.VMEM((tr + 16, FP), F32), pltpu.VMEM((tr + 16, FP), F32),
                        pltpu.VMEM((tr + 8, FP), F32), pltpu.VMEM((tr + 8, FP), F32),
                        pltpu.VMEM((tr + 8, FP), F32)],
        compiler_params=_cparams(("parallel", "arbitrary")),
    )(h, h, h, h, h, h, da, da, cw, cw, cb3, cb3)
    return (jnp.concatenate([dg, dv], axis=0), jnp.concatenate([dwg, dwv], axis=0),
            jnp.concatenate([dbg, dbv], axis=0))


def _rope_tables(posf, *, name):
    S = posf.shape[0]
    half = ROPE_DIM // 2
    d = np.arange(LANE) % SWA_HEAD_DIM
    invf = np.where(d < ROPE_DIM, ROPE_THETA ** (-(d % half).astype(np.float64) / half), 0.0).astype(np.float32)
    m_rot = (d < ROPE_DIM).astype(np.float32)
    m_a = (d < half).astype(np.float32)
    m_b = ((d >= half) & (d < ROPE_DIM)).astype(np.float32)
    consts = jnp.asarray(np.stack([invf, m_rot, m_a, m_b] + [np.zeros(LANE, np.float32)] * 4))

    def body(p_ref, k_ref, c_ref, sa_ref, sb_ref):
        k = k_ref[...]
        ang = p_ref[...] * k[0:1]
        co, si = jnp.cos(ang), jnp.sin(ang)
        c_ref[...] = k[1:2] * co + (1.0 - k[1:2])
        sa_ref[...] = -k[2:3] * si
        sb_ref[...] = k[3:4] * si

    full = pl.BlockSpec((S, LANE), lambda: (0, 0))
    return _call(
        body, name=name,
        in_specs=[pl.BlockSpec((S, 1), lambda: (0, 0)), pl.BlockSpec((8, LANE), lambda: (0, 0))],
        out_specs=[full] * 3, out_shape=[jax.ShapeDtypeStruct((S, LANE), F32)] * 3,
    )(posf, consts)


def _rope_apply(x, tabs, *, col0, width, inverse, name, out_dtype):
    S = x.shape[0]
    tr = _pick(S, 256)
    rep = width // LANE
    cb = col0 // width

    def body(x_ref, c_ref, sa_ref, sb_ref, o_ref):
        xv = x_ref[...].astype(F32)
        c = jnp.tile(c_ref[...], (1, rep))
        sa = jnp.tile(sa_ref[...], (1, rep))
        sb = jnp.tile(sb_ref[...], (1, rep))
        if not inverse:
            out = xv * c + pltpu.roll(xv, width - 8, 1) * sa + pltpu.roll(xv, 8, 1) * sb
        else:
            out = xv * c + pltpu.roll(xv * sa, 8, 1) + pltpu.roll(xv * sb, width - 8, 1)
        o_ref[...] = out.astype(out_dtype)

    tab = pl.BlockSpec((tr, LANE), lambda i: (i, 0))
    return _call(
        body, name=name, grid=(S // tr,),
        in_specs=[pl.BlockSpec((tr, width), lambda i: (i, cb)), tab, tab, tab],
        out_specs=pl.BlockSpec((tr, width), lambda i: (i, 0)),
        out_shape=jax.ShapeDtypeStruct((S, width), out_dtype),
        compiler_params=_cparams(("parallel",)),
    )(x, *tabs)


def _swa_mask(n):
    rows = SWA_GROUPS * SWA_WINDOW
    qi = lax.broadcasted_iota(jnp.int32, (rows, 2 * SWA_WINDOW), 0) & (SWA_WINDOW - 1)
    kj = lax.broadcasted_iota(jnp.int32, (rows, 2 * SWA_WINDOW), 1)
    rel = SWA_WINDOW + qi - kj
    return (rel >= 0) & (rel < SWA_WINDOW) & ((n > 0) | (kj >= SWA_WINDOW))


def _swa_fwd(qT, kT, vT, sink_rows, *, name):
    S = qT.shape[1]
    W, G, Dh = SWA_WINDOW, SWA_GROUPS, SWA_HEAD_DIM
    nb = S // W
    scale = 1.0 / math.sqrt(Dh)

    def body(q_ref, kp_ref, kc_ref, vp_ref, vc_ref, s_ref, o_ref, l_ref):
        n = pl.program_id(1)
        q = q_ref[...].reshape(G * W, Dh)
        kk = jnp.concatenate([kp_ref[...], kc_ref[...]], axis=0)
        vv = jnp.concatenate([vp_ref[...], vc_ref[...]], axis=0)
        s = lax.dot_general(q, kk, (((1,), (1,)), ((), ())), preferred_element_type=F32) * scale
        s = jnp.where(_swa_mask(n), s, -1e30)
        sink = s_ref[...]
        m = jnp.maximum(jnp.max(s, axis=-1, keepdims=True), sink)
        e = jnp.exp(s - m)
        den = jnp.sum(e, axis=-1, keepdims=True) + jnp.exp(sink - m)
        p = e / den
        o_ref[...] = jnp.dot(p.astype(BF16), vv, preferred_element_type=F32).reshape(G, W, Dh)
        l_ref[...] = (m + jnp.log(den)).reshape(G, W, 1)

    qs = pl.BlockSpec((G, W, Dh), lambda g, n: (g, n, 0))
    prev = pl.BlockSpec((None, W, Dh), lambda g, n: (g, jnp.maximum(n - 1, 0), 0))
    cur = pl.BlockSpec((None, W, Dh), lambda g, n: (g, n, 0))
    return _call(
        body, name=name, grid=(SWA_KV_HEADS, nb),
        in_specs=[qs, prev, cur, prev, cur, pl.BlockSpec((None, G * W, 1), lambda g, n: (g, 0, 0))],
        out_specs=[qs, pl.BlockSpec((G, W, 1), lambda g, n: (g, n, 0))],
        out_shape=[jax.ShapeDtypeStruct((SWA_HEADS, S, Dh), F32), jax.ShapeDtypeStruct((SWA_HEADS, S, 1), F32)],
        compiler_params=_cparams(("parallel", "parallel")),
    )(qT, kT, kT, vT, vT, sink_rows)


def _swa_bwd(qT, kT, vT, sink_rows, oT, L, doT, *, name):
    S = qT.shape[1]
    W, G, Dh = SWA_WINDOW, SWA_GROUPS, SWA_HEAD_DIM
    nb = S // W
    scale = 1.0 / math.sqrt(Dh)

    def body(q_ref, kp_ref, kc_ref, vp_ref, vc_ref, s_ref, o_ref, l_ref, do_ref,
             dq_ref, dk_ref, dv_ref, ds_ref):
        n = pl.program_id(1)
        q = q_ref[...].reshape(G * W, Dh)
        kk = jnp.concatenate([kp_ref[...], kc_ref[...]], axis=0)
        vv = jnp.concatenate([vp_ref[...], vc_ref[...]], axis=0)
        s = lax.dot_general(q, kk, (((1,), (1,)), ((), ())), preferred_element_type=F32) * scale
        lrow = l_ref[...].reshape(G * W, 1)
        p = jnp.where(_swa_mask(n), jnp.exp(s - lrow), 0.0)
        do = do_ref[...].reshape(G * W, Dh)
        do_bf = do.astype(BF16)
        dp = lax.dot_general(do_bf, vv, (((1,), (1,)), ((), ())), preferred_element_type=F32)
        delta = jnp.sum(do * o_ref[...].reshape(G * W, Dh), axis=-1, keepdims=True)
        dsc = p * (dp - delta)
        ds_bf = dsc.astype(BF16)
        dq_ref[...] = (jnp.dot(ds_bf, kk, preferred_element_type=F32) * scale).astype(BF16).reshape(G, W, Dh)
        dkk = lax.dot_general(ds_bf, q, (((0,), (0,)), ((), ())), preferred_element_type=F32) * scale
        dvv = lax.dot_general(p.astype(BF16), do_bf, (((0,), (0,)), ((), ())), preferred_element_type=F32)
        dsk = -jnp.exp(s_ref[...] - lrow) * delta
        dsk = jnp.broadcast_to(jnp.sum(dsk.reshape(G, W, 1), axis=1), (G, LANE))

        @pl.when(n == 0)
        def _():
            dk_ref[...] = jnp.zeros_like(dk_ref)
            dv_ref[...] = jnp.zeros_like(dv_ref)
            ds_ref[...] = jnp.zeros_like(ds_ref)

        rows = pl.ds(pl.multiple_of(n * W, W), 2 * W)
        dk_ref[rows, :] += dkk
        dv_ref[rows, :] += dvv
        ds_ref[...] += dsk

    qs = pl.BlockSpec((G, W, Dh), lambda g, n: (g, n, 0))
    prev = pl.BlockSpec((None, W, Dh), lambda g, n: (g, jnp.maximum(n - 1, 0), 0))
    cur = pl.BlockSpec((None, W, Dh), lambda g, n: (g, n, 0))
    lsp = pl.BlockSpec((G, W, 1), lambda g, n: (g, n, 0))
    kvo = pl.BlockSpec((None, S + W, Dh), lambda g, n: (g, 0, 0))
    return _call(
        body, name=name, grid=(SWA_KV_HEADS, nb),
        in_specs=[qs, prev, cur, prev, cur, pl.BlockSpec((None, G * W, 1), lambda g, n: (g, 0, 0)), qs, lsp, qs],
        out_specs=[qs, kvo, kvo, pl.BlockSpec((None, G, LANE), lambda g, n: (g, 0, 0))],
        out_shape=[jax.ShapeDtypeStruct((SWA_HEADS, S, Dh), BF16),
                   jax.ShapeDtypeStruct((SWA_KV_HEADS, S + W, Dh), F32),
                   jax.ShapeDtypeStruct((SWA_KV_HEADS, S + W, Dh), F32),
                   jax.ShapeDtypeStruct((SWA_KV_HEADS, G, LANE), F32)],
        compiler_params=_cparams(("parallel", "arbitrary")),
    )(qT, kT, kT, vT, vT, sink_rows, oT, L, doT)


def _adamw(w, g, m, v, *, name):
    R, C = w.shape
    tr = _pick(R, 256) if R % 8 == 0 else R
    c1 = 1.0 / (1.0 - ADAM_B1 ** ADAM_STEP)
    c2 = 1.0 / (1.0 - ADAM_B2 ** ADAM_STEP)

    def body(w_ref, g_ref, m_ref, v_ref, go_ref, d_ref, mo_ref, vo_ref):
        g_ = g_ref[...]
        mn = ADAM_B1 * m_ref[...] + (1.0 - ADAM_B1) * g_
        vn = ADAM_B2 * v_ref[...] + (1.0 - ADAM_B2) * (g_ * g_)
        go_ref[...] = g_
        mo_ref[...] = mn
        vo_ref[...] = vn
        d_ref[...] = -ADAM_LR * ((mn * c1) / (jnp.sqrt(vn * c2) + ADAM_EPS) + ADAM_WD * w_ref[...])

    row = pl.BlockSpec((tr, C), lambda i: (i, 0))
    return _call(
        body, name=name, grid=(R // tr,), in_specs=[row] * 4, out_specs=[row] * 4,
        out_shape=[jax.ShapeDtypeStruct((R, C), F32)] * 4,
        compiler_params=_cparams(("parallel",)),
    )(w, g, m, v)


def _rowsum(parts, *, name, out_dtype=F32):
    stacked = not isinstance(parts, (list, tuple))
    if stacked:
        n, R, C = parts.shape
    else:
        n = len(parts)
        R, C = parts[0].shape
    tr = _pick(R, 256)

    def body(*refs):
        if stacked:
            acc = refs[0][0].astype(F32)
            for i in range(1, n):
                acc = acc + refs[0][i].astype(F32)
            refs[1][...] = acc.astype(out_dtype)
        else:
            acc = refs[0][...].astype(F32)
            for i in range(1, n):
                acc = acc + refs[i][...].astype(F32)
            refs[n][...] = acc.astype(out_dtype)

    row = pl.BlockSpec((tr, C), lambda i: (i, 0))
    if stacked:
        in_specs = [pl.BlockSpec((n, tr, C), lambda i: (0, i, 0))]
        args = (parts,)
    else:
        in_specs = [row] * n
        args = tuple(parts)
    return _call(
        body, name=name, grid=(R // tr,), in_specs=in_specs, out_specs=row,
        out_shape=jax.ShapeDtypeStruct((R, C), out_dtype),
        compiler_params=_cparams(("parallel",)),
    )(*args)


def _where_am_i():
    x, y, c = lax.axis_index("x"), lax.axis_index("y"), lax.axis_index("c")
    chips = [(1 - x, y), (x, 1 - y), (1 - x, 1 - y)]
    return x, y, c, chips


def _all_gather_shards(shards, *, name):
    n = len(shards)

    def body(*refs):
        ins, outs = refs[:n], refs[n:2 * n]
        send, recv, lsem = refs[2 * n:]
        x, y, c, chips = _where_am_i()
        me = 2 * x + y
        sibling = (x, y, 1 - c)

        def half(i, which):
            hr = shards[i].shape[0] // 2
            return pl.ds(pl.multiple_of(which * hr, 16), hr)

        def cp(i, k, src, dst, to):
            return pltpu.make_async_remote_copy(src_ref=src, dst_ref=dst, send_sem=send.at[i, k],
                                                recv_sem=recv.at[i, k], device_id=to, device_id_type=MESH)

        local = [pltpu.make_async_copy(ins[i], outs[i].at[me], lsem.at[i]) for i in range(n)]
        for l in local:
            l.start()
        first = []
        for i in range(n):
            for k, (px, py) in enumerate(chips):
                d = cp(i, k, ins[i].at[half(i, c)], outs[i].at[me, half(i, c)], (px, py, c))
                d.start()
                first.append(d)
        passed = []
        for i in range(n):
            for k, (px, py) in enumerate(chips):
                blk = outs[i].at[2 * px + py, half(i, c)]
                cp(i, k, blk, blk, (px, py, c)).wait_recv()
                d = cp(i, 3 + k, blk, blk, sibling)
                d.start()
                passed.append(d)
        for i in range(n):
            for k, (px, py) in enumerate(chips):
                blk = outs[i].at[2 * px + py, half(i, 1 - c)]
                cp(i, 3 + k, blk, blk, sibling).wait_recv()
        for d in first + passed:
            d.wait_send()
        for l in local:
            l.wait()

    return _call(
        body, name=name, in_specs=[ANY] * n, out_specs=[ANY] * n,
        out_shape=[jax.ShapeDtypeStruct((N_CHIPS,) + s.shape, s.dtype) for s in shards],
        scratch_shapes=[pltpu.SemaphoreType.DMA((n, 6)), pltpu.SemaphoreType.DMA((n, 6)),
                        pltpu.SemaphoreType.DMA((n,))],
    )(*shards)


def _sibling_send_halves(grads, *, name):
    n = len(grads)

    def body(*refs):
        ins, outs = refs[:n], refs[n:2 * n]
        send, recv = refs[2 * n:]
        x, y, c, _ = _where_am_i()
        sibling = (x, y, 1 - c)
        cps = []
        for i in range(n):
            hr = grads[i].shape[1] // 2
            src = ins[i].at[:, pl.ds(pl.multiple_of((1 - c) * hr, 16), hr)]
            d = pltpu.make_async_remote_copy(src_ref=src, dst_ref=outs[i], send_sem=send.at[i],
                                             recv_sem=recv.at[i], device_id=sibling, device_id_type=MESH)
            d.start()
            cps.append(d)
        for d in cps:
            d.wait()

    return _call(
        body, name=name, in_specs=[ANY] * n, out_specs=[ANY] * n,
        out_shape=[jax.ShapeDtypeStruct((N_CHIPS, g.shape[1] // 2, g.shape[2]), g.dtype) for g in grads],
        scratch_shapes=[pltpu.SemaphoreType.DMA((n,)), pltpu.SemaphoreType.DMA((n,))],
    )(*grads)


def _scatter_to_chips(parts, *, name):
    n = len(parts)

    def body(*refs):
        ins, outs = refs[:n], refs[n:2 * n]
        send, recv, lsem = refs[2 * n:]
        x, y, c, chips = _where_am_i()
        me = 2 * x + y
        local = [pltpu.make_async_copy(ins[i].at[me], outs[i].at[me], lsem.at[i]) for i in range(n)]
        for l in local:
            l.start()
        cps = []
        for i in range(n):
            for k, (px, py) in enumerate(chips):
                d = pltpu.make_async_remote_copy(
                    src_ref=ins[i].at[2 * px + py], dst_ref=outs[i].at[me], send_sem=send.at[i, k],
                    recv_sem=recv.at[i, k], device_id=(px, py, c), device_id_type=MESH)
                d.start()
                cps.append((d, i, k, px, py))
        for d, i, k, px, py in cps:
            blk = outs[i].at[2 * px + py]
            pltpu.make_async_remote_copy(src_ref=blk, dst_ref=blk, send_sem=send.at[i, k], recv_sem=recv.at[i, k],
                                         device_id=(px, py, c), device_id_type=MESH).wait_recv()
        for d, *_ in cps:
            d.wait_send()
        for l in local:
            l.wait()

    return _call(
        body, name=name, in_specs=[ANY] * n, out_specs=[ANY] * n,
        out_shape=[jax.ShapeDtypeStruct(p.shape, p.dtype) for p in parts],
        scratch_shapes=[pltpu.SemaphoreType.DMA((n, 3)), pltpu.SemaphoreType.DMA((n, 3)),
                        pltpu.SemaphoreType.DMA((n,))],
    )(*parts)


def _sibling_join_halves(halves, *, name):
    n = len(halves)

    def body(*refs):
        ins, outs = refs[:n], refs[n:2 * n]
        send, recv, lsem = refs[2 * n:]
        x, y, c, _ = _where_am_i()
        sibling = (x, y, 1 - c)
        cps, local = [], []
        for i in range(n):
            hr = halves[i].shape[0]
            mine = outs[i].at[pl.ds(pl.multiple_of(c * hr, 8), hr)]
            l = pltpu.make_async_copy(ins[i], mine, lsem.at[i])
            l.start()
            local.append(l)
            d = pltpu.make_async_remote_copy(src_ref=ins[i], dst_ref=mine, send_sem=send.at[i],
                                             recv_sem=recv.at[i], device_id=sibling, device_id_type=MESH)
            d.start()
            cps.append(d)
        for i, d in enumerate(cps):
            hr = halves[i].shape[0]
            other = outs[i].at[pl.ds(pl.multiple_of((1 - c) * hr, 8), hr)]
            pltpu.make_async_remote_copy(src_ref=other, dst_ref=other, send_sem=send.at[i], recv_sem=recv.at[i],
                                         device_id=sibling, device_id_type=MESH).wait_recv()
            d.wait_send()
        for l in local:
            l.wait()

    return _call(
        body, name=name, in_specs=[ANY] * n, out_specs=[ANY] * n,
        out_shape=[jax.ShapeDtypeStruct((2 * h.shape[0], h.shape[1]), h.dtype) for h in halves],
        scratch_shapes=[pltpu.SemaphoreType.DMA((n,)), pltpu.SemaphoreType.DMA((n,)),
                        pltpu.SemaphoreType.DMA((n,))],
    )(*halves)


def _all_reduce_small(v, *, name):
    R, C = v.shape

    def body(v_ref, o_ref, sib, slots, send, recv):
        x, y, c, chips = _where_am_i()
        me = 2 * x + y
        sibling = (x, y, 1 - c)
        d = pltpu.make_async_remote_copy(src_ref=v_ref, dst_ref=sib, send_sem=send.at[0], recv_sem=recv.at[0],
                                         device_id=sibling, device_id_type=MESH)
        d.start()
        d.wait()
        slots[me] = v_ref[...] + sib[...]
        cps = []
        for k, (px, py) in enumerate(chips):
            d = pltpu.make_async_remote_copy(src_ref=slots.at[me], dst_ref=slots.at[me], send_sem=send.at[1 + k],
                                             recv_sem=recv.at[1 + k], device_id=(px, py, c), device_id_type=MESH)
            d.start()
            cps.append(d)
        for k, (px, py) in enumerate(chips):
            blk = slots.at[2 * px + py]
            pltpu.make_async_remote_copy(src_ref=blk, dst_ref=blk, send_sem=send.at[1 + k], recv_sem=recv.at[1 + k],
                                         device_id=(px, py, c), device_id_type=MESH).wait_recv()
        for d in cps:
            d.wait_send()
        o_ref[...] = (slots[0] + slots[1]) + (slots[2] + slots[3])

    vm = pl.BlockSpec(memory_space=pltpu.VMEM)
    return _call(
        body, name=name, in_specs=[vm], out_specs=vm,
        out_shape=jax.ShapeDtypeStruct((R, C), F32),
        scratch_shapes=[pltpu.VMEM((R, C), F32), pltpu.VMEM((N_CHIPS, R, C), F32),
                        pltpu.SemaphoreType.DMA((4,)), pltpu.SemaphoreType.DMA((4,))],
        compiler_params=pltpu.CompilerParams(vmem_limit_bytes=VMEM_LIMIT),
    )(v)


def _cols_from_shards(g):
    return jnp.transpose(g, (1, 0, 2)).reshape(g.shape[1], -1)


def _shards_from_cols(w):
    R, C4 = w.shape
    return jnp.transpose(w.reshape(R, N_CHIPS, C4 // N_CHIPS), (1, 0, 2))


def _block_diag(t):
    G, a, b = t.shape
    eye = jnp.eye(G, dtype=t.dtype)
    return (t[:, :, None, :] * eye[:, None, :, None]).reshape(G * a, G * b)


def _diag_blocks(xm, G):
    a, b = xm.shape[0] // G, xm.shape[1] // G
    idx = jnp.arange(G)
    return xm.reshape(G, a, G, b)[idx, :, idx, :]


def _pack(arrs):
    flat = []
    for a in arrs:
        f = a.reshape(-1).astype(F32)
        flat.append(jnp.pad(f, (0, _rup(f.shape[0], LANE) - f.shape[0])))
    v = jnp.concatenate(flat)
    rows = _rup(v.shape[0] // LANE, 8)
    v = jnp.pad(v, (0, rows * LANE - v.shape[0]))
    return v.reshape(rows, LANE)


def _unpack(v, shapes):
    flat = v.reshape(-1)
    out, off = [], 0
    for s in shapes:
        n = int(np.prod(s))
        out.append(flat[off:off + n].reshape(s))
        off += _rup(n, LANE)
    return out


def _ffn_fwd(x, Wup, Wdn, cw, cb, tag):
    h = _mm(x, Wup, 'nn', bmode='bo', tm=512, tn=4096, name=f"ffn_up_{tag}")
    a = _act_fwd(h, cw, cb, name=f"ffn_act_{tag}")
    f = _mm(a, Wdn, 'nn', bmode='abr', tm=512, tn=1024, tk=4096, name=f"ffn_down_{tag}")
    return f, h, a


def _ffn_bwd(df, x, h, a, Wup, Wdn, cw, cb, tag):
    da = _mm(df, Wdn, 'nt', bmode='bo', tm=512, tn=4096, name=f"ffn_da_{tag}")
    dWdn = _mm(a, df, 'tn', bmode='ao', tm=4096, tn=512, name=f"ffn_dwdn_{tag}", out_dtype=BF16)
    da4 = da
    dh, dcw, dcb = _act_bwd(h, da4, cw, cb, name=f"ffn_actb_{tag}")
    dx = _mm(dh, Wup, 'nt', bmode='abr', tm=512, tn=1024, tk=4096, name=f"ffn_dx_{tag}")
    dWup = _mm(x, dh, 'tn', bmode='bo', tm=512, tn=4096, name=f"ffn_dwup_{tag}", out_dtype=BF16)
    return dx, dWup, dWdn, dcw, dcb


def kernel(x, positions, ev_w_in, ev_b_f, ev_lambda_re, ev_lambda_im, ev_log_step, ev_ssm_b_re, ev_ssm_b_im, ev_ssm_c_re, ev_ssm_c_im, ev_ssm_d, ev_w_glu, ev_w_out, od_w_in, od_sinks, od_w_out, ln_mix_g, ln_mix_b, ffn_w_up, ffn_conv_w, ffn_conv_b, ffn_w_down, ln_ffn_g, ln_ffn_b, loss_target, m_ev_w_in, m_ev_b_f, m_ev_lambda_re, m_ev_lambda_im, m_ev_log_step, m_ev_ssm_b_re, m_ev_ssm_b_im, m_ev_ssm_c_re, m_ev_ssm_c_im, m_ev_ssm_d, m_ev_w_glu, m_ev_w_out, m_od_w_in, m_od_sinks, m_od_w_out, m_ln_mix_g, m_ln_mix_b, m_ffn_w_up, m_ffn_conv_w, m_ffn_conv_b, m_ffn_w_down, m_ln_ffn_g, m_ln_ffn_b, v_ev_w_in, v_ev_b_f, v_ev_lambda_re, v_ev_lambda_im, v_ev_log_step, v_ev_ssm_b_re, v_ev_ssm_b_im, v_ev_ssm_c_re, v_ev_ssm_c_im, v_ev_ssm_d, v_ev_w_glu, v_ev_w_out, v_od_w_in, v_od_sinks, v_od_w_out, v_ln_mix_g, v_ln_mix_b, v_ffn_w_up, v_ffn_conv_w, v_ffn_conv_b, v_ffn_w_down, v_ln_ffn_g, v_ln_ffn_b):
    W = dict(ev_w_in=ev_w_in, ev_b_f=ev_b_f, ev_lambda_re=ev_lambda_re, ev_lambda_im=ev_lambda_im, ev_log_step=ev_log_step, ev_ssm_b_re=ev_ssm_b_re, ev_ssm_b_im=ev_ssm_b_im, ev_ssm_c_re=ev_ssm_c_re, ev_ssm_c_im=ev_ssm_c_im, ev_ssm_d=ev_ssm_d, ev_w_glu=ev_w_glu, ev_w_out=ev_w_out, od_w_in=od_w_in, od_sinks=od_sinks, od_w_out=od_w_out, ln_mix_g=ln_mix_g, ln_mix_b=ln_mix_b, ffn_w_up=ffn_w_up, ffn_conv_w=ffn_conv_w, ffn_conv_b=ffn_conv_b, ffn_w_down=ffn_w_down, ln_ffn_g=ln_ffn_g, ln_ffn_b=ln_ffn_b)
    Mo = dict(ev_w_in=m_ev_w_in, ev_b_f=m_ev_b_f, ev_lambda_re=m_ev_lambda_re, ev_lambda_im=m_ev_lambda_im, ev_log_step=m_ev_log_step, ev_ssm_b_re=m_ev_ssm_b_re, ev_ssm_b_im=m_ev_ssm_b_im, ev_ssm_c_re=m_ev_ssm_c_re, ev_ssm_c_im=m_ev_ssm_c_im, ev_ssm_d=m_ev_ssm_d, ev_w_glu=m_ev_w_glu, ev_w_out=m_ev_w_out, od_w_in=m_od_w_in, od_sinks=m_od_sinks, od_w_out=m_od_w_out, ln_mix_g=m_ln_mix_g, ln_mix_b=m_ln_mix_b, ffn_w_up=m_ffn_w_up, ffn_conv_w=m_ffn_conv_w, ffn_conv_b=m_ffn_conv_b, ffn_w_down=m_ffn_w_down, ln_ffn_g=m_ln_ffn_g, ln_ffn_b=m_ln_ffn_b)
    Vo = dict(ev_w_in=v_ev_w_in, ev_b_f=v_ev_b_f, ev_lambda_re=v_ev_lambda_re, ev_lambda_im=v_ev_lambda_im, ev_log_step=v_ev_log_step, ev_ssm_b_re=v_ev_ssm_b_re, ev_ssm_b_im=v_ev_ssm_b_im, ev_ssm_c_re=v_ev_ssm_c_re, ev_ssm_c_im=v_ev_ssm_c_im, ev_ssm_d=v_ev_ssm_d, ev_w_glu=v_ev_w_glu, ev_w_out=v_ev_w_out, od_w_in=v_od_w_in, od_sinks=v_od_sinks, od_w_out=v_od_w_out, ln_mix_g=v_ln_mix_g, ln_mix_b=v_ln_mix_b, ffn_w_up=v_ffn_w_up, ffn_conv_w=v_ffn_conv_w, ffn_conv_b=v_ffn_conv_b, ffn_w_down=v_ffn_w_down, ln_ffn_g=v_ln_ffn_g, ln_ffn_b=v_ln_ffn_b)
    names = list(W.keys())
    big = ['ev_w_in', 'ev_w_glu', 'ev_w_out', 'od_w_in', 'od_w_out', 'ffn_w_up', 'ffn_w_down']

    S, D = x.shape[1], x.shape[2]
    x0 = x.reshape(S, D)
    tgt = loss_target.reshape(S, D)
    G, Pn, Cg = SSM_GROUPS, SSM_STATE, SSM_GROUP
    Fs = ffn_w_up.shape[2]
    FP = _rup(Fs, 256)
    Rd = ffn_w_down.shape[1]
    EIN = N_CHIPS * ev_w_in.shape[2]

    def as2d(a):
        return a.reshape(-1, a.shape[-1])

    cwl = ffn_conv_w.reshape(-1)
    cw_rows = _rup(_rup(cwl.shape[0], LANE) // LANE, 32)
    cw_pad = jnp.pad(cwl, (0, cw_rows * LANE - cwl.shape[0])).reshape(cw_rows, LANE)
    gathered = _all_gather_shards([as2d(W[n]).astype(BF16) for n in big] + [cw_pad], name="ag_weights")
    gw = dict(zip(big, gathered[:-1]))
    w_in_full = _cols_from_shards(gw['ev_w_in'])
    qkv_w = 3 * FOX_WIDTH
    Wmain = jnp.concatenate([w_in_full[:, :qkv_w], w_in_full[:, qkv_w + FOX_HEADS:]], axis=1)
    Wf = jnp.pad(w_in_full[:, qkv_w:qkv_w + FOX_HEADS], ((0, 0), (0, LANE - FOX_HEADS)))
    Wglu = _cols_from_shards(gw['ev_w_glu'])
    Wout_ev = gw['ev_w_out'].reshape(D, D)
    Wodin = _cols_from_shards(gw['od_w_in'])
    Wodout = gw['od_w_out'].reshape(D, D)
    Wup, Wdn, cws, cbs = [], [], [], []
    cw_all = gathered[-1].reshape(N_CHIPS, -1)[:, :cwl.shape[0]].reshape(N_CHIPS, DEPTH, 3, Fs)
    for l in range(DEPTH):
        up = gw['ffn_w_up'][:, l * D:(l + 1) * D, :]
        Wup.append(jnp.pad(up, ((0, 0), (0, 0), (0, FP - Fs))))
        dn = gw['ffn_w_down'][:, l * Rd:(l + 1) * Rd, :].reshape(2, Fs, D)
        Wdn.append(jnp.pad(dn, ((0, 0), (0, FP - Fs), (0, 0))))
        cws.append(jnp.pad(cw_all[:, l], ((0, 0), (0, 0), (0, FP - Fs))))
        cbs.append(jnp.pad(ffn_conv_b[l].reshape(N_CHIPS, Fs), ((0, 0), (0, FP - Fs))))

    lam_r, lam_i = ev_lambda_re[0], ev_lambda_im[0]
    lstep = ev_log_step[0].reshape(G, 1)
    a_re, a_im, g_re, g_im = _s5_disc_fwd(lam_r, lam_i, lstep, name="s5_disc")
    b_re2, b_im2 = ev_ssm_b_re[0].reshape(G * Pn, Cg), ev_ssm_b_im[0].reshape(G * Pn, Cg)
    g_re1, g_im1 = g_re.reshape(G * Pn, 1), g_im.reshape(G * Pn, 1)
    bb_re, bb_im = _s5_bb_fwd(g_re1, g_im1, b_re2, b_im2, name="s5_bb")
    BB = jnp.stack([_block_diag(jnp.transpose(bb_re.reshape(G, Pn, Cg), (0, 2, 1))),
                    _block_diag(jnp.transpose(bb_im.reshape(G, Pn, Cg), (0, 2, 1)))]).astype(BF16)
    CC = jnp.stack([_block_diag(jnp.transpose(ev_ssm_c_re[0], (0, 2, 1))),
                    _block_diag(jnp.transpose(-ev_ssm_c_im[0], (0, 2, 1)))]).astype(BF16)
    a_cat = jnp.stack([a_re.reshape(1, G * Pn), a_im.reshape(1, G * Pn)])
    dskip = ev_ssm_d[0].reshape(1, SSM_WIDTH)

    P = _mm(x0, Wmain, 'nn', name="ev_proj")
    fl = _mm(x0, Wf, 'nn', name="ev_proj_f")
    bf_pad = jnp.pad(ev_b_f.reshape(1, FOX_HEADS), ((0, 0), (0, LANE - FOX_HEADS)))
    cgate, sgate = _gate_fwd(fl, bf_pad, name="fox_gate")
    ccol = jnp.transpose(cgate[:, :FOX_HEADS]).reshape(FOX_HEADS, S, 1)
    crow = jnp.transpose(cgate[:, :FOX_HEADS]).reshape(FOX_HEADS, 1, S)
    fox, lse = _fox_fwd(P, ccol, crow, name="fox_fwd")
    u_s5 = P[:, qkv_w:]
    bu = _mm(u_s5, BB, 'nn', bmode='bo', name="s5_bu")
    hh = _s5_scan_fwd(bu, a_cat, name="s5_scan")
    yc = _mm(hh, CC, 'nn', bmode='abr', name="s5_y")
    y_s5, yg = _s5_out_fwd(yc, P, dskip, name="s5_out")
    z = _mm(yg, Wglu, 'nn', name="s5_glu_proj")
    ssm = _glu_fwd(z, name="s5_glu")
    cat = jnp.concatenate([fox.astype(BF16), ssm], axis=1)
    mix0 = _mm(cat, Wout_ev, 'nn', name="ev_out")
    x1, xh1, rs1 = _add_ln_fwd(x0, mix0, ln_mix_g[0], ln_mix_b[0], name="ln_mix0")
    f0, hf0, af0 = _ffn_fwd(x1, Wup[0], Wdn[0], cws[0], cbs[0], "l0")
    x2, xh2, rs2 = _add_ln_fwd(x1, f0, ln_ffn_g[0], ln_ffn_b[0], name="ln_ffn0")

    QW, KW = SWA_HEADS * SWA_HEAD_DIM, SWA_KV_HEADS * SWA_HEAD_DIM
    P1 = _mm(x2, Wodin, 'nn', name="od_proj")
    tabs = _rope_tables(positions.reshape(S, 1).astype(F32), name="rope_tables")
    qr = _rope_apply(P1, tabs, col0=0, width=QW, inverse=False, name="rope_q", out_dtype=BF16)
    kr = _rope_apply(P1, tabs, col0=QW, width=KW, inverse=False, name="rope_k", out_dtype=BF16)

    def heads(a2, nh):
        return jnp.transpose(a2.reshape(S, nh, SWA_HEAD_DIM), (1, 0, 2))

    def unheads(a3):
        return jnp.transpose(a3, (1, 0, 2)).reshape(S, -1)

    qT, kT = heads(qr, SWA_HEADS), heads(kr, SWA_KV_HEADS)
    vT = heads(P1[:, QW + KW:].astype(BF16), SWA_KV_HEADS)
    sink_rows = jnp.broadcast_to(od_sinks[0].reshape(SWA_KV_HEADS, SWA_GROUPS, 1, 1),
                                 (SWA_KV_HEADS, SWA_GROUPS, SWA_WINDOW, 1)).reshape(SWA_KV_HEADS, -1, 1)
    oT, Lsw = _swa_fwd(qT, kT, vT, sink_rows, name="swa_fwd")
    o_sw = unheads(oT).astype(BF16)
    mix1 = _mm(o_sw, Wodout, 'nn', name="od_out")
    x3, xh3, rs3 = _add_ln_fwd(x2, mix1, ln_mix_g[1], ln_mix_b[1], name="ln_mix1")
    f1, hf1, af1 = _ffn_fwd(x3, Wup[1], Wdn[1], cws[1], cbs[1], "l1")
    x4, xh4, rs4 = _add_ln_fwd(x3, f1, ln_ffn_g[1], ln_ffn_b[1], name="ln_ffn1")
    dy, loss_part = _loss_grad(x4, tgt, name="loss")

    dz4, dg_ffn1, db_ffn1 = _ln_bwd(dy, None, xh4, rs4, ln_ffn_g[1], name="lnb_ffn1")
    dx3f, dWup1, dWdn1, dcw1, dcb1 = _ffn_bwd(dz4, x3, hf1, af1, Wup[1], Wdn[1], cws[1], cbs[1], "l1")
    dz3, dg_mix1, db_mix1 = _ln_bwd(dz4, dx3f, xh3, rs3, ln_mix_g[1], name="lnb_mix1")
    do_sw = _mm(dz3, Wodout, 'nt', name="od_out_dx")
    dWodout = _mm(o_sw, dz3, 'tn', name="od_out_dw", out_dtype=BF16)
    doT = heads(do_sw, SWA_HEADS)
    dqT, dkT, dvT, dsink = _swa_bwd(qT, kT, vT, sink_rows, oT, Lsw, doT, name="swa_bwd")
    dq1 = _rope_apply(unheads(dqT), tabs, col0=0, width=QW, inverse=True, name="rope_dq", out_dtype=BF16)
    dk1 = _rope_apply(unheads(dkT[:, SWA_WINDOW:]), tabs, col0=0, width=KW, inverse=True, name="rope_dk",
                      out_dtype=BF16)
    dP1 = jnp.concatenate([dq1, dk1, unheads(dvT[:, SWA_WINDOW:]).astype(BF16)], axis=1)
    dx2m = _mm(dP1, Wodin, 'nt', name="od_proj_dx")
    dWodin = _mm(x2, dP1, 'tn', name="od_proj_dw", out_dtype=BF16)

    dz2, dg_ffn0, db_ffn0 = _ln_bwd(dz3, dx2m, xh2, rs2, ln_ffn_g[0], name="lnb_ffn0")
    dx1f, dWup0, dWdn0, dcw0, dcb0 = _ffn_bwd(dz2, x1, hf0, af0, Wup[0], Wdn[0], cws[0], cbs[0], "l0")
    dz1, dg_mix0, db_mix0 = _ln_bwd(dz2, dx1f, xh1, rs1, ln_mix_g[0], name="lnb_mix0")
    dcat = _mm(dz1, Wout_ev, 'nt', name="ev_out_dx")
    dWout_ev = _mm(cat, dz1, 'tn', name="ev_out_dw", out_dtype=BF16)
    dz = _glu_bwd(z, dcat, name="s5_glu_bwd")
    dyg = _mm(dz, Wglu, 'nt', name="s5_glu_dx")
    dWglu = _mm(yg, dz, 'tn', name="s5_glu_dw", out_dtype=BF16)
    dy_s5, du_dir, dD = _s5_out_bwd(dyg, y_s5, P, dskip, name="s5_out_bwd")
    dhh = _mm(dy_s5, CC, 'nt', bmode='bo', name="s5_y_dx")
    dCC = _mm(hh, dy_s5, 'tn', bmode='ao', name="s5_y_dw")
    lam, da_s5 = _s5_scan_bwd(dhh, hh, a_cat, name="s5_scan_bwd")
    du_bu = _mm(lam, BB, 'nt', bmode='abr', name="s5_bu_dx")
    dBB = _mm(u_s5, lam, 'tn', bmode='bo', name="s5_bu_dw")
    du = _combine([du_dir, du_bu], [1.0, 1.0], name="s5_du", out_dtype=BF16)
    dq0, dk0, dv0, dccol, dcrow = _fox_bwd(P, ccol, crow, fox, lse, dcat, name="fox_bwd")
    dc = jnp.transpose((dccol.reshape(FOX_HEADS, S) - dcrow.reshape(FOX_HEADS, S)))
    dc = jnp.pad(dc, ((0, 0), (0, LANE - FOX_HEADS)))
    dfl, dbf = _gate_bwd(dc, sgate, name="fox_gate_bwd")
    dP = jnp.concatenate([dq0, dk0, dv0, du], axis=1)
    dx0a = _mm(dP, Wmain, 'nt', name="ev_proj_dx")
    dx0b = _mm(dfl, Wf, 'nt', name="ev_proj_f_dx")
    dWmain = _mm(x0, dP, 'tn', name="ev_proj_dw", out_dtype=BF16)
    dWf = _mm(x0, dfl, 'tn', name="ev_proj_f_dw", out_dtype=BF16)
    grad_x = _combine([dz1, dx0a, dx0b], [ALPHA, 1.0, 1.0], name="grad_x")

    dbb_re = jnp.transpose(_diag_blocks(dBB[0], G), (0, 2, 1)).reshape(G * Pn, Cg)
    dbb_im = jnp.transpose(_diag_blocks(dBB[1], G), (0, 2, 1)).reshape(G * Pn, Cg)
    db_re, db_im, dg_re1, dg_im1 = _s5_bb_bwd(g_re1, g_im1, b_re2, b_im2, dbb_re, dbb_im, name="s5_bb_bwd")
    dlam_re, dlam_im, dlstep = _s5_disc_bwd(lam_r, lam_i, lstep, da_s5[0].reshape(G, Pn), da_s5[1].reshape(G, Pn),
                                            dg_re1.reshape(G, Pn), dg_im1.reshape(G, Pn), name="s5_disc_bwd")
    dc_re = jnp.transpose(_diag_blocks(dCC[0], G), (0, 2, 1))
    dc_im = -jnp.transpose(_diag_blocks(dCC[1], G), (0, 2, 1))

    def conv_w_full(d0, d1):
        return jnp.stack([jnp.reshape(jnp.transpose(d[:, :, :Fs], (1, 0, 2)), (3, N_CHIPS * Fs)) for d in (d0, d1)])

    def conv_b_full(d0, d1):
        return jnp.stack([jnp.reshape(d[:, 0, :Fs], (N_CHIPS * Fs,)) for d in (d0, d1)])

    small_local = dict(
        ev_b_f=dbf[:, :FOX_HEADS], ev_lambda_re=dlam_re, ev_lambda_im=dlam_im, ev_log_step=dlstep,
        ev_ssm_b_re=db_re, ev_ssm_b_im=db_im, ev_ssm_c_re=dc_re, ev_ssm_c_im=dc_im, ev_ssm_d=dD,
        od_sinks=dsink[:, :, 0],
        ln_mix_g=jnp.concatenate([dg_mix0, dg_mix1]), ln_mix_b=jnp.concatenate([db_mix0, db_mix1]),
        ffn_conv_w=conv_w_full(dcw0, dcw1), ffn_conv_b=conv_b_full(dcb0, dcb1),
        ln_ffn_g=jnp.concatenate([dg_ffn0, dg_ffn1]), ln_ffn_b=jnp.concatenate([db_ffn0, db_ffn1]))
    small = list(small_local.keys())
    red = _all_reduce_small(_pack([small_local[n] for n in small] + [loss_part]), name="ar_small")
    full_shapes = [W[n].shape if n != 'ffn_conv_w' else (DEPTH, 3, N_CHIPS * Fs) for n in small]
    pieces = _unpack(red, full_shapes + [()])
    loss = pieces[-1]
    gsmall = dict(zip(small, pieces[:-1]))
    chip = 2 * lax.axis_index("x") + lax.axis_index("y")
    gsmall['ffn_conv_w'] = lax.dynamic_slice_in_dim(gsmall['ffn_conv_w'], chip * Fs, Fs, axis=2)
    shapes = [W[n].shape for n in small]
    gs, ds_, ms, vs = _adamw(_pack([W[n] for n in small]), _pack([gsmall[n] for n in small]),
                             _pack([Mo[n] for n in small]), _pack([Vo[n] for n in small]), name="adamw_small")
    out_g = dict(zip(small, _unpack(gs, shapes)))
    out_d = dict(zip(small, _unpack(ds_, shapes)))
    out_m = dict(zip(small, _unpack(ms, shapes)))
    out_v = dict(zip(small, _unpack(vs, shapes)))

    dw_in_full = jnp.concatenate([dWmain[:, :qkv_w], dWf[:, :FOX_HEADS], dWmain[:, qkv_w:]], axis=1)
    gl = dict(
        ev_w_in=_shards_from_cols(dw_in_full), ev_w_glu=_shards_from_cols(dWglu),
        ev_w_out=dWout_ev.reshape(N_CHIPS, D // N_CHIPS, D), od_w_in=_shards_from_cols(dWodin),
        od_w_out=dWodout.reshape(N_CHIPS, D // N_CHIPS, D),
        ffn_w_up=jnp.stack([dWup0[:, :, :Fs], dWup1[:, :, :Fs]], axis=1).reshape(N_CHIPS, DEPTH * D, Fs),
        ffn_w_down=jnp.stack([dWdn0[:, :Fs].reshape(N_CHIPS, Rd, D), dWdn1[:, :Fs].reshape(N_CHIPS, Rd, D)],
                             axis=1).reshape(N_CHIPS, DEPTH * Rd, D))
    glist = [gl[n] for n in big]
    sib = _sibling_send_halves(glist, name="rs_sibling")
    core = lax.axis_index("c")
    part = []
    for n, g4, s4 in zip(big, glist, sib):
        hr = g4.shape[1] // 2
        mine = lax.dynamic_slice_in_dim(g4, core * hr, hr, axis=1)
        part.append(_rowsum([mine.reshape(-1, g4.shape[2]), s4.reshape(-1, g4.shape[2])], name=f"rs_sum2_{n}",
                            out_dtype=BF16).reshape(N_CHIPS, hr, g4.shape[2]))
    recv = _scatter_to_chips(part, name="rs_chips")
    halves = [_rowsum(r, name=f"rs_sum4_{n}") for n, r in zip(big, recv)]
    gfull = _sibling_join_halves(halves, name="rs_join")
    for n, gsh in zip(big, gfull):
        go, do_, mo, vo = _adamw(as2d(W[n]), gsh, as2d(Mo[n]), as2d(Vo[n]), name=f"adamw_{n}")
        out_g[n], out_d[n], out_m[n], out_v[n] = (t.reshape(W[n].shape) for t in (go, do_, mo, vo))

    return (loss, grad_x.reshape(1, S, D), *[out_g[n] for n in names], *[out_d[n] for n in names],
            *[out_m[n] for n in names], *[out_v[n] for n in names])
```

```python
import functools
import math

import numpy as np
import jax
import jax.numpy as jnp
from jax import lax
from jax.experimental import pallas as pl
from jax.experimental.pallas import tpu as pltpu

F32 = jnp.float32
BF16 = jnp.bfloat16
MESH = pl.DeviceIdType.MESH
ANY = pl.BlockSpec(memory_space=pl.ANY)

D_MODEL = 2048
FOX_HEADS = 8
FOX_HEAD_DIM = 128
FOX_WIDTH = 1024
SSM_WIDTH = 1024
SSM_GROUP = 16
SSM_GROUPS = 64
SSM_STATE = 64
SWA_HEADS = 32
SWA_KV_HEADS = 4
SWA_HEAD_DIM = 64
SWA_GROUPS = 8
SWA_WINDOW = 128
ROPE_DIM = 16
ROPE_THETA = 500000.0
LN_EPS = 1e-5
DEPTH = 2
ALPHA = (2.0 * DEPTH) ** 0.25
ADAM_LR = 0.001
ADAM_B1 = 0.9
ADAM_B2 = 0.999
ADAM_EPS = 1e-08
ADAM_WD = 0.01
ADAM_STEP = 10
N_CHIPS = 4

VMEM_LIMIT = 56 * 1024 * 1024
LANE = 128


def _call(body, **kw):
    return pl.pallas_call(body, **kw)


def _cparams(sem):
    return pltpu.CompilerParams(dimension_semantics=sem, vmem_limit_bytes=VMEM_LIMIT)


def _rup(n, m):
    return (n + m - 1) // m * m


def _pick(n, pref):
    if n <= pref:
        return n
    for step in (128, 16, 8):
        for t in range(pref - pref % step, 0, -step):
            if n % t == 0:
                return t
    return n


def _tile2d(rows, cols, pref_rows=256, budget=256 * 1024):
    tr = _pick(rows, pref_rows)
    if tr < 64:
        tr = rows
    if cols % LANE:
        return tr, cols
    return tr, _pick(cols, max(LANE, budget // tr // LANE * LANE))


def _mm(a, b, mode, *, name, tm=512, tn=1024, tk=2048, bmode=None, out_dtype=F32):
    a3 = a if a.ndim == 3 else a[None]
    b3 = b if b.ndim == 3 else b[None]
    if mode == 'tn':
        K, M = a3.shape[1:]
    else:
        M, K = a3.shape[1:]
    N = b3.shape[1] if mode == 'nt' else b3.shape[2]
    tm, tn, tk = _pick(M, tm), _pick(N, tn), _pick(K, tk)
    nb = max(a3.shape[0], b3.shape[0])
    nbo, nbr = (1, nb) if bmode == 'abr' else (nb, 1)
    nk = K // tk
    nred = nbr * nk
    a_b = bmode in ('ao', 'abr')
    b_b = bmode in ('bo', 'abr')
    o_b = bmode in ('bo', 'ao')

    def bsel(flag, bo, br):
        return (bo + br) if flag else 0

    if mode == 'tn':
        a_spec = pl.BlockSpec((None, tk, tm), lambda bo, i, j, br, k: (bsel(a_b, bo, br), k, i))
    else:
        a_spec = pl.BlockSpec((None, tm, tk), lambda bo, i, j, br, k: (bsel(a_b, bo, br), i, k))
    if mode == 'nt':
        b_spec = pl.BlockSpec((None, tn, tk), lambda bo, i, j, br, k: (bsel(b_b, bo, br), j, k))
    else:
        b_spec = pl.BlockSpec((None, tk, tn), lambda bo, i, j, br, k: (bsel(b_b, bo, br), k, j))
    o_spec = pl.BlockSpec((None, tm, tn), lambda bo, i, j, br, k: (bsel(o_b, bo, br), i, j))
    dn = {'nn': (((1,), (0,)), ((), ())), 'nt': (((1,), (1,)), ((), ())), 'tn': (((0,), (0,)), ((), ()))}[mode]

    def body(a_ref, b_ref, o_ref, *scratch):
        r = lax.dot_general(a_ref[...].astype(BF16), b_ref[...].astype(BF16), dn, preferred_element_type=F32)
        if nred == 1:
            o_ref[...] = r.astype(out_dtype)
        else:
            acc = scratch[0]
            step = pl.program_id(3) * nk + pl.program_id(4)

            @pl.when(step == 0)
            def _():
                acc[...] = r

            @pl.when(step > 0)
            def _():
                acc[...] += r

            @pl.when(step == nred - 1)
            def _():
                o_ref[...] = acc[...].astype(out_dtype)

    out = _call(
        body, name=name,
        grid=(nbo, M // tm, N // tn, nbr, nk),
        in_specs=[a_spec, b_spec], out_specs=o_spec,
        out_shape=jax.ShapeDtypeStruct((nbo if o_b else 1, M, N), out_dtype),
        scratch_shapes=[] if nred == 1 else [pltpu.VMEM((tm, tn), F32)],
        compiler_params=_cparams(("parallel", "parallel", "parallel", "arbitrary", "arbitrary")),
    )(a3, b3)
    return out if o_b else out[0]


def _add_ln_fwd(x, r, g, b, *, name):
    S, D = x.shape
    tr = _pick(S, 256)

    def body(x_ref, r_ref, g_ref, b_ref, o_ref, xh_ref, rs_ref):
        z = ALPHA * x_ref[...] + r_ref[...]
        mu = jnp.mean(z, axis=-1, keepdims=True)
        zc = z - mu
        var = jnp.mean(zc * zc, axis=-1, keepdims=True)
        rstd = lax.rsqrt(var + LN_EPS)
        xh = zc * rstd
        xh_ref[...] = xh
        rs_ref[...] = rstd
        o_ref[...] = xh * g_ref[...] + b_ref[...]

    row = pl.BlockSpec((tr, D), lambda i: (i, 0))
    vec = pl.BlockSpec((1, D), lambda i: (0, 0))
    return _call(
        body, name=name, grid=(S // tr,),
        in_specs=[row, row, vec, vec],
        out_specs=[row, row, pl.BlockSpec((tr, 1), lambda i: (i, 0))],
        out_shape=[jax.ShapeDtypeStruct((S, D), F32), jax.ShapeDtypeStruct((S, D), F32),
                   jax.ShapeDtypeStruct((S, 1), F32)],
        compiler_params=_cparams(("parallel",)),
    )(x, r, g.reshape(1, D), b.reshape(1, D))


def _ln_bwd(da, db, xhat, rstd, g, *, name):
    S, D = xhat.shape
    tr = _pick(S, 256)
    two = db is not None

    def body(*refs):
        if two:
            da_ref, db_ref, xh_ref, rs_ref, g_ref, dz_ref, dg_ref, dbt_ref = refs
            dy = ALPHA * da_ref[...] + db_ref[...]
        else:
            da_ref, xh_ref, rs_ref, g_ref, dz_ref, dg_ref, dbt_ref = refs
            dy = da_ref[...]
        xh = xh_ref[...]
        dxh = dy * g_ref[...]
        m1 = jnp.mean(dxh, axis=-1, keepdims=True)
        m2 = jnp.mean(dxh * xh, axis=-1, keepdims=True)
        dz_ref[...] = rs_ref[...] * (dxh - m1 - xh * m2)
        pg = jnp.sum(dy * xh, axis=0, keepdims=True)
        pb = jnp.sum(dy, axis=0, keepdims=True)

        @pl.when(pl.program_id(0) == 0)
        def _():
            dg_ref[...] = pg
            dbt_ref[...] = pb

        @pl.when(pl.program_id(0) > 0)
        def _():
            dg_ref[...] += pg
            dbt_ref[...] += pb

    row = pl.BlockSpec((tr, D), lambda i: (i, 0))
    vec = pl.BlockSpec((1, D), lambda i: (0, 0))
    ins = [da] + ([db] if two else []) + [xhat, rstd, g.reshape(1, D)]
    in_specs = [row] + ([row] if two else []) + [row, pl.BlockSpec((tr, 1), lambda i: (i, 0)), vec]
    return _call(
        body, name=name, grid=(S // tr,),
        in_specs=in_specs, out_specs=[row, vec, vec],
        out_shape=[jax.ShapeDtypeStruct((S, D), F32), jax.ShapeDtypeStruct((1, D), F32),
                   jax.ShapeDtypeStruct((1, D), F32)],
        compiler_params=_cparams(("arbitrary",)),
    )(*ins)


def _loss_grad(y, t, *, name):
    S, D = y.shape
    tr = _pick(S, 256)

    def body(y_ref, t_ref, dy_ref, l_ref):
        e = y_ref[...] - t_ref[...]
        dy_ref[...] = e * (1.0 / D)
        part = 0.5 * jnp.sum(jnp.sum(e * e, axis=-1, keepdims=True) * (1.0 / D), axis=0, keepdims=True)

        @pl.when(pl.program_id(0) == 0)
        def _():
            l_ref[...] = part

        @pl.when(pl.program_id(0) > 0)
        def _():
            l_ref[...] += part

    row = pl.BlockSpec((tr, D), lambda i: (i, 0))
    return _call(
        body, name=name, grid=(S // tr,), in_specs=[row, row],
        out_specs=[row, pl.BlockSpec((1, 1), lambda i: (0, 0))],
        out_shape=[jax.ShapeDtypeStruct((S, D), F32), jax.ShapeDtypeStruct((1, 1), F32)],
        compiler_params=_cparams(("arbitrary",)),
    )(y, t)


def _combine(terms, scales, *, name, out_dtype=F32):
    S, D = terms[0].shape
    tr = _pick(S, 256)
    n = len(terms)

    def body(*refs):
        acc = scales[0] * refs[0][...].astype(F32)
        for i in range(1, n):
            acc = acc + scales[i] * refs[i][...].astype(F32)
        refs[n][...] = acc.astype(out_dtype)

    row = pl.BlockSpec((tr, D), lambda i: (i, 0))
    return _call(
        body, name=name, grid=(S // tr,), in_specs=[row] * n, out_specs=row,
        out_shape=jax.ShapeDtypeStruct((S, D), out_dtype),
        compiler_params=_cparams(("parallel",)),
    )(*terms)


def _split3(x):
    h = x.astype(BF16)
    r = x - h.astype(F32)
    m = r.astype(BF16)
    l = (r - m.astype(F32)).astype(BF16)
    return h, m, l


def _tri_matmul(tri_bf, x):
    h, m, l = _split3(x)
    dn = (((1,), (0,)), ((), ()))
    return (lax.dot_general(tri_bf, l, dn, preferred_element_type=F32)
            + lax.dot_general(tri_bf, m, dn, preferred_element_type=F32)
            + lax.dot_general(tri_bf, h, dn, preferred_element_type=F32))


def _gate_fwd(fl, bf, *, name):
    S = fl.shape[0]
    tc = _pick(S, 256)
    nchunk = S // tc

    def body(fl_ref, bf_ref, c_ref, sg_ref):
        r = lax.broadcasted_iota(jnp.int32, (tc, tc), 0)
        cidx = lax.broadcasted_iota(jnp.int32, (tc, tc), 1)
        tri = (r >= cidx).astype(BF16)
        carry = jnp.zeros((1, LANE), F32)
        for ch in range(nchunk):
            x = fl_ref[pl.ds(ch * tc, tc), :] + bf_ref[...]
            lf = jnp.minimum(x, 0.0) - jnp.log(1.0 + jnp.exp(-jnp.abs(x)))
            sg_ref[pl.ds(ch * tc, tc), :] = jax.nn.sigmoid(-x)
            c_ref[pl.ds(ch * tc, tc), :] = _tri_matmul(tri, lf) + carry
            carry = carry + jnp.sum(lf, axis=0, keepdims=True)

    full = pl.BlockSpec((S, LANE), lambda: (0, 0))
    return _call(
        body, name=name, in_specs=[full, pl.BlockSpec((1, LANE), lambda: (0, 0))], out_specs=[full, full],
        out_shape=[jax.ShapeDtypeStruct((S, LANE), F32)] * 2,
        compiler_params=pltpu.CompilerParams(vmem_limit_bytes=VMEM_LIMIT),
    )(fl, bf)


def _gate_bwd(dc, sg, *, name):
    S = dc.shape[0]
    tc = _pick(S, 256)
    nchunk = S // tc

    def body(dc_ref, sg_ref, dfl_ref, db_ref):
        r = lax.broadcasted_iota(jnp.int32, (tc, tc), 0)
        cidx = lax.broadcasted_iota(jnp.int32, (tc, tc), 1)
        tri = (r <= cidx).astype(BF16)
        carry = jnp.zeros((1, LANE), F32)
        dbacc = jnp.zeros((1, LANE), F32)
        for ch in reversed(range(nchunk)):
            d = dc_ref[pl.ds(ch * tc, tc), :]
            dfl = (_tri_matmul(tri, d) + carry) * sg_ref[pl.ds(ch * tc, tc), :]
            dfl_ref[pl.ds(ch * tc, tc), :] = dfl
            dbacc = dbacc + jnp.sum(dfl, axis=0, keepdims=True)
            carry = carry + jnp.sum(d, axis=0, keepdims=True)
        db_ref[...] = dbacc

    full = pl.BlockSpec((S, LANE), lambda: (0, 0))
    return _call(
        body, name=name, in_specs=[full, full], out_specs=[full, pl.BlockSpec((1, LANE), lambda: (0, 0))],
        out_shape=[jax.ShapeDtypeStruct((S, LANE), F32), jax.ShapeDtypeStruct((1, LANE), F32)],
        compiler_params=pltpu.CompilerParams(vmem_limit_bytes=VMEM_LIMIT),
    )(dc, sg)


def _fox_scores(q_ref, k_ref, cc_ref, cr_ref, qi, tq, S):
    scale = 1.0 / math.sqrt(FOX_HEAD_DIM)
    s = lax.dot_general(q_ref[...].astype(BF16), k_ref[...].astype(BF16), (((1,), (1,)), ((), ())),
                        preferred_element_type=F32) * scale
    s = s + cc_ref[...] - cr_ref[...]
    row = lax.broadcasted_iota(jnp.int32, (tq, S), 0) + qi * tq
    col = lax.broadcasted_iota(jnp.int32, (tq, S), 1)
    return s, row >= col


def _fox_fwd(P, ccol, crow, *, name):
    S = P.shape[0]
    tq = _pick(S, 256)
    H = FOX_HEADS

    def body(q_ref, k_ref, v_ref, cc_ref, cr_ref, o_ref, l_ref):
        s, causal = _fox_scores(q_ref, k_ref, cc_ref, cr_ref, pl.program_id(1), tq, S)
        s = jnp.where(causal, s, -1e30)
        m = jnp.max(s, axis=-1, keepdims=True)
        e = jnp.exp(s - m)
        den = jnp.sum(e, axis=-1, keepdims=True)
        p = e / den
        o_ref[...] = jnp.dot(p.astype(BF16), v_ref[...].astype(BF16), preferred_element_type=F32)
        l_ref[...] = m + jnp.log(den)

    return _call(
        body, name=name, grid=(H, S // tq),
        in_specs=[pl.BlockSpec((tq, 128), lambda h, i: (i, h)),
                  pl.BlockSpec((S, 128), lambda h, i: (0, H + h)),
                  pl.BlockSpec((S, 128), lambda h, i: (0, 2 * H + h)),
                  pl.BlockSpec((None, tq, 1), lambda h, i: (h, i, 0)),
                  pl.BlockSpec((None, 1, S), lambda h, i: (h, 0, 0))],
        out_specs=[pl.BlockSpec((tq, 128), lambda h, i: (i, h)),
                   pl.BlockSpec((None, tq, 1), lambda h, i: (h, i, 0))],
        out_shape=[jax.ShapeDtypeStruct((S, FOX_WIDTH), F32), jax.ShapeDtypeStruct((H, S, 1), F32)],
        compiler_params=_cparams(("parallel", "parallel")),
    )(P, P, P, ccol, crow)


def _fox_bwd(P, ccol, crow, o, lse, dcat, *, name):
    S = P.shape[0]
    tq = _pick(S, 256)
    H = FOX_HEADS
    nq = S // tq
    scale = 1.0 / math.sqrt(FOX_HEAD_DIM)

    def body(q_ref, k_ref, v_ref, cc_ref, cr_ref, o_ref, l_ref, do_ref,
             dq_ref, dk_ref, dv_ref, dcc_ref, dcr_ref, dk_acc, dv_acc):
        qi = pl.program_id(1)
        s, causal = _fox_scores(q_ref, k_ref, cc_ref, cr_ref, qi, tq, S)
        p = jnp.where(causal, jnp.exp(s - l_ref[...]), 0.0)
        do = do_ref[...]
        do_bf = do.astype(BF16)
        dp = lax.dot_general(do_bf, v_ref[...].astype(BF16), (((1,), (1,)), ((), ())), preferred_element_type=F32)
        delta = jnp.sum(do * o_ref[...], axis=-1, keepdims=True)
        ds = p * (dp - delta)
        ds_bf = ds.astype(BF16)
        dq_ref[...] = (jnp.dot(ds_bf, k_ref[...].astype(BF16), preferred_element_type=F32) * scale).astype(BF16)
        dkp = lax.dot_general(ds_bf, q_ref[...].astype(BF16), (((0,), (0,)), ((), ())),
                              preferred_element_type=F32) * scale
        dvp = lax.dot_general(p.astype(BF16), do_bf, (((0,), (0,)), ((), ())), preferred_element_type=F32)
        dcc_ref[...] = jnp.sum(ds, axis=-1, keepdims=True)
        dcr = jnp.sum(ds, axis=0, keepdims=True)

        @pl.when(qi == 0)
        def _():
            dk_acc[...] = dkp
            dv_acc[...] = dvp
            dcr_ref[...] = dcr

        @pl.when(qi > 0)
        def _():
            dk_acc[...] += dkp
            dv_acc[...] += dvp
            dcr_ref[...] += dcr

        @pl.when(qi == nq - 1)
        def _():
            dk_ref[...] = dk_acc[...].astype(BF16)
            dv_ref[...] = dv_acc[...].astype(BF16)

    qblk = pl.BlockSpec((tq, 128), lambda h, i: (i, h))
    kvo = pl.BlockSpec((S, 128), lambda h, i: (0, h))
    col = pl.BlockSpec((None, tq, 1), lambda h, i: (h, i, 0))
    rowv = pl.BlockSpec((None, 1, S), lambda h, i: (h, 0, 0))
    return _call(
        body, name=name, grid=(H, nq),
        in_specs=[qblk,
                  pl.BlockSpec((S, 128), lambda h, i: (0, H + h)),
                  pl.BlockSpec((S, 128), lambda h, i: (0, 2 * H + h)),
                  col, rowv, qblk, col, qblk],
        out_specs=[qblk, kvo, kvo, col, rowv],
        out_shape=[jax.ShapeDtypeStruct((S, FOX_WIDTH), BF16)] * 3
        + [jax.ShapeDtypeStruct((H, S, 1), F32), jax.ShapeDtypeStruct((H, 1, S), F32)],
        scratch_shapes=[pltpu.VMEM((S, 128), F32), pltpu.VMEM((S, 128), F32)],
        compiler_params=_cparams(("parallel", "arbitrary")),
    )(P, P, P, ccol, crow, o, lse, dcat)


def _s5_disc_fwd(lr, li, ls, *, name):
    G, Pn = lr.shape

    def body(lr_ref, li_ref, ls_ref, ar_ref, ai_ref, gr_ref, gi_ref):
        lr_, li_ = lr_ref[...], li_ref[...]
        dt = jnp.exp(ls_ref[...])
        mag = jnp.exp(lr_ * dt)
        th = li_ * dt
        ar = mag * jnp.cos(th)
        ai = mag * jnp.sin(th)
        den = lr_ * lr_ + li_ * li_
        xr = ar - 1.0
        ar_ref[...] = ar
        ai_ref[...] = ai
        gr_ref[...] = (xr * lr_ + ai * li_) / den
        gi_ref[...] = (ai * lr_ - xr * li_) / den

    sq = pl.BlockSpec((G, Pn), lambda: (0, 0))
    return _call(
        body, name=name, in_specs=[sq, sq, pl.BlockSpec((G, 1), lambda: (0, 0))], out_specs=[sq] * 4,
        out_shape=[jax.ShapeDtypeStruct((G, Pn), F32)] * 4,
    )(lr, li, ls)


def _s5_disc_bwd(lr, li, ls, dar, dai, dgr, dgi, *, name):
    G, Pn = lr.shape

    def body(lr_ref, li_ref, ls_ref, dar_ref, dai_ref, dgr_ref, dgi_ref, dlr_ref, dli_ref, dls_ref):
        lr_, li_ = lr_ref[...], li_ref[...]
        dt = jnp.exp(ls_ref[...])
        mag = jnp.exp(lr_ * dt)
        th = li_ * dt
        ar = mag * jnp.cos(th)
        ai = mag * jnp.sin(th)
        den = lr_ * lr_ + li_ * li_
        xr = ar - 1.0
        xi = ai
        g_re = (xr * lr_ + xi * li_) / den
        g_im = (xi * lr_ - xr * li_) / den
        dgr_, dgi_ = dgr_ref[...], dgi_ref[...]
        dxr = (dgr_ * lr_ - dgi_ * li_) / den
        dxi = (dgr_ * li_ + dgi_ * lr_) / den
        dden = -(dgr_ * g_re + dgi_ * g_im) / den
        dlr = (dgr_ * xr + dgi_ * xi) / den + 2.0 * dden * lr_
        dli = (dgr_ * xi - dgi_ * xr) / den + 2.0 * dden * li_
        da_r = dar_ref[...] + dxr
        da_i = dai_ref[...] + dxi
        dmag_mag = da_r * ar + da_i * ai
        dth = da_i * ar - da_r * ai
        dlr_ref[...] = dlr + dmag_mag * dt
        dli_ref[...] = dli + dth * dt
        ddt = jnp.sum(dmag_mag * lr_ + dth * li_, axis=-1, keepdims=True)
        dls_ref[...] = ddt * dt

    sq = pl.BlockSpec((G, Pn), lambda: (0, 0))
    c1 = pl.BlockSpec((G, 1), lambda: (0, 0))
    return _call(
        body, name=name, in_specs=[sq, sq, c1, sq, sq, sq, sq], out_specs=[sq, sq, c1],
        out_shape=[jax.ShapeDtypeStruct((G, Pn), F32)] * 2 + [jax.ShapeDtypeStruct((G, 1), F32)],
    )(lr, li, ls, dar, dai, dgr, dgi)


def _s5_bb_fwd(gr, gi, br, bi, *, name):
    R, C = br.shape

    def body(gr_ref, gi_ref, br_ref, bi_ref, or_ref, oi_ref):
        g_r, g_i, b_r, b_i = gr_ref[...], gi_ref[...], br_ref[...], bi_ref[...]
        or_ref[...] = g_r * b_r - g_i * b_i
        oi_ref[...] = g_r * b_i + g_i * b_r

    w = pl.BlockSpec((R, C), lambda: (0, 0))
    c1 = pl.BlockSpec((R, 1), lambda: (0, 0))
    return _call(body, name=name, in_specs=[c1, c1, w, w], out_specs=[w, w],
                 out_shape=[jax.ShapeDtypeStruct((R, C), F32)] * 2)(gr, gi, br, bi)


def _s5_bb_bwd(gr, gi, br, bi, dbbr, dbbi, *, name):
    R, C = br.shape

    def body(gr_ref, gi_ref, br_ref, bi_ref, dr_ref, di_ref, dbr_ref, dbi_ref, dgr_ref, dgi_ref):
        g_r, g_i, b_r, b_i = gr_ref[...], gi_ref[...], br_ref[...], bi_ref[...]
        d_r, d_i = dr_ref[...], di_ref[...]
        dbr_ref[...] = g_r * d_r + g_i * d_i
        dbi_ref[...] = g_r * d_i - g_i * d_r
        dgr_ref[...] = jnp.sum(d_r * b_r + d_i * b_i, axis=-1, keepdims=True)
        dgi_ref[...] = jnp.sum(d_i * b_r - d_r * b_i, axis=-1, keepdims=True)

    w = pl.BlockSpec((R, C), lambda: (0, 0))
    c1 = pl.BlockSpec((R, 1), lambda: (0, 0))
    return _call(body, name=name, in_specs=[c1, c1, w, w, w, w], out_specs=[w, w, c1, c1],
                 out_shape=[jax.ShapeDtypeStruct((R, C), F32)] * 2 + [jax.ShapeDtypeStruct((R, 1), F32)] * 2,
                 )(gr, gi, br, bi, dbbr, dbbi)


def _s5_scan_fwd(bu, a, *, name):
    _, S, N = bu.shape
    tc = 512
    nt = N // tc

    def body(a_ref, b_ref, h_ref):
        ar, ai = a_ref[0], a_ref[1]

        def step(t, carry):
            hr, hi = carry
            nr = ar * hr - ai * hi + b_ref[0, pl.ds(t, 1), :]
            ni = ar * hi + ai * hr + b_ref[1, pl.ds(t, 1), :]
            h_ref[0, pl.ds(t, 1), :] = nr
            h_ref[1, pl.ds(t, 1), :] = ni
            return nr, ni

        z = jnp.zeros((1, tc), F32)
        lax.fori_loop(0, S, step, (z, z), unroll=8)

    vec = pl.BlockSpec((2, 1, tc), lambda j: (0, 0, j))
    mat = pl.BlockSpec((2, S, tc), lambda j: (0, 0, j))
    return _call(
        body, name=name, grid=(nt,), in_specs=[vec, mat], out_specs=mat,
        out_shape=jax.ShapeDtypeStruct((2, S, N), F32),
        compiler_params=_cparams(("parallel",)),
    )(a, bu)


def _s5_scan_bwd(g, h, a, *, name):
    _, S, N = g.shape
    tc = 256
    nt = N // tc

    def body(a_ref, g_ref, h_ref, l_ref, da_ref):
        ar, ai = a_ref[0], a_ref[1]

        def step(i, carry):
            t = S - 1 - i
            lr, li, dar, dai = carry
            nr = g_ref[0, pl.ds(t, 1), :] + ar * lr + ai * li
            ni = g_ref[1, pl.ds(t, 1), :] + ar * li - ai * lr
            l_ref[0, pl.ds(t, 1), :] = nr
            l_ref[1, pl.ds(t, 1), :] = ni
            tp = jnp.maximum(t - 1, 0)
            keep = jnp.where(t > 0, 1.0, 0.0).astype(F32)
            hpr = h_ref[0, pl.ds(tp, 1), :] * keep
            hpi = h_ref[1, pl.ds(tp, 1), :] * keep
            return nr, ni, dar + nr * hpr + ni * hpi, dai + ni * hpr - nr * hpi

        z = jnp.zeros((1, tc), F32)
        _, _, dar, dai = lax.fori_loop(0, S, step, (z, z, z, z), unroll=8)
        da_ref[0] = dar
        da_ref[1] = dai

    vec = pl.BlockSpec((2, 1, tc), lambda j: (0, 0, j))
    mat = pl.BlockSpec((2, S, tc), lambda j: (0, 0, j))
    return _call(
        body, name=name, grid=(nt,), in_specs=[vec, mat, mat], out_specs=[mat, vec],
        out_shape=[jax.ShapeDtypeStruct((2, S, N), F32), jax.ShapeDtypeStruct((2, 1, N), F32)],
        compiler_params=_cparams(("parallel",)),
    )(a, g, h)


_GELU_C = math.sqrt(2.0 / math.pi)


def _s5_out_fwd(yc, P, dskip, *, name):
    S, W = yc.shape
    tr = _pick(S, 256)
    ub = 3 * FOX_WIDTH // W

    def body(yc_ref, u_ref, d_ref, y_ref, yg_ref):
        y = yc_ref[...] + d_ref[...] * u_ref[...]
        y_ref[...] = y
        t = jnp.tanh(_GELU_C * (y + 0.044715 * y * y * y))
        yg_ref[...] = (0.5 * y * (1.0 + t)).astype(BF16)

    row = pl.BlockSpec((tr, W), lambda i: (i, 0))
    return _call(
        body, name=name, grid=(S // tr,),
        in_specs=[row, pl.BlockSpec((tr, W), lambda i: (i, ub)), pl.BlockSpec((1, W), lambda i: (0, 0))],
        out_specs=[row, row],
        out_shape=[jax.ShapeDtypeStruct((S, W), F32), jax.ShapeDtypeStruct((S, W), BF16)],
        compiler_params=_cparams(("parallel",)),
    )(yc, P, dskip)


def _s5_out_bwd(dyg, y, P, dskip, *, name):
    S, W = y.shape
    tr = _pick(S, 256)
    ub = 3 * FOX_WIDTH // W

    def body(dyg_ref, y_ref, u_ref, d_ref, dy_ref, du_ref, dd_ref):
        y_ = y_ref[...]
        inner = _GELU_C * (y_ + 0.044715 * y_ * y_ * y_)
        t = jnp.tanh(inner)
        dgelu = 0.5 * (1.0 + t) + 0.5 * y_ * (1.0 - t * t) * _GELU_C * (1.0 + 3.0 * 0.044715 * y_ * y_)
        dy = dyg_ref[...] * dgelu
        dy_ref[...] = dy.astype(BF16)
        du_ref[...] = d_ref[...] * dy
        part = jnp.sum(dy * u_ref[...], axis=0, keepdims=True)

        @pl.when(pl.program_id(0) == 0)
        def _():
            dd_ref[...] = part

        @pl.when(pl.program_id(0) > 0)
        def _():
            dd_ref[...] += part

    row = pl.BlockSpec((tr, W), lambda i: (i, 0))
    vec = pl.BlockSpec((1, W), lambda i: (0, 0))
    return _call(
        body, name=name, grid=(S // tr,),
        in_specs=[row, row, pl.BlockSpec((tr, W), lambda i: (i, ub)), vec],
        out_specs=[row, row, vec],
        out_shape=[jax.ShapeDtypeStruct((S, W), BF16), jax.ShapeDtypeStruct((S, W), F32),
                   jax.ShapeDtypeStruct((1, W), F32)],
        compiler_params=_cparams(("arbitrary",)),
    )(dyg, y, P, dskip)


def _glu_fwd(z, *, name):
    S, W2 = z.shape
    W = W2 // 2
    tr = _pick(S, 256)

    def body(z1_ref, z2_ref, o_ref):
        o_ref[...] = (z1_ref[...] * jax.nn.sigmoid(z2_ref[...])).astype(BF16)

    return _call(
        body, name=name, grid=(S // tr,),
        in_specs=[pl.BlockSpec((tr, W), lambda i: (i, 0)), pl.BlockSpec((tr, W), lambda i: (i, 1))],
        out_specs=pl.BlockSpec((tr, W), lambda i: (i, 0)),
        out_shape=jax.ShapeDtypeStruct((S, W), BF16),
        compiler_params=_cparams(("parallel",)),
    )(z, z)


def _glu_bwd(z, dcat, *, name):
    S, W2 = z.shape
    W = W2 // 2
    tr = _pick(S, 256)

    def body(z1_ref, z2_ref, d_ref, dz1_ref, dz2_ref):
        sg = jax.nn.sigmoid(z2_ref[...])
        d = d_ref[...]
        dz1_ref[...] = (d * sg).astype(BF16)
        dz2_ref[...] = (d * z1_ref[...] * sg * (1.0 - sg)).astype(BF16)

    lo = pl.BlockSpec((tr, W), lambda i: (i, 0))
    hi = pl.BlockSpec((tr, W), lambda i: (i, 1))
    dz1, dz2 = _call(
        body, name=name, grid=(S // tr,), in_specs=[lo, hi, hi], out_specs=[lo, lo],
        out_shape=[jax.ShapeDtypeStruct((S, W), BF16)] * 2,
        compiler_params=_cparams(("parallel",)),
    )(z, z, dcat)
    return jnp.concatenate([dz1, dz2], axis=1)


def _act_fwd(h, cw, cb, *, name):
    _, S, FP = h.shape
    tr = _pick(S, 256)
    hb = tr // 8

    def conv(x_ref, halo_ref, w_ref, b_ref, ext, first):
        ext[pl.ds(0, 8), :] = jnp.where(first, 0.0, halo_ref[...])
        ext[pl.ds(8, tr), :] = x_ref[...]
        return (b_ref[...] + w_ref[pl.ds(2, 1), :] * ext[pl.ds(8, tr), :]
                + w_ref[pl.ds(1, 1), :] * ext[pl.ds(7, tr), :] + w_ref[pl.ds(0, 1), :] * ext[pl.ds(6, tr), :])

    def body(g_ref, gh_ref, v_ref, vh_ref, wg_ref, wv_ref, bg_ref, bv_ref, a_ref, ext):
        first = pl.program_id(1) == 0
        cg = conv(g_ref, gh_ref, wg_ref, bg_ref, ext, first)
        cv = conv(v_ref, vh_ref, wv_ref, bv_ref, ext, first)
        a_ref[...] = (cg * jax.nn.sigmoid(cg) * cv).astype(BF16)

    def main(off):
        return pl.BlockSpec((None, tr, FP), lambda j, i: (j + off, i, 0))

    def halo(off):
        return pl.BlockSpec((None, 8, FP), lambda j, i: (j + off, jnp.maximum(i * hb - 1, 0), 0))

    def wspec(off):
        return pl.BlockSpec((None, 3, FP), lambda j, i: (j + off, 0, 0))

    def bspec(off):
        return pl.BlockSpec((None, 1, FP), lambda j, i: (j + off, 0, 0))

    cb3 = cb.reshape(4, 1, FP)
    return _call(
        body, name=name, grid=(2, S // tr),
        in_specs=[main(0), halo(0), main(2), halo(2), wspec(0), wspec(2), bspec(0), bspec(2)],
        out_specs=pl.BlockSpec((None, tr, FP), lambda j, i: (j, i, 0)),
        out_shape=jax.ShapeDtypeStruct((2, S, FP), BF16),
        scratch_shapes=[pltpu.VMEM((tr + 8, FP), F32)],
        compiler_params=_cparams(("parallel", "arbitrary")),
    )(h, h, h, h, cw, cw, cb3, cb3)


def _act_bwd(h, da, cw, cb, *, name):
    _, S, FP = h.shape
    tr = _pick(S, 128)
    hb = tr // 8
    nr = S // tr

    def fill(ext, x_ref, prev_ref, next_ref, first, last):
        ext[pl.ds(0, 8), :] = jnp.where(first, 0.0, prev_ref[...])
        ext[pl.ds(8, tr), :] = x_ref[...]
        ext[pl.ds(8 + tr, 8), :] = jnp.where(last, 0.0, next_ref[...])

    def convo(ext, w, b, base, n):
        return (b + w[2] * ext[pl.ds(base, n), :] + w[1] * ext[pl.ds(base - 1, n), :]
                + w[0] * ext[pl.ds(base - 2, n), :])

    def body(g_ref, gp_ref, gn_ref, v_ref, vp_ref, vn_ref, da_ref, dan_ref,
             wg_ref, wv_ref, bg_ref, bv_ref,
             dg_ref, dv_ref, dwg_ref, dwv_ref, dbg_ref, dbv_ref, eg, ev, ed, dcg, dcv):
        i = pl.program_id(1)
        first = i == 0
        last = i == nr - 1
        fill(eg, g_ref, gp_ref, gn_ref, first, last)
        fill(ev, v_ref, vp_ref, vn_ref, first, last)
        ed[pl.ds(0, tr), :] = da_ref[...]
        ed[pl.ds(tr, 8), :] = jnp.where(last, 0.0, dan_ref[...])
        wg = [wg_ref[pl.ds(k, 1), :] for k in range(3)]
        wv = [wv_ref[pl.ds(k, 1), :] for k in range(3)]
        n = tr + 8
        cg = convo(eg, wg, bg_ref[...], 8, n)
        cv = convo(ev, wv, bv_ref[...], 8, n)
        sg = jax.nn.sigmoid(cg)
        d = ed[...]
        dcg[...] = d * cv * sg * (1.0 + cg * (1.0 - sg))
        dcv[...] = d * cg * sg
        for (dc, w, e, dh_ref, dw_ref, db_ref) in ((dcg, wg, eg, dg_ref, dwg_ref, dbg_ref),
                                                  (dcv, wv, ev, dv_ref, dwv_ref, dbv_ref)):
            d0 = dc[pl.ds(0, tr), :]
            dh_ref[...] = (w[2] * d0 + w[1] * dc[pl.ds(1, tr), :] + w[0] * dc[pl.ds(2, tr), :]).astype(BF16)
            pw = [jnp.sum(d0 * e[pl.ds(6 + k, tr), :], axis=0, keepdims=True) for k in range(3)]
            pb = jnp.sum(d0, axis=0, keepdims=True)

            @pl.when(first)
            def _():
                for k in range(3):
                    dw_ref[pl.ds(k, 1), :] = pw[k]
                db_ref[...] = pb

            @pl.when(jnp.logical_not(first))
            def _():
                for k in range(3):
                    dw_ref[pl.ds(k, 1), :] += pw[k]
                db_ref[...] += pb

    def main(off):
        return pl.BlockSpec((None, tr, FP), lambda j, i: (j + off, i, 0))

    def prev(off):
        return pl.BlockSpec((None, 8, FP), lambda j, i: (j + off, jnp.maximum(i * hb - 1, 0), 0))

    def nxt(off):
        return pl.BlockSpec((None, 8, FP), lambda j, i: (j + off, jnp.minimum((i + 1) * hb, S // 8 - 1), 0))

    def wspec(off):
        return pl.BlockSpec((None, 3, FP), lambda j, i: (j + off, 0, 0))

    def bspec(off):
        return pl.BlockSpec((None, 1, FP), lambda j, i: (j + off, 0, 0))

    cb3 = cb.reshape(4, 1, FP)
    dg, dv, dwg, dwv, dbg, dbv = _call(
        body, name=name, grid=(2, nr),
        in_specs=[main(0), prev(0), nxt(0), main(2), prev(2), nxt(2), main(0), nxt(0),
                  wspec(0), wspec(2), bspec(0), bspec(2)],
        out_specs=[main(0), main(0), wspec(0), wspec(0), bspec(0), bspec(0)],
        out_shape=[jax.ShapeDtypeStruct((2, S, FP), BF16)] * 2
        + [jax.ShapeDtypeStruct((2, 3, FP), F32)] * 2 + [jax.ShapeDtypeStruct((2, 1, FP), F32)] * 2,
        scratch_shapes=[pltpu.VMEM((tr + 16, FP), F32), pltpu.VMEM((tr + 16, FP), F32),
                        pltpu.VMEM((tr + 8, FP), F32), pltpu.VMEM((tr + 8, FP), F32),
                        pltpu.VMEM((tr + 8, FP), F32)],
        compiler_params=_cparams(("parallel", "arbitrary")),
    )(h, h, h, h, h, h, da, da, cw, cw, cb3, cb3)
    return (jnp.concatenate([dg, dv], axis=0), jnp.concatenate([dwg, dwv], axis=0),
            jnp.concatenate([dbg, dbv], axis=0))


def _rope_tables(posf, *, name):
    S = posf.shape[0]
    half = ROPE_DIM // 2
    d = np.arange(LANE) % SWA_HEAD_DIM
    invf = np.where(d < ROPE_DIM, ROPE_THETA ** (-(d % half).astype(np.float64) / half), 0.0).astype(np.float32)
    m_rot = (d < ROPE_DIM).astype(np.float32)
    m_a = (d < half).astype(np.float32)
    m_b = ((d >= half) & (d < ROPE_DIM)).astype(np.float32)
    consts = jnp.asarray(np.stack([invf, m_rot, m_a, m_b] + [np.zeros(LANE, np.float32)] * 4))

    def body(p_ref, k_ref, c_ref, sa_ref, sb_ref):
        k = k_ref[...]
        ang = p_ref[...] * k[0:1]
        co, si = jnp.cos(ang), jnp.sin(ang)
        c_ref[...] = k[1:2] * co + (1.0 - k[1:2])
        sa_ref[...] = -k[2:3] * si
        sb_ref[...] = k[3:4] * si

    full = pl.BlockSpec((S, LANE), lambda: (0, 0))
    return _call(
        body, name=name,
        in_specs=[pl.BlockSpec((S, 1), lambda: (0, 0)), pl.BlockSpec((8, LANE), lambda: (0, 0))],
        out_specs=[full] * 3, out_shape=[jax.ShapeDtypeStruct((S, LANE), F32)] * 3,
    )(posf, consts)


def _rope_apply(x, tabs, *, col0, width, inverse, name, out_dtype):
    S = x.shape[0]
    tr = _pick(S, 256)
    rep = width // LANE
    cb = col0 // width

    def body(x_ref, c_ref, sa_ref, sb_ref, o_ref):
        xv = x_ref[...].astype(F32)
        c = jnp.tile(c_ref[...], (1, rep))
        sa = jnp.tile(sa_ref[...], (1, rep))
        sb = jnp.tile(sb_ref[...], (1, rep))
        if not inverse:
            out = xv * c + pltpu.roll(xv, width - 8, 1) * sa + pltpu.roll(xv, 8, 1) * sb
        else:
            out = xv * c + pltpu.roll(xv * sa, 8, 1) + pltpu.roll(xv * sb, width - 8, 1)
        o_ref[...] = out.astype(out_dtype)

    tab = pl.BlockSpec((tr, LANE), lambda i: (i, 0))
    return _call(
        body, name=name, grid=(S // tr,),
        in_specs=[pl.BlockSpec((tr, width), lambda i: (i, cb)), tab, tab, tab],
        out_specs=pl.BlockSpec((tr, width), lambda i: (i, 0)),
        out_shape=jax.ShapeDtypeStruct((S, width), out_dtype),
        compiler_params=_cparams(("parallel",)),
    )(x, *tabs)


def _swa_mask(n):
    rows = SWA_GROUPS * SWA_WINDOW
    qi = lax.broadcasted_iota(jnp.int32, (rows, 2 * SWA_WINDOW), 0) & (SWA_WINDOW - 1)
    kj = lax.broadcasted_iota(jnp.int32, (rows, 2 * SWA_WINDOW), 1)
    rel = SWA_WINDOW + qi - kj
    return (rel >= 0) & (rel < SWA_WINDOW) & ((n > 0) | (kj >= SWA_WINDOW))


def _swa_fwd(qT, kT, vT, sink_rows, *, name):
    S = qT.shape[1]
    W, G, Dh = SWA_WINDOW, SWA_GROUPS, SWA_HEAD_DIM
    nb = S // W
    scale = 1.0 / math.sqrt(Dh)

    def body(q_ref, kp_ref, kc_ref, vp_ref, vc_ref, s_ref, o_ref, l_ref):
        n = pl.program_id(1)
        q = q_ref[...].reshape(G * W, Dh)
        kk = jnp.concatenate([kp_ref[...], kc_ref[...]], axis=0)
        vv = jnp.concatenate([vp_ref[...], vc_ref[...]], axis=0)
        s = lax.dot_general(q, kk, (((1,), (1,)), ((), ())), preferred_element_type=F32) * scale
        s = jnp.where(_swa_mask(n), s, -1e30)
        sink = s_ref[...]
        m = jnp.maximum(jnp.max(s, axis=-1, keepdims=True), sink)
        e = jnp.exp(s - m)
        den = jnp.sum(e, axis=-1, keepdims=True) + jnp.exp(sink - m)
        p = e / den
        o_ref[...] = jnp.dot(p.astype(BF16), vv, preferred_element_type=F32).reshape(G, W, Dh)
        l_ref[...] = (m + jnp.log(den)).reshape(G, W, 1)

    qs = pl.BlockSpec((G, W, Dh), lambda g, n: (g, n, 0))
    prev = pl.BlockSpec((None, W, Dh), lambda g, n: (g, jnp.maximum(n - 1, 0), 0))
    cur = pl.BlockSpec((None, W, Dh), lambda g, n: (g, n, 0))
    return _call(
        body, name=name, grid=(SWA_KV_HEADS, nb),
        in_specs=[qs, prev, cur, prev, cur, pl.BlockSpec((None, G * W, 1), lambda g, n: (g, 0, 0))],
        out_specs=[qs, pl.BlockSpec((G, W, 1), lambda g, n: (g, n, 0))],
        out_shape=[jax.ShapeDtypeStruct((SWA_HEADS, S, Dh), F32), jax.ShapeDtypeStruct((SWA_HEADS, S, 1), F32)],
        compiler_params=_cparams(("parallel", "parallel")),
    )(qT, kT, kT, vT, vT, sink_rows)


def _swa_bwd(qT, kT, vT, sink_rows, oT, L, doT, *, name):
    S = qT.shape[1]
    W, G, Dh = SWA_WINDOW, SWA_GROUPS, SWA_HEAD_DIM
    nb = S // W
    scale = 1.0 / math.sqrt(Dh)

    def body(q_ref, kp_ref, kc_ref, vp_ref, vc_ref, s_ref, o_ref, l_ref, do_ref,
             dq_ref, dk_ref, dv_ref, ds_ref):
        n = pl.program_id(1)
        q = q_ref[...].reshape(G * W, Dh)
        kk = jnp.concatenate([kp_ref[...], kc_ref[...]], axis=0)
        vv = jnp.concatenate([vp_ref[...], vc_ref[...]], axis=0)
        s = lax.dot_general(q, kk, (((1,), (1,)), ((), ())), preferred_element_type=F32) * scale
        lrow = l_ref[...].reshape(G * W, 1)
        p = jnp.where(_swa_mask(n), jnp.exp(s - lrow), 0.0)
        do = do_ref[...].reshape(G * W, Dh)
        do_bf = do.astype(BF16)
        dp = lax.dot_general(do_bf, vv, (((1,), (1,)), ((), ())), preferred_element_type=F32)
        delta = jnp.sum(do * o_ref[...].reshape(G * W, Dh), axis=-1, keepdims=True)
        dsc = p * (dp - delta)
        ds_bf = dsc.astype(BF16)
        dq_ref[...] = (jnp.dot(ds_bf, kk, preferred_element_type=F32) * scale).astype(BF16).reshape(G, W, Dh)
        dkk = lax.dot_general(ds_bf, q, (((0,), (0,)), ((), ())), preferred_element_type=F32) * scale
        dvv = lax.dot_general(p.astype(BF16), do_bf, (((0,), (0,)), ((), ())), preferred_element_type=F32)
        dsk = -jnp.exp(s_ref[...] - lrow) * delta
        dsk = jnp.broadcast_to(jnp.sum(dsk.reshape(G, W, 1), axis=1), (G, LANE))

        @pl.when(n == 0)
        def _():
            dk_ref[...] = jnp.zeros_like(dk_ref)
            dv_ref[...] = jnp.zeros_like(dv_ref)
            ds_ref[...] = jnp.zeros_like(ds_ref)

        rows = pl.ds(pl.multiple_of(n * W, W), 2 * W)
        dk_ref[rows, :] += dkk
        dv_ref[rows, :] += dvv
        ds_ref[...] += dsk

    qs = pl.BlockSpec((G, W, Dh), lambda g, n: (g, n, 0))
    prev = pl.BlockSpec((None, W, Dh), lambda g, n: (g, jnp.maximum(n - 1, 0), 0))
    cur = pl.BlockSpec((None, W, Dh), lambda g, n: (g, n, 0))
    lsp = pl.BlockSpec((G, W, 1), lambda g, n: (g, n, 0))
    kvo = pl.BlockSpec((None, S + W, Dh), lambda g, n: (g, 0, 0))
    return _call(
        body, name=name, grid=(SWA_KV_HEADS, nb),
        in_specs=[qs, prev, cur, prev, cur, pl.BlockSpec((None, G * W, 1), lambda g, n: (g, 0, 0)), qs, lsp, qs],
        out_specs=[qs, kvo, kvo, pl.BlockSpec((None, G, LANE), lambda g, n: (g, 0, 0))],
        out_shape=[jax.ShapeDtypeStruct((SWA_HEADS, S, Dh), BF16),
                   jax.ShapeDtypeStruct((SWA_KV_HEADS, S + W, Dh), F32),
                   jax.ShapeDtypeStruct((SWA_KV_HEADS, S + W, Dh), F32),
                   jax.ShapeDtypeStruct((SWA_KV_HEADS, G, LANE), F32)],
        compiler_params=_cparams(("parallel", "arbitrary")),
    )(qT, kT, kT, vT, vT, sink_rows, oT, L, doT)


def _adamw(w, g, m, v, *, name, tr=128):
    L, R, C = w.shape
    split = isinstance(g, (list, tuple))
    H = R // 2 if split else R
    tr, tc = _tile2d(H, C, tr)
    nh = H // tr
    c1 = 1.0 / (1.0 - ADAM_B1 ** ADAM_STEP)
    c2 = 1.0 / (1.0 - ADAM_B2 ** ADAM_STEP)
    ng = 2 * L if split else 1

    def body(*refs):
        w_ref, g_refs, (m_ref, v_ref, go_ref, d_ref, mo_ref, vo_ref) = refs[0], refs[1:1 + ng], refs[1 + ng:]
        if split:
            mine = pl.program_id(1) == lax.axis_index("c")
            g_ = jnp.where(mine, g_refs[0][...], g_refs[1][...])
            for l in range(1, L):
                g_ = jnp.where(pl.program_id(0) == l,
                               jnp.where(mine, g_refs[2 * l][...], g_refs[2 * l + 1][...]), g_)
        else:
            g_ = g_refs[0][...]
        mn = ADAM_B1 * m_ref[...] + (1.0 - ADAM_B1) * g_
        vn = ADAM_B2 * v_ref[...] + (1.0 - ADAM_B2) * (g_ * g_)
        go_ref[...] = g_
        mo_ref[...] = mn
        vo_ref[...] = vn
        d_ref[...] = -ADAM_LR * ((mn * c1) / (jnp.sqrt(vn * c2) + ADAM_EPS) + ADAM_WD * w_ref[...])

    row = pl.BlockSpec((None, tr, tc), lambda l, hf, i, j: (l, hf * nh + i, j))
    half = pl.BlockSpec((tr, tc), lambda l, hf, i, j: (i, j))
    gs = [h for pair in g for h in pair] if split else [g]
    return _call(
        body, name=name, grid=(L, R // H, nh, C // tc),
        in_specs=[row] + [half if split else row] * ng + [row, row],
        out_specs=[row] * 4, out_shape=[jax.ShapeDtypeStruct((L, R, C), F32)] * 4,
        compiler_params=_cparams(("parallel",) * 4),
    )(w, *gs, m, v)


def _sum2_halves(g4, s4, *, name):
    n, R, C = g4.shape
    H = R // 2
    tr, tc = _tile2d(H, C)
    nh = H // tr
    core = lax.axis_index("c").astype(jnp.int32).reshape(1)

    def body(c_ref, g_ref, s_ref, o_ref):
        o_ref[...] = (g_ref[...].astype(F32) + s_ref[...].astype(F32)).astype(BF16)

    blk = pl.BlockSpec((None, tr, tc), lambda k, i, j, c: (k, i, j))
    return _call(
        body, name=name,
        grid_spec=pltpu.PrefetchScalarGridSpec(
            num_scalar_prefetch=1, grid=(n, nh, C // tc),
            in_specs=[pl.BlockSpec((None, tr, tc), lambda k, i, j, c: (k, c[0] * nh + i, j)), blk],
            out_specs=blk),
        out_shape=jax.ShapeDtypeStruct((n, H, C), BF16),
        compiler_params=_cparams(("parallel", "parallel", "parallel")),
    )(core, g4, s4)


def _rowsum(parts, *, name, out_dtype=F32):
    n, R, C = parts.shape
    tr, tc = _tile2d(R, C)

    def body(p_ref, o_ref):
        acc = p_ref[0].astype(F32)
        for i in range(1, n):
            acc = acc + p_ref[i].astype(F32)
        o_ref[...] = acc.astype(out_dtype)

    return _call(
        body, name=name, grid=(R // tr, C // tc),
        in_specs=[pl.BlockSpec((n, tr, tc), lambda i, j: (0, i, j))],
        out_specs=pl.BlockSpec((tr, tc), lambda i, j: (i, j)),
        out_shape=jax.ShapeDtypeStruct((R, C), out_dtype),
        compiler_params=_cparams(("parallel", "parallel")),
    )(parts)


def _where_am_i():
    x, y, c = lax.axis_index("x"), lax.axis_index("y"), lax.axis_index("c")
    chips = [(1 - x, y), (x, 1 - y), (1 - x, 1 - y)]
    return x, y, c, chips


def _all_gather_shards(shards, *, name):
    n = len(shards)

    def body(*refs):
        ins, outs = refs[:n], refs[n:2 * n]
        send, recv = refs[2 * n:]
        x, y, c, chips = _where_am_i()
        me = 2 * x + y
        sibling = (x, y, 1 - c)

        def half(i, which):
            hr = shards[i].shape[0] // 2
            return pl.ds(pl.multiple_of(which * hr, 16), hr)

        def cp(i, k, src, dst, to):
            return pltpu.make_async_remote_copy(src_ref=src, dst_ref=dst, send_sem=send.at[i, k],
                                                recv_sem=recv.at[i, k], device_id=to, device_id_type=MESH)

        first = []
        for i in range(n):
            for k, (px, py) in enumerate(chips):
                d = cp(i, k, ins[i].at[half(i, c)], outs[i].at[me, half(i, c)], (px, py, c))
                d.start()
                first.append(d)
        passed = []
        for i in range(n):
            for k, (px, py) in enumerate(chips):
                blk = outs[i].at[2 * px + py, half(i, c)]
                cp(i, k, blk, blk, (px, py, c)).wait_recv()
                d = cp(i, 3 + k, blk, blk, sibling)
                d.start()
                passed.append(d)
        for i in range(n):
            for k, (px, py) in enumerate(chips):
                blk = outs[i].at[2 * px + py, half(i, 1 - c)]
                cp(i, 3 + k, blk, blk, sibling).wait_recv()
        for d in first + passed:
            d.wait_send()

    got = _call(
        body, name=name, in_specs=[ANY] * n, out_specs=[ANY] * n,
        out_shape=[jax.ShapeDtypeStruct((N_CHIPS,) + s.shape, s.dtype) for s in shards],
        scratch_shapes=[pltpu.SemaphoreType.DMA((n, 6)), pltpu.SemaphoreType.DMA((n, 6))],
    )(*shards)
    me = 2 * lax.axis_index("x") + lax.axis_index("y")
    return [lax.dynamic_update_slice_in_dim(g, s[None], me, axis=0) for g, s in zip(got, shards)]


def _sibling_send_halves(grads, *, name):
    n = len(grads)

    def body(*refs):
        ins, outs = refs[:n], refs[n:2 * n]
        send, recv = refs[2 * n:]
        x, y, c, _ = _where_am_i()
        sibling = (x, y, 1 - c)
        cps = []
        for i in range(n):
            hr = grads[i].shape[1] // 2
            src = ins[i].at[:, pl.ds(pl.multiple_of((1 - c) * hr, 16), hr)]
            d = pltpu.make_async_remote_copy(src_ref=src, dst_ref=outs[i], send_sem=send.at[i],
                                             recv_sem=recv.at[i], device_id=sibling, device_id_type=MESH)
            d.start()
            cps.append(d)
        for d in cps:
            d.wait()

    return _call(
        body, name=name, in_specs=[ANY] * n, out_specs=[ANY] * n,
        out_shape=[jax.ShapeDtypeStruct((N_CHIPS, g.shape[1] // 2, g.shape[2]), g.dtype) for g in grads],
        scratch_shapes=[pltpu.SemaphoreType.DMA((n,)), pltpu.SemaphoreType.DMA((n,))],
    )(*grads)


def _scatter_to_chips(parts, *, name):
    n = len(parts)

    def body(*refs):
        ins, outs = refs[:n], refs[n:2 * n]
        send, recv = refs[2 * n:]
        x, y, c, chips = _where_am_i()
        me = 2 * x + y
        cps = []
        for i in range(n):
            for k, (px, py) in enumerate(chips):
                d = pltpu.make_async_remote_copy(
                    src_ref=ins[i].at[2 * px + py], dst_ref=outs[i].at[me], send_sem=send.at[i, k],
                    recv_sem=recv.at[i, k], device_id=(px, py, c), device_id_type=MESH)
                d.start()
                cps.append((d, i, k, px, py))
        for d, i, k, px, py in cps:
            blk = outs[i].at[2 * px + py]
            pltpu.make_async_remote_copy(src_ref=blk, dst_ref=blk, send_sem=send.at[i, k], recv_sem=recv.at[i, k],
                                         device_id=(px, py, c), device_id_type=MESH).wait_recv()
        for d, *_ in cps:
            d.wait_send()

    got = _call(
        body, name=name, in_specs=[ANY] * n, out_specs=[ANY] * n,
        out_shape=[jax.ShapeDtypeStruct(p.shape, p.dtype) for p in parts],
        scratch_shapes=[pltpu.SemaphoreType.DMA((n, 3)), pltpu.SemaphoreType.DMA((n, 3))],
    )(*parts)
    me = 2 * lax.axis_index("x") + lax.axis_index("y")
    return [lax.dynamic_update_slice_in_dim(g, lax.dynamic_slice_in_dim(p, me, 1, axis=0), me, axis=0)
            for g, p in zip(got, parts)]


def _sibling_join_halves(halves, *, name):
    n = len(halves)

    def body(*refs):
        ins, outs = refs[:n], refs[n:2 * n]
        send, recv = refs[2 * n:]
        x, y, c, _ = _where_am_i()
        sibling = (x, y, 1 - c)
        cps = []
        for i in range(n):
            d = pltpu.make_async_remote_copy(src_ref=ins[i], dst_ref=outs[i], send_sem=send.at[i],
                                             recv_sem=recv.at[i], device_id=sibling, device_id_type=MESH)
            d.start()
            cps.append(d)
        for d in cps:
            d.wait()

    return _call(
        body, name=name, in_specs=[ANY] * n, out_specs=[ANY] * n,
        out_shape=[jax.ShapeDtypeStruct(h.shape, h.dtype) for h in halves],
        scratch_shapes=[pltpu.SemaphoreType.DMA((n,)), pltpu.SemaphoreType.DMA((n,))],
    )(*halves)


def _all_reduce_small(v, *, name):
    R, C = v.shape

    def body(v_ref, o_ref, sib, slots, send, recv):
        x, y, c, chips = _where_am_i()
        me = 2 * x + y
        sibling = (x, y, 1 - c)
        d = pltpu.make_async_remote_copy(src_ref=v_ref, dst_ref=sib, send_sem=send.at[0], recv_sem=recv.at[0],
                                         device_id=sibling, device_id_type=MESH)
        d.start()
        d.wait()
        slots[me] = v_ref[...] + sib[...]
        cps = []
        for k, (px, py) in enumerate(chips):
            d = pltpu.make_async_remote_copy(src_ref=slots.at[me], dst_ref=slots.at[me], send_sem=send.at[1 + k],
                                             recv_sem=recv.at[1 + k], device_id=(px, py, c), device_id_type=MESH)
            d.start()
            cps.append(d)
        for k, (px, py) in enumerate(chips):
            blk = slots.at[2 * px + py]
            pltpu.make_async_remote_copy(src_ref=blk, dst_ref=blk, send_sem=send.at[1 + k], recv_sem=recv.at[1 + k],
                                         device_id=(px, py, c), device_id_type=MESH).wait_recv()
        for d in cps:
            d.wait_send()
        o_ref[...] = (slots[0] + slots[1]) + (slots[2] + slots[3])

    vm = pl.BlockSpec(memory_space=pltpu.VMEM)
    return _call(
        body, name=name, in_specs=[vm], out_specs=vm,
        out_shape=jax.ShapeDtypeStruct((R, C), F32),
        scratch_shapes=[pltpu.VMEM((R, C), F32), pltpu.VMEM((N_CHIPS, R, C), F32),
                        pltpu.SemaphoreType.DMA((4,)), pltpu.SemaphoreType.DMA((4,))],
        compiler_params=pltpu.CompilerParams(vmem_limit_bytes=VMEM_LIMIT),
    )(v)


def _cols_from_shards(g):
    return jnp.transpose(g, (1, 0, 2)).reshape(g.shape[1], -1)


def _shards_from_cols(w):
    R, C4 = w.shape
    return jnp.transpose(w.reshape(R, N_CHIPS, C4 // N_CHIPS), (1, 0, 2))


def _block_diag(t):
    G, a, b = t.shape
    eye = jnp.eye(G, dtype=t.dtype)
    return (t[:, :, None, :] * eye[:, None, :, None]).reshape(G * a, G * b)


def _diag_blocks(xm, G):
    a, b = xm.shape[0] // G, xm.shape[1] // G
    idx = jnp.arange(G)
    return xm.reshape(G, a, G, b)[idx, :, idx, :]


def _pack(arrs):
    flat = []
    for a in arrs:
        f = a.reshape(-1).astype(F32)
        flat.append(jnp.pad(f, (0, _rup(f.shape[0], LANE) - f.shape[0])))
    v = jnp.concatenate(flat)
    rows = _rup(v.shape[0] // LANE, 8)
    v = jnp.pad(v, (0, rows * LANE - v.shape[0]))
    return v.reshape(rows, LANE)


def _unpack(v, shapes):
    flat = v.reshape(-1)
    out, off = [], 0
    for s in shapes:
        n = int(np.prod(s))
        out.append(flat[off:off + n].reshape(s))
        off += _rup(n, LANE)
    return out


def _ffn_fwd(x, Wup, Wdn, cw, cb, tag):
    h = _mm(x, Wup, 'nn', bmode='bo', tm=512, tn=4096, name=f"ffn_up_{tag}")
    a = _act_fwd(h, cw, cb, name=f"ffn_act_{tag}")
    f = _mm(a, Wdn, 'nn', bmode='abr', tm=512, tn=1024, tk=4096, name=f"ffn_down_{tag}")
    return f, h, a


def _ffn_bwd(df, x, h, a, Wup, Wdn, cw, cb, tag):
    da = _mm(df, Wdn, 'nt', bmode='bo', tm=512, tn=4096, name=f"ffn_da_{tag}")
    dWdn = _mm(a, df, 'tn', bmode='ao', tm=4096, tn=512, name=f"ffn_dwdn_{tag}", out_dtype=BF16)
    da4 = da
    dh, dcw, dcb = _act_bwd(h, da4, cw, cb, name=f"ffn_actb_{tag}")
    dx = _mm(dh, Wup, 'nt', bmode='abr', tm=512, tn=1024, tk=4096, name=f"ffn_dx_{tag}")
    dWup = _mm(x, dh, 'tn', bmode='bo', tm=512, tn=4096, name=f"ffn_dwup_{tag}", out_dtype=BF16)
    return dx, dWup, dWdn, dcw, dcb


def kernel(x, positions, ev_w_in, ev_b_f, ev_lambda_re, ev_lambda_im, ev_log_step, ev_ssm_b_re, ev_ssm_b_im, ev_ssm_c_re, ev_ssm_c_im, ev_ssm_d, ev_w_glu, ev_w_out, od_w_in, od_sinks, od_w_out, ln_mix_g, ln_mix_b, ffn_w_up, ffn_conv_w, ffn_conv_b, ffn_w_down, ln_ffn_g, ln_ffn_b, loss_target, m_ev_w_in, m_ev_b_f, m_ev_lambda_re, m_ev_lambda_im, m_ev_log_step, m_ev_ssm_b_re, m_ev_ssm_b_im, m_ev_ssm_c_re, m_ev_ssm_c_im, m_ev_ssm_d, m_ev_w_glu, m_ev_w_out, m_od_w_in, m_od_sinks, m_od_w_out, m_ln_mix_g, m_ln_mix_b, m_ffn_w_up, m_ffn_conv_w, m_ffn_conv_b, m_ffn_w_down, m_ln_ffn_g, m_ln_ffn_b, v_ev_w_in, v_ev_b_f, v_ev_lambda_re, v_ev_lambda_im, v_ev_log_step, v_ev_ssm_b_re, v_ev_ssm_b_im, v_ev_ssm_c_re, v_ev_ssm_c_im, v_ev_ssm_d, v_ev_w_glu, v_ev_w_out, v_od_w_in, v_od_sinks, v_od_w_out, v_ln_mix_g, v_ln_mix_b, v_ffn_w_up, v_ffn_conv_w, v_ffn_conv_b, v_ffn_w_down, v_ln_ffn_g, v_ln_ffn_b):
    W = dict(ev_w_in=ev_w_in, ev_b_f=ev_b_f, ev_lambda_re=ev_lambda_re, ev_lambda_im=ev_lambda_im, ev_log_step=ev_log_step, ev_ssm_b_re=ev_ssm_b_re, ev_ssm_b_im=ev_ssm_b_im, ev_ssm_c_re=ev_ssm_c_re, ev_ssm_c_im=ev_ssm_c_im, ev_ssm_d=ev_ssm_d, ev_w_glu=ev_w_glu, ev_w_out=ev_w_out, od_w_in=od_w_in, od_sinks=od_sinks, od_w_out=od_w_out, ln_mix_g=ln_mix_g, ln_mix_b=ln_mix_b, ffn_w_up=ffn_w_up, ffn_conv_w=ffn_conv_w, ffn_conv_b=ffn_conv_b, ffn_w_down=ffn_w_down, ln_ffn_g=ln_ffn_g, ln_ffn_b=ln_ffn_b)
    Mo = dict(ev_w_in=m_ev_w_in, ev_b_f=m_ev_b_f, ev_lambda_re=m_ev_lambda_re, ev_lambda_im=m_ev_lambda_im, ev_log_step=m_ev_log_step, ev_ssm_b_re=m_ev_ssm_b_re, ev_ssm_b_im=m_ev_ssm_b_im, ev_ssm_c_re=m_ev_ssm_c_re, ev_ssm_c_im=m_ev_ssm_c_im, ev_ssm_d=m_ev_ssm_d, ev_w_glu=m_ev_w_glu, ev_w_out=m_ev_w_out, od_w_in=m_od_w_in, od_sinks=m_od_sinks, od_w_out=m_od_w_out, ln_mix_g=m_ln_mix_g, ln_mix_b=m_ln_mix_b, ffn_w_up=m_ffn_w_up, ffn_conv_w=m_ffn_conv_w, ffn_conv_b=m_ffn_conv_b, ffn_w_down=m_ffn_w_down, ln_ffn_g=m_ln_ffn_g, ln_ffn_b=m_ln_ffn_b)
    Vo = dict(ev_w_in=v_ev_w_in, ev_b_f=v_ev_b_f, ev_lambda_re=v_ev_lambda_re, ev_lambda_im=v_ev_lambda_im, ev_log_step=v_ev_log_step, ev_ssm_b_re=v_ev_ssm_b_re, ev_ssm_b_im=v_ev_ssm_b_im, ev_ssm_c_re=v_ev_ssm_c_re, ev_ssm_c_im=v_ev_ssm_c_im, ev_ssm_d=v_ev_ssm_d, ev_w_glu=v_ev_w_glu, ev_w_out=v_ev_w_out, od_w_in=v_od_w_in, od_sinks=v_od_sinks, od_w_out=v_od_w_out, ln_mix_g=v_ln_mix_g, ln_mix_b=v_ln_mix_b, ffn_w_up=v_ffn_w_up, ffn_conv_w=v_ffn_conv_w, ffn_conv_b=v_ffn_conv_b, ffn_w_down=v_ffn_w_down, ln_ffn_g=v_ln_ffn_g, ln_ffn_b=v_ln_ffn_b)
    names = list(W.keys())
    big = ['ev_w_in', 'ev_w_glu', 'ev_w_out', 'od_w_in', 'od_w_out', 'ffn_w_up', 'ffn_w_down']

    S, D = x.shape[1], x.shape[2]
    x0 = x.reshape(S, D)
    tgt = loss_target.reshape(S, D)
    G, Pn, Cg = SSM_GROUPS, SSM_STATE, SSM_GROUP
    Fs = ffn_w_up.shape[2]
    FP = Fs
    Rd = ffn_w_down.shape[1]
    EIN = N_CHIPS * ev_w_in.shape[2]

    def as2d(a):
        return a.reshape(-1, a.shape[-1])

    cwl = ffn_conv_w.reshape(-1)
    cw_rows = _rup(_rup(cwl.shape[0], LANE) // LANE, 32)
    cw_pad = jnp.pad(cwl, (0, cw_rows * LANE - cwl.shape[0])).reshape(cw_rows, LANE)
    big_e = [(n, l) for n in big for l in range(W[n].shape[0])]
    gathered = _all_gather_shards([W[n][l].astype(BF16) for n, l in big_e] + [cw_pad], name="ag_weights")
    gw = dict(zip(big_e, gathered[:-1]))
    gw.update({n: gw[(n, 0)] for n in big if W[n].shape[0] == 1})
    w_in_full = _cols_from_shards(gw['ev_w_in'])
    qkv_w = 3 * FOX_WIDTH
    Wmain = jnp.concatenate([w_in_full[:, :qkv_w], w_in_full[:, qkv_w + FOX_HEADS:]], axis=1)
    Wf = jnp.pad(w_in_full[:, qkv_w:qkv_w + FOX_HEADS], ((0, 0), (0, LANE - FOX_HEADS)))
    Wglu = _cols_from_shards(gw['ev_w_glu'])
    Wout_ev = gw['ev_w_out'].reshape(D, D)
    Wodin = _cols_from_shards(gw['od_w_in'])
    Wodout = gw['od_w_out'].reshape(D, D)
    Wup, Wdn, cws, cbs = [], [], [], []
    cw_all = gathered[-1].reshape(N_CHIPS, -1)[:, :cwl.shape[0]].reshape(N_CHIPS, DEPTH, 3, Fs)
    for l in range(DEPTH):
        Wup.append(gw[('ffn_w_up', l)])
        Wdn.append(gw[('ffn_w_down', l)].reshape(2, Fs, D))
        cws.append(cw_all[:, l])
        cbs.append(ffn_conv_b[l].reshape(N_CHIPS, Fs))

    lam_r, lam_i = ev_lambda_re[0], ev_lambda_im[0]
    lstep = ev_log_step[0].reshape(G, 1)
    a_re, a_im, g_re, g_im = _s5_disc_fwd(lam_r, lam_i, lstep, name="s5_disc")
    b_re2, b_im2 = ev_ssm_b_re[0].reshape(G * Pn, Cg), ev_ssm_b_im[0].reshape(G * Pn, Cg)
    g_re1, g_im1 = g_re.reshape(G * Pn, 1), g_im.reshape(G * Pn, 1)
    bb_re, bb_im = _s5_bb_fwd(g_re1, g_im1, b_re2, b_im2, name="s5_bb")
    BB = jnp.stack([_block_diag(jnp.transpose(bb_re.reshape(G, Pn, Cg), (0, 2, 1))),
                    _block_diag(jnp.transpose(bb_im.reshape(G, Pn, Cg), (0, 2, 1)))]).astype(BF16)
    CC = jnp.stack([_block_diag(jnp.transpose(ev_ssm_c_re[0], (0, 2, 1))),
                    _block_diag(jnp.transpose(-ev_ssm_c_im[0], (0, 2, 1)))]).astype(BF16)
    a_cat = jnp.stack([a_re.reshape(1, G * Pn), a_im.reshape(1, G * Pn)])
    dskip = ev_ssm_d[0].reshape(1, SSM_WIDTH)

    P = _mm(x0, Wmain, 'nn', name="ev_proj")
    fl = _mm(x0, Wf, 'nn', name="ev_proj_f")
    bf_pad = jnp.pad(ev_b_f.reshape(1, FOX_HEADS), ((0, 0), (0, LANE - FOX_HEADS)))
    cgate, sgate = _gate_fwd(fl, bf_pad, name="fox_gate")
    ccol = jnp.transpose(cgate[:, :FOX_HEADS]).reshape(FOX_HEADS, S, 1)
    crow = jnp.transpose(cgate[:, :FOX_HEADS]).reshape(FOX_HEADS, 1, S)
    fox, lse = _fox_fwd(P, ccol, crow, name="fox_fwd")
    u_s5 = P[:, qkv_w:]
    bu = _mm(u_s5, BB, 'nn', bmode='bo', name="s5_bu")
    hh = _s5_scan_fwd(bu, a_cat, name="s5_scan")
    yc = _mm(hh, CC, 'nn', bmode='abr', name="s5_y")
    y_s5, yg = _s5_out_fwd(yc, P, dskip, name="s5_out")
    z = _mm(yg, Wglu, 'nn', name="s5_glu_proj")
    ssm = _glu_fwd(z, name="s5_glu")
    cat = jnp.concatenate([fox.astype(BF16), ssm], axis=1)
    mix0 = _mm(cat, Wout_ev, 'nn', name="ev_out")
    x1, xh1, rs1 = _add_ln_fwd(x0, mix0, ln_mix_g[0], ln_mix_b[0], name="ln_mix0")
    f0, hf0, af0 = _ffn_fwd(x1, Wup[0], Wdn[0], cws[0], cbs[0], "l0")
    x2, xh2, rs2 = _add_ln_fwd(x1, f0, ln_ffn_g[0], ln_ffn_b[0], name="ln_ffn0")

    QW, KW = SWA_HEADS * SWA_HEAD_DIM, SWA_KV_HEADS * SWA_HEAD_DIM
    P1 = _mm(x2, Wodin, 'nn', name="od_proj")
    tabs = _rope_tables(positions.reshape(S, 1).astype(F32), name="rope_tables")
    qr = _rope_apply(P1, tabs, col0=0, width=QW, inverse=False, name="rope_q", out_dtype=BF16)
    kr = _rope_apply(P1, tabs, col0=QW, width=KW, inverse=False, name="rope_k", out_dtype=BF16)

    def heads(a2, nh):
        return jnp.transpose(a2.reshape(S, nh, SWA_HEAD_DIM), (1, 0, 2))

    def unheads(a3):
        return jnp.transpose(a3, (1, 0, 2)).reshape(S, -1)

    qT, kT = heads(qr, SWA_HEADS), heads(kr, SWA_KV_HEADS)
    vT = heads(P1[:, QW + KW:].astype(BF16), SWA_KV_HEADS)
    sink_rows = jnp.broadcast_to(od_sinks[0].reshape(SWA_KV_HEADS, SWA_GROUPS, 1, 1),
                                 (SWA_KV_HEADS, SWA_GROUPS, SWA_WINDOW, 1)).reshape(SWA_KV_HEADS, -1, 1)
    oT, Lsw = _swa_fwd(qT, kT, vT, sink_rows, name="swa_fwd")
    o_sw = unheads(oT).astype(BF16)
    mix1 = _mm(o_sw, Wodout, 'nn', name="od_out")
    x3, xh3, rs3 = _add_ln_fwd(x2, mix1, ln_mix_g[1], ln_mix_b[1], name="ln_mix1")
    f1, hf1, af1 = _ffn_fwd(x3, Wup[1], Wdn[1], cws[1], cbs[1], "l1")
    x4, xh4, rs4 = _add_ln_fwd(x3, f1, ln_ffn_g[1], ln_ffn_b[1], name="ln_ffn1")
    dy, loss_part = _loss_grad(x4, tgt, name="loss")

    dz4, dg_ffn1, db_ffn1 = _ln_bwd(dy, None, xh4, rs4, ln_ffn_g[1], name="lnb_ffn1")
    dx3f, dWup1, dWdn1, dcw1, dcb1 = _ffn_bwd(dz4, x3, hf1, af1, Wup[1], Wdn[1], cws[1], cbs[1], "l1")
    dz3, dg_mix1, db_mix1 = _ln_bwd(dz4, dx3f, xh3, rs3, ln_mix_g[1], name="lnb_mix1")
    do_sw = _mm(dz3, Wodout, 'nt', name="od_out_dx")
    dWodout = _mm(o_sw, dz3, 'tn', name="od_out_dw", out_dtype=BF16)
    doT = heads(do_sw, SWA_HEADS)
    dqT, dkT, dvT, dsink = _swa_bwd(qT, kT, vT, sink_rows, oT, Lsw, doT, name="swa_bwd")
    dq1 = _rope_apply(unheads(dqT), tabs, col0=0, width=QW, inverse=True, name="rope_dq", out_dtype=BF16)
    dk1 = _rope_apply(unheads(dkT[:, SWA_WINDOW:]), tabs, col0=0, width=KW, inverse=True, name="rope_dk",
                      out_dtype=BF16)
    dP1 = jnp.concatenate([dq1, dk1, unheads(dvT[:, SWA_WINDOW:]).astype(BF16)], axis=1)
    dx2m = _mm(dP1, Wodin, 'nt', name="od_proj_dx")
    dWodin = _mm(x2, dP1, 'tn', name="od_proj_dw", out_dtype=BF16)

    dz2, dg_ffn0, db_ffn0 = _ln_bwd(dz3, dx2m, xh2, rs2, ln_ffn_g[0], name="lnb_ffn0")
    dx1f, dWup0, dWdn0, dcw0, dcb0 = _ffn_bwd(dz2, x1, hf0, af0, Wup[0], Wdn[0], cws[0], cbs[0], "l0")
    dz1, dg_mix0, db_mix0 = _ln_bwd(dz2, dx1f, xh1, rs1, ln_mix_g[0], name="lnb_mix0")
    dcat = _mm(dz1, Wout_ev, 'nt', name="ev_out_dx")
    dWout_ev = _mm(cat, dz1, 'tn', name="ev_out_dw", out_dtype=BF16)
    dz = _glu_bwd(z, dcat, name="s5_glu_bwd")
    dyg = _mm(dz, Wglu, 'nt', name="s5_glu_dx")
    dWglu = _mm(yg, dz, 'tn', name="s5_glu_dw", out_dtype=BF16)
    dy_s5, du_dir, dD = _s5_out_bwd(dyg, y_s5, P, dskip, name="s5_out_bwd")
    dhh = _mm(dy_s5, CC, 'nt', bmode='bo', name="s5_y_dx")
    dCC = _mm(hh, dy_s5, 'tn', bmode='ao', name="s5_y_dw")
    lam, da_s5 = _s5_scan_bwd(dhh, hh, a_cat, name="s5_scan_bwd")
    du_bu = _mm(lam, BB, 'nt', bmode='abr', name="s5_bu_dx")
    dBB = _mm(u_s5, lam, 'tn', bmode='bo', name="s5_bu_dw")
    du = _combine([du_dir, du_bu], [1.0, 1.0], name="s5_du", out_dtype=BF16)
    dq0, dk0, dv0, dccol, dcrow = _fox_bwd(P, ccol, crow, fox, lse, dcat, name="fox_bwd")
    dc = jnp.transpose((dccol.reshape(FOX_HEADS, S) - dcrow.reshape(FOX_HEADS, S)))
    dc = jnp.pad(dc, ((0, 0), (0, LANE - FOX_HEADS)))
    dfl, dbf = _gate_bwd(dc, sgate, name="fox_gate_bwd")
    dP = jnp.concatenate([dq0, dk0, dv0, du], axis=1)
    dx0a = _mm(dP, Wmain, 'nt', name="ev_proj_dx")
    dx0b = _mm(dfl, Wf, 'nt', name="ev_proj_f_dx")
    dWmain = _mm(x0, dP, 'tn', name="ev_proj_dw", out_dtype=BF16)
    dWf = _mm(x0, dfl, 'tn', name="ev_proj_f_dw", out_dtype=BF16)
    grad_x = _combine([dz1, dx0a, dx0b], [ALPHA, 1.0, 1.0], name="grad_x")

    dbb_re = jnp.transpose(_diag_blocks(dBB[0], G), (0, 2, 1)).reshape(G * Pn, Cg)
    dbb_im = jnp.transpose(_diag_blocks(dBB[1], G), (0, 2, 1)).reshape(G * Pn, Cg)
    db_re, db_im, dg_re1, dg_im1 = _s5_bb_bwd(g_re1, g_im1, b_re2, b_im2, dbb_re, dbb_im, name="s5_bb_bwd")
    dlam_re, dlam_im, dlstep = _s5_disc_bwd(lam_r, lam_i, lstep, da_s5[0].reshape(G, Pn), da_s5[1].reshape(G, Pn),
                                            dg_re1.reshape(G, Pn), dg_im1.reshape(G, Pn), name="s5_disc_bwd")
    dc_re = jnp.transpose(_diag_blocks(dCC[0], G), (0, 2, 1))
    dc_im = -jnp.transpose(_diag_blocks(dCC[1], G), (0, 2, 1))

    def conv_w_full(d0, d1):
        return jnp.stack([jnp.reshape(jnp.transpose(d[:, :, :Fs], (1, 0, 2)), (3, N_CHIPS * Fs)) for d in (d0, d1)])

    def conv_b_full(d0, d1):
        return jnp.stack([jnp.reshape(d[:, 0, :Fs], (N_CHIPS * Fs,)) for d in (d0, d1)])

    small_local = dict(
        ev_b_f=dbf[:, :FOX_HEADS], ev_lambda_re=dlam_re, ev_lambda_im=dlam_im, ev_log_step=dlstep,
        ev_ssm_b_re=db_re, ev_ssm_b_im=db_im, ev_ssm_c_re=dc_re, ev_ssm_c_im=dc_im, ev_ssm_d=dD,
        od_sinks=dsink[:, :, 0],
        ln_mix_g=jnp.concatenate([dg_mix0, dg_mix1]), ln_mix_b=jnp.concatenate([db_mix0, db_mix1]),
        ffn_conv_w=conv_w_full(dcw0, dcw1), ffn_conv_b=conv_b_full(dcb0, dcb1),
        ln_ffn_g=jnp.concatenate([dg_ffn0, dg_ffn1]), ln_ffn_b=jnp.concatenate([db_ffn0, db_ffn1]))
    small = list(small_local.keys())
    red = _all_reduce_small(_pack([small_local[n] for n in small] + [loss_part]), name="ar_small")
    full_shapes = [W[n].shape if n != 'ffn_conv_w' else (DEPTH, 3, N_CHIPS * Fs) for n in small]
    pieces = _unpack(red, full_shapes + [()])
    loss = pieces[-1]
    gsmall = dict(zip(small, pieces[:-1]))
    chip = 2 * lax.axis_index("x") + lax.axis_index("y")
    gsmall['ffn_conv_w'] = lax.dynamic_slice_in_dim(gsmall['ffn_conv_w'], chip * Fs, Fs, axis=2)
    shapes = [W[n].shape for n in small]
    gs, ds_, ms, vs = _adamw(_pack([W[n] for n in small])[None], _pack([gsmall[n] for n in small])[None],
                             _pack([Mo[n] for n in small])[None], _pack([Vo[n] for n in small])[None],
                             name="adamw_small", tr=1 << 14)
    out_g = dict(zip(small, _unpack(gs, shapes)))
    out_d = dict(zip(small, _unpack(ds_, shapes)))
    out_m = dict(zip(small, _unpack(ms, shapes)))
    out_v = dict(zip(small, _unpack(vs, shapes)))

    dw_in_full = jnp.concatenate([dWmain[:, :qkv_w], dWf[:, :FOX_HEADS], dWmain[:, qkv_w:]], axis=1)
    gl = {
        ('ev_w_in', 0): _shards_from_cols(dw_in_full), ('ev_w_glu', 0): _shards_from_cols(dWglu),
        ('ev_w_out', 0): dWout_ev.reshape(N_CHIPS, D // N_CHIPS, D), ('od_w_in', 0): _shards_from_cols(dWodin),
        ('od_w_out', 0): dWodout.reshape(N_CHIPS, D // N_CHIPS, D),
        ('ffn_w_up', 0): dWup0, ('ffn_w_up', 1): dWup1,
        ('ffn_w_down', 0): dWdn0.reshape(N_CHIPS, Rd, D), ('ffn_w_down', 1): dWdn1.reshape(N_CHIPS, Rd, D)}
    glist = [gl[e] for e in big_e]
    sib = _sibling_send_halves(glist, name="rs_sibling")
    part = [_sum2_halves(g4, s4, name=f"rs_sum2_{n}{l}") for (n, l), g4, s4 in zip(big_e, glist, sib)]
    recv = _scatter_to_chips(part, name="rs_chips")
    halves = [_rowsum(r, name=f"rs_sum4_{n}{l}") for (n, l), r in zip(big_e, recv)]
    others = _sibling_join_halves(halves, name="rs_join")
    pairs = dict(zip(big_e, zip(halves, others)))
    for n in big:
        out_g[n], out_d[n], out_m[n], out_v[n] = _adamw(
            W[n], [pairs[(n, l)] for l in range(W[n].shape[0])], Mo[n], Vo[n], name=f"adamw_{n}")

    return (loss, grad_x.reshape(1, S, D), *[out_g[n] for n in names], *[out_d[n] for n in names],
            *[out_m[n] for n in names], *[out_v[n] for n in names])
```

```python
import functools
import math

import numpy as np
import jax
import jax.numpy as jnp
from jax import lax
from jax.experimental import pallas as pl
from jax.experimental.pallas import tpu as pltpu

F32 = jnp.float32
BF16 = jnp.bfloat16
MESH = pl.DeviceIdType.MESH
ANY = pl.BlockSpec(memory_space=pl.ANY)

D_MODEL = 2048
FOX_HEADS = 8
FOX_HEAD_DIM = 128
FOX_WIDTH = 1024
SSM_WIDTH = 1024
SSM_GROUP = 16
SSM_GROUPS = 64
SSM_STATE = 64
SWA_HEADS = 32
SWA_KV_HEADS = 4
SWA_HEAD_DIM = 64
SWA_GROUPS = 8
SWA_WINDOW = 128
ROPE_DIM = 16
ROPE_THETA = 500000.0
LN_EPS = 1e-5
DEPTH = 2
ALPHA = (2.0 * DEPTH) ** 0.25
ADAM_LR = 0.001
ADAM_B1 = 0.9
ADAM_B2 = 0.999
ADAM_EPS = 1e-08
ADAM_WD = 0.01
ADAM_STEP = 10
N_CHIPS = 4

VMEM_LIMIT = 56 * 1024 * 1024
LANE = 128


def _call(body, **kw):
    return pl.pallas_call(body, **kw)


def _cparams(sem):
    return pltpu.CompilerParams(dimension_semantics=sem, vmem_limit_bytes=VMEM_LIMIT)


def _rup(n, m):
    return (n + m - 1) // m * m


def _pick(n, pref):
    if n <= pref:
        return n
    for step in (128, 16, 8):
        for t in range(pref - pref % step, 0, -step):
            if n % t == 0:
                return t
    return n


def _tile2d(rows, cols, pref_rows=256, budget=256 * 1024):
    tr = _pick(rows, pref_rows)
    if tr < 64:
        tr = rows
    if cols % LANE:
        return tr, cols
    return tr, _pick(cols, max(LANE, budget // tr // LANE * LANE))


def _mm(a, b, mode, *, name, tm=512, tn=1024, tk=2048, bmode=None, out_dtype=F32):
    a3 = a if a.ndim == 3 else a[None]
    b3 = b if b.ndim == 3 else b[None]
    if mode == 'tn':
        K, M = a3.shape[1:]
    else:
        M, K = a3.shape[1:]
    N = b3.shape[1] if mode == 'nt' else b3.shape[2]
    tm, tn, tk = _pick(M, tm), _pick(N, tn), _pick(K, tk)
    nb = max(a3.shape[0], b3.shape[0])
    nbo, nbr = (1, nb) if bmode == 'abr' else (nb, 1)
    nk = K // tk
    nred = nbr * nk
    a_b = bmode in ('ao', 'abr')
    b_b = bmode in ('bo', 'abr')
    o_b = bmode in ('bo', 'ao')

    def bsel(flag, bo, br):
        return (bo + br) if flag else 0

    if mode == 'tn':
        a_spec = pl.BlockSpec((None, tk, tm), lambda bo, i, j, br, k: (bsel(a_b, bo, br), k, i))
    else:
        a_spec = pl.BlockSpec((None, tm, tk), lambda bo, i, j, br, k: (bsel(a_b, bo, br), i, k))
    if mode == 'nt':
        b_spec = pl.BlockSpec((None, tn, tk), lambda bo, i, j, br, k: (bsel(b_b, bo, br), j, k))
    else:
        b_spec = pl.BlockSpec((None, tk, tn), lambda bo, i, j, br, k: (bsel(b_b, bo, br), k, j))
    o_spec = pl.BlockSpec((None, tm, tn), lambda bo, i, j, br, k: (bsel(o_b, bo, br), i, j))
    dn = {'nn': (((1,), (0,)), ((), ())), 'nt': (((1,), (1,)), ((), ())), 'tn': (((0,), (0,)), ((), ()))}[mode]

    def body(a_ref, b_ref, o_ref, *scratch):
        r = lax.dot_general(a_ref[...].astype(BF16), b_ref[...].astype(BF16), dn, preferred_element_type=F32)
        if nred == 1:
            o_ref[...] = r.astype(out_dtype)
        else:
            acc = scratch[0]
            step = pl.program_id(3) * nk + pl.program_id(4)

            @pl.when(step == 0)
            def _():
                acc[...] = r

            @pl.when(step > 0)
            def _():
                acc[...] += r

            @pl.when(step == nred - 1)
            def _():
                o_ref[...] = acc[...].astype(out_dtype)

    out = _call(
        body, name=name,
        grid=(nbo, M // tm, N // tn, nbr, nk),
        in_specs=[a_spec, b_spec], out_specs=o_spec,
        out_shape=jax.ShapeDtypeStruct((nbo if o_b else 1, M, N), out_dtype),
        scratch_shapes=[] if nred == 1 else [pltpu.VMEM((tm, tn), F32)],
        compiler_params=_cparams(("parallel", "parallel", "parallel", "arbitrary", "arbitrary")),
    )(a3, b3)
    return out if o_b else out[0]


def _add_ln_fwd(x, r, g, b, *, name):
    S, D = x.shape
    tr = _pick(S, 256)

    def body(x_ref, r_ref, g_ref, b_ref, o_ref, xh_ref, rs_ref):
        z = ALPHA * x_ref[...] + r_ref[...]
        mu = jnp.mean(z, axis=-1, keepdims=True)
        zc = z - mu
        var = jnp.mean(zc * zc, axis=-1, keepdims=True)
        rstd = lax.rsqrt(var + LN_EPS)
        xh = zc * rstd
        xh_ref[...] = xh
        rs_ref[...] = rstd
        o_ref[...] = xh * g_ref[...] + b_ref[...]

    row = pl.BlockSpec((tr, D), lambda i: (i, 0))
    vec = pl.BlockSpec((1, D), lambda i: (0, 0))
    return _call(
        body, name=name, grid=(S // tr,),
        in_specs=[row, row, vec, vec],
        out_specs=[row, row, pl.BlockSpec((tr, 1), lambda i: (i, 0))],
        out_shape=[jax.ShapeDtypeStruct((S, D), F32), jax.ShapeDtypeStruct((S, D), F32),
                   jax.ShapeDtypeStruct((S, 1), F32)],
        compiler_params=_cparams(("parallel",)),
    )(x, r, g.reshape(1, D), b.reshape(1, D))


def _ln_bwd(da, db, xhat, rstd, g, *, name):
    S, D = xhat.shape
    tr = _pick(S, 256)
    two = db is not None

    def body(*refs):
        if two:
            da_ref, db_ref, xh_ref, rs_ref, g_ref, dz_ref, dg_ref, dbt_ref = refs
            dy = ALPHA * da_ref[...] + db_ref[...]
        else:
            da_ref, xh_ref, rs_ref, g_ref, dz_ref, dg_ref, dbt_ref = refs
            dy = da_ref[...]
        xh = xh_ref[...]
        dxh = dy * g_ref[...]
        m1 = jnp.mean(dxh, axis=-1, keepdims=True)
        m2 = jnp.mean(dxh * xh, axis=-1, keepdims=True)
        dz_ref[...] = rs_ref[...] * (dxh - m1 - xh * m2)
        pg = jnp.sum(dy * xh, axis=0, keepdims=True)
        pb = jnp.sum(dy, axis=0, keepdims=True)

        @pl.when(pl.program_id(0) == 0)
        def _():
            dg_ref[...] = pg
            dbt_ref[...] = pb

        @pl.when(pl.program_id(0) > 0)
        def _():
            dg_ref[...] += pg
            dbt_ref[...] += pb

    row = pl.BlockSpec((tr, D), lambda i: (i, 0))
    vec = pl.BlockSpec((1, D), lambda i: (0, 0))
    ins = [da] + ([db] if two else []) + [xhat, rstd, g.reshape(1, D)]
    in_specs = [row] + ([row] if two else []) + [row, pl.BlockSpec((tr, 1), lambda i: (i, 0)), vec]
    return _call(
        body, name=name, grid=(S // tr,),
        in_specs=in_specs, out_specs=[row, vec, vec],
        out_shape=[jax.ShapeDtypeStruct((S, D), F32), jax.ShapeDtypeStruct((1, D), F32),
                   jax.ShapeDtypeStruct((1, D), F32)],
        compiler_params=_cparams(("arbitrary",)),
    )(*ins)


def _loss_grad(y, t, *, name):
    S, D = y.shape
    tr = _pick(S, 256)

    def body(y_ref, t_ref, dy_ref, l_ref):
        e = y_ref[...] - t_ref[...]
        dy_ref[...] = e * (1.0 / D)
        part = 0.5 * jnp.sum(jnp.sum(e * e, axis=-1, keepdims=True) * (1.0 / D), axis=0, keepdims=True)

        @pl.when(pl.program_id(0) == 0)
        def _():
            l_ref[...] = part

        @pl.when(pl.program_id(0) > 0)
        def _():
            l_ref[...] += part

    row = pl.BlockSpec((tr, D), lambda i: (i, 0))
    return _call(
        body, name=name, grid=(S // tr,), in_specs=[row, row],
        out_specs=[row, pl.BlockSpec((1, 1), lambda i: (0, 0))],
        out_shape=[jax.ShapeDtypeStruct((S, D), F32), jax.ShapeDtypeStruct((1, 1), F32)],
        compiler_params=_cparams(("arbitrary",)),
    )(y, t)


def _combine(terms, scales, *, name, out_dtype=F32):
    S, D = terms[0].shape
    tr = _pick(S, 256)
    n = len(terms)

    def body(*refs):
        acc = scales[0] * refs[0][...].astype(F32)
        for i in range(1, n):
            acc = acc + scales[i] * refs[i][...].astype(F32)
        refs[n][...] = acc.astype(out_dtype)

    row = pl.BlockSpec((tr, D), lambda i: (i, 0))
    return _call(
        body, name=name, grid=(S // tr,), in_specs=[row] * n, out_specs=row,
        out_shape=jax.ShapeDtypeStruct((S, D), out_dtype),
        compiler_params=_cparams(("parallel",)),
    )(*terms)


def _split3(x):
    h = x.astype(BF16)
    r = x - h.astype(F32)
    m = r.astype(BF16)
    l = (r - m.astype(F32)).astype(BF16)
    return h, m, l


def _tri_matmul(tri_bf, x):
    h, m, l = _split3(x)
    dn = (((1,), (0,)), ((), ()))
    return (lax.dot_general(tri_bf, l, dn, preferred_element_type=F32)
            + lax.dot_general(tri_bf, m, dn, preferred_element_type=F32)
            + lax.dot_general(tri_bf, h, dn, preferred_element_type=F32))


def _gate_fwd(fl, bf, *, name):
    S = fl.shape[0]
    tc = _pick(S, 256)
    nchunk = S // tc

    def body(fl_ref, bf_ref, c_ref, sg_ref):
        r = lax.broadcasted_iota(jnp.int32, (tc, tc), 0)
        cidx = lax.broadcasted_iota(jnp.int32, (tc, tc), 1)
        tri = (r >= cidx).astype(BF16)
        carry = jnp.zeros((1, LANE), F32)
        for ch in range(nchunk):
            x = fl_ref[pl.ds(ch * tc, tc), :] + bf_ref[...]
            lf = jnp.minimum(x, 0.0) - jnp.log(1.0 + jnp.exp(-jnp.abs(x)))
            sg_ref[pl.ds(ch * tc, tc), :] = jax.nn.sigmoid(-x)
            c_ref[pl.ds(ch * tc, tc), :] = _tri_matmul(tri, lf) + carry
            carry = carry + jnp.sum(lf, axis=0, keepdims=True)

    full = pl.BlockSpec((S, LANE), lambda: (0, 0))
    return _call(
        body, name=name, in_specs=[full, pl.BlockSpec((1, LANE), lambda: (0, 0))], out_specs=[full, full],
        out_shape=[jax.ShapeDtypeStruct((S, LANE), F32)] * 2,
        compiler_params=pltpu.CompilerParams(vmem_limit_bytes=VMEM_LIMIT),
    )(fl, bf)


def _gate_bwd(dc, sg, *, name):
    S = dc.shape[0]
    tc = _pick(S, 256)
    nchunk = S // tc

    def body(dc_ref, sg_ref, dfl_ref, db_ref):
        r = lax.broadcasted_iota(jnp.int32, (tc, tc), 0)
        cidx = lax.broadcasted_iota(jnp.int32, (tc, tc), 1)
        tri = (r <= cidx).astype(BF16)
        carry = jnp.zeros((1, LANE), F32)
        dbacc = jnp.zeros((1, LANE), F32)
        for ch in reversed(range(nchunk)):
            d = dc_ref[pl.ds(ch * tc, tc), :]
            dfl = (_tri_matmul(tri, d) + carry) * sg_ref[pl.ds(ch * tc, tc), :]
            dfl_ref[pl.ds(ch * tc, tc), :] = dfl
            dbacc = dbacc + jnp.sum(dfl, axis=0, keepdims=True)
            carry = carry + jnp.sum(d, axis=0, keepdims=True)
        db_ref[...] = dbacc

    full = pl.BlockSpec((S, LANE), lambda: (0, 0))
    return _call(
        body, name=name, in_specs=[full, full], out_specs=[full, pl.BlockSpec((1, LANE), lambda: (0, 0))],
        out_shape=[jax.ShapeDtypeStruct((S, LANE), F32), jax.ShapeDtypeStruct((1, LANE), F32)],
        compiler_params=pltpu.CompilerParams(vmem_limit_bytes=VMEM_LIMIT),
    )(dc, sg)


def _fox_scores(q_ref, k_ref, cc_ref, cr_ref, qi, tq, S):
    scale = 1.0 / math.sqrt(FOX_HEAD_DIM)
    s = lax.dot_general(q_ref[...].astype(BF16), k_ref[...].astype(BF16), (((1,), (1,)), ((), ())),
                        preferred_element_type=F32) * scale
    s = s + cc_ref[...] - cr_ref[...]
    row = lax.broadcasted_iota(jnp.int32, (tq, S), 0) + qi * tq
    col = lax.broadcasted_iota(jnp.int32, (tq, S), 1)
    return s, row >= col


def _fox_fwd(P, ccol, crow, *, name):
    S = P.shape[0]
    tq = _pick(S, 256)
    H = FOX_HEADS

    def body(q_ref, k_ref, v_ref, cc_ref, cr_ref, o_ref, l_ref):
        s, causal = _fox_scores(q_ref, k_ref, cc_ref, cr_ref, pl.program_id(1), tq, S)
        s = jnp.where(causal, s, -1e30)
        m = jnp.max(s, axis=-1, keepdims=True)
        e = jnp.exp(s - m)
        den = jnp.sum(e, axis=-1, keepdims=True)
        p = e / den
        o_ref[...] = jnp.dot(p.astype(BF16), v_ref[...].astype(BF16), preferred_element_type=F32)
        l_ref[...] = m + jnp.log(den)

    return _call(
        body, name=name, grid=(H, S // tq),
        in_specs=[pl.BlockSpec((tq, 128), lambda h, i: (i, h)),
                  pl.BlockSpec((S, 128), lambda h, i: (0, H + h)),
                  pl.BlockSpec((S, 128), lambda h, i: (0, 2 * H + h)),
                  pl.BlockSpec((None, tq, 1), lambda h, i: (h, i, 0)),
                  pl.BlockSpec((None, 1, S), lambda h, i: (h, 0, 0))],
        out_specs=[pl.BlockSpec((tq, 128), lambda h, i: (i, h)),
                   pl.BlockSpec((None, tq, 1), lambda h, i: (h, i, 0))],
        out_shape=[jax.ShapeDtypeStruct((S, FOX_WIDTH), F32), jax.ShapeDtypeStruct((H, S, 1), F32)],
        compiler_params=_cparams(("parallel", "parallel")),
    )(P, P, P, ccol, crow)


def _fox_bwd(P, ccol, crow, o, lse, dcat, *, name):
    S = P.shape[0]
    tq = _pick(S, 256)
    H = FOX_HEADS
    nq = S // tq
    scale = 1.0 / math.sqrt(FOX_HEAD_DIM)

    def body(q_ref, k_ref, v_ref, cc_ref, cr_ref, o_ref, l_ref, do_ref,
             dq_ref, dk_ref, dv_ref, dcc_ref, dcr_ref, dk_acc, dv_acc):
        qi = pl.program_id(1)
        s, causal = _fox_scores(q_ref, k_ref, cc_ref, cr_ref, qi, tq, S)
        p = jnp.where(causal, jnp.exp(s - l_ref[...]), 0.0)
        do = do_ref[...]
        do_bf = do.astype(BF16)
        dp = lax.dot_general(do_bf, v_ref[...].astype(BF16), (((1,), (1,)), ((), ())), preferred_element_type=F32)
        delta = jnp.sum(do * o_ref[...], axis=-1, keepdims=True)
        ds = p * (dp - delta)
        ds_bf = ds.astype(BF16)
        dq_ref[...] = (jnp.dot(ds_bf, k_ref[...].astype(BF16), preferred_element_type=F32) * scale).astype(BF16)
        dkp = lax.dot_general(ds_bf, q_ref[...].astype(BF16), (((0,), (0,)), ((), ())),
                              preferred_element_type=F32) * scale
        dvp = lax.dot_general(p.astype(BF16), do_bf, (((0,), (0,)), ((), ())), preferred_element_type=F32)
        dcc_ref[...] = jnp.sum(ds, axis=-1, keepdims=True)
        dcr = jnp.sum(ds, axis=0, keepdims=True)

        @pl.when(qi == 0)
        def _():
            dk_acc[...] = dkp
            dv_acc[...] = dvp
            dcr_ref[...] = dcr

        @pl.when(qi > 0)
        def _():
            dk_acc[...] += dkp
            dv_acc[...] += dvp
            dcr_ref[...] += dcr

        @pl.when(qi == nq - 1)
        def _():
            dk_ref[...] = dk_acc[...].astype(BF16)
            dv_ref[...] = dv_acc[...].astype(BF16)

    qblk = pl.BlockSpec((tq, 128), lambda h, i: (i, h))
    kvo = pl.BlockSpec((S, 128), lambda h, i: (0, h))
    col = pl.BlockSpec((None, tq, 1), lambda h, i: (h, i, 0))
    rowv = pl.BlockSpec((None, 1, S), lambda h, i: (h, 0, 0))
    return _call(
        body, name=name, grid=(H, nq),
        in_specs=[qblk,
                  pl.BlockSpec((S, 128), lambda h, i: (0, H + h)),
                  pl.BlockSpec((S, 128), lambda h, i: (0, 2 * H + h)),
                  col, rowv, qblk, col, qblk],
        out_specs=[qblk, kvo, kvo, col, rowv],
        out_shape=[jax.ShapeDtypeStruct((S, FOX_WIDTH), BF16)] * 3
        + [jax.ShapeDtypeStruct((H, S, 1), F32), jax.ShapeDtypeStruct((H, 1, S), F32)],
        scratch_shapes=[pltpu.VMEM((S, 128), F32), pltpu.VMEM((S, 128), F32)],
        compiler_params=_cparams(("parallel", "arbitrary")),
    )(P, P, P, ccol, crow, o, lse, dcat)


def _s5_disc_fwd(lr, li, ls, *, name):
    G, Pn = lr.shape

    def body(lr_ref, li_ref, ls_ref, ar_ref, ai_ref, gr_ref, gi_ref):
        lr_, li_ = lr_ref[...], li_ref[...]
        dt = jnp.exp(ls_ref[...])
        mag = jnp.exp(lr_ * dt)
        th = li_ * dt
        ar = mag * jnp.cos(th)
        ai = mag * jnp.sin(th)
        den = lr_ * lr_ + li_ * li_
        xr = ar - 1.0
        ar_ref[...] = ar
        ai_ref[...] = ai
        gr_ref[...] = (xr * lr_ + ai * li_) / den
        gi_ref[...] = (ai * lr_ - xr * li_) / den

    sq = pl.BlockSpec((G, Pn), lambda: (0, 0))
    return _call(
        body, name=name, in_specs=[sq, sq, pl.BlockSpec((G, 1), lambda: (0, 0))], out_specs=[sq] * 4,
        out_shape=[jax.ShapeDtypeStruct((G, Pn), F32)] * 4,
    )(lr, li, ls)


def _s5_disc_bwd(lr, li, ls, dar, dai, dgr, dgi, *, name):
    G, Pn = lr.shape

    def body(lr_ref, li_ref, ls_ref, dar_ref, dai_ref, dgr_ref, dgi_ref, dlr_ref, dli_ref, dls_ref):
        lr_, li_ = lr_ref[...], li_ref[...]
        dt = jnp.exp(ls_ref[...])
        mag = jnp.exp(lr_ * dt)
        th = li_ * dt
        ar = mag * jnp.cos(th)
        ai = mag * jnp.sin(th)
        den = lr_ * lr_ + li_ * li_
        xr = ar - 1.0
        xi = ai
        g_re = (xr * lr_ + xi * li_) / den
        g_im = (xi * lr_ - xr * li_) / den
        dgr_, dgi_ = dgr_ref[...], dgi_ref[...]
        dxr = (dgr_ * lr_ - dgi_ * li_) / den
        dxi = (dgr_ * li_ + dgi_ * lr_) / den
        dden = -(dgr_ * g_re + dgi_ * g_im) / den
        dlr = (dgr_ * xr + dgi_ * xi) / den + 2.0 * dden * lr_
        dli = (dgr_ * xi - dgi_ * xr) / den + 2.0 * dden * li_
        da_r = dar_ref[...] + dxr
        da_i = dai_ref[...] + dxi
        dmag_mag = da_r * ar + da_i * ai
        dth = da_i * ar - da_r * ai
        dlr_ref[...] = dlr + dmag_mag * dt
        dli_ref[...] = dli + dth * dt
        ddt = jnp.sum(dmag_mag * lr_ + dth * li_, axis=-1, keepdims=True)
        dls_ref[...] = ddt * dt

    sq = pl.BlockSpec((G, Pn), lambda: (0, 0))
    c1 = pl.BlockSpec((G, 1), lambda: (0, 0))
    return _call(
        body, name=name, in_specs=[sq, sq, c1, sq, sq, sq, sq], out_specs=[sq, sq, c1],
        out_shape=[jax.ShapeDtypeStruct((G, Pn), F32)] * 2 + [jax.ShapeDtypeStruct((G, 1), F32)],
    )(lr, li, ls, dar, dai, dgr, dgi)


def _s5_bb_fwd(gr, gi, br, bi, *, name):
    R, C = br.shape

    def body(gr_ref, gi_ref, br_ref, bi_ref, or_ref, oi_ref):
        g_r, g_i, b_r, b_i = gr_ref[...], gi_ref[...], br_ref[...], bi_ref[...]
        or_ref[...] = g_r * b_r - g_i * b_i
        oi_ref[...] = g_r * b_i + g_i * b_r

    w = pl.BlockSpec((R, C), lambda: (0, 0))
    c1 = pl.BlockSpec((R, 1), lambda: (0, 0))
    return _call(body, name=name, in_specs=[c1, c1, w, w], out_specs=[w, w],
                 out_shape=[jax.ShapeDtypeStruct((R, C), F32)] * 2)(gr, gi, br, bi)


def _s5_bb_bwd(gr, gi, br, bi, dbbr, dbbi, *, name):
    R, C = br.shape

    def body(gr_ref, gi_ref, br_ref, bi_ref, dr_ref, di_ref, dbr_ref, dbi_ref, dgr_ref, dgi_ref):
        g_r, g_i, b_r, b_i = gr_ref[...], gi_ref[...], br_ref[...], bi_ref[...]
        d_r, d_i = dr_ref[...], di_ref[...]
        dbr_ref[...] = g_r * d_r + g_i * d_i
        dbi_ref[...] = g_r * d_i - g_i * d_r
        dgr_ref[...] = jnp.sum(d_r * b_r + d_i * b_i, axis=-1, keepdims=True)
        dgi_ref[...] = jnp.sum(d_i * b_r - d_r * b_i, axis=-1, keepdims=True)

    w = pl.BlockSpec((R, C), lambda: (0, 0))
    c1 = pl.BlockSpec((R, 1), lambda: (0, 0))
    return _call(body, name=name, in_specs=[c1, c1, w, w, w, w], out_specs=[w, w, c1, c1],
                 out_shape=[jax.ShapeDtypeStruct((R, C), F32)] * 2 + [jax.ShapeDtypeStruct((R, 1), F32)] * 2,
                 )(gr, gi, br, bi, dbbr, dbbi)


_DIAG_TILE = 8


def _diag_mask(gr, gc):
    rows, cols = _DIAG_TILE * gr, _DIAG_TILE * gc
    r = lax.broadcasted_iota(jnp.int32, (rows, cols), 0) >> (gr.bit_length() - 1)
    c = lax.broadcasted_iota(jnp.int32, (rows, cols), 1) >> (gc.bit_length() - 1)
    return r == c


def _diag_expand(t2, gr, gc, *, name):
    _, R, _ = t2.shape
    G = R // gr
    nt = G // _DIAG_TILE
    rows, cols = _DIAG_TILE * gr, _DIAG_TILE * gc

    def body(t_ref, o_ref):
        @pl.when(pl.program_id(1) == pl.program_id(2))
        def _():
            src = lax.broadcasted_iota(jnp.int32, (gc, cols), 0)
            dst = lax.broadcasted_iota(jnp.int32, (gc, cols), 1) & (gc - 1)
            spread = (src == dst).astype(BF16)
            y = jnp.dot(t_ref[...].astype(BF16), spread, preferred_element_type=F32)
            o_ref[...] = jnp.where(_diag_mask(gr, gc), y, 0.0).astype(BF16)

        @pl.when(pl.program_id(1) != pl.program_id(2))
        def _():
            o_ref[...] = jnp.zeros_like(o_ref)

    return _call(
        body, name=name, grid=(2, nt, nt),
        in_specs=[pl.BlockSpec((None, rows, gc), lambda p, i, j: (p, i, 0))],
        out_specs=pl.BlockSpec((None, rows, cols), lambda p, i, j: (p, i, j)),
        out_shape=jax.ShapeDtypeStruct((2, R, G * gc), BF16),
        compiler_params=_cparams(("parallel",) * 3),
    )(t2)


def _diag_extract(xd, gr, gc, *, name):
    _, R, _ = xd.shape
    nt = R // gr // _DIAG_TILE
    rows, cols = _DIAG_TILE * gr, _DIAG_TILE * gc

    def body(x_ref, o_ref):
        src = lax.broadcasted_iota(jnp.int32, (cols, gc), 0) & (gc - 1)
        dst = lax.broadcasted_iota(jnp.int32, (cols, gc), 1)
        fold = (src == dst).astype(BF16)
        parts = _split3(jnp.where(_diag_mask(gr, gc), x_ref[...], 0.0))
        acc = jnp.dot(parts[2], fold, preferred_element_type=F32)
        acc = acc + jnp.dot(parts[1], fold, preferred_element_type=F32)
        o_ref[...] = acc + jnp.dot(parts[0], fold, preferred_element_type=F32)

    return _call(
        body, name=name, grid=(2, nt),
        in_specs=[pl.BlockSpec((None, rows, cols), lambda p, i: (p, i, i))],
        out_specs=pl.BlockSpec((None, rows, gc), lambda p, i: (p, i, 0)),
        out_shape=jax.ShapeDtypeStruct((2, R, gc), F32),
        compiler_params=_cparams(("parallel",) * 2),
    )(xd)


def _s5_scan_fwd(bu, a, *, name):
    _, S, N = bu.shape
    tc = 512
    nt = N // tc

    def body(a_ref, b_ref, h_ref):
        ar, ai = a_ref[0], a_ref[1]

        def step(t, carry):
            hr, hi = carry
            nr = ar * hr - ai * hi + b_ref[0, pl.ds(t, 1), :]
            ni = ar * hi + ai * hr + b_ref[1, pl.ds(t, 1), :]
            h_ref[0, pl.ds(t, 1), :] = nr
            h_ref[1, pl.ds(t, 1), :] = ni
            return nr, ni

        z = jnp.zeros((1, tc), F32)
        lax.fori_loop(0, S, step, (z, z), unroll=8)

    vec = pl.BlockSpec((2, 1, tc), lambda j: (0, 0, j))
    mat = pl.BlockSpec((2, S, tc), lambda j: (0, 0, j))
    return _call(
        body, name=name, grid=(nt,), in_specs=[vec, mat], out_specs=mat,
        out_shape=jax.ShapeDtypeStruct((2, S, N), F32),
        compiler_params=_cparams(("parallel",)),
    )(a, bu)


def _s5_scan_bwd(g, h, a, *, name):
    _, S, N = g.shape
    tc = 256
    nt = N // tc

    def body(a_ref, g_ref, h_ref, l_ref, da_ref):
        ar, ai = a_ref[0], a_ref[1]

        def step(i, carry):
            t = S - 1 - i
            lr, li, dar, dai = carry
            nr = g_ref[0, pl.ds(t, 1), :] + ar * lr + ai * li
            ni = g_ref[1, pl.ds(t, 1), :] + ar * li - ai * lr
            l_ref[0, pl.ds(t, 1), :] = nr
            l_ref[1, pl.ds(t, 1), :] = ni
            tp = jnp.maximum(t - 1, 0)
            keep = jnp.where(t > 0, 1.0, 0.0).astype(F32)
            hpr = h_ref[0, pl.ds(tp, 1), :] * keep
            hpi = h_ref[1, pl.ds(tp, 1), :] * keep
            return nr, ni, dar + nr * hpr + ni * hpi, dai + ni * hpr - nr * hpi

        z = jnp.zeros((1, tc), F32)
        _, _, dar, dai = lax.fori_loop(0, S, step, (z, z, z, z), unroll=8)
        da_ref[0] = dar
        da_ref[1] = dai

    vec = pl.BlockSpec((2, 1, tc), lambda j: (0, 0, j))
    mat = pl.BlockSpec((2, S, tc), lambda j: (0, 0, j))
    return _call(
        body, name=name, grid=(nt,), in_specs=[vec, mat, mat], out_specs=[mat, vec],
        out_shape=[jax.ShapeDtypeStruct((2, S, N), F32), jax.ShapeDtypeStruct((2, 1, N), F32)],
        compiler_params=_cparams(("parallel",)),
    )(a, g, h)


_GELU_C = math.sqrt(2.0 / math.pi)


def _s5_out_fwd(yc, P, dskip, *, name):
    S, W = yc.shape
    tr = _pick(S, 256)
    ub = 3 * FOX_WIDTH // W

    def body(yc_ref, u_ref, d_ref, y_ref, yg_ref):
        y = yc_ref[...] + d_ref[...] * u_ref[...]
        y_ref[...] = y
        t = jnp.tanh(_GELU_C * (y + 0.044715 * y * y * y))
        yg_ref[...] = (0.5 * y * (1.0 + t)).astype(BF16)

    row = pl.BlockSpec((tr, W), lambda i: (i, 0))
    return _call(
        body, name=name, grid=(S // tr,),
        in_specs=[row, pl.BlockSpec((tr, W), lambda i: (i, ub)), pl.BlockSpec((1, W), lambda i: (0, 0))],
        out_specs=[row, row],
        out_shape=[jax.ShapeDtypeStruct((S, W), F32), jax.ShapeDtypeStruct((S, W), BF16)],
        compiler_params=_cparams(("parallel",)),
    )(yc, P, dskip)


def _s5_out_bwd(dyg, y, P, dskip, *, name):
    S, W = y.shape
    tr = _pick(S, 256)
    ub = 3 * FOX_WIDTH // W

    def body(dyg_ref, y_ref, u_ref, d_ref, dy_ref, du_ref, dd_ref):
        y_ = y_ref[...]
        inner = _GELU_C * (y_ + 0.044715 * y_ * y_ * y_)
        t = jnp.tanh(inner)
        dgelu = 0.5 * (1.0 + t) + 0.5 * y_ * (1.0 - t * t) * _GELU_C * (1.0 + 3.0 * 0.044715 * y_ * y_)
        dy = dyg_ref[...] * dgelu
        dy_ref[...] = dy.astype(BF16)
        du_ref[...] = d_ref[...] * dy
        part = jnp.sum(dy * u_ref[...], axis=0, keepdims=True)

        @pl.when(pl.program_id(0) == 0)
        def _():
            dd_ref[...] = part

        @pl.when(pl.program_id(0) > 0)
        def _():
            dd_ref[...] += part

    row = pl.BlockSpec((tr, W), lambda i: (i, 0))
    vec = pl.BlockSpec((1, W), lambda i: (0, 0))
    return _call(
        body, name=name, grid=(S // tr,),
        in_specs=[row, row, pl.BlockSpec((tr, W), lambda i: (i, ub)), vec],
        out_specs=[row, row, vec],
        out_shape=[jax.ShapeDtypeStruct((S, W), BF16), jax.ShapeDtypeStruct((S, W), F32),
                   jax.ShapeDtypeStruct((1, W), F32)],
        compiler_params=_cparams(("arbitrary",)),
    )(dyg, y, P, dskip)


def _glu_fwd(z, *, name):
    S, W2 = z.shape
    W = W2 // 2
    tr = _pick(S, 256)

    def body(z1_ref, z2_ref, o_ref):
        o_ref[...] = (z1_ref[...] * jax.nn.sigmoid(z2_ref[...])).astype(BF16)

    return _call(
        body, name=name, grid=(S // tr,),
        in_specs=[pl.BlockSpec((tr, W), lambda i: (i, 0)), pl.BlockSpec((tr, W), lambda i: (i, 1))],
        out_specs=pl.BlockSpec((tr, W), lambda i: (i, 0)),
        out_shape=jax.ShapeDtypeStruct((S, W), BF16),
        compiler_params=_cparams(("parallel",)),
    )(z, z)


def _glu_bwd(z, dcat, *, name):
    S, W2 = z.shape
    W = W2 // 2
    tr = _pick(S, 256)

    def body(z1_ref, z2_ref, d_ref, dz1_ref, dz2_ref):
        sg = jax.nn.sigmoid(z2_ref[...])
        d = d_ref[...]
        dz1_ref[...] = (d * sg).astype(BF16)
        dz2_ref[...] = (d * z1_ref[...] * sg * (1.0 - sg)).astype(BF16)

    lo = pl.BlockSpec((tr, W), lambda i: (i, 0))
    hi = pl.BlockSpec((tr, W), lambda i: (i, 1))
    dz1, dz2 = _call(
        body, name=name, grid=(S // tr,), in_specs=[lo, hi, hi], out_specs=[lo, lo],
        out_shape=[jax.ShapeDtypeStruct((S, W), BF16)] * 2,
        compiler_params=_cparams(("parallel",)),
    )(z, z, dcat)
    return jnp.concatenate([dz1, dz2], axis=1)


def _act_fwd(h, cw, cb, *, name):
    _, S, FP = h.shape
    tr = _pick(S, 256)
    hb = tr // 8

    def conv(x_ref, halo_ref, w_ref, b_ref, ext, first):
        ext[pl.ds(0, 8), :] = jnp.where(first, 0.0, halo_ref[...])
        ext[pl.ds(8, tr), :] = x_ref[...]
        return (b_ref[...] + w_ref[pl.ds(2, 1), :] * ext[pl.ds(8, tr), :]
                + w_ref[pl.ds(1, 1), :] * ext[pl.ds(7, tr), :] + w_ref[pl.ds(0, 1), :] * ext[pl.ds(6, tr), :])

    def body(g_ref, gh_ref, v_ref, vh_ref, wg_ref, wv_ref, bg_ref, bv_ref, a_ref, ext):
        first = pl.program_id(1) == 0
        cg = conv(g_ref, gh_ref, wg_ref, bg_ref, ext, first)
        cv = conv(v_ref, vh_ref, wv_ref, bv_ref, ext, first)
        a_ref[...] = (cg * jax.nn.sigmoid(cg) * cv).astype(BF16)

    def main(off):
        return pl.BlockSpec((None, tr, FP), lambda j, i: (j + off, i, 0))

    def halo(off):
        return pl.BlockSpec((None, 8, FP), lambda j, i: (j + off, jnp.maximum(i * hb - 1, 0), 0))

    def wspec(off):
        return pl.BlockSpec((None, 3, FP), lambda j, i: (j + off, 0, 0))

    def bspec(off):
        return pl.BlockSpec((None, 1, FP), lambda j, i: (j + off, 0, 0))

    cb3 = cb.reshape(4, 1, FP)
    return _call(
        body, name=name, grid=(2, S // tr),
        in_specs=[main(0), halo(0), main(2), halo(2), wspec(0), wspec(2), bspec(0), bspec(2)],
        out_specs=pl.BlockSpec((None, tr, FP), lambda j, i: (j, i, 0)),
        out_shape=jax.ShapeDtypeStruct((2, S, FP), BF16),
        scratch_shapes=[pltpu.VMEM((tr + 8, FP), F32)],
        compiler_params=_cparams(("parallel", "arbitrary")),
    )(h, h, h, h, cw, cw, cb3, cb3)


def _act_bwd(h, da, cw, cb, *, name):
    _, S, FP = h.shape
    tr = _pick(S, 128)
    hb = tr // 8
    nr = S // tr

    def fill(ext, x_ref, prev_ref, next_ref, first, last):
        ext[pl.ds(0, 8), :] = jnp.where(first, 0.0, prev_ref[...])
        ext[pl.ds(8, tr), :] = x_ref[...]
        ext[pl.ds(8 + tr, 8), :] = jnp.where(last, 0.0, next_ref[...])

    def convo(ext, w, b, base, n):
        return (b + w[2] * ext[pl.ds(base, n), :] + w[1] * ext[pl.ds(base - 1, n), :]
                + w[0] * ext[pl.ds(base - 2, n), :])

    def body(g_ref, gp_ref, gn_ref, v_ref, vp_ref, vn_ref, da_ref, dan_ref,
             wg_ref, wv_ref, bg_ref, bv_ref,
             dg_ref, dv_ref, dwg_ref, dwv_ref, dbg_ref, dbv_ref, eg, ev, ed, dcg, dcv):
        i = pl.program_id(1)
        first = i == 0
        last = i == nr - 1
        fill(eg, g_ref, gp_ref, gn_ref, first, last)
        fill(ev, v_ref, vp_ref, vn_ref, first, last)
        ed[pl.ds(0, tr), :] = da_ref[...]
        ed[pl.ds(tr, 8), :] = jnp.where(last, 0.0, dan_ref[...])
        wg = [wg_ref[pl.ds(k, 1), :] for k in range(3)]
        wv = [wv_ref[pl.ds(k, 1), :] for k in range(3)]
        n = tr + 8
        cg = convo(eg, wg, bg_ref[...], 8, n)
        cv = convo(ev, wv, bv_ref[...], 8, n)
        sg = jax.nn.sigmoid(cg)
        d = ed[...]
        dcg[...] = d * cv * sg * (1.0 + cg * (1.0 - sg))
        dcv[...] = d * cg * sg
        for (dc, w, e, dh_ref, dw_ref, db_ref) in ((dcg, wg, eg, dg_ref, dwg_ref, dbg_ref),
                                                  (dcv, wv, ev, dv_ref, dwv_ref, dbv_ref)):
            d0 = dc[pl.ds(0, tr), :]
            dh_ref[...] = (w[2] * d0 + w[1] * dc[pl.ds(1, tr), :] + w[0] * dc[pl.ds(2, tr), :]).astype(BF16)
            pw = [jnp.sum(d0 * e[pl.ds(6 + k, tr), :], axis=0, keepdims=True) for k in range(3)]
            pb = jnp.sum(d0, axis=0, keepdims=True)

            @pl.when(first)
            def _():
                for k in range(3):
                    dw_ref[pl.ds(k, 1), :] = pw[k]
                db_ref[...] = pb

            @pl.when(jnp.logical_not(first))
            def _():
                for k in range(3):
                    dw_ref[pl.ds(k, 1), :] += pw[k]
                db_ref[...] += pb

    def main(off):
        return pl.BlockSpec((None, tr, FP), lambda j, i: (j + off, i, 0))

    def prev(off):
        return pl.BlockSpec((None, 8, FP), lambda j, i: (j + off, jnp.maximum(i * hb - 1, 0), 0))

    def nxt(off):
        return pl.BlockSpec((None, 8, FP), lambda j, i: (j + off, jnp.minimum((i + 1) * hb, S // 8 - 1), 0))

    def wspec(off):
        return pl.BlockSpec((None, 3, FP), lambda j, i: (j + off, 0, 0))

    def bspec(off):
        return pl.BlockSpec((None, 1, FP), lambda j, i: (j + off, 0, 0))

    cb3 = cb.reshape(4, 1, FP)
    dg, dv, dwg, dwv, dbg, dbv = _call(
        body, name=name, grid=(2, nr),
        in_specs=[main(0), prev(0), nxt(0), main(2), prev(2), nxt(2), main(0), nxt(0),
                  wspec(0), wspec(2), bspec(0), bspec(2)],
        out_specs=[main(0), main(0), wspec(0), wspec(0), bspec(0), bspec(0)],
        out_shape=[jax.ShapeDtypeStruct((2, S, FP), BF16)] * 2
        + [jax.ShapeDtypeStruct((2, 3, FP), F32)] * 2 + [jax.ShapeDtypeStruct((2, 1, FP), F32)] * 2,
        scratch_shapes=[pltpu.VMEM((tr + 16, FP), F32), pltpu.VMEM((tr + 16, FP), F32),
                        pltpu.VMEM((tr + 8, FP), F32), pltpu.VMEM((tr + 8, FP), F32),
                        pltpu.VMEM((tr + 8, FP), F32)],
        compiler_params=_cparams(("parallel", "arbitrary")),
    )(h, h, h, h, h, h, da, da, cw, cw, cb3, cb3)
    return (jnp.concatenate([dg, dv], axis=0), jnp.concatenate([dwg, dwv], axis=0),
            jnp.concatenate([dbg, dbv], axis=0))


def _rope_tables(posf, *, name):
    S = posf.shape[0]
    half = ROPE_DIM // 2
    d = np.arange(LANE) % SWA_HEAD_DIM
    invf = np.where(d < ROPE_DIM, ROPE_THETA ** (-(d % half).astype(np.float64) / half), 0.0).astype(np.float32)
    m_rot = (d < ROPE_DIM).astype(np.float32)
    m_a = (d < half).astype(np.float32)
    m_b = ((d >= half) & (d < ROPE_DIM)).astype(np.float32)
    consts = jnp.asarray(np.stack([invf, m_rot, m_a, m_b] + [np.zeros(LANE, np.float32)] * 4))

    def body(p_ref, k_ref, c_ref, sa_ref, sb_ref):
        k = k_ref[...]
        ang = p_ref[...] * k[0:1]
        co, si = jnp.cos(ang), jnp.sin(ang)
        c_ref[...] = k[1:2] * co + (1.0 - k[1:2])
        sa_ref[...] = -k[2:3] * si
        sb_ref[...] = k[3:4] * si

    full = pl.BlockSpec((S, LANE), lambda: (0, 0))
    return _call(
        body, name=name,
        in_specs=[pl.BlockSpec((S, 1), lambda: (0, 0)), pl.BlockSpec((8, LANE), lambda: (0, 0))],
        out_specs=[full] * 3, out_shape=[jax.ShapeDtypeStruct((S, LANE), F32)] * 3,
    )(posf, consts)


def _rope_apply(x, tabs, *, col0, width, inverse, name, out_dtype):
    S = x.shape[0]
    tr = _pick(S, 256)
    rep = width // LANE
    cb = col0 // width

    def body(x_ref, c_ref, sa_ref, sb_ref, o_ref):
        xv = x_ref[...].astype(F32)
        c = jnp.tile(c_ref[...], (1, rep))
        sa = jnp.tile(sa_ref[...], (1, rep))
        sb = jnp.tile(sb_ref[...], (1, rep))
        if not inverse:
            out = xv * c + pltpu.roll(xv, width - 8, 1) * sa + pltpu.roll(xv, 8, 1) * sb
        else:
            out = xv * c + pltpu.roll(xv * sa, 8, 1) + pltpu.roll(xv * sb, width - 8, 1)
        o_ref[...] = out.astype(out_dtype)

    tab = pl.BlockSpec((tr, LANE), lambda i: (i, 0))
    return _call(
        body, name=name, grid=(S // tr,),
        in_specs=[pl.BlockSpec((tr, width), lambda i: (i, cb)), tab, tab, tab],
        out_specs=pl.BlockSpec((tr, width), lambda i: (i, 0)),
        out_shape=jax.ShapeDtypeStruct((S, width), out_dtype),
        compiler_params=_cparams(("parallel",)),
    )(x, *tabs)


def _swa_mask(n):
    rows = SWA_GROUPS * SWA_WINDOW
    qi = lax.broadcasted_iota(jnp.int32, (rows, 2 * SWA_WINDOW), 0) & (SWA_WINDOW - 1)
    kj = lax.broadcasted_iota(jnp.int32, (rows, 2 * SWA_WINDOW), 1)
    rel = SWA_WINDOW + qi - kj
    return (rel >= 0) & (rel < SWA_WINDOW) & ((n > 0) | (kj >= SWA_WINDOW))


def _swa_fwd(qT, kT, vT, sink_rows, *, name):
    S = qT.shape[1]
    W, G, Dh = SWA_WINDOW, SWA_GROUPS, SWA_HEAD_DIM
    nb = S // W
    scale = 1.0 / math.sqrt(Dh)

    def body(q_ref, kp_ref, kc_ref, vp_ref, vc_ref, s_ref, o_ref, l_ref):
        n = pl.program_id(1)
        q = q_ref[...].reshape(G * W, Dh)
        kk = jnp.concatenate([kp_ref[...], kc_ref[...]], axis=0)
        vv = jnp.concatenate([vp_ref[...], vc_ref[...]], axis=0)
        s = lax.dot_general(q, kk, (((1,), (1,)), ((), ())), preferred_element_type=F32) * scale
        s = jnp.where(_swa_mask(n), s, -1e30)
        sink = s_ref[...]
        m = jnp.maximum(jnp.max(s, axis=-1, keepdims=True), sink)
        e = jnp.exp(s - m)
        den = jnp.sum(e, axis=-1, keepdims=True) + jnp.exp(sink - m)
        p = e / den
        o_ref[...] = jnp.dot(p.astype(BF16), vv, preferred_element_type=F32).reshape(G, W, Dh)
        l_ref[...] = (m + jnp.log(den)).reshape(G, W, 1)

    qs = pl.BlockSpec((G, W, Dh), lambda g, n: (g, n, 0))
    prev = pl.BlockSpec((None, W, Dh), lambda g, n: (g, jnp.maximum(n - 1, 0), 0))
    cur = pl.BlockSpec((None, W, Dh), lambda g, n: (g, n, 0))
    return _call(
        body, name=name, grid=(SWA_KV_HEADS, nb),
        in_specs=[qs, prev, cur, prev, cur, pl.BlockSpec((None, G * W, 1), lambda g, n: (g, 0, 0))],
        out_specs=[qs, pl.BlockSpec((G, W, 1), lambda g, n: (g, n, 0))],
        out_shape=[jax.ShapeDtypeStruct((SWA_HEADS, S, Dh), F32), jax.ShapeDtypeStruct((SWA_HEADS, S, 1), F32)],
        compiler_params=_cparams(("parallel", "parallel")),
    )(qT, kT, kT, vT, vT, sink_rows)


def _swa_bwd(qT, kT, vT, sink_rows, oT, L, doT, *, name):
    S = qT.shape[1]
    W, G, Dh = SWA_WINDOW, SWA_GROUPS, SWA_HEAD_DIM
    nb = S // W
    scale = 1.0 / math.sqrt(Dh)

    def body(q_ref, kp_ref, kc_ref, vp_ref, vc_ref, s_ref, o_ref, l_ref, do_ref,
             dq_ref, dk_ref, dv_ref, ds_ref):
        n = pl.program_id(1)
        q = q_ref[...].reshape(G * W, Dh)
        kk = jnp.concatenate([kp_ref[...], kc_ref[...]], axis=0)
        vv = jnp.concatenate([vp_ref[...], vc_ref[...]], axis=0)
        s = lax.dot_general(q, kk, (((1,), (1,)), ((), ())), preferred_element_type=F32) * scale
        lrow = l_ref[...].reshape(G * W, 1)
        p = jnp.where(_swa_mask(n), jnp.exp(s - lrow), 0.0)
        do = do_ref[...].reshape(G * W, Dh)
        do_bf = do.astype(BF16)
        dp = lax.dot_general(do_bf, vv, (((1,), (1,)), ((), ())), preferred_element_type=F32)
        delta = jnp.sum(do * o_ref[...].reshape(G * W, Dh), axis=-1, keepdims=True)
        dsc = p * (dp - delta)
        ds_bf = dsc.astype(BF16)
        dq_ref[...] = (jnp.dot(ds_bf, kk, preferred_element_type=F32) * scale).astype(BF16).reshape(G, W, Dh)
        dkk = lax.dot_general(ds_bf, q, (((0,), (0,)), ((), ())), preferred_element_type=F32) * scale
        dvv = lax.dot_general(p.astype(BF16), do_bf, (((0,), (0,)), ((), ())), preferred_element_type=F32)
        dsk = -jnp.exp(s_ref[...] - lrow) * delta
        dsk = jnp.broadcast_to(jnp.sum(dsk.reshape(G, W, 1), axis=1), (G, LANE))

        @pl.when(n == 0)
        def _():
            dk_ref[...] = jnp.zeros_like(dk_ref)
            dv_ref[...] = jnp.zeros_like(dv_ref)
            ds_ref[...] = jnp.zeros_like(ds_ref)

        rows = pl.ds(pl.multiple_of(n * W, W), 2 * W)
        dk_ref[rows, :] += dkk
        dv_ref[rows, :] += dvv
        ds_ref[...] += dsk

    qs = pl.BlockSpec((G, W, Dh), lambda g, n: (g, n, 0))
    prev = pl.BlockSpec((None, W, Dh), lambda g, n: (g, jnp.maximum(n - 1, 0), 0))
    cur = pl.BlockSpec((None, W, Dh), lambda g, n: (g, n, 0))
    lsp = pl.BlockSpec((G, W, 1), lambda g, n: (g, n, 0))
    kvo = pl.BlockSpec((None, S + W, Dh), lambda g, n: (g, 0, 0))
    return _call(
        body, name=name, grid=(SWA_KV_HEADS, nb),
        in_specs=[qs, prev, cur, prev, cur, pl.BlockSpec((None, G * W, 1), lambda g, n: (g, 0, 0)), qs, lsp, qs],
        out_specs=[qs, kvo, kvo, pl.BlockSpec((None, G, LANE), lambda g, n: (g, 0, 0))],
        out_shape=[jax.ShapeDtypeStruct((SWA_HEADS, S, Dh), BF16),
                   jax.ShapeDtypeStruct((SWA_KV_HEADS, S + W, Dh), F32),
                   jax.ShapeDtypeStruct((SWA_KV_HEADS, S + W, Dh), F32),
                   jax.ShapeDtypeStruct((SWA_KV_HEADS, G, LANE), F32)],
        compiler_params=_cparams(("parallel", "arbitrary")),
    )(qT, kT, kT, vT, vT, sink_rows, oT, L, doT)


def _adamw(w, g, m, v, *, name, tr=128, by_cols=False):
    L, R, C = w.shape
    split = isinstance(g, (list, tuple))
    HR, HC = _half_shape(R, C, by_cols) if split else (R, C)
    tr, tc = _tile2d(HR, HC, tr)
    nr, nc = HR // tr, HC // tc
    c1 = 1.0 / (1.0 - ADAM_B1 ** ADAM_STEP)
    c2 = 1.0 / (1.0 - ADAM_B2 ** ADAM_STEP)
    ng = 2 * L if split else 1

    def body(*refs):
        w_ref, g_refs, (m_ref, v_ref, go_ref, d_ref, mo_ref, vo_ref) = refs[0], refs[1:1 + ng], refs[1 + ng:]
        if split:
            mine = pl.program_id(1) == lax.axis_index("c")
            g_ = jnp.where(mine, g_refs[0][...], g_refs[1][...])
            for l in range(1, L):
                g_ = jnp.where(pl.program_id(0) == l,
                               jnp.where(mine, g_refs[2 * l][...], g_refs[2 * l + 1][...]), g_)
        else:
            g_ = g_refs[0][...]
        mn = ADAM_B1 * m_ref[...] + (1.0 - ADAM_B1) * g_
        vn = ADAM_B2 * v_ref[...] + (1.0 - ADAM_B2) * (g_ * g_)
        go_ref[...] = g_
        mo_ref[...] = mn
        vo_ref[...] = vn
        d_ref[...] = -ADAM_LR * ((mn * c1) / (jnp.sqrt(vn * c2) + ADAM_EPS) + ADAM_WD * w_ref[...])

    def whole(l, hf, i, j):
        return (l, i, hf * nc + j) if by_cols else (l, hf * nr + i, j)

    row = pl.BlockSpec((None, tr, tc), whole)
    half = pl.BlockSpec((tr, tc), lambda l, hf, i, j: (i, j))
    gs = [h for pair in g for h in pair] if split else [g]
    return _call(
        body, name=name, grid=(L, 2 if split else 1, nr, nc),
        in_specs=[row] + [half if split else row] * ng + [row, row],
        out_specs=[row] * 4, out_shape=[jax.ShapeDtypeStruct((L, R, C), F32)] * 4,
        compiler_params=_cparams(("parallel",) * 4),
    )(w, *gs, m, v)


def _sum2_halves(g4, s4, by_cols, *, name):
    n, R, C = g4.shape
    HR, HC = _half_shape(R, C, by_cols)
    tr, tc = _tile2d(HR, HC)
    nr, nc = HR // tr, HC // tc
    core = lax.axis_index("c").astype(jnp.int32).reshape(1)

    def body(c_ref, g_ref, s_ref, o_ref):
        o_ref[...] = (g_ref[...].astype(F32) + s_ref[...].astype(F32)).astype(BF16)

    def mine(k, i, j, c):
        return (k, i, c[0] * nc + j) if by_cols else (k, c[0] * nr + i, j)

    blk = pl.BlockSpec((None, tr, tc), lambda k, i, j, c: (k, i, j))
    return _call(
        body, name=name,
        grid_spec=pltpu.PrefetchScalarGridSpec(
            num_scalar_prefetch=1, grid=(n, nr, nc),
            in_specs=[pl.BlockSpec((None, tr, tc), mine), blk], out_specs=blk),
        out_shape=jax.ShapeDtypeStruct((n, HR, HC), BF16),
        compiler_params=_cparams(("parallel", "parallel", "parallel")),
    )(core, g4, s4)


def _rowsum(parts, *, name, out_dtype=F32):
    n, R, C = parts.shape
    tr, tc = _tile2d(R, C)

    def body(p_ref, o_ref):
        acc = p_ref[0].astype(F32)
        for i in range(1, n):
            acc = acc + p_ref[i].astype(F32)
        o_ref[...] = acc.astype(out_dtype)

    return _call(
        body, name=name, grid=(R // tr, C // tc),
        in_specs=[pl.BlockSpec((n, tr, tc), lambda i, j: (0, i, j))],
        out_specs=pl.BlockSpec((tr, tc), lambda i, j: (i, j)),
        out_shape=jax.ShapeDtypeStruct((R, C), out_dtype),
        compiler_params=_cparams(("parallel", "parallel")),
    )(parts)


def _where_am_i():
    x, y, c = lax.axis_index("x"), lax.axis_index("y"), lax.axis_index("c")
    chips = [(1 - x, y), (x, 1 - y), (1 - x, 1 - y)]
    return x, y, c, chips


def _half_idx(rows, cols, by_cols, which):
    if by_cols:
        hc = cols // 2
        return (slice(None), pl.ds(pl.multiple_of(which * hc, LANE), hc))
    hr = rows // 2
    return (pl.ds(pl.multiple_of(which * hr, 16), hr), slice(None))


def _half_shape(rows, cols, by_cols):
    return (rows, cols // 2) if by_cols else (rows // 2, cols)


def _all_gather_shards(shards, by_cols, *, name):
    n = len(shards)

    def body(*refs):
        ins, outs = refs[:n], refs[n:2 * n]
        send, recv = refs[2 * n:]
        x, y, c, chips = _where_am_i()
        me = 2 * x + y
        sibling = (x, y, 1 - c)

        def half(i, which):
            return _half_idx(*shards[i].shape, by_cols[i], which)

        def cp(i, k, src, dst, to):
            return pltpu.make_async_remote_copy(src_ref=src, dst_ref=dst, send_sem=send.at[i, k],
                                                recv_sem=recv.at[i, k], device_id=to, device_id_type=MESH)

        first = []
        for i in range(n):
            for k, (px, py) in enumerate(chips):
                d = cp(i, k, ins[i].at[half(i, c)], outs[i].at[(me,) + half(i, c)], (px, py, c))
                d.start()
                first.append(d)
        passed = []
        for i in range(n):
            for k, (px, py) in enumerate(chips):
                blk = outs[i].at[(2 * px + py,) + half(i, c)]
                cp(i, k, blk, blk, (px, py, c)).wait_recv()
                d = cp(i, 3 + k, blk, blk, sibling)
                d.start()
                passed.append(d)
        for i in range(n):
            for k, (px, py) in enumerate(chips):
                blk = outs[i].at[(2 * px + py,) + half(i, 1 - c)]
                cp(i, 3 + k, blk, blk, sibling).wait_recv()
        for d in first + passed:
            d.wait_send()

    got = _call(
        body, name=name, in_specs=[ANY] * n, out_specs=[ANY] * n,
        out_shape=[jax.ShapeDtypeStruct((N_CHIPS,) + s.shape, s.dtype) for s in shards],
        scratch_shapes=[pltpu.SemaphoreType.DMA((n, 6)), pltpu.SemaphoreType.DMA((n, 6))],
    )(*shards)
    me = 2 * lax.axis_index("x") + lax.axis_index("y")
    return [lax.dynamic_update_slice_in_dim(g, s[None], me, axis=0) for g, s in zip(got, shards)]


def _sibling_send_halves(grads, by_cols, *, name):
    n = len(grads)

    def body(*refs):
        ins, outs = refs[:n], refs[n:2 * n]
        send, recv = refs[2 * n:]
        x, y, c, _ = _where_am_i()
        sibling = (x, y, 1 - c)
        cps = []
        for i in range(n):
            src = ins[i].at[(slice(None),) + _half_idx(*grads[i].shape[1:], by_cols[i], 1 - c)]
            d = pltpu.make_async_remote_copy(src_ref=src, dst_ref=outs[i], send_sem=send.at[i],
                                             recv_sem=recv.at[i], device_id=sibling, device_id_type=MESH)
            d.start()
            cps.append(d)
        for d in cps:
            d.wait()

    return _call(
        body, name=name, in_specs=[ANY] * n, out_specs=[ANY] * n,
        out_shape=[jax.ShapeDtypeStruct((N_CHIPS,) + _half_shape(*g.shape[1:], bc), g.dtype)
                   for g, bc in zip(grads, by_cols)],
        scratch_shapes=[pltpu.SemaphoreType.DMA((n,)), pltpu.SemaphoreType.DMA((n,))],
    )(*grads)


def _scatter_to_chips(parts, *, name):
    n = len(parts)

    def body(*refs):
        ins, outs = refs[:n], refs[n:2 * n]
        send, recv = refs[2 * n:]
        x, y, c, chips = _where_am_i()
        me = 2 * x + y
        cps = []
        for i in range(n):
            for k, (px, py) in enumerate(chips):
                d = pltpu.make_async_remote_copy(
                    src_ref=ins[i].at[2 * px + py], dst_ref=outs[i].at[me], send_sem=send.at[i, k],
                    recv_sem=recv.at[i, k], device_id=(px, py, c), device_id_type=MESH)
                d.start()
                cps.append((d, i, k, px, py))
        for d, i, k, px, py in cps:
            blk = outs[i].at[2 * px + py]
            pltpu.make_async_remote_copy(src_ref=blk, dst_ref=blk, send_sem=send.at[i, k], recv_sem=recv.at[i, k],
                                         device_id=(px, py, c), device_id_type=MESH).wait_recv()
        for d, *_ in cps:
            d.wait_send()

    got = _call(
        body, name=name, in_specs=[ANY] * n, out_specs=[ANY] * n,
        out_shape=[jax.ShapeDtypeStruct(p.shape, p.dtype) for p in parts],
        scratch_shapes=[pltpu.SemaphoreType.DMA((n, 3)), pltpu.SemaphoreType.DMA((n, 3))],
    )(*parts)
    me = 2 * lax.axis_index("x") + lax.axis_index("y")
    return [lax.dynamic_update_slice_in_dim(g, lax.dynamic_slice_in_dim(p, me, 1, axis=0), me, axis=0)
            for g, p in zip(got, parts)]


def _sibling_join_halves(halves, *, name):
    n = len(halves)

    def body(*refs):
        ins, outs = refs[:n], refs[n:2 * n]
        send, recv = refs[2 * n:]
        x, y, c, _ = _where_am_i()
        sibling = (x, y, 1 - c)
        cps = []
        for i in range(n):
            d = pltpu.make_async_remote_copy(src_ref=ins[i], dst_ref=outs[i], send_sem=send.at[i],
                                             recv_sem=recv.at[i], device_id=sibling, device_id_type=MESH)
            d.start()
            cps.append(d)
        for d in cps:
            d.wait()

    return _call(
        body, name=name, in_specs=[ANY] * n, out_specs=[ANY] * n,
        out_shape=[jax.ShapeDtypeStruct(h.shape, h.dtype) for h in halves],
        scratch_shapes=[pltpu.SemaphoreType.DMA((n,)), pltpu.SemaphoreType.DMA((n,))],
    )(*halves)


def _all_reduce_small(v, *, name):
    R, C = v.shape

    def body(v_ref, o_ref, sib, slots, send, recv):
        x, y, c, chips = _where_am_i()
        me = 2 * x + y
        sibling = (x, y, 1 - c)
        d = pltpu.make_async_remote_copy(src_ref=v_ref, dst_ref=sib, send_sem=send.at[0], recv_sem=recv.at[0],
                                         device_id=sibling, device_id_type=MESH)
        d.start()
        d.wait()
        slots[me] = v_ref[...] + sib[...]
        cps = []
        for k, (px, py) in enumerate(chips):
            d = pltpu.make_async_remote_copy(src_ref=slots.at[me], dst_ref=slots.at[me], send_sem=send.at[1 + k],
                                             recv_sem=recv.at[1 + k], device_id=(px, py, c), device_id_type=MESH)
            d.start()
            cps.append(d)
        for k, (px, py) in enumerate(chips):
            blk = slots.at[2 * px + py]
            pltpu.make_async_remote_copy(src_ref=blk, dst_ref=blk, send_sem=send.at[1 + k], recv_sem=recv.at[1 + k],
                                         device_id=(px, py, c), device_id_type=MESH).wait_recv()
        for d in cps:
            d.wait_send()
        o_ref[...] = (slots[0] + slots[1]) + (slots[2] + slots[3])

    vm = pl.BlockSpec(memory_space=pltpu.VMEM)
    return _call(
        body, name=name, in_specs=[vm], out_specs=vm,
        out_shape=jax.ShapeDtypeStruct((R, C), F32),
        scratch_shapes=[pltpu.VMEM((R, C), F32), pltpu.VMEM((N_CHIPS, R, C), F32),
                        pltpu.SemaphoreType.DMA((4,)), pltpu.SemaphoreType.DMA((4,))],
        compiler_params=pltpu.CompilerParams(vmem_limit_bytes=VMEM_LIMIT),
    )(v)


def _cols_from_shards(g):
    return jnp.transpose(g, (1, 0, 2)).reshape(g.shape[1], -1)


def _shards_from_cols(w):
    R, C4 = w.shape
    return jnp.transpose(w.reshape(R, N_CHIPS, C4 // N_CHIPS), (1, 0, 2))


def _block_diag(t):
    G, a, b = t.shape
    eye = jnp.eye(G, dtype=t.dtype)
    return (t[:, :, None, :] * eye[:, None, :, None]).reshape(G * a, G * b)


def _diag_blocks(xm, G):
    a, b = xm.shape[0] // G, xm.shape[1] // G
    idx = jnp.arange(G)
    return xm.reshape(G, a, G, b)[idx, :, idx, :]


def _pack(arrs):
    flat = []
    for a in arrs:
        f = a.reshape(-1).astype(F32)
        flat.append(jnp.pad(f, (0, _rup(f.shape[0], LANE) - f.shape[0])))
    v = jnp.concatenate(flat)
    rows = _rup(v.shape[0] // LANE, 8)
    v = jnp.pad(v, (0, rows * LANE - v.shape[0]))
    return v.reshape(rows, LANE)


def _unpack(v, shapes):
    flat = v.reshape(-1)
    out, off = [], 0
    for s in shapes:
        n = int(np.prod(s))
        out.append(flat[off:off + n].reshape(s))
        off += _rup(n, LANE)
    return out


def _ffn_fwd(x, Wup, Wdn, cw, cb, tag):
    h = _mm(x, Wup, 'nt', bmode='bo', tm=512, tn=4096, name=f"ffn_up_{tag}")
    a = _act_fwd(h, cw, cb, name=f"ffn_act_{tag}")
    f = _mm(a, Wdn, 'nn', bmode='abr', tm=512, tn=1024, tk=4096, name=f"ffn_down_{tag}")
    return f, h, a


def _ffn_bwd(df, x, h, a, Wup, Wdn, cw, cb, tag):
    da = _mm(df, Wdn, 'nt', bmode='bo', tm=512, tn=4096, name=f"ffn_da_{tag}")
    dWdn = _mm(a, df, 'tn', bmode='ao', tm=4096, tn=512, name=f"ffn_dwdn_{tag}", out_dtype=BF16)
    dh, dcw, dcb = _act_bwd(h, da, cw, cb, name=f"ffn_actb_{tag}")
    dx = _mm(dh, Wup, 'nn', bmode='abr', tm=512, tn=1024, tk=4096, name=f"ffn_dx_{tag}")
    dWup = _mm(dh, x, 'tn', bmode='ao', tm=4096, tn=512, name=f"ffn_dwup_{tag}", out_dtype=BF16)
    return dx, dWup, dWdn, dcw, dcb


def kernel(x, positions, ev_w_in, ev_b_f, ev_lambda_re, ev_lambda_im, ev_log_step, ev_ssm_b_re, ev_ssm_b_im, ev_ssm_c_re, ev_ssm_c_im, ev_ssm_d, ev_w_glu, ev_w_out, od_w_in, od_sinks, od_w_out, ln_mix_g, ln_mix_b, ffn_w_up, ffn_conv_w, ffn_conv_b, ffn_w_down, ln_ffn_g, ln_ffn_b, loss_target, m_ev_w_in, m_ev_b_f, m_ev_lambda_re, m_ev_lambda_im, m_ev_log_step, m_ev_ssm_b_re, m_ev_ssm_b_im, m_ev_ssm_c_re, m_ev_ssm_c_im, m_ev_ssm_d, m_ev_w_glu, m_ev_w_out, m_od_w_in, m_od_sinks, m_od_w_out, m_ln_mix_g, m_ln_mix_b, m_ffn_w_up, m_ffn_conv_w, m_ffn_conv_b, m_ffn_w_down, m_ln_ffn_g, m_ln_ffn_b, v_ev_w_in, v_ev_b_f, v_ev_lambda_re, v_ev_lambda_im, v_ev_log_step, v_ev_ssm_b_re, v_ev_ssm_b_im, v_ev_ssm_c_re, v_ev_ssm_c_im, v_ev_ssm_d, v_ev_w_glu, v_ev_w_out, v_od_w_in, v_od_sinks, v_od_w_out, v_ln_mix_g, v_ln_mix_b, v_ffn_w_up, v_ffn_conv_w, v_ffn_conv_b, v_ffn_w_down, v_ln_ffn_g, v_ln_ffn_b):
    W = dict(ev_w_in=ev_w_in, ev_b_f=ev_b_f, ev_lambda_re=ev_lambda_re, ev_lambda_im=ev_lambda_im, ev_log_step=ev_log_step, ev_ssm_b_re=ev_ssm_b_re, ev_ssm_b_im=ev_ssm_b_im, ev_ssm_c_re=ev_ssm_c_re, ev_ssm_c_im=ev_ssm_c_im, ev_ssm_d=ev_ssm_d, ev_w_glu=ev_w_glu, ev_w_out=ev_w_out, od_w_in=od_w_in, od_sinks=od_sinks, od_w_out=od_w_out, ln_mix_g=ln_mix_g, ln_mix_b=ln_mix_b, ffn_w_up=ffn_w_up, ffn_conv_w=ffn_conv_w, ffn_conv_b=ffn_conv_b, ffn_w_down=ffn_w_down, ln_ffn_g=ln_ffn_g, ln_ffn_b=ln_ffn_b)
    Mo = dict(ev_w_in=m_ev_w_in, ev_b_f=m_ev_b_f, ev_lambda_re=m_ev_lambda_re, ev_lambda_im=m_ev_lambda_im, ev_log_step=m_ev_log_step, ev_ssm_b_re=m_ev_ssm_b_re, ev_ssm_b_im=m_ev_ssm_b_im, ev_ssm_c_re=m_ev_ssm_c_re, ev_ssm_c_im=m_ev_ssm_c_im, ev_ssm_d=m_ev_ssm_d, ev_w_glu=m_ev_w_glu, ev_w_out=m_ev_w_out, od_w_in=m_od_w_in, od_sinks=m_od_sinks, od_w_out=m_od_w_out, ln_mix_g=m_ln_mix_g, ln_mix_b=m_ln_mix_b, ffn_w_up=m_ffn_w_up, ffn_conv_w=m_ffn_conv_w, ffn_conv_b=m_ffn_conv_b, ffn_w_down=m_ffn_w_down, ln_ffn_g=m_ln_ffn_g, ln_ffn_b=m_ln_ffn_b)
    Vo = dict(ev_w_in=v_ev_w_in, ev_b_f=v_ev_b_f, ev_lambda_re=v_ev_lambda_re, ev_lambda_im=v_ev_lambda_im, ev_log_step=v_ev_log_step, ev_ssm_b_re=v_ev_ssm_b_re, ev_ssm_b_im=v_ev_ssm_b_im, ev_ssm_c_re=v_ev_ssm_c_re, ev_ssm_c_im=v_ev_ssm_c_im, ev_ssm_d=v_ev_ssm_d, ev_w_glu=v_ev_w_glu, ev_w_out=v_ev_w_out, od_w_in=v_od_w_in, od_sinks=v_od_sinks, od_w_out=v_od_w_out, ln_mix_g=v_ln_mix_g, ln_mix_b=v_ln_mix_b, ffn_w_up=v_ffn_w_up, ffn_conv_w=v_ffn_conv_w, ffn_conv_b=v_ffn_conv_b, ffn_w_down=v_ffn_w_down, ln_ffn_g=v_ln_ffn_g, ln_ffn_b=v_ln_ffn_b)
    names = list(W.keys())
    big = ['ev_w_in', 'ev_w_glu', 'ev_w_out', 'od_w_in', 'od_w_out', 'ffn_w_up', 'ffn_w_down']

    S, D = x.shape[1], x.shape[2]
    x0 = x.reshape(S, D)
    tgt = loss_target.reshape(S, D)
    G, Pn, Cg = SSM_GROUPS, SSM_STATE, SSM_GROUP
    Fs = ffn_w_up.shape[2]
    FP = Fs
    Rd = ffn_w_down.shape[1]
    EIN = N_CHIPS * ev_w_in.shape[2]

    def as2d(a):
        return a.reshape(-1, a.shape[-1])

    cwl = ffn_conv_w.reshape(-1)
    cw_rows = _rup(_rup(cwl.shape[0], LANE) // LANE, 32)
    cw_pad = jnp.pad(cwl, (0, cw_rows * LANE - cwl.shape[0])).reshape(cw_rows, LANE)
    transposed = ('ev_w_in', 'ffn_w_up')

    def view(n, a):
        return jnp.transpose(a, (0, 2, 1)) if n in transposed else a

    Wv = {n: view(n, W[n]) for n in big}
    big_e = [(n, l) for n in big for l in range(W[n].shape[0])]
    by_cols = [(Wv[n].shape[1] // 2) % 16 != 0 for n, l in big_e]
    gathered = _all_gather_shards([Wv[n][l].astype(BF16) for n, l in big_e] + [cw_pad], by_cols + [False],
                                  name="ag_weights")
    gw = dict(zip(big_e, gathered[:-1]))
    gw.update({n: gw[(n, 0)] for n in big if W[n].shape[0] == 1})
    w_in_t = gw['ev_w_in'].reshape(EIN, D)
    qkv_w = 3 * FOX_WIDTH
    WmainT = jnp.concatenate([w_in_t[:qkv_w], w_in_t[qkv_w + FOX_HEADS:]], axis=0)
    WfT = jnp.pad(w_in_t[qkv_w:qkv_w + FOX_HEADS], ((0, LANE - FOX_HEADS), (0, 0)))
    Wglu = _cols_from_shards(gw['ev_w_glu'])
    Wout_ev = gw['ev_w_out'].reshape(D, D)
    Wodin = _cols_from_shards(gw['od_w_in'])
    Wodout = gw['od_w_out'].reshape(D, D)
    Wup, Wdn, cws, cbs = [], [], [], []
    cw_all = gathered[-1].reshape(N_CHIPS, -1)[:, :cwl.shape[0]].reshape(N_CHIPS, DEPTH, 3, Fs)
    for l in range(DEPTH):
        Wup.append(gw[('ffn_w_up', l)])
        Wdn.append(gw[('ffn_w_down', l)].reshape(2, Fs, D))
        cws.append(cw_all[:, l])
        cbs.append(ffn_conv_b[l].reshape(N_CHIPS, Fs))

    lam_r, lam_i = ev_lambda_re[0], ev_lambda_im[0]
    lstep = ev_log_step[0].reshape(G, 1)
    a_re, a_im, g_re, g_im = _s5_disc_fwd(lam_r, lam_i, lstep, name="s5_disc")
    b_re2, b_im2 = ev_ssm_b_re[0].reshape(G * Pn, Cg), ev_ssm_b_im[0].reshape(G * Pn, Cg)
    g_re1, g_im1 = g_re.reshape(G * Pn, 1), g_im.reshape(G * Pn, 1)
    bb_re, bb_im = _s5_bb_fwd(g_re1, g_im1, b_re2, b_im2, name="s5_bb")
    bbt = jnp.stack([jnp.transpose(b.reshape(G, Pn, Cg), (0, 2, 1)).reshape(G * Cg, Pn) for b in (bb_re, bb_im)])
    BB = _diag_expand(bbt, Cg, Pn, name="s5_bb_dense")
    cct = jnp.stack([jnp.transpose(ev_ssm_c_re[0], (0, 2, 1)).reshape(G * Pn, Cg),
                     jnp.transpose(-ev_ssm_c_im[0], (0, 2, 1)).reshape(G * Pn, Cg)])
    CC = _diag_expand(cct, Pn, Cg, name="s5_cc_dense")
    a_cat = jnp.stack([a_re.reshape(1, G * Pn), a_im.reshape(1, G * Pn)])
    dskip = ev_ssm_d[0].reshape(1, SSM_WIDTH)

    P = _mm(x0, WmainT, 'nt', name="ev_proj")
    fl = _mm(x0, WfT, 'nt', name="ev_proj_f")
    bf_pad = jnp.pad(ev_b_f.reshape(1, FOX_HEADS), ((0, 0), (0, LANE - FOX_HEADS)))
    cgate, sgate = _gate_fwd(fl, bf_pad, name="fox_gate")
    ccol = jnp.transpose(cgate[:, :FOX_HEADS]).reshape(FOX_HEADS, S, 1)
    crow = jnp.transpose(cgate[:, :FOX_HEADS]).reshape(FOX_HEADS, 1, S)
    fox, lse = _fox_fwd(P, ccol, crow, name="fox_fwd")
    u_s5 = P[:, qkv_w:]
    bu = _mm(u_s5, BB, 'nn', bmode='bo', name="s5_bu")
    hh = _s5_scan_fwd(bu, a_cat, name="s5_scan")
    yc = _mm(hh, CC, 'nn', bmode='abr', name="s5_y")
    y_s5, yg = _s5_out_fwd(yc, P, dskip, name="s5_out")
    z = _mm(yg, Wglu, 'nn', name="s5_glu_proj")
    ssm = _glu_fwd(z, name="s5_glu")
    cat = jnp.concatenate([fox.astype(BF16), ssm], axis=1)
    mix0 = _mm(cat, Wout_ev, 'nn', name="ev_out")
    x1, xh1, rs1 = _add_ln_fwd(x0, mix0, ln_mix_g[0], ln_mix_b[0], name="ln_mix0")
    f0, hf0, af0 = _ffn_fwd(x1, Wup[0], Wdn[0], cws[0], cbs[0], "l0")
    x2, xh2, rs2 = _add_ln_fwd(x1, f0, ln_ffn_g[0], ln_ffn_b[0], name="ln_ffn0")

    QW, KW = SWA_HEADS * SWA_HEAD_DIM, SWA_KV_HEADS * SWA_HEAD_DIM
    P1 = _mm(x2, Wodin, 'nn', name="od_proj")
    tabs = _rope_tables(positions.reshape(S, 1).astype(F32), name="rope_tables")
    qr = _rope_apply(P1, tabs, col0=0, width=QW, inverse=False, name="rope_q", out_dtype=BF16)
    kr = _rope_apply(P1, tabs, col0=QW, width=KW, inverse=False, name="rope_k", out_dtype=BF16)

    def heads(a2, nh):
        return jnp.transpose(a2.reshape(S, nh, SWA_HEAD_DIM), (1, 0, 2))

    def unheads(a3):
        return jnp.transpose(a3, (1, 0, 2)).reshape(S, -1)

    qT, kT = heads(qr, SWA_HEADS), heads(kr, SWA_KV_HEADS)
    vT = heads(P1[:, QW + KW:].astype(BF16), SWA_KV_HEADS)
    sink_rows = jnp.broadcast_to(od_sinks[0].reshape(SWA_KV_HEADS, SWA_GROUPS, 1, 1),
                                 (SWA_KV_HEADS, SWA_GROUPS, SWA_WINDOW, 1)).reshape(SWA_KV_HEADS, -1, 1)
    oT, Lsw = _swa_fwd(qT, kT, vT, sink_rows, name="swa_fwd")
    o_sw = unheads(oT).astype(BF16)
    mix1 = _mm(o_sw, Wodout, 'nn', name="od_out")
    x3, xh3, rs3 = _add_ln_fwd(x2, mix1, ln_mix_g[1], ln_mix_b[1], name="ln_mix1")
    f1, hf1, af1 = _ffn_fwd(x3, Wup[1], Wdn[1], cws[1], cbs[1], "l1")
    x4, xh4, rs4 = _add_ln_fwd(x3, f1, ln_ffn_g[1], ln_ffn_b[1], name="ln_ffn1")
    dy, loss_part = _loss_grad(x4, tgt, name="loss")

    dz4, dg_ffn1, db_ffn1 = _ln_bwd(dy, None, xh4, rs4, ln_ffn_g[1], name="lnb_ffn1")
    dx3f, dWup1, dWdn1, dcw1, dcb1 = _ffn_bwd(dz4, x3, hf1, af1, Wup[1], Wdn[1], cws[1], cbs[1], "l1")
    dz3, dg_mix1, db_mix1 = _ln_bwd(dz4, dx3f, xh3, rs3, ln_mix_g[1], name="lnb_mix1")
    do_sw = _mm(dz3, Wodout, 'nt', name="od_out_dx")
    dWodout = _mm(o_sw, dz3, 'tn', name="od_out_dw", out_dtype=BF16)
    doT = heads(do_sw, SWA_HEADS)
    dqT, dkT, dvT, dsink = _swa_bwd(qT, kT, vT, sink_rows, oT, Lsw, doT, name="swa_bwd")
    dq1 = _rope_apply(unheads(dqT), tabs, col0=0, width=QW, inverse=True, name="rope_dq", out_dtype=BF16)
    dk1 = _rope_apply(unheads(dkT[:, SWA_WINDOW:]), tabs, col0=0, width=KW, inverse=True, name="rope_dk",
                      out_dtype=BF16)
    dP1 = jnp.concatenate([dq1, dk1, unheads(dvT[:, SWA_WINDOW:]).astype(BF16)], axis=1)
    dx2m = _mm(dP1, Wodin, 'nt', name="od_proj_dx")
    dWodin = _mm(x2, dP1, 'tn', name="od_proj_dw", out_dtype=BF16)

    dz2, dg_ffn0, db_ffn0 = _ln_bwd(dz3, dx2m, xh2, rs2, ln_ffn_g[0], name="lnb_ffn0")
    dx1f, dWup0, dWdn0, dcw0, dcb0 = _ffn_bwd(dz2, x1, hf0, af0, Wup[0], Wdn[0], cws[0], cbs[0], "l0")
    dz1, dg_mix0, db_mix0 = _ln_bwd(dz2, dx1f, xh1, rs1, ln_mix_g[0], name="lnb_mix0")
    dcat = _mm(dz1, Wout_ev, 'nt', name="ev_out_dx")
    dWout_ev = _mm(cat, dz1, 'tn', name="ev_out_dw", out_dtype=BF16)
    dz = _glu_bwd(z, dcat, name="s5_glu_bwd")
    dyg = _mm(dz, Wglu, 'nt', name="s5_glu_dx")
    dWglu = _mm(yg, dz, 'tn', name="s5_glu_dw", out_dtype=BF16)
    dy_s5, du_dir, dD = _s5_out_bwd(dyg, y_s5, P, dskip, name="s5_out_bwd")
    dhh = _mm(dy_s5, CC, 'nt', bmode='bo', name="s5_y_dx")
    dCC = _mm(hh, dy_s5, 'tn', bmode='ao', name="s5_y_dw")
    lam, da_s5 = _s5_scan_bwd(dhh, hh, a_cat, name="s5_scan_bwd")
    du_bu = _mm(lam, BB, 'nt', bmode='abr', name="s5_bu_dx")
    dBB = _mm(u_s5, lam, 'tn', bmode='bo', name="s5_bu_dw")
    du = _combine([du_dir, du_bu], [1.0, 1.0], name="s5_du", out_dtype=BF16)
    dq0, dk0, dv0, dccol, dcrow = _fox_bwd(P, ccol, crow, fox, lse, dcat, name="fox_bwd")
    dc = jnp.transpose((dccol.reshape(FOX_HEADS, S) - dcrow.reshape(FOX_HEADS, S)))
    dc = jnp.pad(dc, ((0, 0), (0, LANE - FOX_HEADS)))
    dfl, dbf = _gate_bwd(dc, sgate, name="fox_gate_bwd")
    dP = jnp.concatenate([dq0, dk0, dv0, du], axis=1)
    dx0a = _mm(dP, WmainT, 'nn', name="ev_proj_dx")
    dx0b = _mm(dfl, WfT, 'nn', name="ev_proj_f_dx")
    dWmainT = _mm(dP, x0, 'tn', tm=1024, tn=1024, name="ev_proj_dw", out_dtype=BF16)
    dWfT = _mm(dfl, x0, 'tn', name="ev_proj_f_dw", out_dtype=BF16)
    grad_x = _combine([dz1, dx0a, dx0b], [ALPHA, 1.0, 1.0], name="grad_x")

    dbbt = _diag_extract(dBB, Cg, Pn, name="s5_bb_diag")
    dcct = _diag_extract(dCC, Pn, Cg, name="s5_cc_diag")
    dbb_re = jnp.transpose(dbbt[0].reshape(G, Cg, Pn), (0, 2, 1)).reshape(G * Pn, Cg)
    dbb_im = jnp.transpose(dbbt[1].reshape(G, Cg, Pn), (0, 2, 1)).reshape(G * Pn, Cg)
    db_re, db_im, dg_re1, dg_im1 = _s5_bb_bwd(g_re1, g_im1, b_re2, b_im2, dbb_re, dbb_im, name="s5_bb_bwd")
    dlam_re, dlam_im, dlstep = _s5_disc_bwd(lam_r, lam_i, lstep, da_s5[0].reshape(G, Pn), da_s5[1].reshape(G, Pn),
                                            dg_re1.reshape(G, Pn), dg_im1.reshape(G, Pn), name="s5_disc_bwd")
    dc_re = jnp.transpose(dcct[0].reshape(G, Pn, Cg), (0, 2, 1))
    dc_im = -jnp.transpose(dcct[1].reshape(G, Pn, Cg), (0, 2, 1))

    def conv_w_full(d0, d1):
        return jnp.stack([jnp.reshape(jnp.transpose(d[:, :, :Fs], (1, 0, 2)), (3, N_CHIPS * Fs)) for d in (d0, d1)])

    def conv_b_full(d0, d1):
        return jnp.stack([jnp.reshape(d[:, 0, :Fs], (N_CHIPS * Fs,)) for d in (d0, d1)])

    small_local = dict(
        ev_b_f=dbf[:, :FOX_HEADS], ev_lambda_re=dlam_re, ev_lambda_im=dlam_im, ev_log_step=dlstep,
        ev_ssm_b_re=db_re, ev_ssm_b_im=db_im, ev_ssm_c_re=dc_re, ev_ssm_c_im=dc_im, ev_ssm_d=dD,
        od_sinks=dsink[:, :, 0],
        ln_mix_g=jnp.concatenate([dg_mix0, dg_mix1]), ln_mix_b=jnp.concatenate([db_mix0, db_mix1]),
        ffn_conv_w=conv_w_full(dcw0, dcw1), ffn_conv_b=conv_b_full(dcb0, dcb1),
        ln_ffn_g=jnp.concatenate([dg_ffn0, dg_ffn1]), ln_ffn_b=jnp.concatenate([db_ffn0, db_ffn1]))
    small = list(small_local.keys())
    red = _all_reduce_small(_pack([small_local[n] for n in small] + [loss_part]), name="ar_small")
    full_shapes = [W[n].shape if n != 'ffn_conv_w' else (DEPTH, 3, N_CHIPS * Fs) for n in small]
    pieces = _unpack(red, full_shapes + [()])
    loss = pieces[-1]
    gsmall = dict(zip(small, pieces[:-1]))
    chip = 2 * lax.axis_index("x") + lax.axis_index("y")
    gsmall['ffn_conv_w'] = lax.dynamic_slice_in_dim(gsmall['ffn_conv_w'], chip * Fs, Fs, axis=2)
    shapes = [W[n].shape for n in small]
    gs, ds_, ms, vs = _adamw(_pack([W[n] for n in small])[None], _pack([gsmall[n] for n in small])[None],
                             _pack([Mo[n] for n in small])[None], _pack([Vo[n] for n in small])[None],
                             name="adamw_small", tr=1 << 14)
    out_g = dict(zip(small, _unpack(gs, shapes)))
    out_d = dict(zip(small, _unpack(ds_, shapes)))
    out_m = dict(zip(small, _unpack(ms, shapes)))
    out_v = dict(zip(small, _unpack(vs, shapes)))

    dw_in_t = jnp.concatenate([dWmainT[:qkv_w], dWfT[:FOX_HEADS], dWmainT[qkv_w:]], axis=0)
    gl = {
        ('ev_w_in', 0): dw_in_t.reshape(N_CHIPS, EIN // N_CHIPS, D), ('ev_w_glu', 0): _shards_from_cols(dWglu),
        ('ev_w_out', 0): dWout_ev.reshape(N_CHIPS, D // N_CHIPS, D), ('od_w_in', 0): _shards_from_cols(dWodin),
        ('od_w_out', 0): dWodout.reshape(N_CHIPS, D // N_CHIPS, D),
        ('ffn_w_up', 0): dWup0, ('ffn_w_up', 1): dWup1,
        ('ffn_w_down', 0): dWdn0.reshape(N_CHIPS, Rd, D), ('ffn_w_down', 1): dWdn1.reshape(N_CHIPS, Rd, D)}
    glist = [gl[e] for e in big_e]
    sib = _sibling_send_halves(glist, by_cols, name="rs_sibling")
    part = [_sum2_halves(g4, s4, bc, name=f"rs_sum2_{n}{l}")
            for (n, l), g4, s4, bc in zip(big_e, glist, sib, by_cols)]
    recv = _scatter_to_chips(part, name="rs_chips")
    halves = [_rowsum(r, name=f"rs_sum4_{n}{l}") for (n, l), r in zip(big_e, recv)]
    others = _sibling_join_halves(halves, name="rs_join")
    pairs = dict(zip(big_e, zip(halves, others)))
    split_cols = dict(zip(big_e, by_cols))
    for n in big:
        res = _adamw(Wv[n], [pairs[(n, l)] for l in range(W[n].shape[0])], view(n, Mo[n]), view(n, Vo[n]),
                     name=f"adamw_{n}", by_cols=split_cols[(n, 0)])
        out_g[n], out_d[n], out_m[n], out_v[n] = (view(n, t) for t in res)

    return (loss, grad_x.reshape(1, S, D), *[out_g[n] for n in names], *[out_d[n] for n in names],
            *[out_m[n] for n in names], *[out_v[n] for n in names])
```

```python
import functools
import math

import numpy as np
import jax
import jax.numpy as jnp
from jax import lax
from jax.experimental import pallas as pl
from jax.experimental.pallas import tpu as pltpu

F32 = jnp.float32
BF16 = jnp.bfloat16
MESH = pl.DeviceIdType.MESH
ANY = pl.BlockSpec(memory_space=pl.ANY)

D_MODEL = 2048
FOX_HEADS = 8
FOX_HEAD_DIM = 128
FOX_WIDTH = 1024
SSM_WIDTH = 1024
SSM_GROUP = 16
SSM_GROUPS = 64
SSM_STATE = 64
SWA_HEADS = 32
SWA_KV_HEADS = 4
SWA_HEAD_DIM = 64
SWA_GROUPS = 8
SWA_WINDOW = 128
ROPE_DIM = 16
ROPE_THETA = 500000.0
LN_EPS = 1e-5
DEPTH = 2
ALPHA = (2.0 * DEPTH) ** 0.25
ADAM_LR = 0.001
ADAM_B1 = 0.9
ADAM_B2 = 0.999
ADAM_EPS = 1e-08
ADAM_WD = 0.01
ADAM_STEP = 10
N_CHIPS = 4

VMEM_LIMIT = 56 * 1024 * 1024
LANE = 128


def _call(body, **kw):
    return pl.pallas_call(body, **kw)


def _cparams(sem):
    return pltpu.CompilerParams(dimension_semantics=sem, vmem_limit_bytes=VMEM_LIMIT)


def _rup(n, m):
    return (n + m - 1) // m * m


def _pick(n, pref):
    if n <= pref:
        return n
    for step in (128, 16, 8):
        for t in range(pref - pref % step, 0, -step):
            if n % t == 0:
                return t
    return n


def _tile2d(rows, cols, pref_rows=256, budget=256 * 1024):
    tr = _pick(rows, pref_rows)
    if tr < 64:
        tr = rows
    if cols % LANE:
        return tr, cols
    return tr, _pick(cols, max(LANE, budget // tr // LANE * LANE))


def _mm(a, b, mode, *, name, tm=512, tn=1024, tk=2048, bmode=None, out_dtype=F32):
    a3 = a if a.ndim == 3 else a[None]
    b3 = b if b.ndim == 3 else b[None]
    if mode == 'tn':
        K, M = a3.shape[1:]
    else:
        M, K = a3.shape[1:]
    N = b3.shape[1] if mode == 'nt' else b3.shape[2]
    tm, tn, tk = _pick(M, tm), _pick(N, tn), _pick(K, tk)
    nb = max(a3.shape[0], b3.shape[0])
    nbo, nbr = (1, nb) if bmode == 'abr' else (nb, 1)
    nk = K // tk
    nred = nbr * nk
    a_b = bmode in ('ao', 'abr')
    b_b = bmode in ('bo', 'abr')
    o_b = bmode in ('bo', 'ao')

    def bsel(flag, bo, br):
        return (bo + br) if flag else 0

    if mode == 'tn':
        a_spec = pl.BlockSpec((None, tk, tm), lambda bo, i, j, br, k: (bsel(a_b, bo, br), k, i))
    else:
        a_spec = pl.BlockSpec((None, tm, tk), lambda bo, i, j, br, k: (bsel(a_b, bo, br), i, k))
    if mode == 'nt':
        b_spec = pl.BlockSpec((None, tn, tk), lambda bo, i, j, br, k: (bsel(b_b, bo, br), j, k))
    else:
        b_spec = pl.BlockSpec((None, tk, tn), lambda bo, i, j, br, k: (bsel(b_b, bo, br), k, j))
    o_spec = pl.BlockSpec((None, tm, tn), lambda bo, i, j, br, k: (bsel(o_b, bo, br), i, j))
    dn = {'nn': (((1,), (0,)), ((), ())), 'nt': (((1,), (1,)), ((), ())), 'tn': (((0,), (0,)), ((), ()))}[mode]

    def body(a_ref, b_ref, o_ref, *scratch):
        r = lax.dot_general(a_ref[...].astype(BF16), b_ref[...].astype(BF16), dn, preferred_element_type=F32)
        if nred == 1:
            o_ref[...] = r.astype(out_dtype)
        else:
            acc = scratch[0]
            step = pl.program_id(3) * nk + pl.program_id(4)

            @pl.when(step == 0)
            def _():
                acc[...] = r

            @pl.when(step > 0)
            def _():
                acc[...] += r

            @pl.when(step == nred - 1)
            def _():
                o_ref[...] = acc[...].astype(out_dtype)

    out = _call(
        body, name=name,
        grid=(nbo, M // tm, N // tn, nbr, nk),
        in_specs=[a_spec, b_spec], out_specs=o_spec,
        out_shape=jax.ShapeDtypeStruct((nbo if o_b else 1, M, N), out_dtype),
        scratch_shapes=[] if nred == 1 else [pltpu.VMEM((tm, tn), F32)],
        compiler_params=_cparams(("parallel", "parallel", "parallel", "arbitrary", "arbitrary")),
    )(a3, b3)
    return out if o_b else out[0]


def _add_ln_fwd(x, r, g, b, *, name):
    S, D = x.shape
    tr = _pick(S, 256)

    def body(x_ref, r_ref, g_ref, b_ref, o_ref, xh_ref, rs_ref):
        z = ALPHA * x_ref[...] + r_ref[...]
        mu = jnp.mean(z, axis=-1, keepdims=True)
        zc = z - mu
        var = jnp.mean(zc * zc, axis=-1, keepdims=True)
        rstd = lax.rsqrt(var + LN_EPS)
        xh = zc * rstd
        xh_ref[...] = xh
        rs_ref[...] = rstd
        o_ref[...] = xh * g_ref[...] + b_ref[...]

    row = pl.BlockSpec((tr, D), lambda i: (i, 0))
    vec = pl.BlockSpec((1, D), lambda i: (0, 0))
    return _call(
        body, name=name, grid=(S // tr,),
        in_specs=[row, row, vec, vec],
        out_specs=[row, row, pl.BlockSpec((tr, 1), lambda i: (i, 0))],
        out_shape=[jax.ShapeDtypeStruct((S, D), F32), jax.ShapeDtypeStruct((S, D), F32),
                   jax.ShapeDtypeStruct((S, 1), F32)],
        compiler_params=_cparams(("parallel",)),
    )(x, r, g.reshape(1, D), b.reshape(1, D))


def _ln_bwd(da, db, xhat, rstd, g, *, name):
    S, D = xhat.shape
    tr = _pick(S, 256)
    two = db is not None

    def body(*refs):
        if two:
            da_ref, db_ref, xh_ref, rs_ref, g_ref, dz_ref, dg_ref, dbt_ref = refs
            dy = ALPHA * da_ref[...] + db_ref[...]
        else:
            da_ref, xh_ref, rs_ref, g_ref, dz_ref, dg_ref, dbt_ref = refs
            dy = da_ref[...]
        xh = xh_ref[...]
        dxh = dy * g_ref[...]
        m1 = jnp.mean(dxh, axis=-1, keepdims=True)
        m2 = jnp.mean(dxh * xh, axis=-1, keepdims=True)
        dz_ref[...] = rs_ref[...] * (dxh - m1 - xh * m2)
        pg = jnp.sum(dy * xh, axis=0, keepdims=True)
        pb = jnp.sum(dy, axis=0, keepdims=True)

        @pl.when(pl.program_id(0) == 0)
        def _():
            dg_ref[...] = pg
            dbt_ref[...] = pb

        @pl.when(pl.program_id(0) > 0)
        def _():
            dg_ref[...] += pg
            dbt_ref[...] += pb

    row = pl.BlockSpec((tr, D), lambda i: (i, 0))
    vec = pl.BlockSpec((1, D), lambda i: (0, 0))
    ins = [da] + ([db] if two else []) + [xhat, rstd, g.reshape(1, D)]
    in_specs = [row] + ([row] if two else []) + [row, pl.BlockSpec((tr, 1), lambda i: (i, 0)), vec]
    return _call(
        body, name=name, grid=(S // tr,),
        in_specs=in_specs, out_specs=[row, vec, vec],
        out_shape=[jax.ShapeDtypeStruct((S, D), F32), jax.ShapeDtypeStruct((1, D), F32),
                   jax.ShapeDtypeStruct((1, D), F32)],
        compiler_params=_cparams(("arbitrary",)),
    )(*ins)


def _loss_grad(y, t, *, name):
    S, D = y.shape
    tr = _pick(S, 256)

    def body(y_ref, t_ref, dy_ref, l_ref):
        e = y_ref[...] - t_ref[...]
        dy_ref[...] = e * (1.0 / D)
        part = 0.5 * jnp.sum(jnp.sum(e * e, axis=-1, keepdims=True) * (1.0 / D), axis=0, keepdims=True)

        @pl.when(pl.program_id(0) == 0)
        def _():
            l_ref[...] = part

        @pl.when(pl.program_id(0) > 0)
        def _():
            l_ref[...] += part

    row = pl.BlockSpec((tr, D), lambda i: (i, 0))
    return _call(
        body, name=name, grid=(S // tr,), in_specs=[row, row],
        out_specs=[row, pl.BlockSpec((1, 1), lambda i: (0, 0))],
        out_shape=[jax.ShapeDtypeStruct((S, D), F32), jax.ShapeDtypeStruct((1, 1), F32)],
        compiler_params=_cparams(("arbitrary",)),
    )(y, t)


def _combine(terms, scales, *, name, out_dtype=F32):
    S, D = terms[0].shape
    tr = _pick(S, 256)
    n = len(terms)

    def body(*refs):
        acc = scales[0] * refs[0][...].astype(F32)
        for i in range(1, n):
            acc = acc + scales[i] * refs[i][...].astype(F32)
        refs[n][...] = acc.astype(out_dtype)

    row = pl.BlockSpec((tr, D), lambda i: (i, 0))
    return _call(
        body, name=name, grid=(S // tr,), in_specs=[row] * n, out_specs=row,
        out_shape=jax.ShapeDtypeStruct((S, D), out_dtype),
        compiler_params=_cparams(("parallel",)),
    )(*terms)


def _split3(x):
    h = x.astype(BF16)
    r = x - h.astype(F32)
    m = r.astype(BF16)
    l = (r - m.astype(F32)).astype(BF16)
    return h, m, l


def _tri_matmul(tri_bf, x):
    h, m, l = _split3(x)
    dn = (((1,), (0,)), ((), ()))
    return (lax.dot_general(tri_bf, l, dn, preferred_element_type=F32)
            + lax.dot_general(tri_bf, m, dn, preferred_element_type=F32)
            + lax.dot_general(tri_bf, h, dn, preferred_element_type=F32))


def _gate_fwd(fl, bf, *, name):
    S = fl.shape[0]
    tc = _pick(S, 256)
    nchunk = S // tc

    def body(fl_ref, bf_ref, c_ref, sg_ref):
        r = lax.broadcasted_iota(jnp.int32, (tc, tc), 0)
        cidx = lax.broadcasted_iota(jnp.int32, (tc, tc), 1)
        tri = (r >= cidx).astype(BF16)
        carry = jnp.zeros((1, LANE), F32)
        for ch in range(nchunk):
            x = fl_ref[pl.ds(ch * tc, tc), :] + bf_ref[...]
            lf = jnp.minimum(x, 0.0) - jnp.log(1.0 + jnp.exp(-jnp.abs(x)))
            sg_ref[pl.ds(ch * tc, tc), :] = jax.nn.sigmoid(-x)
            c_ref[pl.ds(ch * tc, tc), :] = _tri_matmul(tri, lf) + carry
            carry = carry + jnp.sum(lf, axis=0, keepdims=True)

    full = pl.BlockSpec((S, LANE), lambda: (0, 0))
    return _call(
        body, name=name, in_specs=[full, pl.BlockSpec((1, LANE), lambda: (0, 0))], out_specs=[full, full],
        out_shape=[jax.ShapeDtypeStruct((S, LANE), F32)] * 2,
        compiler_params=pltpu.CompilerParams(vmem_limit_bytes=VMEM_LIMIT),
    )(fl, bf)


def _gate_bwd(dc, sg, *, name):
    S = dc.shape[0]
    tc = _pick(S, 256)
    nchunk = S // tc

    def body(dc_ref, sg_ref, dfl_ref, db_ref):
        r = lax.broadcasted_iota(jnp.int32, (tc, tc), 0)
        cidx = lax.broadcasted_iota(jnp.int32, (tc, tc), 1)
        tri = (r <= cidx).astype(BF16)
        carry = jnp.zeros((1, LANE), F32)
        dbacc = jnp.zeros((1, LANE), F32)
        for ch in reversed(range(nchunk)):
            d = dc_ref[pl.ds(ch * tc, tc), :]
            dfl = (_tri_matmul(tri, d) + carry) * sg_ref[pl.ds(ch * tc, tc), :]
            dfl_ref[pl.ds(ch * tc, tc), :] = dfl
            dbacc = dbacc + jnp.sum(dfl, axis=0, keepdims=True)
            carry = carry + jnp.sum(d, axis=0, keepdims=True)
        db_ref[...] = dbacc

    full = pl.BlockSpec((S, LANE), lambda: (0, 0))
    return _call(
        body, name=name, in_specs=[full, full], out_specs=[full, pl.BlockSpec((1, LANE), lambda: (0, 0))],
        out_shape=[jax.ShapeDtypeStruct((S, LANE), F32), jax.ShapeDtypeStruct((1, LANE), F32)],
        compiler_params=pltpu.CompilerParams(vmem_limit_bytes=VMEM_LIMIT),
    )(dc, sg)


def _fox_scores(q_ref, k_ref, cc_ref, cr_ref, qi, tq, S):
    scale = 1.0 / math.sqrt(FOX_HEAD_DIM)
    s = lax.dot_general(q_ref[...].astype(BF16), k_ref[...].astype(BF16), (((1,), (1,)), ((), ())),
                        preferred_element_type=F32) * scale
    s = s + cc_ref[...] - cr_ref[...]
    row = lax.broadcasted_iota(jnp.int32, (tq, S), 0) + qi * tq
    col = lax.broadcasted_iota(jnp.int32, (tq, S), 1)
    return s, row >= col


def _fox_fwd(P, ccol, crow, *, name):
    S = P.shape[0]
    tq = _pick(S, 256)
    H = FOX_HEADS

    def body(q_ref, k_ref, v_ref, cc_ref, cr_ref, o_ref, l_ref):
        s, causal = _fox_scores(q_ref, k_ref, cc_ref, cr_ref, pl.program_id(1), tq, S)
        s = jnp.where(causal, s, -1e30)
        m = jnp.max(s, axis=-1, keepdims=True)
        e = jnp.exp(s - m)
        den = jnp.sum(e, axis=-1, keepdims=True)
        p = e / den
        o_ref[...] = jnp.dot(p.astype(BF16), v_ref[...].astype(BF16), preferred_element_type=F32)
        l_ref[...] = m + jnp.log(den)

    return _call(
        body, name=name, grid=(H, S // tq),
        in_specs=[pl.BlockSpec((tq, 128), lambda h, i: (i, h)),
                  pl.BlockSpec((S, 128), lambda h, i: (0, H + h)),
                  pl.BlockSpec((S, 128), lambda h, i: (0, 2 * H + h)),
                  pl.BlockSpec((None, tq, 1), lambda h, i: (h, i, 0)),
                  pl.BlockSpec((None, 1, S), lambda h, i: (h, 0, 0))],
        out_specs=[pl.BlockSpec((tq, 128), lambda h, i: (i, h)),
                   pl.BlockSpec((None, tq, 1), lambda h, i: (h, i, 0))],
        out_shape=[jax.ShapeDtypeStruct((S, FOX_WIDTH), F32), jax.ShapeDtypeStruct((H, S, 1), F32)],
        compiler_params=_cparams(("parallel", "parallel")),
    )(P, P, P, ccol, crow)


def _fox_bwd(P, ccol, crow, o, lse, dcat, *, name):
    S = P.shape[0]
    tq = _pick(S, 256)
    H = FOX_HEADS
    nq = S // tq
    scale = 1.0 / math.sqrt(FOX_HEAD_DIM)

    def body(q_ref, k_ref, v_ref, cc_ref, cr_ref, o_ref, l_ref, do_ref,
             dq_ref, dk_ref, dv_ref, dcc_ref, dcr_ref, dk_acc, dv_acc):
        qi = pl.program_id(1)
        s, causal = _fox_scores(q_ref, k_ref, cc_ref, cr_ref, qi, tq, S)
        p = jnp.where(causal, jnp.exp(s - l_ref[...]), 0.0)
        do = do_ref[...]
        do_bf = do.astype(BF16)
        dp = lax.dot_general(do_bf, v_ref[...].astype(BF16), (((1,), (1,)), ((), ())), preferred_element_type=F32)
        delta = jnp.sum(do * o_ref[...], axis=-1, keepdims=True)
        ds = p * (dp - delta)
        ds_bf = ds.astype(BF16)
        dq_ref[...] = (jnp.dot(ds_bf, k_ref[...].astype(BF16), preferred_element_type=F32) * scale).astype(BF16)
        dkp = lax.dot_general(ds_bf, q_ref[...].astype(BF16), (((0,), (0,)), ((), ())),
                              preferred_element_type=F32) * scale
        dvp = lax.dot_general(p.astype(BF16), do_bf, (((0,), (0,)), ((), ())), preferred_element_type=F32)
        dcc_ref[...] = jnp.sum(ds, axis=-1, keepdims=True)
        dcr = jnp.sum(ds, axis=0, keepdims=True)

        @pl.when(qi == 0)
        def _():
            dk_acc[...] = dkp
            dv_acc[...] = dvp
            dcr_ref[...] = dcr

        @pl.when(qi > 0)
        def _():
            dk_acc[...] += dkp
            dv_acc[...] += dvp
            dcr_ref[...] += dcr

        @pl.when(qi == nq - 1)
        def _():
            dk_ref[...] = dk_acc[...].astype(BF16)
            dv_ref[...] = dv_acc[...].astype(BF16)

    qblk = pl.BlockSpec((tq, 128), lambda h, i: (i, h))
    kvo = pl.BlockSpec((S, 128), lambda h, i: (0, h))
    col = pl.BlockSpec((None, tq, 1), lambda h, i: (h, i, 0))
    rowv = pl.BlockSpec((None, 1, S), lambda h, i: (h, 0, 0))
    return _call(
        body, name=name, grid=(H, nq),
        in_specs=[qblk,
                  pl.BlockSpec((S, 128), lambda h, i: (0, H + h)),
                  pl.BlockSpec((S, 128), lambda h, i: (0, 2 * H + h)),
                  col, rowv, qblk, col, qblk],
        out_specs=[qblk, kvo, kvo, col, rowv],
        out_shape=[jax.ShapeDtypeStruct((S, FOX_WIDTH), BF16)] * 3
        + [jax.ShapeDtypeStruct((H, S, 1), F32), jax.ShapeDtypeStruct((H, 1, S), F32)],
        scratch_shapes=[pltpu.VMEM((S, 128), F32), pltpu.VMEM((S, 128), F32)],
        compiler_params=_cparams(("parallel", "arbitrary")),
    )(P, P, P, ccol, crow, o, lse, dcat)


def _s5_disc_fwd(lr, li, ls, *, name):
    G, Pn = lr.shape

    def body(lr_ref, li_ref, ls_ref, ar_ref, ai_ref, gr_ref, gi_ref):
        lr_, li_ = lr_ref[...], li_ref[...]
        dt = jnp.exp(ls_ref[...])
        mag = jnp.exp(lr_ * dt)
        th = li_ * dt
        ar = mag * jnp.cos(th)
        ai = mag * jnp.sin(th)
        den = lr_ * lr_ + li_ * li_
        xr = ar - 1.0
        ar_ref[...] = ar
        ai_ref[...] = ai
        gr_ref[...] = (xr * lr_ + ai * li_) / den
        gi_ref[...] = (ai * lr_ - xr * li_) / den

    sq = pl.BlockSpec((G, Pn), lambda: (0, 0))
    return _call(
        body, name=name, in_specs=[sq, sq, pl.BlockSpec((G, 1), lambda: (0, 0))], out_specs=[sq] * 4,
        out_shape=[jax.ShapeDtypeStruct((G, Pn), F32)] * 4,
    )(lr, li, ls)


def _s5_disc_bwd(lr, li, ls, dar, dai, dgr, dgi, *, name):
    G, Pn = lr.shape

    def body(lr_ref, li_ref, ls_ref, dar_ref, dai_ref, dgr_ref, dgi_ref, dlr_ref, dli_ref, dls_ref):
        lr_, li_ = lr_ref[...], li_ref[...]
        dt = jnp.exp(ls_ref[...])
        mag = jnp.exp(lr_ * dt)
        th = li_ * dt
        ar = mag * jnp.cos(th)
        ai = mag * jnp.sin(th)
        den = lr_ * lr_ + li_ * li_
        xr = ar - 1.0
        xi = ai
        g_re = (xr * lr_ + xi * li_) / den
        g_im = (xi * lr_ - xr * li_) / den
        dgr_, dgi_ = dgr_ref[...], dgi_ref[...]
        dxr = (dgr_ * lr_ - dgi_ * li_) / den
        dxi = (dgr_ * li_ + dgi_ * lr_) / den
        dden = -(dgr_ * g_re + dgi_ * g_im) / den
        dlr = (dgr_ * xr + dgi_ * xi) / den + 2.0 * dden * lr_
        dli = (dgr_ * xi - dgi_ * xr) / den + 2.0 * dden * li_
        da_r = dar_ref[...] + dxr
        da_i = dai_ref[...] + dxi
        dmag_mag = da_r * ar + da_i * ai
        dth = da_i * ar - da_r * ai
        dlr_ref[...] = dlr + dmag_mag * dt
        dli_ref[...] = dli + dth * dt
        ddt = jnp.sum(dmag_mag * lr_ + dth * li_, axis=-1, keepdims=True)
        dls_ref[...] = ddt * dt

    sq = pl.BlockSpec((G, Pn), lambda: (0, 0))
    c1 = pl.BlockSpec((G, 1), lambda: (0, 0))
    return _call(
        body, name=name, in_specs=[sq, sq, c1, sq, sq, sq, sq], out_specs=[sq, sq, c1],
        out_shape=[jax.ShapeDtypeStruct((G, Pn), F32)] * 2 + [jax.ShapeDtypeStruct((G, 1), F32)],
    )(lr, li, ls, dar, dai, dgr, dgi)


def _s5_bb_fwd(gr, gi, br, bi, *, name):
    R, C = br.shape

    def body(gr_ref, gi_ref, br_ref, bi_ref, or_ref, oi_ref):
        g_r, g_i, b_r, b_i = gr_ref[...], gi_ref[...], br_ref[...], bi_ref[...]
        or_ref[...] = g_r * b_r - g_i * b_i
        oi_ref[...] = g_r * b_i + g_i * b_r

    w = pl.BlockSpec((R, C), lambda: (0, 0))
    c1 = pl.BlockSpec((R, 1), lambda: (0, 0))
    return _call(body, name=name, in_specs=[c1, c1, w, w], out_specs=[w, w],
                 out_shape=[jax.ShapeDtypeStruct((R, C), F32)] * 2)(gr, gi, br, bi)


def _s5_bb_bwd(gr, gi, br, bi, dbbr, dbbi, *, name):
    R, C = br.shape

    def body(gr_ref, gi_ref, br_ref, bi_ref, dr_ref, di_ref, dbr_ref, dbi_ref, dgr_ref, dgi_ref):
        g_r, g_i, b_r, b_i = gr_ref[...], gi_ref[...], br_ref[...], bi_ref[...]
        d_r, d_i = dr_ref[...], di_ref[...]
        dbr_ref[...] = g_r * d_r + g_i * d_i
        dbi_ref[...] = g_r * d_i - g_i * d_r
        dgr_ref[...] = jnp.sum(d_r * b_r + d_i * b_i, axis=-1, keepdims=True)
        dgi_ref[...] = jnp.sum(d_i * b_r - d_r * b_i, axis=-1, keepdims=True)

    w = pl.BlockSpec((R, C), lambda: (0, 0))
    c1 = pl.BlockSpec((R, 1), lambda: (0, 0))
    return _call(body, name=name, in_specs=[c1, c1, w, w, w, w], out_specs=[w, w, c1, c1],
                 out_shape=[jax.ShapeDtypeStruct((R, C), F32)] * 2 + [jax.ShapeDtypeStruct((R, 1), F32)] * 2,
                 )(gr, gi, br, bi, dbbr, dbbi)


_DIAG_TILE = 8


def _diag_mask(gr, gc):
    rows, cols = _DIAG_TILE * gr, _DIAG_TILE * gc
    r = lax.broadcasted_iota(jnp.int32, (rows, cols), 0) >> (gr.bit_length() - 1)
    c = lax.broadcasted_iota(jnp.int32, (rows, cols), 1) >> (gc.bit_length() - 1)
    return r == c


def _diag_expand(t2, gr, gc, *, name):
    _, R, _ = t2.shape
    G = R // gr
    nt = G // _DIAG_TILE
    rows, cols = _DIAG_TILE * gr, _DIAG_TILE * gc

    def body(t_ref, o_ref):
        @pl.when(pl.program_id(1) == pl.program_id(2))
        def _():
            src = lax.broadcasted_iota(jnp.int32, (gc, cols), 0)
            dst = lax.broadcasted_iota(jnp.int32, (gc, cols), 1) & (gc - 1)
            spread = (src == dst).astype(BF16)
            y = jnp.dot(t_ref[...].astype(BF16), spread, preferred_element_type=F32)
            o_ref[...] = jnp.where(_diag_mask(gr, gc), y, 0.0).astype(BF16)

        @pl.when(pl.program_id(1) != pl.program_id(2))
        def _():
            o_ref[...] = jnp.zeros_like(o_ref)

    return _call(
        body, name=name, grid=(2, nt, nt),
        in_specs=[pl.BlockSpec((None, rows, gc), lambda p, i, j: (p, i, 0))],
        out_specs=pl.BlockSpec((None, rows, cols), lambda p, i, j: (p, i, j)),
        out_shape=jax.ShapeDtypeStruct((2, R, G * gc), BF16),
        compiler_params=_cparams(("parallel",) * 3),
    )(t2)


def _diag_extract(xd, gr, gc, *, name):
    _, R, _ = xd.shape
    nt = R // gr // _DIAG_TILE
    rows, cols = _DIAG_TILE * gr, _DIAG_TILE * gc

    def body(x_ref, o_ref):
        src = lax.broadcasted_iota(jnp.int32, (cols, gc), 0) & (gc - 1)
        dst = lax.broadcasted_iota(jnp.int32, (cols, gc), 1)
        fold = (src == dst).astype(BF16)
        parts = _split3(jnp.where(_diag_mask(gr, gc), x_ref[...], 0.0))
        acc = jnp.dot(parts[2], fold, preferred_element_type=F32)
        acc = acc + jnp.dot(parts[1], fold, preferred_element_type=F32)
        o_ref[...] = acc + jnp.dot(parts[0], fold, preferred_element_type=F32)

    return _call(
        body, name=name, grid=(2, nt),
        in_specs=[pl.BlockSpec((None, rows, cols), lambda p, i: (p, i, i))],
        out_specs=pl.BlockSpec((None, rows, gc), lambda p, i: (p, i, 0)),
        out_shape=jax.ShapeDtypeStruct((2, R, gc), F32),
        compiler_params=_cparams(("parallel",) * 2),
    )(xd)


def _s5_scan_fwd(bu, a, *, name):
    _, S, N = bu.shape
    tc = 512
    nt = N // tc

    def body(a_ref, b_ref, h_ref):
        ar, ai = a_ref[0], a_ref[1]

        def step(t, carry):
            hr, hi = carry
            nr = ar * hr - ai * hi + b_ref[0, pl.ds(t, 1), :]
            ni = ar * hi + ai * hr + b_ref[1, pl.ds(t, 1), :]
            h_ref[0, pl.ds(t, 1), :] = nr
            h_ref[1, pl.ds(t, 1), :] = ni
            return nr, ni

        z = jnp.zeros((1, tc), F32)
        lax.fori_loop(0, S, step, (z, z), unroll=8)

    vec = pl.BlockSpec((2, 1, tc), lambda j: (0, 0, j))
    mat = pl.BlockSpec((2, S, tc), lambda j: (0, 0, j))
    return _call(
        body, name=name, grid=(nt,), in_specs=[vec, mat], out_specs=mat,
        out_shape=jax.ShapeDtypeStruct((2, S, N), F32),
        compiler_params=_cparams(("parallel",)),
    )(a, bu)


def _s5_scan_bwd(g, h, a, *, name):
    _, S, N = g.shape
    tc = 256
    nt = N // tc

    def body(a_ref, g_ref, h_ref, l_ref, da_ref):
        ar, ai = a_ref[0], a_ref[1]

        def step(i, carry):
            t = S - 1 - i
            lr, li, dar, dai = carry
            nr = g_ref[0, pl.ds(t, 1), :] + ar * lr + ai * li
            ni = g_ref[1, pl.ds(t, 1), :] + ar * li - ai * lr
            l_ref[0, pl.ds(t, 1), :] = nr
            l_ref[1, pl.ds(t, 1), :] = ni
            tp = jnp.maximum(t - 1, 0)
            keep = jnp.where(t > 0, 1.0, 0.0).astype(F32)
            hpr = h_ref[0, pl.ds(tp, 1), :] * keep
            hpi = h_ref[1, pl.ds(tp, 1), :] * keep
            return nr, ni, dar + nr * hpr + ni * hpi, dai + ni * hpr - nr * hpi

        z = jnp.zeros((1, tc), F32)
        _, _, dar, dai = lax.fori_loop(0, S, step, (z, z, z, z), unroll=8)
        da_ref[0] = dar
        da_ref[1] = dai

    vec = pl.BlockSpec((2, 1, tc), lambda j: (0, 0, j))
    mat = pl.BlockSpec((2, S, tc), lambda j: (0, 0, j))
    return _call(
        body, name=name, grid=(nt,), in_specs=[vec, mat, mat], out_specs=[mat, vec],
        out_shape=[jax.ShapeDtypeStruct((2, S, N), F32), jax.ShapeDtypeStruct((2, 1, N), F32)],
        compiler_params=_cparams(("parallel",)),
    )(a, g, h)


_GELU_C = math.sqrt(2.0 / math.pi)


def _s5_out_fwd(yc, P, dskip, *, name):
    S, W = yc.shape
    tr = _pick(S, 256)
    ub = 3 * FOX_WIDTH // W

    def body(yc_ref, u_ref, d_ref, y_ref, yg_ref):
        y = yc_ref[...] + d_ref[...] * u_ref[...]
        y_ref[...] = y
        t = jnp.tanh(_GELU_C * (y + 0.044715 * y * y * y))
        yg_ref[...] = (0.5 * y * (1.0 + t)).astype(BF16)

    row = pl.BlockSpec((tr, W), lambda i: (i, 0))
    return _call(
        body, name=name, grid=(S // tr,),
        in_specs=[row, pl.BlockSpec((tr, W), lambda i: (i, ub)), pl.BlockSpec((1, W), lambda i: (0, 0))],
        out_specs=[row, row],
        out_shape=[jax.ShapeDtypeStruct((S, W), F32), jax.ShapeDtypeStruct((S, W), BF16)],
        compiler_params=_cparams(("parallel",)),
    )(yc, P, dskip)


def _s5_out_bwd(dyg, y, P, dskip, *, name):
    S, W = y.shape
    tr = _pick(S, 256)
    ub = 3 * FOX_WIDTH // W

    def body(dyg_ref, y_ref, u_ref, d_ref, dy_ref, du_ref, dd_ref):
        y_ = y_ref[...]
        inner = _GELU_C * (y_ + 0.044715 * y_ * y_ * y_)
        t = jnp.tanh(inner)
        dgelu = 0.5 * (1.0 + t) + 0.5 * y_ * (1.0 - t * t) * _GELU_C * (1.0 + 3.0 * 0.044715 * y_ * y_)
        dy = dyg_ref[...] * dgelu
        dy_ref[...] = dy.astype(BF16)
        du_ref[...] = d_ref[...] * dy
        part = jnp.sum(dy * u_ref[...], axis=0, keepdims=True)

        @pl.when(pl.program_id(0) == 0)
        def _():
            dd_ref[...] = part

        @pl.when(pl.program_id(0) > 0)
        def _():
            dd_ref[...] += part

    row = pl.BlockSpec((tr, W), lambda i: (i, 0))
    vec = pl.BlockSpec((1, W), lambda i: (0, 0))
    return _call(
        body, name=name, grid=(S // tr,),
        in_specs=[row, row, pl.BlockSpec((tr, W), lambda i: (i, ub)), vec],
        out_specs=[row, row, vec],
        out_shape=[jax.ShapeDtypeStruct((S, W), BF16), jax.ShapeDtypeStruct((S, W), F32),
                   jax.ShapeDtypeStruct((1, W), F32)],
        compiler_params=_cparams(("arbitrary",)),
    )(dyg, y, P, dskip)


def _glu_fwd(z, *, name):
    S, W2 = z.shape
    W = W2 // 2
    tr = _pick(S, 256)

    def body(z1_ref, z2_ref, o_ref):
        o_ref[...] = (z1_ref[...] * jax.nn.sigmoid(z2_ref[...])).astype(BF16)

    return _call(
        body, name=name, grid=(S // tr,),
        in_specs=[pl.BlockSpec((tr, W), lambda i: (i, 0)), pl.BlockSpec((tr, W), lambda i: (i, 1))],
        out_specs=pl.BlockSpec((tr, W), lambda i: (i, 0)),
        out_shape=jax.ShapeDtypeStruct((S, W), BF16),
        compiler_params=_cparams(("parallel",)),
    )(z, z)


def _glu_bwd(z, dcat, *, name):
    S, W2 = z.shape
    W = W2 // 2
    tr = _pick(S, 256)

    def body(z1_ref, z2_ref, d_ref, dz1_ref, dz2_ref):
        sg = jax.nn.sigmoid(z2_ref[...])
        d = d_ref[...]
        dz1_ref[...] = (d * sg).astype(BF16)
        dz2_ref[...] = (d * z1_ref[...] * sg * (1.0 - sg)).astype(BF16)

    lo = pl.BlockSpec((tr, W), lambda i: (i, 0))
    hi = pl.BlockSpec((tr, W), lambda i: (i, 1))
    dz1, dz2 = _call(
        body, name=name, grid=(S // tr,), in_specs=[lo, hi, hi], out_specs=[lo, lo],
        out_shape=[jax.ShapeDtypeStruct((S, W), BF16)] * 2,
        compiler_params=_cparams(("parallel",)),
    )(z, z, dcat)
    return jnp.concatenate([dz1, dz2], axis=1)


def _act_fwd(h, cw, cb, *, name):
    _, S, FP = h.shape
    tr = _pick(S, 256)
    hb = tr // 8

    def conv(x_ref, halo_ref, w_ref, b_ref, ext, first):
        ext[pl.ds(0, 8), :] = jnp.where(first, 0.0, halo_ref[...])
        ext[pl.ds(8, tr), :] = x_ref[...]
        return (b_ref[...] + w_ref[pl.ds(2, 1), :] * ext[pl.ds(8, tr), :]
                + w_ref[pl.ds(1, 1), :] * ext[pl.ds(7, tr), :] + w_ref[pl.ds(0, 1), :] * ext[pl.ds(6, tr), :])

    def body(g_ref, gh_ref, v_ref, vh_ref, wg_ref, wv_ref, bg_ref, bv_ref, a_ref, ext):
        first = pl.program_id(1) == 0
        cg = conv(g_ref, gh_ref, wg_ref, bg_ref, ext, first)
        cv = conv(v_ref, vh_ref, wv_ref, bv_ref, ext, first)
        a_ref[...] = (cg * jax.nn.sigmoid(cg) * cv).astype(BF16)

    def main(off):
        return pl.BlockSpec((None, tr, FP), lambda j, i: (j + off, i, 0))

    def halo(off):
        return pl.BlockSpec((None, 8, FP), lambda j, i: (j + off, jnp.maximum(i * hb - 1, 0), 0))

    def wspec(off):
        return pl.BlockSpec((None, 3, FP), lambda j, i: (j + off, 0, 0))

    def bspec(off):
        return pl.BlockSpec((None, 1, FP), lambda j, i: (j + off, 0, 0))

    cb3 = cb.reshape(4, 1, FP)
    return _call(
        body, name=name, grid=(2, S // tr),
        in_specs=[main(0), halo(0), main(2), halo(2), wspec(0), wspec(2), bspec(0), bspec(2)],
        out_specs=pl.BlockSpec((None, tr, FP), lambda j, i: (j, i, 0)),
        out_shape=jax.ShapeDtypeStruct((2, S, FP), BF16),
        scratch_shapes=[pltpu.VMEM((tr + 8, FP), F32)],
        compiler_params=_cparams(("parallel", "arbitrary")),
    )(h, h, h, h, cw, cw, cb3, cb3)


def _act_bwd(h, da, cw, cb, *, name):
    _, S, FP = h.shape
    tr = _pick(S, 128)
    hb = tr // 8
    nr = S // tr

    def fill(ext, x_ref, prev_ref, next_ref, first, last):
        ext[pl.ds(0, 8), :] = jnp.where(first, 0.0, prev_ref[...])
        ext[pl.ds(8, tr), :] = x_ref[...]
        ext[pl.ds(8 + tr, 8), :] = jnp.where(last, 0.0, next_ref[...])

    def convo(ext, w, b, base, n):
        return (b + w[2] * ext[pl.ds(base, n), :] + w[1] * ext[pl.ds(base - 1, n), :]
                + w[0] * ext[pl.ds(base - 2, n), :])

    def body(g_ref, gp_ref, gn_ref, v_ref, vp_ref, vn_ref, da_ref, dan_ref,
             wg_ref, wv_ref, bg_ref, bv_ref,
             dg_ref, dv_ref, dwg_ref, dwv_ref, dbg_ref, dbv_ref, eg, ev, ed, dcg, dcv):
        i = pl.program_id(1)
        first = i == 0
        last = i == nr - 1
        fill(eg, g_ref, gp_ref, gn_ref, first, last)
        fill(ev, v_ref, vp_ref, vn_ref, first, last)
        ed[pl.ds(0, tr), :] = da_ref[...]
        ed[pl.ds(tr, 8), :] = jnp.where(last, 0.0, dan_ref[...])
        wg = [wg_ref[pl.ds(k, 1), :] for k in range(3)]
        wv = [wv_ref[pl.ds(k, 1), :] for k in range(3)]
        n = tr + 8
        cg = convo(eg, wg, bg_ref[...], 8, n)
        cv = convo(ev, wv, bv_ref[...], 8, n)
        sg = jax.nn.sigmoid(cg)
        d = ed[...]
        dcg[...] = d * cv * sg * (1.0 + cg * (1.0 - sg))
        dcv[...] = d * cg * sg
        for (dc, w, e, dh_ref, dw_ref, db_ref) in ((dcg, wg, eg, dg_ref, dwg_ref, dbg_ref),
                                                  (dcv, wv, ev, dv_ref, dwv_ref, dbv_ref)):
            d0 = dc[pl.ds(0, tr), :]
            dh_ref[...] = (w[2] * d0 + w[1] * dc[pl.ds(1, tr), :] + w[0] * dc[pl.ds(2, tr), :]).astype(BF16)
            pw = [jnp.sum(d0 * e[pl.ds(6 + k, tr), :], axis=0, keepdims=True) for k in range(3)]
            pb = jnp.sum(d0, axis=0, keepdims=True)

            @pl.when(first)
            def _():
                for k in range(3):
                    dw_ref[pl.ds(k, 1), :] = pw[k]
                db_ref[...] = pb

            @pl.when(jnp.logical_not(first))
            def _():
                for k in range(3):
                    dw_ref[pl.ds(k, 1), :] += pw[k]
                db_ref[...] += pb

    def main(off):
        return pl.BlockSpec((None, tr, FP), lambda j, i: (j + off, i, 0))

    def prev(off):
        return pl.BlockSpec((None, 8, FP), lambda j, i: (j + off, jnp.maximum(i * hb - 1, 0), 0))

    def nxt(off):
        return pl.BlockSpec((None, 8, FP), lambda j, i: (j + off, jnp.minimum((i + 1) * hb, S // 8 - 1), 0))

    def wspec(off):
        return pl.BlockSpec((None, 3, FP), lambda j, i: (j + off, 0, 0))

    def bspec(off):
        return pl.BlockSpec((None, 1, FP), lambda j, i: (j + off, 0, 0))

    cb3 = cb.reshape(4, 1, FP)
    dg, dv, dwg, dwv, dbg, dbv = _call(
        body, name=name, grid=(2, nr),
        in_specs=[main(0), prev(0), nxt(0), main(2), prev(2), nxt(2), main(0), nxt(0),
                  wspec(0), wspec(2), bspec(0), bspec(2)],
        out_specs=[main(0), main(0), wspec(0), wspec(0), bspec(0), bspec(0)],
        out_shape=[jax.ShapeDtypeStruct((2, S, FP), BF16)] * 2
        + [jax.ShapeDtypeStruct((2, 3, FP), F32)] * 2 + [jax.ShapeDtypeStruct((2, 1, FP), F32)] * 2,
        scratch_shapes=[pltpu.VMEM((tr + 16, FP), F32), pltpu.VMEM((tr + 16, FP), F32),
                        pltpu.VMEM((tr + 8, FP), F32), pltpu.VMEM((tr + 8, FP), F32),
                        pltpu.VMEM((tr + 8, FP), F32)],
        compiler_params=_cparams(("parallel", "arbitrary")),
    )(h, h, h, h, h, h, da, da, cw, cw, cb3, cb3)
    return (jnp.concatenate([dg, dv], axis=0), jnp.concatenate([dwg, dwv], axis=0),
            jnp.concatenate([dbg, dbv], axis=0))


def _rope_tables(posf, *, name):
    S = posf.shape[0]
    half = ROPE_DIM // 2
    d = np.arange(LANE) % SWA_HEAD_DIM
    invf = np.where(d < ROPE_DIM, ROPE_THETA ** (-(d % half).astype(np.float64) / half), 0.0).astype(np.float32)
    m_rot = (d < ROPE_DIM).astype(np.float32)
    m_a = (d < half).astype(np.float32)
    m_b = ((d >= half) & (d < ROPE_DIM)).astype(np.float32)
    consts = jnp.asarray(np.stack([invf, m_rot, m_a, m_b] + [np.zeros(LANE, np.float32)] * 4))

    def body(p_ref, k_ref, c_ref, sa_ref, sb_ref):
        k = k_ref[...]
        ang = p_ref[...] * k[0:1]
        co, si = jnp.cos(ang), jnp.sin(ang)
        c_ref[...] = k[1:2] * co + (1.0 - k[1:2])
        sa_ref[...] = -k[2:3] * si
        sb_ref[...] = k[3:4] * si

    full = pl.BlockSpec((S, LANE), lambda: (0, 0))
    return _call(
        body, name=name,
        in_specs=[pl.BlockSpec((S, 1), lambda: (0, 0)), pl.BlockSpec((8, LANE), lambda: (0, 0))],
        out_specs=[full] * 3, out_shape=[jax.ShapeDtypeStruct((S, LANE), F32)] * 3,
    )(posf, consts)


def _rope_apply(x, tabs, *, col0, width, inverse, name, out_dtype):
    S = x.shape[0]
    tr = _pick(S, 256)
    rep = width // LANE
    cb = col0 // width

    def body(x_ref, c_ref, sa_ref, sb_ref, o_ref):
        xv = x_ref[...].astype(F32)
        c = jnp.tile(c_ref[...], (1, rep))
        sa = jnp.tile(sa_ref[...], (1, rep))
        sb = jnp.tile(sb_ref[...], (1, rep))
        if not inverse:
            out = xv * c + pltpu.roll(xv, width - 8, 1) * sa + pltpu.roll(xv, 8, 1) * sb
        else:
            out = xv * c + pltpu.roll(xv * sa, 8, 1) + pltpu.roll(xv * sb, width - 8, 1)
        o_ref[...] = out.astype(out_dtype)

    tab = pl.BlockSpec((tr, LANE), lambda i: (i, 0))
    return _call(
        body, name=name, grid=(S // tr,),
        in_specs=[pl.BlockSpec((tr, width), lambda i: (i, cb)), tab, tab, tab],
        out_specs=pl.BlockSpec((tr, width), lambda i: (i, 0)),
        out_shape=jax.ShapeDtypeStruct((S, width), out_dtype),
        compiler_params=_cparams(("parallel",)),
    )(x, *tabs)


def _swa_mask(n):
    rows = SWA_GROUPS * SWA_WINDOW
    qi = lax.broadcasted_iota(jnp.int32, (rows, 2 * SWA_WINDOW), 0) & (SWA_WINDOW - 1)
    kj = lax.broadcasted_iota(jnp.int32, (rows, 2 * SWA_WINDOW), 1)
    rel = SWA_WINDOW + qi - kj
    return (rel >= 0) & (rel < SWA_WINDOW) & ((n > 0) | (kj >= SWA_WINDOW))


def _swa_fwd(qT, kT, vT, sink_rows, *, name):
    S = qT.shape[1]
    W, G, Dh = SWA_WINDOW, SWA_GROUPS, SWA_HEAD_DIM
    nb = S // W
    scale = 1.0 / math.sqrt(Dh)

    def body(q_ref, kp_ref, kc_ref, vp_ref, vc_ref, s_ref, o_ref, l_ref):
        n = pl.program_id(1)
        q = q_ref[...].reshape(G * W, Dh)
        kk = jnp.concatenate([kp_ref[...], kc_ref[...]], axis=0)
        vv = jnp.concatenate([vp_ref[...], vc_ref[...]], axis=0)
        s = lax.dot_general(q, kk, (((1,), (1,)), ((), ())), preferred_element_type=F32) * scale
        s = jnp.where(_swa_mask(n), s, -1e30)
        sink = s_ref[...]
        m = jnp.maximum(jnp.max(s, axis=-1, keepdims=True), sink)
        e = jnp.exp(s - m)
        den = jnp.sum(e, axis=-1, keepdims=True) + jnp.exp(sink - m)
        p = e / den
        o_ref[...] = jnp.dot(p.astype(BF16), vv, preferred_element_type=F32).reshape(G, W, Dh)
        l_ref[...] = (m + jnp.log(den)).reshape(G, W, 1)

    qs = pl.BlockSpec((G, W, Dh), lambda g, n: (g, n, 0))
    prev = pl.BlockSpec((None, W, Dh), lambda g, n: (g, jnp.maximum(n - 1, 0), 0))
    cur = pl.BlockSpec((None, W, Dh), lambda g, n: (g, n, 0))
    return _call(
        body, name=name, grid=(SWA_KV_HEADS, nb),
        in_specs=[qs, prev, cur, prev, cur, pl.BlockSpec((None, G * W, 1), lambda g, n: (g, 0, 0))],
        out_specs=[qs, pl.BlockSpec((G, W, 1), lambda g, n: (g, n, 0))],
        out_shape=[jax.ShapeDtypeStruct((SWA_HEADS, S, Dh), F32), jax.ShapeDtypeStruct((SWA_HEADS, S, 1), F32)],
        compiler_params=_cparams(("parallel", "parallel")),
    )(qT, kT, kT, vT, vT, sink_rows)


def _swa_bwd(qT, kT, vT, sink_rows, oT, L, doT, *, name):
    S = qT.shape[1]
    W, G, Dh = SWA_WINDOW, SWA_GROUPS, SWA_HEAD_DIM
    nb = S // W
    scale = 1.0 / math.sqrt(Dh)

    def body(q_ref, kp_ref, kc_ref, vp_ref, vc_ref, s_ref, o_ref, l_ref, do_ref,
             dq_ref, dk_ref, dv_ref, ds_ref):
        n = pl.program_id(1)
        q = q_ref[...].reshape(G * W, Dh)
        kk = jnp.concatenate([kp_ref[...], kc_ref[...]], axis=0)
        vv = jnp.concatenate([vp_ref[...], vc_ref[...]], axis=0)
        s = lax.dot_general(q, kk, (((1,), (1,)), ((), ())), preferred_element_type=F32) * scale
        lrow = l_ref[...].reshape(G * W, 1)
        p = jnp.where(_swa_mask(n), jnp.exp(s - lrow), 0.0)
        do = do_ref[...].reshape(G * W, Dh)
        do_bf = do.astype(BF16)
        dp = lax.dot_general(do_bf, vv, (((1,), (1,)), ((), ())), preferred_element_type=F32)
        delta = jnp.sum(do * o_ref[...].reshape(G * W, Dh), axis=-1, keepdims=True)
        dsc = p * (dp - delta)
        ds_bf = dsc.astype(BF16)
        dq_ref[...] = (jnp.dot(ds_bf, kk, preferred_element_type=F32) * scale).astype(BF16).reshape(G, W, Dh)
        dkk = lax.dot_general(ds_bf, q, (((0,), (0,)), ((), ())), preferred_element_type=F32) * scale
        dvv = lax.dot_general(p.astype(BF16), do_bf, (((0,), (0,)), ((), ())), preferred_element_type=F32)
        dsk = -jnp.exp(s_ref[...] - lrow) * delta
        dsk = jnp.broadcast_to(jnp.sum(dsk.reshape(G, W, 1), axis=1), (G, LANE))

        @pl.when(n == 0)
        def _():
            dk_ref[...] = jnp.zeros_like(dk_ref)
            dv_ref[...] = jnp.zeros_like(dv_ref)
            ds_ref[...] = jnp.zeros_like(ds_ref)

        rows = pl.ds(pl.multiple_of(n * W, W), 2 * W)
        dk_ref[rows, :] += dkk
        dv_ref[rows, :] += dvv
        ds_ref[...] += dsk

    qs = pl.BlockSpec((G, W, Dh), lambda g, n: (g, n, 0))
    prev = pl.BlockSpec((None, W, Dh), lambda g, n: (g, jnp.maximum(n - 1, 0), 0))
    cur = pl.BlockSpec((None, W, Dh), lambda g, n: (g, n, 0))
    lsp = pl.BlockSpec((G, W, 1), lambda g, n: (g, n, 0))
    kvo = pl.BlockSpec((None, S + W, Dh), lambda g, n: (g, 0, 0))
    return _call(
        body, name=name, grid=(SWA_KV_HEADS, nb),
        in_specs=[qs, prev, cur, prev, cur, pl.BlockSpec((None, G * W, 1), lambda g, n: (g, 0, 0)), qs, lsp, qs],
        out_specs=[qs, kvo, kvo, pl.BlockSpec((None, G, LANE), lambda g, n: (g, 0, 0))],
        out_shape=[jax.ShapeDtypeStruct((SWA_HEADS, S, Dh), BF16),
                   jax.ShapeDtypeStruct((SWA_KV_HEADS, S + W, Dh), F32),
                   jax.ShapeDtypeStruct((SWA_KV_HEADS, S + W, Dh), F32),
                   jax.ShapeDtypeStruct((SWA_KV_HEADS, G, LANE), F32)],
        compiler_params=_cparams(("parallel", "arbitrary")),
    )(qT, kT, kT, vT, vT, sink_rows, oT, L, doT)


def _adamw(w, g, m, v, *, name, tr=128, by_cols=False):
    L, R, C = w.shape
    split = isinstance(g, (list, tuple))
    HR, HC = _half_shape(R, C, by_cols) if split else (R, C)
    tr, tc = _tile2d(HR, HC, tr)
    nr, nc = HR // tr, HC // tc
    c1 = 1.0 / (1.0 - ADAM_B1 ** ADAM_STEP)
    c2 = 1.0 / (1.0 - ADAM_B2 ** ADAM_STEP)
    ng = 2 * L if split else 1

    def body(*refs):
        w_ref, g_refs, (m_ref, v_ref, go_ref, d_ref, mo_ref, vo_ref) = refs[0], refs[1:1 + ng], refs[1 + ng:]
        if split:
            mine = pl.program_id(1) == lax.axis_index("c")
            g_ = jnp.where(mine, g_refs[0][...], g_refs[1][...])
            for l in range(1, L):
                g_ = jnp.where(pl.program_id(0) == l,
                               jnp.where(mine, g_refs[2 * l][...], g_refs[2 * l + 1][...]), g_)
        else:
            g_ = g_refs[0][...]
        mn = ADAM_B1 * m_ref[...] + (1.0 - ADAM_B1) * g_
        vn = ADAM_B2 * v_ref[...] + (1.0 - ADAM_B2) * (g_ * g_)
        go_ref[...] = g_
        mo_ref[...] = mn
        vo_ref[...] = vn
        d_ref[...] = -ADAM_LR * ((mn * c1) / (jnp.sqrt(vn * c2) + ADAM_EPS) + ADAM_WD * w_ref[...])

    def whole(l, hf, i, j):
        return (l, i, hf * nc + j) if by_cols else (l, hf * nr + i, j)

    row = pl.BlockSpec((None, tr, tc), whole)
    half = pl.BlockSpec((tr, tc), lambda l, hf, i, j: (i, j))
    gs = [h for pair in g for h in pair] if split else [g]
    return _call(
        body, name=name, grid=(L, 2 if split else 1, nr, nc),
        in_specs=[row] + [half if split else row] * ng + [row, row],
        out_specs=[row] * 4, out_shape=[jax.ShapeDtypeStruct((L, R, C), F32)] * 4,
        compiler_params=_cparams(("parallel",) * 4),
    )(w, *gs, m, v)


def _sum2_halves(g4, s4, by_cols, *, name):
    n, R, C = g4.shape
    HR, HC = _half_shape(R, C, by_cols)
    tr, tc = _tile2d(HR, HC)
    nr, nc = HR // tr, HC // tc
    core = lax.axis_index("c").astype(jnp.int32).reshape(1)

    def body(c_ref, g_ref, s_ref, o_ref):
        o_ref[...] = (g_ref[...].astype(F32) + s_ref[...].astype(F32)).astype(BF16)

    def mine(k, i, j, c):
        return (k, i, c[0] * nc + j) if by_cols else (k, c[0] * nr + i, j)

    blk = pl.BlockSpec((None, tr, tc), lambda k, i, j, c: (k, i, j))
    return _call(
        body, name=name,
        grid_spec=pltpu.PrefetchScalarGridSpec(
            num_scalar_prefetch=1, grid=(n, nr, nc),
            in_specs=[pl.BlockSpec((None, tr, tc), mine), blk], out_specs=blk),
        out_shape=jax.ShapeDtypeStruct((n, HR, HC), BF16),
        compiler_params=_cparams(("parallel", "parallel", "parallel")),
    )(core, g4, s4)


def _rowsum(parts, *, name, out_dtype=F32):
    n, R, C = parts.shape
    tr, tc = _tile2d(R, C)

    def body(p_ref, o_ref):
        acc = p_ref[0].astype(F32)
        for i in range(1, n):
            acc = acc + p_ref[i].astype(F32)
        o_ref[...] = acc.astype(out_dtype)

    return _call(
        body, name=name, grid=(R // tr, C // tc),
        in_specs=[pl.BlockSpec((n, tr, tc), lambda i, j: (0, i, j))],
        out_specs=pl.BlockSpec((tr, tc), lambda i, j: (i, j)),
        out_shape=jax.ShapeDtypeStruct((R, C), out_dtype),
        compiler_params=_cparams(("parallel", "parallel")),
    )(parts)


def _where_am_i():
    x, y, c = lax.axis_index("x"), lax.axis_index("y"), lax.axis_index("c")
    chips = [(1 - x, y), (x, 1 - y), (1 - x, 1 - y)]
    return x, y, c, chips


def _half_idx(rows, cols, by_cols, which):
    if by_cols:
        hc = cols // 2
        return (slice(None), pl.ds(pl.multiple_of(which * hc, LANE), hc))
    hr = rows // 2
    return (pl.ds(pl.multiple_of(which * hr, 16), hr), slice(None))


def _half_shape(rows, cols, by_cols):
    return (rows, cols // 2) if by_cols else (rows // 2, cols)


def _all_gather_shards(shards, by_cols, *, name):
    n = len(shards)

    def body(*refs):
        ins, outs = refs[:n], refs[n:2 * n]
        send, recv = refs[2 * n:]
        x, y, c, chips = _where_am_i()
        me = 2 * x + y
        sibling = (x, y, 1 - c)

        def half(i, which):
            return _half_idx(*shards[i].shape, by_cols[i], which)

        def cp(i, k, src, dst, to):
            return pltpu.make_async_remote_copy(src_ref=src, dst_ref=dst, send_sem=send.at[i, k],
                                                recv_sem=recv.at[i, k], device_id=to, device_id_type=MESH)

        first = []
        for i in range(n):
            for k, (px, py) in enumerate(chips):
                d = cp(i, k, ins[i].at[half(i, c)], outs[i].at[(me,) + half(i, c)], (px, py, c))
                d.start()
                first.append(d)
        passed = []
        for i in range(n):
            for k, (px, py) in enumerate(chips):
                blk = outs[i].at[(2 * px + py,) + half(i, c)]
                cp(i, k, blk, blk, (px, py, c)).wait_recv()
                d = cp(i, 3 + k, blk, blk, sibling)
                d.start()
                passed.append(d)
        for i in range(n):
            for k, (px, py) in enumerate(chips):
                blk = outs[i].at[(2 * px + py,) + half(i, 1 - c)]
                cp(i, 3 + k, blk, blk, sibling).wait_recv()
        for d in first + passed:
            d.wait_send()

    got = _call(
        body, name=name, in_specs=[ANY] * n, out_specs=[ANY] * n,
        out_shape=[jax.ShapeDtypeStruct((N_CHIPS,) + s.shape, s.dtype) for s in shards],
        scratch_shapes=[pltpu.SemaphoreType.DMA((n, 6)), pltpu.SemaphoreType.DMA((n, 6))],
    )(*shards)
    me = 2 * lax.axis_index("x") + lax.axis_index("y")
    return [lax.dynamic_update_slice_in_dim(g, s[None], me, axis=0) for g, s in zip(got, shards)]


HBM_SPEC = pl.BlockSpec(memory_space=pltpu.HBM)
SEM_SPEC = pl.BlockSpec(memory_space=pltpu.SEMAPHORE)
DATAFLOW = pltpu.SideEffectType.DATAFLOW_SIDE_EFFECTING


def _chip_exchange_refs(kind, shards_shape, by_cols, src, land, i, chip_k, c, me):
    if kind == 'gather':
        half = _half_idx(*shards_shape, by_cols, c)
        return src.at[half], land.at[(me,) + half], land.at[(chip_k,) + half]
    return src.at[chip_k], land.at[me], land.at[chip_k]


def _chip_exchange_start(kind, srcs, by_cols, *, name):
    n = len(srcs)
    land_shapes = [((N_CHIPS,) + s.shape) if kind == 'gather' else s.shape for s in srcs]

    def body(*refs):
        src_refs, land_refs = refs[:n], refs[n:2 * n]
        send, recv = refs[2 * n], refs[2 * n + 1]
        token = refs[-1]
        x, y, c, chips = _where_am_i()
        me = 2 * x + y
        for i in range(n):
            for k, (px, py) in enumerate(chips):
                s, d, _ = _chip_exchange_refs(kind, srcs[i].shape, by_cols[i], src_refs[i], land_refs[i], i,
                                              2 * px + py, c, me)
                pltpu.make_async_remote_copy(src_ref=s, dst_ref=d, send_sem=send.at[3 * i + k],
                                             recv_sem=recv.at[3 * i + k], device_id=(px, py, c),
                                             device_id_type=MESH).start()
        token[...] = jnp.zeros_like(token)

    lands = [pltpu.with_memory_space_constraint(lax.empty(sh, s.dtype), pltpu.HBM) for sh, s in zip(land_shapes, srcs)]
    outs = _call(
        body, name=name,
        out_shape=(pltpu.SemaphoreType.DMA((3 * n,)), pltpu.SemaphoreType.DMA((3 * n,)),
                   *[pltpu.HBM(s.shape, s.dtype) for s in srcs],
                   *[pltpu.HBM(sh, s.dtype) for sh, s in zip(land_shapes, srcs)],
                   jax.ShapeDtypeStruct((8, LANE), F32)),
        in_specs=[HBM_SPEC] * (2 * n),
        out_specs=(SEM_SPEC, SEM_SPEC, *([HBM_SPEC] * (2 * n)), pl.BlockSpec(memory_space=pltpu.VMEM)),
        input_output_aliases={j: 2 + j for j in range(2 * n)},
        compiler_params=pltpu.CompilerParams(has_side_effects=DATAFLOW),
    )(*[pltpu.with_memory_space_constraint(s, pltpu.HBM) for s in srcs], *lands)
    return outs[0], outs[1], list(outs[2:2 + n]), list(outs[2 + n:2 + 2 * n]), outs[-1]


def _chip_exchange_wait(kind, send, recv, srcs, lands, by_cols, after, *, name):
    n = len(srcs)

    def body(*refs):
        src_refs, land_refs = refs[:n], refs[n:2 * n]
        send_r, recv_r = refs[2 * n], refs[2 * n + 1]
        x, y, c, chips = _where_am_i()
        me = 2 * x + y
        for i in range(n):
            for k, (px, py) in enumerate(chips):
                s, _, d = _chip_exchange_refs(kind, srcs[i].shape, by_cols[i], src_refs[i], land_refs[i], i,
                                              2 * px + py, c, me)
                cp = pltpu.make_async_remote_copy(src_ref=s, dst_ref=d, send_sem=send_r.at[3 * i + k],
                                                  recv_sem=recv_r.at[3 * i + k], device_id=(px, py, c),
                                                  device_id_type=MESH)
                cp.wait_send()
                cp.wait_recv()

    outs = _call(
        body, name=name,
        out_shape=(*[pltpu.HBM(s.shape, s.dtype) for s in srcs], *[pltpu.HBM(l.shape, l.dtype) for l in lands]),
        in_specs=[HBM_SPEC] * (2 * n) + [SEM_SPEC, SEM_SPEC, ANY],
        out_specs=tuple([HBM_SPEC] * (2 * n)),
        input_output_aliases={j: j for j in range(2 * n)},
        compiler_params=pltpu.CompilerParams(has_side_effects=DATAFLOW),
    )(*srcs, *lands, send, recv, after)
    return list(outs[n:])


def _sibling_pass_gathered(lands, shard_shapes, by_cols, *, name):
    n = len(lands)

    def body(*refs):
        outs = refs[n:2 * n]
        send, recv = refs[2 * n:]
        x, y, c, chips = _where_am_i()
        sibling = (x, y, 1 - c)
        cps = []
        for i in range(n):
            for k, (px, py) in enumerate(chips):
                blk = outs[i].at[(2 * px + py,) + _half_idx(*shard_shapes[i], by_cols[i], c)]
                d = pltpu.make_async_remote_copy(src_ref=blk, dst_ref=blk, send_sem=send.at[i, k],
                                                 recv_sem=recv.at[i, k], device_id=sibling, device_id_type=MESH)
                d.start()
                cps.append(d)
        for i in range(n):
            for k, (px, py) in enumerate(chips):
                blk = outs[i].at[(2 * px + py,) + _half_idx(*shard_shapes[i], by_cols[i], 1 - c)]
                pltpu.make_async_remote_copy(src_ref=blk, dst_ref=blk, send_sem=send.at[i, k], recv_sem=recv.at[i, k],
                                             device_id=sibling, device_id_type=MESH).wait_recv()
        for d in cps:
            d.wait_send()

    return _call(
        body, name=name, in_specs=[ANY] * n, out_specs=[ANY] * n,
        out_shape=[jax.ShapeDtypeStruct(l.shape, l.dtype) for l in lands],
        input_output_aliases={j: j for j in range(n)},
        scratch_shapes=[pltpu.SemaphoreType.DMA((n, 3)), pltpu.SemaphoreType.DMA((n, 3))],
    )(*lands)


def _own_slot(lands, owns):
    me = 2 * lax.axis_index("x") + lax.axis_index("y")
    return [lax.dynamic_update_slice_in_dim(g, s, me, axis=0) for g, s in zip(lands, owns)]


def _sibling_send_halves(grads, by_cols, *, name):
    n = len(grads)

    def body(*refs):
        ins, outs = refs[:n], refs[n:2 * n]
        send, recv = refs[2 * n:]
        x, y, c, _ = _where_am_i()
        sibling = (x, y, 1 - c)
        cps = []
        for i in range(n):
            src = ins[i].at[(slice(None),) + _half_idx(*grads[i].shape[1:], by_cols[i], 1 - c)]
            d = pltpu.make_async_remote_copy(src_ref=src, dst_ref=outs[i], send_sem=send.at[i],
                                             recv_sem=recv.at[i], device_id=sibling, device_id_type=MESH)
            d.start()
            cps.append(d)
        for d in cps:
            d.wait()

    return _call(
        body, name=name, in_specs=[ANY] * n, out_specs=[ANY] * n,
        out_shape=[jax.ShapeDtypeStruct((N_CHIPS,) + _half_shape(*g.shape[1:], bc), g.dtype)
                   for g, bc in zip(grads, by_cols)],
        scratch_shapes=[pltpu.SemaphoreType.DMA((n,)), pltpu.SemaphoreType.DMA((n,))],
    )(*grads)


def _scatter_to_chips(parts, *, name):
    n = len(parts)

    def body(*refs):
        ins, outs = refs[:n], refs[n:2 * n]
        send, recv = refs[2 * n:]
        x, y, c, chips = _where_am_i()
        me = 2 * x + y
        cps = []
        for i in range(n):
            for k, (px, py) in enumerate(chips):
                d = pltpu.make_async_remote_copy(
                    src_ref=ins[i].at[2 * px + py], dst_ref=outs[i].at[me], send_sem=send.at[i, k],
                    recv_sem=recv.at[i, k], device_id=(px, py, c), device_id_type=MESH)
                d.start()
                cps.append((d, i, k, px, py))
        for d, i, k, px, py in cps:
            blk = outs[i].at[2 * px + py]
            pltpu.make_async_remote_copy(src_ref=blk, dst_ref=blk, send_sem=send.at[i, k], recv_sem=recv.at[i, k],
                                         device_id=(px, py, c), device_id_type=MESH).wait_recv()
        for d, *_ in cps:
            d.wait_send()

    got = _call(
        body, name=name, in_specs=[ANY] * n, out_specs=[ANY] * n,
        out_shape=[jax.ShapeDtypeStruct(p.shape, p.dtype) for p in parts],
        scratch_shapes=[pltpu.SemaphoreType.DMA((n, 3)), pltpu.SemaphoreType.DMA((n, 3))],
    )(*parts)
    me = 2 * lax.axis_index("x") + lax.axis_index("y")
    return [lax.dynamic_update_slice_in_dim(g, lax.dynamic_slice_in_dim(p, me, 1, axis=0), me, axis=0)
            for g, p in zip(got, parts)]


def _sibling_join_halves(halves, *, name):
    n = len(halves)

    def body(*refs):
        ins, outs = refs[:n], refs[n:2 * n]
        send, recv = refs[2 * n:]
        x, y, c, _ = _where_am_i()
        sibling = (x, y, 1 - c)
        cps = []
        for i in range(n):
            d = pltpu.make_async_remote_copy(src_ref=ins[i], dst_ref=outs[i], send_sem=send.at[i],
                                             recv_sem=recv.at[i], device_id=sibling, device_id_type=MESH)
            d.start()
            cps.append(d)
        for d in cps:
            d.wait()

    return _call(
        body, name=name, in_specs=[ANY] * n, out_specs=[ANY] * n,
        out_shape=[jax.ShapeDtypeStruct(h.shape, h.dtype) for h in halves],
        scratch_shapes=[pltpu.SemaphoreType.DMA((n,)), pltpu.SemaphoreType.DMA((n,))],
    )(*halves)


def _all_reduce_small(v, *, name):
    R, C = v.shape

    def body(v_ref, o_ref, sib, slots, send, recv):
        x, y, c, chips = _where_am_i()
        me = 2 * x + y
        sibling = (x, y, 1 - c)
        d = pltpu.make_async_remote_copy(src_ref=v_ref, dst_ref=sib, send_sem=send.at[0], recv_sem=recv.at[0],
                                         device_id=sibling, device_id_type=MESH)
        d.start()
        d.wait()
        slots[me] = v_ref[...] + sib[...]
        cps = []
        for k, (px, py) in enumerate(chips):
            d = pltpu.make_async_remote_copy(src_ref=slots.at[me], dst_ref=slots.at[me], send_sem=send.at[1 + k],
                                             recv_sem=recv.at[1 + k], device_id=(px, py, c), device_id_type=MESH)
            d.start()
            cps.append(d)
        for k, (px, py) in enumerate(chips):
            blk = slots.at[2 * px + py]
            pltpu.make_async_remote_copy(src_ref=blk, dst_ref=blk, send_sem=send.at[1 + k], recv_sem=recv.at[1 + k],
                                         device_id=(px, py, c), device_id_type=MESH).wait_recv()
        for d in cps:
            d.wait_send()
        o_ref[...] = (slots[0] + slots[1]) + (slots[2] + slots[3])

    vm = pl.BlockSpec(memory_space=pltpu.VMEM)
    return _call(
        body, name=name, in_specs=[vm], out_specs=vm,
        out_shape=jax.ShapeDtypeStruct((R, C), F32),
        scratch_shapes=[pltpu.VMEM((R, C), F32), pltpu.VMEM((N_CHIPS, R, C), F32),
                        pltpu.SemaphoreType.DMA((4,)), pltpu.SemaphoreType.DMA((4,))],
        compiler_params=pltpu.CompilerParams(vmem_limit_bytes=VMEM_LIMIT),
    )(v)


def _cols_from_shards(g):
    return jnp.transpose(g, (1, 0, 2)).reshape(g.shape[1], -1)


def _shards_from_cols(w):
    R, C4 = w.shape
    return jnp.transpose(w.reshape(R, N_CHIPS, C4 // N_CHIPS), (1, 0, 2))


def _block_diag(t):
    G, a, b = t.shape
    eye = jnp.eye(G, dtype=t.dtype)
    return (t[:, :, None, :] * eye[:, None, :, None]).reshape(G * a, G * b)


def _diag_blocks(xm, G):
    a, b = xm.shape[0] // G, xm.shape[1] // G
    idx = jnp.arange(G)
    return xm.reshape(G, a, G, b)[idx, :, idx, :]


def _pack(arrs):
    flat = []
    for a in arrs:
        f = a.reshape(-1).astype(F32)
        flat.append(jnp.pad(f, (0, _rup(f.shape[0], LANE) - f.shape[0])))
    v = jnp.concatenate(flat)
    rows = _rup(v.shape[0] // LANE, 8)
    v = jnp.pad(v, (0, rows * LANE - v.shape[0]))
    return v.reshape(rows, LANE)


def _unpack(v, shapes):
    flat = v.reshape(-1)
    out, off = [], 0
    for s in shapes:
        n = int(np.prod(s))
        out.append(flat[off:off + n].reshape(s))
        off += _rup(n, LANE)
    return out


def _ffn_fwd(x, Wup, Wdn, cw, cb, tag):
    h = _mm(x, Wup, 'nt', bmode='bo', tm=512, tn=4096, name=f"ffn_up_{tag}")
    a = _act_fwd(h, cw, cb, name=f"ffn_act_{tag}")
    f = _mm(a, Wdn, 'nn', bmode='abr', tm=512, tn=1024, tk=4096, name=f"ffn_down_{tag}")
    return f, h, a


def _ffn_bwd(df, x, h, a, Wup, Wdn, cw, cb, tag):
    da = _mm(df, Wdn, 'nt', bmode='bo', tm=512, tn=4096, name=f"ffn_da_{tag}")
    dWdn = _mm(a, df, 'tn', bmode='ao', tm=4096, tn=512, name=f"ffn_dwdn_{tag}", out_dtype=BF16)
    dh, dcw, dcb = _act_bwd(h, da, cw, cb, name=f"ffn_actb_{tag}")
    dx = _mm(dh, Wup, 'nn', bmode='abr', tm=512, tn=1024, tk=4096, name=f"ffn_dx_{tag}")
    dWup = _mm(dh, x, 'tn', bmode='ao', tm=4096, tn=512, name=f"ffn_dwup_{tag}", out_dtype=BF16)
    return dx, dWup, dWdn, dcw, dcb


def kernel(x, positions, ev_w_in, ev_b_f, ev_lambda_re, ev_lambda_im, ev_log_step, ev_ssm_b_re, ev_ssm_b_im, ev_ssm_c_re, ev_ssm_c_im, ev_ssm_d, ev_w_glu, ev_w_out, od_w_in, od_sinks, od_w_out, ln_mix_g, ln_mix_b, ffn_w_up, ffn_conv_w, ffn_conv_b, ffn_w_down, ln_ffn_g, ln_ffn_b, loss_target, m_ev_w_in, m_ev_b_f, m_ev_lambda_re, m_ev_lambda_im, m_ev_log_step, m_ev_ssm_b_re, m_ev_ssm_b_im, m_ev_ssm_c_re, m_ev_ssm_c_im, m_ev_ssm_d, m_ev_w_glu, m_ev_w_out, m_od_w_in, m_od_sinks, m_od_w_out, m_ln_mix_g, m_ln_mix_b, m_ffn_w_up, m_ffn_conv_w, m_ffn_conv_b, m_ffn_w_down, m_ln_ffn_g, m_ln_ffn_b, v_ev_w_in, v_ev_b_f, v_ev_lambda_re, v_ev_lambda_im, v_ev_log_step, v_ev_ssm_b_re, v_ev_ssm_b_im, v_ev_ssm_c_re, v_ev_ssm_c_im, v_ev_ssm_d, v_ev_w_glu, v_ev_w_out, v_od_w_in, v_od_sinks, v_od_w_out, v_ln_mix_g, v_ln_mix_b, v_ffn_w_up, v_ffn_conv_w, v_ffn_conv_b, v_ffn_w_down, v_ln_ffn_g, v_ln_ffn_b):
    W = dict(ev_w_in=ev_w_in, ev_b_f=ev_b_f, ev_lambda_re=ev_lambda_re, ev_lambda_im=ev_lambda_im, ev_log_step=ev_log_step, ev_ssm_b_re=ev_ssm_b_re, ev_ssm_b_im=ev_ssm_b_im, ev_ssm_c_re=ev_ssm_c_re, ev_ssm_c_im=ev_ssm_c_im, ev_ssm_d=ev_ssm_d, ev_w_glu=ev_w_glu, ev_w_out=ev_w_out, od_w_in=od_w_in, od_sinks=od_sinks, od_w_out=od_w_out, ln_mix_g=ln_mix_g, ln_mix_b=ln_mix_b, ffn_w_up=ffn_w_up, ffn_conv_w=ffn_conv_w, ffn_conv_b=ffn_conv_b, ffn_w_down=ffn_w_down, ln_ffn_g=ln_ffn_g, ln_ffn_b=ln_ffn_b)
    Mo = dict(ev_w_in=m_ev_w_in, ev_b_f=m_ev_b_f, ev_lambda_re=m_ev_lambda_re, ev_lambda_im=m_ev_lambda_im, ev_log_step=m_ev_log_step, ev_ssm_b_re=m_ev_ssm_b_re, ev_ssm_b_im=m_ev_ssm_b_im, ev_ssm_c_re=m_ev_ssm_c_re, ev_ssm_c_im=m_ev_ssm_c_im, ev_ssm_d=m_ev_ssm_d, ev_w_glu=m_ev_w_glu, ev_w_out=m_ev_w_out, od_w_in=m_od_w_in, od_sinks=m_od_sinks, od_w_out=m_od_w_out, ln_mix_g=m_ln_mix_g, ln_mix_b=m_ln_mix_b, ffn_w_up=m_ffn_w_up, ffn_conv_w=m_ffn_conv_w, ffn_conv_b=m_ffn_conv_b, ffn_w_down=m_ffn_w_down, ln_ffn_g=m_ln_ffn_g, ln_ffn_b=m_ln_ffn_b)
    Vo = dict(ev_w_in=v_ev_w_in, ev_b_f=v_ev_b_f, ev_lambda_re=v_ev_lambda_re, ev_lambda_im=v_ev_lambda_im, ev_log_step=v_ev_log_step, ev_ssm_b_re=v_ev_ssm_b_re, ev_ssm_b_im=v_ev_ssm_b_im, ev_ssm_c_re=v_ev_ssm_c_re, ev_ssm_c_im=v_ev_ssm_c_im, ev_ssm_d=v_ev_ssm_d, ev_w_glu=v_ev_w_glu, ev_w_out=v_ev_w_out, od_w_in=v_od_w_in, od_sinks=v_od_sinks, od_w_out=v_od_w_out, ln_mix_g=v_ln_mix_g, ln_mix_b=v_ln_mix_b, ffn_w_up=v_ffn_w_up, ffn_conv_w=v_ffn_conv_w, ffn_conv_b=v_ffn_conv_b, ffn_w_down=v_ffn_w_down, ln_ffn_g=v_ln_ffn_g, ln_ffn_b=v_ln_ffn_b)
    names = list(W.keys())
    big = ['ev_w_in', 'ev_w_glu', 'ev_w_out', 'od_w_in', 'od_w_out', 'ffn_w_up', 'ffn_w_down']

    S, D = x.shape[1], x.shape[2]
    x0 = x.reshape(S, D)
    tgt = loss_target.reshape(S, D)
    G, Pn, Cg = SSM_GROUPS, SSM_STATE, SSM_GROUP
    Fs = ffn_w_up.shape[2]
    FP = Fs
    Rd = ffn_w_down.shape[1]
    EIN = N_CHIPS * ev_w_in.shape[2]

    def as2d(a):
        return a.reshape(-1, a.shape[-1])

    cwl = ffn_conv_w.reshape(-1)
    cw_rows = _rup(_rup(cwl.shape[0], LANE) // LANE, 32)
    cw_pad = jnp.pad(cwl, (0, cw_rows * LANE - cwl.shape[0])).reshape(cw_rows, LANE)
    transposed = ('ev_w_in', 'ffn_w_up')

    def view(n, a):
        return jnp.transpose(a, (0, 2, 1)) if n in transposed else a

    Wv = {n: view(n, W[n]) for n in big}
    big_e = [(n, l) for n in big for l in range(W[n].shape[0])]
    split_cols = {e: (Wv[e[0]].shape[1] // 2) % 16 != 0 for e in big_e}
    shard16 = {e: Wv[e[0]][e[1]].astype(BF16) for e in big_e}
    grp_now = [e for e in big_e if e[0].startswith('ev_')]
    grp_ffn0 = [('ffn_w_up', 0), ('ffn_w_down', 0)]
    grp_l1 = [('od_w_in', 0), ('od_w_out', 0), ('ffn_w_up', 1), ('ffn_w_down', 1)]
    gw = dict(zip(grp_now, _all_gather_shards([shard16[e] for e in grp_now], [split_cols[e] for e in grp_now],
                                              name="ag_l0")))
    src_ffn0 = [shard16[e] for e in grp_ffn0] + [cw_pad]
    src_l1 = [shard16[e] for e in grp_l1]
    gw, src_ffn0 = lax.optimization_barrier((gw, src_ffn0))
    cols_ffn0 = [split_cols[e] for e in grp_ffn0] + [False]
    cols_l1 = [split_cols[e] for e in grp_l1]
    ag_ffn0 = _chip_exchange_start('gather', src_ffn0, cols_ffn0, name="ag_ffn0_start")
    tok, src_l1 = lax.optimization_barrier((ag_ffn0[4], src_l1))
    ag_l1 = _chip_exchange_start('gather', src_l1, cols_l1, name="ag_l1_start")
    _, _, x0 = lax.optimization_barrier((tok, ag_l1[4], x0))

    def finish_gather(started, srcs, cols, after, tag):
        send, recv, thru, lands, _ = started
        lands = _chip_exchange_wait('gather', send, recv, thru, lands, cols, after, name=f"ag_{tag}_wait")
        lands = _sibling_pass_gathered(lands, [s.shape for s in srcs], cols, name=f"ag_{tag}_pass")
        return _own_slot(lands, [s[None] for s in srcs])

    gw.update({n: gw[(n, 0)] for n in big if (n, 0) in gw and W[n].shape[0] == 1})
    w_in_t = gw['ev_w_in'].reshape(EIN, D)
    qkv_w = 3 * FOX_WIDTH
    WmainT = jnp.concatenate([w_in_t[:qkv_w], w_in_t[qkv_w + FOX_HEADS:]], axis=0)
    WfT = jnp.pad(w_in_t[qkv_w:qkv_w + FOX_HEADS], ((0, LANE - FOX_HEADS), (0, 0)))
    Wglu = _cols_from_shards(gw['ev_w_glu'])
    Wout_ev = gw['ev_w_out'].reshape(D, D)
    cbs = [ffn_conv_b[l].reshape(N_CHIPS, Fs) for l in range(DEPTH)]

    lam_r, lam_i = ev_lambda_re[0], ev_lambda_im[0]
    lstep = ev_log_step[0].reshape(G, 1)
    a_re, a_im, g_re, g_im = _s5_disc_fwd(lam_r, lam_i, lstep, name="s5_disc")
    b_re2, b_im2 = ev_ssm_b_re[0].reshape(G * Pn, Cg), ev_ssm_b_im[0].reshape(G * Pn, Cg)
    g_re1, g_im1 = g_re.reshape(G * Pn, 1), g_im.reshape(G * Pn, 1)
    bb_re, bb_im = _s5_bb_fwd(g_re1, g_im1, b_re2, b_im2, name="s5_bb")
    bbt = jnp.stack([jnp.transpose(b.reshape(G, Pn, Cg), (0, 2, 1)).reshape(G * Cg, Pn) for b in (bb_re, bb_im)])
    BB = _diag_expand(bbt, Cg, Pn, name="s5_bb_dense")
    cct = jnp.stack([jnp.transpose(ev_ssm_c_re[0], (0, 2, 1)).reshape(G * Pn, Cg),
                     jnp.transpose(-ev_ssm_c_im[0], (0, 2, 1)).reshape(G * Pn, Cg)])
    CC = _diag_expand(cct, Pn, Cg, name="s5_cc_dense")
    a_cat = jnp.stack([a_re.reshape(1, G * Pn), a_im.reshape(1, G * Pn)])
    dskip = ev_ssm_d[0].reshape(1, SSM_WIDTH)

    P = _mm(x0, WmainT, 'nt', name="ev_proj")
    fl = _mm(x0, WfT, 'nt', name="ev_proj_f")
    bf_pad = jnp.pad(ev_b_f.reshape(1, FOX_HEADS), ((0, 0), (0, LANE - FOX_HEADS)))
    cgate, sgate = _gate_fwd(fl, bf_pad, name="fox_gate")
    ccol = jnp.transpose(cgate[:, :FOX_HEADS]).reshape(FOX_HEADS, S, 1)
    crow = jnp.transpose(cgate[:, :FOX_HEADS]).reshape(FOX_HEADS, 1, S)
    fox, lse = _fox_fwd(P, ccol, crow, name="fox_fwd")
    u_s5 = P[:, qkv_w:]
    bu = _mm(u_s5, BB, 'nn', bmode='bo', name="s5_bu")
    hh = _s5_scan_fwd(bu, a_cat, name="s5_scan")
    yc = _mm(hh, CC, 'nn', bmode='abr', name="s5_y")
    y_s5, yg = _s5_out_fwd(yc, P, dskip, name="s5_out")
    z = _mm(yg, Wglu, 'nn', name="s5_glu_proj")
    ssm = _glu_fwd(z, name="s5_glu")
    cat = jnp.concatenate([fox.astype(BF16), ssm], axis=1)
    mix0 = _mm(cat, Wout_ev, 'nn', name="ev_out")
    x1, xh1, rs1 = _add_ln_fwd(x0, mix0, ln_mix_g[0], ln_mix_b[0], name="ln_mix0")
    got = finish_gather(ag_ffn0, src_ffn0, cols_ffn0, x1, "ffn0")
    gw.update(zip(grp_ffn0, got[:-1]))
    cw_all = got[-1].reshape(N_CHIPS, -1)[:, :cwl.shape[0]].reshape(N_CHIPS, DEPTH, 3, Fs)
    cws = [cw_all[:, l] for l in range(DEPTH)]
    Wup = {0: gw[('ffn_w_up', 0)]}
    Wdn = {0: gw[('ffn_w_down', 0)].reshape(2, Fs, D)}
    f0, hf0, af0 = _ffn_fwd(x1, Wup[0], Wdn[0], cws[0], cbs[0], "l0")
    x2, xh2, rs2 = _add_ln_fwd(x1, f0, ln_ffn_g[0], ln_ffn_b[0], name="ln_ffn0")

    gw.update(zip(grp_l1, finish_gather(ag_l1, src_l1, cols_l1, x2, "l1")))
    Wodin = _cols_from_shards(gw[('od_w_in', 0)])
    Wodout = gw[('od_w_out', 0)].reshape(D, D)
    Wup[1] = gw[('ffn_w_up', 1)]
    Wdn[1] = gw[('ffn_w_down', 1)].reshape(2, Fs, D)
    QW, KW = SWA_HEADS * SWA_HEAD_DIM, SWA_KV_HEADS * SWA_HEAD_DIM
    P1 = _mm(x2, Wodin, 'nn', name="od_proj")
    tabs = _rope_tables(positions.reshape(S, 1).astype(F32), name="rope_tables")
    qr = _rope_apply(P1, tabs, col0=0, width=QW, inverse=False, name="rope_q", out_dtype=BF16)
    kr = _rope_apply(P1, tabs, col0=QW, width=KW, inverse=False, name="rope_k", out_dtype=BF16)

    def heads(a2, nh):
        return jnp.transpose(a2.reshape(S, nh, SWA_HEAD_DIM), (1, 0, 2))

    def unheads(a3):
        return jnp.transpose(a3, (1, 0, 2)).reshape(S, -1)

    qT, kT = heads(qr, SWA_HEADS), heads(kr, SWA_KV_HEADS)
    vT = heads(P1[:, QW + KW:].astype(BF16), SWA_KV_HEADS)
    sink_rows = jnp.broadcast_to(od_sinks[0].reshape(SWA_KV_HEADS, SWA_GROUPS, 1, 1),
                                 (SWA_KV_HEADS, SWA_GROUPS, SWA_WINDOW, 1)).reshape(SWA_KV_HEADS, -1, 1)
    oT, Lsw = _swa_fwd(qT, kT, vT, sink_rows, name="swa_fwd")
    o_sw = unheads(oT).astype(BF16)
    mix1 = _mm(o_sw, Wodout, 'nn', name="od_out")
    x3, xh3, rs3 = _add_ln_fwd(x2, mix1, ln_mix_g[1], ln_mix_b[1], name="ln_mix1")
    f1, hf1, af1 = _ffn_fwd(x3, Wup[1], Wdn[1], cws[1], cbs[1], "l1")
    x4, xh4, rs4 = _add_ln_fwd(x3, f1, ln_ffn_g[1], ln_ffn_b[1], name="ln_ffn1")
    dy, loss_part = _loss_grad(x4, tgt, name="loss")

    dz4, dg_ffn1, db_ffn1 = _ln_bwd(dy, None, xh4, rs4, ln_ffn_g[1], name="lnb_ffn1")
    dx3f, dWup1, dWdn1, dcw1, dcb1 = _ffn_bwd(dz4, x3, hf1, af1, Wup[1], Wdn[1], cws[1], cbs[1], "l1")
    dz3, dg_mix1, db_mix1 = _ln_bwd(dz4, dx3f, xh3, rs3, ln_mix_g[1], name="lnb_mix1")
    do_sw = _mm(dz3, Wodout, 'nt', name="od_out_dx")
    dWodout = _mm(o_sw, dz3, 'tn', name="od_out_dw", out_dtype=BF16)
    doT = heads(do_sw, SWA_HEADS)
    dqT, dkT, dvT, dsink = _swa_bwd(qT, kT, vT, sink_rows, oT, Lsw, doT, name="swa_bwd")
    dq1 = _rope_apply(unheads(dqT), tabs, col0=0, width=QW, inverse=True, name="rope_dq", out_dtype=BF16)
    dk1 = _rope_apply(unheads(dkT[:, SWA_WINDOW:]), tabs, col0=0, width=KW, inverse=True, name="rope_dk",
                      out_dtype=BF16)
    dP1 = jnp.concatenate([dq1, dk1, unheads(dvT[:, SWA_WINDOW:]).astype(BF16)], axis=1)
    dx2m = _mm(dP1, Wodin, 'nt', name="od_proj_dx")
    dWodin = _mm(x2, dP1, 'tn', name="od_proj_dw", out_dtype=BF16)

    def rs_begin(entries, grads, tag):
        cols = [split_cols[e] for e in entries]
        sib = _sibling_send_halves(grads, cols, name=f"rs_{tag}_sibling")
        return [_sum2_halves(g4, s4, bc, name=f"rs_sum2_{n}{l}")
                for (n, l), g4, s4, bc in zip(entries, grads, sib, cols)]

    def own_parts(parts):
        me = 2 * lax.axis_index("x") + lax.axis_index("y")
        return [lax.dynamic_slice_in_dim(p, me, 1, axis=0) for p in parts]

    part_l1 = rs_begin(grp_l1, [_shards_from_cols(dWodin), dWodout.reshape(N_CHIPS, D // N_CHIPS, D), dWup1,
                                dWdn1.reshape(N_CHIPS, Rd, D)], "l1")
    rs_l1 = _chip_exchange_start('scatter', part_l1, [False] * len(part_l1), name="rs_l1_start")
    _, dx2m = lax.optimization_barrier((rs_l1[4], dx2m))

    dz2, dg_ffn0, db_ffn0 = _ln_bwd(dz3, dx2m, xh2, rs2, ln_ffn_g[0], name="lnb_ffn0")
    dx1f, dWup0, dWdn0, dcw0, dcb0 = _ffn_bwd(dz2, x1, hf0, af0, Wup[0], Wdn[0], cws[0], cbs[0], "l0")
    part_ffn0 = rs_begin(grp_ffn0, [dWup0, dWdn0.reshape(N_CHIPS, Rd, D)], "ffn0")
    rs_ffn0 = _chip_exchange_start('scatter', part_ffn0, [False] * len(part_ffn0), name="rs_ffn0_start")
    _, dx1f = lax.optimization_barrier((rs_ffn0[4], dx1f))
    dz1, dg_mix0, db_mix0 = _ln_bwd(dz2, dx1f, xh1, rs1, ln_mix_g[0], name="lnb_mix0")
    dcat = _mm(dz1, Wout_ev, 'nt', name="ev_out_dx")
    dWout_ev = _mm(cat, dz1, 'tn', name="ev_out_dw", out_dtype=BF16)
    dz = _glu_bwd(z, dcat, name="s5_glu_bwd")
    dyg = _mm(dz, Wglu, 'nt', name="s5_glu_dx")
    dWglu = _mm(yg, dz, 'tn', name="s5_glu_dw", out_dtype=BF16)
    dy_s5, du_dir, dD = _s5_out_bwd(dyg, y_s5, P, dskip, name="s5_out_bwd")
    dhh = _mm(dy_s5, CC, 'nt', bmode='bo', name="s5_y_dx")
    dCC = _mm(hh, dy_s5, 'tn', bmode='ao', name="s5_y_dw")
    lam, da_s5 = _s5_scan_bwd(dhh, hh, a_cat, name="s5_scan_bwd")
    du_bu = _mm(lam, BB, 'nt', bmode='abr', name="s5_bu_dx")
    dBB = _mm(u_s5, lam, 'tn', bmode='bo', name="s5_bu_dw")
    du = _combine([du_dir, du_bu], [1.0, 1.0], name="s5_du", out_dtype=BF16)
    dq0, dk0, dv0, dccol, dcrow = _fox_bwd(P, ccol, crow, fox, lse, dcat, name="fox_bwd")
    dc = jnp.transpose((dccol.reshape(FOX_HEADS, S) - dcrow.reshape(FOX_HEADS, S)))
    dc = jnp.pad(dc, ((0, 0), (0, LANE - FOX_HEADS)))
    dfl, dbf = _gate_bwd(dc, sgate, name="fox_gate_bwd")
    dP = jnp.concatenate([dq0, dk0, dv0, du], axis=1)
    dx0a = _mm(dP, WmainT, 'nn', name="ev_proj_dx")
    dx0b = _mm(dfl, WfT, 'nn', name="ev_proj_f_dx")
    dWmainT = _mm(dP, x0, 'tn', tm=1024, tn=1024, name="ev_proj_dw", out_dtype=BF16)
    dWfT = _mm(dfl, x0, 'tn', name="ev_proj_f_dw", out_dtype=BF16)
    grad_x = _combine([dz1, dx0a, dx0b], [ALPHA, 1.0, 1.0], name="grad_x")

    dbbt = _diag_extract(dBB, Cg, Pn, name="s5_bb_diag")
    dcct = _diag_extract(dCC, Pn, Cg, name="s5_cc_diag")
    dbb_re = jnp.transpose(dbbt[0].reshape(G, Cg, Pn), (0, 2, 1)).reshape(G * Pn, Cg)
    dbb_im = jnp.transpose(dbbt[1].reshape(G, Cg, Pn), (0, 2, 1)).reshape(G * Pn, Cg)
    db_re, db_im, dg_re1, dg_im1 = _s5_bb_bwd(g_re1, g_im1, b_re2, b_im2, dbb_re, dbb_im, name="s5_bb_bwd")
    dlam_re, dlam_im, dlstep = _s5_disc_bwd(lam_r, lam_i, lstep, da_s5[0].reshape(G, Pn), da_s5[1].reshape(G, Pn),
                                            dg_re1.reshape(G, Pn), dg_im1.reshape(G, Pn), name="s5_disc_bwd")
    dc_re = jnp.transpose(dcct[0].reshape(G, Pn, Cg), (0, 2, 1))
    dc_im = -jnp.transpose(dcct[1].reshape(G, Pn, Cg), (0, 2, 1))

    def conv_w_full(d0, d1):
        return jnp.stack([jnp.reshape(jnp.transpose(d[:, :, :Fs], (1, 0, 2)), (3, N_CHIPS * Fs)) for d in (d0, d1)])

    def conv_b_full(d0, d1):
        return jnp.stack([jnp.reshape(d[:, 0, :Fs], (N_CHIPS * Fs,)) for d in (d0, d1)])

    small_local = dict(
        ev_b_f=dbf[:, :FOX_HEADS], ev_lambda_re=dlam_re, ev_lambda_im=dlam_im, ev_log_step=dlstep,
        ev_ssm_b_re=db_re, ev_ssm_b_im=db_im, ev_ssm_c_re=dc_re, ev_ssm_c_im=dc_im, ev_ssm_d=dD,
        od_sinks=dsink[:, :, 0],
        ln_mix_g=jnp.concatenate([dg_mix0, dg_mix1]), ln_mix_b=jnp.concatenate([db_mix0, db_mix1]),
        ffn_conv_w=conv_w_full(dcw0, dcw1), ffn_conv_b=conv_b_full(dcb0, dcb1),
        ln_ffn_g=jnp.concatenate([dg_ffn0, dg_ffn1]), ln_ffn_b=jnp.concatenate([db_ffn0, db_ffn1]))
    small = list(small_local.keys())
    red = _all_reduce_small(_pack([small_local[n] for n in small] + [loss_part]), name="ar_small")
    full_shapes = [W[n].shape if n != 'ffn_conv_w' else (DEPTH, 3, N_CHIPS * Fs) for n in small]
    pieces = _unpack(red, full_shapes + [()])
    loss = pieces[-1]
    gsmall = dict(zip(small, pieces[:-1]))
    chip = 2 * lax.axis_index("x") + lax.axis_index("y")
    gsmall['ffn_conv_w'] = lax.dynamic_slice_in_dim(gsmall['ffn_conv_w'], chip * Fs, Fs, axis=2)
    shapes = [W[n].shape for n in small]
    gs, ds_, ms, vs = _adamw(_pack([W[n] for n in small])[None], _pack([gsmall[n] for n in small])[None],
                             _pack([Mo[n] for n in small])[None], _pack([Vo[n] for n in small])[None],
                             name="adamw_small", tr=1 << 14)
    out_g = dict(zip(small, _unpack(gs, shapes)))
    out_d = dict(zip(small, _unpack(ds_, shapes)))
    out_m = dict(zip(small, _unpack(ms, shapes)))
    out_v = dict(zip(small, _unpack(vs, shapes)))

    dw_in_t = jnp.concatenate([dWmainT[:qkv_w], dWfT[:FOX_HEADS], dWmainT[qkv_w:]], axis=0)
    part_now = rs_begin(grp_now, [dw_in_t.reshape(N_CHIPS, EIN // N_CHIPS, D), _shards_from_cols(dWglu),
                                  dWout_ev.reshape(N_CHIPS, D // N_CHIPS, D)], "l0")
    recv = dict(zip(grp_now, _scatter_to_chips(part_now, name="rs_l0_chips")))
    for tag, entries, started, parts in (("l1", grp_l1, rs_l1, part_l1), ("ffn0", grp_ffn0, rs_ffn0, part_ffn0)):
        send, rcv, thru, lands, _ = started
        lands = _chip_exchange_wait('scatter', send, rcv, thru, lands, [False] * len(parts), grad_x,
                                    name=f"rs_{tag}_wait")
        recv.update(zip(entries, _own_slot(lands, own_parts(parts))))
    halves = [_rowsum(recv[e], name=f"rs_sum4_{e[0]}{e[1]}") for e in big_e]
    others = _sibling_join_halves(halves, name="rs_join")
    pairs = dict(zip(big_e, zip(halves, others)))
    for n in big:
        res = _adamw(Wv[n], [pairs[(n, l)] for l in range(W[n].shape[0])], view(n, Mo[n]), view(n, Vo[n]),
                     name=f"adamw_{n}", by_cols=split_cols[(n, 0)])
        out_g[n], out_d[n], out_m[n], out_v[n] = (view(n, t) for t in res)

    return (loss, grad_x.reshape(1, S, D), *[out_g[n] for n in names], *[out_d[n] for n in names],
            *[out_m[n] for n in names], *[out_v[n] for n in names])
```

```python
import functools
import math

import numpy as np
import jax
import jax.numpy as jnp
from jax import lax
from jax.experimental import pallas as pl
from jax.experimental.pallas import tpu as pltpu

F32 = jnp.float32
BF16 = jnp.bfloat16
MESH = pl.DeviceIdType.MESH
ANY = pl.BlockSpec(memory_space=pl.ANY)

D_MODEL = 2048
FOX_HEADS = 8
FOX_HEAD_DIM = 128
FOX_WIDTH = 1024
SSM_WIDTH = 1024
SSM_GROUP = 16
SSM_GROUPS = 64
SSM_STATE = 64
SWA_HEADS = 32
SWA_KV_HEADS = 4
SWA_HEAD_DIM = 64
SWA_GROUPS = 8
SWA_WINDOW = 128
ROPE_DIM = 16
ROPE_THETA = 500000.0
LN_EPS = 1e-5
DEPTH = 2
ALPHA = (2.0 * DEPTH) ** 0.25
ADAM_LR = 0.001
ADAM_B1 = 0.9
ADAM_B2 = 0.999
ADAM_EPS = 1e-08
ADAM_WD = 0.01
ADAM_STEP = 10
N_CHIPS = 4

VMEM_LIMIT = 56 * 1024 * 1024
LANE = 128


def _call(body, **kw):
    return pl.pallas_call(body, **kw)


def _cparams(sem):
    return pltpu.CompilerParams(dimension_semantics=sem, vmem_limit_bytes=VMEM_LIMIT)


def _rup(n, m):
    return (n + m - 1) // m * m


def _pick(n, pref):
    if n <= pref:
        return n
    for step in (128, 16, 8):
        for t in range(pref - pref % step, 0, -step):
            if n % t == 0:
                return t
    return n


def _tile2d(rows, cols, pref_rows=256, budget=256 * 1024):
    tr = _pick(rows, pref_rows)
    if tr < 64:
        tr = rows
    if cols % LANE:
        return tr, cols
    return tr, _pick(cols, max(LANE, budget // tr // LANE * LANE))


def _mm(a, b, mode, *, name, tm=512, tn=1024, tk=2048, bmode=None, out_dtype=F32, after=()):
    a3 = a if a.ndim == 3 else a[None]
    b3 = b if b.ndim == 3 else b[None]
    if mode == 'tn':
        K, M = a3.shape[1:]
    else:
        M, K = a3.shape[1:]
    N = b3.shape[1] if mode == 'nt' else b3.shape[2]
    tm, tn, tk = _pick(M, tm), _pick(N, tn), _pick(K, tk)
    nb = max(a3.shape[0], b3.shape[0])
    nbo, nbr = (1, nb) if bmode == 'abr' else (nb, 1)
    nk = K // tk
    nred = nbr * nk
    a_b = bmode in ('ao', 'abr')
    b_b = bmode in ('bo', 'abr')
    o_b = bmode in ('bo', 'ao')

    def bsel(flag, bo, br):
        return (bo + br) if flag else 0

    if mode == 'tn':
        a_spec = pl.BlockSpec((None, tk, tm), lambda bo, i, j, br, k: (bsel(a_b, bo, br), k, i))
    else:
        a_spec = pl.BlockSpec((None, tm, tk), lambda bo, i, j, br, k: (bsel(a_b, bo, br), i, k))
    if mode == 'nt':
        b_spec = pl.BlockSpec((None, tn, tk), lambda bo, i, j, br, k: (bsel(b_b, bo, br), j, k))
    else:
        b_spec = pl.BlockSpec((None, tk, tn), lambda bo, i, j, br, k: (bsel(b_b, bo, br), k, j))
    o_spec = pl.BlockSpec((None, tm, tn), lambda bo, i, j, br, k: (bsel(o_b, bo, br), i, j))
    dn = {'nn': (((1,), (0,)), ((), ())), 'nt': (((1,), (1,)), ((), ())), 'tn': (((0,), (0,)), ((), ()))}[mode]

    def body(a_ref, b_ref, *rest):
        o_ref, scratch = rest[len(after)], rest[len(after) + 1:]
        r = lax.dot_general(a_ref[...].astype(BF16), b_ref[...].astype(BF16), dn, preferred_element_type=F32)
        if nred == 1:
            o_ref[...] = r.astype(out_dtype)
        else:
            acc = scratch[0]
            step = pl.program_id(3) * nk + pl.program_id(4)

            @pl.when(step == 0)
            def _():
                acc[...] = r

            @pl.when(step > 0)
            def _():
                acc[...] += r

            @pl.when(step == nred - 1)
            def _():
                o_ref[...] = acc[...].astype(out_dtype)

    out = _call(
        body, name=name,
        grid=(nbo, M // tm, N // tn, nbr, nk),
        in_specs=[a_spec, b_spec] + [ANY] * len(after), out_specs=o_spec,
        out_shape=jax.ShapeDtypeStruct((nbo if o_b else 1, M, N), out_dtype),
        scratch_shapes=[] if nred == 1 else [pltpu.VMEM((tm, tn), F32)],
        compiler_params=_cparams(("parallel", "parallel", "parallel", "arbitrary", "arbitrary")),
    )(a3, b3, *after)
    return out if o_b else out[0]


def _add_ln_fwd(x, r, g, b, *, name):
    S, D = x.shape
    tr = _pick(S, 256)

    def body(x_ref, r_ref, g_ref, b_ref, o_ref, xh_ref, rs_ref):
        z = ALPHA * x_ref[...] + r_ref[...]
        mu = jnp.mean(z, axis=-1, keepdims=True)
        zc = z - mu
        var = jnp.mean(zc * zc, axis=-1, keepdims=True)
        rstd = lax.rsqrt(var + LN_EPS)
        xh = zc * rstd
        xh_ref[...] = xh
        rs_ref[...] = rstd
        o_ref[...] = xh * g_ref[...] + b_ref[...]

    row = pl.BlockSpec((tr, D), lambda i: (i, 0))
    vec = pl.BlockSpec((1, D), lambda i: (0, 0))
    return _call(
        body, name=name, grid=(S // tr,),
        in_specs=[row, row, vec, vec],
        out_specs=[row, row, pl.BlockSpec((tr, 1), lambda i: (i, 0))],
        out_shape=[jax.ShapeDtypeStruct((S, D), F32), jax.ShapeDtypeStruct((S, D), F32),
                   jax.ShapeDtypeStruct((S, 1), F32)],
        compiler_params=_cparams(("parallel",)),
    )(x, r, g.reshape(1, D), b.reshape(1, D))


def _ln_bwd(da, db, xhat, rstd, g, *, name, after=()):
    S, D = xhat.shape
    tr = _pick(S, 256)
    two = db is not None

    def body(*refs):
        refs = refs[len(after):]
        if two:
            da_ref, db_ref, xh_ref, rs_ref, g_ref, dz_ref, dg_ref, dbt_ref = refs
            dy = ALPHA * da_ref[...] + db_ref[...]
        else:
            da_ref, xh_ref, rs_ref, g_ref, dz_ref, dg_ref, dbt_ref = refs
            dy = da_ref[...]
        xh = xh_ref[...]
        dxh = dy * g_ref[...]
        m1 = jnp.mean(dxh, axis=-1, keepdims=True)
        m2 = jnp.mean(dxh * xh, axis=-1, keepdims=True)
        dz_ref[...] = rs_ref[...] * (dxh - m1 - xh * m2)
        pg = jnp.sum(dy * xh, axis=0, keepdims=True)
        pb = jnp.sum(dy, axis=0, keepdims=True)

        @pl.when(pl.program_id(0) == 0)
        def _():
            dg_ref[...] = pg
            dbt_ref[...] = pb

        @pl.when(pl.program_id(0) > 0)
        def _():
            dg_ref[...] += pg
            dbt_ref[...] += pb

    row = pl.BlockSpec((tr, D), lambda i: (i, 0))
    vec = pl.BlockSpec((1, D), lambda i: (0, 0))
    ins = list(after) + [da] + ([db] if two else []) + [xhat, rstd, g.reshape(1, D)]
    in_specs = [ANY] * len(after) + [row] + ([row] if two else []) + [row, pl.BlockSpec((tr, 1), lambda i: (i, 0)), vec]
    return _call(
        body, name=name, grid=(S // tr,),
        in_specs=in_specs, out_specs=[row, vec, vec],
        out_shape=[jax.ShapeDtypeStruct((S, D), F32), jax.ShapeDtypeStruct((1, D), F32),
                   jax.ShapeDtypeStruct((1, D), F32)],
        compiler_params=_cparams(("arbitrary",)),
    )(*ins)


def _loss_grad(y, t, *, name):
    S, D = y.shape
    tr = _pick(S, 256)

    def body(y_ref, t_ref, dy_ref, l_ref):
        e = y_ref[...] - t_ref[...]
        dy_ref[...] = e * (1.0 / D)
        part = 0.5 * jnp.sum(jnp.sum(e * e, axis=-1, keepdims=True) * (1.0 / D), axis=0, keepdims=True)

        @pl.when(pl.program_id(0) == 0)
        def _():
            l_ref[...] = part

        @pl.when(pl.program_id(0) > 0)
        def _():
            l_ref[...] += part

    row = pl.BlockSpec((tr, D), lambda i: (i, 0))
    return _call(
        body, name=name, grid=(S // tr,), in_specs=[row, row],
        out_specs=[row, pl.BlockSpec((1, 1), lambda i: (0, 0))],
        out_shape=[jax.ShapeDtypeStruct((S, D), F32), jax.ShapeDtypeStruct((1, 1), F32)],
        compiler_params=_cparams(("arbitrary",)),
    )(y, t)


def _combine(terms, scales, *, name, out_dtype=F32):
    S, D = terms[0].shape
    tr = _pick(S, 256)
    n = len(terms)

    def body(*refs):
        acc = scales[0] * refs[0][...].astype(F32)
        for i in range(1, n):
            acc = acc + scales[i] * refs[i][...].astype(F32)
        refs[n][...] = acc.astype(out_dtype)

    row = pl.BlockSpec((tr, D), lambda i: (i, 0))
    return _call(
        body, name=name, grid=(S // tr,), in_specs=[row] * n, out_specs=row,
        out_shape=jax.ShapeDtypeStruct((S, D), out_dtype),
        compiler_params=_cparams(("parallel",)),
    )(*terms)


def _split3(x):
    h = x.astype(BF16)
    r = x - h.astype(F32)
    m = r.astype(BF16)
    l = (r - m.astype(F32)).astype(BF16)
    return h, m, l


def _tri_matmul(tri_bf, x):
    h, m, l = _split3(x)
    dn = (((1,), (0,)), ((), ()))
    return (lax.dot_general(tri_bf, l, dn, preferred_element_type=F32)
            + lax.dot_general(tri_bf, m, dn, preferred_element_type=F32)
            + lax.dot_general(tri_bf, h, dn, preferred_element_type=F32))


def _gate_fwd(fl, bf, *, name):
    S = fl.shape[0]
    tc = _pick(S, 256)
    nchunk = S // tc

    def body(fl_ref, bf_ref, c_ref, sg_ref):
        r = lax.broadcasted_iota(jnp.int32, (tc, tc), 0)
        cidx = lax.broadcasted_iota(jnp.int32, (tc, tc), 1)
        tri = (r >= cidx).astype(BF16)
        carry = jnp.zeros((1, LANE), F32)
        for ch in range(nchunk):
            x = fl_ref[pl.ds(ch * tc, tc), :] + bf_ref[...]
            lf = jnp.minimum(x, 0.0) - jnp.log(1.0 + jnp.exp(-jnp.abs(x)))
            sg_ref[pl.ds(ch * tc, tc), :] = jax.nn.sigmoid(-x)
            c_ref[pl.ds(ch * tc, tc), :] = _tri_matmul(tri, lf) + carry
            carry = carry + jnp.sum(lf, axis=0, keepdims=True)

    full = pl.BlockSpec((S, LANE), lambda: (0, 0))
    return _call(
        body, name=name, in_specs=[full, pl.BlockSpec((1, LANE), lambda: (0, 0))], out_specs=[full, full],
        out_shape=[jax.ShapeDtypeStruct((S, LANE), F32)] * 2,
        compiler_params=pltpu.CompilerParams(vmem_limit_bytes=VMEM_LIMIT),
    )(fl, bf)


def _gate_bwd(dc, sg, *, name):
    S = dc.shape[0]
    tc = _pick(S, 256)
    nchunk = S // tc

    def body(dc_ref, sg_ref, dfl_ref, db_ref):
        r = lax.broadcasted_iota(jnp.int32, (tc, tc), 0)
        cidx = lax.broadcasted_iota(jnp.int32, (tc, tc), 1)
        tri = (r <= cidx).astype(BF16)
        carry = jnp.zeros((1, LANE), F32)
        dbacc = jnp.zeros((1, LANE), F32)
        for ch in reversed(range(nchunk)):
            d = dc_ref[pl.ds(ch * tc, tc), :]
            dfl = (_tri_matmul(tri, d) + carry) * sg_ref[pl.ds(ch * tc, tc), :]
            dfl_ref[pl.ds(ch * tc, tc), :] = dfl
            dbacc = dbacc + jnp.sum(dfl, axis=0, keepdims=True)
            carry = carry + jnp.sum(d, axis=0, keepdims=True)
        db_ref[...] = dbacc

    full = pl.BlockSpec((S, LANE), lambda: (0, 0))
    return _call(
        body, name=name, in_specs=[full, full], out_specs=[full, pl.BlockSpec((1, LANE), lambda: (0, 0))],
        out_shape=[jax.ShapeDtypeStruct((S, LANE), F32), jax.ShapeDtypeStruct((1, LANE), F32)],
        compiler_params=pltpu.CompilerParams(vmem_limit_bytes=VMEM_LIMIT),
    )(dc, sg)


def _fox_scores(q_ref, k_ref, cc_ref, cr_ref, qi, tq, S):
    scale = 1.0 / math.sqrt(FOX_HEAD_DIM)
    s = lax.dot_general(q_ref[...].astype(BF16), k_ref[...].astype(BF16), (((1,), (1,)), ((), ())),
                        preferred_element_type=F32) * scale
    s = s + cc_ref[...] - cr_ref[...]
    row = lax.broadcasted_iota(jnp.int32, (tq, S), 0) + qi * tq
    col = lax.broadcasted_iota(jnp.int32, (tq, S), 1)
    return s, row >= col


def _fox_fwd(P, ccol, crow, *, name):
    S = P.shape[0]
    tq = _pick(S, 256)
    H = FOX_HEADS

    def body(q_ref, k_ref, v_ref, cc_ref, cr_ref, o_ref, l_ref):
        s, causal = _fox_scores(q_ref, k_ref, cc_ref, cr_ref, pl.program_id(1), tq, S)
        s = jnp.where(causal, s, -1e30)
        m = jnp.max(s, axis=-1, keepdims=True)
        e = jnp.exp(s - m)
        den = jnp.sum(e, axis=-1, keepdims=True)
        p = e / den
        o_ref[...] = jnp.dot(p.astype(BF16), v_ref[...].astype(BF16), preferred_element_type=F32)
        l_ref[...] = m + jnp.log(den)

    return _call(
        body, name=name, grid=(H, S // tq),
        in_specs=[pl.BlockSpec((tq, 128), lambda h, i: (i, h)),
                  pl.BlockSpec((S, 128), lambda h, i: (0, H + h)),
                  pl.BlockSpec((S, 128), lambda h, i: (0, 2 * H + h)),
                  pl.BlockSpec((None, tq, 1), lambda h, i: (h, i, 0)),
                  pl.BlockSpec((None, 1, S), lambda h, i: (h, 0, 0))],
        out_specs=[pl.BlockSpec((tq, 128), lambda h, i: (i, h)),
                   pl.BlockSpec((None, tq, 1), lambda h, i: (h, i, 0))],
        out_shape=[jax.ShapeDtypeStruct((S, FOX_WIDTH), F32), jax.ShapeDtypeStruct((H, S, 1), F32)],
        compiler_params=_cparams(("parallel", "parallel")),
    )(P, P, P, ccol, crow)


def _fox_bwd(P, ccol, crow, o, lse, dcat, *, name):
    S = P.shape[0]
    tq = _pick(S, 256)
    H = FOX_HEADS
    nq = S // tq
    scale = 1.0 / math.sqrt(FOX_HEAD_DIM)

    def body(q_ref, k_ref, v_ref, cc_ref, cr_ref, o_ref, l_ref, do_ref,
             dq_ref, dk_ref, dv_ref, dcc_ref, dcr_ref, dk_acc, dv_acc):
        qi = pl.program_id(1)
        s, causal = _fox_scores(q_ref, k_ref, cc_ref, cr_ref, qi, tq, S)
        p = jnp.where(causal, jnp.exp(s - l_ref[...]), 0.0)
        do = do_ref[...]
        do_bf = do.astype(BF16)
        dp = lax.dot_general(do_bf, v_ref[...].astype(BF16), (((1,), (1,)), ((), ())), preferred_element_type=F32)
        delta = jnp.sum(do * o_ref[...], axis=-1, keepdims=True)
        ds = p * (dp - delta)
        ds_bf = ds.astype(BF16)
        dq_ref[...] = (jnp.dot(ds_bf, k_ref[...].astype(BF16), preferred_element_type=F32) * scale).astype(BF16)
        dkp = lax.dot_general(ds_bf, q_ref[...].astype(BF16), (((0,), (0,)), ((), ())),
                              preferred_element_type=F32) * scale
        dvp = lax.dot_general(p.astype(BF16), do_bf, (((0,), (0,)), ((), ())), preferred_element_type=F32)
        dcc_ref[...] = jnp.sum(ds, axis=-1, keepdims=True)
        dcr = jnp.sum(ds, axis=0, keepdims=True)

        @pl.when(qi == 0)
        def _():
            dk_acc[...] = dkp
            dv_acc[...] = dvp
            dcr_ref[...] = dcr

        @pl.when(qi > 0)
        def _():
            dk_acc[...] += dkp
            dv_acc[...] += dvp
            dcr_ref[...] += dcr

        @pl.when(qi == nq - 1)
        def _():
            dk_ref[...] = dk_acc[...].astype(BF16)
            dv_ref[...] = dv_acc[...].astype(BF16)

    qblk = pl.BlockSpec((tq, 128), lambda h, i: (i, h))
    kvo = pl.BlockSpec((S, 128), lambda h, i: (0, h))
    col = pl.BlockSpec((None, tq, 1), lambda h, i: (h, i, 0))
    rowv = pl.BlockSpec((None, 1, S), lambda h, i: (h, 0, 0))
    return _call(
        body, name=name, grid=(H, nq),
        in_specs=[qblk,
                  pl.BlockSpec((S, 128), lambda h, i: (0, H + h)),
                  pl.BlockSpec((S, 128), lambda h, i: (0, 2 * H + h)),
                  col, rowv, qblk, col, qblk],
        out_specs=[qblk, kvo, kvo, col, rowv],
        out_shape=[jax.ShapeDtypeStruct((S, FOX_WIDTH), BF16)] * 3
        + [jax.ShapeDtypeStruct((H, S, 1), F32), jax.ShapeDtypeStruct((H, 1, S), F32)],
        scratch_shapes=[pltpu.VMEM((S, 128), F32), pltpu.VMEM((S, 128), F32)],
        compiler_params=_cparams(("parallel", "arbitrary")),
    )(P, P, P, ccol, crow, o, lse, dcat)


def _s5_disc_fwd(lr, li, ls, *, name):
    G, Pn = lr.shape

    def body(lr_ref, li_ref, ls_ref, ar_ref, ai_ref, gr_ref, gi_ref):
        lr_, li_ = lr_ref[...], li_ref[...]
        dt = jnp.exp(ls_ref[...])
        mag = jnp.exp(lr_ * dt)
        th = li_ * dt
        ar = mag * jnp.cos(th)
        ai = mag * jnp.sin(th)
        den = lr_ * lr_ + li_ * li_
        xr = ar - 1.0
        ar_ref[...] = ar
        ai_ref[...] = ai
        gr_ref[...] = (xr * lr_ + ai * li_) / den
        gi_ref[...] = (ai * lr_ - xr * li_) / den

    sq = pl.BlockSpec((G, Pn), lambda: (0, 0))
    return _call(
        body, name=name, in_specs=[sq, sq, pl.BlockSpec((G, 1), lambda: (0, 0))], out_specs=[sq] * 4,
        out_shape=[jax.ShapeDtypeStruct((G, Pn), F32)] * 4,
    )(lr, li, ls)


def _s5_disc_bwd(lr, li, ls, dar, dai, dgr, dgi, *, name):
    G, Pn = lr.shape

    def body(lr_ref, li_ref, ls_ref, dar_ref, dai_ref, dgr_ref, dgi_ref, dlr_ref, dli_ref, dls_ref):
        lr_, li_ = lr_ref[...], li_ref[...]
        dt = jnp.exp(ls_ref[...])
        mag = jnp.exp(lr_ * dt)
        th = li_ * dt
        ar = mag * jnp.cos(th)
        ai = mag * jnp.sin(th)
        den = lr_ * lr_ + li_ * li_
        xr = ar - 1.0
        xi = ai
        g_re = (xr * lr_ + xi * li_) / den
        g_im = (xi * lr_ - xr * li_) / den
        dgr_, dgi_ = dgr_ref[...], dgi_ref[...]
        dxr = (dgr_ * lr_ - dgi_ * li_) / den
        dxi = (dgr_ * li_ + dgi_ * lr_) / den
        dden = -(dgr_ * g_re + dgi_ * g_im) / den
        dlr = (dgr_ * xr + dgi_ * xi) / den + 2.0 * dden * lr_
        dli = (dgr_ * xi - dgi_ * xr) / den + 2.0 * dden * li_
        da_r = dar_ref[...] + dxr
        da_i = dai_ref[...] + dxi
        dmag_mag = da_r * ar + da_i * ai
        dth = da_i * ar - da_r * ai
        dlr_ref[...] = dlr + dmag_mag * dt
        dli_ref[...] = dli + dth * dt
        ddt = jnp.sum(dmag_mag * lr_ + dth * li_, axis=-1, keepdims=True)
        dls_ref[...] = ddt * dt

    sq = pl.BlockSpec((G, Pn), lambda: (0, 0))
    c1 = pl.BlockSpec((G, 1), lambda: (0, 0))
    return _call(
        body, name=name, in_specs=[sq, sq, c1, sq, sq, sq, sq], out_specs=[sq, sq, c1],
        out_shape=[jax.ShapeDtypeStruct((G, Pn), F32)] * 2 + [jax.ShapeDtypeStruct((G, 1), F32)],
    )(lr, li, ls, dar, dai, dgr, dgi)


def _s5_bb_fwd(gr, gi, br, bi, *, name):
    R, C = br.shape

    def body(gr_ref, gi_ref, br_ref, bi_ref, or_ref, oi_ref):
        g_r, g_i, b_r, b_i = gr_ref[...], gi_ref[...], br_ref[...], bi_ref[...]
        or_ref[...] = g_r * b_r - g_i * b_i
        oi_ref[...] = g_r * b_i + g_i * b_r

    w = pl.BlockSpec((R, C), lambda: (0, 0))
    c1 = pl.BlockSpec((R, 1), lambda: (0, 0))
    return _call(body, name=name, in_specs=[c1, c1, w, w], out_specs=[w, w],
                 out_shape=[jax.ShapeDtypeStruct((R, C), F32)] * 2)(gr, gi, br, bi)


def _s5_bb_bwd(gr, gi, br, bi, dbbr, dbbi, *, name):
    R, C = br.shape

    def body(gr_ref, gi_ref, br_ref, bi_ref, dr_ref, di_ref, dbr_ref, dbi_ref, dgr_ref, dgi_ref):
        g_r, g_i, b_r, b_i = gr_ref[...], gi_ref[...], br_ref[...], bi_ref[...]
        d_r, d_i = dr_ref[...], di_ref[...]
        dbr_ref[...] = g_r * d_r + g_i * d_i
        dbi_ref[...] = g_r * d_i - g_i * d_r
        dgr_ref[...] = jnp.sum(d_r * b_r + d_i * b_i, axis=-1, keepdims=True)
        dgi_ref[...] = jnp.sum(d_i * b_r - d_r * b_i, axis=-1, keepdims=True)

    w = pl.BlockSpec((R, C), lambda: (0, 0))
    c1 = pl.BlockSpec((R, 1), lambda: (0, 0))
    return _call(body, name=name, in_specs=[c1, c1, w, w, w, w], out_specs=[w, w, c1, c1],
                 out_shape=[jax.ShapeDtypeStruct((R, C), F32)] * 2 + [jax.ShapeDtypeStruct((R, 1), F32)] * 2,
                 )(gr, gi, br, bi, dbbr, dbbi)


_DIAG_TILE = 8


def _diag_mask(gr, gc):
    rows, cols = _DIAG_TILE * gr, _DIAG_TILE * gc
    r = lax.broadcasted_iota(jnp.int32, (rows, cols), 0) >> (gr.bit_length() - 1)
    c = lax.broadcasted_iota(jnp.int32, (rows, cols), 1) >> (gc.bit_length() - 1)
    return r == c


def _diag_expand(t2, gr, gc, *, name):
    _, R, _ = t2.shape
    G = R // gr
    nt = G // _DIAG_TILE
    rows, cols = _DIAG_TILE * gr, _DIAG_TILE * gc

    def body(t_ref, o_ref):
        @pl.when(pl.program_id(1) == pl.program_id(2))
        def _():
            src = lax.broadcasted_iota(jnp.int32, (gc, cols), 0)
            dst = lax.broadcasted_iota(jnp.int32, (gc, cols), 1) & (gc - 1)
            spread = (src == dst).astype(BF16)
            y = jnp.dot(t_ref[...].astype(BF16), spread, preferred_element_type=F32)
            o_ref[...] = jnp.where(_diag_mask(gr, gc), y, 0.0).astype(BF16)

        @pl.when(pl.program_id(1) != pl.program_id(2))
        def _():
            o_ref[...] = jnp.zeros_like(o_ref)

    return _call(
        body, name=name, grid=(2, nt, nt),
        in_specs=[pl.BlockSpec((None, rows, gc), lambda p, i, j: (p, i, 0))],
        out_specs=pl.BlockSpec((None, rows, cols), lambda p, i, j: (p, i, j)),
        out_shape=jax.ShapeDtypeStruct((2, R, G * gc), BF16),
        compiler_params=_cparams(("parallel",) * 3),
    )(t2)


def _diag_extract(xd, gr, gc, *, name):
    _, R, _ = xd.shape
    nt = R // gr // _DIAG_TILE
    rows, cols = _DIAG_TILE * gr, _DIAG_TILE * gc

    def body(x_ref, o_ref):
        src = lax.broadcasted_iota(jnp.int32, (cols, gc), 0) & (gc - 1)
        dst = lax.broadcasted_iota(jnp.int32, (cols, gc), 1)
        fold = (src == dst).astype(BF16)
        parts = _split3(jnp.where(_diag_mask(gr, gc), x_ref[...], 0.0))
        acc = jnp.dot(parts[2], fold, preferred_element_type=F32)
        acc = acc + jnp.dot(parts[1], fold, preferred_element_type=F32)
        o_ref[...] = acc + jnp.dot(parts[0], fold, preferred_element_type=F32)

    return _call(
        body, name=name, grid=(2, nt),
        in_specs=[pl.BlockSpec((None, rows, cols), lambda p, i: (p, i, i))],
        out_specs=pl.BlockSpec((None, rows, gc), lambda p, i: (p, i, 0)),
        out_shape=jax.ShapeDtypeStruct((2, R, gc), F32),
        compiler_params=_cparams(("parallel",) * 2),
    )(xd)


def _s5_scan_fwd(bu, a, *, name):
    _, S, N = bu.shape
    tc = 512
    nt = N // tc

    def body(a_ref, b_ref, h_ref):
        ar, ai = a_ref[0], a_ref[1]

        def step(t, carry):
            hr, hi = carry
            nr = ar * hr - ai * hi + b_ref[0, pl.ds(t, 1), :]
            ni = ar * hi + ai * hr + b_ref[1, pl.ds(t, 1), :]
            h_ref[0, pl.ds(t, 1), :] = nr
            h_ref[1, pl.ds(t, 1), :] = ni
            return nr, ni

        z = jnp.zeros((1, tc), F32)
        lax.fori_loop(0, S, step, (z, z), unroll=8)

    vec = pl.BlockSpec((2, 1, tc), lambda j: (0, 0, j))
    mat = pl.BlockSpec((2, S, tc), lambda j: (0, 0, j))
    return _call(
        body, name=name, grid=(nt,), in_specs=[vec, mat], out_specs=mat,
        out_shape=jax.ShapeDtypeStruct((2, S, N), F32),
        compiler_params=_cparams(("parallel",)),
    )(a, bu)


def _s5_scan_bwd(g, h, a, *, name):
    _, S, N = g.shape
    tc = 256
    nt = N // tc

    def body(a_ref, g_ref, h_ref, l_ref, da_ref):
        ar, ai = a_ref[0], a_ref[1]

        def step(i, carry):
            t = S - 1 - i
            lr, li, dar, dai = carry
            nr = g_ref[0, pl.ds(t, 1), :] + ar * lr + ai * li
            ni = g_ref[1, pl.ds(t, 1), :] + ar * li - ai * lr
            l_ref[0, pl.ds(t, 1), :] = nr
            l_ref[1, pl.ds(t, 1), :] = ni
            tp = jnp.maximum(t - 1, 0)
            keep = jnp.where(t > 0, 1.0, 0.0).astype(F32)
            hpr = h_ref[0, pl.ds(tp, 1), :] * keep
            hpi = h_ref[1, pl.ds(tp, 1), :] * keep
            return nr, ni, dar + nr * hpr + ni * hpi, dai + ni * hpr - nr * hpi

        z = jnp.zeros((1, tc), F32)
        _, _, dar, dai = lax.fori_loop(0, S, step, (z, z, z, z), unroll=8)
        da_ref[0] = dar
        da_ref[1] = dai

    vec = pl.BlockSpec((2, 1, tc), lambda j: (0, 0, j))
    mat = pl.BlockSpec((2, S, tc), lambda j: (0, 0, j))
    return _call(
        body, name=name, grid=(nt,), in_specs=[vec, mat, mat], out_specs=[mat, vec],
        out_shape=[jax.ShapeDtypeStruct((2, S, N), F32), jax.ShapeDtypeStruct((2, 1, N), F32)],
        compiler_params=_cparams(("parallel",)),
    )(a, g, h)


_GELU_C = math.sqrt(2.0 / math.pi)


def _s5_out_fwd(yc, P, dskip, *, name):
    S, W = yc.shape
    tr = _pick(S, 256)
    ub = 3 * FOX_WIDTH // W

    def body(yc_ref, u_ref, d_ref, y_ref, yg_ref):
        y = yc_ref[...] + d_ref[...] * u_ref[...]
        y_ref[...] = y
        t = jnp.tanh(_GELU_C * (y + 0.044715 * y * y * y))
        yg_ref[...] = (0.5 * y * (1.0 + t)).astype(BF16)

    row = pl.BlockSpec((tr, W), lambda i: (i, 0))
    return _call(
        body, name=name, grid=(S // tr,),
        in_specs=[row, pl.BlockSpec((tr, W), lambda i: (i, ub)), pl.BlockSpec((1, W), lambda i: (0, 0))],
        out_specs=[row, row],
        out_shape=[jax.ShapeDtypeStruct((S, W), F32), jax.ShapeDtypeStruct((S, W), BF16)],
        compiler_params=_cparams(("parallel",)),
    )(yc, P, dskip)


def _s5_out_bwd(dyg, y, P, dskip, *, name):
    S, W = y.shape
    tr = _pick(S, 256)
    ub = 3 * FOX_WIDTH // W

    def body(dyg_ref, y_ref, u_ref, d_ref, dy_ref, du_ref, dd_ref):
        y_ = y_ref[...]
        inner = _GELU_C * (y_ + 0.044715 * y_ * y_ * y_)
        t = jnp.tanh(inner)
        dgelu = 0.5 * (1.0 + t) + 0.5 * y_ * (1.0 - t * t) * _GELU_C * (1.0 + 3.0 * 0.044715 * y_ * y_)
        dy = dyg_ref[...] * dgelu
        dy_ref[...] = dy.astype(BF16)
        du_ref[...] = d_ref[...] * dy
        part = jnp.sum(dy * u_ref[...], axis=0, keepdims=True)

        @pl.when(pl.program_id(0) == 0)
        def _():
            dd_ref[...] = part

        @pl.when(pl.program_id(0) > 0)
        def _():
            dd_ref[...] += part

    row = pl.BlockSpec((tr, W), lambda i: (i, 0))
    vec = pl.BlockSpec((1, W), lambda i: (0, 0))
    return _call(
        body, name=name, grid=(S // tr,),
        in_specs=[row, row, pl.BlockSpec((tr, W), lambda i: (i, ub)), vec],
        out_specs=[row, row, vec],
        out_shape=[jax.ShapeDtypeStruct((S, W), BF16), jax.ShapeDtypeStruct((S, W), F32),
                   jax.ShapeDtypeStruct((1, W), F32)],
        compiler_params=_cparams(("arbitrary",)),
    )(dyg, y, P, dskip)


def _glu_fwd(z, *, name):
    S, W2 = z.shape
    W = W2 // 2
    tr = _pick(S, 256)

    def body(z1_ref, z2_ref, o_ref):
        o_ref[...] = (z1_ref[...] * jax.nn.sigmoid(z2_ref[...])).astype(BF16)

    return _call(
        body, name=name, grid=(S // tr,),
        in_specs=[pl.BlockSpec((tr, W), lambda i: (i, 0)), pl.BlockSpec((tr, W), lambda i: (i, 1))],
        out_specs=pl.BlockSpec((tr, W), lambda i: (i, 0)),
        out_shape=jax.ShapeDtypeStruct((S, W), BF16),
        compiler_params=_cparams(("parallel",)),
    )(z, z)


def _glu_bwd(z, dcat, *, name):
    S, W2 = z.shape
    W = W2 // 2
    tr = _pick(S, 256)

    def body(z1_ref, z2_ref, d_ref, dz1_ref, dz2_ref):
        sg = jax.nn.sigmoid(z2_ref[...])
        d = d_ref[...]
        dz1_ref[...] = (d * sg).astype(BF16)
        dz2_ref[...] = (d * z1_ref[...] * sg * (1.0 - sg)).astype(BF16)

    lo = pl.BlockSpec((tr, W), lambda i: (i, 0))
    hi = pl.BlockSpec((tr, W), lambda i: (i, 1))
    dz1, dz2 = _call(
        body, name=name, grid=(S // tr,), in_specs=[lo, hi, hi], out_specs=[lo, lo],
        out_shape=[jax.ShapeDtypeStruct((S, W), BF16)] * 2,
        compiler_params=_cparams(("parallel",)),
    )(z, z, dcat)
    return jnp.concatenate([dz1, dz2], axis=1)


def _act_fwd(h, cw, cb, *, name):
    _, S, FP = h.shape
    tr = _pick(S, 256)
    hb = tr // 8

    def conv(x_ref, halo_ref, w_ref, b_ref, ext, first):
        ext[pl.ds(0, 8), :] = jnp.where(first, 0.0, halo_ref[...])
        ext[pl.ds(8, tr), :] = x_ref[...]
        return (b_ref[...] + w_ref[pl.ds(2, 1), :] * ext[pl.ds(8, tr), :]
                + w_ref[pl.ds(1, 1), :] * ext[pl.ds(7, tr), :] + w_ref[pl.ds(0, 1), :] * ext[pl.ds(6, tr), :])

    def body(g_ref, gh_ref, v_ref, vh_ref, wg_ref, wv_ref, bg_ref, bv_ref, a_ref, ext):
        first = pl.program_id(1) == 0
        cg = conv(g_ref, gh_ref, wg_ref, bg_ref, ext, first)
        cv = conv(v_ref, vh_ref, wv_ref, bv_ref, ext, first)
        a_ref[...] = (cg * jax.nn.sigmoid(cg) * cv).astype(BF16)

    def main(off):
        return pl.BlockSpec((None, tr, FP), lambda j, i: (j + off, i, 0))

    def halo(off):
        return pl.BlockSpec((None, 8, FP), lambda j, i: (j + off, jnp.maximum(i * hb - 1, 0), 0))

    def wspec(off):
        return pl.BlockSpec((None, 3, FP), lambda j, i: (j + off, 0, 0))

    def bspec(off):
        return pl.BlockSpec((None, 1, FP), lambda j, i: (j + off, 0, 0))

    cb3 = cb.reshape(4, 1, FP)
    return _call(
        body, name=name, grid=(2, S // tr),
        in_specs=[main(0), halo(0), main(2), halo(2), wspec(0), wspec(2), bspec(0), bspec(2)],
        out_specs=pl.BlockSpec((None, tr, FP), lambda j, i: (j, i, 0)),
        out_shape=jax.ShapeDtypeStruct((2, S, FP), BF16),
        scratch_shapes=[pltpu.VMEM((tr + 8, FP), F32)],
        compiler_params=_cparams(("parallel", "arbitrary")),
    )(h, h, h, h, cw, cw, cb3, cb3)


def _act_bwd(h, da, cw, cb, *, name):
    _, S, FP = h.shape
    tr = _pick(S, 128)
    hb = tr // 8
    nr = S // tr

    def fill(ext, x_ref, prev_ref, next_ref, first, last):
        ext[pl.ds(0, 8), :] = jnp.where(first, 0.0, prev_ref[...])
        ext[pl.ds(8, tr), :] = x_ref[...]
        ext[pl.ds(8 + tr, 8), :] = jnp.where(last, 0.0, next_ref[...])

    def convo(ext, w, b, base, n):
        return (b + w[2] * ext[pl.ds(base, n), :] + w[1] * ext[pl.ds(base - 1, n), :]
                + w[0] * ext[pl.ds(base - 2, n), :])

    def body(g_ref, gp_ref, gn_ref, v_ref, vp_ref, vn_ref, da_ref, dan_ref,
             wg_ref, wv_ref, bg_ref, bv_ref,
             dg_ref, dv_ref, dwg_ref, dwv_ref, dbg_ref, dbv_ref, eg, ev, ed, dcg, dcv):
        i = pl.program_id(1)
        first = i == 0
        last = i == nr - 1
        fill(eg, g_ref, gp_ref, gn_ref, first, last)
        fill(ev, v_ref, vp_ref, vn_ref, first, last)
        ed[pl.ds(0, tr), :] = da_ref[...]
        ed[pl.ds(tr, 8), :] = jnp.where(last, 0.0, dan_ref[...])
        wg = [wg_ref[pl.ds(k, 1), :] for k in range(3)]
        wv = [wv_ref[pl.ds(k, 1), :] for k in range(3)]
        n = tr + 8
        cg = convo(eg, wg, bg_ref[...], 8, n)
        cv = convo(ev, wv, bv_ref[...], 8, n)
        sg = jax.nn.sigmoid(cg)
        d = ed[...]
        dcg[...] = d * cv * sg * (1.0 + cg * (1.0 - sg))
        dcv[...] = d * cg * sg
        for (dc, w, e, dh_ref, dw_ref, db_ref) in ((dcg, wg, eg, dg_ref, dwg_ref, dbg_ref),
                                                  (dcv, wv, ev, dv_ref, dwv_ref, dbv_ref)):
            d0 = dc[pl.ds(0, tr), :]
            dh_ref[...] = (w[2] * d0 + w[1] * dc[pl.ds(1, tr), :] + w[0] * dc[pl.ds(2, tr), :]).astype(BF16)
            pw = [jnp.sum(d0 * e[pl.ds(6 + k, tr), :], axis=0, keepdims=True) for k in range(3)]
            pb = jnp.sum(d0, axis=0, keepdims=True)

            @pl.when(first)
            def _():
                for k in range(3):
                    dw_ref[pl.ds(k, 1), :] = pw[k]
                db_ref[...] = pb

            @pl.when(jnp.logical_not(first))
            def _():
                for k in range(3):
                    dw_ref[pl.ds(k, 1), :] += pw[k]
                db_ref[...] += pb

    def main(off):
        return pl.BlockSpec((None, tr, FP), lambda j, i: (j + off, i, 0))

    def prev(off):
        return pl.BlockSpec((None, 8, FP), lambda j, i: (j + off, jnp.maximum(i * hb - 1, 0), 0))

    def nxt(off):
        return pl.BlockSpec((None, 8, FP), lambda j, i: (j + off, jnp.minimum((i + 1) * hb, S // 8 - 1), 0))

    def wspec(off):
        return pl.BlockSpec((None, 3, FP), lambda j, i: (j + off, 0, 0))

    def bspec(off):
        return pl.BlockSpec((None, 1, FP), lambda j, i: (j + off, 0, 0))

    cb3 = cb.reshape(4, 1, FP)
    dg, dv, dwg, dwv, dbg, dbv = _call(
        body, name=name, grid=(2, nr),
        in_specs=[main(0), prev(0), nxt(0), main(2), prev(2), nxt(2), main(0), nxt(0),
                  wspec(0), wspec(2), bspec(0), bspec(2)],
        out_specs=[main(0), main(0), wspec(0), wspec(0), bspec(0), bspec(0)],
        out_shape=[jax.ShapeDtypeStruct((2, S, FP), BF16)] * 2
        + [jax.ShapeDtypeStruct((2, 3, FP), F32)] * 2 + [jax.ShapeDtypeStruct((2, 1, FP), F32)] * 2,
        scratch_shapes=[pltpu.VMEM((tr + 16, FP), F32), pltpu.VMEM((tr + 16, FP), F32),
                        pltpu.VMEM((tr + 8, FP), F32), pltpu.VMEM((tr + 8, FP), F32),
                        pltpu.VMEM((tr + 8, FP), F32)],
        compiler_params=_cparams(("parallel", "arbitrary")),
    )(h, h, h, h, h, h, da, da, cw, cw, cb3, cb3)
    return (jnp.concatenate([dg, dv], axis=0), jnp.concatenate([dwg, dwv], axis=0),
            jnp.concatenate([dbg, dbv], axis=0))


def _rope_tables(posf, *, name):
    S = posf.shape[0]
    half = ROPE_DIM // 2
    d = np.arange(LANE) % SWA_HEAD_DIM
    invf = np.where(d < ROPE_DIM, ROPE_THETA ** (-(d % half).astype(np.float64) / half), 0.0).astype(np.float32)
    m_rot = (d < ROPE_DIM).astype(np.float32)
    m_a = (d < half).astype(np.float32)
    m_b = ((d >= half) & (d < ROPE_DIM)).astype(np.float32)
    consts = jnp.asarray(np.stack([invf, m_rot, m_a, m_b] + [np.zeros(LANE, np.float32)] * 4))

    def body(p_ref, k_ref, c_ref, sa_ref, sb_ref):
        k = k_ref[...]
        ang = p_ref[...] * k[0:1]
        co, si = jnp.cos(ang), jnp.sin(ang)
        c_ref[...] = k[1:2] * co + (1.0 - k[1:2])
        sa_ref[...] = -k[2:3] * si
        sb_ref[...] = k[3:4] * si

    full = pl.BlockSpec((S, LANE), lambda: (0, 0))
    return _call(
        body, name=name,
        in_specs=[pl.BlockSpec((S, 1), lambda: (0, 0)), pl.BlockSpec((8, LANE), lambda: (0, 0))],
        out_specs=[full] * 3, out_shape=[jax.ShapeDtypeStruct((S, LANE), F32)] * 3,
    )(posf, consts)


def _rope_apply(x, tabs, *, col0, width, inverse, name, out_dtype):
    S = x.shape[0]
    tr = _pick(S, 256)
    rep = width // LANE
    cb = col0 // width

    def body(x_ref, c_ref, sa_ref, sb_ref, o_ref):
        xv = x_ref[...].astype(F32)
        c = jnp.tile(c_ref[...], (1, rep))
        sa = jnp.tile(sa_ref[...], (1, rep))
        sb = jnp.tile(sb_ref[...], (1, rep))
        if not inverse:
            out = xv * c + pltpu.roll(xv, width - 8, 1) * sa + pltpu.roll(xv, 8, 1) * sb
        else:
            out = xv * c + pltpu.roll(xv * sa, 8, 1) + pltpu.roll(xv * sb, width - 8, 1)
        o_ref[...] = out.astype(out_dtype)

    tab = pl.BlockSpec((tr, LANE), lambda i: (i, 0))
    return _call(
        body, name=name, grid=(S // tr,),
        in_specs=[pl.BlockSpec((tr, width), lambda i: (i, cb)), tab, tab, tab],
        out_specs=pl.BlockSpec((tr, width), lambda i: (i, 0)),
        out_shape=jax.ShapeDtypeStruct((S, width), out_dtype),
        compiler_params=_cparams(("parallel",)),
    )(x, *tabs)


def _swa_mask(n):
    rows = SWA_GROUPS * SWA_WINDOW
    qi = lax.broadcasted_iota(jnp.int32, (rows, 2 * SWA_WINDOW), 0) & (SWA_WINDOW - 1)
    kj = lax.broadcasted_iota(jnp.int32, (rows, 2 * SWA_WINDOW), 1)
    rel = SWA_WINDOW + qi - kj
    return (rel >= 0) & (rel < SWA_WINDOW) & ((n > 0) | (kj >= SWA_WINDOW))


def _swa_fwd(qT, kT, vT, sink_rows, *, name):
    S = qT.shape[1]
    W, G, Dh = SWA_WINDOW, SWA_GROUPS, SWA_HEAD_DIM
    nb = S // W
    scale = 1.0 / math.sqrt(Dh)

    def body(q_ref, kp_ref, kc_ref, vp_ref, vc_ref, s_ref, o_ref, l_ref):
        n = pl.program_id(1)
        q = q_ref[...].reshape(G * W, Dh)
        kk = jnp.concatenate([kp_ref[...], kc_ref[...]], axis=0)
        vv = jnp.concatenate([vp_ref[...], vc_ref[...]], axis=0)
        s = lax.dot_general(q, kk, (((1,), (1,)), ((), ())), preferred_element_type=F32) * scale
        s = jnp.where(_swa_mask(n), s, -1e30)
        sink = s_ref[...]
        m = jnp.maximum(jnp.max(s, axis=-1, keepdims=True), sink)
        e = jnp.exp(s - m)
        den = jnp.sum(e, axis=-1, keepdims=True) + jnp.exp(sink - m)
        p = e / den
        o_ref[...] = jnp.dot(p.astype(BF16), vv, preferred_element_type=F32).reshape(G, W, Dh)
        l_ref[...] = (m + jnp.log(den)).reshape(G, W, 1)

    qs = pl.BlockSpec((G, W, Dh), lambda g, n: (g, n, 0))
    prev = pl.BlockSpec((None, W, Dh), lambda g, n: (g, jnp.maximum(n - 1, 0), 0))
    cur = pl.BlockSpec((None, W, Dh), lambda g, n: (g, n, 0))
    return _call(
        body, name=name, grid=(SWA_KV_HEADS, nb),
        in_specs=[qs, prev, cur, prev, cur, pl.BlockSpec((None, G * W, 1), lambda g, n: (g, 0, 0))],
        out_specs=[qs, pl.BlockSpec((G, W, 1), lambda g, n: (g, n, 0))],
        out_shape=[jax.ShapeDtypeStruct((SWA_HEADS, S, Dh), F32), jax.ShapeDtypeStruct((SWA_HEADS, S, 1), F32)],
        compiler_params=_cparams(("parallel", "parallel")),
    )(qT, kT, kT, vT, vT, sink_rows)


def _swa_bwd(qT, kT, vT, sink_rows, oT, L, doT, *, name):
    S = qT.shape[1]
    W, G, Dh = SWA_WINDOW, SWA_GROUPS, SWA_HEAD_DIM
    nb = S // W
    scale = 1.0 / math.sqrt(Dh)

    def body(q_ref, kp_ref, kc_ref, vp_ref, vc_ref, s_ref, o_ref, l_ref, do_ref,
             dq_ref, dk_ref, dv_ref, ds_ref):
        n = pl.program_id(1)
        q = q_ref[...].reshape(G * W, Dh)
        kk = jnp.concatenate([kp_ref[...], kc_ref[...]], axis=0)
        vv = jnp.concatenate([vp_ref[...], vc_ref[...]], axis=0)
        s = lax.dot_general(q, kk, (((1,), (1,)), ((), ())), preferred_element_type=F32) * scale
        lrow = l_ref[...].reshape(G * W, 1)
        p = jnp.where(_swa_mask(n), jnp.exp(s - lrow), 0.0)
        do = do_ref[...].reshape(G * W, Dh)
        do_bf = do.astype(BF16)
        dp = lax.dot_general(do_bf, vv, (((1,), (1,)), ((), ())), preferred_element_type=F32)
        delta = jnp.sum(do * o_ref[...].reshape(G * W, Dh), axis=-1, keepdims=True)
        dsc = p * (dp - delta)
        ds_bf = dsc.astype(BF16)
        dq_ref[...] = (jnp.dot(ds_bf, kk, preferred_element_type=F32) * scale).astype(BF16).reshape(G, W, Dh)
        dkk = lax.dot_general(ds_bf, q, (((0,), (0,)), ((), ())), preferred_element_type=F32) * scale
        dvv = lax.dot_general(p.astype(BF16), do_bf, (((0,), (0,)), ((), ())), preferred_element_type=F32)
        dsk = -jnp.exp(s_ref[...] - lrow) * delta
        dsk = jnp.broadcast_to(jnp.sum(dsk.reshape(G, W, 1), axis=1), (G, LANE))

        @pl.when(n == 0)
        def _():
            dk_ref[...] = jnp.zeros_like(dk_ref)
            dv_ref[...] = jnp.zeros_like(dv_ref)
            ds_ref[...] = jnp.zeros_like(ds_ref)

        rows = pl.ds(pl.multiple_of(n * W, W), 2 * W)
        dk_ref[rows, :] += dkk
        dv_ref[rows, :] += dvv
        ds_ref[...] += dsk

    qs = pl.BlockSpec((G, W, Dh), lambda g, n: (g, n, 0))
    prev = pl.BlockSpec((None, W, Dh), lambda g, n: (g, jnp.maximum(n - 1, 0), 0))
    cur = pl.BlockSpec((None, W, Dh), lambda g, n: (g, n, 0))
    lsp = pl.BlockSpec((G, W, 1), lambda g, n: (g, n, 0))
    kvo = pl.BlockSpec((None, S + W, Dh), lambda g, n: (g, 0, 0))
    return _call(
        body, name=name, grid=(SWA_KV_HEADS, nb),
        in_specs=[qs, prev, cur, prev, cur, pl.BlockSpec((None, G * W, 1), lambda g, n: (g, 0, 0)), qs, lsp, qs],
        out_specs=[qs, kvo, kvo, pl.BlockSpec((None, G, LANE), lambda g, n: (g, 0, 0))],
        out_shape=[jax.ShapeDtypeStruct((SWA_HEADS, S, Dh), BF16),
                   jax.ShapeDtypeStruct((SWA_KV_HEADS, S + W, Dh), F32),
                   jax.ShapeDtypeStruct((SWA_KV_HEADS, S + W, Dh), F32),
                   jax.ShapeDtypeStruct((SWA_KV_HEADS, G, LANE), F32)],
        compiler_params=_cparams(("parallel", "arbitrary")),
    )(qT, kT, kT, vT, vT, sink_rows, oT, L, doT)


def _adamw(w, g, m, v, *, name, tr=128, by_cols=False):
    L, R, C = w.shape
    split = isinstance(g, (list, tuple))
    HR, HC = _half_shape(R, C, by_cols) if split else (R, C)
    tr, tc = _tile2d(HR, HC, tr)
    nr, nc = HR // tr, HC // tc
    c1 = 1.0 / (1.0 - ADAM_B1 ** ADAM_STEP)
    c2 = 1.0 / (1.0 - ADAM_B2 ** ADAM_STEP)
    ng = 2 * L if split else 1

    def body(*refs):
        w_ref, g_refs, (m_ref, v_ref, go_ref, d_ref, mo_ref, vo_ref) = refs[0], refs[1:1 + ng], refs[1 + ng:]
        if split:
            mine = pl.program_id(1) == lax.axis_index("c")
            g_ = jnp.where(mine, g_refs[0][...], g_refs[1][...])
            for l in range(1, L):
                g_ = jnp.where(pl.program_id(0) == l,
                               jnp.where(mine, g_refs[2 * l][...], g_refs[2 * l + 1][...]), g_)
        else:
            g_ = g_refs[0][...]
        mn = ADAM_B1 * m_ref[...] + (1.0 - ADAM_B1) * g_
        vn = ADAM_B2 * v_ref[...] + (1.0 - ADAM_B2) * (g_ * g_)
        go_ref[...] = g_
        mo_ref[...] = mn
        vo_ref[...] = vn
        d_ref[...] = -ADAM_LR * ((mn * c1) / (jnp.sqrt(vn * c2) + ADAM_EPS) + ADAM_WD * w_ref[...])

    def whole(l, hf, i, j):
        return (l, i, hf * nc + j) if by_cols else (l, hf * nr + i, j)

    row = pl.BlockSpec((None, tr, tc), whole)
    half = pl.BlockSpec((tr, tc), lambda l, hf, i, j: (i, j))
    gs = [h for pair in g for h in pair] if split else [g]
    return _call(
        body, name=name, grid=(L, 2 if split else 1, nr, nc),
        in_specs=[row] + [half if split else row] * ng + [row, row],
        out_specs=[row] * 4, out_shape=[jax.ShapeDtypeStruct((L, R, C), F32)] * 4,
        compiler_params=_cparams(("parallel",) * 4),
    )(w, *gs, m, v)


def _sum2_halves(g4, s4, by_cols, *, name):
    n, R, C = g4.shape
    HR, HC = _half_shape(R, C, by_cols)
    tr, tc = _tile2d(HR, HC)
    nr, nc = HR // tr, HC // tc
    core = lax.axis_index("c").astype(jnp.int32).reshape(1)

    def body(c_ref, g_ref, s_ref, o_ref):
        o_ref[...] = (g_ref[...].astype(F32) + s_ref[...].astype(F32)).astype(BF16)

    def mine(k, i, j, c):
        return (k, i, c[0] * nc + j) if by_cols else (k, c[0] * nr + i, j)

    blk = pl.BlockSpec((None, tr, tc), lambda k, i, j, c: (k, i, j))
    return _call(
        body, name=name,
        grid_spec=pltpu.PrefetchScalarGridSpec(
            num_scalar_prefetch=1, grid=(n, nr, nc),
            in_specs=[pl.BlockSpec((None, tr, tc), mine), blk], out_specs=blk),
        out_shape=jax.ShapeDtypeStruct((n, HR, HC), BF16),
        compiler_params=_cparams(("parallel", "parallel", "parallel")),
    )(core, g4, s4)


def _rowsum(parts, *, name, out_dtype=F32):
    n, R, C = parts.shape
    tr, tc = _tile2d(R, C)

    def body(p_ref, o_ref):
        acc = p_ref[0].astype(F32)
        for i in range(1, n):
            acc = acc + p_ref[i].astype(F32)
        o_ref[...] = acc.astype(out_dtype)

    return _call(
        body, name=name, grid=(R // tr, C // tc),
        in_specs=[pl.BlockSpec((n, tr, tc), lambda i, j: (0, i, j))],
        out_specs=pl.BlockSpec((tr, tc), lambda i, j: (i, j)),
        out_shape=jax.ShapeDtypeStruct((R, C), out_dtype),
        compiler_params=_cparams(("parallel", "parallel")),
    )(parts)


def _where_am_i():
    x, y, c = lax.axis_index("x"), lax.axis_index("y"), lax.axis_index("c")
    chips = [(1 - x, y), (x, 1 - y), (1 - x, 1 - y)]
    return x, y, c, chips


def _half_idx(rows, cols, by_cols, which):
    if by_cols:
        hc = cols // 2
        return (slice(None), pl.ds(pl.multiple_of(which * hc, LANE), hc))
    hr = rows // 2
    return (pl.ds(pl.multiple_of(which * hr, 16), hr), slice(None))


def _half_shape(rows, cols, by_cols):
    return (rows, cols // 2) if by_cols else (rows // 2, cols)


def _all_gather_shards(shards, by_cols, *, name):
    n = len(shards)

    def body(*refs):
        ins, outs = refs[:n], refs[n:2 * n]
        send, recv = refs[2 * n:]
        x, y, c, chips = _where_am_i()
        me = 2 * x + y
        sibling = (x, y, 1 - c)

        def half(i, which):
            return _half_idx(*shards[i].shape, by_cols[i], which)

        def cp(i, k, src, dst, to):
            return pltpu.make_async_remote_copy(src_ref=src, dst_ref=dst, send_sem=send.at[i, k],
                                                recv_sem=recv.at[i, k], device_id=to, device_id_type=MESH)

        first = []
        for i in range(n):
            for k, (px, py) in enumerate(chips):
                d = cp(i, k, ins[i].at[half(i, c)], outs[i].at[(me,) + half(i, c)], (px, py, c))
                d.start()
                first.append(d)
        passed = []
        for i in range(n):
            for k, (px, py) in enumerate(chips):
                blk = outs[i].at[(2 * px + py,) + half(i, c)]
                cp(i, k, blk, blk, (px, py, c)).wait_recv()
                d = cp(i, 3 + k, blk, blk, sibling)
                d.start()
                passed.append(d)
        for i in range(n):
            for k, (px, py) in enumerate(chips):
                blk = outs[i].at[(2 * px + py,) + half(i, 1 - c)]
                cp(i, 3 + k, blk, blk, sibling).wait_recv()
        for d in first + passed:
            d.wait_send()

    got = _call(
        body, name=name, in_specs=[ANY] * n, out_specs=[ANY] * n,
        out_shape=[jax.ShapeDtypeStruct((N_CHIPS,) + s.shape, s.dtype) for s in shards],
        scratch_shapes=[pltpu.SemaphoreType.DMA((n, 6)), pltpu.SemaphoreType.DMA((n, 6))],
    )(*shards)
    me = 2 * lax.axis_index("x") + lax.axis_index("y")
    return [lax.dynamic_update_slice_in_dim(g, s[None], me, axis=0) for g, s in zip(got, shards)]


HBM_SPEC = pl.BlockSpec(memory_space=pltpu.HBM)
SEM_SPEC = pl.BlockSpec(memory_space=pltpu.SEMAPHORE)
DATAFLOW = pltpu.SideEffectType.DATAFLOW_SIDE_EFFECTING


def _chip_exchange_refs(kind, shards_shape, by_cols, src, land, i, chip_k, c, me):
    if kind == 'gather':
        half = _half_idx(*shards_shape, by_cols, c)
        return src.at[half], land.at[(me,) + half], land.at[(chip_k,) + half]
    return src.at[chip_k], land.at[me], land.at[chip_k]


def _chip_exchange_start(kind, srcs, by_cols, *, name, after=()):
    n = len(srcs)
    land_shapes = [((N_CHIPS,) + s.shape) if kind == 'gather' else s.shape for s in srcs]

    def body(*refs):
        src_refs, land_refs = refs[:n], refs[n:2 * n]
        send, recv = refs[2 * n + len(after)], refs[2 * n + len(after) + 1]
        token = refs[-1]
        x, y, c, chips = _where_am_i()
        me = 2 * x + y
        for i in range(n):
            for k, (px, py) in enumerate(chips):
                s, d, _ = _chip_exchange_refs(kind, srcs[i].shape, by_cols[i], src_refs[i], land_refs[i], i,
                                              2 * px + py, c, me)
                pltpu.make_async_remote_copy(src_ref=s, dst_ref=d, send_sem=send.at[3 * i + k],
                                             recv_sem=recv.at[3 * i + k], device_id=(px, py, c),
                                             device_id_type=MESH).start()
        token[...] = jnp.zeros_like(token)

    lands = [pltpu.with_memory_space_constraint(lax.empty(sh, s.dtype), pltpu.HBM) for sh, s in zip(land_shapes, srcs)]
    outs = _call(
        body, name=name,
        out_shape=(pltpu.SemaphoreType.DMA((3 * n,)), pltpu.SemaphoreType.DMA((3 * n,)),
                   *[pltpu.HBM(s.shape, s.dtype) for s in srcs],
                   *[pltpu.HBM(sh, s.dtype) for sh, s in zip(land_shapes, srcs)],
                   jax.ShapeDtypeStruct((8, LANE), F32)),
        in_specs=[HBM_SPEC] * (2 * n) + [ANY] * len(after),
        out_specs=(SEM_SPEC, SEM_SPEC, *([HBM_SPEC] * (2 * n)), pl.BlockSpec(memory_space=pltpu.VMEM)),
        input_output_aliases={j: 2 + j for j in range(2 * n)},
        compiler_params=pltpu.CompilerParams(has_side_effects=DATAFLOW),
    )(*[pltpu.with_memory_space_constraint(s, pltpu.HBM) for s in srcs], *lands, *after)
    return outs[0], outs[1], list(outs[2:2 + n]), list(outs[2 + n:2 + 2 * n]), outs[-1]


def _chip_exchange_wait(kind, send, recv, srcs, lands, by_cols, after, *, name):
    n = len(srcs)

    def body(*refs):
        src_refs, land_refs = refs[:n], refs[n:2 * n]
        send_r, recv_r = refs[2 * n], refs[2 * n + 1]
        x, y, c, chips = _where_am_i()
        me = 2 * x + y
        for i in range(n):
            for k, (px, py) in enumerate(chips):
                s, _, d = _chip_exchange_refs(kind, srcs[i].shape, by_cols[i], src_refs[i], land_refs[i], i,
                                              2 * px + py, c, me)
                cp = pltpu.make_async_remote_copy(src_ref=s, dst_ref=d, send_sem=send_r.at[3 * i + k],
                                                  recv_sem=recv_r.at[3 * i + k], device_id=(px, py, c),
                                                  device_id_type=MESH)
                cp.wait_send()
                cp.wait_recv()

    outs = _call(
        body, name=name,
        out_shape=(*[pltpu.HBM(s.shape, s.dtype) for s in srcs], *[pltpu.HBM(l.shape, l.dtype) for l in lands]),
        in_specs=[HBM_SPEC] * (2 * n) + [SEM_SPEC, SEM_SPEC, ANY],
        out_specs=tuple([HBM_SPEC] * (2 * n)),
        input_output_aliases={j: j for j in range(2 * n)},
        compiler_params=pltpu.CompilerParams(has_side_effects=DATAFLOW),
    )(*srcs, *lands, send, recv, after)
    return list(outs[n:])


def _sibling_pass_gathered(lands, shard_shapes, by_cols, *, name):
    n = len(lands)

    def body(*refs):
        outs = refs[n:2 * n]
        send, recv = refs[2 * n:]
        x, y, c, chips = _where_am_i()
        sibling = (x, y, 1 - c)
        cps = []
        for i in range(n):
            for k, (px, py) in enumerate(chips):
                blk = outs[i].at[(2 * px + py,) + _half_idx(*shard_shapes[i], by_cols[i], c)]
                d = pltpu.make_async_remote_copy(src_ref=blk, dst_ref=blk, send_sem=send.at[i, k],
                                                 recv_sem=recv.at[i, k], device_id=sibling, device_id_type=MESH)
                d.start()
                cps.append(d)
        for i in range(n):
            for k, (px, py) in enumerate(chips):
                blk = outs[i].at[(2 * px + py,) + _half_idx(*shard_shapes[i], by_cols[i], 1 - c)]
                pltpu.make_async_remote_copy(src_ref=blk, dst_ref=blk, send_sem=send.at[i, k], recv_sem=recv.at[i, k],
                                             device_id=sibling, device_id_type=MESH).wait_recv()
        for d in cps:
            d.wait_send()

    return _call(
        body, name=name, in_specs=[ANY] * n, out_specs=[ANY] * n,
        out_shape=[jax.ShapeDtypeStruct(l.shape, l.dtype) for l in lands],
        input_output_aliases={j: j for j in range(n)},
        scratch_shapes=[pltpu.SemaphoreType.DMA((n, 3)), pltpu.SemaphoreType.DMA((n, 3))],
    )(*lands)


def _own_slot(lands, owns):
    me = 2 * lax.axis_index("x") + lax.axis_index("y")
    return [lax.dynamic_update_slice_in_dim(g, s, me, axis=0) for g, s in zip(lands, owns)]


def _sibling_send_halves(grads, by_cols, *, name):
    n = len(grads)

    def body(*refs):
        ins, outs = refs[:n], refs[n:2 * n]
        send, recv = refs[2 * n:]
        x, y, c, _ = _where_am_i()
        sibling = (x, y, 1 - c)
        cps = []
        for i in range(n):
            src = ins[i].at[(slice(None),) + _half_idx(*grads[i].shape[1:], by_cols[i], 1 - c)]
            d = pltpu.make_async_remote_copy(src_ref=src, dst_ref=outs[i], send_sem=send.at[i],
                                             recv_sem=recv.at[i], device_id=sibling, device_id_type=MESH)
            d.start()
            cps.append(d)
        for d in cps:
            d.wait()

    return _call(
        body, name=name, in_specs=[ANY] * n, out_specs=[ANY] * n,
        out_shape=[jax.ShapeDtypeStruct((N_CHIPS,) + _half_shape(*g.shape[1:], bc), g.dtype)
                   for g, bc in zip(grads, by_cols)],
        scratch_shapes=[pltpu.SemaphoreType.DMA((n,)), pltpu.SemaphoreType.DMA((n,))],
    )(*grads)


def _scatter_to_chips(parts, *, name):
    n = len(parts)

    def body(*refs):
        ins, outs = refs[:n], refs[n:2 * n]
        send, recv = refs[2 * n:]
        x, y, c, chips = _where_am_i()
        me = 2 * x + y
        cps = []
        for i in range(n):
            for k, (px, py) in enumerate(chips):
                d = pltpu.make_async_remote_copy(
                    src_ref=ins[i].at[2 * px + py], dst_ref=outs[i].at[me], send_sem=send.at[i, k],
                    recv_sem=recv.at[i, k], device_id=(px, py, c), device_id_type=MESH)
                d.start()
                cps.append((d, i, k, px, py))
        for d, i, k, px, py in cps:
            blk = outs[i].at[2 * px + py]
            pltpu.make_async_remote_copy(src_ref=blk, dst_ref=blk, send_sem=send.at[i, k], recv_sem=recv.at[i, k],
                                         device_id=(px, py, c), device_id_type=MESH).wait_recv()
        for d, *_ in cps:
            d.wait_send()

    got = _call(
        body, name=name, in_specs=[ANY] * n, out_specs=[ANY] * n,
        out_shape=[jax.ShapeDtypeStruct(p.shape, p.dtype) for p in parts],
        scratch_shapes=[pltpu.SemaphoreType.DMA((n, 3)), pltpu.SemaphoreType.DMA((n, 3))],
    )(*parts)
    me = 2 * lax.axis_index("x") + lax.axis_index("y")
    return [lax.dynamic_update_slice_in_dim(g, lax.dynamic_slice_in_dim(p, me, 1, axis=0), me, axis=0)
            for g, p in zip(got, parts)]


def _sibling_join_halves(halves, *, name):
    n = len(halves)

    def body(*refs):
        ins, outs = refs[:n], refs[n:2 * n]
        send, recv = refs[2 * n:]
        x, y, c, _ = _where_am_i()
        sibling = (x, y, 1 - c)
        cps = []
        for i in range(n):
            d = pltpu.make_async_remote_copy(src_ref=ins[i], dst_ref=outs[i], send_sem=send.at[i],
                                             recv_sem=recv.at[i], device_id=sibling, device_id_type=MESH)
            d.start()
            cps.append(d)
        for d in cps:
            d.wait()

    return _call(
        body, name=name, in_specs=[ANY] * n, out_specs=[ANY] * n,
        out_shape=[jax.ShapeDtypeStruct(h.shape, h.dtype) for h in halves],
        scratch_shapes=[pltpu.SemaphoreType.DMA((n,)), pltpu.SemaphoreType.DMA((n,))],
    )(*halves)


def _all_reduce_small(v, *, name):
    R, C = v.shape

    def body(v_ref, o_ref, sib, slots, send, recv):
        x, y, c, chips = _where_am_i()
        me = 2 * x + y
        sibling = (x, y, 1 - c)
        d = pltpu.make_async_remote_copy(src_ref=v_ref, dst_ref=sib, send_sem=send.at[0], recv_sem=recv.at[0],
                                         device_id=sibling, device_id_type=MESH)
        d.start()
        d.wait()
        slots[me] = v_ref[...] + sib[...]
        cps = []
        for k, (px, py) in enumerate(chips):
            d = pltpu.make_async_remote_copy(src_ref=slots.at[me], dst_ref=slots.at[me], send_sem=send.at[1 + k],
                                             recv_sem=recv.at[1 + k], device_id=(px, py, c), device_id_type=MESH)
            d.start()
            cps.append(d)
        for k, (px, py) in enumerate(chips):
            blk = slots.at[2 * px + py]
            pltpu.make_async_remote_copy(src_ref=blk, dst_ref=blk, send_sem=send.at[1 + k], recv_sem=recv.at[1 + k],
                                         device_id=(px, py, c), device_id_type=MESH).wait_recv()
        for d in cps:
            d.wait_send()
        o_ref[...] = (slots[0] + slots[1]) + (slots[2] + slots[3])

    vm = pl.BlockSpec(memory_space=pltpu.VMEM)
    return _call(
        body, name=name, in_specs=[vm], out_specs=vm,
        out_shape=jax.ShapeDtypeStruct((R, C), F32),
        scratch_shapes=[pltpu.VMEM((R, C), F32), pltpu.VMEM((N_CHIPS, R, C), F32),
                        pltpu.SemaphoreType.DMA((4,)), pltpu.SemaphoreType.DMA((4,))],
        compiler_params=pltpu.CompilerParams(vmem_limit_bytes=VMEM_LIMIT),
    )(v)


def _cols_from_shards(g):
    return jnp.transpose(g, (1, 0, 2)).reshape(g.shape[1], -1)


def _shards_from_cols(w):
    R, C4 = w.shape
    return jnp.transpose(w.reshape(R, N_CHIPS, C4 // N_CHIPS), (1, 0, 2))


def _block_diag(t):
    G, a, b = t.shape
    eye = jnp.eye(G, dtype=t.dtype)
    return (t[:, :, None, :] * eye[:, None, :, None]).reshape(G * a, G * b)


def _diag_blocks(xm, G):
    a, b = xm.shape[0] // G, xm.shape[1] // G
    idx = jnp.arange(G)
    return xm.reshape(G, a, G, b)[idx, :, idx, :]


def _pack(arrs):
    flat = []
    for a in arrs:
        f = a.reshape(-1).astype(F32)
        flat.append(jnp.pad(f, (0, _rup(f.shape[0], LANE) - f.shape[0])))
    v = jnp.concatenate(flat)
    rows = _rup(v.shape[0] // LANE, 8)
    v = jnp.pad(v, (0, rows * LANE - v.shape[0]))
    return v.reshape(rows, LANE)


def _unpack(v, shapes):
    flat = v.reshape(-1)
    out, off = [], 0
    for s in shapes:
        n = int(np.prod(s))
        out.append(flat[off:off + n].reshape(s))
        off += _rup(n, LANE)
    return out


def _ffn_fwd(x, Wup, Wdn, cw, cb, tag):
    h = _mm(x, Wup, 'nt', bmode='bo', tm=512, tn=4096, name=f"ffn_up_{tag}")
    a = _act_fwd(h, cw, cb, name=f"ffn_act_{tag}")
    f = _mm(a, Wdn, 'nn', bmode='abr', tm=512, tn=1024, tk=4096, name=f"ffn_down_{tag}")
    return f, h, a


def _ffn_bwd(df, x, h, a, Wup, Wdn, cw, cb, tag):
    da = _mm(df, Wdn, 'nt', bmode='bo', tm=512, tn=4096, name=f"ffn_da_{tag}")
    dWdn = _mm(a, df, 'tn', bmode='ao', tm=4096, tn=512, name=f"ffn_dwdn_{tag}", out_dtype=BF16)
    dh, dcw, dcb = _act_bwd(h, da, cw, cb, name=f"ffn_actb_{tag}")
    dx = _mm(dh, Wup, 'nn', bmode='abr', tm=512, tn=1024, tk=4096, name=f"ffn_dx_{tag}")
    dWup = _mm(dh, x, 'tn', bmode='ao', tm=4096, tn=512, name=f"ffn_dwup_{tag}", out_dtype=BF16)
    return dx, dWup, dWdn, dcw, dcb


def kernel(x, positions, ev_w_in, ev_b_f, ev_lambda_re, ev_lambda_im, ev_log_step, ev_ssm_b_re, ev_ssm_b_im, ev_ssm_c_re, ev_ssm_c_im, ev_ssm_d, ev_w_glu, ev_w_out, od_w_in, od_sinks, od_w_out, ln_mix_g, ln_mix_b, ffn_w_up, ffn_conv_w, ffn_conv_b, ffn_w_down, ln_ffn_g, ln_ffn_b, loss_target, m_ev_w_in, m_ev_b_f, m_ev_lambda_re, m_ev_lambda_im, m_ev_log_step, m_ev_ssm_b_re, m_ev_ssm_b_im, m_ev_ssm_c_re, m_ev_ssm_c_im, m_ev_ssm_d, m_ev_w_glu, m_ev_w_out, m_od_w_in, m_od_sinks, m_od_w_out, m_ln_mix_g, m_ln_mix_b, m_ffn_w_up, m_ffn_conv_w, m_ffn_conv_b, m_ffn_w_down, m_ln_ffn_g, m_ln_ffn_b, v_ev_w_in, v_ev_b_f, v_ev_lambda_re, v_ev_lambda_im, v_ev_log_step, v_ev_ssm_b_re, v_ev_ssm_b_im, v_ev_ssm_c_re, v_ev_ssm_c_im, v_ev_ssm_d, v_ev_w_glu, v_ev_w_out, v_od_w_in, v_od_sinks, v_od_w_out, v_ln_mix_g, v_ln_mix_b, v_ffn_w_up, v_ffn_conv_w, v_ffn_conv_b, v_ffn_w_down, v_ln_ffn_g, v_ln_ffn_b):
    W = dict(ev_w_in=ev_w_in, ev_b_f=ev_b_f, ev_lambda_re=ev_lambda_re, ev_lambda_im=ev_lambda_im, ev_log_step=ev_log_step, ev_ssm_b_re=ev_ssm_b_re, ev_ssm_b_im=ev_ssm_b_im, ev_ssm_c_re=ev_ssm_c_re, ev_ssm_c_im=ev_ssm_c_im, ev_ssm_d=ev_ssm_d, ev_w_glu=ev_w_glu, ev_w_out=ev_w_out, od_w_in=od_w_in, od_sinks=od_sinks, od_w_out=od_w_out, ln_mix_g=ln_mix_g, ln_mix_b=ln_mix_b, ffn_w_up=ffn_w_up, ffn_conv_w=ffn_conv_w, ffn_conv_b=ffn_conv_b, ffn_w_down=ffn_w_down, ln_ffn_g=ln_ffn_g, ln_ffn_b=ln_ffn_b)
    Mo = dict(ev_w_in=m_ev_w_in, ev_b_f=m_ev_b_f, ev_lambda_re=m_ev_lambda_re, ev_lambda_im=m_ev_lambda_im, ev_log_step=m_ev_log_step, ev_ssm_b_re=m_ev_ssm_b_re, ev_ssm_b_im=m_ev_ssm_b_im, ev_ssm_c_re=m_ev_ssm_c_re, ev_ssm_c_im=m_ev_ssm_c_im, ev_ssm_d=m_ev_ssm_d, ev_w_glu=m_ev_w_glu, ev_w_out=m_ev_w_out, od_w_in=m_od_w_in, od_sinks=m_od_sinks, od_w_out=m_od_w_out, ln_mix_g=m_ln_mix_g, ln_mix_b=m_ln_mix_b, ffn_w_up=m_ffn_w_up, ffn_conv_w=m_ffn_conv_w, ffn_conv_b=m_ffn_conv_b, ffn_w_down=m_ffn_w_down, ln_ffn_g=m_ln_ffn_g, ln_ffn_b=m_ln_ffn_b)
    Vo = dict(ev_w_in=v_ev_w_in, ev_b_f=v_ev_b_f, ev_lambda_re=v_ev_lambda_re, ev_lambda_im=v_ev_lambda_im, ev_log_step=v_ev_log_step, ev_ssm_b_re=v_ev_ssm_b_re, ev_ssm_b_im=v_ev_ssm_b_im, ev_ssm_c_re=v_ev_ssm_c_re, ev_ssm_c_im=v_ev_ssm_c_im, ev_ssm_d=v_ev_ssm_d, ev_w_glu=v_ev_w_glu, ev_w_out=v_ev_w_out, od_w_in=v_od_w_in, od_sinks=v_od_sinks, od_w_out=v_od_w_out, ln_mix_g=v_ln_mix_g, ln_mix_b=v_ln_mix_b, ffn_w_up=v_ffn_w_up, ffn_conv_w=v_ffn_conv_w, ffn_conv_b=v_ffn_conv_b, ffn_w_down=v_ffn_w_down, ln_ffn_g=v_ln_ffn_g, ln_ffn_b=v_ln_ffn_b)
    names = list(W.keys())
    big = ['ev_w_in', 'ev_w_glu', 'ev_w_out', 'od_w_in', 'od_w_out', 'ffn_w_up', 'ffn_w_down']

    S, D = x.shape[1], x.shape[2]
    x0 = x.reshape(S, D)
    tgt = loss_target.reshape(S, D)
    G, Pn, Cg = SSM_GROUPS, SSM_STATE, SSM_GROUP
    Fs = ffn_w_up.shape[2]
    FP = Fs
    Rd = ffn_w_down.shape[1]
    EIN = N_CHIPS * ev_w_in.shape[2]

    def as2d(a):
        return a.reshape(-1, a.shape[-1])

    cwl = ffn_conv_w.reshape(-1)
    cw_rows = _rup(_rup(cwl.shape[0], LANE) // LANE, 32)
    cw_pad = jnp.pad(cwl, (0, cw_rows * LANE - cwl.shape[0])).reshape(cw_rows, LANE)
    transposed = ('ev_w_in', 'ffn_w_up')

    def view(n, a):
        return jnp.transpose(a, (0, 2, 1)) if n in transposed else a

    Wv = {n: view(n, W[n]) for n in big}
    big_e = [(n, l) for n in big for l in range(W[n].shape[0])]
    split_cols = {e: (Wv[e[0]].shape[1] // 2) % 16 != 0 for e in big_e}
    shard16 = {e: Wv[e[0]][e[1]].astype(BF16) for e in big_e}
    grp_now = [e for e in big_e if e[0].startswith('ev_')]
    grp_ffn0 = [('ffn_w_up', 0), ('ffn_w_down', 0)]
    grp_l1 = [('od_w_in', 0), ('od_w_out', 0), ('ffn_w_up', 1), ('ffn_w_down', 1)]
    gw = dict(zip(grp_now, _all_gather_shards([shard16[e] for e in grp_now], [split_cols[e] for e in grp_now],
                                              name="ag_l0")))
    src_ffn0 = [shard16[e] for e in grp_ffn0] + [cw_pad]
    src_l1 = [shard16[e] for e in grp_l1]
    cols_ffn0 = [split_cols[e] for e in grp_ffn0] + [False]
    cols_l1 = [split_cols[e] for e in grp_l1]
    ag_ffn0 = _chip_exchange_start('gather', src_ffn0, cols_ffn0, name="ag_ffn0_start",
                                   after=[gw[('ev_w_out', 0)]])
    ag_l1 = _chip_exchange_start('gather', src_l1, cols_l1, name="ag_l1_start", after=[ag_ffn0[4]])
    started = [ag_l1[4]]

    def finish_gather(started, srcs, cols, after, tag):
        send, recv, thru, lands, _ = started
        lands = _chip_exchange_wait('gather', send, recv, thru, lands, cols, after, name=f"ag_{tag}_wait")
        lands = _sibling_pass_gathered(lands, [s.shape for s in srcs], cols, name=f"ag_{tag}_pass")
        return _own_slot(lands, [s[None] for s in srcs])

    gw.update({n: gw[(n, 0)] for n in big if (n, 0) in gw and W[n].shape[0] == 1})
    w_in_t = gw['ev_w_in'].reshape(EIN, D)
    qkv_w = 3 * FOX_WIDTH
    WmainT = jnp.concatenate([w_in_t[:qkv_w], w_in_t[qkv_w + FOX_HEADS:]], axis=0)
    WfT = jnp.pad(w_in_t[qkv_w:qkv_w + FOX_HEADS], ((0, LANE - FOX_HEADS), (0, 0)))
    Wglu = _cols_from_shards(gw['ev_w_glu'])
    Wout_ev = gw['ev_w_out'].reshape(D, D)
    cbs = [ffn_conv_b[l].reshape(N_CHIPS, Fs) for l in range(DEPTH)]

    lam_r, lam_i = ev_lambda_re[0], ev_lambda_im[0]
    lstep = ev_log_step[0].reshape(G, 1)
    a_re, a_im, g_re, g_im = _s5_disc_fwd(lam_r, lam_i, lstep, name="s5_disc")
    b_re2, b_im2 = ev_ssm_b_re[0].reshape(G * Pn, Cg), ev_ssm_b_im[0].reshape(G * Pn, Cg)
    g_re1, g_im1 = g_re.reshape(G * Pn, 1), g_im.reshape(G * Pn, 1)
    bb_re, bb_im = _s5_bb_fwd(g_re1, g_im1, b_re2, b_im2, name="s5_bb")
    bbt = jnp.stack([jnp.transpose(b.reshape(G, Pn, Cg), (0, 2, 1)).reshape(G * Cg, Pn) for b in (bb_re, bb_im)])
    BB = _diag_expand(bbt, Cg, Pn, name="s5_bb_dense")
    cct = jnp.stack([jnp.transpose(ev_ssm_c_re[0], (0, 2, 1)).reshape(G * Pn, Cg),
                     jnp.transpose(-ev_ssm_c_im[0], (0, 2, 1)).reshape(G * Pn, Cg)])
    CC = _diag_expand(cct, Pn, Cg, name="s5_cc_dense")
    a_cat = jnp.stack([a_re.reshape(1, G * Pn), a_im.reshape(1, G * Pn)])
    dskip = ev_ssm_d[0].reshape(1, SSM_WIDTH)

    P = _mm(x0, WmainT, 'nt', name="ev_proj", after=started)
    fl = _mm(x0, WfT, 'nt', name="ev_proj_f", after=started)
    bf_pad = jnp.pad(ev_b_f.reshape(1, FOX_HEADS), ((0, 0), (0, LANE - FOX_HEADS)))
    cgate, sgate = _gate_fwd(fl, bf_pad, name="fox_gate")
    ccol = jnp.transpose(cgate[:, :FOX_HEADS]).reshape(FOX_HEADS, S, 1)
    crow = jnp.transpose(cgate[:, :FOX_HEADS]).reshape(FOX_HEADS, 1, S)
    fox, lse = _fox_fwd(P, ccol, crow, name="fox_fwd")
    u_s5 = P[:, qkv_w:]
    bu = _mm(u_s5, BB, 'nn', bmode='bo', name="s5_bu")
    hh = _s5_scan_fwd(bu, a_cat, name="s5_scan")
    yc = _mm(hh, CC, 'nn', bmode='abr', name="s5_y")
    y_s5, yg = _s5_out_fwd(yc, P, dskip, name="s5_out")
    z = _mm(yg, Wglu, 'nn', name="s5_glu_proj")
    ssm = _glu_fwd(z, name="s5_glu")
    cat = jnp.concatenate([fox.astype(BF16), ssm], axis=1)
    mix0 = _mm(cat, Wout_ev, 'nn', name="ev_out")
    x1, xh1, rs1 = _add_ln_fwd(x0, mix0, ln_mix_g[0], ln_mix_b[0], name="ln_mix0")
    got = finish_gather(ag_ffn0, src_ffn0, cols_ffn0, x1, "ffn0")
    gw.update(zip(grp_ffn0, got[:-1]))
    cw_all = got[-1].reshape(N_CHIPS, -1)[:, :cwl.shape[0]].reshape(N_CHIPS, DEPTH, 3, Fs)
    cws = [cw_all[:, l] for l in range(DEPTH)]
    Wup = {0: gw[('ffn_w_up', 0)]}
    Wdn = {0: gw[('ffn_w_down', 0)].reshape(2, Fs, D)}
    f0, hf0, af0 = _ffn_fwd(x1, Wup[0], Wdn[0], cws[0], cbs[0], "l0")
    x2, xh2, rs2 = _add_ln_fwd(x1, f0, ln_ffn_g[0], ln_ffn_b[0], name="ln_ffn0")

    gw.update(zip(grp_l1, finish_gather(ag_l1, src_l1, cols_l1, x2, "l1")))
    Wodin = _cols_from_shards(gw[('od_w_in', 0)])
    Wodout = gw[('od_w_out', 0)].reshape(D, D)
    Wup[1] = gw[('ffn_w_up', 1)]
    Wdn[1] = gw[('ffn_w_down', 1)].reshape(2, Fs, D)
    QW, KW = SWA_HEADS * SWA_HEAD_DIM, SWA_KV_HEADS * SWA_HEAD_DIM
    P1 = _mm(x2, Wodin, 'nn', name="od_proj")
    tabs = _rope_tables(positions.reshape(S, 1).astype(F32), name="rope_tables")
    qr = _rope_apply(P1, tabs, col0=0, width=QW, inverse=False, name="rope_q", out_dtype=BF16)
    kr = _rope_apply(P1, tabs, col0=QW, width=KW, inverse=False, name="rope_k", out_dtype=BF16)

    def heads(a2, nh):
        return jnp.transpose(a2.reshape(S, nh, SWA_HEAD_DIM), (1, 0, 2))

    def unheads(a3):
        return jnp.transpose(a3, (1, 0, 2)).reshape(S, -1)

    qT, kT = heads(qr, SWA_HEADS), heads(kr, SWA_KV_HEADS)
    vT = heads(P1[:, QW + KW:].astype(BF16), SWA_KV_HEADS)
    sink_rows = jnp.broadcast_to(od_sinks[0].reshape(SWA_KV_HEADS, SWA_GROUPS, 1, 1),
                                 (SWA_KV_HEADS, SWA_GROUPS, SWA_WINDOW, 1)).reshape(SWA_KV_HEADS, -1, 1)
    oT, Lsw = _swa_fwd(qT, kT, vT, sink_rows, name="swa_fwd")
    o_sw = unheads(oT).astype(BF16)
    mix1 = _mm(o_sw, Wodout, 'nn', name="od_out")
    x3, xh3, rs3 = _add_ln_fwd(x2, mix1, ln_mix_g[1], ln_mix_b[1], name="ln_mix1")
    f1, hf1, af1 = _ffn_fwd(x3, Wup[1], Wdn[1], cws[1], cbs[1], "l1")
    x4, xh4, rs4 = _add_ln_fwd(x3, f1, ln_ffn_g[1], ln_ffn_b[1], name="ln_ffn1")
    dy, loss_part = _loss_grad(x4, tgt, name="loss")

    dz4, dg_ffn1, db_ffn1 = _ln_bwd(dy, None, xh4, rs4, ln_ffn_g[1], name="lnb_ffn1")
    dx3f, dWup1, dWdn1, dcw1, dcb1 = _ffn_bwd(dz4, x3, hf1, af1, Wup[1], Wdn[1], cws[1], cbs[1], "l1")
    dz3, dg_mix1, db_mix1 = _ln_bwd(dz4, dx3f, xh3, rs3, ln_mix_g[1], name="lnb_mix1")
    do_sw = _mm(dz3, Wodout, 'nt', name="od_out_dx")
    dWodout = _mm(o_sw, dz3, 'tn', name="od_out_dw", out_dtype=BF16)
    doT = heads(do_sw, SWA_HEADS)
    dqT, dkT, dvT, dsink = _swa_bwd(qT, kT, vT, sink_rows, oT, Lsw, doT, name="swa_bwd")
    dq1 = _rope_apply(unheads(dqT), tabs, col0=0, width=QW, inverse=True, name="rope_dq", out_dtype=BF16)
    dk1 = _rope_apply(unheads(dkT[:, SWA_WINDOW:]), tabs, col0=0, width=KW, inverse=True, name="rope_dk",
                      out_dtype=BF16)
    dP1 = jnp.concatenate([dq1, dk1, unheads(dvT[:, SWA_WINDOW:]).astype(BF16)], axis=1)
    dx2m = _mm(dP1, Wodin, 'nt', name="od_proj_dx")
    dWodin = _mm(x2, dP1, 'tn', name="od_proj_dw", out_dtype=BF16)

    def rs_begin(entries, grads, tag):
        cols = [split_cols[e] for e in entries]
        sib = _sibling_send_halves(grads, cols, name=f"rs_{tag}_sibling")
        return [_sum2_halves(g4, s4, bc, name=f"rs_sum2_{n}{l}")
                for (n, l), g4, s4, bc in zip(entries, grads, sib, cols)]

    def own_parts(parts):
        me = 2 * lax.axis_index("x") + lax.axis_index("y")
        return [lax.dynamic_slice_in_dim(p, me, 1, axis=0) for p in parts]

    part_l1 = rs_begin(grp_l1, [_shards_from_cols(dWodin), dWodout.reshape(N_CHIPS, D // N_CHIPS, D), dWup1,
                                dWdn1.reshape(N_CHIPS, Rd, D)], "l1")
    rs_l1 = _chip_exchange_start('scatter', part_l1, [False] * len(part_l1), name="rs_l1_start")

    dz2, dg_ffn0, db_ffn0 = _ln_bwd(dz3, dx2m, xh2, rs2, ln_ffn_g[0], name="lnb_ffn0", after=[rs_l1[4]])
    dx1f, dWup0, dWdn0, dcw0, dcb0 = _ffn_bwd(dz2, x1, hf0, af0, Wup[0], Wdn[0], cws[0], cbs[0], "l0")
    part_ffn0 = rs_begin(grp_ffn0, [dWup0, dWdn0.reshape(N_CHIPS, Rd, D)], "ffn0")
    rs_ffn0 = _chip_exchange_start('scatter', part_ffn0, [False] * len(part_ffn0), name="rs_ffn0_start")
    dz1, dg_mix0, db_mix0 = _ln_bwd(dz2, dx1f, xh1, rs1, ln_mix_g[0], name="lnb_mix0", after=[rs_ffn0[4]])
    dcat = _mm(dz1, Wout_ev, 'nt', name="ev_out_dx")
    dWout_ev = _mm(cat, dz1, 'tn', name="ev_out_dw", out_dtype=BF16)
    dz = _glu_bwd(z, dcat, name="s5_glu_bwd")
    dyg = _mm(dz, Wglu, 'nt', name="s5_glu_dx")
    dWglu = _mm(yg, dz, 'tn', name="s5_glu_dw", out_dtype=BF16)
    dy_s5, du_dir, dD = _s5_out_bwd(dyg, y_s5, P, dskip, name="s5_out_bwd")
    dhh = _mm(dy_s5, CC, 'nt', bmode='bo', name="s5_y_dx")
    dCC = _mm(hh, dy_s5, 'tn', bmode='ao', name="s5_y_dw")
    lam, da_s5 = _s5_scan_bwd(dhh, hh, a_cat, name="s5_scan_bwd")
    du_bu = _mm(lam, BB, 'nt', bmode='abr', name="s5_bu_dx")
    dBB = _mm(u_s5, lam, 'tn', bmode='bo', name="s5_bu_dw")
    du = _combine([du_dir, du_bu], [1.0, 1.0], name="s5_du", out_dtype=BF16)
    dq0, dk0, dv0, dccol, dcrow = _fox_bwd(P, ccol, crow, fox, lse, dcat, name="fox_bwd")
    dc = jnp.transpose((dccol.reshape(FOX_HEADS, S) - dcrow.reshape(FOX_HEADS, S)))
    dc = jnp.pad(dc, ((0, 0), (0, LANE - FOX_HEADS)))
    dfl, dbf = _gate_bwd(dc, sgate, name="fox_gate_bwd")
    dP = jnp.concatenate([dq0, dk0, dv0, du], axis=1)
    dx0a = _mm(dP, WmainT, 'nn', name="ev_proj_dx")
    dx0b = _mm(dfl, WfT, 'nn', name="ev_proj_f_dx")
    dWmainT = _mm(dP, x0, 'tn', tm=1024, tn=1024, name="ev_proj_dw", out_dtype=BF16)
    dWfT = _mm(dfl, x0, 'tn', name="ev_proj_f_dw", out_dtype=BF16)
    grad_x = _combine([dz1, dx0a, dx0b], [ALPHA, 1.0, 1.0], name="grad_x")

    dbbt = _diag_extract(dBB, Cg, Pn, name="s5_bb_diag")
    dcct = _diag_extract(dCC, Pn, Cg, name="s5_cc_diag")
    dbb_re = jnp.transpose(dbbt[0].reshape(G, Cg, Pn), (0, 2, 1)).reshape(G * Pn, Cg)
    dbb_im = jnp.transpose(dbbt[1].reshape(G, Cg, Pn), (0, 2, 1)).reshape(G * Pn, Cg)
    db_re, db_im, dg_re1, dg_im1 = _s5_bb_bwd(g_re1, g_im1, b_re2, b_im2, dbb_re, dbb_im, name="s5_bb_bwd")
    dlam_re, dlam_im, dlstep = _s5_disc_bwd(lam_r, lam_i, lstep, da_s5[0].reshape(G, Pn), da_s5[1].reshape(G, Pn),
                                            dg_re1.reshape(G, Pn), dg_im1.reshape(G, Pn), name="s5_disc_bwd")
    dc_re = jnp.transpose(dcct[0].reshape(G, Pn, Cg), (0, 2, 1))
    dc_im = -jnp.transpose(dcct[1].reshape(G, Pn, Cg), (0, 2, 1))

    def conv_w_full(d0, d1):
        return jnp.stack([jnp.reshape(jnp.transpose(d[:, :, :Fs], (1, 0, 2)), (3, N_CHIPS * Fs)) for d in (d0, d1)])

    def conv_b_full(d0, d1):
        return jnp.stack([jnp.reshape(d[:, 0, :Fs], (N_CHIPS * Fs,)) for d in (d0, d1)])

    small_local = dict(
        ev_b_f=dbf[:, :FOX_HEADS], ev_lambda_re=dlam_re, ev_lambda_im=dlam_im, ev_log_step=dlstep,
        ev_ssm_b_re=db_re, ev_ssm_b_im=db_im, ev_ssm_c_re=dc_re, ev_ssm_c_im=dc_im, ev_ssm_d=dD,
        od_sinks=dsink[:, :, 0],
        ln_mix_g=jnp.concatenate([dg_mix0, dg_mix1]), ln_mix_b=jnp.concatenate([db_mix0, db_mix1]),
        ffn_conv_w=conv_w_full(dcw0, dcw1), ffn_conv_b=conv_b_full(dcb0, dcb1),
        ln_ffn_g=jnp.concatenate([dg_ffn0, dg_ffn1]), ln_ffn_b=jnp.concatenate([db_ffn0, db_ffn1]))
    small = list(small_local.keys())
    red = _all_reduce_small(_pack([small_local[n] for n in small] + [loss_part]), name="ar_small")
    full_shapes = [W[n].shape if n != 'ffn_conv_w' else (DEPTH, 3, N_CHIPS * Fs) for n in small]
    pieces = _unpack(red, full_shapes + [()])
    loss = pieces[-1]
    gsmall = dict(zip(small, pieces[:-1]))
    chip = 2 * lax.axis_index("x") + lax.axis_index("y")
    gsmall['ffn_conv_w'] = lax.dynamic_slice_in_dim(gsmall['ffn_conv_w'], chip * Fs, Fs, axis=2)
    shapes = [W[n].shape for n in small]
    gs, ds_, ms, vs = _adamw(_pack([W[n] for n in small])[None], _pack([gsmall[n] for n in small])[None],
                             _pack([Mo[n] for n in small])[None], _pack([Vo[n] for n in small])[None],
                             name="adamw_small", tr=1 << 14)
    out_g = dict(zip(small, _unpack(gs, shapes)))
    out_d = dict(zip(small, _unpack(ds_, shapes)))
    out_m = dict(zip(small, _unpack(ms, shapes)))
    out_v = dict(zip(small, _unpack(vs, shapes)))

    dw_in_t = jnp.concatenate([dWmainT[:qkv_w], dWfT[:FOX_HEADS], dWmainT[qkv_w:]], axis=0)
    part_now = rs_begin(grp_now, [dw_in_t.reshape(N_CHIPS, EIN // N_CHIPS, D), _shards_from_cols(dWglu),
                                  dWout_ev.reshape(N_CHIPS, D // N_CHIPS, D)], "l0")
    recv = dict(zip(grp_now, _scatter_to_chips(part_now, name="rs_l0_chips")))
    for tag, entries, started, parts in (("l1", grp_l1, rs_l1, part_l1), ("ffn0", grp_ffn0, rs_ffn0, part_ffn0)):
        send, rcv, thru, lands, _ = started
        lands = _chip_exchange_wait('scatter', send, rcv, thru, lands, [False] * len(parts), grad_x,
                                    name=f"rs_{tag}_wait")
        recv.update(zip(entries, _own_slot(lands, own_parts(parts))))
    halves = [_rowsum(recv[e], name=f"rs_sum4_{e[0]}{e[1]}") for e in big_e]
    others = _sibling_join_halves(halves, name="rs_join")
    pairs = dict(zip(big_e, zip(halves, others)))
    for n in big:
        res = _adamw(Wv[n], [pairs[(n, l)] for l in range(W[n].shape[0])], view(n, Mo[n]), view(n, Vo[n]),
                     name=f"adamw_{n}", by_cols=split_cols[(n, 0)])
        out_g[n], out_d[n], out_m[n], out_v[n] = (view(n, t) for t in res)

    return (loss, grad_x.reshape(1, S, D), *[out_g[n] for n in names], *[out_d[n] for n in names],
            *[out_m[n] for n in names], *[out_v[n] for n in names])
```

```python
import functools
import math

import numpy as np
import jax
import jax.numpy as jnp
from jax import lax
from jax.experimental import pallas as pl
from jax.experimental.pallas import tpu as pltpu

F32 = jnp.float32
BF16 = jnp.bfloat16
MESH = pl.DeviceIdType.MESH
ANY = pl.BlockSpec(memory_space=pl.ANY)

D_MODEL = 2048
FOX_HEADS = 8
FOX_HEAD_DIM = 128
FOX_WIDTH = 1024
SSM_WIDTH = 1024
SSM_GROUP = 16
SSM_GROUPS = 64
SSM_STATE = 64
SWA_HEADS = 32
SWA_KV_HEADS = 4
SWA_HEAD_DIM = 64
SWA_GROUPS = 8
SWA_WINDOW = 128
ROPE_DIM = 16
ROPE_THETA = 500000.0
LN_EPS = 1e-5
DEPTH = 2
ALPHA = (2.0 * DEPTH) ** 0.25
ADAM_LR = 0.001
ADAM_B1 = 0.9
ADAM_B2 = 0.999
ADAM_EPS = 1e-08
ADAM_WD = 0.01
ADAM_STEP = 10
N_CHIPS = 4

VMEM_LIMIT = 56 * 1024 * 1024
LANE = 128


def _call(body, after=(), **kw):
    if after:
        n = len(after)

        def shifted(*refs):
            return body(*refs[n:])

        call = _call(shifted, **dict(kw, in_specs=[ANY] * n + list(kw["in_specs"])))
        return lambda *args: call(*after, *args)
    return pl.pallas_call(body, **kw)


def _cparams(sem):
    return pltpu.CompilerParams(dimension_semantics=sem, vmem_limit_bytes=VMEM_LIMIT)


def _rup(n, m):
    return (n + m - 1) // m * m


def _pick(n, pref):
    if n <= pref:
        return n
    for step in (128, 16, 8):
        for t in range(pref - pref % step, 0, -step):
            if n % t == 0:
                return t
    return n


def _tile2d(rows, cols, pref_rows=256, budget=256 * 1024):
    tr = _pick(rows, pref_rows)
    if tr < 64:
        tr = rows
    if cols % LANE:
        return tr, cols
    return tr, _pick(cols, max(LANE, budget // tr // LANE * LANE))


def _mm(a, b, mode, *, name, tm=512, tn=1024, tk=2048, bmode=None, out_dtype=F32, after=()):
    a3 = a if a.ndim == 3 else a[None]
    b3 = b if b.ndim == 3 else b[None]
    if mode == 'tn':
        K, M = a3.shape[1:]
    else:
        M, K = a3.shape[1:]
    N = b3.shape[1] if mode == 'nt' else b3.shape[2]
    tm, tn, tk = _pick(M, tm), _pick(N, tn), _pick(K, tk)
    nb = max(a3.shape[0], b3.shape[0])
    nbo, nbr = (1, nb) if bmode == 'abr' else (nb, 1)
    nk = K // tk
    nred = nbr * nk
    a_b = bmode in ('ao', 'abr')
    b_b = bmode in ('bo', 'abr')
    o_b = bmode in ('bo', 'ao')

    def bsel(flag, bo, br):
        return (bo + br) if flag else 0

    if mode == 'tn':
        a_spec = pl.BlockSpec((None, tk, tm), lambda bo, i, j, br, k: (bsel(a_b, bo, br), k, i))
    else:
        a_spec = pl.BlockSpec((None, tm, tk), lambda bo, i, j, br, k: (bsel(a_b, bo, br), i, k))
    if mode == 'nt':
        b_spec = pl.BlockSpec((None, tn, tk), lambda bo, i, j, br, k: (bsel(b_b, bo, br), j, k))
    else:
        b_spec = pl.BlockSpec((None, tk, tn), lambda bo, i, j, br, k: (bsel(b_b, bo, br), k, j))
    o_spec = pl.BlockSpec((None, tm, tn), lambda bo, i, j, br, k: (bsel(o_b, bo, br), i, j))
    dn = {'nn': (((1,), (0,)), ((), ())), 'nt': (((1,), (1,)), ((), ())), 'tn': (((0,), (0,)), ((), ()))}[mode]

    def body(a_ref, b_ref, *rest):
        o_ref, scratch = rest[len(after)], rest[len(after) + 1:]
        r = lax.dot_general(a_ref[...].astype(BF16), b_ref[...].astype(BF16), dn, preferred_element_type=F32)
        if nred == 1:
            o_ref[...] = r.astype(out_dtype)
        else:
            acc = scratch[0]
            step = pl.program_id(3) * nk + pl.program_id(4)

            @pl.when(step == 0)
            def _():
                acc[...] = r

            @pl.when(step > 0)
            def _():
                acc[...] += r

            @pl.when(step == nred - 1)
            def _():
                o_ref[...] = acc[...].astype(out_dtype)

    out = _call(
        body, name=name,
        grid=(nbo, M // tm, N // tn, nbr, nk),
        in_specs=[a_spec, b_spec] + [ANY] * len(after), out_specs=o_spec,
        out_shape=jax.ShapeDtypeStruct((nbo if o_b else 1, M, N), out_dtype),
        scratch_shapes=[] if nred == 1 else [pltpu.VMEM((tm, tn), F32)],
        compiler_params=_cparams(("parallel", "parallel", "parallel", "arbitrary", "arbitrary")),
    )(a3, b3, *after)
    return out if o_b else out[0]


def _add_ln_fwd(x, r, g, b, *, name):
    S, D = x.shape
    tr = _pick(S, 256)

    def body(x_ref, r_ref, g_ref, b_ref, o_ref, xh_ref, rs_ref):
        z = ALPHA * x_ref[...] + r_ref[...]
        mu = jnp.mean(z, axis=-1, keepdims=True)
        zc = z - mu
        var = jnp.mean(zc * zc, axis=-1, keepdims=True)
        rstd = lax.rsqrt(var + LN_EPS)
        xh = zc * rstd
        xh_ref[...] = xh
        rs_ref[...] = rstd
        o_ref[...] = xh * g_ref[...] + b_ref[...]

    row = pl.BlockSpec((tr, D), lambda i: (i, 0))
    vec = pl.BlockSpec((1, D), lambda i: (0, 0))
    return _call(
        body, name=name, grid=(S // tr,),
        in_specs=[row, row, vec, vec],
        out_specs=[row, row, pl.BlockSpec((tr, 1), lambda i: (i, 0))],
        out_shape=[jax.ShapeDtypeStruct((S, D), F32), jax.ShapeDtypeStruct((S, D), F32),
                   jax.ShapeDtypeStruct((S, 1), F32)],
        compiler_params=_cparams(("parallel",)),
    )(x, r, g.reshape(1, D), b.reshape(1, D))


def _ln_bwd(da, db, xhat, rstd, g, *, name, after=()):
    S, D = xhat.shape
    tr = _pick(S, 256)
    two = db is not None

    def body(*refs):
        refs = refs[len(after):]
        if two:
            da_ref, db_ref, xh_ref, rs_ref, g_ref, dz_ref, dg_ref, dbt_ref = refs
            dy = ALPHA * da_ref[...] + db_ref[...]
        else:
            da_ref, xh_ref, rs_ref, g_ref, dz_ref, dg_ref, dbt_ref = refs
            dy = da_ref[...]
        xh = xh_ref[...]
        dxh = dy * g_ref[...]
        m1 = jnp.mean(dxh, axis=-1, keepdims=True)
        m2 = jnp.mean(dxh * xh, axis=-1, keepdims=True)
        dz_ref[...] = rs_ref[...] * (dxh - m1 - xh * m2)
        pg = jnp.sum(dy * xh, axis=0, keepdims=True)
        pb = jnp.sum(dy, axis=0, keepdims=True)

        @pl.when(pl.program_id(0) == 0)
        def _():
            dg_ref[...] = pg
            dbt_ref[...] = pb

        @pl.when(pl.program_id(0) > 0)
        def _():
            dg_ref[...] += pg
            dbt_ref[...] += pb

    row = pl.BlockSpec((tr, D), lambda i: (i, 0))
    vec = pl.BlockSpec((1, D), lambda i: (0, 0))
    ins = list(after) + [da] + ([db] if two else []) + [xhat, rstd, g.reshape(1, D)]
    in_specs = [ANY] * len(after) + [row] + ([row] if two else []) + [row, pl.BlockSpec((tr, 1), lambda i: (i, 0)), vec]
    return _call(
        body, name=name, grid=(S // tr,),
        in_specs=in_specs, out_specs=[row, vec, vec],
        out_shape=[jax.ShapeDtypeStruct((S, D), F32), jax.ShapeDtypeStruct((1, D), F32),
                   jax.ShapeDtypeStruct((1, D), F32)],
        compiler_params=_cparams(("arbitrary",)),
    )(*ins)


def _loss_grad(y, t, *, name):
    S, D = y.shape
    tr = _pick(S, 256)

    def body(y_ref, t_ref, dy_ref, l_ref):
        e = y_ref[...] - t_ref[...]
        dy_ref[...] = e * (1.0 / D)
        part = 0.5 * jnp.sum(jnp.sum(e * e, axis=-1, keepdims=True) * (1.0 / D), axis=0, keepdims=True)

        @pl.when(pl.program_id(0) == 0)
        def _():
            l_ref[...] = part

        @pl.when(pl.program_id(0) > 0)
        def _():
            l_ref[...] += part

    row = pl.BlockSpec((tr, D), lambda i: (i, 0))
    return _call(
        body, name=name, grid=(S // tr,), in_specs=[row, row],
        out_specs=[row, pl.BlockSpec((1, 1), lambda i: (0, 0))],
        out_shape=[jax.ShapeDtypeStruct((S, D), F32), jax.ShapeDtypeStruct((1, 1), F32)],
        compiler_params=_cparams(("arbitrary",)),
    )(y, t)


def _combine(terms, scales, *, name, out_dtype=F32):
    S, D = terms[0].shape
    tr = _pick(S, 256)
    n = len(terms)

    def body(*refs):
        acc = scales[0] * refs[0][...].astype(F32)
        for i in range(1, n):
            acc = acc + scales[i] * refs[i][...].astype(F32)
        refs[n][...] = acc.astype(out_dtype)

    row = pl.BlockSpec((tr, D), lambda i: (i, 0))
    return _call(
        body, name=name, grid=(S // tr,), in_specs=[row] * n, out_specs=row,
        out_shape=jax.ShapeDtypeStruct((S, D), out_dtype),
        compiler_params=_cparams(("parallel",)),
    )(*terms)


def _split3(x):
    h = x.astype(BF16)
    r = x - h.astype(F32)
    m = r.astype(BF16)
    l = (r - m.astype(F32)).astype(BF16)
    return h, m, l


def _tri_matmul(tri_bf, x):
    h, m, l = _split3(x)
    dn = (((1,), (0,)), ((), ()))
    return (lax.dot_general(tri_bf, l, dn, preferred_element_type=F32)
            + lax.dot_general(tri_bf, m, dn, preferred_element_type=F32)
            + lax.dot_general(tri_bf, h, dn, preferred_element_type=F32))


def _gate_fwd(fl, bf, *, name):
    S = fl.shape[0]
    tc = _pick(S, 256)
    nchunk = S // tc

    def body(fl_ref, bf_ref, c_ref, sg_ref):
        r = lax.broadcasted_iota(jnp.int32, (tc, tc), 0)
        cidx = lax.broadcasted_iota(jnp.int32, (tc, tc), 1)
        tri = (r >= cidx).astype(BF16)
        carry = jnp.zeros((1, LANE), F32)
        for ch in range(nchunk):
            x = fl_ref[pl.ds(ch * tc, tc), :] + bf_ref[...]
            lf = jnp.minimum(x, 0.0) - jnp.log(1.0 + jnp.exp(-jnp.abs(x)))
            sg_ref[pl.ds(ch * tc, tc), :] = jax.nn.sigmoid(-x)
            c_ref[pl.ds(ch * tc, tc), :] = _tri_matmul(tri, lf) + carry
            carry = carry + jnp.sum(lf, axis=0, keepdims=True)

    full = pl.BlockSpec((S, LANE), lambda: (0, 0))
    return _call(
        body, name=name, in_specs=[full, pl.BlockSpec((1, LANE), lambda: (0, 0))], out_specs=[full, full],
        out_shape=[jax.ShapeDtypeStruct((S, LANE), F32)] * 2,
        compiler_params=pltpu.CompilerParams(vmem_limit_bytes=VMEM_LIMIT),
    )(fl, bf)


def _gate_bwd(dc, sg, *, name):
    S = dc.shape[0]
    tc = _pick(S, 256)
    nchunk = S // tc

    def body(dc_ref, sg_ref, dfl_ref, db_ref):
        r = lax.broadcasted_iota(jnp.int32, (tc, tc), 0)
        cidx = lax.broadcasted_iota(jnp.int32, (tc, tc), 1)
        tri = (r <= cidx).astype(BF16)
        carry = jnp.zeros((1, LANE), F32)
        dbacc = jnp.zeros((1, LANE), F32)
        for ch in reversed(range(nchunk)):
            d = dc_ref[pl.ds(ch * tc, tc), :]
            dfl = (_tri_matmul(tri, d) + carry) * sg_ref[pl.ds(ch * tc, tc), :]
            dfl_ref[pl.ds(ch * tc, tc), :] = dfl
            dbacc = dbacc + jnp.sum(dfl, axis=0, keepdims=True)
            carry = carry + jnp.sum(d, axis=0, keepdims=True)
        db_ref[...] = dbacc

    full = pl.BlockSpec((S, LANE), lambda: (0, 0))
    return _call(
        body, name=name, in_specs=[full, full], out_specs=[full, pl.BlockSpec((1, LANE), lambda: (0, 0))],
        out_shape=[jax.ShapeDtypeStruct((S, LANE), F32), jax.ShapeDtypeStruct((1, LANE), F32)],
        compiler_params=pltpu.CompilerParams(vmem_limit_bytes=VMEM_LIMIT),
    )(dc, sg)


def _fox_scores(q_ref, k_ref, cc_ref, cr_ref, qi, tq, S):
    scale = 1.0 / math.sqrt(FOX_HEAD_DIM)
    s = lax.dot_general(q_ref[...].astype(BF16), k_ref[...].astype(BF16), (((1,), (1,)), ((), ())),
                        preferred_element_type=F32) * scale
    s = s + cc_ref[...] - cr_ref[...]
    row = lax.broadcasted_iota(jnp.int32, (tq, S), 0) + qi * tq
    col = lax.broadcasted_iota(jnp.int32, (tq, S), 1)
    return s, row >= col


def _fox_fwd(P, ccol, crow, *, name):
    S = P.shape[0]
    tq = _pick(S, 256)
    H = FOX_HEADS

    def body(q_ref, k_ref, v_ref, cc_ref, cr_ref, o_ref, l_ref):
        s, causal = _fox_scores(q_ref, k_ref, cc_ref, cr_ref, pl.program_id(1), tq, S)
        s = jnp.where(causal, s, -1e30)
        m = jnp.max(s, axis=-1, keepdims=True)
        e = jnp.exp(s - m)
        den = jnp.sum(e, axis=-1, keepdims=True)
        p = e / den
        o_ref[...] = jnp.dot(p.astype(BF16), v_ref[...].astype(BF16), preferred_element_type=F32)
        l_ref[...] = m + jnp.log(den)

    return _call(
        body, name=name, grid=(H, S // tq),
        in_specs=[pl.BlockSpec((tq, 128), lambda h, i: (i, h)),
                  pl.BlockSpec((S, 128), lambda h, i: (0, H + h)),
                  pl.BlockSpec((S, 128), lambda h, i: (0, 2 * H + h)),
                  pl.BlockSpec((None, tq, 1), lambda h, i: (h, i, 0)),
                  pl.BlockSpec((None, 1, S), lambda h, i: (h, 0, 0))],
        out_specs=[pl.BlockSpec((tq, 128), lambda h, i: (i, h)),
                   pl.BlockSpec((None, tq, 1), lambda h, i: (h, i, 0))],
        out_shape=[jax.ShapeDtypeStruct((S, FOX_WIDTH), F32), jax.ShapeDtypeStruct((H, S, 1), F32)],
        compiler_params=_cparams(("parallel", "parallel")),
    )(P, P, P, ccol, crow)


def _fox_bwd(P, ccol, crow, o, lse, dcat, *, name):
    S = P.shape[0]
    tq = _pick(S, 256)
    H = FOX_HEADS
    nq = S // tq
    scale = 1.0 / math.sqrt(FOX_HEAD_DIM)

    def body(q_ref, k_ref, v_ref, cc_ref, cr_ref, o_ref, l_ref, do_ref,
             dq_ref, dk_ref, dv_ref, dcc_ref, dcr_ref, dk_acc, dv_acc):
        qi = pl.program_id(1)
        s, causal = _fox_scores(q_ref, k_ref, cc_ref, cr_ref, qi, tq, S)
        p = jnp.where(causal, jnp.exp(s - l_ref[...]), 0.0)
        do = do_ref[...]
        do_bf = do.astype(BF16)
        dp = lax.dot_general(do_bf, v_ref[...].astype(BF16), (((1,), (1,)), ((), ())), preferred_element_type=F32)
        delta = jnp.sum(do * o_ref[...], axis=-1, keepdims=True)
        ds = p * (dp - delta)
        ds_bf = ds.astype(BF16)
        dq_ref[...] = (jnp.dot(ds_bf, k_ref[...].astype(BF16), preferred_element_type=F32) * scale).astype(BF16)
        dkp = lax.dot_general(ds_bf, q_ref[...].astype(BF16), (((0,), (0,)), ((), ())),
                              preferred_element_type=F32) * scale
        dvp = lax.dot_general(p.astype(BF16), do_bf, (((0,), (0,)), ((), ())), preferred_element_type=F32)
        dcc_ref[...] = jnp.sum(ds, axis=-1, keepdims=True)
        dcr = jnp.sum(ds, axis=0, keepdims=True)

        @pl.when(qi == 0)
        def _():
            dk_acc[...] = dkp
            dv_acc[...] = dvp
            dcr_ref[...] = dcr

        @pl.when(qi > 0)
        def _():
            dk_acc[...] += dkp
            dv_acc[...] += dvp
            dcr_ref[...] += dcr

        @pl.when(qi == nq - 1)
        def _():
            dk_ref[...] = dk_acc[...].astype(BF16)
            dv_ref[...] = dv_acc[...].astype(BF16)

    qblk = pl.BlockSpec((tq, 128), lambda h, i: (i, h))
    kvo = pl.BlockSpec((S, 128), lambda h, i: (0, h))
    col = pl.BlockSpec((None, tq, 1), lambda h, i: (h, i, 0))
    rowv = pl.BlockSpec((None, 1, S), lambda h, i: (h, 0, 0))
    return _call(
        body, name=name, grid=(H, nq),
        in_specs=[qblk,
                  pl.BlockSpec((S, 128), lambda h, i: (0, H + h)),
                  pl.BlockSpec((S, 128), lambda h, i: (0, 2 * H + h)),
                  col, rowv, qblk, col, qblk],
        out_specs=[qblk, kvo, kvo, col, rowv],
        out_shape=[jax.ShapeDtypeStruct((S, FOX_WIDTH), BF16)] * 3
        + [jax.ShapeDtypeStruct((H, S, 1), F32), jax.ShapeDtypeStruct((H, 1, S), F32)],
        scratch_shapes=[pltpu.VMEM((S, 128), F32), pltpu.VMEM((S, 128), F32)],
        compiler_params=_cparams(("parallel", "arbitrary")),
    )(P, P, P, ccol, crow, o, lse, dcat)


def _s5_disc_fwd(lr, li, ls, *, name, after=()):
    G, Pn = lr.shape

    def body(lr_ref, li_ref, ls_ref, ar_ref, ai_ref, gr_ref, gi_ref):
        lr_, li_ = lr_ref[...], li_ref[...]
        dt = jnp.exp(ls_ref[...])
        mag = jnp.exp(lr_ * dt)
        th = li_ * dt
        ar = mag * jnp.cos(th)
        ai = mag * jnp.sin(th)
        den = lr_ * lr_ + li_ * li_
        xr = ar - 1.0
        ar_ref[...] = ar
        ai_ref[...] = ai
        gr_ref[...] = (xr * lr_ + ai * li_) / den
        gi_ref[...] = (ai * lr_ - xr * li_) / den

    sq = pl.BlockSpec((G, Pn), lambda: (0, 0))
    return _call(
        body, after=after, name=name, in_specs=[sq, sq, pl.BlockSpec((G, 1), lambda: (0, 0))], out_specs=[sq] * 4,
        out_shape=[jax.ShapeDtypeStruct((G, Pn), F32)] * 4,
    )(lr, li, ls)


def _s5_disc_bwd(lr, li, ls, dar, dai, dgr, dgi, *, name):
    G, Pn = lr.shape

    def body(lr_ref, li_ref, ls_ref, dar_ref, dai_ref, dgr_ref, dgi_ref, dlr_ref, dli_ref, dls_ref):
        lr_, li_ = lr_ref[...], li_ref[...]
        dt = jnp.exp(ls_ref[...])
        mag = jnp.exp(lr_ * dt)
        th = li_ * dt
        ar = mag * jnp.cos(th)
        ai = mag * jnp.sin(th)
        den = lr_ * lr_ + li_ * li_
        xr = ar - 1.0
        xi = ai
        g_re = (xr * lr_ + xi * li_) / den
        g_im = (xi * lr_ - xr * li_) / den
        dgr_, dgi_ = dgr_ref[...], dgi_ref[...]
        dxr = (dgr_ * lr_ - dgi_ * li_) / den
        dxi = (dgr_ * li_ + dgi_ * lr_) / den
        dden = -(dgr_ * g_re + dgi_ * g_im) / den
        dlr = (dgr_ * xr + dgi_ * xi) / den + 2.0 * dden * lr_
        dli = (dgr_ * xi - dgi_ * xr) / den + 2.0 * dden * li_
        da_r = dar_ref[...] + dxr
        da_i = dai_ref[...] + dxi
        dmag_mag = da_r * ar + da_i * ai
        dth = da_i * ar - da_r * ai
        dlr_ref[...] = dlr + dmag_mag * dt
        dli_ref[...] = dli + dth * dt
        ddt = jnp.sum(dmag_mag * lr_ + dth * li_, axis=-1, keepdims=True)
        dls_ref[...] = ddt * dt

    sq = pl.BlockSpec((G, Pn), lambda: (0, 0))
    c1 = pl.BlockSpec((G, 1), lambda: (0, 0))
    return _call(
        body, name=name, in_specs=[sq, sq, c1, sq, sq, sq, sq], out_specs=[sq, sq, c1],
        out_shape=[jax.ShapeDtypeStruct((G, Pn), F32)] * 2 + [jax.ShapeDtypeStruct((G, 1), F32)],
    )(lr, li, ls, dar, dai, dgr, dgi)


def _s5_bb_fwd(gr, gi, br, bi, *, name):
    R, C = br.shape

    def body(gr_ref, gi_ref, br_ref, bi_ref, or_ref, oi_ref):
        g_r, g_i, b_r, b_i = gr_ref[...], gi_ref[...], br_ref[...], bi_ref[...]
        or_ref[...] = g_r * b_r - g_i * b_i
        oi_ref[...] = g_r * b_i + g_i * b_r

    w = pl.BlockSpec((R, C), lambda: (0, 0))
    c1 = pl.BlockSpec((R, 1), lambda: (0, 0))
    return _call(body, name=name, in_specs=[c1, c1, w, w], out_specs=[w, w],
                 out_shape=[jax.ShapeDtypeStruct((R, C), F32)] * 2)(gr, gi, br, bi)


def _s5_bb_bwd(gr, gi, br, bi, dbbr, dbbi, *, name):
    R, C = br.shape

    def body(gr_ref, gi_ref, br_ref, bi_ref, dr_ref, di_ref, dbr_ref, dbi_ref, dgr_ref, dgi_ref):
        g_r, g_i, b_r, b_i = gr_ref[...], gi_ref[...], br_ref[...], bi_ref[...]
        d_r, d_i = dr_ref[...], di_ref[...]
        dbr_ref[...] = g_r * d_r + g_i * d_i
        dbi_ref[...] = g_r * d_i - g_i * d_r
        dgr_ref[...] = jnp.sum(d_r * b_r + d_i * b_i, axis=-1, keepdims=True)
        dgi_ref[...] = jnp.sum(d_i * b_r - d_r * b_i, axis=-1, keepdims=True)

    w = pl.BlockSpec((R, C), lambda: (0, 0))
    c1 = pl.BlockSpec((R, 1), lambda: (0, 0))
    return _call(body, name=name, in_specs=[c1, c1, w, w, w, w], out_specs=[w, w, c1, c1],
                 out_shape=[jax.ShapeDtypeStruct((R, C), F32)] * 2 + [jax.ShapeDtypeStruct((R, 1), F32)] * 2,
                 )(gr, gi, br, bi, dbbr, dbbi)


_DIAG_TILE = 8


def _diag_mask(gr, gc):
    rows, cols = _DIAG_TILE * gr, _DIAG_TILE * gc
    r = lax.broadcasted_iota(jnp.int32, (rows, cols), 0) >> (gr.bit_length() - 1)
    c = lax.broadcasted_iota(jnp.int32, (rows, cols), 1) >> (gc.bit_length() - 1)
    return r == c


def _diag_expand(t2, gr, gc, *, name, after=()):
    _, R, _ = t2.shape
    G = R // gr
    nt = G // _DIAG_TILE
    rows, cols = _DIAG_TILE * gr, _DIAG_TILE * gc

    def body(t_ref, o_ref):
        @pl.when(pl.program_id(1) == pl.program_id(2))
        def _():
            src = lax.broadcasted_iota(jnp.int32, (gc, cols), 0)
            dst = lax.broadcasted_iota(jnp.int32, (gc, cols), 1) & (gc - 1)
            spread = (src == dst).astype(BF16)
            y = jnp.dot(t_ref[...].astype(BF16), spread, preferred_element_type=F32)
            o_ref[...] = jnp.where(_diag_mask(gr, gc), y, 0.0).astype(BF16)

        @pl.when(pl.program_id(1) != pl.program_id(2))
        def _():
            o_ref[...] = jnp.zeros_like(o_ref)

    return _call(
        body, after=after, name=name, grid=(2, nt, nt),
        in_specs=[pl.BlockSpec((None, rows, gc), lambda p, i, j: (p, i, 0))],
        out_specs=pl.BlockSpec((None, rows, cols), lambda p, i, j: (p, i, j)),
        out_shape=jax.ShapeDtypeStruct((2, R, G * gc), BF16),
        compiler_params=_cparams(("parallel",) * 3),
    )(t2)


def _diag_extract(xd, gr, gc, *, name):
    _, R, _ = xd.shape
    nt = R // gr // _DIAG_TILE
    rows, cols = _DIAG_TILE * gr, _DIAG_TILE * gc

    def body(x_ref, o_ref):
        src = lax.broadcasted_iota(jnp.int32, (cols, gc), 0) & (gc - 1)
        dst = lax.broadcasted_iota(jnp.int32, (cols, gc), 1)
        fold = (src == dst).astype(BF16)
        parts = _split3(jnp.where(_diag_mask(gr, gc), x_ref[...], 0.0))
        acc = jnp.dot(parts[2], fold, preferred_element_type=F32)
        acc = acc + jnp.dot(parts[1], fold, preferred_element_type=F32)
        o_ref[...] = acc + jnp.dot(parts[0], fold, preferred_element_type=F32)

    return _call(
        body, name=name, grid=(2, nt),
        in_specs=[pl.BlockSpec((None, rows, cols), lambda p, i: (p, i, i))],
        out_specs=pl.BlockSpec((None, rows, gc), lambda p, i: (p, i, 0)),
        out_shape=jax.ShapeDtypeStruct((2, R, gc), F32),
        compiler_params=_cparams(("parallel",) * 2),
    )(xd)


SCAN_BLOCK = 8


def _cpowers(ar, ai, sign):
    ai = sign * ai
    out = [(ar, ai)]
    for _ in range(SCAN_BLOCK - 1):
        pr, pi = out[-1]
        out.append((pr * ar - pi * ai, pr * ai + pi * ar))
    return out


def _row_table(pw, row, index_of_row):
    tr_ = jnp.broadcast_to(pw[index_of_row(0)][0], row.shape)
    ti_ = jnp.broadcast_to(pw[index_of_row(0)][1], row.shape)
    for r in range(1, SCAN_BLOCK):
        pr, pi = pw[index_of_row(r)]
        tr_ = jnp.where(row == r, pr, tr_)
        ti_ = jnp.where(row == r, pi, ti_)
    return tr_, ti_


def _s5_scan_fwd(bu, a, *, name):
    _, S, N = bu.shape
    tc = 512
    nt = N // tc

    def body(a_ref, b_ref, h_ref):
        pw = _cpowers(a_ref[0], a_ref[1], 1.0)
        row = lax.broadcasted_iota(jnp.int32, (SCAN_BLOCK, tc), 0)
        lead_r, lead_i = _row_table(pw, row, lambda r: r)

        def step(k, carry):
            cr, ci = carry
            rows = pl.ds(pl.multiple_of(k * SCAN_BLOCK, SCAN_BLOCK), SCAN_BLOCK)
            xr, xi = b_ref[0, rows, :], b_ref[1, rows, :]
            for sh in (1, 2, 4):
                keep = row >= sh
                sr = jnp.where(keep, pltpu.roll(xr, sh, 0), 0.0)
                si = jnp.where(keep, pltpu.roll(xi, sh, 0), 0.0)
                kr, ki = pw[sh - 1]
                xr, xi = xr + kr * sr - ki * si, xi + kr * si + ki * sr
            h_ref[0, rows, :] = xr + lead_r * cr - lead_i * ci
            h_ref[1, rows, :] = xi + lead_r * ci + lead_i * cr
            last = row == SCAN_BLOCK - 1
            tr_ = jnp.sum(jnp.where(last, xr, 0.0), axis=0, keepdims=True)
            ti_ = jnp.sum(jnp.where(last, xi, 0.0), axis=0, keepdims=True)
            a8r, a8i = pw[SCAN_BLOCK - 1]
            return a8r * cr - a8i * ci + tr_, a8r * ci + a8i * cr + ti_

        z = jnp.zeros((1, tc), F32)
        lax.fori_loop(0, S // SCAN_BLOCK, step, (z, z), unroll=2)

    vec = pl.BlockSpec((2, 1, tc), lambda j: (0, 0, j))
    mat = pl.BlockSpec((2, S, tc), lambda j: (0, 0, j))
    return _call(
        body, name=name, grid=(nt,), in_specs=[vec, mat], out_specs=mat,
        out_shape=jax.ShapeDtypeStruct((2, S, N), F32),
        compiler_params=_cparams(("parallel",)),
    )(a, bu)


def _s5_scan_bwd(g, h, a, *, name):
    _, S, N = g.shape
    tc = 256
    nt = N // tc

    def body(a_ref, g_ref, h_ref, l_ref, da_ref):
        pw = _cpowers(a_ref[0], a_ref[1], -1.0)
        row = lax.broadcasted_iota(jnp.int32, (SCAN_BLOCK, tc), 0)
        tail_r, tail_i = _row_table(pw, row, lambda r: SCAN_BLOCK - 1 - r)
        nb = S // SCAN_BLOCK

        def step(i, carry):
            k = nb - 1 - i
            cr, ci, dar, dai = carry
            rows = pl.ds(pl.multiple_of(k * SCAN_BLOCK, SCAN_BLOCK), SCAN_BLOCK)
            xr, xi = g_ref[0, rows, :], g_ref[1, rows, :]
            for sh in (1, 2, 4):
                keep = row < SCAN_BLOCK - sh
                sr = jnp.where(keep, pltpu.roll(xr, SCAN_BLOCK - sh, 0), 0.0)
                si = jnp.where(keep, pltpu.roll(xi, SCAN_BLOCK - sh, 0), 0.0)
                kr, ki = pw[sh - 1]
                xr, xi = xr + kr * sr - ki * si, xi + kr * si + ki * sr
            lr = xr + tail_r * cr - tail_i * ci
            li = xi + tail_r * ci + tail_i * cr
            l_ref[0, rows, :] = lr
            l_ref[1, rows, :] = li
            prev = pl.ds(pl.multiple_of(jnp.maximum(k - 1, 0) * SCAN_BLOCK, SCAN_BLOCK), SCAN_BLOCK)
            has_prev = jnp.where(k > 0, 1.0, 0.0).astype(F32)
            first = row == 0
            hpr = jnp.where(first, pltpu.roll(h_ref[0, prev, :], 1, 0) * has_prev, pltpu.roll(h_ref[0, rows, :], 1, 0))
            hpi = jnp.where(first, pltpu.roll(h_ref[1, prev, :], 1, 0) * has_prev, pltpu.roll(h_ref[1, rows, :], 1, 0))
            tr_ = jnp.sum(jnp.where(first, xr, 0.0), axis=0, keepdims=True)
            ti_ = jnp.sum(jnp.where(first, xi, 0.0), axis=0, keepdims=True)
            a8r, a8i = pw[SCAN_BLOCK - 1]
            return (a8r * cr - a8i * ci + tr_, a8r * ci + a8i * cr + ti_,
                    dar + lr * hpr + li * hpi, dai + li * hpr - lr * hpi)

        z = jnp.zeros((1, tc), F32)
        z8 = jnp.zeros((SCAN_BLOCK, tc), F32)
        _, _, dar, dai = lax.fori_loop(0, nb, step, (z, z, z8, z8), unroll=2)
        da_ref[0] = jnp.sum(dar, axis=0, keepdims=True)
        da_ref[1] = jnp.sum(dai, axis=0, keepdims=True)

    vec = pl.BlockSpec((2, 1, tc), lambda j: (0, 0, j))
    mat = pl.BlockSpec((2, S, tc), lambda j: (0, 0, j))
    return _call(
        body, name=name, grid=(nt,), in_specs=[vec, mat, mat], out_specs=[mat, vec],
        out_shape=[jax.ShapeDtypeStruct((2, S, N), F32), jax.ShapeDtypeStruct((2, 1, N), F32)],
        compiler_params=_cparams(("parallel",)),
    )(a, g, h)


_GELU_C = math.sqrt(2.0 / math.pi)


def _s5_out_fwd(yc, P, dskip, *, name):
    S, W = yc.shape
    tr = _pick(S, 256)
    ub = 3 * FOX_WIDTH // W

    def body(yc_ref, u_ref, d_ref, y_ref, yg_ref):
        y = yc_ref[...] + d_ref[...] * u_ref[...]
        y_ref[...] = y
        t = jnp.tanh(_GELU_C * (y + 0.044715 * y * y * y))
        yg_ref[...] = (0.5 * y * (1.0 + t)).astype(BF16)

    row = pl.BlockSpec((tr, W), lambda i: (i, 0))
    return _call(
        body, name=name, grid=(S // tr,),
        in_specs=[row, pl.BlockSpec((tr, W), lambda i: (i, ub)), pl.BlockSpec((1, W), lambda i: (0, 0))],
        out_specs=[row, row],
        out_shape=[jax.ShapeDtypeStruct((S, W), F32), jax.ShapeDtypeStruct((S, W), BF16)],
        compiler_params=_cparams(("parallel",)),
    )(yc, P, dskip)


def _s5_out_bwd(dyg, y, P, dskip, *, name):
    S, W = y.shape
    tr = _pick(S, 256)
    ub = 3 * FOX_WIDTH // W

    def body(dyg_ref, y_ref, u_ref, d_ref, dy_ref, du_ref, dd_ref):
        y_ = y_ref[...]
        inner = _GELU_C * (y_ + 0.044715 * y_ * y_ * y_)
        t = jnp.tanh(inner)
        dgelu = 0.5 * (1.0 + t) + 0.5 * y_ * (1.0 - t * t) * _GELU_C * (1.0 + 3.0 * 0.044715 * y_ * y_)
        dy = dyg_ref[...] * dgelu
        dy_ref[...] = dy.astype(BF16)
        du_ref[...] = d_ref[...] * dy
        part = jnp.sum(dy * u_ref[...], axis=0, keepdims=True)

        @pl.when(pl.program_id(0) == 0)
        def _():
            dd_ref[...] = part

        @pl.when(pl.program_id(0) > 0)
        def _():
            dd_ref[...] += part

    row = pl.BlockSpec((tr, W), lambda i: (i, 0))
    vec = pl.BlockSpec((1, W), lambda i: (0, 0))
    return _call(
        body, name=name, grid=(S // tr,),
        in_specs=[row, row, pl.BlockSpec((tr, W), lambda i: (i, ub)), vec],
        out_specs=[row, row, vec],
        out_shape=[jax.ShapeDtypeStruct((S, W), BF16), jax.ShapeDtypeStruct((S, W), F32),
                   jax.ShapeDtypeStruct((1, W), F32)],
        compiler_params=_cparams(("arbitrary",)),
    )(dyg, y, P, dskip)


def _glu_fwd(z, *, name):
    S, W2 = z.shape
    W = W2 // 2
    tr = _pick(S, 256)

    def body(z1_ref, z2_ref, o_ref):
        o_ref[...] = (z1_ref[...] * jax.nn.sigmoid(z2_ref[...])).astype(BF16)

    return _call(
        body, name=name, grid=(S // tr,),
        in_specs=[pl.BlockSpec((tr, W), lambda i: (i, 0)), pl.BlockSpec((tr, W), lambda i: (i, 1))],
        out_specs=pl.BlockSpec((tr, W), lambda i: (i, 0)),
        out_shape=jax.ShapeDtypeStruct((S, W), BF16),
        compiler_params=_cparams(("parallel",)),
    )(z, z)


def _glu_bwd(z, dcat, *, name):
    S, W2 = z.shape
    W = W2 // 2
    tr = _pick(S, 256)

    def body(z1_ref, z2_ref, d_ref, dz1_ref, dz2_ref):
        sg = jax.nn.sigmoid(z2_ref[...])
        d = d_ref[...]
        dz1_ref[...] = (d * sg).astype(BF16)
        dz2_ref[...] = (d * z1_ref[...] * sg * (1.0 - sg)).astype(BF16)

    lo = pl.BlockSpec((tr, W), lambda i: (i, 0))
    hi = pl.BlockSpec((tr, W), lambda i: (i, 1))
    dz1, dz2 = _call(
        body, name=name, grid=(S // tr,), in_specs=[lo, hi, hi], out_specs=[lo, lo],
        out_shape=[jax.ShapeDtypeStruct((S, W), BF16)] * 2,
        compiler_params=_cparams(("parallel",)),
    )(z, z, dcat)
    return jnp.concatenate([dz1, dz2], axis=1)


def _act_fwd(h, cw, cb, *, name):
    _, S, FP = h.shape
    tr = _pick(S, 256)
    hb = tr // 8

    def conv(x_ref, halo_ref, w_ref, b_ref, ext, first):
        ext[pl.ds(0, 8), :] = jnp.where(first, 0.0, halo_ref[...])
        ext[pl.ds(8, tr), :] = x_ref[...]
        return (b_ref[...] + w_ref[pl.ds(2, 1), :] * ext[pl.ds(8, tr), :]
                + w_ref[pl.ds(1, 1), :] * ext[pl.ds(7, tr), :] + w_ref[pl.ds(0, 1), :] * ext[pl.ds(6, tr), :])

    def body(g_ref, gh_ref, v_ref, vh_ref, wg_ref, wv_ref, bg_ref, bv_ref, a_ref, ext):
        first = pl.program_id(1) == 0
        cg = conv(g_ref, gh_ref, wg_ref, bg_ref, ext, first)
        cv = conv(v_ref, vh_ref, wv_ref, bv_ref, ext, first)
        a_ref[...] = (cg * jax.nn.sigmoid(cg) * cv).astype(BF16)

    def main(off):
        return pl.BlockSpec((None, tr, FP), lambda j, i: (j + off, i, 0))

    def halo(off):
        return pl.BlockSpec((None, 8, FP), lambda j, i: (j + off, jnp.maximum(i * hb - 1, 0), 0))

    def wspec(off):
        return pl.BlockSpec((None, 3, FP), lambda j, i: (j + off, 0, 0))

    def bspec(off):
        return pl.BlockSpec((None, 1, FP), lambda j, i: (j + off, 0, 0))

    cb3 = cb.reshape(4, 1, FP)
    return _call(
        body, name=name, grid=(2, S // tr),
        in_specs=[main(0), halo(0), main(2), halo(2), wspec(0), wspec(2), bspec(0), bspec(2)],
        out_specs=pl.BlockSpec((None, tr, FP), lambda j, i: (j, i, 0)),
        out_shape=jax.ShapeDtypeStruct((2, S, FP), BF16),
        scratch_shapes=[pltpu.VMEM((tr + 8, FP), F32)],
        compiler_params=_cparams(("parallel", "arbitrary")),
    )(h, h, h, h, cw, cw, cb3, cb3)


def _act_bwd(h, da, cw, cb, *, name):
    _, S, FP = h.shape
    tr = _pick(S, 128)
    hb = tr // 8
    nr = S // tr

    def fill(ext, x_ref, prev_ref, next_ref, first, last):
        ext[pl.ds(0, 8), :] = jnp.where(first, 0.0, prev_ref[...])
        ext[pl.ds(8, tr), :] = x_ref[...]
        ext[pl.ds(8 + tr, 8), :] = jnp.where(last, 0.0, next_ref[...])

    def convo(ext, w, b, base, n):
        return (b + w[2] * ext[pl.ds(base, n), :] + w[1] * ext[pl.ds(base - 1, n), :]
                + w[0] * ext[pl.ds(base - 2, n), :])

    def body(g_ref, gp_ref, gn_ref, v_ref, vp_ref, vn_ref, da_ref, dan_ref,
             wg_ref, wv_ref, bg_ref, bv_ref,
             dg_ref, dv_ref, dwg_ref, dwv_ref, dbg_ref, dbv_ref, eg, ev, ed, dcg, dcv):
        i = pl.program_id(1)
        first = i == 0
        last = i == nr - 1
        fill(eg, g_ref, gp_ref, gn_ref, first, last)
        fill(ev, v_ref, vp_ref, vn_ref, first, last)
        ed[pl.ds(0, tr), :] = da_ref[...]
        ed[pl.ds(tr, 8), :] = jnp.where(last, 0.0, dan_ref[...])
        wg = [wg_ref[pl.ds(k, 1), :] for k in range(3)]
        wv = [wv_ref[pl.ds(k, 1), :] for k in range(3)]
        n = tr + 8
        cg = convo(eg, wg, bg_ref[...], 8, n)
        cv = convo(ev, wv, bv_ref[...], 8, n)
        sg = jax.nn.sigmoid(cg)
        d = ed[...]
        dcg[...] = d * cv * sg * (1.0 + cg * (1.0 - sg))
        dcv[...] = d * cg * sg
        for (dc, w, e, dh_ref, dw_ref, db_ref) in ((dcg, wg, eg, dg_ref, dwg_ref, dbg_ref),
                                                  (dcv, wv, ev, dv_ref, dwv_ref, dbv_ref)):
            d0 = dc[pl.ds(0, tr), :]
            dh_ref[...] = (w[2] * d0 + w[1] * dc[pl.ds(1, tr), :] + w[0] * dc[pl.ds(2, tr), :]).astype(BF16)
            pw = [jnp.sum(d0 * e[pl.ds(6 + k, tr), :], axis=0, keepdims=True) for k in range(3)]
            pb = jnp.sum(d0, axis=0, keepdims=True)

            @pl.when(first)
            def _():
                for k in range(3):
                    dw_ref[pl.ds(k, 1), :] = pw[k]
                db_ref[...] = pb

            @pl.when(jnp.logical_not(first))
            def _():
                for k in range(3):
                    dw_ref[pl.ds(k, 1), :] += pw[k]
                db_ref[...] += pb

    def main(off):
        return pl.BlockSpec((None, tr, FP), lambda j, i: (j + off, i, 0))

    def prev(off):
        return pl.BlockSpec((None, 8, FP), lambda j, i: (j + off, jnp.maximum(i * hb - 1, 0), 0))

    def nxt(off):
        return pl.BlockSpec((None, 8, FP), lambda j, i: (j + off, jnp.minimum((i + 1) * hb, S // 8 - 1), 0))

    def wspec(off):
        return pl.BlockSpec((None, 3, FP), lambda j, i: (j + off, 0, 0))

    def bspec(off):
        return pl.BlockSpec((None, 1, FP), lambda j, i: (j + off, 0, 0))

    cb3 = cb.reshape(4, 1, FP)
    dg, dv, dwg, dwv, dbg, dbv = _call(
        body, name=name, grid=(2, nr),
        in_specs=[main(0), prev(0), nxt(0), main(2), prev(2), nxt(2), main(0), nxt(0),
                  wspec(0), wspec(2), bspec(0), bspec(2)],
        out_specs=[main(0), main(0), wspec(0), wspec(0), bspec(0), bspec(0)],
        out_shape=[jax.ShapeDtypeStruct((2, S, FP), BF16)] * 2
        + [jax.ShapeDtypeStruct((2, 3, FP), F32)] * 2 + [jax.ShapeDtypeStruct((2, 1, FP), F32)] * 2,
        scratch_shapes=[pltpu.VMEM((tr + 16, FP), F32), pltpu.VMEM((tr + 16, FP), F32),
                        pltpu.VMEM((tr + 8, FP), F32), pltpu.VMEM((tr + 8, FP), F32),
                        pltpu.VMEM((tr + 8, FP), F32)],
        compiler_params=_cparams(("parallel", "arbitrary")),
    )(h, h, h, h, h, h, da, da, cw, cw, cb3, cb3)
    return (jnp.concatenate([dg, dv], axis=0), jnp.concatenate([dwg, dwv], axis=0),
            jnp.concatenate([dbg, dbv], axis=0))


def _rope_tables(posf, *, name, after=()):
    S = posf.shape[0]
    half = ROPE_DIM // 2
    d = np.arange(LANE) % SWA_HEAD_DIM
    invf = np.where(d < ROPE_DIM, ROPE_THETA ** (-(d % half).astype(np.float64) / half), 0.0).astype(np.float32)
    m_rot = (d < ROPE_DIM).astype(np.float32)
    m_a = (d < half).astype(np.float32)
    m_b = ((d >= half) & (d < ROPE_DIM)).astype(np.float32)
    consts = jnp.asarray(np.stack([invf, m_rot, m_a, m_b] + [np.zeros(LANE, np.float32)] * 4))

    def body(p_ref, k_ref, c_ref, sa_ref, sb_ref):
        k = k_ref[...]
        ang = p_ref[...] * k[0:1]
        co, si = jnp.cos(ang), jnp.sin(ang)
        c_ref[...] = k[1:2] * co + (1.0 - k[1:2])
        sa_ref[...] = -k[2:3] * si
        sb_ref[...] = k[3:4] * si

    full = pl.BlockSpec((S, LANE), lambda: (0, 0))
    return _call(
        body, after=after, name=name,
        in_specs=[pl.BlockSpec((S, 1), lambda: (0, 0)), pl.BlockSpec((8, LANE), lambda: (0, 0))],
        out_specs=[full] * 3, out_shape=[jax.ShapeDtypeStruct((S, LANE), F32)] * 3,
    )(posf, consts)


def _rope_apply(x, tabs, *, col0, width, inverse, name, out_dtype):
    S = x.shape[0]
    tr = _pick(S, 256)
    rep = width // LANE
    cb = col0 // width

    def body(x_ref, c_ref, sa_ref, sb_ref, o_ref):
        xv = x_ref[...].astype(F32)
        c = jnp.tile(c_ref[...], (1, rep))
        sa = jnp.tile(sa_ref[...], (1, rep))
        sb = jnp.tile(sb_ref[...], (1, rep))
        if not inverse:
            out = xv * c + pltpu.roll(xv, width - 8, 1) * sa + pltpu.roll(xv, 8, 1) * sb
        else:
            out = xv * c + pltpu.roll(xv * sa, 8, 1) + pltpu.roll(xv * sb, width - 8, 1)
        o_ref[...] = out.astype(out_dtype)

    tab = pl.BlockSpec((tr, LANE), lambda i: (i, 0))
    return _call(
        body, name=name, grid=(S // tr,),
        in_specs=[pl.BlockSpec((tr, width), lambda i: (i, cb)), tab, tab, tab],
        out_specs=pl.BlockSpec((tr, width), lambda i: (i, 0)),
        out_shape=jax.ShapeDtypeStruct((S, width), out_dtype),
        compiler_params=_cparams(("parallel",)),
    )(x, *tabs)


def _swa_mask(n):
    rows = SWA_GROUPS * SWA_WINDOW
    qi = lax.broadcasted_iota(jnp.int32, (rows, 2 * SWA_WINDOW), 0) & (SWA_WINDOW - 1)
    kj = lax.broadcasted_iota(jnp.int32, (rows, 2 * SWA_WINDOW), 1)
    rel = SWA_WINDOW + qi - kj
    return (rel >= 0) & (rel < SWA_WINDOW) & ((n > 0) | (kj >= SWA_WINDOW))


def _swa_fwd(qT, kT, vT, sink_rows, *, name):
    S = qT.shape[1]
    W, G, Dh = SWA_WINDOW, SWA_GROUPS, SWA_HEAD_DIM
    nb = S // W
    scale = 1.0 / math.sqrt(Dh)

    def body(q_ref, kp_ref, kc_ref, vp_ref, vc_ref, s_ref, o_ref, l_ref):
        n = pl.program_id(1)
        q = q_ref[...].reshape(G * W, Dh)
        kk = jnp.concatenate([kp_ref[...], kc_ref[...]], axis=0)
        vv = jnp.concatenate([vp_ref[...], vc_ref[...]], axis=0)
        s = lax.dot_general(q, kk, (((1,), (1,)), ((), ())), preferred_element_type=F32) * scale
        s = jnp.where(_swa_mask(n), s, -1e30)
        sink = s_ref[...]
        m = jnp.maximum(jnp.max(s, axis=-1, keepdims=True), sink)
        e = jnp.exp(s - m)
        den = jnp.sum(e, axis=-1, keepdims=True) + jnp.exp(sink - m)
        p = e / den
        o_ref[...] = jnp.dot(p.astype(BF16), vv, preferred_element_type=F32).reshape(G, W, Dh)
        l_ref[...] = (m + jnp.log(den)).reshape(G, W, 1)

    qs = pl.BlockSpec((G, W, Dh), lambda g, n: (g, n, 0))
    prev = pl.BlockSpec((None, W, Dh), lambda g, n: (g, jnp.maximum(n - 1, 0), 0))
    cur = pl.BlockSpec((None, W, Dh), lambda g, n: (g, n, 0))
    return _call(
        body, name=name, grid=(SWA_KV_HEADS, nb),
        in_specs=[qs, prev, cur, prev, cur, pl.BlockSpec((None, G * W, 1), lambda g, n: (g, 0, 0))],
        out_specs=[qs, pl.BlockSpec((G, W, 1), lambda g, n: (g, n, 0))],
        out_shape=[jax.ShapeDtypeStruct((SWA_HEADS, S, Dh), F32), jax.ShapeDtypeStruct((SWA_HEADS, S, 1), F32)],
        compiler_params=_cparams(("parallel", "parallel")),
    )(qT, kT, kT, vT, vT, sink_rows)


def _swa_bwd(qT, kT, vT, sink_rows, oT, L, doT, *, name):
    S = qT.shape[1]
    W, G, Dh = SWA_WINDOW, SWA_GROUPS, SWA_HEAD_DIM
    nb = S // W
    scale = 1.0 / math.sqrt(Dh)

    def body(q_ref, kp_ref, kc_ref, vp_ref, vc_ref, s_ref, o_ref, l_ref, do_ref,
             dq_ref, dk_ref, dv_ref, ds_ref):
        n = pl.program_id(1)
        q = q_ref[...].reshape(G * W, Dh)
        kk = jnp.concatenate([kp_ref[...], kc_ref[...]], axis=0)
        vv = jnp.concatenate([vp_ref[...], vc_ref[...]], axis=0)
        s = lax.dot_general(q, kk, (((1,), (1,)), ((), ())), preferred_element_type=F32) * scale
        lrow = l_ref[...].reshape(G * W, 1)
        p = jnp.where(_swa_mask(n), jnp.exp(s - lrow), 0.0)
        do = do_ref[...].reshape(G * W, Dh)
        do_bf = do.astype(BF16)
        dp = lax.dot_general(do_bf, vv, (((1,), (1,)), ((), ())), preferred_element_type=F32)
        delta = jnp.sum(do * o_ref[...].reshape(G * W, Dh), axis=-1, keepdims=True)
        dsc = p * (dp - delta)
        ds_bf = dsc.astype(BF16)
        dq_ref[...] = (jnp.dot(ds_bf, kk, preferred_element_type=F32) * scale).astype(BF16).reshape(G, W, Dh)
        dkk = lax.dot_general(ds_bf, q, (((0,), (0,)), ((), ())), preferred_element_type=F32) * scale
        dvv = lax.dot_general(p.astype(BF16), do_bf, (((0,), (0,)), ((), ())), preferred_element_type=F32)
        dsk = -jnp.exp(s_ref[...] - lrow) * delta
        dsk = jnp.broadcast_to(jnp.sum(dsk.reshape(G, W, 1), axis=1), (G, LANE))

        @pl.when(n == 0)
        def _():
            dk_ref[...] = jnp.zeros_like(dk_ref)
            dv_ref[...] = jnp.zeros_like(dv_ref)
            ds_ref[...] = jnp.zeros_like(ds_ref)

        rows = pl.ds(pl.multiple_of(n * W, W), 2 * W)
        dk_ref[rows, :] += dkk
        dv_ref[rows, :] += dvv
        ds_ref[...] += dsk

    qs = pl.BlockSpec((G, W, Dh), lambda g, n: (g, n, 0))
    prev = pl.BlockSpec((None, W, Dh), lambda g, n: (g, jnp.maximum(n - 1, 0), 0))
    cur = pl.BlockSpec((None, W, Dh), lambda g, n: (g, n, 0))
    lsp = pl.BlockSpec((G, W, 1), lambda g, n: (g, n, 0))
    kvo = pl.BlockSpec((None, S + W, Dh), lambda g, n: (g, 0, 0))
    return _call(
        body, name=name, grid=(SWA_KV_HEADS, nb),
        in_specs=[qs, prev, cur, prev, cur, pl.BlockSpec((None, G * W, 1), lambda g, n: (g, 0, 0)), qs, lsp, qs],
        out_specs=[qs, kvo, kvo, pl.BlockSpec((None, G, LANE), lambda g, n: (g, 0, 0))],
        out_shape=[jax.ShapeDtypeStruct((SWA_HEADS, S, Dh), BF16),
                   jax.ShapeDtypeStruct((SWA_KV_HEADS, S + W, Dh), F32),
                   jax.ShapeDtypeStruct((SWA_KV_HEADS, S + W, Dh), F32),
                   jax.ShapeDtypeStruct((SWA_KV_HEADS, G, LANE), F32)],
        compiler_params=_cparams(("parallel", "arbitrary")),
    )(qT, kT, kT, vT, vT, sink_rows, oT, L, doT)


def _adamw(w, g, m, v, *, name, tr=128, by_cols=False):
    L, R, C = w.shape
    split = isinstance(g, (list, tuple))
    HR, HC = _half_shape(R, C, by_cols) if split else (R, C)
    tr, tc = _tile2d(HR, HC, tr)
    nr, nc = HR // tr, HC // tc
    c1 = 1.0 / (1.0 - ADAM_B1 ** ADAM_STEP)
    c2 = 1.0 / (1.0 - ADAM_B2 ** ADAM_STEP)
    ng = 2 * L if split else 1

    def body(*refs):
        w_ref, g_refs, (m_ref, v_ref, go_ref, d_ref, mo_ref, vo_ref) = refs[0], refs[1:1 + ng], refs[1 + ng:]
        if split:
            mine = pl.program_id(1) == lax.axis_index("c")
            g_ = jnp.where(mine, g_refs[0][...], g_refs[1][...])
            for l in range(1, L):
                g_ = jnp.where(pl.program_id(0) == l,
                               jnp.where(mine, g_refs[2 * l][...], g_refs[2 * l + 1][...]), g_)
        else:
            g_ = g_refs[0][...]
        mn = ADAM_B1 * m_ref[...] + (1.0 - ADAM_B1) * g_
        vn = ADAM_B2 * v_ref[...] + (1.0 - ADAM_B2) * (g_ * g_)
        go_ref[...] = g_
        mo_ref[...] = mn
        vo_ref[...] = vn
        d_ref[...] = -ADAM_LR * ((mn * c1) / (jnp.sqrt(vn * c2) + ADAM_EPS) + ADAM_WD * w_ref[...])

    def whole(l, hf, i, j):
        return (l, i, hf * nc + j) if by_cols else (l, hf * nr + i, j)

    row = pl.BlockSpec((None, tr, tc), whole)
    half = pl.BlockSpec((tr, tc), lambda l, hf, i, j: (i, j))
    gs = [h for pair in g for h in pair] if split else [g]
    return _call(
        body, name=name, grid=(L, 2 if split else 1, nr, nc),
        in_specs=[row] + [half if split else row] * ng + [row, row],
        out_specs=[row] * 4, out_shape=[jax.ShapeDtypeStruct((L, R, C), F32)] * 4,
        compiler_params=_cparams(("parallel",) * 4),
    )(w, *gs, m, v)


def _sum2_halves(g4, s4, by_cols, *, name):
    n, R, C = g4.shape
    HR, HC = _half_shape(R, C, by_cols)
    tr, tc = _tile2d(HR, HC)
    nr, nc = HR // tr, HC // tc
    core = lax.axis_index("c").astype(jnp.int32).reshape(1)

    def body(c_ref, g_ref, s_ref, o_ref):
        o_ref[...] = (g_ref[...].astype(F32) + s_ref[...].astype(F32)).astype(BF16)

    def mine(k, i, j, c):
        return (k, i, c[0] * nc + j) if by_cols else (k, c[0] * nr + i, j)

    blk = pl.BlockSpec((None, tr, tc), lambda k, i, j, c: (k, i, j))
    return _call(
        body, name=name,
        grid_spec=pltpu.PrefetchScalarGridSpec(
            num_scalar_prefetch=1, grid=(n, nr, nc),
            in_specs=[pl.BlockSpec((None, tr, tc), mine), blk], out_specs=blk),
        out_shape=jax.ShapeDtypeStruct((n, HR, HC), BF16),
        compiler_params=_cparams(("parallel", "parallel", "parallel")),
    )(core, g4, s4)


def _rowsum(parts, *, name, out_dtype=F32):
    n, R, C = parts.shape
    tr, tc = _tile2d(R, C)

    def body(p_ref, o_ref):
        acc = p_ref[0].astype(F32)
        for i in range(1, n):
            acc = acc + p_ref[i].astype(F32)
        o_ref[...] = acc.astype(out_dtype)

    return _call(
        body, name=name, grid=(R // tr, C // tc),
        in_specs=[pl.BlockSpec((n, tr, tc), lambda i, j: (0, i, j))],
        out_specs=pl.BlockSpec((tr, tc), lambda i, j: (i, j)),
        out_shape=jax.ShapeDtypeStruct((R, C), out_dtype),
        compiler_params=_cparams(("parallel", "parallel")),
    )(parts)


def _where_am_i():
    x, y, c = lax.axis_index("x"), lax.axis_index("y"), lax.axis_index("c")
    chips = [(1 - x, y), (x, 1 - y), (1 - x, 1 - y)]
    return x, y, c, chips


def _half_idx(rows, cols, by_cols, which):
    if by_cols:
        hc = cols // 2
        return (slice(None), pl.ds(pl.multiple_of(which * hc, LANE), hc))
    hr = rows // 2
    return (pl.ds(pl.multiple_of(which * hr, 16), hr), slice(None))


def _half_shape(rows, cols, by_cols):
    return (rows, cols // 2) if by_cols else (rows // 2, cols)


def _all_gather_shards(shards, by_cols, *, name):
    n = len(shards)

    def body(*refs):
        ins, outs = refs[:n], refs[n:2 * n]
        send, recv = refs[2 * n:]
        x, y, c, chips = _where_am_i()
        me = 2 * x + y
        sibling = (x, y, 1 - c)

        def half(i, which):
            return _half_idx(*shards[i].shape, by_cols[i], which)

        def cp(i, k, src, dst, to):
            return pltpu.make_async_remote_copy(src_ref=src, dst_ref=dst, send_sem=send.at[i, k],
                                                recv_sem=recv.at[i, k], device_id=to, device_id_type=MESH)

        first = []
        for i in range(n):
            for k, (px, py) in enumerate(chips):
                d = cp(i, k, ins[i].at[half(i, c)], outs[i].at[(me,) + half(i, c)], (px, py, c))
                d.start()
                first.append(d)
        passed = []
        for i in range(n):
            for k, (px, py) in enumerate(chips):
                blk = outs[i].at[(2 * px + py,) + half(i, c)]
                cp(i, k, blk, blk, (px, py, c)).wait_recv()
                d = cp(i, 3 + k, blk, blk, sibling)
                d.start()
                passed.append(d)
        for i in range(n):
            for k, (px, py) in enumerate(chips):
                blk = outs[i].at[(2 * px + py,) + half(i, 1 - c)]
                cp(i, 3 + k, blk, blk, sibling).wait_recv()
        for d in first + passed:
            d.wait_send()

    got = _call(
        body, name=name, in_specs=[ANY] * n, out_specs=[ANY] * n,
        out_shape=[jax.ShapeDtypeStruct((N_CHIPS,) + s.shape, s.dtype) for s in shards],
        scratch_shapes=[pltpu.SemaphoreType.DMA((n, 6)), pltpu.SemaphoreType.DMA((n, 6))],
    )(*shards)
    me = 2 * lax.axis_index("x") + lax.axis_index("y")
    return [lax.dynamic_update_slice_in_dim(g, s[None], me, axis=0) for g, s in zip(got, shards)]


HBM_SPEC = pl.BlockSpec(memory_space=pltpu.HBM)
SEM_SPEC = pl.BlockSpec(memory_space=pltpu.SEMAPHORE)
DATAFLOW = pltpu.SideEffectType.DATAFLOW_SIDE_EFFECTING


def _chip_exchange_refs(kind, shards_shape, by_cols, src, land, i, chip_k, c, me):
    if kind == 'gather':
        half = _half_idx(*shards_shape, by_cols, c)
        return src.at[half], land.at[(me,) + half], land.at[(chip_k,) + half]
    return src.at[chip_k], land.at[me], land.at[chip_k]


def _chip_exchange_start(kind, srcs, by_cols, *, name, after=()):
    n = len(srcs)
    land_shapes = [((N_CHIPS,) + s.shape) if kind == 'gather' else s.shape for s in srcs]

    def body(*refs):
        src_refs, land_refs = refs[:n], refs[n:2 * n]
        send, recv = refs[2 * n + len(after)], refs[2 * n + len(after) + 1]
        token = refs[-1]
        x, y, c, chips = _where_am_i()
        me = 2 * x + y
        for i in range(n):
            for k, (px, py) in enumerate(chips):
                s, d, _ = _chip_exchange_refs(kind, srcs[i].shape, by_cols[i], src_refs[i], land_refs[i], i,
                                              2 * px + py, c, me)
                pltpu.make_async_remote_copy(src_ref=s, dst_ref=d, send_sem=send.at[3 * i + k],
                                             recv_sem=recv.at[3 * i + k], device_id=(px, py, c),
                                             device_id_type=MESH).start()
        token[...] = jnp.zeros_like(token)

    lands = [pltpu.with_memory_space_constraint(lax.empty(sh, s.dtype), pltpu.HBM) for sh, s in zip(land_shapes, srcs)]
    outs = _call(
        body, name=name,
        out_shape=(pltpu.SemaphoreType.DMA((3 * n,)), pltpu.SemaphoreType.DMA((3 * n,)),
                   *[pltpu.HBM(s.shape, s.dtype) for s in srcs],
                   *[pltpu.HBM(sh, s.dtype) for sh, s in zip(land_shapes, srcs)],
                   jax.ShapeDtypeStruct((8, LANE), F32)),
        in_specs=[HBM_SPEC] * (2 * n) + [ANY] * len(after),
        out_specs=(SEM_SPEC, SEM_SPEC, *([HBM_SPEC] * (2 * n)), pl.BlockSpec(memory_space=pltpu.VMEM)),
        input_output_aliases={j: 2 + j for j in range(2 * n)},
        compiler_params=pltpu.CompilerParams(has_side_effects=DATAFLOW),
    )(*[pltpu.with_memory_space_constraint(s, pltpu.HBM) for s in srcs], *lands, *after)
    return outs[0], outs[1], list(outs[2:2 + n]), list(outs[2 + n:2 + 2 * n]), outs[-1]


def _chip_exchange_wait(kind, send, recv, srcs, lands, by_cols, after, *, name):
    n = len(srcs)

    def body(*refs):
        src_refs, land_refs = refs[:n], refs[n:2 * n]
        send_r, recv_r = refs[2 * n], refs[2 * n + 1]
        x, y, c, chips = _where_am_i()
        me = 2 * x + y
        for i in range(n):
            for k, (px, py) in enumerate(chips):
                s, _, d = _chip_exchange_refs(kind, srcs[i].shape, by_cols[i], src_refs[i], land_refs[i], i,
                                              2 * px + py, c, me)
                cp = pltpu.make_async_remote_copy(src_ref=s, dst_ref=d, send_sem=send_r.at[3 * i + k],
                                                  recv_sem=recv_r.at[3 * i + k], device_id=(px, py, c),
                                                  device_id_type=MESH)
                cp.wait_send()
                cp.wait_recv()

    outs = _call(
        body, name=name,
        out_shape=(*[pltpu.HBM(s.shape, s.dtype) for s in srcs], *[pltpu.HBM(l.shape, l.dtype) for l in lands]),
        in_specs=[HBM_SPEC] * (2 * n) + [SEM_SPEC, SEM_SPEC] + [ANY] * len(after),
        out_specs=tuple([HBM_SPEC] * (2 * n)),
        input_output_aliases={j: j for j in range(2 * n)},
        compiler_params=pltpu.CompilerParams(has_side_effects=DATAFLOW),
    )(*srcs, *lands, send, recv, *after)
    return list(outs[n:])


def _sibling_pass_gathered(lands, shard_shapes, by_cols, *, name):
    n = len(lands)

    def body(*refs):
        outs = refs[n:2 * n]
        send, recv = refs[2 * n:]
        x, y, c, chips = _where_am_i()
        sibling = (x, y, 1 - c)
        cps = []
        for i in range(n):
            for k, (px, py) in enumerate(chips):
                blk = outs[i].at[(2 * px + py,) + _half_idx(*shard_shapes[i], by_cols[i], c)]
                d = pltpu.make_async_remote_copy(src_ref=blk, dst_ref=blk, send_sem=send.at[i, k],
                                                 recv_sem=recv.at[i, k], device_id=sibling, device_id_type=MESH)
                d.start()
                cps.append(d)
        for i in range(n):
            for k, (px, py) in enumerate(chips):
                blk = outs[i].at[(2 * px + py,) + _half_idx(*shard_shapes[i], by_cols[i], 1 - c)]
                pltpu.make_async_remote_copy(src_ref=blk, dst_ref=blk, send_sem=send.at[i, k], recv_sem=recv.at[i, k],
                                             device_id=sibling, device_id_type=MESH).wait_recv()
        for d in cps:
            d.wait_send()

    return _call(
        body, name=name, in_specs=[ANY] * n, out_specs=[ANY] * n,
        out_shape=[jax.ShapeDtypeStruct(l.shape, l.dtype) for l in lands],
        input_output_aliases={j: j for j in range(n)},
        scratch_shapes=[pltpu.SemaphoreType.DMA((n, 3)), pltpu.SemaphoreType.DMA((n, 3))],
    )(*lands)


def _own_slot(lands, owns):
    me = 2 * lax.axis_index("x") + lax.axis_index("y")
    return [lax.dynamic_update_slice_in_dim(g, s, me, axis=0) for g, s in zip(lands, owns)]


def _sibling_send_halves(grads, by_cols, *, name):
    n = len(grads)

    def body(*refs):
        ins, outs = refs[:n], refs[n:2 * n]
        send, recv = refs[2 * n:]
        x, y, c, _ = _where_am_i()
        sibling = (x, y, 1 - c)
        cps = []
        for i in range(n):
            src = ins[i].at[(slice(None),) + _half_idx(*grads[i].shape[1:], by_cols[i], 1 - c)]
            d = pltpu.make_async_remote_copy(src_ref=src, dst_ref=outs[i], send_sem=send.at[i],
                                             recv_sem=recv.at[i], device_id=sibling, device_id_type=MESH)
            d.start()
            cps.append(d)
        for d in cps:
            d.wait()

    return _call(
        body, name=name, in_specs=[ANY] * n, out_specs=[ANY] * n,
        out_shape=[jax.ShapeDtypeStruct((N_CHIPS,) + _half_shape(*g.shape[1:], bc), g.dtype)
                   for g, bc in zip(grads, by_cols)],
        scratch_shapes=[pltpu.SemaphoreType.DMA((n,)), pltpu.SemaphoreType.DMA((n,))],
    )(*grads)


def _scatter_to_chips(parts, *, name):
    n = len(parts)

    def body(*refs):
        ins, outs = refs[:n], refs[n:2 * n]
        send, recv = refs[2 * n:]
        x, y, c, chips = _where_am_i()
        me = 2 * x + y
        cps = []
        for i in range(n):
            for k, (px, py) in enumerate(chips):
                d = pltpu.make_async_remote_copy(
                    src_ref=ins[i].at[2 * px + py], dst_ref=outs[i].at[me], send_sem=send.at[i, k],
                    recv_sem=recv.at[i, k], device_id=(px, py, c), device_id_type=MESH)
                d.start()
                cps.append((d, i, k, px, py))
        for d, i, k, px, py in cps:
            blk = outs[i].at[2 * px + py]
            pltpu.make_async_remote_copy(src_ref=blk, dst_ref=blk, send_sem=send.at[i, k], recv_sem=recv.at[i, k],
                                         device_id=(px, py, c), device_id_type=MESH).wait_recv()
        for d, *_ in cps:
            d.wait_send()

    got = _call(
        body, name=name, in_specs=[ANY] * n, out_specs=[ANY] * n,
        out_shape=[jax.ShapeDtypeStruct(p.shape, p.dtype) for p in parts],
        scratch_shapes=[pltpu.SemaphoreType.DMA((n, 3)), pltpu.SemaphoreType.DMA((n, 3))],
    )(*parts)
    me = 2 * lax.axis_index("x") + lax.axis_index("y")
    return [lax.dynamic_update_slice_in_dim(g, lax.dynamic_slice_in_dim(p, me, 1, axis=0), me, axis=0)
            for g, p in zip(got, parts)]


def _sibling_join_halves(halves, *, name):
    n = len(halves)

    def body(*refs):
        ins, outs = refs[:n], refs[n:2 * n]
        send, recv = refs[2 * n:]
        x, y, c, _ = _where_am_i()
        sibling = (x, y, 1 - c)
        cps = []
        for i in range(n):
            d = pltpu.make_async_remote_copy(src_ref=ins[i], dst_ref=outs[i], send_sem=send.at[i],
                                             recv_sem=recv.at[i], device_id=sibling, device_id_type=MESH)
            d.start()
            cps.append(d)
        for d in cps:
            d.wait()

    return _call(
        body, name=name, in_specs=[ANY] * n, out_specs=[ANY] * n,
        out_shape=[jax.ShapeDtypeStruct(h.shape, h.dtype) for h in halves],
        scratch_shapes=[pltpu.SemaphoreType.DMA((n,)), pltpu.SemaphoreType.DMA((n,))],
    )(*halves)


def _all_reduce_small(v, *, name):
    R, C = v.shape

    def body(v_ref, o_ref, sib, slots, send, recv):
        x, y, c, chips = _where_am_i()
        me = 2 * x + y
        sibling = (x, y, 1 - c)
        d = pltpu.make_async_remote_copy(src_ref=v_ref, dst_ref=sib, send_sem=send.at[0], recv_sem=recv.at[0],
                                         device_id=sibling, device_id_type=MESH)
        d.start()
        d.wait()
        slots[me] = v_ref[...] + sib[...]
        cps = []
        for k, (px, py) in enumerate(chips):
            d = pltpu.make_async_remote_copy(src_ref=slots.at[me], dst_ref=slots.at[me], send_sem=send.at[1 + k],
                                             recv_sem=recv.at[1 + k], device_id=(px, py, c), device_id_type=MESH)
            d.start()
            cps.append(d)
        for k, (px, py) in enumerate(chips):
            blk = slots.at[2 * px + py]
            pltpu.make_async_remote_copy(src_ref=blk, dst_ref=blk, send_sem=send.at[1 + k], recv_sem=recv.at[1 + k],
                                         device_id=(px, py, c), device_id_type=MESH).wait_recv()
        for d in cps:
            d.wait_send()
        o_ref[...] = (slots[0] + slots[1]) + (slots[2] + slots[3])

    vm = pl.BlockSpec(memory_space=pltpu.VMEM)
    return _call(
        body, name=name, in_specs=[vm], out_specs=vm,
        out_shape=jax.ShapeDtypeStruct((R, C), F32),
        scratch_shapes=[pltpu.VMEM((R, C), F32), pltpu.VMEM((N_CHIPS, R, C), F32),
                        pltpu.SemaphoreType.DMA((4,)), pltpu.SemaphoreType.DMA((4,))],
        compiler_params=pltpu.CompilerParams(vmem_limit_bytes=VMEM_LIMIT),
    )(v)


def _cols_from_shards(g):
    return jnp.transpose(g, (1, 0, 2)).reshape(g.shape[1], -1)


def _shards_from_cols(w):
    R, C4 = w.shape
    return jnp.transpose(w.reshape(R, N_CHIPS, C4 // N_CHIPS), (1, 0, 2))


def _block_diag(t):
    G, a, b = t.shape
    eye = jnp.eye(G, dtype=t.dtype)
    return (t[:, :, None, :] * eye[:, None, :, None]).reshape(G * a, G * b)


def _diag_blocks(xm, G):
    a, b = xm.shape[0] // G, xm.shape[1] // G
    idx = jnp.arange(G)
    return xm.reshape(G, a, G, b)[idx, :, idx, :]


def _pack(arrs):
    flat = []
    for a in arrs:
        f = a.reshape(-1).astype(F32)
        flat.append(jnp.pad(f, (0, _rup(f.shape[0], LANE) - f.shape[0])))
    v = jnp.concatenate(flat)
    rows = _rup(v.shape[0] // LANE, 8)
    v = jnp.pad(v, (0, rows * LANE - v.shape[0]))
    return v.reshape(rows, LANE)


def _unpack(v, shapes):
    flat = v.reshape(-1)
    out, off = [], 0
    for s in shapes:
        n = int(np.prod(s))
        out.append(flat[off:off + n].reshape(s))
        off += _rup(n, LANE)
    return out


def _ffn_fwd(x, Wup, Wdn, cw, cb, tag):
    h = _mm(x, Wup, 'nt', bmode='bo', tm=512, tn=4096, name=f"ffn_up_{tag}")
    a = _act_fwd(h, cw, cb, name=f"ffn_act_{tag}")
    f = _mm(a, Wdn, 'nn', bmode='abr', tm=512, tn=1024, tk=4096, name=f"ffn_down_{tag}")
    return f, h, a


def _ffn_bwd(df, x, h, a, Wup, Wdn, cw, cb, tag):
    da = _mm(df, Wdn, 'nt', bmode='bo', tm=512, tn=4096, name=f"ffn_da_{tag}")
    dWdn = _mm(a, df, 'tn', bmode='ao', tm=4096, tn=512, name=f"ffn_dwdn_{tag}", out_dtype=BF16)
    dh, dcw, dcb = _act_bwd(h, da, cw, cb, name=f"ffn_actb_{tag}")
    dx = _mm(dh, Wup, 'nn', bmode='abr', tm=512, tn=1024, tk=4096, name=f"ffn_dx_{tag}")
    dWup = _mm(dh, x, 'tn', bmode='ao', tm=4096, tn=512, name=f"ffn_dwup_{tag}", out_dtype=BF16)
    return dx, dWup, dWdn, dcw, dcb


def kernel(x, positions, ev_w_in, ev_b_f, ev_lambda_re, ev_lambda_im, ev_log_step, ev_ssm_b_re, ev_ssm_b_im, ev_ssm_c_re, ev_ssm_c_im, ev_ssm_d, ev_w_glu, ev_w_out, od_w_in, od_sinks, od_w_out, ln_mix_g, ln_mix_b, ffn_w_up, ffn_conv_w, ffn_conv_b, ffn_w_down, ln_ffn_g, ln_ffn_b, loss_target, m_ev_w_in, m_ev_b_f, m_ev_lambda_re, m_ev_lambda_im, m_ev_log_step, m_ev_ssm_b_re, m_ev_ssm_b_im, m_ev_ssm_c_re, m_ev_ssm_c_im, m_ev_ssm_d, m_ev_w_glu, m_ev_w_out, m_od_w_in, m_od_sinks, m_od_w_out, m_ln_mix_g, m_ln_mix_b, m_ffn_w_up, m_ffn_conv_w, m_ffn_conv_b, m_ffn_w_down, m_ln_ffn_g, m_ln_ffn_b, v_ev_w_in, v_ev_b_f, v_ev_lambda_re, v_ev_lambda_im, v_ev_log_step, v_ev_ssm_b_re, v_ev_ssm_b_im, v_ev_ssm_c_re, v_ev_ssm_c_im, v_ev_ssm_d, v_ev_w_glu, v_ev_w_out, v_od_w_in, v_od_sinks, v_od_w_out, v_ln_mix_g, v_ln_mix_b, v_ffn_w_up, v_ffn_conv_w, v_ffn_conv_b, v_ffn_w_down, v_ln_ffn_g, v_ln_ffn_b):
    W = dict(ev_w_in=ev_w_in, ev_b_f=ev_b_f, ev_lambda_re=ev_lambda_re, ev_lambda_im=ev_lambda_im, ev_log_step=ev_log_step, ev_ssm_b_re=ev_ssm_b_re, ev_ssm_b_im=ev_ssm_b_im, ev_ssm_c_re=ev_ssm_c_re, ev_ssm_c_im=ev_ssm_c_im, ev_ssm_d=ev_ssm_d, ev_w_glu=ev_w_glu, ev_w_out=ev_w_out, od_w_in=od_w_in, od_sinks=od_sinks, od_w_out=od_w_out, ln_mix_g=ln_mix_g, ln_mix_b=ln_mix_b, ffn_w_up=ffn_w_up, ffn_conv_w=ffn_conv_w, ffn_conv_b=ffn_conv_b, ffn_w_down=ffn_w_down, ln_ffn_g=ln_ffn_g, ln_ffn_b=ln_ffn_b)
    Mo = dict(ev_w_in=m_ev_w_in, ev_b_f=m_ev_b_f, ev_lambda_re=m_ev_lambda_re, ev_lambda_im=m_ev_lambda_im, ev_log_step=m_ev_log_step, ev_ssm_b_re=m_ev_ssm_b_re, ev_ssm_b_im=m_ev_ssm_b_im, ev_ssm_c_re=m_ev_ssm_c_re, ev_ssm_c_im=m_ev_ssm_c_im, ev_ssm_d=m_ev_ssm_d, ev_w_glu=m_ev_w_glu, ev_w_out=m_ev_w_out, od_w_in=m_od_w_in, od_sinks=m_od_sinks, od_w_out=m_od_w_out, ln_mix_g=m_ln_mix_g, ln_mix_b=m_ln_mix_b, ffn_w_up=m_ffn_w_up, ffn_conv_w=m_ffn_conv_w, ffn_conv_b=m_ffn_conv_b, ffn_w_down=m_ffn_w_down, ln_ffn_g=m_ln_ffn_g, ln_ffn_b=m_ln_ffn_b)
    Vo = dict(ev_w_in=v_ev_w_in, ev_b_f=v_ev_b_f, ev_lambda_re=v_ev_lambda_re, ev_lambda_im=v_ev_lambda_im, ev_log_step=v_ev_log_step, ev_ssm_b_re=v_ev_ssm_b_re, ev_ssm_b_im=v_ev_ssm_b_im, ev_ssm_c_re=v_ev_ssm_c_re, ev_ssm_c_im=v_ev_ssm_c_im, ev_ssm_d=v_ev_ssm_d, ev_w_glu=v_ev_w_glu, ev_w_out=v_ev_w_out, od_w_in=v_od_w_in, od_sinks=v_od_sinks, od_w_out=v_od_w_out, ln_mix_g=v_ln_mix_g, ln_mix_b=v_ln_mix_b, ffn_w_up=v_ffn_w_up, ffn_conv_w=v_ffn_conv_w, ffn_conv_b=v_ffn_conv_b, ffn_w_down=v_ffn_w_down, ln_ffn_g=v_ln_ffn_g, ln_ffn_b=v_ln_ffn_b)
    names = list(W.keys())
    big = ['ev_w_in', 'ev_w_glu', 'ev_w_out', 'od_w_in', 'od_w_out', 'ffn_w_up', 'ffn_w_down']

    S, D = x.shape[1], x.shape[2]
    x0 = x.reshape(S, D)
    tgt = loss_target.reshape(S, D)
    G, Pn, Cg = SSM_GROUPS, SSM_STATE, SSM_GROUP
    Fs = ffn_w_up.shape[2]
    FP = Fs
    Rd = ffn_w_down.shape[1]
    EIN = N_CHIPS * ev_w_in.shape[2]

    def as2d(a):
        return a.reshape(-1, a.shape[-1])

    cwl = ffn_conv_w.reshape(-1)
    cw_rows = _rup(_rup(cwl.shape[0], LANE) // LANE, 32)
    cw_pad = jnp.pad(cwl, (0, cw_rows * LANE - cwl.shape[0])).reshape(cw_rows, LANE)
    transposed = ('ev_w_in', 'ffn_w_up')

    def view(n, a):
        return jnp.transpose(a, (0, 2, 1)) if n in transposed else a

    Wv = {n: view(n, W[n]) for n in big}
    big_e = [(n, l) for n in big for l in range(W[n].shape[0])]
    split_cols = {e: (Wv[e[0]].shape[1] // 2) % 16 != 0 for e in big_e}
    shard16 = {e: Wv[e[0]][e[1]].astype(BF16) for e in big_e}
    grp_now = [e for e in big_e if e[0].startswith('ev_')]
    grp_ffn0 = [('ffn_w_up', 0), ('ffn_w_down', 0)]
    grp_l1 = [('od_w_in', 0), ('od_w_out', 0), ('ffn_w_up', 1), ('ffn_w_down', 1)]
    src_now = [shard16[e] for e in grp_now]
    src_ffn0 = [shard16[e] for e in grp_ffn0] + [cw_pad]
    src_l1 = [shard16[e] for e in grp_l1]
    cols_now = [split_cols[e] for e in grp_now]
    cols_ffn0 = [split_cols[e] for e in grp_ffn0] + [False]
    cols_l1 = [split_cols[e] for e in grp_l1]
    ag_now = _chip_exchange_start('gather', src_now, cols_now, name="ag_l0_start")
    ag_ffn0 = _chip_exchange_start('gather', src_ffn0, cols_ffn0, name="ag_ffn0_start", after=[ag_now[4]])
    ag_l1 = _chip_exchange_start('gather', src_l1, cols_l1, name="ag_l1_start", after=[ag_ffn0[4]])
    started = [ag_l1[4]]

    def finish_gather(started, srcs, cols, after, tag):
        send, recv, thru, lands, _ = started
        lands = _chip_exchange_wait('gather', send, recv, thru, lands, cols, after, name=f"ag_{tag}_wait")
        lands = _sibling_pass_gathered(lands, [s.shape for s in srcs], cols, name=f"ag_{tag}_pass")
        return _own_slot(lands, [s[None] for s in srcs])

    lam_r, lam_i = ev_lambda_re[0], ev_lambda_im[0]
    lstep = ev_log_step[0].reshape(G, 1)
    a_re, a_im, g_re, g_im = _s5_disc_fwd(lam_r, lam_i, lstep, name="s5_disc", after=started)
    b_re2, b_im2 = ev_ssm_b_re[0].reshape(G * Pn, Cg), ev_ssm_b_im[0].reshape(G * Pn, Cg)
    g_re1, g_im1 = g_re.reshape(G * Pn, 1), g_im.reshape(G * Pn, 1)
    bb_re, bb_im = _s5_bb_fwd(g_re1, g_im1, b_re2, b_im2, name="s5_bb")
    bbt = jnp.stack([jnp.transpose(b.reshape(G, Pn, Cg), (0, 2, 1)).reshape(G * Cg, Pn) for b in (bb_re, bb_im)])
    BB = _diag_expand(bbt, Cg, Pn, name="s5_bb_dense")
    cct = jnp.stack([jnp.transpose(ev_ssm_c_re[0], (0, 2, 1)).reshape(G * Pn, Cg),
                     jnp.transpose(-ev_ssm_c_im[0], (0, 2, 1)).reshape(G * Pn, Cg)])
    CC = _diag_expand(cct, Pn, Cg, name="s5_cc_dense", after=started)
    a_cat = jnp.stack([a_re.reshape(1, G * Pn), a_im.reshape(1, G * Pn)])
    dskip = ev_ssm_d[0].reshape(1, SSM_WIDTH)
    tabs = _rope_tables(positions.reshape(S, 1).astype(F32), name="rope_tables", after=[BB, CC])

    gw = dict(zip(grp_now, finish_gather(ag_now, src_now, cols_now, [tabs[2]], "l0")))
    gw.update({n: gw[(n, 0)] for n in big if (n, 0) in gw and W[n].shape[0] == 1})
    w_in_t = gw['ev_w_in'].reshape(EIN, D)
    qkv_w = 3 * FOX_WIDTH
    WmainT = jnp.concatenate([w_in_t[:qkv_w], w_in_t[qkv_w + FOX_HEADS:]], axis=0)
    WfT = jnp.pad(w_in_t[qkv_w:qkv_w + FOX_HEADS], ((0, LANE - FOX_HEADS), (0, 0)))
    Wglu = _cols_from_shards(gw['ev_w_glu'])
    Wout_ev = gw['ev_w_out'].reshape(D, D)
    cbs = [ffn_conv_b[l].reshape(N_CHIPS, Fs) for l in range(DEPTH)]

    P = _mm(x0, WmainT, 'nt', name="ev_proj")
    fl = _mm(x0, WfT, 'nt', name="ev_proj_f")
    bf_pad = jnp.pad(ev_b_f.reshape(1, FOX_HEADS), ((0, 0), (0, LANE - FOX_HEADS)))
    cgate, sgate = _gate_fwd(fl, bf_pad, name="fox_gate")
    ccol = jnp.transpose(cgate[:, :FOX_HEADS]).reshape(FOX_HEADS, S, 1)
    crow = jnp.transpose(cgate[:, :FOX_HEADS]).reshape(FOX_HEADS, 1, S)
    fox, lse = _fox_fwd(P, ccol, crow, name="fox_fwd")
    u_s5 = P[:, qkv_w:]
    bu = _mm(u_s5, BB, 'nn', bmode='bo', name="s5_bu")
    hh = _s5_scan_fwd(bu, a_cat, name="s5_scan")
    yc = _mm(hh, CC, 'nn', bmode='abr', name="s5_y")
    y_s5, yg = _s5_out_fwd(yc, P, dskip, name="s5_out")
    z = _mm(yg, Wglu, 'nn', name="s5_glu_proj")
    ssm = _glu_fwd(z, name="s5_glu")
    cat = jnp.concatenate([fox.astype(BF16), ssm], axis=1)
    mix0 = _mm(cat, Wout_ev, 'nn', name="ev_out")
    x1, xh1, rs1 = _add_ln_fwd(x0, mix0, ln_mix_g[0], ln_mix_b[0], name="ln_mix0")
    got = finish_gather(ag_ffn0, src_ffn0, cols_ffn0, [x1], "ffn0")
    gw.update(zip(grp_ffn0, got[:-1]))
    cw_all = got[-1].reshape(N_CHIPS, -1)[:, :cwl.shape[0]].reshape(N_CHIPS, DEPTH, 3, Fs)
    cws = [cw_all[:, l] for l in range(DEPTH)]
    Wup = {0: gw[('ffn_w_up', 0)]}
    Wdn = {0: gw[('ffn_w_down', 0)].reshape(2, Fs, D)}
    f0, hf0, af0 = _ffn_fwd(x1, Wup[0], Wdn[0], cws[0], cbs[0], "l0")
    x2, xh2, rs2 = _add_ln_fwd(x1, f0, ln_ffn_g[0], ln_ffn_b[0], name="ln_ffn0")

    gw.update(zip(grp_l1, finish_gather(ag_l1, src_l1, cols_l1, [x2], "l1")))
    Wodin = _cols_from_shards(gw[('od_w_in', 0)])
    Wodout = gw[('od_w_out', 0)].reshape(D, D)
    Wup[1] = gw[('ffn_w_up', 1)]
    Wdn[1] = gw[('ffn_w_down', 1)].reshape(2, Fs, D)
    QW, KW = SWA_HEADS * SWA_HEAD_DIM, SWA_KV_HEADS * SWA_HEAD_DIM
    P1 = _mm(x2, Wodin, 'nn', name="od_proj")
    qr = _rope_apply(P1, tabs, col0=0, width=QW, inverse=False, name="rope_q", out_dtype=BF16)
    kr = _rope_apply(P1, tabs, col0=QW, width=KW, inverse=False, name="rope_k", out_dtype=BF16)

    def heads(a2, nh):
        return jnp.transpose(a2.reshape(S, nh, SWA_HEAD_DIM), (1, 0, 2))

    def unheads(a3):
        return jnp.transpose(a3, (1, 0, 2)).reshape(S, -1)

    qT, kT = heads(qr, SWA_HEADS), heads(kr, SWA_KV_HEADS)
    vT = heads(P1[:, QW + KW:].astype(BF16), SWA_KV_HEADS)
    sink_rows = jnp.broadcast_to(od_sinks[0].reshape(SWA_KV_HEADS, SWA_GROUPS, 1, 1),
                                 (SWA_KV_HEADS, SWA_GROUPS, SWA_WINDOW, 1)).reshape(SWA_KV_HEADS, -1, 1)
    oT, Lsw = _swa_fwd(qT, kT, vT, sink_rows, name="swa_fwd")
    o_sw = unheads(oT).astype(BF16)
    mix1 = _mm(o_sw, Wodout, 'nn', name="od_out")
    x3, xh3, rs3 = _add_ln_fwd(x2, mix1, ln_mix_g[1], ln_mix_b[1], name="ln_mix1")
    f1, hf1, af1 = _ffn_fwd(x3, Wup[1], Wdn[1], cws[1], cbs[1], "l1")
    x4, xh4, rs4 = _add_ln_fwd(x3, f1, ln_ffn_g[1], ln_ffn_b[1], name="ln_ffn1")
    dy, loss_part = _loss_grad(x4, tgt, name="loss")

    dz4, dg_ffn1, db_ffn1 = _ln_bwd(dy, None, xh4, rs4, ln_ffn_g[1], name="lnb_ffn1")
    dx3f, dWup1, dWdn1, dcw1, dcb1 = _ffn_bwd(dz4, x3, hf1, af1, Wup[1], Wdn[1], cws[1], cbs[1], "l1")
    dz3, dg_mix1, db_mix1 = _ln_bwd(dz4, dx3f, xh3, rs3, ln_mix_g[1], name="lnb_mix1")
    do_sw = _mm(dz3, Wodout, 'nt', name="od_out_dx")
    dWodout = _mm(o_sw, dz3, 'tn', name="od_out_dw", out_dtype=BF16)
    doT = heads(do_sw, SWA_HEADS)
    dqT, dkT, dvT, dsink = _swa_bwd(qT, kT, vT, sink_rows, oT, Lsw, doT, name="swa_bwd")
    dq1 = _rope_apply(unheads(dqT), tabs, col0=0, width=QW, inverse=True, name="rope_dq", out_dtype=BF16)
    dk1 = _rope_apply(unheads(dkT[:, SWA_WINDOW:]), tabs, col0=0, width=KW, inverse=True, name="rope_dk",
                      out_dtype=BF16)
    dP1 = jnp.concatenate([dq1, dk1, unheads(dvT[:, SWA_WINDOW:]).astype(BF16)], axis=1)
    dx2m = _mm(dP1, Wodin, 'nt', name="od_proj_dx")
    dWodin = _mm(x2, dP1, 'tn', name="od_proj_dw", out_dtype=BF16)

    def rs_begin(entries, grads, tag):
        cols = [split_cols[e] for e in entries]
        sib = _sibling_send_halves(grads, cols, name=f"rs_{tag}_sibling")
        return [_sum2_halves(g4, s4, bc, name=f"rs_sum2_{n}{l}")
                for (n, l), g4, s4, bc in zip(entries, grads, sib, cols)]

    def own_parts(parts):
        me = 2 * lax.axis_index("x") + lax.axis_index("y")
        return [lax.dynamic_slice_in_dim(p, me, 1, axis=0) for p in parts]

    part_l1 = rs_begin(grp_l1, [_shards_from_cols(dWodin), dWodout.reshape(N_CHIPS, D // N_CHIPS, D), dWup1,
                                dWdn1.reshape(N_CHIPS, Rd, D)], "l1")
    rs_l1 = _chip_exchange_start('scatter', part_l1, [False] * len(part_l1), name="rs_l1_start")

    dz2, dg_ffn0, db_ffn0 = _ln_bwd(dz3, dx2m, xh2, rs2, ln_ffn_g[0], name="lnb_ffn0", after=[rs_l1[4]])
    dx1f, dWup0, dWdn0, dcw0, dcb0 = _ffn_bwd(dz2, x1, hf0, af0, Wup[0], Wdn[0], cws[0], cbs[0], "l0")
    part_ffn0 = rs_begin(grp_ffn0, [dWup0, dWdn0.reshape(N_CHIPS, Rd, D)], "ffn0")
    rs_ffn0 = _chip_exchange_start('scatter', part_ffn0, [False] * len(part_ffn0), name="rs_ffn0_start")
    dz1, dg_mix0, db_mix0 = _ln_bwd(dz2, dx1f, xh1, rs1, ln_mix_g[0], name="lnb_mix0", after=[rs_ffn0[4]])
    dcat = _mm(dz1, Wout_ev, 'nt', name="ev_out_dx")
    dWout_ev = _mm(cat, dz1, 'tn', name="ev_out_dw", out_dtype=BF16)
    dz = _glu_bwd(z, dcat, name="s5_glu_bwd")
    dyg = _mm(dz, Wglu, 'nt', name="s5_glu_dx")
    dWglu = _mm(yg, dz, 'tn', name="s5_glu_dw", out_dtype=BF16)
    dy_s5, du_dir, dD = _s5_out_bwd(dyg, y_s5, P, dskip, name="s5_out_bwd")
    dhh = _mm(dy_s5, CC, 'nt', bmode='bo', name="s5_y_dx")
    dCC = _mm(hh, dy_s5, 'tn', bmode='ao', name="s5_y_dw")
    lam, da_s5 = _s5_scan_bwd(dhh, hh, a_cat, name="s5_scan_bwd")
    du_bu = _mm(lam, BB, 'nt', bmode='abr', name="s5_bu_dx")
    dBB = _mm(u_s5, lam, 'tn', bmode='bo', name="s5_bu_dw")
    du = _combine([du_dir, du_bu], [1.0, 1.0], name="s5_du", out_dtype=BF16)
    dq0, dk0, dv0, dccol, dcrow = _fox_bwd(P, ccol, crow, fox, lse, dcat, name="fox_bwd")
    dc = jnp.transpose((dccol.reshape(FOX_HEADS, S) - dcrow.reshape(FOX_HEADS, S)))
    dc = jnp.pad(dc, ((0, 0), (0, LANE - FOX_HEADS)))
    dfl, dbf = _gate_bwd(dc, sgate, name="fox_gate_bwd")
    dP = jnp.concatenate([dq0, dk0, dv0, du], axis=1)
    dx0a = _mm(dP, WmainT, 'nn', name="ev_proj_dx")
    dx0b = _mm(dfl, WfT, 'nn', name="ev_proj_f_dx")
    dWmainT = _mm(dP, x0, 'tn', tm=1024, tn=1024, name="ev_proj_dw", out_dtype=BF16)
    dWfT = _mm(dfl, x0, 'tn', name="ev_proj_f_dw", out_dtype=BF16)
    grad_x = _combine([dz1, dx0a, dx0b], [ALPHA, 1.0, 1.0], name="grad_x")

    dbbt = _diag_extract(dBB, Cg, Pn, name="s5_bb_diag")
    dcct = _diag_extract(dCC, Pn, Cg, name="s5_cc_diag")
    dbb_re = jnp.transpose(dbbt[0].reshape(G, Cg, Pn), (0, 2, 1)).reshape(G * Pn, Cg)
    dbb_im = jnp.transpose(dbbt[1].reshape(G, Cg, Pn), (0, 2, 1)).reshape(G * Pn, Cg)
    db_re, db_im, dg_re1, dg_im1 = _s5_bb_bwd(g_re1, g_im1, b_re2, b_im2, dbb_re, dbb_im, name="s5_bb_bwd")
    dlam_re, dlam_im, dlstep = _s5_disc_bwd(lam_r, lam_i, lstep, da_s5[0].reshape(G, Pn), da_s5[1].reshape(G, Pn),
                                            dg_re1.reshape(G, Pn), dg_im1.reshape(G, Pn), name="s5_disc_bwd")
    dc_re = jnp.transpose(dcct[0].reshape(G, Pn, Cg), (0, 2, 1))
    dc_im = -jnp.transpose(dcct[1].reshape(G, Pn, Cg), (0, 2, 1))

    def conv_w_full(d0, d1):
        return jnp.stack([jnp.reshape(jnp.transpose(d[:, :, :Fs], (1, 0, 2)), (3, N_CHIPS * Fs)) for d in (d0, d1)])

    def conv_b_full(d0, d1):
        return jnp.stack([jnp.reshape(d[:, 0, :Fs], (N_CHIPS * Fs,)) for d in (d0, d1)])

    small_local = dict(
        ev_b_f=dbf[:, :FOX_HEADS], ev_lambda_re=dlam_re, ev_lambda_im=dlam_im, ev_log_step=dlstep,
        ev_ssm_b_re=db_re, ev_ssm_b_im=db_im, ev_ssm_c_re=dc_re, ev_ssm_c_im=dc_im, ev_ssm_d=dD,
        od_sinks=dsink[:, :, 0],
        ln_mix_g=jnp.concatenate([dg_mix0, dg_mix1]), ln_mix_b=jnp.concatenate([db_mix0, db_mix1]),
        ffn_conv_w=conv_w_full(dcw0, dcw1), ffn_conv_b=conv_b_full(dcb0, dcb1),
        ln_ffn_g=jnp.concatenate([dg_ffn0, dg_ffn1]), ln_ffn_b=jnp.concatenate([db_ffn0, db_ffn1]))
    small = list(small_local.keys())
    red = _all_reduce_small(_pack([small_local[n] for n in small] + [loss_part]), name="ar_small")
    full_shapes = [W[n].shape if n != 'ffn_conv_w' else (DEPTH, 3, N_CHIPS * Fs) for n in small]
    pieces = _unpack(red, full_shapes + [()])
    loss = pieces[-1]
    gsmall = dict(zip(small, pieces[:-1]))
    chip = 2 * lax.axis_index("x") + lax.axis_index("y")
    gsmall['ffn_conv_w'] = lax.dynamic_slice_in_dim(gsmall['ffn_conv_w'], chip * Fs, Fs, axis=2)
    shapes = [W[n].shape for n in small]
    gs, ds_, ms, vs = _adamw(_pack([W[n] for n in small])[None], _pack([gsmall[n] for n in small])[None],
                             _pack([Mo[n] for n in small])[None], _pack([Vo[n] for n in small])[None],
                             name="adamw_small", tr=1 << 14)
    out_g = dict(zip(small, _unpack(gs, shapes)))
    out_d = dict(zip(small, _unpack(ds_, shapes)))
    out_m = dict(zip(small, _unpack(ms, shapes)))
    out_v = dict(zip(small, _unpack(vs, shapes)))

    dw_in_t = jnp.concatenate([dWmainT[:qkv_w], dWfT[:FOX_HEADS], dWmainT[qkv_w:]], axis=0)
    part_now = rs_begin(grp_now, [dw_in_t.reshape(N_CHIPS, EIN // N_CHIPS, D), _shards_from_cols(dWglu),
                                  dWout_ev.reshape(N_CHIPS, D // N_CHIPS, D)], "l0")
    rs_now = _chip_exchange_start('scatter', part_now, [False] * len(part_now), name="rs_l0_start")

    def finish_scatter(started, parts, after, tag):
        send, rcv, thru, lands, _ = started
        lands = _chip_exchange_wait('scatter', send, rcv, thru, lands, [False] * len(parts), after,
                                    name=f"rs_{tag}_wait")
        return _own_slot(lands, own_parts(parts))

    def update(entries, recv, tag):
        halves = [_rowsum(r, name=f"rs_sum4_{e[0]}{e[1]}") for e, r in zip(entries, recv)]
        others = _sibling_join_halves(halves, name=f"rs_{tag}_join")
        pairs = dict(zip(entries, zip(halves, others)))
        done = []
        for n in dict.fromkeys(e[0] for e in entries):
            res = _adamw(Wv[n], [pairs[(n, l)] for l in range(W[n].shape[0])], view(n, Mo[n]), view(n, Vo[n]),
                         name=f"adamw_{n}", by_cols=split_cols[(n, 0)])
            out_g[n], out_d[n], out_m[n], out_v[n] = (view(n, t) for t in res)
            done.append(res[3])
        return done

    recv_rest = (finish_scatter(rs_l1, part_l1, [rs_now[4]], "l1")
                 + finish_scatter(rs_ffn0, part_ffn0, [rs_now[4]], "ffn0"))
    done = update(grp_l1 + grp_ffn0, recv_rest, "rest")
    update(grp_now, finish_scatter(rs_now, part_now, done, "l0"), "l0")

    return (loss, grad_x.reshape(1, S, D), *[out_g[n] for n in names], *[out_d[n] for n in names],
            *[out_m[n] for n in names], *[out_v[n] for n in names])
```

```python
import functools
import math

import numpy as np
import jax
import jax.numpy as jnp
from jax import lax
from jax.experimental import pallas as pl
from jax.experimental.pallas import tpu as pltpu

F32 = jnp.float32
BF16 = jnp.bfloat16
MESH = pl.DeviceIdType.MESH
ANY = pl.BlockSpec(memory_space=pl.ANY)

D_MODEL = 2048
FOX_HEADS = 8
FOX_HEAD_DIM = 128
FOX_WIDTH = 1024
SSM_WIDTH = 1024
SSM_GROUP = 16
SSM_GROUPS = 64
SSM_STATE = 64
SWA_HEADS = 32
SWA_KV_HEADS = 4
SWA_HEAD_DIM = 64
SWA_GROUPS = 8
SWA_WINDOW = 128
ROPE_DIM = 16
ROPE_THETA = 500000.0
LN_EPS = 1e-5
DEPTH = 2
ALPHA = (2.0 * DEPTH) ** 0.25
ADAM_LR = 0.001
ADAM_B1 = 0.9
ADAM_B2 = 0.999
ADAM_EPS = 1e-08
ADAM_WD = 0.01
ADAM_STEP = 10
N_CHIPS = 4

VMEM_LIMIT = 56 * 1024 * 1024
LANE = 128


def _call(body, after=(), **kw):
    if after:
        n = len(after)

        def shifted(*refs):
            return body(*refs[n:])

        call = _call(shifted, **dict(kw, in_specs=[ANY] * n + list(kw["in_specs"])))
        return lambda *args: call(*after, *args)
    return pl.pallas_call(body, **kw)


def _cparams(sem):
    return pltpu.CompilerParams(dimension_semantics=sem, vmem_limit_bytes=VMEM_LIMIT)


def _rup(n, m):
    return (n + m - 1) // m * m


def _pick(n, pref):
    if n <= pref:
        return n
    for step in (128, 16, 8):
        for t in range(pref - pref % step, 0, -step):
            if n % t == 0:
                return t
    return n


def _tile2d(rows, cols, pref_rows=256, budget=256 * 1024):
    tr = _pick(rows, pref_rows)
    if tr < 64:
        tr = rows
    if cols % LANE:
        return tr, cols
    return tr, _pick(cols, max(LANE, budget // tr // LANE * LANE))


def _mm(a, b, mode, *, name, tm=512, tn=1024, tk=2048, bmode=None, out_dtype=F32, after=(), b_map=None,
        o_map=None):
    a3 = a if a.ndim == 3 else a[None]
    b3 = b if b.ndim == 3 else b[None]
    if mode == 'tn':
        K, M = a3.shape[1:]
    else:
        M, K = a3.shape[1:]
    N = b3.shape[1] if mode == 'nt' else b3.shape[2]
    tm, tn, tk = _pick(M, tm), _pick(N, tn), _pick(K, tk)
    nb = max(a3.shape[0], b3.shape[0])
    nbo, nbr = (1, nb) if bmode == 'abr' else (nb, 1)
    nk = K // tk
    nred = nbr * nk
    a_b = bmode in ('ao', 'abr')
    b_b = bmode in ('bo', 'abr')
    o_b = bmode in ('bo', 'ao')

    def bsel(flag, bo, br, remap=None):
        if not flag:
            return 0
        return (bo + br) if remap is None else remap(bo + br)

    if mode == 'tn':
        a_spec = pl.BlockSpec((None, tk, tm), lambda bo, i, j, br, k: (bsel(a_b, bo, br), k, i))
    else:
        a_spec = pl.BlockSpec((None, tm, tk), lambda bo, i, j, br, k: (bsel(a_b, bo, br), i, k))
    if mode == 'nt':
        b_spec = pl.BlockSpec((None, tn, tk), lambda bo, i, j, br, k: (bsel(b_b, bo, br, b_map), j, k))
    else:
        b_spec = pl.BlockSpec((None, tk, tn), lambda bo, i, j, br, k: (bsel(b_b, bo, br, b_map), k, j))
    o_spec = pl.BlockSpec((None, tm, tn), lambda bo, i, j, br, k: (bsel(o_b, bo, br, o_map), i, j))
    dn = {'nn': (((1,), (0,)), ((), ())), 'nt': (((1,), (1,)), ((), ())), 'tn': (((0,), (0,)), ((), ()))}[mode]

    def body(a_ref, b_ref, *rest):
        o_ref, scratch = rest[len(after)], rest[len(after) + 1:]
        r = lax.dot_general(a_ref[...].astype(BF16), b_ref[...].astype(BF16), dn, preferred_element_type=F32)
        if nred == 1:
            o_ref[...] = r.astype(out_dtype)
        else:
            acc = scratch[0]
            step = pl.program_id(3) * nk + pl.program_id(4)

            @pl.when(step == 0)
            def _():
                acc[...] = r

            @pl.when(step > 0)
            def _():
                acc[...] += r

            @pl.when(step == nred - 1)
            def _():
                o_ref[...] = acc[...].astype(out_dtype)

    out = _call(
        body, name=name,
        grid=(nbo, M // tm, N // tn, nbr, nk),
        in_specs=[a_spec, b_spec] + [ANY] * len(after), out_specs=o_spec,
        out_shape=jax.ShapeDtypeStruct((nbo if o_b else 1, M, N), out_dtype),
        scratch_shapes=[] if nred == 1 else [pltpu.VMEM((tm, tn), F32)],
        compiler_params=_cparams(("parallel", "parallel", "parallel", "arbitrary", "arbitrary")),
    )(a3, b3, *after)
    return out if o_b else out[0]


def _add_ln_fwd(x, r, g, b, *, name):
    S, D = x.shape
    tr = _pick(S, 256)

    def body(x_ref, r_ref, g_ref, b_ref, o_ref, xh_ref, rs_ref):
        z = ALPHA * x_ref[...] + r_ref[...]
        mu = jnp.mean(z, axis=-1, keepdims=True)
        zc = z - mu
        var = jnp.mean(zc * zc, axis=-1, keepdims=True)
        rstd = lax.rsqrt(var + LN_EPS)
        xh = zc * rstd
        xh_ref[...] = xh
        rs_ref[...] = rstd
        o_ref[...] = xh * g_ref[...] + b_ref[...]

    row = pl.BlockSpec((tr, D), lambda i: (i, 0))
    vec = pl.BlockSpec((1, D), lambda i: (0, 0))
    return _call(
        body, name=name, grid=(S // tr,),
        in_specs=[row, row, vec, vec],
        out_specs=[row, row, pl.BlockSpec((tr, 1), lambda i: (i, 0))],
        out_shape=[jax.ShapeDtypeStruct((S, D), F32), jax.ShapeDtypeStruct((S, D), F32),
                   jax.ShapeDtypeStruct((S, 1), F32)],
        compiler_params=_cparams(("parallel",)),
    )(x, r, g.reshape(1, D), b.reshape(1, D))


def _ln_bwd(da, db, xhat, rstd, g, *, name, after=()):
    S, D = xhat.shape
    tr = _pick(S, 256)
    two = db is not None

    def body(*refs):
        refs = refs[len(after):]
        if two:
            da_ref, db_ref, xh_ref, rs_ref, g_ref, dz_ref, dg_ref, dbt_ref = refs
            dy = ALPHA * da_ref[...] + db_ref[...]
        else:
            da_ref, xh_ref, rs_ref, g_ref, dz_ref, dg_ref, dbt_ref = refs
            dy = da_ref[...]
        xh = xh_ref[...]
        dxh = dy * g_ref[...]
        m1 = jnp.mean(dxh, axis=-1, keepdims=True)
        m2 = jnp.mean(dxh * xh, axis=-1, keepdims=True)
        dz_ref[...] = rs_ref[...] * (dxh - m1 - xh * m2)
        pg = jnp.sum(dy * xh, axis=0, keepdims=True)
        pb = jnp.sum(dy, axis=0, keepdims=True)

        @pl.when(pl.program_id(0) == 0)
        def _():
            dg_ref[...] = pg
            dbt_ref[...] = pb

        @pl.when(pl.program_id(0) > 0)
        def _():
            dg_ref[...] += pg
            dbt_ref[...] += pb

    row = pl.BlockSpec((tr, D), lambda i: (i, 0))
    vec = pl.BlockSpec((1, D), lambda i: (0, 0))
    ins = list(after) + [da] + ([db] if two else []) + [xhat, rstd, g.reshape(1, D)]
    in_specs = [ANY] * len(after) + [row] + ([row] if two else []) + [row, pl.BlockSpec((tr, 1), lambda i: (i, 0)), vec]
    return _call(
        body, name=name, grid=(S // tr,),
        in_specs=in_specs, out_specs=[row, vec, vec],
        out_shape=[jax.ShapeDtypeStruct((S, D), F32), jax.ShapeDtypeStruct((1, D), F32),
                   jax.ShapeDtypeStruct((1, D), F32)],
        compiler_params=_cparams(("arbitrary",)),
    )(*ins)


def _loss_grad(y, t, *, name):
    S, D = y.shape
    tr = _pick(S, 256)

    def body(y_ref, t_ref, dy_ref, l_ref):
        e = y_ref[...] - t_ref[...]
        dy_ref[...] = e * (1.0 / D)
        part = 0.5 * jnp.sum(jnp.sum(e * e, axis=-1, keepdims=True) * (1.0 / D), axis=0, keepdims=True)

        @pl.when(pl.program_id(0) == 0)
        def _():
            l_ref[...] = part

        @pl.when(pl.program_id(0) > 0)
        def _():
            l_ref[...] += part

    row = pl.BlockSpec((tr, D), lambda i: (i, 0))
    return _call(
        body, name=name, grid=(S // tr,), in_specs=[row, row],
        out_specs=[row, pl.BlockSpec((1, 1), lambda i: (0, 0))],
        out_shape=[jax.ShapeDtypeStruct((S, D), F32), jax.ShapeDtypeStruct((1, 1), F32)],
        compiler_params=_cparams(("arbitrary",)),
    )(y, t)


def _combine(terms, scales, *, name, out_dtype=F32):
    S, D = terms[0].shape
    tr = _pick(S, 256)
    n = len(terms)

    def body(*refs):
        acc = scales[0] * refs[0][...].astype(F32)
        for i in range(1, n):
            acc = acc + scales[i] * refs[i][...].astype(F32)
        refs[n][...] = acc.astype(out_dtype)

    row = pl.BlockSpec((tr, D), lambda i: (i, 0))
    return _call(
        body, name=name, grid=(S // tr,), in_specs=[row] * n, out_specs=row,
        out_shape=jax.ShapeDtypeStruct((S, D), out_dtype),
        compiler_params=_cparams(("parallel",)),
    )(*terms)


def _split3(x):
    h = x.astype(BF16)
    r = x - h.astype(F32)
    m = r.astype(BF16)
    l = (r - m.astype(F32)).astype(BF16)
    return h, m, l


def _tri_matmul(tri_bf, x):
    h, m, l = _split3(x)
    dn = (((1,), (0,)), ((), ()))
    return (lax.dot_general(tri_bf, l, dn, preferred_element_type=F32)
            + lax.dot_general(tri_bf, m, dn, preferred_element_type=F32)
            + lax.dot_general(tri_bf, h, dn, preferred_element_type=F32))


def _gate_fwd(fl, bf, *, name):
    S = fl.shape[0]
    tc = _pick(S, 256)
    nchunk = S // tc

    def body(fl_ref, bf_ref, c_ref, sg_ref):
        r = lax.broadcasted_iota(jnp.int32, (tc, tc), 0)
        cidx = lax.broadcasted_iota(jnp.int32, (tc, tc), 1)
        tri = (r >= cidx).astype(BF16)
        carry = jnp.zeros((1, LANE), F32)
        for ch in range(nchunk):
            x = fl_ref[pl.ds(ch * tc, tc), :] + bf_ref[...]
            lf = jnp.minimum(x, 0.0) - jnp.log(1.0 + jnp.exp(-jnp.abs(x)))
            sg_ref[pl.ds(ch * tc, tc), :] = jax.nn.sigmoid(-x)
            c_ref[pl.ds(ch * tc, tc), :] = _tri_matmul(tri, lf) + carry
            carry = carry + jnp.sum(lf, axis=0, keepdims=True)

    full = pl.BlockSpec((S, LANE), lambda: (0, 0))
    return _call(
        body, name=name, in_specs=[full, pl.BlockSpec((1, LANE), lambda: (0, 0))], out_specs=[full, full],
        out_shape=[jax.ShapeDtypeStruct((S, LANE), F32)] * 2,
        compiler_params=pltpu.CompilerParams(vmem_limit_bytes=VMEM_LIMIT),
    )(fl, bf)


def _gate_bwd(dc, sg, *, name):
    S = dc.shape[0]
    tc = _pick(S, 256)
    nchunk = S // tc

    def body(dc_ref, sg_ref, dfl_ref, db_ref):
        r = lax.broadcasted_iota(jnp.int32, (tc, tc), 0)
        cidx = lax.broadcasted_iota(jnp.int32, (tc, tc), 1)
        tri = (r <= cidx).astype(BF16)
        carry = jnp.zeros((1, LANE), F32)
        dbacc = jnp.zeros((1, LANE), F32)
        for ch in reversed(range(nchunk)):
            d = dc_ref[pl.ds(ch * tc, tc), :]
            dfl = (_tri_matmul(tri, d) + carry) * sg_ref[pl.ds(ch * tc, tc), :]
            dfl_ref[pl.ds(ch * tc, tc), :] = dfl
            dbacc = dbacc + jnp.sum(dfl, axis=0, keepdims=True)
            carry = carry + jnp.sum(d, axis=0, keepdims=True)
        db_ref[...] = dbacc

    full = pl.BlockSpec((S, LANE), lambda: (0, 0))
    return _call(
        body, name=name, in_specs=[full, full], out_specs=[full, pl.BlockSpec((1, LANE), lambda: (0, 0))],
        out_shape=[jax.ShapeDtypeStruct((S, LANE), F32), jax.ShapeDtypeStruct((1, LANE), F32)],
        compiler_params=pltpu.CompilerParams(vmem_limit_bytes=VMEM_LIMIT),
    )(dc, sg)


def _fox_scores(q_ref, k_ref, cc_ref, cr_ref, qi, tq, S):
    scale = 1.0 / math.sqrt(FOX_HEAD_DIM)
    s = lax.dot_general(q_ref[...].astype(BF16), k_ref[...].astype(BF16), (((1,), (1,)), ((), ())),
                        preferred_element_type=F32) * scale
    s = s + cc_ref[...] - cr_ref[...]
    row = lax.broadcasted_iota(jnp.int32, (tq, S), 0) + qi * tq
    col = lax.broadcasted_iota(jnp.int32, (tq, S), 1)
    return s, row >= col


def _fox_fwd(P, ccol, crow, *, name):
    S = P.shape[0]
    tq = _pick(S, 256)
    H = FOX_HEADS

    def body(q_ref, k_ref, v_ref, cc_ref, cr_ref, o_ref, l_ref):
        s, causal = _fox_scores(q_ref, k_ref, cc_ref, cr_ref, pl.program_id(1), tq, S)
        s = jnp.where(causal, s, -1e30)
        m = jnp.max(s, axis=-1, keepdims=True)
        e = jnp.exp(s - m)
        den = jnp.sum(e, axis=-1, keepdims=True)
        p = e / den
        o_ref[...] = jnp.dot(p.astype(BF16), v_ref[...].astype(BF16), preferred_element_type=F32)
        l_ref[...] = m + jnp.log(den)

    return _call(
        body, name=name, grid=(H, S // tq),
        in_specs=[pl.BlockSpec((tq, 128), lambda h, i: (i, h)),
                  pl.BlockSpec((S, 128), lambda h, i: (0, H + h)),
                  pl.BlockSpec((S, 128), lambda h, i: (0, 2 * H + h)),
                  pl.BlockSpec((None, tq, 1), lambda h, i: (h, i, 0)),
                  pl.BlockSpec((None, 1, S), lambda h, i: (h, 0, 0))],
        out_specs=[pl.BlockSpec((tq, 128), lambda h, i: (i, h)),
                   pl.BlockSpec((None, tq, 1), lambda h, i: (h, i, 0))],
        out_shape=[jax.ShapeDtypeStruct((S, FOX_WIDTH), F32), jax.ShapeDtypeStruct((H, S, 1), F32)],
        compiler_params=_cparams(("parallel", "parallel")),
    )(P, P, P, ccol, crow)


def _fox_bwd(P, ccol, crow, o, lse, dcat, *, name):
    S = P.shape[0]
    tq = _pick(S, 256)
    H = FOX_HEADS
    nq = S // tq
    scale = 1.0 / math.sqrt(FOX_HEAD_DIM)

    def body(q_ref, k_ref, v_ref, cc_ref, cr_ref, o_ref, l_ref, do_ref,
             dq_ref, dk_ref, dv_ref, dcc_ref, dcr_ref, dk_acc, dv_acc):
        qi = pl.program_id(1)
        s, causal = _fox_scores(q_ref, k_ref, cc_ref, cr_ref, qi, tq, S)
        p = jnp.where(causal, jnp.exp(s - l_ref[...]), 0.0)
        do = do_ref[...]
        do_bf = do.astype(BF16)
        dp = lax.dot_general(do_bf, v_ref[...].astype(BF16), (((1,), (1,)), ((), ())), preferred_element_type=F32)
        delta = jnp.sum(do * o_ref[...], axis=-1, keepdims=True)
        ds = p * (dp - delta)
        ds_bf = ds.astype(BF16)
        dq_ref[...] = (jnp.dot(ds_bf, k_ref[...].astype(BF16), preferred_element_type=F32) * scale).astype(BF16)
        dkp = lax.dot_general(ds_bf, q_ref[...].astype(BF16), (((0,), (0,)), ((), ())),
                              preferred_element_type=F32) * scale
        dvp = lax.dot_general(p.astype(BF16), do_bf, (((0,), (0,)), ((), ())), preferred_element_type=F32)
        dcc_ref[...] = jnp.sum(ds, axis=-1, keepdims=True)
        dcr = jnp.sum(ds, axis=0, keepdims=True)

        @pl.when(qi == 0)
        def _():
            dk_acc[...] = dkp
            dv_acc[...] = dvp
            dcr_ref[...] = dcr

        @pl.when(qi > 0)
        def _():
            dk_acc[...] += dkp
            dv_acc[...] += dvp
            dcr_ref[...] += dcr

        @pl.when(qi == nq - 1)
        def _():
            dk_ref[...] = dk_acc[...].astype(BF16)
            dv_ref[...] = dv_acc[...].astype(BF16)

    qblk = pl.BlockSpec((tq, 128), lambda h, i: (i, h))
    kvo = pl.BlockSpec((S, 128), lambda h, i: (0, h))
    col = pl.BlockSpec((None, tq, 1), lambda h, i: (h, i, 0))
    rowv = pl.BlockSpec((None, 1, S), lambda h, i: (h, 0, 0))
    return _call(
        body, name=name, grid=(H, nq),
        in_specs=[qblk,
                  pl.BlockSpec((S, 128), lambda h, i: (0, H + h)),
                  pl.BlockSpec((S, 128), lambda h, i: (0, 2 * H + h)),
                  col, rowv, qblk, col, qblk],
        out_specs=[qblk, kvo, kvo, col, rowv],
        out_shape=[jax.ShapeDtypeStruct((S, FOX_WIDTH), BF16)] * 3
        + [jax.ShapeDtypeStruct((H, S, 1), F32), jax.ShapeDtypeStruct((H, 1, S), F32)],
        scratch_shapes=[pltpu.VMEM((S, 128), F32), pltpu.VMEM((S, 128), F32)],
        compiler_params=_cparams(("parallel", "arbitrary")),
    )(P, P, P, ccol, crow, o, lse, dcat)


def _s5_disc_fwd(lr, li, ls, *, name, after=()):
    G, Pn = lr.shape

    def body(lr_ref, li_ref, ls_ref, ar_ref, ai_ref, gr_ref, gi_ref):
        lr_, li_ = lr_ref[...], li_ref[...]
        dt = jnp.exp(ls_ref[...])
        mag = jnp.exp(lr_ * dt)
        th = li_ * dt
        ar = mag * jnp.cos(th)
        ai = mag * jnp.sin(th)
        den = lr_ * lr_ + li_ * li_
        xr = ar - 1.0
        ar_ref[...] = ar
        ai_ref[...] = ai
        gr_ref[...] = (xr * lr_ + ai * li_) / den
        gi_ref[...] = (ai * lr_ - xr * li_) / den

    sq = pl.BlockSpec((G, Pn), lambda: (0, 0))
    return _call(
        body, after=after, name=name, in_specs=[sq, sq, pl.BlockSpec((G, 1), lambda: (0, 0))], out_specs=[sq] * 4,
        out_shape=[jax.ShapeDtypeStruct((G, Pn), F32)] * 4,
    )(lr, li, ls)


def _s5_disc_bwd(lr, li, ls, dar, dai, dgr, dgi, *, name):
    G, Pn = lr.shape

    def body(lr_ref, li_ref, ls_ref, dar_ref, dai_ref, dgr_ref, dgi_ref, dlr_ref, dli_ref, dls_ref):
        lr_, li_ = lr_ref[...], li_ref[...]
        dt = jnp.exp(ls_ref[...])
        mag = jnp.exp(lr_ * dt)
        th = li_ * dt
        ar = mag * jnp.cos(th)
        ai = mag * jnp.sin(th)
        den = lr_ * lr_ + li_ * li_
        xr = ar - 1.0
        xi = ai
        g_re = (xr * lr_ + xi * li_) / den
        g_im = (xi * lr_ - xr * li_) / den
        dgr_, dgi_ = dgr_ref[...], dgi_ref[...]
        dxr = (dgr_ * lr_ - dgi_ * li_) / den
        dxi = (dgr_ * li_ + dgi_ * lr_) / den
        dden = -(dgr_ * g_re + dgi_ * g_im) / den
        dlr = (dgr_ * xr + dgi_ * xi) / den + 2.0 * dden * lr_
        dli = (dgr_ * xi - dgi_ * xr) / den + 2.0 * dden * li_
        da_r = dar_ref[...] + dxr
        da_i = dai_ref[...] + dxi
        dmag_mag = da_r * ar + da_i * ai
        dth = da_i * ar - da_r * ai
        dlr_ref[...] = dlr + dmag_mag * dt
        dli_ref[...] = dli + dth * dt
        ddt = jnp.sum(dmag_mag * lr_ + dth * li_, axis=-1, keepdims=True)
        dls_ref[...] = ddt * dt

    sq = pl.BlockSpec((G, Pn), lambda: (0, 0))
    c1 = pl.BlockSpec((G, 1), lambda: (0, 0))
    return _call(
        body, name=name, in_specs=[sq, sq, c1, sq, sq, sq, sq], out_specs=[sq, sq, c1],
        out_shape=[jax.ShapeDtypeStruct((G, Pn), F32)] * 2 + [jax.ShapeDtypeStruct((G, 1), F32)],
    )(lr, li, ls, dar, dai, dgr, dgi)


def _s5_bb_fwd(gr, gi, br, bi, *, name):
    R, C = br.shape

    def body(gr_ref, gi_ref, br_ref, bi_ref, or_ref, oi_ref):
        g_r, g_i, b_r, b_i = gr_ref[...], gi_ref[...], br_ref[...], bi_ref[...]
        or_ref[...] = g_r * b_r - g_i * b_i
        oi_ref[...] = g_r * b_i + g_i * b_r

    w = pl.BlockSpec((R, C), lambda: (0, 0))
    c1 = pl.BlockSpec((R, 1), lambda: (0, 0))
    return _call(body, name=name, in_specs=[c1, c1, w, w], out_specs=[w, w],
                 out_shape=[jax.ShapeDtypeStruct((R, C), F32)] * 2)(gr, gi, br, bi)


def _s5_bb_bwd(gr, gi, br, bi, dbbr, dbbi, *, name):
    R, C = br.shape

    def body(gr_ref, gi_ref, br_ref, bi_ref, dr_ref, di_ref, dbr_ref, dbi_ref, dgr_ref, dgi_ref):
        g_r, g_i, b_r, b_i = gr_ref[...], gi_ref[...], br_ref[...], bi_ref[...]
        d_r, d_i = dr_ref[...], di_ref[...]
        dbr_ref[...] = g_r * d_r + g_i * d_i
        dbi_ref[...] = g_r * d_i - g_i * d_r
        dgr_ref[...] = jnp.sum(d_r * b_r + d_i * b_i, axis=-1, keepdims=True)
        dgi_ref[...] = jnp.sum(d_i * b_r - d_r * b_i, axis=-1, keepdims=True)

    w = pl.BlockSpec((R, C), lambda: (0, 0))
    c1 = pl.BlockSpec((R, 1), lambda: (0, 0))
    return _call(body, name=name, in_specs=[c1, c1, w, w, w, w], out_specs=[w, w, c1, c1],
                 out_shape=[jax.ShapeDtypeStruct((R, C), F32)] * 2 + [jax.ShapeDtypeStruct((R, 1), F32)] * 2,
                 )(gr, gi, br, bi, dbbr, dbbi)


_DIAG_TILE = 8


def _diag_mask(gr, gc):
    rows, cols = _DIAG_TILE * gr, _DIAG_TILE * gc
    r = lax.broadcasted_iota(jnp.int32, (rows, cols), 0) >> (gr.bit_length() - 1)
    c = lax.broadcasted_iota(jnp.int32, (rows, cols), 1) >> (gc.bit_length() - 1)
    return r == c


def _diag_expand(t2, gr, gc, *, name, after=()):
    _, R, _ = t2.shape
    G = R // gr
    nt = G // _DIAG_TILE
    rows, cols = _DIAG_TILE * gr, _DIAG_TILE * gc

    def body(t_ref, o_ref):
        @pl.when(pl.program_id(1) == pl.program_id(2))
        def _():
            src = lax.broadcasted_iota(jnp.int32, (gc, cols), 0)
            dst = lax.broadcasted_iota(jnp.int32, (gc, cols), 1) & (gc - 1)
            spread = (src == dst).astype(BF16)
            y = jnp.dot(t_ref[...].astype(BF16), spread, preferred_element_type=F32)
            o_ref[...] = jnp.where(_diag_mask(gr, gc), y, 0.0).astype(BF16)

        @pl.when(pl.program_id(1) != pl.program_id(2))
        def _():
            o_ref[...] = jnp.zeros_like(o_ref)

    return _call(
        body, after=after, name=name, grid=(2, nt, nt),
        in_specs=[pl.BlockSpec((None, rows, gc), lambda p, i, j: (p, i, 0))],
        out_specs=pl.BlockSpec((None, rows, cols), lambda p, i, j: (p, i, j)),
        out_shape=jax.ShapeDtypeStruct((2, R, G * gc), BF16),
        compiler_params=_cparams(("parallel",) * 3),
    )(t2)


def _diag_extract(xd, gr, gc, *, name):
    _, R, _ = xd.shape
    nt = R // gr // _DIAG_TILE
    rows, cols = _DIAG_TILE * gr, _DIAG_TILE * gc

    def body(x_ref, o_ref):
        src = lax.broadcasted_iota(jnp.int32, (cols, gc), 0) & (gc - 1)
        dst = lax.broadcasted_iota(jnp.int32, (cols, gc), 1)
        fold = (src == dst).astype(BF16)
        parts = _split3(jnp.where(_diag_mask(gr, gc), x_ref[...], 0.0))
        acc = jnp.dot(parts[2], fold, preferred_element_type=F32)
        acc = acc + jnp.dot(parts[1], fold, preferred_element_type=F32)
        o_ref[...] = acc + jnp.dot(parts[0], fold, preferred_element_type=F32)

    return _call(
        body, name=name, grid=(2, nt),
        in_specs=[pl.BlockSpec((None, rows, cols), lambda p, i: (p, i, i))],
        out_specs=pl.BlockSpec((None, rows, gc), lambda p, i: (p, i, 0)),
        out_shape=jax.ShapeDtypeStruct((2, R, gc), F32),
        compiler_params=_cparams(("parallel",) * 2),
    )(xd)


SCAN_BLOCK = 8


def _cpowers(ar, ai, sign):
    ai = sign * ai
    out = [(ar, ai)]
    for _ in range(SCAN_BLOCK - 1):
        pr, pi = out[-1]
        out.append((pr * ar - pi * ai, pr * ai + pi * ar))
    return out


def _row_table(pw, row, index_of_row):
    tr_ = jnp.broadcast_to(pw[index_of_row(0)][0], row.shape)
    ti_ = jnp.broadcast_to(pw[index_of_row(0)][1], row.shape)
    for r in range(1, SCAN_BLOCK):
        pr, pi = pw[index_of_row(r)]
        tr_ = jnp.where(row == r, pr, tr_)
        ti_ = jnp.where(row == r, pi, ti_)
    return tr_, ti_


def _s5_scan_fwd(bu, a, *, name):
    _, S, N = bu.shape
    tc = 512
    nt = N // tc

    def body(a_ref, b_ref, h_ref):
        pw = _cpowers(a_ref[0], a_ref[1], 1.0)
        row = lax.broadcasted_iota(jnp.int32, (SCAN_BLOCK, tc), 0)
        lead_r, lead_i = _row_table(pw, row, lambda r: r)

        def step(k, carry):
            cr, ci = carry
            rows = pl.ds(pl.multiple_of(k * SCAN_BLOCK, SCAN_BLOCK), SCAN_BLOCK)
            xr, xi = b_ref[0, rows, :], b_ref[1, rows, :]
            for sh in (1, 2, 4):
                keep = row >= sh
                sr = jnp.where(keep, pltpu.roll(xr, sh, 0), 0.0)
                si = jnp.where(keep, pltpu.roll(xi, sh, 0), 0.0)
                kr, ki = pw[sh - 1]
                xr, xi = xr + kr * sr - ki * si, xi + kr * si + ki * sr
            h_ref[0, rows, :] = xr + lead_r * cr - lead_i * ci
            h_ref[1, rows, :] = xi + lead_r * ci + lead_i * cr
            last = row == SCAN_BLOCK - 1
            tr_ = jnp.sum(jnp.where(last, xr, 0.0), axis=0, keepdims=True)
            ti_ = jnp.sum(jnp.where(last, xi, 0.0), axis=0, keepdims=True)
            a8r, a8i = pw[SCAN_BLOCK - 1]
            return a8r * cr - a8i * ci + tr_, a8r * ci + a8i * cr + ti_

        z = jnp.zeros((1, tc), F32)
        lax.fori_loop(0, S // SCAN_BLOCK, step, (z, z), unroll=2)

    vec = pl.BlockSpec((2, 1, tc), lambda j: (0, 0, j))
    mat = pl.BlockSpec((2, S, tc), lambda j: (0, 0, j))
    return _call(
        body, name=name, grid=(nt,), in_specs=[vec, mat], out_specs=mat,
        out_shape=jax.ShapeDtypeStruct((2, S, N), F32),
        compiler_params=_cparams(("parallel",)),
    )(a, bu)


def _s5_scan_bwd(g, h, a, *, name):
    _, S, N = g.shape
    tc = 256
    nt = N // tc

    def body(a_ref, g_ref, h_ref, l_ref, da_ref):
        pw = _cpowers(a_ref[0], a_ref[1], -1.0)
        row = lax.broadcasted_iota(jnp.int32, (SCAN_BLOCK, tc), 0)
        tail_r, tail_i = _row_table(pw, row, lambda r: SCAN_BLOCK - 1 - r)
        nb = S // SCAN_BLOCK

        def step(i, carry):
            k = nb - 1 - i
            cr, ci, dar, dai = carry
            rows = pl.ds(pl.multiple_of(k * SCAN_BLOCK, SCAN_BLOCK), SCAN_BLOCK)
            xr, xi = g_ref[0, rows, :], g_ref[1, rows, :]
            for sh in (1, 2, 4):
                keep = row < SCAN_BLOCK - sh
                sr = jnp.where(keep, pltpu.roll(xr, SCAN_BLOCK - sh, 0), 0.0)
                si = jnp.where(keep, pltpu.roll(xi, SCAN_BLOCK - sh, 0), 0.0)
                kr, ki = pw[sh - 1]
                xr, xi = xr + kr * sr - ki * si, xi + kr * si + ki * sr
            lr = xr + tail_r * cr - tail_i * ci
            li = xi + tail_r * ci + tail_i * cr
            l_ref[0, rows, :] = lr
            l_ref[1, rows, :] = li
            prev = pl.ds(pl.multiple_of(jnp.maximum(k - 1, 0) * SCAN_BLOCK, SCAN_BLOCK), SCAN_BLOCK)
            has_prev = jnp.where(k > 0, 1.0, 0.0).astype(F32)
            first = row == 0
            hpr = jnp.where(first, pltpu.roll(h_ref[0, prev, :], 1, 0) * has_prev, pltpu.roll(h_ref[0, rows, :], 1, 0))
            hpi = jnp.where(first, pltpu.roll(h_ref[1, prev, :], 1, 0) * has_prev, pltpu.roll(h_ref[1, rows, :], 1, 0))
            tr_ = jnp.sum(jnp.where(first, xr, 0.0), axis=0, keepdims=True)
            ti_ = jnp.sum(jnp.where(first, xi, 0.0), axis=0, keepdims=True)
            a8r, a8i = pw[SCAN_BLOCK - 1]
            return (a8r * cr - a8i * ci + tr_, a8r * ci + a8i * cr + ti_,
                    dar + lr * hpr + li * hpi, dai + li * hpr - lr * hpi)

        z = jnp.zeros((1, tc), F32)
        z8 = jnp.zeros((SCAN_BLOCK, tc), F32)
        _, _, dar, dai = lax.fori_loop(0, nb, step, (z, z, z8, z8), unroll=2)
        da_ref[0] = jnp.sum(dar, axis=0, keepdims=True)
        da_ref[1] = jnp.sum(dai, axis=0, keepdims=True)

    vec = pl.BlockSpec((2, 1, tc), lambda j: (0, 0, j))
    mat = pl.BlockSpec((2, S, tc), lambda j: (0, 0, j))
    return _call(
        body, name=name, grid=(nt,), in_specs=[vec, mat, mat], out_specs=[mat, vec],
        out_shape=[jax.ShapeDtypeStruct((2, S, N), F32), jax.ShapeDtypeStruct((2, 1, N), F32)],
        compiler_params=_cparams(("parallel",)),
    )(a, g, h)


_GELU_C = math.sqrt(2.0 / math.pi)


def _s5_out_fwd(yc, P, dskip, *, name):
    S, W = yc.shape
    tr = _pick(S, 256)
    ub = 3 * FOX_WIDTH // W

    def body(yc_ref, u_ref, d_ref, y_ref, yg_ref):
        y = yc_ref[...] + d_ref[...] * u_ref[...]
        y_ref[...] = y
        t = jnp.tanh(_GELU_C * (y + 0.044715 * y * y * y))
        yg_ref[...] = (0.5 * y * (1.0 + t)).astype(BF16)

    row = pl.BlockSpec((tr, W), lambda i: (i, 0))
    return _call(
        body, name=name, grid=(S // tr,),
        in_specs=[row, pl.BlockSpec((tr, W), lambda i: (i, ub)), pl.BlockSpec((1, W), lambda i: (0, 0))],
        out_specs=[row, row],
        out_shape=[jax.ShapeDtypeStruct((S, W), F32), jax.ShapeDtypeStruct((S, W), BF16)],
        compiler_params=_cparams(("parallel",)),
    )(yc, P, dskip)


def _s5_out_bwd(dyg, y, P, dskip, *, name):
    S, W = y.shape
    tr = _pick(S, 256)
    ub = 3 * FOX_WIDTH // W

    def body(dyg_ref, y_ref, u_ref, d_ref, dy_ref, du_ref, dd_ref):
        y_ = y_ref[...]
        inner = _GELU_C * (y_ + 0.044715 * y_ * y_ * y_)
        t = jnp.tanh(inner)
        dgelu = 0.5 * (1.0 + t) + 0.5 * y_ * (1.0 - t * t) * _GELU_C * (1.0 + 3.0 * 0.044715 * y_ * y_)
        dy = dyg_ref[...] * dgelu
        dy_ref[...] = dy.astype(BF16)
        du_ref[...] = d_ref[...] * dy
        part = jnp.sum(dy * u_ref[...], axis=0, keepdims=True)

        @pl.when(pl.program_id(0) == 0)
        def _():
            dd_ref[...] = part

        @pl.when(pl.program_id(0) > 0)
        def _():
            dd_ref[...] += part

    row = pl.BlockSpec((tr, W), lambda i: (i, 0))
    vec = pl.BlockSpec((1, W), lambda i: (0, 0))
    return _call(
        body, name=name, grid=(S // tr,),
        in_specs=[row, row, pl.BlockSpec((tr, W), lambda i: (i, ub)), vec],
        out_specs=[row, row, vec],
        out_shape=[jax.ShapeDtypeStruct((S, W), BF16), jax.ShapeDtypeStruct((S, W), F32),
                   jax.ShapeDtypeStruct((1, W), F32)],
        compiler_params=_cparams(("arbitrary",)),
    )(dyg, y, P, dskip)


def _glu_fwd(z, *, name):
    S, W2 = z.shape
    W = W2 // 2
    tr = _pick(S, 256)

    def body(z1_ref, z2_ref, o_ref):
        o_ref[...] = (z1_ref[...] * jax.nn.sigmoid(z2_ref[...])).astype(BF16)

    return _call(
        body, name=name, grid=(S // tr,),
        in_specs=[pl.BlockSpec((tr, W), lambda i: (i, 0)), pl.BlockSpec((tr, W), lambda i: (i, 1))],
        out_specs=pl.BlockSpec((tr, W), lambda i: (i, 0)),
        out_shape=jax.ShapeDtypeStruct((S, W), BF16),
        compiler_params=_cparams(("parallel",)),
    )(z, z)


def _glu_bwd(z, dcat, *, name):
    S, W2 = z.shape
    W = W2 // 2
    tr = _pick(S, 256)

    def body(z1_ref, z2_ref, d_ref, dz1_ref, dz2_ref):
        sg = jax.nn.sigmoid(z2_ref[...])
        d = d_ref[...]
        dz1_ref[...] = (d * sg).astype(BF16)
        dz2_ref[...] = (d * z1_ref[...] * sg * (1.0 - sg)).astype(BF16)

    lo = pl.BlockSpec((tr, W), lambda i: (i, 0))
    hi = pl.BlockSpec((tr, W), lambda i: (i, 1))
    dz1, dz2 = _call(
        body, name=name, grid=(S // tr,), in_specs=[lo, hi, hi], out_specs=[lo, lo],
        out_shape=[jax.ShapeDtypeStruct((S, W), BF16)] * 2,
        compiler_params=_cparams(("parallel",)),
    )(z, z, dcat)
    return jnp.concatenate([dz1, dz2], axis=1)


ACT_ROWS = 16
ACT_COLS = 256


def _shift_down(cur, prev, k, row):
    return jnp.where(row >= k, pltpu.roll(cur, k, 0), pltpu.roll(prev, k, 0))


def _shift_up(cur, nxt, k, row):
    n = cur.shape[0]
    return jnp.where(row < n - k, pltpu.roll(cur, n - k, 0), pltpu.roll(nxt, n - k, 0))


def _act_fwd(h, cw, cb, *, name):
    _, S, FP = h.shape
    tr = _pick(S, 256)
    hb = tr // ACT_ROWS
    nq = tr // ACT_ROWS

    def body(g_ref, gh_ref, v_ref, vh_ref, wg_ref, wv_ref, bg_ref, bv_ref, a_ref):
        first = pl.program_id(1) == 0
        for c0 in range(0, FP, ACT_COLS):
            cw_ = min(ACT_COLS, FP - c0)
            cols = pl.ds(c0, cw_)
            rw = lax.broadcasted_iota(jnp.int32, (ACT_ROWS, cw_), 0)
            wg = [wg_ref[pl.ds(k, 1), cols] for k in range(3)]
            wv = [wv_ref[pl.ds(k, 1), cols] for k in range(3)]
            bg, bv = bg_ref[:, cols], bv_ref[:, cols]
            halo_g = jnp.where(first, 0.0, gh_ref[:, cols])
            halo_v = jnp.where(first, 0.0, vh_ref[:, cols])

            def chunk(q, _):
                rows = pl.ds(pl.multiple_of(q * ACT_ROWS, ACT_ROWS), ACT_ROWS)
                before = pl.ds(pl.multiple_of(jnp.maximum(q - 1, 0) * ACT_ROWS, ACT_ROWS), ACT_ROWS)
                g, v = g_ref[rows, cols], v_ref[rows, cols]
                gp = jnp.where(q > 0, g_ref[before, cols], halo_g)
                vp = jnp.where(q > 0, v_ref[before, cols], halo_v)
                cg = bg + wg[2] * g + wg[1] * _shift_down(g, gp, 1, rw) + wg[0] * _shift_down(g, gp, 2, rw)
                cv = bv + wv[2] * v + wv[1] * _shift_down(v, vp, 1, rw) + wv[0] * _shift_down(v, vp, 2, rw)
                a_ref[rows, cols] = (cg * jax.nn.sigmoid(cg) * cv).astype(BF16)
                return 0

            lax.fori_loop(0, nq, chunk, 0)

    def main(off):
        return pl.BlockSpec((None, tr, FP), lambda j, i: (j + off, i, 0))

    def halo(off):
        return pl.BlockSpec((None, ACT_ROWS, FP), lambda j, i: (j + off, jnp.maximum(i * hb - 1, 0), 0))

    def wspec(off):
        return pl.BlockSpec((None, 3, FP), lambda j, i: (j + off, 0, 0))

    def bspec(off):
        return pl.BlockSpec((None, 1, FP), lambda j, i: (j + off, 0, 0))

    cb3 = cb.reshape(4, 1, FP)
    return _call(
        body, name=name, grid=(2, S // tr),
        in_specs=[main(0), halo(0), main(2), halo(2), wspec(0), wspec(2), bspec(0), bspec(2)],
        out_specs=pl.BlockSpec((None, tr, FP), lambda j, i: (j, i, 0)),
        out_shape=jax.ShapeDtypeStruct((2, S, FP), BF16),
        compiler_params=_cparams(("parallel", "parallel")),
    )(h, h, h, h, cw, cw, cb3, cb3)


def _act_bwd(h, da, cw, cb, *, name):
    _, S, FP = h.shape
    tr = _pick(S, 256)
    hb = tr // ACT_ROWS
    nq = tr // ACT_ROWS
    nr = S // tr
    half = ACT_ROWS // 2

    def fold(x):
        return x[:half] + x[half:]

    def body(g_ref, gp_ref, v_ref, vp_ref, da_ref, wg_ref, wv_ref, bg_ref, bv_ref,
             dh_ref, dwg_ref, dwv_ref, dbg_ref, dbv_ref, carry_g, carry_v):
        i = pl.program_id(1)
        bottom = i == 0
        top = i == nr - 1
        for c0 in range(0, FP, ACT_COLS):
            cw_ = min(ACT_COLS, FP - c0)
            cols = pl.ds(c0, cw_)
            rw = lax.broadcasted_iota(jnp.int32, (ACT_ROWS, cw_), 0)
            wg = [wg_ref[pl.ds(k, 1), cols] for k in range(3)]
            wv = [wv_ref[pl.ds(k, 1), cols] for k in range(3)]
            bg, bv = bg_ref[:, cols], bv_ref[:, cols]
            halo_g = jnp.where(top, 0.0, gp_ref[:, cols])
            halo_v = jnp.where(top, 0.0, vp_ref[:, cols])
            after_g = jnp.where(bottom, 0.0, carry_g[:, cols])
            after_v = jnp.where(bottom, 0.0, carry_v[:, cols])

            def chunk(s, carry):
                ng, nv, acc = carry[0], carry[1], carry[2:]
                q = nq - 1 - s
                rows = pl.ds(pl.multiple_of(q * ACT_ROWS, ACT_ROWS), ACT_ROWS)
                before = pl.ds(pl.multiple_of(jnp.maximum(q - 1, 0) * ACT_ROWS, ACT_ROWS), ACT_ROWS)
                g, v = g_ref[rows, cols], v_ref[rows, cols]
                gp = jnp.where(q > 0, g_ref[before, cols], halo_g)
                vp = jnp.where(q > 0, v_ref[before, cols], halo_v)
                g1, g2 = _shift_down(g, gp, 1, rw), _shift_down(g, gp, 2, rw)
                v1, v2 = _shift_down(v, vp, 1, rw), _shift_down(v, vp, 2, rw)
                cg = bg + wg[2] * g + wg[1] * g1 + wg[0] * g2
                cv = bv + wv[2] * v + wv[1] * v1 + wv[0] * v2
                sg = jax.nn.sigmoid(cg)
                d = da_ref[rows, cols]
                dcg = d * cv * sg * (1.0 + cg * (1.0 - sg))
                dcv = d * cg * sg
                dh_ref[0, rows, cols] = (wg[2] * dcg + wg[1] * _shift_up(dcg, ng, 1, rw)
                                         + wg[0] * _shift_up(dcg, ng, 2, rw)).astype(BF16)
                dh_ref[1, rows, cols] = (wv[2] * dcv + wv[1] * _shift_up(dcv, nv, 1, rw)
                                         + wv[0] * _shift_up(dcv, nv, 2, rw)).astype(BF16)
                terms = (dcg * g2, dcg * g1, dcg * g, dcg, dcv * v2, dcv * v1, dcv * v, dcv)
                return (dcg, dcv) + tuple(a + fold(t) for a, t in zip(acc, terms))

            zero = jnp.zeros((half, cw_), F32)
            out = lax.fori_loop(0, nq, chunk, (after_g, after_v) + (zero,) * 8)
            carry_g[:, cols] = out[0]
            carry_v[:, cols] = out[1]
            sums = [jnp.sum(a, axis=0, keepdims=True) for a in out[2:]]

            @pl.when(bottom)
            def _():
                for k in range(3):
                    dwg_ref[pl.ds(k, 1), cols] = sums[k]
                    dwv_ref[pl.ds(k, 1), cols] = sums[4 + k]
                dbg_ref[:, cols] = sums[3]
                dbv_ref[:, cols] = sums[7]

            @pl.when(jnp.logical_not(bottom))
            def _():
                for k in range(3):
                    dwg_ref[pl.ds(k, 1), cols] += sums[k]
                    dwv_ref[pl.ds(k, 1), cols] += sums[4 + k]
                dbg_ref[:, cols] += sums[3]
                dbv_ref[:, cols] += sums[7]

    def main(off):
        return pl.BlockSpec((None, tr, FP), lambda j, i: (j + off, nr - 1 - i, 0))

    def prev(off):
        return pl.BlockSpec((None, ACT_ROWS, FP), lambda j, i: (j + off, jnp.maximum((nr - 1 - i) * hb - 1, 0), 0))

    def wspec(off):
        return pl.BlockSpec((None, 3, FP), lambda j, i: (j + off, 0, 0))

    def bspec(off):
        return pl.BlockSpec((None, 1, FP), lambda j, i: (j + off, 0, 0))

    cb3 = cb.reshape(4, 1, FP)
    dh, dwg, dwv, dbg, dbv = _call(
        body, name=name, grid=(2, nr),
        in_specs=[main(0), prev(0), main(2), prev(2), main(0), wspec(0), wspec(2), bspec(0), bspec(2)],
        out_specs=[pl.BlockSpec((None, 2, tr, FP), lambda j, i: (j, 0, nr - 1 - i, 0)),
                   wspec(0), wspec(0), bspec(0), bspec(0)],
        out_shape=[jax.ShapeDtypeStruct((2, 2, S, FP), BF16)]
        + [jax.ShapeDtypeStruct((2, 3, FP), F32)] * 2 + [jax.ShapeDtypeStruct((2, 1, FP), F32)] * 2,
        scratch_shapes=[pltpu.VMEM((ACT_ROWS, FP), F32), pltpu.VMEM((ACT_ROWS, FP), F32)],
        compiler_params=_cparams(("parallel", "arbitrary")),
    )(h, h, h, h, da, cw, cw, cb3, cb3)
    return (dh.reshape(4, S, FP), jnp.concatenate([dwg, dwv], axis=0), jnp.concatenate([dbg, dbv], axis=0))


def _rope_tables(posf, *, name, after=()):
    S = posf.shape[0]
    half = ROPE_DIM // 2
    d = np.arange(LANE) % SWA_HEAD_DIM
    invf = np.where(d < ROPE_DIM, ROPE_THETA ** (-(d % half).astype(np.float64) / half), 0.0).astype(np.float32)
    m_rot = (d < ROPE_DIM).astype(np.float32)
    m_a = (d < half).astype(np.float32)
    m_b = ((d >= half) & (d < ROPE_DIM)).astype(np.float32)
    consts = jnp.asarray(np.stack([invf, m_rot, m_a, m_b] + [np.zeros(LANE, np.float32)] * 4))

    def body(p_ref, k_ref, c_ref, sa_ref, sb_ref):
        k = k_ref[...]
        ang = p_ref[...] * k[0:1]
        co, si = jnp.cos(ang), jnp.sin(ang)
        c_ref[...] = k[1:2] * co + (1.0 - k[1:2])
        sa_ref[...] = -k[2:3] * si
        sb_ref[...] = k[3:4] * si

    full = pl.BlockSpec((S, LANE), lambda: (0, 0))
    return _call(
        body, after=after, name=name,
        in_specs=[pl.BlockSpec((S, 1), lambda: (0, 0)), pl.BlockSpec((8, LANE), lambda: (0, 0))],
        out_specs=[full] * 3, out_shape=[jax.ShapeDtypeStruct((S, LANE), F32)] * 3,
    )(posf, consts)


def _rope_apply(x, tabs, *, col0, width, inverse, name, out_dtype):
    S = x.shape[0]
    tr = _pick(S, 256)
    rep = width // LANE
    cb = col0 // width

    def body(x_ref, c_ref, sa_ref, sb_ref, o_ref):
        xv = x_ref[...].astype(F32)
        c = jnp.tile(c_ref[...], (1, rep))
        sa = jnp.tile(sa_ref[...], (1, rep))
        sb = jnp.tile(sb_ref[...], (1, rep))
        if not inverse:
            out = xv * c + pltpu.roll(xv, width - 8, 1) * sa + pltpu.roll(xv, 8, 1) * sb
        else:
            out = xv * c + pltpu.roll(xv * sa, 8, 1) + pltpu.roll(xv * sb, width - 8, 1)
        o_ref[...] = out.astype(out_dtype)

    tab = pl.BlockSpec((tr, LANE), lambda i: (i, 0))
    return _call(
        body, name=name, grid=(S // tr,),
        in_specs=[pl.BlockSpec((tr, width), lambda i: (i, cb)), tab, tab, tab],
        out_specs=pl.BlockSpec((tr, width), lambda i: (i, 0)),
        out_shape=jax.ShapeDtypeStruct((S, width), out_dtype),
        compiler_params=_cparams(("parallel",)),
    )(x, *tabs)


def _swa_mask(n):
    rows = SWA_GROUPS * SWA_WINDOW
    qi = lax.broadcasted_iota(jnp.int32, (rows, 2 * SWA_WINDOW), 0) & (SWA_WINDOW - 1)
    kj = lax.broadcasted_iota(jnp.int32, (rows, 2 * SWA_WINDOW), 1)
    rel = SWA_WINDOW + qi - kj
    return (rel >= 0) & (rel < SWA_WINDOW) & ((n > 0) | (kj >= SWA_WINDOW))


def _swa_fwd(qT, kT, vT, sink_rows, *, name):
    S = qT.shape[1]
    W, G, Dh = SWA_WINDOW, SWA_GROUPS, SWA_HEAD_DIM
    nb = S // W
    scale = 1.0 / math.sqrt(Dh)

    def body(q_ref, kp_ref, kc_ref, vp_ref, vc_ref, s_ref, o_ref, l_ref):
        n = pl.program_id(1)
        q = q_ref[...].reshape(G * W, Dh)
        kk = jnp.concatenate([kp_ref[...], kc_ref[...]], axis=0)
        vv = jnp.concatenate([vp_ref[...], vc_ref[...]], axis=0)
        s = lax.dot_general(q, kk, (((1,), (1,)), ((), ())), preferred_element_type=F32) * scale
        s = jnp.where(_swa_mask(n), s, -1e30)
        sink = s_ref[...]
        m = jnp.maximum(jnp.max(s, axis=-1, keepdims=True), sink)
        e = jnp.exp(s - m)
        den = jnp.sum(e, axis=-1, keepdims=True) + jnp.exp(sink - m)
        p = e / den
        o_ref[...] = jnp.dot(p.astype(BF16), vv, preferred_element_type=F32).reshape(G, W, Dh)
        l_ref[...] = (m + jnp.log(den)).reshape(G, W, 1)

    qs = pl.BlockSpec((G, W, Dh), lambda g, n: (g, n, 0))
    prev = pl.BlockSpec((None, W, Dh), lambda g, n: (g, jnp.maximum(n - 1, 0), 0))
    cur = pl.BlockSpec((None, W, Dh), lambda g, n: (g, n, 0))
    return _call(
        body, name=name, grid=(SWA_KV_HEADS, nb),
        in_specs=[qs, prev, cur, prev, cur, pl.BlockSpec((None, G * W, 1), lambda g, n: (g, 0, 0))],
        out_specs=[qs, pl.BlockSpec((G, W, 1), lambda g, n: (g, n, 0))],
        out_shape=[jax.ShapeDtypeStruct((SWA_HEADS, S, Dh), F32), jax.ShapeDtypeStruct((SWA_HEADS, S, 1), F32)],
        compiler_params=_cparams(("parallel", "parallel")),
    )(qT, kT, kT, vT, vT, sink_rows)


def _swa_bwd(qT, kT, vT, sink_rows, oT, L, doT, *, name):
    S = qT.shape[1]
    W, G, Dh = SWA_WINDOW, SWA_GROUPS, SWA_HEAD_DIM
    nb = S // W
    scale = 1.0 / math.sqrt(Dh)

    def body(q_ref, kp_ref, kc_ref, vp_ref, vc_ref, s_ref, o_ref, l_ref, do_ref,
             dq_ref, dk_ref, dv_ref, ds_ref):
        n = pl.program_id(1)
        q = q_ref[...].reshape(G * W, Dh)
        kk = jnp.concatenate([kp_ref[...], kc_ref[...]], axis=0)
        vv = jnp.concatenate([vp_ref[...], vc_ref[...]], axis=0)
        s = lax.dot_general(q, kk, (((1,), (1,)), ((), ())), preferred_element_type=F32) * scale
        lrow = l_ref[...].reshape(G * W, 1)
        p = jnp.where(_swa_mask(n), jnp.exp(s - lrow), 0.0)
        do = do_ref[...].reshape(G * W, Dh)
        do_bf = do.astype(BF16)
        dp = lax.dot_general(do_bf, vv, (((1,), (1,)), ((), ())), preferred_element_type=F32)
        delta = jnp.sum(do * o_ref[...].reshape(G * W, Dh), axis=-1, keepdims=True)
        dsc = p * (dp - delta)
        ds_bf = dsc.astype(BF16)
        dq_ref[...] = (jnp.dot(ds_bf, kk, preferred_element_type=F32) * scale).astype(BF16).reshape(G, W, Dh)
        dkk = lax.dot_general(ds_bf, q, (((0,), (0,)), ((), ())), preferred_element_type=F32) * scale
        dvv = lax.dot_general(p.astype(BF16), do_bf, (((0,), (0,)), ((), ())), preferred_element_type=F32)
        dsk = -jnp.exp(s_ref[...] - lrow) * delta
        dsk = jnp.broadcast_to(jnp.sum(dsk.reshape(G, W, 1), axis=1), (G, LANE))

        @pl.when(n == 0)
        def _():
            dk_ref[...] = jnp.zeros_like(dk_ref)
            dv_ref[...] = jnp.zeros_like(dv_ref)
            ds_ref[...] = jnp.zeros_like(ds_ref)

        rows = pl.ds(pl.multiple_of(n * W, W), 2 * W)
        dk_ref[rows, :] += dkk
        dv_ref[rows, :] += dvv
        ds_ref[...] += dsk

    qs = pl.BlockSpec((G, W, Dh), lambda g, n: (g, n, 0))
    prev = pl.BlockSpec((None, W, Dh), lambda g, n: (g, jnp.maximum(n - 1, 0), 0))
    cur = pl.BlockSpec((None, W, Dh), lambda g, n: (g, n, 0))
    lsp = pl.BlockSpec((G, W, 1), lambda g, n: (g, n, 0))
    kvo = pl.BlockSpec((None, S + W, Dh), lambda g, n: (g, 0, 0))
    return _call(
        body, name=name, grid=(SWA_KV_HEADS, nb),
        in_specs=[qs, prev, cur, prev, cur, pl.BlockSpec((None, G * W, 1), lambda g, n: (g, 0, 0)), qs, lsp, qs],
        out_specs=[qs, kvo, kvo, pl.BlockSpec((None, G, LANE), lambda g, n: (g, 0, 0))],
        out_shape=[jax.ShapeDtypeStruct((SWA_HEADS, S, Dh), BF16),
                   jax.ShapeDtypeStruct((SWA_KV_HEADS, S + W, Dh), F32),
                   jax.ShapeDtypeStruct((SWA_KV_HEADS, S + W, Dh), F32),
                   jax.ShapeDtypeStruct((SWA_KV_HEADS, G, LANE), F32)],
        compiler_params=_cparams(("parallel", "arbitrary")),
    )(qT, kT, kT, vT, vT, sink_rows, oT, L, doT)


def _adamw(w, g, m, v, *, name, tr=128, by_cols=False):
    L, R, C = w.shape
    split = isinstance(g, (list, tuple))
    HR, HC = _half_shape(R, C, by_cols) if split else (R, C)
    tr, tc = _tile2d(HR, HC, tr)
    nr, nc = HR // tr, HC // tc
    c1 = 1.0 / (1.0 - ADAM_B1 ** ADAM_STEP)
    c2 = 1.0 / (1.0 - ADAM_B2 ** ADAM_STEP)
    ng = 2 * L if split else 1

    def body(*refs):
        w_ref, g_refs, (m_ref, v_ref, go_ref, d_ref, mo_ref, vo_ref) = refs[0], refs[1:1 + ng], refs[1 + ng:]
        if split:
            mine = pl.program_id(1) == lax.axis_index("c")
            g_ = jnp.where(mine, g_refs[0][...], g_refs[1][...])
            for l in range(1, L):
                g_ = jnp.where(pl.program_id(0) == l,
                               jnp.where(mine, g_refs[2 * l][...], g_refs[2 * l + 1][...]), g_)
        else:
            g_ = g_refs[0][...]
        mn = ADAM_B1 * m_ref[...] + (1.0 - ADAM_B1) * g_
        vn = ADAM_B2 * v_ref[...] + (1.0 - ADAM_B2) * (g_ * g_)
        go_ref[...] = g_
        mo_ref[...] = mn
        vo_ref[...] = vn
        d_ref[...] = -ADAM_LR * ((mn * c1) / (jnp.sqrt(vn * c2) + ADAM_EPS) + ADAM_WD * w_ref[...])

    def whole(l, hf, i, j):
        return (l, i, hf * nc + j) if by_cols else (l, hf * nr + i, j)

    row = pl.BlockSpec((None, tr, tc), whole)
    half = pl.BlockSpec((tr, tc), lambda l, hf, i, j: (i, j))
    gs = [h for pair in g for h in pair] if split else [g]
    return _call(
        body, name=name, grid=(L, 2 if split else 1, nr, nc),
        in_specs=[row] + [half if split else row] * ng + [row, row],
        out_specs=[row] * 4, out_shape=[jax.ShapeDtypeStruct((L, R, C), F32)] * 4,
        compiler_params=_cparams(("parallel",) * 4),
    )(w, *gs, m, v)


def _sum2_halves(g4, s4, by_cols, *, name):
    n, R, C = g4.shape
    HR, HC = _half_shape(R, C, by_cols)
    tr, tc = _tile2d(HR, HC)
    nr, nc = HR // tr, HC // tc
    core = lax.axis_index("c").astype(jnp.int32).reshape(1)

    def body(c_ref, g_ref, s_ref, o_ref):
        o_ref[...] = (g_ref[...].astype(F32) + s_ref[...].astype(F32)).astype(BF16)

    def mine(k, i, j, c):
        return (k, i, c[0] * nc + j) if by_cols else (k, c[0] * nr + i, j)

    blk = pl.BlockSpec((None, tr, tc), lambda k, i, j, c: (k, i, j))
    return _call(
        body, name=name,
        grid_spec=pltpu.PrefetchScalarGridSpec(
            num_scalar_prefetch=1, grid=(n, nr, nc),
            in_specs=[pl.BlockSpec((None, tr, tc), mine), blk], out_specs=blk),
        out_shape=jax.ShapeDtypeStruct((n, HR, HC), BF16),
        compiler_params=_cparams(("parallel", "parallel", "parallel")),
    )(core, g4, s4)


def _rowsum(parts, *, name, out_dtype=F32):
    n, R, C = parts.shape
    tr, tc = _tile2d(R, C)

    def body(p_ref, o_ref):
        acc = p_ref[0].astype(F32)
        for i in range(1, n):
            acc = acc + p_ref[i].astype(F32)
        o_ref[...] = acc.astype(out_dtype)

    return _call(
        body, name=name, grid=(R // tr, C // tc),
        in_specs=[pl.BlockSpec((n, tr, tc), lambda i, j: (0, i, j))],
        out_specs=pl.BlockSpec((tr, tc), lambda i, j: (i, j)),
        out_shape=jax.ShapeDtypeStruct((R, C), out_dtype),
        compiler_params=_cparams(("parallel", "parallel")),
    )(parts)


def _where_am_i():
    x, y, c = lax.axis_index("x"), lax.axis_index("y"), lax.axis_index("c")
    chips = [(1 - x, y), (x, 1 - y), (1 - x, 1 - y)]
    return x, y, c, chips


def _half_idx(rows, cols, by_cols, which):
    if by_cols:
        hc = cols // 2
        return (slice(None), pl.ds(pl.multiple_of(which * hc, LANE), hc))
    hr = rows // 2
    return (pl.ds(pl.multiple_of(which * hr, 16), hr), slice(None))


def _half_shape(rows, cols, by_cols):
    return (rows, cols // 2) if by_cols else (rows // 2, cols)


def _all_gather_shards(shards, by_cols, *, name):
    n = len(shards)

    def body(*refs):
        ins, outs = refs[:n], refs[n:2 * n]
        send, recv = refs[2 * n:]
        x, y, c, chips = _where_am_i()
        me = 2 * x + y
        sibling = (x, y, 1 - c)

        def half(i, which):
            return _half_idx(*shards[i].shape, by_cols[i], which)

        def cp(i, k, src, dst, to):
            return pltpu.make_async_remote_copy(src_ref=src, dst_ref=dst, send_sem=send.at[i, k],
                                                recv_sem=recv.at[i, k], device_id=to, device_id_type=MESH)

        first = []
        for i in range(n):
            for k, (px, py) in enumerate(chips):
                d = cp(i, k, ins[i].at[half(i, c)], outs[i].at[(me,) + half(i, c)], (px, py, c))
                d.start()
                first.append(d)
        passed = []
        for i in range(n):
            for k, (px, py) in enumerate(chips):
                blk = outs[i].at[(2 * px + py,) + half(i, c)]
                cp(i, k, blk, blk, (px, py, c)).wait_recv()
                d = cp(i, 3 + k, blk, blk, sibling)
                d.start()
                passed.append(d)
        for i in range(n):
            for k, (px, py) in enumerate(chips):
                blk = outs[i].at[(2 * px + py,) + half(i, 1 - c)]
                cp(i, 3 + k, blk, blk, sibling).wait_recv()
        for d in first + passed:
            d.wait_send()

    got = _call(
        body, name=name, in_specs=[ANY] * n, out_specs=[ANY] * n,
        out_shape=[jax.ShapeDtypeStruct((N_CHIPS,) + s.shape, s.dtype) for s in shards],
        scratch_shapes=[pltpu.SemaphoreType.DMA((n, 6)), pltpu.SemaphoreType.DMA((n, 6))],
    )(*shards)
    me = 2 * lax.axis_index("x") + lax.axis_index("y")
    return [lax.dynamic_update_slice_in_dim(g, s[None], me, axis=0) for g, s in zip(got, shards)]


HBM_SPEC = pl.BlockSpec(memory_space=pltpu.HBM)
SEM_SPEC = pl.BlockSpec(memory_space=pltpu.SEMAPHORE)
DATAFLOW = pltpu.SideEffectType.DATAFLOW_SIDE_EFFECTING


def _chip_exchange_refs(kind, shards_shape, by_cols, src, land, i, chip_k, c, me):
    if kind == 'gather':
        half = _half_idx(*shards_shape, by_cols, c)
        return src.at[half], land.at[(me,) + half], land.at[(chip_k,) + half]
    return src.at[chip_k], land.at[me], land.at[chip_k]


def _chip_exchange_start(kind, srcs, by_cols, *, name, after=()):
    n = len(srcs)
    land_shapes = [((N_CHIPS,) + s.shape) if kind == 'gather' else s.shape for s in srcs]

    def body(*refs):
        src_refs, land_refs = refs[:n], refs[n:2 * n]
        send, recv = refs[2 * n + len(after)], refs[2 * n + len(after) + 1]
        token = refs[-1]
        x, y, c, chips = _where_am_i()
        me = 2 * x + y
        for i in range(n):
            for k, (px, py) in enumerate(chips):
                s, d, _ = _chip_exchange_refs(kind, srcs[i].shape, by_cols[i], src_refs[i], land_refs[i], i,
                                              2 * px + py, c, me)
                pltpu.make_async_remote_copy(src_ref=s, dst_ref=d, send_sem=send.at[3 * i + k],
                                             recv_sem=recv.at[3 * i + k], device_id=(px, py, c),
                                             device_id_type=MESH).start()
        token[...] = jnp.zeros_like(token)

    lands = [pltpu.with_memory_space_constraint(lax.empty(sh, s.dtype), pltpu.HBM) for sh, s in zip(land_shapes, srcs)]
    outs = _call(
        body, name=name,
        out_shape=(pltpu.SemaphoreType.DMA((3 * n,)), pltpu.SemaphoreType.DMA((3 * n,)),
                   *[pltpu.HBM(s.shape, s.dtype) for s in srcs],
                   *[pltpu.HBM(sh, s.dtype) for sh, s in zip(land_shapes, srcs)],
                   jax.ShapeDtypeStruct((8, LANE), F32)),
        in_specs=[HBM_SPEC] * (2 * n) + [ANY] * len(after),
        out_specs=(SEM_SPEC, SEM_SPEC, *([HBM_SPEC] * (2 * n)), pl.BlockSpec(memory_space=pltpu.VMEM)),
        input_output_aliases={j: 2 + j for j in range(2 * n)},
        compiler_params=pltpu.CompilerParams(has_side_effects=DATAFLOW),
    )(*[pltpu.with_memory_space_constraint(s, pltpu.HBM) for s in srcs], *lands, *after)
    return outs[0], outs[1], list(outs[2:2 + n]), list(outs[2 + n:2 + 2 * n]), outs[-1]


def _chip_exchange_wait(kind, send, recv, srcs, lands, by_cols, after, *, name):
    n = len(srcs)

    def body(*refs):
        src_refs, land_refs = refs[:n], refs[n:2 * n]
        send_r, recv_r = refs[2 * n], refs[2 * n + 1]
        x, y, c, chips = _where_am_i()
        me = 2 * x + y
        for i in range(n):
            for k, (px, py) in enumerate(chips):
                s, _, d = _chip_exchange_refs(kind, srcs[i].shape, by_cols[i], src_refs[i], land_refs[i], i,
                                              2 * px + py, c, me)
                cp = pltpu.make_async_remote_copy(src_ref=s, dst_ref=d, send_sem=send_r.at[3 * i + k],
                                                  recv_sem=recv_r.at[3 * i + k], device_id=(px, py, c),
                                                  device_id_type=MESH)
                cp.wait_send()
                cp.wait_recv()

    outs = _call(
        body, name=name,
        out_shape=(*[pltpu.HBM(s.shape, s.dtype) for s in srcs], *[pltpu.HBM(l.shape, l.dtype) for l in lands]),
        in_specs=[HBM_SPEC] * (2 * n) + [SEM_SPEC, SEM_SPEC] + [ANY] * len(after),
        out_specs=tuple([HBM_SPEC] * (2 * n)),
        input_output_aliases={j: j for j in range(2 * n)},
        compiler_params=pltpu.CompilerParams(has_side_effects=DATAFLOW),
    )(*srcs, *lands, send, recv, *after)
    return list(outs[n:])


def _sibling_pass_gathered(lands, shard_shapes, by_cols, *, name):
    n = len(lands)

    def body(*refs):
        outs = refs[n:2 * n]
        send, recv = refs[2 * n:]
        x, y, c, chips = _where_am_i()
        sibling = (x, y, 1 - c)
        cps = []
        for i in range(n):
            for k, (px, py) in enumerate(chips):
                blk = outs[i].at[(2 * px + py,) + _half_idx(*shard_shapes[i], by_cols[i], c)]
                d = pltpu.make_async_remote_copy(src_ref=blk, dst_ref=blk, send_sem=send.at[i, k],
                                                 recv_sem=recv.at[i, k], device_id=sibling, device_id_type=MESH)
                d.start()
                cps.append(d)
        for i in range(n):
            for k, (px, py) in enumerate(chips):
                blk = outs[i].at[(2 * px + py,) + _half_idx(*shard_shapes[i], by_cols[i], 1 - c)]
                pltpu.make_async_remote_copy(src_ref=blk, dst_ref=blk, send_sem=send.at[i, k], recv_sem=recv.at[i, k],
                                             device_id=sibling, device_id_type=MESH).wait_recv()
        for d in cps:
            d.wait_send()

    return _call(
        body, name=name, in_specs=[ANY] * n, out_specs=[ANY] * n,
        out_shape=[jax.ShapeDtypeStruct(l.shape, l.dtype) for l in lands],
        input_output_aliases={j: j for j in range(n)},
        scratch_shapes=[pltpu.SemaphoreType.DMA((n, 3)), pltpu.SemaphoreType.DMA((n, 3))],
    )(*lands)


def _own_slot(lands, owns):
    me = 2 * lax.axis_index("x") + lax.axis_index("y")
    return [lax.dynamic_update_slice_in_dim(g, s, me, axis=0) for g, s in zip(lands, owns)]


def _sibling_send_halves(grads, by_cols, *, name):
    n = len(grads)

    def body(*refs):
        ins, outs = refs[:n], refs[n:2 * n]
        send, recv = refs[2 * n:]
        x, y, c, _ = _where_am_i()
        sibling = (x, y, 1 - c)
        cps = []
        for i in range(n):
            src = ins[i].at[(slice(None),) + _half_idx(*grads[i].shape[1:], by_cols[i], 1 - c)]
            d = pltpu.make_async_remote_copy(src_ref=src, dst_ref=outs[i], send_sem=send.at[i],
                                             recv_sem=recv.at[i], device_id=sibling, device_id_type=MESH)
            d.start()
            cps.append(d)
        for d in cps:
            d.wait()

    return _call(
        body, name=name, in_specs=[ANY] * n, out_specs=[ANY] * n,
        out_shape=[jax.ShapeDtypeStruct((N_CHIPS,) + _half_shape(*g.shape[1:], bc), g.dtype)
                   for g, bc in zip(grads, by_cols)],
        scratch_shapes=[pltpu.SemaphoreType.DMA((n,)), pltpu.SemaphoreType.DMA((n,))],
    )(*grads)


def _scatter_to_chips(parts, *, name):
    n = len(parts)

    def body(*refs):
        ins, outs = refs[:n], refs[n:2 * n]
        send, recv = refs[2 * n:]
        x, y, c, chips = _where_am_i()
        me = 2 * x + y
        cps = []
        for i in range(n):
            for k, (px, py) in enumerate(chips):
                d = pltpu.make_async_remote_copy(
                    src_ref=ins[i].at[2 * px + py], dst_ref=outs[i].at[me], send_sem=send.at[i, k],
                    recv_sem=recv.at[i, k], device_id=(px, py, c), device_id_type=MESH)
                d.start()
                cps.append((d, i, k, px, py))
        for d, i, k, px, py in cps:
            blk = outs[i].at[2 * px + py]
            pltpu.make_async_remote_copy(src_ref=blk, dst_ref=blk, send_sem=send.at[i, k], recv_sem=recv.at[i, k],
                                         device_id=(px, py, c), device_id_type=MESH).wait_recv()
        for d, *_ in cps:
            d.wait_send()

    got = _call(
        body, name=name, in_specs=[ANY] * n, out_specs=[ANY] * n,
        out_shape=[jax.ShapeDtypeStruct(p.shape, p.dtype) for p in parts],
        scratch_shapes=[pltpu.SemaphoreType.DMA((n, 3)), pltpu.SemaphoreType.DMA((n, 3))],
    )(*parts)
    me = 2 * lax.axis_index("x") + lax.axis_index("y")
    return [lax.dynamic_update_slice_in_dim(g, lax.dynamic_slice_in_dim(p, me, 1, axis=0), me, axis=0)
            for g, p in zip(got, parts)]


def _sibling_join_halves(halves, *, name):
    n = len(halves)

    def body(*refs):
        ins, outs = refs[:n], refs[n:2 * n]
        send, recv = refs[2 * n:]
        x, y, c, _ = _where_am_i()
        sibling = (x, y, 1 - c)
        cps = []
        for i in range(n):
            d = pltpu.make_async_remote_copy(src_ref=ins[i], dst_ref=outs[i], send_sem=send.at[i],
                                             recv_sem=recv.at[i], device_id=sibling, device_id_type=MESH)
            d.start()
            cps.append(d)
        for d in cps:
            d.wait()

    return _call(
        body, name=name, in_specs=[ANY] * n, out_specs=[ANY] * n,
        out_shape=[jax.ShapeDtypeStruct(h.shape, h.dtype) for h in halves],
        scratch_shapes=[pltpu.SemaphoreType.DMA((n,)), pltpu.SemaphoreType.DMA((n,))],
    )(*halves)


def _all_reduce_small(v, *, name):
    R, C = v.shape

    def body(v_ref, o_ref, sib, slots, send, recv):
        x, y, c, chips = _where_am_i()
        me = 2 * x + y
        sibling = (x, y, 1 - c)
        d = pltpu.make_async_remote_copy(src_ref=v_ref, dst_ref=sib, send_sem=send.at[0], recv_sem=recv.at[0],
                                         device_id=sibling, device_id_type=MESH)
        d.start()
        d.wait()
        slots[me] = v_ref[...] + sib[...]
        cps = []
        for k, (px, py) in enumerate(chips):
            d = pltpu.make_async_remote_copy(src_ref=slots.at[me], dst_ref=slots.at[me], send_sem=send.at[1 + k],
                                             recv_sem=recv.at[1 + k], device_id=(px, py, c), device_id_type=MESH)
            d.start()
            cps.append(d)
        for k, (px, py) in enumerate(chips):
            blk = slots.at[2 * px + py]
            pltpu.make_async_remote_copy(src_ref=blk, dst_ref=blk, send_sem=send.at[1 + k], recv_sem=recv.at[1 + k],
                                         device_id=(px, py, c), device_id_type=MESH).wait_recv()
        for d in cps:
            d.wait_send()
        o_ref[...] = (slots[0] + slots[1]) + (slots[2] + slots[3])

    vm = pl.BlockSpec(memory_space=pltpu.VMEM)
    return _call(
        body, name=name, in_specs=[vm], out_specs=vm,
        out_shape=jax.ShapeDtypeStruct((R, C), F32),
        scratch_shapes=[pltpu.VMEM((R, C), F32), pltpu.VMEM((N_CHIPS, R, C), F32),
                        pltpu.SemaphoreType.DMA((4,)), pltpu.SemaphoreType.DMA((4,))],
        compiler_params=pltpu.CompilerParams(vmem_limit_bytes=VMEM_LIMIT),
    )(v)


def _cols_from_shards(g):
    return jnp.transpose(g, (1, 0, 2)).reshape(g.shape[1], -1)


def _shards_from_cols(w):
    R, C4 = w.shape
    return jnp.transpose(w.reshape(R, N_CHIPS, C4 // N_CHIPS), (1, 0, 2))


def _block_diag(t):
    G, a, b = t.shape
    eye = jnp.eye(G, dtype=t.dtype)
    return (t[:, :, None, :] * eye[:, None, :, None]).reshape(G * a, G * b)


def _diag_blocks(xm, G):
    a, b = xm.shape[0] // G, xm.shape[1] // G
    idx = jnp.arange(G)
    return xm.reshape(G, a, G, b)[idx, :, idx, :]


def _pack(arrs):
    flat = []
    for a in arrs:
        f = a.reshape(-1).astype(F32)
        flat.append(jnp.pad(f, (0, _rup(f.shape[0], LANE) - f.shape[0])))
    v = jnp.concatenate(flat)
    rows = _rup(v.shape[0] // LANE, 8)
    v = jnp.pad(v, (0, rows * LANE - v.shape[0]))
    return v.reshape(rows, LANE)


def _unpack(v, shapes):
    flat = v.reshape(-1)
    out, off = [], 0
    for s in shapes:
        n = int(np.prod(s))
        out.append(flat[off:off + n].reshape(s))
        off += _rup(n, LANE)
    return out


def _ffn_fwd(x, Wup, Wdn, cw, cb, tag):
    h = _mm(x, Wup, 'nt', bmode='bo', tm=512, tn=4096, name=f"ffn_up_{tag}")
    a = _act_fwd(h, cw, cb, name=f"ffn_act_{tag}")
    f = _mm(a, Wdn, 'nn', bmode='abr', tm=512, tn=1024, tk=4096, name=f"ffn_down_{tag}")
    return f, h, a


def _ffn_bwd(df, x, h, a, Wup, Wdn, cw, cb, tag):
    da = _mm(df, Wdn, 'nt', bmode='bo', tm=512, tn=4096, name=f"ffn_da_{tag}")
    dWdn = _mm(a, df, 'tn', bmode='ao', tm=4096, tn=512, name=f"ffn_dwdn_{tag}", out_dtype=BF16)
    dh, dcw, dcb = _act_bwd(h, da, cw, cb, name=f"ffn_actb_{tag}")

    def shard_of(k):
        return (k % 2) * 2 + k // 2

    dx = _mm(dh, Wup, 'nn', bmode='abr', tm=512, tn=1024, tk=4096, name=f"ffn_dx_{tag}", b_map=shard_of)
    dWup = _mm(dh, x, 'tn', bmode='ao', tm=4096, tn=512, name=f"ffn_dwup_{tag}", out_dtype=BF16,
               o_map=shard_of)
    return dx, dWup, dWdn, dcw, dcb


def kernel(x, positions, ev_w_in, ev_b_f, ev_lambda_re, ev_lambda_im, ev_log_step, ev_ssm_b_re, ev_ssm_b_im, ev_ssm_c_re, ev_ssm_c_im, ev_ssm_d, ev_w_glu, ev_w_out, od_w_in, od_sinks, od_w_out, ln_mix_g, ln_mix_b, ffn_w_up, ffn_conv_w, ffn_conv_b, ffn_w_down, ln_ffn_g, ln_ffn_b, loss_target, m_ev_w_in, m_ev_b_f, m_ev_lambda_re, m_ev_lambda_im, m_ev_log_step, m_ev_ssm_b_re, m_ev_ssm_b_im, m_ev_ssm_c_re, m_ev_ssm_c_im, m_ev_ssm_d, m_ev_w_glu, m_ev_w_out, m_od_w_in, m_od_sinks, m_od_w_out, m_ln_mix_g, m_ln_mix_b, m_ffn_w_up, m_ffn_conv_w, m_ffn_conv_b, m_ffn_w_down, m_ln_ffn_g, m_ln_ffn_b, v_ev_w_in, v_ev_b_f, v_ev_lambda_re, v_ev_lambda_im, v_ev_log_step, v_ev_ssm_b_re, v_ev_ssm_b_im, v_ev_ssm_c_re, v_ev_ssm_c_im, v_ev_ssm_d, v_ev_w_glu, v_ev_w_out, v_od_w_in, v_od_sinks, v_od_w_out, v_ln_mix_g, v_ln_mix_b, v_ffn_w_up, v_ffn_conv_w, v_ffn_conv_b, v_ffn_w_down, v_ln_ffn_g, v_ln_ffn_b):
    W = dict(ev_w_in=ev_w_in, ev_b_f=ev_b_f, ev_lambda_re=ev_lambda_re, ev_lambda_im=ev_lambda_im, ev_log_step=ev_log_step, ev_ssm_b_re=ev_ssm_b_re, ev_ssm_b_im=ev_ssm_b_im, ev_ssm_c_re=ev_ssm_c_re, ev_ssm_c_im=ev_ssm_c_im, ev_ssm_d=ev_ssm_d, ev_w_glu=ev_w_glu, ev_w_out=ev_w_out, od_w_in=od_w_in, od_sinks=od_sinks, od_w_out=od_w_out, ln_mix_g=ln_mix_g, ln_mix_b=ln_mix_b, ffn_w_up=ffn_w_up, ffn_conv_w=ffn_conv_w, ffn_conv_b=ffn_conv_b, ffn_w_down=ffn_w_down, ln_ffn_g=ln_ffn_g, ln_ffn_b=ln_ffn_b)
    Mo = dict(ev_w_in=m_ev_w_in, ev_b_f=m_ev_b_f, ev_lambda_re=m_ev_lambda_re, ev_lambda_im=m_ev_lambda_im, ev_log_step=m_ev_log_step, ev_ssm_b_re=m_ev_ssm_b_re, ev_ssm_b_im=m_ev_ssm_b_im, ev_ssm_c_re=m_ev_ssm_c_re, ev_ssm_c_im=m_ev_ssm_c_im, ev_ssm_d=m_ev_ssm_d, ev_w_glu=m_ev_w_glu, ev_w_out=m_ev_w_out, od_w_in=m_od_w_in, od_sinks=m_od_sinks, od_w_out=m_od_w_out, ln_mix_g=m_ln_mix_g, ln_mix_b=m_ln_mix_b, ffn_w_up=m_ffn_w_up, ffn_conv_w=m_ffn_conv_w, ffn_conv_b=m_ffn_conv_b, ffn_w_down=m_ffn_w_down, ln_ffn_g=m_ln_ffn_g, ln_ffn_b=m_ln_ffn_b)
    Vo = dict(ev_w_in=v_ev_w_in, ev_b_f=v_ev_b_f, ev_lambda_re=v_ev_lambda_re, ev_lambda_im=v_ev_lambda_im, ev_log_step=v_ev_log_step, ev_ssm_b_re=v_ev_ssm_b_re, ev_ssm_b_im=v_ev_ssm_b_im, ev_ssm_c_re=v_ev_ssm_c_re, ev_ssm_c_im=v_ev_ssm_c_im, ev_ssm_d=v_ev_ssm_d, ev_w_glu=v_ev_w_glu, ev_w_out=v_ev_w_out, od_w_in=v_od_w_in, od_sinks=v_od_sinks, od_w_out=v_od_w_out, ln_mix_g=v_ln_mix_g, ln_mix_b=v_ln_mix_b, ffn_w_up=v_ffn_w_up, ffn_conv_w=v_ffn_conv_w, ffn_conv_b=v_ffn_conv_b, ffn_w_down=v_ffn_w_down, ln_ffn_g=v_ln_ffn_g, ln_ffn_b=v_ln_ffn_b)
    names = list(W.keys())
    big = ['ev_w_in', 'ev_w_glu', 'ev_w_out', 'od_w_in', 'od_w_out', 'ffn_w_up', 'ffn_w_down']

    S, D = x.shape[1], x.shape[2]
    x0 = x.reshape(S, D)
    tgt = loss_target.reshape(S, D)
    G, Pn, Cg = SSM_GROUPS, SSM_STATE, SSM_GROUP
    Fs = ffn_w_up.shape[2]
    FP = Fs
    Rd = ffn_w_down.shape[1]
    EIN = N_CHIPS * ev_w_in.shape[2]

    def as2d(a):
        return a.reshape(-1, a.shape[-1])

    cwl = ffn_conv_w.reshape(-1)
    cw_rows = _rup(_rup(cwl.shape[0], LANE) // LANE, 32)
    cw_pad = jnp.pad(cwl, (0, cw_rows * LANE - cwl.shape[0])).reshape(cw_rows, LANE)
    transposed = ('ev_w_in', 'ffn_w_up')

    def view(n, a):
        return jnp.transpose(a, (0, 2, 1)) if n in transposed else a

    Wv = {n: view(n, W[n]) for n in big}
    big_e = [(n, l) for n in big for l in range(W[n].shape[0])]
    split_cols = {e: (Wv[e[0]].shape[1] // 2) % 16 != 0 for e in big_e}
    shard16 = {e: Wv[e[0]][e[1]].astype(BF16) for e in big_e}
    grp_now = [e for e in big_e if e[0].startswith('ev_')]
    grp_ffn0 = [('ffn_w_up', 0), ('ffn_w_down', 0)]
    grp_l1 = [('od_w_in', 0), ('od_w_out', 0), ('ffn_w_up', 1), ('ffn_w_down', 1)]
    src_now = [shard16[e] for e in grp_now]
    src_ffn0 = [shard16[e] for e in grp_ffn0] + [cw_pad]
    src_l1 = [shard16[e] for e in grp_l1]
    cols_now = [split_cols[e] for e in grp_now]
    cols_ffn0 = [split_cols[e] for e in grp_ffn0] + [False]
    cols_l1 = [split_cols[e] for e in grp_l1]
    ag_now = _chip_exchange_start('gather', src_now, cols_now, name="ag_l0_start")
    ag_ffn0 = _chip_exchange_start('gather', src_ffn0, cols_ffn0, name="ag_ffn0_start", after=[ag_now[4]])
    ag_l1 = _chip_exchange_start('gather', src_l1, cols_l1, name="ag_l1_start", after=[ag_ffn0[4]])
    started = [ag_l1[4]]

    def finish_gather(started, srcs, cols, after, tag):
        send, recv, thru, lands, _ = started
        lands = _chip_exchange_wait('gather', send, recv, thru, lands, cols, after, name=f"ag_{tag}_wait")
        lands = _sibling_pass_gathered(lands, [s.shape for s in srcs], cols, name=f"ag_{tag}_pass")
        return _own_slot(lands, [s[None] for s in srcs])

    lam_r, lam_i = ev_lambda_re[0], ev_lambda_im[0]
    lstep = ev_log_step[0].reshape(G, 1)
    a_re, a_im, g_re, g_im = _s5_disc_fwd(lam_r, lam_i, lstep, name="s5_disc", after=started)
    b_re2, b_im2 = ev_ssm_b_re[0].reshape(G * Pn, Cg), ev_ssm_b_im[0].reshape(G * Pn, Cg)
    g_re1, g_im1 = g_re.reshape(G * Pn, 1), g_im.reshape(G * Pn, 1)
    bb_re, bb_im = _s5_bb_fwd(g_re1, g_im1, b_re2, b_im2, name="s5_bb")
    bbt = jnp.stack([jnp.transpose(b.reshape(G, Pn, Cg), (0, 2, 1)).reshape(G * Cg, Pn) for b in (bb_re, bb_im)])
    BB = _diag_expand(bbt, Cg, Pn, name="s5_bb_dense")
    cct = jnp.stack([jnp.transpose(ev_ssm_c_re[0], (0, 2, 1)).reshape(G * Pn, Cg),
                     jnp.transpose(-ev_ssm_c_im[0], (0, 2, 1)).reshape(G * Pn, Cg)])
    CC = _diag_expand(cct, Pn, Cg, name="s5_cc_dense", after=started)
    a_cat = jnp.stack([a_re.reshape(1, G * Pn), a_im.reshape(1, G * Pn)])
    dskip = ev_ssm_d[0].reshape(1, SSM_WIDTH)
    tabs = _rope_tables(positions.reshape(S, 1).astype(F32), name="rope_tables", after=[BB, CC])

    gw = dict(zip(grp_now, finish_gather(ag_now, src_now, cols_now, [tabs[2]], "l0")))
    gw.update({n: gw[(n, 0)] for n in big if (n, 0) in gw and W[n].shape[0] == 1})
    w_in_t = gw['ev_w_in'].reshape(EIN, D)
    qkv_w = 3 * FOX_WIDTH
    WmainT = jnp.concatenate([w_in_t[:qkv_w], w_in_t[qkv_w + FOX_HEADS:]], axis=0)
    WfT = jnp.pad(w_in_t[qkv_w:qkv_w + FOX_HEADS], ((0, LANE - FOX_HEADS), (0, 0)))
    Wglu = _cols_from_shards(gw['ev_w_glu'])
    Wout_ev = gw['ev_w_out'].reshape(D, D)
    cbs = [ffn_conv_b[l].reshape(N_CHIPS, Fs) for l in range(DEPTH)]

    P = _mm(x0, WmainT, 'nt', name="ev_proj")
    fl = _mm(x0, WfT, 'nt', name="ev_proj_f")
    bf_pad = jnp.pad(ev_b_f.reshape(1, FOX_HEADS), ((0, 0), (0, LANE - FOX_HEADS)))
    cgate, sgate = _gate_fwd(fl, bf_pad, name="fox_gate")
    ccol = jnp.transpose(cgate[:, :FOX_HEADS]).reshape(FOX_HEADS, S, 1)
    crow = jnp.transpose(cgate[:, :FOX_HEADS]).reshape(FOX_HEADS, 1, S)
    fox, lse = _fox_fwd(P, ccol, crow, name="fox_fwd")
    u_s5 = P[:, qkv_w:]
    bu = _mm(u_s5, BB, 'nn', bmode='bo', name="s5_bu")
    hh = _s5_scan_fwd(bu, a_cat, name="s5_scan")
    yc = _mm(hh, CC, 'nn', bmode='abr', name="s5_y")
    y_s5, yg = _s5_out_fwd(yc, P, dskip, name="s5_out")
    z = _mm(yg, Wglu, 'nn', name="s5_glu_proj")
    ssm = _glu_fwd(z, name="s5_glu")
    cat = jnp.concatenate([fox.astype(BF16), ssm], axis=1)
    mix0 = _mm(cat, Wout_ev, 'nn', name="ev_out")
    x1, xh1, rs1 = _add_ln_fwd(x0, mix0, ln_mix_g[0], ln_mix_b[0], name="ln_mix0")
    got = finish_gather(ag_ffn0, src_ffn0, cols_ffn0, [x1], "ffn0")
    gw.update(zip(grp_ffn0, got[:-1]))
    cw_all = got[-1].reshape(N_CHIPS, -1)[:, :cwl.shape[0]].reshape(N_CHIPS, DEPTH, 3, Fs)
    cws = [cw_all[:, l] for l in range(DEPTH)]
    Wup = {0: gw[('ffn_w_up', 0)]}
    Wdn = {0: gw[('ffn_w_down', 0)].reshape(2, Fs, D)}
    f0, hf0, af0 = _ffn_fwd(x1, Wup[0], Wdn[0], cws[0], cbs[0], "l0")
    x2, xh2, rs2 = _add_ln_fwd(x1, f0, ln_ffn_g[0], ln_ffn_b[0], name="ln_ffn0")

    gw.update(zip(grp_l1, finish_gather(ag_l1, src_l1, cols_l1, [x2], "l1")))
    Wodin = _cols_from_shards(gw[('od_w_in', 0)])
    Wodout = gw[('od_w_out', 0)].reshape(D, D)
    Wup[1] = gw[('ffn_w_up', 1)]
    Wdn[1] = gw[('ffn_w_down', 1)].reshape(2, Fs, D)
    QW, KW = SWA_HEADS * SWA_HEAD_DIM, SWA_KV_HEADS * SWA_HEAD_DIM
    P1 = _mm(x2, Wodin, 'nn', name="od_proj")
    qr = _rope_apply(P1, tabs, col0=0, width=QW, inverse=False, name="rope_q", out_dtype=BF16)
    kr = _rope_apply(P1, tabs, col0=QW, width=KW, inverse=False, name="rope_k", out_dtype=BF16)

    def heads(a2, nh):
        return jnp.transpose(a2.reshape(S, nh, SWA_HEAD_DIM), (1, 0, 2))

    def unheads(a3):
        return jnp.transpose(a3, (1, 0, 2)).reshape(S, -1)

    qT, kT = heads(qr, SWA_HEADS), heads(kr, SWA_KV_HEADS)
    vT = heads(P1[:, QW + KW:].astype(BF16), SWA_KV_HEADS)
    sink_rows = jnp.broadcast_to(od_sinks[0].reshape(SWA_KV_HEADS, SWA_GROUPS, 1, 1),
                                 (SWA_KV_HEADS, SWA_GROUPS, SWA_WINDOW, 1)).reshape(SWA_KV_HEADS, -1, 1)
    oT, Lsw = _swa_fwd(qT, kT, vT, sink_rows, name="swa_fwd")
    o_sw = unheads(oT).astype(BF16)
    mix1 = _mm(o_sw, Wodout, 'nn', name="od_out")
    x3, xh3, rs3 = _add_ln_fwd(x2, mix1, ln_mix_g[1], ln_mix_b[1], name="ln_mix1")
    f1, hf1, af1 = _ffn_fwd(x3, Wup[1], Wdn[1], cws[1], cbs[1], "l1")
    x4, xh4, rs4 = _add_ln_fwd(x3, f1, ln_ffn_g[1], ln_ffn_b[1], name="ln_ffn1")
    dy, loss_part = _loss_grad(x4, tgt, name="loss")

    dz4, dg_ffn1, db_ffn1 = _ln_bwd(dy, None, xh4, rs4, ln_ffn_g[1], name="lnb_ffn1")
    dx3f, dWup1, dWdn1, dcw1, dcb1 = _ffn_bwd(dz4, x3, hf1, af1, Wup[1], Wdn[1], cws[1], cbs[1], "l1")
    dz3, dg_mix1, db_mix1 = _ln_bwd(dz4, dx3f, xh3, rs3, ln_mix_g[1], name="lnb_mix1")
    do_sw = _mm(dz3, Wodout, 'nt', name="od_out_dx")
    dWodout = _mm(o_sw, dz3, 'tn', name="od_out_dw", out_dtype=BF16)
    doT = heads(do_sw, SWA_HEADS)
    dqT, dkT, dvT, dsink = _swa_bwd(qT, kT, vT, sink_rows, oT, Lsw, doT, name="swa_bwd")
    dq1 = _rope_apply(unheads(dqT), tabs, col0=0, width=QW, inverse=True, name="rope_dq", out_dtype=BF16)
    dk1 = _rope_apply(unheads(dkT[:, SWA_WINDOW:]), tabs, col0=0, width=KW, inverse=True, name="rope_dk",
                      out_dtype=BF16)
    dP1 = jnp.concatenate([dq1, dk1, unheads(dvT[:, SWA_WINDOW:]).astype(BF16)], axis=1)
    dx2m = _mm(dP1, Wodin, 'nt', name="od_proj_dx")
    dWodin = _mm(x2, dP1, 'tn', name="od_proj_dw", out_dtype=BF16)

    def rs_begin(entries, grads, tag):
        cols = [split_cols[e] for e in entries]
        sib = _sibling_send_halves(grads, cols, name=f"rs_{tag}_sibling")
        return [_sum2_halves(g4, s4, bc, name=f"rs_sum2_{n}{l}")
                for (n, l), g4, s4, bc in zip(entries, grads, sib, cols)]

    def own_parts(parts):
        me = 2 * lax.axis_index("x") + lax.axis_index("y")
        return [lax.dynamic_slice_in_dim(p, me, 1, axis=0) for p in parts]

    part_l1 = rs_begin(grp_l1, [_shards_from_cols(dWodin), dWodout.reshape(N_CHIPS, D // N_CHIPS, D), dWup1,
                                dWdn1.reshape(N_CHIPS, Rd, D)], "l1")
    rs_l1 = _chip_exchange_start('scatter', part_l1, [False] * len(part_l1), name="rs_l1_start")

    dz2, dg_ffn0, db_ffn0 = _ln_bwd(dz3, dx2m, xh2, rs2, ln_ffn_g[0], name="lnb_ffn0", after=[rs_l1[4]])
    dx1f, dWup0, dWdn0, dcw0, dcb0 = _ffn_bwd(dz2, x1, hf0, af0, Wup[0], Wdn[0], cws[0], cbs[0], "l0")
    part_ffn0 = rs_begin(grp_ffn0, [dWup0, dWdn0.reshape(N_CHIPS, Rd, D)], "ffn0")
    rs_ffn0 = _chip_exchange_start('scatter', part_ffn0, [False] * len(part_ffn0), name="rs_ffn0_start")
    dz1, dg_mix0, db_mix0 = _ln_bwd(dz2, dx1f, xh1, rs1, ln_mix_g[0], name="lnb_mix0", after=[rs_ffn0[4]])
    dcat = _mm(dz1, Wout_ev, 'nt', name="ev_out_dx")
    dWout_ev = _mm(cat, dz1, 'tn', name="ev_out_dw", out_dtype=BF16)
    dz = _glu_bwd(z, dcat, name="s5_glu_bwd")
    dyg = _mm(dz, Wglu, 'nt', name="s5_glu_dx")
    dWglu = _mm(yg, dz, 'tn', name="s5_glu_dw", out_dtype=BF16)
    dy_s5, du_dir, dD = _s5_out_bwd(dyg, y_s5, P, dskip, name="s5_out_bwd")
    dhh = _mm(dy_s5, CC, 'nt', bmode='bo', name="s5_y_dx")
    dCC = _mm(hh, dy_s5, 'tn', bmode='ao', name="s5_y_dw")
    lam, da_s5 = _s5_scan_bwd(dhh, hh, a_cat, name="s5_scan_bwd")
    du_bu = _mm(lam, BB, 'nt', bmode='abr', name="s5_bu_dx")
    dBB = _mm(u_s5, lam, 'tn', bmode='bo', name="s5_bu_dw")
    du = _combine([du_dir, du_bu], [1.0, 1.0], name="s5_du", out_dtype=BF16)
    dq0, dk0, dv0, dccol, dcrow = _fox_bwd(P, ccol, crow, fox, lse, dcat, name="fox_bwd")
    dc = jnp.transpose((dccol.reshape(FOX_HEADS, S) - dcrow.reshape(FOX_HEADS, S)))
    dc = jnp.pad(dc, ((0, 0), (0, LANE - FOX_HEADS)))
    dfl, dbf = _gate_bwd(dc, sgate, name="fox_gate_bwd")
    dP = jnp.concatenate([dq0, dk0, dv0, du], axis=1)
    dx0a = _mm(dP, WmainT, 'nn', name="ev_proj_dx")
    dx0b = _mm(dfl, WfT, 'nn', name="ev_proj_f_dx")
    dWmainT = _mm(dP, x0, 'tn', tm=1024, tn=1024, name="ev_proj_dw", out_dtype=BF16)
    dWfT = _mm(dfl, x0, 'tn', name="ev_proj_f_dw", out_dtype=BF16)
    grad_x = _combine([dz1, dx0a, dx0b], [ALPHA, 1.0, 1.0], name="grad_x")

    dbbt = _diag_extract(dBB, Cg, Pn, name="s5_bb_diag")
    dcct = _diag_extract(dCC, Pn, Cg, name="s5_cc_diag")
    dbb_re = jnp.transpose(dbbt[0].reshape(G, Cg, Pn), (0, 2, 1)).reshape(G * Pn, Cg)
    dbb_im = jnp.transpose(dbbt[1].reshape(G, Cg, Pn), (0, 2, 1)).reshape(G * Pn, Cg)
    db_re, db_im, dg_re1, dg_im1 = _s5_bb_bwd(g_re1, g_im1, b_re2, b_im2, dbb_re, dbb_im, name="s5_bb_bwd")
    dlam_re, dlam_im, dlstep = _s5_disc_bwd(lam_r, lam_i, lstep, da_s5[0].reshape(G, Pn), da_s5[1].reshape(G, Pn),
                                            dg_re1.reshape(G, Pn), dg_im1.reshape(G, Pn), name="s5_disc_bwd")
    dc_re = jnp.transpose(dcct[0].reshape(G, Pn, Cg), (0, 2, 1))
    dc_im = -jnp.transpose(dcct[1].reshape(G, Pn, Cg), (0, 2, 1))

    def conv_w_full(d0, d1):
        return jnp.stack([jnp.reshape(jnp.transpose(d[:, :, :Fs], (1, 0, 2)), (3, N_CHIPS * Fs)) for d in (d0, d1)])

    def conv_b_full(d0, d1):
        return jnp.stack([jnp.reshape(d[:, 0, :Fs], (N_CHIPS * Fs,)) for d in (d0, d1)])

    small_local = dict(
        ev_b_f=dbf[:, :FOX_HEADS], ev_lambda_re=dlam_re, ev_lambda_im=dlam_im, ev_log_step=dlstep,
        ev_ssm_b_re=db_re, ev_ssm_b_im=db_im, ev_ssm_c_re=dc_re, ev_ssm_c_im=dc_im, ev_ssm_d=dD,
        od_sinks=dsink[:, :, 0],
        ln_mix_g=jnp.concatenate([dg_mix0, dg_mix1]), ln_mix_b=jnp.concatenate([db_mix0, db_mix1]),
        ffn_conv_w=conv_w_full(dcw0, dcw1), ffn_conv_b=conv_b_full(dcb0, dcb1),
        ln_ffn_g=jnp.concatenate([dg_ffn0, dg_ffn1]), ln_ffn_b=jnp.concatenate([db_ffn0, db_ffn1]))
    small = list(small_local.keys())
    red = _all_reduce_small(_pack([small_local[n] for n in small] + [loss_part]), name="ar_small")
    full_shapes = [W[n].shape if n != 'ffn_conv_w' else (DEPTH, 3, N_CHIPS * Fs) for n in small]
    pieces = _unpack(red, full_shapes + [()])
    loss = pieces[-1]
    gsmall = dict(zip(small, pieces[:-1]))
    chip = 2 * lax.axis_index("x") + lax.axis_index("y")
    gsmall['ffn_conv_w'] = lax.dynamic_slice_in_dim(gsmall['ffn_conv_w'], chip * Fs, Fs, axis=2)
    shapes = [W[n].shape for n in small]
    gs, ds_, ms, vs = _adamw(_pack([W[n] for n in small])[None], _pack([gsmall[n] for n in small])[None],
                             _pack([Mo[n] for n in small])[None], _pack([Vo[n] for n in small])[None],
                             name="adamw_small", tr=1 << 14)
    out_g = dict(zip(small, _unpack(gs, shapes)))
    out_d = dict(zip(small, _unpack(ds_, shapes)))
    out_m = dict(zip(small, _unpack(ms, shapes)))
    out_v = dict(zip(small, _unpack(vs, shapes)))

    dw_in_t = jnp.concatenate([dWmainT[:qkv_w], dWfT[:FOX_HEADS], dWmainT[qkv_w:]], axis=0)
    part_now = rs_begin(grp_now, [dw_in_t.reshape(N_CHIPS, EIN // N_CHIPS, D), _shards_from_cols(dWglu),
                                  dWout_ev.reshape(N_CHIPS, D // N_CHIPS, D)], "l0")
    rs_now = _chip_exchange_start('scatter', part_now, [False] * len(part_now), name="rs_l0_start")

    def finish_scatter(started, parts, after, tag):
        send, rcv, thru, lands, _ = started
        lands = _chip_exchange_wait('scatter', send, rcv, thru, lands, [False] * len(parts), after,
                                    name=f"rs_{tag}_wait")
        return _own_slot(lands, own_parts(parts))

    def update(entries, recv, tag):
        halves = [_rowsum(r, name=f"rs_sum4_{e[0]}{e[1]}") for e, r in zip(entries, recv)]
        others = _sibling_join_halves(halves, name=f"rs_{tag}_join")
        pairs = dict(zip(entries, zip(halves, others)))
        done = []
        for n in dict.fromkeys(e[0] for e in entries):
            res = _adamw(Wv[n], [pairs[(n, l)] for l in range(W[n].shape[0])], view(n, Mo[n]), view(n, Vo[n]),
                         name=f"adamw_{n}", by_cols=split_cols[(n, 0)])
            out_g[n], out_d[n], out_m[n], out_v[n] = (view(n, t) for t in res)
            done.append(res[3])
        return done

    recv_rest = (finish_scatter(rs_l1, part_l1, [rs_now[4]], "l1")
                 + finish_scatter(rs_ffn0, part_ffn0, [rs_now[4]], "ffn0"))
    done = update(grp_l1 + grp_ffn0, recv_rest, "rest")
    update(grp_now, finish_scatter(rs_now, part_now, done, "l0"), "l0")

    return (loss, grad_x.reshape(1, S, D), *[out_g[n] for n in names], *[out_d[n] for n in names],
            *[out_m[n] for n in names], *[out_v[n] for n in names])
```

```python
import functools
import math

import numpy as np
import jax
import jax.numpy as jnp
from jax import lax
from jax.experimental import pallas as pl
from jax.experimental.pallas import tpu as pltpu

F32 = jnp.float32
BF16 = jnp.bfloat16
MESH = pl.DeviceIdType.MESH
ANY = pl.BlockSpec(memory_space=pl.ANY)

D_MODEL = 2048
FOX_HEADS = 8
FOX_HEAD_DIM = 128
FOX_WIDTH = 1024
SSM_WIDTH = 1024
SSM_GROUP = 16
SSM_GROUPS = 64
SSM_STATE = 64
SWA_HEADS = 32
SWA_KV_HEADS = 4
SWA_HEAD_DIM = 64
SWA_GROUPS = 8
SWA_WINDOW = 128
ROPE_DIM = 16
ROPE_THETA = 500000.0
LN_EPS = 1e-5
DEPTH = 2
ALPHA = (2.0 * DEPTH) ** 0.25
ADAM_LR = 0.001
ADAM_B1 = 0.9
ADAM_B2 = 0.999
ADAM_EPS = 1e-08
ADAM_WD = 0.01
ADAM_STEP = 10
N_CHIPS = 4

VMEM_LIMIT = 56 * 1024 * 1024
LANE = 128


def _call(body, after=(), **kw):
    if after:
        n = len(after)

        def shifted(*refs):
            return body(*refs[n:])

        call = _call(shifted, **dict(kw, in_specs=[ANY] * n + list(kw["in_specs"])))
        return lambda *args: call(*after, *args)
    return pl.pallas_call(body, **kw)


def _cparams(sem):
    return pltpu.CompilerParams(dimension_semantics=sem, vmem_limit_bytes=VMEM_LIMIT)


def _rup(n, m):
    return (n + m - 1) // m * m


def _pick(n, pref):
    if n <= pref:
        return n
    for step in (128, 16, 8):
        for t in range(pref - pref % step, 0, -step):
            if n % t == 0:
                return t
    return n


def _tile2d(rows, cols, pref_rows=256, budget=256 * 1024):
    tr = _pick(rows, pref_rows)
    if tr < 64:
        tr = rows
    if cols % LANE:
        return tr, cols
    return tr, _pick(cols, max(LANE, budget // tr // LANE * LANE))


def _mm(a, b, mode, *, name, tm=512, tn=1024, tk=2048, bmode=None, out_dtype=F32, after=(), b_map=None,
        o_map=None):
    a3 = a if a.ndim == 3 else a[None]
    b3 = b if b.ndim == 3 else b[None]
    if mode == 'tn':
        K, M = a3.shape[1:]
    else:
        M, K = a3.shape[1:]
    N = b3.shape[1] if mode == 'nt' else b3.shape[2]
    tm, tn, tk = _pick(M, tm), _pick(N, tn), _pick(K, tk)
    nb = max(a3.shape[0], b3.shape[0])
    nbo, nbr = (1, nb) if bmode == 'abr' else (nb, 1)
    nk = K // tk
    nred = nbr * nk
    a_b = bmode in ('ao', 'abr')
    b_b = bmode in ('bo', 'abr')
    o_b = bmode in ('bo', 'ao')

    def bsel(flag, bo, br, remap=None):
        if not flag:
            return 0
        return (bo + br) if remap is None else remap(bo + br)

    if mode == 'tn':
        a_spec = pl.BlockSpec((None, tk, tm), lambda bo, i, j, br, k: (bsel(a_b, bo, br), k, i))
    else:
        a_spec = pl.BlockSpec((None, tm, tk), lambda bo, i, j, br, k: (bsel(a_b, bo, br), i, k))
    if mode == 'nt':
        b_spec = pl.BlockSpec((None, tn, tk), lambda bo, i, j, br, k: (bsel(b_b, bo, br, b_map), j, k))
    else:
        b_spec = pl.BlockSpec((None, tk, tn), lambda bo, i, j, br, k: (bsel(b_b, bo, br, b_map), k, j))
    o_spec = pl.BlockSpec((None, tm, tn), lambda bo, i, j, br, k: (bsel(o_b, bo, br, o_map), i, j))
    dn = {'nn': (((1,), (0,)), ((), ())), 'nt': (((1,), (1,)), ((), ())), 'tn': (((0,), (0,)), ((), ()))}[mode]

    def body(a_ref, b_ref, *rest):
        o_ref, scratch = rest[len(after)], rest[len(after) + 1:]
        r = lax.dot_general(a_ref[...].astype(BF16), b_ref[...].astype(BF16), dn, preferred_element_type=F32)
        if nred == 1:
            o_ref[...] = r.astype(out_dtype)
        else:
            acc = scratch[0]
            step = pl.program_id(3) * nk + pl.program_id(4)

            @pl.when(step == 0)
            def _():
                acc[...] = r

            @pl.when(step > 0)
            def _():
                acc[...] += r

            @pl.when(step == nred - 1)
            def _():
                o_ref[...] = acc[...].astype(out_dtype)

    out = _call(
        body, name=name,
        grid=(nbo, M // tm, N // tn, nbr, nk),
        in_specs=[a_spec, b_spec] + [ANY] * len(after), out_specs=o_spec,
        out_shape=jax.ShapeDtypeStruct((nbo if o_b else 1, M, N), out_dtype),
        scratch_shapes=[] if nred == 1 else [pltpu.VMEM((tm, tn), F32)],
        compiler_params=_cparams(("parallel", "parallel", "parallel", "arbitrary", "arbitrary")),
    )(a3, b3, *after)
    return out if o_b else out[0]


def _add_ln_fwd(x, r, g, b, *, name):
    S, D = x.shape
    tr = _pick(S, 256)

    def body(x_ref, r_ref, g_ref, b_ref, o_ref, xh_ref, rs_ref):
        z = ALPHA * x_ref[...] + r_ref[...]
        mu = jnp.mean(z, axis=-1, keepdims=True)
        zc = z - mu
        var = jnp.mean(zc * zc, axis=-1, keepdims=True)
        rstd = lax.rsqrt(var + LN_EPS)
        xh = zc * rstd
        xh_ref[...] = xh
        rs_ref[...] = rstd
        o_ref[...] = xh * g_ref[...] + b_ref[...]

    row = pl.BlockSpec((tr, D), lambda i: (i, 0))
    vec = pl.BlockSpec((1, D), lambda i: (0, 0))
    return _call(
        body, name=name, grid=(S // tr,),
        in_specs=[row, row, vec, vec],
        out_specs=[row, row, pl.BlockSpec((tr, 1), lambda i: (i, 0))],
        out_shape=[jax.ShapeDtypeStruct((S, D), F32), jax.ShapeDtypeStruct((S, D), F32),
                   jax.ShapeDtypeStruct((S, 1), F32)],
        compiler_params=_cparams(("parallel",)),
    )(x, r, g.reshape(1, D), b.reshape(1, D))


def _ln_bwd(da, db, xhat, rstd, g, *, name, after=()):
    S, D = xhat.shape
    tr = _pick(S, 256)
    two = db is not None

    def body(*refs):
        refs = refs[len(after):]
        if two:
            da_ref, db_ref, xh_ref, rs_ref, g_ref, dz_ref, dg_ref, dbt_ref = refs
            dy = ALPHA * da_ref[...] + db_ref[...]
        else:
            da_ref, xh_ref, rs_ref, g_ref, dz_ref, dg_ref, dbt_ref = refs
            dy = da_ref[...]
        xh = xh_ref[...]
        dxh = dy * g_ref[...]
        m1 = jnp.mean(dxh, axis=-1, keepdims=True)
        m2 = jnp.mean(dxh * xh, axis=-1, keepdims=True)
        dz_ref[...] = rs_ref[...] * (dxh - m1 - xh * m2)
        pg = jnp.sum(dy * xh, axis=0, keepdims=True)
        pb = jnp.sum(dy, axis=0, keepdims=True)

        @pl.when(pl.program_id(0) == 0)
        def _():
            dg_ref[...] = pg
            dbt_ref[...] = pb

        @pl.when(pl.program_id(0) > 0)
        def _():
            dg_ref[...] += pg
            dbt_ref[...] += pb

    row = pl.BlockSpec((tr, D), lambda i: (i, 0))
    vec = pl.BlockSpec((1, D), lambda i: (0, 0))
    ins = list(after) + [da] + ([db] if two else []) + [xhat, rstd, g.reshape(1, D)]
    in_specs = [ANY] * len(after) + [row] + ([row] if two else []) + [row, pl.BlockSpec((tr, 1), lambda i: (i, 0)), vec]
    return _call(
        body, name=name, grid=(S // tr,),
        in_specs=in_specs, out_specs=[row, vec, vec],
        out_shape=[jax.ShapeDtypeStruct((S, D), F32), jax.ShapeDtypeStruct((1, D), F32),
                   jax.ShapeDtypeStruct((1, D), F32)],
        compiler_params=_cparams(("arbitrary",)),
    )(*ins)


def _loss_grad(y, t, *, name):
    S, D = y.shape
    tr = _pick(S, 256)

    def body(y_ref, t_ref, dy_ref, l_ref):
        e = y_ref[...] - t_ref[...]
        dy_ref[...] = e * (1.0 / D)
        part = 0.5 * jnp.sum(jnp.sum(e * e, axis=-1, keepdims=True) * (1.0 / D), axis=0, keepdims=True)

        @pl.when(pl.program_id(0) == 0)
        def _():
            l_ref[...] = part

        @pl.when(pl.program_id(0) > 0)
        def _():
            l_ref[...] += part

    row = pl.BlockSpec((tr, D), lambda i: (i, 0))
    return _call(
        body, name=name, grid=(S // tr,), in_specs=[row, row],
        out_specs=[row, pl.BlockSpec((1, 1), lambda i: (0, 0))],
        out_shape=[jax.ShapeDtypeStruct((S, D), F32), jax.ShapeDtypeStruct((1, 1), F32)],
        compiler_params=_cparams(("arbitrary",)),
    )(y, t)


def _combine(terms, scales, *, name, out_dtype=F32):
    S, D = terms[0].shape
    tr = _pick(S, 256)
    n = len(terms)

    def body(*refs):
        acc = scales[0] * refs[0][...].astype(F32)
        for i in range(1, n):
            acc = acc + scales[i] * refs[i][...].astype(F32)
        refs[n][...] = acc.astype(out_dtype)

    row = pl.BlockSpec((tr, D), lambda i: (i, 0))
    return _call(
        body, name=name, grid=(S // tr,), in_specs=[row] * n, out_specs=row,
        out_shape=jax.ShapeDtypeStruct((S, D), out_dtype),
        compiler_params=_cparams(("parallel",)),
    )(*terms)


def _split3(x):
    h = x.astype(BF16)
    r = x - h.astype(F32)
    m = r.astype(BF16)
    l = (r - m.astype(F32)).astype(BF16)
    return h, m, l


def _tri_matmul(tri_bf, x):
    h, m, l = _split3(x)
    dn = (((1,), (0,)), ((), ()))
    return (lax.dot_general(tri_bf, l, dn, preferred_element_type=F32)
            + lax.dot_general(tri_bf, m, dn, preferred_element_type=F32)
            + lax.dot_general(tri_bf, h, dn, preferred_element_type=F32))


def _gate_fwd(fl, bf, *, name):
    S = fl.shape[0]
    tc = _pick(S, 256)
    nchunk = S // tc

    def body(fl_ref, bf_ref, c_ref, sg_ref):
        r = lax.broadcasted_iota(jnp.int32, (tc, tc), 0)
        cidx = lax.broadcasted_iota(jnp.int32, (tc, tc), 1)
        tri = (r >= cidx).astype(BF16)
        carry = jnp.zeros((1, LANE), F32)
        for ch in range(nchunk):
            x = fl_ref[pl.ds(ch * tc, tc), :] + bf_ref[...]
            lf = jnp.minimum(x, 0.0) - jnp.log(1.0 + jnp.exp(-jnp.abs(x)))
            sg_ref[pl.ds(ch * tc, tc), :] = jax.nn.sigmoid(-x)
            c_ref[pl.ds(ch * tc, tc), :] = _tri_matmul(tri, lf) + carry
            carry = carry + jnp.sum(lf, axis=0, keepdims=True)

    full = pl.BlockSpec((S, LANE), lambda: (0, 0))
    return _call(
        body, name=name, in_specs=[full, pl.BlockSpec((1, LANE), lambda: (0, 0))], out_specs=[full, full],
        out_shape=[jax.ShapeDtypeStruct((S, LANE), F32)] * 2,
        compiler_params=pltpu.CompilerParams(vmem_limit_bytes=VMEM_LIMIT),
    )(fl, bf)


def _gate_bwd(dc, sg, *, name):
    S = dc.shape[0]
    tc = _pick(S, 256)
    nchunk = S // tc

    def body(dc_ref, sg_ref, dfl_ref, db_ref):
        r = lax.broadcasted_iota(jnp.int32, (tc, tc), 0)
        cidx = lax.broadcasted_iota(jnp.int32, (tc, tc), 1)
        tri = (r <= cidx).astype(BF16)
        carry = jnp.zeros((1, LANE), F32)
        dbacc = jnp.zeros((1, LANE), F32)
        for ch in reversed(range(nchunk)):
            d = dc_ref[pl.ds(ch * tc, tc), :]
            dfl = (_tri_matmul(tri, d) + carry) * sg_ref[pl.ds(ch * tc, tc), :]
            dfl_ref[pl.ds(ch * tc, tc), :] = dfl
            dbacc = dbacc + jnp.sum(dfl, axis=0, keepdims=True)
            carry = carry + jnp.sum(d, axis=0, keepdims=True)
        db_ref[...] = dbacc

    full = pl.BlockSpec((S, LANE), lambda: (0, 0))
    return _call(
        body, name=name, in_specs=[full, full], out_specs=[full, pl.BlockSpec((1, LANE), lambda: (0, 0))],
        out_shape=[jax.ShapeDtypeStruct((S, LANE), F32), jax.ShapeDtypeStruct((1, LANE), F32)],
        compiler_params=pltpu.CompilerParams(vmem_limit_bytes=VMEM_LIMIT),
    )(dc, sg)


def _fox_chunk_scores(q, k_ref, ccol, cr_ref, qi, kc, tq):
    scale = 1.0 / math.sqrt(FOX_HEAD_DIM)
    ks = pl.ds(pl.multiple_of(kc * tq, tq), tq)
    s = lax.dot_general(q, k_ref[ks, :].astype(BF16), (((1,), (1,)), ((), ())), preferred_element_type=F32) * scale
    s = s + ccol - cr_ref[:, ks]
    row = lax.broadcasted_iota(jnp.int32, (tq, tq), 0) + qi * tq
    col = lax.broadcasted_iota(jnp.int32, (tq, tq), 1) + kc * tq
    return s, row >= col, ks


def _fox_fwd(P, ccol, crow, *, name):
    S = P.shape[0]
    tq = _pick(S, 256)
    H = FOX_HEADS

    def body(q_ref, k_ref, v_ref, cc_ref, cr_ref, o_ref, l_ref):
        qi = pl.program_id(1)
        q = q_ref[...].astype(BF16)
        ccol = cc_ref[...]

        def chunk(kc, carry):
            m, den, acc = carry
            s, causal, ks = _fox_chunk_scores(q, k_ref, ccol, cr_ref, qi, kc, tq)
            s = jnp.where(causal, s, -1e30)
            m_new = jnp.maximum(m, jnp.max(s, axis=-1, keepdims=True))
            alpha = jnp.exp(m - m_new)
            p = jnp.exp(s - m_new)
            den = alpha * den + jnp.sum(p, axis=-1, keepdims=True)
            acc = alpha * acc + jnp.dot(p.astype(BF16), v_ref[ks, :].astype(BF16), preferred_element_type=F32)
            return m_new, den, acc

        init = (jnp.full((tq, 1), -1e30, F32), jnp.zeros((tq, 1), F32), jnp.zeros((tq, FOX_HEAD_DIM), F32))
        m, den, acc = lax.fori_loop(0, qi + 1, chunk, init)
        o_ref[...] = acc / den
        l_ref[...] = m + jnp.log(den)

    return _call(
        body, name=name, grid=(H, S // tq),
        in_specs=[pl.BlockSpec((tq, 128), lambda h, i: (i, h)),
                  pl.BlockSpec((S, 128), lambda h, i: (0, H + h)),
                  pl.BlockSpec((S, 128), lambda h, i: (0, 2 * H + h)),
                  pl.BlockSpec((None, tq, 1), lambda h, i: (h, i, 0)),
                  pl.BlockSpec((None, 1, S), lambda h, i: (h, 0, 0))],
        out_specs=[pl.BlockSpec((tq, 128), lambda h, i: (i, h)),
                   pl.BlockSpec((None, tq, 1), lambda h, i: (h, i, 0))],
        out_shape=[jax.ShapeDtypeStruct((S, FOX_WIDTH), F32), jax.ShapeDtypeStruct((H, S, 1), F32)],
        compiler_params=_cparams(("parallel", "parallel")),
    )(P, P, P, ccol, crow)


def _fox_bwd(P, ccol, crow, o, lse, dcat, *, name):
    S = P.shape[0]
    tq = _pick(S, 256)
    H = FOX_HEADS
    nq = S // tq
    scale = 1.0 / math.sqrt(FOX_HEAD_DIM)

    def body(q_ref, k_ref, v_ref, cc_ref, cr_ref, o_ref, l_ref, do_ref,
             dq_ref, dk_ref, dv_ref, dcc_ref, dcr_ref, dk_acc, dv_acc):
        qi = pl.program_id(1)

        @pl.when(qi == 0)
        def _():
            dk_acc[...] = jnp.zeros_like(dk_acc)
            dv_acc[...] = jnp.zeros_like(dv_acc)
            dcr_ref[...] = jnp.zeros_like(dcr_ref)

        q = q_ref[...].astype(BF16)
        ccol, lse = cc_ref[...], l_ref[...]
        do = do_ref[...]
        do_bf = do.astype(BF16)
        delta = jnp.sum(do * o_ref[...], axis=-1, keepdims=True)

        def chunk(kc, carry):
            dq, dcc = carry
            s, causal, ks = _fox_chunk_scores(q, k_ref, ccol, cr_ref, qi, kc, tq)
            p = jnp.where(causal, jnp.exp(s - lse), 0.0)
            dp = lax.dot_general(do_bf, v_ref[ks, :].astype(BF16), (((1,), (1,)), ((), ())),
                                 preferred_element_type=F32)
            ds = p * (dp - delta)
            ds_bf = ds.astype(BF16)
            dk_acc[ks, :] += lax.dot_general(ds_bf, q, (((0,), (0,)), ((), ())), preferred_element_type=F32) * scale
            dv_acc[ks, :] += lax.dot_general(p.astype(BF16), do_bf, (((0,), (0,)), ((), ())),
                                             preferred_element_type=F32)
            dcr_ref[:, ks] += jnp.sum(ds, axis=0, keepdims=True)
            dq = dq + jnp.dot(ds_bf, k_ref[ks, :].astype(BF16), preferred_element_type=F32)
            return dq, dcc + jnp.sum(ds, axis=-1, keepdims=True)

        dq, dcc = lax.fori_loop(0, qi + 1, chunk,
                                (jnp.zeros((tq, FOX_HEAD_DIM), F32), jnp.zeros((tq, 1), F32)))
        dq_ref[...] = (dq * scale).astype(BF16)
        dcc_ref[...] = dcc

        @pl.when(qi == nq - 1)
        def _():
            dk_ref[...] = dk_acc[...].astype(BF16)
            dv_ref[...] = dv_acc[...].astype(BF16)

    qblk = pl.BlockSpec((tq, 128), lambda h, i: (i, h))
    kvo = pl.BlockSpec((S, 128), lambda h, i: (0, h))
    col = pl.BlockSpec((None, tq, 1), lambda h, i: (h, i, 0))
    rowv = pl.BlockSpec((None, 1, S), lambda h, i: (h, 0, 0))
    return _call(
        body, name=name, grid=(H, nq),
        in_specs=[qblk,
                  pl.BlockSpec((S, 128), lambda h, i: (0, H + h)),
                  pl.BlockSpec((S, 128), lambda h, i: (0, 2 * H + h)),
                  col, rowv, qblk, col, qblk],
        out_specs=[qblk, kvo, kvo, col, rowv],
        out_shape=[jax.ShapeDtypeStruct((S, FOX_WIDTH), BF16)] * 3
        + [jax.ShapeDtypeStruct((H, S, 1), F32), jax.ShapeDtypeStruct((H, 1, S), F32)],
        scratch_shapes=[pltpu.VMEM((S, 128), F32), pltpu.VMEM((S, 128), F32)],
        compiler_params=_cparams(("parallel", "arbitrary")),
    )(P, P, P, ccol, crow, o, lse, dcat)


def _s5_disc_fwd(lr, li, ls, *, name, after=()):
    G, Pn = lr.shape

    def body(lr_ref, li_ref, ls_ref, ar_ref, ai_ref, gr_ref, gi_ref):
        lr_, li_ = lr_ref[...], li_ref[...]
        dt = jnp.exp(ls_ref[...])
        mag = jnp.exp(lr_ * dt)
        th = li_ * dt
        ar = mag * jnp.cos(th)
        ai = mag * jnp.sin(th)
        den = lr_ * lr_ + li_ * li_
        xr = ar - 1.0
        ar_ref[...] = ar
        ai_ref[...] = ai
        gr_ref[...] = (xr * lr_ + ai * li_) / den
        gi_ref[...] = (ai * lr_ - xr * li_) / den

    sq = pl.BlockSpec((G, Pn), lambda: (0, 0))
    return _call(
        body, after=after, name=name, in_specs=[sq, sq, pl.BlockSpec((G, 1), lambda: (0, 0))], out_specs=[sq] * 4,
        out_shape=[jax.ShapeDtypeStruct((G, Pn), F32)] * 4,
    )(lr, li, ls)


def _s5_disc_bwd(lr, li, ls, dar, dai, dgr, dgi, *, name):
    G, Pn = lr.shape

    def body(lr_ref, li_ref, ls_ref, dar_ref, dai_ref, dgr_ref, dgi_ref, dlr_ref, dli_ref, dls_ref):
        lr_, li_ = lr_ref[...], li_ref[...]
        dt = jnp.exp(ls_ref[...])
        mag = jnp.exp(lr_ * dt)
        th = li_ * dt
        ar = mag * jnp.cos(th)
        ai = mag * jnp.sin(th)
        den = lr_ * lr_ + li_ * li_
        xr = ar - 1.0
        xi = ai
        g_re = (xr * lr_ + xi * li_) / den
        g_im = (xi * lr_ - xr * li_) / den
        dgr_, dgi_ = dgr_ref[...], dgi_ref[...]
        dxr = (dgr_ * lr_ - dgi_ * li_) / den
        dxi = (dgr_ * li_ + dgi_ * lr_) / den
        dden = -(dgr_ * g_re + dgi_ * g_im) / den
        dlr = (dgr_ * xr + dgi_ * xi) / den + 2.0 * dden * lr_
        dli = (dgr_ * xi - dgi_ * xr) / den + 2.0 * dden * li_
        da_r = dar_ref[...] + dxr
        da_i = dai_ref[...] + dxi
        dmag_mag = da_r * ar + da_i * ai
        dth = da_i * ar - da_r * ai
        dlr_ref[...] = dlr + dmag_mag * dt
        dli_ref[...] = dli + dth * dt
        ddt = jnp.sum(dmag_mag * lr_ + dth * li_, axis=-1, keepdims=True)
        dls_ref[...] = ddt * dt

    sq = pl.BlockSpec((G, Pn), lambda: (0, 0))
    c1 = pl.BlockSpec((G, 1), lambda: (0, 0))
    return _call(
        body, name=name, in_specs=[sq, sq, c1, sq, sq, sq, sq], out_specs=[sq, sq, c1],
        out_shape=[jax.ShapeDtypeStruct((G, Pn), F32)] * 2 + [jax.ShapeDtypeStruct((G, 1), F32)],
    )(lr, li, ls, dar, dai, dgr, dgi)


def _s5_bb_fwd(gr, gi, br, bi, *, name):
    R, C = br.shape

    def body(gr_ref, gi_ref, br_ref, bi_ref, or_ref, oi_ref):
        g_r, g_i, b_r, b_i = gr_ref[...], gi_ref[...], br_ref[...], bi_ref[...]
        or_ref[...] = g_r * b_r - g_i * b_i
        oi_ref[...] = g_r * b_i + g_i * b_r

    w = pl.BlockSpec((R, C), lambda: (0, 0))
    c1 = pl.BlockSpec((R, 1), lambda: (0, 0))
    return _call(body, name=name, in_specs=[c1, c1, w, w], out_specs=[w, w],
                 out_shape=[jax.ShapeDtypeStruct((R, C), F32)] * 2)(gr, gi, br, bi)


def _s5_bb_bwd(gr, gi, br, bi, dbbr, dbbi, *, name):
    R, C = br.shape

    def body(gr_ref, gi_ref, br_ref, bi_ref, dr_ref, di_ref, dbr_ref, dbi_ref, dgr_ref, dgi_ref):
        g_r, g_i, b_r, b_i = gr_ref[...], gi_ref[...], br_ref[...], bi_ref[...]
        d_r, d_i = dr_ref[...], di_ref[...]
        dbr_ref[...] = g_r * d_r + g_i * d_i
        dbi_ref[...] = g_r * d_i - g_i * d_r
        dgr_ref[...] = jnp.sum(d_r * b_r + d_i * b_i, axis=-1, keepdims=True)
        dgi_ref[...] = jnp.sum(d_i * b_r - d_r * b_i, axis=-1, keepdims=True)

    w = pl.BlockSpec((R, C), lambda: (0, 0))
    c1 = pl.BlockSpec((R, 1), lambda: (0, 0))
    return _call(body, name=name, in_specs=[c1, c1, w, w, w, w], out_specs=[w, w, c1, c1],
                 out_shape=[jax.ShapeDtypeStruct((R, C), F32)] * 2 + [jax.ShapeDtypeStruct((R, 1), F32)] * 2,
                 )(gr, gi, br, bi, dbbr, dbbi)


_DIAG_TILE = 8


def _diag_mask(gr, gc):
    rows, cols = _DIAG_TILE * gr, _DIAG_TILE * gc
    r = lax.broadcasted_iota(jnp.int32, (rows, cols), 0) >> (gr.bit_length() - 1)
    c = lax.broadcasted_iota(jnp.int32, (rows, cols), 1) >> (gc.bit_length() - 1)
    return r == c


def _diag_expand(t2, gr, gc, *, name, after=()):
    _, R, _ = t2.shape
    G = R // gr
    nt = G // _DIAG_TILE
    rows, cols = _DIAG_TILE * gr, _DIAG_TILE * gc

    def body(t_ref, o_ref):
        @pl.when(pl.program_id(1) == pl.program_id(2))
        def _():
            src = lax.broadcasted_iota(jnp.int32, (gc, cols), 0)
            dst = lax.broadcasted_iota(jnp.int32, (gc, cols), 1) & (gc - 1)
            spread = (src == dst).astype(BF16)
            y = jnp.dot(t_ref[...].astype(BF16), spread, preferred_element_type=F32)
            o_ref[...] = jnp.where(_diag_mask(gr, gc), y, 0.0).astype(BF16)

        @pl.when(pl.program_id(1) != pl.program_id(2))
        def _():
            o_ref[...] = jnp.zeros_like(o_ref)

    return _call(
        body, after=after, name=name, grid=(2, nt, nt),
        in_specs=[pl.BlockSpec((None, rows, gc), lambda p, i, j: (p, i, 0))],
        out_specs=pl.BlockSpec((None, rows, cols), lambda p, i, j: (p, i, j)),
        out_shape=jax.ShapeDtypeStruct((2, R, G * gc), BF16),
        compiler_params=_cparams(("parallel",) * 3),
    )(t2)


def _diag_extract(xd, gr, gc, *, name):
    _, R, _ = xd.shape
    nt = R // gr // _DIAG_TILE
    rows, cols = _DIAG_TILE * gr, _DIAG_TILE * gc

    def body(x_ref, o_ref):
        src = lax.broadcasted_iota(jnp.int32, (cols, gc), 0) & (gc - 1)
        dst = lax.broadcasted_iota(jnp.int32, (cols, gc), 1)
        fold = (src == dst).astype(BF16)
        parts = _split3(jnp.where(_diag_mask(gr, gc), x_ref[...], 0.0))
        acc = jnp.dot(parts[2], fold, preferred_element_type=F32)
        acc = acc + jnp.dot(parts[1], fold, preferred_element_type=F32)
        o_ref[...] = acc + jnp.dot(parts[0], fold, preferred_element_type=F32)

    return _call(
        body, name=name, grid=(2, nt),
        in_specs=[pl.BlockSpec((None, rows, cols), lambda p, i: (p, i, i))],
        out_specs=pl.BlockSpec((None, rows, gc), lambda p, i: (p, i, 0)),
        out_shape=jax.ShapeDtypeStruct((2, R, gc), F32),
        compiler_params=_cparams(("parallel",) * 2),
    )(xd)


SCAN_BLOCK = 8


def _cpowers(ar, ai, sign):
    ai = sign * ai
    out = [(ar, ai)]
    for _ in range(SCAN_BLOCK - 1):
        pr, pi = out[-1]
        out.append((pr * ar - pi * ai, pr * ai + pi * ar))
    return out


def _row_table(pw, row, index_of_row):
    tr_ = jnp.broadcast_to(pw[index_of_row(0)][0], row.shape)
    ti_ = jnp.broadcast_to(pw[index_of_row(0)][1], row.shape)
    for r in range(1, SCAN_BLOCK):
        pr, pi = pw[index_of_row(r)]
        tr_ = jnp.where(row == r, pr, tr_)
        ti_ = jnp.where(row == r, pi, ti_)
    return tr_, ti_


def _s5_scan_fwd(bu, a, *, name):
    _, S, N = bu.shape
    tc = 512
    nt = N // tc

    def body(a_ref, b_ref, h_ref):
        pw = _cpowers(a_ref[0], a_ref[1], 1.0)
        row = lax.broadcasted_iota(jnp.int32, (SCAN_BLOCK, tc), 0)
        lead_r, lead_i = _row_table(pw, row, lambda r: r)

        def step(k, carry):
            cr, ci = carry
            rows = pl.ds(pl.multiple_of(k * SCAN_BLOCK, SCAN_BLOCK), SCAN_BLOCK)
            xr, xi = b_ref[0, rows, :], b_ref[1, rows, :]
            for sh in (1, 2, 4):
                keep = row >= sh
                sr = jnp.where(keep, pltpu.roll(xr, sh, 0), 0.0)
                si = jnp.where(keep, pltpu.roll(xi, sh, 0), 0.0)
                kr, ki = pw[sh - 1]
                xr, xi = xr + kr * sr - ki * si, xi + kr * si + ki * sr
            h_ref[0, rows, :] = xr + lead_r * cr - lead_i * ci
            h_ref[1, rows, :] = xi + lead_r * ci + lead_i * cr
            last = row == SCAN_BLOCK - 1
            tr_ = jnp.sum(jnp.where(last, xr, 0.0), axis=0, keepdims=True)
            ti_ = jnp.sum(jnp.where(last, xi, 0.0), axis=0, keepdims=True)
            a8r, a8i = pw[SCAN_BLOCK - 1]
            return a8r * cr - a8i * ci + tr_, a8r * ci + a8i * cr + ti_

        z = jnp.zeros((1, tc), F32)
        lax.fori_loop(0, S // SCAN_BLOCK, step, (z, z), unroll=2)

    vec = pl.BlockSpec((2, 1, tc), lambda j: (0, 0, j))
    mat = pl.BlockSpec((2, S, tc), lambda j: (0, 0, j))
    return _call(
        body, name=name, grid=(nt,), in_specs=[vec, mat], out_specs=mat,
        out_shape=jax.ShapeDtypeStruct((2, S, N), F32),
        compiler_params=_cparams(("parallel",)),
    )(a, bu)


def _s5_scan_bwd(g, h, a, *, name):
    _, S, N = g.shape
    tc = 256
    nt = N // tc

    def body(a_ref, g_ref, h_ref, l_ref, da_ref):
        pw = _cpowers(a_ref[0], a_ref[1], -1.0)
        row = lax.broadcasted_iota(jnp.int32, (SCAN_BLOCK, tc), 0)
        tail_r, tail_i = _row_table(pw, row, lambda r: SCAN_BLOCK - 1 - r)
        nb = S // SCAN_BLOCK

        def step(i, carry):
            k = nb - 1 - i
            cr, ci, dar, dai = carry
            rows = pl.ds(pl.multiple_of(k * SCAN_BLOCK, SCAN_BLOCK), SCAN_BLOCK)
            xr, xi = g_ref[0, rows, :], g_ref[1, rows, :]
            for sh in (1, 2, 4):
                keep = row < SCAN_BLOCK - sh
                sr = jnp.where(keep, pltpu.roll(xr, SCAN_BLOCK - sh, 0), 0.0)
                si = jnp.where(keep, pltpu.roll(xi, SCAN_BLOCK - sh, 0), 0.0)
                kr, ki = pw[sh - 1]
                xr, xi = xr + kr * sr - ki * si, xi + kr * si + ki * sr
            lr = xr + tail_r * cr - tail_i * ci
            li = xi + tail_r * ci + tail_i * cr
            l_ref[0, rows, :] = lr
            l_ref[1, rows, :] = li
            prev = pl.ds(pl.multiple_of(jnp.maximum(k - 1, 0) * SCAN_BLOCK, SCAN_BLOCK), SCAN_BLOCK)
            has_prev = jnp.where(k > 0, 1.0, 0.0).astype(F32)
            first = row == 0
            hpr = jnp.where(first, pltpu.roll(h_ref[0, prev, :], 1, 0) * has_prev, pltpu.roll(h_ref[0, rows, :], 1, 0))
            hpi = jnp.where(first, pltpu.roll(h_ref[1, prev, :], 1, 0) * has_prev, pltpu.roll(h_ref[1, rows, :], 1, 0))
            tr_ = jnp.sum(jnp.where(first, xr, 0.0), axis=0, keepdims=True)
            ti_ = jnp.sum(jnp.where(first, xi, 0.0), axis=0, keepdims=True)
            a8r, a8i = pw[SCAN_BLOCK - 1]
            return (a8r * cr - a8i * ci + tr_, a8r * ci + a8i * cr + ti_,
                    dar + lr * hpr + li * hpi, dai + li * hpr - lr * hpi)

        z = jnp.zeros((1, tc), F32)
        z8 = jnp.zeros((SCAN_BLOCK, tc), F32)
        _, _, dar, dai = lax.fori_loop(0, nb, step, (z, z, z8, z8), unroll=2)
        da_ref[0] = jnp.sum(dar, axis=0, keepdims=True)
        da_ref[1] = jnp.sum(dai, axis=0, keepdims=True)

    vec = pl.BlockSpec((2, 1, tc), lambda j: (0, 0, j))
    mat = pl.BlockSpec((2, S, tc), lambda j: (0, 0, j))
    return _call(
        body, name=name, grid=(nt,), in_specs=[vec, mat, mat], out_specs=[mat, vec],
        out_shape=[jax.ShapeDtypeStruct((2, S, N), F32), jax.ShapeDtypeStruct((2, 1, N), F32)],
        compiler_params=_cparams(("parallel",)),
    )(a, g, h)


_GELU_C = math.sqrt(2.0 / math.pi)


def _s5_out_fwd(yc, P, dskip, *, name):
    S, W = yc.shape
    tr = _pick(S, 256)
    ub = 3 * FOX_WIDTH // W

    def body(yc_ref, u_ref, d_ref, y_ref, yg_ref):
        y = yc_ref[...] + d_ref[...] * u_ref[...]
        y_ref[...] = y
        t = jnp.tanh(_GELU_C * (y + 0.044715 * y * y * y))
        yg_ref[...] = (0.5 * y * (1.0 + t)).astype(BF16)

    row = pl.BlockSpec((tr, W), lambda i: (i, 0))
    return _call(
        body, name=name, grid=(S // tr,),
        in_specs=[row, pl.BlockSpec((tr, W), lambda i: (i, ub)), pl.BlockSpec((1, W), lambda i: (0, 0))],
        out_specs=[row, row],
        out_shape=[jax.ShapeDtypeStruct((S, W), F32), jax.ShapeDtypeStruct((S, W), BF16)],
        compiler_params=_cparams(("parallel",)),
    )(yc, P, dskip)


def _s5_out_bwd(dyg, y, P, dskip, *, name):
    S, W = y.shape
    tr = _pick(S, 256)
    ub = 3 * FOX_WIDTH // W

    def body(dyg_ref, y_ref, u_ref, d_ref, dy_ref, du_ref, dd_ref):
        y_ = y_ref[...]
        inner = _GELU_C * (y_ + 0.044715 * y_ * y_ * y_)
        t = jnp.tanh(inner)
        dgelu = 0.5 * (1.0 + t) + 0.5 * y_ * (1.0 - t * t) * _GELU_C * (1.0 + 3.0 * 0.044715 * y_ * y_)
        dy = dyg_ref[...] * dgelu
        dy_ref[...] = dy.astype(BF16)
        du_ref[...] = d_ref[...] * dy
        part = jnp.sum(dy * u_ref[...], axis=0, keepdims=True)

        @pl.when(pl.program_id(0) == 0)
        def _():
            dd_ref[...] = part

        @pl.when(pl.program_id(0) > 0)
        def _():
            dd_ref[...] += part

    row = pl.BlockSpec((tr, W), lambda i: (i, 0))
    vec = pl.BlockSpec((1, W), lambda i: (0, 0))
    return _call(
        body, name=name, grid=(S // tr,),
        in_specs=[row, row, pl.BlockSpec((tr, W), lambda i: (i, ub)), vec],
        out_specs=[row, row, vec],
        out_shape=[jax.ShapeDtypeStruct((S, W), BF16), jax.ShapeDtypeStruct((S, W), F32),
                   jax.ShapeDtypeStruct((1, W), F32)],
        compiler_params=_cparams(("arbitrary",)),
    )(dyg, y, P, dskip)


def _glu_fwd(z, *, name):
    S, W2 = z.shape
    W = W2 // 2
    tr = _pick(S, 256)

    def body(z1_ref, z2_ref, o_ref):
        o_ref[...] = (z1_ref[...] * jax.nn.sigmoid(z2_ref[...])).astype(BF16)

    return _call(
        body, name=name, grid=(S // tr,),
        in_specs=[pl.BlockSpec((tr, W), lambda i: (i, 0)), pl.BlockSpec((tr, W), lambda i: (i, 1))],
        out_specs=pl.BlockSpec((tr, W), lambda i: (i, 0)),
        out_shape=jax.ShapeDtypeStruct((S, W), BF16),
        compiler_params=_cparams(("parallel",)),
    )(z, z)


def _glu_bwd(z, dcat, *, name):
    S, W2 = z.shape
    W = W2 // 2
    tr = _pick(S, 256)

    def body(z1_ref, z2_ref, d_ref, dz1_ref, dz2_ref):
        sg = jax.nn.sigmoid(z2_ref[...])
        d = d_ref[...]
        dz1_ref[...] = (d * sg).astype(BF16)
        dz2_ref[...] = (d * z1_ref[...] * sg * (1.0 - sg)).astype(BF16)

    lo = pl.BlockSpec((tr, W), lambda i: (i, 0))
    hi = pl.BlockSpec((tr, W), lambda i: (i, 1))
    dz1, dz2 = _call(
        body, name=name, grid=(S // tr,), in_specs=[lo, hi, hi], out_specs=[lo, lo],
        out_shape=[jax.ShapeDtypeStruct((S, W), BF16)] * 2,
        compiler_params=_cparams(("parallel",)),
    )(z, z, dcat)
    return jnp.concatenate([dz1, dz2], axis=1)


ACT_ROWS = 16
ACT_COLS = 256


def _shift_down(cur, prev, k, row):
    return jnp.where(row >= k, pltpu.roll(cur, k, 0), pltpu.roll(prev, k, 0))


def _shift_up(cur, nxt, k, row):
    n = cur.shape[0]
    return jnp.where(row < n - k, pltpu.roll(cur, n - k, 0), pltpu.roll(nxt, n - k, 0))


def _act_fwd(h, cw, cb, *, name):
    _, S, FP = h.shape
    tr = _pick(S, 256)
    hb = tr // ACT_ROWS
    nq = tr // ACT_ROWS

    def body(g_ref, gh_ref, v_ref, vh_ref, wg_ref, wv_ref, bg_ref, bv_ref, a_ref):
        first = pl.program_id(1) == 0
        for c0 in range(0, FP, ACT_COLS):
            cw_ = min(ACT_COLS, FP - c0)
            cols = pl.ds(c0, cw_)
            rw = lax.broadcasted_iota(jnp.int32, (ACT_ROWS, cw_), 0)
            wg = [wg_ref[pl.ds(k, 1), cols] for k in range(3)]
            wv = [wv_ref[pl.ds(k, 1), cols] for k in range(3)]
            bg, bv = bg_ref[:, cols], bv_ref[:, cols]
            halo_g = jnp.where(first, 0.0, gh_ref[:, cols])
            halo_v = jnp.where(first, 0.0, vh_ref[:, cols])

            def chunk(q, _):
                rows = pl.ds(pl.multiple_of(q * ACT_ROWS, ACT_ROWS), ACT_ROWS)
                before = pl.ds(pl.multiple_of(jnp.maximum(q - 1, 0) * ACT_ROWS, ACT_ROWS), ACT_ROWS)
                g, v = g_ref[rows, cols], v_ref[rows, cols]
                gp = jnp.where(q > 0, g_ref[before, cols], halo_g)
                vp = jnp.where(q > 0, v_ref[before, cols], halo_v)
                cg = bg + wg[2] * g + wg[1] * _shift_down(g, gp, 1, rw) + wg[0] * _shift_down(g, gp, 2, rw)
                cv = bv + wv[2] * v + wv[1] * _shift_down(v, vp, 1, rw) + wv[0] * _shift_down(v, vp, 2, rw)
                a_ref[rows, cols] = (cg * jax.nn.sigmoid(cg) * cv).astype(BF16)
                return 0

            lax.fori_loop(0, nq, chunk, 0, unroll=2)

    def main(off):
        return pl.BlockSpec((None, tr, FP), lambda j, i: (j + off, i, 0))

    def halo(off):
        return pl.BlockSpec((None, ACT_ROWS, FP), lambda j, i: (j + off, jnp.maximum(i * hb - 1, 0), 0))

    def wspec(off):
        return pl.BlockSpec((None, 3, FP), lambda j, i: (j + off, 0, 0))

    def bspec(off):
        return pl.BlockSpec((None, 1, FP), lambda j, i: (j + off, 0, 0))

    cb3 = cb.reshape(4, 1, FP)
    return _call(
        body, name=name, grid=(2, S // tr),
        in_specs=[main(0), halo(0), main(2), halo(2), wspec(0), wspec(2), bspec(0), bspec(2)],
        out_specs=pl.BlockSpec((None, tr, FP), lambda j, i: (j, i, 0)),
        out_shape=jax.ShapeDtypeStruct((2, S, FP), BF16),
        compiler_params=_cparams(("parallel", "parallel")),
    )(h, h, h, h, cw, cw, cb3, cb3)


def _act_bwd(h, da, cw, cb, *, name):
    _, S, FP = h.shape
    tr = _pick(S, 256)
    hb = tr // ACT_ROWS
    nq = tr // ACT_ROWS
    nr = S // tr
    half = ACT_ROWS // 2

    def fold(x):
        return x[:half] + x[half:]

    def body(g_ref, gp_ref, v_ref, vp_ref, da_ref, wg_ref, wv_ref, bg_ref, bv_ref,
             dh_ref, dwg_ref, dwv_ref, dbg_ref, dbv_ref, carry_g, carry_v):
        i = pl.program_id(1)
        bottom = i == 0
        top = i == nr - 1
        for c0 in range(0, FP, ACT_COLS):
            cw_ = min(ACT_COLS, FP - c0)
            cols = pl.ds(c0, cw_)
            rw = lax.broadcasted_iota(jnp.int32, (ACT_ROWS, cw_), 0)
            wg = [wg_ref[pl.ds(k, 1), cols] for k in range(3)]
            wv = [wv_ref[pl.ds(k, 1), cols] for k in range(3)]
            bg, bv = bg_ref[:, cols], bv_ref[:, cols]
            halo_g = jnp.where(top, 0.0, gp_ref[:, cols])
            halo_v = jnp.where(top, 0.0, vp_ref[:, cols])
            after_g = jnp.where(bottom, 0.0, carry_g[:, cols])
            after_v = jnp.where(bottom, 0.0, carry_v[:, cols])

            def chunk(s, carry):
                ng, nv, acc = carry[0], carry[1], carry[2:]
                q = nq - 1 - s
                rows = pl.ds(pl.multiple_of(q * ACT_ROWS, ACT_ROWS), ACT_ROWS)
                before = pl.ds(pl.multiple_of(jnp.maximum(q - 1, 0) * ACT_ROWS, ACT_ROWS), ACT_ROWS)
                g, v = g_ref[rows, cols], v_ref[rows, cols]
                gp = jnp.where(q > 0, g_ref[before, cols], halo_g)
                vp = jnp.where(q > 0, v_ref[before, cols], halo_v)
                g1, g2 = _shift_down(g, gp, 1, rw), _shift_down(g, gp, 2, rw)
                v1, v2 = _shift_down(v, vp, 1, rw), _shift_down(v, vp, 2, rw)
                cg = bg + wg[2] * g + wg[1] * g1 + wg[0] * g2
                cv = bv + wv[2] * v + wv[1] * v1 + wv[0] * v2
                sg = jax.nn.sigmoid(cg)
                d = da_ref[rows, cols]
                dcg = d * cv * sg * (1.0 + cg * (1.0 - sg))
                dcv = d * cg * sg
                dh_ref[0, rows, cols] = (wg[2] * dcg + wg[1] * _shift_up(dcg, ng, 1, rw)
                                         + wg[0] * _shift_up(dcg, ng, 2, rw)).astype(BF16)
                dh_ref[1, rows, cols] = (wv[2] * dcv + wv[1] * _shift_up(dcv, nv, 1, rw)
                                         + wv[0] * _shift_up(dcv, nv, 2, rw)).astype(BF16)
                terms = (dcg * g2, dcg * g1, dcg * g, dcg, dcv * v2, dcv * v1, dcv * v, dcv)
                return (dcg, dcv) + tuple(a + fold(t) for a, t in zip(acc, terms))

            zero = jnp.zeros((half, cw_), F32)
            out = lax.fori_loop(0, nq, chunk, (after_g, after_v) + (zero,) * 8, unroll=2)
            carry_g[:, cols] = out[0]
            carry_v[:, cols] = out[1]
            sums = [jnp.sum(a, axis=0, keepdims=True) for a in out[2:]]

            @pl.when(bottom)
            def _():
                for k in range(3):
                    dwg_ref[pl.ds(k, 1), cols] = sums[k]
                    dwv_ref[pl.ds(k, 1), cols] = sums[4 + k]
                dbg_ref[:, cols] = sums[3]
                dbv_ref[:, cols] = sums[7]

            @pl.when(jnp.logical_not(bottom))
            def _():
                for k in range(3):
                    dwg_ref[pl.ds(k, 1), cols] += sums[k]
                    dwv_ref[pl.ds(k, 1), cols] += sums[4 + k]
                dbg_ref[:, cols] += sums[3]
                dbv_ref[:, cols] += sums[7]

    def main(off):
        return pl.BlockSpec((None, tr, FP), lambda j, i: (j + off, nr - 1 - i, 0))

    def prev(off):
        return pl.BlockSpec((None, ACT_ROWS, FP), lambda j, i: (j + off, jnp.maximum((nr - 1 - i) * hb - 1, 0), 0))

    def wspec(off):
        return pl.BlockSpec((None, 3, FP), lambda j, i: (j + off, 0, 0))

    def bspec(off):
        return pl.BlockSpec((None, 1, FP), lambda j, i: (j + off, 0, 0))

    cb3 = cb.reshape(4, 1, FP)
    dh, dwg, dwv, dbg, dbv = _call(
        body, name=name, grid=(2, nr),
        in_specs=[main(0), prev(0), main(2), prev(2), main(0), wspec(0), wspec(2), bspec(0), bspec(2)],
        out_specs=[pl.BlockSpec((None, 2, tr, FP), lambda j, i: (j, 0, nr - 1 - i, 0)),
                   wspec(0), wspec(0), bspec(0), bspec(0)],
        out_shape=[jax.ShapeDtypeStruct((2, 2, S, FP), BF16)]
        + [jax.ShapeDtypeStruct((2, 3, FP), F32)] * 2 + [jax.ShapeDtypeStruct((2, 1, FP), F32)] * 2,
        scratch_shapes=[pltpu.VMEM((ACT_ROWS, FP), F32), pltpu.VMEM((ACT_ROWS, FP), F32)],
        compiler_params=_cparams(("parallel", "arbitrary")),
    )(h, h, h, h, da, cw, cw, cb3, cb3)
    return (dh.reshape(4, S, FP), jnp.concatenate([dwg, dwv], axis=0), jnp.concatenate([dbg, dbv], axis=0))


def _rope_tables(posf, *, name, after=()):
    S = posf.shape[0]
    half = ROPE_DIM // 2
    d = np.arange(LANE) % SWA_HEAD_DIM
    invf = np.where(d < ROPE_DIM, ROPE_THETA ** (-(d % half).astype(np.float64) / half), 0.0).astype(np.float32)
    m_rot = (d < ROPE_DIM).astype(np.float32)
    m_a = (d < half).astype(np.float32)
    m_b = ((d >= half) & (d < ROPE_DIM)).astype(np.float32)
    consts = jnp.asarray(np.stack([invf, m_rot, m_a, m_b] + [np.zeros(LANE, np.float32)] * 4))

    def body(p_ref, k_ref, c_ref, sa_ref, sb_ref):
        k = k_ref[...]
        ang = p_ref[...] * k[0:1]
        co, si = jnp.cos(ang), jnp.sin(ang)
        c_ref[...] = k[1:2] * co + (1.0 - k[1:2])
        sa_ref[...] = -k[2:3] * si
        sb_ref[...] = k[3:4] * si

    full = pl.BlockSpec((S, LANE), lambda: (0, 0))
    return _call(
        body, after=after, name=name,
        in_specs=[pl.BlockSpec((S, 1), lambda: (0, 0)), pl.BlockSpec((8, LANE), lambda: (0, 0))],
        out_specs=[full] * 3, out_shape=[jax.ShapeDtypeStruct((S, LANE), F32)] * 3,
    )(posf, consts)


def _rope_apply(x, tabs, *, col0, width, inverse, name, out_dtype):
    S = x.shape[0]
    tr = _pick(S, 256)
    rep = width // LANE
    cb = col0 // width

    def body(x_ref, c_ref, sa_ref, sb_ref, o_ref):
        xv = x_ref[...].astype(F32)
        c = jnp.tile(c_ref[...], (1, rep))
        sa = jnp.tile(sa_ref[...], (1, rep))
        sb = jnp.tile(sb_ref[...], (1, rep))
        if not inverse:
            out = xv * c + pltpu.roll(xv, width - 8, 1) * sa + pltpu.roll(xv, 8, 1) * sb
        else:
            out = xv * c + pltpu.roll(xv * sa, 8, 1) + pltpu.roll(xv * sb, width - 8, 1)
        o_ref[...] = out.astype(out_dtype)

    tab = pl.BlockSpec((tr, LANE), lambda i: (i, 0))
    return _call(
        body, name=name, grid=(S // tr,),
        in_specs=[pl.BlockSpec((tr, width), lambda i: (i, cb)), tab, tab, tab],
        out_specs=pl.BlockSpec((tr, width), lambda i: (i, 0)),
        out_shape=jax.ShapeDtypeStruct((S, width), out_dtype),
        compiler_params=_cparams(("parallel",)),
    )(x, *tabs)


def _swa_mask(n):
    rows = SWA_GROUPS * SWA_WINDOW
    qi = lax.broadcasted_iota(jnp.int32, (rows, 2 * SWA_WINDOW), 0) & (SWA_WINDOW - 1)
    kj = lax.broadcasted_iota(jnp.int32, (rows, 2 * SWA_WINDOW), 1)
    rel = SWA_WINDOW + qi - kj
    return (rel >= 0) & (rel < SWA_WINDOW) & ((n > 0) | (kj >= SWA_WINDOW))


def _swa_fwd(qT, kT, vT, sink_rows, *, name):
    S = qT.shape[1]
    W, G, Dh = SWA_WINDOW, SWA_GROUPS, SWA_HEAD_DIM
    nb = S // W
    scale = 1.0 / math.sqrt(Dh)

    def body(q_ref, kp_ref, kc_ref, vp_ref, vc_ref, s_ref, o_ref, l_ref):
        n = pl.program_id(1)
        q = q_ref[...].reshape(G * W, Dh)
        kk = jnp.concatenate([kp_ref[...], kc_ref[...]], axis=0)
        vv = jnp.concatenate([vp_ref[...], vc_ref[...]], axis=0)
        s = lax.dot_general(q, kk, (((1,), (1,)), ((), ())), preferred_element_type=F32) * scale
        s = jnp.where(_swa_mask(n), s, -1e30)
        sink = s_ref[...]
        m = jnp.maximum(jnp.max(s, axis=-1, keepdims=True), sink)
        e = jnp.exp(s - m)
        den = jnp.sum(e, axis=-1, keepdims=True) + jnp.exp(sink - m)
        p = e / den
        o_ref[...] = jnp.dot(p.astype(BF16), vv, preferred_element_type=F32).reshape(G, W, Dh)
        l_ref[...] = (m + jnp.log(den)).reshape(G, W, 1)

    qs = pl.BlockSpec((G, W, Dh), lambda g, n: (g, n, 0))
    prev = pl.BlockSpec((None, W, Dh), lambda g, n: (g, jnp.maximum(n - 1, 0), 0))
    cur = pl.BlockSpec((None, W, Dh), lambda g, n: (g, n, 0))
    return _call(
        body, name=name, grid=(SWA_KV_HEADS, nb),
        in_specs=[qs, prev, cur, prev, cur, pl.BlockSpec((None, G * W, 1), lambda g, n: (g, 0, 0))],
        out_specs=[qs, pl.BlockSpec((G, W, 1), lambda g, n: (g, n, 0))],
        out_shape=[jax.ShapeDtypeStruct((SWA_HEADS, S, Dh), F32), jax.ShapeDtypeStruct((SWA_HEADS, S, 1), F32)],
        compiler_params=_cparams(("parallel", "parallel")),
    )(qT, kT, kT, vT, vT, sink_rows)


def _swa_bwd(qT, kT, vT, sink_rows, oT, L, doT, *, name):
    S = qT.shape[1]
    W, G, Dh = SWA_WINDOW, SWA_GROUPS, SWA_HEAD_DIM
    nb = S // W
    scale = 1.0 / math.sqrt(Dh)

    def body(q_ref, kp_ref, kc_ref, vp_ref, vc_ref, s_ref, o_ref, l_ref, do_ref,
             dq_ref, dk_ref, dv_ref, ds_ref):
        n = pl.program_id(1)
        q = q_ref[...].reshape(G * W, Dh)
        kk = jnp.concatenate([kp_ref[...], kc_ref[...]], axis=0)
        vv = jnp.concatenate([vp_ref[...], vc_ref[...]], axis=0)
        s = lax.dot_general(q, kk, (((1,), (1,)), ((), ())), preferred_element_type=F32) * scale
        lrow = l_ref[...].reshape(G * W, 1)
        p = jnp.where(_swa_mask(n), jnp.exp(s - lrow), 0.0)
        do = do_ref[...].reshape(G * W, Dh)
        do_bf = do.astype(BF16)
        dp = lax.dot_general(do_bf, vv, (((1,), (1,)), ((), ())), preferred_element_type=F32)
        delta = jnp.sum(do * o_ref[...].reshape(G * W, Dh), axis=-1, keepdims=True)
        dsc = p * (dp - delta)
        ds_bf = dsc.astype(BF16)
        dq_ref[...] = (jnp.dot(ds_bf, kk, preferred_element_type=F32) * scale).astype(BF16).reshape(G, W, Dh)
        dkk = lax.dot_general(ds_bf, q, (((0,), (0,)), ((), ())), preferred_element_type=F32) * scale
        dvv = lax.dot_general(p.astype(BF16), do_bf, (((0,), (0,)), ((), ())), preferred_element_type=F32)
        dsk = -jnp.exp(s_ref[...] - lrow) * delta
        dsk = jnp.broadcast_to(jnp.sum(dsk.reshape(G, W, 1), axis=1), (G, LANE))

        @pl.when(n == 0)
        def _():
            dk_ref[...] = jnp.zeros_like(dk_ref)
            dv_ref[...] = jnp.zeros_like(dv_ref)
            ds_ref[...] = jnp.zeros_like(ds_ref)

        rows = pl.ds(pl.multiple_of(n * W, W), 2 * W)
        dk_ref[rows, :] += dkk
        dv_ref[rows, :] += dvv
        ds_ref[...] += dsk

    qs = pl.BlockSpec((G, W, Dh), lambda g, n: (g, n, 0))
    prev = pl.BlockSpec((None, W, Dh), lambda g, n: (g, jnp.maximum(n - 1, 0), 0))
    cur = pl.BlockSpec((None, W, Dh), lambda g, n: (g, n, 0))
    lsp = pl.BlockSpec((G, W, 1), lambda g, n: (g, n, 0))
    kvo = pl.BlockSpec((None, S + W, Dh), lambda g, n: (g, 0, 0))
    return _call(
        body, name=name, grid=(SWA_KV_HEADS, nb),
        in_specs=[qs, prev, cur, prev, cur, pl.BlockSpec((None, G * W, 1), lambda g, n: (g, 0, 0)), qs, lsp, qs],
        out_specs=[qs, kvo, kvo, pl.BlockSpec((None, G, LANE), lambda g, n: (g, 0, 0))],
        out_shape=[jax.ShapeDtypeStruct((SWA_HEADS, S, Dh), BF16),
                   jax.ShapeDtypeStruct((SWA_KV_HEADS, S + W, Dh), F32),
                   jax.ShapeDtypeStruct((SWA_KV_HEADS, S + W, Dh), F32),
                   jax.ShapeDtypeStruct((SWA_KV_HEADS, G, LANE), F32)],
        compiler_params=_cparams(("parallel", "arbitrary")),
    )(qT, kT, kT, vT, vT, sink_rows, oT, L, doT)


def _adamw(w, g, m, v, *, name, tr=128, by_cols=False):
    L, R, C = w.shape
    split = isinstance(g, (list, tuple))
    HR, HC = _half_shape(R, C, by_cols) if split else (R, C)
    tr, tc = _tile2d(HR, HC, tr)
    nr, nc = HR // tr, HC // tc
    c1 = 1.0 / (1.0 - ADAM_B1 ** ADAM_STEP)
    c2 = 1.0 / (1.0 - ADAM_B2 ** ADAM_STEP)
    ng = 2 * L if split else 1

    def body(*refs):
        w_ref, g_refs, (m_ref, v_ref, go_ref, d_ref, mo_ref, vo_ref) = refs[0], refs[1:1 + ng], refs[1 + ng:]
        if split:
            mine = pl.program_id(1) == lax.axis_index("c")
            g_ = jnp.where(mine, g_refs[0][...], g_refs[1][...])
            for l in range(1, L):
                g_ = jnp.where(pl.program_id(0) == l,
                               jnp.where(mine, g_refs[2 * l][...], g_refs[2 * l + 1][...]), g_)
        else:
            g_ = g_refs[0][...]
        mn = ADAM_B1 * m_ref[...] + (1.0 - ADAM_B1) * g_
        vn = ADAM_B2 * v_ref[...] + (1.0 - ADAM_B2) * (g_ * g_)
        go_ref[...] = g_
        mo_ref[...] = mn
        vo_ref[...] = vn
        d_ref[...] = -ADAM_LR * ((mn * c1) / (jnp.sqrt(vn * c2) + ADAM_EPS) + ADAM_WD * w_ref[...])

    def whole(l, hf, i, j):
        return (l, i, hf * nc + j) if by_cols else (l, hf * nr + i, j)

    row = pl.BlockSpec((None, tr, tc), whole)
    half = pl.BlockSpec((tr, tc), lambda l, hf, i, j: (i, j))
    gs = [h for pair in g for h in pair] if split else [g]
    return _call(
        body, name=name, grid=(L, 2 if split else 1, nr, nc),
        in_specs=[row] + [half if split else row] * ng + [row, row],
        out_specs=[row] * 4, out_shape=[jax.ShapeDtypeStruct((L, R, C), F32)] * 4,
        compiler_params=_cparams(("parallel",) * 4),
    )(w, *gs, m, v)


def _sum2_halves(g4, s4, by_cols, *, name):
    n, R, C = g4.shape
    HR, HC = _half_shape(R, C, by_cols)
    tr, tc = _tile2d(HR, HC, budget=1024 * 1024)
    nr, nc = HR // tr, HC // tc
    core = lax.axis_index("c").astype(jnp.int32).reshape(1)

    def body(c_ref, g_ref, s_ref, o_ref):
        o_ref[...] = (g_ref[...].astype(F32) + s_ref[...].astype(F32)).astype(BF16)

    def mine(k, i, j, c):
        return (k, i, c[0] * nc + j) if by_cols else (k, c[0] * nr + i, j)

    blk = pl.BlockSpec((None, tr, tc), lambda k, i, j, c: (k, i, j))
    return _call(
        body, name=name,
        grid_spec=pltpu.PrefetchScalarGridSpec(
            num_scalar_prefetch=1, grid=(n, nr, nc),
            in_specs=[pl.BlockSpec((None, tr, tc), mine), blk], out_specs=blk),
        out_shape=jax.ShapeDtypeStruct((n, HR, HC), BF16),
        compiler_params=_cparams(("parallel", "parallel", "parallel")),
    )(core, g4, s4)


def _rowsum(parts, *, name, out_dtype=F32):
    n, R, C = parts.shape
    tr, tc = _tile2d(R, C, budget=512 * 1024)

    def body(p_ref, o_ref):
        acc = p_ref[0].astype(F32)
        for i in range(1, n):
            acc = acc + p_ref[i].astype(F32)
        o_ref[...] = acc.astype(out_dtype)

    return _call(
        body, name=name, grid=(R // tr, C // tc),
        in_specs=[pl.BlockSpec((n, tr, tc), lambda i, j: (0, i, j))],
        out_specs=pl.BlockSpec((tr, tc), lambda i, j: (i, j)),
        out_shape=jax.ShapeDtypeStruct((R, C), out_dtype),
        compiler_params=_cparams(("parallel", "parallel")),
    )(parts)


def _where_am_i():
    x, y, c = lax.axis_index("x"), lax.axis_index("y"), lax.axis_index("c")
    chips = [(1 - x, y), (x, 1 - y), (1 - x, 1 - y)]
    return x, y, c, chips


def _half_idx(rows, cols, by_cols, which):
    if by_cols:
        hc = cols // 2
        return (slice(None), pl.ds(pl.multiple_of(which * hc, LANE), hc))
    hr = rows // 2
    return (pl.ds(pl.multiple_of(which * hr, 16), hr), slice(None))


def _half_shape(rows, cols, by_cols):
    return (rows, cols // 2) if by_cols else (rows // 2, cols)


def _all_gather_shards(shards, by_cols, *, name):
    n = len(shards)

    def body(*refs):
        ins, outs = refs[:n], refs[n:2 * n]
        send, recv = refs[2 * n:]
        x, y, c, chips = _where_am_i()
        me = 2 * x + y
        sibling = (x, y, 1 - c)

        def half(i, which):
            return _half_idx(*shards[i].shape, by_cols[i], which)

        def cp(i, k, src, dst, to):
            return pltpu.make_async_remote_copy(src_ref=src, dst_ref=dst, send_sem=send.at[i, k],
                                                recv_sem=recv.at[i, k], device_id=to, device_id_type=MESH)

        first = []
        for i in range(n):
            for k, (px, py) in enumerate(chips):
                d = cp(i, k, ins[i].at[half(i, c)], outs[i].at[(me,) + half(i, c)], (px, py, c))
                d.start()
                first.append(d)
        passed = []
        for i in range(n):
            for k, (px, py) in enumerate(chips):
                blk = outs[i].at[(2 * px + py,) + half(i, c)]
                cp(i, k, blk, blk, (px, py, c)).wait_recv()
                d = cp(i, 3 + k, blk, blk, sibling)
                d.start()
                passed.append(d)
        for i in range(n):
            for k, (px, py) in enumerate(chips):
                blk = outs[i].at[(2 * px + py,) + half(i, 1 - c)]
                cp(i, 3 + k, blk, blk, sibling).wait_recv()
        for d in first + passed:
            d.wait_send()

    got = _call(
        body, name=name, in_specs=[ANY] * n, out_specs=[ANY] * n,
        out_shape=[jax.ShapeDtypeStruct((N_CHIPS,) + s.shape, s.dtype) for s in shards],
        scratch_shapes=[pltpu.SemaphoreType.DMA((n, 6)), pltpu.SemaphoreType.DMA((n, 6))],
    )(*shards)
    me = 2 * lax.axis_index("x") + lax.axis_index("y")
    return [lax.dynamic_update_slice_in_dim(g, s[None], me, axis=0) for g, s in zip(got, shards)]


HBM_SPEC = pl.BlockSpec(memory_space=pltpu.HBM)
SEM_SPEC = pl.BlockSpec(memory_space=pltpu.SEMAPHORE)
DATAFLOW = pltpu.SideEffectType.DATAFLOW_SIDE_EFFECTING


def _chip_exchange_refs(kind, shards_shape, by_cols, src, land, i, chip_k, c, me):
    if kind == 'gather':
        half = _half_idx(*shards_shape, by_cols, c)
        return src.at[half], land.at[(me,) + half], land.at[(chip_k,) + half]
    return src.at[chip_k], land.at[me], land.at[chip_k]


def _chip_exchange_start(kind, srcs, by_cols, *, name, after=()):
    n = len(srcs)
    land_shapes = [((N_CHIPS,) + s.shape) if kind == 'gather' else s.shape for s in srcs]

    def body(*refs):
        src_refs, land_refs = refs[:n], refs[n:2 * n]
        send, recv = refs[2 * n + len(after)], refs[2 * n + len(after) + 1]
        token = refs[-1]
        x, y, c, chips = _where_am_i()
        me = 2 * x + y
        for i in range(n):
            for k, (px, py) in enumerate(chips):
                s, d, _ = _chip_exchange_refs(kind, srcs[i].shape, by_cols[i], src_refs[i], land_refs[i], i,
                                              2 * px + py, c, me)
                pltpu.make_async_remote_copy(src_ref=s, dst_ref=d, send_sem=send.at[3 * i + k],
                                             recv_sem=recv.at[3 * i + k], device_id=(px, py, c),
                                             device_id_type=MESH).start()
        token[...] = jnp.zeros_like(token)

    lands = [pltpu.with_memory_space_constraint(lax.empty(sh, s.dtype), pltpu.HBM) for sh, s in zip(land_shapes, srcs)]
    outs = _call(
        body, name=name,
        out_shape=(pltpu.SemaphoreType.DMA((3 * n,)), pltpu.SemaphoreType.DMA((3 * n,)),
                   *[pltpu.HBM(s.shape, s.dtype) for s in srcs],
                   *[pltpu.HBM(sh, s.dtype) for sh, s in zip(land_shapes, srcs)],
                   jax.ShapeDtypeStruct((8, LANE), F32)),
        in_specs=[HBM_SPEC] * (2 * n) + [ANY] * len(after),
        out_specs=(SEM_SPEC, SEM_SPEC, *([HBM_SPEC] * (2 * n)), pl.BlockSpec(memory_space=pltpu.VMEM)),
        input_output_aliases={j: 2 + j for j in range(2 * n)},
        compiler_params=pltpu.CompilerParams(has_side_effects=DATAFLOW),
    )(*[pltpu.with_memory_space_constraint(s, pltpu.HBM) for s in srcs], *lands, *after)
    return outs[0], outs[1], list(outs[2:2 + n]), list(outs[2 + n:2 + 2 * n]), outs[-1]


def _chip_exchange_wait(kind, send, recv, srcs, lands, by_cols, after, *, name):
    n = len(srcs)

    def body(*refs):
        src_refs, land_refs = refs[:n], refs[n:2 * n]
        send_r, recv_r = refs[2 * n], refs[2 * n + 1]
        x, y, c, chips = _where_am_i()
        me = 2 * x + y
        for i in range(n):
            for k, (px, py) in enumerate(chips):
                s, _, d = _chip_exchange_refs(kind, srcs[i].shape, by_cols[i], src_refs[i], land_refs[i], i,
                                              2 * px + py, c, me)
                cp = pltpu.make_async_remote_copy(src_ref=s, dst_ref=d, send_sem=send_r.at[3 * i + k],
                                                  recv_sem=recv_r.at[3 * i + k], device_id=(px, py, c),
                                                  device_id_type=MESH)
                cp.wait_send()
                cp.wait_recv()

    outs = _call(
        body, name=name,
        out_shape=(*[pltpu.HBM(s.shape, s.dtype) for s in srcs], *[pltpu.HBM(l.shape, l.dtype) for l in lands]),
        in_specs=[HBM_SPEC] * (2 * n) + [SEM_SPEC, SEM_SPEC] + [ANY] * len(after),
        out_specs=tuple([HBM_SPEC] * (2 * n)),
        input_output_aliases={j: j for j in range(2 * n)},
        compiler_params=pltpu.CompilerParams(has_side_effects=DATAFLOW),
    )(*srcs, *lands, send, recv, *after)
    return list(outs[:n]), list(outs[n:])


def _sibling_halves_start(grads, by_cols, *, name, after=()):
    n = len(grads)
    land_shapes = [(N_CHIPS,) + _half_shape(*g.shape[1:], bc) for g, bc in zip(grads, by_cols)]

    def body(*refs):
        src_refs, land_refs = refs[:n], refs[n:2 * n]
        send, recv = refs[2 * n + len(after)], refs[2 * n + len(after) + 1]
        token = refs[-1]
        x, y, c, _ = _where_am_i()
        for i in range(n):
            src = src_refs[i].at[(slice(None),) + _half_idx(*grads[i].shape[1:], by_cols[i], 1 - c)]
            pltpu.make_async_remote_copy(src_ref=src, dst_ref=land_refs[i], send_sem=send.at[i], recv_sem=recv.at[i],
                                         device_id=(x, y, 1 - c), device_id_type=MESH).start()
        token[...] = jnp.zeros_like(token)

    lands = [pltpu.with_memory_space_constraint(lax.empty(sh, g.dtype), pltpu.HBM) for sh, g in zip(land_shapes, grads)]
    outs = _call(
        body, name=name,
        out_shape=(pltpu.SemaphoreType.DMA((n,)), pltpu.SemaphoreType.DMA((n,)),
                   *[pltpu.HBM(g.shape, g.dtype) for g in grads],
                   *[pltpu.HBM(sh, g.dtype) for sh, g in zip(land_shapes, grads)],
                   jax.ShapeDtypeStruct((8, LANE), F32)),
        in_specs=[HBM_SPEC] * (2 * n) + [ANY] * len(after),
        out_specs=(SEM_SPEC, SEM_SPEC, *([HBM_SPEC] * (2 * n)), pl.BlockSpec(memory_space=pltpu.VMEM)),
        input_output_aliases={j: 2 + j for j in range(2 * n)},
        compiler_params=pltpu.CompilerParams(has_side_effects=DATAFLOW),
    )(*[pltpu.with_memory_space_constraint(g, pltpu.HBM) for g in grads], *lands, *after)
    return outs[0], outs[1], list(outs[2:2 + n]), list(outs[2 + n:2 + 2 * n]), outs[-1]


def _sibling_halves_wait(send, recv, grads, lands, by_cols, after, *, name):
    n = len(grads)

    def body(*refs):
        src_refs, land_refs = refs[:n], refs[n:2 * n]
        send_r, recv_r = refs[2 * n], refs[2 * n + 1]
        x, y, c, _ = _where_am_i()
        for i in range(n):
            src = src_refs[i].at[(slice(None),) + _half_idx(*grads[i].shape[1:], by_cols[i], 1 - c)]
            cp = pltpu.make_async_remote_copy(src_ref=src, dst_ref=land_refs[i], send_sem=send_r.at[i],
                                              recv_sem=recv_r.at[i], device_id=(x, y, 1 - c), device_id_type=MESH)
            cp.wait_send()
            cp.wait_recv()

    outs = _call(
        body, name=name,
        out_shape=(*[pltpu.HBM(g.shape, g.dtype) for g in grads], *[pltpu.HBM(l.shape, l.dtype) for l in lands]),
        in_specs=[HBM_SPEC] * (2 * n) + [SEM_SPEC, SEM_SPEC] + [ANY] * len(after),
        out_specs=tuple([HBM_SPEC] * (2 * n)),
        input_output_aliases={j: j for j in range(2 * n)},
        compiler_params=pltpu.CompilerParams(has_side_effects=DATAFLOW),
    )(*grads, *lands, send, recv, *after)
    return list(outs[:n]), list(outs[n:])


def _sibling_pass_gathered(lands, shard_shapes, by_cols, *, name):
    n = len(lands)

    def body(*refs):
        outs = refs[n:2 * n]
        send, recv = refs[2 * n:]
        x, y, c, chips = _where_am_i()
        sibling = (x, y, 1 - c)
        cps = []
        for i in range(n):
            for k, (px, py) in enumerate(chips):
                blk = outs[i].at[(2 * px + py,) + _half_idx(*shard_shapes[i], by_cols[i], c)]
                d = pltpu.make_async_remote_copy(src_ref=blk, dst_ref=blk, send_sem=send.at[i, k],
                                                 recv_sem=recv.at[i, k], device_id=sibling, device_id_type=MESH)
                d.start()
                cps.append(d)
        for i in range(n):
            for k, (px, py) in enumerate(chips):
                blk = outs[i].at[(2 * px + py,) + _half_idx(*shard_shapes[i], by_cols[i], 1 - c)]
                pltpu.make_async_remote_copy(src_ref=blk, dst_ref=blk, send_sem=send.at[i, k], recv_sem=recv.at[i, k],
                                             device_id=sibling, device_id_type=MESH).wait_recv()
        for d in cps:
            d.wait_send()

    return _call(
        body, name=name, in_specs=[ANY] * n, out_specs=[ANY] * n,
        out_shape=[jax.ShapeDtypeStruct(l.shape, l.dtype) for l in lands],
        input_output_aliases={j: j for j in range(n)},
        scratch_shapes=[pltpu.SemaphoreType.DMA((n, 3)), pltpu.SemaphoreType.DMA((n, 3))],
    )(*lands)


def _own_slot(lands, owns):
    me = 2 * lax.axis_index("x") + lax.axis_index("y")
    return [lax.dynamic_update_slice_in_dim(g, s, me, axis=0) for g, s in zip(lands, owns)]


def _sibling_send_halves(grads, by_cols, *, name):
    n = len(grads)

    def body(*refs):
        ins, outs = refs[:n], refs[n:2 * n]
        send, recv = refs[2 * n:]
        x, y, c, _ = _where_am_i()
        sibling = (x, y, 1 - c)
        cps = []
        for i in range(n):
            src = ins[i].at[(slice(None),) + _half_idx(*grads[i].shape[1:], by_cols[i], 1 - c)]
            d = pltpu.make_async_remote_copy(src_ref=src, dst_ref=outs[i], send_sem=send.at[i],
                                             recv_sem=recv.at[i], device_id=sibling, device_id_type=MESH)
            d.start()
            cps.append(d)
        for d in cps:
            d.wait()

    return _call(
        body, name=name, in_specs=[ANY] * n, out_specs=[ANY] * n,
        out_shape=[jax.ShapeDtypeStruct((N_CHIPS,) + _half_shape(*g.shape[1:], bc), g.dtype)
                   for g, bc in zip(grads, by_cols)],
        scratch_shapes=[pltpu.SemaphoreType.DMA((n,)), pltpu.SemaphoreType.DMA((n,))],
    )(*grads)


def _scatter_to_chips(parts, *, name):
    n = len(parts)

    def body(*refs):
        ins, outs = refs[:n], refs[n:2 * n]
        send, recv = refs[2 * n:]
        x, y, c, chips = _where_am_i()
        me = 2 * x + y
        cps = []
        for i in range(n):
            for k, (px, py) in enumerate(chips):
                d = pltpu.make_async_remote_copy(
                    src_ref=ins[i].at[2 * px + py], dst_ref=outs[i].at[me], send_sem=send.at[i, k],
                    recv_sem=recv.at[i, k], device_id=(px, py, c), device_id_type=MESH)
                d.start()
                cps.append((d, i, k, px, py))
        for d, i, k, px, py in cps:
            blk = outs[i].at[2 * px + py]
            pltpu.make_async_remote_copy(src_ref=blk, dst_ref=blk, send_sem=send.at[i, k], recv_sem=recv.at[i, k],
                                         device_id=(px, py, c), device_id_type=MESH).wait_recv()
        for d, *_ in cps:
            d.wait_send()

    got = _call(
        body, name=name, in_specs=[ANY] * n, out_specs=[ANY] * n,
        out_shape=[jax.ShapeDtypeStruct(p.shape, p.dtype) for p in parts],
        scratch_shapes=[pltpu.SemaphoreType.DMA((n, 3)), pltpu.SemaphoreType.DMA((n, 3))],
    )(*parts)
    me = 2 * lax.axis_index("x") + lax.axis_index("y")
    return [lax.dynamic_update_slice_in_dim(g, lax.dynamic_slice_in_dim(p, me, 1, axis=0), me, axis=0)
            for g, p in zip(got, parts)]


def _sibling_join_halves(halves, *, name):
    n = len(halves)

    def body(*refs):
        ins, outs = refs[:n], refs[n:2 * n]
        send, recv = refs[2 * n:]
        x, y, c, _ = _where_am_i()
        sibling = (x, y, 1 - c)
        cps = []
        for i in range(n):
            d = pltpu.make_async_remote_copy(src_ref=ins[i], dst_ref=outs[i], send_sem=send.at[i],
                                             recv_sem=recv.at[i], device_id=sibling, device_id_type=MESH)
            d.start()
            cps.append(d)
        for d in cps:
            d.wait()

    return _call(
        body, name=name, in_specs=[ANY] * n, out_specs=[ANY] * n,
        out_shape=[jax.ShapeDtypeStruct(h.shape, h.dtype) for h in halves],
        scratch_shapes=[pltpu.SemaphoreType.DMA((n,)), pltpu.SemaphoreType.DMA((n,))],
    )(*halves)


def _all_reduce_small(v, *, name):
    R, C = v.shape

    def body(v_ref, o_ref, sib, slots, send, recv):
        x, y, c, chips = _where_am_i()
        me = 2 * x + y
        sibling = (x, y, 1 - c)
        d = pltpu.make_async_remote_copy(src_ref=v_ref, dst_ref=sib, send_sem=send.at[0], recv_sem=recv.at[0],
                                         device_id=sibling, device_id_type=MESH)
        d.start()
        d.wait()
        slots[me] = v_ref[...] + sib[...]
        cps = []
        for k, (px, py) in enumerate(chips):
            d = pltpu.make_async_remote_copy(src_ref=slots.at[me], dst_ref=slots.at[me], send_sem=send.at[1 + k],
                                             recv_sem=recv.at[1 + k], device_id=(px, py, c), device_id_type=MESH)
            d.start()
            cps.append(d)
        for k, (px, py) in enumerate(chips):
            blk = slots.at[2 * px + py]
            pltpu.make_async_remote_copy(src_ref=blk, dst_ref=blk, send_sem=send.at[1 + k], recv_sem=recv.at[1 + k],
                                         device_id=(px, py, c), device_id_type=MESH).wait_recv()
        for d in cps:
            d.wait_send()
        o_ref[...] = (slots[0] + slots[1]) + (slots[2] + slots[3])

    vm = pl.BlockSpec(memory_space=pltpu.VMEM)
    return _call(
        body, name=name, in_specs=[vm], out_specs=vm,
        out_shape=jax.ShapeDtypeStruct((R, C), F32),
        scratch_shapes=[pltpu.VMEM((R, C), F32), pltpu.VMEM((N_CHIPS, R, C), F32),
                        pltpu.SemaphoreType.DMA((4,)), pltpu.SemaphoreType.DMA((4,))],
        compiler_params=pltpu.CompilerParams(vmem_limit_bytes=VMEM_LIMIT),
    )(v)


def _cols_from_shards(g):
    return jnp.transpose(g, (1, 0, 2)).reshape(g.shape[1], -1)


def _shards_from_cols(w):
    R, C4 = w.shape
    return jnp.transpose(w.reshape(R, N_CHIPS, C4 // N_CHIPS), (1, 0, 2))


def _block_diag(t):
    G, a, b = t.shape
    eye = jnp.eye(G, dtype=t.dtype)
    return (t[:, :, None, :] * eye[:, None, :, None]).reshape(G * a, G * b)


def _diag_blocks(xm, G):
    a, b = xm.shape[0] // G, xm.shape[1] // G
    idx = jnp.arange(G)
    return xm.reshape(G, a, G, b)[idx, :, idx, :]


def _pack(arrs):
    flat = []
    for a in arrs:
        f = a.reshape(-1).astype(F32)
        flat.append(jnp.pad(f, (0, _rup(f.shape[0], LANE) - f.shape[0])))
    v = jnp.concatenate(flat)
    rows = _rup(v.shape[0] // LANE, 8)
    v = jnp.pad(v, (0, rows * LANE - v.shape[0]))
    return v.reshape(rows, LANE)


def _unpack(v, shapes):
    flat = v.reshape(-1)
    out, off = [], 0
    for s in shapes:
        n = int(np.prod(s))
        out.append(flat[off:off + n].reshape(s))
        off += _rup(n, LANE)
    return out


def _ffn_fwd(x, Wup, Wdn, cw, cb, tag):
    h = _mm(x, Wup, 'nt', bmode='bo', tm=512, tn=4096, name=f"ffn_up_{tag}")
    a = _act_fwd(h, cw, cb, name=f"ffn_act_{tag}")
    f = _mm(a, Wdn, 'nn', bmode='abr', tm=512, tn=1024, tk=4096, name=f"ffn_down_{tag}")
    return f, h, a


def _ffn_bwd(df, x, h, a, Wup, Wdn, cw, cb, tag):
    da = _mm(df, Wdn, 'nt', bmode='bo', tm=512, tn=4096, name=f"ffn_da_{tag}")
    dWdn = _mm(a, df, 'tn', bmode='ao', tm=4096, tn=512, name=f"ffn_dwdn_{tag}", out_dtype=BF16)
    dh, dcw, dcb = _act_bwd(h, da, cw, cb, name=f"ffn_actb_{tag}")

    def shard_of(k):
        return (k % 2) * 2 + k // 2

    dx = _mm(dh, Wup, 'nn', bmode='abr', tm=512, tn=1024, tk=4096, name=f"ffn_dx_{tag}", b_map=shard_of)
    dWup = _mm(dh, x, 'tn', bmode='ao', tm=4096, tn=512, name=f"ffn_dwup_{tag}", out_dtype=BF16,
               o_map=shard_of)
    return dx, dWup, dWdn, dcw, dcb


def kernel(x, positions, ev_w_in, ev_b_f, ev_lambda_re, ev_lambda_im, ev_log_step, ev_ssm_b_re, ev_ssm_b_im, ev_ssm_c_re, ev_ssm_c_im, ev_ssm_d, ev_w_glu, ev_w_out, od_w_in, od_sinks, od_w_out, ln_mix_g, ln_mix_b, ffn_w_up, ffn_conv_w, ffn_conv_b, ffn_w_down, ln_ffn_g, ln_ffn_b, loss_target, m_ev_w_in, m_ev_b_f, m_ev_lambda_re, m_ev_lambda_im, m_ev_log_step, m_ev_ssm_b_re, m_ev_ssm_b_im, m_ev_ssm_c_re, m_ev_ssm_c_im, m_ev_ssm_d, m_ev_w_glu, m_ev_w_out, m_od_w_in, m_od_sinks, m_od_w_out, m_ln_mix_g, m_ln_mix_b, m_ffn_w_up, m_ffn_conv_w, m_ffn_conv_b, m_ffn_w_down, m_ln_ffn_g, m_ln_ffn_b, v_ev_w_in, v_ev_b_f, v_ev_lambda_re, v_ev_lambda_im, v_ev_log_step, v_ev_ssm_b_re, v_ev_ssm_b_im, v_ev_ssm_c_re, v_ev_ssm_c_im, v_ev_ssm_d, v_ev_w_glu, v_ev_w_out, v_od_w_in, v_od_sinks, v_od_w_out, v_ln_mix_g, v_ln_mix_b, v_ffn_w_up, v_ffn_conv_w, v_ffn_conv_b, v_ffn_w_down, v_ln_ffn_g, v_ln_ffn_b):
    W = dict(ev_w_in=ev_w_in, ev_b_f=ev_b_f, ev_lambda_re=ev_lambda_re, ev_lambda_im=ev_lambda_im, ev_log_step=ev_log_step, ev_ssm_b_re=ev_ssm_b_re, ev_ssm_b_im=ev_ssm_b_im, ev_ssm_c_re=ev_ssm_c_re, ev_ssm_c_im=ev_ssm_c_im, ev_ssm_d=ev_ssm_d, ev_w_glu=ev_w_glu, ev_w_out=ev_w_out, od_w_in=od_w_in, od_sinks=od_sinks, od_w_out=od_w_out, ln_mix_g=ln_mix_g, ln_mix_b=ln_mix_b, ffn_w_up=ffn_w_up, ffn_conv_w=ffn_conv_w, ffn_conv_b=ffn_conv_b, ffn_w_down=ffn_w_down, ln_ffn_g=ln_ffn_g, ln_ffn_b=ln_ffn_b)
    Mo = dict(ev_w_in=m_ev_w_in, ev_b_f=m_ev_b_f, ev_lambda_re=m_ev_lambda_re, ev_lambda_im=m_ev_lambda_im, ev_log_step=m_ev_log_step, ev_ssm_b_re=m_ev_ssm_b_re, ev_ssm_b_im=m_ev_ssm_b_im, ev_ssm_c_re=m_ev_ssm_c_re, ev_ssm_c_im=m_ev_ssm_c_im, ev_ssm_d=m_ev_ssm_d, ev_w_glu=m_ev_w_glu, ev_w_out=m_ev_w_out, od_w_in=m_od_w_in, od_sinks=m_od_sinks, od_w_out=m_od_w_out, ln_mix_g=m_ln_mix_g, ln_mix_b=m_ln_mix_b, ffn_w_up=m_ffn_w_up, ffn_conv_w=m_ffn_conv_w, ffn_conv_b=m_ffn_conv_b, ffn_w_down=m_ffn_w_down, ln_ffn_g=m_ln_ffn_g, ln_ffn_b=m_ln_ffn_b)
    Vo = dict(ev_w_in=v_ev_w_in, ev_b_f=v_ev_b_f, ev_lambda_re=v_ev_lambda_re, ev_lambda_im=v_ev_lambda_im, ev_log_step=v_ev_log_step, ev_ssm_b_re=v_ev_ssm_b_re, ev_ssm_b_im=v_ev_ssm_b_im, ev_ssm_c_re=v_ev_ssm_c_re, ev_ssm_c_im=v_ev_ssm_c_im, ev_ssm_d=v_ev_ssm_d, ev_w_glu=v_ev_w_glu, ev_w_out=v_ev_w_out, od_w_in=v_od_w_in, od_sinks=v_od_sinks, od_w_out=v_od_w_out, ln_mix_g=v_ln_mix_g, ln_mix_b=v_ln_mix_b, ffn_w_up=v_ffn_w_up, ffn_conv_w=v_ffn_conv_w, ffn_conv_b=v_ffn_conv_b, ffn_w_down=v_ffn_w_down, ln_ffn_g=v_ln_ffn_g, ln_ffn_b=v_ln_ffn_b)
    names = list(W.keys())
    big = ['ev_w_in', 'ev_w_glu', 'ev_w_out', 'od_w_in', 'od_w_out', 'ffn_w_up', 'ffn_w_down']

    S, D = x.shape[1], x.shape[2]
    x0 = x.reshape(S, D)
    tgt = loss_target.reshape(S, D)
    G, Pn, Cg = SSM_GROUPS, SSM_STATE, SSM_GROUP
    Fs = ffn_w_up.shape[2]
    FP = Fs
    Rd = ffn_w_down.shape[1]
    EIN = N_CHIPS * ev_w_in.shape[2]

    def as2d(a):
        return a.reshape(-1, a.shape[-1])

    cwl = ffn_conv_w.reshape(-1)
    cw_rows = _rup(_rup(cwl.shape[0], LANE) // LANE, 32)
    cw_pad = jnp.pad(cwl, (0, cw_rows * LANE - cwl.shape[0])).reshape(cw_rows, LANE)
    transposed = ('ev_w_in', 'ffn_w_up')

    def view(n, a):
        return jnp.transpose(a, (0, 2, 1)) if n in transposed else a

    Wv = {n: view(n, W[n]) for n in big}
    big_e = [(n, l) for n in big for l in range(W[n].shape[0])]
    split_cols = {e: (Wv[e[0]].shape[1] // 2) % 16 != 0 for e in big_e}
    shard16 = {e: Wv[e[0]][e[1]].astype(BF16) for e in big_e}
    grp_now = [e for e in big_e if e[0].startswith('ev_')]
    grp_ffn0 = [('ffn_w_up', 0), ('ffn_w_down', 0)]
    grp_l1 = [('od_w_in', 0), ('od_w_out', 0), ('ffn_w_up', 1), ('ffn_w_down', 1)]
    src_now = [shard16[e] for e in grp_now]
    src_ffn0 = [shard16[e] for e in grp_ffn0] + [cw_pad]
    src_l1 = [shard16[e] for e in grp_l1]
    cols_now = [split_cols[e] for e in grp_now]
    cols_ffn0 = [split_cols[e] for e in grp_ffn0] + [False]
    cols_l1 = [split_cols[e] for e in grp_l1]
    ag_now = _chip_exchange_start('gather', src_now, cols_now, name="ag_l0_start")
    ag_ffn0 = _chip_exchange_start('gather', src_ffn0, cols_ffn0, name="ag_ffn0_start", after=[ag_now[4]])
    ag_l1 = _chip_exchange_start('gather', src_l1, cols_l1, name="ag_l1_start", after=[ag_ffn0[4]])
    started = [ag_l1[4]]

    def finish_gather(started, srcs, cols, after, tag):
        send, recv, thru, lands, _ = started
        thru, lands = _chip_exchange_wait('gather', send, recv, thru, lands, cols, after, name=f"ag_{tag}_wait")
        lands = _sibling_pass_gathered(lands, [s.shape for s in srcs], cols, name=f"ag_{tag}_pass")
        return _own_slot(lands, [s[None] for s in thru])

    lam_r, lam_i = ev_lambda_re[0], ev_lambda_im[0]
    lstep = ev_log_step[0].reshape(G, 1)
    a_re, a_im, g_re, g_im = _s5_disc_fwd(lam_r, lam_i, lstep, name="s5_disc", after=started)
    b_re2, b_im2 = ev_ssm_b_re[0].reshape(G * Pn, Cg), ev_ssm_b_im[0].reshape(G * Pn, Cg)
    g_re1, g_im1 = g_re.reshape(G * Pn, 1), g_im.reshape(G * Pn, 1)
    bb_re, bb_im = _s5_bb_fwd(g_re1, g_im1, b_re2, b_im2, name="s5_bb")
    bbt = jnp.stack([jnp.transpose(b.reshape(G, Pn, Cg), (0, 2, 1)).reshape(G * Cg, Pn) for b in (bb_re, bb_im)])
    BB = _diag_expand(bbt, Cg, Pn, name="s5_bb_dense")
    cct = jnp.stack([jnp.transpose(ev_ssm_c_re[0], (0, 2, 1)).reshape(G * Pn, Cg),
                     jnp.transpose(-ev_ssm_c_im[0], (0, 2, 1)).reshape(G * Pn, Cg)])
    CC = _diag_expand(cct, Pn, Cg, name="s5_cc_dense", after=started)
    a_cat = jnp.stack([a_re.reshape(1, G * Pn), a_im.reshape(1, G * Pn)])
    dskip = ev_ssm_d[0].reshape(1, SSM_WIDTH)
    tabs = _rope_tables(positions.reshape(S, 1).astype(F32), name="rope_tables", after=[BB, CC])

    gw = dict(zip(grp_now, finish_gather(ag_now, src_now, cols_now, [tabs[2]], "l0")))
    gw.update({n: gw[(n, 0)] for n in big if (n, 0) in gw and W[n].shape[0] == 1})
    w_in_t = gw['ev_w_in'].reshape(EIN, D)
    qkv_w = 3 * FOX_WIDTH
    WmainT = jnp.concatenate([w_in_t[:qkv_w], w_in_t[qkv_w + FOX_HEADS:]], axis=0)
    WfT = jnp.pad(w_in_t[qkv_w:qkv_w + FOX_HEADS], ((0, LANE - FOX_HEADS), (0, 0)))
    Wglu = _cols_from_shards(gw['ev_w_glu'])
    Wout_ev = gw['ev_w_out'].reshape(D, D)
    cbs = [ffn_conv_b[l].reshape(N_CHIPS, Fs) for l in range(DEPTH)]

    P = _mm(x0, WmainT, 'nt', name="ev_proj")
    fl = _mm(x0, WfT, 'nt', name="ev_proj_f")
    bf_pad = jnp.pad(ev_b_f.reshape(1, FOX_HEADS), ((0, 0), (0, LANE - FOX_HEADS)))
    cgate, sgate = _gate_fwd(fl, bf_pad, name="fox_gate")
    ccol = jnp.transpose(cgate[:, :FOX_HEADS]).reshape(FOX_HEADS, S, 1)
    crow = jnp.transpose(cgate[:, :FOX_HEADS]).reshape(FOX_HEADS, 1, S)
    fox, lse = _fox_fwd(P, ccol, crow, name="fox_fwd")
    u_s5 = P[:, qkv_w:]
    bu = _mm(u_s5, BB, 'nn', bmode='bo', name="s5_bu")
    hh = _s5_scan_fwd(bu, a_cat, name="s5_scan")
    yc = _mm(hh, CC, 'nn', bmode='abr', name="s5_y")
    y_s5, yg = _s5_out_fwd(yc, P, dskip, name="s5_out")
    z = _mm(yg, Wglu, 'nn', name="s5_glu_proj")
    ssm = _glu_fwd(z, name="s5_glu")
    cat = jnp.concatenate([fox.astype(BF16), ssm], axis=1)
    mix0 = _mm(cat, Wout_ev, 'nn', name="ev_out")
    x1, xh1, rs1 = _add_ln_fwd(x0, mix0, ln_mix_g[0], ln_mix_b[0], name="ln_mix0")
    got = finish_gather(ag_ffn0, src_ffn0, cols_ffn0, [x1], "ffn0")
    gw.update(zip(grp_ffn0, got[:-1]))
    cw_all = got[-1].reshape(N_CHIPS, -1)[:, :cwl.shape[0]].reshape(N_CHIPS, DEPTH, 3, Fs)
    cws = [cw_all[:, l] for l in range(DEPTH)]
    Wup = {0: gw[('ffn_w_up', 0)]}
    Wdn = {0: gw[('ffn_w_down', 0)].reshape(2, Fs, D)}
    f0, hf0, af0 = _ffn_fwd(x1, Wup[0], Wdn[0], cws[0], cbs[0], "l0")
    x2, xh2, rs2 = _add_ln_fwd(x1, f0, ln_ffn_g[0], ln_ffn_b[0], name="ln_ffn0")

    gw.update(zip(grp_l1, finish_gather(ag_l1, src_l1, cols_l1, [x2], "l1")))
    Wodin = _cols_from_shards(gw[('od_w_in', 0)])
    Wodout = gw[('od_w_out', 0)].reshape(D, D)
    Wup[1] = gw[('ffn_w_up', 1)]
    Wdn[1] = gw[('ffn_w_down', 1)].reshape(2, Fs, D)
    QW, KW = SWA_HEADS * SWA_HEAD_DIM, SWA_KV_HEADS * SWA_HEAD_DIM
    P1 = _mm(x2, Wodin, 'nn', name="od_proj")
    qr = _rope_apply(P1, tabs, col0=0, width=QW, inverse=False, name="rope_q", out_dtype=BF16)
    kr = _rope_apply(P1, tabs, col0=QW, width=KW, inverse=False, name="rope_k", out_dtype=BF16)

    def heads(a2, nh):
        return jnp.transpose(a2.reshape(S, nh, SWA_HEAD_DIM), (1, 0, 2))

    def unheads(a3):
        return jnp.transpose(a3, (1, 0, 2)).reshape(S, -1)

    qT, kT = heads(qr, SWA_HEADS), heads(kr, SWA_KV_HEADS)
    vT = heads(P1[:, QW + KW:].astype(BF16), SWA_KV_HEADS)
    sink_rows = jnp.broadcast_to(od_sinks[0].reshape(SWA_KV_HEADS, SWA_GROUPS, 1, 1),
                                 (SWA_KV_HEADS, SWA_GROUPS, SWA_WINDOW, 1)).reshape(SWA_KV_HEADS, -1, 1)
    oT, Lsw = _swa_fwd(qT, kT, vT, sink_rows, name="swa_fwd")
    o_sw = unheads(oT).astype(BF16)
    mix1 = _mm(o_sw, Wodout, 'nn', name="od_out")
    x3, xh3, rs3 = _add_ln_fwd(x2, mix1, ln_mix_g[1], ln_mix_b[1], name="ln_mix1")
    f1, hf1, af1 = _ffn_fwd(x3, Wup[1], Wdn[1], cws[1], cbs[1], "l1")
    x4, xh4, rs4 = _add_ln_fwd(x3, f1, ln_ffn_g[1], ln_ffn_b[1], name="ln_ffn1")
    dy, loss_part = _loss_grad(x4, tgt, name="loss")

    dz4, dg_ffn1, db_ffn1 = _ln_bwd(dy, None, xh4, rs4, ln_ffn_g[1], name="lnb_ffn1")
    dx3f, dWup1, dWdn1, dcw1, dcb1 = _ffn_bwd(dz4, x3, hf1, af1, Wup[1], Wdn[1], cws[1], cbs[1], "l1")
    sib_ffn1 = _sibling_halves_start([dWup1, dWdn1.reshape(N_CHIPS, Rd, D)], [False, False], name="rs_ffn1_sib_start")
    dz3, dg_mix1, db_mix1 = _ln_bwd(dz4, dx3f, xh3, rs3, ln_mix_g[1], name="lnb_mix1", after=[sib_ffn1[4]])
    do_sw = _mm(dz3, Wodout, 'nt', name="od_out_dx")
    dWodout = _mm(o_sw, dz3, 'tn', name="od_out_dw", out_dtype=BF16)
    doT = heads(do_sw, SWA_HEADS)
    dqT, dkT, dvT, dsink = _swa_bwd(qT, kT, vT, sink_rows, oT, Lsw, doT, name="swa_bwd")
    dq1 = _rope_apply(unheads(dqT), tabs, col0=0, width=QW, inverse=True, name="rope_dq", out_dtype=BF16)
    dk1 = _rope_apply(unheads(dkT[:, SWA_WINDOW:]), tabs, col0=0, width=KW, inverse=True, name="rope_dk",
                      out_dtype=BF16)
    dP1 = jnp.concatenate([dq1, dk1, unheads(dvT[:, SWA_WINDOW:]).astype(BF16)], axis=1)
    dx2m = _mm(dP1, Wodin, 'nt', name="od_proj_dx")
    dWodin = _mm(x2, dP1, 'tn', name="od_proj_dw", out_dtype=BF16)

    def rs_begin(entries, grads, tag):
        cols = [split_cols[e] for e in entries]
        sib = _sibling_send_halves(grads, cols, name=f"rs_{tag}_sibling")
        return [_sum2_halves(g4, s4, bc, name=f"rs_sum2_{n}{l}")
                for (n, l), g4, s4, bc in zip(entries, grads, sib, cols)]

    def rs_begin_started(entries, started, after, tag):
        send, rcv, thru, lands, _ = started
        thru, lands = _sibling_halves_wait(send, rcv, thru, lands, [False] * len(thru), after,
                                           name=f"rs_{tag}_sib_wait")
        return [_sum2_halves(g4, s4, False, name=f"rs_sum2_{n}{l}") for (n, l), g4, s4 in zip(entries, thru, lands)]

    def own_parts(parts):
        me = 2 * lax.axis_index("x") + lax.axis_index("y")
        return [lax.dynamic_slice_in_dim(p, me, 1, axis=0) for p in parts]

    part_l1 = (rs_begin(grp_l1[:2], [_shards_from_cols(dWodin), dWodout.reshape(N_CHIPS, D // N_CHIPS, D)], "od")
               + rs_begin_started(grp_l1[2:], sib_ffn1, [dWodin], "ffn1"))
    rs_l1 = _chip_exchange_start('scatter', part_l1, [False] * len(part_l1), name="rs_l1_start")

    dz2, dg_ffn0, db_ffn0 = _ln_bwd(dz3, dx2m, xh2, rs2, ln_ffn_g[0], name="lnb_ffn0", after=[rs_l1[4]])
    dx1f, dWup0, dWdn0, dcw0, dcb0 = _ffn_bwd(dz2, x1, hf0, af0, Wup[0], Wdn[0], cws[0], cbs[0], "l0")
    sib_ffn0 = _sibling_halves_start([dWup0, dWdn0.reshape(N_CHIPS, Rd, D)], [False, False], name="rs_ffn0_sib_start")
    dz1, dg_mix0, db_mix0 = _ln_bwd(dz2, dx1f, xh1, rs1, ln_mix_g[0], name="lnb_mix0", after=[sib_ffn0[4]])
    dcat = _mm(dz1, Wout_ev, 'nt', name="ev_out_dx")
    dWout_ev = _mm(cat, dz1, 'tn', name="ev_out_dw", out_dtype=BF16)
    part_ffn0 = rs_begin_started(grp_ffn0, sib_ffn0, [dWout_ev], "ffn0")
    rs_ffn0 = _chip_exchange_start('scatter', part_ffn0, [False] * len(part_ffn0), name="rs_ffn0_start")
    dz = _glu_bwd(z, dcat, name="s5_glu_bwd")
    dyg = _mm(dz, Wglu, 'nt', name="s5_glu_dx", after=[rs_ffn0[4]])
    dWglu = _mm(yg, dz, 'tn', name="s5_glu_dw", out_dtype=BF16)
    dy_s5, du_dir, dD = _s5_out_bwd(dyg, y_s5, P, dskip, name="s5_out_bwd")
    dhh = _mm(dy_s5, CC, 'nt', bmode='bo', name="s5_y_dx")
    dCC = _mm(hh, dy_s5, 'tn', bmode='ao', name="s5_y_dw")
    lam, da_s5 = _s5_scan_bwd(dhh, hh, a_cat, name="s5_scan_bwd")
    du_bu = _mm(lam, BB, 'nt', bmode='abr', name="s5_bu_dx")
    dBB = _mm(u_s5, lam, 'tn', bmode='bo', name="s5_bu_dw")
    du = _combine([du_dir, du_bu], [1.0, 1.0], name="s5_du", out_dtype=BF16)
    dq0, dk0, dv0, dccol, dcrow = _fox_bwd(P, ccol, crow, fox, lse, dcat, name="fox_bwd")
    dc = jnp.transpose((dccol.reshape(FOX_HEADS, S) - dcrow.reshape(FOX_HEADS, S)))
    dc = jnp.pad(dc, ((0, 0), (0, LANE - FOX_HEADS)))
    dfl, dbf = _gate_bwd(dc, sgate, name="fox_gate_bwd")
    dP = jnp.concatenate([dq0, dk0, dv0, du], axis=1)
    dx0a = _mm(dP, WmainT, 'nn', name="ev_proj_dx")
    dx0b = _mm(dfl, WfT, 'nn', name="ev_proj_f_dx")
    dWmainT = _mm(dP, x0, 'tn', tm=1024, tn=1024, name="ev_proj_dw", out_dtype=BF16)
    dWfT = _mm(dfl, x0, 'tn', name="ev_proj_f_dw", out_dtype=BF16)
    grad_x = _combine([dz1, dx0a, dx0b], [ALPHA, 1.0, 1.0], name="grad_x")

    dbbt = _diag_extract(dBB, Cg, Pn, name="s5_bb_diag")
    dcct = _diag_extract(dCC, Pn, Cg, name="s5_cc_diag")
    dbb_re = jnp.transpose(dbbt[0].reshape(G, Cg, Pn), (0, 2, 1)).reshape(G * Pn, Cg)
    dbb_im = jnp.transpose(dbbt[1].reshape(G, Cg, Pn), (0, 2, 1)).reshape(G * Pn, Cg)
    db_re, db_im, dg_re1, dg_im1 = _s5_bb_bwd(g_re1, g_im1, b_re2, b_im2, dbb_re, dbb_im, name="s5_bb_bwd")
    dlam_re, dlam_im, dlstep = _s5_disc_bwd(lam_r, lam_i, lstep, da_s5[0].reshape(G, Pn), da_s5[1].reshape(G, Pn),
                                            dg_re1.reshape(G, Pn), dg_im1.reshape(G, Pn), name="s5_disc_bwd")
    dc_re = jnp.transpose(dcct[0].reshape(G, Pn, Cg), (0, 2, 1))
    dc_im = -jnp.transpose(dcct[1].reshape(G, Pn, Cg), (0, 2, 1))

    def conv_w_full(d0, d1):
        return jnp.stack([jnp.reshape(jnp.transpose(d[:, :, :Fs], (1, 0, 2)), (3, N_CHIPS * Fs)) for d in (d0, d1)])

    def conv_b_full(d0, d1):
        return jnp.stack([jnp.reshape(d[:, 0, :Fs], (N_CHIPS * Fs,)) for d in (d0, d1)])

    small_local = dict(
        ev_b_f=dbf[:, :FOX_HEADS], ev_lambda_re=dlam_re, ev_lambda_im=dlam_im, ev_log_step=dlstep,
        ev_ssm_b_re=db_re, ev_ssm_b_im=db_im, ev_ssm_c_re=dc_re, ev_ssm_c_im=dc_im, ev_ssm_d=dD,
        od_sinks=dsink[:, :, 0],
        ln_mix_g=jnp.concatenate([dg_mix0, dg_mix1]), ln_mix_b=jnp.concatenate([db_mix0, db_mix1]),
        ffn_conv_w=conv_w_full(dcw0, dcw1), ffn_conv_b=conv_b_full(dcb0, dcb1),
        ln_ffn_g=jnp.concatenate([dg_ffn0, dg_ffn1]), ln_ffn_b=jnp.concatenate([db_ffn0, db_ffn1]))
    small = list(small_local.keys())
    red = _all_reduce_small(_pack([small_local[n] for n in small] + [loss_part]), name="ar_small")
    full_shapes = [W[n].shape if n != 'ffn_conv_w' else (DEPTH, 3, N_CHIPS * Fs) for n in small]
    pieces = _unpack(red, full_shapes + [()])
    loss = pieces[-1]
    gsmall = dict(zip(small, pieces[:-1]))
    chip = 2 * lax.axis_index("x") + lax.axis_index("y")
    gsmall['ffn_conv_w'] = lax.dynamic_slice_in_dim(gsmall['ffn_conv_w'], chip * Fs, Fs, axis=2)
    shapes = [W[n].shape for n in small]
    gs, ds_, ms, vs = _adamw(_pack([W[n] for n in small])[None], _pack([gsmall[n] for n in small])[None],
                             _pack([Mo[n] for n in small])[None], _pack([Vo[n] for n in small])[None],
                             name="adamw_small", tr=1 << 14)
    out_g = dict(zip(small, _unpack(gs, shapes)))
    out_d = dict(zip(small, _unpack(ds_, shapes)))
    out_m = dict(zip(small, _unpack(ms, shapes)))
    out_v = dict(zip(small, _unpack(vs, shapes)))

    dw_in_t = jnp.concatenate([dWmainT[:qkv_w], dWfT[:FOX_HEADS], dWmainT[qkv_w:]], axis=0)
    part_now = rs_begin(grp_now, [dw_in_t.reshape(N_CHIPS, EIN // N_CHIPS, D), _shards_from_cols(dWglu),
                                  dWout_ev.reshape(N_CHIPS, D // N_CHIPS, D)], "l0")
    rs_now = _chip_exchange_start('scatter', part_now, [False] * len(part_now), name="rs_l0_start")

    def finish_scatter(started, parts, after, tag):
        send, rcv, thru, lands, _ = started
        thru, lands = _chip_exchange_wait('scatter', send, rcv, thru, lands, [False] * len(parts), after,
                                          name=f"rs_{tag}_wait")
        return _own_slot(lands, own_parts(thru))

    def update(entries, recv, tag):
        halves = [_rowsum(r, name=f"rs_sum4_{e[0]}{e[1]}") for e, r in zip(entries, recv)]
        others = _sibling_join_halves(halves, name=f"rs_{tag}_join")
        pairs = dict(zip(entries, zip(halves, others)))
        done = []
        for n in dict.fromkeys(e[0] for e in entries):
            res = _adamw(Wv[n], [pairs[(n, l)] for l in range(W[n].shape[0])], view(n, Mo[n]), view(n, Vo[n]),
                         name=f"adamw_{n}", by_cols=split_cols[(n, 0)])
            out_g[n], out_d[n], out_m[n], out_v[n] = (view(n, t) for t in res)
            done.append(res[3])
        return done

    recv_rest = (finish_scatter(rs_l1, part_l1, [rs_now[4]], "l1")
                 + finish_scatter(rs_ffn0, part_ffn0, [rs_now[4]], "ffn0"))
    done = update(grp_l1 + grp_ffn0, recv_rest, "rest")
    update(grp_now, finish_scatter(rs_now, part_now, done, "l0"), "l0")

    return (loss, grad_x.reshape(1, S, D), *[out_g[n] for n in names], *[out_d[n] for n in names],
            *[out_m[n] for n in names], *[out_v[n] for n in names])
```

```python
import functools
import math

import numpy as np
import jax
import jax.numpy as jnp
from jax import lax
from jax.experimental import pallas as pl
from jax.experimental.pallas import tpu as pltpu

F32 = jnp.float32
BF16 = jnp.bfloat16
MESH = pl.DeviceIdType.MESH
ANY = pl.BlockSpec(memory_space=pl.ANY)

D_MODEL = 2048
FOX_HEADS = 8
FOX_HEAD_DIM = 128
FOX_WIDTH = 1024
SSM_WIDTH = 1024
SSM_GROUP = 16
SSM_GROUPS = 64
SSM_STATE = 64
SWA_HEADS = 32
SWA_KV_HEADS = 4
SWA_HEAD_DIM = 64
SWA_GROUPS = 8
SWA_WINDOW = 128
ROPE_DIM = 16
ROPE_THETA = 500000.0
LN_EPS = 1e-5
DEPTH = 2
ALPHA = (2.0 * DEPTH) ** 0.25
ADAM_LR = 0.001
ADAM_B1 = 0.9
ADAM_B2 = 0.999
ADAM_EPS = 1e-08
ADAM_WD = 0.01
ADAM_STEP = 10
N_CHIPS = 4

VMEM_LIMIT = 56 * 1024 * 1024
LANE = 128


def _call(body, after=(), **kw):
    if after:
        n = len(after)

        def shifted(*refs):
            return body(*refs[n:])

        call = _call(shifted, **dict(kw, in_specs=[ANY] * n + list(kw["in_specs"])))
        return lambda *args: call(*after, *args)
    return pl.pallas_call(body, **kw)


def _cparams(sem):
    return pltpu.CompilerParams(dimension_semantics=sem, vmem_limit_bytes=VMEM_LIMIT)


def _rup(n, m):
    return (n + m - 1) // m * m


def _pick(n, pref):
    if n <= pref:
        return n
    for step in (128, 16, 8):
        for t in range(pref - pref % step, 0, -step):
            if n % t == 0:
                return t
    return n


def _tile2d(rows, cols, pref_rows=256, budget=256 * 1024):
    tr = _pick(rows, pref_rows)
    if tr < 64:
        tr = rows
    if cols % LANE:
        return tr, cols
    return tr, _pick(cols, max(LANE, budget // tr // LANE * LANE))


def _mm(a, b, mode, *, name, tm=512, tn=1024, tk=2048, bmode=None, out_dtype=F32, after=(), b_map=None,
        o_map=None, diag=None):
    a3 = a if a.ndim == 3 else a[None]
    b3 = b if b.ndim == 3 else b[None]
    if mode == 'tn':
        K, M = a3.shape[1:]
    else:
        M, K = a3.shape[1:]
    N = b3.shape[1] if mode == 'nt' else b3.shape[2]
    tm, tn, tk = _pick(M, tm), _pick(N, tn), _pick(K, tk)
    nb = max(a3.shape[0], b3.shape[0])
    nbo, nbr = (1, nb) if bmode == 'abr' else (nb, 1)
    nm, nk = M // tm, K // tk
    if diag == 'kn':
        assert K // tk == N // tn
        nk = 1
    if diag == 'mn':
        assert M // tm == N // tn
        nm = 1
    nred = nbr * nk
    a_b = bmode in ('ao', 'abr')
    b_b = bmode in ('bo', 'abr')
    o_b = bmode in ('bo', 'ao')

    def bsel(flag, bo, br, remap=None):
        if not flag:
            return 0
        return (bo + br) if remap is None else remap(bo + br)

    def mi(i, j):
        return j if diag == 'mn' else i

    def ki(j, k):
        return j if diag == 'kn' else k

    if mode == 'tn':
        a_spec = pl.BlockSpec((None, tk, tm), lambda bo, i, j, br, k: (bsel(a_b, bo, br), ki(j, k), mi(i, j)))
    else:
        a_spec = pl.BlockSpec((None, tm, tk), lambda bo, i, j, br, k: (bsel(a_b, bo, br), mi(i, j), ki(j, k)))
    if mode == 'nt':
        b_spec = pl.BlockSpec((None, tn, tk), lambda bo, i, j, br, k: (bsel(b_b, bo, br, b_map), j, ki(j, k)))
    else:
        b_spec = pl.BlockSpec((None, tk, tn), lambda bo, i, j, br, k: (bsel(b_b, bo, br, b_map), ki(j, k), j))
    o_spec = pl.BlockSpec((None, tm, tn), lambda bo, i, j, br, k: (bsel(o_b, bo, br, o_map), mi(i, j), j))
    dn = {'nn': (((1,), (0,)), ((), ())), 'nt': (((1,), (1,)), ((), ())), 'tn': (((0,), (0,)), ((), ()))}[mode]

    def body(a_ref, b_ref, *rest):
        o_ref, scratch = rest[len(after)], rest[len(after) + 1:]
        r = lax.dot_general(a_ref[...].astype(BF16), b_ref[...].astype(BF16), dn, preferred_element_type=F32)
        if nred == 1:
            o_ref[...] = r.astype(out_dtype)
        else:
            acc = scratch[0]
            step = pl.program_id(3) * nk + pl.program_id(4)

            @pl.when(step == 0)
            def _():
                acc[...] = r

            @pl.when(step > 0)
            def _():
                acc[...] += r

            @pl.when(step == nred - 1)
            def _():
                o_ref[...] = acc[...].astype(out_dtype)

    out = _call(
        body, name=name,
        grid=(nbo, nm, N // tn, nbr, nk),
        in_specs=[a_spec, b_spec] + [ANY] * len(after), out_specs=o_spec,
        out_shape=jax.ShapeDtypeStruct((nbo if o_b else 1, M, N), out_dtype),
        scratch_shapes=[] if nred == 1 else [pltpu.VMEM((tm, tn), F32)],
        compiler_params=_cparams(("parallel", "parallel", "parallel", "arbitrary", "arbitrary")),
    )(a3, b3, *after)
    return out if o_b else out[0]


def _add_ln_fwd(x, r, g, b, *, name):
    S, D = x.shape
    tr = _pick(S, 256)

    def body(x_ref, r_ref, g_ref, b_ref, o_ref, xh_ref, rs_ref):
        z = ALPHA * x_ref[...] + r_ref[...]
        mu = jnp.mean(z, axis=-1, keepdims=True)
        zc = z - mu
        var = jnp.mean(zc * zc, axis=-1, keepdims=True)
        rstd = lax.rsqrt(var + LN_EPS)
        xh = zc * rstd
        xh_ref[...] = xh
        rs_ref[...] = rstd
        o_ref[...] = xh * g_ref[...] + b_ref[...]

    row = pl.BlockSpec((tr, D), lambda i: (i, 0))
    vec = pl.BlockSpec((1, D), lambda i: (0, 0))
    return _call(
        body, name=name, grid=(S // tr,),
        in_specs=[row, row, vec, vec],
        out_specs=[row, row, pl.BlockSpec((tr, 1), lambda i: (i, 0))],
        out_shape=[jax.ShapeDtypeStruct((S, D), F32), jax.ShapeDtypeStruct((S, D), F32),
                   jax.ShapeDtypeStruct((S, 1), F32)],
        compiler_params=_cparams(("parallel",)),
    )(x, r, g.reshape(1, D), b.reshape(1, D))


def _ln_bwd(da, db, xhat, rstd, g, *, name, after=()):
    S, D = xhat.shape
    tr = _pick(S, 256)
    two = db is not None

    def body(*refs):
        refs = refs[len(after):]
        if two:
            da_ref, db_ref, xh_ref, rs_ref, g_ref, dz_ref, dg_ref, dbt_ref = refs
            dy = ALPHA * da_ref[...] + db_ref[...]
        else:
            da_ref, xh_ref, rs_ref, g_ref, dz_ref, dg_ref, dbt_ref = refs
            dy = da_ref[...]
        xh = xh_ref[...]
        dxh = dy * g_ref[...]
        m1 = jnp.mean(dxh, axis=-1, keepdims=True)
        m2 = jnp.mean(dxh * xh, axis=-1, keepdims=True)
        dz_ref[...] = rs_ref[...] * (dxh - m1 - xh * m2)
        pg = jnp.sum(dy * xh, axis=0, keepdims=True)
        pb = jnp.sum(dy, axis=0, keepdims=True)

        @pl.when(pl.program_id(0) == 0)
        def _():
            dg_ref[...] = pg
            dbt_ref[...] = pb

        @pl.when(pl.program_id(0) > 0)
        def _():
            dg_ref[...] += pg
            dbt_ref[...] += pb

    row = pl.BlockSpec((tr, D), lambda i: (i, 0))
    vec = pl.BlockSpec((1, D), lambda i: (0, 0))
    ins = list(after) + [da] + ([db] if two else []) + [xhat, rstd, g.reshape(1, D)]
    in_specs = [ANY] * len(after) + [row] + ([row] if two else []) + [row, pl.BlockSpec((tr, 1), lambda i: (i, 0)), vec]
    return _call(
        body, name=name, grid=(S // tr,),
        in_specs=in_specs, out_specs=[row, vec, vec],
        out_shape=[jax.ShapeDtypeStruct((S, D), F32), jax.ShapeDtypeStruct((1, D), F32),
                   jax.ShapeDtypeStruct((1, D), F32)],
        compiler_params=_cparams(("arbitrary",)),
    )(*ins)


def _loss_grad(y, t, *, name):
    S, D = y.shape
    tr = _pick(S, 256)

    def body(y_ref, t_ref, dy_ref, l_ref):
        e = y_ref[...] - t_ref[...]
        dy_ref[...] = e * (1.0 / D)
        part = 0.5 * jnp.sum(jnp.sum(e * e, axis=-1, keepdims=True) * (1.0 / D), axis=0, keepdims=True)

        @pl.when(pl.program_id(0) == 0)
        def _():
            l_ref[...] = part

        @pl.when(pl.program_id(0) > 0)
        def _():
            l_ref[...] += part

    row = pl.BlockSpec((tr, D), lambda i: (i, 0))
    return _call(
        body, name=name, grid=(S // tr,), in_specs=[row, row],
        out_specs=[row, pl.BlockSpec((1, 1), lambda i: (0, 0))],
        out_shape=[jax.ShapeDtypeStruct((S, D), F32), jax.ShapeDtypeStruct((1, 1), F32)],
        compiler_params=_cparams(("arbitrary",)),
    )(y, t)


def _combine(terms, scales, *, name, out_dtype=F32):
    S, D = terms[0].shape
    tr = _pick(S, 256)
    n = len(terms)

    def body(*refs):
        acc = scales[0] * refs[0][...].astype(F32)
        for i in range(1, n):
            acc = acc + scales[i] * refs[i][...].astype(F32)
        refs[n][...] = acc.astype(out_dtype)

    row = pl.BlockSpec((tr, D), lambda i: (i, 0))
    return _call(
        body, name=name, grid=(S // tr,), in_specs=[row] * n, out_specs=row,
        out_shape=jax.ShapeDtypeStruct((S, D), out_dtype),
        compiler_params=_cparams(("parallel",)),
    )(*terms)


def _split3(x):
    h = x.astype(BF16)
    r = x - h.astype(F32)
    m = r.astype(BF16)
    l = (r - m.astype(F32)).astype(BF16)
    return h, m, l


def _tri_matmul(tri_bf, x):
    h, m, l = _split3(x)
    dn = (((1,), (0,)), ((), ()))
    return (lax.dot_general(tri_bf, l, dn, preferred_element_type=F32)
            + lax.dot_general(tri_bf, m, dn, preferred_element_type=F32)
            + lax.dot_general(tri_bf, h, dn, preferred_element_type=F32))


def _gate_fwd(fl, bf, *, name):
    S = fl.shape[0]
    tc = _pick(S, 256)
    nchunk = S // tc

    def body(fl_ref, bf_ref, c_ref, sg_ref):
        r = lax.broadcasted_iota(jnp.int32, (tc, tc), 0)
        cidx = lax.broadcasted_iota(jnp.int32, (tc, tc), 1)
        tri = (r >= cidx).astype(BF16)
        carry = jnp.zeros((1, LANE), F32)
        for ch in range(nchunk):
            x = fl_ref[pl.ds(ch * tc, tc), :] + bf_ref[...]
            lf = jnp.minimum(x, 0.0) - jnp.log(1.0 + jnp.exp(-jnp.abs(x)))
            sg_ref[pl.ds(ch * tc, tc), :] = jax.nn.sigmoid(-x)
            c_ref[pl.ds(ch * tc, tc), :] = _tri_matmul(tri, lf) + carry
            carry = carry + jnp.sum(lf, axis=0, keepdims=True)

    full = pl.BlockSpec((S, LANE), lambda: (0, 0))
    return _call(
        body, name=name, in_specs=[full, pl.BlockSpec((1, LANE), lambda: (0, 0))], out_specs=[full, full],
        out_shape=[jax.ShapeDtypeStruct((S, LANE), F32)] * 2,
        compiler_params=pltpu.CompilerParams(vmem_limit_bytes=VMEM_LIMIT),
    )(fl, bf)


def _gate_bwd(dc, sg, *, name):
    S = dc.shape[0]
    tc = _pick(S, 256)
    nchunk = S // tc

    def body(dc_ref, sg_ref, dfl_ref, db_ref):
        r = lax.broadcasted_iota(jnp.int32, (tc, tc), 0)
        cidx = lax.broadcasted_iota(jnp.int32, (tc, tc), 1)
        tri = (r <= cidx).astype(BF16)
        carry = jnp.zeros((1, LANE), F32)
        dbacc = jnp.zeros((1, LANE), F32)
        for ch in reversed(range(nchunk)):
            d = dc_ref[pl.ds(ch * tc, tc), :]
            dfl = (_tri_matmul(tri, d) + carry) * sg_ref[pl.ds(ch * tc, tc), :]
            dfl_ref[pl.ds(ch * tc, tc), :] = dfl
            dbacc = dbacc + jnp.sum(dfl, axis=0, keepdims=True)
            carry = carry + jnp.sum(d, axis=0, keepdims=True)
        db_ref[...] = dbacc

    full = pl.BlockSpec((S, LANE), lambda: (0, 0))
    return _call(
        body, name=name, in_specs=[full, full], out_specs=[full, pl.BlockSpec((1, LANE), lambda: (0, 0))],
        out_shape=[jax.ShapeDtypeStruct((S, LANE), F32), jax.ShapeDtypeStruct((1, LANE), F32)],
        compiler_params=pltpu.CompilerParams(vmem_limit_bytes=VMEM_LIMIT),
    )(dc, sg)


def _fox_scores(q_ref, k_ref, cc_ref, cr_ref, qi, tq, S):
    scale = 1.0 / math.sqrt(FOX_HEAD_DIM)
    s = lax.dot_general(q_ref[...].astype(BF16), k_ref[...].astype(BF16), (((1,), (1,)), ((), ())),
                        preferred_element_type=F32) * scale
    s = s + cc_ref[...] - cr_ref[...]
    row = lax.broadcasted_iota(jnp.int32, (tq, S), 0) + qi * tq
    col = lax.broadcasted_iota(jnp.int32, (tq, S), 1)
    return s, row >= col


def _fox_fwd(P, ccol, crow, *, name):
    S = P.shape[0]
    tq = _pick(S, 256)
    H = FOX_HEADS

    def body(q_ref, k_ref, v_ref, cc_ref, cr_ref, o_ref, l_ref):
        s, causal = _fox_scores(q_ref, k_ref, cc_ref, cr_ref, pl.program_id(1), tq, S)
        s = jnp.where(causal, s, -1e30)
        m = jnp.max(s, axis=-1, keepdims=True)
        e = jnp.exp(s - m)
        den = jnp.sum(e, axis=-1, keepdims=True)
        p = e / den
        o_ref[...] = jnp.dot(p.astype(BF16), v_ref[...].astype(BF16), preferred_element_type=F32)
        l_ref[...] = m + jnp.log(den)

    return _call(
        body, name=name, grid=(H, S // tq),
        in_specs=[pl.BlockSpec((tq, 128), lambda h, i: (i, h)),
                  pl.BlockSpec((S, 128), lambda h, i: (0, H + h)),
                  pl.BlockSpec((S, 128), lambda h, i: (0, 2 * H + h)),
                  pl.BlockSpec((None, tq, 1), lambda h, i: (h, i, 0)),
                  pl.BlockSpec((None, 1, S), lambda h, i: (h, 0, 0))],
        out_specs=[pl.BlockSpec((tq, 128), lambda h, i: (i, h)),
                   pl.BlockSpec((None, tq, 1), lambda h, i: (h, i, 0))],
        out_shape=[jax.ShapeDtypeStruct((S, FOX_WIDTH), F32), jax.ShapeDtypeStruct((H, S, 1), F32)],
        compiler_params=_cparams(("parallel", "parallel")),
    )(P, P, P, ccol, crow)


def _fox_bwd(P, ccol, crow, o, lse, dcat, *, name):
    S = P.shape[0]
    tq = _pick(S, 256)
    H = FOX_HEADS
    nq = S // tq
    scale = 1.0 / math.sqrt(FOX_HEAD_DIM)

    def body(q_ref, k_ref, v_ref, cc_ref, cr_ref, o_ref, l_ref, do_ref,
             dq_ref, dk_ref, dv_ref, dcc_ref, dcr_ref, dk_acc, dv_acc):
        qi = pl.program_id(1)
        s, causal = _fox_scores(q_ref, k_ref, cc_ref, cr_ref, qi, tq, S)
        p = jnp.where(causal, jnp.exp(s - l_ref[...]), 0.0)
        do = do_ref[...]
        do_bf = do.astype(BF16)
        dp = lax.dot_general(do_bf, v_ref[...].astype(BF16), (((1,), (1,)), ((), ())), preferred_element_type=F32)
        delta = jnp.sum(do * o_ref[...], axis=-1, keepdims=True)
        ds = p * (dp - delta)
        ds_bf = ds.astype(BF16)
        dq_ref[...] = (jnp.dot(ds_bf, k_ref[...].astype(BF16), preferred_element_type=F32) * scale).astype(BF16)
        dkp = lax.dot_general(ds_bf, q_ref[...].astype(BF16), (((0,), (0,)), ((), ())),
                              preferred_element_type=F32) * scale
        dvp = lax.dot_general(p.astype(BF16), do_bf, (((0,), (0,)), ((), ())), preferred_element_type=F32)
        dcc_ref[...] = jnp.sum(ds, axis=-1, keepdims=True)
        dcr = jnp.sum(ds, axis=0, keepdims=True)

        @pl.when(qi == 0)
        def _():
            dk_acc[...] = dkp
            dv_acc[...] = dvp
            dcr_ref[...] = dcr

        @pl.when(qi > 0)
        def _():
            dk_acc[...] += dkp
            dv_acc[...] += dvp
            dcr_ref[...] += dcr

        @pl.when(qi == nq - 1)
        def _():
            dk_ref[...] = dk_acc[...].astype(BF16)
            dv_ref[...] = dv_acc[...].astype(BF16)

    qblk = pl.BlockSpec((tq, 128), lambda h, i: (i, h))
    kvo = pl.BlockSpec((S, 128), lambda h, i: (0, h))
    col = pl.BlockSpec((None, tq, 1), lambda h, i: (h, i, 0))
    rowv = pl.BlockSpec((None, 1, S), lambda h, i: (h, 0, 0))
    return _call(
        body, name=name, grid=(H, nq),
        in_specs=[qblk,
                  pl.BlockSpec((S, 128), lambda h, i: (0, H + h)),
                  pl.BlockSpec((S, 128), lambda h, i: (0, 2 * H + h)),
                  col, rowv, qblk, col, qblk],
        out_specs=[qblk, kvo, kvo, col, rowv],
        out_shape=[jax.ShapeDtypeStruct((S, FOX_WIDTH), BF16)] * 3
        + [jax.ShapeDtypeStruct((H, S, 1), F32), jax.ShapeDtypeStruct((H, 1, S), F32)],
        scratch_shapes=[pltpu.VMEM((S, 128), F32), pltpu.VMEM((S, 128), F32)],
        compiler_params=_cparams(("parallel", "arbitrary")),
    )(P, P, P, ccol, crow, o, lse, dcat)


def _s5_disc_fwd(lr, li, ls, *, name, after=()):
    G, Pn = lr.shape

    def body(lr_ref, li_ref, ls_ref, ar_ref, ai_ref, gr_ref, gi_ref):
        lr_, li_ = lr_ref[...], li_ref[...]
        dt = jnp.exp(ls_ref[...])
        mag = jnp.exp(lr_ * dt)
        th = li_ * dt
        ar = mag * jnp.cos(th)
        ai = mag * jnp.sin(th)
        den = lr_ * lr_ + li_ * li_
        xr = ar - 1.0
        ar_ref[...] = ar
        ai_ref[...] = ai
        gr_ref[...] = (xr * lr_ + ai * li_) / den
        gi_ref[...] = (ai * lr_ - xr * li_) / den

    sq = pl.BlockSpec((G, Pn), lambda: (0, 0))
    return _call(
        body, after=after, name=name, in_specs=[sq, sq, pl.BlockSpec((G, 1), lambda: (0, 0))], out_specs=[sq] * 4,
        out_shape=[jax.ShapeDtypeStruct((G, Pn), F32)] * 4,
    )(lr, li, ls)


def _s5_disc_bwd(lr, li, ls, dar, dai, dgr, dgi, *, name):
    G, Pn = lr.shape

    def body(lr_ref, li_ref, ls_ref, dar_ref, dai_ref, dgr_ref, dgi_ref, dlr_ref, dli_ref, dls_ref):
        lr_, li_ = lr_ref[...], li_ref[...]
        dt = jnp.exp(ls_ref[...])
        mag = jnp.exp(lr_ * dt)
        th = li_ * dt
        ar = mag * jnp.cos(th)
        ai = mag * jnp.sin(th)
        den = lr_ * lr_ + li_ * li_
        xr = ar - 1.0
        xi = ai
        g_re = (xr * lr_ + xi * li_) / den
        g_im = (xi * lr_ - xr * li_) / den
        dgr_, dgi_ = dgr_ref[...], dgi_ref[...]
        dxr = (dgr_ * lr_ - dgi_ * li_) / den
        dxi = (dgr_ * li_ + dgi_ * lr_) / den
        dden = -(dgr_ * g_re + dgi_ * g_im) / den
        dlr = (dgr_ * xr + dgi_ * xi) / den + 2.0 * dden * lr_
        dli = (dgr_ * xi - dgi_ * xr) / den + 2.0 * dden * li_
        da_r = dar_ref[...] + dxr
        da_i = dai_ref[...] + dxi
        dmag_mag = da_r * ar + da_i * ai
        dth = da_i * ar - da_r * ai
        dlr_ref[...] = dlr + dmag_mag * dt
        dli_ref[...] = dli + dth * dt
        ddt = jnp.sum(dmag_mag * lr_ + dth * li_, axis=-1, keepdims=True)
        dls_ref[...] = ddt * dt

    sq = pl.BlockSpec((G, Pn), lambda: (0, 0))
    c1 = pl.BlockSpec((G, 1), lambda: (0, 0))
    return _call(
        body, name=name, in_specs=[sq, sq, c1, sq, sq, sq, sq], out_specs=[sq, sq, c1],
        out_shape=[jax.ShapeDtypeStruct((G, Pn), F32)] * 2 + [jax.ShapeDtypeStruct((G, 1), F32)],
    )(lr, li, ls, dar, dai, dgr, dgi)


def _s5_bb_fwd(gr, gi, br, bi, *, name):
    R, C = br.shape

    def body(gr_ref, gi_ref, br_ref, bi_ref, or_ref, oi_ref):
        g_r, g_i, b_r, b_i = gr_ref[...], gi_ref[...], br_ref[...], bi_ref[...]
        or_ref[...] = g_r * b_r - g_i * b_i
        oi_ref[...] = g_r * b_i + g_i * b_r

    w = pl.BlockSpec((R, C), lambda: (0, 0))
    c1 = pl.BlockSpec((R, 1), lambda: (0, 0))
    return _call(body, name=name, in_specs=[c1, c1, w, w], out_specs=[w, w],
                 out_shape=[jax.ShapeDtypeStruct((R, C), F32)] * 2)(gr, gi, br, bi)


def _s5_bb_bwd(gr, gi, br, bi, dbbr, dbbi, *, name):
    R, C = br.shape

    def body(gr_ref, gi_ref, br_ref, bi_ref, dr_ref, di_ref, dbr_ref, dbi_ref, dgr_ref, dgi_ref):
        g_r, g_i, b_r, b_i = gr_ref[...], gi_ref[...], br_ref[...], bi_ref[...]
        d_r, d_i = dr_ref[...], di_ref[...]
        dbr_ref[...] = g_r * d_r + g_i * d_i
        dbi_ref[...] = g_r * d_i - g_i * d_r
        dgr_ref[...] = jnp.sum(d_r * b_r + d_i * b_i, axis=-1, keepdims=True)
        dgi_ref[...] = jnp.sum(d_i * b_r - d_r * b_i, axis=-1, keepdims=True)

    w = pl.BlockSpec((R, C), lambda: (0, 0))
    c1 = pl.BlockSpec((R, 1), lambda: (0, 0))
    return _call(body, name=name, in_specs=[c1, c1, w, w, w, w], out_specs=[w, w, c1, c1],
                 out_shape=[jax.ShapeDtypeStruct((R, C), F32)] * 2 + [jax.ShapeDtypeStruct((R, 1), F32)] * 2,
                 )(gr, gi, br, bi, dbbr, dbbi)


_DIAG_TILE = 8


def _diag_mask(gr, gc):
    rows, cols = _DIAG_TILE * gr, _DIAG_TILE * gc
    r = lax.broadcasted_iota(jnp.int32, (rows, cols), 0) >> (gr.bit_length() - 1)
    c = lax.broadcasted_iota(jnp.int32, (rows, cols), 1) >> (gc.bit_length() - 1)
    return r == c


def _diag_expand(t2, gr, gc, *, name, after=()):
    _, R, _ = t2.shape
    G = R // gr
    nt = G // _DIAG_TILE
    rows, cols = _DIAG_TILE * gr, _DIAG_TILE * gc

    def body(t_ref, o_ref):
        src = lax.broadcasted_iota(jnp.int32, (gc, cols), 0)
        dst = lax.broadcasted_iota(jnp.int32, (gc, cols), 1) & (gc - 1)
        spread = (src == dst).astype(BF16)
        y = jnp.dot(t_ref[...].astype(BF16), spread, preferred_element_type=F32)
        o_ref[...] = jnp.where(_diag_mask(gr, gc), y, 0.0).astype(BF16)

    return _call(
        body, after=after, name=name, grid=(2, nt),
        in_specs=[pl.BlockSpec((None, rows, gc), lambda p, i: (p, i, 0))],
        out_specs=pl.BlockSpec((None, rows, cols), lambda p, i: (p, i, i)),
        out_shape=jax.ShapeDtypeStruct((2, R, G * gc), BF16),
        compiler_params=_cparams(("parallel",) * 2),
    )(t2)


def _diag_extract(xd, gr, gc, *, name):
    _, R, _ = xd.shape
    nt = R // gr // _DIAG_TILE
    rows, cols = _DIAG_TILE * gr, _DIAG_TILE * gc

    def body(x_ref, o_ref):
        src = lax.broadcasted_iota(jnp.int32, (cols, gc), 0) & (gc - 1)
        dst = lax.broadcasted_iota(jnp.int32, (cols, gc), 1)
        fold = (src == dst).astype(BF16)
        parts = _split3(jnp.where(_diag_mask(gr, gc), x_ref[...], 0.0))
        acc = jnp.dot(parts[2], fold, preferred_element_type=F32)
        acc = acc + jnp.dot(parts[1], fold, preferred_element_type=F32)
        o_ref[...] = acc + jnp.dot(parts[0], fold, preferred_element_type=F32)

    return _call(
        body, name=name, grid=(2, nt),
        in_specs=[pl.BlockSpec((None, rows, cols), lambda p, i: (p, i, i))],
        out_specs=pl.BlockSpec((None, rows, gc), lambda p, i: (p, i, 0)),
        out_shape=jax.ShapeDtypeStruct((2, R, gc), F32),
        compiler_params=_cparams(("parallel",) * 2),
    )(xd)


SCAN_BLOCK = 8


def _cpowers(ar, ai, sign):
    ai = sign * ai
    out = [(ar, ai)]
    for _ in range(SCAN_BLOCK - 1):
        pr, pi = out[-1]
        out.append((pr * ar - pi * ai, pr * ai + pi * ar))
    return out


def _row_table(pw, row, index_of_row):
    tr_ = jnp.broadcast_to(pw[index_of_row(0)][0], row.shape)
    ti_ = jnp.broadcast_to(pw[index_of_row(0)][1], row.shape)
    for r in range(1, SCAN_BLOCK):
        pr, pi = pw[index_of_row(r)]
        tr_ = jnp.where(row == r, pr, tr_)
        ti_ = jnp.where(row == r, pi, ti_)
    return tr_, ti_


def _s5_scan_fwd(bu, a, *, name):
    _, S, N = bu.shape
    tc = 512
    nt = N // tc

    def body(a_ref, b_ref, h_ref):
        pw = _cpowers(a_ref[0], a_ref[1], 1.0)
        row = lax.broadcasted_iota(jnp.int32, (SCAN_BLOCK, tc), 0)
        lead_r, lead_i = _row_table(pw, row, lambda r: r)

        def step(k, carry):
            cr, ci = carry
            rows = pl.ds(pl.multiple_of(k * SCAN_BLOCK, SCAN_BLOCK), SCAN_BLOCK)
            xr, xi = b_ref[0, rows, :], b_ref[1, rows, :]
            for sh in (1, 2, 4):
                keep = row >= sh
                sr = jnp.where(keep, pltpu.roll(xr, sh, 0), 0.0)
                si = jnp.where(keep, pltpu.roll(xi, sh, 0), 0.0)
                kr, ki = pw[sh - 1]
                xr, xi = xr + kr * sr - ki * si, xi + kr * si + ki * sr
            h_ref[0, rows, :] = xr + lead_r * cr - lead_i * ci
            h_ref[1, rows, :] = xi + lead_r * ci + lead_i * cr
            last = row == SCAN_BLOCK - 1
            tr_ = jnp.sum(jnp.where(last, xr, 0.0), axis=0, keepdims=True)
            ti_ = jnp.sum(jnp.where(last, xi, 0.0), axis=0, keepdims=True)
            a8r, a8i = pw[SCAN_BLOCK - 1]
            return a8r * cr - a8i * ci + tr_, a8r * ci + a8i * cr + ti_

        z = jnp.zeros((1, tc), F32)
        lax.fori_loop(0, S // SCAN_BLOCK, step, (z, z), unroll=2)

    vec = pl.BlockSpec((2, 1, tc), lambda j: (0, 0, j))
    mat = pl.BlockSpec((2, S, tc), lambda j: (0, 0, j))
    return _call(
        body, name=name, grid=(nt,), in_specs=[vec, mat], out_specs=mat,
        out_shape=jax.ShapeDtypeStruct((2, S, N), F32),
        compiler_params=_cparams(("parallel",)),
    )(a, bu)


def _s5_scan_bwd(g, h, a, *, name):
    _, S, N = g.shape
    tc = 256
    nt = N // tc

    def body(a_ref, g_ref, h_ref, l_ref, da_ref):
        pw = _cpowers(a_ref[0], a_ref[1], -1.0)
        row = lax.broadcasted_iota(jnp.int32, (SCAN_BLOCK, tc), 0)
        tail_r, tail_i = _row_table(pw, row, lambda r: SCAN_BLOCK - 1 - r)
        nb = S // SCAN_BLOCK

        def step(i, carry):
            k = nb - 1 - i
            cr, ci, dar, dai = carry
            rows = pl.ds(pl.multiple_of(k * SCAN_BLOCK, SCAN_BLOCK), SCAN_BLOCK)
            xr, xi = g_ref[0, rows, :], g_ref[1, rows, :]
            for sh in (1, 2, 4):
                keep = row < SCAN_BLOCK - sh
                sr = jnp.where(keep, pltpu.roll(xr, SCAN_BLOCK - sh, 0), 0.0)
                si = jnp.where(keep, pltpu.roll(xi, SCAN_BLOCK - sh, 0), 0.0)
                kr, ki = pw[sh - 1]
                xr, xi = xr + kr * sr - ki * si, xi + kr * si + ki * sr
            lr = xr + tail_r * cr - tail_i * ci
            li = xi + tail_r * ci + tail_i * cr
            l_ref[0, rows, :] = lr
            l_ref[1, rows, :] = li
            prev = pl.ds(pl.multiple_of(jnp.maximum(k - 1, 0) * SCAN_BLOCK, SCAN_BLOCK), SCAN_BLOCK)
            has_prev = jnp.where(k > 0, 1.0, 0.0).astype(F32)
            first = row == 0
            hpr = jnp.where(first, pltpu.roll(h_ref[0, prev, :], 1, 0) * has_prev, pltpu.roll(h_ref[0, rows, :], 1, 0))
            hpi = jnp.where(first, pltpu.roll(h_ref[1, prev, :], 1, 0) * has_prev, pltpu.roll(h_ref[1, rows, :], 1, 0))
            tr_ = jnp.sum(jnp.where(first, xr, 0.0), axis=0, keepdims=True)
            ti_ = jnp.sum(jnp.where(first, xi, 0.0), axis=0, keepdims=True)
            a8r, a8i = pw[SCAN_BLOCK - 1]
            return (a8r * cr - a8i * ci + tr_, a8r * ci + a8i * cr + ti_,
                    dar + lr * hpr + li * hpi, dai + li * hpr - lr * hpi)

        z = jnp.zeros((1, tc), F32)
        z8 = jnp.zeros((SCAN_BLOCK, tc), F32)
        _, _, dar, dai = lax.fori_loop(0, nb, step, (z, z, z8, z8), unroll=2)
        da_ref[0] = jnp.sum(dar, axis=0, keepdims=True)
        da_ref[1] = jnp.sum(dai, axis=0, keepdims=True)

    vec = pl.BlockSpec((2, 1, tc), lambda j: (0, 0, j))
    mat = pl.BlockSpec((2, S, tc), lambda j: (0, 0, j))
    return _call(
        body, name=name, grid=(nt,), in_specs=[vec, mat, mat], out_specs=[mat, vec],
        out_shape=[jax.ShapeDtypeStruct((2, S, N), F32), jax.ShapeDtypeStruct((2, 1, N), F32)],
        compiler_params=_cparams(("parallel",)),
    )(a, g, h)


_GELU_C = math.sqrt(2.0 / math.pi)


def _s5_out_fwd(yc, P, dskip, *, name):
    S, W = yc.shape
    tr = _pick(S, 256)
    ub = 3 * FOX_WIDTH // W

    def body(yc_ref, u_ref, d_ref, y_ref, yg_ref):
        y = yc_ref[...] + d_ref[...] * u_ref[...]
        y_ref[...] = y
        t = jnp.tanh(_GELU_C * (y + 0.044715 * y * y * y))
        yg_ref[...] = (0.5 * y * (1.0 + t)).astype(BF16)

    row = pl.BlockSpec((tr, W), lambda i: (i, 0))
    return _call(
        body, name=name, grid=(S // tr,),
        in_specs=[row, pl.BlockSpec((tr, W), lambda i: (i, ub)), pl.BlockSpec((1, W), lambda i: (0, 0))],
        out_specs=[row, row],
        out_shape=[jax.ShapeDtypeStruct((S, W), F32), jax.ShapeDtypeStruct((S, W), BF16)],
        compiler_params=_cparams(("parallel",)),
    )(yc, P, dskip)


def _s5_out_bwd(dyg, y, P, dskip, *, name):
    S, W = y.shape
    tr = _pick(S, 256)
    ub = 3 * FOX_WIDTH // W

    def body(dyg_ref, y_ref, u_ref, d_ref, dy_ref, du_ref, dd_ref):
        y_ = y_ref[...]
        inner = _GELU_C * (y_ + 0.044715 * y_ * y_ * y_)
        t = jnp.tanh(inner)
        dgelu = 0.5 * (1.0 + t) + 0.5 * y_ * (1.0 - t * t) * _GELU_C * (1.0 + 3.0 * 0.044715 * y_ * y_)
        dy = dyg_ref[...] * dgelu
        dy_ref[...] = dy.astype(BF16)
        du_ref[...] = d_ref[...] * dy
        part = jnp.sum(dy * u_ref[...], axis=0, keepdims=True)

        @pl.when(pl.program_id(0) == 0)
        def _():
            dd_ref[...] = part

        @pl.when(pl.program_id(0) > 0)
        def _():
            dd_ref[...] += part

    row = pl.BlockSpec((tr, W), lambda i: (i, 0))
    vec = pl.BlockSpec((1, W), lambda i: (0, 0))
    return _call(
        body, name=name, grid=(S // tr,),
        in_specs=[row, row, pl.BlockSpec((tr, W), lambda i: (i, ub)), vec],
        out_specs=[row, row, vec],
        out_shape=[jax.ShapeDtypeStruct((S, W), BF16), jax.ShapeDtypeStruct((S, W), F32),
                   jax.ShapeDtypeStruct((1, W), F32)],
        compiler_params=_cparams(("arbitrary",)),
    )(dyg, y, P, dskip)


def _glu_fwd(z, *, name):
    S, W2 = z.shape
    W = W2 // 2
    tr = _pick(S, 256)

    def body(z1_ref, z2_ref, o_ref):
        o_ref[...] = (z1_ref[...] * jax.nn.sigmoid(z2_ref[...])).astype(BF16)

    return _call(
        body, name=name, grid=(S // tr,),
        in_specs=[pl.BlockSpec((tr, W), lambda i: (i, 0)), pl.BlockSpec((tr, W), lambda i: (i, 1))],
        out_specs=pl.BlockSpec((tr, W), lambda i: (i, 0)),
        out_shape=jax.ShapeDtypeStruct((S, W), BF16),
        compiler_params=_cparams(("parallel",)),
    )(z, z)


def _glu_bwd(z, dcat, *, name):
    S, W2 = z.shape
    W = W2 // 2
    tr = _pick(S, 256)

    def body(z1_ref, z2_ref, d_ref, dz1_ref, dz2_ref):
        sg = jax.nn.sigmoid(z2_ref[...])
        d = d_ref[...]
        dz1_ref[...] = (d * sg).astype(BF16)
        dz2_ref[...] = (d * z1_ref[...] * sg * (1.0 - sg)).astype(BF16)

    lo = pl.BlockSpec((tr, W), lambda i: (i, 0))
    hi = pl.BlockSpec((tr, W), lambda i: (i, 1))
    dz1, dz2 = _call(
        body, name=name, grid=(S // tr,), in_specs=[lo, hi, hi], out_specs=[lo, lo],
        out_shape=[jax.ShapeDtypeStruct((S, W), BF16)] * 2,
        compiler_params=_cparams(("parallel",)),
    )(z, z, dcat)
    return jnp.concatenate([dz1, dz2], axis=1)


ACT_ROWS = 16
ACT_COLS = 256


def _shift_down(cur, prev, k, row):
    return jnp.where(row >= k, pltpu.roll(cur, k, 0), pltpu.roll(prev, k, 0))


def _shift_up(cur, nxt, k, row):
    n = cur.shape[0]
    return jnp.where(row < n - k, pltpu.roll(cur, n - k, 0), pltpu.roll(nxt, n - k, 0))


def _act_fwd(h, cw, cb, *, name):
    _, S, FP = h.shape
    tr = _pick(S, 256)
    hb = tr // ACT_ROWS
    nq = tr // ACT_ROWS

    def body(g_ref, gh_ref, v_ref, vh_ref, wg_ref, wv_ref, bg_ref, bv_ref, a_ref):
        first = pl.program_id(1) == 0
        for c0 in range(0, FP, ACT_COLS):
            cw_ = min(ACT_COLS, FP - c0)
            cols = pl.ds(c0, cw_)
            rw = lax.broadcasted_iota(jnp.int32, (ACT_ROWS, cw_), 0)
            wg = [wg_ref[pl.ds(k, 1), cols] for k in range(3)]
            wv = [wv_ref[pl.ds(k, 1), cols] for k in range(3)]
            bg, bv = bg_ref[:, cols], bv_ref[:, cols]
            halo_g = jnp.where(first, 0.0, gh_ref[:, cols])
            halo_v = jnp.where(first, 0.0, vh_ref[:, cols])

            def chunk(q, _):
                rows = pl.ds(pl.multiple_of(q * ACT_ROWS, ACT_ROWS), ACT_ROWS)
                before = pl.ds(pl.multiple_of(jnp.maximum(q - 1, 0) * ACT_ROWS, ACT_ROWS), ACT_ROWS)
                g, v = g_ref[rows, cols], v_ref[rows, cols]
                gp = jnp.where(q > 0, g_ref[before, cols], halo_g)
                vp = jnp.where(q > 0, v_ref[before, cols], halo_v)
                cg = bg + wg[2] * g + wg[1] * _shift_down(g, gp, 1, rw) + wg[0] * _shift_down(g, gp, 2, rw)
                cv = bv + wv[2] * v + wv[1] * _shift_down(v, vp, 1, rw) + wv[0] * _shift_down(v, vp, 2, rw)
                a_ref[rows, cols] = (cg * jax.nn.sigmoid(cg) * cv).astype(BF16)
                return 0

            lax.fori_loop(0, nq, chunk, 0, unroll=2)

    def main(off):
        return pl.BlockSpec((None, tr, FP), lambda j, i: (j + off, i, 0))

    def halo(off):
        return pl.BlockSpec((None, ACT_ROWS, FP), lambda j, i: (j + off, jnp.maximum(i * hb - 1, 0), 0))

    def wspec(off):
        return pl.BlockSpec((None, 3, FP), lambda j, i: (j + off, 0, 0))

    def bspec(off):
        return pl.BlockSpec((None, 1, FP), lambda j, i: (j + off, 0, 0))

    cb3 = cb.reshape(4, 1, FP)
    return _call(
        body, name=name, grid=(2, S // tr),
        in_specs=[main(0), halo(0), main(2), halo(2), wspec(0), wspec(2), bspec(0), bspec(2)],
        out_specs=pl.BlockSpec((None, tr, FP), lambda j, i: (j, i, 0)),
        out_shape=jax.ShapeDtypeStruct((2, S, FP), BF16),
        compiler_params=_cparams(("parallel", "parallel")),
    )(h, h, h, h, cw, cw, cb3, cb3)


def _act_bwd(h, da, cw, cb, *, name):
    _, S, FP = h.shape
    tr = _pick(S, 256)
    hb = tr // ACT_ROWS
    nq = tr // ACT_ROWS
    nr = S // tr
    half = ACT_ROWS // 2

    def fold(x):
        return x[:half] + x[half:]

    def body(g_ref, gp_ref, v_ref, vp_ref, da_ref, wg_ref, wv_ref, bg_ref, bv_ref,
             dh_ref, dwg_ref, dwv_ref, dbg_ref, dbv_ref, carry_g, carry_v):
        i = pl.program_id(1)
        bottom = i == 0
        top = i == nr - 1
        for c0 in range(0, FP, ACT_COLS):
            cw_ = min(ACT_COLS, FP - c0)
            cols = pl.ds(c0, cw_)
            rw = lax.broadcasted_iota(jnp.int32, (ACT_ROWS, cw_), 0)
            wg = [wg_ref[pl.ds(k, 1), cols] for k in range(3)]
            wv = [wv_ref[pl.ds(k, 1), cols] for k in range(3)]
            bg, bv = bg_ref[:, cols], bv_ref[:, cols]
            halo_g = jnp.where(top, 0.0, gp_ref[:, cols])
            halo_v = jnp.where(top, 0.0, vp_ref[:, cols])
            after_g = jnp.where(bottom, 0.0, carry_g[:, cols])
            after_v = jnp.where(bottom, 0.0, carry_v[:, cols])

            def chunk(s, carry):
                ng, nv, acc = carry[0], carry[1], carry[2:]
                q = nq - 1 - s
                rows = pl.ds(pl.multiple_of(q * ACT_ROWS, ACT_ROWS), ACT_ROWS)
                before = pl.ds(pl.multiple_of(jnp.maximum(q - 1, 0) * ACT_ROWS, ACT_ROWS), ACT_ROWS)
                g, v = g_ref[rows, cols], v_ref[rows, cols]
                gp = jnp.where(q > 0, g_ref[before, cols], halo_g)
                vp = jnp.where(q > 0, v_ref[before, cols], halo_v)
                g1, g2 = _shift_down(g, gp, 1, rw), _shift_down(g, gp, 2, rw)
                v1, v2 = _shift_down(v, vp, 1, rw), _shift_down(v, vp, 2, rw)
                cg = bg + wg[2] * g + wg[1] * g1 + wg[0] * g2
                cv = bv + wv[2] * v + wv[1] * v1 + wv[0] * v2
                sg = jax.nn.sigmoid(cg)
                d = da_ref[rows, cols]
                dcg = d * cv * sg * (1.0 + cg * (1.0 - sg))
                dcv = d * cg * sg
                dh_ref[0, rows, cols] = (wg[2] * dcg + wg[1] * _shift_up(dcg, ng, 1, rw)
                                         + wg[0] * _shift_up(dcg, ng, 2, rw)).astype(BF16)
                dh_ref[1, rows, cols] = (wv[2] * dcv + wv[1] * _shift_up(dcv, nv, 1, rw)
                                         + wv[0] * _shift_up(dcv, nv, 2, rw)).astype(BF16)
                terms = (dcg * g2, dcg * g1, dcg * g, dcg, dcv * v2, dcv * v1, dcv * v, dcv)
                return (dcg, dcv) + tuple(a + fold(t) for a, t in zip(acc, terms))

            zero = jnp.zeros((half, cw_), F32)
            out = lax.fori_loop(0, nq, chunk, (after_g, after_v) + (zero,) * 8, unroll=2)
            carry_g[:, cols] = out[0]
            carry_v[:, cols] = out[1]
            sums = [jnp.sum(a, axis=0, keepdims=True) for a in out[2:]]

            @pl.when(bottom)
            def _():
                for k in range(3):
                    dwg_ref[pl.ds(k, 1), cols] = sums[k]
                    dwv_ref[pl.ds(k, 1), cols] = sums[4 + k]
                dbg_ref[:, cols] = sums[3]
                dbv_ref[:, cols] = sums[7]

            @pl.when(jnp.logical_not(bottom))
            def _():
                for k in range(3):
                    dwg_ref[pl.ds(k, 1), cols] += sums[k]
                    dwv_ref[pl.ds(k, 1), cols] += sums[4 + k]
                dbg_ref[:, cols] += sums[3]
                dbv_ref[:, cols] += sums[7]

    def main(off):
        return pl.BlockSpec((None, tr, FP), lambda j, i: (j + off, nr - 1 - i, 0))

    def prev(off):
        return pl.BlockSpec((None, ACT_ROWS, FP), lambda j, i: (j + off, jnp.maximum((nr - 1 - i) * hb - 1, 0), 0))

    def wspec(off):
        return pl.BlockSpec((None, 3, FP), lambda j, i: (j + off, 0, 0))

    def bspec(off):
        return pl.BlockSpec((None, 1, FP), lambda j, i: (j + off, 0, 0))

    cb3 = cb.reshape(4, 1, FP)
    dh, dwg, dwv, dbg, dbv = _call(
        body, name=name, grid=(2, nr),
        in_specs=[main(0), prev(0), main(2), prev(2), main(0), wspec(0), wspec(2), bspec(0), bspec(2)],
        out_specs=[pl.BlockSpec((None, 2, tr, FP), lambda j, i: (j, 0, nr - 1 - i, 0)),
                   wspec(0), wspec(0), bspec(0), bspec(0)],
        out_shape=[jax.ShapeDtypeStruct((2, 2, S, FP), BF16)]
        + [jax.ShapeDtypeStruct((2, 3, FP), F32)] * 2 + [jax.ShapeDtypeStruct((2, 1, FP), F32)] * 2,
        scratch_shapes=[pltpu.VMEM((ACT_ROWS, FP), F32), pltpu.VMEM((ACT_ROWS, FP), F32)],
        compiler_params=_cparams(("parallel", "arbitrary")),
    )(h, h, h, h, da, cw, cw, cb3, cb3)
    return (dh.reshape(4, S, FP), jnp.concatenate([dwg, dwv], axis=0), jnp.concatenate([dbg, dbv], axis=0))


def _rope_tables(posf, *, name, after=()):
    S = posf.shape[0]
    half = ROPE_DIM // 2
    d = np.arange(LANE) % SWA_HEAD_DIM
    invf = np.where(d < ROPE_DIM, ROPE_THETA ** (-(d % half).astype(np.float64) / half), 0.0).astype(np.float32)
    m_rot = (d < ROPE_DIM).astype(np.float32)
    m_a = (d < half).astype(np.float32)
    m_b = ((d >= half) & (d < ROPE_DIM)).astype(np.float32)
    consts = jnp.asarray(np.stack([invf, m_rot, m_a, m_b] + [np.zeros(LANE, np.float32)] * 4))

    def body(p_ref, k_ref, c_ref, sa_ref, sb_ref):
        k = k_ref[...]
        ang = p_ref[...] * k[0:1]
        co, si = jnp.cos(ang), jnp.sin(ang)
        c_ref[...] = k[1:2] * co + (1.0 - k[1:2])
        sa_ref[...] = -k[2:3] * si
        sb_ref[...] = k[3:4] * si

    full = pl.BlockSpec((S, LANE), lambda: (0, 0))
    return _call(
        body, after=after, name=name,
        in_specs=[pl.BlockSpec((S, 1), lambda: (0, 0)), pl.BlockSpec((8, LANE), lambda: (0, 0))],
        out_specs=[full] * 3, out_shape=[jax.ShapeDtypeStruct((S, LANE), F32)] * 3,
    )(posf, consts)


def _rope_apply(x, tabs, *, col0, width, inverse, name, out_dtype):
    S = x.shape[0]
    tr = _pick(S, 256)
    rep = width // LANE
    cb = col0 // width

    def body(x_ref, c_ref, sa_ref, sb_ref, o_ref):
        xv = x_ref[...].astype(F32)
        c = jnp.tile(c_ref[...], (1, rep))
        sa = jnp.tile(sa_ref[...], (1, rep))
        sb = jnp.tile(sb_ref[...], (1, rep))
        if not inverse:
            out = xv * c + pltpu.roll(xv, width - 8, 1) * sa + pltpu.roll(xv, 8, 1) * sb
        else:
            out = xv * c + pltpu.roll(xv * sa, 8, 1) + pltpu.roll(xv * sb, width - 8, 1)
        o_ref[...] = out.astype(out_dtype)

    tab = pl.BlockSpec((tr, LANE), lambda i: (i, 0))
    return _call(
        body, name=name, grid=(S // tr,),
        in_specs=[pl.BlockSpec((tr, width), lambda i: (i, cb)), tab, tab, tab],
        out_specs=pl.BlockSpec((tr, width), lambda i: (i, 0)),
        out_shape=jax.ShapeDtypeStruct((S, width), out_dtype),
        compiler_params=_cparams(("parallel",)),
    )(x, *tabs)


def _swa_mask(n):
    rows = SWA_GROUPS * SWA_WINDOW
    qi = lax.broadcasted_iota(jnp.int32, (rows, 2 * SWA_WINDOW), 0) & (SWA_WINDOW - 1)
    kj = lax.broadcasted_iota(jnp.int32, (rows, 2 * SWA_WINDOW), 1)
    rel = SWA_WINDOW + qi - kj
    return (rel >= 0) & (rel < SWA_WINDOW) & ((n > 0) | (kj >= SWA_WINDOW))


def _swa_fwd(qT, kT, vT, sink_rows, *, name):
    S = qT.shape[1]
    W, G, Dh = SWA_WINDOW, SWA_GROUPS, SWA_HEAD_DIM
    nb = S // W
    scale = 1.0 / math.sqrt(Dh)

    def body(q_ref, kp_ref, kc_ref, vp_ref, vc_ref, s_ref, o_ref, l_ref):
        n = pl.program_id(1)
        q = q_ref[...].reshape(G * W, Dh)
        kk = jnp.concatenate([kp_ref[...], kc_ref[...]], axis=0)
        vv = jnp.concatenate([vp_ref[...], vc_ref[...]], axis=0)
        s = lax.dot_general(q, kk, (((1,), (1,)), ((), ())), preferred_element_type=F32) * scale
        s = jnp.where(_swa_mask(n), s, -1e30)
        sink = s_ref[...]
        m = jnp.maximum(jnp.max(s, axis=-1, keepdims=True), sink)
        e = jnp.exp(s - m)
        den = jnp.sum(e, axis=-1, keepdims=True) + jnp.exp(sink - m)
        p = e / den
        o_ref[...] = jnp.dot(p.astype(BF16), vv, preferred_element_type=F32).reshape(G, W, Dh)
        l_ref[...] = (m + jnp.log(den)).reshape(G, W, 1)

    qs = pl.BlockSpec((G, W, Dh), lambda g, n: (g, n, 0))
    prev = pl.BlockSpec((None, W, Dh), lambda g, n: (g, jnp.maximum(n - 1, 0), 0))
    cur = pl.BlockSpec((None, W, Dh), lambda g, n: (g, n, 0))
    return _call(
        body, name=name, grid=(SWA_KV_HEADS, nb),
        in_specs=[qs, prev, cur, prev, cur, pl.BlockSpec((None, G * W, 1), lambda g, n: (g, 0, 0))],
        out_specs=[qs, pl.BlockSpec((G, W, 1), lambda g, n: (g, n, 0))],
        out_shape=[jax.ShapeDtypeStruct((SWA_HEADS, S, Dh), F32), jax.ShapeDtypeStruct((SWA_HEADS, S, 1), F32)],
        compiler_params=_cparams(("parallel", "parallel")),
    )(qT, kT, kT, vT, vT, sink_rows)


def _swa_bwd(qT, kT, vT, sink_rows, oT, L, doT, *, name):
    S = qT.shape[1]
    W, G, Dh = SWA_WINDOW, SWA_GROUPS, SWA_HEAD_DIM
    nb = S // W
    scale = 1.0 / math.sqrt(Dh)

    def body(q_ref, kp_ref, kc_ref, vp_ref, vc_ref, s_ref, o_ref, l_ref, do_ref,
             dq_ref, dk_ref, dv_ref, ds_ref):
        n = pl.program_id(1)
        q = q_ref[...].reshape(G * W, Dh)
        kk = jnp.concatenate([kp_ref[...], kc_ref[...]], axis=0)
        vv = jnp.concatenate([vp_ref[...], vc_ref[...]], axis=0)
        s = lax.dot_general(q, kk, (((1,), (1,)), ((), ())), preferred_element_type=F32) * scale
        lrow = l_ref[...].reshape(G * W, 1)
        p = jnp.where(_swa_mask(n), jnp.exp(s - lrow), 0.0)
        do = do_ref[...].reshape(G * W, Dh)
        do_bf = do.astype(BF16)
        dp = lax.dot_general(do_bf, vv, (((1,), (1,)), ((), ())), preferred_element_type=F32)
        delta = jnp.sum(do * o_ref[...].reshape(G * W, Dh), axis=-1, keepdims=True)
        dsc = p * (dp - delta)
        ds_bf = dsc.astype(BF16)
        dq_ref[...] = (jnp.dot(ds_bf, kk, preferred_element_type=F32) * scale).astype(BF16).reshape(G, W, Dh)
        dkk = lax.dot_general(ds_bf, q, (((0,), (0,)), ((), ())), preferred_element_type=F32) * scale
        dvv = lax.dot_general(p.astype(BF16), do_bf, (((0,), (0,)), ((), ())), preferred_element_type=F32)
        dsk = -jnp.exp(s_ref[...] - lrow) * delta
        dsk = jnp.broadcast_to(jnp.sum(dsk.reshape(G, W, 1), axis=1), (G, LANE))

        @pl.when(n == 0)
        def _():
            dk_ref[...] = jnp.zeros_like(dk_ref)
            dv_ref[...] = jnp.zeros_like(dv_ref)
            ds_ref[...] = jnp.zeros_like(ds_ref)

        rows = pl.ds(pl.multiple_of(n * W, W), 2 * W)
        dk_ref[rows, :] += dkk
        dv_ref[rows, :] += dvv
        ds_ref[...] += dsk

    qs = pl.BlockSpec((G, W, Dh), lambda g, n: (g, n, 0))
    prev = pl.BlockSpec((None, W, Dh), lambda g, n: (g, jnp.maximum(n - 1, 0), 0))
    cur = pl.BlockSpec((None, W, Dh), lambda g, n: (g, n, 0))
    lsp = pl.BlockSpec((G, W, 1), lambda g, n: (g, n, 0))
    kvo = pl.BlockSpec((None, S + W, Dh), lambda g, n: (g, 0, 0))
    return _call(
        body, name=name, grid=(SWA_KV_HEADS, nb),
        in_specs=[qs, prev, cur, prev, cur, pl.BlockSpec((None, G * W, 1), lambda g, n: (g, 0, 0)), qs, lsp, qs],
        out_specs=[qs, kvo, kvo, pl.BlockSpec((None, G, LANE), lambda g, n: (g, 0, 0))],
        out_shape=[jax.ShapeDtypeStruct((SWA_HEADS, S, Dh), BF16),
                   jax.ShapeDtypeStruct((SWA_KV_HEADS, S + W, Dh), F32),
                   jax.ShapeDtypeStruct((SWA_KV_HEADS, S + W, Dh), F32),
                   jax.ShapeDtypeStruct((SWA_KV_HEADS, G, LANE), F32)],
        compiler_params=_cparams(("parallel", "arbitrary")),
    )(qT, kT, kT, vT, vT, sink_rows, oT, L, doT)


def _adamw(w, g, m, v, *, name, tr=128, by_cols=False):
    L, R, C = w.shape
    split = isinstance(g, (list, tuple))
    HR, HC = _half_shape(R, C, by_cols) if split else (R, C)
    tr, tc = _tile2d(HR, HC, tr)
    nr, nc = HR // tr, HC // tc
    c1 = 1.0 / (1.0 - ADAM_B1 ** ADAM_STEP)
    c2 = 1.0 / (1.0 - ADAM_B2 ** ADAM_STEP)
    ng = 2 * L if split else 1

    def body(*refs):
        w_ref, g_refs, (m_ref, v_ref, go_ref, d_ref, mo_ref, vo_ref) = refs[0], refs[1:1 + ng], refs[1 + ng:]
        if split:
            mine = pl.program_id(1) == lax.axis_index("c")
            g_ = jnp.where(mine, g_refs[0][...], g_refs[1][...])
            for l in range(1, L):
                g_ = jnp.where(pl.program_id(0) == l,
                               jnp.where(mine, g_refs[2 * l][...], g_refs[2 * l + 1][...]), g_)
        else:
            g_ = g_refs[0][...]
        mn = ADAM_B1 * m_ref[...] + (1.0 - ADAM_B1) * g_
        vn = ADAM_B2 * v_ref[...] + (1.0 - ADAM_B2) * (g_ * g_)
        go_ref[...] = g_
        mo_ref[...] = mn
        vo_ref[...] = vn
        d_ref[...] = -ADAM_LR * ((mn * c1) / (jnp.sqrt(vn * c2) + ADAM_EPS) + ADAM_WD * w_ref[...])

    def whole(l, hf, i, j):
        return (l, i, hf * nc + j) if by_cols else (l, hf * nr + i, j)

    row = pl.BlockSpec((None, tr, tc), whole)
    half = pl.BlockSpec((tr, tc), lambda l, hf, i, j: (i, j))
    gs = [h for pair in g for h in pair] if split else [g]
    return _call(
        body, name=name, grid=(L, 2 if split else 1, nr, nc),
        in_specs=[row] + [half if split else row] * ng + [row, row],
        out_specs=[row] * 4, out_shape=[jax.ShapeDtypeStruct((L, R, C), F32)] * 4,
        compiler_params=_cparams(("parallel",) * 4),
    )(w, *gs, m, v)


def _sum2_halves(g4, s4, by_cols, *, name):
    n, R, C = g4.shape
    HR, HC = _half_shape(R, C, by_cols)
    tr, tc = _tile2d(HR, HC, budget=1024 * 1024)
    nr, nc = HR // tr, HC // tc
    core = lax.axis_index("c").astype(jnp.int32).reshape(1)

    def body(c_ref, g_ref, s_ref, o_ref):
        o_ref[...] = (g_ref[...].astype(F32) + s_ref[...].astype(F32)).astype(BF16)

    def mine(k, i, j, c):
        return (k, i, c[0] * nc + j) if by_cols else (k, c[0] * nr + i, j)

    blk = pl.BlockSpec((None, tr, tc), lambda k, i, j, c: (k, i, j))
    return _call(
        body, name=name,
        grid_spec=pltpu.PrefetchScalarGridSpec(
            num_scalar_prefetch=1, grid=(n, nr, nc),
            in_specs=[pl.BlockSpec((None, tr, tc), mine), blk], out_specs=blk),
        out_shape=jax.ShapeDtypeStruct((n, HR, HC), BF16),
        compiler_params=_cparams(("parallel", "parallel", "parallel")),
    )(core, g4, s4)


def _rowsum(parts, *, name, out_dtype=F32):
    n, R, C = parts.shape
    tr, tc = _tile2d(R, C, budget=512 * 1024)

    def body(p_ref, o_ref):
        acc = p_ref[0].astype(F32)
        for i in range(1, n):
            acc = acc + p_ref[i].astype(F32)
        o_ref[...] = acc.astype(out_dtype)

    return _call(
        body, name=name, grid=(R // tr, C // tc),
        in_specs=[pl.BlockSpec((n, tr, tc), lambda i, j: (0, i, j))],
        out_specs=pl.BlockSpec((tr, tc), lambda i, j: (i, j)),
        out_shape=jax.ShapeDtypeStruct((R, C), out_dtype),
        compiler_params=_cparams(("parallel", "parallel")),
    )(parts)


def _where_am_i():
    x, y, c = lax.axis_index("x"), lax.axis_index("y"), lax.axis_index("c")
    chips = [(1 - x, y), (x, 1 - y), (1 - x, 1 - y)]
    return x, y, c, chips


def _half_idx(rows, cols, by_cols, which):
    if by_cols:
        hc = cols // 2
        return (slice(None), pl.ds(pl.multiple_of(which * hc, LANE), hc))
    hr = rows // 2
    return (pl.ds(pl.multiple_of(which * hr, 16), hr), slice(None))


def _half_shape(rows, cols, by_cols):
    return (rows, cols // 2) if by_cols else (rows // 2, cols)


def _all_gather_shards(shards, by_cols, *, name):
    n = len(shards)

    def body(*refs):
        ins, outs = refs[:n], refs[n:2 * n]
        send, recv = refs[2 * n:]
        x, y, c, chips = _where_am_i()
        me = 2 * x + y
        sibling = (x, y, 1 - c)

        def half(i, which):
            return _half_idx(*shards[i].shape, by_cols[i], which)

        def cp(i, k, src, dst, to):
            return pltpu.make_async_remote_copy(src_ref=src, dst_ref=dst, send_sem=send.at[i, k],
                                                recv_sem=recv.at[i, k], device_id=to, device_id_type=MESH)

        first = []
        for i in range(n):
            for k, (px, py) in enumerate(chips):
                d = cp(i, k, ins[i].at[half(i, c)], outs[i].at[(me,) + half(i, c)], (px, py, c))
                d.start()
                first.append(d)
        passed = []
        for i in range(n):
            for k, (px, py) in enumerate(chips):
                blk = outs[i].at[(2 * px + py,) + half(i, c)]
                cp(i, k, blk, blk, (px, py, c)).wait_recv()
                d = cp(i, 3 + k, blk, blk, sibling)
                d.start()
                passed.append(d)
        for i in range(n):
            for k, (px, py) in enumerate(chips):
                blk = outs[i].at[(2 * px + py,) + half(i, 1 - c)]
                cp(i, 3 + k, blk, blk, sibling).wait_recv()
        for d in first + passed:
            d.wait_send()

    got = _call(
        body, name=name, in_specs=[ANY] * n, out_specs=[ANY] * n,
        out_shape=[jax.ShapeDtypeStruct((N_CHIPS,) + s.shape, s.dtype) for s in shards],
        scratch_shapes=[pltpu.SemaphoreType.DMA((n, 6)), pltpu.SemaphoreType.DMA((n, 6))],
    )(*shards)
    me = 2 * lax.axis_index("x") + lax.axis_index("y")
    return [lax.dynamic_update_slice_in_dim(g, s[None], me, axis=0) for g, s in zip(got, shards)]


HBM_SPEC = pl.BlockSpec(memory_space=pltpu.HBM)
SEM_SPEC = pl.BlockSpec(memory_space=pltpu.SEMAPHORE)
DATAFLOW = pltpu.SideEffectType.DATAFLOW_SIDE_EFFECTING


def _chip_exchange_refs(kind, shards_shape, by_cols, src, land, i, chip_k, c, me):
    if kind == 'gather':
        half = _half_idx(*shards_shape, by_cols, c)
        return src.at[half], land.at[(me,) + half], land.at[(chip_k,) + half]
    return src.at[chip_k], land.at[me], land.at[chip_k]


def _chip_exchange_start(kind, srcs, by_cols, *, name, after=()):
    n = len(srcs)
    land_shapes = [((N_CHIPS,) + s.shape) if kind == 'gather' else s.shape for s in srcs]

    def body(*refs):
        src_refs, land_refs = refs[:n], refs[n:2 * n]
        send, recv = refs[2 * n + len(after)], refs[2 * n + len(after) + 1]
        token = refs[-1]
        x, y, c, chips = _where_am_i()
        me = 2 * x + y
        for i in range(n):
            for k, (px, py) in enumerate(chips):
                s, d, _ = _chip_exchange_refs(kind, srcs[i].shape, by_cols[i], src_refs[i], land_refs[i], i,
                                              2 * px + py, c, me)
                pltpu.make_async_remote_copy(src_ref=s, dst_ref=d, send_sem=send.at[3 * i + k],
                                             recv_sem=recv.at[3 * i + k], device_id=(px, py, c),
                                             device_id_type=MESH).start()
        token[...] = jnp.zeros_like(token)

    lands = [pltpu.with_memory_space_constraint(lax.empty(sh, s.dtype), pltpu.HBM) for sh, s in zip(land_shapes, srcs)]
    outs = _call(
        body, name=name,
        out_shape=(pltpu.SemaphoreType.DMA((3 * n,)), pltpu.SemaphoreType.DMA((3 * n,)),
                   *[pltpu.HBM(s.shape, s.dtype) for s in srcs],
                   *[pltpu.HBM(sh, s.dtype) for sh, s in zip(land_shapes, srcs)],
                   jax.ShapeDtypeStruct((8, LANE), F32)),
        in_specs=[HBM_SPEC] * (2 * n) + [ANY] * len(after),
        out_specs=(SEM_SPEC, SEM_SPEC, *([HBM_SPEC] * (2 * n)), pl.BlockSpec(memory_space=pltpu.VMEM)),
        input_output_aliases={j: 2 + j for j in range(2 * n)},
        compiler_params=pltpu.CompilerParams(has_side_effects=DATAFLOW),
    )(*[pltpu.with_memory_space_constraint(s, pltpu.HBM) for s in srcs], *lands, *after)
    return outs[0], outs[1], list(outs[2:2 + n]), list(outs[2 + n:2 + 2 * n]), outs[-1]


def _chip_exchange_wait(kind, send, recv, srcs, lands, by_cols, after, *, name):
    n = len(srcs)

    def body(*refs):
        src_refs, land_refs = refs[:n], refs[n:2 * n]
        send_r, recv_r = refs[2 * n], refs[2 * n + 1]
        x, y, c, chips = _where_am_i()
        me = 2 * x + y
        for i in range(n):
            for k, (px, py) in enumerate(chips):
                s, _, d = _chip_exchange_refs(kind, srcs[i].shape, by_cols[i], src_refs[i], land_refs[i], i,
                                              2 * px + py, c, me)
                cp = pltpu.make_async_remote_copy(src_ref=s, dst_ref=d, send_sem=send_r.at[3 * i + k],
                                                  recv_sem=recv_r.at[3 * i + k], device_id=(px, py, c),
                                                  device_id_type=MESH)
                cp.wait_send()
                cp.wait_recv()

    outs = _call(
        body, name=name,
        out_shape=(*[pltpu.HBM(s.shape, s.dtype) for s in srcs], *[pltpu.HBM(l.shape, l.dtype) for l in lands]),
        in_specs=[HBM_SPEC] * (2 * n) + [SEM_SPEC, SEM_SPEC] + [ANY] * len(after),
        out_specs=tuple([HBM_SPEC] * (2 * n)),
        input_output_aliases={j: j for j in range(2 * n)},
        compiler_params=pltpu.CompilerParams(has_side_effects=DATAFLOW),
    )(*srcs, *lands, send, recv, *after)
    return list(outs[:n]), list(outs[n:])


def _sibling_halves_start(grads, by_cols, *, name, after=()):
    n = len(grads)
    land_shapes = [(N_CHIPS,) + _half_shape(*g.shape[1:], bc) for g, bc in zip(grads, by_cols)]

    def body(*refs):
        src_refs, land_refs = refs[:n], refs[n:2 * n]
        send, recv = refs[2 * n + len(after)], refs[2 * n + len(after) + 1]
        token = refs[-1]
        x, y, c, _ = _where_am_i()
        for i in range(n):
            src = src_refs[i].at[(slice(None),) + _half_idx(*grads[i].shape[1:], by_cols[i], 1 - c)]
            pltpu.make_async_remote_copy(src_ref=src, dst_ref=land_refs[i], send_sem=send.at[i], recv_sem=recv.at[i],
                                         device_id=(x, y, 1 - c), device_id_type=MESH).start()
        token[...] = jnp.zeros_like(token)

    lands = [pltpu.with_memory_space_constraint(lax.empty(sh, g.dtype), pltpu.HBM) for sh, g in zip(land_shapes, grads)]
    outs = _call(
        body, name=name,
        out_shape=(pltpu.SemaphoreType.DMA((n,)), pltpu.SemaphoreType.DMA((n,)),
                   *[pltpu.HBM(g.shape, g.dtype) for g in grads],
                   *[pltpu.HBM(sh, g.dtype) for sh, g in zip(land_shapes, grads)],
                   jax.ShapeDtypeStruct((8, LANE), F32)),
        in_specs=[HBM_SPEC] * (2 * n) + [ANY] * len(after),
        out_specs=(SEM_SPEC, SEM_SPEC, *([HBM_SPEC] * (2 * n)), pl.BlockSpec(memory_space=pltpu.VMEM)),
        input_output_aliases={j: 2 + j for j in range(2 * n)},
        compiler_params=pltpu.CompilerParams(has_side_effects=DATAFLOW),
    )(*[pltpu.with_memory_space_constraint(g, pltpu.HBM) for g in grads], *lands, *after)
    return outs[0], outs[1], list(outs[2:2 + n]), list(outs[2 + n:2 + 2 * n]), outs[-1]


def _sibling_halves_wait(send, recv, grads, lands, by_cols, after, *, name):
    n = len(grads)

    def body(*refs):
        src_refs, land_refs = refs[:n], refs[n:2 * n]
        send_r, recv_r = refs[2 * n], refs[2 * n + 1]
        x, y, c, _ = _where_am_i()
        for i in range(n):
            src = src_refs[i].at[(slice(None),) + _half_idx(*grads[i].shape[1:], by_cols[i], 1 - c)]
            cp = pltpu.make_async_remote_copy(src_ref=src, dst_ref=land_refs[i], send_sem=send_r.at[i],
                                              recv_sem=recv_r.at[i], device_id=(x, y, 1 - c), device_id_type=MESH)
            cp.wait_send()
            cp.wait_recv()

    outs = _call(
        body, name=name,
        out_shape=(*[pltpu.HBM(g.shape, g.dtype) for g in grads], *[pltpu.HBM(l.shape, l.dtype) for l in lands]),
        in_specs=[HBM_SPEC] * (2 * n) + [SEM_SPEC, SEM_SPEC] + [ANY] * len(after),
        out_specs=tuple([HBM_SPEC] * (2 * n)),
        input_output_aliases={j: j for j in range(2 * n)},
        compiler_params=pltpu.CompilerParams(has_side_effects=DATAFLOW),
    )(*grads, *lands, send, recv, *after)
    return list(outs[:n]), list(outs[n:])


def _sibling_pass_gathered(lands, shard_shapes, by_cols, *, name):
    n = len(lands)

    def body(*refs):
        outs = refs[n:2 * n]
        send, recv = refs[2 * n:]
        x, y, c, chips = _where_am_i()
        sibling = (x, y, 1 - c)
        cps = []
        for i in range(n):
            for k, (px, py) in enumerate(chips):
                blk = outs[i].at[(2 * px + py,) + _half_idx(*shard_shapes[i], by_cols[i], c)]
                d = pltpu.make_async_remote_copy(src_ref=blk, dst_ref=blk, send_sem=send.at[i, k],
                                                 recv_sem=recv.at[i, k], device_id=sibling, device_id_type=MESH)
                d.start()
                cps.append(d)
        for i in range(n):
            for k, (px, py) in enumerate(chips):
                blk = outs[i].at[(2 * px + py,) + _half_idx(*shard_shapes[i], by_cols[i], 1 - c)]
                pltpu.make_async_remote_copy(src_ref=blk, dst_ref=blk, send_sem=send.at[i, k], recv_sem=recv.at[i, k],
                                             device_id=sibling, device_id_type=MESH).wait_recv()
        for d in cps:
            d.wait_send()

    return _call(
        body, name=name, in_specs=[ANY] * n, out_specs=[ANY] * n,
        out_shape=[jax.ShapeDtypeStruct(l.shape, l.dtype) for l in lands],
        input_output_aliases={j: j for j in range(n)},
        scratch_shapes=[pltpu.SemaphoreType.DMA((n, 3)), pltpu.SemaphoreType.DMA((n, 3))],
    )(*lands)


def _own_slot(lands, owns):
    me = 2 * lax.axis_index("x") + lax.axis_index("y")
    return [lax.dynamic_update_slice_in_dim(g, s, me, axis=0) for g, s in zip(lands, owns)]


def _sibling_send_halves(grads, by_cols, *, name):
    n = len(grads)

    def body(*refs):
        ins, outs = refs[:n], refs[n:2 * n]
        send, recv = refs[2 * n:]
        x, y, c, _ = _where_am_i()
        sibling = (x, y, 1 - c)
        cps = []
        for i in range(n):
            src = ins[i].at[(slice(None),) + _half_idx(*grads[i].shape[1:], by_cols[i], 1 - c)]
            d = pltpu.make_async_remote_copy(src_ref=src, dst_ref=outs[i], send_sem=send.at[i],
                                             recv_sem=recv.at[i], device_id=sibling, device_id_type=MESH)
            d.start()
            cps.append(d)
        for d in cps:
            d.wait()

    return _call(
        body, name=name, in_specs=[ANY] * n, out_specs=[ANY] * n,
        out_shape=[jax.ShapeDtypeStruct((N_CHIPS,) + _half_shape(*g.shape[1:], bc), g.dtype)
                   for g, bc in zip(grads, by_cols)],
        scratch_shapes=[pltpu.SemaphoreType.DMA((n,)), pltpu.SemaphoreType.DMA((n,))],
    )(*grads)


def _scatter_to_chips(parts, *, name):
    n = len(parts)

    def body(*refs):
        ins, outs = refs[:n], refs[n:2 * n]
        send, recv = refs[2 * n:]
        x, y, c, chips = _where_am_i()
        me = 2 * x + y
        cps = []
        for i in range(n):
            for k, (px, py) in enumerate(chips):
                d = pltpu.make_async_remote_copy(
                    src_ref=ins[i].at[2 * px + py], dst_ref=outs[i].at[me], send_sem=send.at[i, k],
                    recv_sem=recv.at[i, k], device_id=(px, py, c), device_id_type=MESH)
                d.start()
                cps.append((d, i, k, px, py))
        for d, i, k, px, py in cps:
            blk = outs[i].at[2 * px + py]
            pltpu.make_async_remote_copy(src_ref=blk, dst_ref=blk, send_sem=send.at[i, k], recv_sem=recv.at[i, k],
                                         device_id=(px, py, c), device_id_type=MESH).wait_recv()
        for d, *_ in cps:
            d.wait_send()

    got = _call(
        body, name=name, in_specs=[ANY] * n, out_specs=[ANY] * n,
        out_shape=[jax.ShapeDtypeStruct(p.shape, p.dtype) for p in parts],
        scratch_shapes=[pltpu.SemaphoreType.DMA((n, 3)), pltpu.SemaphoreType.DMA((n, 3))],
    )(*parts)
    me = 2 * lax.axis_index("x") + lax.axis_index("y")
    return [lax.dynamic_update_slice_in_dim(g, lax.dynamic_slice_in_dim(p, me, 1, axis=0), me, axis=0)
            for g, p in zip(got, parts)]


def _sibling_join_halves(halves, *, name):
    n = len(halves)

    def body(*refs):
        ins, outs = refs[:n], refs[n:2 * n]
        send, recv = refs[2 * n:]
        x, y, c, _ = _where_am_i()
        sibling = (x, y, 1 - c)
        cps = []
        for i in range(n):
            d = pltpu.make_async_remote_copy(src_ref=ins[i], dst_ref=outs[i], send_sem=send.at[i],
                                             recv_sem=recv.at[i], device_id=sibling, device_id_type=MESH)
            d.start()
            cps.append(d)
        for d in cps:
            d.wait()

    return _call(
        body, name=name, in_specs=[ANY] * n, out_specs=[ANY] * n,
        out_shape=[jax.ShapeDtypeStruct(h.shape, h.dtype) for h in halves],
        scratch_shapes=[pltpu.SemaphoreType.DMA((n,)), pltpu.SemaphoreType.DMA((n,))],
    )(*halves)


def _all_reduce_small(v, *, name):
    R, C = v.shape

    def body(v_ref, o_ref, sib, slots, send, recv):
        x, y, c, chips = _where_am_i()
        me = 2 * x + y
        sibling = (x, y, 1 - c)
        d = pltpu.make_async_remote_copy(src_ref=v_ref, dst_ref=sib, send_sem=send.at[0], recv_sem=recv.at[0],
                                         device_id=sibling, device_id_type=MESH)
        d.start()
        d.wait()
        slots[me] = v_ref[...] + sib[...]
        cps = []
        for k, (px, py) in enumerate(chips):
            d = pltpu.make_async_remote_copy(src_ref=slots.at[me], dst_ref=slots.at[me], send_sem=send.at[1 + k],
                                             recv_sem=recv.at[1 + k], device_id=(px, py, c), device_id_type=MESH)
            d.start()
            cps.append(d)
        for k, (px, py) in enumerate(chips):
            blk = slots.at[2 * px + py]
            pltpu.make_async_remote_copy(src_ref=blk, dst_ref=blk, send_sem=send.at[1 + k], recv_sem=recv.at[1 + k],
                                         device_id=(px, py, c), device_id_type=MESH).wait_recv()
        for d in cps:
            d.wait_send()
        o_ref[...] = (slots[0] + slots[1]) + (slots[2] + slots[3])

    vm = pl.BlockSpec(memory_space=pltpu.VMEM)
    return _call(
        body, name=name, in_specs=[vm], out_specs=vm,
        out_shape=jax.ShapeDtypeStruct((R, C), F32),
        scratch_shapes=[pltpu.VMEM((R, C), F32), pltpu.VMEM((N_CHIPS, R, C), F32),
                        pltpu.SemaphoreType.DMA((4,)), pltpu.SemaphoreType.DMA((4,))],
        compiler_params=pltpu.CompilerParams(vmem_limit_bytes=VMEM_LIMIT),
    )(v)


def _cols_from_shards(g):
    return jnp.transpose(g, (1, 0, 2)).reshape(g.shape[1], -1)


def _shards_from_cols(w):
    R, C4 = w.shape
    return jnp.transpose(w.reshape(R, N_CHIPS, C4 // N_CHIPS), (1, 0, 2))


def _block_diag(t):
    G, a, b = t.shape
    eye = jnp.eye(G, dtype=t.dtype)
    return (t[:, :, None, :] * eye[:, None, :, None]).reshape(G * a, G * b)


def _diag_blocks(xm, G):
    a, b = xm.shape[0] // G, xm.shape[1] // G
    idx = jnp.arange(G)
    return xm.reshape(G, a, G, b)[idx, :, idx, :]


def _pack(arrs):
    flat = []
    for a in arrs:
        f = a.reshape(-1).astype(F32)
        flat.append(jnp.pad(f, (0, _rup(f.shape[0], LANE) - f.shape[0])))
    v = jnp.concatenate(flat)
    rows = _rup(v.shape[0] // LANE, 8)
    v = jnp.pad(v, (0, rows * LANE - v.shape[0]))
    return v.reshape(rows, LANE)


def _unpack(v, shapes):
    flat = v.reshape(-1)
    out, off = [], 0
    for s in shapes:
        n = int(np.prod(s))
        out.append(flat[off:off + n].reshape(s))
        off += _rup(n, LANE)
    return out


def _ffn_fwd(x, Wup, Wdn, cw, cb, tag):
    h = _mm(x, Wup, 'nt', bmode='bo', tm=512, tn=4096, name=f"ffn_up_{tag}")
    a = _act_fwd(h, cw, cb, name=f"ffn_act_{tag}")
    f = _mm(a, Wdn, 'nn', bmode='abr', tm=512, tn=1024, tk=4096, name=f"ffn_down_{tag}")
    return f, h, a


def _ffn_bwd(df, x, h, a, Wup, Wdn, cw, cb, tag):
    da = _mm(df, Wdn, 'nt', bmode='bo', tm=512, tn=4096, name=f"ffn_da_{tag}")
    dWdn = _mm(a, df, 'tn', bmode='ao', tm=4096, tn=512, name=f"ffn_dwdn_{tag}", out_dtype=BF16)
    dh, dcw, dcb = _act_bwd(h, da, cw, cb, name=f"ffn_actb_{tag}")

    def shard_of(k):
        return (k % 2) * 2 + k // 2

    dx = _mm(dh, Wup, 'nn', bmode='abr', tm=512, tn=1024, tk=4096, name=f"ffn_dx_{tag}", b_map=shard_of)
    dWup = _mm(dh, x, 'tn', bmode='ao', tm=4096, tn=512, name=f"ffn_dwup_{tag}", out_dtype=BF16,
               o_map=shard_of)
    return dx, dWup, dWdn, dcw, dcb


def kernel(x, positions, ev_w_in, ev_b_f, ev_lambda_re, ev_lambda_im, ev_log_step, ev_ssm_b_re, ev_ssm_b_im, ev_ssm_c_re, ev_ssm_c_im, ev_ssm_d, ev_w_glu, ev_w_out, od_w_in, od_sinks, od_w_out, ln_mix_g, ln_mix_b, ffn_w_up, ffn_conv_w, ffn_conv_b, ffn_w_down, ln_ffn_g, ln_ffn_b, loss_target, m_ev_w_in, m_ev_b_f, m_ev_lambda_re, m_ev_lambda_im, m_ev_log_step, m_ev_ssm_b_re, m_ev_ssm_b_im, m_ev_ssm_c_re, m_ev_ssm_c_im, m_ev_ssm_d, m_ev_w_glu, m_ev_w_out, m_od_w_in, m_od_sinks, m_od_w_out, m_ln_mix_g, m_ln_mix_b, m_ffn_w_up, m_ffn_conv_w, m_ffn_conv_b, m_ffn_w_down, m_ln_ffn_g, m_ln_ffn_b, v_ev_w_in, v_ev_b_f, v_ev_lambda_re, v_ev_lambda_im, v_ev_log_step, v_ev_ssm_b_re, v_ev_ssm_b_im, v_ev_ssm_c_re, v_ev_ssm_c_im, v_ev_ssm_d, v_ev_w_glu, v_ev_w_out, v_od_w_in, v_od_sinks, v_od_w_out, v_ln_mix_g, v_ln_mix_b, v_ffn_w_up, v_ffn_conv_w, v_ffn_conv_b, v_ffn_w_down, v_ln_ffn_g, v_ln_ffn_b):
    W = dict(ev_w_in=ev_w_in, ev_b_f=ev_b_f, ev_lambda_re=ev_lambda_re, ev_lambda_im=ev_lambda_im, ev_log_step=ev_log_step, ev_ssm_b_re=ev_ssm_b_re, ev_ssm_b_im=ev_ssm_b_im, ev_ssm_c_re=ev_ssm_c_re, ev_ssm_c_im=ev_ssm_c_im, ev_ssm_d=ev_ssm_d, ev_w_glu=ev_w_glu, ev_w_out=ev_w_out, od_w_in=od_w_in, od_sinks=od_sinks, od_w_out=od_w_out, ln_mix_g=ln_mix_g, ln_mix_b=ln_mix_b, ffn_w_up=ffn_w_up, ffn_conv_w=ffn_conv_w, ffn_conv_b=ffn_conv_b, ffn_w_down=ffn_w_down, ln_ffn_g=ln_ffn_g, ln_ffn_b=ln_ffn_b)
    Mo = dict(ev_w_in=m_ev_w_in, ev_b_f=m_ev_b_f, ev_lambda_re=m_ev_lambda_re, ev_lambda_im=m_ev_lambda_im, ev_log_step=m_ev_log_step, ev_ssm_b_re=m_ev_ssm_b_re, ev_ssm_b_im=m_ev_ssm_b_im, ev_ssm_c_re=m_ev_ssm_c_re, ev_ssm_c_im=m_ev_ssm_c_im, ev_ssm_d=m_ev_ssm_d, ev_w_glu=m_ev_w_glu, ev_w_out=m_ev_w_out, od_w_in=m_od_w_in, od_sinks=m_od_sinks, od_w_out=m_od_w_out, ln_mix_g=m_ln_mix_g, ln_mix_b=m_ln_mix_b, ffn_w_up=m_ffn_w_up, ffn_conv_w=m_ffn_conv_w, ffn_conv_b=m_ffn_conv_b, ffn_w_down=m_ffn_w_down, ln_ffn_g=m_ln_ffn_g, ln_ffn_b=m_ln_ffn_b)
    Vo = dict(ev_w_in=v_ev_w_in, ev_b_f=v_ev_b_f, ev_lambda_re=v_ev_lambda_re, ev_lambda_im=v_ev_lambda_im, ev_log_step=v_ev_log_step, ev_ssm_b_re=v_ev_ssm_b_re, ev_ssm_b_im=v_ev_ssm_b_im, ev_ssm_c_re=v_ev_ssm_c_re, ev_ssm_c_im=v_ev_ssm_c_im, ev_ssm_d=v_ev_ssm_d, ev_w_glu=v_ev_w_glu, ev_w_out=v_ev_w_out, od_w_in=v_od_w_in, od_sinks=v_od_sinks, od_w_out=v_od_w_out, ln_mix_g=v_ln_mix_g, ln_mix_b=v_ln_mix_b, ffn_w_up=v_ffn_w_up, ffn_conv_w=v_ffn_conv_w, ffn_conv_b=v_ffn_conv_b, ffn_w_down=v_ffn_w_down, ln_ffn_g=v_ln_ffn_g, ln_ffn_b=v_ln_ffn_b)
    names = list(W.keys())
    big = ['ev_w_in', 'ev_w_glu', 'ev_w_out', 'od_w_in', 'od_w_out', 'ffn_w_up', 'ffn_w_down']

    S, D = x.shape[1], x.shape[2]
    x0 = x.reshape(S, D)
    tgt = loss_target.reshape(S, D)
    G, Pn, Cg = SSM_GROUPS, SSM_STATE, SSM_GROUP
    Fs = ffn_w_up.shape[2]
    FP = Fs
    Rd = ffn_w_down.shape[1]
    EIN = N_CHIPS * ev_w_in.shape[2]

    def as2d(a):
        return a.reshape(-1, a.shape[-1])

    cwl = ffn_conv_w.reshape(-1)
    cw_rows = _rup(_rup(cwl.shape[0], LANE) // LANE, 32)
    cw_pad = jnp.pad(cwl, (0, cw_rows * LANE - cwl.shape[0])).reshape(cw_rows, LANE)
    transposed = ('ev_w_in', 'ffn_w_up')

    def view(n, a):
        return jnp.transpose(a, (0, 2, 1)) if n in transposed else a

    Wv = {n: view(n, W[n]) for n in big}
    big_e = [(n, l) for n in big for l in range(W[n].shape[0])]
    split_cols = {e: (Wv[e[0]].shape[1] // 2) % 16 != 0 for e in big_e}
    shard16 = {e: Wv[e[0]][e[1]].astype(BF16) for e in big_e}
    grp_now = [e for e in big_e if e[0].startswith('ev_')]
    grp_ffn0 = [('ffn_w_up', 0), ('ffn_w_down', 0)]
    grp_l1 = [('od_w_in', 0), ('od_w_out', 0), ('ffn_w_up', 1), ('ffn_w_down', 1)]
    src_now = [shard16[e] for e in grp_now]
    src_ffn0 = [shard16[e] for e in grp_ffn0] + [cw_pad]
    src_l1 = [shard16[e] for e in grp_l1]
    cols_now = [split_cols[e] for e in grp_now]
    cols_ffn0 = [split_cols[e] for e in grp_ffn0] + [False]
    cols_l1 = [split_cols[e] for e in grp_l1]
    ag_now = _chip_exchange_start('gather', src_now, cols_now, name="ag_l0_start")
    ag_ffn0 = _chip_exchange_start('gather', src_ffn0, cols_ffn0, name="ag_ffn0_start", after=[ag_now[4]])
    ag_l1 = _chip_exchange_start('gather', src_l1, cols_l1, name="ag_l1_start", after=[ag_ffn0[4]])
    started = [ag_l1[4]]

    def finish_gather(started, srcs, cols, after, tag):
        send, recv, thru, lands, _ = started
        thru, lands = _chip_exchange_wait('gather', send, recv, thru, lands, cols, after, name=f"ag_{tag}_wait")
        lands = _sibling_pass_gathered(lands, [s.shape for s in srcs], cols, name=f"ag_{tag}_pass")
        return _own_slot(lands, [s[None] for s in thru])

    lam_r, lam_i = ev_lambda_re[0], ev_lambda_im[0]
    lstep = ev_log_step[0].reshape(G, 1)
    a_re, a_im, g_re, g_im = _s5_disc_fwd(lam_r, lam_i, lstep, name="s5_disc", after=started)
    b_re2, b_im2 = ev_ssm_b_re[0].reshape(G * Pn, Cg), ev_ssm_b_im[0].reshape(G * Pn, Cg)
    g_re1, g_im1 = g_re.reshape(G * Pn, 1), g_im.reshape(G * Pn, 1)
    bb_re, bb_im = _s5_bb_fwd(g_re1, g_im1, b_re2, b_im2, name="s5_bb")
    bbt = jnp.stack([jnp.transpose(b.reshape(G, Pn, Cg), (0, 2, 1)).reshape(G * Cg, Pn) for b in (bb_re, bb_im)])
    BB = _diag_expand(bbt, Cg, Pn, name="s5_bb_dense")
    cct = jnp.stack([jnp.transpose(ev_ssm_c_re[0], (0, 2, 1)).reshape(G * Pn, Cg),
                     jnp.transpose(-ev_ssm_c_im[0], (0, 2, 1)).reshape(G * Pn, Cg)])
    CC = _diag_expand(cct, Pn, Cg, name="s5_cc_dense", after=started)
    a_cat = jnp.stack([a_re.reshape(1, G * Pn), a_im.reshape(1, G * Pn)])
    dskip = ev_ssm_d[0].reshape(1, SSM_WIDTH)
    tabs = _rope_tables(positions.reshape(S, 1).astype(F32), name="rope_tables", after=[BB, CC])

    gw = dict(zip(grp_now, finish_gather(ag_now, src_now, cols_now, [tabs[2]], "l0")))
    gw.update({n: gw[(n, 0)] for n in big if (n, 0) in gw and W[n].shape[0] == 1})
    w_in_t = gw['ev_w_in'].reshape(EIN, D)
    qkv_w = 3 * FOX_WIDTH
    WmainT = jnp.concatenate([w_in_t[:qkv_w], w_in_t[qkv_w + FOX_HEADS:]], axis=0)
    WfT = jnp.pad(w_in_t[qkv_w:qkv_w + FOX_HEADS], ((0, LANE - FOX_HEADS), (0, 0)))
    Wglu = _cols_from_shards(gw['ev_w_glu'])
    Wout_ev = gw['ev_w_out'].reshape(D, D)
    cbs = [ffn_conv_b[l].reshape(N_CHIPS, Fs) for l in range(DEPTH)]

    P = _mm(x0, WmainT, 'nt', name="ev_proj")
    fl = _mm(x0, WfT, 'nt', name="ev_proj_f")
    bf_pad = jnp.pad(ev_b_f.reshape(1, FOX_HEADS), ((0, 0), (0, LANE - FOX_HEADS)))
    cgate, sgate = _gate_fwd(fl, bf_pad, name="fox_gate")
    ccol = jnp.transpose(cgate[:, :FOX_HEADS]).reshape(FOX_HEADS, S, 1)
    crow = jnp.transpose(cgate[:, :FOX_HEADS]).reshape(FOX_HEADS, 1, S)
    fox, lse = _fox_fwd(P, ccol, crow, name="fox_fwd")
    u_s5 = P[:, qkv_w:]
    UT, HT = _DIAG_TILE * Cg, _DIAG_TILE * Pn
    bu = _mm(u_s5, BB, 'nn', bmode='bo', tm=2048, tn=HT, tk=UT, diag='kn', name="s5_bu")
    hh = _s5_scan_fwd(bu, a_cat, name="s5_scan")
    yc = _mm(hh, CC, 'nn', bmode='abr', tm=2048, tn=UT, tk=HT, diag='kn', name="s5_y")
    y_s5, yg = _s5_out_fwd(yc, P, dskip, name="s5_out")
    z = _mm(yg, Wglu, 'nn', name="s5_glu_proj")
    ssm = _glu_fwd(z, name="s5_glu")
    cat = jnp.concatenate([fox.astype(BF16), ssm], axis=1)
    mix0 = _mm(cat, Wout_ev, 'nn', name="ev_out")
    x1, xh1, rs1 = _add_ln_fwd(x0, mix0, ln_mix_g[0], ln_mix_b[0], name="ln_mix0")
    got = finish_gather(ag_ffn0, src_ffn0, cols_ffn0, [x1], "ffn0")
    gw.update(zip(grp_ffn0, got[:-1]))
    cw_all = got[-1].reshape(N_CHIPS, -1)[:, :cwl.shape[0]].reshape(N_CHIPS, DEPTH, 3, Fs)
    cws = [cw_all[:, l] for l in range(DEPTH)]
    Wup = {0: gw[('ffn_w_up', 0)]}
    Wdn = {0: gw[('ffn_w_down', 0)].reshape(2, Fs, D)}
    f0, hf0, af0 = _ffn_fwd(x1, Wup[0], Wdn[0], cws[0], cbs[0], "l0")
    x2, xh2, rs2 = _add_ln_fwd(x1, f0, ln_ffn_g[0], ln_ffn_b[0], name="ln_ffn0")

    gw.update(zip(grp_l1, finish_gather(ag_l1, src_l1, cols_l1, [x2], "l1")))
    Wodin = _cols_from_shards(gw[('od_w_in', 0)])
    Wodout = gw[('od_w_out', 0)].reshape(D, D)
    Wup[1] = gw[('ffn_w_up', 1)]
    Wdn[1] = gw[('ffn_w_down', 1)].reshape(2, Fs, D)
    QW, KW = SWA_HEADS * SWA_HEAD_DIM, SWA_KV_HEADS * SWA_HEAD_DIM
    P1 = _mm(x2, Wodin, 'nn', name="od_proj")
    qr = _rope_apply(P1, tabs, col0=0, width=QW, inverse=False, name="rope_q", out_dtype=BF16)
    kr = _rope_apply(P1, tabs, col0=QW, width=KW, inverse=False, name="rope_k", out_dtype=BF16)

    def heads(a2, nh):
        return jnp.transpose(a2.reshape(S, nh, SWA_HEAD_DIM), (1, 0, 2))

    def unheads(a3):
        return jnp.transpose(a3, (1, 0, 2)).reshape(S, -1)

    qT, kT = heads(qr, SWA_HEADS), heads(kr, SWA_KV_HEADS)
    vT = heads(P1[:, QW + KW:].astype(BF16), SWA_KV_HEADS)
    sink_rows = jnp.broadcast_to(od_sinks[0].reshape(SWA_KV_HEADS, SWA_GROUPS, 1, 1),
                                 (SWA_KV_HEADS, SWA_GROUPS, SWA_WINDOW, 1)).reshape(SWA_KV_HEADS, -1, 1)
    oT, Lsw = _swa_fwd(qT, kT, vT, sink_rows, name="swa_fwd")
    o_sw = unheads(oT).astype(BF16)
    mix1 = _mm(o_sw, Wodout, 'nn', name="od_out")
    x3, xh3, rs3 = _add_ln_fwd(x2, mix1, ln_mix_g[1], ln_mix_b[1], name="ln_mix1")
    f1, hf1, af1 = _ffn_fwd(x3, Wup[1], Wdn[1], cws[1], cbs[1], "l1")
    x4, xh4, rs4 = _add_ln_fwd(x3, f1, ln_ffn_g[1], ln_ffn_b[1], name="ln_ffn1")
    dy, loss_part = _loss_grad(x4, tgt, name="loss")

    dz4, dg_ffn1, db_ffn1 = _ln_bwd(dy, None, xh4, rs4, ln_ffn_g[1], name="lnb_ffn1")
    dx3f, dWup1, dWdn1, dcw1, dcb1 = _ffn_bwd(dz4, x3, hf1, af1, Wup[1], Wdn[1], cws[1], cbs[1], "l1")
    sib_ffn1 = _sibling_halves_start([dWup1, dWdn1.reshape(N_CHIPS, Rd, D)], [False, False], name="rs_ffn1_sib_start")
    dz3, dg_mix1, db_mix1 = _ln_bwd(dz4, dx3f, xh3, rs3, ln_mix_g[1], name="lnb_mix1", after=[sib_ffn1[4]])
    do_sw = _mm(dz3, Wodout, 'nt', name="od_out_dx")
    dWodout = _mm(o_sw, dz3, 'tn', name="od_out_dw", out_dtype=BF16)
    doT = heads(do_sw, SWA_HEADS)
    dqT, dkT, dvT, dsink = _swa_bwd(qT, kT, vT, sink_rows, oT, Lsw, doT, name="swa_bwd")
    dq1 = _rope_apply(unheads(dqT), tabs, col0=0, width=QW, inverse=True, name="rope_dq", out_dtype=BF16)
    dk1 = _rope_apply(unheads(dkT[:, SWA_WINDOW:]), tabs, col0=0, width=KW, inverse=True, name="rope_dk",
                      out_dtype=BF16)
    dP1 = jnp.concatenate([dq1, dk1, unheads(dvT[:, SWA_WINDOW:]).astype(BF16)], axis=1)
    dx2m = _mm(dP1, Wodin, 'nt', name="od_proj_dx")
    dWodin = _mm(x2, dP1, 'tn', name="od_proj_dw", out_dtype=BF16)

    def rs_begin(entries, grads, tag):
        cols = [split_cols[e] for e in entries]
        sib = _sibling_send_halves(grads, cols, name=f"rs_{tag}_sibling")
        return [_sum2_halves(g4, s4, bc, name=f"rs_sum2_{n}{l}")
                for (n, l), g4, s4, bc in zip(entries, grads, sib, cols)]

    def rs_begin_started(entries, started, after, tag):
        send, rcv, thru, lands, _ = started
        thru, lands = _sibling_halves_wait(send, rcv, thru, lands, [False] * len(thru), after,
                                           name=f"rs_{tag}_sib_wait")
        return [_sum2_halves(g4, s4, False, name=f"rs_sum2_{n}{l}") for (n, l), g4, s4 in zip(entries, thru, lands)]

    def own_parts(parts):
        me = 2 * lax.axis_index("x") + lax.axis_index("y")
        return [lax.dynamic_slice_in_dim(p, me, 1, axis=0) for p in parts]

    part_l1 = (rs_begin(grp_l1[:2], [_shards_from_cols(dWodin), dWodout.reshape(N_CHIPS, D // N_CHIPS, D)], "od")
               + rs_begin_started(grp_l1[2:], sib_ffn1, [dWodin], "ffn1"))
    rs_l1 = _chip_exchange_start('scatter', part_l1, [False] * len(part_l1), name="rs_l1_start")

    dz2, dg_ffn0, db_ffn0 = _ln_bwd(dz3, dx2m, xh2, rs2, ln_ffn_g[0], name="lnb_ffn0", after=[rs_l1[4]])
    dx1f, dWup0, dWdn0, dcw0, dcb0 = _ffn_bwd(dz2, x1, hf0, af0, Wup[0], Wdn[0], cws[0], cbs[0], "l0")
    sib_ffn0 = _sibling_halves_start([dWup0, dWdn0.reshape(N_CHIPS, Rd, D)], [False, False], name="rs_ffn0_sib_start")
    dz1, dg_mix0, db_mix0 = _ln_bwd(dz2, dx1f, xh1, rs1, ln_mix_g[0], name="lnb_mix0", after=[sib_ffn0[4]])
    dcat = _mm(dz1, Wout_ev, 'nt', name="ev_out_dx")
    dWout_ev = _mm(cat, dz1, 'tn', name="ev_out_dw", out_dtype=BF16)
    part_ffn0 = rs_begin_started(grp_ffn0, sib_ffn0, [dWout_ev], "ffn0")
    rs_ffn0 = _chip_exchange_start('scatter', part_ffn0, [False] * len(part_ffn0), name="rs_ffn0_start")
    dz = _glu_bwd(z, dcat, name="s5_glu_bwd")
    dyg = _mm(dz, Wglu, 'nt', name="s5_glu_dx", after=[rs_ffn0[4]])
    dWglu = _mm(yg, dz, 'tn', name="s5_glu_dw", out_dtype=BF16)
    dy_s5, du_dir, dD = _s5_out_bwd(dyg, y_s5, P, dskip, name="s5_out_bwd")
    dhh = _mm(dy_s5, CC, 'nt', bmode='bo', tm=2048, tn=HT, tk=UT, diag='kn', name="s5_y_dx")
    dCC = _mm(hh, dy_s5, 'tn', bmode='ao', tm=HT, tn=UT, diag='mn', name="s5_y_dw")
    lam, da_s5 = _s5_scan_bwd(dhh, hh, a_cat, name="s5_scan_bwd")
    du_bu = _mm(lam, BB, 'nt', bmode='abr', tm=2048, tn=UT, tk=HT, diag='kn', name="s5_bu_dx")
    dBB = _mm(u_s5, lam, 'tn', bmode='bo', tm=UT, tn=HT, diag='mn', name="s5_bu_dw")
    du = _combine([du_dir, du_bu], [1.0, 1.0], name="s5_du", out_dtype=BF16)
    dq0, dk0, dv0, dccol, dcrow = _fox_bwd(P, ccol, crow, fox, lse, dcat, name="fox_bwd")
    dc = jnp.transpose((dccol.reshape(FOX_HEADS, S) - dcrow.reshape(FOX_HEADS, S)))
    dc = jnp.pad(dc, ((0, 0), (0, LANE - FOX_HEADS)))
    dfl, dbf = _gate_bwd(dc, sgate, name="fox_gate_bwd")
    dP = jnp.concatenate([dq0, dk0, dv0, du], axis=1)
    dx0a = _mm(dP, WmainT, 'nn', name="ev_proj_dx")
    dx0b = _mm(dfl, WfT, 'nn', name="ev_proj_f_dx")
    dWmainT = _mm(dP, x0, 'tn', tm=1024, tn=1024, name="ev_proj_dw", out_dtype=BF16)
    dWfT = _mm(dfl, x0, 'tn', name="ev_proj_f_dw", out_dtype=BF16)
    grad_x = _combine([dz1, dx0a, dx0b], [ALPHA, 1.0, 1.0], name="grad_x")

    dbbt = _diag_extract(dBB, Cg, Pn, name="s5_bb_diag")
    dcct = _diag_extract(dCC, Pn, Cg, name="s5_cc_diag")
    dbb_re = jnp.transpose(dbbt[0].reshape(G, Cg, Pn), (0, 2, 1)).reshape(G * Pn, Cg)
    dbb_im = jnp.transpose(dbbt[1].reshape(G, Cg, Pn), (0, 2, 1)).reshape(G * Pn, Cg)
    db_re, db_im, dg_re1, dg_im1 = _s5_bb_bwd(g_re1, g_im1, b_re2, b_im2, dbb_re, dbb_im, name="s5_bb_bwd")
    dlam_re, dlam_im, dlstep = _s5_disc_bwd(lam_r, lam_i, lstep, da_s5[0].reshape(G, Pn), da_s5[1].reshape(G, Pn),
                                            dg_re1.reshape(G, Pn), dg_im1.reshape(G, Pn), name="s5_disc_bwd")
    dc_re = jnp.transpose(dcct[0].reshape(G, Pn, Cg), (0, 2, 1))
    dc_im = -jnp.transpose(dcct[1].reshape(G, Pn, Cg), (0, 2, 1))

    def conv_w_full(d0, d1):
        return jnp.stack([jnp.reshape(jnp.transpose(d[:, :, :Fs], (1, 0, 2)), (3, N_CHIPS * Fs)) for d in (d0, d1)])

    def conv_b_full(d0, d1):
        return jnp.stack([jnp.reshape(d[:, 0, :Fs], (N_CHIPS * Fs,)) for d in (d0, d1)])

    small_local = dict(
        ev_b_f=dbf[:, :FOX_HEADS], ev_lambda_re=dlam_re, ev_lambda_im=dlam_im, ev_log_step=dlstep,
        ev_ssm_b_re=db_re, ev_ssm_b_im=db_im, ev_ssm_c_re=dc_re, ev_ssm_c_im=dc_im, ev_ssm_d=dD,
        od_sinks=dsink[:, :, 0],
        ln_mix_g=jnp.concatenate([dg_mix0, dg_mix1]), ln_mix_b=jnp.concatenate([db_mix0, db_mix1]),
        ffn_conv_w=conv_w_full(dcw0, dcw1), ffn_conv_b=conv_b_full(dcb0, dcb1),
        ln_ffn_g=jnp.concatenate([dg_ffn0, dg_ffn1]), ln_ffn_b=jnp.concatenate([db_ffn0, db_ffn1]))
    small = list(small_local.keys())
    red = _all_reduce_small(_pack([small_local[n] for n in small] + [loss_part]), name="ar_small")
    full_shapes = [W[n].shape if n != 'ffn_conv_w' else (DEPTH, 3, N_CHIPS * Fs) for n in small]
    pieces = _unpack(red, full_shapes + [()])
    loss = pieces[-1]
    gsmall = dict(zip(small, pieces[:-1]))
    chip = 2 * lax.axis_index("x") + lax.axis_index("y")
    gsmall['ffn_conv_w'] = lax.dynamic_slice_in_dim(gsmall['ffn_conv_w'], chip * Fs, Fs, axis=2)
    shapes = [W[n].shape for n in small]
    gs, ds_, ms, vs = _adamw(_pack([W[n] for n in small])[None], _pack([gsmall[n] for n in small])[None],
                             _pack([Mo[n] for n in small])[None], _pack([Vo[n] for n in small])[None],
                             name="adamw_small", tr=1 << 14)
    out_g = dict(zip(small, _unpack(gs, shapes)))
    out_d = dict(zip(small, _unpack(ds_, shapes)))
    out_m = dict(zip(small, _unpack(ms, shapes)))
    out_v = dict(zip(small, _unpack(vs, shapes)))

    dw_in_t = jnp.concatenate([dWmainT[:qkv_w], dWfT[:FOX_HEADS], dWmainT[qkv_w:]], axis=0)
    part_now = rs_begin(grp_now, [dw_in_t.reshape(N_CHIPS, EIN // N_CHIPS, D), _shards_from_cols(dWglu),
                                  dWout_ev.reshape(N_CHIPS, D // N_CHIPS, D)], "l0")
    rs_now = _chip_exchange_start('scatter', part_now, [False] * len(part_now), name="rs_l0_start")

    def finish_scatter(started, parts, after, tag):
        send, rcv, thru, lands, _ = started
        thru, lands = _chip_exchange_wait('scatter', send, rcv, thru, lands, [False] * len(parts), after,
                                          name=f"rs_{tag}_wait")
        return _own_slot(lands, own_parts(thru))

    def update(entries, recv, tag):
        halves = [_rowsum(r, name=f"rs_sum4_{e[0]}{e[1]}") for e, r in zip(entries, recv)]
        others = _sibling_join_halves(halves, name=f"rs_{tag}_join")
        pairs = dict(zip(entries, zip(halves, others)))
        done = []
        for n in dict.fromkeys(e[0] for e in entries):
            res = _adamw(Wv[n], [pairs[(n, l)] for l in range(W[n].shape[0])], view(n, Mo[n]), view(n, Vo[n]),
                         name=f"adamw_{n}", by_cols=split_cols[(n, 0)])
            out_g[n], out_d[n], out_m[n], out_v[n] = (view(n, t) for t in res)
            done.append(res[3])
        return done

    recv_rest = (finish_scatter(rs_l1, part_l1, [rs_now[4]], "l1")
                 + finish_scatter(rs_ffn0, part_ffn0, [rs_now[4]], "ffn0"))
    done = update(grp_l1 + grp_ffn0, recv_rest, "rest")
    update(grp_now, finish_scatter(rs_now, part_now, done, "l0"), "l0")

    return (loss, grad_x.reshape(1, S, D), *[out_g[n] for n in names], *[out_d[n] for n in names],
            *[out_m[n] for n in names], *[out_v[n] for n in names])
```

```python
import functools
import math

import numpy as np
import jax
import jax.numpy as jnp
from jax import lax
from jax.experimental import pallas as pl
from jax.experimental.pallas import tpu as pltpu

F32 = jnp.float32
BF16 = jnp.bfloat16
MESH = pl.DeviceIdType.MESH
ANY = pl.BlockSpec(memory_space=pl.ANY)

D_MODEL = 2048
FOX_HEADS = 8
FOX_HEAD_DIM = 128
FOX_WIDTH = 1024
SSM_WIDTH = 1024
SSM_GROUP = 16
SSM_GROUPS = 64
SSM_STATE = 64
SWA_HEADS = 32
SWA_KV_HEADS = 4
SWA_HEAD_DIM = 64
SWA_GROUPS = 8
SWA_WINDOW = 128
ROPE_DIM = 16
ROPE_THETA = 500000.0
LN_EPS = 1e-5
DEPTH = 2
ALPHA = (2.0 * DEPTH) ** 0.25
ADAM_LR = 0.001
ADAM_B1 = 0.9
ADAM_B2 = 0.999
ADAM_EPS = 1e-08
ADAM_WD = 0.01
ADAM_STEP = 10
N_CHIPS = 4

VMEM_LIMIT = 56 * 1024 * 1024
LANE = 128


def _call(body, after=(), **kw):
    if after:
        n = len(after)

        def shifted(*refs):
            return body(*refs[n:])

        call = _call(shifted, **dict(kw, in_specs=[ANY] * n + list(kw["in_specs"])))
        return lambda *args: call(*after, *args)
    return pl.pallas_call(body, **kw)


def _cparams(sem):
    return pltpu.CompilerParams(dimension_semantics=sem, vmem_limit_bytes=VMEM_LIMIT)


def _rup(n, m):
    return (n + m - 1) // m * m


def _pick(n, pref):
    if n <= pref:
        return n
    for step in (128, 16, 8):
        for t in range(pref - pref % step, 0, -step):
            if n % t == 0:
                return t
    return n


def _tile2d(rows, cols, pref_rows=256, budget=256 * 1024):
    tr = _pick(rows, pref_rows)
    if tr < 64:
        tr = rows
    if cols % LANE:
        return tr, cols
    return tr, _pick(cols, max(LANE, budget // tr // LANE * LANE))


def _mm(a, b, mode, *, name, tm=512, tn=1024, tk=2048, bmode=None, out_dtype=F32, after=(), b_map=None,
        o_map=None, diag=None):
    a3 = a if a.ndim == 3 else a[None]
    b3 = b if b.ndim == 3 else b[None]
    if mode == 'tn':
        K, M = a3.shape[1:]
    else:
        M, K = a3.shape[1:]
    N = b3.shape[1] if mode == 'nt' else b3.shape[2]
    tm, tn, tk = _pick(M, tm), _pick(N, tn), _pick(K, tk)
    nb = max(a3.shape[0], b3.shape[0])
    nbo, nbr = (1, nb) if bmode == 'abr' else (nb, 1)
    nm, nk = M // tm, K // tk
    if diag == 'kn':
        assert K // tk == N // tn
        nk = 1
    if diag == 'mn':
        assert M // tm == N // tn
        nm = 1
    nred = nbr * nk
    a_b = bmode in ('ao', 'abr')
    b_b = bmode in ('bo', 'abr')
    o_b = bmode in ('bo', 'ao')

    def bsel(flag, bo, br, remap=None):
        if not flag:
            return 0
        return (bo + br) if remap is None else remap(bo + br)

    def mi(i, j):
        return j if diag == 'mn' else i

    def ki(j, k):
        return j if diag == 'kn' else k

    if mode == 'tn':
        a_spec = pl.BlockSpec((None, tk, tm), lambda bo, i, j, br, k: (bsel(a_b, bo, br), ki(j, k), mi(i, j)))
    else:
        a_spec = pl.BlockSpec((None, tm, tk), lambda bo, i, j, br, k: (bsel(a_b, bo, br), mi(i, j), ki(j, k)))
    if mode == 'nt':
        b_spec = pl.BlockSpec((None, tn, tk), lambda bo, i, j, br, k: (bsel(b_b, bo, br, b_map), j, ki(j, k)))
    else:
        b_spec = pl.BlockSpec((None, tk, tn), lambda bo, i, j, br, k: (bsel(b_b, bo, br, b_map), ki(j, k), j))
    o_spec = pl.BlockSpec((None, tm, tn), lambda bo, i, j, br, k: (bsel(o_b, bo, br, o_map), mi(i, j), j))
    dn = {'nn': (((1,), (0,)), ((), ())), 'nt': (((1,), (1,)), ((), ())), 'tn': (((0,), (0,)), ((), ()))}[mode]

    def body(a_ref, b_ref, *rest):
        o_ref, scratch = rest[len(after)], rest[len(after) + 1:]
        r = lax.dot_general(a_ref[...].astype(BF16), b_ref[...].astype(BF16), dn, preferred_element_type=F32)
        if nred == 1:
            o_ref[...] = r.astype(out_dtype)
        else:
            acc = scratch[0]
            step = pl.program_id(3) * nk + pl.program_id(4)

            @pl.when(step == 0)
            def _():
                acc[...] = r

            @pl.when(step > 0)
            def _():
                acc[...] += r

            @pl.when(step == nred - 1)
            def _():
                o_ref[...] = acc[...].astype(out_dtype)

    out = _call(
        body, name=name,
        grid=(nbo, nm, N // tn, nbr, nk),
        in_specs=[a_spec, b_spec] + [ANY] * len(after), out_specs=o_spec,
        out_shape=jax.ShapeDtypeStruct((nbo if o_b else 1, M, N), out_dtype),
        scratch_shapes=[] if nred == 1 else [pltpu.VMEM((tm, tn), F32)],
        compiler_params=_cparams(("parallel", "parallel", "parallel", "arbitrary", "arbitrary")),
    )(a3, b3, *after)
    return out if o_b else out[0]


def _add_ln_fwd(x, r, g, b, *, name):
    S, D = x.shape
    tr = _pick(S, 256)

    def body(x_ref, r_ref, g_ref, b_ref, o_ref, xh_ref, rs_ref):
        z = ALPHA * x_ref[...] + r_ref[...]
        mu = jnp.mean(z, axis=-1, keepdims=True)
        zc = z - mu
        var = jnp.mean(zc * zc, axis=-1, keepdims=True)
        rstd = lax.rsqrt(var + LN_EPS)
        xh = zc * rstd
        xh_ref[...] = xh
        rs_ref[...] = rstd
        o_ref[...] = xh * g_ref[...] + b_ref[...]

    row = pl.BlockSpec((tr, D), lambda i: (i, 0))
    vec = pl.BlockSpec((1, D), lambda i: (0, 0))
    return _call(
        body, name=name, grid=(S // tr,),
        in_specs=[row, row, vec, vec],
        out_specs=[row, row, pl.BlockSpec((tr, 1), lambda i: (i, 0))],
        out_shape=[jax.ShapeDtypeStruct((S, D), F32), jax.ShapeDtypeStruct((S, D), F32),
                   jax.ShapeDtypeStruct((S, 1), F32)],
        compiler_params=_cparams(("parallel",)),
    )(x, r, g.reshape(1, D), b.reshape(1, D))


def _ln_bwd(da, db, xhat, rstd, g, *, name, after=()):
    S, D = xhat.shape
    tr = _pick(S, 256)
    two = db is not None

    def body(*refs):
        refs = refs[len(after):]
        if two:
            da_ref, db_ref, xh_ref, rs_ref, g_ref, dz_ref, dg_ref, dbt_ref = refs
            dy = ALPHA * da_ref[...] + db_ref[...]
        else:
            da_ref, xh_ref, rs_ref, g_ref, dz_ref, dg_ref, dbt_ref = refs
            dy = da_ref[...]
        xh = xh_ref[...]
        dxh = dy * g_ref[...]
        m1 = jnp.mean(dxh, axis=-1, keepdims=True)
        m2 = jnp.mean(dxh * xh, axis=-1, keepdims=True)
        dz_ref[...] = rs_ref[...] * (dxh - m1 - xh * m2)
        pg = jnp.sum(dy * xh, axis=0, keepdims=True)
        pb = jnp.sum(dy, axis=0, keepdims=True)

        @pl.when(pl.program_id(0) == 0)
        def _():
            dg_ref[...] = pg
            dbt_ref[...] = pb

        @pl.when(pl.program_id(0) > 0)
        def _():
            dg_ref[...] += pg
            dbt_ref[...] += pb

    row = pl.BlockSpec((tr, D), lambda i: (i, 0))
    vec = pl.BlockSpec((1, D), lambda i: (0, 0))
    ins = list(after) + [da] + ([db] if two else []) + [xhat, rstd, g.reshape(1, D)]
    in_specs = [ANY] * len(after) + [row] + ([row] if two else []) + [row, pl.BlockSpec((tr, 1), lambda i: (i, 0)), vec]
    return _call(
        body, name=name, grid=(S // tr,),
        in_specs=in_specs, out_specs=[row, vec, vec],
        out_shape=[jax.ShapeDtypeStruct((S, D), F32), jax.ShapeDtypeStruct((1, D), F32),
                   jax.ShapeDtypeStruct((1, D), F32)],
        compiler_params=_cparams(("arbitrary",)),
    )(*ins)


def _loss_grad(y, t, *, name):
    S, D = y.shape
    tr = _pick(S, 256)

    def body(y_ref, t_ref, dy_ref, l_ref):
        e = y_ref[...] - t_ref[...]
        dy_ref[...] = e * (1.0 / D)
        part = 0.5 * jnp.sum(jnp.sum(e * e, axis=-1, keepdims=True) * (1.0 / D), axis=0, keepdims=True)

        @pl.when(pl.program_id(0) == 0)
        def _():
            l_ref[...] = part

        @pl.when(pl.program_id(0) > 0)
        def _():
            l_ref[...] += part

    row = pl.BlockSpec((tr, D), lambda i: (i, 0))
    return _call(
        body, name=name, grid=(S // tr,), in_specs=[row, row],
        out_specs=[row, pl.BlockSpec((1, 1), lambda i: (0, 0))],
        out_shape=[jax.ShapeDtypeStruct((S, D), F32), jax.ShapeDtypeStruct((1, 1), F32)],
        compiler_params=_cparams(("arbitrary",)),
    )(y, t)


def _combine(terms, scales, *, name, out_dtype=F32):
    S, D = terms[0].shape
    tr = _pick(S, 256)
    n = len(terms)

    def body(*refs):
        acc = scales[0] * refs[0][...].astype(F32)
        for i in range(1, n):
            acc = acc + scales[i] * refs[i][...].astype(F32)
        refs[n][...] = acc.astype(out_dtype)

    row = pl.BlockSpec((tr, D), lambda i: (i, 0))
    return _call(
        body, name=name, grid=(S // tr,), in_specs=[row] * n, out_specs=row,
        out_shape=jax.ShapeDtypeStruct((S, D), out_dtype),
        compiler_params=_cparams(("parallel",)),
    )(*terms)


def _split3(x):
    h = x.astype(BF16)
    r = x - h.astype(F32)
    m = r.astype(BF16)
    l = (r - m.astype(F32)).astype(BF16)
    return h, m, l


def _tri_matmul(tri_bf, x):
    h, m, l = _split3(x)
    dn = (((1,), (0,)), ((), ()))
    return (lax.dot_general(tri_bf, l, dn, preferred_element_type=F32)
            + lax.dot_general(tri_bf, m, dn, preferred_element_type=F32)
            + lax.dot_general(tri_bf, h, dn, preferred_element_type=F32))


def _gate_fwd(fl, bf, *, name):
    S = fl.shape[0]
    tc = _pick(S, 256)
    nchunk = S // tc

    def body(fl_ref, bf_ref, c_ref, sg_ref):
        r = lax.broadcasted_iota(jnp.int32, (tc, tc), 0)
        cidx = lax.broadcasted_iota(jnp.int32, (tc, tc), 1)
        tri = (r >= cidx).astype(BF16)
        carry = jnp.zeros((1, LANE), F32)
        for ch in range(nchunk):
            x = fl_ref[pl.ds(ch * tc, tc), :] + bf_ref[...]
            lf = jnp.minimum(x, 0.0) - jnp.log(1.0 + jnp.exp(-jnp.abs(x)))
            sg_ref[pl.ds(ch * tc, tc), :] = jax.nn.sigmoid(-x)
            c_ref[pl.ds(ch * tc, tc), :] = _tri_matmul(tri, lf) + carry
            carry = carry + jnp.sum(lf, axis=0, keepdims=True)

    full = pl.BlockSpec((S, LANE), lambda: (0, 0))
    return _call(
        body, name=name, in_specs=[full, pl.BlockSpec((1, LANE), lambda: (0, 0))], out_specs=[full, full],
        out_shape=[jax.ShapeDtypeStruct((S, LANE), F32)] * 2,
        compiler_params=pltpu.CompilerParams(vmem_limit_bytes=VMEM_LIMIT),
    )(fl, bf)


def _gate_bwd(dc, sg, *, name):
    S = dc.shape[0]
    tc = _pick(S, 256)
    nchunk = S // tc

    def body(dc_ref, sg_ref, dfl_ref, db_ref):
        r = lax.broadcasted_iota(jnp.int32, (tc, tc), 0)
        cidx = lax.broadcasted_iota(jnp.int32, (tc, tc), 1)
        tri = (r <= cidx).astype(BF16)
        carry = jnp.zeros((1, LANE), F32)
        dbacc = jnp.zeros((1, LANE), F32)
        for ch in reversed(range(nchunk)):
            d = dc_ref[pl.ds(ch * tc, tc), :]
            dfl = (_tri_matmul(tri, d) + carry) * sg_ref[pl.ds(ch * tc, tc), :]
            dfl_ref[pl.ds(ch * tc, tc), :] = dfl
            dbacc = dbacc + jnp.sum(dfl, axis=0, keepdims=True)
            carry = carry + jnp.sum(d, axis=0, keepdims=True)
        db_ref[...] = dbacc

    full = pl.BlockSpec((S, LANE), lambda: (0, 0))
    return _call(
        body, name=name, in_specs=[full, full], out_specs=[full, pl.BlockSpec((1, LANE), lambda: (0, 0))],
        out_shape=[jax.ShapeDtypeStruct((S, LANE), F32), jax.ShapeDtypeStruct((1, LANE), F32)],
        compiler_params=pltpu.CompilerParams(vmem_limit_bytes=VMEM_LIMIT),
    )(dc, sg)


def _fox_scores(q_ref, k_ref, cc_ref, cr_ref, qi, tq, S):
    scale = 1.0 / math.sqrt(FOX_HEAD_DIM)
    s = lax.dot_general(q_ref[...].astype(BF16), k_ref[...].astype(BF16), (((1,), (1,)), ((), ())),
                        preferred_element_type=F32) * scale
    s = s + cc_ref[...] - cr_ref[...]
    row = lax.broadcasted_iota(jnp.int32, (tq, S), 0) + qi * tq
    col = lax.broadcasted_iota(jnp.int32, (tq, S), 1)
    return s, row >= col


def _fox_fwd(P, ccol, crow, *, name):
    S = P.shape[0]
    tq = _pick(S, 256)
    H = FOX_HEADS

    def body(q_ref, k_ref, v_ref, cc_ref, cr_ref, o_ref, l_ref):
        s, causal = _fox_scores(q_ref, k_ref, cc_ref, cr_ref, pl.program_id(1), tq, S)
        s = jnp.where(causal, s, -1e30)
        m = jnp.max(s, axis=-1, keepdims=True)
        e = jnp.exp(s - m)
        den = jnp.sum(e, axis=-1, keepdims=True)
        p = e / den
        o_ref[...] = jnp.dot(p.astype(BF16), v_ref[...].astype(BF16), preferred_element_type=F32)
        l_ref[...] = m + jnp.log(den)

    return _call(
        body, name=name, grid=(H, S // tq),
        in_specs=[pl.BlockSpec((tq, 128), lambda h, i: (i, h)),
                  pl.BlockSpec((S, 128), lambda h, i: (0, H + h)),
                  pl.BlockSpec((S, 128), lambda h, i: (0, 2 * H + h)),
                  pl.BlockSpec((None, tq, 1), lambda h, i: (h, i, 0)),
                  pl.BlockSpec((None, 1, S), lambda h, i: (h, 0, 0))],
        out_specs=[pl.BlockSpec((tq, 128), lambda h, i: (i, h)),
                   pl.BlockSpec((None, tq, 1), lambda h, i: (h, i, 0))],
        out_shape=[jax.ShapeDtypeStruct((S, FOX_WIDTH), F32), jax.ShapeDtypeStruct((H, S, 1), F32)],
        compiler_params=_cparams(("parallel", "parallel")),
    )(P, P, P, ccol, crow)


def _fox_bwd(P, ccol, crow, o, lse, dcat, *, name):
    S = P.shape[0]
    tq = _pick(S, 256)
    H = FOX_HEADS
    nq = S // tq
    scale = 1.0 / math.sqrt(FOX_HEAD_DIM)

    def body(q_ref, k_ref, v_ref, cc_ref, cr_ref, o_ref, l_ref, do_ref,
             dq_ref, dk_ref, dv_ref, dcc_ref, dcr_ref, dk_acc, dv_acc):
        qi = pl.program_id(1)
        s, causal = _fox_scores(q_ref, k_ref, cc_ref, cr_ref, qi, tq, S)
        p = jnp.where(causal, jnp.exp(s - l_ref[...]), 0.0)
        do = do_ref[...]
        do_bf = do.astype(BF16)
        dp = lax.dot_general(do_bf, v_ref[...].astype(BF16), (((1,), (1,)), ((), ())), preferred_element_type=F32)
        delta = jnp.sum(do * o_ref[...], axis=-1, keepdims=True)
        ds = p * (dp - delta)
        ds_bf = ds.astype(BF16)
        dq_ref[...] = (jnp.dot(ds_bf, k_ref[...].astype(BF16), preferred_element_type=F32) * scale).astype(BF16)
        dkp = lax.dot_general(ds_bf, q_ref[...].astype(BF16), (((0,), (0,)), ((), ())),
                              preferred_element_type=F32) * scale
        dvp = lax.dot_general(p.astype(BF16), do_bf, (((0,), (0,)), ((), ())), preferred_element_type=F32)
        dcc_ref[...] = jnp.sum(ds, axis=-1, keepdims=True)
        dcr = jnp.sum(ds, axis=0, keepdims=True)

        @pl.when(qi == 0)
        def _():
            dk_acc[...] = dkp
            dv_acc[...] = dvp
            dcr_ref[...] = dcr

        @pl.when(qi > 0)
        def _():
            dk_acc[...] += dkp
            dv_acc[...] += dvp
            dcr_ref[...] += dcr

        @pl.when(qi == nq - 1)
        def _():
            dk_ref[...] = dk_acc[...].astype(BF16)
            dv_ref[...] = dv_acc[...].astype(BF16)

    qblk = pl.BlockSpec((tq, 128), lambda h, i: (i, h))
    kvo = pl.BlockSpec((S, 128), lambda h, i: (0, h))
    col = pl.BlockSpec((None, tq, 1), lambda h, i: (h, i, 0))
    rowv = pl.BlockSpec((None, 1, S), lambda h, i: (h, 0, 0))
    return _call(
        body, name=name, grid=(H, nq),
        in_specs=[qblk,
                  pl.BlockSpec((S, 128), lambda h, i: (0, H + h)),
                  pl.BlockSpec((S, 128), lambda h, i: (0, 2 * H + h)),
                  col, rowv, qblk, col, qblk],
        out_specs=[qblk, kvo, kvo, col, rowv],
        out_shape=[jax.ShapeDtypeStruct((S, FOX_WIDTH), BF16)] * 3
        + [jax.ShapeDtypeStruct((H, S, 1), F32), jax.ShapeDtypeStruct((H, 1, S), F32)],
        scratch_shapes=[pltpu.VMEM((S, 128), F32), pltpu.VMEM((S, 128), F32)],
        compiler_params=_cparams(("parallel", "arbitrary")),
    )(P, P, P, ccol, crow, o, lse, dcat)


def _s5_disc_fwd(lr, li, ls, *, name, after=()):
    G, Pn = lr.shape

    def body(lr_ref, li_ref, ls_ref, ar_ref, ai_ref, gr_ref, gi_ref):
        lr_, li_ = lr_ref[...], li_ref[...]
        dt = jnp.exp(ls_ref[...])
        mag = jnp.exp(lr_ * dt)
        th = li_ * dt
        ar = mag * jnp.cos(th)
        ai = mag * jnp.sin(th)
        den = lr_ * lr_ + li_ * li_
        xr = ar - 1.0
        ar_ref[...] = ar
        ai_ref[...] = ai
        gr_ref[...] = (xr * lr_ + ai * li_) / den
        gi_ref[...] = (ai * lr_ - xr * li_) / den

    sq = pl.BlockSpec((G, Pn), lambda: (0, 0))
    return _call(
        body, after=after, name=name, in_specs=[sq, sq, pl.BlockSpec((G, 1), lambda: (0, 0))], out_specs=[sq] * 4,
        out_shape=[jax.ShapeDtypeStruct((G, Pn), F32)] * 4,
    )(lr, li, ls)


def _s5_disc_bwd(lr, li, ls, dar, dai, dgr, dgi, *, name):
    G, Pn = lr.shape

    def body(lr_ref, li_ref, ls_ref, dar_ref, dai_ref, dgr_ref, dgi_ref, dlr_ref, dli_ref, dls_ref):
        lr_, li_ = lr_ref[...], li_ref[...]
        dt = jnp.exp(ls_ref[...])
        mag = jnp.exp(lr_ * dt)
        th = li_ * dt
        ar = mag * jnp.cos(th)
        ai = mag * jnp.sin(th)
        den = lr_ * lr_ + li_ * li_
        xr = ar - 1.0
        xi = ai
        g_re = (xr * lr_ + xi * li_) / den
        g_im = (xi * lr_ - xr * li_) / den
        dgr_, dgi_ = dgr_ref[...], dgi_ref[...]
        dxr = (dgr_ * lr_ - dgi_ * li_) / den
        dxi = (dgr_ * li_ + dgi_ * lr_) / den
        dden = -(dgr_ * g_re + dgi_ * g_im) / den
        dlr = (dgr_ * xr + dgi_ * xi) / den + 2.0 * dden * lr_
        dli = (dgr_ * xi - dgi_ * xr) / den + 2.0 * dden * li_
        da_r = dar_ref[...] + dxr
        da_i = dai_ref[...] + dxi
        dmag_mag = da_r * ar + da_i * ai
        dth = da_i * ar - da_r * ai
        dlr_ref[...] = dlr + dmag_mag * dt
        dli_ref[...] = dli + dth * dt
        ddt = jnp.sum(dmag_mag * lr_ + dth * li_, axis=-1, keepdims=True)
        dls_ref[...] = ddt * dt

    sq = pl.BlockSpec((G, Pn), lambda: (0, 0))
    c1 = pl.BlockSpec((G, 1), lambda: (0, 0))
    return _call(
        body, name=name, in_specs=[sq, sq, c1, sq, sq, sq, sq], out_specs=[sq, sq, c1],
        out_shape=[jax.ShapeDtypeStruct((G, Pn), F32)] * 2 + [jax.ShapeDtypeStruct((G, 1), F32)],
    )(lr, li, ls, dar, dai, dgr, dgi)


def _s5_bb_fwd(gr, gi, br, bi, *, name):
    R, C = br.shape

    def body(gr_ref, gi_ref, br_ref, bi_ref, or_ref, oi_ref):
        g_r, g_i, b_r, b_i = gr_ref[...], gi_ref[...], br_ref[...], bi_ref[...]
        or_ref[...] = g_r * b_r - g_i * b_i
        oi_ref[...] = g_r * b_i + g_i * b_r

    w = pl.BlockSpec((R, C), lambda: (0, 0))
    c1 = pl.BlockSpec((R, 1), lambda: (0, 0))
    return _call(body, name=name, in_specs=[c1, c1, w, w], out_specs=[w, w],
                 out_shape=[jax.ShapeDtypeStruct((R, C), F32)] * 2)(gr, gi, br, bi)


def _s5_bb_bwd(gr, gi, br, bi, dbbr, dbbi, *, name):
    R, C = br.shape

    def body(gr_ref, gi_ref, br_ref, bi_ref, dr_ref, di_ref, dbr_ref, dbi_ref, dgr_ref, dgi_ref):
        g_r, g_i, b_r, b_i = gr_ref[...], gi_ref[...], br_ref[...], bi_ref[...]
        d_r, d_i = dr_ref[...], di_ref[...]
        dbr_ref[...] = g_r * d_r + g_i * d_i
        dbi_ref[...] = g_r * d_i - g_i * d_r
        dgr_ref[...] = jnp.sum(d_r * b_r + d_i * b_i, axis=-1, keepdims=True)
        dgi_ref[...] = jnp.sum(d_i * b_r - d_r * b_i, axis=-1, keepdims=True)

    w = pl.BlockSpec((R, C), lambda: (0, 0))
    c1 = pl.BlockSpec((R, 1), lambda: (0, 0))
    return _call(body, name=name, in_specs=[c1, c1, w, w, w, w], out_specs=[w, w, c1, c1],
                 out_shape=[jax.ShapeDtypeStruct((R, C), F32)] * 2 + [jax.ShapeDtypeStruct((R, 1), F32)] * 2,
                 )(gr, gi, br, bi, dbbr, dbbi)


_DIAG_TILE = 8


def _diag_mask(gr, gc):
    rows, cols = _DIAG_TILE * gr, _DIAG_TILE * gc
    r = lax.broadcasted_iota(jnp.int32, (rows, cols), 0) >> (gr.bit_length() - 1)
    c = lax.broadcasted_iota(jnp.int32, (rows, cols), 1) >> (gc.bit_length() - 1)
    return r == c


def _diag_expand(t2, gr, gc, *, name, after=()):
    _, R, _ = t2.shape
    G = R // gr
    nt = G // _DIAG_TILE
    rows, cols = _DIAG_TILE * gr, _DIAG_TILE * gc

    def body(t_ref, o_ref):
        src = lax.broadcasted_iota(jnp.int32, (gc, cols), 0)
        dst = lax.broadcasted_iota(jnp.int32, (gc, cols), 1) & (gc - 1)
        spread = (src == dst).astype(BF16)
        y = jnp.dot(t_ref[...].astype(BF16), spread, preferred_element_type=F32)
        o_ref[...] = jnp.where(_diag_mask(gr, gc), y, 0.0).astype(BF16)

    return _call(
        body, after=after, name=name, grid=(2, nt),
        in_specs=[pl.BlockSpec((None, rows, gc), lambda p, i: (p, i, 0))],
        out_specs=pl.BlockSpec((None, rows, cols), lambda p, i: (p, i, i)),
        out_shape=jax.ShapeDtypeStruct((2, R, G * gc), BF16),
        compiler_params=_cparams(("parallel",) * 2),
    )(t2)


def _diag_extract(xd, gr, gc, *, name):
    _, R, _ = xd.shape
    nt = R // gr // _DIAG_TILE
    rows, cols = _DIAG_TILE * gr, _DIAG_TILE * gc

    def body(x_ref, o_ref):
        src = lax.broadcasted_iota(jnp.int32, (cols, gc), 0) & (gc - 1)
        dst = lax.broadcasted_iota(jnp.int32, (cols, gc), 1)
        fold = (src == dst).astype(BF16)
        parts = _split3(jnp.where(_diag_mask(gr, gc), x_ref[...], 0.0))
        acc = jnp.dot(parts[2], fold, preferred_element_type=F32)
        acc = acc + jnp.dot(parts[1], fold, preferred_element_type=F32)
        o_ref[...] = acc + jnp.dot(parts[0], fold, preferred_element_type=F32)

    return _call(
        body, name=name, grid=(2, nt),
        in_specs=[pl.BlockSpec((None, rows, cols), lambda p, i: (p, i, i))],
        out_specs=pl.BlockSpec((None, rows, gc), lambda p, i: (p, i, 0)),
        out_shape=jax.ShapeDtypeStruct((2, R, gc), F32),
        compiler_params=_cparams(("parallel",) * 2),
    )(xd)


SCAN_BLOCK = 8


def _cpowers(ar, ai, sign):
    ai = sign * ai
    out = [(ar, ai)]
    for _ in range(SCAN_BLOCK - 1):
        pr, pi = out[-1]
        out.append((pr * ar - pi * ai, pr * ai + pi * ar))
    return out


def _row_table(pw, row, index_of_row):
    tr_ = jnp.broadcast_to(pw[index_of_row(0)][0], row.shape)
    ti_ = jnp.broadcast_to(pw[index_of_row(0)][1], row.shape)
    for r in range(1, SCAN_BLOCK):
        pr, pi = pw[index_of_row(r)]
        tr_ = jnp.where(row == r, pr, tr_)
        ti_ = jnp.where(row == r, pi, ti_)
    return tr_, ti_


def _s5_scan_fwd(bu, a, *, name):
    _, S, N = bu.shape
    tc = 512
    nt = N // tc

    def body(a_ref, b_ref, h_ref):
        pw = _cpowers(a_ref[0], a_ref[1], 1.0)
        row = lax.broadcasted_iota(jnp.int32, (SCAN_BLOCK, tc), 0)
        lead_r, lead_i = _row_table(pw, row, lambda r: r)

        def step(k, carry):
            cr, ci = carry
            rows = pl.ds(pl.multiple_of(k * SCAN_BLOCK, SCAN_BLOCK), SCAN_BLOCK)
            xr, xi = b_ref[0, rows, :], b_ref[1, rows, :]
            for sh in (1, 2, 4):
                keep = row >= sh
                sr = jnp.where(keep, pltpu.roll(xr, sh, 0), 0.0)
                si = jnp.where(keep, pltpu.roll(xi, sh, 0), 0.0)
                kr, ki = pw[sh - 1]
                xr, xi = xr + kr * sr - ki * si, xi + kr * si + ki * sr
            h_ref[0, rows, :] = xr + lead_r * cr - lead_i * ci
            h_ref[1, rows, :] = xi + lead_r * ci + lead_i * cr
            last = row == SCAN_BLOCK - 1
            tr_ = jnp.sum(jnp.where(last, xr, 0.0), axis=0, keepdims=True)
            ti_ = jnp.sum(jnp.where(last, xi, 0.0), axis=0, keepdims=True)
            a8r, a8i = pw[SCAN_BLOCK - 1]
            return a8r * cr - a8i * ci + tr_, a8r * ci + a8i * cr + ti_

        z = jnp.zeros((1, tc), F32)
        lax.fori_loop(0, S // SCAN_BLOCK, step, (z, z), unroll=2)

    vec = pl.BlockSpec((2, 1, tc), lambda j: (0, 0, j))
    mat = pl.BlockSpec((2, S, tc), lambda j: (0, 0, j))
    return _call(
        body, name=name, grid=(nt,), in_specs=[vec, mat], out_specs=mat,
        out_shape=jax.ShapeDtypeStruct((2, S, N), F32),
        compiler_params=_cparams(("parallel",)),
    )(a, bu)


def _s5_scan_bwd(g, h, a, *, name):
    _, S, N = g.shape
    tc = 256
    nt = N // tc

    def body(a_ref, g_ref, h_ref, l_ref, da_ref):
        pw = _cpowers(a_ref[0], a_ref[1], -1.0)
        row = lax.broadcasted_iota(jnp.int32, (SCAN_BLOCK, tc), 0)
        tail_r, tail_i = _row_table(pw, row, lambda r: SCAN_BLOCK - 1 - r)
        nb = S // SCAN_BLOCK

        def step(i, carry):
            k = nb - 1 - i
            cr, ci, dar, dai = carry
            rows = pl.ds(pl.multiple_of(k * SCAN_BLOCK, SCAN_BLOCK), SCAN_BLOCK)
            xr, xi = g_ref[0, rows, :], g_ref[1, rows, :]
            for sh in (1, 2, 4):
                keep = row < SCAN_BLOCK - sh
                sr = jnp.where(keep, pltpu.roll(xr, SCAN_BLOCK - sh, 0), 0.0)
                si = jnp.where(keep, pltpu.roll(xi, SCAN_BLOCK - sh, 0), 0.0)
                kr, ki = pw[sh - 1]
                xr, xi = xr + kr * sr - ki * si, xi + kr * si + ki * sr
            lr = xr + tail_r * cr - tail_i * ci
            li = xi + tail_r * ci + tail_i * cr
            l_ref[0, rows, :] = lr
            l_ref[1, rows, :] = li
            prev = pl.ds(pl.multiple_of(jnp.maximum(k - 1, 0) * SCAN_BLOCK, SCAN_BLOCK), SCAN_BLOCK)
            has_prev = jnp.where(k > 0, 1.0, 0.0).astype(F32)
            first = row == 0
            hpr = jnp.where(first, pltpu.roll(h_ref[0, prev, :], 1, 0) * has_prev, pltpu.roll(h_ref[0, rows, :], 1, 0))
            hpi = jnp.where(first, pltpu.roll(h_ref[1, prev, :], 1, 0) * has_prev, pltpu.roll(h_ref[1, rows, :], 1, 0))
            tr_ = jnp.sum(jnp.where(first, xr, 0.0), axis=0, keepdims=True)
            ti_ = jnp.sum(jnp.where(first, xi, 0.0), axis=0, keepdims=True)
            a8r, a8i = pw[SCAN_BLOCK - 1]
            return (a8r * cr - a8i * ci + tr_, a8r * ci + a8i * cr + ti_,
                    dar + lr * hpr + li * hpi, dai + li * hpr - lr * hpi)

        z = jnp.zeros((1, tc), F32)
        z8 = jnp.zeros((SCAN_BLOCK, tc), F32)
        _, _, dar, dai = lax.fori_loop(0, nb, step, (z, z, z8, z8), unroll=2)
        da_ref[0] = jnp.sum(dar, axis=0, keepdims=True)
        da_ref[1] = jnp.sum(dai, axis=0, keepdims=True)

    vec = pl.BlockSpec((2, 1, tc), lambda j: (0, 0, j))
    mat = pl.BlockSpec((2, S, tc), lambda j: (0, 0, j))
    return _call(
        body, name=name, grid=(nt,), in_specs=[vec, mat, mat], out_specs=[mat, vec],
        out_shape=[jax.ShapeDtypeStruct((2, S, N), F32), jax.ShapeDtypeStruct((2, 1, N), F32)],
        compiler_params=_cparams(("parallel",)),
    )(a, g, h)


_GELU_C = math.sqrt(2.0 / math.pi)


def _s5_out_fwd(yc, P, dskip, *, name):
    S, W = yc.shape
    tr = _pick(S, 256)
    ub = 3 * FOX_WIDTH // W

    def body(yc_ref, u_ref, d_ref, y_ref, yg_ref):
        y = yc_ref[...] + d_ref[...] * u_ref[...]
        y_ref[...] = y
        t = jnp.tanh(_GELU_C * (y + 0.044715 * y * y * y))
        yg_ref[...] = (0.5 * y * (1.0 + t)).astype(BF16)

    row = pl.BlockSpec((tr, W), lambda i: (i, 0))
    return _call(
        body, name=name, grid=(S // tr,),
        in_specs=[row, pl.BlockSpec((tr, W), lambda i: (i, ub)), pl.BlockSpec((1, W), lambda i: (0, 0))],
        out_specs=[row, row],
        out_shape=[jax.ShapeDtypeStruct((S, W), F32), jax.ShapeDtypeStruct((S, W), BF16)],
        compiler_params=_cparams(("parallel",)),
    )(yc, P, dskip)


def _s5_out_bwd(dyg, y, P, dskip, *, name):
    S, W = y.shape
    tr = _pick(S, 256)
    ub = 3 * FOX_WIDTH // W

    def body(dyg_ref, y_ref, u_ref, d_ref, dy_ref, du_ref, dd_ref):
        y_ = y_ref[...]
        inner = _GELU_C * (y_ + 0.044715 * y_ * y_ * y_)
        t = jnp.tanh(inner)
        dgelu = 0.5 * (1.0 + t) + 0.5 * y_ * (1.0 - t * t) * _GELU_C * (1.0 + 3.0 * 0.044715 * y_ * y_)
        dy = dyg_ref[...] * dgelu
        dy_ref[...] = dy.astype(BF16)
        du_ref[...] = d_ref[...] * dy
        part = jnp.sum(dy * u_ref[...], axis=0, keepdims=True)

        @pl.when(pl.program_id(0) == 0)
        def _():
            dd_ref[...] = part

        @pl.when(pl.program_id(0) > 0)
        def _():
            dd_ref[...] += part

    row = pl.BlockSpec((tr, W), lambda i: (i, 0))
    vec = pl.BlockSpec((1, W), lambda i: (0, 0))
    return _call(
        body, name=name, grid=(S // tr,),
        in_specs=[row, row, pl.BlockSpec((tr, W), lambda i: (i, ub)), vec],
        out_specs=[row, row, vec],
        out_shape=[jax.ShapeDtypeStruct((S, W), BF16), jax.ShapeDtypeStruct((S, W), F32),
                   jax.ShapeDtypeStruct((1, W), F32)],
        compiler_params=_cparams(("arbitrary",)),
    )(dyg, y, P, dskip)


def _glu_fwd(z, *, name):
    S, W2 = z.shape
    W = W2 // 2
    tr = _pick(S, 256)

    def body(z1_ref, z2_ref, o_ref):
        o_ref[...] = (z1_ref[...] * jax.nn.sigmoid(z2_ref[...])).astype(BF16)

    return _call(
        body, name=name, grid=(S // tr,),
        in_specs=[pl.BlockSpec((tr, W), lambda i: (i, 0)), pl.BlockSpec((tr, W), lambda i: (i, 1))],
        out_specs=pl.BlockSpec((tr, W), lambda i: (i, 0)),
        out_shape=jax.ShapeDtypeStruct((S, W), BF16),
        compiler_params=_cparams(("parallel",)),
    )(z, z)


def _glu_bwd(z, dcat, *, name):
    S, W2 = z.shape
    W = W2 // 2
    tr = _pick(S, 256)

    def body(z1_ref, z2_ref, d_ref, dz1_ref, dz2_ref):
        sg = jax.nn.sigmoid(z2_ref[...])
        d = d_ref[...]
        dz1_ref[...] = (d * sg).astype(BF16)
        dz2_ref[...] = (d * z1_ref[...] * sg * (1.0 - sg)).astype(BF16)

    lo = pl.BlockSpec((tr, W), lambda i: (i, 0))
    hi = pl.BlockSpec((tr, W), lambda i: (i, 1))
    dz1, dz2 = _call(
        body, name=name, grid=(S // tr,), in_specs=[lo, hi, hi], out_specs=[lo, lo],
        out_shape=[jax.ShapeDtypeStruct((S, W), BF16)] * 2,
        compiler_params=_cparams(("parallel",)),
    )(z, z, dcat)
    return jnp.concatenate([dz1, dz2], axis=1)


ACT_ROWS = 16
ACT_COLS = 256


def _shift_down(cur, prev, k, row):
    return jnp.where(row >= k, pltpu.roll(cur, k, 0), pltpu.roll(prev, k, 0))


def _shift_up(cur, nxt, k, row):
    n = cur.shape[0]
    return jnp.where(row < n - k, pltpu.roll(cur, n - k, 0), pltpu.roll(nxt, n - k, 0))


def _act_fwd(h, cw, cb, *, name):
    _, S, FP = h.shape
    tr = _pick(S, 256)
    hb = tr // ACT_ROWS
    nq = tr // ACT_ROWS

    def body(g_ref, gh_ref, v_ref, vh_ref, wg_ref, wv_ref, bg_ref, bv_ref, a_ref):
        first = pl.program_id(1) == 0
        for c0 in range(0, FP, ACT_COLS):
            cw_ = min(ACT_COLS, FP - c0)
            cols = pl.ds(c0, cw_)
            rw = lax.broadcasted_iota(jnp.int32, (ACT_ROWS, cw_), 0)
            wg = [wg_ref[pl.ds(k, 1), cols] for k in range(3)]
            wv = [wv_ref[pl.ds(k, 1), cols] for k in range(3)]
            bg, bv = bg_ref[:, cols], bv_ref[:, cols]
            halo_g = jnp.where(first, 0.0, gh_ref[:, cols])
            halo_v = jnp.where(first, 0.0, vh_ref[:, cols])

            def chunk(q, _):
                rows = pl.ds(pl.multiple_of(q * ACT_ROWS, ACT_ROWS), ACT_ROWS)
                before = pl.ds(pl.multiple_of(jnp.maximum(q - 1, 0) * ACT_ROWS, ACT_ROWS), ACT_ROWS)
                g, v = g_ref[rows, cols], v_ref[rows, cols]
                gp = jnp.where(q > 0, g_ref[before, cols], halo_g)
                vp = jnp.where(q > 0, v_ref[before, cols], halo_v)
                cg = bg + wg[2] * g + wg[1] * _shift_down(g, gp, 1, rw) + wg[0] * _shift_down(g, gp, 2, rw)
                cv = bv + wv[2] * v + wv[1] * _shift_down(v, vp, 1, rw) + wv[0] * _shift_down(v, vp, 2, rw)
                a_ref[rows, cols] = (cg * jax.nn.sigmoid(cg) * cv).astype(BF16)
                return 0

            lax.fori_loop(0, nq, chunk, 0, unroll=2)

    def main(off):
        return pl.BlockSpec((None, tr, FP), lambda j, i: (j + off, i, 0))

    def halo(off):
        return pl.BlockSpec((None, ACT_ROWS, FP), lambda j, i: (j + off, jnp.maximum(i * hb - 1, 0), 0))

    def wspec(off):
        return pl.BlockSpec((None, 3, FP), lambda j, i: (j + off, 0, 0))

    def bspec(off):
        return pl.BlockSpec((None, 1, FP), lambda j, i: (j + off, 0, 0))

    cb3 = cb.reshape(4, 1, FP)
    return _call(
        body, name=name, grid=(2, S // tr),
        in_specs=[main(0), halo(0), main(2), halo(2), wspec(0), wspec(2), bspec(0), bspec(2)],
        out_specs=pl.BlockSpec((None, tr, FP), lambda j, i: (j, i, 0)),
        out_shape=jax.ShapeDtypeStruct((2, S, FP), BF16),
        compiler_params=_cparams(("parallel", "parallel")),
    )(h, h, h, h, cw, cw, cb3, cb3)


def _act_bwd(h, da, cw, cb, *, name):
    _, S, FP = h.shape
    tr = _pick(S, 256)
    hb = tr // ACT_ROWS
    nq = tr // ACT_ROWS
    nr = S // tr
    half = ACT_ROWS // 2

    def fold(x):
        return x[:half] + x[half:]

    def body(g_ref, gp_ref, v_ref, vp_ref, da_ref, wg_ref, wv_ref, bg_ref, bv_ref,
             dh_ref, dwg_ref, dwv_ref, dbg_ref, dbv_ref, carry_g, carry_v):
        i = pl.program_id(1)
        bottom = i == 0
        top = i == nr - 1
        for c0 in range(0, FP, ACT_COLS):
            cw_ = min(ACT_COLS, FP - c0)
            cols = pl.ds(c0, cw_)
            rw = lax.broadcasted_iota(jnp.int32, (ACT_ROWS, cw_), 0)
            wg = [wg_ref[pl.ds(k, 1), cols] for k in range(3)]
            wv = [wv_ref[pl.ds(k, 1), cols] for k in range(3)]
            bg, bv = bg_ref[:, cols], bv_ref[:, cols]
            halo_g = jnp.where(top, 0.0, gp_ref[:, cols])
            halo_v = jnp.where(top, 0.0, vp_ref[:, cols])
            after_g = jnp.where(bottom, 0.0, carry_g[:, cols])
            after_v = jnp.where(bottom, 0.0, carry_v[:, cols])

            def chunk(s, carry):
                ng, nv, acc = carry[0], carry[1], carry[2:]
                q = nq - 1 - s
                rows = pl.ds(pl.multiple_of(q * ACT_ROWS, ACT_ROWS), ACT_ROWS)
                before = pl.ds(pl.multiple_of(jnp.maximum(q - 1, 0) * ACT_ROWS, ACT_ROWS), ACT_ROWS)
                g, v = g_ref[rows, cols], v_ref[rows, cols]
                gp = jnp.where(q > 0, g_ref[before, cols], halo_g)
                vp = jnp.where(q > 0, v_ref[before, cols], halo_v)
                g1, g2 = _shift_down(g, gp, 1, rw), _shift_down(g, gp, 2, rw)
                v1, v2 = _shift_down(v, vp, 1, rw), _shift_down(v, vp, 2, rw)
                cg = bg + wg[2] * g + wg[1] * g1 + wg[0] * g2
                cv = bv + wv[2] * v + wv[1] * v1 + wv[0] * v2
                sg = jax.nn.sigmoid(cg)
                d = da_ref[rows, cols]
                dcg = d * cv * sg * (1.0 + cg * (1.0 - sg))
                dcv = d * cg * sg
                dh_ref[0, rows, cols] = (wg[2] * dcg + wg[1] * _shift_up(dcg, ng, 1, rw)
                                         + wg[0] * _shift_up(dcg, ng, 2, rw)).astype(BF16)
                dh_ref[1, rows, cols] = (wv[2] * dcv + wv[1] * _shift_up(dcv, nv, 1, rw)
                                         + wv[0] * _shift_up(dcv, nv, 2, rw)).astype(BF16)
                terms = (dcg * g2, dcg * g1, dcg * g, dcg, dcv * v2, dcv * v1, dcv * v, dcv)
                return (dcg, dcv) + tuple(a + fold(t) for a, t in zip(acc, terms))

            zero = jnp.zeros((half, cw_), F32)
            out = lax.fori_loop(0, nq, chunk, (after_g, after_v) + (zero,) * 8, unroll=2)
            carry_g[:, cols] = out[0]
            carry_v[:, cols] = out[1]
            sums = [jnp.sum(a, axis=0, keepdims=True) for a in out[2:]]

            @pl.when(bottom)
            def _():
                for k in range(3):
                    dwg_ref[pl.ds(k, 1), cols] = sums[k]
                    dwv_ref[pl.ds(k, 1), cols] = sums[4 + k]
                dbg_ref[:, cols] = sums[3]
                dbv_ref[:, cols] = sums[7]

            @pl.when(jnp.logical_not(bottom))
            def _():
                for k in range(3):
                    dwg_ref[pl.ds(k, 1), cols] += sums[k]
                    dwv_ref[pl.ds(k, 1), cols] += sums[4 + k]
                dbg_ref[:, cols] += sums[3]
                dbv_ref[:, cols] += sums[7]

    def main(off):
        return pl.BlockSpec((None, tr, FP), lambda j, i: (j + off, nr - 1 - i, 0))

    def prev(off):
        return pl.BlockSpec((None, ACT_ROWS, FP), lambda j, i: (j + off, jnp.maximum((nr - 1 - i) * hb - 1, 0), 0))

    def wspec(off):
        return pl.BlockSpec((None, 3, FP), lambda j, i: (j + off, 0, 0))

    def bspec(off):
        return pl.BlockSpec((None, 1, FP), lambda j, i: (j + off, 0, 0))

    cb3 = cb.reshape(4, 1, FP)
    dh, dwg, dwv, dbg, dbv = _call(
        body, name=name, grid=(2, nr),
        in_specs=[main(0), prev(0), main(2), prev(2), main(0), wspec(0), wspec(2), bspec(0), bspec(2)],
        out_specs=[pl.BlockSpec((None, 2, tr, FP), lambda j, i: (j, 0, nr - 1 - i, 0)),
                   wspec(0), wspec(0), bspec(0), bspec(0)],
        out_shape=[jax.ShapeDtypeStruct((2, 2, S, FP), BF16)]
        + [jax.ShapeDtypeStruct((2, 3, FP), F32)] * 2 + [jax.ShapeDtypeStruct((2, 1, FP), F32)] * 2,
        scratch_shapes=[pltpu.VMEM((ACT_ROWS, FP), F32), pltpu.VMEM((ACT_ROWS, FP), F32)],
        compiler_params=_cparams(("parallel", "arbitrary")),
    )(h, h, h, h, da, cw, cw, cb3, cb3)
    return (dh.reshape(4, S, FP), jnp.concatenate([dwg, dwv], axis=0), jnp.concatenate([dbg, dbv], axis=0))


def _rope_tables(posf, *, name, after=()):
    S = posf.shape[0]
    half = ROPE_DIM // 2
    d = np.arange(LANE) % SWA_HEAD_DIM
    invf = np.where(d < ROPE_DIM, ROPE_THETA ** (-(d % half).astype(np.float64) / half), 0.0).astype(np.float32)
    m_rot = (d < ROPE_DIM).astype(np.float32)
    m_a = (d < half).astype(np.float32)
    m_b = ((d >= half) & (d < ROPE_DIM)).astype(np.float32)
    consts = jnp.asarray(np.stack([invf, m_rot, m_a, m_b] + [np.zeros(LANE, np.float32)] * 4))

    def body(p_ref, k_ref, c_ref, sa_ref, sb_ref):
        k = k_ref[...]
        ang = p_ref[...] * k[0:1]
        co, si = jnp.cos(ang), jnp.sin(ang)
        c_ref[...] = k[1:2] * co + (1.0 - k[1:2])
        sa_ref[...] = -k[2:3] * si
        sb_ref[...] = k[3:4] * si

    full = pl.BlockSpec((S, LANE), lambda: (0, 0))
    return _call(
        body, after=after, name=name,
        in_specs=[pl.BlockSpec((S, 1), lambda: (0, 0)), pl.BlockSpec((8, LANE), lambda: (0, 0))],
        out_specs=[full] * 3, out_shape=[jax.ShapeDtypeStruct((S, LANE), F32)] * 3,
    )(posf, consts)


def _rope_apply(x, tabs, *, col0, width, inverse, name, out_dtype):
    S = x.shape[0]
    tr = _pick(S, 256)
    rep = width // LANE
    cb = col0 // width

    def body(x_ref, c_ref, sa_ref, sb_ref, o_ref):
        xv = x_ref[...].astype(F32)
        c = jnp.tile(c_ref[...], (1, rep))
        sa = jnp.tile(sa_ref[...], (1, rep))
        sb = jnp.tile(sb_ref[...], (1, rep))
        if not inverse:
            out = xv * c + pltpu.roll(xv, width - 8, 1) * sa + pltpu.roll(xv, 8, 1) * sb
        else:
            out = xv * c + pltpu.roll(xv * sa, 8, 1) + pltpu.roll(xv * sb, width - 8, 1)
        o_ref[...] = out.astype(out_dtype)

    tab = pl.BlockSpec((tr, LANE), lambda i: (i, 0))
    return _call(
        body, name=name, grid=(S // tr,),
        in_specs=[pl.BlockSpec((tr, width), lambda i: (i, cb)), tab, tab, tab],
        out_specs=pl.BlockSpec((tr, width), lambda i: (i, 0)),
        out_shape=jax.ShapeDtypeStruct((S, width), out_dtype),
        compiler_params=_cparams(("parallel",)),
    )(x, *tabs)


def _swa_mask(n):
    rows = SWA_GROUPS * SWA_WINDOW
    qi = lax.broadcasted_iota(jnp.int32, (rows, 2 * SWA_WINDOW), 0) & (SWA_WINDOW - 1)
    kj = lax.broadcasted_iota(jnp.int32, (rows, 2 * SWA_WINDOW), 1)
    rel = SWA_WINDOW + qi - kj
    return (rel >= 0) & (rel < SWA_WINDOW) & ((n > 0) | (kj >= SWA_WINDOW))


def _swa_fwd(qT, kT, vT, sink_rows, *, name):
    S = qT.shape[1]
    W, G, Dh = SWA_WINDOW, SWA_GROUPS, SWA_HEAD_DIM
    nb = S // W
    scale = 1.0 / math.sqrt(Dh)

    def body(q_ref, kp_ref, kc_ref, vp_ref, vc_ref, s_ref, o_ref, l_ref):
        n = pl.program_id(1)
        q = q_ref[...].reshape(G * W, Dh)
        kk = jnp.concatenate([kp_ref[...], kc_ref[...]], axis=0)
        vv = jnp.concatenate([vp_ref[...], vc_ref[...]], axis=0)
        s = lax.dot_general(q, kk, (((1,), (1,)), ((), ())), preferred_element_type=F32) * scale
        s = jnp.where(_swa_mask(n), s, -1e30)
        sink = s_ref[...]
        m = jnp.maximum(jnp.max(s, axis=-1, keepdims=True), sink)
        e = jnp.exp(s - m)
        den = jnp.sum(e, axis=-1, keepdims=True) + jnp.exp(sink - m)
        p = e / den
        o_ref[...] = jnp.dot(p.astype(BF16), vv, preferred_element_type=F32).reshape(G, W, Dh)
        l_ref[...] = (m + jnp.log(den)).reshape(G, W, 1)

    qs = pl.BlockSpec((G, W, Dh), lambda g, n: (g, n, 0))
    prev = pl.BlockSpec((None, W, Dh), lambda g, n: (g, jnp.maximum(n - 1, 0), 0))
    cur = pl.BlockSpec((None, W, Dh), lambda g, n: (g, n, 0))
    return _call(
        body, name=name, grid=(SWA_KV_HEADS, nb),
        in_specs=[qs, prev, cur, prev, cur, pl.BlockSpec((None, G * W, 1), lambda g, n: (g, 0, 0))],
        out_specs=[qs, pl.BlockSpec((G, W, 1), lambda g, n: (g, n, 0))],
        out_shape=[jax.ShapeDtypeStruct((SWA_HEADS, S, Dh), F32), jax.ShapeDtypeStruct((SWA_HEADS, S, 1), F32)],
        compiler_params=_cparams(("parallel", "parallel")),
    )(qT, kT, kT, vT, vT, sink_rows)


def _swa_bwd(qT, kT, vT, sink_rows, oT, L, doT, *, name):
    S = qT.shape[1]
    W, G, Dh = SWA_WINDOW, SWA_GROUPS, SWA_HEAD_DIM
    nb = S // W
    scale = 1.0 / math.sqrt(Dh)

    def body(q_ref, kp_ref, kc_ref, vp_ref, vc_ref, s_ref, o_ref, l_ref, do_ref,
             dq_ref, dk_ref, dv_ref, ds_ref):
        n = pl.program_id(1)
        q = q_ref[...].reshape(G * W, Dh)
        kk = jnp.concatenate([kp_ref[...], kc_ref[...]], axis=0)
        vv = jnp.concatenate([vp_ref[...], vc_ref[...]], axis=0)
        s = lax.dot_general(q, kk, (((1,), (1,)), ((), ())), preferred_element_type=F32) * scale
        lrow = l_ref[...].reshape(G * W, 1)
        p = jnp.where(_swa_mask(n), jnp.exp(s - lrow), 0.0)
        do = do_ref[...].reshape(G * W, Dh)
        do_bf = do.astype(BF16)
        dp = lax.dot_general(do_bf, vv, (((1,), (1,)), ((), ())), preferred_element_type=F32)
        delta = jnp.sum(do * o_ref[...].reshape(G * W, Dh), axis=-1, keepdims=True)
        dsc = p * (dp - delta)
        ds_bf = dsc.astype(BF16)
        dq_ref[...] = (jnp.dot(ds_bf, kk, preferred_element_type=F32) * scale).astype(BF16).reshape(G, W, Dh)
        dkk = lax.dot_general(ds_bf, q, (((0,), (0,)), ((), ())), preferred_element_type=F32) * scale
        dvv = lax.dot_general(p.astype(BF16), do_bf, (((0,), (0,)), ((), ())), preferred_element_type=F32)
        dsk = -jnp.exp(s_ref[...] - lrow) * delta
        dsk = jnp.broadcast_to(jnp.sum(dsk.reshape(G, W, 1), axis=1), (G, LANE))

        @pl.when(n == 0)
        def _():
            dk_ref[...] = jnp.zeros_like(dk_ref)
            dv_ref[...] = jnp.zeros_like(dv_ref)
            ds_ref[...] = jnp.zeros_like(ds_ref)

        rows = pl.ds(pl.multiple_of(n * W, W), 2 * W)
        dk_ref[rows, :] += dkk
        dv_ref[rows, :] += dvv
        ds_ref[...] += dsk

    qs = pl.BlockSpec((G, W, Dh), lambda g, n: (g, n, 0))
    prev = pl.BlockSpec((None, W, Dh), lambda g, n: (g, jnp.maximum(n - 1, 0), 0))
    cur = pl.BlockSpec((None, W, Dh), lambda g, n: (g, n, 0))
    lsp = pl.BlockSpec((G, W, 1), lambda g, n: (g, n, 0))
    kvo = pl.BlockSpec((None, S + W, Dh), lambda g, n: (g, 0, 0))
    return _call(
        body, name=name, grid=(SWA_KV_HEADS, nb),
        in_specs=[qs, prev, cur, prev, cur, pl.BlockSpec((None, G * W, 1), lambda g, n: (g, 0, 0)), qs, lsp, qs],
        out_specs=[qs, kvo, kvo, pl.BlockSpec((None, G, LANE), lambda g, n: (g, 0, 0))],
        out_shape=[jax.ShapeDtypeStruct((SWA_HEADS, S, Dh), BF16),
                   jax.ShapeDtypeStruct((SWA_KV_HEADS, S + W, Dh), F32),
                   jax.ShapeDtypeStruct((SWA_KV_HEADS, S + W, Dh), F32),
                   jax.ShapeDtypeStruct((SWA_KV_HEADS, G, LANE), F32)],
        compiler_params=_cparams(("parallel", "arbitrary")),
    )(qT, kT, kT, vT, vT, sink_rows, oT, L, doT)


def _adamw(w, g, m, v, *, name, tr=128, by_cols=False):
    L, R, C = w.shape
    split = isinstance(g, (list, tuple))
    HR, HC = _half_shape(R, C, by_cols) if split else (R, C)
    tr, tc = _tile2d(HR, HC, tr)
    nr, nc = HR // tr, HC // tc
    c1 = 1.0 / (1.0 - ADAM_B1 ** ADAM_STEP)
    c2 = 1.0 / (1.0 - ADAM_B2 ** ADAM_STEP)
    ng = 2 * L if split else 1

    def body(c_ref, *refs):
        w_ref, g_refs, (m_ref, v_ref, go_ref, d_ref, mo_ref, vo_ref) = refs[0], refs[1:1 + ng], refs[1 + ng:]
        if split:
            mine = pl.program_id(1) == c_ref[0]
            g_ = jnp.where(mine, g_refs[0][...], g_refs[1][...])
            for l in range(1, L):
                g_ = jnp.where(pl.program_id(0) == l,
                               jnp.where(mine, g_refs[2 * l][...], g_refs[2 * l + 1][...]), g_)
        else:
            g_ = g_refs[0][...]
        mn = ADAM_B1 * m_ref[...] + (1.0 - ADAM_B1) * g_
        vn = ADAM_B2 * v_ref[...] + (1.0 - ADAM_B2) * (g_ * g_)
        go_ref[...] = g_
        mo_ref[...] = mn
        vo_ref[...] = vn
        d_ref[...] = -ADAM_LR * ((mn * c1) / (jnp.sqrt(vn * c2) + ADAM_EPS) + ADAM_WD * w_ref[...])

    def whole(l, hf, i, j, c):
        return (l, i, hf * nc + j) if by_cols else (l, hf * nr + i, j)

    def half(layer, own):
        def index(l, hf, i, j, c):
            used = (l == layer) & ((hf == c[0]) if own else (hf != c[0]))
            return jnp.where(used, i, 0), jnp.where(used, j, 0)
        return pl.BlockSpec((tr, tc), index)

    row = pl.BlockSpec((None, tr, tc), whole)
    gs = [h for pair in g for h in pair] if split else [g]
    g_specs = [half(l, own) for l in range(L) for own in (True, False)] if split else [row]
    core = lax.axis_index("c").astype(jnp.int32).reshape(1)
    return _call(
        body, name=name,
        grid_spec=pltpu.PrefetchScalarGridSpec(
            num_scalar_prefetch=1, grid=(L, 2 if split else 1, nr, nc),
            in_specs=[row] + g_specs + [row, row], out_specs=[row] * 4),
        out_shape=[jax.ShapeDtypeStruct((L, R, C), F32)] * 4,
        compiler_params=_cparams(("parallel",) * 4),
    )(core, w, *gs, m, v)


def _sum2_halves(g4, s4, by_cols, *, name):
    n, R, C = g4.shape
    HR, HC = _half_shape(R, C, by_cols)
    tr, tc = _tile2d(HR, HC, budget=1024 * 1024)
    nr, nc = HR // tr, HC // tc
    core = lax.axis_index("c").astype(jnp.int32).reshape(1)

    def body(c_ref, g_ref, s_ref, o_ref):
        o_ref[...] = (g_ref[...].astype(F32) + s_ref[...].astype(F32)).astype(BF16)

    def mine(k, i, j, c):
        return (k, i, c[0] * nc + j) if by_cols else (k, c[0] * nr + i, j)

    blk = pl.BlockSpec((None, tr, tc), lambda k, i, j, c: (k, i, j))
    return _call(
        body, name=name,
        grid_spec=pltpu.PrefetchScalarGridSpec(
            num_scalar_prefetch=1, grid=(n, nr, nc),
            in_specs=[pl.BlockSpec((None, tr, tc), mine), blk], out_specs=blk),
        out_shape=jax.ShapeDtypeStruct((n, HR, HC), BF16),
        compiler_params=_cparams(("parallel", "parallel", "parallel")),
    )(core, g4, s4)


def _rowsum(parts, *, name, out_dtype=F32):
    n, R, C = parts.shape
    tr, tc = _tile2d(R, C, budget=512 * 1024)

    def body(p_ref, o_ref):
        acc = p_ref[0].astype(F32)
        for i in range(1, n):
            acc = acc + p_ref[i].astype(F32)
        o_ref[...] = acc.astype(out_dtype)

    return _call(
        body, name=name, grid=(R // tr, C // tc),
        in_specs=[pl.BlockSpec((n, tr, tc), lambda i, j: (0, i, j))],
        out_specs=pl.BlockSpec((tr, tc), lambda i, j: (i, j)),
        out_shape=jax.ShapeDtypeStruct((R, C), out_dtype),
        compiler_params=_cparams(("parallel", "parallel")),
    )(parts)


def _where_am_i():
    x, y, c = lax.axis_index("x"), lax.axis_index("y"), lax.axis_index("c")
    chips = [(1 - x, y), (x, 1 - y), (1 - x, 1 - y)]
    return x, y, c, chips


def _half_idx(rows, cols, by_cols, which):
    if by_cols:
        hc = cols // 2
        return (slice(None), pl.ds(pl.multiple_of(which * hc, LANE), hc))
    hr = rows // 2
    return (pl.ds(pl.multiple_of(which * hr, 16), hr), slice(None))


def _half_shape(rows, cols, by_cols):
    return (rows, cols // 2) if by_cols else (rows // 2, cols)


def _all_gather_shards(shards, by_cols, *, name):
    n = len(shards)

    def body(*refs):
        ins, outs = refs[:n], refs[n:2 * n]
        send, recv = refs[2 * n:]
        x, y, c, chips = _where_am_i()
        me = 2 * x + y
        sibling = (x, y, 1 - c)

        def half(i, which):
            return _half_idx(*shards[i].shape, by_cols[i], which)

        def cp(i, k, src, dst, to):
            return pltpu.make_async_remote_copy(src_ref=src, dst_ref=dst, send_sem=send.at[i, k],
                                                recv_sem=recv.at[i, k], device_id=to, device_id_type=MESH)

        first = []
        for i in range(n):
            for k, (px, py) in enumerate(chips):
                d = cp(i, k, ins[i].at[half(i, c)], outs[i].at[(me,) + half(i, c)], (px, py, c))
                d.start()
                first.append(d)
        passed = []
        for i in range(n):
            for k, (px, py) in enumerate(chips):
                blk = outs[i].at[(2 * px + py,) + half(i, c)]
                cp(i, k, blk, blk, (px, py, c)).wait_recv()
                d = cp(i, 3 + k, blk, blk, sibling)
                d.start()
                passed.append(d)
        for i in range(n):
            for k, (px, py) in enumerate(chips):
                blk = outs[i].at[(2 * px + py,) + half(i, 1 - c)]
                cp(i, 3 + k, blk, blk, sibling).wait_recv()
        for d in first + passed:
            d.wait_send()

    got = _call(
        body, name=name, in_specs=[ANY] * n, out_specs=[ANY] * n,
        out_shape=[jax.ShapeDtypeStruct((N_CHIPS,) + s.shape, s.dtype) for s in shards],
        scratch_shapes=[pltpu.SemaphoreType.DMA((n, 6)), pltpu.SemaphoreType.DMA((n, 6))],
    )(*shards)
    me = 2 * lax.axis_index("x") + lax.axis_index("y")
    return [lax.dynamic_update_slice_in_dim(g, s[None], me, axis=0) for g, s in zip(got, shards)]


HBM_SPEC = pl.BlockSpec(memory_space=pltpu.HBM)
SEM_SPEC = pl.BlockSpec(memory_space=pltpu.SEMAPHORE)
DATAFLOW = pltpu.SideEffectType.DATAFLOW_SIDE_EFFECTING


def _chip_exchange_refs(kind, shards_shape, by_cols, src, land, i, chip_k, c, me):
    if kind == 'gather':
        half = _half_idx(*shards_shape, by_cols, c)
        return src.at[half], land.at[(me,) + half], land.at[(chip_k,) + half]
    return src.at[chip_k], land.at[me], land.at[chip_k]


def _chip_exchange_start(kind, srcs, by_cols, *, name, after=()):
    n = len(srcs)
    land_shapes = [((N_CHIPS,) + s.shape) if kind == 'gather' else s.shape for s in srcs]

    def body(*refs):
        src_refs, land_refs = refs[:n], refs[n:2 * n]
        send, recv = refs[2 * n + len(after)], refs[2 * n + len(after) + 1]
        token = refs[-1]
        x, y, c, chips = _where_am_i()
        me = 2 * x + y
        for i in range(n):
            for k, (px, py) in enumerate(chips):
                s, d, _ = _chip_exchange_refs(kind, srcs[i].shape, by_cols[i], src_refs[i], land_refs[i], i,
                                              2 * px + py, c, me)
                pltpu.make_async_remote_copy(src_ref=s, dst_ref=d, send_sem=send.at[3 * i + k],
                                             recv_sem=recv.at[3 * i + k], device_id=(px, py, c),
                                             device_id_type=MESH).start()
        token[...] = jnp.zeros_like(token)

    lands = [pltpu.with_memory_space_constraint(lax.empty(sh, s.dtype), pltpu.HBM) for sh, s in zip(land_shapes, srcs)]
    outs = _call(
        body, name=name,
        out_shape=(pltpu.SemaphoreType.DMA((3 * n,)), pltpu.SemaphoreType.DMA((3 * n,)),
                   *[pltpu.HBM(s.shape, s.dtype) for s in srcs],
                   *[pltpu.HBM(sh, s.dtype) for sh, s in zip(land_shapes, srcs)],
                   jax.ShapeDtypeStruct((8, LANE), F32)),
        in_specs=[HBM_SPEC] * (2 * n) + [ANY] * len(after),
        out_specs=(SEM_SPEC, SEM_SPEC, *([HBM_SPEC] * (2 * n)), pl.BlockSpec(memory_space=pltpu.VMEM)),
        input_output_aliases={j: 2 + j for j in range(2 * n)},
        compiler_params=pltpu.CompilerParams(has_side_effects=DATAFLOW),
    )(*[pltpu.with_memory_space_constraint(s, pltpu.HBM) for s in srcs], *lands, *after)
    return outs[0], outs[1], list(outs[2:2 + n]), list(outs[2 + n:2 + 2 * n]), outs[-1]


def _chip_exchange_wait(kind, send, recv, srcs, lands, by_cols, after, *, name):
    n = len(srcs)

    def body(*refs):
        src_refs, land_refs = refs[:n], refs[n:2 * n]
        send_r, recv_r = refs[2 * n], refs[2 * n + 1]
        x, y, c, chips = _where_am_i()
        me = 2 * x + y
        for i in range(n):
            for k, (px, py) in enumerate(chips):
                s, _, d = _chip_exchange_refs(kind, srcs[i].shape, by_cols[i], src_refs[i], land_refs[i], i,
                                              2 * px + py, c, me)
                cp = pltpu.make_async_remote_copy(src_ref=s, dst_ref=d, send_sem=send_r.at[3 * i + k],
                                                  recv_sem=recv_r.at[3 * i + k], device_id=(px, py, c),
                                                  device_id_type=MESH)
                cp.wait_send()
                cp.wait_recv()

    outs = _call(
        body, name=name,
        out_shape=(*[pltpu.HBM(s.shape, s.dtype) for s in srcs], *[pltpu.HBM(l.shape, l.dtype) for l in lands]),
        in_specs=[HBM_SPEC] * (2 * n) + [SEM_SPEC, SEM_SPEC] + [ANY] * len(after),
        out_specs=tuple([HBM_SPEC] * (2 * n)),
        input_output_aliases={j: j for j in range(2 * n)},
        compiler_params=pltpu.CompilerParams(has_side_effects=DATAFLOW),
    )(*srcs, *lands, send, recv, *after)
    return list(outs[:n]), list(outs[n:])


def _sibling_halves_start(grads, by_cols, *, name, after=()):
    n = len(grads)
    land_shapes = [(N_CHIPS,) + _half_shape(*g.shape[1:], bc) for g, bc in zip(grads, by_cols)]

    def body(*refs):
        src_refs, land_refs = refs[:n], refs[n:2 * n]
        send, recv = refs[2 * n + len(after)], refs[2 * n + len(after) + 1]
        token = refs[-1]
        x, y, c, _ = _where_am_i()
        for i in range(n):
            src = src_refs[i].at[(slice(None),) + _half_idx(*grads[i].shape[1:], by_cols[i], 1 - c)]
            pltpu.make_async_remote_copy(src_ref=src, dst_ref=land_refs[i], send_sem=send.at[i], recv_sem=recv.at[i],
                                         device_id=(x, y, 1 - c), device_id_type=MESH).start()
        token[...] = jnp.zeros_like(token)

    lands = [pltpu.with_memory_space_constraint(lax.empty(sh, g.dtype), pltpu.HBM) for sh, g in zip(land_shapes, grads)]
    outs = _call(
        body, name=name,
        out_shape=(pltpu.SemaphoreType.DMA((n,)), pltpu.SemaphoreType.DMA((n,)),
                   *[pltpu.HBM(g.shape, g.dtype) for g in grads],
                   *[pltpu.HBM(sh, g.dtype) for sh, g in zip(land_shapes, grads)],
                   jax.ShapeDtypeStruct((8, LANE), F32)),
        in_specs=[HBM_SPEC] * (2 * n) + [ANY] * len(after),
        out_specs=(SEM_SPEC, SEM_SPEC, *([HBM_SPEC] * (2 * n)), pl.BlockSpec(memory_space=pltpu.VMEM)),
        input_output_aliases={j: 2 + j for j in range(2 * n)},
        compiler_params=pltpu.CompilerParams(has_side_effects=DATAFLOW),
    )(*[pltpu.with_memory_space_constraint(g, pltpu.HBM) for g in grads], *lands, *after)
    return outs[0], outs[1], list(outs[2:2 + n]), list(outs[2 + n:2 + 2 * n]), outs[-1]


def _sibling_halves_wait(send, recv, grads, lands, by_cols, after, *, name):
    n = len(grads)

    def body(*refs):
        src_refs, land_refs = refs[:n], refs[n:2 * n]
        send_r, recv_r = refs[2 * n], refs[2 * n + 1]
        x, y, c, _ = _where_am_i()
        for i in range(n):
            src = src_refs[i].at[(slice(None),) + _half_idx(*grads[i].shape[1:], by_cols[i], 1 - c)]
            cp = pltpu.make_async_remote_copy(src_ref=src, dst_ref=land_refs[i], send_sem=send_r.at[i],
                                              recv_sem=recv_r.at[i], device_id=(x, y, 1 - c), device_id_type=MESH)
            cp.wait_send()
            cp.wait_recv()

    outs = _call(
        body, name=name,
        out_shape=(*[pltpu.HBM(g.shape, g.dtype) for g in grads], *[pltpu.HBM(l.shape, l.dtype) for l in lands]),
        in_specs=[HBM_SPEC] * (2 * n) + [SEM_SPEC, SEM_SPEC] + [ANY] * len(after),
        out_specs=tuple([HBM_SPEC] * (2 * n)),
        input_output_aliases={j: j for j in range(2 * n)},
        compiler_params=pltpu.CompilerParams(has_side_effects=DATAFLOW),
    )(*grads, *lands, send, recv, *after)
    return list(outs[:n]), list(outs[n:])


def _sibling_swap_start(arrs, *, name, after=()):
    n = len(arrs)

    def body(*refs):
        src_refs, land_refs = refs[:n], refs[n:2 * n]
        send, recv = refs[2 * n + len(after)], refs[2 * n + len(after) + 1]
        token = refs[-1]
        x, y, c, _ = _where_am_i()
        for i in range(n):
            pltpu.make_async_remote_copy(src_ref=src_refs[i], dst_ref=land_refs[i], send_sem=send.at[i],
                                         recv_sem=recv.at[i], device_id=(x, y, 1 - c), device_id_type=MESH).start()
        token[...] = jnp.zeros_like(token)

    lands = [pltpu.with_memory_space_constraint(lax.empty(a.shape, a.dtype), pltpu.HBM) for a in arrs]
    outs = _call(
        body, name=name,
        out_shape=(pltpu.SemaphoreType.DMA((n,)), pltpu.SemaphoreType.DMA((n,)),
                   *[pltpu.HBM(a.shape, a.dtype) for a in arrs] * 2, jax.ShapeDtypeStruct((8, LANE), F32)),
        in_specs=[HBM_SPEC] * (2 * n) + [ANY] * len(after),
        out_specs=(SEM_SPEC, SEM_SPEC, *([HBM_SPEC] * (2 * n)), pl.BlockSpec(memory_space=pltpu.VMEM)),
        input_output_aliases={j: 2 + j for j in range(2 * n)},
        compiler_params=pltpu.CompilerParams(has_side_effects=DATAFLOW),
    )(*[pltpu.with_memory_space_constraint(a, pltpu.HBM) for a in arrs], *lands, *after)
    return outs[0], outs[1], list(outs[2:2 + n]), list(outs[2 + n:2 + 2 * n]), outs[-1]


def _sibling_swap_wait(send, recv, arrs, lands, after, *, name):
    n = len(arrs)

    def body(*refs):
        src_refs, land_refs = refs[:n], refs[n:2 * n]
        send_r, recv_r = refs[2 * n], refs[2 * n + 1]
        x, y, c, _ = _where_am_i()
        for i in range(n):
            cp = pltpu.make_async_remote_copy(src_ref=src_refs[i], dst_ref=land_refs[i], send_sem=send_r.at[i],
                                              recv_sem=recv_r.at[i], device_id=(x, y, 1 - c), device_id_type=MESH)
            cp.wait_send()
            cp.wait_recv()

    outs = _call(
        body, name=name,
        out_shape=tuple(pltpu.HBM(a.shape, a.dtype) for a in list(arrs) + list(lands)),
        in_specs=[HBM_SPEC] * (2 * n) + [SEM_SPEC, SEM_SPEC] + [ANY] * len(after),
        out_specs=tuple([HBM_SPEC] * (2 * n)),
        input_output_aliases={j: j for j in range(2 * n)},
        compiler_params=pltpu.CompilerParams(has_side_effects=DATAFLOW),
    )(*arrs, *lands, send, recv, *after)
    return list(outs[:n]), list(outs[n:])


def _sibling_pass_gathered(lands, shard_shapes, by_cols, *, name):
    n = len(lands)

    def body(*refs):
        outs = refs[n:2 * n]
        send, recv = refs[2 * n:]
        x, y, c, chips = _where_am_i()
        sibling = (x, y, 1 - c)
        cps = []
        for i in range(n):
            for k, (px, py) in enumerate(chips):
                blk = outs[i].at[(2 * px + py,) + _half_idx(*shard_shapes[i], by_cols[i], c)]
                d = pltpu.make_async_remote_copy(src_ref=blk, dst_ref=blk, send_sem=send.at[i, k],
                                                 recv_sem=recv.at[i, k], device_id=sibling, device_id_type=MESH)
                d.start()
                cps.append(d)
        for i in range(n):
            for k, (px, py) in enumerate(chips):
                blk = outs[i].at[(2 * px + py,) + _half_idx(*shard_shapes[i], by_cols[i], 1 - c)]
                pltpu.make_async_remote_copy(src_ref=blk, dst_ref=blk, send_sem=send.at[i, k], recv_sem=recv.at[i, k],
                                             device_id=sibling, device_id_type=MESH).wait_recv()
        for d in cps:
            d.wait_send()

    return _call(
        body, name=name, in_specs=[ANY] * n, out_specs=[ANY] * n,
        out_shape=[jax.ShapeDtypeStruct(l.shape, l.dtype) for l in lands],
        input_output_aliases={j: j for j in range(n)},
        scratch_shapes=[pltpu.SemaphoreType.DMA((n, 3)), pltpu.SemaphoreType.DMA((n, 3))],
    )(*lands)


def _own_slot(lands, owns):
    me = 2 * lax.axis_index("x") + lax.axis_index("y")
    return [lax.dynamic_update_slice_in_dim(g, s, me, axis=0) for g, s in zip(lands, owns)]


def _sibling_send_halves(grads, by_cols, *, name):
    n = len(grads)

    def body(*refs):
        ins, outs = refs[:n], refs[n:2 * n]
        send, recv = refs[2 * n:]
        x, y, c, _ = _where_am_i()
        sibling = (x, y, 1 - c)
        cps = []
        for i in range(n):
            src = ins[i].at[(slice(None),) + _half_idx(*grads[i].shape[1:], by_cols[i], 1 - c)]
            d = pltpu.make_async_remote_copy(src_ref=src, dst_ref=outs[i], send_sem=send.at[i],
                                             recv_sem=recv.at[i], device_id=sibling, device_id_type=MESH)
            d.start()
            cps.append(d)
        for d in cps:
            d.wait()

    return _call(
        body, name=name, in_specs=[ANY] * n, out_specs=[ANY] * n,
        out_shape=[jax.ShapeDtypeStruct((N_CHIPS,) + _half_shape(*g.shape[1:], bc), g.dtype)
                   for g, bc in zip(grads, by_cols)],
        scratch_shapes=[pltpu.SemaphoreType.DMA((n,)), pltpu.SemaphoreType.DMA((n,))],
    )(*grads)


def _scatter_to_chips(parts, *, name):
    n = len(parts)

    def body(*refs):
        ins, outs = refs[:n], refs[n:2 * n]
        send, recv = refs[2 * n:]
        x, y, c, chips = _where_am_i()
        me = 2 * x + y
        cps = []
        for i in range(n):
            for k, (px, py) in enumerate(chips):
                d = pltpu.make_async_remote_copy(
                    src_ref=ins[i].at[2 * px + py], dst_ref=outs[i].at[me], send_sem=send.at[i, k],
                    recv_sem=recv.at[i, k], device_id=(px, py, c), device_id_type=MESH)
                d.start()
                cps.append((d, i, k, px, py))
        for d, i, k, px, py in cps:
            blk = outs[i].at[2 * px + py]
            pltpu.make_async_remote_copy(src_ref=blk, dst_ref=blk, send_sem=send.at[i, k], recv_sem=recv.at[i, k],
                                         device_id=(px, py, c), device_id_type=MESH).wait_recv()
        for d, *_ in cps:
            d.wait_send()

    got = _call(
        body, name=name, in_specs=[ANY] * n, out_specs=[ANY] * n,
        out_shape=[jax.ShapeDtypeStruct(p.shape, p.dtype) for p in parts],
        scratch_shapes=[pltpu.SemaphoreType.DMA((n, 3)), pltpu.SemaphoreType.DMA((n, 3))],
    )(*parts)
    me = 2 * lax.axis_index("x") + lax.axis_index("y")
    return [lax.dynamic_update_slice_in_dim(g, lax.dynamic_slice_in_dim(p, me, 1, axis=0), me, axis=0)
            for g, p in zip(got, parts)]


def _sibling_join_halves(halves, *, name):
    n = len(halves)

    def body(*refs):
        ins, outs = refs[:n], refs[n:2 * n]
        send, recv = refs[2 * n:]
        x, y, c, _ = _where_am_i()
        sibling = (x, y, 1 - c)
        cps = []
        for i in range(n):
            d = pltpu.make_async_remote_copy(src_ref=ins[i], dst_ref=outs[i], send_sem=send.at[i],
                                             recv_sem=recv.at[i], device_id=sibling, device_id_type=MESH)
            d.start()
            cps.append(d)
        for d in cps:
            d.wait()

    return _call(
        body, name=name, in_specs=[ANY] * n, out_specs=[ANY] * n,
        out_shape=[jax.ShapeDtypeStruct(h.shape, h.dtype) for h in halves],
        scratch_shapes=[pltpu.SemaphoreType.DMA((n,)), pltpu.SemaphoreType.DMA((n,))],
    )(*halves)


def _all_reduce_small(v, *, name, after=()):
    R, C = v.shape
    H = R // 2

    def body(v_ref, o_ref, sib, slots, send, recv):
        x, y, c, chips = _where_am_i()
        me = 2 * x + y
        sibling = (x, y, 1 - c)
        mine = pl.ds(pl.multiple_of(c * H, 8), H)
        other = pl.ds(pl.multiple_of((1 - c) * H, 8), H)

        def copy(k, src, dst, to):
            return pltpu.make_async_remote_copy(src_ref=src, dst_ref=dst, send_sem=send.at[k], recv_sem=recv.at[k],
                                                device_id=to, device_id_type=MESH)

        d = copy(0, v_ref.at[other], sib, sibling)
        d.start()
        d.wait()
        slots[me] = v_ref[mine, :] + sib[...]
        cps = [copy(1 + k, slots.at[me], slots.at[me], (px, py, c)) for k, (px, py) in enumerate(chips)]
        for d in cps:
            d.start()
        for k, (px, py) in enumerate(chips):
            blk = slots.at[2 * px + py]
            copy(1 + k, blk, blk, (px, py, c)).wait_recv()
        for d in cps:
            d.wait_send()
        o_ref[mine, :] = (slots[0] + slots[1]) + (slots[2] + slots[3])
        d = copy(4, o_ref.at[mine], o_ref.at[mine], sibling)
        d.start()
        copy(4, o_ref.at[other], o_ref.at[other], sibling).wait_recv()
        d.wait_send()

    vm = pl.BlockSpec(memory_space=pltpu.VMEM)
    return _call(
        body, after=after, name=name, in_specs=[vm], out_specs=vm,
        out_shape=jax.ShapeDtypeStruct((R, C), F32),
        scratch_shapes=[pltpu.VMEM((H, C), F32), pltpu.VMEM((N_CHIPS, H, C), F32),
                        pltpu.SemaphoreType.DMA((5,)), pltpu.SemaphoreType.DMA((5,))],
        compiler_params=pltpu.CompilerParams(vmem_limit_bytes=VMEM_LIMIT),
    )(v)


def _cols_from_shards(g):
    return jnp.transpose(g, (1, 0, 2)).reshape(g.shape[1], -1)


def _shards_from_cols(w):
    R, C4 = w.shape
    return jnp.transpose(w.reshape(R, N_CHIPS, C4 // N_CHIPS), (1, 0, 2))


def _block_diag(t):
    G, a, b = t.shape
    eye = jnp.eye(G, dtype=t.dtype)
    return (t[:, :, None, :] * eye[:, None, :, None]).reshape(G * a, G * b)


def _diag_blocks(xm, G):
    a, b = xm.shape[0] // G, xm.shape[1] // G
    idx = jnp.arange(G)
    return xm.reshape(G, a, G, b)[idx, :, idx, :]


def _pack(arrs):
    flat = []
    for a in arrs:
        f = a.reshape(-1).astype(F32)
        flat.append(jnp.pad(f, (0, _rup(f.shape[0], LANE) - f.shape[0])))
    v = jnp.concatenate(flat)
    rows = _rup(v.shape[0] // LANE, 16)
    v = jnp.pad(v, (0, rows * LANE - v.shape[0]))
    return v.reshape(rows, LANE)


def _unpack(v, shapes):
    flat = v.reshape(-1)
    out, off = [], 0
    for s in shapes:
        n = int(np.prod(s))
        out.append(flat[off:off + n].reshape(s))
        off += _rup(n, LANE)
    return out


def _ffn_fwd(x, Wup, Wdn, cw, cb, tag):
    h = _mm(x, Wup, 'nt', bmode='bo', tm=512, tn=4096, name=f"ffn_up_{tag}")
    a = _act_fwd(h, cw, cb, name=f"ffn_act_{tag}")
    f = _mm(a, Wdn, 'nn', bmode='abr', tm=512, tn=1024, tk=4096, name=f"ffn_down_{tag}")
    return f, h, a


def _ffn_bwd(df, x, h, a, Wup, Wdn, cw, cb, tag):
    da = _mm(df, Wdn, 'nt', bmode='bo', tm=512, tn=4096, name=f"ffn_da_{tag}")
    dWdn = _mm(a, df, 'tn', bmode='ao', tm=4096, tn=512, name=f"ffn_dwdn_{tag}", out_dtype=BF16)
    dh, dcw, dcb = _act_bwd(h, da, cw, cb, name=f"ffn_actb_{tag}")

    def shard_of(k):
        return (k % 2) * 2 + k // 2

    dx = _mm(dh, Wup, 'nn', bmode='abr', tm=512, tn=1024, tk=4096, name=f"ffn_dx_{tag}", b_map=shard_of)
    dWup = _mm(dh, x, 'tn', bmode='ao', tm=4096, tn=512, name=f"ffn_dwup_{tag}", out_dtype=BF16,
               o_map=shard_of)
    return dx, dWup, dWdn, dcw, dcb


def kernel(x, positions, ev_w_in, ev_b_f, ev_lambda_re, ev_lambda_im, ev_log_step, ev_ssm_b_re, ev_ssm_b_im, ev_ssm_c_re, ev_ssm_c_im, ev_ssm_d, ev_w_glu, ev_w_out, od_w_in, od_sinks, od_w_out, ln_mix_g, ln_mix_b, ffn_w_up, ffn_conv_w, ffn_conv_b, ffn_w_down, ln_ffn_g, ln_ffn_b, loss_target, m_ev_w_in, m_ev_b_f, m_ev_lambda_re, m_ev_lambda_im, m_ev_log_step, m_ev_ssm_b_re, m_ev_ssm_b_im, m_ev_ssm_c_re, m_ev_ssm_c_im, m_ev_ssm_d, m_ev_w_glu, m_ev_w_out, m_od_w_in, m_od_sinks, m_od_w_out, m_ln_mix_g, m_ln_mix_b, m_ffn_w_up, m_ffn_conv_w, m_ffn_conv_b, m_ffn_w_down, m_ln_ffn_g, m_ln_ffn_b, v_ev_w_in, v_ev_b_f, v_ev_lambda_re, v_ev_lambda_im, v_ev_log_step, v_ev_ssm_b_re, v_ev_ssm_b_im, v_ev_ssm_c_re, v_ev_ssm_c_im, v_ev_ssm_d, v_ev_w_glu, v_ev_w_out, v_od_w_in, v_od_sinks, v_od_w_out, v_ln_mix_g, v_ln_mix_b, v_ffn_w_up, v_ffn_conv_w, v_ffn_conv_b, v_ffn_w_down, v_ln_ffn_g, v_ln_ffn_b):
    W = dict(ev_w_in=ev_w_in, ev_b_f=ev_b_f, ev_lambda_re=ev_lambda_re, ev_lambda_im=ev_lambda_im, ev_log_step=ev_log_step, ev_ssm_b_re=ev_ssm_b_re, ev_ssm_b_im=ev_ssm_b_im, ev_ssm_c_re=ev_ssm_c_re, ev_ssm_c_im=ev_ssm_c_im, ev_ssm_d=ev_ssm_d, ev_w_glu=ev_w_glu, ev_w_out=ev_w_out, od_w_in=od_w_in, od_sinks=od_sinks, od_w_out=od_w_out, ln_mix_g=ln_mix_g, ln_mix_b=ln_mix_b, ffn_w_up=ffn_w_up, ffn_conv_w=ffn_conv_w, ffn_conv_b=ffn_conv_b, ffn_w_down=ffn_w_down, ln_ffn_g=ln_ffn_g, ln_ffn_b=ln_ffn_b)
    Mo = dict(ev_w_in=m_ev_w_in, ev_b_f=m_ev_b_f, ev_lambda_re=m_ev_lambda_re, ev_lambda_im=m_ev_lambda_im, ev_log_step=m_ev_log_step, ev_ssm_b_re=m_ev_ssm_b_re, ev_ssm_b_im=m_ev_ssm_b_im, ev_ssm_c_re=m_ev_ssm_c_re, ev_ssm_c_im=m_ev_ssm_c_im, ev_ssm_d=m_ev_ssm_d, ev_w_glu=m_ev_w_glu, ev_w_out=m_ev_w_out, od_w_in=m_od_w_in, od_sinks=m_od_sinks, od_w_out=m_od_w_out, ln_mix_g=m_ln_mix_g, ln_mix_b=m_ln_mix_b, ffn_w_up=m_ffn_w_up, ffn_conv_w=m_ffn_conv_w, ffn_conv_b=m_ffn_conv_b, ffn_w_down=m_ffn_w_down, ln_ffn_g=m_ln_ffn_g, ln_ffn_b=m_ln_ffn_b)
    Vo = dict(ev_w_in=v_ev_w_in, ev_b_f=v_ev_b_f, ev_lambda_re=v_ev_lambda_re, ev_lambda_im=v_ev_lambda_im, ev_log_step=v_ev_log_step, ev_ssm_b_re=v_ev_ssm_b_re, ev_ssm_b_im=v_ev_ssm_b_im, ev_ssm_c_re=v_ev_ssm_c_re, ev_ssm_c_im=v_ev_ssm_c_im, ev_ssm_d=v_ev_ssm_d, ev_w_glu=v_ev_w_glu, ev_w_out=v_ev_w_out, od_w_in=v_od_w_in, od_sinks=v_od_sinks, od_w_out=v_od_w_out, ln_mix_g=v_ln_mix_g, ln_mix_b=v_ln_mix_b, ffn_w_up=v_ffn_w_up, ffn_conv_w=v_ffn_conv_w, ffn_conv_b=v_ffn_conv_b, ffn_w_down=v_ffn_w_down, ln_ffn_g=v_ln_ffn_g, ln_ffn_b=v_ln_ffn_b)
    names = list(W.keys())
    big = ['ev_w_in', 'ev_w_glu', 'ev_w_out', 'od_w_in', 'od_w_out', 'ffn_w_up', 'ffn_w_down']

    S, D = x.shape[1], x.shape[2]
    x0 = x.reshape(S, D)
    tgt = loss_target.reshape(S, D)
    G, Pn, Cg = SSM_GROUPS, SSM_STATE, SSM_GROUP
    Fs = ffn_w_up.shape[2]
    FP = Fs
    Rd = ffn_w_down.shape[1]
    EIN = N_CHIPS * ev_w_in.shape[2]

    def as2d(a):
        return a.reshape(-1, a.shape[-1])

    cwl = ffn_conv_w.reshape(-1)
    cw_rows = _rup(_rup(cwl.shape[0], LANE) // LANE, 32)
    cw_pad = jnp.pad(cwl, (0, cw_rows * LANE - cwl.shape[0])).reshape(cw_rows, LANE)
    transposed = ('ev_w_in', 'ffn_w_up')

    def view(n, a):
        return jnp.transpose(a, (0, 2, 1)) if n in transposed else a

    Wv = {n: view(n, W[n]) for n in big}
    big_e = [(n, l) for n in big for l in range(W[n].shape[0])]
    split_cols = {e: (Wv[e[0]].shape[1] // 2) % 16 != 0 for e in big_e}
    shard16 = {e: Wv[e[0]][e[1]].astype(BF16) for e in big_e}
    grp_now = [e for e in big_e if e[0].startswith('ev_')]
    grp_ffn0 = [('ffn_w_up', 0), ('ffn_w_down', 0)]
    grp_l1 = [('od_w_in', 0), ('od_w_out', 0), ('ffn_w_up', 1), ('ffn_w_down', 1)]
    src_now = [shard16[e] for e in grp_now]
    src_ffn0 = [shard16[e] for e in grp_ffn0] + [cw_pad]
    src_l1 = [shard16[e] for e in grp_l1]
    cols_now = [split_cols[e] for e in grp_now]
    cols_ffn0 = [split_cols[e] for e in grp_ffn0] + [False]
    cols_l1 = [split_cols[e] for e in grp_l1]
    ag_now = _chip_exchange_start('gather', src_now, cols_now, name="ag_l0_start")
    ag_ffn0 = _chip_exchange_start('gather', src_ffn0, cols_ffn0, name="ag_ffn0_start", after=[ag_now[4]])
    ag_l1 = _chip_exchange_start('gather', src_l1, cols_l1, name="ag_l1_start", after=[ag_ffn0[4]])
    started = [ag_l1[4]]

    def finish_gather(started, srcs, cols, after, tag):
        send, recv, thru, lands, _ = started
        thru, lands = _chip_exchange_wait('gather', send, recv, thru, lands, cols, after, name=f"ag_{tag}_wait")
        lands = _sibling_pass_gathered(lands, [s.shape for s in srcs], cols, name=f"ag_{tag}_pass")
        return _own_slot(lands, [s[None] for s in thru])

    lam_r, lam_i = ev_lambda_re[0], ev_lambda_im[0]
    lstep = ev_log_step[0].reshape(G, 1)
    a_re, a_im, g_re, g_im = _s5_disc_fwd(lam_r, lam_i, lstep, name="s5_disc", after=started)
    b_re2, b_im2 = ev_ssm_b_re[0].reshape(G * Pn, Cg), ev_ssm_b_im[0].reshape(G * Pn, Cg)
    g_re1, g_im1 = g_re.reshape(G * Pn, 1), g_im.reshape(G * Pn, 1)
    bb_re, bb_im = _s5_bb_fwd(g_re1, g_im1, b_re2, b_im2, name="s5_bb")
    bbt = jnp.stack([jnp.transpose(b.reshape(G, Pn, Cg), (0, 2, 1)).reshape(G * Cg, Pn) for b in (bb_re, bb_im)])
    BB = _diag_expand(bbt, Cg, Pn, name="s5_bb_dense")
    cct = jnp.stack([jnp.transpose(ev_ssm_c_re[0], (0, 2, 1)).reshape(G * Pn, Cg),
                     jnp.transpose(-ev_ssm_c_im[0], (0, 2, 1)).reshape(G * Pn, Cg)])
    CC = _diag_expand(cct, Pn, Cg, name="s5_cc_dense", after=started)
    a_cat = jnp.stack([a_re.reshape(1, G * Pn), a_im.reshape(1, G * Pn)])
    dskip = ev_ssm_d[0].reshape(1, SSM_WIDTH)
    tabs = _rope_tables(positions.reshape(S, 1).astype(F32), name="rope_tables", after=[BB, CC])

    gw = dict(zip(grp_now, finish_gather(ag_now, src_now, cols_now, [tabs[2]], "l0")))
    gw.update({n: gw[(n, 0)] for n in big if (n, 0) in gw and W[n].shape[0] == 1})
    w_in_t = gw['ev_w_in'].reshape(EIN, D)
    qkv_w = 3 * FOX_WIDTH
    WmainT = jnp.concatenate([w_in_t[:qkv_w], w_in_t[qkv_w + FOX_HEADS:]], axis=0)
    WfT = jnp.pad(w_in_t[qkv_w:qkv_w + FOX_HEADS], ((0, LANE - FOX_HEADS), (0, 0)))
    Wglu = _cols_from_shards(gw['ev_w_glu'])
    Wout_ev = gw['ev_w_out'].reshape(D, D)
    cbs = [ffn_conv_b[l].reshape(N_CHIPS, Fs) for l in range(DEPTH)]

    P = _mm(x0, WmainT, 'nt', name="ev_proj")
    fl = _mm(x0, WfT, 'nt', name="ev_proj_f")
    bf_pad = jnp.pad(ev_b_f.reshape(1, FOX_HEADS), ((0, 0), (0, LANE - FOX_HEADS)))
    cgate, sgate = _gate_fwd(fl, bf_pad, name="fox_gate")
    ccol = jnp.transpose(cgate[:, :FOX_HEADS]).reshape(FOX_HEADS, S, 1)
    crow = jnp.transpose(cgate[:, :FOX_HEADS]).reshape(FOX_HEADS, 1, S)
    fox, lse = _fox_fwd(P, ccol, crow, name="fox_fwd")
    u_s5 = P[:, qkv_w:]
    UT, HT = _DIAG_TILE * Cg, _DIAG_TILE * Pn
    bu = _mm(u_s5, BB, 'nn', bmode='bo', tm=2048, tn=HT, tk=UT, diag='kn', name="s5_bu")
    hh = _s5_scan_fwd(bu, a_cat, name="s5_scan")
    yc = _mm(hh, CC, 'nn', bmode='abr', tm=2048, tn=UT, tk=HT, diag='kn', name="s5_y")
    y_s5, yg = _s5_out_fwd(yc, P, dskip, name="s5_out")
    z = _mm(yg, Wglu, 'nn', name="s5_glu_proj")
    ssm = _glu_fwd(z, name="s5_glu")
    cat = jnp.concatenate([fox.astype(BF16), ssm], axis=1)
    mix0 = _mm(cat, Wout_ev, 'nn', name="ev_out")
    x1, xh1, rs1 = _add_ln_fwd(x0, mix0, ln_mix_g[0], ln_mix_b[0], name="ln_mix0")
    got = finish_gather(ag_ffn0, src_ffn0, cols_ffn0, [x1], "ffn0")
    gw.update(zip(grp_ffn0, got[:-1]))
    cw_all = got[-1].reshape(N_CHIPS, -1)[:, :cwl.shape[0]].reshape(N_CHIPS, DEPTH, 3, Fs)
    cws = [cw_all[:, l] for l in range(DEPTH)]
    Wup = {0: gw[('ffn_w_up', 0)]}
    Wdn = {0: gw[('ffn_w_down', 0)].reshape(2, Fs, D)}
    f0, hf0, af0 = _ffn_fwd(x1, Wup[0], Wdn[0], cws[0], cbs[0], "l0")
    x2, xh2, rs2 = _add_ln_fwd(x1, f0, ln_ffn_g[0], ln_ffn_b[0], name="ln_ffn0")

    gw.update(zip(grp_l1, finish_gather(ag_l1, src_l1, cols_l1, [x2], "l1")))
    Wodin = _cols_from_shards(gw[('od_w_in', 0)])
    Wodout = gw[('od_w_out', 0)].reshape(D, D)
    Wup[1] = gw[('ffn_w_up', 1)]
    Wdn[1] = gw[('ffn_w_down', 1)].reshape(2, Fs, D)
    QW, KW = SWA_HEADS * SWA_HEAD_DIM, SWA_KV_HEADS * SWA_HEAD_DIM
    P1 = _mm(x2, Wodin, 'nn', name="od_proj")
    qr = _rope_apply(P1, tabs, col0=0, width=QW, inverse=False, name="rope_q", out_dtype=BF16)
    kr = _rope_apply(P1, tabs, col0=QW, width=KW, inverse=False, name="rope_k", out_dtype=BF16)

    def heads(a2, nh):
        return jnp.transpose(a2.reshape(S, nh, SWA_HEAD_DIM), (1, 0, 2))

    def unheads(a3):
        return jnp.transpose(a3, (1, 0, 2)).reshape(S, -1)

    qT, kT = heads(qr, SWA_HEADS), heads(kr, SWA_KV_HEADS)
    vT = heads(P1[:, QW + KW:].astype(BF16), SWA_KV_HEADS)
    sink_rows = jnp.broadcast_to(od_sinks[0].reshape(SWA_KV_HEADS, SWA_GROUPS, 1, 1),
                                 (SWA_KV_HEADS, SWA_GROUPS, SWA_WINDOW, 1)).reshape(SWA_KV_HEADS, -1, 1)
    oT, Lsw = _swa_fwd(qT, kT, vT, sink_rows, name="swa_fwd")
    o_sw = unheads(oT).astype(BF16)
    mix1 = _mm(o_sw, Wodout, 'nn', name="od_out")
    x3, xh3, rs3 = _add_ln_fwd(x2, mix1, ln_mix_g[1], ln_mix_b[1], name="ln_mix1")
    f1, hf1, af1 = _ffn_fwd(x3, Wup[1], Wdn[1], cws[1], cbs[1], "l1")
    x4, xh4, rs4 = _add_ln_fwd(x3, f1, ln_ffn_g[1], ln_ffn_b[1], name="ln_ffn1")
    dy, loss_part = _loss_grad(x4, tgt, name="loss")

    dz4, dg_ffn1, db_ffn1 = _ln_bwd(dy, None, xh4, rs4, ln_ffn_g[1], name="lnb_ffn1")
    dx3f, dWup1, dWdn1, dcw1, dcb1 = _ffn_bwd(dz4, x3, hf1, af1, Wup[1], Wdn[1], cws[1], cbs[1], "l1")
    sib_ffn1 = _sibling_halves_start([dWup1, dWdn1.reshape(N_CHIPS, Rd, D)], [False, False], name="rs_ffn1_sib_start")
    dz3, dg_mix1, db_mix1 = _ln_bwd(dz4, dx3f, xh3, rs3, ln_mix_g[1], name="lnb_mix1", after=[sib_ffn1[4]])
    do_sw = _mm(dz3, Wodout, 'nt', name="od_out_dx")
    dWodout = _mm(o_sw, dz3, 'tn', name="od_out_dw", out_dtype=BF16)
    doT = heads(do_sw, SWA_HEADS)
    dqT, dkT, dvT, dsink = _swa_bwd(qT, kT, vT, sink_rows, oT, Lsw, doT, name="swa_bwd")
    dq1 = _rope_apply(unheads(dqT), tabs, col0=0, width=QW, inverse=True, name="rope_dq", out_dtype=BF16)
    dk1 = _rope_apply(unheads(dkT[:, SWA_WINDOW:]), tabs, col0=0, width=KW, inverse=True, name="rope_dk",
                      out_dtype=BF16)
    dP1 = jnp.concatenate([dq1, dk1, unheads(dvT[:, SWA_WINDOW:]).astype(BF16)], axis=1)
    dx2m = _mm(dP1, Wodin, 'nt', name="od_proj_dx")
    dWodin = _mm(x2, dP1, 'tn', name="od_proj_dw", out_dtype=BF16)

    def rs_begin(entries, grads, tag):
        cols = [split_cols[e] for e in entries]
        sib = _sibling_send_halves(grads, cols, name=f"rs_{tag}_sibling")
        return [_sum2_halves(g4, s4, bc, name=f"rs_sum2_{n}{l}")
                for (n, l), g4, s4, bc in zip(entries, grads, sib, cols)]

    def rs_begin_started(entries, started, after, tag):
        send, rcv, thru, lands, _ = started
        thru, lands = _sibling_halves_wait(send, rcv, thru, lands, [False] * len(thru), after,
                                           name=f"rs_{tag}_sib_wait")
        return [_sum2_halves(g4, s4, False, name=f"rs_sum2_{n}{l}") for (n, l), g4, s4 in zip(entries, thru, lands)]

    def own_parts(parts):
        me = 2 * lax.axis_index("x") + lax.axis_index("y")
        return [lax.dynamic_slice_in_dim(p, me, 1, axis=0) for p in parts]

    part_l1 = (rs_begin(grp_l1[:2], [_shards_from_cols(dWodin), dWodout.reshape(N_CHIPS, D // N_CHIPS, D)], "od")
               + rs_begin_started(grp_l1[2:], sib_ffn1, [dWodin], "ffn1"))
    rs_l1 = _chip_exchange_start('scatter', part_l1, [False] * len(part_l1), name="rs_l1_start")

    dz2, dg_ffn0, db_ffn0 = _ln_bwd(dz3, dx2m, xh2, rs2, ln_ffn_g[0], name="lnb_ffn0", after=[rs_l1[4]])
    dx1f, dWup0, dWdn0, dcw0, dcb0 = _ffn_bwd(dz2, x1, hf0, af0, Wup[0], Wdn[0], cws[0], cbs[0], "l0")
    sib_ffn0 = _sibling_halves_start([dWup0, dWdn0.reshape(N_CHIPS, Rd, D)], [False, False], name="rs_ffn0_sib_start")
    dz1, dg_mix0, db_mix0 = _ln_bwd(dz2, dx1f, xh1, rs1, ln_mix_g[0], name="lnb_mix0", after=[sib_ffn0[4]])
    dcat = _mm(dz1, Wout_ev, 'nt', name="ev_out_dx")
    dWout_ev = _mm(cat, dz1, 'tn', name="ev_out_dw", out_dtype=BF16)
    part_ffn0 = rs_begin_started(grp_ffn0, sib_ffn0, [dWout_ev], "ffn0")
    rs_ffn0 = _chip_exchange_start('scatter', part_ffn0, [False] * len(part_ffn0), name="rs_ffn0_start")
    dz = _glu_bwd(z, dcat, name="s5_glu_bwd")
    dyg = _mm(dz, Wglu, 'nt', name="s5_glu_dx", after=[rs_ffn0[4]])
    dWglu = _mm(yg, dz, 'tn', name="s5_glu_dw", out_dtype=BF16)
    dy_s5, du_dir, dD = _s5_out_bwd(dyg, y_s5, P, dskip, name="s5_out_bwd")
    dhh = _mm(dy_s5, CC, 'nt', bmode='bo', tm=2048, tn=HT, tk=UT, diag='kn', name="s5_y_dx")
    dCC = _mm(hh, dy_s5, 'tn', bmode='ao', tm=HT, tn=UT, diag='mn', name="s5_y_dw")
    lam, da_s5 = _s5_scan_bwd(dhh, hh, a_cat, name="s5_scan_bwd")
    du_bu = _mm(lam, BB, 'nt', bmode='abr', tm=2048, tn=UT, tk=HT, diag='kn', name="s5_bu_dx")
    dBB = _mm(u_s5, lam, 'tn', bmode='bo', tm=UT, tn=HT, diag='mn', name="s5_bu_dw")
    du = _combine([du_dir, du_bu], [1.0, 1.0], name="s5_du", out_dtype=BF16)
    dq0, dk0, dv0, dccol, dcrow = _fox_bwd(P, ccol, crow, fox, lse, dcat, name="fox_bwd")
    dc = jnp.transpose((dccol.reshape(FOX_HEADS, S) - dcrow.reshape(FOX_HEADS, S)))
    dc = jnp.pad(dc, ((0, 0), (0, LANE - FOX_HEADS)))
    dfl, dbf = _gate_bwd(dc, sgate, name="fox_gate_bwd")
    dP = jnp.concatenate([dq0, dk0, dv0, du], axis=1)
    dx0a = _mm(dP, WmainT, 'nn', name="ev_proj_dx")
    dx0b = _mm(dfl, WfT, 'nn', name="ev_proj_f_dx")
    dWmainT = _mm(dP, x0, 'tn', tm=1024, tn=1024, name="ev_proj_dw", out_dtype=BF16)
    dWfT = _mm(dfl, x0, 'tn', name="ev_proj_f_dw", out_dtype=BF16)
    grad_x = _combine([dz1, dx0a, dx0b], [ALPHA, 1.0, 1.0], name="grad_x")

    dbbt = _diag_extract(dBB, Cg, Pn, name="s5_bb_diag")
    dcct = _diag_extract(dCC, Pn, Cg, name="s5_cc_diag")
    dbb_re = jnp.transpose(dbbt[0].reshape(G, Cg, Pn), (0, 2, 1)).reshape(G * Pn, Cg)
    dbb_im = jnp.transpose(dbbt[1].reshape(G, Cg, Pn), (0, 2, 1)).reshape(G * Pn, Cg)
    db_re, db_im, dg_re1, dg_im1 = _s5_bb_bwd(g_re1, g_im1, b_re2, b_im2, dbb_re, dbb_im, name="s5_bb_bwd")
    dlam_re, dlam_im, dlstep = _s5_disc_bwd(lam_r, lam_i, lstep, da_s5[0].reshape(G, Pn), da_s5[1].reshape(G, Pn),
                                            dg_re1.reshape(G, Pn), dg_im1.reshape(G, Pn), name="s5_disc_bwd")
    dc_re = jnp.transpose(dcct[0].reshape(G, Pn, Cg), (0, 2, 1))
    dc_im = -jnp.transpose(dcct[1].reshape(G, Pn, Cg), (0, 2, 1))

    def conv_w_full(d0, d1):
        return jnp.stack([jnp.reshape(jnp.transpose(d[:, :, :Fs], (1, 0, 2)), (3, N_CHIPS * Fs)) for d in (d0, d1)])

    def conv_b_full(d0, d1):
        return jnp.stack([jnp.reshape(d[:, 0, :Fs], (N_CHIPS * Fs,)) for d in (d0, d1)])

    small_local = dict(
        ev_b_f=dbf[:, :FOX_HEADS], ev_lambda_re=dlam_re, ev_lambda_im=dlam_im, ev_log_step=dlstep,
        ev_ssm_b_re=db_re, ev_ssm_b_im=db_im, ev_ssm_c_re=dc_re, ev_ssm_c_im=dc_im, ev_ssm_d=dD,
        od_sinks=dsink[:, :, 0],
        ln_mix_g=jnp.concatenate([dg_mix0, dg_mix1]), ln_mix_b=jnp.concatenate([db_mix0, db_mix1]),
        ffn_conv_w=conv_w_full(dcw0, dcw1), ffn_conv_b=conv_b_full(dcb0, dcb1),
        ln_ffn_g=jnp.concatenate([dg_ffn0, dg_ffn1]), ln_ffn_b=jnp.concatenate([db_ffn0, db_ffn1]))
    small = list(small_local.keys())
    out_g, out_d, out_m, out_v = {}, {}, {}, {}
    loss_out = []

    def small_update(after):
        red = _all_reduce_small(_pack([small_local[n] for n in small] + [loss_part]), name="ar_small", after=after)
        full_shapes = [W[n].shape if n != 'ffn_conv_w' else (DEPTH, 3, N_CHIPS * Fs) for n in small]
        pieces = _unpack(red, full_shapes + [()])
        loss_out.append(pieces[-1])
        gsmall = dict(zip(small, pieces[:-1]))
        chip = 2 * lax.axis_index("x") + lax.axis_index("y")
        gsmall['ffn_conv_w'] = lax.dynamic_slice_in_dim(gsmall['ffn_conv_w'], chip * Fs, Fs, axis=2)
        shapes = [W[n].shape for n in small]
        gs, ds_, ms, vs = _adamw(_pack([W[n] for n in small])[None], _pack([gsmall[n] for n in small])[None],
                                 _pack([Mo[n] for n in small])[None], _pack([Vo[n] for n in small])[None],
                                 name="adamw_small", tr=1 << 14)
        out_g.update(zip(small, _unpack(gs, shapes)))
        out_d.update(zip(small, _unpack(ds_, shapes)))
        out_m.update(zip(small, _unpack(ms, shapes)))
        out_v.update(zip(small, _unpack(vs, shapes)))
        return vs

    dw_in_t = jnp.concatenate([dWmainT[:qkv_w], dWfT[:FOX_HEADS], dWmainT[qkv_w:]], axis=0)
    part_now = rs_begin(grp_now, [dw_in_t.reshape(N_CHIPS, EIN // N_CHIPS, D), _shards_from_cols(dWglu),
                                  dWout_ev.reshape(N_CHIPS, D // N_CHIPS, D)], "l0")
    rs_now = _chip_exchange_start('scatter', part_now, [False] * len(part_now), name="rs_l0_start")

    def finish_scatter(started, parts, after, tag):
        send, rcv, thru, lands, _ = started
        thru, lands = _chip_exchange_wait('scatter', send, rcv, thru, lands, [False] * len(parts), after,
                                          name=f"rs_{tag}_wait")
        return _own_slot(lands, own_parts(thru))

    def update(entries, recv, tag, meanwhile=None):
        halves = [_rowsum(r, name=f"rs_sum4_{e[0]}{e[1]}") for e, r in zip(entries, recv)]
        if meanwhile is None:
            others = _sibling_join_halves(halves, name=f"rs_{tag}_join")
        else:
            send, rcv, thru, lands, tok = _sibling_swap_start(halves, name=f"rs_{tag}_join_start")
            halves, others = _sibling_swap_wait(send, rcv, thru, lands, [meanwhile([tok])],
                                                name=f"rs_{tag}_join_wait")
        pairs = dict(zip(entries, zip(halves, others)))
        done = []
        for n in dict.fromkeys(e[0] for e in entries):
            res = _adamw(Wv[n], [pairs[(n, l)] for l in range(W[n].shape[0])], view(n, Mo[n]), view(n, Vo[n]),
                         name=f"adamw_{n}", by_cols=split_cols[(n, 0)])
            out_g[n], out_d[n], out_m[n], out_v[n] = (view(n, t) for t in res)
            done.append(res[3])
        return done

    recv_rest = (finish_scatter(rs_l1, part_l1, [rs_now[4]], "l1")
                 + finish_scatter(rs_ffn0, part_ffn0, [rs_now[4]], "ffn0"))
    done = update(grp_l1 + grp_ffn0, recv_rest, "rest", meanwhile=small_update)
    update(grp_now, finish_scatter(rs_now, part_now, done, "l0"), "l0")
    loss = loss_out[0]

    return (loss, grad_x.reshape(1, S, D), *[out_g[n] for n in names], *[out_d[n] for n in names],
            *[out_m[n] for n in names], *[out_v[n] for n in names])
```

```python
import functools
import math

import numpy as np
import jax
import jax.numpy as jnp
from jax import lax
from jax.experimental import pallas as pl
from jax.experimental.pallas import tpu as pltpu

F32 = jnp.float32
BF16 = jnp.bfloat16
MESH = pl.DeviceIdType.MESH
ANY = pl.BlockSpec(memory_space=pl.ANY)

D_MODEL = 2048
FOX_HEADS = 8
FOX_HEAD_DIM = 128
FOX_WIDTH = 1024
SSM_WIDTH = 1024
SSM_GROUP = 16
SSM_GROUPS = 64
SSM_STATE = 64
SWA_HEADS = 32
SWA_KV_HEADS = 4
SWA_HEAD_DIM = 64
SWA_GROUPS = 8
SWA_WINDOW = 128
ROPE_DIM = 16
ROPE_THETA = 500000.0
LN_EPS = 1e-5
DEPTH = 2
ALPHA = (2.0 * DEPTH) ** 0.25
ADAM_LR = 0.001
ADAM_B1 = 0.9
ADAM_B2 = 0.999
ADAM_EPS = 1e-08
ADAM_WD = 0.01
ADAM_STEP = 10
N_CHIPS = 4

VMEM_LIMIT = 56 * 1024 * 1024
LANE = 128


def _call(body, after=(), **kw):
    if after:
        n = len(after)

        def shifted(*refs):
            return body(*refs[n:])

        call = _call(shifted, **dict(kw, in_specs=[ANY] * n + list(kw["in_specs"])))
        return lambda *args: call(*after, *args)
    return pl.pallas_call(body, **kw)


def _cparams(sem):
    return pltpu.CompilerParams(dimension_semantics=sem, vmem_limit_bytes=VMEM_LIMIT)


def _rup(n, m):
    return (n + m - 1) // m * m


def _pick(n, pref):
    if n <= pref:
        return n
    for step in (128, 16, 8):
        for t in range(pref - pref % step, 0, -step):
            if n % t == 0:
                return t
    return n


def _tile2d(rows, cols, pref_rows=256, budget=256 * 1024):
    tr = _pick(rows, pref_rows)
    if tr < 64:
        tr = rows
    if cols % LANE:
        return tr, cols
    return tr, _pick(cols, max(LANE, budget // tr // LANE * LANE))


def _mm(a, b, mode, *, name, tm=512, tn=1024, tk=2048, bmode=None, out_dtype=F32, after=(), b_map=None,
        o_map=None, diag=None):
    a3 = a if a.ndim == 3 else a[None]
    b3 = b if b.ndim == 3 else b[None]
    if mode == 'tn':
        K, M = a3.shape[1:]
    else:
        M, K = a3.shape[1:]
    N = b3.shape[1] if mode == 'nt' else b3.shape[2]
    tm, tn, tk = _pick(M, tm), _pick(N, tn), _pick(K, tk)
    nb = max(a3.shape[0], b3.shape[0])
    nbo, nbr = (1, nb) if bmode == 'abr' else (nb, 1)
    nm, nk = M // tm, K // tk
    if diag == 'kn':
        assert K // tk == N // tn
        nk = 1
    if diag == 'mn':
        assert M // tm == N // tn
        nm = 1
    nred = nbr * nk
    a_b = bmode in ('ao', 'abr')
    b_b = bmode in ('bo', 'abr')
    o_b = bmode in ('bo', 'ao')

    def bsel(flag, bo, br, remap=None):
        if not flag:
            return 0
        return (bo + br) if remap is None else remap(bo + br)

    def mi(i, j):
        return j if diag == 'mn' else i

    def ki(j, k):
        return j if diag == 'kn' else k

    if mode == 'tn':
        a_spec = pl.BlockSpec((None, tk, tm), lambda bo, i, j, br, k: (bsel(a_b, bo, br), ki(j, k), mi(i, j)))
    else:
        a_spec = pl.BlockSpec((None, tm, tk), lambda bo, i, j, br, k: (bsel(a_b, bo, br), mi(i, j), ki(j, k)))
    if mode == 'nt':
        b_spec = pl.BlockSpec((None, tn, tk), lambda bo, i, j, br, k: (bsel(b_b, bo, br, b_map), j, ki(j, k)))
    else:
        b_spec = pl.BlockSpec((None, tk, tn), lambda bo, i, j, br, k: (bsel(b_b, bo, br, b_map), ki(j, k), j))
    o_spec = pl.BlockSpec((None, tm, tn), lambda bo, i, j, br, k: (bsel(o_b, bo, br, o_map), mi(i, j), j))
    dn = {'nn': (((1,), (0,)), ((), ())), 'nt': (((1,), (1,)), ((), ())), 'tn': (((0,), (0,)), ((), ()))}[mode]

    def body(a_ref, b_ref, *rest):
        o_ref, scratch = rest[len(after)], rest[len(after) + 1:]
        r = lax.dot_general(a_ref[...].astype(BF16), b_ref[...].astype(BF16), dn, preferred_element_type=F32)
        if nred == 1:
            o_ref[...] = r.astype(out_dtype)
        else:
            acc = scratch[0]
            step = pl.program_id(3) * nk + pl.program_id(4)

            @pl.when(step == 0)
            def _():
                acc[...] = r

            @pl.when(step > 0)
            def _():
                acc[...] += r

            @pl.when(step == nred - 1)
            def _():
                o_ref[...] = acc[...].astype(out_dtype)

    out = _call(
        body, name=name,
        grid=(nbo, nm, N // tn, nbr, nk),
        in_specs=[a_spec, b_spec] + [ANY] * len(after), out_specs=o_spec,
        out_shape=jax.ShapeDtypeStruct((nbo if o_b else 1, M, N), out_dtype),
        scratch_shapes=[] if nred == 1 else [pltpu.VMEM((tm, tn), F32)],
        compiler_params=_cparams(("parallel", "parallel", "parallel", "arbitrary", "arbitrary")),
    )(a3, b3, *after)
    return out if o_b else out[0]


def _add_ln_fwd(x, r, g, b, *, name):
    S, D = x.shape
    tr = _pick(S, 256)

    def body(x_ref, r_ref, g_ref, b_ref, o_ref, xh_ref, rs_ref):
        z = ALPHA * x_ref[...] + r_ref[...]
        mu = jnp.mean(z, axis=-1, keepdims=True)
        zc = z - mu
        var = jnp.mean(zc * zc, axis=-1, keepdims=True)
        rstd = lax.rsqrt(var + LN_EPS)
        xh = zc * rstd
        xh_ref[...] = xh
        rs_ref[...] = rstd
        o_ref[...] = xh * g_ref[...] + b_ref[...]

    row = pl.BlockSpec((tr, D), lambda i: (i, 0))
    vec = pl.BlockSpec((1, D), lambda i: (0, 0))
    return _call(
        body, name=name, grid=(S // tr,),
        in_specs=[row, row, vec, vec],
        out_specs=[row, row, pl.BlockSpec((tr, 1), lambda i: (i, 0))],
        out_shape=[jax.ShapeDtypeStruct((S, D), F32), jax.ShapeDtypeStruct((S, D), F32),
                   jax.ShapeDtypeStruct((S, 1), F32)],
        compiler_params=_cparams(("parallel",)),
    )(x, r, g.reshape(1, D), b.reshape(1, D))


def _ln_bwd(da, db, xhat, rstd, g, *, name, after=()):
    S, D = xhat.shape
    tr = _pick(S, 256)
    two = db is not None

    def body(*refs):
        refs = refs[len(after):]
        if two:
            da_ref, db_ref, xh_ref, rs_ref, g_ref, dz_ref, dg_ref, dbt_ref = refs
            dy = ALPHA * da_ref[...] + db_ref[...]
        else:
            da_ref, xh_ref, rs_ref, g_ref, dz_ref, dg_ref, dbt_ref = refs
            dy = da_ref[...]
        xh = xh_ref[...]
        dxh = dy * g_ref[...]
        m1 = jnp.mean(dxh, axis=-1, keepdims=True)
        m2 = jnp.mean(dxh * xh, axis=-1, keepdims=True)
        dz_ref[...] = rs_ref[...] * (dxh - m1 - xh * m2)
        pg = jnp.sum(dy * xh, axis=0, keepdims=True)
        pb = jnp.sum(dy, axis=0, keepdims=True)

        @pl.when(pl.program_id(0) == 0)
        def _():
            dg_ref[...] = pg
            dbt_ref[...] = pb

        @pl.when(pl.program_id(0) > 0)
        def _():
            dg_ref[...] += pg
            dbt_ref[...] += pb

    row = pl.BlockSpec((tr, D), lambda i: (i, 0))
    vec = pl.BlockSpec((1, D), lambda i: (0, 0))
    ins = list(after) + [da] + ([db] if two else []) + [xhat, rstd, g.reshape(1, D)]
    in_specs = [ANY] * len(after) + [row] + ([row] if two else []) + [row, pl.BlockSpec((tr, 1), lambda i: (i, 0)), vec]
    return _call(
        body, name=name, grid=(S // tr,),
        in_specs=in_specs, out_specs=[row, vec, vec],
        out_shape=[jax.ShapeDtypeStruct((S, D), F32), jax.ShapeDtypeStruct((1, D), F32),
                   jax.ShapeDtypeStruct((1, D), F32)],
        compiler_params=_cparams(("arbitrary",)),
    )(*ins)


def _loss_grad(y, t, *, name):
    S, D = y.shape
    tr = _pick(S, 256)

    def body(y_ref, t_ref, dy_ref, l_ref):
        e = y_ref[...] - t_ref[...]
        dy_ref[...] = e * (1.0 / D)
        part = 0.5 * jnp.sum(jnp.sum(e * e, axis=-1, keepdims=True) * (1.0 / D), axis=0, keepdims=True)

        @pl.when(pl.program_id(0) == 0)
        def _():
            l_ref[...] = part

        @pl.when(pl.program_id(0) > 0)
        def _():
            l_ref[...] += part

    row = pl.BlockSpec((tr, D), lambda i: (i, 0))
    return _call(
        body, name=name, grid=(S // tr,), in_specs=[row, row],
        out_specs=[row, pl.BlockSpec((1, 1), lambda i: (0, 0))],
        out_shape=[jax.ShapeDtypeStruct((S, D), F32), jax.ShapeDtypeStruct((1, 1), F32)],
        compiler_params=_cparams(("arbitrary",)),
    )(y, t)


def _combine(terms, scales, *, name, out_dtype=F32):
    S, D = terms[0].shape
    tr = _pick(S, 256)
    n = len(terms)

    def body(*refs):
        acc = scales[0] * refs[0][...].astype(F32)
        for i in range(1, n):
            acc = acc + scales[i] * refs[i][...].astype(F32)
        refs[n][...] = acc.astype(out_dtype)

    row = pl.BlockSpec((tr, D), lambda i: (i, 0))
    return _call(
        body, name=name, grid=(S // tr,), in_specs=[row] * n, out_specs=row,
        out_shape=jax.ShapeDtypeStruct((S, D), out_dtype),
        compiler_params=_cparams(("parallel",)),
    )(*terms)


def _split3(x):
    h = x.astype(BF16)
    r = x - h.astype(F32)
    m = r.astype(BF16)
    l = (r - m.astype(F32)).astype(BF16)
    return h, m, l


def _tri_matmul(tri_bf, x):
    h, m, l = _split3(x)
    dn = (((1,), (0,)), ((), ()))
    return (lax.dot_general(tri_bf, l, dn, preferred_element_type=F32)
            + lax.dot_general(tri_bf, m, dn, preferred_element_type=F32)
            + lax.dot_general(tri_bf, h, dn, preferred_element_type=F32))


def _gate_fwd(fl, bf, *, name):
    S = fl.shape[0]
    tc = _pick(S, 256)
    nchunk = S // tc

    def body(fl_ref, bf_ref, c_ref, sg_ref):
        r = lax.broadcasted_iota(jnp.int32, (tc, tc), 0)
        cidx = lax.broadcasted_iota(jnp.int32, (tc, tc), 1)
        tri = (r >= cidx).astype(BF16)
        carry = jnp.zeros((1, LANE), F32)
        for ch in range(nchunk):
            x = fl_ref[pl.ds(ch * tc, tc), :] + bf_ref[...]
            lf = jnp.minimum(x, 0.0) - jnp.log(1.0 + jnp.exp(-jnp.abs(x)))
            sg_ref[pl.ds(ch * tc, tc), :] = jax.nn.sigmoid(-x)
            c_ref[pl.ds(ch * tc, tc), :] = _tri_matmul(tri, lf) + carry
            carry = carry + jnp.sum(lf, axis=0, keepdims=True)

    full = pl.BlockSpec((S, LANE), lambda: (0, 0))
    return _call(
        body, name=name, in_specs=[full, pl.BlockSpec((1, LANE), lambda: (0, 0))], out_specs=[full, full],
        out_shape=[jax.ShapeDtypeStruct((S, LANE), F32)] * 2,
        compiler_params=pltpu.CompilerParams(vmem_limit_bytes=VMEM_LIMIT),
    )(fl, bf)


def _gate_bwd(dc, sg, *, name):
    S = dc.shape[0]
    tc = _pick(S, 256)
    nchunk = S // tc

    def body(dc_ref, sg_ref, dfl_ref, db_ref):
        r = lax.broadcasted_iota(jnp.int32, (tc, tc), 0)
        cidx = lax.broadcasted_iota(jnp.int32, (tc, tc), 1)
        tri = (r <= cidx).astype(BF16)
        carry = jnp.zeros((1, LANE), F32)
        dbacc = jnp.zeros((1, LANE), F32)
        for ch in reversed(range(nchunk)):
            d = dc_ref[pl.ds(ch * tc, tc), :]
            dfl = (_tri_matmul(tri, d) + carry) * sg_ref[pl.ds(ch * tc, tc), :]
            dfl_ref[pl.ds(ch * tc, tc), :] = dfl
            dbacc = dbacc + jnp.sum(dfl, axis=0, keepdims=True)
            carry = carry + jnp.sum(d, axis=0, keepdims=True)
        db_ref[...] = dbacc

    full = pl.BlockSpec((S, LANE), lambda: (0, 0))
    return _call(
        body, name=name, in_specs=[full, full], out_specs=[full, pl.BlockSpec((1, LANE), lambda: (0, 0))],
        out_shape=[jax.ShapeDtypeStruct((S, LANE), F32), jax.ShapeDtypeStruct((1, LANE), F32)],
        compiler_params=pltpu.CompilerParams(vmem_limit_bytes=VMEM_LIMIT),
    )(dc, sg)


def _fox_scores(q_ref, k_ref, cc_ref, cr_ref, qi, tq, S):
    scale = 1.0 / math.sqrt(FOX_HEAD_DIM)
    s = lax.dot_general(q_ref[...].astype(BF16), k_ref[...].astype(BF16), (((1,), (1,)), ((), ())),
                        preferred_element_type=F32) * scale
    s = s + cc_ref[...] - cr_ref[...]
    row = lax.broadcasted_iota(jnp.int32, (tq, S), 0) + qi * tq
    col = lax.broadcasted_iota(jnp.int32, (tq, S), 1)
    return s, row >= col


def _fox_fwd(P, ccol, crow, *, name):
    S = P.shape[0]
    tq = _pick(S, 256)
    H = FOX_HEADS

    def body(q_ref, k_ref, v_ref, cc_ref, cr_ref, o_ref, l_ref):
        s, causal = _fox_scores(q_ref, k_ref, cc_ref, cr_ref, pl.program_id(1), tq, S)
        s = jnp.where(causal, s, -1e30)
        m = jnp.max(s, axis=-1, keepdims=True)
        e = jnp.exp(s - m)
        den = jnp.sum(e, axis=-1, keepdims=True)
        p = e / den
        o_ref[...] = jnp.dot(p.astype(BF16), v_ref[...].astype(BF16), preferred_element_type=F32)
        l_ref[...] = m + jnp.log(den)

    return _call(
        body, name=name, grid=(H, S // tq),
        in_specs=[pl.BlockSpec((tq, 128), lambda h, i: (i, h)),
                  pl.BlockSpec((S, 128), lambda h, i: (0, H + h)),
                  pl.BlockSpec((S, 128), lambda h, i: (0, 2 * H + h)),
                  pl.BlockSpec((None, tq, 1), lambda h, i: (h, i, 0)),
                  pl.BlockSpec((None, 1, S), lambda h, i: (h, 0, 0))],
        out_specs=[pl.BlockSpec((tq, 128), lambda h, i: (i, h)),
                   pl.BlockSpec((None, tq, 1), lambda h, i: (h, i, 0))],
        out_shape=[jax.ShapeDtypeStruct((S, FOX_WIDTH), F32), jax.ShapeDtypeStruct((H, S, 1), F32)],
        compiler_params=_cparams(("parallel", "parallel")),
    )(P, P, P, ccol, crow)


def _fox_bwd(P, ccol, crow, o, lse, dcat, *, name):
    S = P.shape[0]
    tq = _pick(S, 256)
    H = FOX_HEADS
    nq = S // tq
    scale = 1.0 / math.sqrt(FOX_HEAD_DIM)

    def body(q_ref, k_ref, v_ref, cc_ref, cr_ref, o_ref, l_ref, do_ref,
             dq_ref, dk_ref, dv_ref, dcc_ref, dcr_ref, dk_acc, dv_acc):
        qi = pl.program_id(1)
        s, causal = _fox_scores(q_ref, k_ref, cc_ref, cr_ref, qi, tq, S)
        p = jnp.where(causal, jnp.exp(s - l_ref[...]), 0.0)
        do = do_ref[...]
        do_bf = do.astype(BF16)
        dp = lax.dot_general(do_bf, v_ref[...].astype(BF16), (((1,), (1,)), ((), ())), preferred_element_type=F32)
        delta = jnp.sum(do * o_ref[...], axis=-1, keepdims=True)
        ds = p * (dp - delta)
        ds_bf = ds.astype(BF16)
        dq_ref[...] = (jnp.dot(ds_bf, k_ref[...].astype(BF16), preferred_element_type=F32) * scale).astype(BF16)
        dkp = lax.dot_general(ds_bf, q_ref[...].astype(BF16), (((0,), (0,)), ((), ())),
                              preferred_element_type=F32) * scale
        dvp = lax.dot_general(p.astype(BF16), do_bf, (((0,), (0,)), ((), ())), preferred_element_type=F32)
        dcc_ref[...] = jnp.sum(ds, axis=-1, keepdims=True)
        dcr = jnp.sum(ds, axis=0, keepdims=True)

        @pl.when(qi == 0)
        def _():
            dk_acc[...] = dkp
            dv_acc[...] = dvp
            dcr_ref[...] = dcr

        @pl.when(qi > 0)
        def _():
            dk_acc[...] += dkp
            dv_acc[...] += dvp
            dcr_ref[...] += dcr

        @pl.when(qi == nq - 1)
        def _():
            dk_ref[...] = dk_acc[...].astype(BF16)
            dv_ref[...] = dv_acc[...].astype(BF16)

    qblk = pl.BlockSpec((tq, 128), lambda h, i: (i, h))
    kvo = pl.BlockSpec((S, 128), lambda h, i: (0, h))
    col = pl.BlockSpec((None, tq, 1), lambda h, i: (h, i, 0))
    rowv = pl.BlockSpec((None, 1, S), lambda h, i: (h, 0, 0))
    return _call(
        body, name=name, grid=(H, nq),
        in_specs=[qblk,
                  pl.BlockSpec((S, 128), lambda h, i: (0, H + h)),
                  pl.BlockSpec((S, 128), lambda h, i: (0, 2 * H + h)),
                  col, rowv, qblk, col, qblk],
        out_specs=[qblk, kvo, kvo, col, rowv],
        out_shape=[jax.ShapeDtypeStruct((S, FOX_WIDTH), BF16)] * 3
        + [jax.ShapeDtypeStruct((H, S, 1), F32), jax.ShapeDtypeStruct((H, 1, S), F32)],
        scratch_shapes=[pltpu.VMEM((S, 128), F32), pltpu.VMEM((S, 128), F32)],
        compiler_params=_cparams(("parallel", "arbitrary")),
    )(P, P, P, ccol, crow, o, lse, dcat)


def _s5_disc_fwd(lr, li, ls, *, name, after=()):
    G, Pn = lr.shape

    def body(lr_ref, li_ref, ls_ref, ar_ref, ai_ref, gr_ref, gi_ref):
        lr_, li_ = lr_ref[...], li_ref[...]
        dt = jnp.exp(ls_ref[...])
        mag = jnp.exp(lr_ * dt)
        th = li_ * dt
        ar = mag * jnp.cos(th)
        ai = mag * jnp.sin(th)
        den = lr_ * lr_ + li_ * li_
        xr = ar - 1.0
        ar_ref[...] = ar
        ai_ref[...] = ai
        gr_ref[...] = (xr * lr_ + ai * li_) / den
        gi_ref[...] = (ai * lr_ - xr * li_) / den

    sq = pl.BlockSpec((G, Pn), lambda: (0, 0))
    return _call(
        body, after=after, name=name, in_specs=[sq, sq, pl.BlockSpec((G, 1), lambda: (0, 0))], out_specs=[sq] * 4,
        out_shape=[jax.ShapeDtypeStruct((G, Pn), F32)] * 4,
    )(lr, li, ls)


def _s5_disc_bwd(lr, li, ls, dar, dai, dgr, dgi, *, name):
    G, Pn = lr.shape

    def body(lr_ref, li_ref, ls_ref, dar_ref, dai_ref, dgr_ref, dgi_ref, dlr_ref, dli_ref, dls_ref):
        lr_, li_ = lr_ref[...], li_ref[...]
        dt = jnp.exp(ls_ref[...])
        mag = jnp.exp(lr_ * dt)
        th = li_ * dt
        ar = mag * jnp.cos(th)
        ai = mag * jnp.sin(th)
        den = lr_ * lr_ + li_ * li_
        xr = ar - 1.0
        xi = ai
        g_re = (xr * lr_ + xi * li_) / den
        g_im = (xi * lr_ - xr * li_) / den
        dgr_, dgi_ = dgr_ref[...], dgi_ref[...]
        dxr = (dgr_ * lr_ - dgi_ * li_) / den
        dxi = (dgr_ * li_ + dgi_ * lr_) / den
        dden = -(dgr_ * g_re + dgi_ * g_im) / den
        dlr = (dgr_ * xr + dgi_ * xi) / den + 2.0 * dden * lr_
        dli = (dgr_ * xi - dgi_ * xr) / den + 2.0 * dden * li_
        da_r = dar_ref[...] + dxr
        da_i = dai_ref[...] + dxi
        dmag_mag = da_r * ar + da_i * ai
        dth = da_i * ar - da_r * ai
        dlr_ref[...] = dlr + dmag_mag * dt
        dli_ref[...] = dli + dth * dt
        ddt = jnp.sum(dmag_mag * lr_ + dth * li_, axis=-1, keepdims=True)
        dls_ref[...] = ddt * dt

    sq = pl.BlockSpec((G, Pn), lambda: (0, 0))
    c1 = pl.BlockSpec((G, 1), lambda: (0, 0))
    return _call(
        body, name=name, in_specs=[sq, sq, c1, sq, sq, sq, sq], out_specs=[sq, sq, c1],
        out_shape=[jax.ShapeDtypeStruct((G, Pn), F32)] * 2 + [jax.ShapeDtypeStruct((G, 1), F32)],
    )(lr, li, ls, dar, dai, dgr, dgi)


def _s5_bb_fwd(gr, gi, br, bi, *, name):
    R, C = br.shape

    def body(gr_ref, gi_ref, br_ref, bi_ref, or_ref, oi_ref):
        g_r, g_i, b_r, b_i = gr_ref[...], gi_ref[...], br_ref[...], bi_ref[...]
        or_ref[...] = g_r * b_r - g_i * b_i
        oi_ref[...] = g_r * b_i + g_i * b_r

    w = pl.BlockSpec((R, C), lambda: (0, 0))
    c1 = pl.BlockSpec((R, 1), lambda: (0, 0))
    return _call(body, name=name, in_specs=[c1, c1, w, w], out_specs=[w, w],
                 out_shape=[jax.ShapeDtypeStruct((R, C), F32)] * 2)(gr, gi, br, bi)


def _s5_bb_bwd(gr, gi, br, bi, dbbr, dbbi, *, name):
    R, C = br.shape

    def body(gr_ref, gi_ref, br_ref, bi_ref, dr_ref, di_ref, dbr_ref, dbi_ref, dgr_ref, dgi_ref):
        g_r, g_i, b_r, b_i = gr_ref[...], gi_ref[...], br_ref[...], bi_ref[...]
        d_r, d_i = dr_ref[...], di_ref[...]
        dbr_ref[...] = g_r * d_r + g_i * d_i
        dbi_ref[...] = g_r * d_i - g_i * d_r
        dgr_ref[...] = jnp.sum(d_r * b_r + d_i * b_i, axis=-1, keepdims=True)
        dgi_ref[...] = jnp.sum(d_i * b_r - d_r * b_i, axis=-1, keepdims=True)

    w = pl.BlockSpec((R, C), lambda: (0, 0))
    c1 = pl.BlockSpec((R, 1), lambda: (0, 0))
    return _call(body, name=name, in_specs=[c1, c1, w, w, w, w], out_specs=[w, w, c1, c1],
                 out_shape=[jax.ShapeDtypeStruct((R, C), F32)] * 2 + [jax.ShapeDtypeStruct((R, 1), F32)] * 2,
                 )(gr, gi, br, bi, dbbr, dbbi)


_DIAG_TILE = 8


def _diag_mask(gr, gc):
    rows, cols = _DIAG_TILE * gr, _DIAG_TILE * gc
    r = lax.broadcasted_iota(jnp.int32, (rows, cols), 0) >> (gr.bit_length() - 1)
    c = lax.broadcasted_iota(jnp.int32, (rows, cols), 1) >> (gc.bit_length() - 1)
    return r == c


def _diag_expand(t2, gr, gc, *, name, after=()):
    _, R, _ = t2.shape
    G = R // gr
    nt = G // _DIAG_TILE
    rows, cols = _DIAG_TILE * gr, _DIAG_TILE * gc

    def body(t_ref, o_ref):
        src = lax.broadcasted_iota(jnp.int32, (gc, cols), 0)
        dst = lax.broadcasted_iota(jnp.int32, (gc, cols), 1) & (gc - 1)
        spread = (src == dst).astype(BF16)
        y = jnp.dot(t_ref[...].astype(BF16), spread, preferred_element_type=F32)
        o_ref[...] = jnp.where(_diag_mask(gr, gc), y, 0.0).astype(BF16)

    return _call(
        body, after=after, name=name, grid=(2, nt),
        in_specs=[pl.BlockSpec((None, rows, gc), lambda p, i: (p, i, 0))],
        out_specs=pl.BlockSpec((None, rows, cols), lambda p, i: (p, i, i)),
        out_shape=jax.ShapeDtypeStruct((2, R, G * gc), BF16),
        compiler_params=_cparams(("parallel",) * 2),
    )(t2)


def _diag_extract(xd, gr, gc, *, name):
    _, R, _ = xd.shape
    nt = R // gr // _DIAG_TILE
    rows, cols = _DIAG_TILE * gr, _DIAG_TILE * gc

    def body(x_ref, o_ref):
        src = lax.broadcasted_iota(jnp.int32, (cols, gc), 0) & (gc - 1)
        dst = lax.broadcasted_iota(jnp.int32, (cols, gc), 1)
        fold = (src == dst).astype(BF16)
        parts = _split3(jnp.where(_diag_mask(gr, gc), x_ref[...], 0.0))
        acc = jnp.dot(parts[2], fold, preferred_element_type=F32)
        acc = acc + jnp.dot(parts[1], fold, preferred_element_type=F32)
        o_ref[...] = acc + jnp.dot(parts[0], fold, preferred_element_type=F32)

    return _call(
        body, name=name, grid=(2, nt),
        in_specs=[pl.BlockSpec((None, rows, cols), lambda p, i: (p, i, i))],
        out_specs=pl.BlockSpec((None, rows, gc), lambda p, i: (p, i, 0)),
        out_shape=jax.ShapeDtypeStruct((2, R, gc), F32),
        compiler_params=_cparams(("parallel",) * 2),
    )(xd)


SCAN_BLOCK = 8


def _cpowers(ar, ai, sign):
    ai = sign * ai
    out = [(ar, ai)]
    for _ in range(SCAN_BLOCK - 1):
        pr, pi = out[-1]
        out.append((pr * ar - pi * ai, pr * ai + pi * ar))
    return out


def _row_table(pw, row, index_of_row):
    tr_ = jnp.broadcast_to(pw[index_of_row(0)][0], row.shape)
    ti_ = jnp.broadcast_to(pw[index_of_row(0)][1], row.shape)
    for r in range(1, SCAN_BLOCK):
        pr, pi = pw[index_of_row(r)]
        tr_ = jnp.where(row == r, pr, tr_)
        ti_ = jnp.where(row == r, pi, ti_)
    return tr_, ti_


def _s5_scan_fwd(bu, a, *, name):
    _, S, N = bu.shape
    tc = 512
    nt = N // tc

    def body(a_ref, b_ref, h_ref):
        pw = _cpowers(a_ref[0], a_ref[1], 1.0)
        row = lax.broadcasted_iota(jnp.int32, (SCAN_BLOCK, tc), 0)
        lead_r, lead_i = _row_table(pw, row, lambda r: r)

        def step(k, carry):
            cr, ci = carry
            rows = pl.ds(pl.multiple_of(k * SCAN_BLOCK, SCAN_BLOCK), SCAN_BLOCK)
            xr, xi = b_ref[0, rows, :], b_ref[1, rows, :]
            for sh in (1, 2, 4):
                keep = row >= sh
                sr = jnp.where(keep, pltpu.roll(xr, sh, 0), 0.0)
                si = jnp.where(keep, pltpu.roll(xi, sh, 0), 0.0)
                kr, ki = pw[sh - 1]
                xr, xi = xr + kr * sr - ki * si, xi + kr * si + ki * sr
            h_ref[0, rows, :] = xr + lead_r * cr - lead_i * ci
            h_ref[1, rows, :] = xi + lead_r * ci + lead_i * cr
            last = row == SCAN_BLOCK - 1
            tr_ = jnp.sum(jnp.where(last, xr, 0.0), axis=0, keepdims=True)
            ti_ = jnp.sum(jnp.where(last, xi, 0.0), axis=0, keepdims=True)
            a8r, a8i = pw[SCAN_BLOCK - 1]
            return a8r * cr - a8i * ci + tr_, a8r * ci + a8i * cr + ti_

        z = jnp.zeros((1, tc), F32)
        lax.fori_loop(0, S // SCAN_BLOCK, step, (z, z), unroll=2)

    vec = pl.BlockSpec((2, 1, tc), lambda j: (0, 0, j))
    mat = pl.BlockSpec((2, S, tc), lambda j: (0, 0, j))
    return _call(
        body, name=name, grid=(nt,), in_specs=[vec, mat], out_specs=mat,
        out_shape=jax.ShapeDtypeStruct((2, S, N), F32),
        compiler_params=_cparams(("parallel",)),
    )(a, bu)


def _s5_scan_bwd(g, h, a, *, name):
    _, S, N = g.shape
    tc = 256
    nt = N // tc

    def body(a_ref, g_ref, h_ref, l_ref, da_ref):
        pw = _cpowers(a_ref[0], a_ref[1], -1.0)
        row = lax.broadcasted_iota(jnp.int32, (SCAN_BLOCK, tc), 0)
        tail_r, tail_i = _row_table(pw, row, lambda r: SCAN_BLOCK - 1 - r)
        nb = S // SCAN_BLOCK

        def step(i, carry):
            k = nb - 1 - i
            cr, ci, dar, dai = carry
            rows = pl.ds(pl.multiple_of(k * SCAN_BLOCK, SCAN_BLOCK), SCAN_BLOCK)
            xr, xi = g_ref[0, rows, :], g_ref[1, rows, :]
            for sh in (1, 2, 4):
                keep = row < SCAN_BLOCK - sh
                sr = jnp.where(keep, pltpu.roll(xr, SCAN_BLOCK - sh, 0), 0.0)
                si = jnp.where(keep, pltpu.roll(xi, SCAN_BLOCK - sh, 0), 0.0)
                kr, ki = pw[sh - 1]
                xr, xi = xr + kr * sr - ki * si, xi + kr * si + ki * sr
            lr = xr + tail_r * cr - tail_i * ci
            li = xi + tail_r * ci + tail_i * cr
            l_ref[0, rows, :] = lr
            l_ref[1, rows, :] = li
            prev = pl.ds(pl.multiple_of(jnp.maximum(k - 1, 0) * SCAN_BLOCK, SCAN_BLOCK), SCAN_BLOCK)
            has_prev = jnp.where(k > 0, 1.0, 0.0).astype(F32)
            first = row == 0
            hpr = jnp.where(first, pltpu.roll(h_ref[0, prev, :], 1, 0) * has_prev, pltpu.roll(h_ref[0, rows, :], 1, 0))
            hpi = jnp.where(first, pltpu.roll(h_ref[1, prev, :], 1, 0) * has_prev, pltpu.roll(h_ref[1, rows, :], 1, 0))
            tr_ = jnp.sum(jnp.where(first, xr, 0.0), axis=0, keepdims=True)
            ti_ = jnp.sum(jnp.where(first, xi, 0.0), axis=0, keepdims=True)
            a8r, a8i = pw[SCAN_BLOCK - 1]
            return (a8r * cr - a8i * ci + tr_, a8r * ci + a8i * cr + ti_,
                    dar + lr * hpr + li * hpi, dai + li * hpr - lr * hpi)

        z = jnp.zeros((1, tc), F32)
        z8 = jnp.zeros((SCAN_BLOCK, tc), F32)
        _, _, dar, dai = lax.fori_loop(0, nb, step, (z, z, z8, z8), unroll=2)
        da_ref[0] = jnp.sum(dar, axis=0, keepdims=True)
        da_ref[1] = jnp.sum(dai, axis=0, keepdims=True)

    vec = pl.BlockSpec((2, 1, tc), lambda j: (0, 0, j))
    mat = pl.BlockSpec((2, S, tc), lambda j: (0, 0, j))
    return _call(
        body, name=name, grid=(nt,), in_specs=[vec, mat, mat], out_specs=[mat, vec],
        out_shape=[jax.ShapeDtypeStruct((2, S, N), F32), jax.ShapeDtypeStruct((2, 1, N), F32)],
        compiler_params=_cparams(("parallel",)),
    )(a, g, h)


_GELU_C = math.sqrt(2.0 / math.pi)


def _s5_out_fwd(yc, P, dskip, *, name):
    S, W = yc.shape
    tr = _pick(S, 256)
    ub = 3 * FOX_WIDTH // W

    def body(yc_ref, u_ref, d_ref, y_ref, yg_ref):
        y = yc_ref[...] + d_ref[...] * u_ref[...]
        y_ref[...] = y
        t = jnp.tanh(_GELU_C * (y + 0.044715 * y * y * y))
        yg_ref[...] = (0.5 * y * (1.0 + t)).astype(BF16)

    row = pl.BlockSpec((tr, W), lambda i: (i, 0))
    return _call(
        body, name=name, grid=(S // tr,),
        in_specs=[row, pl.BlockSpec((tr, W), lambda i: (i, ub)), pl.BlockSpec((1, W), lambda i: (0, 0))],
        out_specs=[row, row],
        out_shape=[jax.ShapeDtypeStruct((S, W), F32), jax.ShapeDtypeStruct((S, W), BF16)],
        compiler_params=_cparams(("parallel",)),
    )(yc, P, dskip)


def _s5_out_bwd(dyg, y, P, dskip, *, name):
    S, W = y.shape
    tr = _pick(S, 256)
    ub = 3 * FOX_WIDTH // W

    def body(dyg_ref, y_ref, u_ref, d_ref, dy_ref, du_ref, dd_ref):
        y_ = y_ref[...]
        inner = _GELU_C * (y_ + 0.044715 * y_ * y_ * y_)
        t = jnp.tanh(inner)
        dgelu = 0.5 * (1.0 + t) + 0.5 * y_ * (1.0 - t * t) * _GELU_C * (1.0 + 3.0 * 0.044715 * y_ * y_)
        dy = dyg_ref[...] * dgelu
        dy_ref[...] = dy.astype(BF16)
        du_ref[...] = d_ref[...] * dy
        part = jnp.sum(dy * u_ref[...], axis=0, keepdims=True)

        @pl.when(pl.program_id(0) == 0)
        def _():
            dd_ref[...] = part

        @pl.when(pl.program_id(0) > 0)
        def _():
            dd_ref[...] += part

    row = pl.BlockSpec((tr, W), lambda i: (i, 0))
    vec = pl.BlockSpec((1, W), lambda i: (0, 0))
    return _call(
        body, name=name, grid=(S // tr,),
        in_specs=[row, row, pl.BlockSpec((tr, W), lambda i: (i, ub)), vec],
        out_specs=[row, row, vec],
        out_shape=[jax.ShapeDtypeStruct((S, W), BF16), jax.ShapeDtypeStruct((S, W), F32),
                   jax.ShapeDtypeStruct((1, W), F32)],
        compiler_params=_cparams(("arbitrary",)),
    )(dyg, y, P, dskip)


def _glu_fwd(z, *, name):
    S, W2 = z.shape
    W = W2 // 2
    tr = _pick(S, 256)

    def body(z1_ref, z2_ref, o_ref):
        o_ref[...] = (z1_ref[...] * jax.nn.sigmoid(z2_ref[...])).astype(BF16)

    return _call(
        body, name=name, grid=(S // tr,),
        in_specs=[pl.BlockSpec((tr, W), lambda i: (i, 0)), pl.BlockSpec((tr, W), lambda i: (i, 1))],
        out_specs=pl.BlockSpec((tr, W), lambda i: (i, 0)),
        out_shape=jax.ShapeDtypeStruct((S, W), BF16),
        compiler_params=_cparams(("parallel",)),
    )(z, z)


def _glu_bwd(z, dcat, *, name):
    S, W2 = z.shape
    W = W2 // 2
    tr = _pick(S, 256)

    def body(z1_ref, z2_ref, d_ref, dz1_ref, dz2_ref):
        sg = jax.nn.sigmoid(z2_ref[...])
        d = d_ref[...]
        dz1_ref[...] = (d * sg).astype(BF16)
        dz2_ref[...] = (d * z1_ref[...] * sg * (1.0 - sg)).astype(BF16)

    lo = pl.BlockSpec((tr, W), lambda i: (i, 0))
    hi = pl.BlockSpec((tr, W), lambda i: (i, 1))
    dz1, dz2 = _call(
        body, name=name, grid=(S // tr,), in_specs=[lo, hi, hi], out_specs=[lo, lo],
        out_shape=[jax.ShapeDtypeStruct((S, W), BF16)] * 2,
        compiler_params=_cparams(("parallel",)),
    )(z, z, dcat)
    return jnp.concatenate([dz1, dz2], axis=1)


ACT_ROWS = 16
ACT_COLS = 256


def _shift_down(cur, prev, k, row):
    return jnp.where(row >= k, pltpu.roll(cur, k, 0), pltpu.roll(prev, k, 0))


def _shift_up(cur, nxt, k, row):
    n = cur.shape[0]
    return jnp.where(row < n - k, pltpu.roll(cur, n - k, 0), pltpu.roll(nxt, n - k, 0))


def _act_fwd(h, cw, cb, *, name):
    _, S, FP = h.shape
    tr = _pick(S, 256)
    hb = tr // ACT_ROWS
    nq = tr // ACT_ROWS

    def body(g_ref, gh_ref, v_ref, vh_ref, wg_ref, wv_ref, bg_ref, bv_ref, a_ref):
        first = pl.program_id(1) == 0
        for c0 in range(0, FP, ACT_COLS):
            cw_ = min(ACT_COLS, FP - c0)
            cols = pl.ds(c0, cw_)
            rw = lax.broadcasted_iota(jnp.int32, (ACT_ROWS, cw_), 0)
            wg = [wg_ref[pl.ds(k, 1), cols] for k in range(3)]
            wv = [wv_ref[pl.ds(k, 1), cols] for k in range(3)]
            bg, bv = bg_ref[:, cols], bv_ref[:, cols]
            halo_g = jnp.where(first, 0.0, gh_ref[:, cols])
            halo_v = jnp.where(first, 0.0, vh_ref[:, cols])

            def chunk(q, _):
                rows = pl.ds(pl.multiple_of(q * ACT_ROWS, ACT_ROWS), ACT_ROWS)
                before = pl.ds(pl.multiple_of(jnp.maximum(q - 1, 0) * ACT_ROWS, ACT_ROWS), ACT_ROWS)
                g, v = g_ref[rows, cols], v_ref[rows, cols]
                gp = jnp.where(q > 0, g_ref[before, cols], halo_g)
                vp = jnp.where(q > 0, v_ref[before, cols], halo_v)
                cg = bg + wg[2] * g + wg[1] * _shift_down(g, gp, 1, rw) + wg[0] * _shift_down(g, gp, 2, rw)
                cv = bv + wv[2] * v + wv[1] * _shift_down(v, vp, 1, rw) + wv[0] * _shift_down(v, vp, 2, rw)
                a_ref[rows, cols] = (cg * jax.nn.sigmoid(cg) * cv).astype(BF16)
                return 0

            lax.fori_loop(0, nq, chunk, 0, unroll=2)

    def main(off):
        return pl.BlockSpec((None, tr, FP), lambda j, i: (j + off, i, 0))

    def halo(off):
        return pl.BlockSpec((None, ACT_ROWS, FP), lambda j, i: (j + off, jnp.maximum(i * hb - 1, 0), 0))

    def wspec(off):
        return pl.BlockSpec((None, 3, FP), lambda j, i: (j + off, 0, 0))

    def bspec(off):
        return pl.BlockSpec((None, 1, FP), lambda j, i: (j + off, 0, 0))

    cb3 = cb.reshape(4, 1, FP)
    return _call(
        body, name=name, grid=(2, S // tr),
        in_specs=[main(0), halo(0), main(2), halo(2), wspec(0), wspec(2), bspec(0), bspec(2)],
        out_specs=pl.BlockSpec((None, tr, FP), lambda j, i: (j, i, 0)),
        out_shape=jax.ShapeDtypeStruct((2, S, FP), BF16),
        compiler_params=_cparams(("parallel", "parallel")),
    )(h, h, h, h, cw, cw, cb3, cb3)


def _act_bwd(h, da, cw, cb, *, name):
    _, S, FP = h.shape
    tr = _pick(S, 256)
    hb = tr // ACT_ROWS
    nq = tr // ACT_ROWS
    nr = S // tr
    half = ACT_ROWS // 2

    def fold(x):
        return x[:half] + x[half:]

    def body(g_ref, gp_ref, v_ref, vp_ref, da_ref, wg_ref, wv_ref, bg_ref, bv_ref,
             dh_ref, dwg_ref, dwv_ref, dbg_ref, dbv_ref, carry_g, carry_v):
        i = pl.program_id(1)
        bottom = i == 0
        top = i == nr - 1
        for c0 in range(0, FP, ACT_COLS):
            cw_ = min(ACT_COLS, FP - c0)
            cols = pl.ds(c0, cw_)
            rw = lax.broadcasted_iota(jnp.int32, (ACT_ROWS, cw_), 0)
            wg = [wg_ref[pl.ds(k, 1), cols] for k in range(3)]
            wv = [wv_ref[pl.ds(k, 1), cols] for k in range(3)]
            bg, bv = bg_ref[:, cols], bv_ref[:, cols]
            halo_g = jnp.where(top, 0.0, gp_ref[:, cols])
            halo_v = jnp.where(top, 0.0, vp_ref[:, cols])
            after_g = jnp.where(bottom, 0.0, carry_g[:, cols])
            after_v = jnp.where(bottom, 0.0, carry_v[:, cols])

            def chunk(s, carry):
                ng, nv, acc = carry[0], carry[1], carry[2:]
                q = nq - 1 - s
                rows = pl.ds(pl.multiple_of(q * ACT_ROWS, ACT_ROWS), ACT_ROWS)
                before = pl.ds(pl.multiple_of(jnp.maximum(q - 1, 0) * ACT_ROWS, ACT_ROWS), ACT_ROWS)
                g, v = g_ref[rows, cols], v_ref[rows, cols]
                gp = jnp.where(q > 0, g_ref[before, cols], halo_g)
                vp = jnp.where(q > 0, v_ref[before, cols], halo_v)
                g1, g2 = _shift_down(g, gp, 1, rw), _shift_down(g, gp, 2, rw)
                v1, v2 = _shift_down(v, vp, 1, rw), _shift_down(v, vp, 2, rw)
                cg = bg + wg[2] * g + wg[1] * g1 + wg[0] * g2
                cv = bv + wv[2] * v + wv[1] * v1 + wv[0] * v2
                sg = jax.nn.sigmoid(cg)
                d = da_ref[rows, cols]
                dcg = d * cv * sg * (1.0 + cg * (1.0 - sg))
                dcv = d * cg * sg
                dh_ref[0, rows, cols] = (wg[2] * dcg + wg[1] * _shift_up(dcg, ng, 1, rw)
                                         + wg[0] * _shift_up(dcg, ng, 2, rw)).astype(BF16)
                dh_ref[1, rows, cols] = (wv[2] * dcv + wv[1] * _shift_up(dcv, nv, 1, rw)
                                         + wv[0] * _shift_up(dcv, nv, 2, rw)).astype(BF16)
                terms = (dcg * g2, dcg * g1, dcg * g, dcg, dcv * v2, dcv * v1, dcv * v, dcv)
                return (dcg, dcv) + tuple(a + fold(t) for a, t in zip(acc, terms))

            zero = jnp.zeros((half, cw_), F32)
            out = lax.fori_loop(0, nq, chunk, (after_g, after_v) + (zero,) * 8, unroll=2)
            carry_g[:, cols] = out[0]
            carry_v[:, cols] = out[1]
            sums = [jnp.sum(a, axis=0, keepdims=True) for a in out[2:]]

            @pl.when(bottom)
            def _():
                for k in range(3):
                    dwg_ref[pl.ds(k, 1), cols] = sums[k]
                    dwv_ref[pl.ds(k, 1), cols] = sums[4 + k]
                dbg_ref[:, cols] = sums[3]
                dbv_ref[:, cols] = sums[7]

            @pl.when(jnp.logical_not(bottom))
            def _():
                for k in range(3):
                    dwg_ref[pl.ds(k, 1), cols] += sums[k]
                    dwv_ref[pl.ds(k, 1), cols] += sums[4 + k]
                dbg_ref[:, cols] += sums[3]
                dbv_ref[:, cols] += sums[7]

    def main(off):
        return pl.BlockSpec((None, tr, FP), lambda j, i: (j + off, nr - 1 - i, 0))

    def prev(off):
        return pl.BlockSpec((None, ACT_ROWS, FP), lambda j, i: (j + off, jnp.maximum((nr - 1 - i) * hb - 1, 0), 0))

    def wspec(off):
        return pl.BlockSpec((None, 3, FP), lambda j, i: (j + off, 0, 0))

    def bspec(off):
        return pl.BlockSpec((None, 1, FP), lambda j, i: (j + off, 0, 0))

    cb3 = cb.reshape(4, 1, FP)
    dh, dwg, dwv, dbg, dbv = _call(
        body, name=name, grid=(2, nr),
        in_specs=[main(0), prev(0), main(2), prev(2), main(0), wspec(0), wspec(2), bspec(0), bspec(2)],
        out_specs=[pl.BlockSpec((None, 2, tr, FP), lambda j, i: (j, 0, nr - 1 - i, 0)),
                   wspec(0), wspec(0), bspec(0), bspec(0)],
        out_shape=[jax.ShapeDtypeStruct((2, 2, S, FP), BF16)]
        + [jax.ShapeDtypeStruct((2, 3, FP), F32)] * 2 + [jax.ShapeDtypeStruct((2, 1, FP), F32)] * 2,
        scratch_shapes=[pltpu.VMEM((ACT_ROWS, FP), F32), pltpu.VMEM((ACT_ROWS, FP), F32)],
        compiler_params=_cparams(("parallel", "arbitrary")),
    )(h, h, h, h, da, cw, cw, cb3, cb3)
    return (dh.reshape(4, S, FP), jnp.concatenate([dwg, dwv], axis=0), jnp.concatenate([dbg, dbv], axis=0))


def _rope_tables(posf, *, name, after=()):
    S = posf.shape[0]
    half = ROPE_DIM // 2
    d = np.arange(LANE) % SWA_HEAD_DIM
    invf = np.where(d < ROPE_DIM, ROPE_THETA ** (-(d % half).astype(np.float64) / half), 0.0).astype(np.float32)
    m_rot = (d < ROPE_DIM).astype(np.float32)
    m_a = (d < half).astype(np.float32)
    m_b = ((d >= half) & (d < ROPE_DIM)).astype(np.float32)
    consts = jnp.asarray(np.stack([invf, m_rot, m_a, m_b] + [np.zeros(LANE, np.float32)] * 4))

    def body(p_ref, k_ref, c_ref, sa_ref, sb_ref):
        k = k_ref[...]
        ang = p_ref[...] * k[0:1]
        co, si = jnp.cos(ang), jnp.sin(ang)
        c_ref[...] = k[1:2] * co + (1.0 - k[1:2])
        sa_ref[...] = -k[2:3] * si
        sb_ref[...] = k[3:4] * si

    full = pl.BlockSpec((S, LANE), lambda: (0, 0))
    return _call(
        body, after=after, name=name,
        in_specs=[pl.BlockSpec((S, 1), lambda: (0, 0)), pl.BlockSpec((8, LANE), lambda: (0, 0))],
        out_specs=[full] * 3, out_shape=[jax.ShapeDtypeStruct((S, LANE), F32)] * 3,
    )(posf, consts)


def _rope_apply(x, tabs, *, col0, width, inverse, name, out_dtype):
    S = x.shape[0]
    tr = _pick(S, 256)
    rep = width // LANE
    cb = col0 // width

    def body(x_ref, c_ref, sa_ref, sb_ref, o_ref):
        xv = x_ref[...].astype(F32)
        c = jnp.tile(c_ref[...], (1, rep))
        sa = jnp.tile(sa_ref[...], (1, rep))
        sb = jnp.tile(sb_ref[...], (1, rep))
        if not inverse:
            out = xv * c + pltpu.roll(xv, width - 8, 1) * sa + pltpu.roll(xv, 8, 1) * sb
        else:
            out = xv * c + pltpu.roll(xv * sa, 8, 1) + pltpu.roll(xv * sb, width - 8, 1)
        o_ref[...] = out.astype(out_dtype)

    tab = pl.BlockSpec((tr, LANE), lambda i: (i, 0))
    return _call(
        body, name=name, grid=(S // tr,),
        in_specs=[pl.BlockSpec((tr, width), lambda i: (i, cb)), tab, tab, tab],
        out_specs=pl.BlockSpec((tr, width), lambda i: (i, 0)),
        out_shape=jax.ShapeDtypeStruct((S, width), out_dtype),
        compiler_params=_cparams(("parallel",)),
    )(x, *tabs)


def _swa_mask(n):
    rows = SWA_GROUPS * SWA_WINDOW
    qi = lax.broadcasted_iota(jnp.int32, (rows, 2 * SWA_WINDOW), 0) & (SWA_WINDOW - 1)
    kj = lax.broadcasted_iota(jnp.int32, (rows, 2 * SWA_WINDOW), 1)
    rel = SWA_WINDOW + qi - kj
    return (rel >= 0) & (rel < SWA_WINDOW) & ((n > 0) | (kj >= SWA_WINDOW))


def _swa_fwd(qT, kT, vT, sink_rows, *, name):
    S = qT.shape[1]
    W, G, Dh = SWA_WINDOW, SWA_GROUPS, SWA_HEAD_DIM
    nb = S // W
    scale = 1.0 / math.sqrt(Dh)

    def body(q_ref, kp_ref, kc_ref, vp_ref, vc_ref, s_ref, o_ref, l_ref):
        n = pl.program_id(1)
        q = q_ref[...].reshape(G * W, Dh)
        kk = jnp.concatenate([kp_ref[...], kc_ref[...]], axis=0)
        vv = jnp.concatenate([vp_ref[...], vc_ref[...]], axis=0)
        s = lax.dot_general(q, kk, (((1,), (1,)), ((), ())), preferred_element_type=F32) * scale
        s = jnp.where(_swa_mask(n), s, -1e30)
        sink = s_ref[...]
        m = jnp.maximum(jnp.max(s, axis=-1, keepdims=True), sink)
        e = jnp.exp(s - m)
        den = jnp.sum(e, axis=-1, keepdims=True) + jnp.exp(sink - m)
        p = e / den
        o_ref[...] = jnp.dot(p.astype(BF16), vv, preferred_element_type=F32).reshape(G, W, Dh)
        l_ref[...] = (m + jnp.log(den)).reshape(G, W, 1)

    qs = pl.BlockSpec((G, W, Dh), lambda g, n: (g, n, 0))
    prev = pl.BlockSpec((None, W, Dh), lambda g, n: (g, jnp.maximum(n - 1, 0), 0))
    cur = pl.BlockSpec((None, W, Dh), lambda g, n: (g, n, 0))
    return _call(
        body, name=name, grid=(SWA_KV_HEADS, nb),
        in_specs=[qs, prev, cur, prev, cur, pl.BlockSpec((None, G * W, 1), lambda g, n: (g, 0, 0))],
        out_specs=[qs, pl.BlockSpec((G, W, 1), lambda g, n: (g, n, 0))],
        out_shape=[jax.ShapeDtypeStruct((SWA_HEADS, S, Dh), F32), jax.ShapeDtypeStruct((SWA_HEADS, S, 1), F32)],
        compiler_params=_cparams(("parallel", "parallel")),
    )(qT, kT, kT, vT, vT, sink_rows)


def _swa_bwd(qT, kT, vT, sink_rows, oT, L, doT, *, name):
    S = qT.shape[1]
    W, G, Dh = SWA_WINDOW, SWA_GROUPS, SWA_HEAD_DIM
    nb = S // W
    scale = 1.0 / math.sqrt(Dh)

    def body(q_ref, kp_ref, kc_ref, vp_ref, vc_ref, s_ref, o_ref, l_ref, do_ref,
             dq_ref, dk_ref, dv_ref, ds_ref):
        n = pl.program_id(1)
        q = q_ref[...].reshape(G * W, Dh)
        kk = jnp.concatenate([kp_ref[...], kc_ref[...]], axis=0)
        vv = jnp.concatenate([vp_ref[...], vc_ref[...]], axis=0)
        s = lax.dot_general(q, kk, (((1,), (1,)), ((), ())), preferred_element_type=F32) * scale
        lrow = l_ref[...].reshape(G * W, 1)
        p = jnp.where(_swa_mask(n), jnp.exp(s - lrow), 0.0)
        do = do_ref[...].reshape(G * W, Dh)
        do_bf = do.astype(BF16)
        dp = lax.dot_general(do_bf, vv, (((1,), (1,)), ((), ())), preferred_element_type=F32)
        delta = jnp.sum(do * o_ref[...].reshape(G * W, Dh), axis=-1, keepdims=True)
        dsc = p * (dp - delta)
        ds_bf = dsc.astype(BF16)
        dq_ref[...] = (jnp.dot(ds_bf, kk, preferred_element_type=F32) * scale).astype(BF16).reshape(G, W, Dh)
        dkk = lax.dot_general(ds_bf, q, (((0,), (0,)), ((), ())), preferred_element_type=F32) * scale
        dvv = lax.dot_general(p.astype(BF16), do_bf, (((0,), (0,)), ((), ())), preferred_element_type=F32)
        dsk = -jnp.exp(s_ref[...] - lrow) * delta
        dsk = jnp.broadcast_to(jnp.sum(dsk.reshape(G, W, 1), axis=1), (G, LANE))

        @pl.when(n == 0)
        def _():
            dk_ref[...] = jnp.zeros_like(dk_ref)
            dv_ref[...] = jnp.zeros_like(dv_ref)
            ds_ref[...] = jnp.zeros_like(ds_ref)

        rows = pl.ds(pl.multiple_of(n * W, W), 2 * W)
        dk_ref[rows, :] += dkk
        dv_ref[rows, :] += dvv
        ds_ref[...] += dsk

    qs = pl.BlockSpec((G, W, Dh), lambda g, n: (g, n, 0))
    prev = pl.BlockSpec((None, W, Dh), lambda g, n: (g, jnp.maximum(n - 1, 0), 0))
    cur = pl.BlockSpec((None, W, Dh), lambda g, n: (g, n, 0))
    lsp = pl.BlockSpec((G, W, 1), lambda g, n: (g, n, 0))
    kvo = pl.BlockSpec((None, S + W, Dh), lambda g, n: (g, 0, 0))
    return _call(
        body, name=name, grid=(SWA_KV_HEADS, nb),
        in_specs=[qs, prev, cur, prev, cur, pl.BlockSpec((None, G * W, 1), lambda g, n: (g, 0, 0)), qs, lsp, qs],
        out_specs=[qs, kvo, kvo, pl.BlockSpec((None, G, LANE), lambda g, n: (g, 0, 0))],
        out_shape=[jax.ShapeDtypeStruct((SWA_HEADS, S, Dh), BF16),
                   jax.ShapeDtypeStruct((SWA_KV_HEADS, S + W, Dh), F32),
                   jax.ShapeDtypeStruct((SWA_KV_HEADS, S + W, Dh), F32),
                   jax.ShapeDtypeStruct((SWA_KV_HEADS, G, LANE), F32)],
        compiler_params=_cparams(("parallel", "arbitrary")),
    )(qT, kT, kT, vT, vT, sink_rows, oT, L, doT)


def _adamw(w, g, m, v, *, name, tr=128, by_cols=False):
    L, R, C = w.shape
    split = isinstance(g, (list, tuple))
    HR, HC = _half_shape(R, C, by_cols) if split else (R, C)
    tr, tc = _tile2d(HR, HC, tr)
    nr, nc = HR // tr, HC // tc
    c1 = 1.0 / (1.0 - ADAM_B1 ** ADAM_STEP)
    c2 = 1.0 / (1.0 - ADAM_B2 ** ADAM_STEP)
    ng = 2 * L if split else 1

    def body(c_ref, *refs):
        w_ref, g_refs, (m_ref, v_ref, go_ref, d_ref, mo_ref, vo_ref) = refs[0], refs[1:1 + ng], refs[1 + ng:]
        if split:
            mine = pl.program_id(1) == c_ref[0]
            g_ = jnp.where(mine, g_refs[0][...], g_refs[1][...])
            for l in range(1, L):
                g_ = jnp.where(pl.program_id(0) == l,
                               jnp.where(mine, g_refs[2 * l][...], g_refs[2 * l + 1][...]), g_)
        else:
            g_ = g_refs[0][...]
        mn = ADAM_B1 * m_ref[...] + (1.0 - ADAM_B1) * g_
        vn = ADAM_B2 * v_ref[...] + (1.0 - ADAM_B2) * (g_ * g_)
        go_ref[...] = g_
        mo_ref[...] = mn
        vo_ref[...] = vn
        d_ref[...] = -ADAM_LR * ((mn * c1) / (jnp.sqrt(vn * c2) + ADAM_EPS) + ADAM_WD * w_ref[...])

    def whole(l, hf, i, j, c):
        return (l, i, hf * nc + j) if by_cols else (l, hf * nr + i, j)

    def half(layer, own):
        def index(l, hf, i, j, c):
            used = (l == layer) & ((hf == c[0]) if own else (hf != c[0]))
            return jnp.where(used, i, 0), jnp.where(used, j, 0)
        return pl.BlockSpec((tr, tc), index)

    row = pl.BlockSpec((None, tr, tc), whole)
    gs = [h for pair in g for h in pair] if split else [g]
    g_specs = [half(l, own) for l in range(L) for own in (True, False)] if split else [row]
    core = lax.axis_index("c").astype(jnp.int32).reshape(1)
    return _call(
        body, name=name,
        grid_spec=pltpu.PrefetchScalarGridSpec(
            num_scalar_prefetch=1, grid=(L, 2 if split else 1, nr, nc),
            in_specs=[row] + g_specs + [row, row], out_specs=[row] * 4),
        out_shape=[jax.ShapeDtypeStruct((L, R, C), F32)] * 4,
        compiler_params=_cparams(("parallel",) * 4),
    )(core, w, *gs, m, v)


def _adamw_half(w, g, m, v, *, name, own, prev=None, tr=128, by_cols=False):
    L, R, C = w.shape
    HR, HC = _half_shape(R, C, by_cols)
    tr, tc = _tile2d(HR, HC, tr)
    nr, nc = HR // tr, HC // tc
    c1 = 1.0 / (1.0 - ADAM_B1 ** ADAM_STEP)
    c2 = 1.0 / (1.0 - ADAM_B2 ** ADAM_STEP)

    def body(c_ref, *refs):
        w_ref, g_refs, m_ref, v_ref = refs[0], refs[1:1 + L], refs[1 + L], refs[2 + L]
        go_ref, d_ref, mo_ref, vo_ref = refs[-4:]
        g_ = g_refs[0][...]
        for l in range(1, L):
            g_ = jnp.where(pl.program_id(0) == l, g_refs[l][...], g_)
        mn = ADAM_B1 * m_ref[...] + (1.0 - ADAM_B1) * g_
        vn = ADAM_B2 * v_ref[...] + (1.0 - ADAM_B2) * (g_ * g_)
        go_ref[...] = g_
        mo_ref[...] = mn
        vo_ref[...] = vn
        d_ref[...] = -ADAM_LR * ((mn * c1) / (jnp.sqrt(vn * c2) + ADAM_EPS) + ADAM_WD * w_ref[...])

    def whole(l, i, j, c):
        hf = c[0] if own else 1 - c[0]
        return (l, i, hf * nc + j) if by_cols else (l, hf * nr + i, j)

    def layer_half(layer):
        def index(l, i, j, c):
            return jnp.where(l == layer, i, 0), jnp.where(l == layer, j, 0)
        return pl.BlockSpec((tr, tc), index)

    row = pl.BlockSpec((None, tr, tc), whole)
    core = lax.axis_index("c").astype(jnp.int32).reshape(1)
    prev = list(prev) if prev is not None else []
    return _call(
        body, name=name,
        grid_spec=pltpu.PrefetchScalarGridSpec(
            num_scalar_prefetch=1, grid=(L, nr, nc),
            in_specs=[row] + [layer_half(l) for l in range(L)] + [row, row] + [ANY] * len(prev),
            out_specs=[row] * 4),
        out_shape=[jax.ShapeDtypeStruct((L, R, C), F32)] * 4,
        input_output_aliases={4 + L + k: k for k in range(len(prev))},
        compiler_params=_cparams(("parallel",) * 3),
    )(core, w, *g, m, v, *prev)


def _sum2_halves(g4, s4, by_cols, *, name):
    n, R, C = g4.shape
    HR, HC = _half_shape(R, C, by_cols)
    tr, tc = _tile2d(HR, HC, budget=1024 * 1024)
    nr, nc = HR // tr, HC // tc
    core = lax.axis_index("c").astype(jnp.int32).reshape(1)

    def body(c_ref, g_ref, s_ref, o_ref):
        o_ref[...] = (g_ref[...].astype(F32) + s_ref[...].astype(F32)).astype(BF16)

    def mine(k, i, j, c):
        return (k, i, c[0] * nc + j) if by_cols else (k, c[0] * nr + i, j)

    blk = pl.BlockSpec((None, tr, tc), lambda k, i, j, c: (k, i, j))
    return _call(
        body, name=name,
        grid_spec=pltpu.PrefetchScalarGridSpec(
            num_scalar_prefetch=1, grid=(n, nr, nc),
            in_specs=[pl.BlockSpec((None, tr, tc), mine), blk], out_specs=blk),
        out_shape=jax.ShapeDtypeStruct((n, HR, HC), BF16),
        compiler_params=_cparams(("parallel", "parallel", "parallel")),
    )(core, g4, s4)


def _rowsum(parts, *, name, out_dtype=F32):
    n, R, C = parts.shape
    tr, tc = _tile2d(R, C, budget=512 * 1024)

    def body(p_ref, o_ref):
        acc = p_ref[0].astype(F32)
        for i in range(1, n):
            acc = acc + p_ref[i].astype(F32)
        o_ref[...] = acc.astype(out_dtype)

    return _call(
        body, name=name, grid=(R // tr, C // tc),
        in_specs=[pl.BlockSpec((n, tr, tc), lambda i, j: (0, i, j))],
        out_specs=pl.BlockSpec((tr, tc), lambda i, j: (i, j)),
        out_shape=jax.ShapeDtypeStruct((R, C), out_dtype),
        compiler_params=_cparams(("parallel", "parallel")),
    )(parts)


def _where_am_i():
    x, y, c = lax.axis_index("x"), lax.axis_index("y"), lax.axis_index("c")
    chips = [(1 - x, y), (x, 1 - y), (1 - x, 1 - y)]
    return x, y, c, chips


def _half_idx(rows, cols, by_cols, which):
    if by_cols:
        hc = cols // 2
        return (slice(None), pl.ds(pl.multiple_of(which * hc, LANE), hc))
    hr = rows // 2
    return (pl.ds(pl.multiple_of(which * hr, 16), hr), slice(None))


def _half_shape(rows, cols, by_cols):
    return (rows, cols // 2) if by_cols else (rows // 2, cols)


def _all_gather_shards(shards, by_cols, *, name):
    n = len(shards)

    def body(*refs):
        ins, outs = refs[:n], refs[n:2 * n]
        send, recv = refs[2 * n:]
        x, y, c, chips = _where_am_i()
        me = 2 * x + y
        sibling = (x, y, 1 - c)

        def half(i, which):
            return _half_idx(*shards[i].shape, by_cols[i], which)

        def cp(i, k, src, dst, to):
            return pltpu.make_async_remote_copy(src_ref=src, dst_ref=dst, send_sem=send.at[i, k],
                                                recv_sem=recv.at[i, k], device_id=to, device_id_type=MESH)

        first = []
        for i in range(n):
            for k, (px, py) in enumerate(chips):
                d = cp(i, k, ins[i].at[half(i, c)], outs[i].at[(me,) + half(i, c)], (px, py, c))
                d.start()
                first.append(d)
        passed = []
        for i in range(n):
            for k, (px, py) in enumerate(chips):
                blk = outs[i].at[(2 * px + py,) + half(i, c)]
                cp(i, k, blk, blk, (px, py, c)).wait_recv()
                d = cp(i, 3 + k, blk, blk, sibling)
                d.start()
                passed.append(d)
        for i in range(n):
            for k, (px, py) in enumerate(chips):
                blk = outs[i].at[(2 * px + py,) + half(i, 1 - c)]
                cp(i, 3 + k, blk, blk, sibling).wait_recv()
        for d in first + passed:
            d.wait_send()

    got = _call(
        body, name=name, in_specs=[ANY] * n, out_specs=[ANY] * n,
        out_shape=[jax.ShapeDtypeStruct((N_CHIPS,) + s.shape, s.dtype) for s in shards],
        scratch_shapes=[pltpu.SemaphoreType.DMA((n, 6)), pltpu.SemaphoreType.DMA((n, 6))],
    )(*shards)
    me = 2 * lax.axis_index("x") + lax.axis_index("y")
    return [lax.dynamic_update_slice_in_dim(g, s[None], me, axis=0) for g, s in zip(got, shards)]


HBM_SPEC = pl.BlockSpec(memory_space=pltpu.HBM)
SEM_SPEC = pl.BlockSpec(memory_space=pltpu.SEMAPHORE)
DATAFLOW = pltpu.SideEffectType.DATAFLOW_SIDE_EFFECTING


def _chip_exchange_refs(kind, shards_shape, by_cols, src, land, i, chip_k, c, me):
    if kind == 'gather':
        half = _half_idx(*shards_shape, by_cols, c)
        return src.at[half], land.at[(me,) + half], land.at[(chip_k,) + half]
    return src.at[chip_k], land.at[me], land.at[chip_k]


def _chip_exchange_start(kind, srcs, by_cols, *, name, after=()):
    n = len(srcs)
    land_shapes = [((N_CHIPS,) + s.shape) if kind == 'gather' else s.shape for s in srcs]

    def body(*refs):
        src_refs, land_refs = refs[:n], refs[n:2 * n]
        send, recv = refs[2 * n + len(after)], refs[2 * n + len(after) + 1]
        token = refs[-1]
        x, y, c, chips = _where_am_i()
        me = 2 * x + y
        for i in range(n):
            for k, (px, py) in enumerate(chips):
                s, d, _ = _chip_exchange_refs(kind, srcs[i].shape, by_cols[i], src_refs[i], land_refs[i], i,
                                              2 * px + py, c, me)
                pltpu.make_async_remote_copy(src_ref=s, dst_ref=d, send_sem=send.at[3 * i + k],
                                             recv_sem=recv.at[3 * i + k], device_id=(px, py, c),
                                             device_id_type=MESH).start()
        token[...] = jnp.zeros_like(token)

    lands = [pltpu.with_memory_space_constraint(lax.empty(sh, s.dtype), pltpu.HBM) for sh, s in zip(land_shapes, srcs)]
    outs = _call(
        body, name=name,
        out_shape=(pltpu.SemaphoreType.DMA((3 * n,)), pltpu.SemaphoreType.DMA((3 * n,)),
                   *[pltpu.HBM(s.shape, s.dtype) for s in srcs],
                   *[pltpu.HBM(sh, s.dtype) for sh, s in zip(land_shapes, srcs)],
                   jax.ShapeDtypeStruct((8, LANE), F32)),
        in_specs=[HBM_SPEC] * (2 * n) + [ANY] * len(after),
        out_specs=(SEM_SPEC, SEM_SPEC, *([HBM_SPEC] * (2 * n)), pl.BlockSpec(memory_space=pltpu.VMEM)),
        input_output_aliases={j: 2 + j for j in range(2 * n)},
        compiler_params=pltpu.CompilerParams(has_side_effects=DATAFLOW),
    )(*[pltpu.with_memory_space_constraint(s, pltpu.HBM) for s in srcs], *lands, *after)
    return outs[0], outs[1], list(outs[2:2 + n]), list(outs[2 + n:2 + 2 * n]), outs[-1]


def _chip_exchange_wait(kind, send, recv, srcs, lands, by_cols, after, *, name):
    n = len(srcs)

    def body(*refs):
        src_refs, land_refs = refs[:n], refs[n:2 * n]
        send_r, recv_r = refs[2 * n], refs[2 * n + 1]
        x, y, c, chips = _where_am_i()
        me = 2 * x + y
        for i in range(n):
            for k, (px, py) in enumerate(chips):
                s, _, d = _chip_exchange_refs(kind, srcs[i].shape, by_cols[i], src_refs[i], land_refs[i], i,
                                              2 * px + py, c, me)
                cp = pltpu.make_async_remote_copy(src_ref=s, dst_ref=d, send_sem=send_r.at[3 * i + k],
                                                  recv_sem=recv_r.at[3 * i + k], device_id=(px, py, c),
                                                  device_id_type=MESH)
                cp.wait_send()
                cp.wait_recv()

    outs = _call(
        body, name=name,
        out_shape=(*[pltpu.HBM(s.shape, s.dtype) for s in srcs], *[pltpu.HBM(l.shape, l.dtype) for l in lands]),
        in_specs=[HBM_SPEC] * (2 * n) + [SEM_SPEC, SEM_SPEC] + [ANY] * len(after),
        out_specs=tuple([HBM_SPEC] * (2 * n)),
        input_output_aliases={j: j for j in range(2 * n)},
        compiler_params=pltpu.CompilerParams(has_side_effects=DATAFLOW),
    )(*srcs, *lands, send, recv, *after)
    return list(outs[:n]), list(outs[n:])


def _sibling_halves_start(grads, by_cols, *, name, after=()):
    n = len(grads)
    land_shapes = [(N_CHIPS,) + _half_shape(*g.shape[1:], bc) for g, bc in zip(grads, by_cols)]

    def body(*refs):
        src_refs, land_refs = refs[:n], refs[n:2 * n]
        send, recv = refs[2 * n + len(after)], refs[2 * n + len(after) + 1]
        token = refs[-1]
        x, y, c, _ = _where_am_i()
        for i in range(n):
            src = src_refs[i].at[(slice(None),) + _half_idx(*grads[i].shape[1:], by_cols[i], 1 - c)]
            pltpu.make_async_remote_copy(src_ref=src, dst_ref=land_refs[i], send_sem=send.at[i], recv_sem=recv.at[i],
                                         device_id=(x, y, 1 - c), device_id_type=MESH).start()
        token[...] = jnp.zeros_like(token)

    lands = [pltpu.with_memory_space_constraint(lax.empty(sh, g.dtype), pltpu.HBM) for sh, g in zip(land_shapes, grads)]
    outs = _call(
        body, name=name,
        out_shape=(pltpu.SemaphoreType.DMA((n,)), pltpu.SemaphoreType.DMA((n,)),
                   *[pltpu.HBM(g.shape, g.dtype) for g in grads],
                   *[pltpu.HBM(sh, g.dtype) for sh, g in zip(land_shapes, grads)],
                   jax.ShapeDtypeStruct((8, LANE), F32)),
        in_specs=[HBM_SPEC] * (2 * n) + [ANY] * len(after),
        out_specs=(SEM_SPEC, SEM_SPEC, *([HBM_SPEC] * (2 * n)), pl.BlockSpec(memory_space=pltpu.VMEM)),
        input_output_aliases={j: 2 + j for j in range(2 * n)},
        compiler_params=pltpu.CompilerParams(has_side_effects=DATAFLOW),
    )(*[pltpu.with_memory_space_constraint(g, pltpu.HBM) for g in grads], *lands, *after)
    return outs[0], outs[1], list(outs[2:2 + n]), list(outs[2 + n:2 + 2 * n]), outs[-1]


def _sibling_halves_wait(send, recv, grads, lands, by_cols, after, *, name):
    n = len(grads)

    def body(*refs):
        src_refs, land_refs = refs[:n], refs[n:2 * n]
        send_r, recv_r = refs[2 * n], refs[2 * n + 1]
        x, y, c, _ = _where_am_i()
        for i in range(n):
            src = src_refs[i].at[(slice(None),) + _half_idx(*grads[i].shape[1:], by_cols[i], 1 - c)]
            cp = pltpu.make_async_remote_copy(src_ref=src, dst_ref=land_refs[i], send_sem=send_r.at[i],
                                              recv_sem=recv_r.at[i], device_id=(x, y, 1 - c), device_id_type=MESH)
            cp.wait_send()
            cp.wait_recv()

    outs = _call(
        body, name=name,
        out_shape=(*[pltpu.HBM(g.shape, g.dtype) for g in grads], *[pltpu.HBM(l.shape, l.dtype) for l in lands]),
        in_specs=[HBM_SPEC] * (2 * n) + [SEM_SPEC, SEM_SPEC] + [ANY] * len(after),
        out_specs=tuple([HBM_SPEC] * (2 * n)),
        input_output_aliases={j: j for j in range(2 * n)},
        compiler_params=pltpu.CompilerParams(has_side_effects=DATAFLOW),
    )(*grads, *lands, send, recv, *after)
    return list(outs[:n]), list(outs[n:])


def _sibling_swap_start(arrs, *, name, after=()):
    n = len(arrs)

    def body(*refs):
        src_refs, land_refs = refs[:n], refs[n:2 * n]
        send, recv = refs[2 * n + len(after)], refs[2 * n + len(after) + 1]
        token = refs[-1]
        x, y, c, _ = _where_am_i()
        for i in range(n):
            pltpu.make_async_remote_copy(src_ref=src_refs[i], dst_ref=land_refs[i], send_sem=send.at[i],
                                         recv_sem=recv.at[i], device_id=(x, y, 1 - c), device_id_type=MESH).start()
        token[...] = jnp.zeros_like(token)

    lands = [pltpu.with_memory_space_constraint(lax.empty(a.shape, a.dtype), pltpu.HBM) for a in arrs]
    outs = _call(
        body, name=name,
        out_shape=(pltpu.SemaphoreType.DMA((n,)), pltpu.SemaphoreType.DMA((n,)),
                   *[pltpu.HBM(a.shape, a.dtype) for a in arrs] * 2, jax.ShapeDtypeStruct((8, LANE), F32)),
        in_specs=[HBM_SPEC] * (2 * n) + [ANY] * len(after),
        out_specs=(SEM_SPEC, SEM_SPEC, *([HBM_SPEC] * (2 * n)), pl.BlockSpec(memory_space=pltpu.VMEM)),
        input_output_aliases={j: 2 + j for j in range(2 * n)},
        compiler_params=pltpu.CompilerParams(has_side_effects=DATAFLOW),
    )(*[pltpu.with_memory_space_constraint(a, pltpu.HBM) for a in arrs], *lands, *after)
    return outs[0], outs[1], list(outs[2:2 + n]), list(outs[2 + n:2 + 2 * n]), outs[-1]


def _sibling_swap_wait(send, recv, arrs, lands, after, *, name):
    n = len(arrs)

    def body(*refs):
        src_refs, land_refs = refs[:n], refs[n:2 * n]
        send_r, recv_r = refs[2 * n], refs[2 * n + 1]
        x, y, c, _ = _where_am_i()
        for i in range(n):
            cp = pltpu.make_async_remote_copy(src_ref=src_refs[i], dst_ref=land_refs[i], send_sem=send_r.at[i],
                                              recv_sem=recv_r.at[i], device_id=(x, y, 1 - c), device_id_type=MESH)
            cp.wait_send()
            cp.wait_recv()

    outs = _call(
        body, name=name,
        out_shape=tuple(pltpu.HBM(a.shape, a.dtype) for a in list(arrs) + list(lands)),
        in_specs=[HBM_SPEC] * (2 * n) + [SEM_SPEC, SEM_SPEC] + [ANY] * len(after),
        out_specs=tuple([HBM_SPEC] * (2 * n)),
        input_output_aliases={j: j for j in range(2 * n)},
        compiler_params=pltpu.CompilerParams(has_side_effects=DATAFLOW),
    )(*arrs, *lands, send, recv, *after)
    return list(outs[:n]), list(outs[n:])


def _sibling_pass_gathered(lands, shard_shapes, by_cols, *, name):
    n = len(lands)

    def body(*refs):
        outs = refs[n:2 * n]
        send, recv = refs[2 * n:]
        x, y, c, chips = _where_am_i()
        sibling = (x, y, 1 - c)
        cps = []
        for i in range(n):
            for k, (px, py) in enumerate(chips):
                blk = outs[i].at[(2 * px + py,) + _half_idx(*shard_shapes[i], by_cols[i], c)]
                d = pltpu.make_async_remote_copy(src_ref=blk, dst_ref=blk, send_sem=send.at[i, k],
                                                 recv_sem=recv.at[i, k], device_id=sibling, device_id_type=MESH)
                d.start()
                cps.append(d)
        for i in range(n):
            for k, (px, py) in enumerate(chips):
                blk = outs[i].at[(2 * px + py,) + _half_idx(*shard_shapes[i], by_cols[i], 1 - c)]
                pltpu.make_async_remote_copy(src_ref=blk, dst_ref=blk, send_sem=send.at[i, k], recv_sem=recv.at[i, k],
                                             device_id=sibling, device_id_type=MESH).wait_recv()
        for d in cps:
            d.wait_send()

    return _call(
        body, name=name, in_specs=[ANY] * n, out_specs=[ANY] * n,
        out_shape=[jax.ShapeDtypeStruct(l.shape, l.dtype) for l in lands],
        input_output_aliases={j: j for j in range(n)},
        scratch_shapes=[pltpu.SemaphoreType.DMA((n, 3)), pltpu.SemaphoreType.DMA((n, 3))],
    )(*lands)


def _own_slot(lands, owns):
    me = 2 * lax.axis_index("x") + lax.axis_index("y")
    return [lax.dynamic_update_slice_in_dim(g, s, me, axis=0) for g, s in zip(lands, owns)]


def _sibling_send_halves(grads, by_cols, *, name):
    n = len(grads)

    def body(*refs):
        ins, outs = refs[:n], refs[n:2 * n]
        send, recv = refs[2 * n:]
        x, y, c, _ = _where_am_i()
        sibling = (x, y, 1 - c)
        cps = []
        for i in range(n):
            src = ins[i].at[(slice(None),) + _half_idx(*grads[i].shape[1:], by_cols[i], 1 - c)]
            d = pltpu.make_async_remote_copy(src_ref=src, dst_ref=outs[i], send_sem=send.at[i],
                                             recv_sem=recv.at[i], device_id=sibling, device_id_type=MESH)
            d.start()
            cps.append(d)
        for d in cps:
            d.wait()

    return _call(
        body, name=name, in_specs=[ANY] * n, out_specs=[ANY] * n,
        out_shape=[jax.ShapeDtypeStruct((N_CHIPS,) + _half_shape(*g.shape[1:], bc), g.dtype)
                   for g, bc in zip(grads, by_cols)],
        scratch_shapes=[pltpu.SemaphoreType.DMA((n,)), pltpu.SemaphoreType.DMA((n,))],
    )(*grads)


def _scatter_to_chips(parts, *, name):
    n = len(parts)

    def body(*refs):
        ins, outs = refs[:n], refs[n:2 * n]
        send, recv = refs[2 * n:]
        x, y, c, chips = _where_am_i()
        me = 2 * x + y
        cps = []
        for i in range(n):
            for k, (px, py) in enumerate(chips):
                d = pltpu.make_async_remote_copy(
                    src_ref=ins[i].at[2 * px + py], dst_ref=outs[i].at[me], send_sem=send.at[i, k],
                    recv_sem=recv.at[i, k], device_id=(px, py, c), device_id_type=MESH)
                d.start()
                cps.append((d, i, k, px, py))
        for d, i, k, px, py in cps:
            blk = outs[i].at[2 * px + py]
            pltpu.make_async_remote_copy(src_ref=blk, dst_ref=blk, send_sem=send.at[i, k], recv_sem=recv.at[i, k],
                                         device_id=(px, py, c), device_id_type=MESH).wait_recv()
        for d, *_ in cps:
            d.wait_send()

    got = _call(
        body, name=name, in_specs=[ANY] * n, out_specs=[ANY] * n,
        out_shape=[jax.ShapeDtypeStruct(p.shape, p.dtype) for p in parts],
        scratch_shapes=[pltpu.SemaphoreType.DMA((n, 3)), pltpu.SemaphoreType.DMA((n, 3))],
    )(*parts)
    me = 2 * lax.axis_index("x") + lax.axis_index("y")
    return [lax.dynamic_update_slice_in_dim(g, lax.dynamic_slice_in_dim(p, me, 1, axis=0), me, axis=0)
            for g, p in zip(got, parts)]


def _sibling_join_halves(halves, *, name):
    n = len(halves)

    def body(*refs):
        ins, outs = refs[:n], refs[n:2 * n]
        send, recv = refs[2 * n:]
        x, y, c, _ = _where_am_i()
        sibling = (x, y, 1 - c)
        cps = []
        for i in range(n):
            d = pltpu.make_async_remote_copy(src_ref=ins[i], dst_ref=outs[i], send_sem=send.at[i],
                                             recv_sem=recv.at[i], device_id=sibling, device_id_type=MESH)
            d.start()
            cps.append(d)
        for d in cps:
            d.wait()

    return _call(
        body, name=name, in_specs=[ANY] * n, out_specs=[ANY] * n,
        out_shape=[jax.ShapeDtypeStruct(h.shape, h.dtype) for h in halves],
        scratch_shapes=[pltpu.SemaphoreType.DMA((n,)), pltpu.SemaphoreType.DMA((n,))],
    )(*halves)


def _all_reduce_small(v, *, name, after=()):
    R, C = v.shape
    H = R // 2

    def body(v_ref, o_ref, sib, slots, send, recv):
        x, y, c, chips = _where_am_i()
        me = 2 * x + y
        sibling = (x, y, 1 - c)
        mine = pl.ds(pl.multiple_of(c * H, 8), H)
        other = pl.ds(pl.multiple_of((1 - c) * H, 8), H)

        def copy(k, src, dst, to):
            return pltpu.make_async_remote_copy(src_ref=src, dst_ref=dst, send_sem=send.at[k], recv_sem=recv.at[k],
                                                device_id=to, device_id_type=MESH)

        d = copy(0, v_ref.at[other], sib, sibling)
        d.start()
        d.wait()
        slots[me] = v_ref[mine, :] + sib[...]
        cps = [copy(1 + k, slots.at[me], slots.at[me], (px, py, c)) for k, (px, py) in enumerate(chips)]
        for d in cps:
            d.start()
        for k, (px, py) in enumerate(chips):
            blk = slots.at[2 * px + py]
            copy(1 + k, blk, blk, (px, py, c)).wait_recv()
        for d in cps:
            d.wait_send()
        o_ref[mine, :] = (slots[0] + slots[1]) + (slots[2] + slots[3])
        d = copy(4, o_ref.at[mine], o_ref.at[mine], sibling)
        d.start()
        copy(4, o_ref.at[other], o_ref.at[other], sibling).wait_recv()
        d.wait_send()

    vm = pl.BlockSpec(memory_space=pltpu.VMEM)
    return _call(
        body, after=after, name=name, in_specs=[vm], out_specs=vm,
        out_shape=jax.ShapeDtypeStruct((R, C), F32),
        scratch_shapes=[pltpu.VMEM((H, C), F32), pltpu.VMEM((N_CHIPS, H, C), F32),
                        pltpu.SemaphoreType.DMA((5,)), pltpu.SemaphoreType.DMA((5,))],
        compiler_params=pltpu.CompilerParams(vmem_limit_bytes=VMEM_LIMIT),
    )(v)


def _cols_from_shards(g):
    return jnp.transpose(g, (1, 0, 2)).reshape(g.shape[1], -1)


def _shards_from_cols(w):
    R, C4 = w.shape
    return jnp.transpose(w.reshape(R, N_CHIPS, C4 // N_CHIPS), (1, 0, 2))


def _block_diag(t):
    G, a, b = t.shape
    eye = jnp.eye(G, dtype=t.dtype)
    return (t[:, :, None, :] * eye[:, None, :, None]).reshape(G * a, G * b)


def _diag_blocks(xm, G):
    a, b = xm.shape[0] // G, xm.shape[1] // G
    idx = jnp.arange(G)
    return xm.reshape(G, a, G, b)[idx, :, idx, :]


def _pack(arrs):
    flat = []
    for a in arrs:
        f = a.reshape(-1).astype(F32)
        flat.append(jnp.pad(f, (0, _rup(f.shape[0], LANE) - f.shape[0])))
    v = jnp.concatenate(flat)
    rows = _rup(v.shape[0] // LANE, 16)
    v = jnp.pad(v, (0, rows * LANE - v.shape[0]))
    return v.reshape(rows, LANE)


def _unpack(v, shapes):
    flat = v.reshape(-1)
    out, off = [], 0
    for s in shapes:
        n = int(np.prod(s))
        out.append(flat[off:off + n].reshape(s))
        off += _rup(n, LANE)
    return out


def _ffn_fwd(x, Wup, Wdn, cw, cb, tag):
    h = _mm(x, Wup, 'nt', bmode='bo', tm=512, tn=4096, name=f"ffn_up_{tag}")
    a = _act_fwd(h, cw, cb, name=f"ffn_act_{tag}")
    f = _mm(a, Wdn, 'nn', bmode='abr', tm=512, tn=1024, tk=4096, name=f"ffn_down_{tag}")
    return f, h, a


def _ffn_bwd(df, x, h, a, Wup, Wdn, cw, cb, tag):
    da = _mm(df, Wdn, 'nt', bmode='bo', tm=512, tn=4096, name=f"ffn_da_{tag}")
    dWdn = _mm(a, df, 'tn', bmode='ao', tm=4096, tn=512, name=f"ffn_dwdn_{tag}", out_dtype=BF16)
    dh, dcw, dcb = _act_bwd(h, da, cw, cb, name=f"ffn_actb_{tag}")

    def shard_of(k):
        return (k % 2) * 2 + k // 2

    dx = _mm(dh, Wup, 'nn', bmode='abr', tm=512, tn=1024, tk=4096, name=f"ffn_dx_{tag}", b_map=shard_of)
    dWup = _mm(dh, x, 'tn', bmode='ao', tm=4096, tn=512, name=f"ffn_dwup_{tag}", out_dtype=BF16,
               o_map=shard_of)
    return dx, dWup, dWdn, dcw, dcb


def kernel(x, positions, ev_w_in, ev_b_f, ev_lambda_re, ev_lambda_im, ev_log_step, ev_ssm_b_re, ev_ssm_b_im, ev_ssm_c_re, ev_ssm_c_im, ev_ssm_d, ev_w_glu, ev_w_out, od_w_in, od_sinks, od_w_out, ln_mix_g, ln_mix_b, ffn_w_up, ffn_conv_w, ffn_conv_b, ffn_w_down, ln_ffn_g, ln_ffn_b, loss_target, m_ev_w_in, m_ev_b_f, m_ev_lambda_re, m_ev_lambda_im, m_ev_log_step, m_ev_ssm_b_re, m_ev_ssm_b_im, m_ev_ssm_c_re, m_ev_ssm_c_im, m_ev_ssm_d, m_ev_w_glu, m_ev_w_out, m_od_w_in, m_od_sinks, m_od_w_out, m_ln_mix_g, m_ln_mix_b, m_ffn_w_up, m_ffn_conv_w, m_ffn_conv_b, m_ffn_w_down, m_ln_ffn_g, m_ln_ffn_b, v_ev_w_in, v_ev_b_f, v_ev_lambda_re, v_ev_lambda_im, v_ev_log_step, v_ev_ssm_b_re, v_ev_ssm_b_im, v_ev_ssm_c_re, v_ev_ssm_c_im, v_ev_ssm_d, v_ev_w_glu, v_ev_w_out, v_od_w_in, v_od_sinks, v_od_w_out, v_ln_mix_g, v_ln_mix_b, v_ffn_w_up, v_ffn_conv_w, v_ffn_conv_b, v_ffn_w_down, v_ln_ffn_g, v_ln_ffn_b):
    W = dict(ev_w_in=ev_w_in, ev_b_f=ev_b_f, ev_lambda_re=ev_lambda_re, ev_lambda_im=ev_lambda_im, ev_log_step=ev_log_step, ev_ssm_b_re=ev_ssm_b_re, ev_ssm_b_im=ev_ssm_b_im, ev_ssm_c_re=ev_ssm_c_re, ev_ssm_c_im=ev_ssm_c_im, ev_ssm_d=ev_ssm_d, ev_w_glu=ev_w_glu, ev_w_out=ev_w_out, od_w_in=od_w_in, od_sinks=od_sinks, od_w_out=od_w_out, ln_mix_g=ln_mix_g, ln_mix_b=ln_mix_b, ffn_w_up=ffn_w_up, ffn_conv_w=ffn_conv_w, ffn_conv_b=ffn_conv_b, ffn_w_down=ffn_w_down, ln_ffn_g=ln_ffn_g, ln_ffn_b=ln_ffn_b)
    Mo = dict(ev_w_in=m_ev_w_in, ev_b_f=m_ev_b_f, ev_lambda_re=m_ev_lambda_re, ev_lambda_im=m_ev_lambda_im, ev_log_step=m_ev_log_step, ev_ssm_b_re=m_ev_ssm_b_re, ev_ssm_b_im=m_ev_ssm_b_im, ev_ssm_c_re=m_ev_ssm_c_re, ev_ssm_c_im=m_ev_ssm_c_im, ev_ssm_d=m_ev_ssm_d, ev_w_glu=m_ev_w_glu, ev_w_out=m_ev_w_out, od_w_in=m_od_w_in, od_sinks=m_od_sinks, od_w_out=m_od_w_out, ln_mix_g=m_ln_mix_g, ln_mix_b=m_ln_mix_b, ffn_w_up=m_ffn_w_up, ffn_conv_w=m_ffn_conv_w, ffn_conv_b=m_ffn_conv_b, ffn_w_down=m_ffn_w_down, ln_ffn_g=m_ln_ffn_g, ln_ffn_b=m_ln_ffn_b)
    Vo = dict(ev_w_in=v_ev_w_in, ev_b_f=v_ev_b_f, ev_lambda_re=v_ev_lambda_re, ev_lambda_im=v_ev_lambda_im, ev_log_step=v_ev_log_step, ev_ssm_b_re=v_ev_ssm_b_re, ev_ssm_b_im=v_ev_ssm_b_im, ev_ssm_c_re=v_ev_ssm_c_re, ev_ssm_c_im=v_ev_ssm_c_im, ev_ssm_d=v_ev_ssm_d, ev_w_glu=v_ev_w_glu, ev_w_out=v_ev_w_out, od_w_in=v_od_w_in, od_sinks=v_od_sinks, od_w_out=v_od_w_out, ln_mix_g=v_ln_mix_g, ln_mix_b=v_ln_mix_b, ffn_w_up=v_ffn_w_up, ffn_conv_w=v_ffn_conv_w, ffn_conv_b=v_ffn_conv_b, ffn_w_down=v_ffn_w_down, ln_ffn_g=v_ln_ffn_g, ln_ffn_b=v_ln_ffn_b)
    names = list(W.keys())
    big = ['ev_w_in', 'ev_w_glu', 'ev_w_out', 'od_w_in', 'od_w_out', 'ffn_w_up', 'ffn_w_down']

    S, D = x.shape[1], x.shape[2]
    x0 = x.reshape(S, D)
    tgt = loss_target.reshape(S, D)
    G, Pn, Cg = SSM_GROUPS, SSM_STATE, SSM_GROUP
    Fs = ffn_w_up.shape[2]
    FP = Fs
    Rd = ffn_w_down.shape[1]
    EIN = N_CHIPS * ev_w_in.shape[2]

    def as2d(a):
        return a.reshape(-1, a.shape[-1])

    cwl = ffn_conv_w.reshape(-1)
    cw_rows = _rup(_rup(cwl.shape[0], LANE) // LANE, 32)
    cw_pad = jnp.pad(cwl, (0, cw_rows * LANE - cwl.shape[0])).reshape(cw_rows, LANE)
    transposed = ('ev_w_in', 'ffn_w_up')

    def view(n, a):
        return jnp.transpose(a, (0, 2, 1)) if n in transposed else a

    Wv = {n: view(n, W[n]) for n in big}
    big_e = [(n, l) for n in big for l in range(W[n].shape[0])]
    split_cols = {e: (Wv[e[0]].shape[1] // 2) % 16 != 0 for e in big_e}
    shard16 = {e: Wv[e[0]][e[1]].astype(BF16) for e in big_e}
    grp_now = [e for e in big_e if e[0].startswith('ev_')]
    grp_ffn0 = [('ffn_w_up', 0), ('ffn_w_down', 0)]
    grp_l1 = [('od_w_in', 0), ('od_w_out', 0), ('ffn_w_up', 1), ('ffn_w_down', 1)]
    src_now = [shard16[e] for e in grp_now]
    src_ffn0 = [shard16[e] for e in grp_ffn0] + [cw_pad]
    src_l1 = [shard16[e] for e in grp_l1]
    cols_now = [split_cols[e] for e in grp_now]
    cols_ffn0 = [split_cols[e] for e in grp_ffn0] + [False]
    cols_l1 = [split_cols[e] for e in grp_l1]
    ag_in = _chip_exchange_start('gather', src_now[:1], cols_now[:1], name="ag_in_start")
    ag_mix = _chip_exchange_start('gather', src_now[1:], cols_now[1:], name="ag_mix_start", after=[ag_in[4]])
    ag_ffn0 = _chip_exchange_start('gather', src_ffn0, cols_ffn0, name="ag_ffn0_start", after=[ag_mix[4]])
    ag_l1 = _chip_exchange_start('gather', src_l1, cols_l1, name="ag_l1_start", after=[ag_ffn0[4]])
    started = [ag_l1[4]]

    def finish_gather(started, srcs, cols, after, tag):
        send, recv, thru, lands, _ = started
        thru, lands = _chip_exchange_wait('gather', send, recv, thru, lands, cols, after, name=f"ag_{tag}_wait")
        lands = _sibling_pass_gathered(lands, [s.shape for s in srcs], cols, name=f"ag_{tag}_pass")
        return _own_slot(lands, [s[None] for s in thru])

    lam_r, lam_i = ev_lambda_re[0], ev_lambda_im[0]
    lstep = ev_log_step[0].reshape(G, 1)
    a_re, a_im, g_re, g_im = _s5_disc_fwd(lam_r, lam_i, lstep, name="s5_disc", after=started)
    b_re2, b_im2 = ev_ssm_b_re[0].reshape(G * Pn, Cg), ev_ssm_b_im[0].reshape(G * Pn, Cg)
    g_re1, g_im1 = g_re.reshape(G * Pn, 1), g_im.reshape(G * Pn, 1)
    bb_re, bb_im = _s5_bb_fwd(g_re1, g_im1, b_re2, b_im2, name="s5_bb")
    bbt = jnp.stack([jnp.transpose(b.reshape(G, Pn, Cg), (0, 2, 1)).reshape(G * Cg, Pn) for b in (bb_re, bb_im)])
    BB = _diag_expand(bbt, Cg, Pn, name="s5_bb_dense")
    cct = jnp.stack([jnp.transpose(ev_ssm_c_re[0], (0, 2, 1)).reshape(G * Pn, Cg),
                     jnp.transpose(-ev_ssm_c_im[0], (0, 2, 1)).reshape(G * Pn, Cg)])
    CC = _diag_expand(cct, Pn, Cg, name="s5_cc_dense", after=started)
    a_cat = jnp.stack([a_re.reshape(1, G * Pn), a_im.reshape(1, G * Pn)])
    dskip = ev_ssm_d[0].reshape(1, SSM_WIDTH)
    tabs = _rope_tables(positions.reshape(S, 1).astype(F32), name="rope_tables", after=[BB, CC])

    gw = dict(zip(grp_now[:1], finish_gather(ag_in, src_now[:1], cols_now[:1], [tabs[2]], "in")))
    w_in_t = gw[('ev_w_in', 0)].reshape(EIN, D)
    qkv_w = 3 * FOX_WIDTH
    WmainT = jnp.concatenate([w_in_t[:qkv_w], w_in_t[qkv_w + FOX_HEADS:]], axis=0)
    WfT = jnp.pad(w_in_t[qkv_w:qkv_w + FOX_HEADS], ((0, LANE - FOX_HEADS), (0, 0)))
    cbs = [ffn_conv_b[l].reshape(N_CHIPS, Fs) for l in range(DEPTH)]

    P = _mm(x0, WmainT, 'nt', name="ev_proj")
    fl = _mm(x0, WfT, 'nt', name="ev_proj_f")
    bf_pad = jnp.pad(ev_b_f.reshape(1, FOX_HEADS), ((0, 0), (0, LANE - FOX_HEADS)))
    cgate, sgate = _gate_fwd(fl, bf_pad, name="fox_gate")
    ccol = jnp.transpose(cgate[:, :FOX_HEADS]).reshape(FOX_HEADS, S, 1)
    crow = jnp.transpose(cgate[:, :FOX_HEADS]).reshape(FOX_HEADS, 1, S)
    fox, lse = _fox_fwd(P, ccol, crow, name="fox_fwd")
    u_s5 = P[:, qkv_w:]
    UT, HT = _DIAG_TILE * Cg, _DIAG_TILE * Pn
    bu = _mm(u_s5, BB, 'nn', bmode='bo', tm=2048, tn=HT, tk=UT, diag='kn', name="s5_bu")
    hh = _s5_scan_fwd(bu, a_cat, name="s5_scan")
    yc = _mm(hh, CC, 'nn', bmode='abr', tm=2048, tn=UT, tk=HT, diag='kn', name="s5_y")
    y_s5, yg = _s5_out_fwd(yc, P, dskip, name="s5_out")
    gw.update(zip(grp_now[1:], finish_gather(ag_mix, src_now[1:], cols_now[1:], [yg], "mix")))
    Wglu = _cols_from_shards(gw[('ev_w_glu', 0)])
    Wout_ev = gw[('ev_w_out', 0)].reshape(D, D)
    z = _mm(yg, Wglu, 'nn', name="s5_glu_proj")
    ssm = _glu_fwd(z, name="s5_glu")
    cat = jnp.concatenate([fox.astype(BF16), ssm], axis=1)
    mix0 = _mm(cat, Wout_ev, 'nn', name="ev_out")
    x1, xh1, rs1 = _add_ln_fwd(x0, mix0, ln_mix_g[0], ln_mix_b[0], name="ln_mix0")
    got = finish_gather(ag_ffn0, src_ffn0, cols_ffn0, [x1], "ffn0")
    gw.update(zip(grp_ffn0, got[:-1]))
    cw_all = got[-1].reshape(N_CHIPS, -1)[:, :cwl.shape[0]].reshape(N_CHIPS, DEPTH, 3, Fs)
    cws = [cw_all[:, l] for l in range(DEPTH)]
    Wup = {0: gw[('ffn_w_up', 0)]}
    Wdn = {0: gw[('ffn_w_down', 0)].reshape(2, Fs, D)}
    f0, hf0, af0 = _ffn_fwd(x1, Wup[0], Wdn[0], cws[0], cbs[0], "l0")
    x2, xh2, rs2 = _add_ln_fwd(x1, f0, ln_ffn_g[0], ln_ffn_b[0], name="ln_ffn0")

    gw.update(zip(grp_l1, finish_gather(ag_l1, src_l1, cols_l1, [x2], "l1")))
    Wodin = _cols_from_shards(gw[('od_w_in', 0)])
    Wodout = gw[('od_w_out', 0)].reshape(D, D)
    Wup[1] = gw[('ffn_w_up', 1)]
    Wdn[1] = gw[('ffn_w_down', 1)].reshape(2, Fs, D)
    QW, KW = SWA_HEADS * SWA_HEAD_DIM, SWA_KV_HEADS * SWA_HEAD_DIM
    P1 = _mm(x2, Wodin, 'nn', name="od_proj")
    qr = _rope_apply(P1, tabs, col0=0, width=QW, inverse=False, name="rope_q", out_dtype=BF16)
    kr = _rope_apply(P1, tabs, col0=QW, width=KW, inverse=False, name="rope_k", out_dtype=BF16)

    def heads(a2, nh):
        return jnp.transpose(a2.reshape(S, nh, SWA_HEAD_DIM), (1, 0, 2))

    def unheads(a3):
        return jnp.transpose(a3, (1, 0, 2)).reshape(S, -1)

    qT, kT = heads(qr, SWA_HEADS), heads(kr, SWA_KV_HEADS)
    vT = heads(P1[:, QW + KW:].astype(BF16), SWA_KV_HEADS)
    sink_rows = jnp.broadcast_to(od_sinks[0].reshape(SWA_KV_HEADS, SWA_GROUPS, 1, 1),
                                 (SWA_KV_HEADS, SWA_GROUPS, SWA_WINDOW, 1)).reshape(SWA_KV_HEADS, -1, 1)
    oT, Lsw = _swa_fwd(qT, kT, vT, sink_rows, name="swa_fwd")
    o_sw = unheads(oT).astype(BF16)
    mix1 = _mm(o_sw, Wodout, 'nn', name="od_out")
    x3, xh3, rs3 = _add_ln_fwd(x2, mix1, ln_mix_g[1], ln_mix_b[1], name="ln_mix1")
    f1, hf1, af1 = _ffn_fwd(x3, Wup[1], Wdn[1], cws[1], cbs[1], "l1")
    x4, xh4, rs4 = _add_ln_fwd(x3, f1, ln_ffn_g[1], ln_ffn_b[1], name="ln_ffn1")
    dy, loss_part = _loss_grad(x4, tgt, name="loss")

    dz4, dg_ffn1, db_ffn1 = _ln_bwd(dy, None, xh4, rs4, ln_ffn_g[1], name="lnb_ffn1")
    dx3f, dWup1, dWdn1, dcw1, dcb1 = _ffn_bwd(dz4, x3, hf1, af1, Wup[1], Wdn[1], cws[1], cbs[1], "l1")
    sib_ffn1 = _sibling_halves_start([dWup1, dWdn1.reshape(N_CHIPS, Rd, D)], [False, False], name="rs_ffn1_sib_start")
    dz3, dg_mix1, db_mix1 = _ln_bwd(dz4, dx3f, xh3, rs3, ln_mix_g[1], name="lnb_mix1", after=[sib_ffn1[4]])
    do_sw = _mm(dz3, Wodout, 'nt', name="od_out_dx")
    dWodout = _mm(o_sw, dz3, 'tn', name="od_out_dw", out_dtype=BF16)
    doT = heads(do_sw, SWA_HEADS)
    dqT, dkT, dvT, dsink = _swa_bwd(qT, kT, vT, sink_rows, oT, Lsw, doT, name="swa_bwd")
    dq1 = _rope_apply(unheads(dqT), tabs, col0=0, width=QW, inverse=True, name="rope_dq", out_dtype=BF16)
    dk1 = _rope_apply(unheads(dkT[:, SWA_WINDOW:]), tabs, col0=0, width=KW, inverse=True, name="rope_dk",
                      out_dtype=BF16)
    dP1 = jnp.concatenate([dq1, dk1, unheads(dvT[:, SWA_WINDOW:]).astype(BF16)], axis=1)
    dx2m = _mm(dP1, Wodin, 'nt', name="od_proj_dx")
    dWodin = _mm(x2, dP1, 'tn', name="od_proj_dw", out_dtype=BF16)

    def rs_begin(entries, grads, tag):
        cols = [split_cols[e] for e in entries]
        sib = _sibling_send_halves(grads, cols, name=f"rs_{tag}_sibling")
        return [_sum2_halves(g4, s4, bc, name=f"rs_sum2_{n}{l}")
                for (n, l), g4, s4, bc in zip(entries, grads, sib, cols)]

    def rs_begin_started(entries, started, after, tag):
        send, rcv, thru, lands, _ = started
        thru, lands = _sibling_halves_wait(send, rcv, thru, lands, [False] * len(thru), after,
                                           name=f"rs_{tag}_sib_wait")
        return [_sum2_halves(g4, s4, False, name=f"rs_sum2_{n}{l}") for (n, l), g4, s4 in zip(entries, thru, lands)]

    def own_parts(parts):
        me = 2 * lax.axis_index("x") + lax.axis_index("y")
        return [lax.dynamic_slice_in_dim(p, me, 1, axis=0) for p in parts]

    part_l1 = (rs_begin(grp_l1[:2], [_shards_from_cols(dWodin), dWodout.reshape(N_CHIPS, D // N_CHIPS, D)], "od")
               + rs_begin_started(grp_l1[2:], sib_ffn1, [dWodin], "ffn1"))
    rs_l1 = _chip_exchange_start('scatter', part_l1, [False] * len(part_l1), name="rs_l1_start")

    dz2, dg_ffn0, db_ffn0 = _ln_bwd(dz3, dx2m, xh2, rs2, ln_ffn_g[0], name="lnb_ffn0", after=[rs_l1[4]])
    dx1f, dWup0, dWdn0, dcw0, dcb0 = _ffn_bwd(dz2, x1, hf0, af0, Wup[0], Wdn[0], cws[0], cbs[0], "l0")
    sib_ffn0 = _sibling_halves_start([dWup0, dWdn0.reshape(N_CHIPS, Rd, D)], [False, False], name="rs_ffn0_sib_start")
    dz1, dg_mix0, db_mix0 = _ln_bwd(dz2, dx1f, xh1, rs1, ln_mix_g[0], name="lnb_mix0", after=[sib_ffn0[4]])
    dcat = _mm(dz1, Wout_ev, 'nt', name="ev_out_dx")
    dWout_ev = _mm(cat, dz1, 'tn', name="ev_out_dw", out_dtype=BF16)
    part_ffn0 = rs_begin_started(grp_ffn0, sib_ffn0, [dWout_ev], "ffn0")
    rs_ffn0 = _chip_exchange_start('scatter', part_ffn0, [False] * len(part_ffn0), name="rs_ffn0_start")
    dz = _glu_bwd(z, dcat, name="s5_glu_bwd")
    dyg = _mm(dz, Wglu, 'nt', name="s5_glu_dx", after=[rs_ffn0[4]])
    dWglu = _mm(yg, dz, 'tn', name="s5_glu_dw", out_dtype=BF16)
    dy_s5, du_dir, dD = _s5_out_bwd(dyg, y_s5, P, dskip, name="s5_out_bwd")
    dhh = _mm(dy_s5, CC, 'nt', bmode='bo', tm=2048, tn=HT, tk=UT, diag='kn', name="s5_y_dx")
    dCC = _mm(hh, dy_s5, 'tn', bmode='ao', tm=HT, tn=UT, diag='mn', name="s5_y_dw")
    lam, da_s5 = _s5_scan_bwd(dhh, hh, a_cat, name="s5_scan_bwd")
    du_bu = _mm(lam, BB, 'nt', bmode='abr', tm=2048, tn=UT, tk=HT, diag='kn', name="s5_bu_dx")
    dBB = _mm(u_s5, lam, 'tn', bmode='bo', tm=UT, tn=HT, diag='mn', name="s5_bu_dw")
    du = _combine([du_dir, du_bu], [1.0, 1.0], name="s5_du", out_dtype=BF16)
    dq0, dk0, dv0, dccol, dcrow = _fox_bwd(P, ccol, crow, fox, lse, dcat, name="fox_bwd")
    dc = jnp.transpose((dccol.reshape(FOX_HEADS, S) - dcrow.reshape(FOX_HEADS, S)))
    dc = jnp.pad(dc, ((0, 0), (0, LANE - FOX_HEADS)))
    dfl, dbf = _gate_bwd(dc, sgate, name="fox_gate_bwd")
    dP = jnp.concatenate([dq0, dk0, dv0, du], axis=1)
    dx0a = _mm(dP, WmainT, 'nn', name="ev_proj_dx")
    dx0b = _mm(dfl, WfT, 'nn', name="ev_proj_f_dx")
    dWmainT = _mm(dP, x0, 'tn', tm=1024, tn=1024, name="ev_proj_dw", out_dtype=BF16)
    dWfT = _mm(dfl, x0, 'tn', name="ev_proj_f_dw", out_dtype=BF16)
    grad_x = _combine([dz1, dx0a, dx0b], [ALPHA, 1.0, 1.0], name="grad_x")

    dbbt = _diag_extract(dBB, Cg, Pn, name="s5_bb_diag")
    dcct = _diag_extract(dCC, Pn, Cg, name="s5_cc_diag")
    dbb_re = jnp.transpose(dbbt[0].reshape(G, Cg, Pn), (0, 2, 1)).reshape(G * Pn, Cg)
    dbb_im = jnp.transpose(dbbt[1].reshape(G, Cg, Pn), (0, 2, 1)).reshape(G * Pn, Cg)
    db_re, db_im, dg_re1, dg_im1 = _s5_bb_bwd(g_re1, g_im1, b_re2, b_im2, dbb_re, dbb_im, name="s5_bb_bwd")
    dlam_re, dlam_im, dlstep = _s5_disc_bwd(lam_r, lam_i, lstep, da_s5[0].reshape(G, Pn), da_s5[1].reshape(G, Pn),
                                            dg_re1.reshape(G, Pn), dg_im1.reshape(G, Pn), name="s5_disc_bwd")
    dc_re = jnp.transpose(dcct[0].reshape(G, Pn, Cg), (0, 2, 1))
    dc_im = -jnp.transpose(dcct[1].reshape(G, Pn, Cg), (0, 2, 1))

    def conv_w_full(d0, d1):
        return jnp.stack([jnp.reshape(jnp.transpose(d[:, :, :Fs], (1, 0, 2)), (3, N_CHIPS * Fs)) for d in (d0, d1)])

    def conv_b_full(d0, d1):
        return jnp.stack([jnp.reshape(d[:, 0, :Fs], (N_CHIPS * Fs,)) for d in (d0, d1)])

    small_local = dict(
        ev_b_f=dbf[:, :FOX_HEADS], ev_lambda_re=dlam_re, ev_lambda_im=dlam_im, ev_log_step=dlstep,
        ev_ssm_b_re=db_re, ev_ssm_b_im=db_im, ev_ssm_c_re=dc_re, ev_ssm_c_im=dc_im, ev_ssm_d=dD,
        od_sinks=dsink[:, :, 0],
        ln_mix_g=jnp.concatenate([dg_mix0, dg_mix1]), ln_mix_b=jnp.concatenate([db_mix0, db_mix1]),
        ffn_conv_w=conv_w_full(dcw0, dcw1), ffn_conv_b=conv_b_full(dcb0, dcb1),
        ln_ffn_g=jnp.concatenate([dg_ffn0, dg_ffn1]), ln_ffn_b=jnp.concatenate([db_ffn0, db_ffn1]))
    small = list(small_local.keys())
    out_g, out_d, out_m, out_v = {}, {}, {}, {}
    loss_out = []

    def small_update(after):
        red = _all_reduce_small(_pack([small_local[n] for n in small] + [loss_part]), name="ar_small", after=after)
        full_shapes = [W[n].shape if n != 'ffn_conv_w' else (DEPTH, 3, N_CHIPS * Fs) for n in small]
        pieces = _unpack(red, full_shapes + [()])
        loss_out.append(pieces[-1])
        gsmall = dict(zip(small, pieces[:-1]))
        chip = 2 * lax.axis_index("x") + lax.axis_index("y")
        gsmall['ffn_conv_w'] = lax.dynamic_slice_in_dim(gsmall['ffn_conv_w'], chip * Fs, Fs, axis=2)
        shapes = [W[n].shape for n in small]
        gs, ds_, ms, vs = _adamw(_pack([W[n] for n in small])[None], _pack([gsmall[n] for n in small])[None],
                                 _pack([Mo[n] for n in small])[None], _pack([Vo[n] for n in small])[None],
                                 name="adamw_small", tr=1 << 14)
        out_g.update(zip(small, _unpack(gs, shapes)))
        out_d.update(zip(small, _unpack(ds_, shapes)))
        out_m.update(zip(small, _unpack(ms, shapes)))
        out_v.update(zip(small, _unpack(vs, shapes)))
        return vs

    dw_in_t = jnp.concatenate([dWmainT[:qkv_w], dWfT[:FOX_HEADS], dWmainT[qkv_w:]], axis=0)
    part_now = rs_begin(grp_now, [dw_in_t.reshape(N_CHIPS, EIN // N_CHIPS, D), _shards_from_cols(dWglu),
                                  dWout_ev.reshape(N_CHIPS, D // N_CHIPS, D)], "l0")
    small_done = small_update([grad_x])
    rs_now = _chip_exchange_start('scatter', part_now, [False] * len(part_now), name="rs_l0_start",
                                  after=[small_done])

    def finish_scatter(started, parts, after, tag):
        send, rcv, thru, lands, _ = started
        thru, lands = _chip_exchange_wait('scatter', send, rcv, thru, lands, [False] * len(parts), after,
                                          name=f"rs_{tag}_wait")
        return _own_slot(lands, own_parts(thru))

    def update(entries, recv, tag):
        halves = [_rowsum(r, name=f"rs_sum4_{e[0]}{e[1]}") for e, r in zip(entries, recv)]
        send, rcv, thru, lands, tok = _sibling_swap_start(halves, name=f"rs_{tag}_join_start")
        own = dict(zip(entries, thru))
        params = list(dict.fromkeys(e[0] for e in entries))

        def half_update(n, grads, is_own, prev, after_name):
            return _adamw_half(Wv[n], [grads[(n, l)] for l in range(W[n].shape[0])], view(n, Mo[n]), view(n, Vo[n]),
                               name=f"adamw_{after_name}_{n}", own=is_own, prev=prev, by_cols=split_cols[(n, 0)])

        first = {n: half_update(n, own, True, None, "own") for n in params}
        _, others = _sibling_swap_wait(send, rcv, thru, lands, [first[n][3] for n in params] + [tok],
                                       name=f"rs_{tag}_join_wait")
        oth = dict(zip(entries, others))
        done = []
        for n in params:
            res = half_update(n, oth, False, first[n], "sib")
            out_g[n], out_d[n], out_m[n], out_v[n] = (view(n, t) for t in res)
            done.append(res[3])
        return done

    recv_rest = (finish_scatter(rs_l1, part_l1, [rs_now[4]], "l1")
                 + finish_scatter(rs_ffn0, part_ffn0, [rs_now[4]], "ffn0"))
    done = update(grp_l1 + grp_ffn0, recv_rest, "rest")
    update(grp_now, finish_scatter(rs_now, part_now, done, "l0"), "l0")
    loss = loss_out[0]

    return (loss, grad_x.reshape(1, S, D), *[out_g[n] for n in names], *[out_d[n] for n in names],
            *[out_m[n] for n in names], *[out_v[n] for n in names])
```

```python
import functools
import math

import numpy as np
import jax
import jax.numpy as jnp
from jax import lax
from jax.experimental import pallas as pl
from jax.experimental.pallas import tpu as pltpu

F32 = jnp.float32
BF16 = jnp.bfloat16
MESH = pl.DeviceIdType.MESH
ANY = pl.BlockSpec(memory_space=pl.ANY)

D_MODEL = 2048
FOX_HEADS = 8
FOX_HEAD_DIM = 128
FOX_WIDTH = 1024
SSM_WIDTH = 1024
SSM_GROUP = 16
SSM_GROUPS = 64
SSM_STATE = 64
SWA_HEADS = 32
SWA_KV_HEADS = 4
SWA_HEAD_DIM = 64
SWA_GROUPS = 8
SWA_WINDOW = 128
ROPE_DIM = 16
ROPE_THETA = 500000.0
LN_EPS = 1e-5
DEPTH = 2
ALPHA = (2.0 * DEPTH) ** 0.25
ADAM_LR = 0.001
ADAM_B1 = 0.9
ADAM_B2 = 0.999
ADAM_EPS = 1e-08
ADAM_WD = 0.01
ADAM_STEP = 10
N_CHIPS = 4

VMEM_LIMIT = 56 * 1024 * 1024
LANE = 128


def _call(body, after=(), **kw):
    if after:
        n = len(after)

        def shifted(*refs):
            return body(*refs[n:])

        call = _call(shifted, **dict(kw, in_specs=[ANY] * n + list(kw["in_specs"])))
        return lambda *args: call(*after, *args)
    return pl.pallas_call(body, **kw)


def _cparams(sem):
    return pltpu.CompilerParams(dimension_semantics=sem, vmem_limit_bytes=VMEM_LIMIT)


def _rup(n, m):
    return (n + m - 1) // m * m


def _pick(n, pref):
    if n <= pref:
        return n
    for step in (128, 16, 8):
        for t in range(pref - pref % step, 0, -step):
            if n % t == 0:
                return t
    return n


def _tile2d(rows, cols, pref_rows=256, budget=256 * 1024):
    tr = _pick(rows, pref_rows)
    if tr < 64:
        tr = rows
    if cols % LANE:
        return tr, cols
    return tr, _pick(cols, max(LANE, budget // tr // LANE * LANE))


def _mm(a, b, mode, *, name, tm=512, tn=1024, tk=2048, bmode=None, out_dtype=F32, after=(), b_map=None,
        o_map=None, diag=None):
    a3 = a if a.ndim == 3 else a[None]
    b3 = b if b.ndim == 3 else b[None]
    if mode == 'tn':
        K, M = a3.shape[1:]
    else:
        M, K = a3.shape[1:]
    N = b3.shape[1] if mode == 'nt' else b3.shape[2]
    tm, tn, tk = _pick(M, tm), _pick(N, tn), _pick(K, tk)
    nb = max(a3.shape[0], b3.shape[0])
    nbo, nbr = (1, nb) if bmode == 'abr' else (nb, 1)
    nm, nk = M // tm, K // tk
    if diag == 'kn':
        assert K // tk == N // tn
        nk = 1
    if diag == 'mn':
        assert M // tm == N // tn
        nm = 1
    nred = nbr * nk
    a_b = bmode in ('ao', 'abr')
    b_b = bmode in ('bo', 'abr')
    o_b = bmode in ('bo', 'ao')

    def bsel(flag, bo, br, remap=None):
        if not flag:
            return 0
        return (bo + br) if remap is None else remap(bo + br)

    def mi(i, j):
        return j if diag == 'mn' else i

    def ki(j, k):
        return j if diag == 'kn' else k

    if mode == 'tn':
        a_spec = pl.BlockSpec((None, tk, tm), lambda bo, i, j, br, k: (bsel(a_b, bo, br), ki(j, k), mi(i, j)))
    else:
        a_spec = pl.BlockSpec((None, tm, tk), lambda bo, i, j, br, k: (bsel(a_b, bo, br), mi(i, j), ki(j, k)))
    if mode == 'nt':
        b_spec = pl.BlockSpec((None, tn, tk), lambda bo, i, j, br, k: (bsel(b_b, bo, br, b_map), j, ki(j, k)))
    else:
        b_spec = pl.BlockSpec((None, tk, tn), lambda bo, i, j, br, k: (bsel(b_b, bo, br, b_map), ki(j, k), j))
    o_spec = pl.BlockSpec((None, tm, tn), lambda bo, i, j, br, k: (bsel(o_b, bo, br, o_map), mi(i, j), j))
    dn = {'nn': (((1,), (0,)), ((), ())), 'nt': (((1,), (1,)), ((), ())), 'tn': (((0,), (0,)), ((), ()))}[mode]

    def body(a_ref, b_ref, *rest):
        o_ref, scratch = rest[len(after)], rest[len(after) + 1:]
        r = lax.dot_general(a_ref[...].astype(BF16), b_ref[...].astype(BF16), dn, preferred_element_type=F32)
        if nred == 1:
            o_ref[...] = r.astype(out_dtype)
        else:
            acc = scratch[0]
            step = pl.program_id(3) * nk + pl.program_id(4)

            @pl.when(step == 0)
            def _():
                acc[...] = r

            @pl.when(step > 0)
            def _():
                acc[...] += r

            @pl.when(step == nred - 1)
            def _():
                o_ref[...] = acc[...].astype(out_dtype)

    out = _call(
        body, name=name,
        grid=(nbo, nm, N // tn, nbr, nk),
        in_specs=[a_spec, b_spec] + [ANY] * len(after), out_specs=o_spec,
        out_shape=jax.ShapeDtypeStruct((nbo if o_b else 1, M, N), out_dtype),
        scratch_shapes=[] if nred == 1 else [pltpu.VMEM((tm, tn), F32)],
        compiler_params=_cparams(("parallel", "parallel", "parallel", "arbitrary", "arbitrary")),
    )(a3, b3, *after)
    return out if o_b else out[0]


def _add_ln_fwd(x, r, g, b, *, name):
    S, D = x.shape
    tr = _pick(S, 256)

    def body(x_ref, r_ref, g_ref, b_ref, o_ref, xh_ref, rs_ref):
        z = ALPHA * x_ref[...] + r_ref[...]
        mu = jnp.mean(z, axis=-1, keepdims=True)
        zc = z - mu
        var = jnp.mean(zc * zc, axis=-1, keepdims=True)
        rstd = lax.rsqrt(var + LN_EPS)
        xh = zc * rstd
        xh_ref[...] = xh
        rs_ref[...] = rstd
        o_ref[...] = xh * g_ref[...] + b_ref[...]

    row = pl.BlockSpec((tr, D), lambda i: (i, 0))
    vec = pl.BlockSpec((1, D), lambda i: (0, 0))
    return _call(
        body, name=name, grid=(S // tr,),
        in_specs=[row, row, vec, vec],
        out_specs=[row, row, pl.BlockSpec((tr, 1), lambda i: (i, 0))],
        out_shape=[jax.ShapeDtypeStruct((S, D), F32), jax.ShapeDtypeStruct((S, D), F32),
                   jax.ShapeDtypeStruct((S, 1), F32)],
        compiler_params=_cparams(("parallel",)),
    )(x, r, g.reshape(1, D), b.reshape(1, D))


def _ln_bwd(da, db, xhat, rstd, g, *, name, after=()):
    S, D = xhat.shape
    tr = _pick(S, 256)
    two = db is not None

    def body(*refs):
        refs = refs[len(after):]
        if two:
            da_ref, db_ref, xh_ref, rs_ref, g_ref, dz_ref, dg_ref, dbt_ref = refs
            dy = ALPHA * da_ref[...] + db_ref[...]
        else:
            da_ref, xh_ref, rs_ref, g_ref, dz_ref, dg_ref, dbt_ref = refs
            dy = da_ref[...]
        xh = xh_ref[...]
        dxh = dy * g_ref[...]
        m1 = jnp.mean(dxh, axis=-1, keepdims=True)
        m2 = jnp.mean(dxh * xh, axis=-1, keepdims=True)
        dz_ref[...] = rs_ref[...] * (dxh - m1 - xh * m2)
        pg = jnp.sum(dy * xh, axis=0, keepdims=True)
        pb = jnp.sum(dy, axis=0, keepdims=True)

        @pl.when(pl.program_id(0) == 0)
        def _():
            dg_ref[...] = pg
            dbt_ref[...] = pb

        @pl.when(pl.program_id(0) > 0)
        def _():
            dg_ref[...] += pg
            dbt_ref[...] += pb

    row = pl.BlockSpec((tr, D), lambda i: (i, 0))
    vec = pl.BlockSpec((1, D), lambda i: (0, 0))
    ins = list(after) + [da] + ([db] if two else []) + [xhat, rstd, g.reshape(1, D)]
    in_specs = [ANY] * len(after) + [row] + ([row] if two else []) + [row, pl.BlockSpec((tr, 1), lambda i: (i, 0)), vec]
    return _call(
        body, name=name, grid=(S // tr,),
        in_specs=in_specs, out_specs=[row, vec, vec],
        out_shape=[jax.ShapeDtypeStruct((S, D), F32), jax.ShapeDtypeStruct((1, D), F32),
                   jax.ShapeDtypeStruct((1, D), F32)],
        compiler_params=_cparams(("arbitrary",)),
    )(*ins)


def _loss_grad(y, t, *, name):
    S, D = y.shape
    tr = _pick(S, 256)

    def body(y_ref, t_ref, dy_ref, l_ref):
        e = y_ref[...] - t_ref[...]
        dy_ref[...] = e * (1.0 / D)
        part = 0.5 * jnp.sum(jnp.sum(e * e, axis=-1, keepdims=True) * (1.0 / D), axis=0, keepdims=True)

        @pl.when(pl.program_id(0) == 0)
        def _():
            l_ref[...] = part

        @pl.when(pl.program_id(0) > 0)
        def _():
            l_ref[...] += part

    row = pl.BlockSpec((tr, D), lambda i: (i, 0))
    return _call(
        body, name=name, grid=(S // tr,), in_specs=[row, row],
        out_specs=[row, pl.BlockSpec((1, 1), lambda i: (0, 0))],
        out_shape=[jax.ShapeDtypeStruct((S, D), F32), jax.ShapeDtypeStruct((1, 1), F32)],
        compiler_params=_cparams(("arbitrary",)),
    )(y, t)


def _combine(terms, scales, *, name, out_dtype=F32):
    S, D = terms[0].shape
    tr = _pick(S, 256)
    n = len(terms)

    def body(*refs):
        acc = scales[0] * refs[0][...].astype(F32)
        for i in range(1, n):
            acc = acc + scales[i] * refs[i][...].astype(F32)
        refs[n][...] = acc.astype(out_dtype)

    row = pl.BlockSpec((tr, D), lambda i: (i, 0))
    return _call(
        body, name=name, grid=(S // tr,), in_specs=[row] * n, out_specs=row,
        out_shape=jax.ShapeDtypeStruct((S, D), out_dtype),
        compiler_params=_cparams(("parallel",)),
    )(*terms)


def _split3(x):
    h = x.astype(BF16)
    r = x - h.astype(F32)
    m = r.astype(BF16)
    l = (r - m.astype(F32)).astype(BF16)
    return h, m, l


def _tri_matmul(tri_bf, x):
    h, m, l = _split3(x)
    dn = (((1,), (0,)), ((), ()))
    return (lax.dot_general(tri_bf, l, dn, preferred_element_type=F32)
            + lax.dot_general(tri_bf, m, dn, preferred_element_type=F32)
            + lax.dot_general(tri_bf, h, dn, preferred_element_type=F32))


def _gate_fwd(fl, bf, *, name):
    S = fl.shape[0]
    tc = _pick(S, 256)
    nchunk = S // tc

    def body(fl_ref, bf_ref, c_ref, sg_ref):
        r = lax.broadcasted_iota(jnp.int32, (tc, tc), 0)
        cidx = lax.broadcasted_iota(jnp.int32, (tc, tc), 1)
        tri = (r >= cidx).astype(BF16)
        carry = jnp.zeros((1, LANE), F32)
        for ch in range(nchunk):
            x = fl_ref[pl.ds(ch * tc, tc), :] + bf_ref[...]
            lf = jnp.minimum(x, 0.0) - jnp.log(1.0 + jnp.exp(-jnp.abs(x)))
            sg_ref[pl.ds(ch * tc, tc), :] = jax.nn.sigmoid(-x)
            c_ref[pl.ds(ch * tc, tc), :] = _tri_matmul(tri, lf) + carry
            carry = carry + jnp.sum(lf, axis=0, keepdims=True)

    full = pl.BlockSpec((S, LANE), lambda: (0, 0))
    return _call(
        body, name=name, in_specs=[full, pl.BlockSpec((1, LANE), lambda: (0, 0))], out_specs=[full, full],
        out_shape=[jax.ShapeDtypeStruct((S, LANE), F32)] * 2,
        compiler_params=pltpu.CompilerParams(vmem_limit_bytes=VMEM_LIMIT),
    )(fl, bf)


def _gate_bwd(dc, sg, *, name):
    S = dc.shape[0]
    tc = _pick(S, 256)
    nchunk = S // tc

    def body(dc_ref, sg_ref, dfl_ref, db_ref):
        r = lax.broadcasted_iota(jnp.int32, (tc, tc), 0)
        cidx = lax.broadcasted_iota(jnp.int32, (tc, tc), 1)
        tri = (r <= cidx).astype(BF16)
        carry = jnp.zeros((1, LANE), F32)
        dbacc = jnp.zeros((1, LANE), F32)
        for ch in reversed(range(nchunk)):
            d = dc_ref[pl.ds(ch * tc, tc), :]
            dfl = (_tri_matmul(tri, d) + carry) * sg_ref[pl.ds(ch * tc, tc), :]
            dfl_ref[pl.ds(ch * tc, tc), :] = dfl
            dbacc = dbacc + jnp.sum(dfl, axis=0, keepdims=True)
            carry = carry + jnp.sum(d, axis=0, keepdims=True)
        db_ref[...] = dbacc

    full = pl.BlockSpec((S, LANE), lambda: (0, 0))
    return _call(
        body, name=name, in_specs=[full, full], out_specs=[full, pl.BlockSpec((1, LANE), lambda: (0, 0))],
        out_shape=[jax.ShapeDtypeStruct((S, LANE), F32), jax.ShapeDtypeStruct((1, LANE), F32)],
        compiler_params=pltpu.CompilerParams(vmem_limit_bytes=VMEM_LIMIT),
    )(dc, sg)


def _fox_scores(q_ref, k_ref, cc_ref, cr_ref, qi, tq, S):
    scale = 1.0 / math.sqrt(FOX_HEAD_DIM)
    s = lax.dot_general(q_ref[...].astype(BF16), k_ref[...].astype(BF16), (((1,), (1,)), ((), ())),
                        preferred_element_type=F32) * scale
    s = s + cc_ref[...] - cr_ref[...]
    row = lax.broadcasted_iota(jnp.int32, (tq, S), 0) + qi * tq
    col = lax.broadcasted_iota(jnp.int32, (tq, S), 1)
    return s, row >= col


def _fox_fwd(P, ccol, crow, *, name):
    S = P.shape[0]
    tq = _pick(S, 256)
    H = FOX_HEADS

    def body(q_ref, k_ref, v_ref, cc_ref, cr_ref, o_ref, l_ref):
        s, causal = _fox_scores(q_ref, k_ref, cc_ref, cr_ref, pl.program_id(1), tq, S)
        s = jnp.where(causal, s, -1e30)
        m = jnp.max(s, axis=-1, keepdims=True)
        e = jnp.exp(s - m)
        den = jnp.sum(e, axis=-1, keepdims=True)
        p = e / den
        o_ref[...] = jnp.dot(p.astype(BF16), v_ref[...].astype(BF16), preferred_element_type=F32)
        l_ref[...] = m + jnp.log(den)

    return _call(
        body, name=name, grid=(H, S // tq),
        in_specs=[pl.BlockSpec((tq, 128), lambda h, i: (i, h)),
                  pl.BlockSpec((S, 128), lambda h, i: (0, H + h)),
                  pl.BlockSpec((S, 128), lambda h, i: (0, 2 * H + h)),
                  pl.BlockSpec((None, tq, 1), lambda h, i: (h, i, 0)),
                  pl.BlockSpec((None, 1, S), lambda h, i: (h, 0, 0))],
        out_specs=[pl.BlockSpec((tq, 128), lambda h, i: (i, h)),
                   pl.BlockSpec((None, tq, 1), lambda h, i: (h, i, 0))],
        out_shape=[jax.ShapeDtypeStruct((S, FOX_WIDTH), F32), jax.ShapeDtypeStruct((H, S, 1), F32)],
        compiler_params=_cparams(("parallel", "parallel")),
    )(P, P, P, ccol, crow)


def _fox_bwd(P, ccol, crow, o, lse, dcat, *, name):
    S = P.shape[0]
    tq = _pick(S, 256)
    H = FOX_HEADS
    nq = S // tq
    scale = 1.0 / math.sqrt(FOX_HEAD_DIM)

    def body(q_ref, k_ref, v_ref, cc_ref, cr_ref, o_ref, l_ref, do_ref,
             dq_ref, dk_ref, dv_ref, dcc_ref, dcr_ref, dk_acc, dv_acc):
        qi = pl.program_id(1)
        s, causal = _fox_scores(q_ref, k_ref, cc_ref, cr_ref, qi, tq, S)
        p = jnp.where(causal, jnp.exp(s - l_ref[...]), 0.0)
        do = do_ref[...]
        do_bf = do.astype(BF16)
        dp = lax.dot_general(do_bf, v_ref[...].astype(BF16), (((1,), (1,)), ((), ())), preferred_element_type=F32)
        delta = jnp.sum(do * o_ref[...], axis=-1, keepdims=True)
        ds = p * (dp - delta)
        ds_bf = ds.astype(BF16)
        dq_ref[...] = (jnp.dot(ds_bf, k_ref[...].astype(BF16), preferred_element_type=F32) * scale).astype(BF16)
        dkp = lax.dot_general(ds_bf, q_ref[...].astype(BF16), (((0,), (0,)), ((), ())),
                              preferred_element_type=F32) * scale
        dvp = lax.dot_general(p.astype(BF16), do_bf, (((0,), (0,)), ((), ())), preferred_element_type=F32)
        dcc_ref[...] = jnp.sum(ds, axis=-1, keepdims=True)
        dcr = jnp.sum(ds, axis=0, keepdims=True)

        @pl.when(qi == 0)
        def _():
            dk_acc[...] = dkp
            dv_acc[...] = dvp
            dcr_ref[...] = dcr

        @pl.when(qi > 0)
        def _():
            dk_acc[...] += dkp
            dv_acc[...] += dvp
            dcr_ref[...] += dcr

        @pl.when(qi == nq - 1)
        def _():
            dk_ref[...] = dk_acc[...].astype(BF16)
            dv_ref[...] = dv_acc[...].astype(BF16)

    qblk = pl.BlockSpec((tq, 128), lambda h, i: (i, h))
    kvo = pl.BlockSpec((S, 128), lambda h, i: (0, h))
    col = pl.BlockSpec((None, tq, 1), lambda h, i: (h, i, 0))
    rowv = pl.BlockSpec((None, 1, S), lambda h, i: (h, 0, 0))
    return _call(
        body, name=name, grid=(H, nq),
        in_specs=[qblk,
                  pl.BlockSpec((S, 128), lambda h, i: (0, H + h)),
                  pl.BlockSpec((S, 128), lambda h, i: (0, 2 * H + h)),
                  col, rowv, qblk, col, qblk],
        out_specs=[qblk, kvo, kvo, col, rowv],
        out_shape=[jax.ShapeDtypeStruct((S, FOX_WIDTH), BF16)] * 3
        + [jax.ShapeDtypeStruct((H, S, 1), F32), jax.ShapeDtypeStruct((H, 1, S), F32)],
        scratch_shapes=[pltpu.VMEM((S, 128), F32), pltpu.VMEM((S, 128), F32)],
        compiler_params=_cparams(("parallel", "arbitrary")),
    )(P, P, P, ccol, crow, o, lse, dcat)


def _s5_disc_fwd(lr, li, ls, *, name, after=()):
    G, Pn = lr.shape

    def body(lr_ref, li_ref, ls_ref, ar_ref, ai_ref, gr_ref, gi_ref):
        lr_, li_ = lr_ref[...], li_ref[...]
        dt = jnp.exp(ls_ref[...])
        mag = jnp.exp(lr_ * dt)
        th = li_ * dt
        ar = mag * jnp.cos(th)
        ai = mag * jnp.sin(th)
        den = lr_ * lr_ + li_ * li_
        xr = ar - 1.0
        ar_ref[...] = ar
        ai_ref[...] = ai
        gr_ref[...] = (xr * lr_ + ai * li_) / den
        gi_ref[...] = (ai * lr_ - xr * li_) / den

    sq = pl.BlockSpec((G, Pn), lambda: (0, 0))
    return _call(
        body, after=after, name=name, in_specs=[sq, sq, pl.BlockSpec((G, 1), lambda: (0, 0))], out_specs=[sq] * 4,
        out_shape=[jax.ShapeDtypeStruct((G, Pn), F32)] * 4,
    )(lr, li, ls)


def _s5_disc_bwd(lr, li, ls, dar, dai, dgr, dgi, *, name):
    G, Pn = lr.shape

    def body(lr_ref, li_ref, ls_ref, dar_ref, dai_ref, dgr_ref, dgi_ref, dlr_ref, dli_ref, dls_ref):
        lr_, li_ = lr_ref[...], li_ref[...]
        dt = jnp.exp(ls_ref[...])
        mag = jnp.exp(lr_ * dt)
        th = li_ * dt
        ar = mag * jnp.cos(th)
        ai = mag * jnp.sin(th)
        den = lr_ * lr_ + li_ * li_
        xr = ar - 1.0
        xi = ai
        g_re = (xr * lr_ + xi * li_) / den
        g_im = (xi * lr_ - xr * li_) / den
        dgr_, dgi_ = dgr_ref[...], dgi_ref[...]
        dxr = (dgr_ * lr_ - dgi_ * li_) / den
        dxi = (dgr_ * li_ + dgi_ * lr_) / den
        dden = -(dgr_ * g_re + dgi_ * g_im) / den
        dlr = (dgr_ * xr + dgi_ * xi) / den + 2.0 * dden * lr_
        dli = (dgr_ * xi - dgi_ * xr) / den + 2.0 * dden * li_
        da_r = dar_ref[...] + dxr
        da_i = dai_ref[...] + dxi
        dmag_mag = da_r * ar + da_i * ai
        dth = da_i * ar - da_r * ai
        dlr_ref[...] = dlr + dmag_mag * dt
        dli_ref[...] = dli + dth * dt
        ddt = jnp.sum(dmag_mag * lr_ + dth * li_, axis=-1, keepdims=True)
        dls_ref[...] = ddt * dt

    sq = pl.BlockSpec((G, Pn), lambda: (0, 0))
    c1 = pl.BlockSpec((G, 1), lambda: (0, 0))
    return _call(
        body, name=name, in_specs=[sq, sq, c1, sq, sq, sq, sq], out_specs=[sq, sq, c1],
        out_shape=[jax.ShapeDtypeStruct((G, Pn), F32)] * 2 + [jax.ShapeDtypeStruct((G, 1), F32)],
    )(lr, li, ls, dar, dai, dgr, dgi)


def _s5_bb_fwd(gr, gi, br, bi, *, name):
    R, C = br.shape

    def body(gr_ref, gi_ref, br_ref, bi_ref, or_ref, oi_ref):
        g_r, g_i, b_r, b_i = gr_ref[...], gi_ref[...], br_ref[...], bi_ref[...]
        or_ref[...] = g_r * b_r - g_i * b_i
        oi_ref[...] = g_r * b_i + g_i * b_r

    w = pl.BlockSpec((R, C), lambda: (0, 0))
    c1 = pl.BlockSpec((R, 1), lambda: (0, 0))
    return _call(body, name=name, in_specs=[c1, c1, w, w], out_specs=[w, w],
                 out_shape=[jax.ShapeDtypeStruct((R, C), F32)] * 2)(gr, gi, br, bi)


def _s5_bb_bwd(gr, gi, br, bi, dbbr, dbbi, *, name):
    R, C = br.shape

    def body(gr_ref, gi_ref, br_ref, bi_ref, dr_ref, di_ref, dbr_ref, dbi_ref, dgr_ref, dgi_ref):
        g_r, g_i, b_r, b_i = gr_ref[...], gi_ref[...], br_ref[...], bi_ref[...]
        d_r, d_i = dr_ref[...], di_ref[...]
        dbr_ref[...] = g_r * d_r + g_i * d_i
        dbi_ref[...] = g_r * d_i - g_i * d_r
        dgr_ref[...] = jnp.sum(d_r * b_r + d_i * b_i, axis=-1, keepdims=True)
        dgi_ref[...] = jnp.sum(d_i * b_r - d_r * b_i, axis=-1, keepdims=True)

    w = pl.BlockSpec((R, C), lambda: (0, 0))
    c1 = pl.BlockSpec((R, 1), lambda: (0, 0))
    return _call(body, name=name, in_specs=[c1, c1, w, w, w, w], out_specs=[w, w, c1, c1],
                 out_shape=[jax.ShapeDtypeStruct((R, C), F32)] * 2 + [jax.ShapeDtypeStruct((R, 1), F32)] * 2,
                 )(gr, gi, br, bi, dbbr, dbbi)


_DIAG_TILE = 8


def _diag_mask(gr, gc):
    rows, cols = _DIAG_TILE * gr, _DIAG_TILE * gc
    r = lax.broadcasted_iota(jnp.int32, (rows, cols), 0) >> (gr.bit_length() - 1)
    c = lax.broadcasted_iota(jnp.int32, (rows, cols), 1) >> (gc.bit_length() - 1)
    return r == c


def _diag_expand(t2, gr, gc, *, name, after=()):
    _, R, _ = t2.shape
    G = R // gr
    nt = G // _DIAG_TILE
    rows, cols = _DIAG_TILE * gr, _DIAG_TILE * gc

    def body(t_ref, o_ref):
        src = lax.broadcasted_iota(jnp.int32, (gc, cols), 0)
        dst = lax.broadcasted_iota(jnp.int32, (gc, cols), 1) & (gc - 1)
        spread = (src == dst).astype(BF16)
        y = jnp.dot(t_ref[...].astype(BF16), spread, preferred_element_type=F32)
        o_ref[...] = jnp.where(_diag_mask(gr, gc), y, 0.0).astype(BF16)

    return _call(
        body, after=after, name=name, grid=(2, nt),
        in_specs=[pl.BlockSpec((None, rows, gc), lambda p, i: (p, i, 0))],
        out_specs=pl.BlockSpec((None, rows, cols), lambda p, i: (p, i, i)),
        out_shape=jax.ShapeDtypeStruct((2, R, G * gc), BF16),
        compiler_params=_cparams(("parallel",) * 2),
    )(t2)


def _diag_extract(xd, gr, gc, *, name):
    _, R, _ = xd.shape
    nt = R // gr // _DIAG_TILE
    rows, cols = _DIAG_TILE * gr, _DIAG_TILE * gc

    def body(x_ref, o_ref):
        src = lax.broadcasted_iota(jnp.int32, (cols, gc), 0) & (gc - 1)
        dst = lax.broadcasted_iota(jnp.int32, (cols, gc), 1)
        fold = (src == dst).astype(BF16)
        parts = _split3(jnp.where(_diag_mask(gr, gc), x_ref[...], 0.0))
        acc = jnp.dot(parts[2], fold, preferred_element_type=F32)
        acc = acc + jnp.dot(parts[1], fold, preferred_element_type=F32)
        o_ref[...] = acc + jnp.dot(parts[0], fold, preferred_element_type=F32)

    return _call(
        body, name=name, grid=(2, nt),
        in_specs=[pl.BlockSpec((None, rows, cols), lambda p, i: (p, i, i))],
        out_specs=pl.BlockSpec((None, rows, gc), lambda p, i: (p, i, 0)),
        out_shape=jax.ShapeDtypeStruct((2, R, gc), F32),
        compiler_params=_cparams(("parallel",) * 2),
    )(xd)


SCAN_BLOCK = 8


def _cpowers(ar, ai, sign):
    ai = sign * ai
    out = [(ar, ai)]
    for _ in range(SCAN_BLOCK - 1):
        pr, pi = out[-1]
        out.append((pr * ar - pi * ai, pr * ai + pi * ar))
    return out


def _row_table(pw, row, index_of_row):
    tr_ = jnp.broadcast_to(pw[index_of_row(0)][0], row.shape)
    ti_ = jnp.broadcast_to(pw[index_of_row(0)][1], row.shape)
    for r in range(1, SCAN_BLOCK):
        pr, pi = pw[index_of_row(r)]
        tr_ = jnp.where(row == r, pr, tr_)
        ti_ = jnp.where(row == r, pi, ti_)
    return tr_, ti_


def _s5_scan_fwd(bu, a, *, name):
    _, S, N = bu.shape
    tc = 512
    nt = N // tc

    def body(a_ref, b_ref, h_ref):
        pw = _cpowers(a_ref[0], a_ref[1], 1.0)
        row = lax.broadcasted_iota(jnp.int32, (SCAN_BLOCK, tc), 0)
        lead_r, lead_i = _row_table(pw, row, lambda r: r)

        def step(k, carry):
            cr, ci = carry
            rows = pl.ds(pl.multiple_of(k * SCAN_BLOCK, SCAN_BLOCK), SCAN_BLOCK)
            xr, xi = b_ref[0, rows, :], b_ref[1, rows, :]
            for sh in (1, 2, 4):
                keep = row >= sh
                sr = jnp.where(keep, pltpu.roll(xr, sh, 0), 0.0)
                si = jnp.where(keep, pltpu.roll(xi, sh, 0), 0.0)
                kr, ki = pw[sh - 1]
                xr, xi = xr + kr * sr - ki * si, xi + kr * si + ki * sr
            h_ref[0, rows, :] = xr + lead_r * cr - lead_i * ci
            h_ref[1, rows, :] = xi + lead_r * ci + lead_i * cr
            last = row == SCAN_BLOCK - 1
            tr_ = jnp.sum(jnp.where(last, xr, 0.0), axis=0, keepdims=True)
            ti_ = jnp.sum(jnp.where(last, xi, 0.0), axis=0, keepdims=True)
            a8r, a8i = pw[SCAN_BLOCK - 1]
            return a8r * cr - a8i * ci + tr_, a8r * ci + a8i * cr + ti_

        z = jnp.zeros((1, tc), F32)
        lax.fori_loop(0, S // SCAN_BLOCK, step, (z, z), unroll=2)

    vec = pl.BlockSpec((2, 1, tc), lambda j: (0, 0, j))
    mat = pl.BlockSpec((2, S, tc), lambda j: (0, 0, j))
    return _call(
        body, name=name, grid=(nt,), in_specs=[vec, mat], out_specs=mat,
        out_shape=jax.ShapeDtypeStruct((2, S, N), F32),
        compiler_params=_cparams(("parallel",)),
    )(a, bu)


def _s5_scan_bwd(g, h, a, *, name):
    _, S, N = g.shape
    tc = 256
    nt = N // tc

    def body(a_ref, g_ref, h_ref, l_ref, da_ref):
        pw = _cpowers(a_ref[0], a_ref[1], -1.0)
        row = lax.broadcasted_iota(jnp.int32, (SCAN_BLOCK, tc), 0)
        tail_r, tail_i = _row_table(pw, row, lambda r: SCAN_BLOCK - 1 - r)
        nb = S // SCAN_BLOCK

        def step(i, carry):
            k = nb - 1 - i
            cr, ci, dar, dai = carry
            rows = pl.ds(pl.multiple_of(k * SCAN_BLOCK, SCAN_BLOCK), SCAN_BLOCK)
            xr, xi = g_ref[0, rows, :], g_ref[1, rows, :]
            for sh in (1, 2, 4):
                keep = row < SCAN_BLOCK - sh
                sr = jnp.where(keep, pltpu.roll(xr, SCAN_BLOCK - sh, 0), 0.0)
                si = jnp.where(keep, pltpu.roll(xi, SCAN_BLOCK - sh, 0), 0.0)
                kr, ki = pw[sh - 1]
                xr, xi = xr + kr * sr - ki * si, xi + kr * si + ki * sr
            lr = xr + tail_r * cr - tail_i * ci
            li = xi + tail_r * ci + tail_i * cr
            l_ref[0, rows, :] = lr
            l_ref[1, rows, :] = li
            prev = pl.ds(pl.multiple_of(jnp.maximum(k - 1, 0) * SCAN_BLOCK, SCAN_BLOCK), SCAN_BLOCK)
            has_prev = jnp.where(k > 0, 1.0, 0.0).astype(F32)
            first = row == 0
            hpr = jnp.where(first, pltpu.roll(h_ref[0, prev, :], 1, 0) * has_prev, pltpu.roll(h_ref[0, rows, :], 1, 0))
            hpi = jnp.where(first, pltpu.roll(h_ref[1, prev, :], 1, 0) * has_prev, pltpu.roll(h_ref[1, rows, :], 1, 0))
            tr_ = jnp.sum(jnp.where(first, xr, 0.0), axis=0, keepdims=True)
            ti_ = jnp.sum(jnp.where(first, xi, 0.0), axis=0, keepdims=True)
            a8r, a8i = pw[SCAN_BLOCK - 1]
            return (a8r * cr - a8i * ci + tr_, a8r * ci + a8i * cr + ti_,
                    dar + lr * hpr + li * hpi, dai + li * hpr - lr * hpi)

        z = jnp.zeros((1, tc), F32)
        z8 = jnp.zeros((SCAN_BLOCK, tc), F32)
        _, _, dar, dai = lax.fori_loop(0, nb, step, (z, z, z8, z8), unroll=2)
        da_ref[0] = jnp.sum(dar, axis=0, keepdims=True)
        da_ref[1] = jnp.sum(dai, axis=0, keepdims=True)

    vec = pl.BlockSpec((2, 1, tc), lambda j: (0, 0, j))
    mat = pl.BlockSpec((2, S, tc), lambda j: (0, 0, j))
    return _call(
        body, name=name, grid=(nt,), in_specs=[vec, mat, mat], out_specs=[mat, vec],
        out_shape=[jax.ShapeDtypeStruct((2, S, N), F32), jax.ShapeDtypeStruct((2, 1, N), F32)],
        compiler_params=_cparams(("parallel",)),
    )(a, g, h)


_GELU_C = math.sqrt(2.0 / math.pi)


def _s5_out_fwd(yc, P, dskip, *, name):
    S, W = yc.shape
    tr = _pick(S, 256)
    ub = 3 * FOX_WIDTH // W

    def body(yc_ref, u_ref, d_ref, y_ref, yg_ref):
        y = yc_ref[...] + d_ref[...] * u_ref[...]
        y_ref[...] = y
        t = jnp.tanh(_GELU_C * (y + 0.044715 * y * y * y))
        yg_ref[...] = (0.5 * y * (1.0 + t)).astype(BF16)

    row = pl.BlockSpec((tr, W), lambda i: (i, 0))
    return _call(
        body, name=name, grid=(S // tr,),
        in_specs=[row, pl.BlockSpec((tr, W), lambda i: (i, ub)), pl.BlockSpec((1, W), lambda i: (0, 0))],
        out_specs=[row, row],
        out_shape=[jax.ShapeDtypeStruct((S, W), F32), jax.ShapeDtypeStruct((S, W), BF16)],
        compiler_params=_cparams(("parallel",)),
    )(yc, P, dskip)


def _s5_out_bwd(dyg, y, P, dskip, *, name):
    S, W = y.shape
    tr = _pick(S, 256)
    ub = 3 * FOX_WIDTH // W

    def body(dyg_ref, y_ref, u_ref, d_ref, dy_ref, du_ref, dd_ref):
        y_ = y_ref[...]
        inner = _GELU_C * (y_ + 0.044715 * y_ * y_ * y_)
        t = jnp.tanh(inner)
        dgelu = 0.5 * (1.0 + t) + 0.5 * y_ * (1.0 - t * t) * _GELU_C * (1.0 + 3.0 * 0.044715 * y_ * y_)
        dy = dyg_ref[...] * dgelu
        dy_ref[...] = dy.astype(BF16)
        du_ref[...] = d_ref[...] * dy
        part = jnp.sum(dy * u_ref[...], axis=0, keepdims=True)

        @pl.when(pl.program_id(0) == 0)
        def _():
            dd_ref[...] = part

        @pl.when(pl.program_id(0) > 0)
        def _():
            dd_ref[...] += part

    row = pl.BlockSpec((tr, W), lambda i: (i, 0))
    vec = pl.BlockSpec((1, W), lambda i: (0, 0))
    return _call(
        body, name=name, grid=(S // tr,),
        in_specs=[row, row, pl.BlockSpec((tr, W), lambda i: (i, ub)), vec],
        out_specs=[row, row, vec],
        out_shape=[jax.ShapeDtypeStruct((S, W), BF16), jax.ShapeDtypeStruct((S, W), F32),
                   jax.ShapeDtypeStruct((1, W), F32)],
        compiler_params=_cparams(("arbitrary",)),
    )(dyg, y, P, dskip)


def _glu_fwd(z, *, name):
    S, W2 = z.shape
    W = W2 // 2
    tr = _pick(S, 256)

    def body(z1_ref, z2_ref, o_ref):
        o_ref[...] = (z1_ref[...] * jax.nn.sigmoid(z2_ref[...])).astype(BF16)

    return _call(
        body, name=name, grid=(S // tr,),
        in_specs=[pl.BlockSpec((tr, W), lambda i: (i, 0)), pl.BlockSpec((tr, W), lambda i: (i, 1))],
        out_specs=pl.BlockSpec((tr, W), lambda i: (i, 0)),
        out_shape=jax.ShapeDtypeStruct((S, W), BF16),
        compiler_params=_cparams(("parallel",)),
    )(z, z)


def _glu_bwd(z, dcat, *, name):
    S, W2 = z.shape
    W = W2 // 2
    tr = _pick(S, 256)

    def body(z1_ref, z2_ref, d_ref, dz1_ref, dz2_ref):
        sg = jax.nn.sigmoid(z2_ref[...])
        d = d_ref[...]
        dz1_ref[...] = (d * sg).astype(BF16)
        dz2_ref[...] = (d * z1_ref[...] * sg * (1.0 - sg)).astype(BF16)

    lo = pl.BlockSpec((tr, W), lambda i: (i, 0))
    hi = pl.BlockSpec((tr, W), lambda i: (i, 1))
    dz1, dz2 = _call(
        body, name=name, grid=(S // tr,), in_specs=[lo, hi, hi], out_specs=[lo, lo],
        out_shape=[jax.ShapeDtypeStruct((S, W), BF16)] * 2,
        compiler_params=_cparams(("parallel",)),
    )(z, z, dcat)
    return jnp.concatenate([dz1, dz2], axis=1)


ACT_ROWS = 16
ACT_COLS = 256


def _shift_down(cur, prev, k, row):
    return jnp.where(row >= k, pltpu.roll(cur, k, 0), pltpu.roll(prev, k, 0))


def _shift_up(cur, nxt, k, row):
    n = cur.shape[0]
    return jnp.where(row < n - k, pltpu.roll(cur, n - k, 0), pltpu.roll(nxt, n - k, 0))


def _act_fwd(h, cw, cb, *, name):
    _, S, FP = h.shape
    tr = _pick(S, 256)
    hb = tr // ACT_ROWS
    nq = tr // ACT_ROWS

    def body(g_ref, gh_ref, v_ref, vh_ref, wg_ref, wv_ref, bg_ref, bv_ref, a_ref, hc_ref):
        first = pl.program_id(1) == 0
        for c0 in range(0, FP, ACT_COLS):
            cw_ = min(ACT_COLS, FP - c0)
            cols = pl.ds(c0, cw_)
            rw = lax.broadcasted_iota(jnp.int32, (ACT_ROWS, cw_), 0)
            wg = [wg_ref[pl.ds(k, 1), cols] for k in range(3)]
            wv = [wv_ref[pl.ds(k, 1), cols] for k in range(3)]
            bg, bv = bg_ref[:, cols], bv_ref[:, cols]
            halo_g = jnp.where(first, 0.0, gh_ref[:, cols])
            halo_v = jnp.where(first, 0.0, vh_ref[:, cols])

            def chunk(q, _):
                rows = pl.ds(pl.multiple_of(q * ACT_ROWS, ACT_ROWS), ACT_ROWS)
                before = pl.ds(pl.multiple_of(jnp.maximum(q - 1, 0) * ACT_ROWS, ACT_ROWS), ACT_ROWS)
                g, v = g_ref[rows, cols], v_ref[rows, cols]
                gp = jnp.where(q > 0, g_ref[before, cols], halo_g)
                vp = jnp.where(q > 0, v_ref[before, cols], halo_v)
                cg = bg + wg[2] * g + wg[1] * _shift_down(g, gp, 1, rw) + wg[0] * _shift_down(g, gp, 2, rw)
                cv = bv + wv[2] * v + wv[1] * _shift_down(v, vp, 1, rw) + wv[0] * _shift_down(v, vp, 2, rw)
                a_ref[rows, cols] = (cg * jax.nn.sigmoid(cg) * cv).astype(BF16)
                hc_ref[0, rows, cols] = cg
                hc_ref[1, rows, cols] = cv
                return 0

            lax.fori_loop(0, nq, chunk, 0, unroll=2)

    def main(off):
        return pl.BlockSpec((None, tr, FP), lambda j, i: (j + off, i, 0))

    def halo(off):
        return pl.BlockSpec((None, ACT_ROWS, FP), lambda j, i: (j + off, jnp.maximum(i * hb - 1, 0), 0))

    def wspec(off):
        return pl.BlockSpec((None, 3, FP), lambda j, i: (j + off, 0, 0))

    def bspec(off):
        return pl.BlockSpec((None, 1, FP), lambda j, i: (j + off, 0, 0))

    cb3 = cb.reshape(4, 1, FP)
    return _call(
        body, name=name, grid=(2, S // tr),
        in_specs=[main(0), halo(0), main(2), halo(2), wspec(0), wspec(2), bspec(0), bspec(2)],
        out_specs=[pl.BlockSpec((None, tr, FP), lambda j, i: (j, i, 0)),
                   pl.BlockSpec((None, 2, tr, FP), lambda j, i: (j, 0, i, 0))],
        out_shape=[jax.ShapeDtypeStruct((2, S, FP), BF16), jax.ShapeDtypeStruct((2, 2, S, FP), F32)],
        compiler_params=_cparams(("parallel", "parallel")),
    )(h, h, h, h, cw, cw, cb3, cb3)


def _act_bwd(h, hc, da, cw, *, name):
    _, S, FP = h.shape
    tr = _pick(S, 256)
    nq = tr // ACT_ROWS
    nr = S // tr
    half = ACT_ROWS // 2

    def fold(x):
        return x[:half] + x[half:]

    def body(g_ref, v_ref, hc_ref, da_ref, wg_ref, wv_ref,
             dh_ref, dwg_ref, dwv_ref, dbg_ref, dbv_ref, carry_g, carry_v):
        i = pl.program_id(1)
        bottom = i == 0
        for c0 in range(0, FP, ACT_COLS):
            cw_ = min(ACT_COLS, FP - c0)
            cols = pl.ds(c0, cw_)
            rw = lax.broadcasted_iota(jnp.int32, (ACT_ROWS, cw_), 0)
            wg = [wg_ref[pl.ds(k, 1), cols] for k in range(3)]
            wv = [wv_ref[pl.ds(k, 1), cols] for k in range(3)]
            after_g = jnp.where(bottom, 0.0, carry_g[:, cols])
            after_v = jnp.where(bottom, 0.0, carry_v[:, cols])

            def chunk(s, carry):
                ng, nv, acc = carry[0], carry[1], carry[2:]
                q = nq - 1 - s
                rows = pl.ds(pl.multiple_of(q * ACT_ROWS, ACT_ROWS), ACT_ROWS)
                g, v = g_ref[rows, cols], v_ref[rows, cols]
                cg, cv = hc_ref[0, rows, cols], hc_ref[1, rows, cols]
                sg = jax.nn.sigmoid(cg)
                d = da_ref[rows, cols]
                dcg = d * cv * sg * (1.0 + cg * (1.0 - sg))
                dcv = d * cg * sg
                ug1, ug2 = _shift_up(dcg, ng, 1, rw), _shift_up(dcg, ng, 2, rw)
                uv1, uv2 = _shift_up(dcv, nv, 1, rw), _shift_up(dcv, nv, 2, rw)
                dh_ref[0, rows, cols] = (wg[2] * dcg + wg[1] * ug1 + wg[0] * ug2).astype(BF16)
                dh_ref[1, rows, cols] = (wv[2] * dcv + wv[1] * uv1 + wv[0] * uv2).astype(BF16)
                terms = (ug2 * g, ug1 * g, dcg * g, dcg, uv2 * v, uv1 * v, dcv * v, dcv)
                return (dcg, dcv) + tuple(a + fold(t) for a, t in zip(acc, terms))

            zero = jnp.zeros((half, cw_), F32)
            out = lax.fori_loop(0, nq, chunk, (after_g, after_v) + (zero,) * 8, unroll=2)
            carry_g[:, cols] = out[0]
            carry_v[:, cols] = out[1]
            sums = [jnp.sum(a, axis=0, keepdims=True) for a in out[2:]]

            @pl.when(bottom)
            def _():
                for k in range(3):
                    dwg_ref[pl.ds(k, 1), cols] = sums[k]
                    dwv_ref[pl.ds(k, 1), cols] = sums[4 + k]
                dbg_ref[:, cols] = sums[3]
                dbv_ref[:, cols] = sums[7]

            @pl.when(jnp.logical_not(bottom))
            def _():
                for k in range(3):
                    dwg_ref[pl.ds(k, 1), cols] += sums[k]
                    dwv_ref[pl.ds(k, 1), cols] += sums[4 + k]
                dbg_ref[:, cols] += sums[3]
                dbv_ref[:, cols] += sums[7]

    def main(off):
        return pl.BlockSpec((None, tr, FP), lambda j, i: (j + off, nr - 1 - i, 0))

    def wspec(off):
        return pl.BlockSpec((None, 3, FP), lambda j, i: (j + off, 0, 0))

    bspec = pl.BlockSpec((None, 1, FP), lambda j, i: (j, 0, 0))
    pair = pl.BlockSpec((None, 2, tr, FP), lambda j, i: (j, 0, nr - 1 - i, 0))
    dh, dwg, dwv, dbg, dbv = _call(
        body, name=name, grid=(2, nr),
        in_specs=[main(0), main(2), pair, main(0), wspec(0), wspec(2)],
        out_specs=[pair, wspec(0), wspec(0), bspec, bspec],
        out_shape=[jax.ShapeDtypeStruct((2, 2, S, FP), BF16)]
        + [jax.ShapeDtypeStruct((2, 3, FP), F32)] * 2 + [jax.ShapeDtypeStruct((2, 1, FP), F32)] * 2,
        scratch_shapes=[pltpu.VMEM((ACT_ROWS, FP), F32), pltpu.VMEM((ACT_ROWS, FP), F32)],
        compiler_params=_cparams(("parallel", "arbitrary")),
    )(h, h, hc, da, cw, cw)
    return (dh.reshape(4, S, FP), jnp.concatenate([dwg, dwv], axis=0), jnp.concatenate([dbg, dbv], axis=0))


def _rope_tables(posf, *, name, after=()):
    S = posf.shape[0]
    half = ROPE_DIM // 2
    d = np.arange(LANE) % SWA_HEAD_DIM
    invf = np.where(d < ROPE_DIM, ROPE_THETA ** (-(d % half).astype(np.float64) / half), 0.0).astype(np.float32)
    m_rot = (d < ROPE_DIM).astype(np.float32)
    m_a = (d < half).astype(np.float32)
    m_b = ((d >= half) & (d < ROPE_DIM)).astype(np.float32)
    consts = jnp.asarray(np.stack([invf, m_rot, m_a, m_b] + [np.zeros(LANE, np.float32)] * 4))

    def body(p_ref, k_ref, c_ref, sa_ref, sb_ref):
        k = k_ref[...]
        ang = p_ref[...] * k[0:1]
        co, si = jnp.cos(ang), jnp.sin(ang)
        c_ref[...] = k[1:2] * co + (1.0 - k[1:2])
        sa_ref[...] = -k[2:3] * si
        sb_ref[...] = k[3:4] * si

    full = pl.BlockSpec((S, LANE), lambda: (0, 0))
    return _call(
        body, after=after, name=name,
        in_specs=[pl.BlockSpec((S, 1), lambda: (0, 0)), pl.BlockSpec((8, LANE), lambda: (0, 0))],
        out_specs=[full] * 3, out_shape=[jax.ShapeDtypeStruct((S, LANE), F32)] * 3,
    )(posf, consts)


def _rope_apply(x, tabs, *, col0, width, inverse, name, out_dtype):
    S = x.shape[0]
    tr = _pick(S, 256)
    rep = width // LANE
    cb = col0 // width

    def body(x_ref, c_ref, sa_ref, sb_ref, o_ref):
        xv = x_ref[...].astype(F32)
        c = jnp.tile(c_ref[...], (1, rep))
        sa = jnp.tile(sa_ref[...], (1, rep))
        sb = jnp.tile(sb_ref[...], (1, rep))
        if not inverse:
            out = xv * c + pltpu.roll(xv, width - 8, 1) * sa + pltpu.roll(xv, 8, 1) * sb
        else:
            out = xv * c + pltpu.roll(xv * sa, 8, 1) + pltpu.roll(xv * sb, width - 8, 1)
        o_ref[...] = out.astype(out_dtype)

    tab = pl.BlockSpec((tr, LANE), lambda i: (i, 0))
    return _call(
        body, name=name, grid=(S // tr,),
        in_specs=[pl.BlockSpec((tr, width), lambda i: (i, cb)), tab, tab, tab],
        out_specs=pl.BlockSpec((tr, width), lambda i: (i, 0)),
        out_shape=jax.ShapeDtypeStruct((S, width), out_dtype),
        compiler_params=_cparams(("parallel",)),
    )(x, *tabs)


def _swa_mask(n):
    rows = SWA_GROUPS * SWA_WINDOW
    qi = lax.broadcasted_iota(jnp.int32, (rows, 2 * SWA_WINDOW), 0) & (SWA_WINDOW - 1)
    kj = lax.broadcasted_iota(jnp.int32, (rows, 2 * SWA_WINDOW), 1)
    rel = SWA_WINDOW + qi - kj
    return (rel >= 0) & (rel < SWA_WINDOW) & ((n > 0) | (kj >= SWA_WINDOW))


def _swa_fwd(qT, kT, vT, sink_rows, *, name):
    S = qT.shape[1]
    W, G, Dh = SWA_WINDOW, SWA_GROUPS, SWA_HEAD_DIM
    nb = S // W
    scale = 1.0 / math.sqrt(Dh)

    def body(q_ref, kp_ref, kc_ref, vp_ref, vc_ref, s_ref, o_ref, l_ref):
        n = pl.program_id(1)
        q = q_ref[...].reshape(G * W, Dh)
        kk = jnp.concatenate([kp_ref[...], kc_ref[...]], axis=0)
        vv = jnp.concatenate([vp_ref[...], vc_ref[...]], axis=0)
        s = lax.dot_general(q, kk, (((1,), (1,)), ((), ())), preferred_element_type=F32) * scale
        s = jnp.where(_swa_mask(n), s, -1e30)
        sink = s_ref[...]
        m = jnp.maximum(jnp.max(s, axis=-1, keepdims=True), sink)
        e = jnp.exp(s - m)
        den = jnp.sum(e, axis=-1, keepdims=True) + jnp.exp(sink - m)
        p = e / den
        o_ref[...] = jnp.dot(p.astype(BF16), vv, preferred_element_type=F32).reshape(G, W, Dh)
        l_ref[...] = (m + jnp.log(den)).reshape(G, W, 1)

    qs = pl.BlockSpec((G, W, Dh), lambda g, n: (g, n, 0))
    prev = pl.BlockSpec((None, W, Dh), lambda g, n: (g, jnp.maximum(n - 1, 0), 0))
    cur = pl.BlockSpec((None, W, Dh), lambda g, n: (g, n, 0))
    return _call(
        body, name=name, grid=(SWA_KV_HEADS, nb),
        in_specs=[qs, prev, cur, prev, cur, pl.BlockSpec((None, G * W, 1), lambda g, n: (g, 0, 0))],
        out_specs=[qs, pl.BlockSpec((G, W, 1), lambda g, n: (g, n, 0))],
        out_shape=[jax.ShapeDtypeStruct((SWA_HEADS, S, Dh), F32), jax.ShapeDtypeStruct((SWA_HEADS, S, 1), F32)],
        compiler_params=_cparams(("parallel", "parallel")),
    )(qT, kT, kT, vT, vT, sink_rows)


def _swa_bwd(qT, kT, vT, sink_rows, oT, L, doT, *, name):
    S = qT.shape[1]
    W, G, Dh = SWA_WINDOW, SWA_GROUPS, SWA_HEAD_DIM
    nb = S // W
    scale = 1.0 / math.sqrt(Dh)

    def body(q_ref, kp_ref, kc_ref, vp_ref, vc_ref, s_ref, o_ref, l_ref, do_ref,
             dq_ref, dk_ref, dv_ref, ds_ref):
        n = pl.program_id(1)
        q = q_ref[...].reshape(G * W, Dh)
        kk = jnp.concatenate([kp_ref[...], kc_ref[...]], axis=0)
        vv = jnp.concatenate([vp_ref[...], vc_ref[...]], axis=0)
        s = lax.dot_general(q, kk, (((1,), (1,)), ((), ())), preferred_element_type=F32) * scale
        lrow = l_ref[...].reshape(G * W, 1)
        p = jnp.where(_swa_mask(n), jnp.exp(s - lrow), 0.0)
        do = do_ref[...].reshape(G * W, Dh)
        do_bf = do.astype(BF16)
        dp = lax.dot_general(do_bf, vv, (((1,), (1,)), ((), ())), preferred_element_type=F32)
        delta = jnp.sum(do * o_ref[...].reshape(G * W, Dh), axis=-1, keepdims=True)
        dsc = p * (dp - delta)
        ds_bf = dsc.astype(BF16)
        dq_ref[...] = (jnp.dot(ds_bf, kk, preferred_element_type=F32) * scale).astype(BF16).reshape(G, W, Dh)
        dkk = lax.dot_general(ds_bf, q, (((0,), (0,)), ((), ())), preferred_element_type=F32) * scale
        dvv = lax.dot_general(p.astype(BF16), do_bf, (((0,), (0,)), ((), ())), preferred_element_type=F32)
        dsk = -jnp.exp(s_ref[...] - lrow) * delta
        dsk = jnp.broadcast_to(jnp.sum(dsk.reshape(G, W, 1), axis=1), (G, LANE))

        @pl.when(n == 0)
        def _():
            dk_ref[...] = jnp.zeros_like(dk_ref)
            dv_ref[...] = jnp.zeros_like(dv_ref)
            ds_ref[...] = jnp.zeros_like(ds_ref)

        rows = pl.ds(pl.multiple_of(n * W, W), 2 * W)
        dk_ref[rows, :] += dkk
        dv_ref[rows, :] += dvv
        ds_ref[...] += dsk

    qs = pl.BlockSpec((G, W, Dh), lambda g, n: (g, n, 0))
    prev = pl.BlockSpec((None, W, Dh), lambda g, n: (g, jnp.maximum(n - 1, 0), 0))
    cur = pl.BlockSpec((None, W, Dh), lambda g, n: (g, n, 0))
    lsp = pl.BlockSpec((G, W, 1), lambda g, n: (g, n, 0))
    kvo = pl.BlockSpec((None, S + W, Dh), lambda g, n: (g, 0, 0))
    return _call(
        body, name=name, grid=(SWA_KV_HEADS, nb),
        in_specs=[qs, prev, cur, prev, cur, pl.BlockSpec((None, G * W, 1), lambda g, n: (g, 0, 0)), qs, lsp, qs],
        out_specs=[qs, kvo, kvo, pl.BlockSpec((None, G, LANE), lambda g, n: (g, 0, 0))],
        out_shape=[jax.ShapeDtypeStruct((SWA_HEADS, S, Dh), BF16),
                   jax.ShapeDtypeStruct((SWA_KV_HEADS, S + W, Dh), F32),
                   jax.ShapeDtypeStruct((SWA_KV_HEADS, S + W, Dh), F32),
                   jax.ShapeDtypeStruct((SWA_KV_HEADS, G, LANE), F32)],
        compiler_params=_cparams(("parallel", "arbitrary")),
    )(qT, kT, kT, vT, vT, sink_rows, oT, L, doT)


def _adamw(w, g, m, v, *, name, tr=128, by_cols=False):
    L, R, C = w.shape
    split = isinstance(g, (list, tuple))
    HR, HC = _half_shape(R, C, by_cols) if split else (R, C)
    tr, tc = _tile2d(HR, HC, tr)
    nr, nc = HR // tr, HC // tc
    c1 = 1.0 / (1.0 - ADAM_B1 ** ADAM_STEP)
    c2 = 1.0 / (1.0 - ADAM_B2 ** ADAM_STEP)
    ng = 2 * L if split else 1

    def body(c_ref, *refs):
        w_ref, g_refs, (m_ref, v_ref, go_ref, d_ref, mo_ref, vo_ref) = refs[0], refs[1:1 + ng], refs[1 + ng:]
        if split:
            mine = pl.program_id(1) == c_ref[0]
            g_ = jnp.where(mine, g_refs[0][...], g_refs[1][...])
            for l in range(1, L):
                g_ = jnp.where(pl.program_id(0) == l,
                               jnp.where(mine, g_refs[2 * l][...], g_refs[2 * l + 1][...]), g_)
        else:
            g_ = g_refs[0][...]
        mn = ADAM_B1 * m_ref[...] + (1.0 - ADAM_B1) * g_
        vn = ADAM_B2 * v_ref[...] + (1.0 - ADAM_B2) * (g_ * g_)
        go_ref[...] = g_
        mo_ref[...] = mn
        vo_ref[...] = vn
        d_ref[...] = -ADAM_LR * ((mn * c1) / (jnp.sqrt(vn * c2) + ADAM_EPS) + ADAM_WD * w_ref[...])

    def whole(l, hf, i, j, c):
        return (l, i, hf * nc + j) if by_cols else (l, hf * nr + i, j)

    def half(layer, own):
        def index(l, hf, i, j, c):
            used = (l == layer) & ((hf == c[0]) if own else (hf != c[0]))
            return jnp.where(used, i, 0), jnp.where(used, j, 0)
        return pl.BlockSpec((tr, tc), index)

    row = pl.BlockSpec((None, tr, tc), whole)
    gs = [h for pair in g for h in pair] if split else [g]
    g_specs = [half(l, own) for l in range(L) for own in (True, False)] if split else [row]
    core = lax.axis_index("c").astype(jnp.int32).reshape(1)
    return _call(
        body, name=name,
        grid_spec=pltpu.PrefetchScalarGridSpec(
            num_scalar_prefetch=1, grid=(L, 2 if split else 1, nr, nc),
            in_specs=[row] + g_specs + [row, row], out_specs=[row] * 4),
        out_shape=[jax.ShapeDtypeStruct((L, R, C), F32)] * 4,
        compiler_params=_cparams(("parallel",) * 4),
    )(core, w, *gs, m, v)


def _adamw_half(w, g, m, v, *, name, own, prev=None, tr=128, by_cols=False):
    L, R, C = w.shape
    HR, HC = _half_shape(R, C, by_cols)
    tr, tc = _tile2d(HR, HC, tr)
    nr, nc = HR // tr, HC // tc
    c1 = 1.0 / (1.0 - ADAM_B1 ** ADAM_STEP)
    c2 = 1.0 / (1.0 - ADAM_B2 ** ADAM_STEP)

    def body(c_ref, *refs):
        w_ref, g_refs, m_ref, v_ref = refs[0], refs[1:1 + L], refs[1 + L], refs[2 + L]
        go_ref, d_ref, mo_ref, vo_ref = refs[-4:]
        g_ = g_refs[0][...]
        for l in range(1, L):
            g_ = jnp.where(pl.program_id(0) == l, g_refs[l][...], g_)
        mn = ADAM_B1 * m_ref[...] + (1.0 - ADAM_B1) * g_
        vn = ADAM_B2 * v_ref[...] + (1.0 - ADAM_B2) * (g_ * g_)
        go_ref[...] = g_
        mo_ref[...] = mn
        vo_ref[...] = vn
        d_ref[...] = -ADAM_LR * ((mn * c1) / (jnp.sqrt(vn * c2) + ADAM_EPS) + ADAM_WD * w_ref[...])

    def whole(l, i, j, c):
        hf = c[0] if own else 1 - c[0]
        return (l, i, hf * nc + j) if by_cols else (l, hf * nr + i, j)

    def layer_half(layer):
        def index(l, i, j, c):
            return jnp.where(l == layer, i, 0), jnp.where(l == layer, j, 0)
        return pl.BlockSpec((tr, tc), index)

    row = pl.BlockSpec((None, tr, tc), whole)
    core = lax.axis_index("c").astype(jnp.int32).reshape(1)
    prev = list(prev) if prev is not None else []
    return _call(
        body, name=name,
        grid_spec=pltpu.PrefetchScalarGridSpec(
            num_scalar_prefetch=1, grid=(L, nr, nc),
            in_specs=[row] + [layer_half(l) for l in range(L)] + [row, row] + [ANY] * len(prev),
            out_specs=[row] * 4),
        out_shape=[jax.ShapeDtypeStruct((L, R, C), F32)] * 4,
        input_output_aliases={4 + L + k: k for k in range(len(prev))},
        compiler_params=_cparams(("parallel",) * 3),
    )(core, w, *g, m, v, *prev)


def _sum2_halves(g4, s4, by_cols, *, name):
    n, R, C = g4.shape
    HR, HC = _half_shape(R, C, by_cols)
    tr, tc = _tile2d(HR, HC, budget=1024 * 1024)
    nr, nc = HR // tr, HC // tc
    core = lax.axis_index("c").astype(jnp.int32).reshape(1)

    def body(c_ref, g_ref, s_ref, o_ref):
        o_ref[...] = (g_ref[...].astype(F32) + s_ref[...].astype(F32)).astype(BF16)

    def mine(k, i, j, c):
        return (k, i, c[0] * nc + j) if by_cols else (k, c[0] * nr + i, j)

    blk = pl.BlockSpec((None, tr, tc), lambda k, i, j, c: (k, i, j))
    return _call(
        body, name=name,
        grid_spec=pltpu.PrefetchScalarGridSpec(
            num_scalar_prefetch=1, grid=(n, nr, nc),
            in_specs=[pl.BlockSpec((None, tr, tc), mine), blk], out_specs=blk),
        out_shape=jax.ShapeDtypeStruct((n, HR, HC), BF16),
        compiler_params=_cparams(("parallel", "parallel", "parallel")),
    )(core, g4, s4)


def _rowsum(parts, *, name, out_dtype=F32):
    n, R, C = parts.shape
    tr, tc = _tile2d(R, C, budget=512 * 1024)

    def body(p_ref, o_ref):
        acc = p_ref[0].astype(F32)
        for i in range(1, n):
            acc = acc + p_ref[i].astype(F32)
        o_ref[...] = acc.astype(out_dtype)

    return _call(
        body, name=name, grid=(R // tr, C // tc),
        in_specs=[pl.BlockSpec((n, tr, tc), lambda i, j: (0, i, j))],
        out_specs=pl.BlockSpec((tr, tc), lambda i, j: (i, j)),
        out_shape=jax.ShapeDtypeStruct((R, C), out_dtype),
        compiler_params=_cparams(("parallel", "parallel")),
    )(parts)


def _where_am_i():
    x, y, c = lax.axis_index("x"), lax.axis_index("y"), lax.axis_index("c")
    chips = [(1 - x, y), (x, 1 - y), (1 - x, 1 - y)]
    return x, y, c, chips


def _half_idx(rows, cols, by_cols, which):
    if by_cols:
        hc = cols // 2
        return (slice(None), pl.ds(pl.multiple_of(which * hc, LANE), hc))
    hr = rows // 2
    return (pl.ds(pl.multiple_of(which * hr, 16), hr), slice(None))


def _half_shape(rows, cols, by_cols):
    return (rows, cols // 2) if by_cols else (rows // 2, cols)


def _all_gather_shards(shards, by_cols, *, name):
    n = len(shards)

    def body(*refs):
        ins, outs = refs[:n], refs[n:2 * n]
        send, recv = refs[2 * n:]
        x, y, c, chips = _where_am_i()
        me = 2 * x + y
        sibling = (x, y, 1 - c)

        def half(i, which):
            return _half_idx(*shards[i].shape, by_cols[i], which)

        def cp(i, k, src, dst, to):
            return pltpu.make_async_remote_copy(src_ref=src, dst_ref=dst, send_sem=send.at[i, k],
                                                recv_sem=recv.at[i, k], device_id=to, device_id_type=MESH)

        first = []
        for i in range(n):
            for k, (px, py) in enumerate(chips):
                d = cp(i, k, ins[i].at[half(i, c)], outs[i].at[(me,) + half(i, c)], (px, py, c))
                d.start()
                first.append(d)
        passed = []
        for i in range(n):
            for k, (px, py) in enumerate(chips):
                blk = outs[i].at[(2 * px + py,) + half(i, c)]
                cp(i, k, blk, blk, (px, py, c)).wait_recv()
                d = cp(i, 3 + k, blk, blk, sibling)
                d.start()
                passed.append(d)
        for i in range(n):
            for k, (px, py) in enumerate(chips):
                blk = outs[i].at[(2 * px + py,) + half(i, 1 - c)]
                cp(i, 3 + k, blk, blk, sibling).wait_recv()
        for d in first + passed:
            d.wait_send()

    got = _call(
        body, name=name, in_specs=[ANY] * n, out_specs=[ANY] * n,
        out_shape=[jax.ShapeDtypeStruct((N_CHIPS,) + s.shape, s.dtype) for s in shards],
        scratch_shapes=[pltpu.SemaphoreType.DMA((n, 6)), pltpu.SemaphoreType.DMA((n, 6))],
    )(*shards)
    me = 2 * lax.axis_index("x") + lax.axis_index("y")
    return [lax.dynamic_update_slice_in_dim(g, s[None], me, axis=0) for g, s in zip(got, shards)]


HBM_SPEC = pl.BlockSpec(memory_space=pltpu.HBM)
SEM_SPEC = pl.BlockSpec(memory_space=pltpu.SEMAPHORE)
DATAFLOW = pltpu.SideEffectType.DATAFLOW_SIDE_EFFECTING


def _chip_exchange_refs(kind, shards_shape, by_cols, src, land, i, chip_k, c, me):
    if kind == 'gather':
        half = _half_idx(*shards_shape, by_cols, c)
        return src.at[half], land.at[(me,) + half], land.at[(chip_k,) + half]
    return src.at[chip_k], land.at[me], land.at[chip_k]


def _chip_exchange_start(kind, srcs, by_cols, *, name, after=()):
    n = len(srcs)
    land_shapes = [((N_CHIPS,) + s.shape) if kind == 'gather' else s.shape for s in srcs]

    def body(*refs):
        src_refs, land_refs = refs[:n], refs[n:2 * n]
        send, recv = refs[2 * n + len(after)], refs[2 * n + len(after) + 1]
        token = refs[-1]
        x, y, c, chips = _where_am_i()
        me = 2 * x + y
        for i in range(n):
            for k, (px, py) in enumerate(chips):
                s, d, _ = _chip_exchange_refs(kind, srcs[i].shape, by_cols[i], src_refs[i], land_refs[i], i,
                                              2 * px + py, c, me)
                pltpu.make_async_remote_copy(src_ref=s, dst_ref=d, send_sem=send.at[3 * i + k],
                                             recv_sem=recv.at[3 * i + k], device_id=(px, py, c),
                                             device_id_type=MESH).start()
        token[...] = jnp.zeros_like(token)

    lands = [pltpu.with_memory_space_constraint(lax.empty(sh, s.dtype), pltpu.HBM) for sh, s in zip(land_shapes, srcs)]
    outs = _call(
        body, name=name,
        out_shape=(pltpu.SemaphoreType.DMA((3 * n,)), pltpu.SemaphoreType.DMA((3 * n,)),
                   *[pltpu.HBM(s.shape, s.dtype) for s in srcs],
                   *[pltpu.HBM(sh, s.dtype) for sh, s in zip(land_shapes, srcs)],
                   jax.ShapeDtypeStruct((8, LANE), F32)),
        in_specs=[HBM_SPEC] * (2 * n) + [ANY] * len(after),
        out_specs=(SEM_SPEC, SEM_SPEC, *([HBM_SPEC] * (2 * n)), pl.BlockSpec(memory_space=pltpu.VMEM)),
        input_output_aliases={j: 2 + j for j in range(2 * n)},
        compiler_params=pltpu.CompilerParams(has_side_effects=DATAFLOW),
    )(*[pltpu.with_memory_space_constraint(s, pltpu.HBM) for s in srcs], *lands, *after)
    return outs[0], outs[1], list(outs[2:2 + n]), list(outs[2 + n:2 + 2 * n]), outs[-1]


def _chip_exchange_wait(kind, send, recv, srcs, lands, by_cols, after, *, name):
    n = len(srcs)

    def body(*refs):
        src_refs, land_refs = refs[:n], refs[n:2 * n]
        send_r, recv_r = refs[2 * n], refs[2 * n + 1]
        x, y, c, chips = _where_am_i()
        me = 2 * x + y
        for i in range(n):
            for k, (px, py) in enumerate(chips):
                s, _, d = _chip_exchange_refs(kind, srcs[i].shape, by_cols[i], src_refs[i], land_refs[i], i,
                                              2 * px + py, c, me)
                cp = pltpu.make_async_remote_copy(src_ref=s, dst_ref=d, send_sem=send_r.at[3 * i + k],
                                                  recv_sem=recv_r.at[3 * i + k], device_id=(px, py, c),
                                                  device_id_type=MESH)
                cp.wait_send()
                cp.wait_recv()

    outs = _call(
        body, name=name,
        out_shape=(*[pltpu.HBM(s.shape, s.dtype) for s in srcs], *[pltpu.HBM(l.shape, l.dtype) for l in lands]),
        in_specs=[HBM_SPEC] * (2 * n) + [SEM_SPEC, SEM_SPEC] + [ANY] * len(after),
        out_specs=tuple([HBM_SPEC] * (2 * n)),
        input_output_aliases={j: j for j in range(2 * n)},
        compiler_params=pltpu.CompilerParams(has_side_effects=DATAFLOW),
    )(*srcs, *lands, send, recv, *after)
    return list(outs[:n]), list(outs[n:])


def _sibling_halves_start(grads, by_cols, *, name, after=()):
    n = len(grads)
    land_shapes = [(N_CHIPS,) + _half_shape(*g.shape[1:], bc) for g, bc in zip(grads, by_cols)]

    def body(*refs):
        src_refs, land_refs = refs[:n], refs[n:2 * n]
        send, recv = refs[2 * n + len(after)], refs[2 * n + len(after) + 1]
        token = refs[-1]
        x, y, c, _ = _where_am_i()
        for i in range(n):
            src = src_refs[i].at[(slice(None),) + _half_idx(*grads[i].shape[1:], by_cols[i], 1 - c)]
            pltpu.make_async_remote_copy(src_ref=src, dst_ref=land_refs[i], send_sem=send.at[i], recv_sem=recv.at[i],
                                         device_id=(x, y, 1 - c), device_id_type=MESH).start()
        token[...] = jnp.zeros_like(token)

    lands = [pltpu.with_memory_space_constraint(lax.empty(sh, g.dtype), pltpu.HBM) for sh, g in zip(land_shapes, grads)]
    outs = _call(
        body, name=name,
        out_shape=(pltpu.SemaphoreType.DMA((n,)), pltpu.SemaphoreType.DMA((n,)),
                   *[pltpu.HBM(g.shape, g.dtype) for g in grads],
                   *[pltpu.HBM(sh, g.dtype) for sh, g in zip(land_shapes, grads)],
                   jax.ShapeDtypeStruct((8, LANE), F32)),
        in_specs=[HBM_SPEC] * (2 * n) + [ANY] * len(after),
        out_specs=(SEM_SPEC, SEM_SPEC, *([HBM_SPEC] * (2 * n)), pl.BlockSpec(memory_space=pltpu.VMEM)),
        input_output_aliases={j: 2 + j for j in range(2 * n)},
        compiler_params=pltpu.CompilerParams(has_side_effects=DATAFLOW),
    )(*[pltpu.with_memory_space_constraint(g, pltpu.HBM) for g in grads], *lands, *after)
    return outs[0], outs[1], list(outs[2:2 + n]), list(outs[2 + n:2 + 2 * n]), outs[-1]


def _sibling_halves_wait(send, recv, grads, lands, by_cols, after, *, name):
    n = len(grads)

    def body(*refs):
        src_refs, land_refs = refs[:n], refs[n:2 * n]
        send_r, recv_r = refs[2 * n], refs[2 * n + 1]
        x, y, c, _ = _where_am_i()
        for i in range(n):
            src = src_refs[i].at[(slice(None),) + _half_idx(*grads[i].shape[1:], by_cols[i], 1 - c)]
            cp = pltpu.make_async_remote_copy(src_ref=src, dst_ref=land_refs[i], send_sem=send_r.at[i],
                                              recv_sem=recv_r.at[i], device_id=(x, y, 1 - c), device_id_type=MESH)
            cp.wait_send()
            cp.wait_recv()

    outs = _call(
        body, name=name,
        out_shape=(*[pltpu.HBM(g.shape, g.dtype) for g in grads], *[pltpu.HBM(l.shape, l.dtype) for l in lands]),
        in_specs=[HBM_SPEC] * (2 * n) + [SEM_SPEC, SEM_SPEC] + [ANY] * len(after),
        out_specs=tuple([HBM_SPEC] * (2 * n)),
        input_output_aliases={j: j for j in range(2 * n)},
        compiler_params=pltpu.CompilerParams(has_side_effects=DATAFLOW),
    )(*grads, *lands, send, recv, *after)
    return list(outs[:n]), list(outs[n:])


def _sibling_swap_start(arrs, *, name, after=()):
    n = len(arrs)

    def body(*refs):
        src_refs, land_refs = refs[:n], refs[n:2 * n]
        send, recv = refs[2 * n + len(after)], refs[2 * n + len(after) + 1]
        token = refs[-1]
        x, y, c, _ = _where_am_i()
        for i in range(n):
            pltpu.make_async_remote_copy(src_ref=src_refs[i], dst_ref=land_refs[i], send_sem=send.at[i],
                                         recv_sem=recv.at[i], device_id=(x, y, 1 - c), device_id_type=MESH).start()
        token[...] = jnp.zeros_like(token)

    lands = [pltpu.with_memory_space_constraint(lax.empty(a.shape, a.dtype), pltpu.HBM) for a in arrs]
    outs = _call(
        body, name=name,
        out_shape=(pltpu.SemaphoreType.DMA((n,)), pltpu.SemaphoreType.DMA((n,)),
                   *[pltpu.HBM(a.shape, a.dtype) for a in arrs] * 2, jax.ShapeDtypeStruct((8, LANE), F32)),
        in_specs=[HBM_SPEC] * (2 * n) + [ANY] * len(after),
        out_specs=(SEM_SPEC, SEM_SPEC, *([HBM_SPEC] * (2 * n)), pl.BlockSpec(memory_space=pltpu.VMEM)),
        input_output_aliases={j: 2 + j for j in range(2 * n)},
        compiler_params=pltpu.CompilerParams(has_side_effects=DATAFLOW),
    )(*[pltpu.with_memory_space_constraint(a, pltpu.HBM) for a in arrs], *lands, *after)
    return outs[0], outs[1], list(outs[2:2 + n]), list(outs[2 + n:2 + 2 * n]), outs[-1]


def _sibling_swap_wait(send, recv, arrs, lands, after, *, name):
    n = len(arrs)

    def body(*refs):
        src_refs, land_refs = refs[:n], refs[n:2 * n]
        send_r, recv_r = refs[2 * n], refs[2 * n + 1]
        x, y, c, _ = _where_am_i()
        for i in range(n):
            cp = pltpu.make_async_remote_copy(src_ref=src_refs[i], dst_ref=land_refs[i], send_sem=send_r.at[i],
                                              recv_sem=recv_r.at[i], device_id=(x, y, 1 - c), device_id_type=MESH)
            cp.wait_send()
            cp.wait_recv()

    outs = _call(
        body, name=name,
        out_shape=tuple(pltpu.HBM(a.shape, a.dtype) for a in list(arrs) + list(lands)),
        in_specs=[HBM_SPEC] * (2 * n) + [SEM_SPEC, SEM_SPEC] + [ANY] * len(after),
        out_specs=tuple([HBM_SPEC] * (2 * n)),
        input_output_aliases={j: j for j in range(2 * n)},
        compiler_params=pltpu.CompilerParams(has_side_effects=DATAFLOW),
    )(*arrs, *lands, send, recv, *after)
    return list(outs[:n]), list(outs[n:])


def _sibling_pass_gathered(lands, shard_shapes, by_cols, *, name):
    n = len(lands)

    def body(*refs):
        outs = refs[n:2 * n]
        send, recv = refs[2 * n:]
        x, y, c, chips = _where_am_i()
        sibling = (x, y, 1 - c)
        cps = []
        for i in range(n):
            for k, (px, py) in enumerate(chips):
                blk = outs[i].at[(2 * px + py,) + _half_idx(*shard_shapes[i], by_cols[i], c)]
                d = pltpu.make_async_remote_copy(src_ref=blk, dst_ref=blk, send_sem=send.at[i, k],
                                                 recv_sem=recv.at[i, k], device_id=sibling, device_id_type=MESH)
                d.start()
                cps.append(d)
        for i in range(n):
            for k, (px, py) in enumerate(chips):
                blk = outs[i].at[(2 * px + py,) + _half_idx(*shard_shapes[i], by_cols[i], 1 - c)]
                pltpu.make_async_remote_copy(src_ref=blk, dst_ref=blk, send_sem=send.at[i, k], recv_sem=recv.at[i, k],
                                             device_id=sibling, device_id_type=MESH).wait_recv()
        for d in cps:
            d.wait_send()

    return _call(
        body, name=name, in_specs=[ANY] * n, out_specs=[ANY] * n,
        out_shape=[jax.ShapeDtypeStruct(l.shape, l.dtype) for l in lands],
        input_output_aliases={j: j for j in range(n)},
        scratch_shapes=[pltpu.SemaphoreType.DMA((n, 3)), pltpu.SemaphoreType.DMA((n, 3))],
    )(*lands)


def _own_slot(lands, owns):
    me = 2 * lax.axis_index("x") + lax.axis_index("y")
    return [lax.dynamic_update_slice_in_dim(g, s, me, axis=0) for g, s in zip(lands, owns)]


def _sibling_send_halves(grads, by_cols, *, name):
    n = len(grads)

    def body(*refs):
        ins, outs = refs[:n], refs[n:2 * n]
        send, recv = refs[2 * n:]
        x, y, c, _ = _where_am_i()
        sibling = (x, y, 1 - c)
        cps = []
        for i in range(n):
            src = ins[i].at[(slice(None),) + _half_idx(*grads[i].shape[1:], by_cols[i], 1 - c)]
            d = pltpu.make_async_remote_copy(src_ref=src, dst_ref=outs[i], send_sem=send.at[i],
                                             recv_sem=recv.at[i], device_id=sibling, device_id_type=MESH)
            d.start()
            cps.append(d)
        for d in cps:
            d.wait()

    return _call(
        body, name=name, in_specs=[ANY] * n, out_specs=[ANY] * n,
        out_shape=[jax.ShapeDtypeStruct((N_CHIPS,) + _half_shape(*g.shape[1:], bc), g.dtype)
                   for g, bc in zip(grads, by_cols)],
        scratch_shapes=[pltpu.SemaphoreType.DMA((n,)), pltpu.SemaphoreType.DMA((n,))],
    )(*grads)


def _scatter_to_chips(parts, *, name):
    n = len(parts)

    def body(*refs):
        ins, outs = refs[:n], refs[n:2 * n]
        send, recv = refs[2 * n:]
        x, y, c, chips = _where_am_i()
        me = 2 * x + y
        cps = []
        for i in range(n):
            for k, (px, py) in enumerate(chips):
                d = pltpu.make_async_remote_copy(
                    src_ref=ins[i].at[2 * px + py], dst_ref=outs[i].at[me], send_sem=send.at[i, k],
                    recv_sem=recv.at[i, k], device_id=(px, py, c), device_id_type=MESH)
                d.start()
                cps.append((d, i, k, px, py))
        for d, i, k, px, py in cps:
            blk = outs[i].at[2 * px + py]
            pltpu.make_async_remote_copy(src_ref=blk, dst_ref=blk, send_sem=send.at[i, k], recv_sem=recv.at[i, k],
                                         device_id=(px, py, c), device_id_type=MESH).wait_recv()
        for d, *_ in cps:
            d.wait_send()

    got = _call(
        body, name=name, in_specs=[ANY] * n, out_specs=[ANY] * n,
        out_shape=[jax.ShapeDtypeStruct(p.shape, p.dtype) for p in parts],
        scratch_shapes=[pltpu.SemaphoreType.DMA((n, 3)), pltpu.SemaphoreType.DMA((n, 3))],
    )(*parts)
    me = 2 * lax.axis_index("x") + lax.axis_index("y")
    return [lax.dynamic_update_slice_in_dim(g, lax.dynamic_slice_in_dim(p, me, 1, axis=0), me, axis=0)
            for g, p in zip(got, parts)]


def _sibling_join_halves(halves, *, name):
    n = len(halves)

    def body(*refs):
        ins, outs = refs[:n], refs[n:2 * n]
        send, recv = refs[2 * n:]
        x, y, c, _ = _where_am_i()
        sibling = (x, y, 1 - c)
        cps = []
        for i in range(n):
            d = pltpu.make_async_remote_copy(src_ref=ins[i], dst_ref=outs[i], send_sem=send.at[i],
                                             recv_sem=recv.at[i], device_id=sibling, device_id_type=MESH)
            d.start()
            cps.append(d)
        for d in cps:
            d.wait()

    return _call(
        body, name=name, in_specs=[ANY] * n, out_specs=[ANY] * n,
        out_shape=[jax.ShapeDtypeStruct(h.shape, h.dtype) for h in halves],
        scratch_shapes=[pltpu.SemaphoreType.DMA((n,)), pltpu.SemaphoreType.DMA((n,))],
    )(*halves)


def _all_reduce_small(v, *, name, after=()):
    R, C = v.shape
    H = R // 2

    def body(v_ref, o_ref, sib, slots, send, recv):
        x, y, c, chips = _where_am_i()
        me = 2 * x + y
        sibling = (x, y, 1 - c)
        mine = pl.ds(pl.multiple_of(c * H, 8), H)
        other = pl.ds(pl.multiple_of((1 - c) * H, 8), H)

        def copy(k, src, dst, to):
            return pltpu.make_async_remote_copy(src_ref=src, dst_ref=dst, send_sem=send.at[k], recv_sem=recv.at[k],
                                                device_id=to, device_id_type=MESH)

        d = copy(0, v_ref.at[other], sib, sibling)
        d.start()
        d.wait()
        slots[me] = v_ref[mine, :] + sib[...]
        cps = [copy(1 + k, slots.at[me], slots.at[me], (px, py, c)) for k, (px, py) in enumerate(chips)]
        for d in cps:
            d.start()
        for k, (px, py) in enumerate(chips):
            blk = slots.at[2 * px + py]
            copy(1 + k, blk, blk, (px, py, c)).wait_recv()
        for d in cps:
            d.wait_send()
        o_ref[mine, :] = (slots[0] + slots[1]) + (slots[2] + slots[3])
        d = copy(4, o_ref.at[mine], o_ref.at[mine], sibling)
        d.start()
        copy(4, o_ref.at[other], o_ref.at[other], sibling).wait_recv()
        d.wait_send()

    vm = pl.BlockSpec(memory_space=pltpu.VMEM)
    return _call(
        body, after=after, name=name, in_specs=[vm], out_specs=vm,
        out_shape=jax.ShapeDtypeStruct((R, C), F32),
        scratch_shapes=[pltpu.VMEM((H, C), F32), pltpu.VMEM((N_CHIPS, H, C), F32),
                        pltpu.SemaphoreType.DMA((5,)), pltpu.SemaphoreType.DMA((5,))],
        compiler_params=pltpu.CompilerParams(vmem_limit_bytes=VMEM_LIMIT),
    )(v)


def _cols_from_shards(g):
    return jnp.transpose(g, (1, 0, 2)).reshape(g.shape[1], -1)


def _shards_from_cols(w):
    R, C4 = w.shape
    return jnp.transpose(w.reshape(R, N_CHIPS, C4 // N_CHIPS), (1, 0, 2))


def _block_diag(t):
    G, a, b = t.shape
    eye = jnp.eye(G, dtype=t.dtype)
    return (t[:, :, None, :] * eye[:, None, :, None]).reshape(G * a, G * b)


def _diag_blocks(xm, G):
    a, b = xm.shape[0] // G, xm.shape[1] // G
    idx = jnp.arange(G)
    return xm.reshape(G, a, G, b)[idx, :, idx, :]


def _pack(arrs):
    flat = []
    for a in arrs:
        f = a.reshape(-1).astype(F32)
        flat.append(jnp.pad(f, (0, _rup(f.shape[0], LANE) - f.shape[0])))
    v = jnp.concatenate(flat)
    rows = _rup(v.shape[0] // LANE, 16)
    v = jnp.pad(v, (0, rows * LANE - v.shape[0]))
    return v.reshape(rows, LANE)


def _unpack(v, shapes):
    flat = v.reshape(-1)
    out, off = [], 0
    for s in shapes:
        n = int(np.prod(s))
        out.append(flat[off:off + n].reshape(s))
        off += _rup(n, LANE)
    return out


def _ffn_fwd(x, Wup, Wdn, cw, cb, tag):
    h = _mm(x, Wup, 'nt', bmode='bo', tm=512, tn=4096, name=f"ffn_up_{tag}")
    a, hc = _act_fwd(h, cw, cb, name=f"ffn_act_{tag}")
    f = _mm(a, Wdn, 'nn', bmode='abr', tm=512, tn=1024, tk=4096, name=f"ffn_down_{tag}")
    return f, (h, hc), a


def _ffn_bwd(df, x, saved, a, Wup, Wdn, cw, tag):
    h, hc = saved
    da = _mm(df, Wdn, 'nt', bmode='bo', tm=512, tn=4096, name=f"ffn_da_{tag}")
    dWdn = _mm(a, df, 'tn', bmode='ao', tm=4096, tn=512, name=f"ffn_dwdn_{tag}", out_dtype=BF16)
    dh, dcw, dcb = _act_bwd(h, hc, da, cw, name=f"ffn_actb_{tag}")

    def shard_of(k):
        return (k % 2) * 2 + k // 2

    dx = _mm(dh, Wup, 'nn', bmode='abr', tm=512, tn=1024, tk=4096, name=f"ffn_dx_{tag}", b_map=shard_of)
    dWup = _mm(dh, x, 'tn', bmode='ao', tm=4096, tn=512, name=f"ffn_dwup_{tag}", out_dtype=BF16,
               o_map=shard_of)
    return dx, dWup, dWdn, dcw, dcb


def kernel(x, positions, ev_w_in, ev_b_f, ev_lambda_re, ev_lambda_im, ev_log_step, ev_ssm_b_re, ev_ssm_b_im, ev_ssm_c_re, ev_ssm_c_im, ev_ssm_d, ev_w_glu, ev_w_out, od_w_in, od_sinks, od_w_out, ln_mix_g, ln_mix_b, ffn_w_up, ffn_conv_w, ffn_conv_b, ffn_w_down, ln_ffn_g, ln_ffn_b, loss_target, m_ev_w_in, m_ev_b_f, m_ev_lambda_re, m_ev_lambda_im, m_ev_log_step, m_ev_ssm_b_re, m_ev_ssm_b_im, m_ev_ssm_c_re, m_ev_ssm_c_im, m_ev_ssm_d, m_ev_w_glu, m_ev_w_out, m_od_w_in, m_od_sinks, m_od_w_out, m_ln_mix_g, m_ln_mix_b, m_ffn_w_up, m_ffn_conv_w, m_ffn_conv_b, m_ffn_w_down, m_ln_ffn_g, m_ln_ffn_b, v_ev_w_in, v_ev_b_f, v_ev_lambda_re, v_ev_lambda_im, v_ev_log_step, v_ev_ssm_b_re, v_ev_ssm_b_im, v_ev_ssm_c_re, v_ev_ssm_c_im, v_ev_ssm_d, v_ev_w_glu, v_ev_w_out, v_od_w_in, v_od_sinks, v_od_w_out, v_ln_mix_g, v_ln_mix_b, v_ffn_w_up, v_ffn_conv_w, v_ffn_conv_b, v_ffn_w_down, v_ln_ffn_g, v_ln_ffn_b):
    W = dict(ev_w_in=ev_w_in, ev_b_f=ev_b_f, ev_lambda_re=ev_lambda_re, ev_lambda_im=ev_lambda_im, ev_log_step=ev_log_step, ev_ssm_b_re=ev_ssm_b_re, ev_ssm_b_im=ev_ssm_b_im, ev_ssm_c_re=ev_ssm_c_re, ev_ssm_c_im=ev_ssm_c_im, ev_ssm_d=ev_ssm_d, ev_w_glu=ev_w_glu, ev_w_out=ev_w_out, od_w_in=od_w_in, od_sinks=od_sinks, od_w_out=od_w_out, ln_mix_g=ln_mix_g, ln_mix_b=ln_mix_b, ffn_w_up=ffn_w_up, ffn_conv_w=ffn_conv_w, ffn_conv_b=ffn_conv_b, ffn_w_down=ffn_w_down, ln_ffn_g=ln_ffn_g, ln_ffn_b=ln_ffn_b)
    Mo = dict(ev_w_in=m_ev_w_in, ev_b_f=m_ev_b_f, ev_lambda_re=m_ev_lambda_re, ev_lambda_im=m_ev_lambda_im, ev_log_step=m_ev_log_step, ev_ssm_b_re=m_ev_ssm_b_re, ev_ssm_b_im=m_ev_ssm_b_im, ev_ssm_c_re=m_ev_ssm_c_re, ev_ssm_c_im=m_ev_ssm_c_im, ev_ssm_d=m_ev_ssm_d, ev_w_glu=m_ev_w_glu, ev_w_out=m_ev_w_out, od_w_in=m_od_w_in, od_sinks=m_od_sinks, od_w_out=m_od_w_out, ln_mix_g=m_ln_mix_g, ln_mix_b=m_ln_mix_b, ffn_w_up=m_ffn_w_up, ffn_conv_w=m_ffn_conv_w, ffn_conv_b=m_ffn_conv_b, ffn_w_down=m_ffn_w_down, ln_ffn_g=m_ln_ffn_g, ln_ffn_b=m_ln_ffn_b)
    Vo = dict(ev_w_in=v_ev_w_in, ev_b_f=v_ev_b_f, ev_lambda_re=v_ev_lambda_re, ev_lambda_im=v_ev_lambda_im, ev_log_step=v_ev_log_step, ev_ssm_b_re=v_ev_ssm_b_re, ev_ssm_b_im=v_ev_ssm_b_im, ev_ssm_c_re=v_ev_ssm_c_re, ev_ssm_c_im=v_ev_ssm_c_im, ev_ssm_d=v_ev_ssm_d, ev_w_glu=v_ev_w_glu, ev_w_out=v_ev_w_out, od_w_in=v_od_w_in, od_sinks=v_od_sinks, od_w_out=v_od_w_out, ln_mix_g=v_ln_mix_g, ln_mix_b=v_ln_mix_b, ffn_w_up=v_ffn_w_up, ffn_conv_w=v_ffn_conv_w, ffn_conv_b=v_ffn_conv_b, ffn_w_down=v_ffn_w_down, ln_ffn_g=v_ln_ffn_g, ln_ffn_b=v_ln_ffn_b)
    names = list(W.keys())
    big = ['ev_w_in', 'ev_w_glu', 'ev_w_out', 'od_w_in', 'od_w_out', 'ffn_w_up', 'ffn_w_down']

    S, D = x.shape[1], x.shape[2]
    x0 = x.reshape(S, D)
    tgt = loss_target.reshape(S, D)
    G, Pn, Cg = SSM_GROUPS, SSM_STATE, SSM_GROUP
    Fs = ffn_w_up.shape[2]
    FP = Fs
    Rd = ffn_w_down.shape[1]
    EIN = N_CHIPS * ev_w_in.shape[2]

    def as2d(a):
        return a.reshape(-1, a.shape[-1])

    cwl = ffn_conv_w.reshape(-1)
    cw_rows = _rup(_rup(cwl.shape[0], LANE) // LANE, 32)
    cw_pad = jnp.pad(cwl, (0, cw_rows * LANE - cwl.shape[0])).reshape(cw_rows, LANE)
    transposed = ('ev_w_in', 'ffn_w_up')

    def view(n, a):
        return jnp.transpose(a, (0, 2, 1)) if n in transposed else a

    Wv = {n: view(n, W[n]) for n in big}
    big_e = [(n, l) for n in big for l in range(W[n].shape[0])]
    split_cols = {e: (Wv[e[0]].shape[1] // 2) % 16 != 0 for e in big_e}
    shard16 = {e: Wv[e[0]][e[1]].astype(BF16) for e in big_e}
    grp_now = [e for e in big_e if e[0].startswith('ev_')]
    grp_ffn0 = [('ffn_w_up', 0), ('ffn_w_down', 0)]
    grp_l1 = [('od_w_in', 0), ('od_w_out', 0), ('ffn_w_up', 1), ('ffn_w_down', 1)]
    src_now = [shard16[e] for e in grp_now]
    src_ffn0 = [shard16[e] for e in grp_ffn0] + [cw_pad]
    src_l1 = [shard16[e] for e in grp_l1]
    cols_now = [split_cols[e] for e in grp_now]
    cols_ffn0 = [split_cols[e] for e in grp_ffn0] + [False]
    cols_l1 = [split_cols[e] for e in grp_l1]
    ag_in = _chip_exchange_start('gather', src_now[:1], cols_now[:1], name="ag_in_start")
    ag_mix = _chip_exchange_start('gather', src_now[1:], cols_now[1:], name="ag_mix_start", after=[ag_in[4]])
    ag_ffn0 = _chip_exchange_start('gather', src_ffn0, cols_ffn0, name="ag_ffn0_start", after=[ag_mix[4]])
    ag_l1 = _chip_exchange_start('gather', src_l1, cols_l1, name="ag_l1_start", after=[ag_ffn0[4]])
    started = [ag_l1[4]]

    def finish_gather(started, srcs, cols, after, tag):
        send, recv, thru, lands, _ = started
        thru, lands = _chip_exchange_wait('gather', send, recv, thru, lands, cols, after, name=f"ag_{tag}_wait")
        lands = _sibling_pass_gathered(lands, [s.shape for s in srcs], cols, name=f"ag_{tag}_pass")
        return _own_slot(lands, [s[None] for s in thru])

    lam_r, lam_i = ev_lambda_re[0], ev_lambda_im[0]
    lstep = ev_log_step[0].reshape(G, 1)
    a_re, a_im, g_re, g_im = _s5_disc_fwd(lam_r, lam_i, lstep, name="s5_disc", after=started)
    b_re2, b_im2 = ev_ssm_b_re[0].reshape(G * Pn, Cg), ev_ssm_b_im[0].reshape(G * Pn, Cg)
    g_re1, g_im1 = g_re.reshape(G * Pn, 1), g_im.reshape(G * Pn, 1)
    bb_re, bb_im = _s5_bb_fwd(g_re1, g_im1, b_re2, b_im2, name="s5_bb")
    bbt = jnp.stack([jnp.transpose(b.reshape(G, Pn, Cg), (0, 2, 1)).reshape(G * Cg, Pn) for b in (bb_re, bb_im)])
    BB = _diag_expand(bbt, Cg, Pn, name="s5_bb_dense")
    cct = jnp.stack([jnp.transpose(ev_ssm_c_re[0], (0, 2, 1)).reshape(G * Pn, Cg),
                     jnp.transpose(-ev_ssm_c_im[0], (0, 2, 1)).reshape(G * Pn, Cg)])
    CC = _diag_expand(cct, Pn, Cg, name="s5_cc_dense", after=started)
    a_cat = jnp.stack([a_re.reshape(1, G * Pn), a_im.reshape(1, G * Pn)])
    dskip = ev_ssm_d[0].reshape(1, SSM_WIDTH)
    tabs = _rope_tables(positions.reshape(S, 1).astype(F32), name="rope_tables", after=[BB, CC])

    gw = dict(zip(grp_now[:1], finish_gather(ag_in, src_now[:1], cols_now[:1], [tabs[2]], "in")))
    w_in_t = gw[('ev_w_in', 0)].reshape(EIN, D)
    qkv_w = 3 * FOX_WIDTH
    WmainT = jnp.concatenate([w_in_t[:qkv_w], w_in_t[qkv_w + FOX_HEADS:]], axis=0)
    WfT = jnp.pad(w_in_t[qkv_w:qkv_w + FOX_HEADS], ((0, LANE - FOX_HEADS), (0, 0)))
    cbs = [ffn_conv_b[l].reshape(N_CHIPS, Fs) for l in range(DEPTH)]

    P = _mm(x0, WmainT, 'nt', name="ev_proj")
    fl = _mm(x0, WfT, 'nt', name="ev_proj_f")
    bf_pad = jnp.pad(ev_b_f.reshape(1, FOX_HEADS), ((0, 0), (0, LANE - FOX_HEADS)))
    cgate, sgate = _gate_fwd(fl, bf_pad, name="fox_gate")
    ccol = jnp.transpose(cgate[:, :FOX_HEADS]).reshape(FOX_HEADS, S, 1)
    crow = jnp.transpose(cgate[:, :FOX_HEADS]).reshape(FOX_HEADS, 1, S)
    fox, lse = _fox_fwd(P, ccol, crow, name="fox_fwd")
    u_s5 = P[:, qkv_w:]
    UT, HT = _DIAG_TILE * Cg, _DIAG_TILE * Pn
    bu = _mm(u_s5, BB, 'nn', bmode='bo', tm=2048, tn=HT, tk=UT, diag='kn', name="s5_bu")
    hh = _s5_scan_fwd(bu, a_cat, name="s5_scan")
    yc = _mm(hh, CC, 'nn', bmode='abr', tm=2048, tn=UT, tk=HT, diag='kn', name="s5_y")
    y_s5, yg = _s5_out_fwd(yc, P, dskip, name="s5_out")
    gw.update(zip(grp_now[1:], finish_gather(ag_mix, src_now[1:], cols_now[1:], [yg], "mix")))
    Wglu = _cols_from_shards(gw[('ev_w_glu', 0)])
    Wout_ev = gw[('ev_w_out', 0)].reshape(D, D)
    z = _mm(yg, Wglu, 'nn', name="s5_glu_proj")
    ssm = _glu_fwd(z, name="s5_glu")
    cat = jnp.concatenate([fox.astype(BF16), ssm], axis=1)
    mix0 = _mm(cat, Wout_ev, 'nn', name="ev_out")
    x1, xh1, rs1 = _add_ln_fwd(x0, mix0, ln_mix_g[0], ln_mix_b[0], name="ln_mix0")
    got = finish_gather(ag_ffn0, src_ffn0, cols_ffn0, [x1], "ffn0")
    gw.update(zip(grp_ffn0, got[:-1]))
    cw_all = got[-1].reshape(N_CHIPS, -1)[:, :cwl.shape[0]].reshape(N_CHIPS, DEPTH, 3, Fs)
    cws = [cw_all[:, l] for l in range(DEPTH)]
    Wup = {0: gw[('ffn_w_up', 0)]}
    Wdn = {0: gw[('ffn_w_down', 0)].reshape(2, Fs, D)}
    f0, hf0, af0 = _ffn_fwd(x1, Wup[0], Wdn[0], cws[0], cbs[0], "l0")
    x2, xh2, rs2 = _add_ln_fwd(x1, f0, ln_ffn_g[0], ln_ffn_b[0], name="ln_ffn0")

    gw.update(zip(grp_l1, finish_gather(ag_l1, src_l1, cols_l1, [x2], "l1")))
    Wodin = _cols_from_shards(gw[('od_w_in', 0)])
    Wodout = gw[('od_w_out', 0)].reshape(D, D)
    Wup[1] = gw[('ffn_w_up', 1)]
    Wdn[1] = gw[('ffn_w_down', 1)].reshape(2, Fs, D)
    QW, KW = SWA_HEADS * SWA_HEAD_DIM, SWA_KV_HEADS * SWA_HEAD_DIM
    P1 = _mm(x2, Wodin, 'nn', name="od_proj")
    qr = _rope_apply(P1, tabs, col0=0, width=QW, inverse=False, name="rope_q", out_dtype=BF16)
    kr = _rope_apply(P1, tabs, col0=QW, width=KW, inverse=False, name="rope_k", out_dtype=BF16)

    def heads(a2, nh):
        return jnp.transpose(a2.reshape(S, nh, SWA_HEAD_DIM), (1, 0, 2))

    def unheads(a3):
        return jnp.transpose(a3, (1, 0, 2)).reshape(S, -1)

    qT, kT = heads(qr, SWA_HEADS), heads(kr, SWA_KV_HEADS)
    vT = heads(P1[:, QW + KW:].astype(BF16), SWA_KV_HEADS)
    sink_rows = jnp.broadcast_to(od_sinks[0].reshape(SWA_KV_HEADS, SWA_GROUPS, 1, 1),
                                 (SWA_KV_HEADS, SWA_GROUPS, SWA_WINDOW, 1)).reshape(SWA_KV_HEADS, -1, 1)
    oT, Lsw = _swa_fwd(qT, kT, vT, sink_rows, name="swa_fwd")
    o_sw = unheads(oT).astype(BF16)
    mix1 = _mm(o_sw, Wodout, 'nn', name="od_out")
    x3, xh3, rs3 = _add_ln_fwd(x2, mix1, ln_mix_g[1], ln_mix_b[1], name="ln_mix1")
    f1, hf1, af1 = _ffn_fwd(x3, Wup[1], Wdn[1], cws[1], cbs[1], "l1")
    x4, xh4, rs4 = _add_ln_fwd(x3, f1, ln_ffn_g[1], ln_ffn_b[1], name="ln_ffn1")
    dy, loss_part = _loss_grad(x4, tgt, name="loss")

    dz4, dg_ffn1, db_ffn1 = _ln_bwd(dy, None, xh4, rs4, ln_ffn_g[1], name="lnb_ffn1")
    dx3f, dWup1, dWdn1, dcw1, dcb1 = _ffn_bwd(dz4, x3, hf1, af1, Wup[1], Wdn[1], cws[1], "l1")
    sib_ffn1 = _sibling_halves_start([dWup1, dWdn1.reshape(N_CHIPS, Rd, D)], [False, False], name="rs_ffn1_sib_start")
    dz3, dg_mix1, db_mix1 = _ln_bwd(dz4, dx3f, xh3, rs3, ln_mix_g[1], name="lnb_mix1", after=[sib_ffn1[4]])
    do_sw = _mm(dz3, Wodout, 'nt', name="od_out_dx")
    dWodout = _mm(o_sw, dz3, 'tn', name="od_out_dw", out_dtype=BF16)
    doT = heads(do_sw, SWA_HEADS)
    dqT, dkT, dvT, dsink = _swa_bwd(qT, kT, vT, sink_rows, oT, Lsw, doT, name="swa_bwd")
    dq1 = _rope_apply(unheads(dqT), tabs, col0=0, width=QW, inverse=True, name="rope_dq", out_dtype=BF16)
    dk1 = _rope_apply(unheads(dkT[:, SWA_WINDOW:]), tabs, col0=0, width=KW, inverse=True, name="rope_dk",
                      out_dtype=BF16)
    dP1 = jnp.concatenate([dq1, dk1, unheads(dvT[:, SWA_WINDOW:]).astype(BF16)], axis=1)
    dx2m = _mm(dP1, Wodin, 'nt', name="od_proj_dx")
    dWodin = _mm(x2, dP1, 'tn', name="od_proj_dw", out_dtype=BF16)

    def rs_begin(entries, grads, tag):
        cols = [split_cols[e] for e in entries]
        sib = _sibling_send_halves(grads, cols, name=f"rs_{tag}_sibling")
        return [_sum2_halves(g4, s4, bc, name=f"rs_sum2_{n}{l}")
                for (n, l), g4, s4, bc in zip(entries, grads, sib, cols)]

    def rs_begin_started(entries, started, after, tag):
        send, rcv, thru, lands, _ = started
        thru, lands = _sibling_halves_wait(send, rcv, thru, lands, [False] * len(thru), after,
                                           name=f"rs_{tag}_sib_wait")
        return [_sum2_halves(g4, s4, False, name=f"rs_sum2_{n}{l}") for (n, l), g4, s4 in zip(entries, thru, lands)]

    def own_parts(parts):
        me = 2 * lax.axis_index("x") + lax.axis_index("y")
        return [lax.dynamic_slice_in_dim(p, me, 1, axis=0) for p in parts]

    part_l1 = (rs_begin(grp_l1[:2], [_shards_from_cols(dWodin), dWodout.reshape(N_CHIPS, D // N_CHIPS, D)], "od")
               + rs_begin_started(grp_l1[2:], sib_ffn1, [dWodin], "ffn1"))
    rs_l1 = _chip_exchange_start('scatter', part_l1, [False] * len(part_l1), name="rs_l1_start")

    dz2, dg_ffn0, db_ffn0 = _ln_bwd(dz3, dx2m, xh2, rs2, ln_ffn_g[0], name="lnb_ffn0", after=[rs_l1[4]])
    dx1f, dWup0, dWdn0, dcw0, dcb0 = _ffn_bwd(dz2, x1, hf0, af0, Wup[0], Wdn[0], cws[0], "l0")
    sib_ffn0 = _sibling_halves_start([dWup0, dWdn0.reshape(N_CHIPS, Rd, D)], [False, False], name="rs_ffn0_sib_start")
    dz1, dg_mix0, db_mix0 = _ln_bwd(dz2, dx1f, xh1, rs1, ln_mix_g[0], name="lnb_mix0", after=[sib_ffn0[4]])
    dcat = _mm(dz1, Wout_ev, 'nt', name="ev_out_dx")
    dWout_ev = _mm(cat, dz1, 'tn', name="ev_out_dw", out_dtype=BF16)
    part_ffn0 = rs_begin_started(grp_ffn0, sib_ffn0, [dWout_ev], "ffn0")
    rs_ffn0 = _chip_exchange_start('scatter', part_ffn0, [False] * len(part_ffn0), name="rs_ffn0_start")
    dz = _glu_bwd(z, dcat, name="s5_glu_bwd")
    dyg = _mm(dz, Wglu, 'nt', name="s5_glu_dx", after=[rs_ffn0[4]])
    dWglu = _mm(yg, dz, 'tn', name="s5_glu_dw", out_dtype=BF16)
    dy_s5, du_dir, dD = _s5_out_bwd(dyg, y_s5, P, dskip, name="s5_out_bwd")
    dhh = _mm(dy_s5, CC, 'nt', bmode='bo', tm=2048, tn=HT, tk=UT, diag='kn', name="s5_y_dx")
    dCC = _mm(hh, dy_s5, 'tn', bmode='ao', tm=HT, tn=UT, diag='mn', name="s5_y_dw")
    lam, da_s5 = _s5_scan_bwd(dhh, hh, a_cat, name="s5_scan_bwd")
    du_bu = _mm(lam, BB, 'nt', bmode='abr', tm=2048, tn=UT, tk=HT, diag='kn', name="s5_bu_dx")
    dBB = _mm(u_s5, lam, 'tn', bmode='bo', tm=UT, tn=HT, diag='mn', name="s5_bu_dw")
    du = _combine([du_dir, du_bu], [1.0, 1.0], name="s5_du", out_dtype=BF16)
    dq0, dk0, dv0, dccol, dcrow = _fox_bwd(P, ccol, crow, fox, lse, dcat, name="fox_bwd")
    dc = jnp.transpose((dccol.reshape(FOX_HEADS, S) - dcrow.reshape(FOX_HEADS, S)))
    dc = jnp.pad(dc, ((0, 0), (0, LANE - FOX_HEADS)))
    dfl, dbf = _gate_bwd(dc, sgate, name="fox_gate_bwd")
    dP = jnp.concatenate([dq0, dk0, dv0, du], axis=1)
    dx0a = _mm(dP, WmainT, 'nn', name="ev_proj_dx")
    dx0b = _mm(dfl, WfT, 'nn', name="ev_proj_f_dx")
    dWmainT = _mm(dP, x0, 'tn', tm=1024, tn=1024, name="ev_proj_dw", out_dtype=BF16)
    dWfT = _mm(dfl, x0, 'tn', name="ev_proj_f_dw", out_dtype=BF16)
    grad_x = _combine([dz1, dx0a, dx0b], [ALPHA, 1.0, 1.0], name="grad_x")

    dbbt = _diag_extract(dBB, Cg, Pn, name="s5_bb_diag")
    dcct = _diag_extract(dCC, Pn, Cg, name="s5_cc_diag")
    dbb_re = jnp.transpose(dbbt[0].reshape(G, Cg, Pn), (0, 2, 1)).reshape(G * Pn, Cg)
    dbb_im = jnp.transpose(dbbt[1].reshape(G, Cg, Pn), (0, 2, 1)).reshape(G * Pn, Cg)
    db_re, db_im, dg_re1, dg_im1 = _s5_bb_bwd(g_re1, g_im1, b_re2, b_im2, dbb_re, dbb_im, name="s5_bb_bwd")
    dlam_re, dlam_im, dlstep = _s5_disc_bwd(lam_r, lam_i, lstep, da_s5[0].reshape(G, Pn), da_s5[1].reshape(G, Pn),
                                            dg_re1.reshape(G, Pn), dg_im1.reshape(G, Pn), name="s5_disc_bwd")
    dc_re = jnp.transpose(dcct[0].reshape(G, Pn, Cg), (0, 2, 1))
    dc_im = -jnp.transpose(dcct[1].reshape(G, Pn, Cg), (0, 2, 1))

    def conv_w_full(d0, d1):
        return jnp.stack([jnp.reshape(jnp.transpose(d[:, :, :Fs], (1, 0, 2)), (3, N_CHIPS * Fs)) for d in (d0, d1)])

    def conv_b_full(d0, d1):
        return jnp.stack([jnp.reshape(d[:, 0, :Fs], (N_CHIPS * Fs,)) for d in (d0, d1)])

    small_local = dict(
        ev_b_f=dbf[:, :FOX_HEADS], ev_lambda_re=dlam_re, ev_lambda_im=dlam_im, ev_log_step=dlstep,
        ev_ssm_b_re=db_re, ev_ssm_b_im=db_im, ev_ssm_c_re=dc_re, ev_ssm_c_im=dc_im, ev_ssm_d=dD,
        od_sinks=dsink[:, :, 0],
        ln_mix_g=jnp.concatenate([dg_mix0, dg_mix1]), ln_mix_b=jnp.concatenate([db_mix0, db_mix1]),
        ffn_conv_w=conv_w_full(dcw0, dcw1), ffn_conv_b=conv_b_full(dcb0, dcb1),
        ln_ffn_g=jnp.concatenate([dg_ffn0, dg_ffn1]), ln_ffn_b=jnp.concatenate([db_ffn0, db_ffn1]))
    small = list(small_local.keys())
    out_g, out_d, out_m, out_v = {}, {}, {}, {}
    loss_out = []

    def small_update(after):
        red = _all_reduce_small(_pack([small_local[n] for n in small] + [loss_part]), name="ar_small", after=after)
        full_shapes = [W[n].shape if n != 'ffn_conv_w' else (DEPTH, 3, N_CHIPS * Fs) for n in small]
        pieces = _unpack(red, full_shapes + [()])
        loss_out.append(pieces[-1])
        gsmall = dict(zip(small, pieces[:-1]))
        chip = 2 * lax.axis_index("x") + lax.axis_index("y")
        gsmall['ffn_conv_w'] = lax.dynamic_slice_in_dim(gsmall['ffn_conv_w'], chip * Fs, Fs, axis=2)
        shapes = [W[n].shape for n in small]
        gs, ds_, ms, vs = _adamw(_pack([W[n] for n in small])[None], _pack([gsmall[n] for n in small])[None],
                                 _pack([Mo[n] for n in small])[None], _pack([Vo[n] for n in small])[None],
                                 name="adamw_small", tr=1 << 14)
        out_g.update(zip(small, _unpack(gs, shapes)))
        out_d.update(zip(small, _unpack(ds_, shapes)))
        out_m.update(zip(small, _unpack(ms, shapes)))
        out_v.update(zip(small, _unpack(vs, shapes)))
        return vs

    dw_in_t = jnp.concatenate([dWmainT[:qkv_w], dWfT[:FOX_HEADS], dWmainT[qkv_w:]], axis=0)
    part_now = rs_begin(grp_now, [dw_in_t.reshape(N_CHIPS, EIN // N_CHIPS, D), _shards_from_cols(dWglu),
                                  dWout_ev.reshape(N_CHIPS, D // N_CHIPS, D)], "l0")
    small_done = small_update([grad_x])
    rs_now = _chip_exchange_start('scatter', part_now, [False] * len(part_now), name="rs_l0_start",
                                  after=[small_done])

    def finish_scatter(started, parts, after, tag):
        send, rcv, thru, lands, _ = started
        thru, lands = _chip_exchange_wait('scatter', send, rcv, thru, lands, [False] * len(parts), after,
                                          name=f"rs_{tag}_wait")
        return _own_slot(lands, own_parts(thru))

    def update(entries, recv, tag):
        halves = [_rowsum(r, name=f"rs_sum4_{e[0]}{e[1]}") for e, r in zip(entries, recv)]
        send, rcv, thru, lands, tok = _sibling_swap_start(halves, name=f"rs_{tag}_join_start")
        own = dict(zip(entries, thru))
        params = list(dict.fromkeys(e[0] for e in entries))

        def half_update(n, grads, is_own, prev, after_name):
            return _adamw_half(Wv[n], [grads[(n, l)] for l in range(W[n].shape[0])], view(n, Mo[n]), view(n, Vo[n]),
                               name=f"adamw_{after_name}_{n}", own=is_own, prev=prev, by_cols=split_cols[(n, 0)])

        first = {n: half_update(n, own, True, None, "own") for n in params}
        _, others = _sibling_swap_wait(send, rcv, thru, lands, [first[n][3] for n in params] + [tok],
                                       name=f"rs_{tag}_join_wait")
        oth = dict(zip(entries, others))
        done = []
        for n in params:
            res = half_update(n, oth, False, first[n], "sib")
            out_g[n], out_d[n], out_m[n], out_v[n] = (view(n, t) for t in res)
            done.append(res[3])
        return done

    recv_rest = (finish_scatter(rs_l1, part_l1, [rs_now[4]], "l1")
                 + finish_scatter(rs_ffn0, part_ffn0, [rs_now[4]], "ffn0"))
    done = update(grp_l1 + grp_ffn0, recv_rest, "rest")
    update(grp_now, finish_scatter(rs_now, part_now, done, "l0"), "l0")
    loss = loss_out[0]

    return (loss, grad_x.reshape(1, S, D), *[out_g[n] for n in names], *[out_d[n] for n in names],
            *[out_m[n] for n in names], *[out_v[n] for n in names])
```

```python
import math

import numpy as np
import jax
import jax.numpy as jnp
from jax import lax
from jax.experimental import pallas as pl
from jax.experimental.pallas import tpu as pltpu

F32 = jnp.float32
BF16 = jnp.bfloat16
MESH = pl.DeviceIdType.MESH
ANY = pl.BlockSpec(memory_space=pl.ANY)

D_MODEL = 2048
FOX_HEADS = 8
FOX_HEAD_DIM = 128
FOX_WIDTH = 1024
SSM_WIDTH = 1024
SSM_GROUP = 16
SSM_GROUPS = 64
SSM_STATE = 64
SWA_HEADS = 32
SWA_KV_HEADS = 4
SWA_HEAD_DIM = 64
SWA_GROUPS = 8
SWA_WINDOW = 128
ROPE_DIM = 16
ROPE_THETA = 500000.0
LN_EPS = 1e-5
DEPTH = 2
ALPHA = (2.0 * DEPTH) ** 0.25
ADAM_LR = 0.001
ADAM_B1 = 0.9
ADAM_B2 = 0.999
ADAM_EPS = 1e-08
ADAM_WD = 0.01
ADAM_STEP = 10
N_CHIPS = 4

VMEM_LIMIT = 56 * 1024 * 1024
LANE = 128


def _call(body, after=(), **kw):
    if after:
        n = len(after)

        def shifted(*refs):
            return body(*refs[n:])

        call = _call(shifted, **dict(kw, in_specs=[ANY] * n + list(kw["in_specs"])))
        return lambda *args: call(*after, *args)
    return pl.pallas_call(body, **kw)


def _cparams(sem):
    return pltpu.CompilerParams(dimension_semantics=sem, vmem_limit_bytes=VMEM_LIMIT)


def _rup(n, m):
    return (n + m - 1) // m * m


def _pick(n, pref):
    if n <= pref:
        return n
    for step in (128, 16, 8):
        for t in range(pref - pref % step, 0, -step):
            if n % t == 0:
                return t
    return n


def _tile2d(rows, cols, pref_rows=256, budget=256 * 1024):
    tr = _pick(rows, pref_rows)
    if tr < 64:
        tr = rows
    if cols % LANE:
        return tr, cols
    return tr, _pick(cols, max(LANE, budget // tr // LANE * LANE))


def _mm(a, b, mode, *, name, tm=512, tn=1024, tk=2048, bmode=None, out_dtype=F32, after=(), b_map=None,
        o_map=None, diag=None):
    a3 = a if a.ndim == 3 else a[None]
    b3 = b if b.ndim == 3 else b[None]
    if mode == 'tn':
        K, M = a3.shape[1:]
    else:
        M, K = a3.shape[1:]
    N = b3.shape[1] if mode == 'nt' else b3.shape[2]
    tm, tn, tk = _pick(M, tm), _pick(N, tn), _pick(K, tk)
    nb = max(a3.shape[0], b3.shape[0])
    nbo, nbr = (1, nb) if bmode == 'abr' else (nb, 1)
    nm, nk = M // tm, K // tk
    if diag == 'kn':
        assert K // tk == N // tn
        nk = 1
    if diag == 'mn':
        assert M // tm == N // tn
        nm = 1
    nred = nbr * nk
    a_b = bmode in ('ao', 'abr')
    b_b = bmode in ('bo', 'abr')
    o_b = bmode in ('bo', 'ao')

    def bsel(flag, bo, br, remap=None):
        if not flag:
            return 0
        return (bo + br) if remap is None else remap(bo + br)

    def mi(i, j):
        return j if diag == 'mn' else i

    def ki(j, k):
        return j if diag == 'kn' else k

    if mode == 'tn':
        a_spec = pl.BlockSpec((None, tk, tm), lambda bo, i, j, br, k: (bsel(a_b, bo, br), ki(j, k), mi(i, j)))
    else:
        a_spec = pl.BlockSpec((None, tm, tk), lambda bo, i, j, br, k: (bsel(a_b, bo, br), mi(i, j), ki(j, k)))
    if mode == 'nt':
        b_spec = pl.BlockSpec((None, tn, tk), lambda bo, i, j, br, k: (bsel(b_b, bo, br, b_map), j, ki(j, k)))
    else:
        b_spec = pl.BlockSpec((None, tk, tn), lambda bo, i, j, br, k: (bsel(b_b, bo, br, b_map), ki(j, k), j))
    o_spec = pl.BlockSpec((None, tm, tn), lambda bo, i, j, br, k: (bsel(o_b, bo, br, o_map), mi(i, j), j))
    dn = {'nn': (((1,), (0,)), ((), ())), 'nt': (((1,), (1,)), ((), ())), 'tn': (((0,), (0,)), ((), ()))}[mode]

    def body(a_ref, b_ref, *rest):
        o_ref, scratch = rest[len(after)], rest[len(after) + 1:]
        r = lax.dot_general(a_ref[...].astype(BF16), b_ref[...].astype(BF16), dn, preferred_element_type=F32)
        if nred == 1:
            o_ref[...] = r.astype(out_dtype)
        else:
            acc = scratch[0]
            step = pl.program_id(3) * nk + pl.program_id(4)

            @pl.when(step == 0)
            def _():
                acc[...] = r

            @pl.when(step > 0)
            def _():
                acc[...] += r

            @pl.when(step == nred - 1)
            def _():
                o_ref[...] = acc[...].astype(out_dtype)

    out = _call(
        body, name=name,
        grid=(nbo, nm, N // tn, nbr, nk),
        in_specs=[a_spec, b_spec] + [ANY] * len(after), out_specs=o_spec,
        out_shape=jax.ShapeDtypeStruct((nbo if o_b else 1, M, N), out_dtype),
        scratch_shapes=[] if nred == 1 else [pltpu.VMEM((tm, tn), F32)],
        compiler_params=_cparams(("parallel", "parallel", "parallel", "arbitrary", "arbitrary")),
    )(a3, b3, *after)
    return out if o_b else out[0]


def _add_ln_fwd(x, r, g, b, *, name):
    S, D = x.shape
    tr = _pick(S, 256)

    def body(x_ref, r_ref, g_ref, b_ref, o_ref, xh_ref, rs_ref):
        z = ALPHA * x_ref[...] + r_ref[...]
        mu = jnp.mean(z, axis=-1, keepdims=True)
        zc = z - mu
        var = jnp.mean(zc * zc, axis=-1, keepdims=True)
        rstd = lax.rsqrt(var + LN_EPS)
        xh = zc * rstd
        xh_ref[...] = xh
        rs_ref[...] = rstd
        o_ref[...] = xh * g_ref[...] + b_ref[...]

    row = pl.BlockSpec((tr, D), lambda i: (i, 0))
    vec = pl.BlockSpec((1, D), lambda i: (0, 0))
    return _call(
        body, name=name, grid=(S // tr,),
        in_specs=[row, row, vec, vec],
        out_specs=[row, row, pl.BlockSpec((tr, 1), lambda i: (i, 0))],
        out_shape=[jax.ShapeDtypeStruct((S, D), F32), jax.ShapeDtypeStruct((S, D), F32),
                   jax.ShapeDtypeStruct((S, 1), F32)],
        compiler_params=_cparams(("parallel",)),
    )(x, r, g.reshape(1, D), b.reshape(1, D))


def _ln_bwd(da, db, xhat, rstd, g, *, name, after=()):
    S, D = xhat.shape
    tr = _pick(S, 256)
    two = db is not None

    def body(*refs):
        refs = refs[len(after):]
        if two:
            da_ref, db_ref, xh_ref, rs_ref, g_ref, dz_ref, dg_ref, dbt_ref = refs
            dy = ALPHA * da_ref[...] + db_ref[...]
        else:
            da_ref, xh_ref, rs_ref, g_ref, dz_ref, dg_ref, dbt_ref = refs
            dy = da_ref[...]
        xh = xh_ref[...]
        dxh = dy * g_ref[...]
        m1 = jnp.mean(dxh, axis=-1, keepdims=True)
        m2 = jnp.mean(dxh * xh, axis=-1, keepdims=True)
        dz_ref[...] = rs_ref[...] * (dxh - m1 - xh * m2)
        pg = jnp.sum(dy * xh, axis=0, keepdims=True)
        pb = jnp.sum(dy, axis=0, keepdims=True)

        @pl.when(pl.program_id(0) == 0)
        def _():
            dg_ref[...] = pg
            dbt_ref[...] = pb

        @pl.when(pl.program_id(0) > 0)
        def _():
            dg_ref[...] += pg
            dbt_ref[...] += pb

    row = pl.BlockSpec((tr, D), lambda i: (i, 0))
    vec = pl.BlockSpec((1, D), lambda i: (0, 0))
    ins = list(after) + [da] + ([db] if two else []) + [xhat, rstd, g.reshape(1, D)]
    in_specs = [ANY] * len(after) + [row] + ([row] if two else []) + [row, pl.BlockSpec((tr, 1), lambda i: (i, 0)), vec]
    return _call(
        body, name=name, grid=(S // tr,),
        in_specs=in_specs, out_specs=[row, vec, vec],
        out_shape=[jax.ShapeDtypeStruct((S, D), F32), jax.ShapeDtypeStruct((1, D), F32),
                   jax.ShapeDtypeStruct((1, D), F32)],
        compiler_params=_cparams(("arbitrary",)),
    )(*ins)


def _loss_grad(y, t, *, name):
    S, D = y.shape
    tr = _pick(S, 256)

    def body(y_ref, t_ref, dy_ref, l_ref):
        e = y_ref[...] - t_ref[...]
        dy_ref[...] = e * (1.0 / D)
        part = 0.5 * jnp.sum(jnp.sum(e * e, axis=-1, keepdims=True) * (1.0 / D), axis=0, keepdims=True)

        @pl.when(pl.program_id(0) == 0)
        def _():
            l_ref[...] = part

        @pl.when(pl.program_id(0) > 0)
        def _():
            l_ref[...] += part

    row = pl.BlockSpec((tr, D), lambda i: (i, 0))
    return _call(
        body, name=name, grid=(S // tr,), in_specs=[row, row],
        out_specs=[row, pl.BlockSpec((1, 1), lambda i: (0, 0))],
        out_shape=[jax.ShapeDtypeStruct((S, D), F32), jax.ShapeDtypeStruct((1, 1), F32)],
        compiler_params=_cparams(("arbitrary",)),
    )(y, t)


def _combine(terms, scales, *, name, out_dtype=F32):
    S, D = terms[0].shape
    tr = _pick(S, 256)
    n = len(terms)

    def body(*refs):
        acc = scales[0] * refs[0][...].astype(F32)
        for i in range(1, n):
            acc = acc + scales[i] * refs[i][...].astype(F32)
        refs[n][...] = acc.astype(out_dtype)

    row = pl.BlockSpec((tr, D), lambda i: (i, 0))
    return _call(
        body, name=name, grid=(S // tr,), in_specs=[row] * n, out_specs=row,
        out_shape=jax.ShapeDtypeStruct((S, D), out_dtype),
        compiler_params=_cparams(("parallel",)),
    )(*terms)


def _split3(x):
    h = x.astype(BF16)
    r = x - h.astype(F32)
    m = r.astype(BF16)
    l = (r - m.astype(F32)).astype(BF16)
    return h, m, l


def _tri_matmul(tri_bf, x):
    h, m, l = _split3(x)
    dn = (((1,), (0,)), ((), ()))
    return (lax.dot_general(tri_bf, l, dn, preferred_element_type=F32)
            + lax.dot_general(tri_bf, m, dn, preferred_element_type=F32)
            + lax.dot_general(tri_bf, h, dn, preferred_element_type=F32))


def _gate_fwd(fl, bf, *, name):
    S = fl.shape[0]
    tc = _pick(S, 256)
    nchunk = S // tc

    def body(fl_ref, bf_ref, c_ref, sg_ref):
        r = lax.broadcasted_iota(jnp.int32, (tc, tc), 0)
        cidx = lax.broadcasted_iota(jnp.int32, (tc, tc), 1)
        tri = (r >= cidx).astype(BF16)
        carry = jnp.zeros((1, LANE), F32)
        for ch in range(nchunk):
            x = fl_ref[pl.ds(ch * tc, tc), :] + bf_ref[...]
            lf = jnp.minimum(x, 0.0) - jnp.log(1.0 + jnp.exp(-jnp.abs(x)))
            sg_ref[pl.ds(ch * tc, tc), :] = jax.nn.sigmoid(-x)
            c_ref[pl.ds(ch * tc, tc), :] = _tri_matmul(tri, lf) + carry
            carry = carry + jnp.sum(lf, axis=0, keepdims=True)

    full = pl.BlockSpec((S, LANE), lambda: (0, 0))
    return _call(
        body, name=name, in_specs=[full, pl.BlockSpec((1, LANE), lambda: (0, 0))], out_specs=[full, full],
        out_shape=[jax.ShapeDtypeStruct((S, LANE), F32)] * 2,
        compiler_params=pltpu.CompilerParams(vmem_limit_bytes=VMEM_LIMIT),
    )(fl, bf)


def _gate_bwd(dc, sg, *, name):
    S = dc.shape[0]
    tc = _pick(S, 256)
    nchunk = S // tc

    def body(dc_ref, sg_ref, dfl_ref, db_ref):
        r = lax.broadcasted_iota(jnp.int32, (tc, tc), 0)
        cidx = lax.broadcasted_iota(jnp.int32, (tc, tc), 1)
        tri = (r <= cidx).astype(BF16)
        carry = jnp.zeros((1, LANE), F32)
        dbacc = jnp.zeros((1, LANE), F32)
        for ch in reversed(range(nchunk)):
            d = dc_ref[pl.ds(ch * tc, tc), :]
            dfl = (_tri_matmul(tri, d) + carry) * sg_ref[pl.ds(ch * tc, tc), :]
            dfl_ref[pl.ds(ch * tc, tc), :] = dfl
            dbacc = dbacc + jnp.sum(dfl, axis=0, keepdims=True)
            carry = carry + jnp.sum(d, axis=0, keepdims=True)
        db_ref[...] = dbacc

    full = pl.BlockSpec((S, LANE), lambda: (0, 0))
    return _call(
        body, name=name, in_specs=[full, full], out_specs=[full, pl.BlockSpec((1, LANE), lambda: (0, 0))],
        out_shape=[jax.ShapeDtypeStruct((S, LANE), F32), jax.ShapeDtypeStruct((1, LANE), F32)],
        compiler_params=pltpu.CompilerParams(vmem_limit_bytes=VMEM_LIMIT),
    )(dc, sg)


def _fox_scores(q_ref, k_ref, cc_ref, cr_ref, qi, tq, S):
    scale = 1.0 / math.sqrt(FOX_HEAD_DIM)
    s = lax.dot_general(q_ref[...].astype(BF16), k_ref[...].astype(BF16), (((1,), (1,)), ((), ())),
                        preferred_element_type=F32) * scale
    s = s + cc_ref[...] - cr_ref[...]
    row = lax.broadcasted_iota(jnp.int32, (tq, S), 0) + qi * tq
    col = lax.broadcasted_iota(jnp.int32, (tq, S), 1)
    return s, row >= col


def _fox_fwd(P, ccol, crow, *, name):
    S = P.shape[0]
    tq = _pick(S, 256)
    H = FOX_HEADS

    def body(q_ref, k_ref, v_ref, cc_ref, cr_ref, o_ref, l_ref):
        s, causal = _fox_scores(q_ref, k_ref, cc_ref, cr_ref, pl.program_id(1), tq, S)
        s = jnp.where(causal, s, -1e30)
        m = jnp.max(s, axis=-1, keepdims=True)
        e = jnp.exp(s - m)
        den = jnp.sum(e, axis=-1, keepdims=True)
        p = e / den
        o_ref[...] = jnp.dot(p.astype(BF16), v_ref[...].astype(BF16), preferred_element_type=F32)
        l_ref[...] = m + jnp.log(den)

    return _call(
        body, name=name, grid=(H, S // tq),
        in_specs=[pl.BlockSpec((tq, 128), lambda h, i: (i, h)),
                  pl.BlockSpec((S, 128), lambda h, i: (0, H + h)),
                  pl.BlockSpec((S, 128), lambda h, i: (0, 2 * H + h)),
                  pl.BlockSpec((None, tq, 1), lambda h, i: (h, i, 0)),
                  pl.BlockSpec((None, 1, S), lambda h, i: (h, 0, 0))],
        out_specs=[pl.BlockSpec((tq, 128), lambda h, i: (i, h)),
                   pl.BlockSpec((None, tq, 1), lambda h, i: (h, i, 0))],
        out_shape=[jax.ShapeDtypeStruct((S, FOX_WIDTH), F32), jax.ShapeDtypeStruct((H, S, 1), F32)],
        compiler_params=_cparams(("parallel", "parallel")),
    )(P, P, P, ccol, crow)


def _fox_bwd(P, ccol, crow, o, lse, dcat, *, name):
    S = P.shape[0]
    tq = _pick(S, 256)
    H = FOX_HEADS
    nq = S // tq
    scale = 1.0 / math.sqrt(FOX_HEAD_DIM)

    def body(q_ref, k_ref, v_ref, cc_ref, cr_ref, o_ref, l_ref, do_ref,
             dq_ref, dk_ref, dv_ref, dcc_ref, dcr_ref, dk_acc, dv_acc):
        qi = pl.program_id(1)
        s, causal = _fox_scores(q_ref, k_ref, cc_ref, cr_ref, qi, tq, S)
        p = jnp.where(causal, jnp.exp(s - l_ref[...]), 0.0)
        do = do_ref[...]
        do_bf = do.astype(BF16)
        dp = lax.dot_general(do_bf, v_ref[...].astype(BF16), (((1,), (1,)), ((), ())), preferred_element_type=F32)
        delta = jnp.sum(do * o_ref[...], axis=-1, keepdims=True)
        ds = p * (dp - delta)
        ds_bf = ds.astype(BF16)
        dq_ref[...] = (jnp.dot(ds_bf, k_ref[...].astype(BF16), preferred_element_type=F32) * scale).astype(BF16)
        dkp = lax.dot_general(ds_bf, q_ref[...].astype(BF16), (((0,), (0,)), ((), ())),
                              preferred_element_type=F32) * scale
        dvp = lax.dot_general(p.astype(BF16), do_bf, (((0,), (0,)), ((), ())), preferred_element_type=F32)
        dcc_ref[...] = jnp.sum(ds, axis=-1, keepdims=True)
        dcr = jnp.sum(ds, axis=0, keepdims=True)

        @pl.when(qi == 0)
        def _():
            dk_acc[...] = dkp
            dv_acc[...] = dvp
            dcr_ref[...] = dcr

        @pl.when(qi > 0)
        def _():
            dk_acc[...] += dkp
            dv_acc[...] += dvp
            dcr_ref[...] += dcr

        @pl.when(qi == nq - 1)
        def _():
            dk_ref[...] = dk_acc[...].astype(BF16)
            dv_ref[...] = dv_acc[...].astype(BF16)

    qblk = pl.BlockSpec((tq, 128), lambda h, i: (i, h))
    kvo = pl.BlockSpec((S, 128), lambda h, i: (0, h))
    col = pl.BlockSpec((None, tq, 1), lambda h, i: (h, i, 0))
    rowv = pl.BlockSpec((None, 1, S), lambda h, i: (h, 0, 0))
    return _call(
        body, name=name, grid=(H, nq),
        in_specs=[qblk,
                  pl.BlockSpec((S, 128), lambda h, i: (0, H + h)),
                  pl.BlockSpec((S, 128), lambda h, i: (0, 2 * H + h)),
                  col, rowv, qblk, col, qblk],
        out_specs=[qblk, kvo, kvo, col, rowv],
        out_shape=[jax.ShapeDtypeStruct((S, FOX_WIDTH), BF16)] * 3
        + [jax.ShapeDtypeStruct((H, S, 1), F32), jax.ShapeDtypeStruct((H, 1, S), F32)],
        scratch_shapes=[pltpu.VMEM((S, 128), F32), pltpu.VMEM((S, 128), F32)],
        compiler_params=_cparams(("parallel", "arbitrary")),
    )(P, P, P, ccol, crow, o, lse, dcat)


def _s5_disc_fwd(lr, li, ls, *, name, after=()):
    G, Pn = lr.shape

    def body(lr_ref, li_ref, ls_ref, ar_ref, ai_ref, gr_ref, gi_ref):
        lr_, li_ = lr_ref[...], li_ref[...]
        dt = jnp.exp(ls_ref[...])
        mag = jnp.exp(lr_ * dt)
        th = li_ * dt
        ar = mag * jnp.cos(th)
        ai = mag * jnp.sin(th)
        den = lr_ * lr_ + li_ * li_
        xr = ar - 1.0
        ar_ref[...] = ar
        ai_ref[...] = ai
        gr_ref[...] = (xr * lr_ + ai * li_) / den
        gi_ref[...] = (ai * lr_ - xr * li_) / den

    sq = pl.BlockSpec((G, Pn), lambda: (0, 0))
    return _call(
        body, after=after, name=name, in_specs=[sq, sq, pl.BlockSpec((G, 1), lambda: (0, 0))], out_specs=[sq] * 4,
        out_shape=[jax.ShapeDtypeStruct((G, Pn), F32)] * 4,
    )(lr, li, ls)


def _s5_disc_bwd(lr, li, ls, dar, dai, dgr, dgi, *, name):
    G, Pn = lr.shape

    def body(lr_ref, li_ref, ls_ref, dar_ref, dai_ref, dgr_ref, dgi_ref, dlr_ref, dli_ref, dls_ref):
        lr_, li_ = lr_ref[...], li_ref[...]
        dt = jnp.exp(ls_ref[...])
        mag = jnp.exp(lr_ * dt)
        th = li_ * dt
        ar = mag * jnp.cos(th)
        ai = mag * jnp.sin(th)
        den = lr_ * lr_ + li_ * li_
        xr = ar - 1.0
        xi = ai
        g_re = (xr * lr_ + xi * li_) / den
        g_im = (xi * lr_ - xr * li_) / den
        dgr_, dgi_ = dgr_ref[...], dgi_ref[...]
        dxr = (dgr_ * lr_ - dgi_ * li_) / den
        dxi = (dgr_ * li_ + dgi_ * lr_) / den
        dden = -(dgr_ * g_re + dgi_ * g_im) / den
        dlr = (dgr_ * xr + dgi_ * xi) / den + 2.0 * dden * lr_
        dli = (dgr_ * xi - dgi_ * xr) / den + 2.0 * dden * li_
        da_r = dar_ref[...] + dxr
        da_i = dai_ref[...] + dxi
        dmag_mag = da_r * ar + da_i * ai
        dth = da_i * ar - da_r * ai
        dlr_ref[...] = dlr + dmag_mag * dt
        dli_ref[...] = dli + dth * dt
        ddt = jnp.sum(dmag_mag * lr_ + dth * li_, axis=-1, keepdims=True)
        dls_ref[...] = ddt * dt

    sq = pl.BlockSpec((G, Pn), lambda: (0, 0))
    c1 = pl.BlockSpec((G, 1), lambda: (0, 0))
    return _call(
        body, name=name, in_specs=[sq, sq, c1, sq, sq, sq, sq], out_specs=[sq, sq, c1],
        out_shape=[jax.ShapeDtypeStruct((G, Pn), F32)] * 2 + [jax.ShapeDtypeStruct((G, 1), F32)],
    )(lr, li, ls, dar, dai, dgr, dgi)


def _s5_bb_fwd(gr, gi, br, bi, *, name):
    R, C = br.shape

    def body(gr_ref, gi_ref, br_ref, bi_ref, or_ref, oi_ref):
        g_r, g_i, b_r, b_i = gr_ref[...], gi_ref[...], br_ref[...], bi_ref[...]
        or_ref[...] = g_r * b_r - g_i * b_i
        oi_ref[...] = g_r * b_i + g_i * b_r

    w = pl.BlockSpec((R, C), lambda: (0, 0))
    c1 = pl.BlockSpec((R, 1), lambda: (0, 0))
    return _call(body, name=name, in_specs=[c1, c1, w, w], out_specs=[w, w],
                 out_shape=[jax.ShapeDtypeStruct((R, C), F32)] * 2)(gr, gi, br, bi)


def _s5_bb_bwd(gr, gi, br, bi, dbbr, dbbi, *, name):
    R, C = br.shape

    def body(gr_ref, gi_ref, br_ref, bi_ref, dr_ref, di_ref, dbr_ref, dbi_ref, dgr_ref, dgi_ref):
        g_r, g_i, b_r, b_i = gr_ref[...], gi_ref[...], br_ref[...], bi_ref[...]
        d_r, d_i = dr_ref[...], di_ref[...]
        dbr_ref[...] = g_r * d_r + g_i * d_i
        dbi_ref[...] = g_r * d_i - g_i * d_r
        dgr_ref[...] = jnp.sum(d_r * b_r + d_i * b_i, axis=-1, keepdims=True)
        dgi_ref[...] = jnp.sum(d_i * b_r - d_r * b_i, axis=-1, keepdims=True)

    w = pl.BlockSpec((R, C), lambda: (0, 0))
    c1 = pl.BlockSpec((R, 1), lambda: (0, 0))
    return _call(body, name=name, in_specs=[c1, c1, w, w, w, w], out_specs=[w, w, c1, c1],
                 out_shape=[jax.ShapeDtypeStruct((R, C), F32)] * 2 + [jax.ShapeDtypeStruct((R, 1), F32)] * 2,
                 )(gr, gi, br, bi, dbbr, dbbi)


_DIAG_TILE = 8


def _diag_mask(gr, gc):
    rows, cols = _DIAG_TILE * gr, _DIAG_TILE * gc
    r = lax.broadcasted_iota(jnp.int32, (rows, cols), 0) >> (gr.bit_length() - 1)
    c = lax.broadcasted_iota(jnp.int32, (rows, cols), 1) >> (gc.bit_length() - 1)
    return r == c


def _diag_expand(t2, gr, gc, *, name, after=()):
    _, R, _ = t2.shape
    G = R // gr
    nt = G // _DIAG_TILE
    rows, cols = _DIAG_TILE * gr, _DIAG_TILE * gc

    def body(t_ref, o_ref):
        src = lax.broadcasted_iota(jnp.int32, (gc, cols), 0)
        dst = lax.broadcasted_iota(jnp.int32, (gc, cols), 1) & (gc - 1)
        spread = (src == dst).astype(BF16)
        y = jnp.dot(t_ref[...].astype(BF16), spread, preferred_element_type=F32)
        o_ref[...] = jnp.where(_diag_mask(gr, gc), y, 0.0).astype(BF16)

    return _call(
        body, after=after, name=name, grid=(2, nt),
        in_specs=[pl.BlockSpec((None, rows, gc), lambda p, i: (p, i, 0))],
        out_specs=pl.BlockSpec((None, rows, cols), lambda p, i: (p, i, i)),
        out_shape=jax.ShapeDtypeStruct((2, R, G * gc), BF16),
        compiler_params=_cparams(("parallel",) * 2),
    )(t2)


def _diag_extract(xd, gr, gc, *, name):
    _, R, _ = xd.shape
    nt = R // gr // _DIAG_TILE
    rows, cols = _DIAG_TILE * gr, _DIAG_TILE * gc

    def body(x_ref, o_ref):
        src = lax.broadcasted_iota(jnp.int32, (cols, gc), 0) & (gc - 1)
        dst = lax.broadcasted_iota(jnp.int32, (cols, gc), 1)
        fold = (src == dst).astype(BF16)
        parts = _split3(jnp.where(_diag_mask(gr, gc), x_ref[...], 0.0))
        acc = jnp.dot(parts[2], fold, preferred_element_type=F32)
        acc = acc + jnp.dot(parts[1], fold, preferred_element_type=F32)
        o_ref[...] = acc + jnp.dot(parts[0], fold, preferred_element_type=F32)

    return _call(
        body, name=name, grid=(2, nt),
        in_specs=[pl.BlockSpec((None, rows, cols), lambda p, i: (p, i, i))],
        out_specs=pl.BlockSpec((None, rows, gc), lambda p, i: (p, i, 0)),
        out_shape=jax.ShapeDtypeStruct((2, R, gc), F32),
        compiler_params=_cparams(("parallel",) * 2),
    )(xd)


SCAN_BLOCK = 8


def _cpowers(ar, ai, sign):
    ai = sign * ai
    out = [(ar, ai)]
    for _ in range(SCAN_BLOCK - 1):
        pr, pi = out[-1]
        out.append((pr * ar - pi * ai, pr * ai + pi * ar))
    return out


def _row_table(pw, row, index_of_row):
    tr_ = jnp.broadcast_to(pw[index_of_row(0)][0], row.shape)
    ti_ = jnp.broadcast_to(pw[index_of_row(0)][1], row.shape)
    for r in range(1, SCAN_BLOCK):
        pr, pi = pw[index_of_row(r)]
        tr_ = jnp.where(row == r, pr, tr_)
        ti_ = jnp.where(row == r, pi, ti_)
    return tr_, ti_


def _s5_scan_fwd(bu, a, *, name):
    _, S, N = bu.shape
    tc = 512
    nt = N // tc

    def body(a_ref, b_ref, h_ref):
        pw = _cpowers(a_ref[0], a_ref[1], 1.0)
        row = lax.broadcasted_iota(jnp.int32, (SCAN_BLOCK, tc), 0)
        lead_r, lead_i = _row_table(pw, row, lambda r: r)

        def step(k, carry):
            cr, ci = carry
            rows = pl.ds(pl.multiple_of(k * SCAN_BLOCK, SCAN_BLOCK), SCAN_BLOCK)
            xr, xi = b_ref[0, rows, :], b_ref[1, rows, :]
            for sh in (1, 2, 4):
                keep = row >= sh
                sr = jnp.where(keep, pltpu.roll(xr, sh, 0), 0.0)
                si = jnp.where(keep, pltpu.roll(xi, sh, 0), 0.0)
                kr, ki = pw[sh - 1]
                xr, xi = xr + kr * sr - ki * si, xi + kr * si + ki * sr
            h_ref[0, rows, :] = xr + lead_r * cr - lead_i * ci
            h_ref[1, rows, :] = xi + lead_r * ci + lead_i * cr
            last = row == SCAN_BLOCK - 1
            tr_ = jnp.sum(jnp.where(last, xr, 0.0), axis=0, keepdims=True)
            ti_ = jnp.sum(jnp.where(last, xi, 0.0), axis=0, keepdims=True)
            a8r, a8i = pw[SCAN_BLOCK - 1]
            return a8r * cr - a8i * ci + tr_, a8r * ci + a8i * cr + ti_

        z = jnp.zeros((1, tc), F32)
        lax.fori_loop(0, S // SCAN_BLOCK, step, (z, z), unroll=2)

    vec = pl.BlockSpec((2, 1, tc), lambda j: (0, 0, j))
    mat = pl.BlockSpec((2, S, tc), lambda j: (0, 0, j))
    return _call(
        body, name=name, grid=(nt,), in_specs=[vec, mat], out_specs=mat,
        out_shape=jax.ShapeDtypeStruct((2, S, N), F32),
        compiler_params=_cparams(("parallel",)),
    )(a, bu)


def _s5_scan_bwd(g, h, a, *, name):
    _, S, N = g.shape
    tc = 256
    nt = N // tc

    def body(a_ref, g_ref, h_ref, l_ref, da_ref):
        pw = _cpowers(a_ref[0], a_ref[1], -1.0)
        row = lax.broadcasted_iota(jnp.int32, (SCAN_BLOCK, tc), 0)
        tail_r, tail_i = _row_table(pw, row, lambda r: SCAN_BLOCK - 1 - r)
        nb = S // SCAN_BLOCK

        def step(i, carry):
            k = nb - 1 - i
            cr, ci, dar, dai = carry
            rows = pl.ds(pl.multiple_of(k * SCAN_BLOCK, SCAN_BLOCK), SCAN_BLOCK)
            xr, xi = g_ref[0, rows, :], g_ref[1, rows, :]
            for sh in (1, 2, 4):
                keep = row < SCAN_BLOCK - sh
                sr = jnp.where(keep, pltpu.roll(xr, SCAN_BLOCK - sh, 0), 0.0)
                si = jnp.where(keep, pltpu.roll(xi, SCAN_BLOCK - sh, 0), 0.0)
                kr, ki = pw[sh - 1]
                xr, xi = xr + kr * sr - ki * si, xi + kr * si + ki * sr
            lr = xr + tail_r * cr - tail_i * ci
            li = xi + tail_r * ci + tail_i * cr
            l_ref[0, rows, :] = lr
            l_ref[1, rows, :] = li
            prev = pl.ds(pl.multiple_of(jnp.maximum(k - 1, 0) * SCAN_BLOCK, SCAN_BLOCK), SCAN_BLOCK)
            has_prev = jnp.where(k > 0, 1.0, 0.0).astype(F32)
            first = row == 0
            hpr = jnp.where(first, pltpu.roll(h_ref[0, prev, :], 1, 0) * has_prev, pltpu.roll(h_ref[0, rows, :], 1, 0))
            hpi = jnp.where(first, pltpu.roll(h_ref[1, prev, :], 1, 0) * has_prev, pltpu.roll(h_ref[1, rows, :], 1, 0))
            tr_ = jnp.sum(jnp.where(first, xr, 0.0), axis=0, keepdims=True)
            ti_ = jnp.sum(jnp.where(first, xi, 0.0), axis=0, keepdims=True)
            a8r, a8i = pw[SCAN_BLOCK - 1]
            return (a8r * cr - a8i * ci + tr_, a8r * ci + a8i * cr + ti_,
                    dar + lr * hpr + li * hpi, dai + li * hpr - lr * hpi)

        z = jnp.zeros((1, tc), F32)
        z8 = jnp.zeros((SCAN_BLOCK, tc), F32)
        _, _, dar, dai = lax.fori_loop(0, nb, step, (z, z, z8, z8), unroll=2)
        da_ref[0] = jnp.sum(dar, axis=0, keepdims=True)
        da_ref[1] = jnp.sum(dai, axis=0, keepdims=True)

    vec = pl.BlockSpec((2, 1, tc), lambda j: (0, 0, j))
    mat = pl.BlockSpec((2, S, tc), lambda j: (0, 0, j))
    return _call(
        body, name=name, grid=(nt,), in_specs=[vec, mat, mat], out_specs=[mat, vec],
        out_shape=[jax.ShapeDtypeStruct((2, S, N), F32), jax.ShapeDtypeStruct((2, 1, N), F32)],
        compiler_params=_cparams(("parallel",)),
    )(a, g, h)


_GELU_C = math.sqrt(2.0 / math.pi)


def _s5_out_fwd(yc, P, dskip, *, name):
    S, W = yc.shape
    tr = _pick(S, 256)
    ub = 3 * FOX_WIDTH // W

    def body(yc_ref, u_ref, d_ref, y_ref, yg_ref):
        y = yc_ref[...] + d_ref[...] * u_ref[...]
        y_ref[...] = y
        t = jnp.tanh(_GELU_C * (y + 0.044715 * y * y * y))
        yg_ref[...] = (0.5 * y * (1.0 + t)).astype(BF16)

    row = pl.BlockSpec((tr, W), lambda i: (i, 0))
    return _call(
        body, name=name, grid=(S // tr,),
        in_specs=[row, pl.BlockSpec((tr, W), lambda i: (i, ub)), pl.BlockSpec((1, W), lambda i: (0, 0))],
        out_specs=[row, row],
        out_shape=[jax.ShapeDtypeStruct((S, W), F32), jax.ShapeDtypeStruct((S, W), BF16)],
        compiler_params=_cparams(("parallel",)),
    )(yc, P, dskip)


def _s5_out_bwd(dyg, y, P, dskip, *, name):
    S, W = y.shape
    tr = _pick(S, 256)
    ub = 3 * FOX_WIDTH // W

    def body(dyg_ref, y_ref, u_ref, d_ref, dy_ref, du_ref, dd_ref):
        y_ = y_ref[...]
        inner = _GELU_C * (y_ + 0.044715 * y_ * y_ * y_)
        t = jnp.tanh(inner)
        dgelu = 0.5 * (1.0 + t) + 0.5 * y_ * (1.0 - t * t) * _GELU_C * (1.0 + 3.0 * 0.044715 * y_ * y_)
        dy = dyg_ref[...] * dgelu
        dy_ref[...] = dy.astype(BF16)
        du_ref[...] = d_ref[...] * dy
        part = jnp.sum(dy * u_ref[...], axis=0, keepdims=True)

        @pl.when(pl.program_id(0) == 0)
        def _():
            dd_ref[...] = part

        @pl.when(pl.program_id(0) > 0)
        def _():
            dd_ref[...] += part

    row = pl.BlockSpec((tr, W), lambda i: (i, 0))
    vec = pl.BlockSpec((1, W), lambda i: (0, 0))
    return _call(
        body, name=name, grid=(S // tr,),
        in_specs=[row, row, pl.BlockSpec((tr, W), lambda i: (i, ub)), vec],
        out_specs=[row, row, vec],
        out_shape=[jax.ShapeDtypeStruct((S, W), BF16), jax.ShapeDtypeStruct((S, W), F32),
                   jax.ShapeDtypeStruct((1, W), F32)],
        compiler_params=_cparams(("arbitrary",)),
    )(dyg, y, P, dskip)


def _glu_fwd(z, *, name):
    S, W2 = z.shape
    W = W2 // 2
    tr = _pick(S, 256)

    def body(z1_ref, z2_ref, o_ref):
        o_ref[...] = (z1_ref[...] * jax.nn.sigmoid(z2_ref[...])).astype(BF16)

    return _call(
        body, name=name, grid=(S // tr,),
        in_specs=[pl.BlockSpec((tr, W), lambda i: (i, 0)), pl.BlockSpec((tr, W), lambda i: (i, 1))],
        out_specs=pl.BlockSpec((tr, W), lambda i: (i, 0)),
        out_shape=jax.ShapeDtypeStruct((S, W), BF16),
        compiler_params=_cparams(("parallel",)),
    )(z, z)


def _glu_bwd(z, dcat, *, name):
    S, W2 = z.shape
    W = W2 // 2
    tr = _pick(S, 256)

    def body(z1_ref, z2_ref, d_ref, dz1_ref, dz2_ref):
        sg = jax.nn.sigmoid(z2_ref[...])
        d = d_ref[...]
        dz1_ref[...] = (d * sg).astype(BF16)
        dz2_ref[...] = (d * z1_ref[...] * sg * (1.0 - sg)).astype(BF16)

    lo = pl.BlockSpec((tr, W), lambda i: (i, 0))
    hi = pl.BlockSpec((tr, W), lambda i: (i, 1))
    dz1, dz2 = _call(
        body, name=name, grid=(S // tr,), in_specs=[lo, hi, hi], out_specs=[lo, lo],
        out_shape=[jax.ShapeDtypeStruct((S, W), BF16)] * 2,
        compiler_params=_cparams(("parallel",)),
    )(z, z, dcat)
    return jnp.concatenate([dz1, dz2], axis=1)


ACT_ROWS = 16
ACT_COLS = 256


def _shift_down(cur, prev, k, row):
    return jnp.where(row >= k, pltpu.roll(cur, k, 0), pltpu.roll(prev, k, 0))


def _shift_up(cur, nxt, k, row):
    n = cur.shape[0]
    return jnp.where(row < n - k, pltpu.roll(cur, n - k, 0), pltpu.roll(nxt, n - k, 0))


def _act_fwd(h, cw, cb, *, name):
    _, S, FP = h.shape
    tr = _pick(S, 256)
    hb = tr // ACT_ROWS
    nq = tr // ACT_ROWS

    def body(g_ref, gh_ref, v_ref, vh_ref, wg_ref, wv_ref, bg_ref, bv_ref, a_ref, hc_ref):
        first = pl.program_id(1) == 0
        for c0 in range(0, FP, ACT_COLS):
            cw_ = min(ACT_COLS, FP - c0)
            cols = pl.ds(c0, cw_)
            rw = lax.broadcasted_iota(jnp.int32, (ACT_ROWS, cw_), 0)
            wg = [wg_ref[pl.ds(k, 1), cols] for k in range(3)]
            wv = [wv_ref[pl.ds(k, 1), cols] for k in range(3)]
            bg, bv = bg_ref[:, cols], bv_ref[:, cols]
            halo_g = jnp.where(first, 0.0, gh_ref[:, cols])
            halo_v = jnp.where(first, 0.0, vh_ref[:, cols])

            def chunk(q, _):
                rows = pl.ds(pl.multiple_of(q * ACT_ROWS, ACT_ROWS), ACT_ROWS)
                before = pl.ds(pl.multiple_of(jnp.maximum(q - 1, 0) * ACT_ROWS, ACT_ROWS), ACT_ROWS)
                g, v = g_ref[rows, cols], v_ref[rows, cols]
                gp = jnp.where(q > 0, g_ref[before, cols], halo_g)
                vp = jnp.where(q > 0, v_ref[before, cols], halo_v)
                cg = bg + wg[2] * g + wg[1] * _shift_down(g, gp, 1, rw) + wg[0] * _shift_down(g, gp, 2, rw)
                cv = bv + wv[2] * v + wv[1] * _shift_down(v, vp, 1, rw) + wv[0] * _shift_down(v, vp, 2, rw)
                a_ref[rows, cols] = (cg * jax.nn.sigmoid(cg) * cv).astype(BF16)
                hc_ref[0, rows, cols] = cg
                hc_ref[1, rows, cols] = cv
                return 0

            lax.fori_loop(0, nq, chunk, 0, unroll=2)

    def main(off):
        return pl.BlockSpec((None, tr, FP), lambda j, i: (j + off, i, 0))

    def halo(off):
        return pl.BlockSpec((None, ACT_ROWS, FP), lambda j, i: (j + off, jnp.maximum(i * hb - 1, 0), 0))

    def wspec(off):
        return pl.BlockSpec((None, 3, FP), lambda j, i: (j + off, 0, 0))

    def bspec(off):
        return pl.BlockSpec((None, 1, FP), lambda j, i: (j + off, 0, 0))

    cb3 = cb.reshape(4, 1, FP)
    return _call(
        body, name=name, grid=(2, S // tr),
        in_specs=[main(0), halo(0), main(2), halo(2), wspec(0), wspec(2), bspec(0), bspec(2)],
        out_specs=[pl.BlockSpec((None, tr, FP), lambda j, i: (j, i, 0)),
                   pl.BlockSpec((None, 2, tr, FP), lambda j, i: (j, 0, i, 0))],
        out_shape=[jax.ShapeDtypeStruct((2, S, FP), BF16), jax.ShapeDtypeStruct((2, 2, S, FP), F32)],
        compiler_params=_cparams(("parallel", "parallel")),
    )(h, h, h, h, cw, cw, cb3, cb3)


def _act_bwd(h, hc, da, cw, *, name):
    _, S, FP = h.shape
    tr = _pick(S, 256)
    nq = tr // ACT_ROWS
    nr = S // tr
    half = ACT_ROWS // 2

    def fold(x):
        return x[:half] + x[half:]

    def body(g_ref, v_ref, hc_ref, da_ref, wg_ref, wv_ref,
             dh_ref, dwg_ref, dwv_ref, dbg_ref, dbv_ref, carry_g, carry_v):
        i = pl.program_id(1)
        bottom = i == 0
        for c0 in range(0, FP, ACT_COLS):
            cw_ = min(ACT_COLS, FP - c0)
            cols = pl.ds(c0, cw_)
            rw = lax.broadcasted_iota(jnp.int32, (ACT_ROWS, cw_), 0)
            wg = [wg_ref[pl.ds(k, 1), cols] for k in range(3)]
            wv = [wv_ref[pl.ds(k, 1), cols] for k in range(3)]
            after_g = jnp.where(bottom, 0.0, carry_g[:, cols])
            after_v = jnp.where(bottom, 0.0, carry_v[:, cols])

            def chunk(s, carry):
                ng, nv, acc = carry[0], carry[1], carry[2:]
                q = nq - 1 - s
                rows = pl.ds(pl.multiple_of(q * ACT_ROWS, ACT_ROWS), ACT_ROWS)
                g, v = g_ref[rows, cols], v_ref[rows, cols]
                cg, cv = hc_ref[0, rows, cols], hc_ref[1, rows, cols]
                sg = jax.nn.sigmoid(cg)
                d = da_ref[rows, cols]
                dcg = d * cv * sg * (1.0 + cg * (1.0 - sg))
                dcv = d * cg * sg
                ug1, ug2 = _shift_up(dcg, ng, 1, rw), _shift_up(dcg, ng, 2, rw)
                uv1, uv2 = _shift_up(dcv, nv, 1, rw), _shift_up(dcv, nv, 2, rw)
                dh_ref[0, rows, cols] = (wg[2] * dcg + wg[1] * ug1 + wg[0] * ug2).astype(BF16)
                dh_ref[1, rows, cols] = (wv[2] * dcv + wv[1] * uv1 + wv[0] * uv2).astype(BF16)
                terms = (ug2 * g, ug1 * g, dcg * g, dcg, uv2 * v, uv1 * v, dcv * v, dcv)
                return (dcg, dcv) + tuple(a + fold(t) for a, t in zip(acc, terms))

            zero = jnp.zeros((half, cw_), F32)
            out = lax.fori_loop(0, nq, chunk, (after_g, after_v) + (zero,) * 8, unroll=2)
            carry_g[:, cols] = out[0]
            carry_v[:, cols] = out[1]
            sums = [jnp.sum(a, axis=0, keepdims=True) for a in out[2:]]

            @pl.when(bottom)
            def _():
                for k in range(3):
                    dwg_ref[pl.ds(k, 1), cols] = sums[k]
                    dwv_ref[pl.ds(k, 1), cols] = sums[4 + k]
                dbg_ref[:, cols] = sums[3]
                dbv_ref[:, cols] = sums[7]

            @pl.when(jnp.logical_not(bottom))
            def _():
                for k in range(3):
                    dwg_ref[pl.ds(k, 1), cols] += sums[k]
                    dwv_ref[pl.ds(k, 1), cols] += sums[4 + k]
                dbg_ref[:, cols] += sums[3]
                dbv_ref[:, cols] += sums[7]

    def main(off):
        return pl.BlockSpec((None, tr, FP), lambda j, i: (j + off, nr - 1 - i, 0))

    def wspec(off):
        return pl.BlockSpec((None, 3, FP), lambda j, i: (j + off, 0, 0))

    bspec = pl.BlockSpec((None, 1, FP), lambda j, i: (j, 0, 0))
    pair = pl.BlockSpec((None, 2, tr, FP), lambda j, i: (j, 0, nr - 1 - i, 0))
    dh, dwg, dwv, dbg, dbv = _call(
        body, name=name, grid=(2, nr),
        in_specs=[main(0), main(2), pair, main(0), wspec(0), wspec(2)],
        out_specs=[pair, wspec(0), wspec(0), bspec, bspec],
        out_shape=[jax.ShapeDtypeStruct((2, 2, S, FP), BF16)]
        + [jax.ShapeDtypeStruct((2, 3, FP), F32)] * 2 + [jax.ShapeDtypeStruct((2, 1, FP), F32)] * 2,
        scratch_shapes=[pltpu.VMEM((ACT_ROWS, FP), F32), pltpu.VMEM((ACT_ROWS, FP), F32)],
        compiler_params=_cparams(("parallel", "arbitrary")),
    )(h, h, hc, da, cw, cw)
    return (dh.reshape(4, S, FP), jnp.concatenate([dwg, dwv], axis=0), jnp.concatenate([dbg, dbv], axis=0))


def _rope_tables(posf, *, name, after=()):
    S = posf.shape[0]
    half = ROPE_DIM // 2
    d = np.arange(LANE) % SWA_HEAD_DIM
    invf = np.where(d < ROPE_DIM, ROPE_THETA ** (-(d % half).astype(np.float64) / half), 0.0).astype(np.float32)
    m_rot = (d < ROPE_DIM).astype(np.float32)
    m_a = (d < half).astype(np.float32)
    m_b = ((d >= half) & (d < ROPE_DIM)).astype(np.float32)
    consts = jnp.asarray(np.stack([invf, m_rot, m_a, m_b] + [np.zeros(LANE, np.float32)] * 4))

    def body(p_ref, k_ref, c_ref, sa_ref, sb_ref):
        k = k_ref[...]
        ang = p_ref[...] * k[0:1]
        co, si = jnp.cos(ang), jnp.sin(ang)
        c_ref[...] = k[1:2] * co + (1.0 - k[1:2])
        sa_ref[...] = -k[2:3] * si
        sb_ref[...] = k[3:4] * si

    full = pl.BlockSpec((S, LANE), lambda: (0, 0))
    return _call(
        body, after=after, name=name,
        in_specs=[pl.BlockSpec((S, 1), lambda: (0, 0)), pl.BlockSpec((8, LANE), lambda: (0, 0))],
        out_specs=[full] * 3, out_shape=[jax.ShapeDtypeStruct((S, LANE), F32)] * 3,
    )(posf, consts)


def _rope(xv, tabs_refs, width, inverse):
    rep = width // LANE
    c, sa, sb = (jnp.tile(t[...], (1, rep)) for t in tabs_refs)
    if not inverse:
        return xv * c + pltpu.roll(xv, width - 8, 1) * sa + pltpu.roll(xv, 8, 1) * sb
    return xv * c + pltpu.roll(xv * sa, 8, 1) + pltpu.roll(xv * sb, width - 8, 1)


def _to_heads(x, tabs, *, col0, width, rotate, name, out_dtype):
    S = x.shape[0]
    tr = _pick(S, 256)
    nh = width // SWA_HEAD_DIM
    cb = col0 // width

    def body(x_ref, c_ref, sa_ref, sb_ref, o_ref):
        xv = x_ref[...].astype(F32)
        if rotate:
            xv = _rope(xv, (c_ref, sa_ref, sb_ref), width, False)
        for h in range(nh):
            o_ref[h] = xv[:, h * SWA_HEAD_DIM:(h + 1) * SWA_HEAD_DIM].astype(out_dtype)

    tab = pl.BlockSpec((tr, LANE), lambda i: (i, 0))
    return _call(
        body, name=name, grid=(S // tr,),
        in_specs=[pl.BlockSpec((tr, width), lambda i: (i, cb)), tab, tab, tab],
        out_specs=pl.BlockSpec((nh, tr, SWA_HEAD_DIM), lambda i: (0, i, 0)),
        out_shape=jax.ShapeDtypeStruct((nh, S, SWA_HEAD_DIM), out_dtype),
        compiler_params=_cparams(("parallel",)),
    )(x, *tabs)


def _from_heads(x3, tabs, *, rotate_back, name, out_dtype, skip_rows=0):
    nh = x3.shape[0]
    S = x3.shape[1] - skip_rows
    width = nh * SWA_HEAD_DIM
    tr = _pick(S, 256) if skip_rows == 0 else skip_rows
    off = skip_rows // tr

    def body(x_ref, c_ref, sa_ref, sb_ref, o_ref):
        xv = jnp.concatenate([x_ref[h].astype(F32) for h in range(nh)], axis=1)
        if rotate_back:
            xv = _rope(xv, (c_ref, sa_ref, sb_ref), width, True)
        o_ref[...] = xv.astype(out_dtype)

    tab = pl.BlockSpec((tr, LANE), lambda i: (i, 0))
    return _call(
        body, name=name, grid=(S // tr,),
        in_specs=[pl.BlockSpec((nh, tr, SWA_HEAD_DIM), lambda i: (0, i + off, 0)), tab, tab, tab],
        out_specs=pl.BlockSpec((tr, width), lambda i: (i, 0)),
        out_shape=jax.ShapeDtypeStruct((S, width), out_dtype),
        compiler_params=_cparams(("parallel",)),
    )(x3, *tabs)


def _swa_mask(n):
    rows = SWA_GROUPS * SWA_WINDOW
    qi = lax.broadcasted_iota(jnp.int32, (rows, 2 * SWA_WINDOW), 0) & (SWA_WINDOW - 1)
    kj = lax.broadcasted_iota(jnp.int32, (rows, 2 * SWA_WINDOW), 1)
    rel = SWA_WINDOW + qi - kj
    return (rel >= 0) & (rel < SWA_WINDOW) & ((n > 0) | (kj >= SWA_WINDOW))


def _swa_fwd(qT, kT, vT, sink_rows, *, name):
    S = qT.shape[1]
    W, G, Dh = SWA_WINDOW, SWA_GROUPS, SWA_HEAD_DIM
    nb = S // W
    scale = 1.0 / math.sqrt(Dh)

    def body(q_ref, kp_ref, kc_ref, vp_ref, vc_ref, s_ref, o_ref, l_ref):
        n = pl.program_id(1)
        q = q_ref[...].reshape(G * W, Dh)
        kk = jnp.concatenate([kp_ref[...], kc_ref[...]], axis=0)
        vv = jnp.concatenate([vp_ref[...], vc_ref[...]], axis=0)
        s = lax.dot_general(q, kk, (((1,), (1,)), ((), ())), preferred_element_type=F32) * scale
        s = jnp.where(_swa_mask(n), s, -1e30)
        sink = s_ref[...]
        m = jnp.maximum(jnp.max(s, axis=-1, keepdims=True), sink)
        e = jnp.exp(s - m)
        den = jnp.sum(e, axis=-1, keepdims=True) + jnp.exp(sink - m)
        p = e / den
        o_ref[...] = jnp.dot(p.astype(BF16), vv, preferred_element_type=F32).reshape(G, W, Dh)
        l_ref[...] = (m + jnp.log(den)).reshape(G, W, 1)

    qs = pl.BlockSpec((G, W, Dh), lambda g, n: (g, n, 0))
    prev = pl.BlockSpec((None, W, Dh), lambda g, n: (g, jnp.maximum(n - 1, 0), 0))
    cur = pl.BlockSpec((None, W, Dh), lambda g, n: (g, n, 0))
    return _call(
        body, name=name, grid=(SWA_KV_HEADS, nb),
        in_specs=[qs, prev, cur, prev, cur, pl.BlockSpec((None, G * W, 1), lambda g, n: (g, 0, 0))],
        out_specs=[qs, pl.BlockSpec((G, W, 1), lambda g, n: (g, n, 0))],
        out_shape=[jax.ShapeDtypeStruct((SWA_HEADS, S, Dh), F32), jax.ShapeDtypeStruct((SWA_HEADS, S, 1), F32)],
        compiler_params=_cparams(("parallel", "parallel")),
    )(qT, kT, kT, vT, vT, sink_rows)


def _swa_bwd(qT, kT, vT, sink_rows, oT, L, doT, *, name):
    S = qT.shape[1]
    W, G, Dh = SWA_WINDOW, SWA_GROUPS, SWA_HEAD_DIM
    nb = S // W
    scale = 1.0 / math.sqrt(Dh)

    def body(q_ref, kp_ref, kc_ref, vp_ref, vc_ref, s_ref, o_ref, l_ref, do_ref,
             dq_ref, dk_ref, dv_ref, ds_ref):
        n = pl.program_id(1)
        q = q_ref[...].reshape(G * W, Dh)
        kk = jnp.concatenate([kp_ref[...], kc_ref[...]], axis=0)
        vv = jnp.concatenate([vp_ref[...], vc_ref[...]], axis=0)
        s = lax.dot_general(q, kk, (((1,), (1,)), ((), ())), preferred_element_type=F32) * scale
        lrow = l_ref[...].reshape(G * W, 1)
        p = jnp.where(_swa_mask(n), jnp.exp(s - lrow), 0.0)
        do = do_ref[...].reshape(G * W, Dh)
        do_bf = do.astype(BF16)
        dp = lax.dot_general(do_bf, vv, (((1,), (1,)), ((), ())), preferred_element_type=F32)
        delta = jnp.sum(do * o_ref[...].reshape(G * W, Dh), axis=-1, keepdims=True)
        dsc = p * (dp - delta)
        ds_bf = dsc.astype(BF16)
        dq_ref[...] = (jnp.dot(ds_bf, kk, preferred_element_type=F32) * scale).astype(BF16).reshape(G, W, Dh)
        dkk = lax.dot_general(ds_bf, q, (((0,), (0,)), ((), ())), preferred_element_type=F32) * scale
        dvv = lax.dot_general(p.astype(BF16), do_bf, (((0,), (0,)), ((), ())), preferred_element_type=F32)
        dsk = -jnp.exp(s_ref[...] - lrow) * delta
        dsk = jnp.broadcast_to(jnp.sum(dsk.reshape(G, W, 1), axis=1), (G, LANE))

        @pl.when(n == 0)
        def _():
            dk_ref[...] = jnp.zeros_like(dk_ref)
            dv_ref[...] = jnp.zeros_like(dv_ref)
            ds_ref[...] = jnp.zeros_like(ds_ref)

        rows = pl.ds(pl.multiple_of(n * W, W), 2 * W)
        dk_ref[rows, :] += dkk
        dv_ref[rows, :] += dvv
        ds_ref[...] += dsk

    qs = pl.BlockSpec((G, W, Dh), lambda g, n: (g, n, 0))
    prev = pl.BlockSpec((None, W, Dh), lambda g, n: (g, jnp.maximum(n - 1, 0), 0))
    cur = pl.BlockSpec((None, W, Dh), lambda g, n: (g, n, 0))
    lsp = pl.BlockSpec((G, W, 1), lambda g, n: (g, n, 0))
    kvo = pl.BlockSpec((None, S + W, Dh), lambda g, n: (g, 0, 0))
    return _call(
        body, name=name, grid=(SWA_KV_HEADS, nb),
        in_specs=[qs, prev, cur, prev, cur, pl.BlockSpec((None, G * W, 1), lambda g, n: (g, 0, 0)), qs, lsp, qs],
        out_specs=[qs, kvo, kvo, pl.BlockSpec((None, G, LANE), lambda g, n: (g, 0, 0))],
        out_shape=[jax.ShapeDtypeStruct((SWA_HEADS, S, Dh), BF16),
                   jax.ShapeDtypeStruct((SWA_KV_HEADS, S + W, Dh), F32),
                   jax.ShapeDtypeStruct((SWA_KV_HEADS, S + W, Dh), F32),
                   jax.ShapeDtypeStruct((SWA_KV_HEADS, G, LANE), F32)],
        compiler_params=_cparams(("parallel", "arbitrary")),
    )(qT, kT, kT, vT, vT, sink_rows, oT, L, doT)


def _adamw(w, g, m, v, *, name, tr=128, by_cols=False):
    L, R, C = w.shape
    split = isinstance(g, (list, tuple))
    HR, HC = _half_shape(R, C, by_cols) if split else (R, C)
    tr, tc = _tile2d(HR, HC, tr)
    nr, nc = HR // tr, HC // tc
    c1 = 1.0 / (1.0 - ADAM_B1 ** ADAM_STEP)
    c2 = 1.0 / (1.0 - ADAM_B2 ** ADAM_STEP)
    ng = 2 * L if split else 1

    def body(c_ref, *refs):
        w_ref, g_refs, (m_ref, v_ref, go_ref, d_ref, mo_ref, vo_ref) = refs[0], refs[1:1 + ng], refs[1 + ng:]
        if split:
            mine = pl.program_id(1) == c_ref[0]
            g_ = jnp.where(mine, g_refs[0][...], g_refs[1][...])
            for l in range(1, L):
                g_ = jnp.where(pl.program_id(0) == l,
                               jnp.where(mine, g_refs[2 * l][...], g_refs[2 * l + 1][...]), g_)
        else:
            g_ = g_refs[0][...]
        mn = ADAM_B1 * m_ref[...] + (1.0 - ADAM_B1) * g_
        vn = ADAM_B2 * v_ref[...] + (1.0 - ADAM_B2) * (g_ * g_)
        go_ref[...] = g_
        mo_ref[...] = mn
        vo_ref[...] = vn
        d_ref[...] = -ADAM_LR * ((mn * c1) / (jnp.sqrt(vn * c2) + ADAM_EPS) + ADAM_WD * w_ref[...])

    def whole(l, hf, i, j, c):
        return (l, i, hf * nc + j) if by_cols else (l, hf * nr + i, j)

    def half(layer, own):
        def index(l, hf, i, j, c):
            used = (l == layer) & ((hf == c[0]) if own else (hf != c[0]))
            return jnp.where(used, i, 0), jnp.where(used, j, 0)
        return pl.BlockSpec((tr, tc), index)

    row = pl.BlockSpec((None, tr, tc), whole)
    gs = [h for pair in g for h in pair] if split else [g]
    g_specs = [half(l, own) for l in range(L) for own in (True, False)] if split else [row]
    core = lax.axis_index("c").astype(jnp.int32).reshape(1)
    return _call(
        body, name=name,
        grid_spec=pltpu.PrefetchScalarGridSpec(
            num_scalar_prefetch=1, grid=(L, 2 if split else 1, nr, nc),
            in_specs=[row] + g_specs + [row, row], out_specs=[row] * 4),
        out_shape=[jax.ShapeDtypeStruct((L, R, C), F32)] * 4,
        compiler_params=_cparams(("parallel",) * 4),
    )(core, w, *gs, m, v)


def _adamw_half(w, g, m, v, *, name, own, prev=None, tr=128, by_cols=False):
    L, R, C = w.shape
    HR, HC = _half_shape(R, C, by_cols)
    tr, tc = _tile2d(HR, HC, tr)
    nr, nc = HR // tr, HC // tc
    c1 = 1.0 / (1.0 - ADAM_B1 ** ADAM_STEP)
    c2 = 1.0 / (1.0 - ADAM_B2 ** ADAM_STEP)

    def body(c_ref, *refs):
        w_ref, g_refs, m_ref, v_ref = refs[0], refs[1:1 + L], refs[1 + L], refs[2 + L]
        go_ref, d_ref, mo_ref, vo_ref = refs[-4:]
        g_ = g_refs[0][...]
        for l in range(1, L):
            g_ = jnp.where(pl.program_id(0) == l, g_refs[l][...], g_)
        mn = ADAM_B1 * m_ref[...] + (1.0 - ADAM_B1) * g_
        vn = ADAM_B2 * v_ref[...] + (1.0 - ADAM_B2) * (g_ * g_)
        go_ref[...] = g_
        mo_ref[...] = mn
        vo_ref[...] = vn
        d_ref[...] = -ADAM_LR * ((mn * c1) / (jnp.sqrt(vn * c2) + ADAM_EPS) + ADAM_WD * w_ref[...])

    def whole(l, i, j, c):
        hf = c[0] if own else 1 - c[0]
        return (l, i, hf * nc + j) if by_cols else (l, hf * nr + i, j)

    def layer_half(layer):
        def index(l, i, j, c):
            return jnp.where(l == layer, i, 0), jnp.where(l == layer, j, 0)
        return pl.BlockSpec((tr, tc), index)

    row = pl.BlockSpec((None, tr, tc), whole)
    core = lax.axis_index("c").astype(jnp.int32).reshape(1)
    prev = list(prev) if prev is not None else []
    return _call(
        body, name=name,
        grid_spec=pltpu.PrefetchScalarGridSpec(
            num_scalar_prefetch=1, grid=(L, nr, nc),
            in_specs=[row] + [layer_half(l) for l in range(L)] + [row, row] + [ANY] * len(prev),
            out_specs=[row] * 4),
        out_shape=[jax.ShapeDtypeStruct((L, R, C), F32)] * 4,
        input_output_aliases={4 + L + k: k for k in range(len(prev))},
        compiler_params=_cparams(("parallel",) * 3),
    )(core, w, *g, m, v, *prev)


def _sum2_halves(g4, s4, by_cols, *, name):
    n, R, C = g4.shape
    HR, HC = _half_shape(R, C, by_cols)
    tr, tc = _tile2d(HR, HC, budget=1024 * 1024)
    nr, nc = HR // tr, HC // tc
    core = lax.axis_index("c").astype(jnp.int32).reshape(1)

    def body(c_ref, g_ref, s_ref, o_ref):
        o_ref[...] = (g_ref[...].astype(F32) + s_ref[...].astype(F32)).astype(BF16)

    def mine(k, i, j, c):
        return (k, i, c[0] * nc + j) if by_cols else (k, c[0] * nr + i, j)

    blk = pl.BlockSpec((None, tr, tc), lambda k, i, j, c: (k, i, j))
    return _call(
        body, name=name,
        grid_spec=pltpu.PrefetchScalarGridSpec(
            num_scalar_prefetch=1, grid=(n, nr, nc),
            in_specs=[pl.BlockSpec((None, tr, tc), mine), blk], out_specs=blk),
        out_shape=jax.ShapeDtypeStruct((n, HR, HC), BF16),
        compiler_params=_cparams(("parallel", "parallel", "parallel")),
    )(core, g4, s4)


def _rowsum(parts, *, name, out_dtype=F32):
    n, R, C = parts.shape
    tr, tc = _tile2d(R, C, budget=512 * 1024)

    def body(p_ref, o_ref):
        acc = p_ref[0].astype(F32)
        for i in range(1, n):
            acc = acc + p_ref[i].astype(F32)
        o_ref[...] = acc.astype(out_dtype)

    return _call(
        body, name=name, grid=(R // tr, C // tc),
        in_specs=[pl.BlockSpec((n, tr, tc), lambda i, j: (0, i, j))],
        out_specs=pl.BlockSpec((tr, tc), lambda i, j: (i, j)),
        out_shape=jax.ShapeDtypeStruct((R, C), out_dtype),
        compiler_params=_cparams(("parallel", "parallel")),
    )(parts)


def _where_am_i():
    x, y, c = lax.axis_index("x"), lax.axis_index("y"), lax.axis_index("c")
    chips = [(1 - x, y), (x, 1 - y), (1 - x, 1 - y)]
    return x, y, c, chips


def _half_idx(rows, cols, by_cols, which):
    if by_cols:
        hc = cols // 2
        return (slice(None), pl.ds(pl.multiple_of(which * hc, LANE), hc))
    hr = rows // 2
    return (pl.ds(pl.multiple_of(which * hr, 16), hr), slice(None))


def _half_shape(rows, cols, by_cols):
    return (rows, cols // 2) if by_cols else (rows // 2, cols)


HBM_SPEC = pl.BlockSpec(memory_space=pltpu.HBM)
SEM_SPEC = pl.BlockSpec(memory_space=pltpu.SEMAPHORE)
DATAFLOW = pltpu.SideEffectType.DATAFLOW_SIDE_EFFECTING


def _chip_exchange_refs(kind, shards_shape, by_cols, src, land, i, chip_k, c, me):
    if kind == 'gather':
        half = _half_idx(*shards_shape, by_cols, c)
        return src.at[half], land.at[(me,) + half], land.at[(chip_k,) + half]
    return src.at[chip_k], land.at[me], land.at[chip_k]


def _chip_exchange_start(kind, srcs, by_cols, *, name, after=()):
    n = len(srcs)
    land_shapes = [((N_CHIPS,) + s.shape) if kind == 'gather' else s.shape for s in srcs]

    def body(*refs):
        src_refs, land_refs = refs[:n], refs[n:2 * n]
        send, recv = refs[2 * n + len(after)], refs[2 * n + len(after) + 1]
        token = refs[-1]
        x, y, c, chips = _where_am_i()
        me = 2 * x + y
        for i in range(n):
            for k, (px, py) in enumerate(chips):
                s, d, _ = _chip_exchange_refs(kind, srcs[i].shape, by_cols[i], src_refs[i], land_refs[i], i,
                                              2 * px + py, c, me)
                pltpu.make_async_remote_copy(src_ref=s, dst_ref=d, send_sem=send.at[3 * i + k],
                                             recv_sem=recv.at[3 * i + k], device_id=(px, py, c),
                                             device_id_type=MESH).start()
        token[...] = jnp.zeros_like(token)

    lands = [pltpu.with_memory_space_constraint(lax.empty(sh, s.dtype), pltpu.HBM) for sh, s in zip(land_shapes, srcs)]
    outs = _call(
        body, name=name,
        out_shape=(pltpu.SemaphoreType.DMA((3 * n,)), pltpu.SemaphoreType.DMA((3 * n,)),
                   *[pltpu.HBM(s.shape, s.dtype) for s in srcs],
                   *[pltpu.HBM(sh, s.dtype) for sh, s in zip(land_shapes, srcs)],
                   jax.ShapeDtypeStruct((8, LANE), F32)),
        in_specs=[HBM_SPEC] * (2 * n) + [ANY] * len(after),
        out_specs=(SEM_SPEC, SEM_SPEC, *([HBM_SPEC] * (2 * n)), pl.BlockSpec(memory_space=pltpu.VMEM)),
        input_output_aliases={j: 2 + j for j in range(2 * n)},
        compiler_params=pltpu.CompilerParams(has_side_effects=DATAFLOW),
    )(*[pltpu.with_memory_space_constraint(s, pltpu.HBM) for s in srcs], *lands, *after)
    return outs[0], outs[1], list(outs[2:2 + n]), list(outs[2 + n:2 + 2 * n]), outs[-1]


def _chip_exchange_wait(kind, send, recv, srcs, lands, by_cols, after, *, name):
    n = len(srcs)

    def body(*refs):
        src_refs, land_refs = refs[:n], refs[n:2 * n]
        send_r, recv_r = refs[2 * n], refs[2 * n + 1]
        x, y, c, chips = _where_am_i()
        me = 2 * x + y
        for i in range(n):
            for k, (px, py) in enumerate(chips):
                s, _, d = _chip_exchange_refs(kind, srcs[i].shape, by_cols[i], src_refs[i], land_refs[i], i,
                                              2 * px + py, c, me)
                cp = pltpu.make_async_remote_copy(src_ref=s, dst_ref=d, send_sem=send_r.at[3 * i + k],
                                                  recv_sem=recv_r.at[3 * i + k], device_id=(px, py, c),
                                                  device_id_type=MESH)
                cp.wait_send()
                cp.wait_recv()

    outs = _call(
        body, name=name,
        out_shape=(*[pltpu.HBM(s.shape, s.dtype) for s in srcs], *[pltpu.HBM(l.shape, l.dtype) for l in lands]),
        in_specs=[HBM_SPEC] * (2 * n) + [SEM_SPEC, SEM_SPEC] + [ANY] * len(after),
        out_specs=tuple([HBM_SPEC] * (2 * n)),
        input_output_aliases={j: j for j in range(2 * n)},
        compiler_params=pltpu.CompilerParams(has_side_effects=DATAFLOW),
    )(*srcs, *lands, send, recv, *after)
    return list(outs[:n]), list(outs[n:])


def _sibling_halves_start(grads, by_cols, *, name, after=()):
    n = len(grads)
    land_shapes = [(N_CHIPS,) + _half_shape(*g.shape[1:], bc) for g, bc in zip(grads, by_cols)]

    def body(*refs):
        src_refs, land_refs = refs[:n], refs[n:2 * n]
        send, recv = refs[2 * n + len(after)], refs[2 * n + len(after) + 1]
        token = refs[-1]
        x, y, c, _ = _where_am_i()
        for i in range(n):
            src = src_refs[i].at[(slice(None),) + _half_idx(*grads[i].shape[1:], by_cols[i], 1 - c)]
            pltpu.make_async_remote_copy(src_ref=src, dst_ref=land_refs[i], send_sem=send.at[i], recv_sem=recv.at[i],
                                         device_id=(x, y, 1 - c), device_id_type=MESH).start()
        token[...] = jnp.zeros_like(token)

    lands = [pltpu.with_memory_space_constraint(lax.empty(sh, g.dtype), pltpu.HBM) for sh, g in zip(land_shapes, grads)]
    outs = _call(
        body, name=name,
        out_shape=(pltpu.SemaphoreType.DMA((n,)), pltpu.SemaphoreType.DMA((n,)),
                   *[pltpu.HBM(g.shape, g.dtype) for g in grads],
                   *[pltpu.HBM(sh, g.dtype) for sh, g in zip(land_shapes, grads)],
                   jax.ShapeDtypeStruct((8, LANE), F32)),
        in_specs=[HBM_SPEC] * (2 * n) + [ANY] * len(after),
        out_specs=(SEM_SPEC, SEM_SPEC, *([HBM_SPEC] * (2 * n)), pl.BlockSpec(memory_space=pltpu.VMEM)),
        input_output_aliases={j: 2 + j for j in range(2 * n)},
        compiler_params=pltpu.CompilerParams(has_side_effects=DATAFLOW),
    )(*[pltpu.with_memory_space_constraint(g, pltpu.HBM) for g in grads], *lands, *after)
    return outs[0], outs[1], list(outs[2:2 + n]), list(outs[2 + n:2 + 2 * n]), outs[-1]


def _sibling_halves_wait(send, recv, grads, lands, by_cols, after, *, name):
    n = len(grads)

    def body(*refs):
        src_refs, land_refs = refs[:n], refs[n:2 * n]
        send_r, recv_r = refs[2 * n], refs[2 * n + 1]
        x, y, c, _ = _where_am_i()
        for i in range(n):
            src = src_refs[i].at[(slice(None),) + _half_idx(*grads[i].shape[1:], by_cols[i], 1 - c)]
            cp = pltpu.make_async_remote_copy(src_ref=src, dst_ref=land_refs[i], send_sem=send_r.at[i],
                                              recv_sem=recv_r.at[i], device_id=(x, y, 1 - c), device_id_type=MESH)
            cp.wait_send()
            cp.wait_recv()

    outs = _call(
        body, name=name,
        out_shape=(*[pltpu.HBM(g.shape, g.dtype) for g in grads], *[pltpu.HBM(l.shape, l.dtype) for l in lands]),
        in_specs=[HBM_SPEC] * (2 * n) + [SEM_SPEC, SEM_SPEC] + [ANY] * len(after),
        out_specs=tuple([HBM_SPEC] * (2 * n)),
        input_output_aliases={j: j for j in range(2 * n)},
        compiler_params=pltpu.CompilerParams(has_side_effects=DATAFLOW),
    )(*grads, *lands, send, recv, *after)
    return list(outs[:n]), list(outs[n:])


def _sibling_swap_start(arrs, *, name, after=()):
    n = len(arrs)

    def body(*refs):
        src_refs, land_refs = refs[:n], refs[n:2 * n]
        send, recv = refs[2 * n + len(after)], refs[2 * n + len(after) + 1]
        token = refs[-1]
        x, y, c, _ = _where_am_i()
        for i in range(n):
            pltpu.make_async_remote_copy(src_ref=src_refs[i], dst_ref=land_refs[i], send_sem=send.at[i],
                                         recv_sem=recv.at[i], device_id=(x, y, 1 - c), device_id_type=MESH).start()
        token[...] = jnp.zeros_like(token)

    lands = [pltpu.with_memory_space_constraint(lax.empty(a.shape, a.dtype), pltpu.HBM) for a in arrs]
    outs = _call(
        body, name=name,
        out_shape=(pltpu.SemaphoreType.DMA((n,)), pltpu.SemaphoreType.DMA((n,)),
                   *[pltpu.HBM(a.shape, a.dtype) for a in arrs] * 2, jax.ShapeDtypeStruct((8, LANE), F32)),
        in_specs=[HBM_SPEC] * (2 * n) + [ANY] * len(after),
        out_specs=(SEM_SPEC, SEM_SPEC, *([HBM_SPEC] * (2 * n)), pl.BlockSpec(memory_space=pltpu.VMEM)),
        input_output_aliases={j: 2 + j for j in range(2 * n)},
        compiler_params=pltpu.CompilerParams(has_side_effects=DATAFLOW),
    )(*[pltpu.with_memory_space_constraint(a, pltpu.HBM) for a in arrs], *lands, *after)
    return outs[0], outs[1], list(outs[2:2 + n]), list(outs[2 + n:2 + 2 * n]), outs[-1]


def _sibling_swap_wait(send, recv, arrs, lands, after, *, name):
    n = len(arrs)

    def body(*refs):
        src_refs, land_refs = refs[:n], refs[n:2 * n]
        send_r, recv_r = refs[2 * n], refs[2 * n + 1]
        x, y, c, _ = _where_am_i()
        for i in range(n):
            cp = pltpu.make_async_remote_copy(src_ref=src_refs[i], dst_ref=land_refs[i], send_sem=send_r.at[i],
                                              recv_sem=recv_r.at[i], device_id=(x, y, 1 - c), device_id_type=MESH)
            cp.wait_send()
            cp.wait_recv()

    outs = _call(
        body, name=name,
        out_shape=tuple(pltpu.HBM(a.shape, a.dtype) for a in list(arrs) + list(lands)),
        in_specs=[HBM_SPEC] * (2 * n) + [SEM_SPEC, SEM_SPEC] + [ANY] * len(after),
        out_specs=tuple([HBM_SPEC] * (2 * n)),
        input_output_aliases={j: j for j in range(2 * n)},
        compiler_params=pltpu.CompilerParams(has_side_effects=DATAFLOW),
    )(*arrs, *lands, send, recv, *after)
    return list(outs[:n]), list(outs[n:])


def _sibling_pass_gathered(lands, shard_shapes, by_cols, *, name):
    n = len(lands)

    def body(*refs):
        outs = refs[n:2 * n]
        send, recv = refs[2 * n:]
        x, y, c, chips = _where_am_i()
        sibling = (x, y, 1 - c)
        cps = []
        for i in range(n):
            for k, (px, py) in enumerate(chips):
                blk = outs[i].at[(2 * px + py,) + _half_idx(*shard_shapes[i], by_cols[i], c)]
                d = pltpu.make_async_remote_copy(src_ref=blk, dst_ref=blk, send_sem=send.at[i, k],
                                                 recv_sem=recv.at[i, k], device_id=sibling, device_id_type=MESH)
                d.start()
                cps.append(d)
        for i in range(n):
            for k, (px, py) in enumerate(chips):
                blk = outs[i].at[(2 * px + py,) + _half_idx(*shard_shapes[i], by_cols[i], 1 - c)]
                pltpu.make_async_remote_copy(src_ref=blk, dst_ref=blk, send_sem=send.at[i, k], recv_sem=recv.at[i, k],
                                             device_id=sibling, device_id_type=MESH).wait_recv()
        for d in cps:
            d.wait_send()

    return _call(
        body, name=name, in_specs=[ANY] * n, out_specs=[ANY] * n,
        out_shape=[jax.ShapeDtypeStruct(l.shape, l.dtype) for l in lands],
        input_output_aliases={j: j for j in range(n)},
        scratch_shapes=[pltpu.SemaphoreType.DMA((n, 3)), pltpu.SemaphoreType.DMA((n, 3))],
    )(*lands)


def _own_slot(lands, owns):
    me = 2 * lax.axis_index("x") + lax.axis_index("y")
    return [lax.dynamic_update_slice_in_dim(g, s, me, axis=0) for g, s in zip(lands, owns)]


def _sibling_send_halves(grads, by_cols, *, name):
    n = len(grads)

    def body(*refs):
        ins, outs = refs[:n], refs[n:2 * n]
        send, recv = refs[2 * n:]
        x, y, c, _ = _where_am_i()
        sibling = (x, y, 1 - c)
        cps = []
        for i in range(n):
            src = ins[i].at[(slice(None),) + _half_idx(*grads[i].shape[1:], by_cols[i], 1 - c)]
            d = pltpu.make_async_remote_copy(src_ref=src, dst_ref=outs[i], send_sem=send.at[i],
                                             recv_sem=recv.at[i], device_id=sibling, device_id_type=MESH)
            d.start()
            cps.append(d)
        for d in cps:
            d.wait()

    return _call(
        body, name=name, in_specs=[ANY] * n, out_specs=[ANY] * n,
        out_shape=[jax.ShapeDtypeStruct((N_CHIPS,) + _half_shape(*g.shape[1:], bc), g.dtype)
                   for g, bc in zip(grads, by_cols)],
        scratch_shapes=[pltpu.SemaphoreType.DMA((n,)), pltpu.SemaphoreType.DMA((n,))],
    )(*grads)


def _all_reduce_small(v, *, name, after=()):
    R, C = v.shape
    H = R // 2

    def body(v_ref, o_ref, sib, slots, send, recv):
        x, y, c, chips = _where_am_i()
        me = 2 * x + y
        sibling = (x, y, 1 - c)
        mine = pl.ds(pl.multiple_of(c * H, 8), H)
        other = pl.ds(pl.multiple_of((1 - c) * H, 8), H)

        def copy(k, src, dst, to):
            return pltpu.make_async_remote_copy(src_ref=src, dst_ref=dst, send_sem=send.at[k], recv_sem=recv.at[k],
                                                device_id=to, device_id_type=MESH)

        d = copy(0, v_ref.at[other], sib, sibling)
        d.start()
        d.wait()
        slots[me] = v_ref[mine, :] + sib[...]
        cps = [copy(1 + k, slots.at[me], slots.at[me], (px, py, c)) for k, (px, py) in enumerate(chips)]
        for d in cps:
            d.start()
        for k, (px, py) in enumerate(chips):
            blk = slots.at[2 * px + py]
            copy(1 + k, blk, blk, (px, py, c)).wait_recv()
        for d in cps:
            d.wait_send()
        o_ref[mine, :] = (slots[0] + slots[1]) + (slots[2] + slots[3])
        d = copy(4, o_ref.at[mine], o_ref.at[mine], sibling)
        d.start()
        copy(4, o_ref.at[other], o_ref.at[other], sibling).wait_recv()
        d.wait_send()

    vm = pl.BlockSpec(memory_space=pltpu.VMEM)
    return _call(
        body, after=after, name=name, in_specs=[vm], out_specs=vm,
        out_shape=jax.ShapeDtypeStruct((R, C), F32),
        scratch_shapes=[pltpu.VMEM((H, C), F32), pltpu.VMEM((N_CHIPS, H, C), F32),
                        pltpu.SemaphoreType.DMA((5,)), pltpu.SemaphoreType.DMA((5,))],
        compiler_params=pltpu.CompilerParams(vmem_limit_bytes=VMEM_LIMIT),
    )(v)


def _cols_from_shards(g):
    return jnp.transpose(g, (1, 0, 2)).reshape(g.shape[1], -1)


def _shards_from_cols(w):
    R, C4 = w.shape
    return jnp.transpose(w.reshape(R, N_CHIPS, C4 // N_CHIPS), (1, 0, 2))


def _pack(arrs):
    flat = []
    for a in arrs:
        f = a.reshape(-1).astype(F32)
        flat.append(jnp.pad(f, (0, _rup(f.shape[0], LANE) - f.shape[0])))
    v = jnp.concatenate(flat)
    rows = _rup(v.shape[0] // LANE, 16)
    v = jnp.pad(v, (0, rows * LANE - v.shape[0]))
    return v.reshape(rows, LANE)


def _unpack(v, shapes):
    flat = v.reshape(-1)
    out, off = [], 0
    for s in shapes:
        n = int(np.prod(s))
        out.append(flat[off:off + n].reshape(s))
        off += _rup(n, LANE)
    return out


def _ffn_fwd(x, Wup, Wdn, cw, cb, tag):
    h = _mm(x, Wup, 'nt', bmode='bo', tm=512, tn=4096, name=f"ffn_up_{tag}")
    a, hc = _act_fwd(h, cw, cb, name=f"ffn_act_{tag}")
    f = _mm(a, Wdn, 'nn', bmode='abr', tm=512, tn=1024, tk=4096, name=f"ffn_down_{tag}")
    return f, (h, hc), a


def _ffn_bwd(df, x, saved, a, Wup, Wdn, cw, tag):
    h, hc = saved
    da = _mm(df, Wdn, 'nt', bmode='bo', tm=512, tn=4096, name=f"ffn_da_{tag}")
    dWdn = _mm(a, df, 'tn', bmode='ao', tm=4096, tn=512, name=f"ffn_dwdn_{tag}", out_dtype=BF16)
    dh, dcw, dcb = _act_bwd(h, hc, da, cw, name=f"ffn_actb_{tag}")

    def shard_of(k):
        return (k % 2) * 2 + k // 2

    dx = _mm(dh, Wup, 'nn', bmode='abr', tm=512, tn=1024, tk=4096, name=f"ffn_dx_{tag}", b_map=shard_of)
    dWup = _mm(dh, x, 'tn', bmode='ao', tm=4096, tn=512, name=f"ffn_dwup_{tag}", out_dtype=BF16,
               o_map=shard_of)
    return dx, dWup, dWdn, dcw, dcb


def kernel(x, positions, ev_w_in, ev_b_f, ev_lambda_re, ev_lambda_im, ev_log_step, ev_ssm_b_re, ev_ssm_b_im, ev_ssm_c_re, ev_ssm_c_im, ev_ssm_d, ev_w_glu, ev_w_out, od_w_in, od_sinks, od_w_out, ln_mix_g, ln_mix_b, ffn_w_up, ffn_conv_w, ffn_conv_b, ffn_w_down, ln_ffn_g, ln_ffn_b, loss_target, m_ev_w_in, m_ev_b_f, m_ev_lambda_re, m_ev_lambda_im, m_ev_log_step, m_ev_ssm_b_re, m_ev_ssm_b_im, m_ev_ssm_c_re, m_ev_ssm_c_im, m_ev_ssm_d, m_ev_w_glu, m_ev_w_out, m_od_w_in, m_od_sinks, m_od_w_out, m_ln_mix_g, m_ln_mix_b, m_ffn_w_up, m_ffn_conv_w, m_ffn_conv_b, m_ffn_w_down, m_ln_ffn_g, m_ln_ffn_b, v_ev_w_in, v_ev_b_f, v_ev_lambda_re, v_ev_lambda_im, v_ev_log_step, v_ev_ssm_b_re, v_ev_ssm_b_im, v_ev_ssm_c_re, v_ev_ssm_c_im, v_ev_ssm_d, v_ev_w_glu, v_ev_w_out, v_od_w_in, v_od_sinks, v_od_w_out, v_ln_mix_g, v_ln_mix_b, v_ffn_w_up, v_ffn_conv_w, v_ffn_conv_b, v_ffn_w_down, v_ln_ffn_g, v_ln_ffn_b):
    W = dict(ev_w_in=ev_w_in, ev_b_f=ev_b_f, ev_lambda_re=ev_lambda_re, ev_lambda_im=ev_lambda_im, ev_log_step=ev_log_step, ev_ssm_b_re=ev_ssm_b_re, ev_ssm_b_im=ev_ssm_b_im, ev_ssm_c_re=ev_ssm_c_re, ev_ssm_c_im=ev_ssm_c_im, ev_ssm_d=ev_ssm_d, ev_w_glu=ev_w_glu, ev_w_out=ev_w_out, od_w_in=od_w_in, od_sinks=od_sinks, od_w_out=od_w_out, ln_mix_g=ln_mix_g, ln_mix_b=ln_mix_b, ffn_w_up=ffn_w_up, ffn_conv_w=ffn_conv_w, ffn_conv_b=ffn_conv_b, ffn_w_down=ffn_w_down, ln_ffn_g=ln_ffn_g, ln_ffn_b=ln_ffn_b)
    Mo = dict(ev_w_in=m_ev_w_in, ev_b_f=m_ev_b_f, ev_lambda_re=m_ev_lambda_re, ev_lambda_im=m_ev_lambda_im, ev_log_step=m_ev_log_step, ev_ssm_b_re=m_ev_ssm_b_re, ev_ssm_b_im=m_ev_ssm_b_im, ev_ssm_c_re=m_ev_ssm_c_re, ev_ssm_c_im=m_ev_ssm_c_im, ev_ssm_d=m_ev_ssm_d, ev_w_glu=m_ev_w_glu, ev_w_out=m_ev_w_out, od_w_in=m_od_w_in, od_sinks=m_od_sinks, od_w_out=m_od_w_out, ln_mix_g=m_ln_mix_g, ln_mix_b=m_ln_mix_b, ffn_w_up=m_ffn_w_up, ffn_conv_w=m_ffn_conv_w, ffn_conv_b=m_ffn_conv_b, ffn_w_down=m_ffn_w_down, ln_ffn_g=m_ln_ffn_g, ln_ffn_b=m_ln_ffn_b)
    Vo = dict(ev_w_in=v_ev_w_in, ev_b_f=v_ev_b_f, ev_lambda_re=v_ev_lambda_re, ev_lambda_im=v_ev_lambda_im, ev_log_step=v_ev_log_step, ev_ssm_b_re=v_ev_ssm_b_re, ev_ssm_b_im=v_ev_ssm_b_im, ev_ssm_c_re=v_ev_ssm_c_re, ev_ssm_c_im=v_ev_ssm_c_im, ev_ssm_d=v_ev_ssm_d, ev_w_glu=v_ev_w_glu, ev_w_out=v_ev_w_out, od_w_in=v_od_w_in, od_sinks=v_od_sinks, od_w_out=v_od_w_out, ln_mix_g=v_ln_mix_g, ln_mix_b=v_ln_mix_b, ffn_w_up=v_ffn_w_up, ffn_conv_w=v_ffn_conv_w, ffn_conv_b=v_ffn_conv_b, ffn_w_down=v_ffn_w_down, ln_ffn_g=v_ln_ffn_g, ln_ffn_b=v_ln_ffn_b)
    names = list(W.keys())
    big = ['ev_w_in', 'ev_w_glu', 'ev_w_out', 'od_w_in', 'od_w_out', 'ffn_w_up', 'ffn_w_down']

    S, D = x.shape[1], x.shape[2]
    x0 = x.reshape(S, D)
    tgt = loss_target.reshape(S, D)
    G, Pn, Cg = SSM_GROUPS, SSM_STATE, SSM_GROUP
    Fs = ffn_w_up.shape[2]
    FP = Fs
    Rd = ffn_w_down.shape[1]
    EIN = N_CHIPS * ev_w_in.shape[2]

    cwl = ffn_conv_w.reshape(-1)
    cw_rows = _rup(_rup(cwl.shape[0], LANE) // LANE, 32)
    cw_pad = jnp.pad(cwl, (0, cw_rows * LANE - cwl.shape[0])).reshape(cw_rows, LANE)
    transposed = ('ev_w_in', 'ffn_w_up')

    def view(n, a):
        return jnp.transpose(a, (0, 2, 1)) if n in transposed else a

    Wv = {n: view(n, W[n]) for n in big}
    big_e = [(n, l) for n in big for l in range(W[n].shape[0])]
    split_cols = {e: (Wv[e[0]].shape[1] // 2) % 16 != 0 for e in big_e}
    shard16 = {e: Wv[e[0]][e[1]].astype(BF16) for e in big_e}
    grp_now = [e for e in big_e if e[0].startswith('ev_')]
    grp_ffn0 = [('ffn_w_up', 0), ('ffn_w_down', 0)]
    grp_l1 = [('od_w_in', 0), ('od_w_out', 0), ('ffn_w_up', 1), ('ffn_w_down', 1)]
    src_now = [shard16[e] for e in grp_now]
    src_ffn0 = [shard16[e] for e in grp_ffn0] + [cw_pad]
    src_l1 = [shard16[e] for e in grp_l1]
    cols_now = [split_cols[e] for e in grp_now]
    cols_ffn0 = [split_cols[e] for e in grp_ffn0] + [False]
    cols_l1 = [split_cols[e] for e in grp_l1]
    ag_in = _chip_exchange_start('gather', src_now[:1], cols_now[:1], name="ag_in_start")
    ag_mix = _chip_exchange_start('gather', src_now[1:], cols_now[1:], name="ag_mix_start", after=[ag_in[4]])
    ag_ffn0 = _chip_exchange_start('gather', src_ffn0, cols_ffn0, name="ag_ffn0_start", after=[ag_mix[4]])
    ag_l1 = _chip_exchange_start('gather', src_l1, cols_l1, name="ag_l1_start", after=[ag_ffn0[4]])
    started = [ag_l1[4]]

    def finish_gather(started, srcs, cols, after, tag):
        send, recv, thru, lands, _ = started
        thru, lands = _chip_exchange_wait('gather', send, recv, thru, lands, cols, after, name=f"ag_{tag}_wait")
        lands = _sibling_pass_gathered(lands, [s.shape for s in srcs], cols, name=f"ag_{tag}_pass")
        return _own_slot(lands, [s[None] for s in thru])

    lam_r, lam_i = ev_lambda_re[0], ev_lambda_im[0]
    lstep = ev_log_step[0].reshape(G, 1)
    a_re, a_im, g_re, g_im = _s5_disc_fwd(lam_r, lam_i, lstep, name="s5_disc", after=started)
    b_re2, b_im2 = ev_ssm_b_re[0].reshape(G * Pn, Cg), ev_ssm_b_im[0].reshape(G * Pn, Cg)
    g_re1, g_im1 = g_re.reshape(G * Pn, 1), g_im.reshape(G * Pn, 1)
    bb_re, bb_im = _s5_bb_fwd(g_re1, g_im1, b_re2, b_im2, name="s5_bb")
    bbt = jnp.stack([jnp.transpose(b.reshape(G, Pn, Cg), (0, 2, 1)).reshape(G * Cg, Pn) for b in (bb_re, bb_im)])
    BB = _diag_expand(bbt, Cg, Pn, name="s5_bb_dense")
    cct = jnp.stack([jnp.transpose(ev_ssm_c_re[0], (0, 2, 1)).reshape(G * Pn, Cg),
                     jnp.transpose(-ev_ssm_c_im[0], (0, 2, 1)).reshape(G * Pn, Cg)])
    CC = _diag_expand(cct, Pn, Cg, name="s5_cc_dense", after=started)
    a_cat = jnp.stack([a_re.reshape(1, G * Pn), a_im.reshape(1, G * Pn)])
    dskip = ev_ssm_d[0].reshape(1, SSM_WIDTH)
    tabs = _rope_tables(positions.reshape(S, 1).astype(F32), name="rope_tables", after=[BB, CC])

    gw = dict(zip(grp_now[:1], finish_gather(ag_in, src_now[:1], cols_now[:1], [tabs[2]], "in")))
    w_in_t = gw[('ev_w_in', 0)].reshape(EIN, D)
    qkv_w = 3 * FOX_WIDTH
    WmainT = jnp.concatenate([w_in_t[:qkv_w], w_in_t[qkv_w + FOX_HEADS:]], axis=0)
    WfT = jnp.pad(w_in_t[qkv_w:qkv_w + FOX_HEADS], ((0, LANE - FOX_HEADS), (0, 0)))
    cbs = [ffn_conv_b[l].reshape(N_CHIPS, Fs) for l in range(DEPTH)]

    P = _mm(x0, WmainT, 'nt', name="ev_proj")
    fl = _mm(x0, WfT, 'nt', name="ev_proj_f")
    bf_pad = jnp.pad(ev_b_f.reshape(1, FOX_HEADS), ((0, 0), (0, LANE - FOX_HEADS)))
    cgate, sgate = _gate_fwd(fl, bf_pad, name="fox_gate")
    ccol = jnp.transpose(cgate[:, :FOX_HEADS]).reshape(FOX_HEADS, S, 1)
    crow = jnp.transpose(cgate[:, :FOX_HEADS]).reshape(FOX_HEADS, 1, S)
    fox, lse = _fox_fwd(P, ccol, crow, name="fox_fwd")
    u_s5 = P[:, qkv_w:]
    UT, HT = _DIAG_TILE * Cg, _DIAG_TILE * Pn
    bu = _mm(u_s5, BB, 'nn', bmode='bo', tm=2048, tn=HT, tk=UT, diag='kn', name="s5_bu")
    hh = _s5_scan_fwd(bu, a_cat, name="s5_scan")
    yc = _mm(hh, CC, 'nn', bmode='abr', tm=2048, tn=UT, tk=HT, diag='kn', name="s5_y")
    y_s5, yg = _s5_out_fwd(yc, P, dskip, name="s5_out")
    gw.update(zip(grp_now[1:], finish_gather(ag_mix, src_now[1:], cols_now[1:], [yg], "mix")))
    Wglu = _cols_from_shards(gw[('ev_w_glu', 0)])
    Wout_ev = gw[('ev_w_out', 0)].reshape(D, D)
    z = _mm(yg, Wglu, 'nn', name="s5_glu_proj")
    ssm = _glu_fwd(z, name="s5_glu")
    cat = jnp.concatenate([fox.astype(BF16), ssm], axis=1)
    mix0 = _mm(cat, Wout_ev, 'nn', name="ev_out")
    x1, xh1, rs1 = _add_ln_fwd(x0, mix0, ln_mix_g[0], ln_mix_b[0], name="ln_mix0")
    got = finish_gather(ag_ffn0, src_ffn0, cols_ffn0, [x1], "ffn0")
    gw.update(zip(grp_ffn0, got[:-1]))
    cw_all = got[-1].reshape(N_CHIPS, -1)[:, :cwl.shape[0]].reshape(N_CHIPS, DEPTH, 3, Fs)
    cws = [cw_all[:, l] for l in range(DEPTH)]
    Wup = {0: gw[('ffn_w_up', 0)]}
    Wdn = {0: gw[('ffn_w_down', 0)].reshape(2, Fs, D)}
    f0, hf0, af0 = _ffn_fwd(x1, Wup[0], Wdn[0], cws[0], cbs[0], "l0")
    x2, xh2, rs2 = _add_ln_fwd(x1, f0, ln_ffn_g[0], ln_ffn_b[0], name="ln_ffn0")

    gw.update(zip(grp_l1, finish_gather(ag_l1, src_l1, cols_l1, [x2], "l1")))
    Wodin = _cols_from_shards(gw[('od_w_in', 0)])
    Wodout = gw[('od_w_out', 0)].reshape(D, D)
    Wup[1] = gw[('ffn_w_up', 1)]
    Wdn[1] = gw[('ffn_w_down', 1)].reshape(2, Fs, D)
    QW, KW = SWA_HEADS * SWA_HEAD_DIM, SWA_KV_HEADS * SWA_HEAD_DIM
    P1 = _mm(x2, Wodin, 'nn', name="od_proj")
    qT = _to_heads(P1, tabs, col0=0, width=QW, rotate=True, name="rope_q", out_dtype=BF16)
    kT = _to_heads(P1, tabs, col0=QW, width=KW, rotate=True, name="rope_k", out_dtype=BF16)
    vT = _to_heads(P1, tabs, col0=QW + KW, width=KW, rotate=False, name="heads_v", out_dtype=BF16)
    sink_rows = jnp.broadcast_to(od_sinks[0].reshape(SWA_KV_HEADS, SWA_GROUPS, 1, 1),
                                 (SWA_KV_HEADS, SWA_GROUPS, SWA_WINDOW, 1)).reshape(SWA_KV_HEADS, -1, 1)
    oT, Lsw = _swa_fwd(qT, kT, vT, sink_rows, name="swa_fwd")
    o_sw = _from_heads(oT, tabs, rotate_back=False, name="heads_o", out_dtype=BF16)
    mix1 = _mm(o_sw, Wodout, 'nn', name="od_out")
    x3, xh3, rs3 = _add_ln_fwd(x2, mix1, ln_mix_g[1], ln_mix_b[1], name="ln_mix1")
    f1, hf1, af1 = _ffn_fwd(x3, Wup[1], Wdn[1], cws[1], cbs[1], "l1")
    x4, xh4, rs4 = _add_ln_fwd(x3, f1, ln_ffn_g[1], ln_ffn_b[1], name="ln_ffn1")
    dy, loss_part = _loss_grad(x4, tgt, name="loss")

    dz4, dg_ffn1, db_ffn1 = _ln_bwd(dy, None, xh4, rs4, ln_ffn_g[1], name="lnb_ffn1")
    dx3f, dWup1, dWdn1, dcw1, dcb1 = _ffn_bwd(dz4, x3, hf1, af1, Wup[1], Wdn[1], cws[1], "l1")
    sib_ffn1 = _sibling_halves_start([dWup1, dWdn1.reshape(N_CHIPS, Rd, D)], [False, False], name="rs_ffn1_sib_start")
    dz3, dg_mix1, db_mix1 = _ln_bwd(dz4, dx3f, xh3, rs3, ln_mix_g[1], name="lnb_mix1", after=[sib_ffn1[4]])
    do_sw = _mm(dz3, Wodout, 'nt', name="od_out_dx")
    dWodout = _mm(o_sw, dz3, 'tn', name="od_out_dw", out_dtype=BF16)
    doT = _to_heads(do_sw, tabs, col0=0, width=QW, rotate=False, name="heads_do", out_dtype=F32)
    dqT, dkT, dvT, dsink = _swa_bwd(qT, kT, vT, sink_rows, oT, Lsw, doT, name="swa_bwd")
    dq1 = _from_heads(dqT, tabs, rotate_back=True, name="rope_dq", out_dtype=BF16)
    dk1 = _from_heads(dkT, tabs, rotate_back=True, name="rope_dk", out_dtype=BF16, skip_rows=SWA_WINDOW)
    dv1 = _from_heads(dvT, tabs, rotate_back=False, name="heads_dv", out_dtype=BF16, skip_rows=SWA_WINDOW)
    dP1 = jnp.concatenate([dq1, dk1, dv1], axis=1)
    dx2m = _mm(dP1, Wodin, 'nt', name="od_proj_dx")
    dWodin = _mm(x2, dP1, 'tn', name="od_proj_dw", out_dtype=BF16)

    def rs_begin(entries, grads, tag):
        cols = [split_cols[e] for e in entries]
        sib = _sibling_send_halves(grads, cols, name=f"rs_{tag}_sibling")
        return [_sum2_halves(g4, s4, bc, name=f"rs_sum2_{n}{l}")
                for (n, l), g4, s4, bc in zip(entries, grads, sib, cols)]

    def rs_begin_started(entries, started, after, tag):
        send, rcv, thru, lands, _ = started
        thru, lands = _sibling_halves_wait(send, rcv, thru, lands, [False] * len(thru), after,
                                           name=f"rs_{tag}_sib_wait")
        return [_sum2_halves(g4, s4, False, name=f"rs_sum2_{n}{l}") for (n, l), g4, s4 in zip(entries, thru, lands)]

    def own_parts(parts):
        me = 2 * lax.axis_index("x") + lax.axis_index("y")
        return [lax.dynamic_slice_in_dim(p, me, 1, axis=0) for p in parts]

    part_l1 = (rs_begin(grp_l1[:2], [_shards_from_cols(dWodin), dWodout.reshape(N_CHIPS, D // N_CHIPS, D)], "od")
               + rs_begin_started(grp_l1[2:], sib_ffn1, [dWodin], "ffn1"))
    rs_l1 = _chip_exchange_start('scatter', part_l1, [False] * len(part_l1), name="rs_l1_start")

    dz2, dg_ffn0, db_ffn0 = _ln_bwd(dz3, dx2m, xh2, rs2, ln_ffn_g[0], name="lnb_ffn0", after=[rs_l1[4]])
    dx1f, dWup0, dWdn0, dcw0, dcb0 = _ffn_bwd(dz2, x1, hf0, af0, Wup[0], Wdn[0], cws[0], "l0")
    sib_ffn0 = _sibling_halves_start([dWup0, dWdn0.reshape(N_CHIPS, Rd, D)], [False, False], name="rs_ffn0_sib_start")
    dz1, dg_mix0, db_mix0 = _ln_bwd(dz2, dx1f, xh1, rs1, ln_mix_g[0], name="lnb_mix0", after=[sib_ffn0[4]])
    dcat = _mm(dz1, Wout_ev, 'nt', name="ev_out_dx")
    dWout_ev = _mm(cat, dz1, 'tn', name="ev_out_dw", out_dtype=BF16)
    part_ffn0 = rs_begin_started(grp_ffn0, sib_ffn0, [dWout_ev], "ffn0")
    rs_ffn0 = _chip_exchange_start('scatter', part_ffn0, [False] * len(part_ffn0), name="rs_ffn0_start")
    dz = _glu_bwd(z, dcat, name="s5_glu_bwd")
    dyg = _mm(dz, Wglu, 'nt', name="s5_glu_dx", after=[rs_ffn0[4]])
    dWglu = _mm(yg, dz, 'tn', name="s5_glu_dw", out_dtype=BF16)
    dy_s5, du_dir, dD = _s5_out_bwd(dyg, y_s5, P, dskip, name="s5_out_bwd")
    dhh = _mm(dy_s5, CC, 'nt', bmode='bo', tm=2048, tn=HT, tk=UT, diag='kn', name="s5_y_dx")
    dCC = _mm(hh, dy_s5, 'tn', bmode='ao', tm=HT, tn=UT, diag='mn', name="s5_y_dw")
    lam, da_s5 = _s5_scan_bwd(dhh, hh, a_cat, name="s5_scan_bwd")
    du_bu = _mm(lam, BB, 'nt', bmode='abr', tm=2048, tn=UT, tk=HT, diag='kn', name="s5_bu_dx")
    dBB = _mm(u_s5, lam, 'tn', bmode='bo', tm=UT, tn=HT, diag='mn', name="s5_bu_dw")
    du = _combine([du_dir, du_bu], [1.0, 1.0], name="s5_du", out_dtype=BF16)
    dq0, dk0, dv0, dccol, dcrow = _fox_bwd(P, ccol, crow, fox, lse, dcat, name="fox_bwd")
    dc = jnp.transpose((dccol.reshape(FOX_HEADS, S) - dcrow.reshape(FOX_HEADS, S)))
    dc = jnp.pad(dc, ((0, 0), (0, LANE - FOX_HEADS)))
    dfl, dbf = _gate_bwd(dc, sgate, name="fox_gate_bwd")
    dP = jnp.concatenate([dq0, dk0, dv0, du], axis=1)
    dx0a = _mm(dP, WmainT, 'nn', name="ev_proj_dx")
    dx0b = _mm(dfl, WfT, 'nn', name="ev_proj_f_dx")
    dWmainT = _mm(dP, x0, 'tn', tm=1024, tn=1024, name="ev_proj_dw", out_dtype=BF16)
    dWfT = _mm(dfl, x0, 'tn', name="ev_proj_f_dw", out_dtype=BF16)
    grad_x = _combine([dz1, dx0a, dx0b], [ALPHA, 1.0, 1.0], name="grad_x")

    dbbt = _diag_extract(dBB, Cg, Pn, name="s5_bb_diag")
    dcct = _diag_extract(dCC, Pn, Cg, name="s5_cc_diag")
    dbb_re = jnp.transpose(dbbt[0].reshape(G, Cg, Pn), (0, 2, 1)).reshape(G * Pn, Cg)
    dbb_im = jnp.transpose(dbbt[1].reshape(G, Cg, Pn), (0, 2, 1)).reshape(G * Pn, Cg)
    db_re, db_im, dg_re1, dg_im1 = _s5_bb_bwd(g_re1, g_im1, b_re2, b_im2, dbb_re, dbb_im, name="s5_bb_bwd")
    dlam_re, dlam_im, dlstep = _s5_disc_bwd(lam_r, lam_i, lstep, da_s5[0].reshape(G, Pn), da_s5[1].reshape(G, Pn),
                                            dg_re1.reshape(G, Pn), dg_im1.reshape(G, Pn), name="s5_disc_bwd")
    dc_re = jnp.transpose(dcct[0].reshape(G, Pn, Cg), (0, 2, 1))
    dc_im = -jnp.transpose(dcct[1].reshape(G, Pn, Cg), (0, 2, 1))

    def conv_w_full(d0, d1):
        return jnp.stack([jnp.reshape(jnp.transpose(d[:, :, :Fs], (1, 0, 2)), (3, N_CHIPS * Fs)) for d in (d0, d1)])

    def conv_b_full(d0, d1):
        return jnp.stack([jnp.reshape(d[:, 0, :Fs], (N_CHIPS * Fs,)) for d in (d0, d1)])

    small_local = dict(
        ev_b_f=dbf[:, :FOX_HEADS], ev_lambda_re=dlam_re, ev_lambda_im=dlam_im, ev_log_step=dlstep,
        ev_ssm_b_re=db_re, ev_ssm_b_im=db_im, ev_ssm_c_re=dc_re, ev_ssm_c_im=dc_im, ev_ssm_d=dD,
        od_sinks=dsink[:, :, 0],
        ln_mix_g=jnp.concatenate([dg_mix0, dg_mix1]), ln_mix_b=jnp.concatenate([db_mix0, db_mix1]),
        ffn_conv_w=conv_w_full(dcw0, dcw1), ffn_conv_b=conv_b_full(dcb0, dcb1),
        ln_ffn_g=jnp.concatenate([dg_ffn0, dg_ffn1]), ln_ffn_b=jnp.concatenate([db_ffn0, db_ffn1]))
    small = list(small_local.keys())
    out_g, out_d, out_m, out_v = {}, {}, {}, {}
    loss_out = []

    def small_update(after):
        red = _all_reduce_small(_pack([small_local[n] for n in small] + [loss_part]), name="ar_small", after=after)
        full_shapes = [W[n].shape if n != 'ffn_conv_w' else (DEPTH, 3, N_CHIPS * Fs) for n in small]
        pieces = _unpack(red, full_shapes + [()])
        loss_out.append(pieces[-1])
        gsmall = dict(zip(small, pieces[:-1]))
        chip = 2 * lax.axis_index("x") + lax.axis_index("y")
        gsmall['ffn_conv_w'] = lax.dynamic_slice_in_dim(gsmall['ffn_conv_w'], chip * Fs, Fs, axis=2)
        shapes = [W[n].shape for n in small]
        gs, ds_, ms, vs = _adamw(_pack([W[n] for n in small])[None], _pack([gsmall[n] for n in small])[None],
                                 _pack([Mo[n] for n in small])[None], _pack([Vo[n] for n in small])[None],
                                 name="adamw_small", tr=1 << 14)
        out_g.update(zip(small, _unpack(gs, shapes)))
        out_d.update(zip(small, _unpack(ds_, shapes)))
        out_m.update(zip(small, _unpack(ms, shapes)))
        out_v.update(zip(small, _unpack(vs, shapes)))
        return vs

    dw_in_t = jnp.concatenate([dWmainT[:qkv_w], dWfT[:FOX_HEADS], dWmainT[qkv_w:]], axis=0)
    part_now = rs_begin(grp_now, [dw_in_t.reshape(N_CHIPS, EIN // N_CHIPS, D), _shards_from_cols(dWglu),
                                  dWout_ev.reshape(N_CHIPS, D // N_CHIPS, D)], "l0")
    small_done = small_update([grad_x])
    rs_now = _chip_exchange_start('scatter', part_now, [False] * len(part_now), name="rs_l0_start",
                                  after=[small_done])

    def finish_scatter(started, parts, after, tag):
        send, rcv, thru, lands, _ = started
        thru, lands = _chip_exchange_wait('scatter', send, rcv, thru, lands, [False] * len(parts), after,
                                          name=f"rs_{tag}_wait")
        return _own_slot(lands, own_parts(thru))

    def update(entries, recv, tag):
        halves = [_rowsum(r, name=f"rs_sum4_{e[0]}{e[1]}") for e, r in zip(entries, recv)]
        send, rcv, thru, lands, tok = _sibling_swap_start(halves, name=f"rs_{tag}_join_start")
        own = dict(zip(entries, thru))
        params = list(dict.fromkeys(e[0] for e in entries))

        def half_update(n, grads, is_own, prev, after_name):
            return _adamw_half(Wv[n], [grads[(n, l)] for l in range(W[n].shape[0])], view(n, Mo[n]), view(n, Vo[n]),
                               name=f"adamw_{after_name}_{n}", own=is_own, prev=prev, by_cols=split_cols[(n, 0)])

        first = {n: half_update(n, own, True, None, "own") for n in params}
        _, others = _sibling_swap_wait(send, rcv, thru, lands, [first[n][3] for n in params] + [tok],
                                       name=f"rs_{tag}_join_wait")
        oth = dict(zip(entries, others))
        done = []
        for n in params:
            res = half_update(n, oth, False, first[n], "sib")
            out_g[n], out_d[n], out_m[n], out_v[n] = (view(n, t) for t in res)
            done.append(res[3])
        return done

    recv_rest = (finish_scatter(rs_l1, part_l1, [rs_now[4]], "l1")
                 + finish_scatter(rs_ffn0, part_ffn0, [rs_now[4]], "ffn0"))
    done = update(grp_l1 + grp_ffn0, recv_rest, "rest")
    update(grp_now, finish_scatter(rs_now, part_now, done, "l0"), "l0")
    loss = loss_out[0]

    return (loss, grad_x.reshape(1, S, D), *[out_g[n] for n in names], *[out_d[n] for n in names],
            *[out_m[n] for n in names], *[out_v[n] for n in names])
```

```python
import math

import numpy as np
import jax
import jax.numpy as jnp
from jax import lax
from jax.experimental import pallas as pl
from jax.experimental.pallas import tpu as pltpu

F32 = jnp.float32
BF16 = jnp.bfloat16
MESH = pl.DeviceIdType.MESH
ANY = pl.BlockSpec(memory_space=pl.ANY)

D_MODEL = 2048
FOX_HEADS = 8
FOX_HEAD_DIM = 128
FOX_WIDTH = 1024
SSM_WIDTH = 1024
SSM_GROUP = 16
SSM_GROUPS = 64
SSM_STATE = 64
SWA_HEADS = 32
SWA_KV_HEADS = 4
SWA_HEAD_DIM = 64
SWA_GROUPS = 8
SWA_WINDOW = 128
ROPE_DIM = 16
ROPE_THETA = 500000.0
LN_EPS = 1e-5
DEPTH = 2
ALPHA = (2.0 * DEPTH) ** 0.25
ADAM_LR = 0.001
ADAM_B1 = 0.9
ADAM_B2 = 0.999
ADAM_EPS = 1e-08
ADAM_WD = 0.01
ADAM_STEP = 10
N_CHIPS = 4

VMEM_LIMIT = 56 * 1024 * 1024
LANE = 128


def _call(body, after=(), **kw):
    if after:
        n = len(after)

        def shifted(*refs):
            return body(*refs[n:])

        call = _call(shifted, **dict(kw, in_specs=[ANY] * n + list(kw["in_specs"])))
        return lambda *args: call(*after, *args)
    return pl.pallas_call(body, **kw)


def _cparams(sem):
    return pltpu.CompilerParams(dimension_semantics=sem, vmem_limit_bytes=VMEM_LIMIT)


def _rup(n, m):
    return (n + m - 1) // m * m


def _pick(n, pref):
    if n <= pref:
        return n
    for step in (128, 16, 8):
        for t in range(pref - pref % step, 0, -step):
            if n % t == 0:
                return t
    return n


def _tile2d(rows, cols, pref_rows=256, budget=256 * 1024):
    tr = _pick(rows, pref_rows)
    if tr < 64:
        tr = rows
    if cols % LANE:
        return tr, cols
    return tr, _pick(cols, max(LANE, budget // tr // LANE * LANE))


def _mm(a, b, mode, *, name, tm=512, tn=1024, tk=2048, bmode=None, out_dtype=F32, after=(), b_map=None,
        o_map=None, diag=None):
    a3 = a if a.ndim == 3 else a[None]
    b3 = b if b.ndim == 3 else b[None]
    if mode == 'tn':
        K, M = a3.shape[1:]
    else:
        M, K = a3.shape[1:]
    N = b3.shape[1] if mode == 'nt' else b3.shape[2]
    tm, tn, tk = _pick(M, tm), _pick(N, tn), _pick(K, tk)
    nb = max(a3.shape[0], b3.shape[0])
    nbo, nbr = (1, nb) if bmode == 'abr' else (nb, 1)
    nm, nk = M // tm, K // tk
    if diag == 'kn':
        assert K // tk == N // tn
        nk = 1
    if diag == 'mn':
        assert M // tm == N // tn
        nm = 1
    nred = nbr * nk
    a_b = bmode in ('ao', 'abr')
    b_b = bmode in ('bo', 'abr')
    o_b = bmode in ('bo', 'ao')

    def bsel(flag, bo, br, remap=None):
        if not flag:
            return 0
        return (bo + br) if remap is None else remap(bo + br)

    def mi(i, j):
        return j if diag == 'mn' else i

    def ki(j, k):
        return j if diag == 'kn' else k

    if mode == 'tn':
        a_spec = pl.BlockSpec((None, tk, tm), lambda bo, i, j, br, k: (bsel(a_b, bo, br), ki(j, k), mi(i, j)))
    else:
        a_spec = pl.BlockSpec((None, tm, tk), lambda bo, i, j, br, k: (bsel(a_b, bo, br), mi(i, j), ki(j, k)))
    if mode == 'nt':
        b_spec = pl.BlockSpec((None, tn, tk), lambda bo, i, j, br, k: (bsel(b_b, bo, br, b_map), j, ki(j, k)))
    else:
        b_spec = pl.BlockSpec((None, tk, tn), lambda bo, i, j, br, k: (bsel(b_b, bo, br, b_map), ki(j, k), j))
    o_spec = pl.BlockSpec((None, tm, tn), lambda bo, i, j, br, k: (bsel(o_b, bo, br, o_map), mi(i, j), j))
    dn = {'nn': (((1,), (0,)), ((), ())), 'nt': (((1,), (1,)), ((), ())), 'tn': (((0,), (0,)), ((), ()))}[mode]

    def body(a_ref, b_ref, *rest):
        o_ref, scratch = rest[len(after)], rest[len(after) + 1:]
        r = lax.dot_general(a_ref[...].astype(BF16), b_ref[...].astype(BF16), dn, preferred_element_type=F32)
        if nred == 1:
            o_ref[...] = r.astype(out_dtype)
        else:
            acc = scratch[0]
            step = pl.program_id(3) * nk + pl.program_id(4)

            @pl.when(step == 0)
            def _():
                acc[...] = r

            @pl.when(step > 0)
            def _():
                acc[...] += r

            @pl.when(step == nred - 1)
            def _():
                o_ref[...] = acc[...].astype(out_dtype)

    out = _call(
        body, name=name,
        grid=(nbo, nm, N // tn, nbr, nk),
        in_specs=[a_spec, b_spec] + [ANY] * len(after), out_specs=o_spec,
        out_shape=jax.ShapeDtypeStruct((nbo if o_b else 1, M, N), out_dtype),
        scratch_shapes=[] if nred == 1 else [pltpu.VMEM((tm, tn), F32)],
        compiler_params=_cparams(("parallel", "parallel", "parallel", "arbitrary", "arbitrary")),
    )(a3, b3, *after)
    return out if o_b else out[0]


def _add_ln_fwd(x, r, g, b, *, name):
    S, D = x.shape
    tr = _pick(S, 256)

    def body(x_ref, r_ref, g_ref, b_ref, o_ref, xh_ref, rs_ref):
        z = ALPHA * x_ref[...] + r_ref[...]
        mu = jnp.mean(z, axis=-1, keepdims=True)
        zc = z - mu
        var = jnp.mean(zc * zc, axis=-1, keepdims=True)
        rstd = lax.rsqrt(var + LN_EPS)
        xh = zc * rstd
        xh_ref[...] = xh
        rs_ref[...] = rstd
        o_ref[...] = xh * g_ref[...] + b_ref[...]

    row = pl.BlockSpec((tr, D), lambda i: (i, 0))
    vec = pl.BlockSpec((1, D), lambda i: (0, 0))
    return _call(
        body, name=name, grid=(S // tr,),
        in_specs=[row, row, vec, vec],
        out_specs=[row, row, pl.BlockSpec((tr, 1), lambda i: (i, 0))],
        out_shape=[jax.ShapeDtypeStruct((S, D), F32), jax.ShapeDtypeStruct((S, D), F32),
                   jax.ShapeDtypeStruct((S, 1), F32)],
        compiler_params=_cparams(("parallel",)),
    )(x, r, g.reshape(1, D), b.reshape(1, D))


def _ln_bwd(da, db, xhat, rstd, g, *, name, after=()):
    S, D = xhat.shape
    tr = _pick(S, 256)
    two = db is not None

    def body(*refs):
        refs = refs[len(after):]
        if two:
            da_ref, db_ref, xh_ref, rs_ref, g_ref, dz_ref, dg_ref, dbt_ref = refs
            dy = ALPHA * da_ref[...] + db_ref[...]
        else:
            da_ref, xh_ref, rs_ref, g_ref, dz_ref, dg_ref, dbt_ref = refs
            dy = da_ref[...]
        xh = xh_ref[...]
        dxh = dy * g_ref[...]
        m1 = jnp.mean(dxh, axis=-1, keepdims=True)
        m2 = jnp.mean(dxh * xh, axis=-1, keepdims=True)
        dz_ref[...] = rs_ref[...] * (dxh - m1 - xh * m2)
        pg = jnp.sum(dy * xh, axis=0, keepdims=True)
        pb = jnp.sum(dy, axis=0, keepdims=True)

        @pl.when(pl.program_id(0) == 0)
        def _():
            dg_ref[...] = pg
            dbt_ref[...] = pb

        @pl.when(pl.program_id(0) > 0)
        def _():
            dg_ref[...] += pg
            dbt_ref[...] += pb

    row = pl.BlockSpec((tr, D), lambda i: (i, 0))
    vec = pl.BlockSpec((1, D), lambda i: (0, 0))
    ins = list(after) + [da] + ([db] if two else []) + [xhat, rstd, g.reshape(1, D)]
    in_specs = [ANY] * len(after) + [row] + ([row] if two else []) + [row, pl.BlockSpec((tr, 1), lambda i: (i, 0)), vec]
    return _call(
        body, name=name, grid=(S // tr,),
        in_specs=in_specs, out_specs=[row, vec, vec],
        out_shape=[jax.ShapeDtypeStruct((S, D), F32), jax.ShapeDtypeStruct((1, D), F32),
                   jax.ShapeDtypeStruct((1, D), F32)],
        compiler_params=_cparams(("arbitrary",)),
    )(*ins)


def _loss_grad(y, t, *, name):
    S, D = y.shape
    tr = _pick(S, 256)

    def body(y_ref, t_ref, dy_ref, l_ref):
        e = y_ref[...] - t_ref[...]
        dy_ref[...] = e * (1.0 / D)
        part = 0.5 * jnp.sum(jnp.sum(e * e, axis=-1, keepdims=True) * (1.0 / D), axis=0, keepdims=True)

        @pl.when(pl.program_id(0) == 0)
        def _():
            l_ref[...] = part

        @pl.when(pl.program_id(0) > 0)
        def _():
            l_ref[...] += part

    row = pl.BlockSpec((tr, D), lambda i: (i, 0))
    return _call(
        body, name=name, grid=(S // tr,), in_specs=[row, row],
        out_specs=[row, pl.BlockSpec((1, 1), lambda i: (0, 0))],
        out_shape=[jax.ShapeDtypeStruct((S, D), F32), jax.ShapeDtypeStruct((1, 1), F32)],
        compiler_params=_cparams(("arbitrary",)),
    )(y, t)


def _combine(terms, scales, *, name, out_dtype=F32):
    S, D = terms[0].shape
    tr = _pick(S, 256)
    n = len(terms)

    def body(*refs):
        acc = scales[0] * refs[0][...].astype(F32)
        for i in range(1, n):
            acc = acc + scales[i] * refs[i][...].astype(F32)
        refs[n][...] = acc.astype(out_dtype)

    row = pl.BlockSpec((tr, D), lambda i: (i, 0))
    return _call(
        body, name=name, grid=(S // tr,), in_specs=[row] * n, out_specs=row,
        out_shape=jax.ShapeDtypeStruct((S, D), out_dtype),
        compiler_params=_cparams(("parallel",)),
    )(*terms)


def _split3(x):
    h = x.astype(BF16)
    r = x - h.astype(F32)
    m = r.astype(BF16)
    l = (r - m.astype(F32)).astype(BF16)
    return h, m, l


def _tri_matmul(tri_bf, x):
    h, m, l = _split3(x)
    dn = (((1,), (0,)), ((), ()))
    return (lax.dot_general(tri_bf, l, dn, preferred_element_type=F32)
            + lax.dot_general(tri_bf, m, dn, preferred_element_type=F32)
            + lax.dot_general(tri_bf, h, dn, preferred_element_type=F32))


def _gate_fwd(fl, bf, *, name):
    S = fl.shape[0]
    tc = _pick(S, 256)
    nchunk = S // tc

    def body(fl_ref, bf_ref, c_ref, sg_ref):
        r = lax.broadcasted_iota(jnp.int32, (tc, tc), 0)
        cidx = lax.broadcasted_iota(jnp.int32, (tc, tc), 1)
        tri = (r >= cidx).astype(BF16)
        carry = jnp.zeros((1, LANE), F32)
        for ch in range(nchunk):
            x = fl_ref[pl.ds(ch * tc, tc), :] + bf_ref[...]
            lf = jnp.minimum(x, 0.0) - jnp.log(1.0 + jnp.exp(-jnp.abs(x)))
            sg_ref[pl.ds(ch * tc, tc), :] = jax.nn.sigmoid(-x)
            c_ref[pl.ds(ch * tc, tc), :] = _tri_matmul(tri, lf) + carry
            carry = carry + jnp.sum(lf, axis=0, keepdims=True)

    full = pl.BlockSpec((S, LANE), lambda: (0, 0))
    return _call(
        body, name=name, in_specs=[full, pl.BlockSpec((1, LANE), lambda: (0, 0))], out_specs=[full, full],
        out_shape=[jax.ShapeDtypeStruct((S, LANE), F32)] * 2,
        compiler_params=pltpu.CompilerParams(vmem_limit_bytes=VMEM_LIMIT),
    )(fl, bf)


def _gate_bwd(dc, sg, *, name):
    S = dc.shape[0]
    tc = _pick(S, 256)
    nchunk = S // tc

    def body(dc_ref, sg_ref, dfl_ref, db_ref):
        r = lax.broadcasted_iota(jnp.int32, (tc, tc), 0)
        cidx = lax.broadcasted_iota(jnp.int32, (tc, tc), 1)
        tri = (r <= cidx).astype(BF16)
        carry = jnp.zeros((1, LANE), F32)
        dbacc = jnp.zeros((1, LANE), F32)
        for ch in reversed(range(nchunk)):
            d = dc_ref[pl.ds(ch * tc, tc), :]
            dfl = (_tri_matmul(tri, d) + carry) * sg_ref[pl.ds(ch * tc, tc), :]
            dfl_ref[pl.ds(ch * tc, tc), :] = dfl
            dbacc = dbacc + jnp.sum(dfl, axis=0, keepdims=True)
            carry = carry + jnp.sum(d, axis=0, keepdims=True)
        db_ref[...] = dbacc

    full = pl.BlockSpec((S, LANE), lambda: (0, 0))
    return _call(
        body, name=name, in_specs=[full, full], out_specs=[full, pl.BlockSpec((1, LANE), lambda: (0, 0))],
        out_shape=[jax.ShapeDtypeStruct((S, LANE), F32), jax.ShapeDtypeStruct((1, LANE), F32)],
        compiler_params=pltpu.CompilerParams(vmem_limit_bytes=VMEM_LIMIT),
    )(dc, sg)


def _fox_scores(q_ref, k_ref, cc_ref, cr_ref, qi, tq, S):
    scale = 1.0 / math.sqrt(FOX_HEAD_DIM)
    s = lax.dot_general(q_ref[...].astype(BF16), k_ref[...].astype(BF16), (((1,), (1,)), ((), ())),
                        preferred_element_type=F32) * scale
    s = s + cc_ref[...] - cr_ref[...]
    row = lax.broadcasted_iota(jnp.int32, (tq, S), 0) + qi * tq
    col = lax.broadcasted_iota(jnp.int32, (tq, S), 1)
    return s, row >= col


def _fox_fwd(P, ccol, crow, *, name):
    S = P.shape[0]
    tq = _pick(S, 256)
    H = FOX_HEADS

    def body(q_ref, k_ref, v_ref, cc_ref, cr_ref, o_ref, l_ref):
        s, causal = _fox_scores(q_ref, k_ref, cc_ref, cr_ref, pl.program_id(1), tq, S)
        s = jnp.where(causal, s, -1e30)
        m = jnp.max(s, axis=-1, keepdims=True)
        e = jnp.exp(s - m)
        den = jnp.sum(e, axis=-1, keepdims=True)
        p = e / den
        o_ref[...] = jnp.dot(p.astype(BF16), v_ref[...].astype(BF16), preferred_element_type=F32)
        l_ref[...] = m + jnp.log(den)

    return _call(
        body, name=name, grid=(H, S // tq),
        in_specs=[pl.BlockSpec((tq, 128), lambda h, i: (i, h)),
                  pl.BlockSpec((S, 128), lambda h, i: (0, H + h)),
                  pl.BlockSpec((S, 128), lambda h, i: (0, 2 * H + h)),
                  pl.BlockSpec((None, tq, 1), lambda h, i: (h, i, 0)),
                  pl.BlockSpec((None, 1, S), lambda h, i: (h, 0, 0))],
        out_specs=[pl.BlockSpec((tq, 128), lambda h, i: (i, h)),
                   pl.BlockSpec((None, tq, 1), lambda h, i: (h, i, 0))],
        out_shape=[jax.ShapeDtypeStruct((S, FOX_WIDTH), F32), jax.ShapeDtypeStruct((H, S, 1), F32)],
        compiler_params=_cparams(("parallel", "parallel")),
    )(P, P, P, ccol, crow)


def _fox_bwd(P, ccol, crow, o, lse, dcat, *, name):
    S = P.shape[0]
    tq = _pick(S, 256)
    H = FOX_HEADS
    nq = S // tq
    scale = 1.0 / math.sqrt(FOX_HEAD_DIM)

    def body(q_ref, k_ref, v_ref, cc_ref, cr_ref, o_ref, l_ref, do_ref,
             dq_ref, dk_ref, dv_ref, dcc_ref, dcr_ref, dk_acc, dv_acc):
        qi = pl.program_id(1)
        s, causal = _fox_scores(q_ref, k_ref, cc_ref, cr_ref, qi, tq, S)
        p = jnp.where(causal, jnp.exp(s - l_ref[...]), 0.0)
        do = do_ref[...]
        do_bf = do.astype(BF16)
        dp = lax.dot_general(do_bf, v_ref[...].astype(BF16), (((1,), (1,)), ((), ())), preferred_element_type=F32)
        delta = jnp.sum(do * o_ref[...], axis=-1, keepdims=True)
        ds = p * (dp - delta)
        ds_bf = ds.astype(BF16)
        dq_ref[...] = (jnp.dot(ds_bf, k_ref[...].astype(BF16), preferred_element_type=F32) * scale).astype(BF16)
        dkp = lax.dot_general(ds_bf, q_ref[...].astype(BF16), (((0,), (0,)), ((), ())),
                              preferred_element_type=F32) * scale
        dvp = lax.dot_general(p.astype(BF16), do_bf, (((0,), (0,)), ((), ())), preferred_element_type=F32)
        dcc_ref[...] = jnp.sum(ds, axis=-1, keepdims=True)
        dcr = jnp.sum(ds, axis=0, keepdims=True)

        @pl.when(qi == 0)
        def _():
            dk_acc[...] = dkp
            dv_acc[...] = dvp
            dcr_ref[...] = dcr

        @pl.when(qi > 0)
        def _():
            dk_acc[...] += dkp
            dv_acc[...] += dvp
            dcr_ref[...] += dcr

        @pl.when(qi == nq - 1)
        def _():
            dk_ref[...] = dk_acc[...].astype(BF16)
            dv_ref[...] = dv_acc[...].astype(BF16)

    qblk = pl.BlockSpec((tq, 128), lambda h, i: (i, h))
    kvo = pl.BlockSpec((S, 128), lambda h, i: (0, h))
    col = pl.BlockSpec((None, tq, 1), lambda h, i: (h, i, 0))
    rowv = pl.BlockSpec((None, 1, S), lambda h, i: (h, 0, 0))
    return _call(
        body, name=name, grid=(H, nq),
        in_specs=[qblk,
                  pl.BlockSpec((S, 128), lambda h, i: (0, H + h)),
                  pl.BlockSpec((S, 128), lambda h, i: (0, 2 * H + h)),
                  col, rowv, qblk, col, qblk],
        out_specs=[qblk, kvo, kvo, col, rowv],
        out_shape=[jax.ShapeDtypeStruct((S, FOX_WIDTH), BF16)] * 3
        + [jax.ShapeDtypeStruct((H, S, 1), F32), jax.ShapeDtypeStruct((H, 1, S), F32)],
        scratch_shapes=[pltpu.VMEM((S, 128), F32), pltpu.VMEM((S, 128), F32)],
        compiler_params=_cparams(("parallel", "arbitrary")),
    )(P, P, P, ccol, crow, o, lse, dcat)


def _s5_disc_fwd(lr, li, ls, *, name, after=()):
    G, Pn = lr.shape

    def body(lr_ref, li_ref, ls_ref, ar_ref, ai_ref, gr_ref, gi_ref):
        lr_, li_ = lr_ref[...], li_ref[...]
        dt = jnp.exp(ls_ref[...])
        mag = jnp.exp(lr_ * dt)
        th = li_ * dt
        ar = mag * jnp.cos(th)
        ai = mag * jnp.sin(th)
        den = lr_ * lr_ + li_ * li_
        xr = ar - 1.0
        ar_ref[...] = ar
        ai_ref[...] = ai
        gr_ref[...] = (xr * lr_ + ai * li_) / den
        gi_ref[...] = (ai * lr_ - xr * li_) / den

    sq = pl.BlockSpec((G, Pn), lambda: (0, 0))
    return _call(
        body, after=after, name=name, in_specs=[sq, sq, pl.BlockSpec((G, 1), lambda: (0, 0))], out_specs=[sq] * 4,
        out_shape=[jax.ShapeDtypeStruct((G, Pn), F32)] * 4,
    )(lr, li, ls)


def _s5_disc_bwd(lr, li, ls, dar, dai, dgr, dgi, *, name):
    G, Pn = lr.shape

    def body(lr_ref, li_ref, ls_ref, dar_ref, dai_ref, dgr_ref, dgi_ref, dlr_ref, dli_ref, dls_ref):
        lr_, li_ = lr_ref[...], li_ref[...]
        dt = jnp.exp(ls_ref[...])
        mag = jnp.exp(lr_ * dt)
        th = li_ * dt
        ar = mag * jnp.cos(th)
        ai = mag * jnp.sin(th)
        den = lr_ * lr_ + li_ * li_
        xr = ar - 1.0
        xi = ai
        g_re = (xr * lr_ + xi * li_) / den
        g_im = (xi * lr_ - xr * li_) / den
        dgr_, dgi_ = dgr_ref[...], dgi_ref[...]
        dxr = (dgr_ * lr_ - dgi_ * li_) / den
        dxi = (dgr_ * li_ + dgi_ * lr_) / den
        dden = -(dgr_ * g_re + dgi_ * g_im) / den
        dlr = (dgr_ * xr + dgi_ * xi) / den + 2.0 * dden * lr_
        dli = (dgr_ * xi - dgi_ * xr) / den + 2.0 * dden * li_
        da_r = dar_ref[...] + dxr
        da_i = dai_ref[...] + dxi
        dmag_mag = da_r * ar + da_i * ai
        dth = da_i * ar - da_r * ai
        dlr_ref[...] = dlr + dmag_mag * dt
        dli_ref[...] = dli + dth * dt
        ddt = jnp.sum(dmag_mag * lr_ + dth * li_, axis=-1, keepdims=True)
        dls_ref[...] = ddt * dt

    sq = pl.BlockSpec((G, Pn), lambda: (0, 0))
    c1 = pl.BlockSpec((G, 1), lambda: (0, 0))
    return _call(
        body, name=name, in_specs=[sq, sq, c1, sq, sq, sq, sq], out_specs=[sq, sq, c1],
        out_shape=[jax.ShapeDtypeStruct((G, Pn), F32)] * 2 + [jax.ShapeDtypeStruct((G, 1), F32)],
    )(lr, li, ls, dar, dai, dgr, dgi)


def _s5_bb_fwd(gr, gi, br, bi, *, name):
    R, C = br.shape

    def body(gr_ref, gi_ref, br_ref, bi_ref, or_ref, oi_ref):
        g_r, g_i, b_r, b_i = gr_ref[...], gi_ref[...], br_ref[...], bi_ref[...]
        or_ref[...] = g_r * b_r - g_i * b_i
        oi_ref[...] = g_r * b_i + g_i * b_r

    w = pl.BlockSpec((R, C), lambda: (0, 0))
    c1 = pl.BlockSpec((R, 1), lambda: (0, 0))
    return _call(body, name=name, in_specs=[c1, c1, w, w], out_specs=[w, w],
                 out_shape=[jax.ShapeDtypeStruct((R, C), F32)] * 2)(gr, gi, br, bi)


def _s5_bb_bwd(gr, gi, br, bi, dbbr, dbbi, *, name):
    R, C = br.shape

    def body(gr_ref, gi_ref, br_ref, bi_ref, dr_ref, di_ref, dbr_ref, dbi_ref, dgr_ref, dgi_ref):
        g_r, g_i, b_r, b_i = gr_ref[...], gi_ref[...], br_ref[...], bi_ref[...]
        d_r, d_i = dr_ref[...], di_ref[...]
        dbr_ref[...] = g_r * d_r + g_i * d_i
        dbi_ref[...] = g_r * d_i - g_i * d_r
        dgr_ref[...] = jnp.sum(d_r * b_r + d_i * b_i, axis=-1, keepdims=True)
        dgi_ref[...] = jnp.sum(d_i * b_r - d_r * b_i, axis=-1, keepdims=True)

    w = pl.BlockSpec((R, C), lambda: (0, 0))
    c1 = pl.BlockSpec((R, 1), lambda: (0, 0))
    return _call(body, name=name, in_specs=[c1, c1, w, w, w, w], out_specs=[w, w, c1, c1],
                 out_shape=[jax.ShapeDtypeStruct((R, C), F32)] * 2 + [jax.ShapeDtypeStruct((R, 1), F32)] * 2,
                 )(gr, gi, br, bi, dbbr, dbbi)


_DIAG_TILE = 8


def _diag_mask(gr, gc):
    rows, cols = _DIAG_TILE * gr, _DIAG_TILE * gc
    r = lax.broadcasted_iota(jnp.int32, (rows, cols), 0) >> (gr.bit_length() - 1)
    c = lax.broadcasted_iota(jnp.int32, (rows, cols), 1) >> (gc.bit_length() - 1)
    return r == c


def _diag_expand(t2, gr, gc, *, name, after=()):
    _, R, _ = t2.shape
    G = R // gr
    nt = G // _DIAG_TILE
    rows, cols = _DIAG_TILE * gr, _DIAG_TILE * gc

    def body(t_ref, o_ref):
        src = lax.broadcasted_iota(jnp.int32, (gc, cols), 0)
        dst = lax.broadcasted_iota(jnp.int32, (gc, cols), 1) & (gc - 1)
        spread = (src == dst).astype(BF16)
        y = jnp.dot(t_ref[...].astype(BF16), spread, preferred_element_type=F32)
        o_ref[...] = jnp.where(_diag_mask(gr, gc), y, 0.0).astype(BF16)

    return _call(
        body, after=after, name=name, grid=(2, nt),
        in_specs=[pl.BlockSpec((None, rows, gc), lambda p, i: (p, i, 0))],
        out_specs=pl.BlockSpec((None, rows, cols), lambda p, i: (p, i, i)),
        out_shape=jax.ShapeDtypeStruct((2, R, G * gc), BF16),
        compiler_params=_cparams(("parallel",) * 2),
    )(t2)


def _diag_extract(xd, gr, gc, *, name):
    _, R, _ = xd.shape
    nt = R // gr // _DIAG_TILE
    rows, cols = _DIAG_TILE * gr, _DIAG_TILE * gc

    def body(x_ref, o_ref):
        src = lax.broadcasted_iota(jnp.int32, (cols, gc), 0) & (gc - 1)
        dst = lax.broadcasted_iota(jnp.int32, (cols, gc), 1)
        fold = (src == dst).astype(BF16)
        parts = _split3(jnp.where(_diag_mask(gr, gc), x_ref[...], 0.0))
        acc = jnp.dot(parts[2], fold, preferred_element_type=F32)
        acc = acc + jnp.dot(parts[1], fold, preferred_element_type=F32)
        o_ref[...] = acc + jnp.dot(parts[0], fold, preferred_element_type=F32)

    return _call(
        body, name=name, grid=(2, nt),
        in_specs=[pl.BlockSpec((None, rows, cols), lambda p, i: (p, i, i))],
        out_specs=pl.BlockSpec((None, rows, gc), lambda p, i: (p, i, 0)),
        out_shape=jax.ShapeDtypeStruct((2, R, gc), F32),
        compiler_params=_cparams(("parallel",) * 2),
    )(xd)


SCAN_BLOCK = 8


def _cpowers(ar, ai, sign):
    ai = sign * ai
    out = [(ar, ai)]
    for _ in range(SCAN_BLOCK - 1):
        pr, pi = out[-1]
        out.append((pr * ar - pi * ai, pr * ai + pi * ar))
    return out


def _row_table(pw, row, index_of_row):
    tr_ = jnp.broadcast_to(pw[index_of_row(0)][0], row.shape)
    ti_ = jnp.broadcast_to(pw[index_of_row(0)][1], row.shape)
    for r in range(1, SCAN_BLOCK):
        pr, pi = pw[index_of_row(r)]
        tr_ = jnp.where(row == r, pr, tr_)
        ti_ = jnp.where(row == r, pi, ti_)
    return tr_, ti_


def _s5_scan_fwd(bu, a, *, name):
    _, S, N = bu.shape
    tc = 512
    nt = N // tc

    def body(a_ref, b_ref, h_ref):
        pw = _cpowers(a_ref[0], a_ref[1], 1.0)
        row = lax.broadcasted_iota(jnp.int32, (SCAN_BLOCK, tc), 0)
        lead_r, lead_i = _row_table(pw, row, lambda r: r)
        mult = {sh: (jnp.where(row >= sh, pw[sh - 1][0], 0.0), jnp.where(row >= sh, pw[sh - 1][1], 0.0))
                for sh in (1, 2, 4)}

        def step(k, carry):
            cr, ci = carry
            rows = pl.ds(pl.multiple_of(k * SCAN_BLOCK, SCAN_BLOCK), SCAN_BLOCK)
            xr, xi = b_ref[0, rows, :], b_ref[1, rows, :]
            for sh in (1, 2, 4):
                sr, si = pltpu.roll(xr, sh, 0), pltpu.roll(xi, sh, 0)
                kr, ki = mult[sh]
                xr, xi = xr + kr * sr - ki * si, xi + kr * si + ki * sr
            h_ref[0, rows, :] = xr + lead_r * cr - lead_i * ci
            h_ref[1, rows, :] = xi + lead_r * ci + lead_i * cr
            last = row == SCAN_BLOCK - 1
            tr_ = jnp.sum(jnp.where(last, xr, 0.0), axis=0, keepdims=True)
            ti_ = jnp.sum(jnp.where(last, xi, 0.0), axis=0, keepdims=True)
            a8r, a8i = pw[SCAN_BLOCK - 1]
            return a8r * cr - a8i * ci + tr_, a8r * ci + a8i * cr + ti_

        z = jnp.zeros((1, tc), F32)
        lax.fori_loop(0, S // SCAN_BLOCK, step, (z, z), unroll=2)

    vec = pl.BlockSpec((2, 1, tc), lambda j: (0, 0, j))
    mat = pl.BlockSpec((2, S, tc), lambda j: (0, 0, j))
    return _call(
        body, name=name, grid=(nt,), in_specs=[vec, mat], out_specs=mat,
        out_shape=jax.ShapeDtypeStruct((2, S, N), F32),
        compiler_params=_cparams(("parallel",)),
    )(a, bu)


def _s5_scan_bwd(g, h, a, *, name):
    _, S, N = g.shape
    tc = 256
    nt = N // tc

    def body(a_ref, g_ref, h_ref, l_ref, da_ref):
        pw = _cpowers(a_ref[0], a_ref[1], -1.0)
        row = lax.broadcasted_iota(jnp.int32, (SCAN_BLOCK, tc), 0)
        tail_r, tail_i = _row_table(pw, row, lambda r: SCAN_BLOCK - 1 - r)
        nb = S // SCAN_BLOCK
        mult = {sh: (jnp.where(row < SCAN_BLOCK - sh, pw[sh - 1][0], 0.0),
                     jnp.where(row < SCAN_BLOCK - sh, pw[sh - 1][1], 0.0)) for sh in (1, 2, 4)}

        def step(i, carry):
            k = nb - 1 - i
            cr, ci, dar, dai = carry
            rows = pl.ds(pl.multiple_of(k * SCAN_BLOCK, SCAN_BLOCK), SCAN_BLOCK)
            xr, xi = g_ref[0, rows, :], g_ref[1, rows, :]
            for sh in (1, 2, 4):
                sr, si = pltpu.roll(xr, SCAN_BLOCK - sh, 0), pltpu.roll(xi, SCAN_BLOCK - sh, 0)
                kr, ki = mult[sh]
                xr, xi = xr + kr * sr - ki * si, xi + kr * si + ki * sr
            lr = xr + tail_r * cr - tail_i * ci
            li = xi + tail_r * ci + tail_i * cr
            l_ref[0, rows, :] = lr
            l_ref[1, rows, :] = li
            prev = pl.ds(pl.multiple_of(jnp.maximum(k - 1, 0) * SCAN_BLOCK, SCAN_BLOCK), SCAN_BLOCK)
            has_prev = jnp.where(k > 0, 1.0, 0.0).astype(F32)
            first = row == 0
            hpr = jnp.where(first, pltpu.roll(h_ref[0, prev, :], 1, 0) * has_prev, pltpu.roll(h_ref[0, rows, :], 1, 0))
            hpi = jnp.where(first, pltpu.roll(h_ref[1, prev, :], 1, 0) * has_prev, pltpu.roll(h_ref[1, rows, :], 1, 0))
            tr_ = jnp.sum(jnp.where(first, xr, 0.0), axis=0, keepdims=True)
            ti_ = jnp.sum(jnp.where(first, xi, 0.0), axis=0, keepdims=True)
            a8r, a8i = pw[SCAN_BLOCK - 1]
            return (a8r * cr - a8i * ci + tr_, a8r * ci + a8i * cr + ti_,
                    dar + lr * hpr + li * hpi, dai + li * hpr - lr * hpi)

        z = jnp.zeros((1, tc), F32)
        z8 = jnp.zeros((SCAN_BLOCK, tc), F32)
        _, _, dar, dai = lax.fori_loop(0, nb, step, (z, z, z8, z8), unroll=2)
        da_ref[0] = jnp.sum(dar, axis=0, keepdims=True)
        da_ref[1] = jnp.sum(dai, axis=0, keepdims=True)

    vec = pl.BlockSpec((2, 1, tc), lambda j: (0, 0, j))
    mat = pl.BlockSpec((2, S, tc), lambda j: (0, 0, j))
    return _call(
        body, name=name, grid=(nt,), in_specs=[vec, mat, mat], out_specs=[mat, vec],
        out_shape=[jax.ShapeDtypeStruct((2, S, N), F32), jax.ShapeDtypeStruct((2, 1, N), F32)],
        compiler_params=_cparams(("parallel",)),
    )(a, g, h)


_GELU_C = math.sqrt(2.0 / math.pi)


def _s5_out_fwd(yc, P, dskip, *, name):
    S, W = yc.shape
    tr = _pick(S, 256)
    ub = 3 * FOX_WIDTH // W

    def body(yc_ref, u_ref, d_ref, y_ref, yg_ref):
        y = yc_ref[...] + d_ref[...] * u_ref[...]
        y_ref[...] = y
        t = jnp.tanh(_GELU_C * (y + 0.044715 * y * y * y))
        yg_ref[...] = (0.5 * y * (1.0 + t)).astype(BF16)

    row = pl.BlockSpec((tr, W), lambda i: (i, 0))
    return _call(
        body, name=name, grid=(S // tr,),
        in_specs=[row, pl.BlockSpec((tr, W), lambda i: (i, ub)), pl.BlockSpec((1, W), lambda i: (0, 0))],
        out_specs=[row, row],
        out_shape=[jax.ShapeDtypeStruct((S, W), F32), jax.ShapeDtypeStruct((S, W), BF16)],
        compiler_params=_cparams(("parallel",)),
    )(yc, P, dskip)


def _s5_out_bwd(dyg, y, P, dskip, *, name):
    S, W = y.shape
    tr = _pick(S, 256)
    ub = 3 * FOX_WIDTH // W

    def body(dyg_ref, y_ref, u_ref, d_ref, dy_ref, du_ref, dd_ref):
        y_ = y_ref[...]
        inner = _GELU_C * (y_ + 0.044715 * y_ * y_ * y_)
        t = jnp.tanh(inner)
        dgelu = 0.5 * (1.0 + t) + 0.5 * y_ * (1.0 - t * t) * _GELU_C * (1.0 + 3.0 * 0.044715 * y_ * y_)
        dy = dyg_ref[...] * dgelu
        dy_ref[...] = dy.astype(BF16)
        du_ref[...] = d_ref[...] * dy
        part = jnp.sum(dy * u_ref[...], axis=0, keepdims=True)

        @pl.when(pl.program_id(0) == 0)
        def _():
            dd_ref[...] = part

        @pl.when(pl.program_id(0) > 0)
        def _():
            dd_ref[...] += part

    row = pl.BlockSpec((tr, W), lambda i: (i, 0))
    vec = pl.BlockSpec((1, W), lambda i: (0, 0))
    return _call(
        body, name=name, grid=(S // tr,),
        in_specs=[row, row, pl.BlockSpec((tr, W), lambda i: (i, ub)), vec],
        out_specs=[row, row, vec],
        out_shape=[jax.ShapeDtypeStruct((S, W), BF16), jax.ShapeDtypeStruct((S, W), F32),
                   jax.ShapeDtypeStruct((1, W), F32)],
        compiler_params=_cparams(("arbitrary",)),
    )(dyg, y, P, dskip)


def _glu_fwd(z, *, name):
    S, W2 = z.shape
    W = W2 // 2
    tr = _pick(S, 256)

    def body(z1_ref, z2_ref, o_ref):
        o_ref[...] = (z1_ref[...] * jax.nn.sigmoid(z2_ref[...])).astype(BF16)

    return _call(
        body, name=name, grid=(S // tr,),
        in_specs=[pl.BlockSpec((tr, W), lambda i: (i, 0)), pl.BlockSpec((tr, W), lambda i: (i, 1))],
        out_specs=pl.BlockSpec((tr, W), lambda i: (i, 0)),
        out_shape=jax.ShapeDtypeStruct((S, W), BF16),
        compiler_params=_cparams(("parallel",)),
    )(z, z)


def _glu_bwd(z, dcat, *, name):
    S, W2 = z.shape
    W = W2 // 2
    tr = _pick(S, 256)

    def body(z1_ref, z2_ref, d_ref, dz1_ref, dz2_ref):
        sg = jax.nn.sigmoid(z2_ref[...])
        d = d_ref[...]
        dz1_ref[...] = (d * sg).astype(BF16)
        dz2_ref[...] = (d * z1_ref[...] * sg * (1.0 - sg)).astype(BF16)

    lo = pl.BlockSpec((tr, W), lambda i: (i, 0))
    hi = pl.BlockSpec((tr, W), lambda i: (i, 1))
    dz1, dz2 = _call(
        body, name=name, grid=(S // tr,), in_specs=[lo, hi, hi], out_specs=[lo, lo],
        out_shape=[jax.ShapeDtypeStruct((S, W), BF16)] * 2,
        compiler_params=_cparams(("parallel",)),
    )(z, z, dcat)
    return jnp.concatenate([dz1, dz2], axis=1)


ACT_ROWS = 16
ACT_COLS = 256


def _shift_down(cur, prev, k, row):
    return jnp.where(row >= k, pltpu.roll(cur, k, 0), pltpu.roll(prev, k, 0))


def _shift_up(cur, nxt, k, row):
    n = cur.shape[0]
    return jnp.where(row < n - k, pltpu.roll(cur, n - k, 0), pltpu.roll(nxt, n - k, 0))


def _act_fwd(h, cw, cb, *, name):
    _, S, FP = h.shape
    tr = _pick(S, 256)
    hb = tr // ACT_ROWS
    nq = tr // ACT_ROWS

    def body(g_ref, gh_ref, v_ref, vh_ref, wg_ref, wv_ref, bg_ref, bv_ref, a_ref, hc_ref):
        first = pl.program_id(1) == 0
        for c0 in range(0, FP, ACT_COLS):
            cw_ = min(ACT_COLS, FP - c0)
            cols = pl.ds(c0, cw_)
            rw = lax.broadcasted_iota(jnp.int32, (ACT_ROWS, cw_), 0)
            wg = [wg_ref[pl.ds(k, 1), cols] for k in range(3)]
            wv = [wv_ref[pl.ds(k, 1), cols] for k in range(3)]
            bg, bv = bg_ref[:, cols], bv_ref[:, cols]
            halo_g = jnp.where(first, 0.0, gh_ref[:, cols])
            halo_v = jnp.where(first, 0.0, vh_ref[:, cols])

            def chunk(q, _):
                rows = pl.ds(pl.multiple_of(q * ACT_ROWS, ACT_ROWS), ACT_ROWS)
                before = pl.ds(pl.multiple_of(jnp.maximum(q - 1, 0) * ACT_ROWS, ACT_ROWS), ACT_ROWS)
                g, v = g_ref[rows, cols], v_ref[rows, cols]
                gp = jnp.where(q > 0, g_ref[before, cols], halo_g)
                vp = jnp.where(q > 0, v_ref[before, cols], halo_v)
                cg = bg + wg[2] * g + wg[1] * _shift_down(g, gp, 1, rw) + wg[0] * _shift_down(g, gp, 2, rw)
                cv = bv + wv[2] * v + wv[1] * _shift_down(v, vp, 1, rw) + wv[0] * _shift_down(v, vp, 2, rw)
                a_ref[rows, cols] = (cg * jax.nn.sigmoid(cg) * cv).astype(BF16)
                hc_ref[0, rows, cols] = cg
                hc_ref[1, rows, cols] = cv
                return 0

            lax.fori_loop(0, nq, chunk, 0, unroll=2)

    def main(off):
        return pl.BlockSpec((None, tr, FP), lambda j, i: (j + off, i, 0))

    def halo(off):
        return pl.BlockSpec((None, ACT_ROWS, FP), lambda j, i: (j + off, jnp.maximum(i * hb - 1, 0), 0))

    def wspec(off):
        return pl.BlockSpec((None, 3, FP), lambda j, i: (j + off, 0, 0))

    def bspec(off):
        return pl.BlockSpec((None, 1, FP), lambda j, i: (j + off, 0, 0))

    cb3 = cb.reshape(4, 1, FP)
    return _call(
        body, name=name, grid=(2, S // tr),
        in_specs=[main(0), halo(0), main(2), halo(2), wspec(0), wspec(2), bspec(0), bspec(2)],
        out_specs=[pl.BlockSpec((None, tr, FP), lambda j, i: (j, i, 0)),
                   pl.BlockSpec((None, 2, tr, FP), lambda j, i: (j, 0, i, 0))],
        out_shape=[jax.ShapeDtypeStruct((2, S, FP), BF16), jax.ShapeDtypeStruct((2, 2, S, FP), F32)],
        compiler_params=_cparams(("parallel", "parallel")),
    )(h, h, h, h, cw, cw, cb3, cb3)


def _act_bwd(h, hc, da, cw, *, name):
    _, S, FP = h.shape
    tr = _pick(S, 256)
    nq = tr // ACT_ROWS
    nr = S // tr
    half = ACT_ROWS // 2

    def fold(x):
        return x[:half] + x[half:]

    def body(g_ref, v_ref, hc_ref, da_ref, wg_ref, wv_ref,
             dh_ref, dwg_ref, dwv_ref, dbg_ref, dbv_ref, carry_g, carry_v):
        i = pl.program_id(1)
        bottom = i == 0
        for c0 in range(0, FP, ACT_COLS):
            cw_ = min(ACT_COLS, FP - c0)
            cols = pl.ds(c0, cw_)
            rw = lax.broadcasted_iota(jnp.int32, (ACT_ROWS, cw_), 0)
            wg = [wg_ref[pl.ds(k, 1), cols] for k in range(3)]
            wv = [wv_ref[pl.ds(k, 1), cols] for k in range(3)]
            after_g = jnp.where(bottom, 0.0, carry_g[:, cols])
            after_v = jnp.where(bottom, 0.0, carry_v[:, cols])

            def chunk(s, carry):
                ng, nv, acc = carry[0], carry[1], carry[2:]
                q = nq - 1 - s
                rows = pl.ds(pl.multiple_of(q * ACT_ROWS, ACT_ROWS), ACT_ROWS)
                g, v = g_ref[rows, cols], v_ref[rows, cols]
                cg, cv = hc_ref[0, rows, cols], hc_ref[1, rows, cols]
                sg = jax.nn.sigmoid(cg)
                d = da_ref[rows, cols]
                dcg = d * cv * sg * (1.0 + cg * (1.0 - sg))
                dcv = d * cg * sg
                ug1, ug2 = _shift_up(dcg, ng, 1, rw), _shift_up(dcg, ng, 2, rw)
                uv1, uv2 = _shift_up(dcv, nv, 1, rw), _shift_up(dcv, nv, 2, rw)
                dh_ref[0, rows, cols] = (wg[2] * dcg + wg[1] * ug1 + wg[0] * ug2).astype(BF16)
                dh_ref[1, rows, cols] = (wv[2] * dcv + wv[1] * uv1 + wv[0] * uv2).astype(BF16)
                terms = (ug2 * g, ug1 * g, dcg * g, dcg, uv2 * v, uv1 * v, dcv * v, dcv)
                return (dcg, dcv) + tuple(a + fold(t) for a, t in zip(acc, terms))

            zero = jnp.zeros((half, cw_), F32)
            out = lax.fori_loop(0, nq, chunk, (after_g, after_v) + (zero,) * 8, unroll=2)
            carry_g[:, cols] = out[0]
            carry_v[:, cols] = out[1]
            sums = [jnp.sum(a, axis=0, keepdims=True) for a in out[2:]]

            @pl.when(bottom)
            def _():
                for k in range(3):
                    dwg_ref[pl.ds(k, 1), cols] = sums[k]
                    dwv_ref[pl.ds(k, 1), cols] = sums[4 + k]
                dbg_ref[:, cols] = sums[3]
                dbv_ref[:, cols] = sums[7]

            @pl.when(jnp.logical_not(bottom))
            def _():
                for k in range(3):
                    dwg_ref[pl.ds(k, 1), cols] += sums[k]
                    dwv_ref[pl.ds(k, 1), cols] += sums[4 + k]
                dbg_ref[:, cols] += sums[3]
                dbv_ref[:, cols] += sums[7]

    def main(off):
        return pl.BlockSpec((None, tr, FP), lambda j, i: (j + off, nr - 1 - i, 0))

    def wspec(off):
        return pl.BlockSpec((None, 3, FP), lambda j, i: (j + off, 0, 0))

    bspec = pl.BlockSpec((None, 1, FP), lambda j, i: (j, 0, 0))
    pair = pl.BlockSpec((None, 2, tr, FP), lambda j, i: (j, 0, nr - 1 - i, 0))
    dh, dwg, dwv, dbg, dbv = _call(
        body, name=name, grid=(2, nr),
        in_specs=[main(0), main(2), pair, main(0), wspec(0), wspec(2)],
        out_specs=[pair, wspec(0), wspec(0), bspec, bspec],
        out_shape=[jax.ShapeDtypeStruct((2, 2, S, FP), BF16)]
        + [jax.ShapeDtypeStruct((2, 3, FP), F32)] * 2 + [jax.ShapeDtypeStruct((2, 1, FP), F32)] * 2,
        scratch_shapes=[pltpu.VMEM((ACT_ROWS, FP), F32), pltpu.VMEM((ACT_ROWS, FP), F32)],
        compiler_params=_cparams(("parallel", "arbitrary")),
    )(h, h, hc, da, cw, cw)
    return (dh.reshape(4, S, FP), jnp.concatenate([dwg, dwv], axis=0), jnp.concatenate([dbg, dbv], axis=0))


def _rope_tables(posf, *, name, after=()):
    S = posf.shape[0]
    half = ROPE_DIM // 2
    d = np.arange(LANE) % SWA_HEAD_DIM
    invf = np.where(d < ROPE_DIM, ROPE_THETA ** (-(d % half).astype(np.float64) / half), 0.0).astype(np.float32)
    m_rot = (d < ROPE_DIM).astype(np.float32)
    m_a = (d < half).astype(np.float32)
    m_b = ((d >= half) & (d < ROPE_DIM)).astype(np.float32)
    consts = jnp.asarray(np.stack([invf, m_rot, m_a, m_b] + [np.zeros(LANE, np.float32)] * 4))

    def body(p_ref, k_ref, c_ref, sa_ref, sb_ref):
        k = k_ref[...]
        ang = p_ref[...] * k[0:1]
        co, si = jnp.cos(ang), jnp.sin(ang)
        c_ref[...] = k[1:2] * co + (1.0 - k[1:2])
        sa_ref[...] = -k[2:3] * si
        sb_ref[...] = k[3:4] * si

    full = pl.BlockSpec((S, LANE), lambda: (0, 0))
    return _call(
        body, after=after, name=name,
        in_specs=[pl.BlockSpec((S, 1), lambda: (0, 0)), pl.BlockSpec((8, LANE), lambda: (0, 0))],
        out_specs=[full] * 3, out_shape=[jax.ShapeDtypeStruct((S, LANE), F32)] * 3,
    )(posf, consts)


def _rope(xv, tabs_refs, width, inverse):
    rep = width // LANE
    c, sa, sb = (jnp.tile(t[...], (1, rep)) for t in tabs_refs)
    if not inverse:
        return xv * c + pltpu.roll(xv, width - 8, 1) * sa + pltpu.roll(xv, 8, 1) * sb
    return xv * c + pltpu.roll(xv * sa, 8, 1) + pltpu.roll(xv * sb, width - 8, 1)


def _to_heads(x, tabs, *, col0, width, rotate, name, out_dtype):
    S = x.shape[0]
    tr = _pick(S, 256)
    nh = width // SWA_HEAD_DIM
    cb = col0 // width

    def body(x_ref, c_ref, sa_ref, sb_ref, o_ref):
        xv = x_ref[...].astype(F32)
        if rotate:
            xv = _rope(xv, (c_ref, sa_ref, sb_ref), width, False)
        for h in range(nh):
            o_ref[h] = xv[:, h * SWA_HEAD_DIM:(h + 1) * SWA_HEAD_DIM].astype(out_dtype)

    tab = pl.BlockSpec((tr, LANE), lambda i: (i, 0))
    return _call(
        body, name=name, grid=(S // tr,),
        in_specs=[pl.BlockSpec((tr, width), lambda i: (i, cb)), tab, tab, tab],
        out_specs=pl.BlockSpec((nh, tr, SWA_HEAD_DIM), lambda i: (0, i, 0)),
        out_shape=jax.ShapeDtypeStruct((nh, S, SWA_HEAD_DIM), out_dtype),
        compiler_params=_cparams(("parallel",)),
    )(x, *tabs)


def _from_heads(x3, tabs, *, rotate_back, name, out_dtype, skip_rows=0):
    nh = x3.shape[0]
    S = x3.shape[1] - skip_rows
    width = nh * SWA_HEAD_DIM
    tr = _pick(S, 256) if skip_rows == 0 else skip_rows
    off = skip_rows // tr

    def body(x_ref, c_ref, sa_ref, sb_ref, o_ref):
        xv = jnp.concatenate([x_ref[h].astype(F32) for h in range(nh)], axis=1)
        if rotate_back:
            xv = _rope(xv, (c_ref, sa_ref, sb_ref), width, True)
        o_ref[...] = xv.astype(out_dtype)

    tab = pl.BlockSpec((tr, LANE), lambda i: (i, 0))
    return _call(
        body, name=name, grid=(S // tr,),
        in_specs=[pl.BlockSpec((nh, tr, SWA_HEAD_DIM), lambda i: (0, i + off, 0)), tab, tab, tab],
        out_specs=pl.BlockSpec((tr, width), lambda i: (i, 0)),
        out_shape=jax.ShapeDtypeStruct((S, width), out_dtype),
        compiler_params=_cparams(("parallel",)),
    )(x3, *tabs)


def _swa_mask(n):
    rows = SWA_GROUPS * SWA_WINDOW
    qi = lax.broadcasted_iota(jnp.int32, (rows, 2 * SWA_WINDOW), 0) & (SWA_WINDOW - 1)
    kj = lax.broadcasted_iota(jnp.int32, (rows, 2 * SWA_WINDOW), 1)
    rel = SWA_WINDOW + qi - kj
    return (rel >= 0) & (rel < SWA_WINDOW) & ((n > 0) | (kj >= SWA_WINDOW))


def _swa_fwd(qT, kT, vT, sink_rows, *, name):
    S = qT.shape[1]
    W, G, Dh = SWA_WINDOW, SWA_GROUPS, SWA_HEAD_DIM
    nb = S // W
    scale = 1.0 / math.sqrt(Dh)

    def body(q_ref, kp_ref, kc_ref, vp_ref, vc_ref, s_ref, o_ref, l_ref):
        n = pl.program_id(1)
        q = q_ref[...].reshape(G * W, Dh)
        kk = jnp.concatenate([kp_ref[...], kc_ref[...]], axis=0)
        vv = jnp.concatenate([vp_ref[...], vc_ref[...]], axis=0)
        s = lax.dot_general(q, kk, (((1,), (1,)), ((), ())), preferred_element_type=F32) * scale
        s = jnp.where(_swa_mask(n), s, -1e30)
        sink = s_ref[...]
        m = jnp.maximum(jnp.max(s, axis=-1, keepdims=True), sink)
        e = jnp.exp(s - m)
        den = jnp.sum(e, axis=-1, keepdims=True) + jnp.exp(sink - m)
        p = e / den
        o_ref[...] = jnp.dot(p.astype(BF16), vv, preferred_element_type=F32).reshape(G, W, Dh)
        l_ref[...] = (m + jnp.log(den)).reshape(G, W, 1)

    qs = pl.BlockSpec((G, W, Dh), lambda g, n: (g, n, 0))
    prev = pl.BlockSpec((None, W, Dh), lambda g, n: (g, jnp.maximum(n - 1, 0), 0))
    cur = pl.BlockSpec((None, W, Dh), lambda g, n: (g, n, 0))
    return _call(
        body, name=name, grid=(SWA_KV_HEADS, nb),
        in_specs=[qs, prev, cur, prev, cur, pl.BlockSpec((None, G * W, 1), lambda g, n: (g, 0, 0))],
        out_specs=[qs, pl.BlockSpec((G, W, 1), lambda g, n: (g, n, 0))],
        out_shape=[jax.ShapeDtypeStruct((SWA_HEADS, S, Dh), F32), jax.ShapeDtypeStruct((SWA_HEADS, S, 1), F32)],
        compiler_params=_cparams(("parallel", "parallel")),
    )(qT, kT, kT, vT, vT, sink_rows)


def _swa_bwd(qT, kT, vT, sink_rows, oT, L, doT, *, name):
    S = qT.shape[1]
    W, G, Dh = SWA_WINDOW, SWA_GROUPS, SWA_HEAD_DIM
    nb = S // W
    scale = 1.0 / math.sqrt(Dh)

    def body(q_ref, kp_ref, kc_ref, vp_ref, vc_ref, s_ref, o_ref, l_ref, do_ref,
             dq_ref, dk_ref, dv_ref, ds_ref):
        n = pl.program_id(1)
        q = q_ref[...].reshape(G * W, Dh)
        kk = jnp.concatenate([kp_ref[...], kc_ref[...]], axis=0)
        vv = jnp.concatenate([vp_ref[...], vc_ref[...]], axis=0)
        s = lax.dot_general(q, kk, (((1,), (1,)), ((), ())), preferred_element_type=F32) * scale
        lrow = l_ref[...].reshape(G * W, 1)
        p = jnp.where(_swa_mask(n), jnp.exp(s - lrow), 0.0)
        do = do_ref[...].reshape(G * W, Dh)
        do_bf = do.astype(BF16)
        dp = lax.dot_general(do_bf, vv, (((1,), (1,)), ((), ())), preferred_element_type=F32)
        delta = jnp.sum(do * o_ref[...].reshape(G * W, Dh), axis=-1, keepdims=True)
        dsc = p * (dp - delta)
        ds_bf = dsc.astype(BF16)
        dq_ref[...] = (jnp.dot(ds_bf, kk, preferred_element_type=F32) * scale).astype(BF16).reshape(G, W, Dh)
        dkk = lax.dot_general(ds_bf, q, (((0,), (0,)), ((), ())), preferred_element_type=F32) * scale
        dvv = lax.dot_general(p.astype(BF16), do_bf, (((0,), (0,)), ((), ())), preferred_element_type=F32)
        dsk = -jnp.exp(s_ref[...] - lrow) * delta
        dsk = jnp.broadcast_to(jnp.sum(dsk.reshape(G, W, 1), axis=1), (G, LANE))

        @pl.when(n == 0)
        def _():
            dk_ref[...] = jnp.zeros_like(dk_ref)
            dv_ref[...] = jnp.zeros_like(dv_ref)
            ds_ref[...] = jnp.zeros_like(ds_ref)

        rows = pl.ds(pl.multiple_of(n * W, W), 2 * W)
        dk_ref[rows, :] += dkk
        dv_ref[rows, :] += dvv
        ds_ref[...] += dsk

    qs = pl.BlockSpec((G, W, Dh), lambda g, n: (g, n, 0))
    prev = pl.BlockSpec((None, W, Dh), lambda g, n: (g, jnp.maximum(n - 1, 0), 0))
    cur = pl.BlockSpec((None, W, Dh), lambda g, n: (g, n, 0))
    lsp = pl.BlockSpec((G, W, 1), lambda g, n: (g, n, 0))
    kvo = pl.BlockSpec((None, S + W, Dh), lambda g, n: (g, 0, 0))
    return _call(
        body, name=name, grid=(SWA_KV_HEADS, nb),
        in_specs=[qs, prev, cur, prev, cur, pl.BlockSpec((None, G * W, 1), lambda g, n: (g, 0, 0)), qs, lsp, qs],
        out_specs=[qs, kvo, kvo, pl.BlockSpec((None, G, LANE), lambda g, n: (g, 0, 0))],
        out_shape=[jax.ShapeDtypeStruct((SWA_HEADS, S, Dh), BF16),
                   jax.ShapeDtypeStruct((SWA_KV_HEADS, S + W, Dh), F32),
                   jax.ShapeDtypeStruct((SWA_KV_HEADS, S + W, Dh), F32),
                   jax.ShapeDtypeStruct((SWA_KV_HEADS, G, LANE), F32)],
        compiler_params=_cparams(("parallel", "arbitrary")),
    )(qT, kT, kT, vT, vT, sink_rows, oT, L, doT)


def _adamw(w, g, m, v, *, name, tr=128, by_cols=False):
    L, R, C = w.shape
    split = isinstance(g, (list, tuple))
    HR, HC = _half_shape(R, C, by_cols) if split else (R, C)
    tr, tc = _tile2d(HR, HC, tr)
    nr, nc = HR // tr, HC // tc
    c1 = 1.0 / (1.0 - ADAM_B1 ** ADAM_STEP)
    c2 = 1.0 / (1.0 - ADAM_B2 ** ADAM_STEP)
    ng = 2 * L if split else 1

    def body(c_ref, *refs):
        w_ref, g_refs, (m_ref, v_ref, go_ref, d_ref, mo_ref, vo_ref) = refs[0], refs[1:1 + ng], refs[1 + ng:]
        if split:
            mine = pl.program_id(1) == c_ref[0]
            g_ = jnp.where(mine, g_refs[0][...], g_refs[1][...])
            for l in range(1, L):
                g_ = jnp.where(pl.program_id(0) == l,
                               jnp.where(mine, g_refs[2 * l][...], g_refs[2 * l + 1][...]), g_)
        else:
            g_ = g_refs[0][...]
        mn = ADAM_B1 * m_ref[...] + (1.0 - ADAM_B1) * g_
        vn = ADAM_B2 * v_ref[...] + (1.0 - ADAM_B2) * (g_ * g_)
        go_ref[...] = g_
        mo_ref[...] = mn
        vo_ref[...] = vn
        d_ref[...] = -ADAM_LR * ((mn * c1) / (jnp.sqrt(vn * c2) + ADAM_EPS) + ADAM_WD * w_ref[...])

    def whole(l, hf, i, j, c):
        return (l, i, hf * nc + j) if by_cols else (l, hf * nr + i, j)

    def half(layer, own):
        def index(l, hf, i, j, c):
            used = (l == layer) & ((hf == c[0]) if own else (hf != c[0]))
            return jnp.where(used, i, 0), jnp.where(used, j, 0)
        return pl.BlockSpec((tr, tc), index)

    row = pl.BlockSpec((None, tr, tc), whole)
    gs = [h for pair in g for h in pair] if split else [g]
    g_specs = [half(l, own) for l in range(L) for own in (True, False)] if split else [row]
    core = lax.axis_index("c").astype(jnp.int32).reshape(1)
    return _call(
        body, name=name,
        grid_spec=pltpu.PrefetchScalarGridSpec(
            num_scalar_prefetch=1, grid=(L, 2 if split else 1, nr, nc),
            in_specs=[row] + g_specs + [row, row], out_specs=[row] * 4),
        out_shape=[jax.ShapeDtypeStruct((L, R, C), F32)] * 4,
        compiler_params=_cparams(("parallel",) * 4),
    )(core, w, *gs, m, v)


def _adamw_half(w, g, m, v, *, name, own, prev=None, tr=128, by_cols=False):
    L, R, C = w.shape
    HR, HC = _half_shape(R, C, by_cols)
    tr, tc = _tile2d(HR, HC, tr)
    nr, nc = HR // tr, HC // tc
    c1 = 1.0 / (1.0 - ADAM_B1 ** ADAM_STEP)
    c2 = 1.0 / (1.0 - ADAM_B2 ** ADAM_STEP)

    def body(c_ref, *refs):
        w_ref, g_refs, m_ref, v_ref = refs[0], refs[1:1 + L], refs[1 + L], refs[2 + L]
        go_ref, d_ref, mo_ref, vo_ref = refs[-4:]
        g_ = g_refs[0][...]
        for l in range(1, L):
            g_ = jnp.where(pl.program_id(0) == l, g_refs[l][...], g_)
        mn = ADAM_B1 * m_ref[...] + (1.0 - ADAM_B1) * g_
        vn = ADAM_B2 * v_ref[...] + (1.0 - ADAM_B2) * (g_ * g_)
        go_ref[...] = g_
        mo_ref[...] = mn
        vo_ref[...] = vn
        d_ref[...] = -ADAM_LR * ((mn * c1) / (jnp.sqrt(vn * c2) + ADAM_EPS) + ADAM_WD * w_ref[...])

    def whole(l, i, j, c):
        hf = c[0] if own else 1 - c[0]
        return (l, i, hf * nc + j) if by_cols else (l, hf * nr + i, j)

    def layer_half(layer):
        def index(l, i, j, c):
            return jnp.where(l == layer, i, 0), jnp.where(l == layer, j, 0)
        return pl.BlockSpec((tr, tc), index)

    row = pl.BlockSpec((None, tr, tc), whole)
    core = lax.axis_index("c").astype(jnp.int32).reshape(1)
    prev = list(prev) if prev is not None else []
    return _call(
        body, name=name,
        grid_spec=pltpu.PrefetchScalarGridSpec(
            num_scalar_prefetch=1, grid=(L, nr, nc),
            in_specs=[row] + [layer_half(l) for l in range(L)] + [row, row] + [ANY] * len(prev),
            out_specs=[row] * 4),
        out_shape=[jax.ShapeDtypeStruct((L, R, C), F32)] * 4,
        input_output_aliases={4 + L + k: k for k in range(len(prev))},
        compiler_params=_cparams(("parallel",) * 3),
    )(core, w, *g, m, v, *prev)


def _sum2_halves(g4, s4, by_cols, *, name):
    n, R, C = g4.shape
    HR, HC = _half_shape(R, C, by_cols)
    tr, tc = _tile2d(HR, HC, budget=1024 * 1024)
    nr, nc = HR // tr, HC // tc
    core = lax.axis_index("c").astype(jnp.int32).reshape(1)

    def body(c_ref, g_ref, s_ref, o_ref):
        o_ref[...] = (g_ref[...].astype(F32) + s_ref[...].astype(F32)).astype(BF16)

    def mine(k, i, j, c):
        return (k, i, c[0] * nc + j) if by_cols else (k, c[0] * nr + i, j)

    blk = pl.BlockSpec((None, tr, tc), lambda k, i, j, c: (k, i, j))
    return _call(
        body, name=name,
        grid_spec=pltpu.PrefetchScalarGridSpec(
            num_scalar_prefetch=1, grid=(n, nr, nc),
            in_specs=[pl.BlockSpec((None, tr, tc), mine), blk], out_specs=blk),
        out_shape=jax.ShapeDtypeStruct((n, HR, HC), BF16),
        compiler_params=_cparams(("parallel", "parallel", "parallel")),
    )(core, g4, s4)


def _rowsum(parts, *, name, out_dtype=F32):
    n, R, C = parts.shape
    tr, tc = _tile2d(R, C, budget=512 * 1024)

    def body(p_ref, o_ref):
        acc = p_ref[0].astype(F32)
        for i in range(1, n):
            acc = acc + p_ref[i].astype(F32)
        o_ref[...] = acc.astype(out_dtype)

    return _call(
        body, name=name, grid=(R // tr, C // tc),
        in_specs=[pl.BlockSpec((n, tr, tc), lambda i, j: (0, i, j))],
        out_specs=pl.BlockSpec((tr, tc), lambda i, j: (i, j)),
        out_shape=jax.ShapeDtypeStruct((R, C), out_dtype),
        compiler_params=_cparams(("parallel", "parallel")),
    )(parts)


def _where_am_i():
    x, y, c = lax.axis_index("x"), lax.axis_index("y"), lax.axis_index("c")
    chips = [(1 - x, y), (x, 1 - y), (1 - x, 1 - y)]
    return x, y, c, chips


def _half_idx(rows, cols, by_cols, which):
    if by_cols:
        hc = cols // 2
        return (slice(None), pl.ds(pl.multiple_of(which * hc, LANE), hc))
    hr = rows // 2
    return (pl.ds(pl.multiple_of(which * hr, 16), hr), slice(None))


def _half_shape(rows, cols, by_cols):
    return (rows, cols // 2) if by_cols else (rows // 2, cols)


HBM_SPEC = pl.BlockSpec(memory_space=pltpu.HBM)
SEM_SPEC = pl.BlockSpec(memory_space=pltpu.SEMAPHORE)
DATAFLOW = pltpu.SideEffectType.DATAFLOW_SIDE_EFFECTING


def _chip_exchange_refs(kind, shards_shape, by_cols, src, land, i, chip_k, c, me):
    if kind == 'gather':
        half = _half_idx(*shards_shape, by_cols, c)
        return src.at[half], land.at[(me,) + half], land.at[(chip_k,) + half]
    return src.at[chip_k], land.at[me], land.at[chip_k]


def _chip_exchange_start(kind, srcs, by_cols, *, name, after=()):
    n = len(srcs)
    land_shapes = [((N_CHIPS,) + s.shape) if kind == 'gather' else s.shape for s in srcs]

    def body(*refs):
        src_refs, land_refs = refs[:n], refs[n:2 * n]
        send, recv = refs[2 * n + len(after)], refs[2 * n + len(after) + 1]
        token = refs[-1]
        x, y, c, chips = _where_am_i()
        me = 2 * x + y
        for i in range(n):
            for k, (px, py) in enumerate(chips):
                s, d, _ = _chip_exchange_refs(kind, srcs[i].shape, by_cols[i], src_refs[i], land_refs[i], i,
                                              2 * px + py, c, me)
                pltpu.make_async_remote_copy(src_ref=s, dst_ref=d, send_sem=send.at[3 * i + k],
                                             recv_sem=recv.at[3 * i + k], device_id=(px, py, c),
                                             device_id_type=MESH).start()
        token[...] = jnp.zeros_like(token)

    lands = [pltpu.with_memory_space_constraint(lax.empty(sh, s.dtype), pltpu.HBM) for sh, s in zip(land_shapes, srcs)]
    outs = _call(
        body, name=name,
        out_shape=(pltpu.SemaphoreType.DMA((3 * n,)), pltpu.SemaphoreType.DMA((3 * n,)),
                   *[pltpu.HBM(s.shape, s.dtype) for s in srcs],
                   *[pltpu.HBM(sh, s.dtype) for sh, s in zip(land_shapes, srcs)],
                   jax.ShapeDtypeStruct((8, LANE), F32)),
        in_specs=[HBM_SPEC] * (2 * n) + [ANY] * len(after),
        out_specs=(SEM_SPEC, SEM_SPEC, *([HBM_SPEC] * (2 * n)), pl.BlockSpec(memory_space=pltpu.VMEM)),
        input_output_aliases={j: 2 + j for j in range(2 * n)},
        compiler_params=pltpu.CompilerParams(has_side_effects=DATAFLOW),
    )(*[pltpu.with_memory_space_constraint(s, pltpu.HBM) for s in srcs], *lands, *after)
    return outs[0], outs[1], list(outs[2:2 + n]), list(outs[2 + n:2 + 2 * n]), outs[-1]


def _chip_exchange_wait(kind, send, recv, srcs, lands, by_cols, after, *, name):
    n = len(srcs)

    def body(*refs):
        src_refs, land_refs = refs[:n], refs[n:2 * n]
        send_r, recv_r = refs[2 * n], refs[2 * n + 1]
        x, y, c, chips = _where_am_i()
        me = 2 * x + y
        for i in range(n):
            for k, (px, py) in enumerate(chips):
                s, _, d = _chip_exchange_refs(kind, srcs[i].shape, by_cols[i], src_refs[i], land_refs[i], i,
                                              2 * px + py, c, me)
                cp = pltpu.make_async_remote_copy(src_ref=s, dst_ref=d, send_sem=send_r.at[3 * i + k],
                                                  recv_sem=recv_r.at[3 * i + k], device_id=(px, py, c),
                                                  device_id_type=MESH)
                cp.wait_send()
                cp.wait_recv()

    outs = _call(
        body, name=name,
        out_shape=(*[pltpu.HBM(s.shape, s.dtype) for s in srcs], *[pltpu.HBM(l.shape, l.dtype) for l in lands]),
        in_specs=[HBM_SPEC] * (2 * n) + [SEM_SPEC, SEM_SPEC] + [ANY] * len(after),
        out_specs=tuple([HBM_SPEC] * (2 * n)),
        input_output_aliases={j: j for j in range(2 * n)},
        compiler_params=pltpu.CompilerParams(has_side_effects=DATAFLOW),
    )(*srcs, *lands, send, recv, *after)
    return list(outs[:n]), list(outs[n:])


def _sibling_halves_start(grads, by_cols, *, name, after=()):
    n = len(grads)
    land_shapes = [(N_CHIPS,) + _half_shape(*g.shape[1:], bc) for g, bc in zip(grads, by_cols)]

    def body(*refs):
        src_refs, land_refs = refs[:n], refs[n:2 * n]
        send, recv = refs[2 * n + len(after)], refs[2 * n + len(after) + 1]
        token = refs[-1]
        x, y, c, _ = _where_am_i()
        for i in range(n):
            src = src_refs[i].at[(slice(None),) + _half_idx(*grads[i].shape[1:], by_cols[i], 1 - c)]
            pltpu.make_async_remote_copy(src_ref=src, dst_ref=land_refs[i], send_sem=send.at[i], recv_sem=recv.at[i],
                                         device_id=(x, y, 1 - c), device_id_type=MESH).start()
        token[...] = jnp.zeros_like(token)

    lands = [pltpu.with_memory_space_constraint(lax.empty(sh, g.dtype), pltpu.HBM) for sh, g in zip(land_shapes, grads)]
    outs = _call(
        body, name=name,
        out_shape=(pltpu.SemaphoreType.DMA((n,)), pltpu.SemaphoreType.DMA((n,)),
                   *[pltpu.HBM(g.shape, g.dtype) for g in grads],
                   *[pltpu.HBM(sh, g.dtype) for sh, g in zip(land_shapes, grads)],
                   jax.ShapeDtypeStruct((8, LANE), F32)),
        in_specs=[HBM_SPEC] * (2 * n) + [ANY] * len(after),
        out_specs=(SEM_SPEC, SEM_SPEC, *([HBM_SPEC] * (2 * n)), pl.BlockSpec(memory_space=pltpu.VMEM)),
        input_output_aliases={j: 2 + j for j in range(2 * n)},
        compiler_params=pltpu.CompilerParams(has_side_effects=DATAFLOW),
    )(*[pltpu.with_memory_space_constraint(g, pltpu.HBM) for g in grads], *lands, *after)
    return outs[0], outs[1], list(outs[2:2 + n]), list(outs[2 + n:2 + 2 * n]), outs[-1]


def _sibling_halves_wait(send, recv, grads, lands, by_cols, after, *, name):
    n = len(grads)

    def body(*refs):
        src_refs, land_refs = refs[:n], refs[n:2 * n]
        send_r, recv_r = refs[2 * n], refs[2 * n + 1]
        x, y, c, _ = _where_am_i()
        for i in range(n):
            src = src_refs[i].at[(slice(None),) + _half_idx(*grads[i].shape[1:], by_cols[i], 1 - c)]
            cp = pltpu.make_async_remote_copy(src_ref=src, dst_ref=land_refs[i], send_sem=send_r.at[i],
                                              recv_sem=recv_r.at[i], device_id=(x, y, 1 - c), device_id_type=MESH)
            cp.wait_send()
            cp.wait_recv()

    outs = _call(
        body, name=name,
        out_shape=(*[pltpu.HBM(g.shape, g.dtype) for g in grads], *[pltpu.HBM(l.shape, l.dtype) for l in lands]),
        in_specs=[HBM_SPEC] * (2 * n) + [SEM_SPEC, SEM_SPEC] + [ANY] * len(after),
        out_specs=tuple([HBM_SPEC] * (2 * n)),
        input_output_aliases={j: j for j in range(2 * n)},
        compiler_params=pltpu.CompilerParams(has_side_effects=DATAFLOW),
    )(*grads, *lands, send, recv, *after)
    return list(outs[:n]), list(outs[n:])


def _sibling_swap_start(arrs, *, name, after=()):
    n = len(arrs)

    def body(*refs):
        src_refs, land_refs = refs[:n], refs[n:2 * n]
        send, recv = refs[2 * n + len(after)], refs[2 * n + len(after) + 1]
        token = refs[-1]
        x, y, c, _ = _where_am_i()
        for i in range(n):
            pltpu.make_async_remote_copy(src_ref=src_refs[i], dst_ref=land_refs[i], send_sem=send.at[i],
                                         recv_sem=recv.at[i], device_id=(x, y, 1 - c), device_id_type=MESH).start()
        token[...] = jnp.zeros_like(token)

    lands = [pltpu.with_memory_space_constraint(lax.empty(a.shape, a.dtype), pltpu.HBM) for a in arrs]
    outs = _call(
        body, name=name,
        out_shape=(pltpu.SemaphoreType.DMA((n,)), pltpu.SemaphoreType.DMA((n,)),
                   *[pltpu.HBM(a.shape, a.dtype) for a in arrs] * 2, jax.ShapeDtypeStruct((8, LANE), F32)),
        in_specs=[HBM_SPEC] * (2 * n) + [ANY] * len(after),
        out_specs=(SEM_SPEC, SEM_SPEC, *([HBM_SPEC] * (2 * n)), pl.BlockSpec(memory_space=pltpu.VMEM)),
        input_output_aliases={j: 2 + j for j in range(2 * n)},
        compiler_params=pltpu.CompilerParams(has_side_effects=DATAFLOW),
    )(*[pltpu.with_memory_space_constraint(a, pltpu.HBM) for a in arrs], *lands, *after)
    return outs[0], outs[1], list(outs[2:2 + n]), list(outs[2 + n:2 + 2 * n]), outs[-1]


def _sibling_swap_wait(send, recv, arrs, lands, after, *, name):
    n = len(arrs)

    def body(*refs):
        src_refs, land_refs = refs[:n], refs[n:2 * n]
        send_r, recv_r = refs[2 * n], refs[2 * n + 1]
        x, y, c, _ = _where_am_i()
        for i in range(n):
            cp = pltpu.make_async_remote_copy(src_ref=src_refs[i], dst_ref=land_refs[i], send_sem=send_r.at[i],
                                              recv_sem=recv_r.at[i], device_id=(x, y, 1 - c), device_id_type=MESH)
            cp.wait_send()
            cp.wait_recv()

    outs = _call(
        body, name=name,
        out_shape=tuple(pltpu.HBM(a.shape, a.dtype) for a in list(arrs) + list(lands)),
        in_specs=[HBM_SPEC] * (2 * n) + [SEM_SPEC, SEM_SPEC] + [ANY] * len(after),
        out_specs=tuple([HBM_SPEC] * (2 * n)),
        input_output_aliases={j: j for j in range(2 * n)},
        compiler_params=pltpu.CompilerParams(has_side_effects=DATAFLOW),
    )(*arrs, *lands, send, recv, *after)
    return list(outs[:n]), list(outs[n:])


def _sibling_pass_gathered(lands, shard_shapes, by_cols, *, name):
    n = len(lands)

    def body(*refs):
        outs = refs[n:2 * n]
        send, recv = refs[2 * n:]
        x, y, c, chips = _where_am_i()
        sibling = (x, y, 1 - c)
        cps = []
        for i in range(n):
            for k, (px, py) in enumerate(chips):
                blk = outs[i].at[(2 * px + py,) + _half_idx(*shard_shapes[i], by_cols[i], c)]
                d = pltpu.make_async_remote_copy(src_ref=blk, dst_ref=blk, send_sem=send.at[i, k],
                                                 recv_sem=recv.at[i, k], device_id=sibling, device_id_type=MESH)
                d.start()
                cps.append(d)
        for i in range(n):
            for k, (px, py) in enumerate(chips):
                blk = outs[i].at[(2 * px + py,) + _half_idx(*shard_shapes[i], by_cols[i], 1 - c)]
                pltpu.make_async_remote_copy(src_ref=blk, dst_ref=blk, send_sem=send.at[i, k], recv_sem=recv.at[i, k],
                                             device_id=sibling, device_id_type=MESH).wait_recv()
        for d in cps:
            d.wait_send()

    return _call(
        body, name=name, in_specs=[ANY] * n, out_specs=[ANY] * n,
        out_shape=[jax.ShapeDtypeStruct(l.shape, l.dtype) for l in lands],
        input_output_aliases={j: j for j in range(n)},
        scratch_shapes=[pltpu.SemaphoreType.DMA((n, 3)), pltpu.SemaphoreType.DMA((n, 3))],
    )(*lands)


def _own_slot(lands, owns):
    me = 2 * lax.axis_index("x") + lax.axis_index("y")
    return [lax.dynamic_update_slice_in_dim(g, s, me, axis=0) for g, s in zip(lands, owns)]


def _sibling_send_halves(grads, by_cols, *, name):
    n = len(grads)

    def body(*refs):
        ins, outs = refs[:n], refs[n:2 * n]
        send, recv = refs[2 * n:]
        x, y, c, _ = _where_am_i()
        sibling = (x, y, 1 - c)
        cps = []
        for i in range(n):
            src = ins[i].at[(slice(None),) + _half_idx(*grads[i].shape[1:], by_cols[i], 1 - c)]
            d = pltpu.make_async_remote_copy(src_ref=src, dst_ref=outs[i], send_sem=send.at[i],
                                             recv_sem=recv.at[i], device_id=sibling, device_id_type=MESH)
            d.start()
            cps.append(d)
        for d in cps:
            d.wait()

    return _call(
        body, name=name, in_specs=[ANY] * n, out_specs=[ANY] * n,
        out_shape=[jax.ShapeDtypeStruct((N_CHIPS,) + _half_shape(*g.shape[1:], bc), g.dtype)
                   for g, bc in zip(grads, by_cols)],
        scratch_shapes=[pltpu.SemaphoreType.DMA((n,)), pltpu.SemaphoreType.DMA((n,))],
    )(*grads)


def _all_reduce_small(v, *, name, after=()):
    R, C = v.shape
    H = R // 2

    def body(v_ref, o_ref, sib, slots, send, recv):
        x, y, c, chips = _where_am_i()
        me = 2 * x + y
        sibling = (x, y, 1 - c)
        mine = pl.ds(pl.multiple_of(c * H, 8), H)
        other = pl.ds(pl.multiple_of((1 - c) * H, 8), H)

        def copy(k, src, dst, to):
            return pltpu.make_async_remote_copy(src_ref=src, dst_ref=dst, send_sem=send.at[k], recv_sem=recv.at[k],
                                                device_id=to, device_id_type=MESH)

        d = copy(0, v_ref.at[other], sib, sibling)
        d.start()
        d.wait()
        slots[me] = v_ref[mine, :] + sib[...]
        cps = [copy(1 + k, slots.at[me], slots.at[me], (px, py, c)) for k, (px, py) in enumerate(chips)]
        for d in cps:
            d.start()
        for k, (px, py) in enumerate(chips):
            blk = slots.at[2 * px + py]
            copy(1 + k, blk, blk, (px, py, c)).wait_recv()
        for d in cps:
            d.wait_send()
        o_ref[mine, :] = (slots[0] + slots[1]) + (slots[2] + slots[3])
        d = copy(4, o_ref.at[mine], o_ref.at[mine], sibling)
        d.start()
        copy(4, o_ref.at[other], o_ref.at[other], sibling).wait_recv()
        d.wait_send()

    vm = pl.BlockSpec(memory_space=pltpu.VMEM)
    return _call(
        body, after=after, name=name, in_specs=[vm], out_specs=vm,
        out_shape=jax.ShapeDtypeStruct((R, C), F32),
        scratch_shapes=[pltpu.VMEM((H, C), F32), pltpu.VMEM((N_CHIPS, H, C), F32),
                        pltpu.SemaphoreType.DMA((5,)), pltpu.SemaphoreType.DMA((5,))],
        compiler_params=pltpu.CompilerParams(vmem_limit_bytes=VMEM_LIMIT),
    )(v)


def _cols_from_shards(g):
    return jnp.transpose(g, (1, 0, 2)).reshape(g.shape[1], -1)


def _shards_from_cols(w):
    R, C4 = w.shape
    return jnp.transpose(w.reshape(R, N_CHIPS, C4 // N_CHIPS), (1, 0, 2))


def _pack(arrs):
    flat = []
    for a in arrs:
        f = a.reshape(-1).astype(F32)
        flat.append(jnp.pad(f, (0, _rup(f.shape[0], LANE) - f.shape[0])))
    v = jnp.concatenate(flat)
    rows = _rup(v.shape[0] // LANE, 16)
    v = jnp.pad(v, (0, rows * LANE - v.shape[0]))
    return v.reshape(rows, LANE)


def _unpack(v, shapes):
    flat = v.reshape(-1)
    out, off = [], 0
    for s in shapes:
        n = int(np.prod(s))
        out.append(flat[off:off + n].reshape(s))
        off += _rup(n, LANE)
    return out


def _ffn_fwd(x, Wup, Wdn, cw, cb, tag):
    h = _mm(x, Wup, 'nt', bmode='bo', tm=512, tn=4096, name=f"ffn_up_{tag}")
    a, hc = _act_fwd(h, cw, cb, name=f"ffn_act_{tag}")
    f = _mm(a, Wdn, 'nn', bmode='abr', tm=512, tn=1024, tk=4096, name=f"ffn_down_{tag}")
    return f, (h, hc), a


def _ffn_bwd(df, x, saved, a, Wup, Wdn, cw, tag):
    h, hc = saved
    da = _mm(df, Wdn, 'nt', bmode='bo', tm=512, tn=4096, name=f"ffn_da_{tag}")
    dWdn = _mm(a, df, 'tn', bmode='ao', tm=4096, tn=512, name=f"ffn_dwdn_{tag}", out_dtype=BF16)
    dh, dcw, dcb = _act_bwd(h, hc, da, cw, name=f"ffn_actb_{tag}")

    def shard_of(k):
        return (k % 2) * 2 + k // 2

    dx = _mm(dh, Wup, 'nn', bmode='abr', tm=512, tn=1024, tk=4096, name=f"ffn_dx_{tag}", b_map=shard_of)
    dWup = _mm(dh, x, 'tn', bmode='ao', tm=4096, tn=512, name=f"ffn_dwup_{tag}", out_dtype=BF16,
               o_map=shard_of)
    return dx, dWup, dWdn, dcw, dcb


def kernel(x, positions, ev_w_in, ev_b_f, ev_lambda_re, ev_lambda_im, ev_log_step, ev_ssm_b_re, ev_ssm_b_im, ev_ssm_c_re, ev_ssm_c_im, ev_ssm_d, ev_w_glu, ev_w_out, od_w_in, od_sinks, od_w_out, ln_mix_g, ln_mix_b, ffn_w_up, ffn_conv_w, ffn_conv_b, ffn_w_down, ln_ffn_g, ln_ffn_b, loss_target, m_ev_w_in, m_ev_b_f, m_ev_lambda_re, m_ev_lambda_im, m_ev_log_step, m_ev_ssm_b_re, m_ev_ssm_b_im, m_ev_ssm_c_re, m_ev_ssm_c_im, m_ev_ssm_d, m_ev_w_glu, m_ev_w_out, m_od_w_in, m_od_sinks, m_od_w_out, m_ln_mix_g, m_ln_mix_b, m_ffn_w_up, m_ffn_conv_w, m_ffn_conv_b, m_ffn_w_down, m_ln_ffn_g, m_ln_ffn_b, v_ev_w_in, v_ev_b_f, v_ev_lambda_re, v_ev_lambda_im, v_ev_log_step, v_ev_ssm_b_re, v_ev_ssm_b_im, v_ev_ssm_c_re, v_ev_ssm_c_im, v_ev_ssm_d, v_ev_w_glu, v_ev_w_out, v_od_w_in, v_od_sinks, v_od_w_out, v_ln_mix_g, v_ln_mix_b, v_ffn_w_up, v_ffn_conv_w, v_ffn_conv_b, v_ffn_w_down, v_ln_ffn_g, v_ln_ffn_b):
    W = dict(ev_w_in=ev_w_in, ev_b_f=ev_b_f, ev_lambda_re=ev_lambda_re, ev_lambda_im=ev_lambda_im, ev_log_step=ev_log_step, ev_ssm_b_re=ev_ssm_b_re, ev_ssm_b_im=ev_ssm_b_im, ev_ssm_c_re=ev_ssm_c_re, ev_ssm_c_im=ev_ssm_c_im, ev_ssm_d=ev_ssm_d, ev_w_glu=ev_w_glu, ev_w_out=ev_w_out, od_w_in=od_w_in, od_sinks=od_sinks, od_w_out=od_w_out, ln_mix_g=ln_mix_g, ln_mix_b=ln_mix_b, ffn_w_up=ffn_w_up, ffn_conv_w=ffn_conv_w, ffn_conv_b=ffn_conv_b, ffn_w_down=ffn_w_down, ln_ffn_g=ln_ffn_g, ln_ffn_b=ln_ffn_b)
    Mo = dict(ev_w_in=m_ev_w_in, ev_b_f=m_ev_b_f, ev_lambda_re=m_ev_lambda_re, ev_lambda_im=m_ev_lambda_im, ev_log_step=m_ev_log_step, ev_ssm_b_re=m_ev_ssm_b_re, ev_ssm_b_im=m_ev_ssm_b_im, ev_ssm_c_re=m_ev_ssm_c_re, ev_ssm_c_im=m_ev_ssm_c_im, ev_ssm_d=m_ev_ssm_d, ev_w_glu=m_ev_w_glu, ev_w_out=m_ev_w_out, od_w_in=m_od_w_in, od_sinks=m_od_sinks, od_w_out=m_od_w_out, ln_mix_g=m_ln_mix_g, ln_mix_b=m_ln_mix_b, ffn_w_up=m_ffn_w_up, ffn_conv_w=m_ffn_conv_w, ffn_conv_b=m_ffn_conv_b, ffn_w_down=m_ffn_w_down, ln_ffn_g=m_ln_ffn_g, ln_ffn_b=m_ln_ffn_b)
    Vo = dict(ev_w_in=v_ev_w_in, ev_b_f=v_ev_b_f, ev_lambda_re=v_ev_lambda_re, ev_lambda_im=v_ev_lambda_im, ev_log_step=v_ev_log_step, ev_ssm_b_re=v_ev_ssm_b_re, ev_ssm_b_im=v_ev_ssm_b_im, ev_ssm_c_re=v_ev_ssm_c_re, ev_ssm_c_im=v_ev_ssm_c_im, ev_ssm_d=v_ev_ssm_d, ev_w_glu=v_ev_w_glu, ev_w_out=v_ev_w_out, od_w_in=v_od_w_in, od_sinks=v_od_sinks, od_w_out=v_od_w_out, ln_mix_g=v_ln_mix_g, ln_mix_b=v_ln_mix_b, ffn_w_up=v_ffn_w_up, ffn_conv_w=v_ffn_conv_w, ffn_conv_b=v_ffn_conv_b, ffn_w_down=v_ffn_w_down, ln_ffn_g=v_ln_ffn_g, ln_ffn_b=v_ln_ffn_b)
    names = list(W.keys())
    big = ['ev_w_in', 'ev_w_glu', 'ev_w_out', 'od_w_in', 'od_w_out', 'ffn_w_up', 'ffn_w_down']

    S, D = x.shape[1], x.shape[2]
    x0 = x.reshape(S, D)
    tgt = loss_target.reshape(S, D)
    G, Pn, Cg = SSM_GROUPS, SSM_STATE, SSM_GROUP
    Fs = ffn_w_up.shape[2]
    FP = Fs
    Rd = ffn_w_down.shape[1]
    EIN = N_CHIPS * ev_w_in.shape[2]

    cwl = ffn_conv_w.reshape(-1)
    cw_rows = _rup(_rup(cwl.shape[0], LANE) // LANE, 32)
    cw_pad = jnp.pad(cwl, (0, cw_rows * LANE - cwl.shape[0])).reshape(cw_rows, LANE)
    transposed = ('ev_w_in', 'ffn_w_up')

    def view(n, a):
        return jnp.transpose(a, (0, 2, 1)) if n in transposed else a

    Wv = {n: view(n, W[n]) for n in big}
    big_e = [(n, l) for n in big for l in range(W[n].shape[0])]
    split_cols = {e: (Wv[e[0]].shape[1] // 2) % 16 != 0 for e in big_e}
    shard16 = {e: Wv[e[0]][e[1]].astype(BF16) for e in big_e}
    grp_now = [e for e in big_e if e[0].startswith('ev_')]
    grp_ffn0 = [('ffn_w_up', 0), ('ffn_w_down', 0)]
    grp_l1 = [('od_w_in', 0), ('od_w_out', 0), ('ffn_w_up', 1), ('ffn_w_down', 1)]
    src_now = [shard16[e] for e in grp_now]
    src_ffn0 = [shard16[e] for e in grp_ffn0] + [cw_pad]
    src_l1 = [shard16[e] for e in grp_l1]
    cols_now = [split_cols[e] for e in grp_now]
    cols_ffn0 = [split_cols[e] for e in grp_ffn0] + [False]
    cols_l1 = [split_cols[e] for e in grp_l1]
    ag_in = _chip_exchange_start('gather', src_now[:1], cols_now[:1], name="ag_in_start")
    ag_mix = _chip_exchange_start('gather', src_now[1:], cols_now[1:], name="ag_mix_start", after=[ag_in[4]])
    ag_ffn0 = _chip_exchange_start('gather', src_ffn0, cols_ffn0, name="ag_ffn0_start", after=[ag_mix[4]])
    ag_l1 = _chip_exchange_start('gather', src_l1, cols_l1, name="ag_l1_start", after=[ag_ffn0[4]])
    started = [ag_l1[4]]

    def finish_gather(started, srcs, cols, after, tag):
        send, recv, thru, lands, _ = started
        thru, lands = _chip_exchange_wait('gather', send, recv, thru, lands, cols, after, name=f"ag_{tag}_wait")
        lands = _sibling_pass_gathered(lands, [s.shape for s in srcs], cols, name=f"ag_{tag}_pass")
        return _own_slot(lands, [s[None] for s in thru])

    lam_r, lam_i = ev_lambda_re[0], ev_lambda_im[0]
    lstep = ev_log_step[0].reshape(G, 1)
    a_re, a_im, g_re, g_im = _s5_disc_fwd(lam_r, lam_i, lstep, name="s5_disc", after=started)
    b_re2, b_im2 = ev_ssm_b_re[0].reshape(G * Pn, Cg), ev_ssm_b_im[0].reshape(G * Pn, Cg)
    g_re1, g_im1 = g_re.reshape(G * Pn, 1), g_im.reshape(G * Pn, 1)
    bb_re, bb_im = _s5_bb_fwd(g_re1, g_im1, b_re2, b_im2, name="s5_bb")
    bbt = jnp.stack([jnp.transpose(b.reshape(G, Pn, Cg), (0, 2, 1)).reshape(G * Cg, Pn) for b in (bb_re, bb_im)])
    BB = _diag_expand(bbt, Cg, Pn, name="s5_bb_dense")
    cct = jnp.stack([jnp.transpose(ev_ssm_c_re[0], (0, 2, 1)).reshape(G * Pn, Cg),
                     jnp.transpose(-ev_ssm_c_im[0], (0, 2, 1)).reshape(G * Pn, Cg)])
    CC = _diag_expand(cct, Pn, Cg, name="s5_cc_dense", after=started)
    a_cat = jnp.stack([a_re.reshape(1, G * Pn), a_im.reshape(1, G * Pn)])
    dskip = ev_ssm_d[0].reshape(1, SSM_WIDTH)
    tabs = _rope_tables(positions.reshape(S, 1).astype(F32), name="rope_tables", after=[BB, CC])

    gw = dict(zip(grp_now[:1], finish_gather(ag_in, src_now[:1], cols_now[:1], [tabs[2]], "in")))
    qkv_w = 3 * FOX_WIDTH
    ESH = EIN // N_CHIPS

    def rows_of(arr4, lo, hi):
        out = []
        for k in range(N_CHIPS):
            a, b = max(lo, k * ESH), min(hi, (k + 1) * ESH)
            if a < b:
                out.append(arr4[k, a - k * ESH:b - k * ESH])
        return out

    w_in4 = gw[('ev_w_in', 0)]
    WmainT = jnp.concatenate(rows_of(w_in4, 0, qkv_w) + rows_of(w_in4, qkv_w + FOX_HEADS, EIN), axis=0)
    WfT = jnp.pad(jnp.concatenate(rows_of(w_in4, qkv_w, qkv_w + FOX_HEADS), axis=0),
                  ((0, LANE - FOX_HEADS), (0, 0)))
    cbs = [ffn_conv_b[l].reshape(N_CHIPS, Fs) for l in range(DEPTH)]

    P = _mm(x0, WmainT, 'nt', name="ev_proj")
    fl = _mm(x0, WfT, 'nt', name="ev_proj_f")
    bf_pad = jnp.pad(ev_b_f.reshape(1, FOX_HEADS), ((0, 0), (0, LANE - FOX_HEADS)))
    cgate, sgate = _gate_fwd(fl, bf_pad, name="fox_gate")
    ccol = jnp.transpose(cgate[:, :FOX_HEADS]).reshape(FOX_HEADS, S, 1)
    crow = jnp.transpose(cgate[:, :FOX_HEADS]).reshape(FOX_HEADS, 1, S)
    fox, lse = _fox_fwd(P, ccol, crow, name="fox_fwd")
    u_s5 = P[:, qkv_w:]
    UT, HT = _DIAG_TILE * Cg, _DIAG_TILE * Pn
    bu = _mm(u_s5, BB, 'nn', bmode='bo', tm=2048, tn=HT, tk=UT, diag='kn', name="s5_bu")
    hh = _s5_scan_fwd(bu, a_cat, name="s5_scan")
    yc = _mm(hh, CC, 'nn', bmode='abr', tm=2048, tn=UT, tk=HT, diag='kn', name="s5_y")
    y_s5, yg = _s5_out_fwd(yc, P, dskip, name="s5_out")
    gw.update(zip(grp_now[1:], finish_gather(ag_mix, src_now[1:], cols_now[1:], [yg], "mix")))
    Wglu = _cols_from_shards(gw[('ev_w_glu', 0)])
    Wout_ev = gw[('ev_w_out', 0)].reshape(D, D)
    z = _mm(yg, Wglu, 'nn', name="s5_glu_proj")
    ssm = _glu_fwd(z, name="s5_glu")
    cat = jnp.concatenate([fox.astype(BF16), ssm], axis=1)
    mix0 = _mm(cat, Wout_ev, 'nn', name="ev_out")
    x1, xh1, rs1 = _add_ln_fwd(x0, mix0, ln_mix_g[0], ln_mix_b[0], name="ln_mix0")
    got = finish_gather(ag_ffn0, src_ffn0, cols_ffn0, [x1], "ffn0")
    gw.update(zip(grp_ffn0, got[:-1]))
    cw_all = got[-1].reshape(N_CHIPS, -1)[:, :cwl.shape[0]].reshape(N_CHIPS, DEPTH, 3, Fs)
    cws = [cw_all[:, l] for l in range(DEPTH)]
    Wup = {0: gw[('ffn_w_up', 0)]}
    Wdn = {0: gw[('ffn_w_down', 0)].reshape(2, Fs, D)}
    f0, hf0, af0 = _ffn_fwd(x1, Wup[0], Wdn[0], cws[0], cbs[0], "l0")
    x2, xh2, rs2 = _add_ln_fwd(x1, f0, ln_ffn_g[0], ln_ffn_b[0], name="ln_ffn0")

    gw.update(zip(grp_l1, finish_gather(ag_l1, src_l1, cols_l1, [x2], "l1")))
    Wodin = _cols_from_shards(gw[('od_w_in', 0)])
    Wodout = gw[('od_w_out', 0)].reshape(D, D)
    Wup[1] = gw[('ffn_w_up', 1)]
    Wdn[1] = gw[('ffn_w_down', 1)].reshape(2, Fs, D)
    QW, KW = SWA_HEADS * SWA_HEAD_DIM, SWA_KV_HEADS * SWA_HEAD_DIM
    P1 = _mm(x2, Wodin, 'nn', name="od_proj")
    qT = _to_heads(P1, tabs, col0=0, width=QW, rotate=True, name="rope_q", out_dtype=BF16)
    kT = _to_heads(P1, tabs, col0=QW, width=KW, rotate=True, name="rope_k", out_dtype=BF16)
    vT = _to_heads(P1, tabs, col0=QW + KW, width=KW, rotate=False, name="heads_v", out_dtype=BF16)
    sink_rows = jnp.broadcast_to(od_sinks[0].reshape(SWA_KV_HEADS, SWA_GROUPS, 1, 1),
                                 (SWA_KV_HEADS, SWA_GROUPS, SWA_WINDOW, 1)).reshape(SWA_KV_HEADS, -1, 1)
    oT, Lsw = _swa_fwd(qT, kT, vT, sink_rows, name="swa_fwd")
    o_sw = _from_heads(oT, tabs, rotate_back=False, name="heads_o", out_dtype=BF16)
    mix1 = _mm(o_sw, Wodout, 'nn', name="od_out")
    x3, xh3, rs3 = _add_ln_fwd(x2, mix1, ln_mix_g[1], ln_mix_b[1], name="ln_mix1")
    f1, hf1, af1 = _ffn_fwd(x3, Wup[1], Wdn[1], cws[1], cbs[1], "l1")
    x4, xh4, rs4 = _add_ln_fwd(x3, f1, ln_ffn_g[1], ln_ffn_b[1], name="ln_ffn1")
    dy, loss_part = _loss_grad(x4, tgt, name="loss")

    dz4, dg_ffn1, db_ffn1 = _ln_bwd(dy, None, xh4, rs4, ln_ffn_g[1], name="lnb_ffn1")
    dx3f, dWup1, dWdn1, dcw1, dcb1 = _ffn_bwd(dz4, x3, hf1, af1, Wup[1], Wdn[1], cws[1], "l1")
    sib_ffn1 = _sibling_halves_start([dWup1, dWdn1.reshape(N_CHIPS, Rd, D)], [False, False], name="rs_ffn1_sib_start")
    dz3, dg_mix1, db_mix1 = _ln_bwd(dz4, dx3f, xh3, rs3, ln_mix_g[1], name="lnb_mix1", after=[sib_ffn1[4]])
    do_sw = _mm(dz3, Wodout, 'nt', name="od_out_dx")
    dWodout = _mm(o_sw, dz3, 'tn', name="od_out_dw", out_dtype=BF16)
    doT = _to_heads(do_sw, tabs, col0=0, width=QW, rotate=False, name="heads_do", out_dtype=F32)
    dqT, dkT, dvT, dsink = _swa_bwd(qT, kT, vT, sink_rows, oT, Lsw, doT, name="swa_bwd")
    dq1 = _from_heads(dqT, tabs, rotate_back=True, name="rope_dq", out_dtype=BF16)
    dk1 = _from_heads(dkT, tabs, rotate_back=True, name="rope_dk", out_dtype=BF16, skip_rows=SWA_WINDOW)
    dv1 = _from_heads(dvT, tabs, rotate_back=False, name="heads_dv", out_dtype=BF16, skip_rows=SWA_WINDOW)
    dP1 = jnp.concatenate([dq1, dk1, dv1], axis=1)
    dx2m = _mm(dP1, Wodin, 'nt', name="od_proj_dx")
    dWodin = _mm(x2, dP1, 'tn', name="od_proj_dw", out_dtype=BF16)

    def rs_begin(entries, grads, tag):
        cols = [split_cols[e] for e in entries]
        sib = _sibling_send_halves(grads, cols, name=f"rs_{tag}_sibling")
        return [_sum2_halves(g4, s4, bc, name=f"rs_sum2_{n}{l}")
                for (n, l), g4, s4, bc in zip(entries, grads, sib, cols)]

    def rs_begin_started(entries, started, after, tag):
        send, rcv, thru, lands, _ = started
        thru, lands = _sibling_halves_wait(send, rcv, thru, lands, [False] * len(thru), after,
                                           name=f"rs_{tag}_sib_wait")
        return [_sum2_halves(g4, s4, False, name=f"rs_sum2_{n}{l}") for (n, l), g4, s4 in zip(entries, thru, lands)]

    def own_parts(parts):
        me = 2 * lax.axis_index("x") + lax.axis_index("y")
        return [lax.dynamic_slice_in_dim(p, me, 1, axis=0) for p in parts]

    part_l1 = (rs_begin(grp_l1[:2], [_shards_from_cols(dWodin), dWodout.reshape(N_CHIPS, D // N_CHIPS, D)], "od")
               + rs_begin_started(grp_l1[2:], sib_ffn1, [dWodin], "ffn1"))
    rs_l1 = _chip_exchange_start('scatter', part_l1, [False] * len(part_l1), name="rs_l1_start")

    dz2, dg_ffn0, db_ffn0 = _ln_bwd(dz3, dx2m, xh2, rs2, ln_ffn_g[0], name="lnb_ffn0", after=[rs_l1[4]])
    dx1f, dWup0, dWdn0, dcw0, dcb0 = _ffn_bwd(dz2, x1, hf0, af0, Wup[0], Wdn[0], cws[0], "l0")
    sib_ffn0 = _sibling_halves_start([dWup0, dWdn0.reshape(N_CHIPS, Rd, D)], [False, False], name="rs_ffn0_sib_start")
    dz1, dg_mix0, db_mix0 = _ln_bwd(dz2, dx1f, xh1, rs1, ln_mix_g[0], name="lnb_mix0", after=[sib_ffn0[4]])
    dcat = _mm(dz1, Wout_ev, 'nt', name="ev_out_dx")
    dWout_ev = _mm(cat, dz1, 'tn', name="ev_out_dw", out_dtype=BF16)
    part_ffn0 = rs_begin_started(grp_ffn0, sib_ffn0, [dWout_ev], "ffn0")
    rs_ffn0 = _chip_exchange_start('scatter', part_ffn0, [False] * len(part_ffn0), name="rs_ffn0_start")
    dz = _glu_bwd(z, dcat, name="s5_glu_bwd")
    dyg = _mm(dz, Wglu, 'nt', name="s5_glu_dx", after=[rs_ffn0[4]])
    dWglu = _mm(yg, dz, 'tn', name="s5_glu_dw", out_dtype=BF16)
    dy_s5, du_dir, dD = _s5_out_bwd(dyg, y_s5, P, dskip, name="s5_out_bwd")
    dhh = _mm(dy_s5, CC, 'nt', bmode='bo', tm=2048, tn=HT, tk=UT, diag='kn', name="s5_y_dx")
    dCC = _mm(hh, dy_s5, 'tn', bmode='ao', tm=HT, tn=UT, diag='mn', name="s5_y_dw")
    lam, da_s5 = _s5_scan_bwd(dhh, hh, a_cat, name="s5_scan_bwd")
    du_bu = _mm(lam, BB, 'nt', bmode='abr', tm=2048, tn=UT, tk=HT, diag='kn', name="s5_bu_dx")
    dBB = _mm(u_s5, lam, 'tn', bmode='bo', tm=UT, tn=HT, diag='mn', name="s5_bu_dw")
    du = _combine([du_dir, du_bu], [1.0, 1.0], name="s5_du", out_dtype=BF16)
    dq0, dk0, dv0, dccol, dcrow = _fox_bwd(P, ccol, crow, fox, lse, dcat, name="fox_bwd")
    dc = jnp.transpose((dccol.reshape(FOX_HEADS, S) - dcrow.reshape(FOX_HEADS, S)))
    dc = jnp.pad(dc, ((0, 0), (0, LANE - FOX_HEADS)))
    dfl, dbf = _gate_bwd(dc, sgate, name="fox_gate_bwd")
    dP = jnp.concatenate([dq0, dk0, dv0, du], axis=1)
    dx0a = _mm(dP, WmainT, 'nn', name="ev_proj_dx")
    dx0b = _mm(dfl, WfT, 'nn', name="ev_proj_f_dx")
    dWmainT = _mm(dP, x0, 'tn', tm=1024, tn=1024, name="ev_proj_dw", out_dtype=BF16)
    dWfT = _mm(dfl, x0, 'tn', name="ev_proj_f_dw", out_dtype=BF16)
    grad_x = _combine([dz1, dx0a, dx0b], [ALPHA, 1.0, 1.0], name="grad_x")

    dbbt = _diag_extract(dBB, Cg, Pn, name="s5_bb_diag")
    dcct = _diag_extract(dCC, Pn, Cg, name="s5_cc_diag")
    dbb_re = jnp.transpose(dbbt[0].reshape(G, Cg, Pn), (0, 2, 1)).reshape(G * Pn, Cg)
    dbb_im = jnp.transpose(dbbt[1].reshape(G, Cg, Pn), (0, 2, 1)).reshape(G * Pn, Cg)
    db_re, db_im, dg_re1, dg_im1 = _s5_bb_bwd(g_re1, g_im1, b_re2, b_im2, dbb_re, dbb_im, name="s5_bb_bwd")
    dlam_re, dlam_im, dlstep = _s5_disc_bwd(lam_r, lam_i, lstep, da_s5[0].reshape(G, Pn), da_s5[1].reshape(G, Pn),
                                            dg_re1.reshape(G, Pn), dg_im1.reshape(G, Pn), name="s5_disc_bwd")
    dc_re = jnp.transpose(dcct[0].reshape(G, Pn, Cg), (0, 2, 1))
    dc_im = -jnp.transpose(dcct[1].reshape(G, Pn, Cg), (0, 2, 1))

    def conv_w_full(d0, d1):
        return jnp.stack([jnp.reshape(jnp.transpose(d[:, :, :Fs], (1, 0, 2)), (3, N_CHIPS * Fs)) for d in (d0, d1)])

    def conv_b_full(d0, d1):
        return jnp.stack([jnp.reshape(d[:, 0, :Fs], (N_CHIPS * Fs,)) for d in (d0, d1)])

    small_local = dict(
        ev_b_f=dbf[:, :FOX_HEADS], ev_lambda_re=dlam_re, ev_lambda_im=dlam_im, ev_log_step=dlstep,
        ev_ssm_b_re=db_re, ev_ssm_b_im=db_im, ev_ssm_c_re=dc_re, ev_ssm_c_im=dc_im, ev_ssm_d=dD,
        od_sinks=dsink[:, :, 0],
        ln_mix_g=jnp.concatenate([dg_mix0, dg_mix1]), ln_mix_b=jnp.concatenate([db_mix0, db_mix1]),
        ffn_conv_w=conv_w_full(dcw0, dcw1), ffn_conv_b=conv_b_full(dcb0, dcb1),
        ln_ffn_g=jnp.concatenate([dg_ffn0, dg_ffn1]), ln_ffn_b=jnp.concatenate([db_ffn0, db_ffn1]))
    small = list(small_local.keys())
    out_g, out_d, out_m, out_v = {}, {}, {}, {}
    loss_out = []

    def small_update(after):
        red = _all_reduce_small(_pack([small_local[n] for n in small] + [loss_part]), name="ar_small", after=after)
        full_shapes = [W[n].shape if n != 'ffn_conv_w' else (DEPTH, 3, N_CHIPS * Fs) for n in small]
        pieces = _unpack(red, full_shapes + [()])
        loss_out.append(pieces[-1])
        gsmall = dict(zip(small, pieces[:-1]))
        chip = 2 * lax.axis_index("x") + lax.axis_index("y")
        gsmall['ffn_conv_w'] = lax.dynamic_slice_in_dim(gsmall['ffn_conv_w'], chip * Fs, Fs, axis=2)
        shapes = [W[n].shape for n in small]
        gs, ds_, ms, vs = _adamw(_pack([W[n] for n in small])[None], _pack([gsmall[n] for n in small])[None],
                                 _pack([Mo[n] for n in small])[None], _pack([Vo[n] for n in small])[None],
                                 name="adamw_small", tr=1 << 14)
        out_g.update(zip(small, _unpack(gs, shapes)))
        out_d.update(zip(small, _unpack(ds_, shapes)))
        out_m.update(zip(small, _unpack(ms, shapes)))
        out_v.update(zip(small, _unpack(vs, shapes)))
        return vs

    def in_rows(lo, hi):
        pieces = ((0, qkv_w, dWmainT, 0), (qkv_w, qkv_w + FOX_HEADS, dWfT, 0), (qkv_w + FOX_HEADS, EIN, dWmainT, qkv_w))
        return [src[max(lo, a) - a + off:min(hi, b) - a + off] for a, b, src, off in pieces if max(lo, a) < min(hi, b)]

    dw_in4 = jnp.stack([jnp.concatenate(in_rows(k * ESH, (k + 1) * ESH), axis=0) for k in range(N_CHIPS)])
    part_now = rs_begin(grp_now, [dw_in4, _shards_from_cols(dWglu),
                                  dWout_ev.reshape(N_CHIPS, D // N_CHIPS, D)], "l0")
    small_done = small_update([grad_x])
    rs_now = _chip_exchange_start('scatter', part_now, [False] * len(part_now), name="rs_l0_start",
                                  after=[small_done])

    def finish_scatter(started, parts, after, tag):
        send, rcv, thru, lands, _ = started
        thru, lands = _chip_exchange_wait('scatter', send, rcv, thru, lands, [False] * len(parts), after,
                                          name=f"rs_{tag}_wait")
        return _own_slot(lands, own_parts(thru))

    def update(entries, recv, tag):
        halves = [_rowsum(r, name=f"rs_sum4_{e[0]}{e[1]}") for e, r in zip(entries, recv)]
        send, rcv, thru, lands, tok = _sibling_swap_start(halves, name=f"rs_{tag}_join_start")
        own = dict(zip(entries, thru))
        params = list(dict.fromkeys(e[0] for e in entries))

        def half_update(n, grads, is_own, prev, after_name):
            return _adamw_half(Wv[n], [grads[(n, l)] for l in range(W[n].shape[0])], view(n, Mo[n]), view(n, Vo[n]),
                               name=f"adamw_{after_name}_{n}", own=is_own, prev=prev, by_cols=split_cols[(n, 0)])

        first = {n: half_update(n, own, True, None, "own") for n in params}
        _, others = _sibling_swap_wait(send, rcv, thru, lands, [first[n][3] for n in params] + [tok],
                                       name=f"rs_{tag}_join_wait")
        oth = dict(zip(entries, others))
        done = []
        for n in params:
            res = half_update(n, oth, False, first[n], "sib")
            out_g[n], out_d[n], out_m[n], out_v[n] = (view(n, t) for t in res)
            done.append(res[3])
        return done

    recv_rest = (finish_scatter(rs_l1, part_l1, [rs_now[4]], "l1")
                 + finish_scatter(rs_ffn0, part_ffn0, [rs_now[4]], "ffn0"))
    done = update(grp_l1 + grp_ffn0, recv_rest, "rest")
    update(grp_now, finish_scatter(rs_now, part_now, done, "l0"), "l0")
    loss = loss_out[0]

    return (loss, grad_x.reshape(1, S, D), *[out_g[n] for n in names], *[out_d[n] for n in names],
            *[out_m[n] for n in names], *[out_v[n] for n in names])
```

```python
import math

import numpy as np
import jax
import jax.numpy as jnp
from jax import lax
from jax.experimental import pallas as pl
from jax.experimental.pallas import tpu as pltpu

F32 = jnp.float32
BF16 = jnp.bfloat16
MESH = pl.DeviceIdType.MESH
ANY = pl.BlockSpec(memory_space=pl.ANY)

D_MODEL = 2048
FOX_HEADS = 8
FOX_HEAD_DIM = 128
FOX_WIDTH = 1024
SSM_WIDTH = 1024
SSM_GROUP = 16
SSM_GROUPS = 64
SSM_STATE = 64
SWA_HEADS = 32
SWA_KV_HEADS = 4
SWA_HEAD_DIM = 64
SWA_GROUPS = 8
SWA_WINDOW = 128
ROPE_DIM = 16
ROPE_THETA = 500000.0
LN_EPS = 1e-5
DEPTH = 2
ALPHA = (2.0 * DEPTH) ** 0.25
ADAM_LR = 0.001
ADAM_B1 = 0.9
ADAM_B2 = 0.999
ADAM_EPS = 1e-08
ADAM_WD = 0.01
ADAM_STEP = 10
N_CHIPS = 4

VMEM_LIMIT = 56 * 1024 * 1024
LANE = 128


def _call(body, after=(), **kw):
    if after:
        n = len(after)

        def shifted(*refs):
            return body(*refs[n:])

        call = _call(shifted, **dict(kw, in_specs=[ANY] * n + list(kw["in_specs"])))
        return lambda *args: call(*after, *args)
    return pl.pallas_call(body, **kw)


def _cparams(sem):
    return pltpu.CompilerParams(dimension_semantics=sem, vmem_limit_bytes=VMEM_LIMIT)


def _rup(n, m):
    return (n + m - 1) // m * m


def _pick(n, pref):
    if n <= pref:
        return n
    for step in (128, 16, 8):
        for t in range(pref - pref % step, 0, -step):
            if n % t == 0:
                return t
    return n


def _tile2d(rows, cols, pref_rows=256, budget=256 * 1024):
    tr = _pick(rows, pref_rows)
    if tr < 64:
        tr = rows
    if cols % LANE:
        return tr, cols
    return tr, _pick(cols, max(LANE, budget // tr // LANE * LANE))


def _mm(a, b, mode, *, name, tm=512, tn=1024, tk=2048, bmode=None, out_dtype=F32, after=(), b_map=None,
        o_map=None, diag=None, plus=()):
    a3 = a if a.ndim == 3 else a[None]
    b3 = b if b.ndim == 3 else b[None]
    if mode == 'tn':
        K, M = a3.shape[1:]
    else:
        M, K = a3.shape[1:]
    N = b3.shape[1] if mode == 'nt' else b3.shape[2]
    tm, tn, tk = _pick(M, tm), _pick(N, tn), _pick(K, tk)
    nb = max(a3.shape[0], b3.shape[0])
    nbo, nbr = (1, nb) if bmode == 'abr' else (nb, 1)
    nm, nk = M // tm, K // tk
    if diag == 'kn':
        assert K // tk == N // tn
        nk = 1
    if diag == 'mn':
        assert M // tm == N // tn
        nm = 1
    nred = nbr * nk
    a_b = bmode in ('ao', 'abr')
    b_b = bmode in ('bo', 'abr')
    o_b = bmode in ('bo', 'ao')

    def bsel(flag, bo, br, remap=None):
        if not flag:
            return 0
        return (bo + br) if remap is None else remap(bo + br)

    def mi(i, j):
        return j if diag == 'mn' else i

    def ki(j, k):
        return j if diag == 'kn' else k

    if mode == 'tn':
        a_spec = pl.BlockSpec((None, tk, tm), lambda bo, i, j, br, k: (bsel(a_b, bo, br), ki(j, k), mi(i, j)))
    else:
        a_spec = pl.BlockSpec((None, tm, tk), lambda bo, i, j, br, k: (bsel(a_b, bo, br), mi(i, j), ki(j, k)))
    if mode == 'nt':
        b_spec = pl.BlockSpec((None, tn, tk), lambda bo, i, j, br, k: (bsel(b_b, bo, br, b_map), j, ki(j, k)))
    else:
        b_spec = pl.BlockSpec((None, tk, tn), lambda bo, i, j, br, k: (bsel(b_b, bo, br, b_map), ki(j, k), j))
    o_spec = pl.BlockSpec((None, tm, tn), lambda bo, i, j, br, k: (bsel(o_b, bo, br, o_map), mi(i, j), j))
    dn = {'nn': (((1,), (0,)), ((), ())), 'nt': (((1,), (1,)), ((), ())), 'tn': (((0,), (0,)), ((), ()))}[mode]

    na = len(plus)

    def body(a_ref, b_ref, *rest):
        plus_refs = rest[:na]
        o_ref, scratch = rest[na + len(after)], rest[na + len(after) + 1:]
        r = lax.dot_general(a_ref[...].astype(BF16), b_ref[...].astype(BF16), dn, preferred_element_type=F32)

        def finish(total):
            for (_, scale), p_ref in zip(plus, plus_refs):
                total = total + scale * p_ref[...].astype(F32)
            o_ref[...] = total.astype(out_dtype)

        if nred == 1:
            finish(r)
        else:
            acc = scratch[0]
            step = pl.program_id(3) * nk + pl.program_id(4)

            @pl.when(step == 0)
            def _():
                acc[...] = r

            @pl.when(step > 0)
            def _():
                acc[...] += r

            @pl.when(step == nred - 1)
            def _():
                finish(acc[...])

    out = _call(
        body, name=name,
        grid=(nbo, nm, N // tn, nbr, nk),
        in_specs=[a_spec, b_spec] + [o_spec] * na + [ANY] * len(after), out_specs=o_spec,
        out_shape=jax.ShapeDtypeStruct((nbo if o_b else 1, M, N), out_dtype),
        scratch_shapes=[] if nred == 1 else [pltpu.VMEM((tm, tn), F32)],
        compiler_params=_cparams(("parallel", "parallel", "parallel", "arbitrary", "arbitrary")),
    )(a3, b3, *[p if p.ndim == 3 else p[None] for p, _ in plus], *after)
    return out if o_b else out[0]


def _add_ln_fwd(x, r, g, b, *, name):
    S, D = x.shape
    tr = _pick(S, 256)

    def body(x_ref, r_ref, g_ref, b_ref, o_ref, xh_ref, rs_ref):
        z = ALPHA * x_ref[...] + r_ref[...]
        mu = jnp.mean(z, axis=-1, keepdims=True)
        zc = z - mu
        var = jnp.mean(zc * zc, axis=-1, keepdims=True)
        rstd = lax.rsqrt(var + LN_EPS)
        xh = zc * rstd
        xh_ref[...] = xh
        rs_ref[...] = rstd
        o_ref[...] = xh * g_ref[...] + b_ref[...]

    row = pl.BlockSpec((tr, D), lambda i: (i, 0))
    vec = pl.BlockSpec((1, D), lambda i: (0, 0))
    return _call(
        body, name=name, grid=(S // tr,),
        in_specs=[row, row, vec, vec],
        out_specs=[row, row, pl.BlockSpec((tr, 1), lambda i: (i, 0))],
        out_shape=[jax.ShapeDtypeStruct((S, D), F32), jax.ShapeDtypeStruct((S, D), F32),
                   jax.ShapeDtypeStruct((S, 1), F32)],
        compiler_params=_cparams(("parallel",)),
    )(x, r, g.reshape(1, D), b.reshape(1, D))


def _ln_bwd(da, db, xhat, rstd, g, *, name, after=()):
    S, D = xhat.shape
    tr = _pick(S, 256)
    two = db is not None

    def body(*refs):
        refs = refs[len(after):]
        if two:
            da_ref, db_ref, xh_ref, rs_ref, g_ref, dz_ref, dg_ref, dbt_ref = refs
            dy = ALPHA * da_ref[...] + db_ref[...]
        else:
            da_ref, xh_ref, rs_ref, g_ref, dz_ref, dg_ref, dbt_ref = refs
            dy = da_ref[...]
        xh = xh_ref[...]
        dxh = dy * g_ref[...]
        m1 = jnp.mean(dxh, axis=-1, keepdims=True)
        m2 = jnp.mean(dxh * xh, axis=-1, keepdims=True)
        dz_ref[...] = rs_ref[...] * (dxh - m1 - xh * m2)
        pg = jnp.sum(dy * xh, axis=0, keepdims=True)
        pb = jnp.sum(dy, axis=0, keepdims=True)

        @pl.when(pl.program_id(0) == 0)
        def _():
            dg_ref[...] = pg
            dbt_ref[...] = pb

        @pl.when(pl.program_id(0) > 0)
        def _():
            dg_ref[...] += pg
            dbt_ref[...] += pb

    row = pl.BlockSpec((tr, D), lambda i: (i, 0))
    vec = pl.BlockSpec((1, D), lambda i: (0, 0))
    ins = list(after) + [da] + ([db] if two else []) + [xhat, rstd, g.reshape(1, D)]
    in_specs = [ANY] * len(after) + [row] + ([row] if two else []) + [row, pl.BlockSpec((tr, 1), lambda i: (i, 0)), vec]
    return _call(
        body, name=name, grid=(S // tr,),
        in_specs=in_specs, out_specs=[row, vec, vec],
        out_shape=[jax.ShapeDtypeStruct((S, D), F32), jax.ShapeDtypeStruct((1, D), F32),
                   jax.ShapeDtypeStruct((1, D), F32)],
        compiler_params=_cparams(("arbitrary",)),
    )(*ins)


def _loss_grad(y, t, *, name):
    S, D = y.shape
    tr = _pick(S, 256)

    def body(y_ref, t_ref, dy_ref, l_ref):
        e = y_ref[...] - t_ref[...]
        dy_ref[...] = e * (1.0 / D)
        part = 0.5 * jnp.sum(jnp.sum(e * e, axis=-1, keepdims=True) * (1.0 / D), axis=0, keepdims=True)

        @pl.when(pl.program_id(0) == 0)
        def _():
            l_ref[...] = part

        @pl.when(pl.program_id(0) > 0)
        def _():
            l_ref[...] += part

    row = pl.BlockSpec((tr, D), lambda i: (i, 0))
    return _call(
        body, name=name, grid=(S // tr,), in_specs=[row, row],
        out_specs=[row, pl.BlockSpec((1, 1), lambda i: (0, 0))],
        out_shape=[jax.ShapeDtypeStruct((S, D), F32), jax.ShapeDtypeStruct((1, 1), F32)],
        compiler_params=_cparams(("arbitrary",)),
    )(y, t)


def _split3(x):
    h = x.astype(BF16)
    r = x - h.astype(F32)
    m = r.astype(BF16)
    l = (r - m.astype(F32)).astype(BF16)
    return h, m, l


def _tri_matmul(tri_bf, x):
    h, m, l = _split3(x)
    dn = (((1,), (0,)), ((), ()))
    return (lax.dot_general(tri_bf, l, dn, preferred_element_type=F32)
            + lax.dot_general(tri_bf, m, dn, preferred_element_type=F32)
            + lax.dot_general(tri_bf, h, dn, preferred_element_type=F32))


def _gate_fwd(fl, bf, *, name):
    S = fl.shape[0]
    tc = _pick(S, 256)
    nchunk = S // tc

    def body(fl_ref, bf_ref, c_ref, sg_ref):
        r = lax.broadcasted_iota(jnp.int32, (tc, tc), 0)
        cidx = lax.broadcasted_iota(jnp.int32, (tc, tc), 1)
        tri = (r >= cidx).astype(BF16)
        carry = jnp.zeros((1, LANE), F32)
        for ch in range(nchunk):
            x = fl_ref[pl.ds(ch * tc, tc), :] + bf_ref[...]
            lf = jnp.minimum(x, 0.0) - jnp.log(1.0 + jnp.exp(-jnp.abs(x)))
            sg_ref[pl.ds(ch * tc, tc), :] = jax.nn.sigmoid(-x)
            c_ref[pl.ds(ch * tc, tc), :] = _tri_matmul(tri, lf) + carry
            carry = carry + jnp.sum(lf, axis=0, keepdims=True)

    full = pl.BlockSpec((S, LANE), lambda: (0, 0))
    return _call(
        body, name=name, in_specs=[full, pl.BlockSpec((1, LANE), lambda: (0, 0))], out_specs=[full, full],
        out_shape=[jax.ShapeDtypeStruct((S, LANE), F32)] * 2,
        compiler_params=pltpu.CompilerParams(vmem_limit_bytes=VMEM_LIMIT),
    )(fl, bf)


def _gate_bwd(dc, sg, *, name):
    S = dc.shape[0]
    tc = _pick(S, 256)
    nchunk = S // tc

    def body(dc_ref, sg_ref, dfl_ref, db_ref):
        r = lax.broadcasted_iota(jnp.int32, (tc, tc), 0)
        cidx = lax.broadcasted_iota(jnp.int32, (tc, tc), 1)
        tri = (r <= cidx).astype(BF16)
        carry = jnp.zeros((1, LANE), F32)
        dbacc = jnp.zeros((1, LANE), F32)
        for ch in reversed(range(nchunk)):
            d = dc_ref[pl.ds(ch * tc, tc), :]
            dfl = (_tri_matmul(tri, d) + carry) * sg_ref[pl.ds(ch * tc, tc), :]
            dfl_ref[pl.ds(ch * tc, tc), :] = dfl
            dbacc = dbacc + jnp.sum(dfl, axis=0, keepdims=True)
            carry = carry + jnp.sum(d, axis=0, keepdims=True)
        db_ref[...] = dbacc

    full = pl.BlockSpec((S, LANE), lambda: (0, 0))
    return _call(
        body, name=name, in_specs=[full, full], out_specs=[full, pl.BlockSpec((1, LANE), lambda: (0, 0))],
        out_shape=[jax.ShapeDtypeStruct((S, LANE), F32), jax.ShapeDtypeStruct((1, LANE), F32)],
        compiler_params=pltpu.CompilerParams(vmem_limit_bytes=VMEM_LIMIT),
    )(dc, sg)


def _fox_scores(q_ref, k_ref, cc_ref, cr_ref, qi, tq, S):
    scale = 1.0 / math.sqrt(FOX_HEAD_DIM)
    s = lax.dot_general(q_ref[...].astype(BF16), k_ref[...].astype(BF16), (((1,), (1,)), ((), ())),
                        preferred_element_type=F32) * scale
    s = s + cc_ref[...] - cr_ref[...]
    row = lax.broadcasted_iota(jnp.int32, (tq, S), 0) + qi * tq
    col = lax.broadcasted_iota(jnp.int32, (tq, S), 1)
    return s, row >= col


def _fox_fwd(P, ccol, crow, *, name):
    S = P.shape[0]
    tq = _pick(S, 256)
    H = FOX_HEADS

    def body(q_ref, k_ref, v_ref, cc_ref, cr_ref, o_ref, l_ref):
        s, causal = _fox_scores(q_ref, k_ref, cc_ref, cr_ref, pl.program_id(1), tq, S)
        s = jnp.where(causal, s, -1e30)
        m = jnp.max(s, axis=-1, keepdims=True)
        e = jnp.exp(s - m)
        den = jnp.sum(e, axis=-1, keepdims=True)
        p = e / den
        o_ref[...] = jnp.dot(p.astype(BF16), v_ref[...].astype(BF16), preferred_element_type=F32)
        l_ref[...] = m + jnp.log(den)

    return _call(
        body, name=name, grid=(H, S // tq),
        in_specs=[pl.BlockSpec((tq, 128), lambda h, i: (i, h)),
                  pl.BlockSpec((S, 128), lambda h, i: (0, H + h)),
                  pl.BlockSpec((S, 128), lambda h, i: (0, 2 * H + h)),
                  pl.BlockSpec((None, tq, 1), lambda h, i: (h, i, 0)),
                  pl.BlockSpec((None, 1, S), lambda h, i: (h, 0, 0))],
        out_specs=[pl.BlockSpec((tq, 128), lambda h, i: (i, h)),
                   pl.BlockSpec((None, tq, 1), lambda h, i: (h, i, 0))],
        out_shape=[jax.ShapeDtypeStruct((S, FOX_WIDTH), F32), jax.ShapeDtypeStruct((H, S, 1), F32)],
        compiler_params=_cparams(("parallel", "parallel")),
    )(P, P, P, ccol, crow)


def _fox_bwd(P, ccol, crow, o, lse, dcat, *, name):
    S = P.shape[0]
    tq = _pick(S, 256)
    H = FOX_HEADS
    nq = S // tq
    scale = 1.0 / math.sqrt(FOX_HEAD_DIM)

    def body(q_ref, k_ref, v_ref, cc_ref, cr_ref, o_ref, l_ref, do_ref,
             dq_ref, dk_ref, dv_ref, dcc_ref, dcr_ref, dk_acc, dv_acc):
        qi = pl.program_id(1)
        s, causal = _fox_scores(q_ref, k_ref, cc_ref, cr_ref, qi, tq, S)
        p = jnp.where(causal, jnp.exp(s - l_ref[...]), 0.0)
        do = do_ref[...]
        do_bf = do.astype(BF16)
        dp = lax.dot_general(do_bf, v_ref[...].astype(BF16), (((1,), (1,)), ((), ())), preferred_element_type=F32)
        delta = jnp.sum(do * o_ref[...], axis=-1, keepdims=True)
        ds = p * (dp - delta)
        ds_bf = ds.astype(BF16)
        dq_ref[...] = (jnp.dot(ds_bf, k_ref[...].astype(BF16), preferred_element_type=F32) * scale).astype(BF16)
        dkp = lax.dot_general(ds_bf, q_ref[...].astype(BF16), (((0,), (0,)), ((), ())),
                              preferred_element_type=F32) * scale
        dvp = lax.dot_general(p.astype(BF16), do_bf, (((0,), (0,)), ((), ())), preferred_element_type=F32)
        dcc_ref[...] = jnp.sum(ds, axis=-1, keepdims=True)
        dcr = jnp.sum(ds, axis=0, keepdims=True)

        @pl.when(qi == 0)
        def _():
            dk_acc[...] = dkp
            dv_acc[...] = dvp
            dcr_ref[...] = dcr

        @pl.when(qi > 0)
        def _():
            dk_acc[...] += dkp
            dv_acc[...] += dvp
            dcr_ref[...] += dcr

        @pl.when(qi == nq - 1)
        def _():
            dk_ref[...] = dk_acc[...].astype(BF16)
            dv_ref[...] = dv_acc[...].astype(BF16)

    qblk = pl.BlockSpec((tq, 128), lambda h, i: (i, h))
    kvo = pl.BlockSpec((S, 128), lambda h, i: (0, h))
    col = pl.BlockSpec((None, tq, 1), lambda h, i: (h, i, 0))
    rowv = pl.BlockSpec((None, 1, S), lambda h, i: (h, 0, 0))
    return _call(
        body, name=name, grid=(H, nq),
        in_specs=[qblk,
                  pl.BlockSpec((S, 128), lambda h, i: (0, H + h)),
                  pl.BlockSpec((S, 128), lambda h, i: (0, 2 * H + h)),
                  col, rowv, qblk, col, qblk],
        out_specs=[qblk, kvo, kvo, col, rowv],
        out_shape=[jax.ShapeDtypeStruct((S, FOX_WIDTH), BF16)] * 3
        + [jax.ShapeDtypeStruct((H, S, 1), F32), jax.ShapeDtypeStruct((H, 1, S), F32)],
        scratch_shapes=[pltpu.VMEM((S, 128), F32), pltpu.VMEM((S, 128), F32)],
        compiler_params=_cparams(("parallel", "arbitrary")),
    )(P, P, P, ccol, crow, o, lse, dcat)


def _s5_disc_fwd(lr, li, ls, *, name, after=()):
    G, Pn = lr.shape

    def body(lr_ref, li_ref, ls_ref, ar_ref, ai_ref, gr_ref, gi_ref):
        lr_, li_ = lr_ref[...], li_ref[...]
        dt = jnp.exp(ls_ref[...])
        mag = jnp.exp(lr_ * dt)
        th = li_ * dt
        ar = mag * jnp.cos(th)
        ai = mag * jnp.sin(th)
        den = lr_ * lr_ + li_ * li_
        xr = ar - 1.0
        ar_ref[...] = ar
        ai_ref[...] = ai
        gr_ref[...] = (xr * lr_ + ai * li_) / den
        gi_ref[...] = (ai * lr_ - xr * li_) / den

    sq = pl.BlockSpec((G, Pn), lambda: (0, 0))
    return _call(
        body, after=after, name=name, in_specs=[sq, sq, pl.BlockSpec((G, 1), lambda: (0, 0))], out_specs=[sq] * 4,
        out_shape=[jax.ShapeDtypeStruct((G, Pn), F32)] * 4,
    )(lr, li, ls)


def _s5_disc_bwd(lr, li, ls, dar, dai, dgr, dgi, *, name):
    G, Pn = lr.shape

    def body(lr_ref, li_ref, ls_ref, dar_ref, dai_ref, dgr_ref, dgi_ref, dlr_ref, dli_ref, dls_ref):
        lr_, li_ = lr_ref[...], li_ref[...]
        dt = jnp.exp(ls_ref[...])
        mag = jnp.exp(lr_ * dt)
        th = li_ * dt
        ar = mag * jnp.cos(th)
        ai = mag * jnp.sin(th)
        den = lr_ * lr_ + li_ * li_
        xr = ar - 1.0
        xi = ai
        g_re = (xr * lr_ + xi * li_) / den
        g_im = (xi * lr_ - xr * li_) / den
        dgr_, dgi_ = dgr_ref[...], dgi_ref[...]
        dxr = (dgr_ * lr_ - dgi_ * li_) / den
        dxi = (dgr_ * li_ + dgi_ * lr_) / den
        dden = -(dgr_ * g_re + dgi_ * g_im) / den
        dlr = (dgr_ * xr + dgi_ * xi) / den + 2.0 * dden * lr_
        dli = (dgr_ * xi - dgi_ * xr) / den + 2.0 * dden * li_
        da_r = dar_ref[...] + dxr
        da_i = dai_ref[...] + dxi
        dmag_mag = da_r * ar + da_i * ai
        dth = da_i * ar - da_r * ai
        dlr_ref[...] = dlr + dmag_mag * dt
        dli_ref[...] = dli + dth * dt
        ddt = jnp.sum(dmag_mag * lr_ + dth * li_, axis=-1, keepdims=True)
        dls_ref[...] = ddt * dt

    sq = pl.BlockSpec((G, Pn), lambda: (0, 0))
    c1 = pl.BlockSpec((G, 1), lambda: (0, 0))
    return _call(
        body, name=name, in_specs=[sq, sq, c1, sq, sq, sq, sq], out_specs=[sq, sq, c1],
        out_shape=[jax.ShapeDtypeStruct((G, Pn), F32)] * 2 + [jax.ShapeDtypeStruct((G, 1), F32)],
    )(lr, li, ls, dar, dai, dgr, dgi)


def _s5_bb_fwd(gr, gi, br, bi, *, name):
    R, C = br.shape

    def body(gr_ref, gi_ref, br_ref, bi_ref, or_ref, oi_ref):
        g_r, g_i, b_r, b_i = gr_ref[...], gi_ref[...], br_ref[...], bi_ref[...]
        or_ref[...] = g_r * b_r - g_i * b_i
        oi_ref[...] = g_r * b_i + g_i * b_r

    w = pl.BlockSpec((R, C), lambda: (0, 0))
    c1 = pl.BlockSpec((R, 1), lambda: (0, 0))
    return _call(body, name=name, in_specs=[c1, c1, w, w], out_specs=[w, w],
                 out_shape=[jax.ShapeDtypeStruct((R, C), F32)] * 2)(gr, gi, br, bi)


def _s5_bb_bwd(gr, gi, br, bi, dbbr, dbbi, *, name):
    R, C = br.shape

    def body(gr_ref, gi_ref, br_ref, bi_ref, dr_ref, di_ref, dbr_ref, dbi_ref, dgr_ref, dgi_ref):
        g_r, g_i, b_r, b_i = gr_ref[...], gi_ref[...], br_ref[...], bi_ref[...]
        d_r, d_i = dr_ref[...], di_ref[...]
        dbr_ref[...] = g_r * d_r + g_i * d_i
        dbi_ref[...] = g_r * d_i - g_i * d_r
        dgr_ref[...] = jnp.sum(d_r * b_r + d_i * b_i, axis=-1, keepdims=True)
        dgi_ref[...] = jnp.sum(d_i * b_r - d_r * b_i, axis=-1, keepdims=True)

    w = pl.BlockSpec((R, C), lambda: (0, 0))
    c1 = pl.BlockSpec((R, 1), lambda: (0, 0))
    return _call(body, name=name, in_specs=[c1, c1, w, w, w, w], out_specs=[w, w, c1, c1],
                 out_shape=[jax.ShapeDtypeStruct((R, C), F32)] * 2 + [jax.ShapeDtypeStruct((R, 1), F32)] * 2,
                 )(gr, gi, br, bi, dbbr, dbbi)


_DIAG_TILE = 8


def _diag_mask(gr, gc):
    rows, cols = _DIAG_TILE * gr, _DIAG_TILE * gc
    r = lax.broadcasted_iota(jnp.int32, (rows, cols), 0) >> (gr.bit_length() - 1)
    c = lax.broadcasted_iota(jnp.int32, (rows, cols), 1) >> (gc.bit_length() - 1)
    return r == c


def _diag_expand(t2, gr, gc, *, name, after=()):
    _, R, _ = t2.shape
    G = R // gr
    nt = G // _DIAG_TILE
    rows, cols = _DIAG_TILE * gr, _DIAG_TILE * gc

    def body(t_ref, o_ref):
        src = lax.broadcasted_iota(jnp.int32, (gc, cols), 0)
        dst = lax.broadcasted_iota(jnp.int32, (gc, cols), 1) & (gc - 1)
        spread = (src == dst).astype(BF16)
        y = jnp.dot(t_ref[...].astype(BF16), spread, preferred_element_type=F32)
        o_ref[...] = jnp.where(_diag_mask(gr, gc), y, 0.0).astype(BF16)

    return _call(
        body, after=after, name=name, grid=(2, nt),
        in_specs=[pl.BlockSpec((None, rows, gc), lambda p, i: (p, i, 0))],
        out_specs=pl.BlockSpec((None, rows, cols), lambda p, i: (p, i, i)),
        out_shape=jax.ShapeDtypeStruct((2, R, G * gc), BF16),
        compiler_params=_cparams(("parallel",) * 2),
    )(t2)


def _diag_extract(xd, gr, gc, *, name):
    _, R, _ = xd.shape
    nt = R // gr // _DIAG_TILE
    rows, cols = _DIAG_TILE * gr, _DIAG_TILE * gc

    def body(x_ref, o_ref):
        src = lax.broadcasted_iota(jnp.int32, (cols, gc), 0) & (gc - 1)
        dst = lax.broadcasted_iota(jnp.int32, (cols, gc), 1)
        fold = (src == dst).astype(BF16)
        parts = _split3(jnp.where(_diag_mask(gr, gc), x_ref[...], 0.0))
        acc = jnp.dot(parts[2], fold, preferred_element_type=F32)
        acc = acc + jnp.dot(parts[1], fold, preferred_element_type=F32)
        o_ref[...] = acc + jnp.dot(parts[0], fold, preferred_element_type=F32)

    return _call(
        body, name=name, grid=(2, nt),
        in_specs=[pl.BlockSpec((None, rows, cols), lambda p, i: (p, i, i))],
        out_specs=pl.BlockSpec((None, rows, gc), lambda p, i: (p, i, 0)),
        out_shape=jax.ShapeDtypeStruct((2, R, gc), F32),
        compiler_params=_cparams(("parallel",) * 2),
    )(xd)


SCAN_BLOCK = 8


def _cpowers(ar, ai, sign):
    ai = sign * ai
    out = [(ar, ai)]
    for _ in range(SCAN_BLOCK - 1):
        pr, pi = out[-1]
        out.append((pr * ar - pi * ai, pr * ai + pi * ar))
    return out


def _row_table(pw, row, index_of_row):
    tr_ = jnp.broadcast_to(pw[index_of_row(0)][0], row.shape)
    ti_ = jnp.broadcast_to(pw[index_of_row(0)][1], row.shape)
    for r in range(1, SCAN_BLOCK):
        pr, pi = pw[index_of_row(r)]
        tr_ = jnp.where(row == r, pr, tr_)
        ti_ = jnp.where(row == r, pi, ti_)
    return tr_, ti_


def _s5_scan_fwd(bu, a, *, name):
    _, S, N = bu.shape
    tc = 512
    nt = N // tc

    def body(a_ref, b_ref, h_ref):
        pw = _cpowers(a_ref[0], a_ref[1], 1.0)
        row = lax.broadcasted_iota(jnp.int32, (SCAN_BLOCK, tc), 0)
        lead_r, lead_i = _row_table(pw, row, lambda r: r)
        mult = {sh: (jnp.where(row >= sh, pw[sh - 1][0], 0.0), jnp.where(row >= sh, pw[sh - 1][1], 0.0))
                for sh in (1, 2, 4)}

        def step(k, carry):
            cr, ci = carry
            rows = pl.ds(pl.multiple_of(k * SCAN_BLOCK, SCAN_BLOCK), SCAN_BLOCK)
            xr, xi = b_ref[0, rows, :], b_ref[1, rows, :]
            for sh in (1, 2, 4):
                sr, si = pltpu.roll(xr, sh, 0), pltpu.roll(xi, sh, 0)
                kr, ki = mult[sh]
                xr, xi = xr + kr * sr - ki * si, xi + kr * si + ki * sr
            h_ref[0, rows, :] = xr + lead_r * cr - lead_i * ci
            h_ref[1, rows, :] = xi + lead_r * ci + lead_i * cr
            last = row == SCAN_BLOCK - 1
            tr_ = jnp.sum(jnp.where(last, xr, 0.0), axis=0, keepdims=True)
            ti_ = jnp.sum(jnp.where(last, xi, 0.0), axis=0, keepdims=True)
            a8r, a8i = pw[SCAN_BLOCK - 1]
            return a8r * cr - a8i * ci + tr_, a8r * ci + a8i * cr + ti_

        z = jnp.zeros((1, tc), F32)
        lax.fori_loop(0, S // SCAN_BLOCK, step, (z, z), unroll=2)

    vec = pl.BlockSpec((2, 1, tc), lambda j: (0, 0, j))
    mat = pl.BlockSpec((2, S, tc), lambda j: (0, 0, j))
    return _call(
        body, name=name, grid=(nt,), in_specs=[vec, mat], out_specs=mat,
        out_shape=jax.ShapeDtypeStruct((2, S, N), F32),
        compiler_params=_cparams(("parallel",)),
    )(a, bu)


def _s5_scan_bwd(g, h, a, *, name):
    _, S, N = g.shape
    tc = 256
    nt = N // tc

    def body(a_ref, g_ref, h_ref, l_ref, da_ref):
        pw = _cpowers(a_ref[0], a_ref[1], -1.0)
        row = lax.broadcasted_iota(jnp.int32, (SCAN_BLOCK, tc), 0)
        tail_r, tail_i = _row_table(pw, row, lambda r: SCAN_BLOCK - 1 - r)
        nb = S // SCAN_BLOCK
        mult = {sh: (jnp.where(row < SCAN_BLOCK - sh, pw[sh - 1][0], 0.0),
                     jnp.where(row < SCAN_BLOCK - sh, pw[sh - 1][1], 0.0)) for sh in (1, 2, 4)}

        def step(i, carry):
            k = nb - 1 - i
            cr, ci, dar, dai = carry
            rows = pl.ds(pl.multiple_of(k * SCAN_BLOCK, SCAN_BLOCK), SCAN_BLOCK)
            xr, xi = g_ref[0, rows, :], g_ref[1, rows, :]
            for sh in (1, 2, 4):
                sr, si = pltpu.roll(xr, SCAN_BLOCK - sh, 0), pltpu.roll(xi, SCAN_BLOCK - sh, 0)
                kr, ki = mult[sh]
                xr, xi = xr + kr * sr - ki * si, xi + kr * si + ki * sr
            lr = xr + tail_r * cr - tail_i * ci
            li = xi + tail_r * ci + tail_i * cr
            l_ref[0, rows, :] = lr
            l_ref[1, rows, :] = li
            prev = pl.ds(pl.multiple_of(jnp.maximum(k - 1, 0) * SCAN_BLOCK, SCAN_BLOCK), SCAN_BLOCK)
            has_prev = jnp.where(k > 0, 1.0, 0.0).astype(F32)
            first = row == 0
            hpr = jnp.where(first, pltpu.roll(h_ref[0, prev, :], 1, 0) * has_prev, pltpu.roll(h_ref[0, rows, :], 1, 0))
            hpi = jnp.where(first, pltpu.roll(h_ref[1, prev, :], 1, 0) * has_prev, pltpu.roll(h_ref[1, rows, :], 1, 0))
            tr_ = jnp.sum(jnp.where(first, xr, 0.0), axis=0, keepdims=True)
            ti_ = jnp.sum(jnp.where(first, xi, 0.0), axis=0, keepdims=True)
            a8r, a8i = pw[SCAN_BLOCK - 1]
            return (a8r * cr - a8i * ci + tr_, a8r * ci + a8i * cr + ti_,
                    dar + lr * hpr + li * hpi, dai + li * hpr - lr * hpi)

        z = jnp.zeros((1, tc), F32)
        z8 = jnp.zeros((SCAN_BLOCK, tc), F32)
        _, _, dar, dai = lax.fori_loop(0, nb, step, (z, z, z8, z8), unroll=2)
        da_ref[0] = jnp.sum(dar, axis=0, keepdims=True)
        da_ref[1] = jnp.sum(dai, axis=0, keepdims=True)

    vec = pl.BlockSpec((2, 1, tc), lambda j: (0, 0, j))
    mat = pl.BlockSpec((2, S, tc), lambda j: (0, 0, j))
    return _call(
        body, name=name, grid=(nt,), in_specs=[vec, mat, mat], out_specs=[mat, vec],
        out_shape=[jax.ShapeDtypeStruct((2, S, N), F32), jax.ShapeDtypeStruct((2, 1, N), F32)],
        compiler_params=_cparams(("parallel",)),
    )(a, g, h)


_GELU_C = math.sqrt(2.0 / math.pi)


def _s5_out_fwd(yc, P, dskip, *, name):
    S, W = yc.shape
    tr = _pick(S, 256)
    ub = 3 * FOX_WIDTH // W

    def body(yc_ref, u_ref, d_ref, y_ref, yg_ref):
        y = yc_ref[...] + d_ref[...] * u_ref[...]
        y_ref[...] = y
        t = jnp.tanh(_GELU_C * (y + 0.044715 * y * y * y))
        yg_ref[...] = (0.5 * y * (1.0 + t)).astype(BF16)

    row = pl.BlockSpec((tr, W), lambda i: (i, 0))
    return _call(
        body, name=name, grid=(S // tr,),
        in_specs=[row, pl.BlockSpec((tr, W), lambda i: (i, ub)), pl.BlockSpec((1, W), lambda i: (0, 0))],
        out_specs=[row, row],
        out_shape=[jax.ShapeDtypeStruct((S, W), F32), jax.ShapeDtypeStruct((S, W), BF16)],
        compiler_params=_cparams(("parallel",)),
    )(yc, P, dskip)


def _s5_out_bwd(dyg, y, P, dskip, *, name):
    S, W = y.shape
    tr = _pick(S, 256)
    ub = 3 * FOX_WIDTH // W

    def body(dyg_ref, y_ref, u_ref, d_ref, dy_ref, du_ref, dd_ref):
        y_ = y_ref[...]
        inner = _GELU_C * (y_ + 0.044715 * y_ * y_ * y_)
        t = jnp.tanh(inner)
        dgelu = 0.5 * (1.0 + t) + 0.5 * y_ * (1.0 - t * t) * _GELU_C * (1.0 + 3.0 * 0.044715 * y_ * y_)
        dy = dyg_ref[...] * dgelu
        dy_ref[...] = dy.astype(BF16)
        du_ref[...] = d_ref[...] * dy
        part = jnp.sum(dy * u_ref[...], axis=0, keepdims=True)

        @pl.when(pl.program_id(0) == 0)
        def _():
            dd_ref[...] = part

        @pl.when(pl.program_id(0) > 0)
        def _():
            dd_ref[...] += part

    row = pl.BlockSpec((tr, W), lambda i: (i, 0))
    vec = pl.BlockSpec((1, W), lambda i: (0, 0))
    return _call(
        body, name=name, grid=(S // tr,),
        in_specs=[row, row, pl.BlockSpec((tr, W), lambda i: (i, ub)), vec],
        out_specs=[row, row, vec],
        out_shape=[jax.ShapeDtypeStruct((S, W), BF16), jax.ShapeDtypeStruct((S, W), F32),
                   jax.ShapeDtypeStruct((1, W), F32)],
        compiler_params=_cparams(("arbitrary",)),
    )(dyg, y, P, dskip)


def _glu_fwd(z, *, name):
    S, W2 = z.shape
    W = W2 // 2
    tr = _pick(S, 256)

    def body(z1_ref, z2_ref, o_ref):
        o_ref[...] = (z1_ref[...] * jax.nn.sigmoid(z2_ref[...])).astype(BF16)

    return _call(
        body, name=name, grid=(S // tr,),
        in_specs=[pl.BlockSpec((tr, W), lambda i: (i, 0)), pl.BlockSpec((tr, W), lambda i: (i, 1))],
        out_specs=pl.BlockSpec((tr, W), lambda i: (i, 0)),
        out_shape=jax.ShapeDtypeStruct((S, W), BF16),
        compiler_params=_cparams(("parallel",)),
    )(z, z)


def _glu_bwd(z, dcat, *, name):
    S, W2 = z.shape
    W = W2 // 2
    tr = _pick(S, 256)

    def body(z1_ref, z2_ref, d_ref, dz1_ref, dz2_ref):
        sg = jax.nn.sigmoid(z2_ref[...])
        d = d_ref[...]
        dz1_ref[...] = (d * sg).astype(BF16)
        dz2_ref[...] = (d * z1_ref[...] * sg * (1.0 - sg)).astype(BF16)

    lo = pl.BlockSpec((tr, W), lambda i: (i, 0))
    hi = pl.BlockSpec((tr, W), lambda i: (i, 1))
    dz1, dz2 = _call(
        body, name=name, grid=(S // tr,), in_specs=[lo, hi, hi], out_specs=[lo, lo],
        out_shape=[jax.ShapeDtypeStruct((S, W), BF16)] * 2,
        compiler_params=_cparams(("parallel",)),
    )(z, z, dcat)
    return jnp.concatenate([dz1, dz2], axis=1)


ACT_ROWS = 16
ACT_COLS = 256


def _shift_down(cur, prev, k, row):
    return jnp.where(row >= k, pltpu.roll(cur, k, 0), pltpu.roll(prev, k, 0))


def _shift_up(cur, nxt, k, row):
    n = cur.shape[0]
    return jnp.where(row < n - k, pltpu.roll(cur, n - k, 0), pltpu.roll(nxt, n - k, 0))


def _act_fwd(h, cw, cb, *, name):
    _, S, FP = h.shape
    tr = _pick(S, 256)
    hb = tr // ACT_ROWS
    nq = tr // ACT_ROWS

    def body(g_ref, gh_ref, v_ref, vh_ref, wg_ref, wv_ref, bg_ref, bv_ref, a_ref, hc_ref):
        first = pl.program_id(1) == 0
        for c0 in range(0, FP, ACT_COLS):
            cw_ = min(ACT_COLS, FP - c0)
            cols = pl.ds(c0, cw_)
            rw = lax.broadcasted_iota(jnp.int32, (ACT_ROWS, cw_), 0)
            wg = [wg_ref[pl.ds(k, 1), cols] for k in range(3)]
            wv = [wv_ref[pl.ds(k, 1), cols] for k in range(3)]
            bg, bv = bg_ref[:, cols], bv_ref[:, cols]
            halo_g = jnp.where(first, 0.0, gh_ref[:, cols])
            halo_v = jnp.where(first, 0.0, vh_ref[:, cols])

            def chunk(q, _):
                rows = pl.ds(pl.multiple_of(q * ACT_ROWS, ACT_ROWS), ACT_ROWS)
                before = pl.ds(pl.multiple_of(jnp.maximum(q - 1, 0) * ACT_ROWS, ACT_ROWS), ACT_ROWS)
                g, v = g_ref[rows, cols], v_ref[rows, cols]
                gp = jnp.where(q > 0, g_ref[before, cols], halo_g)
                vp = jnp.where(q > 0, v_ref[before, cols], halo_v)
                cg = bg + wg[2] * g + wg[1] * _shift_down(g, gp, 1, rw) + wg[0] * _shift_down(g, gp, 2, rw)
                cv = bv + wv[2] * v + wv[1] * _shift_down(v, vp, 1, rw) + wv[0] * _shift_down(v, vp, 2, rw)
                a_ref[rows, cols] = (cg * jax.nn.sigmoid(cg) * cv).astype(BF16)
                hc_ref[0, rows, cols] = cg
                hc_ref[1, rows, cols] = cv
                return 0

            lax.fori_loop(0, nq, chunk, 0, unroll=2)

    def main(off):
        return pl.BlockSpec((None, tr, FP), lambda j, i: (j + off, i, 0))

    def halo(off):
        return pl.BlockSpec((None, ACT_ROWS, FP), lambda j, i: (j + off, jnp.maximum(i * hb - 1, 0), 0))

    def wspec(off):
        return pl.BlockSpec((None, 3, FP), lambda j, i: (j + off, 0, 0))

    def bspec(off):
        return pl.BlockSpec((None, 1, FP), lambda j, i: (j + off, 0, 0))

    cb3 = cb.reshape(4, 1, FP)
    return _call(
        body, name=name, grid=(2, S // tr),
        in_specs=[main(0), halo(0), main(2), halo(2), wspec(0), wspec(2), bspec(0), bspec(2)],
        out_specs=[pl.BlockSpec((None, tr, FP), lambda j, i: (j, i, 0)),
                   pl.BlockSpec((None, 2, tr, FP), lambda j, i: (j, 0, i, 0))],
        out_shape=[jax.ShapeDtypeStruct((2, S, FP), BF16), jax.ShapeDtypeStruct((2, 2, S, FP), F32)],
        compiler_params=_cparams(("parallel", "parallel")),
    )(h, h, h, h, cw, cw, cb3, cb3)


def _act_bwd(h, hc, da, cw, *, name):
    _, S, FP = h.shape
    tr = _pick(S, 256)
    nq = tr // ACT_ROWS
    nr = S // tr
    half = ACT_ROWS // 2

    def fold(x):
        return x[:half] + x[half:]

    def body(g_ref, v_ref, hc_ref, da_ref, wg_ref, wv_ref,
             dh_ref, dwg_ref, dwv_ref, dbg_ref, dbv_ref, carry_g, carry_v):
        i = pl.program_id(1)
        bottom = i == 0
        for c0 in range(0, FP, ACT_COLS):
            cw_ = min(ACT_COLS, FP - c0)
            cols = pl.ds(c0, cw_)
            rw = lax.broadcasted_iota(jnp.int32, (ACT_ROWS, cw_), 0)
            wg = [wg_ref[pl.ds(k, 1), cols] for k in range(3)]
            wv = [wv_ref[pl.ds(k, 1), cols] for k in range(3)]
            after_g = jnp.where(bottom, 0.0, carry_g[:, cols])
            after_v = jnp.where(bottom, 0.0, carry_v[:, cols])

            def chunk(s, carry):
                ng, nv, acc = carry[0], carry[1], carry[2:]
                q = nq - 1 - s
                rows = pl.ds(pl.multiple_of(q * ACT_ROWS, ACT_ROWS), ACT_ROWS)
                g, v = g_ref[rows, cols], v_ref[rows, cols]
                cg, cv = hc_ref[0, rows, cols], hc_ref[1, rows, cols]
                sg = jax.nn.sigmoid(cg)
                d = da_ref[rows, cols]
                dcg = d * cv * sg * (1.0 + cg * (1.0 - sg))
                dcv = d * cg * sg
                ug1, ug2 = _shift_up(dcg, ng, 1, rw), _shift_up(dcg, ng, 2, rw)
                uv1, uv2 = _shift_up(dcv, nv, 1, rw), _shift_up(dcv, nv, 2, rw)
                dh_ref[0, rows, cols] = (wg[2] * dcg + wg[1] * ug1 + wg[0] * ug2).astype(BF16)
                dh_ref[1, rows, cols] = (wv[2] * dcv + wv[1] * uv1 + wv[0] * uv2).astype(BF16)
                terms = (ug2 * g, ug1 * g, dcg * g, dcg, uv2 * v, uv1 * v, dcv * v, dcv)
                return (dcg, dcv) + tuple(a + fold(t) for a, t in zip(acc, terms))

            zero = jnp.zeros((half, cw_), F32)
            out = lax.fori_loop(0, nq, chunk, (after_g, after_v) + (zero,) * 8, unroll=2)
            carry_g[:, cols] = out[0]
            carry_v[:, cols] = out[1]
            sums = [jnp.sum(a, axis=0, keepdims=True) for a in out[2:]]

            @pl.when(bottom)
            def _():
                for k in range(3):
                    dwg_ref[pl.ds(k, 1), cols] = sums[k]
                    dwv_ref[pl.ds(k, 1), cols] = sums[4 + k]
                dbg_ref[:, cols] = sums[3]
                dbv_ref[:, cols] = sums[7]

            @pl.when(jnp.logical_not(bottom))
            def _():
                for k in range(3):
                    dwg_ref[pl.ds(k, 1), cols] += sums[k]
                    dwv_ref[pl.ds(k, 1), cols] += sums[4 + k]
                dbg_ref[:, cols] += sums[3]
                dbv_ref[:, cols] += sums[7]

    def main(off):
        return pl.BlockSpec((None, tr, FP), lambda j, i: (j + off, nr - 1 - i, 0))

    def wspec(off):
        return pl.BlockSpec((None, 3, FP), lambda j, i: (j + off, 0, 0))

    bspec = pl.BlockSpec((None, 1, FP), lambda j, i: (j, 0, 0))
    pair = pl.BlockSpec((None, 2, tr, FP), lambda j, i: (j, 0, nr - 1 - i, 0))
    dh, dwg, dwv, dbg, dbv = _call(
        body, name=name, grid=(2, nr),
        in_specs=[main(0), main(2), pair, main(0), wspec(0), wspec(2)],
        out_specs=[pair, wspec(0), wspec(0), bspec, bspec],
        out_shape=[jax.ShapeDtypeStruct((2, 2, S, FP), BF16)]
        + [jax.ShapeDtypeStruct((2, 3, FP), F32)] * 2 + [jax.ShapeDtypeStruct((2, 1, FP), F32)] * 2,
        scratch_shapes=[pltpu.VMEM((ACT_ROWS, FP), F32), pltpu.VMEM((ACT_ROWS, FP), F32)],
        compiler_params=_cparams(("parallel", "arbitrary")),
    )(h, h, hc, da, cw, cw)
    return (dh.reshape(4, S, FP), jnp.concatenate([dwg, dwv], axis=0), jnp.concatenate([dbg, dbv], axis=0))


def _rope_tables(posf, *, name, after=()):
    S = posf.shape[0]
    half = ROPE_DIM // 2
    d = np.arange(LANE) % SWA_HEAD_DIM
    invf = np.where(d < ROPE_DIM, ROPE_THETA ** (-(d % half).astype(np.float64) / half), 0.0).astype(np.float32)
    m_rot = (d < ROPE_DIM).astype(np.float32)
    m_a = (d < half).astype(np.float32)
    m_b = ((d >= half) & (d < ROPE_DIM)).astype(np.float32)
    consts = jnp.asarray(np.stack([invf, m_rot, m_a, m_b] + [np.zeros(LANE, np.float32)] * 4))

    def body(p_ref, k_ref, c_ref, sa_ref, sb_ref):
        k = k_ref[...]
        ang = p_ref[...] * k[0:1]
        co, si = jnp.cos(ang), jnp.sin(ang)
        c_ref[...] = k[1:2] * co + (1.0 - k[1:2])
        sa_ref[...] = -k[2:3] * si
        sb_ref[...] = k[3:4] * si

    full = pl.BlockSpec((S, LANE), lambda: (0, 0))
    return _call(
        body, after=after, name=name,
        in_specs=[pl.BlockSpec((S, 1), lambda: (0, 0)), pl.BlockSpec((8, LANE), lambda: (0, 0))],
        out_specs=[full] * 3, out_shape=[jax.ShapeDtypeStruct((S, LANE), F32)] * 3,
    )(posf, consts)


def _rope(xv, tabs_refs, width, inverse):
    rep = width // LANE
    c, sa, sb = (jnp.tile(t[...], (1, rep)) for t in tabs_refs)
    if not inverse:
        return xv * c + pltpu.roll(xv, width - 8, 1) * sa + pltpu.roll(xv, 8, 1) * sb
    return xv * c + pltpu.roll(xv * sa, 8, 1) + pltpu.roll(xv * sb, width - 8, 1)


def _to_heads(x, tabs, *, col0, width, rotate, name, out_dtype):
    S = x.shape[0]
    tr = _pick(S, 256)
    nh = width // SWA_HEAD_DIM
    cb = col0 // width

    def body(x_ref, c_ref, sa_ref, sb_ref, o_ref):
        xv = x_ref[...].astype(F32)
        if rotate:
            xv = _rope(xv, (c_ref, sa_ref, sb_ref), width, False)
        for h in range(nh):
            o_ref[h] = xv[:, h * SWA_HEAD_DIM:(h + 1) * SWA_HEAD_DIM].astype(out_dtype)

    tab = pl.BlockSpec((tr, LANE), lambda i: (i, 0))
    return _call(
        body, name=name, grid=(S // tr,),
        in_specs=[pl.BlockSpec((tr, width), lambda i: (i, cb)), tab, tab, tab],
        out_specs=pl.BlockSpec((nh, tr, SWA_HEAD_DIM), lambda i: (0, i, 0)),
        out_shape=jax.ShapeDtypeStruct((nh, S, SWA_HEAD_DIM), out_dtype),
        compiler_params=_cparams(("parallel",)),
    )(x, *tabs)


def _from_heads(x3, tabs, *, rotate_back, name, out_dtype, skip_rows=0):
    nh = x3.shape[0]
    S = x3.shape[1] - skip_rows
    width = nh * SWA_HEAD_DIM
    tr = _pick(S, 256) if skip_rows == 0 else skip_rows
    off = skip_rows // tr

    def body(x_ref, c_ref, sa_ref, sb_ref, o_ref):
        xv = jnp.concatenate([x_ref[h].astype(F32) for h in range(nh)], axis=1)
        if rotate_back:
            xv = _rope(xv, (c_ref, sa_ref, sb_ref), width, True)
        o_ref[...] = xv.astype(out_dtype)

    tab = pl.BlockSpec((tr, LANE), lambda i: (i, 0))
    return _call(
        body, name=name, grid=(S // tr,),
        in_specs=[pl.BlockSpec((nh, tr, SWA_HEAD_DIM), lambda i: (0, i + off, 0)), tab, tab, tab],
        out_specs=pl.BlockSpec((tr, width), lambda i: (i, 0)),
        out_shape=jax.ShapeDtypeStruct((S, width), out_dtype),
        compiler_params=_cparams(("parallel",)),
    )(x3, *tabs)


def _swa_mask(n):
    rows = SWA_GROUPS * SWA_WINDOW
    qi = lax.broadcasted_iota(jnp.int32, (rows, 2 * SWA_WINDOW), 0) & (SWA_WINDOW - 1)
    kj = lax.broadcasted_iota(jnp.int32, (rows, 2 * SWA_WINDOW), 1)
    rel = SWA_WINDOW + qi - kj
    return (rel >= 0) & (rel < SWA_WINDOW) & ((n > 0) | (kj >= SWA_WINDOW))


def _swa_fwd(qT, kT, vT, sink_rows, *, name):
    S = qT.shape[1]
    W, G, Dh = SWA_WINDOW, SWA_GROUPS, SWA_HEAD_DIM
    nb = S // W
    scale = 1.0 / math.sqrt(Dh)

    def body(q_ref, kp_ref, kc_ref, vp_ref, vc_ref, s_ref, o_ref, l_ref):
        n = pl.program_id(1)
        q = q_ref[...].reshape(G * W, Dh)
        kk = jnp.concatenate([kp_ref[...], kc_ref[...]], axis=0)
        vv = jnp.concatenate([vp_ref[...], vc_ref[...]], axis=0)
        s = lax.dot_general(q, kk, (((1,), (1,)), ((), ())), preferred_element_type=F32) * scale
        s = jnp.where(_swa_mask(n), s, -1e30)
        sink = s_ref[...]
        m = jnp.maximum(jnp.max(s, axis=-1, keepdims=True), sink)
        e = jnp.exp(s - m)
        den = jnp.sum(e, axis=-1, keepdims=True) + jnp.exp(sink - m)
        p = e / den
        o_ref[...] = jnp.dot(p.astype(BF16), vv, preferred_element_type=F32).reshape(G, W, Dh)
        l_ref[...] = (m + jnp.log(den)).reshape(G, W, 1)

    qs = pl.BlockSpec((G, W, Dh), lambda g, n: (g, n, 0))
    prev = pl.BlockSpec((None, W, Dh), lambda g, n: (g, jnp.maximum(n - 1, 0), 0))
    cur = pl.BlockSpec((None, W, Dh), lambda g, n: (g, n, 0))
    return _call(
        body, name=name, grid=(SWA_KV_HEADS, nb),
        in_specs=[qs, prev, cur, prev, cur, pl.BlockSpec((None, G * W, 1), lambda g, n: (g, 0, 0))],
        out_specs=[qs, pl.BlockSpec((G, W, 1), lambda g, n: (g, n, 0))],
        out_shape=[jax.ShapeDtypeStruct((SWA_HEADS, S, Dh), F32), jax.ShapeDtypeStruct((SWA_HEADS, S, 1), F32)],
        compiler_params=_cparams(("parallel", "parallel")),
    )(qT, kT, kT, vT, vT, sink_rows)


def _swa_bwd(qT, kT, vT, sink_rows, oT, L, doT, *, name):
    S = qT.shape[1]
    W, G, Dh = SWA_WINDOW, SWA_GROUPS, SWA_HEAD_DIM
    nb = S // W
    scale = 1.0 / math.sqrt(Dh)

    def body(q_ref, kp_ref, kc_ref, vp_ref, vc_ref, s_ref, o_ref, l_ref, do_ref,
             dq_ref, dk_ref, dv_ref, ds_ref):
        n = pl.program_id(1)
        q = q_ref[...].reshape(G * W, Dh)
        kk = jnp.concatenate([kp_ref[...], kc_ref[...]], axis=0)
        vv = jnp.concatenate([vp_ref[...], vc_ref[...]], axis=0)
        s = lax.dot_general(q, kk, (((1,), (1,)), ((), ())), preferred_element_type=F32) * scale
        lrow = l_ref[...].reshape(G * W, 1)
        p = jnp.where(_swa_mask(n), jnp.exp(s - lrow), 0.0)
        do = do_ref[...].reshape(G * W, Dh)
        do_bf = do.astype(BF16)
        dp = lax.dot_general(do_bf, vv, (((1,), (1,)), ((), ())), preferred_element_type=F32)
        delta = jnp.sum(do * o_ref[...].reshape(G * W, Dh), axis=-1, keepdims=True)
        dsc = p * (dp - delta)
        ds_bf = dsc.astype(BF16)
        dq_ref[...] = (jnp.dot(ds_bf, kk, preferred_element_type=F32) * scale).astype(BF16).reshape(G, W, Dh)
        dkk = lax.dot_general(ds_bf, q, (((0,), (0,)), ((), ())), preferred_element_type=F32) * scale
        dvv = lax.dot_general(p.astype(BF16), do_bf, (((0,), (0,)), ((), ())), preferred_element_type=F32)
        dsk = -jnp.exp(s_ref[...] - lrow) * delta
        dsk = jnp.broadcast_to(jnp.sum(dsk.reshape(G, W, 1), axis=1), (G, LANE))

        @pl.when(n == 0)
        def _():
            dk_ref[...] = jnp.zeros_like(dk_ref)
            dv_ref[...] = jnp.zeros_like(dv_ref)
            ds_ref[...] = jnp.zeros_like(ds_ref)

        rows = pl.ds(pl.multiple_of(n * W, W), 2 * W)
        dk_ref[rows, :] += dkk
        dv_ref[rows, :] += dvv
        ds_ref[...] += dsk

    qs = pl.BlockSpec((G, W, Dh), lambda g, n: (g, n, 0))
    prev = pl.BlockSpec((None, W, Dh), lambda g, n: (g, jnp.maximum(n - 1, 0), 0))
    cur = pl.BlockSpec((None, W, Dh), lambda g, n: (g, n, 0))
    lsp = pl.BlockSpec((G, W, 1), lambda g, n: (g, n, 0))
    kvo = pl.BlockSpec((None, S + W, Dh), lambda g, n: (g, 0, 0))
    return _call(
        body, name=name, grid=(SWA_KV_HEADS, nb),
        in_specs=[qs, prev, cur, prev, cur, pl.BlockSpec((None, G * W, 1), lambda g, n: (g, 0, 0)), qs, lsp, qs],
        out_specs=[qs, kvo, kvo, pl.BlockSpec((None, G, LANE), lambda g, n: (g, 0, 0))],
        out_shape=[jax.ShapeDtypeStruct((SWA_HEADS, S, Dh), BF16),
                   jax.ShapeDtypeStruct((SWA_KV_HEADS, S + W, Dh), F32),
                   jax.ShapeDtypeStruct((SWA_KV_HEADS, S + W, Dh), F32),
                   jax.ShapeDtypeStruct((SWA_KV_HEADS, G, LANE), F32)],
        compiler_params=_cparams(("parallel", "arbitrary")),
    )(qT, kT, kT, vT, vT, sink_rows, oT, L, doT)


def _adamw(w, g, m, v, *, name, tr=128, by_cols=False):
    L, R, C = w.shape
    split = isinstance(g, (list, tuple))
    HR, HC = _half_shape(R, C, by_cols) if split else (R, C)
    tr, tc = _tile2d(HR, HC, tr)
    nr, nc = HR // tr, HC // tc
    c1 = 1.0 / (1.0 - ADAM_B1 ** ADAM_STEP)
    c2 = 1.0 / (1.0 - ADAM_B2 ** ADAM_STEP)
    ng = 2 * L if split else 1

    def body(c_ref, *refs):
        w_ref, g_refs, (m_ref, v_ref, go_ref, d_ref, mo_ref, vo_ref) = refs[0], refs[1:1 + ng], refs[1 + ng:]
        if split:
            mine = pl.program_id(1) == c_ref[0]
            g_ = jnp.where(mine, g_refs[0][...], g_refs[1][...])
            for l in range(1, L):
                g_ = jnp.where(pl.program_id(0) == l,
                               jnp.where(mine, g_refs[2 * l][...], g_refs[2 * l + 1][...]), g_)
        else:
            g_ = g_refs[0][...]
        mn = ADAM_B1 * m_ref[...] + (1.0 - ADAM_B1) * g_
        vn = ADAM_B2 * v_ref[...] + (1.0 - ADAM_B2) * (g_ * g_)
        go_ref[...] = g_
        mo_ref[...] = mn
        vo_ref[...] = vn
        d_ref[...] = -ADAM_LR * ((mn * c1) / (jnp.sqrt(vn * c2) + ADAM_EPS) + ADAM_WD * w_ref[...])

    def whole(l, hf, i, j, c):
        return (l, i, hf * nc + j) if by_cols else (l, hf * nr + i, j)

    def half(layer, own):
        def index(l, hf, i, j, c):
            used = (l == layer) & ((hf == c[0]) if own else (hf != c[0]))
            return jnp.where(used, i, 0), jnp.where(used, j, 0)
        return pl.BlockSpec((tr, tc), index)

    row = pl.BlockSpec((None, tr, tc), whole)
    gs = [h for pair in g for h in pair] if split else [g]
    g_specs = [half(l, own) for l in range(L) for own in (True, False)] if split else [row]
    core = lax.axis_index("c").astype(jnp.int32).reshape(1)
    return _call(
        body, name=name,
        grid_spec=pltpu.PrefetchScalarGridSpec(
            num_scalar_prefetch=1, grid=(L, 2 if split else 1, nr, nc),
            in_specs=[row] + g_specs + [row, row], out_specs=[row] * 4),
        out_shape=[jax.ShapeDtypeStruct((L, R, C), F32)] * 4,
        compiler_params=_cparams(("parallel",) * 4),
    )(core, w, *gs, m, v)


def _adamw_half(w, g, m, v, *, name, own, prev=None, tr=128, by_cols=False):
    L, R, C = w.shape
    HR, HC = _half_shape(R, C, by_cols)
    tr, tc = _tile2d(HR, HC, tr)
    nr, nc = HR // tr, HC // tc
    c1 = 1.0 / (1.0 - ADAM_B1 ** ADAM_STEP)
    c2 = 1.0 / (1.0 - ADAM_B2 ** ADAM_STEP)

    def body(c_ref, *refs):
        w_ref, g_refs, m_ref, v_ref = refs[0], refs[1:1 + L], refs[1 + L], refs[2 + L]
        go_ref, d_ref, mo_ref, vo_ref = refs[-4:]
        g_ = g_refs[0][...]
        for l in range(1, L):
            g_ = jnp.where(pl.program_id(0) == l, g_refs[l][...], g_)
        mn = ADAM_B1 * m_ref[...] + (1.0 - ADAM_B1) * g_
        vn = ADAM_B2 * v_ref[...] + (1.0 - ADAM_B2) * (g_ * g_)
        go_ref[...] = g_
        mo_ref[...] = mn
        vo_ref[...] = vn
        d_ref[...] = -ADAM_LR * ((mn * c1) / (jnp.sqrt(vn * c2) + ADAM_EPS) + ADAM_WD * w_ref[...])

    def whole(l, i, j, c):
        hf = c[0] if own else 1 - c[0]
        return (l, i, hf * nc + j) if by_cols else (l, hf * nr + i, j)

    def layer_half(layer):
        def index(l, i, j, c):
            return jnp.where(l == layer, i, 0), jnp.where(l == layer, j, 0)
        return pl.BlockSpec((tr, tc), index)

    row = pl.BlockSpec((None, tr, tc), whole)
    core = lax.axis_index("c").astype(jnp.int32).reshape(1)
    prev = list(prev) if prev is not None else []
    return _call(
        body, name=name,
        grid_spec=pltpu.PrefetchScalarGridSpec(
            num_scalar_prefetch=1, grid=(L, nr, nc),
            in_specs=[row] + [layer_half(l) for l in range(L)] + [row, row] + [ANY] * len(prev),
            out_specs=[row] * 4),
        out_shape=[jax.ShapeDtypeStruct((L, R, C), F32)] * 4,
        input_output_aliases={4 + L + k: k for k in range(len(prev))},
        compiler_params=_cparams(("parallel",) * 3),
    )(core, w, *g, m, v, *prev)


def _sum2_halves(g4, s4, by_cols, *, name):
    n, R, C = g4.shape
    HR, HC = _half_shape(R, C, by_cols)
    tr, tc = _tile2d(HR, HC, budget=1024 * 1024)
    nr, nc = HR // tr, HC // tc
    core = lax.axis_index("c").astype(jnp.int32).reshape(1)

    def body(c_ref, g_ref, s_ref, o_ref):
        o_ref[...] = (g_ref[...].astype(F32) + s_ref[...].astype(F32)).astype(BF16)

    def mine(k, i, j, c):
        return (k, i, c[0] * nc + j) if by_cols else (k, c[0] * nr + i, j)

    blk = pl.BlockSpec((None, tr, tc), lambda k, i, j, c: (k, i, j))
    return _call(
        body, name=name,
        grid_spec=pltpu.PrefetchScalarGridSpec(
            num_scalar_prefetch=1, grid=(n, nr, nc),
            in_specs=[pl.BlockSpec((None, tr, tc), mine), blk], out_specs=blk),
        out_shape=jax.ShapeDtypeStruct((n, HR, HC), BF16),
        compiler_params=_cparams(("parallel", "parallel", "parallel")),
    )(core, g4, s4)


def _rowsum(parts, *, name, out_dtype=F32):
    n, R, C = parts.shape
    tr, tc = _tile2d(R, C, budget=512 * 1024)

    def body(p_ref, o_ref):
        acc = p_ref[0].astype(F32)
        for i in range(1, n):
            acc = acc + p_ref[i].astype(F32)
        o_ref[...] = acc.astype(out_dtype)

    return _call(
        body, name=name, grid=(R // tr, C // tc),
        in_specs=[pl.BlockSpec((n, tr, tc), lambda i, j: (0, i, j))],
        out_specs=pl.BlockSpec((tr, tc), lambda i, j: (i, j)),
        out_shape=jax.ShapeDtypeStruct((R, C), out_dtype),
        compiler_params=_cparams(("parallel", "parallel")),
    )(parts)


def _where_am_i():
    x, y, c = lax.axis_index("x"), lax.axis_index("y"), lax.axis_index("c")
    chips = [(1 - x, y), (x, 1 - y), (1 - x, 1 - y)]
    return x, y, c, chips


def _half_idx(rows, cols, by_cols, which):
    if by_cols:
        hc = cols // 2
        return (slice(None), pl.ds(pl.multiple_of(which * hc, LANE), hc))
    hr = rows // 2
    return (pl.ds(pl.multiple_of(which * hr, 16), hr), slice(None))


def _half_shape(rows, cols, by_cols):
    return (rows, cols // 2) if by_cols else (rows // 2, cols)


HBM_SPEC = pl.BlockSpec(memory_space=pltpu.HBM)
SEM_SPEC = pl.BlockSpec(memory_space=pltpu.SEMAPHORE)
DATAFLOW = pltpu.SideEffectType.DATAFLOW_SIDE_EFFECTING


def _chip_exchange_refs(kind, shards_shape, by_cols, src, land, i, chip_k, c, me):
    if kind == 'gather':
        half = _half_idx(*shards_shape, by_cols, c)
        return src.at[half], land.at[(me,) + half], land.at[(chip_k,) + half]
    return src.at[chip_k], land.at[me], land.at[chip_k]


def _chip_exchange_start(kind, srcs, by_cols, *, name, after=()):
    n = len(srcs)
    land_shapes = [((N_CHIPS,) + s.shape) if kind == 'gather' else s.shape for s in srcs]

    def body(*refs):
        src_refs, land_refs = refs[:n], refs[n:2 * n]
        send, recv = refs[2 * n + len(after)], refs[2 * n + len(after) + 1]
        token = refs[-1]
        x, y, c, chips = _where_am_i()
        me = 2 * x + y
        for i in range(n):
            for k, (px, py) in enumerate(chips):
                s, d, _ = _chip_exchange_refs(kind, srcs[i].shape, by_cols[i], src_refs[i], land_refs[i], i,
                                              2 * px + py, c, me)
                pltpu.make_async_remote_copy(src_ref=s, dst_ref=d, send_sem=send.at[3 * i + k],
                                             recv_sem=recv.at[3 * i + k], device_id=(px, py, c),
                                             device_id_type=MESH).start()
        token[...] = jnp.zeros_like(token)

    lands = [pltpu.with_memory_space_constraint(lax.empty(sh, s.dtype), pltpu.HBM) for sh, s in zip(land_shapes, srcs)]
    outs = _call(
        body, name=name,
        out_shape=(pltpu.SemaphoreType.DMA((3 * n,)), pltpu.SemaphoreType.DMA((3 * n,)),
                   *[pltpu.HBM(s.shape, s.dtype) for s in srcs],
                   *[pltpu.HBM(sh, s.dtype) for sh, s in zip(land_shapes, srcs)],
                   jax.ShapeDtypeStruct((8, LANE), F32)),
        in_specs=[HBM_SPEC] * (2 * n) + [ANY] * len(after),
        out_specs=(SEM_SPEC, SEM_SPEC, *([HBM_SPEC] * (2 * n)), pl.BlockSpec(memory_space=pltpu.VMEM)),
        input_output_aliases={j: 2 + j for j in range(2 * n)},
        compiler_params=pltpu.CompilerParams(has_side_effects=DATAFLOW),
    )(*[pltpu.with_memory_space_constraint(s, pltpu.HBM) for s in srcs], *lands, *after)
    return outs[0], outs[1], list(outs[2:2 + n]), list(outs[2 + n:2 + 2 * n]), outs[-1]


def _chip_exchange_wait(kind, send, recv, srcs, lands, by_cols, after, *, name):
    n = len(srcs)

    def body(*refs):
        src_refs, land_refs = refs[:n], refs[n:2 * n]
        send_r, recv_r = refs[2 * n], refs[2 * n + 1]
        x, y, c, chips = _where_am_i()
        me = 2 * x + y
        for i in range(n):
            for k, (px, py) in enumerate(chips):
                s, _, d = _chip_exchange_refs(kind, srcs[i].shape, by_cols[i], src_refs[i], land_refs[i], i,
                                              2 * px + py, c, me)
                cp = pltpu.make_async_remote_copy(src_ref=s, dst_ref=d, send_sem=send_r.at[3 * i + k],
                                                  recv_sem=recv_r.at[3 * i + k], device_id=(px, py, c),
                                                  device_id_type=MESH)
                cp.wait_send()
                cp.wait_recv()

    outs = _call(
        body, name=name,
        out_shape=(*[pltpu.HBM(s.shape, s.dtype) for s in srcs], *[pltpu.HBM(l.shape, l.dtype) for l in lands]),
        in_specs=[HBM_SPEC] * (2 * n) + [SEM_SPEC, SEM_SPEC] + [ANY] * len(after),
        out_specs=tuple([HBM_SPEC] * (2 * n)),
        input_output_aliases={j: j for j in range(2 * n)},
        compiler_params=pltpu.CompilerParams(has_side_effects=DATAFLOW),
    )(*srcs, *lands, send, recv, *after)
    return list(outs[:n]), list(outs[n:])


def _sibling_halves_start(grads, by_cols, *, name, after=()):
    n = len(grads)
    land_shapes = [(N_CHIPS,) + _half_shape(*g.shape[1:], bc) for g, bc in zip(grads, by_cols)]

    def body(*refs):
        src_refs, land_refs = refs[:n], refs[n:2 * n]
        send, recv = refs[2 * n + len(after)], refs[2 * n + len(after) + 1]
        token = refs[-1]
        x, y, c, _ = _where_am_i()
        for i in range(n):
            src = src_refs[i].at[(slice(None),) + _half_idx(*grads[i].shape[1:], by_cols[i], 1 - c)]
            pltpu.make_async_remote_copy(src_ref=src, dst_ref=land_refs[i], send_sem=send.at[i], recv_sem=recv.at[i],
                                         device_id=(x, y, 1 - c), device_id_type=MESH).start()
        token[...] = jnp.zeros_like(token)

    lands = [pltpu.with_memory_space_constraint(lax.empty(sh, g.dtype), pltpu.HBM) for sh, g in zip(land_shapes, grads)]
    outs = _call(
        body, name=name,
        out_shape=(pltpu.SemaphoreType.DMA((n,)), pltpu.SemaphoreType.DMA((n,)),
                   *[pltpu.HBM(g.shape, g.dtype) for g in grads],
                   *[pltpu.HBM(sh, g.dtype) for sh, g in zip(land_shapes, grads)],
                   jax.ShapeDtypeStruct((8, LANE), F32)),
        in_specs=[HBM_SPEC] * (2 * n) + [ANY] * len(after),
        out_specs=(SEM_SPEC, SEM_SPEC, *([HBM_SPEC] * (2 * n)), pl.BlockSpec(memory_space=pltpu.VMEM)),
        input_output_aliases={j: 2 + j for j in range(2 * n)},
        compiler_params=pltpu.CompilerParams(has_side_effects=DATAFLOW),
    )(*[pltpu.with_memory_space_constraint(g, pltpu.HBM) for g in grads], *lands, *after)
    return outs[0], outs[1], list(outs[2:2 + n]), list(outs[2 + n:2 + 2 * n]), outs[-1]


def _sibling_halves_wait(send, recv, grads, lands, by_cols, after, *, name):
    n = len(grads)

    def body(*refs):
        src_refs, land_refs = refs[:n], refs[n:2 * n]
        send_r, recv_r = refs[2 * n], refs[2 * n + 1]
        x, y, c, _ = _where_am_i()
        for i in range(n):
            src = src_refs[i].at[(slice(None),) + _half_idx(*grads[i].shape[1:], by_cols[i], 1 - c)]
            cp = pltpu.make_async_remote_copy(src_ref=src, dst_ref=land_refs[i], send_sem=send_r.at[i],
                                              recv_sem=recv_r.at[i], device_id=(x, y, 1 - c), device_id_type=MESH)
            cp.wait_send()
            cp.wait_recv()

    outs = _call(
        body, name=name,
        out_shape=(*[pltpu.HBM(g.shape, g.dtype) for g in grads], *[pltpu.HBM(l.shape, l.dtype) for l in lands]),
        in_specs=[HBM_SPEC] * (2 * n) + [SEM_SPEC, SEM_SPEC] + [ANY] * len(after),
        out_specs=tuple([HBM_SPEC] * (2 * n)),
        input_output_aliases={j: j for j in range(2 * n)},
        compiler_params=pltpu.CompilerParams(has_side_effects=DATAFLOW),
    )(*grads, *lands, send, recv, *after)
    return list(outs[:n]), list(outs[n:])


def _sibling_swap_start(arrs, *, name, after=()):
    n = len(arrs)

    def body(*refs):
        src_refs, land_refs = refs[:n], refs[n:2 * n]
        send, recv = refs[2 * n + len(after)], refs[2 * n + len(after) + 1]
        token = refs[-1]
        x, y, c, _ = _where_am_i()
        for i in range(n):
            pltpu.make_async_remote_copy(src_ref=src_refs[i], dst_ref=land_refs[i], send_sem=send.at[i],
                                         recv_sem=recv.at[i], device_id=(x, y, 1 - c), device_id_type=MESH).start()
        token[...] = jnp.zeros_like(token)

    lands = [pltpu.with_memory_space_constraint(lax.empty(a.shape, a.dtype), pltpu.HBM) for a in arrs]
    outs = _call(
        body, name=name,
        out_shape=(pltpu.SemaphoreType.DMA((n,)), pltpu.SemaphoreType.DMA((n,)),
                   *[pltpu.HBM(a.shape, a.dtype) for a in arrs] * 2, jax.ShapeDtypeStruct((8, LANE), F32)),
        in_specs=[HBM_SPEC] * (2 * n) + [ANY] * len(after),
        out_specs=(SEM_SPEC, SEM_SPEC, *([HBM_SPEC] * (2 * n)), pl.BlockSpec(memory_space=pltpu.VMEM)),
        input_output_aliases={j: 2 + j for j in range(2 * n)},
        compiler_params=pltpu.CompilerParams(has_side_effects=DATAFLOW),
    )(*[pltpu.with_memory_space_constraint(a, pltpu.HBM) for a in arrs], *lands, *after)
    return outs[0], outs[1], list(outs[2:2 + n]), list(outs[2 + n:2 + 2 * n]), outs[-1]


def _sibling_swap_wait(send, recv, arrs, lands, after, *, name):
    n = len(arrs)

    def body(*refs):
        src_refs, land_refs = refs[:n], refs[n:2 * n]
        send_r, recv_r = refs[2 * n], refs[2 * n + 1]
        x, y, c, _ = _where_am_i()
        for i in range(n):
            cp = pltpu.make_async_remote_copy(src_ref=src_refs[i], dst_ref=land_refs[i], send_sem=send_r.at[i],
                                              recv_sem=recv_r.at[i], device_id=(x, y, 1 - c), device_id_type=MESH)
            cp.wait_send()
            cp.wait_recv()

    outs = _call(
        body, name=name,
        out_shape=tuple(pltpu.HBM(a.shape, a.dtype) for a in list(arrs) + list(lands)),
        in_specs=[HBM_SPEC] * (2 * n) + [SEM_SPEC, SEM_SPEC] + [ANY] * len(after),
        out_specs=tuple([HBM_SPEC] * (2 * n)),
        input_output_aliases={j: j for j in range(2 * n)},
        compiler_params=pltpu.CompilerParams(has_side_effects=DATAFLOW),
    )(*arrs, *lands, send, recv, *after)
    return list(outs[:n]), list(outs[n:])


def _sibling_pass_gathered(lands, shard_shapes, by_cols, *, name):
    n = len(lands)

    def body(*refs):
        outs = refs[n:2 * n]
        send, recv = refs[2 * n:]
        x, y, c, chips = _where_am_i()
        sibling = (x, y, 1 - c)
        cps = []
        for i in range(n):
            for k, (px, py) in enumerate(chips):
                blk = outs[i].at[(2 * px + py,) + _half_idx(*shard_shapes[i], by_cols[i], c)]
                d = pltpu.make_async_remote_copy(src_ref=blk, dst_ref=blk, send_sem=send.at[i, k],
                                                 recv_sem=recv.at[i, k], device_id=sibling, device_id_type=MESH)
                d.start()
                cps.append(d)
        for i in range(n):
            for k, (px, py) in enumerate(chips):
                blk = outs[i].at[(2 * px + py,) + _half_idx(*shard_shapes[i], by_cols[i], 1 - c)]
                pltpu.make_async_remote_copy(src_ref=blk, dst_ref=blk, send_sem=send.at[i, k], recv_sem=recv.at[i, k],
                                             device_id=sibling, device_id_type=MESH).wait_recv()
        for d in cps:
            d.wait_send()

    return _call(
        body, name=name, in_specs=[ANY] * n, out_specs=[ANY] * n,
        out_shape=[jax.ShapeDtypeStruct(l.shape, l.dtype) for l in lands],
        input_output_aliases={j: j for j in range(n)},
        scratch_shapes=[pltpu.SemaphoreType.DMA((n, 3)), pltpu.SemaphoreType.DMA((n, 3))],
    )(*lands)


def _own_slot(lands, owns):
    me = 2 * lax.axis_index("x") + lax.axis_index("y")
    return [lax.dynamic_update_slice_in_dim(g, s, me, axis=0) for g, s in zip(lands, owns)]


def _sibling_send_halves(grads, by_cols, *, name):
    n = len(grads)

    def body(*refs):
        ins, outs = refs[:n], refs[n:2 * n]
        send, recv = refs[2 * n:]
        x, y, c, _ = _where_am_i()
        sibling = (x, y, 1 - c)
        cps = []
        for i in range(n):
            src = ins[i].at[(slice(None),) + _half_idx(*grads[i].shape[1:], by_cols[i], 1 - c)]
            d = pltpu.make_async_remote_copy(src_ref=src, dst_ref=outs[i], send_sem=send.at[i],
                                             recv_sem=recv.at[i], device_id=sibling, device_id_type=MESH)
            d.start()
            cps.append(d)
        for d in cps:
            d.wait()

    return _call(
        body, name=name, in_specs=[ANY] * n, out_specs=[ANY] * n,
        out_shape=[jax.ShapeDtypeStruct((N_CHIPS,) + _half_shape(*g.shape[1:], bc), g.dtype)
                   for g, bc in zip(grads, by_cols)],
        scratch_shapes=[pltpu.SemaphoreType.DMA((n,)), pltpu.SemaphoreType.DMA((n,))],
    )(*grads)


def _all_reduce_small(v, *, name, after=()):
    R, C = v.shape
    H = R // 2

    def body(v_ref, o_ref, sib, slots, send, recv):
        x, y, c, chips = _where_am_i()
        me = 2 * x + y
        sibling = (x, y, 1 - c)
        mine = pl.ds(pl.multiple_of(c * H, 8), H)
        other = pl.ds(pl.multiple_of((1 - c) * H, 8), H)

        def copy(k, src, dst, to):
            return pltpu.make_async_remote_copy(src_ref=src, dst_ref=dst, send_sem=send.at[k], recv_sem=recv.at[k],
                                                device_id=to, device_id_type=MESH)

        d = copy(0, v_ref.at[other], sib, sibling)
        d.start()
        d.wait()
        slots[me] = v_ref[mine, :] + sib[...]
        cps = [copy(1 + k, slots.at[me], slots.at[me], (px, py, c)) for k, (px, py) in enumerate(chips)]
        for d in cps:
            d.start()
        for k, (px, py) in enumerate(chips):
            blk = slots.at[2 * px + py]
            copy(1 + k, blk, blk, (px, py, c)).wait_recv()
        for d in cps:
            d.wait_send()
        o_ref[mine, :] = (slots[0] + slots[1]) + (slots[2] + slots[3])
        d = copy(4, o_ref.at[mine], o_ref.at[mine], sibling)
        d.start()
        copy(4, o_ref.at[other], o_ref.at[other], sibling).wait_recv()
        d.wait_send()

    vm = pl.BlockSpec(memory_space=pltpu.VMEM)
    return _call(
        body, after=after, name=name, in_specs=[vm], out_specs=vm,
        out_shape=jax.ShapeDtypeStruct((R, C), F32),
        scratch_shapes=[pltpu.VMEM((H, C), F32), pltpu.VMEM((N_CHIPS, H, C), F32),
                        pltpu.SemaphoreType.DMA((5,)), pltpu.SemaphoreType.DMA((5,))],
        compiler_params=pltpu.CompilerParams(vmem_limit_bytes=VMEM_LIMIT),
    )(v)


def _cols_from_shards(g):
    return jnp.transpose(g, (1, 0, 2)).reshape(g.shape[1], -1)


def _shards_from_cols(w):
    R, C4 = w.shape
    return jnp.transpose(w.reshape(R, N_CHIPS, C4 // N_CHIPS), (1, 0, 2))


def _pack(arrs):
    flat = []
    for a in arrs:
        f = a.reshape(-1).astype(F32)
        flat.append(jnp.pad(f, (0, _rup(f.shape[0], LANE) - f.shape[0])))
    v = jnp.concatenate(flat)
    rows = _rup(v.shape[0] // LANE, 16)
    v = jnp.pad(v, (0, rows * LANE - v.shape[0]))
    return v.reshape(rows, LANE)


def _unpack(v, shapes):
    flat = v.reshape(-1)
    out, off = [], 0
    for s in shapes:
        n = int(np.prod(s))
        out.append(flat[off:off + n].reshape(s))
        off += _rup(n, LANE)
    return out


def _ffn_fwd(x, Wup, Wdn, cw, cb, tag):
    h = _mm(x, Wup, 'nt', bmode='bo', tm=512, tn=4096, name=f"ffn_up_{tag}")
    a, hc = _act_fwd(h, cw, cb, name=f"ffn_act_{tag}")
    f = _mm(a, Wdn, 'nn', bmode='abr', tm=512, tn=1024, tk=4096, name=f"ffn_down_{tag}")
    return f, (h, hc), a


def _ffn_bwd(df, x, saved, a, Wup, Wdn, cw, tag):
    h, hc = saved
    da = _mm(df, Wdn, 'nt', bmode='bo', tm=512, tn=4096, name=f"ffn_da_{tag}")
    dWdn = _mm(a, df, 'tn', bmode='ao', tm=4096, tn=512, name=f"ffn_dwdn_{tag}", out_dtype=BF16)
    dh, dcw, dcb = _act_bwd(h, hc, da, cw, name=f"ffn_actb_{tag}")

    def shard_of(k):
        return (k % 2) * 2 + k // 2

    dx = _mm(dh, Wup, 'nn', bmode='abr', tm=512, tn=1024, tk=4096, name=f"ffn_dx_{tag}", b_map=shard_of)
    dWup = _mm(dh, x, 'tn', bmode='ao', tm=4096, tn=512, name=f"ffn_dwup_{tag}", out_dtype=BF16,
               o_map=shard_of)
    return dx, dWup, dWdn, dcw, dcb


def kernel(x, positions, ev_w_in, ev_b_f, ev_lambda_re, ev_lambda_im, ev_log_step, ev_ssm_b_re, ev_ssm_b_im, ev_ssm_c_re, ev_ssm_c_im, ev_ssm_d, ev_w_glu, ev_w_out, od_w_in, od_sinks, od_w_out, ln_mix_g, ln_mix_b, ffn_w_up, ffn_conv_w, ffn_conv_b, ffn_w_down, ln_ffn_g, ln_ffn_b, loss_target, m_ev_w_in, m_ev_b_f, m_ev_lambda_re, m_ev_lambda_im, m_ev_log_step, m_ev_ssm_b_re, m_ev_ssm_b_im, m_ev_ssm_c_re, m_ev_ssm_c_im, m_ev_ssm_d, m_ev_w_glu, m_ev_w_out, m_od_w_in, m_od_sinks, m_od_w_out, m_ln_mix_g, m_ln_mix_b, m_ffn_w_up, m_ffn_conv_w, m_ffn_conv_b, m_ffn_w_down, m_ln_ffn_g, m_ln_ffn_b, v_ev_w_in, v_ev_b_f, v_ev_lambda_re, v_ev_lambda_im, v_ev_log_step, v_ev_ssm_b_re, v_ev_ssm_b_im, v_ev_ssm_c_re, v_ev_ssm_c_im, v_ev_ssm_d, v_ev_w_glu, v_ev_w_out, v_od_w_in, v_od_sinks, v_od_w_out, v_ln_mix_g, v_ln_mix_b, v_ffn_w_up, v_ffn_conv_w, v_ffn_conv_b, v_ffn_w_down, v_ln_ffn_g, v_ln_ffn_b):
    W = dict(ev_w_in=ev_w_in, ev_b_f=ev_b_f, ev_lambda_re=ev_lambda_re, ev_lambda_im=ev_lambda_im, ev_log_step=ev_log_step, ev_ssm_b_re=ev_ssm_b_re, ev_ssm_b_im=ev_ssm_b_im, ev_ssm_c_re=ev_ssm_c_re, ev_ssm_c_im=ev_ssm_c_im, ev_ssm_d=ev_ssm_d, ev_w_glu=ev_w_glu, ev_w_out=ev_w_out, od_w_in=od_w_in, od_sinks=od_sinks, od_w_out=od_w_out, ln_mix_g=ln_mix_g, ln_mix_b=ln_mix_b, ffn_w_up=ffn_w_up, ffn_conv_w=ffn_conv_w, ffn_conv_b=ffn_conv_b, ffn_w_down=ffn_w_down, ln_ffn_g=ln_ffn_g, ln_ffn_b=ln_ffn_b)
    Mo = dict(ev_w_in=m_ev_w_in, ev_b_f=m_ev_b_f, ev_lambda_re=m_ev_lambda_re, ev_lambda_im=m_ev_lambda_im, ev_log_step=m_ev_log_step, ev_ssm_b_re=m_ev_ssm_b_re, ev_ssm_b_im=m_ev_ssm_b_im, ev_ssm_c_re=m_ev_ssm_c_re, ev_ssm_c_im=m_ev_ssm_c_im, ev_ssm_d=m_ev_ssm_d, ev_w_glu=m_ev_w_glu, ev_w_out=m_ev_w_out, od_w_in=m_od_w_in, od_sinks=m_od_sinks, od_w_out=m_od_w_out, ln_mix_g=m_ln_mix_g, ln_mix_b=m_ln_mix_b, ffn_w_up=m_ffn_w_up, ffn_conv_w=m_ffn_conv_w, ffn_conv_b=m_ffn_conv_b, ffn_w_down=m_ffn_w_down, ln_ffn_g=m_ln_ffn_g, ln_ffn_b=m_ln_ffn_b)
    Vo = dict(ev_w_in=v_ev_w_in, ev_b_f=v_ev_b_f, ev_lambda_re=v_ev_lambda_re, ev_lambda_im=v_ev_lambda_im, ev_log_step=v_ev_log_step, ev_ssm_b_re=v_ev_ssm_b_re, ev_ssm_b_im=v_ev_ssm_b_im, ev_ssm_c_re=v_ev_ssm_c_re, ev_ssm_c_im=v_ev_ssm_c_im, ev_ssm_d=v_ev_ssm_d, ev_w_glu=v_ev_w_glu, ev_w_out=v_ev_w_out, od_w_in=v_od_w_in, od_sinks=v_od_sinks, od_w_out=v_od_w_out, ln_mix_g=v_ln_mix_g, ln_mix_b=v_ln_mix_b, ffn_w_up=v_ffn_w_up, ffn_conv_w=v_ffn_conv_w, ffn_conv_b=v_ffn_conv_b, ffn_w_down=v_ffn_w_down, ln_ffn_g=v_ln_ffn_g, ln_ffn_b=v_ln_ffn_b)
    names = list(W.keys())
    big = ['ev_w_in', 'ev_w_glu', 'ev_w_out', 'od_w_in', 'od_w_out', 'ffn_w_up', 'ffn_w_down']

    S, D = x.shape[1], x.shape[2]
    x0 = x.reshape(S, D)
    tgt = loss_target.reshape(S, D)
    G, Pn, Cg = SSM_GROUPS, SSM_STATE, SSM_GROUP
    Fs = ffn_w_up.shape[2]
    FP = Fs
    Rd = ffn_w_down.shape[1]
    EIN = N_CHIPS * ev_w_in.shape[2]

    cwl = ffn_conv_w.reshape(-1)
    cw_rows = _rup(_rup(cwl.shape[0], LANE) // LANE, 32)
    cw_pad = jnp.pad(cwl, (0, cw_rows * LANE - cwl.shape[0])).reshape(cw_rows, LANE)
    transposed = ('ev_w_in', 'ffn_w_up')

    def view(n, a):
        return jnp.transpose(a, (0, 2, 1)) if n in transposed else a

    Wv = {n: view(n, W[n]) for n in big}
    big_e = [(n, l) for n in big for l in range(W[n].shape[0])]
    split_cols = {e: (Wv[e[0]].shape[1] // 2) % 16 != 0 for e in big_e}
    shard16 = {e: Wv[e[0]][e[1]].astype(BF16) for e in big_e}
    grp_now = [e for e in big_e if e[0].startswith('ev_')]
    grp_ffn0 = [('ffn_w_up', 0), ('ffn_w_down', 0)]
    grp_l1 = [('od_w_in', 0), ('od_w_out', 0), ('ffn_w_up', 1), ('ffn_w_down', 1)]
    src_now = [shard16[e] for e in grp_now]
    src_ffn0 = [shard16[e] for e in grp_ffn0] + [cw_pad]
    src_l1 = [shard16[e] for e in grp_l1]
    cols_now = [split_cols[e] for e in grp_now]
    cols_ffn0 = [split_cols[e] for e in grp_ffn0] + [False]
    cols_l1 = [split_cols[e] for e in grp_l1]
    ag_in = _chip_exchange_start('gather', src_now[:1], cols_now[:1], name="ag_in_start")
    ag_mix = _chip_exchange_start('gather', src_now[1:], cols_now[1:], name="ag_mix_start", after=[ag_in[4]])
    ag_ffn0 = _chip_exchange_start('gather', src_ffn0, cols_ffn0, name="ag_ffn0_start", after=[ag_mix[4]])
    ag_l1 = _chip_exchange_start('gather', src_l1, cols_l1, name="ag_l1_start", after=[ag_ffn0[4]])
    started = [ag_l1[4]]

    def finish_gather(started, srcs, cols, after, tag):
        send, recv, thru, lands, _ = started
        thru, lands = _chip_exchange_wait('gather', send, recv, thru, lands, cols, after, name=f"ag_{tag}_wait")
        lands = _sibling_pass_gathered(lands, [s.shape for s in srcs], cols, name=f"ag_{tag}_pass")
        return _own_slot(lands, [s[None] for s in thru])

    lam_r, lam_i = ev_lambda_re[0], ev_lambda_im[0]
    lstep = ev_log_step[0].reshape(G, 1)
    a_re, a_im, g_re, g_im = _s5_disc_fwd(lam_r, lam_i, lstep, name="s5_disc", after=started)
    b_re2, b_im2 = ev_ssm_b_re[0].reshape(G * Pn, Cg), ev_ssm_b_im[0].reshape(G * Pn, Cg)
    g_re1, g_im1 = g_re.reshape(G * Pn, 1), g_im.reshape(G * Pn, 1)
    bb_re, bb_im = _s5_bb_fwd(g_re1, g_im1, b_re2, b_im2, name="s5_bb")
    bbt = jnp.stack([jnp.transpose(b.reshape(G, Pn, Cg), (0, 2, 1)).reshape(G * Cg, Pn) for b in (bb_re, bb_im)])
    BB = _diag_expand(bbt, Cg, Pn, name="s5_bb_dense")
    cct = jnp.stack([jnp.transpose(ev_ssm_c_re[0], (0, 2, 1)).reshape(G * Pn, Cg),
                     jnp.transpose(-ev_ssm_c_im[0], (0, 2, 1)).reshape(G * Pn, Cg)])
    CC = _diag_expand(cct, Pn, Cg, name="s5_cc_dense", after=started)
    a_cat = jnp.stack([a_re.reshape(1, G * Pn), a_im.reshape(1, G * Pn)])
    dskip = ev_ssm_d[0].reshape(1, SSM_WIDTH)
    tabs = _rope_tables(positions.reshape(S, 1).astype(F32), name="rope_tables", after=[BB, CC])

    gw = dict(zip(grp_now[:1], finish_gather(ag_in, src_now[:1], cols_now[:1], [tabs[2]], "in")))
    w_in_t = gw[('ev_w_in', 0)].reshape(EIN, D)
    qkv_w = 3 * FOX_WIDTH
    WmainT = jnp.concatenate([w_in_t[:qkv_w], w_in_t[qkv_w + FOX_HEADS:]], axis=0)
    WfT = jnp.pad(w_in_t[qkv_w:qkv_w + FOX_HEADS], ((0, LANE - FOX_HEADS), (0, 0)))
    cbs = [ffn_conv_b[l].reshape(N_CHIPS, Fs) for l in range(DEPTH)]

    P = _mm(x0, WmainT, 'nt', name="ev_proj")
    fl = _mm(x0, WfT, 'nt', name="ev_proj_f")
    bf_pad = jnp.pad(ev_b_f.reshape(1, FOX_HEADS), ((0, 0), (0, LANE - FOX_HEADS)))
    cgate, sgate = _gate_fwd(fl, bf_pad, name="fox_gate")
    ccol = jnp.transpose(cgate[:, :FOX_HEADS]).reshape(FOX_HEADS, S, 1)
    crow = jnp.transpose(cgate[:, :FOX_HEADS]).reshape(FOX_HEADS, 1, S)
    fox, lse = _fox_fwd(P, ccol, crow, name="fox_fwd")
    u_s5 = P[:, qkv_w:]
    UT, HT = _DIAG_TILE * Cg, _DIAG_TILE * Pn
    bu = _mm(u_s5, BB, 'nn', bmode='bo', tm=2048, tn=HT, tk=UT, diag='kn', name="s5_bu")
    hh = _s5_scan_fwd(bu, a_cat, name="s5_scan")
    yc = _mm(hh, CC, 'nn', bmode='abr', tm=2048, tn=UT, tk=HT, diag='kn', name="s5_y")
    y_s5, yg = _s5_out_fwd(yc, P, dskip, name="s5_out")
    gw.update(zip(grp_now[1:], finish_gather(ag_mix, src_now[1:], cols_now[1:], [yg], "mix")))
    Wglu = _cols_from_shards(gw[('ev_w_glu', 0)])
    Wout_ev = gw[('ev_w_out', 0)].reshape(D, D)
    z = _mm(yg, Wglu, 'nn', name="s5_glu_proj")
    ssm = _glu_fwd(z, name="s5_glu")
    cat = jnp.concatenate([fox.astype(BF16), ssm], axis=1)
    mix0 = _mm(cat, Wout_ev, 'nn', name="ev_out")
    x1, xh1, rs1 = _add_ln_fwd(x0, mix0, ln_mix_g[0], ln_mix_b[0], name="ln_mix0")
    got = finish_gather(ag_ffn0, src_ffn0, cols_ffn0, [x1], "ffn0")
    gw.update(zip(grp_ffn0, got[:-1]))
    cw_all = got[-1].reshape(N_CHIPS, -1)[:, :cwl.shape[0]].reshape(N_CHIPS, DEPTH, 3, Fs)
    cws = [cw_all[:, l] for l in range(DEPTH)]
    Wup = {0: gw[('ffn_w_up', 0)]}
    Wdn = {0: gw[('ffn_w_down', 0)].reshape(2, Fs, D)}
    f0, hf0, af0 = _ffn_fwd(x1, Wup[0], Wdn[0], cws[0], cbs[0], "l0")
    x2, xh2, rs2 = _add_ln_fwd(x1, f0, ln_ffn_g[0], ln_ffn_b[0], name="ln_ffn0")

    gw.update(zip(grp_l1, finish_gather(ag_l1, src_l1, cols_l1, [x2], "l1")))
    Wodin = _cols_from_shards(gw[('od_w_in', 0)])
    Wodout = gw[('od_w_out', 0)].reshape(D, D)
    Wup[1] = gw[('ffn_w_up', 1)]
    Wdn[1] = gw[('ffn_w_down', 1)].reshape(2, Fs, D)
    QW, KW = SWA_HEADS * SWA_HEAD_DIM, SWA_KV_HEADS * SWA_HEAD_DIM
    P1 = _mm(x2, Wodin, 'nn', name="od_proj")
    qT = _to_heads(P1, tabs, col0=0, width=QW, rotate=True, name="rope_q", out_dtype=BF16)
    kT = _to_heads(P1, tabs, col0=QW, width=KW, rotate=True, name="rope_k", out_dtype=BF16)
    vT = _to_heads(P1, tabs, col0=QW + KW, width=KW, rotate=False, name="heads_v", out_dtype=BF16)
    sink_rows = jnp.broadcast_to(od_sinks[0].reshape(SWA_KV_HEADS, SWA_GROUPS, 1, 1),
                                 (SWA_KV_HEADS, SWA_GROUPS, SWA_WINDOW, 1)).reshape(SWA_KV_HEADS, -1, 1)
    oT, Lsw = _swa_fwd(qT, kT, vT, sink_rows, name="swa_fwd")
    o_sw = _from_heads(oT, tabs, rotate_back=False, name="heads_o", out_dtype=BF16)
    mix1 = _mm(o_sw, Wodout, 'nn', name="od_out")
    x3, xh3, rs3 = _add_ln_fwd(x2, mix1, ln_mix_g[1], ln_mix_b[1], name="ln_mix1")
    f1, hf1, af1 = _ffn_fwd(x3, Wup[1], Wdn[1], cws[1], cbs[1], "l1")
    x4, xh4, rs4 = _add_ln_fwd(x3, f1, ln_ffn_g[1], ln_ffn_b[1], name="ln_ffn1")
    dy, loss_part = _loss_grad(x4, tgt, name="loss")

    dz4, dg_ffn1, db_ffn1 = _ln_bwd(dy, None, xh4, rs4, ln_ffn_g[1], name="lnb_ffn1")
    dx3f, dWup1, dWdn1, dcw1, dcb1 = _ffn_bwd(dz4, x3, hf1, af1, Wup[1], Wdn[1], cws[1], "l1")
    sib_ffn1 = _sibling_halves_start([dWup1, dWdn1.reshape(N_CHIPS, Rd, D)], [False, False], name="rs_ffn1_sib_start")
    dz3, dg_mix1, db_mix1 = _ln_bwd(dz4, dx3f, xh3, rs3, ln_mix_g[1], name="lnb_mix1", after=[sib_ffn1[4]])
    do_sw = _mm(dz3, Wodout, 'nt', name="od_out_dx")
    dWodout = _mm(o_sw, dz3, 'tn', name="od_out_dw", out_dtype=BF16)
    doT = _to_heads(do_sw, tabs, col0=0, width=QW, rotate=False, name="heads_do", out_dtype=F32)
    dqT, dkT, dvT, dsink = _swa_bwd(qT, kT, vT, sink_rows, oT, Lsw, doT, name="swa_bwd")
    dq1 = _from_heads(dqT, tabs, rotate_back=True, name="rope_dq", out_dtype=BF16)
    dk1 = _from_heads(dkT, tabs, rotate_back=True, name="rope_dk", out_dtype=BF16, skip_rows=SWA_WINDOW)
    dv1 = _from_heads(dvT, tabs, rotate_back=False, name="heads_dv", out_dtype=BF16, skip_rows=SWA_WINDOW)
    dP1 = jnp.concatenate([dq1, dk1, dv1], axis=1)
    dx2m = _mm(dP1, Wodin, 'nt', name="od_proj_dx")
    dWodin = _mm(x2, dP1, 'tn', name="od_proj_dw", out_dtype=BF16)

    def rs_begin(entries, grads, tag):
        cols = [split_cols[e] for e in entries]
        sib = _sibling_send_halves(grads, cols, name=f"rs_{tag}_sibling")
        return [_sum2_halves(g4, s4, bc, name=f"rs_sum2_{n}{l}")
                for (n, l), g4, s4, bc in zip(entries, grads, sib, cols)]

    def rs_begin_started(entries, started, after, tag):
        send, rcv, thru, lands, _ = started
        thru, lands = _sibling_halves_wait(send, rcv, thru, lands, [False] * len(thru), after,
                                           name=f"rs_{tag}_sib_wait")
        return [_sum2_halves(g4, s4, False, name=f"rs_sum2_{n}{l}") for (n, l), g4, s4 in zip(entries, thru, lands)]

    def own_parts(parts):
        me = 2 * lax.axis_index("x") + lax.axis_index("y")
        return [lax.dynamic_slice_in_dim(p, me, 1, axis=0) for p in parts]

    part_l1 = (rs_begin(grp_l1[:2], [_shards_from_cols(dWodin), dWodout.reshape(N_CHIPS, D // N_CHIPS, D)], "od")
               + rs_begin_started(grp_l1[2:], sib_ffn1, [dWodin], "ffn1"))
    rs_l1 = _chip_exchange_start('scatter', part_l1, [False] * len(part_l1), name="rs_l1_start")

    dz2, dg_ffn0, db_ffn0 = _ln_bwd(dz3, dx2m, xh2, rs2, ln_ffn_g[0], name="lnb_ffn0", after=[rs_l1[4]])
    dx1f, dWup0, dWdn0, dcw0, dcb0 = _ffn_bwd(dz2, x1, hf0, af0, Wup[0], Wdn[0], cws[0], "l0")
    sib_ffn0 = _sibling_halves_start([dWup0, dWdn0.reshape(N_CHIPS, Rd, D)], [False, False], name="rs_ffn0_sib_start")
    dz1, dg_mix0, db_mix0 = _ln_bwd(dz2, dx1f, xh1, rs1, ln_mix_g[0], name="lnb_mix0", after=[sib_ffn0[4]])
    dcat = _mm(dz1, Wout_ev, 'nt', name="ev_out_dx")
    dWout_ev = _mm(cat, dz1, 'tn', name="ev_out_dw", out_dtype=BF16)
    part_ffn0 = rs_begin_started(grp_ffn0, sib_ffn0, [dWout_ev], "ffn0")
    rs_ffn0 = _chip_exchange_start('scatter', part_ffn0, [False] * len(part_ffn0), name="rs_ffn0_start")
    dz = _glu_bwd(z, dcat, name="s5_glu_bwd")
    dyg = _mm(dz, Wglu, 'nt', name="s5_glu_dx", after=[rs_ffn0[4]])
    dWglu = _mm(yg, dz, 'tn', name="s5_glu_dw", out_dtype=BF16)
    dy_s5, du_dir, dD = _s5_out_bwd(dyg, y_s5, P, dskip, name="s5_out_bwd")
    dhh = _mm(dy_s5, CC, 'nt', bmode='bo', tm=2048, tn=HT, tk=UT, diag='kn', name="s5_y_dx")
    dCC = _mm(hh, dy_s5, 'tn', bmode='ao', tm=HT, tn=UT, diag='mn', name="s5_y_dw")
    lam, da_s5 = _s5_scan_bwd(dhh, hh, a_cat, name="s5_scan_bwd")
    du = _mm(lam, BB, 'nt', bmode='abr', tm=2048, tn=UT, tk=HT, diag='kn', name="s5_bu_dx", plus=[(du_dir, 1.0)],
             out_dtype=BF16)
    dBB = _mm(u_s5, lam, 'tn', bmode='bo', tm=UT, tn=HT, diag='mn', name="s5_bu_dw")
    dq0, dk0, dv0, dccol, dcrow = _fox_bwd(P, ccol, crow, fox, lse, dcat, name="fox_bwd")
    dc = jnp.transpose((dccol.reshape(FOX_HEADS, S) - dcrow.reshape(FOX_HEADS, S)))
    dc = jnp.pad(dc, ((0, 0), (0, LANE - FOX_HEADS)))
    dfl, dbf = _gate_bwd(dc, sgate, name="fox_gate_bwd")
    dP = jnp.concatenate([dq0, dk0, dv0, du], axis=1)
    dx0b = _mm(dfl, WfT, 'nn', name="ev_proj_f_dx")
    grad_x = _mm(dP, WmainT, 'nn', name="ev_proj_dx", plus=[(dz1, ALPHA), (dx0b, 1.0)])
    dWmainT = _mm(dP, x0, 'tn', tm=1024, tn=1024, name="ev_proj_dw", out_dtype=BF16)
    dWfT = _mm(dfl, x0, 'tn', name="ev_proj_f_dw", out_dtype=BF16)

    dbbt = _diag_extract(dBB, Cg, Pn, name="s5_bb_diag")
    dcct = _diag_extract(dCC, Pn, Cg, name="s5_cc_diag")
    dbb_re = jnp.transpose(dbbt[0].reshape(G, Cg, Pn), (0, 2, 1)).reshape(G * Pn, Cg)
    dbb_im = jnp.transpose(dbbt[1].reshape(G, Cg, Pn), (0, 2, 1)).reshape(G * Pn, Cg)
    db_re, db_im, dg_re1, dg_im1 = _s5_bb_bwd(g_re1, g_im1, b_re2, b_im2, dbb_re, dbb_im, name="s5_bb_bwd")
    dlam_re, dlam_im, dlstep = _s5_disc_bwd(lam_r, lam_i, lstep, da_s5[0].reshape(G, Pn), da_s5[1].reshape(G, Pn),
                                            dg_re1.reshape(G, Pn), dg_im1.reshape(G, Pn), name="s5_disc_bwd")
    dc_re = jnp.transpose(dcct[0].reshape(G, Pn, Cg), (0, 2, 1))
    dc_im = -jnp.transpose(dcct[1].reshape(G, Pn, Cg), (0, 2, 1))

    def conv_w_full(d0, d1):
        return jnp.stack([jnp.reshape(jnp.transpose(d[:, :, :Fs], (1, 0, 2)), (3, N_CHIPS * Fs)) for d in (d0, d1)])

    def conv_b_full(d0, d1):
        return jnp.stack([jnp.reshape(d[:, 0, :Fs], (N_CHIPS * Fs,)) for d in (d0, d1)])

    small_local = dict(
        ev_b_f=dbf[:, :FOX_HEADS], ev_lambda_re=dlam_re, ev_lambda_im=dlam_im, ev_log_step=dlstep,
        ev_ssm_b_re=db_re, ev_ssm_b_im=db_im, ev_ssm_c_re=dc_re, ev_ssm_c_im=dc_im, ev_ssm_d=dD,
        od_sinks=dsink[:, :, 0],
        ln_mix_g=jnp.concatenate([dg_mix0, dg_mix1]), ln_mix_b=jnp.concatenate([db_mix0, db_mix1]),
        ffn_conv_w=conv_w_full(dcw0, dcw1), ffn_conv_b=conv_b_full(dcb0, dcb1),
        ln_ffn_g=jnp.concatenate([dg_ffn0, dg_ffn1]), ln_ffn_b=jnp.concatenate([db_ffn0, db_ffn1]))
    small = list(small_local.keys())
    out_g, out_d, out_m, out_v = {}, {}, {}, {}
    loss_out = []

    def small_update(after):
        red = _all_reduce_small(_pack([small_local[n] for n in small] + [loss_part]), name="ar_small", after=after)
        full_shapes = [W[n].shape if n != 'ffn_conv_w' else (DEPTH, 3, N_CHIPS * Fs) for n in small]
        pieces = _unpack(red, full_shapes + [()])
        loss_out.append(pieces[-1])
        gsmall = dict(zip(small, pieces[:-1]))
        chip = 2 * lax.axis_index("x") + lax.axis_index("y")
        gsmall['ffn_conv_w'] = lax.dynamic_slice_in_dim(gsmall['ffn_conv_w'], chip * Fs, Fs, axis=2)
        shapes = [W[n].shape for n in small]
        gs, ds_, ms, vs = _adamw(_pack([W[n] for n in small])[None], _pack([gsmall[n] for n in small])[None],
                                 _pack([Mo[n] for n in small])[None], _pack([Vo[n] for n in small])[None],
                                 name="adamw_small", tr=1 << 14)
        out_g.update(zip(small, _unpack(gs, shapes)))
        out_d.update(zip(small, _unpack(ds_, shapes)))
        out_m.update(zip(small, _unpack(ms, shapes)))
        out_v.update(zip(small, _unpack(vs, shapes)))
        return vs

    dw_in_t = jnp.concatenate([dWmainT[:qkv_w], dWfT[:FOX_HEADS], dWmainT[qkv_w:]], axis=0)
    part_now = rs_begin(grp_now, [dw_in_t.reshape(N_CHIPS, EIN // N_CHIPS, D), _shards_from_cols(dWglu),
                                  dWout_ev.reshape(N_CHIPS, D // N_CHIPS, D)], "l0")
    small_done = small_update([grad_x])
    rs_now = _chip_exchange_start('scatter', part_now, [False] * len(part_now), name="rs_l0_start",
                                  after=[small_done])

    def finish_scatter(started, parts, after, tag):
        send, rcv, thru, lands, _ = started
        thru, lands = _chip_exchange_wait('scatter', send, rcv, thru, lands, [False] * len(parts), after,
                                          name=f"rs_{tag}_wait")
        return _own_slot(lands, own_parts(thru))

    def update(entries, recv, tag):
        halves = [_rowsum(r, name=f"rs_sum4_{e[0]}{e[1]}") for e, r in zip(entries, recv)]
        send, rcv, thru, lands, tok = _sibling_swap_start(halves, name=f"rs_{tag}_join_start")
        own = dict(zip(entries, thru))
        params = list(dict.fromkeys(e[0] for e in entries))

        def half_update(n, grads, is_own, prev, after_name):
            return _adamw_half(Wv[n], [grads[(n, l)] for l in range(W[n].shape[0])], view(n, Mo[n]), view(n, Vo[n]),
                               name=f"adamw_{after_name}_{n}", own=is_own, prev=prev, by_cols=split_cols[(n, 0)])

        first = {n: half_update(n, own, True, None, "own") for n in params}
        _, others = _sibling_swap_wait(send, rcv, thru, lands, [first[n][3] for n in params] + [tok],
                                       name=f"rs_{tag}_join_wait")
        oth = dict(zip(entries, others))
        done = []
        for n in params:
            res = half_update(n, oth, False, first[n], "sib")
            out_g[n], out_d[n], out_m[n], out_v[n] = (view(n, t) for t in res)
            done.append(res[3])
        return done

    recv_rest = (finish_scatter(rs_l1, part_l1, [rs_now[4]], "l1")
                 + finish_scatter(rs_ffn0, part_ffn0, [rs_now[4]], "ffn0"))
    done = update(grp_l1 + grp_ffn0, recv_rest, "rest")
    update(grp_now, finish_scatter(rs_now, part_now, done, "l0"), "l0")
    loss = loss_out[0]

    return (loss, grad_x.reshape(1, S, D), *[out_g[n] for n in names], *[out_d[n] for n in names],
            *[out_m[n] for n in names], *[out_v[n] for n in names])
```

```python
import math

import numpy as np
import jax
import jax.numpy as jnp
from jax import lax
from jax.experimental import pallas as pl
from jax.experimental.pallas import tpu as pltpu

F32 = jnp.float32
BF16 = jnp.bfloat16
MESH = pl.DeviceIdType.MESH
ANY = pl.BlockSpec(memory_space=pl.ANY)

D_MODEL = 2048
FOX_HEADS = 8
FOX_HEAD_DIM = 128
FOX_WIDTH = 1024
SSM_WIDTH = 1024
SSM_GROUP = 16
SSM_GROUPS = 64
SSM_STATE = 64
SWA_HEADS = 32
SWA_KV_HEADS = 4
SWA_HEAD_DIM = 64
SWA_GROUPS = 8
SWA_WINDOW = 128
ROPE_DIM = 16
ROPE_THETA = 500000.0
LN_EPS = 1e-5
DEPTH = 2
ALPHA = (2.0 * DEPTH) ** 0.25
ADAM_LR = 0.001
ADAM_B1 = 0.9
ADAM_B2 = 0.999
ADAM_EPS = 1e-08
ADAM_WD = 0.01
ADAM_STEP = 10
N_CHIPS = 4

VMEM_LIMIT = 56 * 1024 * 1024
LANE = 128


def _call(body, after=(), **kw):
    if after:
        n = len(after)

        def shifted(*refs):
            return body(*refs[n:])

        call = _call(shifted, **dict(kw, in_specs=[ANY] * n + list(kw["in_specs"])))
        return lambda *args: call(*after, *args)
    return pl.pallas_call(body, **kw)


def _cparams(sem):
    return pltpu.CompilerParams(dimension_semantics=sem, vmem_limit_bytes=VMEM_LIMIT)


def _rup(n, m):
    return (n + m - 1) // m * m


def _pick(n, pref):
    if n <= pref:
        return n
    for step in (128, 16, 8):
        for t in range(pref - pref % step, 0, -step):
            if n % t == 0:
                return t
    return n


def _tile2d(rows, cols, pref_rows=256, budget=256 * 1024):
    tr = _pick(rows, pref_rows)
    if tr < 64:
        tr = rows
    if cols % LANE:
        return tr, cols
    return tr, _pick(cols, max(LANE, budget // tr // LANE * LANE))


def _mm(a, b, mode, *, name, tm=512, tn=1024, tk=2048, bmode=None, out_dtype=F32, after=(), b_map=None,
        o_map=None, diag=None, plus=()):
    a3 = a if a.ndim == 3 else a[None]
    b3 = b if b.ndim == 3 else b[None]
    if mode == 'tn':
        K, M = a3.shape[1:]
    else:
        M, K = a3.shape[1:]
    N = b3.shape[1] if mode == 'nt' else b3.shape[2]
    tm, tn, tk = _pick(M, tm), _pick(N, tn), _pick(K, tk)
    nb = max(a3.shape[0], b3.shape[0])
    nbo, nbr = (1, nb) if bmode == 'abr' else (nb, 1)
    nm, nk = M // tm, K // tk
    if diag == 'kn':
        assert K // tk == N // tn
        nk = 1
    if diag == 'mn':
        assert M // tm == N // tn
        nm = 1
    nred = nbr * nk
    a_b = bmode in ('ao', 'abr')
    b_b = bmode in ('bo', 'abr')
    o_b = bmode in ('bo', 'ao')

    def bsel(flag, bo, br, remap=None):
        if not flag:
            return 0
        return (bo + br) if remap is None else remap(bo + br)

    def mi(i, j):
        return j if diag == 'mn' else i

    def ki(j, k):
        return j if diag == 'kn' else k

    if mode == 'tn':
        a_spec = pl.BlockSpec((None, tk, tm), lambda bo, i, j, br, k: (bsel(a_b, bo, br), ki(j, k), mi(i, j)))
    else:
        a_spec = pl.BlockSpec((None, tm, tk), lambda bo, i, j, br, k: (bsel(a_b, bo, br), mi(i, j), ki(j, k)))
    if mode == 'nt':
        b_spec = pl.BlockSpec((None, tn, tk), lambda bo, i, j, br, k: (bsel(b_b, bo, br, b_map), j, ki(j, k)))
    else:
        b_spec = pl.BlockSpec((None, tk, tn), lambda bo, i, j, br, k: (bsel(b_b, bo, br, b_map), ki(j, k), j))
    o_spec = pl.BlockSpec((None, tm, tn), lambda bo, i, j, br, k: (bsel(o_b, bo, br, o_map), mi(i, j), j))
    dn = {'nn': (((1,), (0,)), ((), ())), 'nt': (((1,), (1,)), ((), ())), 'tn': (((0,), (0,)), ((), ()))}[mode]

    na = len(plus)

    def body(a_ref, b_ref, *rest):
        plus_refs = rest[:na]
        o_ref, scratch = rest[na + len(after)], rest[na + len(after) + 1:]
        r = lax.dot_general(a_ref[...].astype(BF16), b_ref[...].astype(BF16), dn, preferred_element_type=F32)

        def finish(total):
            for (_, scale), p_ref in zip(plus, plus_refs):
                total = total + scale * p_ref[...].astype(F32)
            o_ref[...] = total.astype(out_dtype)

        if nred == 1:
            finish(r)
        else:
            acc = scratch[0]
            step = pl.program_id(3) * nk + pl.program_id(4)

            @pl.when(step == 0)
            def _():
                acc[...] = r

            @pl.when(step > 0)
            def _():
                acc[...] += r

            @pl.when(step == nred - 1)
            def _():
                finish(acc[...])

    out = _call(
        body, name=name,
        grid=(nbo, nm, N // tn, nbr, nk),
        in_specs=[a_spec, b_spec] + [o_spec] * na + [ANY] * len(after), out_specs=o_spec,
        out_shape=jax.ShapeDtypeStruct((nbo if o_b else 1, M, N), out_dtype),
        scratch_shapes=[] if nred == 1 else [pltpu.VMEM((tm, tn), F32)],
        compiler_params=_cparams(("parallel", "parallel", "parallel", "arbitrary", "arbitrary")),
    )(a3, b3, *[p if p.ndim == 3 else p[None] for p, _ in plus], *after)
    return out if o_b else out[0]


def _add_ln_fwd(x, r, g, b, *, name):
    S, D = x.shape
    tr = _pick(S, 256)

    def body(x_ref, r_ref, g_ref, b_ref, o_ref, xh_ref, rs_ref):
        z = ALPHA * x_ref[...] + r_ref[...]
        mu = jnp.mean(z, axis=-1, keepdims=True)
        zc = z - mu
        var = jnp.mean(zc * zc, axis=-1, keepdims=True)
        rstd = lax.rsqrt(var + LN_EPS)
        xh = zc * rstd
        xh_ref[...] = xh
        rs_ref[...] = rstd
        o_ref[...] = xh * g_ref[...] + b_ref[...]

    row = pl.BlockSpec((tr, D), lambda i: (i, 0))
    vec = pl.BlockSpec((1, D), lambda i: (0, 0))
    return _call(
        body, name=name, grid=(S // tr,),
        in_specs=[row, row, vec, vec],
        out_specs=[row, row, pl.BlockSpec((tr, 1), lambda i: (i, 0))],
        out_shape=[jax.ShapeDtypeStruct((S, D), F32), jax.ShapeDtypeStruct((S, D), F32),
                   jax.ShapeDtypeStruct((S, 1), F32)],
        compiler_params=_cparams(("parallel",)),
    )(x, r, g.reshape(1, D), b.reshape(1, D))


def _ln_bwd(da, db, xhat, rstd, g, *, name, after=()):
    S, D = xhat.shape
    tr = _pick(S, 256)

    def body(*refs):
        da_ref, db_ref, xh_ref, rs_ref, g_ref, dz_ref, dg_ref, dbt_ref = refs[len(after):]
        dy = ALPHA * da_ref[...] + db_ref[...]
        xh = xh_ref[...]
        dxh = dy * g_ref[...]
        m1 = jnp.mean(dxh, axis=-1, keepdims=True)
        m2 = jnp.mean(dxh * xh, axis=-1, keepdims=True)
        dz_ref[...] = rs_ref[...] * (dxh - m1 - xh * m2)
        pg = jnp.sum(dy * xh, axis=0, keepdims=True)
        pb = jnp.sum(dy, axis=0, keepdims=True)

        @pl.when(pl.program_id(0) == 0)
        def _():
            dg_ref[...] = pg
            dbt_ref[...] = pb

        @pl.when(pl.program_id(0) > 0)
        def _():
            dg_ref[...] += pg
            dbt_ref[...] += pb

    row = pl.BlockSpec((tr, D), lambda i: (i, 0))
    vec = pl.BlockSpec((1, D), lambda i: (0, 0))
    ins = list(after) + [da, db, xhat, rstd, g.reshape(1, D)]
    in_specs = [ANY] * len(after) + [row, row, row, pl.BlockSpec((tr, 1), lambda i: (i, 0)), vec]
    return _call(
        body, name=name, grid=(S // tr,),
        in_specs=in_specs, out_specs=[row, vec, vec],
        out_shape=[jax.ShapeDtypeStruct((S, D), F32), jax.ShapeDtypeStruct((1, D), F32),
                   jax.ShapeDtypeStruct((1, D), F32)],
        compiler_params=_cparams(("arbitrary",)),
    )(*ins)


def _loss_ln_bwd(t, xhat, rstd, g, b, *, name):
    S, D = xhat.shape
    tr = _pick(S, 256)

    def body(t_ref, xh_ref, rs_ref, g_ref, b_ref, dz_ref, dg_ref, dbt_ref, l_ref):
        xh = xh_ref[...]
        e = xh * g_ref[...] + b_ref[...] - t_ref[...]
        dy = e * (1.0 / D)
        part = 0.5 * jnp.sum(jnp.sum(e * e, axis=-1, keepdims=True) * (1.0 / D), axis=0, keepdims=True)
        dxh = dy * g_ref[...]
        m1 = jnp.mean(dxh, axis=-1, keepdims=True)
        m2 = jnp.mean(dxh * xh, axis=-1, keepdims=True)
        dz_ref[...] = rs_ref[...] * (dxh - m1 - xh * m2)
        pg = jnp.sum(dy * xh, axis=0, keepdims=True)
        pb = jnp.sum(dy, axis=0, keepdims=True)

        @pl.when(pl.program_id(0) == 0)
        def _():
            dg_ref[...] = pg
            dbt_ref[...] = pb
            l_ref[...] = part

        @pl.when(pl.program_id(0) > 0)
        def _():
            dg_ref[...] += pg
            dbt_ref[...] += pb
            l_ref[...] += part

    row = pl.BlockSpec((tr, D), lambda i: (i, 0))
    vec = pl.BlockSpec((1, D), lambda i: (0, 0))
    return _call(
        body, name=name, grid=(S // tr,),
        in_specs=[row, row, pl.BlockSpec((tr, 1), lambda i: (i, 0)), vec, vec],
        out_specs=[row, vec, vec, pl.BlockSpec((1, 1), lambda i: (0, 0))],
        out_shape=[jax.ShapeDtypeStruct((S, D), F32), jax.ShapeDtypeStruct((1, D), F32),
                   jax.ShapeDtypeStruct((1, D), F32), jax.ShapeDtypeStruct((1, 1), F32)],
        compiler_params=_cparams(("arbitrary",)),
    )(t, xhat, rstd, g.reshape(1, D), b.reshape(1, D))


def _split3(x):
    h = x.astype(BF16)
    r = x - h.astype(F32)
    m = r.astype(BF16)
    l = (r - m.astype(F32)).astype(BF16)
    return h, m, l


def _tri_matmul(tri_bf, x):
    h, m, l = _split3(x)
    dn = (((1,), (0,)), ((), ()))
    return (lax.dot_general(tri_bf, l, dn, preferred_element_type=F32)
            + lax.dot_general(tri_bf, m, dn, preferred_element_type=F32)
            + lax.dot_general(tri_bf, h, dn, preferred_element_type=F32))


def _gate_fwd(fl, bf, *, name):
    S = fl.shape[0]
    tc = _pick(S, 256)
    nchunk = S // tc

    def body(fl_ref, bf_ref, c_ref, sg_ref):
        r = lax.broadcasted_iota(jnp.int32, (tc, tc), 0)
        cidx = lax.broadcasted_iota(jnp.int32, (tc, tc), 1)
        tri = (r >= cidx).astype(BF16)
        carry = jnp.zeros((1, LANE), F32)
        for ch in range(nchunk):
            x = fl_ref[pl.ds(ch * tc, tc), :] + bf_ref[...]
            lf = jnp.minimum(x, 0.0) - jnp.log(1.0 + jnp.exp(-jnp.abs(x)))
            sg_ref[pl.ds(ch * tc, tc), :] = jax.nn.sigmoid(-x)
            c_ref[pl.ds(ch * tc, tc), :] = _tri_matmul(tri, lf) + carry
            carry = carry + jnp.sum(lf, axis=0, keepdims=True)

    full = pl.BlockSpec((S, LANE), lambda: (0, 0))
    return _call(
        body, name=name, in_specs=[full, pl.BlockSpec((1, LANE), lambda: (0, 0))], out_specs=[full, full],
        out_shape=[jax.ShapeDtypeStruct((S, LANE), F32)] * 2,
        compiler_params=pltpu.CompilerParams(vmem_limit_bytes=VMEM_LIMIT),
    )(fl, bf)


def _gate_bwd(dc, sg, *, name):
    S = dc.shape[0]
    tc = _pick(S, 256)
    nchunk = S // tc

    def body(dc_ref, sg_ref, dfl_ref, db_ref):
        r = lax.broadcasted_iota(jnp.int32, (tc, tc), 0)
        cidx = lax.broadcasted_iota(jnp.int32, (tc, tc), 1)
        tri = (r <= cidx).astype(BF16)
        carry = jnp.zeros((1, LANE), F32)
        dbacc = jnp.zeros((1, LANE), F32)
        for ch in reversed(range(nchunk)):
            d = dc_ref[pl.ds(ch * tc, tc), :]
            dfl = (_tri_matmul(tri, d) + carry) * sg_ref[pl.ds(ch * tc, tc), :]
            dfl_ref[pl.ds(ch * tc, tc), :] = dfl
            dbacc = dbacc + jnp.sum(dfl, axis=0, keepdims=True)
            carry = carry + jnp.sum(d, axis=0, keepdims=True)
        db_ref[...] = dbacc

    full = pl.BlockSpec((S, LANE), lambda: (0, 0))
    return _call(
        body, name=name, in_specs=[full, full], out_specs=[full, pl.BlockSpec((1, LANE), lambda: (0, 0))],
        out_shape=[jax.ShapeDtypeStruct((S, LANE), F32), jax.ShapeDtypeStruct((1, LANE), F32)],
        compiler_params=pltpu.CompilerParams(vmem_limit_bytes=VMEM_LIMIT),
    )(dc, sg)


def _fox_scores(q_ref, k_ref, cc_ref, cr_ref, qi, tq, S):
    scale = 1.0 / math.sqrt(FOX_HEAD_DIM)
    s = lax.dot_general(q_ref[...].astype(BF16), k_ref[...].astype(BF16), (((1,), (1,)), ((), ())),
                        preferred_element_type=F32) * scale
    s = s + cc_ref[...] - cr_ref[...]
    row = lax.broadcasted_iota(jnp.int32, (tq, S), 0) + qi * tq
    col = lax.broadcasted_iota(jnp.int32, (tq, S), 1)
    return s, row >= col


def _fox_fwd(P, ccol, crow, *, name):
    S = P.shape[0]
    tq = _pick(S, 256)
    H = FOX_HEADS

    def body(q_ref, k_ref, v_ref, cc_ref, cr_ref, o_ref, l_ref):
        s, causal = _fox_scores(q_ref, k_ref, cc_ref, cr_ref, pl.program_id(1), tq, S)
        s = jnp.where(causal, s, -1e30)
        m = jnp.max(s, axis=-1, keepdims=True)
        e = jnp.exp(s - m)
        den = jnp.sum(e, axis=-1, keepdims=True)
        p = e / den
        o_ref[...] = jnp.dot(p.astype(BF16), v_ref[...].astype(BF16), preferred_element_type=F32)
        l_ref[...] = m + jnp.log(den)

    return _call(
        body, name=name, grid=(H, S // tq),
        in_specs=[pl.BlockSpec((tq, 128), lambda h, i: (i, h)),
                  pl.BlockSpec((S, 128), lambda h, i: (0, H + h)),
                  pl.BlockSpec((S, 128), lambda h, i: (0, 2 * H + h)),
                  pl.BlockSpec((None, tq, 1), lambda h, i: (h, i, 0)),
                  pl.BlockSpec((None, 1, S), lambda h, i: (h, 0, 0))],
        out_specs=[pl.BlockSpec((tq, 128), lambda h, i: (i, h)),
                   pl.BlockSpec((None, tq, 1), lambda h, i: (h, i, 0))],
        out_shape=[jax.ShapeDtypeStruct((S, FOX_WIDTH), F32), jax.ShapeDtypeStruct((H, S, 1), F32)],
        compiler_params=_cparams(("parallel", "parallel")),
    )(P, P, P, ccol, crow)


def _fox_bwd(P, ccol, crow, o, lse, dcat, *, name):
    S = P.shape[0]
    tq = _pick(S, 256)
    H = FOX_HEADS
    nq = S // tq
    scale = 1.0 / math.sqrt(FOX_HEAD_DIM)

    def body(q_ref, k_ref, v_ref, cc_ref, cr_ref, o_ref, l_ref, do_ref,
             dq_ref, dk_ref, dv_ref, dcc_ref, dcr_ref, dk_acc, dv_acc):
        qi = pl.program_id(1)
        s, causal = _fox_scores(q_ref, k_ref, cc_ref, cr_ref, qi, tq, S)
        p = jnp.where(causal, jnp.exp(s - l_ref[...]), 0.0)
        do = do_ref[...]
        do_bf = do.astype(BF16)
        dp = lax.dot_general(do_bf, v_ref[...].astype(BF16), (((1,), (1,)), ((), ())), preferred_element_type=F32)
        delta = jnp.sum(do * o_ref[...], axis=-1, keepdims=True)
        ds = p * (dp - delta)
        ds_bf = ds.astype(BF16)
        dq_ref[...] = (jnp.dot(ds_bf, k_ref[...].astype(BF16), preferred_element_type=F32) * scale).astype(BF16)
        dkp = lax.dot_general(ds_bf, q_ref[...].astype(BF16), (((0,), (0,)), ((), ())),
                              preferred_element_type=F32) * scale
        dvp = lax.dot_general(p.astype(BF16), do_bf, (((0,), (0,)), ((), ())), preferred_element_type=F32)
        dcc_ref[...] = jnp.sum(ds, axis=-1, keepdims=True)
        dcr = jnp.sum(ds, axis=0, keepdims=True)

        @pl.when(qi == 0)
        def _():
            dk_acc[...] = dkp
            dv_acc[...] = dvp
            dcr_ref[...] = dcr

        @pl.when(qi > 0)
        def _():
            dk_acc[...] += dkp
            dv_acc[...] += dvp
            dcr_ref[...] += dcr

        @pl.when(qi == nq - 1)
        def _():
            dk_ref[...] = dk_acc[...].astype(BF16)
            dv_ref[...] = dv_acc[...].astype(BF16)

    qblk = pl.BlockSpec((tq, 128), lambda h, i: (i, h))
    kvo = pl.BlockSpec((S, 128), lambda h, i: (0, h))
    col = pl.BlockSpec((None, tq, 1), lambda h, i: (h, i, 0))
    rowv = pl.BlockSpec((None, 1, S), lambda h, i: (h, 0, 0))
    return _call(
        body, name=name, grid=(H, nq),
        in_specs=[qblk,
                  pl.BlockSpec((S, 128), lambda h, i: (0, H + h)),
                  pl.BlockSpec((S, 128), lambda h, i: (0, 2 * H + h)),
                  col, rowv, qblk, col, qblk],
        out_specs=[qblk, kvo, kvo, col, rowv],
        out_shape=[jax.ShapeDtypeStruct((S, FOX_WIDTH), BF16)] * 3
        + [jax.ShapeDtypeStruct((H, S, 1), F32), jax.ShapeDtypeStruct((H, 1, S), F32)],
        scratch_shapes=[pltpu.VMEM((S, 128), F32), pltpu.VMEM((S, 128), F32)],
        compiler_params=_cparams(("parallel", "arbitrary")),
    )(P, P, P, ccol, crow, o, lse, dcat)


def _s5_disc_fwd(lr, li, ls, *, name, after=()):
    G, Pn = lr.shape

    def body(lr_ref, li_ref, ls_ref, ar_ref, ai_ref, gr_ref, gi_ref):
        lr_, li_ = lr_ref[...], li_ref[...]
        dt = jnp.exp(ls_ref[...])
        mag = jnp.exp(lr_ * dt)
        th = li_ * dt
        ar = mag * jnp.cos(th)
        ai = mag * jnp.sin(th)
        den = lr_ * lr_ + li_ * li_
        xr = ar - 1.0
        ar_ref[...] = ar
        ai_ref[...] = ai
        gr_ref[...] = (xr * lr_ + ai * li_) / den
        gi_ref[...] = (ai * lr_ - xr * li_) / den

    sq = pl.BlockSpec((G, Pn), lambda: (0, 0))
    return _call(
        body, after=after, name=name, in_specs=[sq, sq, pl.BlockSpec((G, 1), lambda: (0, 0))], out_specs=[sq] * 4,
        out_shape=[jax.ShapeDtypeStruct((G, Pn), F32)] * 4,
    )(lr, li, ls)


def _s5_disc_bwd(lr, li, ls, dar, dai, dgr, dgi, *, name):
    G, Pn = lr.shape

    def body(lr_ref, li_ref, ls_ref, dar_ref, dai_ref, dgr_ref, dgi_ref, dlr_ref, dli_ref, dls_ref):
        lr_, li_ = lr_ref[...], li_ref[...]
        dt = jnp.exp(ls_ref[...])
        mag = jnp.exp(lr_ * dt)
        th = li_ * dt
        ar = mag * jnp.cos(th)
        ai = mag * jnp.sin(th)
        den = lr_ * lr_ + li_ * li_
        xr = ar - 1.0
        xi = ai
        g_re = (xr * lr_ + xi * li_) / den
        g_im = (xi * lr_ - xr * li_) / den
        dgr_, dgi_ = dgr_ref[...], dgi_ref[...]
        dxr = (dgr_ * lr_ - dgi_ * li_) / den
        dxi = (dgr_ * li_ + dgi_ * lr_) / den
        dden = -(dgr_ * g_re + dgi_ * g_im) / den
        dlr = (dgr_ * xr + dgi_ * xi) / den + 2.0 * dden * lr_
        dli = (dgr_ * xi - dgi_ * xr) / den + 2.0 * dden * li_
        da_r = dar_ref[...] + dxr
        da_i = dai_ref[...] + dxi
        dmag_mag = da_r * ar + da_i * ai
        dth = da_i * ar - da_r * ai
        dlr_ref[...] = dlr + dmag_mag * dt
        dli_ref[...] = dli + dth * dt
        ddt = jnp.sum(dmag_mag * lr_ + dth * li_, axis=-1, keepdims=True)
        dls_ref[...] = ddt * dt

    sq = pl.BlockSpec((G, Pn), lambda: (0, 0))
    c1 = pl.BlockSpec((G, 1), lambda: (0, 0))
    return _call(
        body, name=name, in_specs=[sq, sq, c1, sq, sq, sq, sq], out_specs=[sq, sq, c1],
        out_shape=[jax.ShapeDtypeStruct((G, Pn), F32)] * 2 + [jax.ShapeDtypeStruct((G, 1), F32)],
    )(lr, li, ls, dar, dai, dgr, dgi)


def _s5_bb_fwd(gr, gi, br, bi, *, name):
    R, C = br.shape

    def body(gr_ref, gi_ref, br_ref, bi_ref, or_ref, oi_ref):
        g_r, g_i, b_r, b_i = gr_ref[...], gi_ref[...], br_ref[...], bi_ref[...]
        or_ref[...] = g_r * b_r - g_i * b_i
        oi_ref[...] = g_r * b_i + g_i * b_r

    w = pl.BlockSpec((R, C), lambda: (0, 0))
    c1 = pl.BlockSpec((R, 1), lambda: (0, 0))
    return _call(body, name=name, in_specs=[c1, c1, w, w], out_specs=[w, w],
                 out_shape=[jax.ShapeDtypeStruct((R, C), F32)] * 2)(gr, gi, br, bi)


def _s5_bb_bwd(gr, gi, br, bi, dbbr, dbbi, *, name):
    R, C = br.shape

    def body(gr_ref, gi_ref, br_ref, bi_ref, dr_ref, di_ref, dbr_ref, dbi_ref, dgr_ref, dgi_ref):
        g_r, g_i, b_r, b_i = gr_ref[...], gi_ref[...], br_ref[...], bi_ref[...]
        d_r, d_i = dr_ref[...], di_ref[...]
        dbr_ref[...] = g_r * d_r + g_i * d_i
        dbi_ref[...] = g_r * d_i - g_i * d_r
        dgr_ref[...] = jnp.sum(d_r * b_r + d_i * b_i, axis=-1, keepdims=True)
        dgi_ref[...] = jnp.sum(d_i * b_r - d_r * b_i, axis=-1, keepdims=True)

    w = pl.BlockSpec((R, C), lambda: (0, 0))
    c1 = pl.BlockSpec((R, 1), lambda: (0, 0))
    return _call(body, name=name, in_specs=[c1, c1, w, w, w, w], out_specs=[w, w, c1, c1],
                 out_shape=[jax.ShapeDtypeStruct((R, C), F32)] * 2 + [jax.ShapeDtypeStruct((R, 1), F32)] * 2,
                 )(gr, gi, br, bi, dbbr, dbbi)


_DIAG_TILE = 8


def _diag_mask(gr, gc):
    rows, cols = _DIAG_TILE * gr, _DIAG_TILE * gc
    r = lax.broadcasted_iota(jnp.int32, (rows, cols), 0) >> (gr.bit_length() - 1)
    c = lax.broadcasted_iota(jnp.int32, (rows, cols), 1) >> (gc.bit_length() - 1)
    return r == c


def _diag_expand(t2, gr, gc, *, name, after=()):
    _, R, _ = t2.shape
    G = R // gr
    nt = G // _DIAG_TILE
    rows, cols = _DIAG_TILE * gr, _DIAG_TILE * gc

    def body(t_ref, o_ref):
        src = lax.broadcasted_iota(jnp.int32, (gc, cols), 0)
        dst = lax.broadcasted_iota(jnp.int32, (gc, cols), 1) & (gc - 1)
        spread = (src == dst).astype(BF16)
        y = jnp.dot(t_ref[...].astype(BF16), spread, preferred_element_type=F32)
        o_ref[...] = jnp.where(_diag_mask(gr, gc), y, 0.0).astype(BF16)

    return _call(
        body, after=after, name=name, grid=(2, nt),
        in_specs=[pl.BlockSpec((None, rows, gc), lambda p, i: (p, i, 0))],
        out_specs=pl.BlockSpec((None, rows, cols), lambda p, i: (p, i, i)),
        out_shape=jax.ShapeDtypeStruct((2, R, G * gc), BF16),
        compiler_params=_cparams(("parallel",) * 2),
    )(t2)


def _diag_extract(xd, gr, gc, *, name):
    _, R, _ = xd.shape
    nt = R // gr // _DIAG_TILE
    rows, cols = _DIAG_TILE * gr, _DIAG_TILE * gc

    def body(x_ref, o_ref):
        src = lax.broadcasted_iota(jnp.int32, (cols, gc), 0) & (gc - 1)
        dst = lax.broadcasted_iota(jnp.int32, (cols, gc), 1)
        fold = (src == dst).astype(BF16)
        parts = _split3(jnp.where(_diag_mask(gr, gc), x_ref[...], 0.0))
        acc = jnp.dot(parts[2], fold, preferred_element_type=F32)
        acc = acc + jnp.dot(parts[1], fold, preferred_element_type=F32)
        o_ref[...] = acc + jnp.dot(parts[0], fold, preferred_element_type=F32)

    return _call(
        body, name=name, grid=(2, nt),
        in_specs=[pl.BlockSpec((None, rows, cols), lambda p, i: (p, i, i))],
        out_specs=pl.BlockSpec((None, rows, gc), lambda p, i: (p, i, 0)),
        out_shape=jax.ShapeDtypeStruct((2, R, gc), F32),
        compiler_params=_cparams(("parallel",) * 2),
    )(xd)


SCAN_BLOCK = 8


def _cpowers(ar, ai, sign):
    ai = sign * ai
    out = [(ar, ai)]
    for _ in range(SCAN_BLOCK - 1):
        pr, pi = out[-1]
        out.append((pr * ar - pi * ai, pr * ai + pi * ar))
    return out


def _row_table(pw, row, index_of_row):
    tr_ = jnp.broadcast_to(pw[index_of_row(0)][0], row.shape)
    ti_ = jnp.broadcast_to(pw[index_of_row(0)][1], row.shape)
    for r in range(1, SCAN_BLOCK):
        pr, pi = pw[index_of_row(r)]
        tr_ = jnp.where(row == r, pr, tr_)
        ti_ = jnp.where(row == r, pi, ti_)
    return tr_, ti_


def _s5_scan_fwd(bu, a, *, name):
    _, S, N = bu.shape
    tc = 512
    nt = N // tc

    def body(a_ref, b_ref, h_ref):
        pw = _cpowers(a_ref[0], a_ref[1], 1.0)
        row = lax.broadcasted_iota(jnp.int32, (SCAN_BLOCK, tc), 0)
        lead_r, lead_i = _row_table(pw, row, lambda r: r)
        mult = {sh: (jnp.where(row >= sh, pw[sh - 1][0], 0.0), jnp.where(row >= sh, pw[sh - 1][1], 0.0))
                for sh in (1, 2, 4)}

        def step(k, carry):
            cr, ci = carry
            rows = pl.ds(pl.multiple_of(k * SCAN_BLOCK, SCAN_BLOCK), SCAN_BLOCK)
            xr, xi = b_ref[0, rows, :], b_ref[1, rows, :]
            for sh in (1, 2, 4):
                sr, si = pltpu.roll(xr, sh, 0), pltpu.roll(xi, sh, 0)
                kr, ki = mult[sh]
                xr, xi = xr + kr * sr - ki * si, xi + kr * si + ki * sr
            h_ref[0, rows, :] = xr + lead_r * cr - lead_i * ci
            h_ref[1, rows, :] = xi + lead_r * ci + lead_i * cr
            last = row == SCAN_BLOCK - 1
            tr_ = jnp.sum(jnp.where(last, xr, 0.0), axis=0, keepdims=True)
            ti_ = jnp.sum(jnp.where(last, xi, 0.0), axis=0, keepdims=True)
            a8r, a8i = pw[SCAN_BLOCK - 1]
            return a8r * cr - a8i * ci + tr_, a8r * ci + a8i * cr + ti_

        z = jnp.zeros((1, tc), F32)
        lax.fori_loop(0, S // SCAN_BLOCK, step, (z, z), unroll=2)

    vec = pl.BlockSpec((2, 1, tc), lambda j: (0, 0, j))
    mat = pl.BlockSpec((2, S, tc), lambda j: (0, 0, j))
    return _call(
        body, name=name, grid=(nt,), in_specs=[vec, mat], out_specs=mat,
        out_shape=jax.ShapeDtypeStruct((2, S, N), F32),
        compiler_params=_cparams(("parallel",)),
    )(a, bu)


def _s5_scan_bwd(g, h, a, *, name):
    _, S, N = g.shape
    tc = 256
    nt = N // tc

    def body(a_ref, g_ref, h_ref, l_ref, da_ref):
        pw = _cpowers(a_ref[0], a_ref[1], -1.0)
        row = lax.broadcasted_iota(jnp.int32, (SCAN_BLOCK, tc), 0)
        tail_r, tail_i = _row_table(pw, row, lambda r: SCAN_BLOCK - 1 - r)
        nb = S // SCAN_BLOCK
        mult = {sh: (jnp.where(row < SCAN_BLOCK - sh, pw[sh - 1][0], 0.0),
                     jnp.where(row < SCAN_BLOCK - sh, pw[sh - 1][1], 0.0)) for sh in (1, 2, 4)}

        def step(i, carry):
            k = nb - 1 - i
            cr, ci, dar, dai = carry
            rows = pl.ds(pl.multiple_of(k * SCAN_BLOCK, SCAN_BLOCK), SCAN_BLOCK)
            xr, xi = g_ref[0, rows, :], g_ref[1, rows, :]
            for sh in (1, 2, 4):
                sr, si = pltpu.roll(xr, SCAN_BLOCK - sh, 0), pltpu.roll(xi, SCAN_BLOCK - sh, 0)
                kr, ki = mult[sh]
                xr, xi = xr + kr * sr - ki * si, xi + kr * si + ki * sr
            lr = xr + tail_r * cr - tail_i * ci
            li = xi + tail_r * ci + tail_i * cr
            l_ref[0, rows, :] = lr
            l_ref[1, rows, :] = li
            prev = pl.ds(pl.multiple_of(jnp.maximum(k - 1, 0) * SCAN_BLOCK, SCAN_BLOCK), SCAN_BLOCK)
            has_prev = jnp.where(k > 0, 1.0, 0.0).astype(F32)
            first = row == 0
            hpr = jnp.where(first, pltpu.roll(h_ref[0, prev, :], 1, 0) * has_prev, pltpu.roll(h_ref[0, rows, :], 1, 0))
            hpi = jnp.where(first, pltpu.roll(h_ref[1, prev, :], 1, 0) * has_prev, pltpu.roll(h_ref[1, rows, :], 1, 0))
            tr_ = jnp.sum(jnp.where(first, xr, 0.0), axis=0, keepdims=True)
            ti_ = jnp.sum(jnp.where(first, xi, 0.0), axis=0, keepdims=True)
            a8r, a8i = pw[SCAN_BLOCK - 1]
            return (a8r * cr - a8i * ci + tr_, a8r * ci + a8i * cr + ti_,
                    dar + lr * hpr + li * hpi, dai + li * hpr - lr * hpi)

        z = jnp.zeros((1, tc), F32)
        z8 = jnp.zeros((SCAN_BLOCK, tc), F32)
        _, _, dar, dai = lax.fori_loop(0, nb, step, (z, z, z8, z8), unroll=2)
        da_ref[0] = jnp.sum(dar, axis=0, keepdims=True)
        da_ref[1] = jnp.sum(dai, axis=0, keepdims=True)

    vec = pl.BlockSpec((2, 1, tc), lambda j: (0, 0, j))
    mat = pl.BlockSpec((2, S, tc), lambda j: (0, 0, j))
    return _call(
        body, name=name, grid=(nt,), in_specs=[vec, mat, mat], out_specs=[mat, vec],
        out_shape=[jax.ShapeDtypeStruct((2, S, N), F32), jax.ShapeDtypeStruct((2, 1, N), F32)],
        compiler_params=_cparams(("parallel",)),
    )(a, g, h)


_GELU_C = math.sqrt(2.0 / math.pi)


def _s5_out_fwd(yc, P, dskip, *, name):
    S, W = yc.shape
    tr = _pick(S, 256)
    ub = 3 * FOX_WIDTH // W

    def body(yc_ref, u_ref, d_ref, y_ref, yg_ref):
        y = yc_ref[...] + d_ref[...] * u_ref[...]
        y_ref[...] = y
        t = jnp.tanh(_GELU_C * (y + 0.044715 * y * y * y))
        yg_ref[...] = (0.5 * y * (1.0 + t)).astype(BF16)

    row = pl.BlockSpec((tr, W), lambda i: (i, 0))
    return _call(
        body, name=name, grid=(S // tr,),
        in_specs=[row, pl.BlockSpec((tr, W), lambda i: (i, ub)), pl.BlockSpec((1, W), lambda i: (0, 0))],
        out_specs=[row, row],
        out_shape=[jax.ShapeDtypeStruct((S, W), F32), jax.ShapeDtypeStruct((S, W), BF16)],
        compiler_params=_cparams(("parallel",)),
    )(yc, P, dskip)


def _s5_out_bwd(dyg, y, P, dskip, *, name):
    S, W = y.shape
    tr = _pick(S, 256)
    ub = 3 * FOX_WIDTH // W

    def body(dyg_ref, y_ref, u_ref, d_ref, dy_ref, du_ref, dd_ref):
        y_ = y_ref[...]
        inner = _GELU_C * (y_ + 0.044715 * y_ * y_ * y_)
        t = jnp.tanh(inner)
        dgelu = 0.5 * (1.0 + t) + 0.5 * y_ * (1.0 - t * t) * _GELU_C * (1.0 + 3.0 * 0.044715 * y_ * y_)
        dy = dyg_ref[...] * dgelu
        dy_ref[...] = dy.astype(BF16)
        du_ref[...] = d_ref[...] * dy
        part = jnp.sum(dy * u_ref[...], axis=0, keepdims=True)

        @pl.when(pl.program_id(0) == 0)
        def _():
            dd_ref[...] = part

        @pl.when(pl.program_id(0) > 0)
        def _():
            dd_ref[...] += part

    row = pl.BlockSpec((tr, W), lambda i: (i, 0))
    vec = pl.BlockSpec((1, W), lambda i: (0, 0))
    return _call(
        body, name=name, grid=(S // tr,),
        in_specs=[row, row, pl.BlockSpec((tr, W), lambda i: (i, ub)), vec],
        out_specs=[row, row, vec],
        out_shape=[jax.ShapeDtypeStruct((S, W), BF16), jax.ShapeDtypeStruct((S, W), F32),
                   jax.ShapeDtypeStruct((1, W), F32)],
        compiler_params=_cparams(("arbitrary",)),
    )(dyg, y, P, dskip)


def _glu_fwd(z, *, name):
    S, W2 = z.shape
    W = W2 // 2
    tr = _pick(S, 256)

    def body(z1_ref, z2_ref, o_ref):
        o_ref[...] = (z1_ref[...] * jax.nn.sigmoid(z2_ref[...])).astype(BF16)

    return _call(
        body, name=name, grid=(S // tr,),
        in_specs=[pl.BlockSpec((tr, W), lambda i: (i, 0)), pl.BlockSpec((tr, W), lambda i: (i, 1))],
        out_specs=pl.BlockSpec((tr, W), lambda i: (i, 0)),
        out_shape=jax.ShapeDtypeStruct((S, W), BF16),
        compiler_params=_cparams(("parallel",)),
    )(z, z)


def _glu_bwd(z, dcat, *, name):
    S, W2 = z.shape
    W = W2 // 2
    tr = _pick(S, 256)

    def body(z1_ref, z2_ref, d_ref, dz1_ref, dz2_ref):
        sg = jax.nn.sigmoid(z2_ref[...])
        d = d_ref[...]
        dz1_ref[...] = (d * sg).astype(BF16)
        dz2_ref[...] = (d * z1_ref[...] * sg * (1.0 - sg)).astype(BF16)

    lo = pl.BlockSpec((tr, W), lambda i: (i, 0))
    hi = pl.BlockSpec((tr, W), lambda i: (i, 1))
    dz1, dz2 = _call(
        body, name=name, grid=(S // tr,), in_specs=[lo, hi, hi], out_specs=[lo, lo],
        out_shape=[jax.ShapeDtypeStruct((S, W), BF16)] * 2,
        compiler_params=_cparams(("parallel",)),
    )(z, z, dcat)
    return jnp.concatenate([dz1, dz2], axis=1)


ACT_ROWS = 16
ACT_COLS = 256


def _shift_down(cur, prev, k, row):
    return jnp.where(row >= k, pltpu.roll(cur, k, 0), pltpu.roll(prev, k, 0))


def _shift_up(cur, nxt, k, row):
    n = cur.shape[0]
    return jnp.where(row < n - k, pltpu.roll(cur, n - k, 0), pltpu.roll(nxt, n - k, 0))


def _act_fwd(h, cw, cb, *, name):
    _, S, FP = h.shape
    tr = _pick(S, 256)
    hb = tr // ACT_ROWS
    nq = tr // ACT_ROWS

    def body(g_ref, gh_ref, v_ref, vh_ref, wg_ref, wv_ref, bg_ref, bv_ref, a_ref, hc_ref):
        first = pl.program_id(1) == 0
        for c0 in range(0, FP, ACT_COLS):
            cw_ = min(ACT_COLS, FP - c0)
            cols = pl.ds(c0, cw_)
            rw = lax.broadcasted_iota(jnp.int32, (ACT_ROWS, cw_), 0)
            wg = [wg_ref[pl.ds(k, 1), cols] for k in range(3)]
            wv = [wv_ref[pl.ds(k, 1), cols] for k in range(3)]
            bg, bv = bg_ref[:, cols], bv_ref[:, cols]
            halo_g = jnp.where(first, 0.0, gh_ref[:, cols])
            halo_v = jnp.where(first, 0.0, vh_ref[:, cols])

            def chunk(q, _):
                rows = pl.ds(pl.multiple_of(q * ACT_ROWS, ACT_ROWS), ACT_ROWS)
                before = pl.ds(pl.multiple_of(jnp.maximum(q - 1, 0) * ACT_ROWS, ACT_ROWS), ACT_ROWS)
                g, v = g_ref[rows, cols], v_ref[rows, cols]
                gp = jnp.where(q > 0, g_ref[before, cols], halo_g)
                vp = jnp.where(q > 0, v_ref[before, cols], halo_v)
                cg = bg + wg[2] * g + wg[1] * _shift_down(g, gp, 1, rw) + wg[0] * _shift_down(g, gp, 2, rw)
                cv = bv + wv[2] * v + wv[1] * _shift_down(v, vp, 1, rw) + wv[0] * _shift_down(v, vp, 2, rw)
                a_ref[rows, cols] = (cg * jax.nn.sigmoid(cg) * cv).astype(BF16)
                hc_ref[0, rows, cols] = cg
                hc_ref[1, rows, cols] = cv
                return 0

            lax.fori_loop(0, nq, chunk, 0, unroll=2)

    def main(off):
        return pl.BlockSpec((None, tr, FP), lambda j, i: (j + off, i, 0))

    def halo(off):
        return pl.BlockSpec((None, ACT_ROWS, FP), lambda j, i: (j + off, jnp.maximum(i * hb - 1, 0), 0))

    def wspec(off):
        return pl.BlockSpec((None, 3, FP), lambda j, i: (j + off, 0, 0))

    def bspec(off):
        return pl.BlockSpec((None, 1, FP), lambda j, i: (j + off, 0, 0))

    cb3 = cb.reshape(4, 1, FP)
    return _call(
        body, name=name, grid=(2, S // tr),
        in_specs=[main(0), halo(0), main(2), halo(2), wspec(0), wspec(2), bspec(0), bspec(2)],
        out_specs=[pl.BlockSpec((None, tr, FP), lambda j, i: (j, i, 0)),
                   pl.BlockSpec((None, 2, tr, FP), lambda j, i: (j, 0, i, 0))],
        out_shape=[jax.ShapeDtypeStruct((2, S, FP), BF16), jax.ShapeDtypeStruct((2, 2, S, FP), F32)],
        compiler_params=_cparams(("parallel", "parallel")),
    )(h, h, h, h, cw, cw, cb3, cb3)


def _act_bwd(h, hc, da, cw, *, name):
    _, S, FP = h.shape
    tr = _pick(S, 256)
    nq = tr // ACT_ROWS
    nr = S // tr
    half = ACT_ROWS // 2

    def fold(x):
        return x[:half] + x[half:]

    def body(g_ref, v_ref, hc_ref, da_ref, wg_ref, wv_ref,
             dh_ref, dwg_ref, dwv_ref, dbg_ref, dbv_ref, carry_g, carry_v):
        i = pl.program_id(1)
        bottom = i == 0
        for c0 in range(0, FP, ACT_COLS):
            cw_ = min(ACT_COLS, FP - c0)
            cols = pl.ds(c0, cw_)
            rw = lax.broadcasted_iota(jnp.int32, (ACT_ROWS, cw_), 0)
            wg = [wg_ref[pl.ds(k, 1), cols] for k in range(3)]
            wv = [wv_ref[pl.ds(k, 1), cols] for k in range(3)]
            after_g = jnp.where(bottom, 0.0, carry_g[:, cols])
            after_v = jnp.where(bottom, 0.0, carry_v[:, cols])

            def chunk(s, carry):
                ng, nv, acc = carry[0], carry[1], carry[2:]
                q = nq - 1 - s
                rows = pl.ds(pl.multiple_of(q * ACT_ROWS, ACT_ROWS), ACT_ROWS)
                g, v = g_ref[rows, cols], v_ref[rows, cols]
                cg, cv = hc_ref[0, rows, cols], hc_ref[1, rows, cols]
                sg = jax.nn.sigmoid(cg)
                d = da_ref[rows, cols]
                dcg = d * cv * sg * (1.0 + cg * (1.0 - sg))
                dcv = d * cg * sg
                ug1, ug2 = _shift_up(dcg, ng, 1, rw), _shift_up(dcg, ng, 2, rw)
                uv1, uv2 = _shift_up(dcv, nv, 1, rw), _shift_up(dcv, nv, 2, rw)
                dh_ref[0, rows, cols] = (wg[2] * dcg + wg[1] * ug1 + wg[0] * ug2).astype(BF16)
                dh_ref[1, rows, cols] = (wv[2] * dcv + wv[1] * uv1 + wv[0] * uv2).astype(BF16)
                terms = (ug2 * g, ug1 * g, dcg * g, dcg, uv2 * v, uv1 * v, dcv * v, dcv)
                return (dcg, dcv) + tuple(a + fold(t) for a, t in zip(acc, terms))

            zero = jnp.zeros((half, cw_), F32)
            out = lax.fori_loop(0, nq, chunk, (after_g, after_v) + (zero,) * 8, unroll=2)
            carry_g[:, cols] = out[0]
            carry_v[:, cols] = out[1]
            sums = [jnp.sum(a, axis=0, keepdims=True) for a in out[2:]]

            @pl.when(bottom)
            def _():
                for k in range(3):
                    dwg_ref[pl.ds(k, 1), cols] = sums[k]
                    dwv_ref[pl.ds(k, 1), cols] = sums[4 + k]
                dbg_ref[:, cols] = sums[3]
                dbv_ref[:, cols] = sums[7]

            @pl.when(jnp.logical_not(bottom))
            def _():
                for k in range(3):
                    dwg_ref[pl.ds(k, 1), cols] += sums[k]
                    dwv_ref[pl.ds(k, 1), cols] += sums[4 + k]
                dbg_ref[:, cols] += sums[3]
                dbv_ref[:, cols] += sums[7]

    def main(off):
        return pl.BlockSpec((None, tr, FP), lambda j, i: (j + off, nr - 1 - i, 0))

    def wspec(off):
        return pl.BlockSpec((None, 3, FP), lambda j, i: (j + off, 0, 0))

    bspec = pl.BlockSpec((None, 1, FP), lambda j, i: (j, 0, 0))
    pair = pl.BlockSpec((None, 2, tr, FP), lambda j, i: (j, 0, nr - 1 - i, 0))
    dh, dwg, dwv, dbg, dbv = _call(
        body, name=name, grid=(2, nr),
        in_specs=[main(0), main(2), pair, main(0), wspec(0), wspec(2)],
        out_specs=[pair, wspec(0), wspec(0), bspec, bspec],
        out_shape=[jax.ShapeDtypeStruct((2, 2, S, FP), BF16)]
        + [jax.ShapeDtypeStruct((2, 3, FP), F32)] * 2 + [jax.ShapeDtypeStruct((2, 1, FP), F32)] * 2,
        scratch_shapes=[pltpu.VMEM((ACT_ROWS, FP), F32), pltpu.VMEM((ACT_ROWS, FP), F32)],
        compiler_params=_cparams(("parallel", "arbitrary")),
    )(h, h, hc, da, cw, cw)
    return (dh.reshape(4, S, FP), jnp.concatenate([dwg, dwv], axis=0), jnp.concatenate([dbg, dbv], axis=0))


def _rope_tables(posf, *, name, after=()):
    S = posf.shape[0]
    half = ROPE_DIM // 2
    d = np.arange(LANE) % SWA_HEAD_DIM
    invf = np.where(d < ROPE_DIM, ROPE_THETA ** (-(d % half).astype(np.float64) / half), 0.0).astype(np.float32)
    m_rot = (d < ROPE_DIM).astype(np.float32)
    m_a = (d < half).astype(np.float32)
    m_b = ((d >= half) & (d < ROPE_DIM)).astype(np.float32)
    consts = jnp.asarray(np.stack([invf, m_rot, m_a, m_b] + [np.zeros(LANE, np.float32)] * 4))

    def body(p_ref, k_ref, c_ref, sa_ref, sb_ref):
        k = k_ref[...]
        ang = p_ref[...] * k[0:1]
        co, si = jnp.cos(ang), jnp.sin(ang)
        c_ref[...] = k[1:2] * co + (1.0 - k[1:2])
        sa_ref[...] = -k[2:3] * si
        sb_ref[...] = k[3:4] * si

    full = pl.BlockSpec((S, LANE), lambda: (0, 0))
    return _call(
        body, after=after, name=name,
        in_specs=[pl.BlockSpec((S, 1), lambda: (0, 0)), pl.BlockSpec((8, LANE), lambda: (0, 0))],
        out_specs=[full] * 3, out_shape=[jax.ShapeDtypeStruct((S, LANE), F32)] * 3,
    )(posf, consts)


def _rope(xv, tabs_refs, width, inverse):
    rep = width // LANE
    c, sa, sb = (jnp.tile(t[...], (1, rep)) for t in tabs_refs)
    if not inverse:
        return xv * c + pltpu.roll(xv, width - 8, 1) * sa + pltpu.roll(xv, 8, 1) * sb
    return xv * c + pltpu.roll(xv * sa, 8, 1) + pltpu.roll(xv * sb, width - 8, 1)


def _to_heads(x, tabs, *, col0, width, rotate, name, out_dtype):
    S = x.shape[0]
    tr = _pick(S, 256)
    nh = width // SWA_HEAD_DIM
    cb = col0 // width

    def body(x_ref, c_ref, sa_ref, sb_ref, o_ref):
        xv = x_ref[...].astype(F32)
        if rotate:
            xv = _rope(xv, (c_ref, sa_ref, sb_ref), width, False)
        for h in range(nh):
            o_ref[h] = xv[:, h * SWA_HEAD_DIM:(h + 1) * SWA_HEAD_DIM].astype(out_dtype)

    tab = pl.BlockSpec((tr, LANE), lambda i: (i, 0))
    return _call(
        body, name=name, grid=(S // tr,),
        in_specs=[pl.BlockSpec((tr, width), lambda i: (i, cb)), tab, tab, tab],
        out_specs=pl.BlockSpec((nh, tr, SWA_HEAD_DIM), lambda i: (0, i, 0)),
        out_shape=jax.ShapeDtypeStruct((nh, S, SWA_HEAD_DIM), out_dtype),
        compiler_params=_cparams(("parallel",)),
    )(x, *tabs)


def _from_heads(x3, tabs, *, rotate_back, name, out_dtype, skip_rows=0):
    nh = x3.shape[0]
    S = x3.shape[1] - skip_rows
    width = nh * SWA_HEAD_DIM
    tr = _pick(S, 256) if skip_rows == 0 else skip_rows
    off = skip_rows // tr

    def body(x_ref, c_ref, sa_ref, sb_ref, o_ref):
        xv = jnp.concatenate([x_ref[h].astype(F32) for h in range(nh)], axis=1)
        if rotate_back:
            xv = _rope(xv, (c_ref, sa_ref, sb_ref), width, True)
        o_ref[...] = xv.astype(out_dtype)

    tab = pl.BlockSpec((tr, LANE), lambda i: (i, 0))
    return _call(
        body, name=name, grid=(S // tr,),
        in_specs=[pl.BlockSpec((nh, tr, SWA_HEAD_DIM), lambda i: (0, i + off, 0)), tab, tab, tab],
        out_specs=pl.BlockSpec((tr, width), lambda i: (i, 0)),
        out_shape=jax.ShapeDtypeStruct((S, width), out_dtype),
        compiler_params=_cparams(("parallel",)),
    )(x3, *tabs)


def _swa_mask(n):
    rows = SWA_GROUPS * SWA_WINDOW
    qi = lax.broadcasted_iota(jnp.int32, (rows, 2 * SWA_WINDOW), 0) & (SWA_WINDOW - 1)
    kj = lax.broadcasted_iota(jnp.int32, (rows, 2 * SWA_WINDOW), 1)
    rel = SWA_WINDOW + qi - kj
    return (rel >= 0) & (rel < SWA_WINDOW) & ((n > 0) | (kj >= SWA_WINDOW))


def _swa_fwd(qT, kT, vT, sink_rows, *, name):
    S = qT.shape[1]
    W, G, Dh = SWA_WINDOW, SWA_GROUPS, SWA_HEAD_DIM
    nb = S // W
    scale = 1.0 / math.sqrt(Dh)

    def body(q_ref, kp_ref, kc_ref, vp_ref, vc_ref, s_ref, o_ref, l_ref):
        n = pl.program_id(1)
        q = q_ref[...].reshape(G * W, Dh)
        kk = jnp.concatenate([kp_ref[...], kc_ref[...]], axis=0)
        vv = jnp.concatenate([vp_ref[...], vc_ref[...]], axis=0)
        s = lax.dot_general(q, kk, (((1,), (1,)), ((), ())), preferred_element_type=F32) * scale
        s = jnp.where(_swa_mask(n), s, -1e30)
        sink = s_ref[...]
        m = jnp.maximum(jnp.max(s, axis=-1, keepdims=True), sink)
        e = jnp.exp(s - m)
        den = jnp.sum(e, axis=-1, keepdims=True) + jnp.exp(sink - m)
        p = e / den
        o_ref[...] = jnp.dot(p.astype(BF16), vv, preferred_element_type=F32).reshape(G, W, Dh)
        l_ref[...] = (m + jnp.log(den)).reshape(G, W, 1)

    qs = pl.BlockSpec((G, W, Dh), lambda g, n: (g, n, 0))
    prev = pl.BlockSpec((None, W, Dh), lambda g, n: (g, jnp.maximum(n - 1, 0), 0))
    cur = pl.BlockSpec((None, W, Dh), lambda g, n: (g, n, 0))
    return _call(
        body, name=name, grid=(SWA_KV_HEADS, nb),
        in_specs=[qs, prev, cur, prev, cur, pl.BlockSpec((None, G * W, 1), lambda g, n: (g, 0, 0))],
        out_specs=[qs, pl.BlockSpec((G, W, 1), lambda g, n: (g, n, 0))],
        out_shape=[jax.ShapeDtypeStruct((SWA_HEADS, S, Dh), F32), jax.ShapeDtypeStruct((SWA_HEADS, S, 1), F32)],
        compiler_params=_cparams(("parallel", "parallel")),
    )(qT, kT, kT, vT, vT, sink_rows)


def _swa_bwd(qT, kT, vT, sink_rows, oT, L, doT, *, name):
    S = qT.shape[1]
    W, G, Dh = SWA_WINDOW, SWA_GROUPS, SWA_HEAD_DIM
    nb = S // W
    scale = 1.0 / math.sqrt(Dh)

    def body(q_ref, kp_ref, kc_ref, vp_ref, vc_ref, s_ref, o_ref, l_ref, do_ref,
             dq_ref, dk_ref, dv_ref, ds_ref):
        n = pl.program_id(1)
        q = q_ref[...].reshape(G * W, Dh)
        kk = jnp.concatenate([kp_ref[...], kc_ref[...]], axis=0)
        vv = jnp.concatenate([vp_ref[...], vc_ref[...]], axis=0)
        s = lax.dot_general(q, kk, (((1,), (1,)), ((), ())), preferred_element_type=F32) * scale
        lrow = l_ref[...].reshape(G * W, 1)
        p = jnp.where(_swa_mask(n), jnp.exp(s - lrow), 0.0)
        do = do_ref[...].reshape(G * W, Dh)
        do_bf = do.astype(BF16)
        dp = lax.dot_general(do_bf, vv, (((1,), (1,)), ((), ())), preferred_element_type=F32)
        delta = jnp.sum(do * o_ref[...].reshape(G * W, Dh), axis=-1, keepdims=True)
        dsc = p * (dp - delta)
        ds_bf = dsc.astype(BF16)
        dq_ref[...] = (jnp.dot(ds_bf, kk, preferred_element_type=F32) * scale).astype(BF16).reshape(G, W, Dh)
        dkk = lax.dot_general(ds_bf, q, (((0,), (0,)), ((), ())), preferred_element_type=F32) * scale
        dvv = lax.dot_general(p.astype(BF16), do_bf, (((0,), (0,)), ((), ())), preferred_element_type=F32)
        dsk = -jnp.exp(s_ref[...] - lrow) * delta
        dsk = jnp.broadcast_to(jnp.sum(dsk.reshape(G, W, 1), axis=1), (G, LANE))

        @pl.when(n == 0)
        def _():
            dk_ref[...] = jnp.zeros_like(dk_ref)
            dv_ref[...] = jnp.zeros_like(dv_ref)
            ds_ref[...] = jnp.zeros_like(ds_ref)

        rows = pl.ds(pl.multiple_of(n * W, W), 2 * W)
        dk_ref[rows, :] += dkk
        dv_ref[rows, :] += dvv
        ds_ref[...] += dsk

    qs = pl.BlockSpec((G, W, Dh), lambda g, n: (g, n, 0))
    prev = pl.BlockSpec((None, W, Dh), lambda g, n: (g, jnp.maximum(n - 1, 0), 0))
    cur = pl.BlockSpec((None, W, Dh), lambda g, n: (g, n, 0))
    lsp = pl.BlockSpec((G, W, 1), lambda g, n: (g, n, 0))
    kvo = pl.BlockSpec((None, S + W, Dh), lambda g, n: (g, 0, 0))
    return _call(
        body, name=name, grid=(SWA_KV_HEADS, nb),
        in_specs=[qs, prev, cur, prev, cur, pl.BlockSpec((None, G * W, 1), lambda g, n: (g, 0, 0)), qs, lsp, qs],
        out_specs=[qs, kvo, kvo, pl.BlockSpec((None, G, LANE), lambda g, n: (g, 0, 0))],
        out_shape=[jax.ShapeDtypeStruct((SWA_HEADS, S, Dh), BF16),
                   jax.ShapeDtypeStruct((SWA_KV_HEADS, S + W, Dh), F32),
                   jax.ShapeDtypeStruct((SWA_KV_HEADS, S + W, Dh), F32),
                   jax.ShapeDtypeStruct((SWA_KV_HEADS, G, LANE), F32)],
        compiler_params=_cparams(("parallel", "arbitrary")),
    )(qT, kT, kT, vT, vT, sink_rows, oT, L, doT)


def _adamw(w, g, m, v, *, name, tr=128, by_cols=False):
    L, R, C = w.shape
    split = isinstance(g, (list, tuple))
    HR, HC = _half_shape(R, C, by_cols) if split else (R, C)
    tr, tc = _tile2d(HR, HC, tr)
    nr, nc = HR // tr, HC // tc
    c1 = 1.0 / (1.0 - ADAM_B1 ** ADAM_STEP)
    c2 = 1.0 / (1.0 - ADAM_B2 ** ADAM_STEP)
    ng = 2 * L if split else 1

    def body(c_ref, *refs):
        w_ref, g_refs, (m_ref, v_ref, go_ref, d_ref, mo_ref, vo_ref) = refs[0], refs[1:1 + ng], refs[1 + ng:]
        if split:
            mine = pl.program_id(1) == c_ref[0]
            g_ = jnp.where(mine, g_refs[0][...], g_refs[1][...])
            for l in range(1, L):
                g_ = jnp.where(pl.program_id(0) == l,
                               jnp.where(mine, g_refs[2 * l][...], g_refs[2 * l + 1][...]), g_)
        else:
            g_ = g_refs[0][...]
        mn = ADAM_B1 * m_ref[...] + (1.0 - ADAM_B1) * g_
        vn = ADAM_B2 * v_ref[...] + (1.0 - ADAM_B2) * (g_ * g_)
        go_ref[...] = g_
        mo_ref[...] = mn
        vo_ref[...] = vn
        d_ref[...] = -ADAM_LR * ((mn * c1) / (jnp.sqrt(vn * c2) + ADAM_EPS) + ADAM_WD * w_ref[...])

    def whole(l, hf, i, j, c):
        return (l, i, hf * nc + j) if by_cols else (l, hf * nr + i, j)

    def half(layer, own):
        def index(l, hf, i, j, c):
            used = (l == layer) & ((hf == c[0]) if own else (hf != c[0]))
            return jnp.where(used, i, 0), jnp.where(used, j, 0)
        return pl.BlockSpec((tr, tc), index)

    row = pl.BlockSpec((None, tr, tc), whole)
    gs = [h for pair in g for h in pair] if split else [g]
    g_specs = [half(l, own) for l in range(L) for own in (True, False)] if split else [row]
    core = lax.axis_index("c").astype(jnp.int32).reshape(1)
    return _call(
        body, name=name,
        grid_spec=pltpu.PrefetchScalarGridSpec(
            num_scalar_prefetch=1, grid=(L, 2 if split else 1, nr, nc),
            in_specs=[row] + g_specs + [row, row], out_specs=[row] * 4),
        out_shape=[jax.ShapeDtypeStruct((L, R, C), F32)] * 4,
        compiler_params=_cparams(("parallel",) * 4),
    )(core, w, *gs, m, v)


def _adamw_half(w, g, m, v, *, name, own, prev=None, tr=128, by_cols=False):
    L, R, C = w.shape
    HR, HC = _half_shape(R, C, by_cols)
    tr, tc = _tile2d(HR, HC, tr)
    nr, nc = HR // tr, HC // tc
    c1 = 1.0 / (1.0 - ADAM_B1 ** ADAM_STEP)
    c2 = 1.0 / (1.0 - ADAM_B2 ** ADAM_STEP)

    def body(c_ref, *refs):
        w_ref, g_refs, m_ref, v_ref = refs[0], refs[1:1 + L], refs[1 + L], refs[2 + L]
        go_ref, d_ref, mo_ref, vo_ref = refs[-4:]
        g_ = g_refs[0][...]
        for l in range(1, L):
            g_ = jnp.where(pl.program_id(0) == l, g_refs[l][...], g_)
        mn = ADAM_B1 * m_ref[...] + (1.0 - ADAM_B1) * g_
        vn = ADAM_B2 * v_ref[...] + (1.0 - ADAM_B2) * (g_ * g_)
        go_ref[...] = g_
        mo_ref[...] = mn
        vo_ref[...] = vn
        d_ref[...] = -ADAM_LR * ((mn * c1) / (jnp.sqrt(vn * c2) + ADAM_EPS) + ADAM_WD * w_ref[...])

    def whole(l, i, j, c):
        hf = c[0] if own else 1 - c[0]
        return (l, i, hf * nc + j) if by_cols else (l, hf * nr + i, j)

    def layer_half(layer):
        def index(l, i, j, c):
            return jnp.where(l == layer, i, 0), jnp.where(l == layer, j, 0)
        return pl.BlockSpec((tr, tc), index)

    row = pl.BlockSpec((None, tr, tc), whole)
    core = lax.axis_index("c").astype(jnp.int32).reshape(1)
    prev = list(prev) if prev is not None else []
    return _call(
        body, name=name,
        grid_spec=pltpu.PrefetchScalarGridSpec(
            num_scalar_prefetch=1, grid=(L, nr, nc),
            in_specs=[row] + [layer_half(l) for l in range(L)] + [row, row] + [ANY] * len(prev),
            out_specs=[row] * 4),
        out_shape=[jax.ShapeDtypeStruct((L, R, C), F32)] * 4,
        input_output_aliases={4 + L + k: k for k in range(len(prev))},
        compiler_params=_cparams(("parallel",) * 3),
    )(core, w, *g, m, v, *prev)


def _sum2_halves(g4, s4, by_cols, *, name):
    n, R, C = g4.shape
    HR, HC = _half_shape(R, C, by_cols)
    tr, tc = _tile2d(HR, HC, budget=1024 * 1024)
    nr, nc = HR // tr, HC // tc
    core = lax.axis_index("c").astype(jnp.int32).reshape(1)

    def body(c_ref, g_ref, s_ref, o_ref):
        o_ref[...] = (g_ref[...].astype(F32) + s_ref[...].astype(F32)).astype(BF16)

    def mine(k, i, j, c):
        return (k, i, c[0] * nc + j) if by_cols else (k, c[0] * nr + i, j)

    blk = pl.BlockSpec((None, tr, tc), lambda k, i, j, c: (k, i, j))
    return _call(
        body, name=name,
        grid_spec=pltpu.PrefetchScalarGridSpec(
            num_scalar_prefetch=1, grid=(n, nr, nc),
            in_specs=[pl.BlockSpec((None, tr, tc), mine), blk], out_specs=blk),
        out_shape=jax.ShapeDtypeStruct((n, HR, HC), BF16),
        compiler_params=_cparams(("parallel", "parallel", "parallel")),
    )(core, g4, s4)


def _rowsum(parts, *, name, out_dtype=F32):
    n, R, C = parts.shape
    tr, tc = _tile2d(R, C, budget=512 * 1024)

    def body(p_ref, o_ref):
        acc = p_ref[0].astype(F32)
        for i in range(1, n):
            acc = acc + p_ref[i].astype(F32)
        o_ref[...] = acc.astype(out_dtype)

    return _call(
        body, name=name, grid=(R // tr, C // tc),
        in_specs=[pl.BlockSpec((n, tr, tc), lambda i, j: (0, i, j))],
        out_specs=pl.BlockSpec((tr, tc), lambda i, j: (i, j)),
        out_shape=jax.ShapeDtypeStruct((R, C), out_dtype),
        compiler_params=_cparams(("parallel", "parallel")),
    )(parts)


def _where_am_i():
    x, y, c = lax.axis_index("x"), lax.axis_index("y"), lax.axis_index("c")
    chips = [(1 - x, y), (x, 1 - y), (1 - x, 1 - y)]
    return x, y, c, chips


def _half_idx(rows, cols, by_cols, which):
    if by_cols:
        hc = cols // 2
        return (slice(None), pl.ds(pl.multiple_of(which * hc, LANE), hc))
    hr = rows // 2
    return (pl.ds(pl.multiple_of(which * hr, 16), hr), slice(None))


def _half_shape(rows, cols, by_cols):
    return (rows, cols // 2) if by_cols else (rows // 2, cols)


HBM_SPEC = pl.BlockSpec(memory_space=pltpu.HBM)
SEM_SPEC = pl.BlockSpec(memory_space=pltpu.SEMAPHORE)
DATAFLOW = pltpu.SideEffectType.DATAFLOW_SIDE_EFFECTING


def _chip_exchange_refs(kind, shards_shape, by_cols, src, land, i, chip_k, c, me):
    if kind == 'gather':
        half = _half_idx(*shards_shape, by_cols, c)
        return src.at[half], land.at[(me,) + half], land.at[(chip_k,) + half]
    return src.at[chip_k], land.at[me], land.at[chip_k]


def _chip_exchange_start(kind, srcs, by_cols, *, name, after=()):
    n = len(srcs)
    land_shapes = [((N_CHIPS,) + s.shape) if kind == 'gather' else s.shape for s in srcs]

    def body(*refs):
        src_refs, land_refs = refs[:n], refs[n:2 * n]
        send, recv = refs[2 * n + len(after)], refs[2 * n + len(after) + 1]
        token = refs[-1]
        x, y, c, chips = _where_am_i()
        me = 2 * x + y
        for i in range(n):
            for k, (px, py) in enumerate(chips):
                s, d, _ = _chip_exchange_refs(kind, srcs[i].shape, by_cols[i], src_refs[i], land_refs[i], i,
                                              2 * px + py, c, me)
                pltpu.make_async_remote_copy(src_ref=s, dst_ref=d, send_sem=send.at[3 * i + k],
                                             recv_sem=recv.at[3 * i + k], device_id=(px, py, c),
                                             device_id_type=MESH).start()
        token[...] = jnp.zeros_like(token)

    lands = [pltpu.with_memory_space_constraint(lax.empty(sh, s.dtype), pltpu.HBM) for sh, s in zip(land_shapes, srcs)]
    outs = _call(
        body, name=name,
        out_shape=(pltpu.SemaphoreType.DMA((3 * n,)), pltpu.SemaphoreType.DMA((3 * n,)),
                   *[pltpu.HBM(s.shape, s.dtype) for s in srcs],
                   *[pltpu.HBM(sh, s.dtype) for sh, s in zip(land_shapes, srcs)],
                   jax.ShapeDtypeStruct((8, LANE), F32)),
        in_specs=[HBM_SPEC] * (2 * n) + [ANY] * len(after),
        out_specs=(SEM_SPEC, SEM_SPEC, *([HBM_SPEC] * (2 * n)), pl.BlockSpec(memory_space=pltpu.VMEM)),
        input_output_aliases={j: 2 + j for j in range(2 * n)},
        compiler_params=pltpu.CompilerParams(has_side_effects=DATAFLOW),
    )(*[pltpu.with_memory_space_constraint(s, pltpu.HBM) for s in srcs], *lands, *after)
    return outs[0], outs[1], list(outs[2:2 + n]), list(outs[2 + n:2 + 2 * n]), outs[-1]


def _chip_exchange_wait(kind, send, recv, srcs, lands, by_cols, after, *, name):
    n = len(srcs)

    def body(*refs):
        src_refs, land_refs = refs[:n], refs[n:2 * n]
        send_r, recv_r = refs[2 * n], refs[2 * n + 1]
        x, y, c, chips = _where_am_i()
        me = 2 * x + y
        for i in range(n):
            for k, (px, py) in enumerate(chips):
                s, _, d = _chip_exchange_refs(kind, srcs[i].shape, by_cols[i], src_refs[i], land_refs[i], i,
                                              2 * px + py, c, me)
                cp = pltpu.make_async_remote_copy(src_ref=s, dst_ref=d, send_sem=send_r.at[3 * i + k],
                                                  recv_sem=recv_r.at[3 * i + k], device_id=(px, py, c),
                                                  device_id_type=MESH)
                cp.wait_send()
                cp.wait_recv()

    outs = _call(
        body, name=name,
        out_shape=(*[pltpu.HBM(s.shape, s.dtype) for s in srcs], *[pltpu.HBM(l.shape, l.dtype) for l in lands]),
        in_specs=[HBM_SPEC] * (2 * n) + [SEM_SPEC, SEM_SPEC] + [ANY] * len(after),
        out_specs=tuple([HBM_SPEC] * (2 * n)),
        input_output_aliases={j: j for j in range(2 * n)},
        compiler_params=pltpu.CompilerParams(has_side_effects=DATAFLOW),
    )(*srcs, *lands, send, recv, *after)
    return list(outs[:n]), list(outs[n:])


def _sibling_halves_start(grads, by_cols, *, name, after=()):
    n = len(grads)
    land_shapes = [(N_CHIPS,) + _half_shape(*g.shape[1:], bc) for g, bc in zip(grads, by_cols)]

    def body(*refs):
        src_refs, land_refs = refs[:n], refs[n:2 * n]
        send, recv = refs[2 * n + len(after)], refs[2 * n + len(after) + 1]
        token = refs[-1]
        x, y, c, _ = _where_am_i()
        for i in range(n):
            src = src_refs[i].at[(slice(None),) + _half_idx(*grads[i].shape[1:], by_cols[i], 1 - c)]
            pltpu.make_async_remote_copy(src_ref=src, dst_ref=land_refs[i], send_sem=send.at[i], recv_sem=recv.at[i],
                                         device_id=(x, y, 1 - c), device_id_type=MESH).start()
        token[...] = jnp.zeros_like(token)

    lands = [pltpu.with_memory_space_constraint(lax.empty(sh, g.dtype), pltpu.HBM) for sh, g in zip(land_shapes, grads)]
    outs = _call(
        body, name=name,
        out_shape=(pltpu.SemaphoreType.DMA((n,)), pltpu.SemaphoreType.DMA((n,)),
                   *[pltpu.HBM(g.shape, g.dtype) for g in grads],
                   *[pltpu.HBM(sh, g.dtype) for sh, g in zip(land_shapes, grads)],
                   jax.ShapeDtypeStruct((8, LANE), F32)),
        in_specs=[HBM_SPEC] * (2 * n) + [ANY] * len(after),
        out_specs=(SEM_SPEC, SEM_SPEC, *([HBM_SPEC] * (2 * n)), pl.BlockSpec(memory_space=pltpu.VMEM)),
        input_output_aliases={j: 2 + j for j in range(2 * n)},
        compiler_params=pltpu.CompilerParams(has_side_effects=DATAFLOW),
    )(*[pltpu.with_memory_space_constraint(g, pltpu.HBM) for g in grads], *lands, *after)
    return outs[0], outs[1], list(outs[2:2 + n]), list(outs[2 + n:2 + 2 * n]), outs[-1]


def _sibling_halves_wait(send, recv, grads, lands, by_cols, after, *, name):
    n = len(grads)

    def body(*refs):
        src_refs, land_refs = refs[:n], refs[n:2 * n]
        send_r, recv_r = refs[2 * n], refs[2 * n + 1]
        x, y, c, _ = _where_am_i()
        for i in range(n):
            src = src_refs[i].at[(slice(None),) + _half_idx(*grads[i].shape[1:], by_cols[i], 1 - c)]
            cp = pltpu.make_async_remote_copy(src_ref=src, dst_ref=land_refs[i], send_sem=send_r.at[i],
                                              recv_sem=recv_r.at[i], device_id=(x, y, 1 - c), device_id_type=MESH)
            cp.wait_send()
            cp.wait_recv()

    outs = _call(
        body, name=name,
        out_shape=(*[pltpu.HBM(g.shape, g.dtype) for g in grads], *[pltpu.HBM(l.shape, l.dtype) for l in lands]),
        in_specs=[HBM_SPEC] * (2 * n) + [SEM_SPEC, SEM_SPEC] + [ANY] * len(after),
        out_specs=tuple([HBM_SPEC] * (2 * n)),
        input_output_aliases={j: j for j in range(2 * n)},
        compiler_params=pltpu.CompilerParams(has_side_effects=DATAFLOW),
    )(*grads, *lands, send, recv, *after)
    return list(outs[:n]), list(outs[n:])


def _sibling_swap_start(arrs, *, name, after=()):
    n = len(arrs)

    def body(*refs):
        src_refs, land_refs = refs[:n], refs[n:2 * n]
        send, recv = refs[2 * n + len(after)], refs[2 * n + len(after) + 1]
        token = refs[-1]
        x, y, c, _ = _where_am_i()
        for i in range(n):
            pltpu.make_async_remote_copy(src_ref=src_refs[i], dst_ref=land_refs[i], send_sem=send.at[i],
                                         recv_sem=recv.at[i], device_id=(x, y, 1 - c), device_id_type=MESH).start()
        token[...] = jnp.zeros_like(token)

    lands = [pltpu.with_memory_space_constraint(lax.empty(a.shape, a.dtype), pltpu.HBM) for a in arrs]
    outs = _call(
        body, name=name,
        out_shape=(pltpu.SemaphoreType.DMA((n,)), pltpu.SemaphoreType.DMA((n,)),
                   *[pltpu.HBM(a.shape, a.dtype) for a in arrs] * 2, jax.ShapeDtypeStruct((8, LANE), F32)),
        in_specs=[HBM_SPEC] * (2 * n) + [ANY] * len(after),
        out_specs=(SEM_SPEC, SEM_SPEC, *([HBM_SPEC] * (2 * n)), pl.BlockSpec(memory_space=pltpu.VMEM)),
        input_output_aliases={j: 2 + j for j in range(2 * n)},
        compiler_params=pltpu.CompilerParams(has_side_effects=DATAFLOW),
    )(*[pltpu.with_memory_space_constraint(a, pltpu.HBM) for a in arrs], *lands, *after)
    return outs[0], outs[1], list(outs[2:2 + n]), list(outs[2 + n:2 + 2 * n]), outs[-1]


def _sibling_swap_wait(send, recv, arrs, lands, after, *, name):
    n = len(arrs)

    def body(*refs):
        src_refs, land_refs = refs[:n], refs[n:2 * n]
        send_r, recv_r = refs[2 * n], refs[2 * n + 1]
        x, y, c, _ = _where_am_i()
        for i in range(n):
            cp = pltpu.make_async_remote_copy(src_ref=src_refs[i], dst_ref=land_refs[i], send_sem=send_r.at[i],
                                              recv_sem=recv_r.at[i], device_id=(x, y, 1 - c), device_id_type=MESH)
            cp.wait_send()
            cp.wait_recv()

    outs = _call(
        body, name=name,
        out_shape=tuple(pltpu.HBM(a.shape, a.dtype) for a in list(arrs) + list(lands)),
        in_specs=[HBM_SPEC] * (2 * n) + [SEM_SPEC, SEM_SPEC] + [ANY] * len(after),
        out_specs=tuple([HBM_SPEC] * (2 * n)),
        input_output_aliases={j: j for j in range(2 * n)},
        compiler_params=pltpu.CompilerParams(has_side_effects=DATAFLOW),
    )(*arrs, *lands, send, recv, *after)
    return list(outs[:n]), list(outs[n:])


def _sibling_pass_gathered(lands, shard_shapes, by_cols, *, name):
    n = len(lands)

    def body(*refs):
        outs = refs[n:2 * n]
        send, recv = refs[2 * n:]
        x, y, c, chips = _where_am_i()
        sibling = (x, y, 1 - c)
        cps = []
        for i in range(n):
            for k, (px, py) in enumerate(chips):
                blk = outs[i].at[(2 * px + py,) + _half_idx(*shard_shapes[i], by_cols[i], c)]
                d = pltpu.make_async_remote_copy(src_ref=blk, dst_ref=blk, send_sem=send.at[i, k],
                                                 recv_sem=recv.at[i, k], device_id=sibling, device_id_type=MESH)
                d.start()
                cps.append(d)
        for i in range(n):
            for k, (px, py) in enumerate(chips):
                blk = outs[i].at[(2 * px + py,) + _half_idx(*shard_shapes[i], by_cols[i], 1 - c)]
                pltpu.make_async_remote_copy(src_ref=blk, dst_ref=blk, send_sem=send.at[i, k], recv_sem=recv.at[i, k],
                                             device_id=sibling, device_id_type=MESH).wait_recv()
        for d in cps:
            d.wait_send()

    return _call(
        body, name=name, in_specs=[ANY] * n, out_specs=[ANY] * n,
        out_shape=[jax.ShapeDtypeStruct(l.shape, l.dtype) for l in lands],
        input_output_aliases={j: j for j in range(n)},
        scratch_shapes=[pltpu.SemaphoreType.DMA((n, 3)), pltpu.SemaphoreType.DMA((n, 3))],
    )(*lands)


def _own_slot(lands, owns):
    me = 2 * lax.axis_index("x") + lax.axis_index("y")
    return [lax.dynamic_update_slice_in_dim(g, s, me, axis=0) for g, s in zip(lands, owns)]


def _sibling_send_halves(grads, by_cols, *, name):
    n = len(grads)

    def body(*refs):
        ins, outs = refs[:n], refs[n:2 * n]
        send, recv = refs[2 * n:]
        x, y, c, _ = _where_am_i()
        sibling = (x, y, 1 - c)
        cps = []
        for i in range(n):
            src = ins[i].at[(slice(None),) + _half_idx(*grads[i].shape[1:], by_cols[i], 1 - c)]
            d = pltpu.make_async_remote_copy(src_ref=src, dst_ref=outs[i], send_sem=send.at[i],
                                             recv_sem=recv.at[i], device_id=sibling, device_id_type=MESH)
            d.start()
            cps.append(d)
        for d in cps:
            d.wait()

    return _call(
        body, name=name, in_specs=[ANY] * n, out_specs=[ANY] * n,
        out_shape=[jax.ShapeDtypeStruct((N_CHIPS,) + _half_shape(*g.shape[1:], bc), g.dtype)
                   for g, bc in zip(grads, by_cols)],
        scratch_shapes=[pltpu.SemaphoreType.DMA((n,)), pltpu.SemaphoreType.DMA((n,))],
    )(*grads)


def _all_reduce_small(v, *, name, after=()):
    R, C = v.shape
    H = R // 2

    def body(v_ref, o_ref, sib, slots, send, recv):
        x, y, c, chips = _where_am_i()
        me = 2 * x + y
        sibling = (x, y, 1 - c)
        mine = pl.ds(pl.multiple_of(c * H, 8), H)
        other = pl.ds(pl.multiple_of((1 - c) * H, 8), H)

        def copy(k, src, dst, to):
            return pltpu.make_async_remote_copy(src_ref=src, dst_ref=dst, send_sem=send.at[k], recv_sem=recv.at[k],
                                                device_id=to, device_id_type=MESH)

        d = copy(0, v_ref.at[other], sib, sibling)
        d.start()
        d.wait()
        slots[me] = v_ref[mine, :] + sib[...]
        cps = [copy(1 + k, slots.at[me], slots.at[me], (px, py, c)) for k, (px, py) in enumerate(chips)]
        for d in cps:
            d.start()
        for k, (px, py) in enumerate(chips):
            blk = slots.at[2 * px + py]
            copy(1 + k, blk, blk, (px, py, c)).wait_recv()
        for d in cps:
            d.wait_send()
        o_ref[mine, :] = (slots[0] + slots[1]) + (slots[2] + slots[3])
        d = copy(4, o_ref.at[mine], o_ref.at[mine], sibling)
        d.start()
        copy(4, o_ref.at[other], o_ref.at[other], sibling).wait_recv()
        d.wait_send()

    vm = pl.BlockSpec(memory_space=pltpu.VMEM)
    return _call(
        body, after=after, name=name, in_specs=[vm], out_specs=vm,
        out_shape=jax.ShapeDtypeStruct((R, C), F32),
        scratch_shapes=[pltpu.VMEM((H, C), F32), pltpu.VMEM((N_CHIPS, H, C), F32),
                        pltpu.SemaphoreType.DMA((5,)), pltpu.SemaphoreType.DMA((5,))],
        compiler_params=pltpu.CompilerParams(vmem_limit_bytes=VMEM_LIMIT),
    )(v)


def _cols_from_shards(g):
    return jnp.transpose(g, (1, 0, 2)).reshape(g.shape[1], -1)


def _shards_from_cols(w):
    R, C4 = w.shape
    return jnp.transpose(w.reshape(R, N_CHIPS, C4 // N_CHIPS), (1, 0, 2))


def _pack(arrs):
    flat = []
    for a in arrs:
        f = a.reshape(-1).astype(F32)
        flat.append(jnp.pad(f, (0, _rup(f.shape[0], LANE) - f.shape[0])))
    v = jnp.concatenate(flat)
    rows = _rup(v.shape[0] // LANE, 16)
    v = jnp.pad(v, (0, rows * LANE - v.shape[0]))
    return v.reshape(rows, LANE)


def _unpack(v, shapes):
    flat = v.reshape(-1)
    out, off = [], 0
    for s in shapes:
        n = int(np.prod(s))
        out.append(flat[off:off + n].reshape(s))
        off += _rup(n, LANE)
    return out


def _ffn_fwd(x, Wup, Wdn, cw, cb, tag):
    h = _mm(x, Wup, 'nt', bmode='bo', tm=512, tn=4096, name=f"ffn_up_{tag}")
    a, hc = _act_fwd(h, cw, cb, name=f"ffn_act_{tag}")
    f = _mm(a, Wdn, 'nn', bmode='abr', tm=512, tn=1024, tk=4096, name=f"ffn_down_{tag}")
    return f, (h, hc), a


def _ffn_bwd(df, x, saved, a, Wup, Wdn, cw, tag):
    h, hc = saved
    da = _mm(df, Wdn, 'nt', bmode='bo', tm=512, tn=4096, name=f"ffn_da_{tag}")
    dWdn = _mm(a, df, 'tn', bmode='ao', tm=4096, tn=512, name=f"ffn_dwdn_{tag}", out_dtype=BF16)
    dh, dcw, dcb = _act_bwd(h, hc, da, cw, name=f"ffn_actb_{tag}")

    def shard_of(k):
        return (k % 2) * 2 + k // 2

    dx = _mm(dh, Wup, 'nn', bmode='abr', tm=512, tn=1024, tk=4096, name=f"ffn_dx_{tag}", b_map=shard_of)
    dWup = _mm(dh, x, 'tn', bmode='ao', tm=4096, tn=512, name=f"ffn_dwup_{tag}", out_dtype=BF16,
               o_map=shard_of)
    return dx, dWup, dWdn, dcw, dcb


def kernel(x, positions, ev_w_in, ev_b_f, ev_lambda_re, ev_lambda_im, ev_log_step, ev_ssm_b_re, ev_ssm_b_im, ev_ssm_c_re, ev_ssm_c_im, ev_ssm_d, ev_w_glu, ev_w_out, od_w_in, od_sinks, od_w_out, ln_mix_g, ln_mix_b, ffn_w_up, ffn_conv_w, ffn_conv_b, ffn_w_down, ln_ffn_g, ln_ffn_b, loss_target, m_ev_w_in, m_ev_b_f, m_ev_lambda_re, m_ev_lambda_im, m_ev_log_step, m_ev_ssm_b_re, m_ev_ssm_b_im, m_ev_ssm_c_re, m_ev_ssm_c_im, m_ev_ssm_d, m_ev_w_glu, m_ev_w_out, m_od_w_in, m_od_sinks, m_od_w_out, m_ln_mix_g, m_ln_mix_b, m_ffn_w_up, m_ffn_conv_w, m_ffn_conv_b, m_ffn_w_down, m_ln_ffn_g, m_ln_ffn_b, v_ev_w_in, v_ev_b_f, v_ev_lambda_re, v_ev_lambda_im, v_ev_log_step, v_ev_ssm_b_re, v_ev_ssm_b_im, v_ev_ssm_c_re, v_ev_ssm_c_im, v_ev_ssm_d, v_ev_w_glu, v_ev_w_out, v_od_w_in, v_od_sinks, v_od_w_out, v_ln_mix_g, v_ln_mix_b, v_ffn_w_up, v_ffn_conv_w, v_ffn_conv_b, v_ffn_w_down, v_ln_ffn_g, v_ln_ffn_b):
    W = dict(ev_w_in=ev_w_in, ev_b_f=ev_b_f, ev_lambda_re=ev_lambda_re, ev_lambda_im=ev_lambda_im, ev_log_step=ev_log_step, ev_ssm_b_re=ev_ssm_b_re, ev_ssm_b_im=ev_ssm_b_im, ev_ssm_c_re=ev_ssm_c_re, ev_ssm_c_im=ev_ssm_c_im, ev_ssm_d=ev_ssm_d, ev_w_glu=ev_w_glu, ev_w_out=ev_w_out, od_w_in=od_w_in, od_sinks=od_sinks, od_w_out=od_w_out, ln_mix_g=ln_mix_g, ln_mix_b=ln_mix_b, ffn_w_up=ffn_w_up, ffn_conv_w=ffn_conv_w, ffn_conv_b=ffn_conv_b, ffn_w_down=ffn_w_down, ln_ffn_g=ln_ffn_g, ln_ffn_b=ln_ffn_b)
    Mo = dict(ev_w_in=m_ev_w_in, ev_b_f=m_ev_b_f, ev_lambda_re=m_ev_lambda_re, ev_lambda_im=m_ev_lambda_im, ev_log_step=m_ev_log_step, ev_ssm_b_re=m_ev_ssm_b_re, ev_ssm_b_im=m_ev_ssm_b_im, ev_ssm_c_re=m_ev_ssm_c_re, ev_ssm_c_im=m_ev_ssm_c_im, ev_ssm_d=m_ev_ssm_d, ev_w_glu=m_ev_w_glu, ev_w_out=m_ev_w_out, od_w_in=m_od_w_in, od_sinks=m_od_sinks, od_w_out=m_od_w_out, ln_mix_g=m_ln_mix_g, ln_mix_b=m_ln_mix_b, ffn_w_up=m_ffn_w_up, ffn_conv_w=m_ffn_conv_w, ffn_conv_b=m_ffn_conv_b, ffn_w_down=m_ffn_w_down, ln_ffn_g=m_ln_ffn_g, ln_ffn_b=m_ln_ffn_b)
    Vo = dict(ev_w_in=v_ev_w_in, ev_b_f=v_ev_b_f, ev_lambda_re=v_ev_lambda_re, ev_lambda_im=v_ev_lambda_im, ev_log_step=v_ev_log_step, ev_ssm_b_re=v_ev_ssm_b_re, ev_ssm_b_im=v_ev_ssm_b_im, ev_ssm_c_re=v_ev_ssm_c_re, ev_ssm_c_im=v_ev_ssm_c_im, ev_ssm_d=v_ev_ssm_d, ev_w_glu=v_ev_w_glu, ev_w_out=v_ev_w_out, od_w_in=v_od_w_in, od_sinks=v_od_sinks, od_w_out=v_od_w_out, ln_mix_g=v_ln_mix_g, ln_mix_b=v_ln_mix_b, ffn_w_up=v_ffn_w_up, ffn_conv_w=v_ffn_conv_w, ffn_conv_b=v_ffn_conv_b, ffn_w_down=v_ffn_w_down, ln_ffn_g=v_ln_ffn_g, ln_ffn_b=v_ln_ffn_b)
    names = list(W.keys())
    big = ['ev_w_in', 'ev_w_glu', 'ev_w_out', 'od_w_in', 'od_w_out', 'ffn_w_up', 'ffn_w_down']

    S, D = x.shape[1], x.shape[2]
    x0 = x.reshape(S, D)
    tgt = loss_target.reshape(S, D)
    G, Pn, Cg = SSM_GROUPS, SSM_STATE, SSM_GROUP
    Fs = ffn_w_up.shape[2]
    FP = Fs
    Rd = ffn_w_down.shape[1]
    EIN = N_CHIPS * ev_w_in.shape[2]

    cwl = ffn_conv_w.reshape(-1)
    cw_rows = _rup(_rup(cwl.shape[0], LANE) // LANE, 32)
    cw_pad = jnp.pad(cwl, (0, cw_rows * LANE - cwl.shape[0])).reshape(cw_rows, LANE)
    transposed = ('ev_w_in', 'ffn_w_up')

    def view(n, a):
        return jnp.transpose(a, (0, 2, 1)) if n in transposed else a

    Wv = {n: view(n, W[n]) for n in big}
    big_e = [(n, l) for n in big for l in range(W[n].shape[0])]
    split_cols = {e: (Wv[e[0]].shape[1] // 2) % 16 != 0 for e in big_e}
    shard16 = {e: Wv[e[0]][e[1]].astype(BF16) for e in big_e}
    grp_now = [e for e in big_e if e[0].startswith('ev_')]
    grp_ffn0 = [('ffn_w_up', 0), ('ffn_w_down', 0)]
    grp_l1 = [('od_w_in', 0), ('od_w_out', 0), ('ffn_w_up', 1), ('ffn_w_down', 1)]
    src_now = [shard16[e] for e in grp_now]
    src_ffn0 = [shard16[e] for e in grp_ffn0] + [cw_pad]
    src_l1 = [shard16[e] for e in grp_l1]
    cols_now = [split_cols[e] for e in grp_now]
    cols_ffn0 = [split_cols[e] for e in grp_ffn0] + [False]
    cols_l1 = [split_cols[e] for e in grp_l1]
    ag_in = _chip_exchange_start('gather', src_now[:1], cols_now[:1], name="ag_in_start")
    ag_mix = _chip_exchange_start('gather', src_now[1:], cols_now[1:], name="ag_mix_start", after=[ag_in[4]])
    ag_ffn0 = _chip_exchange_start('gather', src_ffn0, cols_ffn0, name="ag_ffn0_start", after=[ag_mix[4]])
    ag_l1 = _chip_exchange_start('gather', src_l1, cols_l1, name="ag_l1_start", after=[ag_ffn0[4]])
    started = [ag_l1[4]]

    def finish_gather(started, srcs, cols, after, tag):
        send, recv, thru, lands, _ = started
        thru, lands = _chip_exchange_wait('gather', send, recv, thru, lands, cols, after, name=f"ag_{tag}_wait")
        lands = _sibling_pass_gathered(lands, [s.shape for s in srcs], cols, name=f"ag_{tag}_pass")
        return _own_slot(lands, [s[None] for s in thru])

    lam_r, lam_i = ev_lambda_re[0], ev_lambda_im[0]
    lstep = ev_log_step[0].reshape(G, 1)
    a_re, a_im, g_re, g_im = _s5_disc_fwd(lam_r, lam_i, lstep, name="s5_disc", after=started)
    b_re2, b_im2 = ev_ssm_b_re[0].reshape(G * Pn, Cg), ev_ssm_b_im[0].reshape(G * Pn, Cg)
    g_re1, g_im1 = g_re.reshape(G * Pn, 1), g_im.reshape(G * Pn, 1)
    bb_re, bb_im = _s5_bb_fwd(g_re1, g_im1, b_re2, b_im2, name="s5_bb")
    bbt = jnp.stack([jnp.transpose(b.reshape(G, Pn, Cg), (0, 2, 1)).reshape(G * Cg, Pn) for b in (bb_re, bb_im)])
    BB = _diag_expand(bbt, Cg, Pn, name="s5_bb_dense")
    cct = jnp.stack([jnp.transpose(ev_ssm_c_re[0], (0, 2, 1)).reshape(G * Pn, Cg),
                     jnp.transpose(-ev_ssm_c_im[0], (0, 2, 1)).reshape(G * Pn, Cg)])
    CC = _diag_expand(cct, Pn, Cg, name="s5_cc_dense", after=started)
    a_cat = jnp.stack([a_re.reshape(1, G * Pn), a_im.reshape(1, G * Pn)])
    dskip = ev_ssm_d[0].reshape(1, SSM_WIDTH)
    tabs = _rope_tables(positions.reshape(S, 1).astype(F32), name="rope_tables", after=[BB, CC])

    gw = dict(zip(grp_now[:1], finish_gather(ag_in, src_now[:1], cols_now[:1], [tabs[2]], "in")))
    w_in_t = gw[('ev_w_in', 0)].reshape(EIN, D)
    qkv_w = 3 * FOX_WIDTH
    WmainT = jnp.concatenate([w_in_t[:qkv_w], w_in_t[qkv_w + FOX_HEADS:]], axis=0)
    WfT = jnp.pad(w_in_t[qkv_w:qkv_w + FOX_HEADS], ((0, LANE - FOX_HEADS), (0, 0)))
    cbs = [ffn_conv_b[l].reshape(N_CHIPS, Fs) for l in range(DEPTH)]

    P = _mm(x0, WmainT, 'nt', name="ev_proj")
    fl = _mm(x0, WfT, 'nt', name="ev_proj_f")
    bf_pad = jnp.pad(ev_b_f.reshape(1, FOX_HEADS), ((0, 0), (0, LANE - FOX_HEADS)))
    cgate, sgate = _gate_fwd(fl, bf_pad, name="fox_gate")
    ccol = jnp.transpose(cgate[:, :FOX_HEADS]).reshape(FOX_HEADS, S, 1)
    crow = jnp.transpose(cgate[:, :FOX_HEADS]).reshape(FOX_HEADS, 1, S)
    fox, lse = _fox_fwd(P, ccol, crow, name="fox_fwd")
    u_s5 = P[:, qkv_w:]
    UT, HT = _DIAG_TILE * Cg, _DIAG_TILE * Pn
    bu = _mm(u_s5, BB, 'nn', bmode='bo', tm=2048, tn=HT, tk=UT, diag='kn', name="s5_bu")
    hh = _s5_scan_fwd(bu, a_cat, name="s5_scan")
    yc = _mm(hh, CC, 'nn', bmode='abr', tm=2048, tn=UT, tk=HT, diag='kn', name="s5_y")
    y_s5, yg = _s5_out_fwd(yc, P, dskip, name="s5_out")
    gw.update(zip(grp_now[1:], finish_gather(ag_mix, src_now[1:], cols_now[1:], [yg], "mix")))
    Wglu = _cols_from_shards(gw[('ev_w_glu', 0)])
    Wout_ev = gw[('ev_w_out', 0)].reshape(D, D)
    z = _mm(yg, Wglu, 'nn', name="s5_glu_proj")
    ssm = _glu_fwd(z, name="s5_glu")
    cat = jnp.concatenate([fox.astype(BF16), ssm], axis=1)
    mix0 = _mm(cat, Wout_ev, 'nn', name="ev_out")
    x1, xh1, rs1 = _add_ln_fwd(x0, mix0, ln_mix_g[0], ln_mix_b[0], name="ln_mix0")
    got = finish_gather(ag_ffn0, src_ffn0, cols_ffn0, [x1], "ffn0")
    gw.update(zip(grp_ffn0, got[:-1]))
    cw_all = got[-1].reshape(N_CHIPS, -1)[:, :cwl.shape[0]].reshape(N_CHIPS, DEPTH, 3, Fs)
    cws = [cw_all[:, l] for l in range(DEPTH)]
    Wup = {0: gw[('ffn_w_up', 0)]}
    Wdn = {0: gw[('ffn_w_down', 0)].reshape(2, Fs, D)}
    f0, hf0, af0 = _ffn_fwd(x1, Wup[0], Wdn[0], cws[0], cbs[0], "l0")
    x2, xh2, rs2 = _add_ln_fwd(x1, f0, ln_ffn_g[0], ln_ffn_b[0], name="ln_ffn0")

    gw.update(zip(grp_l1, finish_gather(ag_l1, src_l1, cols_l1, [x2], "l1")))
    Wodin = _cols_from_shards(gw[('od_w_in', 0)])
    Wodout = gw[('od_w_out', 0)].reshape(D, D)
    Wup[1] = gw[('ffn_w_up', 1)]
    Wdn[1] = gw[('ffn_w_down', 1)].reshape(2, Fs, D)
    QW, KW = SWA_HEADS * SWA_HEAD_DIM, SWA_KV_HEADS * SWA_HEAD_DIM
    P1 = _mm(x2, Wodin, 'nn', name="od_proj")
    qT = _to_heads(P1, tabs, col0=0, width=QW, rotate=True, name="rope_q", out_dtype=BF16)
    kT = _to_heads(P1, tabs, col0=QW, width=KW, rotate=True, name="rope_k", out_dtype=BF16)
    vT = _to_heads(P1, tabs, col0=QW + KW, width=KW, rotate=False, name="heads_v", out_dtype=BF16)
    sink_rows = jnp.broadcast_to(od_sinks[0].reshape(SWA_KV_HEADS, SWA_GROUPS, 1, 1),
                                 (SWA_KV_HEADS, SWA_GROUPS, SWA_WINDOW, 1)).reshape(SWA_KV_HEADS, -1, 1)
    oT, Lsw = _swa_fwd(qT, kT, vT, sink_rows, name="swa_fwd")
    o_sw = _from_heads(oT, tabs, rotate_back=False, name="heads_o", out_dtype=BF16)
    mix1 = _mm(o_sw, Wodout, 'nn', name="od_out")
    x3, xh3, rs3 = _add_ln_fwd(x2, mix1, ln_mix_g[1], ln_mix_b[1], name="ln_mix1")
    f1, hf1, af1 = _ffn_fwd(x3, Wup[1], Wdn[1], cws[1], cbs[1], "l1")
    _, xh4, rs4 = _add_ln_fwd(x3, f1, ln_ffn_g[1], ln_ffn_b[1], name="ln_ffn1")

    dz4, dg_ffn1, db_ffn1, loss_part = _loss_ln_bwd(tgt, xh4, rs4, ln_ffn_g[1], ln_ffn_b[1], name="loss_lnb_ffn1")
    dx3f, dWup1, dWdn1, dcw1, dcb1 = _ffn_bwd(dz4, x3, hf1, af1, Wup[1], Wdn[1], cws[1], "l1")
    sib_ffn1 = _sibling_halves_start([dWup1, dWdn1.reshape(N_CHIPS, Rd, D)], [False, False], name="rs_ffn1_sib_start")
    dz3, dg_mix1, db_mix1 = _ln_bwd(dz4, dx3f, xh3, rs3, ln_mix_g[1], name="lnb_mix1", after=[sib_ffn1[4]])
    do_sw = _mm(dz3, Wodout, 'nt', name="od_out_dx")
    dWodout = _mm(o_sw, dz3, 'tn', name="od_out_dw", out_dtype=BF16)
    doT = _to_heads(do_sw, tabs, col0=0, width=QW, rotate=False, name="heads_do", out_dtype=F32)
    dqT, dkT, dvT, dsink = _swa_bwd(qT, kT, vT, sink_rows, oT, Lsw, doT, name="swa_bwd")
    dq1 = _from_heads(dqT, tabs, rotate_back=True, name="rope_dq", out_dtype=BF16)
    dk1 = _from_heads(dkT, tabs, rotate_back=True, name="rope_dk", out_dtype=BF16, skip_rows=SWA_WINDOW)
    dv1 = _from_heads(dvT, tabs, rotate_back=False, name="heads_dv", out_dtype=BF16, skip_rows=SWA_WINDOW)
    dP1 = jnp.concatenate([dq1, dk1, dv1], axis=1)
    dx2m = _mm(dP1, Wodin, 'nt', name="od_proj_dx")
    dWodin = _mm(x2, dP1, 'tn', name="od_proj_dw", out_dtype=BF16)

    def rs_begin(entries, grads, tag):
        cols = [split_cols[e] for e in entries]
        sib = _sibling_send_halves(grads, cols, name=f"rs_{tag}_sibling")
        return [_sum2_halves(g4, s4, bc, name=f"rs_sum2_{n}{l}")
                for (n, l), g4, s4, bc in zip(entries, grads, sib, cols)]

    def rs_begin_started(entries, started, after, tag):
        send, rcv, thru, lands, _ = started
        thru, lands = _sibling_halves_wait(send, rcv, thru, lands, [False] * len(thru), after,
                                           name=f"rs_{tag}_sib_wait")
        return [_sum2_halves(g4, s4, False, name=f"rs_sum2_{n}{l}") for (n, l), g4, s4 in zip(entries, thru, lands)]

    def own_parts(parts):
        me = 2 * lax.axis_index("x") + lax.axis_index("y")
        return [lax.dynamic_slice_in_dim(p, me, 1, axis=0) for p in parts]

    part_l1 = (rs_begin(grp_l1[:2], [_shards_from_cols(dWodin), dWodout.reshape(N_CHIPS, D // N_CHIPS, D)], "od")
               + rs_begin_started(grp_l1[2:], sib_ffn1, [dWodin], "ffn1"))
    rs_l1 = _chip_exchange_start('scatter', part_l1, [False] * len(part_l1), name="rs_l1_start")

    dz2, dg_ffn0, db_ffn0 = _ln_bwd(dz3, dx2m, xh2, rs2, ln_ffn_g[0], name="lnb_ffn0", after=[rs_l1[4]])
    dx1f, dWup0, dWdn0, dcw0, dcb0 = _ffn_bwd(dz2, x1, hf0, af0, Wup[0], Wdn[0], cws[0], "l0")
    sib_ffn0 = _sibling_halves_start([dWup0, dWdn0.reshape(N_CHIPS, Rd, D)], [False, False], name="rs_ffn0_sib_start")
    dz1, dg_mix0, db_mix0 = _ln_bwd(dz2, dx1f, xh1, rs1, ln_mix_g[0], name="lnb_mix0", after=[sib_ffn0[4]])
    dcat = _mm(dz1, Wout_ev, 'nt', name="ev_out_dx")
    dWout_ev = _mm(cat, dz1, 'tn', name="ev_out_dw", out_dtype=BF16)
    part_ffn0 = rs_begin_started(grp_ffn0, sib_ffn0, [dWout_ev], "ffn0")
    rs_ffn0 = _chip_exchange_start('scatter', part_ffn0, [False] * len(part_ffn0), name="rs_ffn0_start")
    dz = _glu_bwd(z, dcat, name="s5_glu_bwd")
    dyg = _mm(dz, Wglu, 'nt', name="s5_glu_dx", after=[rs_ffn0[4]])
    dWglu = _mm(yg, dz, 'tn', name="s5_glu_dw", out_dtype=BF16)
    dy_s5, du_dir, dD = _s5_out_bwd(dyg, y_s5, P, dskip, name="s5_out_bwd")
    dhh = _mm(dy_s5, CC, 'nt', bmode='bo', tm=2048, tn=HT, tk=UT, diag='kn', name="s5_y_dx")
    dCC = _mm(hh, dy_s5, 'tn', bmode='ao', tm=HT, tn=UT, diag='mn', name="s5_y_dw")
    lam, da_s5 = _s5_scan_bwd(dhh, hh, a_cat, name="s5_scan_bwd")
    du = _mm(lam, BB, 'nt', bmode='abr', tm=2048, tn=UT, tk=HT, diag='kn', name="s5_bu_dx", plus=[(du_dir, 1.0)],
             out_dtype=BF16)
    dBB = _mm(u_s5, lam, 'tn', bmode='bo', tm=UT, tn=HT, diag='mn', name="s5_bu_dw")
    dq0, dk0, dv0, dccol, dcrow = _fox_bwd(P, ccol, crow, fox, lse, dcat, name="fox_bwd")
    dc = jnp.transpose((dccol.reshape(FOX_HEADS, S) - dcrow.reshape(FOX_HEADS, S)))
    dc = jnp.pad(dc, ((0, 0), (0, LANE - FOX_HEADS)))
    dfl, dbf = _gate_bwd(dc, sgate, name="fox_gate_bwd")
    dP = jnp.concatenate([dq0, dk0, dv0, du], axis=1)
    dx0b = _mm(dfl, WfT, 'nn', name="ev_proj_f_dx")
    grad_x = _mm(dP, WmainT, 'nn', name="ev_proj_dx", plus=[(dz1, ALPHA), (dx0b, 1.0)])
    dWmainT = _mm(dP, x0, 'tn', tm=1024, tn=1024, name="ev_proj_dw", out_dtype=BF16)
    dWfT = _mm(dfl, x0, 'tn', name="ev_proj_f_dw", out_dtype=BF16)

    dbbt = _diag_extract(dBB, Cg, Pn, name="s5_bb_diag")
    dcct = _diag_extract(dCC, Pn, Cg, name="s5_cc_diag")
    dbb_re = jnp.transpose(dbbt[0].reshape(G, Cg, Pn), (0, 2, 1)).reshape(G * Pn, Cg)
    dbb_im = jnp.transpose(dbbt[1].reshape(G, Cg, Pn), (0, 2, 1)).reshape(G * Pn, Cg)
    db_re, db_im, dg_re1, dg_im1 = _s5_bb_bwd(g_re1, g_im1, b_re2, b_im2, dbb_re, dbb_im, name="s5_bb_bwd")
    dlam_re, dlam_im, dlstep = _s5_disc_bwd(lam_r, lam_i, lstep, da_s5[0].reshape(G, Pn), da_s5[1].reshape(G, Pn),
                                            dg_re1.reshape(G, Pn), dg_im1.reshape(G, Pn), name="s5_disc_bwd")
    dc_re = jnp.transpose(dcct[0].reshape(G, Pn, Cg), (0, 2, 1))
    dc_im = -jnp.transpose(dcct[1].reshape(G, Pn, Cg), (0, 2, 1))

    def conv_w_full(d0, d1):
        return jnp.stack([jnp.reshape(jnp.transpose(d[:, :, :Fs], (1, 0, 2)), (3, N_CHIPS * Fs)) for d in (d0, d1)])

    def conv_b_full(d0, d1):
        return jnp.stack([jnp.reshape(d[:, 0, :Fs], (N_CHIPS * Fs,)) for d in (d0, d1)])

    small_local = dict(
        ev_b_f=dbf[:, :FOX_HEADS], ev_lambda_re=dlam_re, ev_lambda_im=dlam_im, ev_log_step=dlstep,
        ev_ssm_b_re=db_re, ev_ssm_b_im=db_im, ev_ssm_c_re=dc_re, ev_ssm_c_im=dc_im, ev_ssm_d=dD,
        od_sinks=dsink[:, :, 0],
        ln_mix_g=jnp.concatenate([dg_mix0, dg_mix1]), ln_mix_b=jnp.concatenate([db_mix0, db_mix1]),
        ffn_conv_w=conv_w_full(dcw0, dcw1), ffn_conv_b=conv_b_full(dcb0, dcb1),
        ln_ffn_g=jnp.concatenate([dg_ffn0, dg_ffn1]), ln_ffn_b=jnp.concatenate([db_ffn0, db_ffn1]))
    small = list(small_local.keys())
    out_g, out_d, out_m, out_v = {}, {}, {}, {}
    loss_out = []

    def small_update(after):
        red = _all_reduce_small(_pack([small_local[n] for n in small] + [loss_part]), name="ar_small", after=after)
        full_shapes = [W[n].shape if n != 'ffn_conv_w' else (DEPTH, 3, N_CHIPS * Fs) for n in small]
        pieces = _unpack(red, full_shapes + [()])
        loss_out.append(pieces[-1])
        gsmall = dict(zip(small, pieces[:-1]))
        chip = 2 * lax.axis_index("x") + lax.axis_index("y")
        gsmall['ffn_conv_w'] = lax.dynamic_slice_in_dim(gsmall['ffn_conv_w'], chip * Fs, Fs, axis=2)
        shapes = [W[n].shape for n in small]
        gs, ds_, ms, vs = _adamw(_pack([W[n] for n in small])[None], _pack([gsmall[n] for n in small])[None],
                                 _pack([Mo[n] for n in small])[None], _pack([Vo[n] for n in small])[None],
                                 name="adamw_small", tr=1 << 14)
        out_g.update(zip(small, _unpack(gs, shapes)))
        out_d.update(zip(small, _unpack(ds_, shapes)))
        out_m.update(zip(small, _unpack(ms, shapes)))
        out_v.update(zip(small, _unpack(vs, shapes)))
        return vs

    dw_in_t = jnp.concatenate([dWmainT[:qkv_w], dWfT[:FOX_HEADS], dWmainT[qkv_w:]], axis=0)
    part_now = rs_begin(grp_now, [dw_in_t.reshape(N_CHIPS, EIN // N_CHIPS, D), _shards_from_cols(dWglu),
                                  dWout_ev.reshape(N_CHIPS, D // N_CHIPS, D)], "l0")
    small_done = small_update([grad_x])
    rs_now = _chip_exchange_start('scatter', part_now, [False] * len(part_now), name="rs_l0_start",
                                  after=[small_done])

    def finish_scatter(started, parts, after, tag):
        send, rcv, thru, lands, _ = started
        thru, lands = _chip_exchange_wait('scatter', send, rcv, thru, lands, [False] * len(parts), after,
                                          name=f"rs_{tag}_wait")
        return _own_slot(lands, own_parts(thru))

    def update(entries, recv, tag):
        halves = [_rowsum(r, name=f"rs_sum4_{e[0]}{e[1]}") for e, r in zip(entries, recv)]
        send, rcv, thru, lands, tok = _sibling_swap_start(halves, name=f"rs_{tag}_join_start")
        own = dict(zip(entries, thru))
        params = list(dict.fromkeys(e[0] for e in entries))

        def half_update(n, grads, is_own, prev, after_name):
            return _adamw_half(Wv[n], [grads[(n, l)] for l in range(W[n].shape[0])], view(n, Mo[n]), view(n, Vo[n]),
                               name=f"adamw_{after_name}_{n}", own=is_own, prev=prev, by_cols=split_cols[(n, 0)])

        first = {n: half_update(n, own, True, None, "own") for n in params}
        _, others = _sibling_swap_wait(send, rcv, thru, lands, [first[n][3] for n in params] + [tok],
                                       name=f"rs_{tag}_join_wait")
        oth = dict(zip(entries, others))
        done = []
        for n in params:
            res = half_update(n, oth, False, first[n], "sib")
            out_g[n], out_d[n], out_m[n], out_v[n] = (view(n, t) for t in res)
            done.append(res[3])
        return done

    recv_rest = (finish_scatter(rs_l1, part_l1, [rs_now[4]], "l1")
                 + finish_scatter(rs_ffn0, part_ffn0, [rs_now[4]], "ffn0"))
    done = update(grp_l1 + grp_ffn0, recv_rest, "rest")
    update(grp_now, finish_scatter(rs_now, part_now, done, "l0"), "l0")
    loss = loss_out[0]

    return (loss, grad_x.reshape(1, S, D), *[out_g[n] for n in names], *[out_d[n] for n in names],
            *[out_m[n] for n in names], *[out_v[n] for n in names])
```

```python
import math

import numpy as np
import jax
import jax.numpy as jnp
from jax import lax
from jax.experimental import pallas as pl
from jax.experimental.pallas import tpu as pltpu

F32 = jnp.float32
BF16 = jnp.bfloat16
MESH = pl.DeviceIdType.MESH
ANY = pl.BlockSpec(memory_space=pl.ANY)

D_MODEL = 2048
FOX_HEADS = 8
FOX_HEAD_DIM = 128
FOX_WIDTH = 1024
SSM_WIDTH = 1024
SSM_GROUP = 16
SSM_GROUPS = 64
SSM_STATE = 64
SWA_HEADS = 32
SWA_KV_HEADS = 4
SWA_HEAD_DIM = 64
SWA_GROUPS = 8
SWA_WINDOW = 128
ROPE_DIM = 16
ROPE_THETA = 500000.0
LN_EPS = 1e-5
DEPTH = 2
ALPHA = (2.0 * DEPTH) ** 0.25
ADAM_LR = 0.001
ADAM_B1 = 0.9
ADAM_B2 = 0.999
ADAM_EPS = 1e-08
ADAM_WD = 0.01
ADAM_STEP = 10
N_CHIPS = 4

VMEM_LIMIT = 56 * 1024 * 1024
LANE = 128


def _call(body, after=(), **kw):
    if after:
        n = len(after)

        def shifted(*refs):
            return body(*refs[n:])

        call = _call(shifted, **dict(kw, in_specs=[ANY] * n + list(kw["in_specs"])))
        return lambda *args: call(*after, *args)
    return pl.pallas_call(body, **kw)


def _cparams(sem):
    return pltpu.CompilerParams(dimension_semantics=sem, vmem_limit_bytes=VMEM_LIMIT)


def _rup(n, m):
    return (n + m - 1) // m * m


def _pick(n, pref):
    if n <= pref:
        return n
    for step in (128, 16, 8):
        for t in range(pref - pref % step, 0, -step):
            if n % t == 0:
                return t
    return n


def _tile2d(rows, cols, pref_rows=256, budget=256 * 1024):
    tr = _pick(rows, pref_rows)
    if tr < 64:
        tr = rows
    if cols % LANE:
        return tr, cols
    return tr, _pick(cols, max(LANE, budget // tr // LANE * LANE))


def _mm(a, b, mode, *, name, tm=512, tn=1024, tk=2048, bmode=None, out_dtype=F32, after=(), b_map=None,
        o_map=None, diag=None, plus=()):
    a3 = a if a.ndim == 3 else a[None]
    b3 = b if b.ndim == 3 else b[None]
    if mode == 'tn':
        K, M = a3.shape[1:]
    else:
        M, K = a3.shape[1:]
    N = b3.shape[1] if mode == 'nt' else b3.shape[2]
    tm, tn, tk = _pick(M, tm), _pick(N, tn), _pick(K, tk)
    nb = max(a3.shape[0], b3.shape[0])
    nbo, nbr = (1, nb) if bmode == 'abr' else (nb, 1)
    nm, nk = M // tm, K // tk
    if diag == 'kn':
        assert K // tk == N // tn
        nk = 1
    if diag == 'mn':
        assert M // tm == N // tn
        nm = 1
    nred = nbr * nk
    a_b = bmode in ('ao', 'abr')
    b_b = bmode in ('bo', 'abr')
    o_b = bmode in ('bo', 'ao')

    def bsel(flag, bo, br, remap=None):
        if not flag:
            return 0
        return (bo + br) if remap is None else remap(bo + br)

    def mi(i, j):
        return j if diag == 'mn' else i

    def ki(j, k):
        return j if diag == 'kn' else k

    if mode == 'tn':
        a_spec = pl.BlockSpec((None, tk, tm), lambda bo, i, j, br, k: (bsel(a_b, bo, br), ki(j, k), mi(i, j)))
    else:
        a_spec = pl.BlockSpec((None, tm, tk), lambda bo, i, j, br, k: (bsel(a_b, bo, br), mi(i, j), ki(j, k)))
    if mode == 'nt':
        b_spec = pl.BlockSpec((None, tn, tk), lambda bo, i, j, br, k: (bsel(b_b, bo, br, b_map), j, ki(j, k)))
    else:
        b_spec = pl.BlockSpec((None, tk, tn), lambda bo, i, j, br, k: (bsel(b_b, bo, br, b_map), ki(j, k), j))
    o_spec = pl.BlockSpec((None, tm, tn), lambda bo, i, j, br, k: (bsel(o_b, bo, br, o_map), mi(i, j), j))
    dn = {'nn': (((1,), (0,)), ((), ())), 'nt': (((1,), (1,)), ((), ())), 'tn': (((0,), (0,)), ((), ()))}[mode]

    na = len(plus)

    def body(a_ref, b_ref, *rest):
        plus_refs = rest[:na]
        o_ref, scratch = rest[na + len(after)], rest[na + len(after) + 1:]
        r = lax.dot_general(a_ref[...].astype(BF16), b_ref[...].astype(BF16), dn, preferred_element_type=F32)

        def finish(total):
            for (_, scale), p_ref in zip(plus, plus_refs):
                total = total + scale * p_ref[...].astype(F32)
            o_ref[...] = total.astype(out_dtype)

        if nred == 1:
            finish(r)
        else:
            acc = scratch[0]
            step = pl.program_id(3) * nk + pl.program_id(4)

            @pl.when(step == 0)
            def _():
                acc[...] = r

            @pl.when(step > 0)
            def _():
                acc[...] += r

            @pl.when(step == nred - 1)
            def _():
                finish(acc[...])

    out = _call(
        body, name=name,
        grid=(nbo, nm, N // tn, nbr, nk),
        in_specs=[a_spec, b_spec] + [o_spec] * na + [ANY] * len(after), out_specs=o_spec,
        out_shape=jax.ShapeDtypeStruct((nbo if o_b else 1, M, N), out_dtype),
        scratch_shapes=[] if nred == 1 else [pltpu.VMEM((tm, tn), F32)],
        compiler_params=_cparams(("parallel", "parallel", "parallel", "arbitrary", "arbitrary")),
    )(a3, b3, *[p if p.ndim == 3 else p[None] for p, _ in plus], *after)
    return out if o_b else out[0]


def _add_ln_fwd(x, r, g, b, *, name):
    S, D = x.shape
    tr = _pick(S, 256)

    def body(x_ref, r_ref, g_ref, b_ref, o_ref, xh_ref, rs_ref):
        z = ALPHA * x_ref[...] + r_ref[...]
        mu = jnp.mean(z, axis=-1, keepdims=True)
        zc = z - mu
        var = jnp.mean(zc * zc, axis=-1, keepdims=True)
        rstd = lax.rsqrt(var + LN_EPS)
        xh = zc * rstd
        xh_ref[...] = xh
        rs_ref[...] = rstd
        o_ref[...] = xh * g_ref[...] + b_ref[...]

    row = pl.BlockSpec((tr, D), lambda i: (i, 0))
    vec = pl.BlockSpec((1, D), lambda i: (0, 0))
    return _call(
        body, name=name, grid=(S // tr,),
        in_specs=[row, row, vec, vec],
        out_specs=[row, row, pl.BlockSpec((tr, 1), lambda i: (i, 0))],
        out_shape=[jax.ShapeDtypeStruct((S, D), F32), jax.ShapeDtypeStruct((S, D), F32),
                   jax.ShapeDtypeStruct((S, 1), F32)],
        compiler_params=_cparams(("parallel",)),
    )(x, r, g.reshape(1, D), b.reshape(1, D))


def _ln_bwd(da, db, xhat, rstd, g, *, name, after=()):
    S, D = xhat.shape
    tr = _pick(S, 256)

    def body(*refs):
        da_ref, db_ref, xh_ref, rs_ref, g_ref, dz_ref, dg_ref, dbt_ref = refs[len(after):]
        dy = ALPHA * da_ref[...] + db_ref[...]
        xh = xh_ref[...]
        dxh = dy * g_ref[...]
        m1 = jnp.mean(dxh, axis=-1, keepdims=True)
        m2 = jnp.mean(dxh * xh, axis=-1, keepdims=True)
        dz_ref[...] = rs_ref[...] * (dxh - m1 - xh * m2)
        pg = jnp.sum(dy * xh, axis=0, keepdims=True)
        pb = jnp.sum(dy, axis=0, keepdims=True)

        @pl.when(pl.program_id(0) == 0)
        def _():
            dg_ref[...] = pg
            dbt_ref[...] = pb

        @pl.when(pl.program_id(0) > 0)
        def _():
            dg_ref[...] += pg
            dbt_ref[...] += pb

    row = pl.BlockSpec((tr, D), lambda i: (i, 0))
    vec = pl.BlockSpec((1, D), lambda i: (0, 0))
    ins = list(after) + [da, db, xhat, rstd, g.reshape(1, D)]
    in_specs = [ANY] * len(after) + [row, row, row, pl.BlockSpec((tr, 1), lambda i: (i, 0)), vec]
    return _call(
        body, name=name, grid=(S // tr,),
        in_specs=in_specs, out_specs=[row, vec, vec],
        out_shape=[jax.ShapeDtypeStruct((S, D), F32), jax.ShapeDtypeStruct((1, D), F32),
                   jax.ShapeDtypeStruct((1, D), F32)],
        compiler_params=_cparams(("arbitrary",)),
    )(*ins)


def _loss_ln_bwd(t, xhat, rstd, g, b, *, name):
    S, D = xhat.shape
    tr = _pick(S, 256)

    def body(t_ref, xh_ref, rs_ref, g_ref, b_ref, dz_ref, dg_ref, dbt_ref, l_ref):
        xh = xh_ref[...]
        e = xh * g_ref[...] + b_ref[...] - t_ref[...]
        dy = e * (1.0 / D)
        part = 0.5 * jnp.sum(jnp.sum(e * e, axis=-1, keepdims=True) * (1.0 / D), axis=0, keepdims=True)
        dxh = dy * g_ref[...]
        m1 = jnp.mean(dxh, axis=-1, keepdims=True)
        m2 = jnp.mean(dxh * xh, axis=-1, keepdims=True)
        dz_ref[...] = rs_ref[...] * (dxh - m1 - xh * m2)
        pg = jnp.sum(dy * xh, axis=0, keepdims=True)
        pb = jnp.sum(dy, axis=0, keepdims=True)

        @pl.when(pl.program_id(0) == 0)
        def _():
            dg_ref[...] = pg
            dbt_ref[...] = pb
            l_ref[...] = part

        @pl.when(pl.program_id(0) > 0)
        def _():
            dg_ref[...] += pg
            dbt_ref[...] += pb
            l_ref[...] += part

    row = pl.BlockSpec((tr, D), lambda i: (i, 0))
    vec = pl.BlockSpec((1, D), lambda i: (0, 0))
    return _call(
        body, name=name, grid=(S // tr,),
        in_specs=[row, row, pl.BlockSpec((tr, 1), lambda i: (i, 0)), vec, vec],
        out_specs=[row, vec, vec, pl.BlockSpec((1, 1), lambda i: (0, 0))],
        out_shape=[jax.ShapeDtypeStruct((S, D), F32), jax.ShapeDtypeStruct((1, D), F32),
                   jax.ShapeDtypeStruct((1, D), F32), jax.ShapeDtypeStruct((1, 1), F32)],
        compiler_params=_cparams(("arbitrary",)),
    )(t, xhat, rstd, g.reshape(1, D), b.reshape(1, D))


def _split3(x):
    h = x.astype(BF16)
    r = x - h.astype(F32)
    m = r.astype(BF16)
    l = (r - m.astype(F32)).astype(BF16)
    return h, m, l


def _tri_matmul(tri_bf, x):
    h, m, l = _split3(x)
    dn = (((1,), (0,)), ((), ()))
    return (lax.dot_general(tri_bf, l, dn, preferred_element_type=F32)
            + lax.dot_general(tri_bf, m, dn, preferred_element_type=F32)
            + lax.dot_general(tri_bf, h, dn, preferred_element_type=F32))


def _gate_fwd(fl, bf, *, name):
    S = fl.shape[0]
    tc = _pick(S, 256)
    nchunk = S // tc

    def body(fl_ref, bf_ref, c_ref, sg_ref):
        r = lax.broadcasted_iota(jnp.int32, (tc, tc), 0)
        cidx = lax.broadcasted_iota(jnp.int32, (tc, tc), 1)
        tri = (r >= cidx).astype(BF16)
        carry = jnp.zeros((1, LANE), F32)
        for ch in range(nchunk):
            x = fl_ref[pl.ds(ch * tc, tc), :] + bf_ref[...]
            lf = jnp.minimum(x, 0.0) - jnp.log(1.0 + jnp.exp(-jnp.abs(x)))
            sg_ref[pl.ds(ch * tc, tc), :] = jax.nn.sigmoid(-x)
            c_ref[pl.ds(ch * tc, tc), :] = _tri_matmul(tri, lf) + carry
            carry = carry + jnp.sum(lf, axis=0, keepdims=True)

    full = pl.BlockSpec((S, LANE), lambda: (0, 0))
    return _call(
        body, name=name, in_specs=[full, pl.BlockSpec((1, LANE), lambda: (0, 0))], out_specs=[full, full],
        out_shape=[jax.ShapeDtypeStruct((S, LANE), F32)] * 2,
        compiler_params=pltpu.CompilerParams(vmem_limit_bytes=VMEM_LIMIT),
    )(fl, bf)


def _gate_bwd(dc, sg, *, name):
    S = dc.shape[0]
    tc = _pick(S, 256)
    nchunk = S // tc

    def body(dc_ref, sg_ref, dfl_ref, db_ref):
        r = lax.broadcasted_iota(jnp.int32, (tc, tc), 0)
        cidx = lax.broadcasted_iota(jnp.int32, (tc, tc), 1)
        tri = (r <= cidx).astype(BF16)
        carry = jnp.zeros((1, LANE), F32)
        dbacc = jnp.zeros((1, LANE), F32)
        for ch in reversed(range(nchunk)):
            d = dc_ref[pl.ds(ch * tc, tc), :]
            dfl = (_tri_matmul(tri, d) + carry) * sg_ref[pl.ds(ch * tc, tc), :]
            dfl_ref[pl.ds(ch * tc, tc), :] = dfl
            dbacc = dbacc + jnp.sum(dfl, axis=0, keepdims=True)
            carry = carry + jnp.sum(d, axis=0, keepdims=True)
        db_ref[...] = dbacc

    full = pl.BlockSpec((S, LANE), lambda: (0, 0))
    return _call(
        body, name=name, in_specs=[full, full], out_specs=[full, pl.BlockSpec((1, LANE), lambda: (0, 0))],
        out_shape=[jax.ShapeDtypeStruct((S, LANE), F32), jax.ShapeDtypeStruct((1, LANE), F32)],
        compiler_params=pltpu.CompilerParams(vmem_limit_bytes=VMEM_LIMIT),
    )(dc, sg)


def _fox_scores(q_ref, k_ref, cc_ref, cr_ref, qi, tq, S):
    scale = 1.0 / math.sqrt(FOX_HEAD_DIM)
    s = lax.dot_general(q_ref[...].astype(BF16), k_ref[...].astype(BF16), (((1,), (1,)), ((), ())),
                        preferred_element_type=F32) * scale
    s = s + cc_ref[...] - cr_ref[...]
    row = lax.broadcasted_iota(jnp.int32, (tq, S), 0) + qi * tq
    col = lax.broadcasted_iota(jnp.int32, (tq, S), 1)
    return s, row >= col


def _fox_fwd(P, ccol, crow, *, name):
    S = P.shape[0]
    tq = _pick(S, 256)
    H = FOX_HEADS

    def body(q_ref, k_ref, v_ref, cc_ref, cr_ref, o_ref, l_ref):
        s, causal = _fox_scores(q_ref, k_ref, cc_ref, cr_ref, pl.program_id(1), tq, S)
        s = jnp.where(causal, s, -1e30)
        m = jnp.max(s, axis=-1, keepdims=True)
        e = jnp.exp(s - m)
        den = jnp.sum(e, axis=-1, keepdims=True)
        p = e / den
        o_ref[...] = jnp.dot(p.astype(BF16), v_ref[...].astype(BF16), preferred_element_type=F32)
        l_ref[...] = m + jnp.log(den)

    return _call(
        body, name=name, grid=(H, S // tq),
        in_specs=[pl.BlockSpec((tq, 128), lambda h, i: (i, h)),
                  pl.BlockSpec((S, 128), lambda h, i: (0, H + h)),
                  pl.BlockSpec((S, 128), lambda h, i: (0, 2 * H + h)),
                  pl.BlockSpec((None, tq, 1), lambda h, i: (h, i, 0)),
                  pl.BlockSpec((None, 1, S), lambda h, i: (h, 0, 0))],
        out_specs=[pl.BlockSpec((tq, 128), lambda h, i: (i, h)),
                   pl.BlockSpec((None, tq, 1), lambda h, i: (h, i, 0))],
        out_shape=[jax.ShapeDtypeStruct((S, FOX_WIDTH), F32), jax.ShapeDtypeStruct((H, S, 1), F32)],
        compiler_params=_cparams(("parallel", "parallel")),
    )(P, P, P, ccol, crow)


def _fox_bwd(P, ccol, crow, o, lse, dcat, *, name):
    S = P.shape[0]
    tq = _pick(S, 256)
    H = FOX_HEADS
    nq = S // tq
    scale = 1.0 / math.sqrt(FOX_HEAD_DIM)

    def body(q_ref, k_ref, v_ref, cc_ref, cr_ref, o_ref, l_ref, do_ref,
             dq_ref, dk_ref, dv_ref, dcc_ref, dcr_ref, dk_acc, dv_acc):
        qi = pl.program_id(1)
        s, causal = _fox_scores(q_ref, k_ref, cc_ref, cr_ref, qi, tq, S)
        p = jnp.where(causal, jnp.exp(s - l_ref[...]), 0.0)
        do = do_ref[...]
        do_bf = do.astype(BF16)
        dp = lax.dot_general(do_bf, v_ref[...].astype(BF16), (((1,), (1,)), ((), ())), preferred_element_type=F32)
        delta = jnp.sum(do * o_ref[...], axis=-1, keepdims=True)
        ds = p * (dp - delta)
        ds_bf = ds.astype(BF16)
        dq_ref[...] = (jnp.dot(ds_bf, k_ref[...].astype(BF16), preferred_element_type=F32) * scale).astype(BF16)
        dkp = lax.dot_general(ds_bf, q_ref[...].astype(BF16), (((0,), (0,)), ((), ())),
                              preferred_element_type=F32) * scale
        dvp = lax.dot_general(p.astype(BF16), do_bf, (((0,), (0,)), ((), ())), preferred_element_type=F32)
        dcc_ref[...] = jnp.sum(ds, axis=-1, keepdims=True)
        dcr = jnp.sum(ds, axis=0, keepdims=True)

        @pl.when(qi == 0)
        def _():
            dk_acc[...] = dkp
            dv_acc[...] = dvp
            dcr_ref[...] = dcr

        @pl.when(qi > 0)
        def _():
            dk_acc[...] += dkp
            dv_acc[...] += dvp
            dcr_ref[...] += dcr

        @pl.when(qi == nq - 1)
        def _():
            dk_ref[...] = dk_acc[...].astype(BF16)
            dv_ref[...] = dv_acc[...].astype(BF16)

    qblk = pl.BlockSpec((tq, 128), lambda h, i: (i, h))
    kvo = pl.BlockSpec((S, 128), lambda h, i: (0, h))
    col = pl.BlockSpec((None, tq, 1), lambda h, i: (h, i, 0))
    rowv = pl.BlockSpec((None, 1, S), lambda h, i: (h, 0, 0))
    return _call(
        body, name=name, grid=(H, nq),
        in_specs=[qblk,
                  pl.BlockSpec((S, 128), lambda h, i: (0, H + h)),
                  pl.BlockSpec((S, 128), lambda h, i: (0, 2 * H + h)),
                  col, rowv, qblk, col, qblk],
        out_specs=[qblk, kvo, kvo, col, rowv],
        out_shape=[jax.ShapeDtypeStruct((S, FOX_WIDTH), BF16)] * 3
        + [jax.ShapeDtypeStruct((H, S, 1), F32), jax.ShapeDtypeStruct((H, 1, S), F32)],
        scratch_shapes=[pltpu.VMEM((S, 128), F32), pltpu.VMEM((S, 128), F32)],
        compiler_params=_cparams(("parallel", "arbitrary")),
    )(P, P, P, ccol, crow, o, lse, dcat)


def _s5_disc_fwd(lr, li, ls, *, name, after=()):
    G, Pn = lr.shape

    def body(lr_ref, li_ref, ls_ref, ar_ref, ai_ref, gr_ref, gi_ref):
        lr_, li_ = lr_ref[...], li_ref[...]
        dt = jnp.exp(ls_ref[...])
        mag = jnp.exp(lr_ * dt)
        th = li_ * dt
        ar = mag * jnp.cos(th)
        ai = mag * jnp.sin(th)
        den = lr_ * lr_ + li_ * li_
        xr = ar - 1.0
        ar_ref[...] = ar
        ai_ref[...] = ai
        gr_ref[...] = (xr * lr_ + ai * li_) / den
        gi_ref[...] = (ai * lr_ - xr * li_) / den

    sq = pl.BlockSpec((G, Pn), lambda: (0, 0))
    return _call(
        body, after=after, name=name, in_specs=[sq, sq, pl.BlockSpec((G, 1), lambda: (0, 0))], out_specs=[sq] * 4,
        out_shape=[jax.ShapeDtypeStruct((G, Pn), F32)] * 4,
    )(lr, li, ls)


def _s5_disc_bwd(lr, li, ls, dar, dai, dgr, dgi, *, name):
    G, Pn = lr.shape

    def body(lr_ref, li_ref, ls_ref, dar_ref, dai_ref, dgr_ref, dgi_ref, dlr_ref, dli_ref, dls_ref):
        lr_, li_ = lr_ref[...], li_ref[...]
        dt = jnp.exp(ls_ref[...])
        mag = jnp.exp(lr_ * dt)
        th = li_ * dt
        ar = mag * jnp.cos(th)
        ai = mag * jnp.sin(th)
        den = lr_ * lr_ + li_ * li_
        xr = ar - 1.0
        xi = ai
        g_re = (xr * lr_ + xi * li_) / den
        g_im = (xi * lr_ - xr * li_) / den
        dgr_, dgi_ = dgr_ref[...], dgi_ref[...]
        dxr = (dgr_ * lr_ - dgi_ * li_) / den
        dxi = (dgr_ * li_ + dgi_ * lr_) / den
        dden = -(dgr_ * g_re + dgi_ * g_im) / den
        dlr = (dgr_ * xr + dgi_ * xi) / den + 2.0 * dden * lr_
        dli = (dgr_ * xi - dgi_ * xr) / den + 2.0 * dden * li_
        da_r = dar_ref[...] + dxr
        da_i = dai_ref[...] + dxi
        dmag_mag = da_r * ar + da_i * ai
        dth = da_i * ar - da_r * ai
        dlr_ref[...] = dlr + dmag_mag * dt
        dli_ref[...] = dli + dth * dt
        ddt = jnp.sum(dmag_mag * lr_ + dth * li_, axis=-1, keepdims=True)
        dls_ref[...] = ddt * dt

    sq = pl.BlockSpec((G, Pn), lambda: (0, 0))
    c1 = pl.BlockSpec((G, 1), lambda: (0, 0))
    return _call(
        body, name=name, in_specs=[sq, sq, c1, sq, sq, sq, sq], out_specs=[sq, sq, c1],
        out_shape=[jax.ShapeDtypeStruct((G, Pn), F32)] * 2 + [jax.ShapeDtypeStruct((G, 1), F32)],
    )(lr, li, ls, dar, dai, dgr, dgi)


def _s5_bb_fwd(gr, gi, br, bi, *, name):
    R, C = br.shape

    def body(gr_ref, gi_ref, br_ref, bi_ref, or_ref, oi_ref):
        g_r, g_i, b_r, b_i = gr_ref[...], gi_ref[...], br_ref[...], bi_ref[...]
        or_ref[...] = g_r * b_r - g_i * b_i
        oi_ref[...] = g_r * b_i + g_i * b_r

    w = pl.BlockSpec((R, C), lambda: (0, 0))
    c1 = pl.BlockSpec((R, 1), lambda: (0, 0))
    return _call(body, name=name, in_specs=[c1, c1, w, w], out_specs=[w, w],
                 out_shape=[jax.ShapeDtypeStruct((R, C), F32)] * 2)(gr, gi, br, bi)


def _s5_bb_bwd(gr, gi, br, bi, dbbr, dbbi, *, name):
    R, C = br.shape

    def body(gr_ref, gi_ref, br_ref, bi_ref, dr_ref, di_ref, dbr_ref, dbi_ref, dgr_ref, dgi_ref):
        g_r, g_i, b_r, b_i = gr_ref[...], gi_ref[...], br_ref[...], bi_ref[...]
        d_r, d_i = dr_ref[...], di_ref[...]
        dbr_ref[...] = g_r * d_r + g_i * d_i
        dbi_ref[...] = g_r * d_i - g_i * d_r
        dgr_ref[...] = jnp.sum(d_r * b_r + d_i * b_i, axis=-1, keepdims=True)
        dgi_ref[...] = jnp.sum(d_i * b_r - d_r * b_i, axis=-1, keepdims=True)

    w = pl.BlockSpec((R, C), lambda: (0, 0))
    c1 = pl.BlockSpec((R, 1), lambda: (0, 0))
    return _call(body, name=name, in_specs=[c1, c1, w, w, w, w], out_specs=[w, w, c1, c1],
                 out_shape=[jax.ShapeDtypeStruct((R, C), F32)] * 2 + [jax.ShapeDtypeStruct((R, 1), F32)] * 2,
                 )(gr, gi, br, bi, dbbr, dbbi)


_DIAG_TILE = 8


def _diag_mask(gr, gc):
    rows, cols = _DIAG_TILE * gr, _DIAG_TILE * gc
    r = lax.broadcasted_iota(jnp.int32, (rows, cols), 0) >> (gr.bit_length() - 1)
    c = lax.broadcasted_iota(jnp.int32, (rows, cols), 1) >> (gc.bit_length() - 1)
    return r == c


def _diag_expand(t2, gr, gc, *, name, after=()):
    _, R, _ = t2.shape
    G = R // gr
    nt = G // _DIAG_TILE
    rows, cols = _DIAG_TILE * gr, _DIAG_TILE * gc

    def body(t_ref, o_ref):
        src = lax.broadcasted_iota(jnp.int32, (gc, cols), 0)
        dst = lax.broadcasted_iota(jnp.int32, (gc, cols), 1) & (gc - 1)
        spread = (src == dst).astype(BF16)
        y = jnp.dot(t_ref[...].astype(BF16), spread, preferred_element_type=F32)
        o_ref[...] = jnp.where(_diag_mask(gr, gc), y, 0.0).astype(BF16)

    return _call(
        body, after=after, name=name, grid=(2, nt),
        in_specs=[pl.BlockSpec((None, rows, gc), lambda p, i: (p, i, 0))],
        out_specs=pl.BlockSpec((None, rows, cols), lambda p, i: (p, i, i)),
        out_shape=jax.ShapeDtypeStruct((2, R, G * gc), BF16),
        compiler_params=_cparams(("parallel",) * 2),
    )(t2)


def _diag_extract(xd, gr, gc, *, name):
    _, R, _ = xd.shape
    nt = R // gr // _DIAG_TILE
    rows, cols = _DIAG_TILE * gr, _DIAG_TILE * gc

    def body(x_ref, o_ref):
        src = lax.broadcasted_iota(jnp.int32, (cols, gc), 0) & (gc - 1)
        dst = lax.broadcasted_iota(jnp.int32, (cols, gc), 1)
        fold = (src == dst).astype(BF16)
        parts = _split3(jnp.where(_diag_mask(gr, gc), x_ref[...], 0.0))
        acc = jnp.dot(parts[2], fold, preferred_element_type=F32)
        acc = acc + jnp.dot(parts[1], fold, preferred_element_type=F32)
        o_ref[...] = acc + jnp.dot(parts[0], fold, preferred_element_type=F32)

    return _call(
        body, name=name, grid=(2, nt),
        in_specs=[pl.BlockSpec((None, rows, cols), lambda p, i: (p, i, i))],
        out_specs=pl.BlockSpec((None, rows, gc), lambda p, i: (p, i, 0)),
        out_shape=jax.ShapeDtypeStruct((2, R, gc), F32),
        compiler_params=_cparams(("parallel",) * 2),
    )(xd)


SCAN_BLOCK = 8


def _cpowers(ar, ai, sign):
    ai = sign * ai
    out = [(ar, ai)]
    for _ in range(SCAN_BLOCK - 1):
        pr, pi = out[-1]
        out.append((pr * ar - pi * ai, pr * ai + pi * ar))
    return out


def _row_table(pw, row, index_of_row):
    tr_ = jnp.broadcast_to(pw[index_of_row(0)][0], row.shape)
    ti_ = jnp.broadcast_to(pw[index_of_row(0)][1], row.shape)
    for r in range(1, SCAN_BLOCK):
        pr, pi = pw[index_of_row(r)]
        tr_ = jnp.where(row == r, pr, tr_)
        ti_ = jnp.where(row == r, pi, ti_)
    return tr_, ti_


def _s5_scan_fwd(bu, a, *, name):
    _, S, N = bu.shape
    tc = 512
    nt = N // tc

    def body(a_ref, b_ref, h_ref):
        pw = _cpowers(a_ref[0], a_ref[1], 1.0)
        row = lax.broadcasted_iota(jnp.int32, (SCAN_BLOCK, tc), 0)
        lead_r, lead_i = _row_table(pw, row, lambda r: r)
        mult = {sh: (jnp.where(row >= sh, pw[sh - 1][0], 0.0), jnp.where(row >= sh, pw[sh - 1][1], 0.0))
                for sh in (1, 2, 4)}

        def step(k, carry):
            cr, ci = carry
            rows = pl.ds(pl.multiple_of(k * SCAN_BLOCK, SCAN_BLOCK), SCAN_BLOCK)
            xr, xi = b_ref[0, rows, :], b_ref[1, rows, :]
            for sh in (1, 2, 4):
                sr, si = pltpu.roll(xr, sh, 0), pltpu.roll(xi, sh, 0)
                kr, ki = mult[sh]
                xr, xi = xr + kr * sr - ki * si, xi + kr * si + ki * sr
            h_ref[0, rows, :] = xr + lead_r * cr - lead_i * ci
            h_ref[1, rows, :] = xi + lead_r * ci + lead_i * cr
            last = row == SCAN_BLOCK - 1
            tr_ = jnp.sum(jnp.where(last, xr, 0.0), axis=0, keepdims=True)
            ti_ = jnp.sum(jnp.where(last, xi, 0.0), axis=0, keepdims=True)
            a8r, a8i = pw[SCAN_BLOCK - 1]
            return a8r * cr - a8i * ci + tr_, a8r * ci + a8i * cr + ti_

        z = jnp.zeros((1, tc), F32)
        lax.fori_loop(0, S // SCAN_BLOCK, step, (z, z), unroll=2)

    vec = pl.BlockSpec((2, 1, tc), lambda j: (0, 0, j))
    mat = pl.BlockSpec((2, S, tc), lambda j: (0, 0, j))
    return _call(
        body, name=name, grid=(nt,), in_specs=[vec, mat], out_specs=mat,
        out_shape=jax.ShapeDtypeStruct((2, S, N), F32),
        compiler_params=_cparams(("parallel",)),
    )(a, bu)


def _s5_scan_bwd(g, h, a, *, name):
    _, S, N = g.shape
    tc = 256
    nt = N // tc

    def body(a_ref, g_ref, h_ref, l_ref, da_ref):
        pw = _cpowers(a_ref[0], a_ref[1], -1.0)
        row = lax.broadcasted_iota(jnp.int32, (SCAN_BLOCK, tc), 0)
        tail_r, tail_i = _row_table(pw, row, lambda r: SCAN_BLOCK - 1 - r)
        nb = S // SCAN_BLOCK
        mult = {sh: (jnp.where(row < SCAN_BLOCK - sh, pw[sh - 1][0], 0.0),
                     jnp.where(row < SCAN_BLOCK - sh, pw[sh - 1][1], 0.0)) for sh in (1, 2, 4)}

        def step(i, carry):
            k = nb - 1 - i
            cr, ci, dar, dai = carry
            rows = pl.ds(pl.multiple_of(k * SCAN_BLOCK, SCAN_BLOCK), SCAN_BLOCK)
            xr, xi = g_ref[0, rows, :], g_ref[1, rows, :]
            for sh in (1, 2, 4):
                sr, si = pltpu.roll(xr, SCAN_BLOCK - sh, 0), pltpu.roll(xi, SCAN_BLOCK - sh, 0)
                kr, ki = mult[sh]
                xr, xi = xr + kr * sr - ki * si, xi + kr * si + ki * sr
            lr = xr + tail_r * cr - tail_i * ci
            li = xi + tail_r * ci + tail_i * cr
            l_ref[0, rows, :] = lr
            l_ref[1, rows, :] = li
            prev = pl.ds(pl.multiple_of(jnp.maximum(k - 1, 0) * SCAN_BLOCK, SCAN_BLOCK), SCAN_BLOCK)
            has_prev = jnp.where(k > 0, 1.0, 0.0).astype(F32)
            first = row == 0
            hpr = jnp.where(first, pltpu.roll(h_ref[0, prev, :], 1, 0) * has_prev, pltpu.roll(h_ref[0, rows, :], 1, 0))
            hpi = jnp.where(first, pltpu.roll(h_ref[1, prev, :], 1, 0) * has_prev, pltpu.roll(h_ref[1, rows, :], 1, 0))
            tr_ = jnp.sum(jnp.where(first, xr, 0.0), axis=0, keepdims=True)
            ti_ = jnp.sum(jnp.where(first, xi, 0.0), axis=0, keepdims=True)
            a8r, a8i = pw[SCAN_BLOCK - 1]
            return (a8r * cr - a8i * ci + tr_, a8r * ci + a8i * cr + ti_,
                    dar + lr * hpr + li * hpi, dai + li * hpr - lr * hpi)

        z = jnp.zeros((1, tc), F32)
        z8 = jnp.zeros((SCAN_BLOCK, tc), F32)
        _, _, dar, dai = lax.fori_loop(0, nb, step, (z, z, z8, z8), unroll=2)
        da_ref[0] = jnp.sum(dar, axis=0, keepdims=True)
        da_ref[1] = jnp.sum(dai, axis=0, keepdims=True)

    vec = pl.BlockSpec((2, 1, tc), lambda j: (0, 0, j))
    mat = pl.BlockSpec((2, S, tc), lambda j: (0, 0, j))
    return _call(
        body, name=name, grid=(nt,), in_specs=[vec, mat, mat], out_specs=[mat, vec],
        out_shape=[jax.ShapeDtypeStruct((2, S, N), F32), jax.ShapeDtypeStruct((2, 1, N), F32)],
        compiler_params=_cparams(("parallel",)),
    )(a, g, h)


_GELU_C = math.sqrt(2.0 / math.pi)


def _s5_out_fwd(yc, P, dskip, *, name):
    S, W = yc.shape
    tr = _pick(S, 256)
    ub = 3 * FOX_WIDTH // W

    def body(yc_ref, u_ref, d_ref, y_ref, yg_ref):
        y = yc_ref[...] + d_ref[...] * u_ref[...]
        y_ref[...] = y
        t = jnp.tanh(_GELU_C * (y + 0.044715 * y * y * y))
        yg_ref[...] = (0.5 * y * (1.0 + t)).astype(BF16)

    row = pl.BlockSpec((tr, W), lambda i: (i, 0))
    return _call(
        body, name=name, grid=(S // tr,),
        in_specs=[row, pl.BlockSpec((tr, W), lambda i: (i, ub)), pl.BlockSpec((1, W), lambda i: (0, 0))],
        out_specs=[row, row],
        out_shape=[jax.ShapeDtypeStruct((S, W), F32), jax.ShapeDtypeStruct((S, W), BF16)],
        compiler_params=_cparams(("parallel",)),
    )(yc, P, dskip)


def _s5_out_bwd(dyg, y, P, dskip, *, name):
    S, W = y.shape
    tr = _pick(S, 256)
    ub = 3 * FOX_WIDTH // W

    def body(dyg_ref, y_ref, u_ref, d_ref, dy_ref, du_ref, dd_ref):
        y_ = y_ref[...]
        inner = _GELU_C * (y_ + 0.044715 * y_ * y_ * y_)
        t = jnp.tanh(inner)
        dgelu = 0.5 * (1.0 + t) + 0.5 * y_ * (1.0 - t * t) * _GELU_C * (1.0 + 3.0 * 0.044715 * y_ * y_)
        dy = dyg_ref[...] * dgelu
        dy_ref[...] = dy.astype(BF16)
        du_ref[...] = d_ref[...] * dy
        part = jnp.sum(dy * u_ref[...], axis=0, keepdims=True)

        @pl.when(pl.program_id(0) == 0)
        def _():
            dd_ref[...] = part

        @pl.when(pl.program_id(0) > 0)
        def _():
            dd_ref[...] += part

    row = pl.BlockSpec((tr, W), lambda i: (i, 0))
    vec = pl.BlockSpec((1, W), lambda i: (0, 0))
    return _call(
        body, name=name, grid=(S // tr,),
        in_specs=[row, row, pl.BlockSpec((tr, W), lambda i: (i, ub)), vec],
        out_specs=[row, row, vec],
        out_shape=[jax.ShapeDtypeStruct((S, W), BF16), jax.ShapeDtypeStruct((S, W), F32),
                   jax.ShapeDtypeStruct((1, W), F32)],
        compiler_params=_cparams(("arbitrary",)),
    )(dyg, y, P, dskip)


def _glu_fwd(z, fox, *, name):
    S, W2 = z.shape
    W = W2 // 2
    tr = _pick(S, 256)

    def body(z1_ref, z2_ref, f_ref, o_ref):
        o_ref[:, :W] = f_ref[...].astype(BF16)
        o_ref[:, W:] = (z1_ref[...] * jax.nn.sigmoid(z2_ref[...])).astype(BF16)

    lo = pl.BlockSpec((tr, W), lambda i: (i, 0))
    return _call(
        body, name=name, grid=(S // tr,),
        in_specs=[lo, pl.BlockSpec((tr, W), lambda i: (i, 1)), lo],
        out_specs=pl.BlockSpec((tr, W2), lambda i: (i, 0)),
        out_shape=jax.ShapeDtypeStruct((S, W2), BF16),
        compiler_params=_cparams(("parallel",)),
    )(z, z, fox)


def _glu_bwd(z, dcat, *, name):
    S, W2 = z.shape
    W = W2 // 2
    tr = _pick(S, 256)

    def body(z1_ref, z2_ref, d_ref, dz_ref):
        sg = jax.nn.sigmoid(z2_ref[...])
        d = d_ref[...]
        dz_ref[:, :W] = (d * sg).astype(BF16)
        dz_ref[:, W:] = (d * z1_ref[...] * sg * (1.0 - sg)).astype(BF16)

    lo = pl.BlockSpec((tr, W), lambda i: (i, 0))
    hi = pl.BlockSpec((tr, W), lambda i: (i, 1))
    return _call(
        body, name=name, grid=(S // tr,), in_specs=[lo, hi, hi],
        out_specs=pl.BlockSpec((tr, W2), lambda i: (i, 0)),
        out_shape=jax.ShapeDtypeStruct((S, W2), BF16),
        compiler_params=_cparams(("parallel",)),
    )(z, z, dcat)


ACT_ROWS = 16
ACT_COLS = 256


def _shift_down(cur, prev, k, row):
    return jnp.where(row >= k, pltpu.roll(cur, k, 0), pltpu.roll(prev, k, 0))


def _shift_up(cur, nxt, k, row):
    n = cur.shape[0]
    return jnp.where(row < n - k, pltpu.roll(cur, n - k, 0), pltpu.roll(nxt, n - k, 0))


def _act_fwd(h, cw, cb, *, name):
    _, S, FP = h.shape
    tr = _pick(S, 256)
    hb = tr // ACT_ROWS
    nq = tr // ACT_ROWS

    def body(g_ref, gh_ref, v_ref, vh_ref, wg_ref, wv_ref, bg_ref, bv_ref, a_ref, hc_ref):
        first = pl.program_id(1) == 0
        for c0 in range(0, FP, ACT_COLS):
            cw_ = min(ACT_COLS, FP - c0)
            cols = pl.ds(c0, cw_)
            rw = lax.broadcasted_iota(jnp.int32, (ACT_ROWS, cw_), 0)
            wg = [wg_ref[pl.ds(k, 1), cols] for k in range(3)]
            wv = [wv_ref[pl.ds(k, 1), cols] for k in range(3)]
            bg, bv = bg_ref[:, cols], bv_ref[:, cols]
            halo_g = jnp.where(first, 0.0, gh_ref[:, cols])
            halo_v = jnp.where(first, 0.0, vh_ref[:, cols])

            def chunk(q, _):
                rows = pl.ds(pl.multiple_of(q * ACT_ROWS, ACT_ROWS), ACT_ROWS)
                before = pl.ds(pl.multiple_of(jnp.maximum(q - 1, 0) * ACT_ROWS, ACT_ROWS), ACT_ROWS)
                g, v = g_ref[rows, cols], v_ref[rows, cols]
                gp = jnp.where(q > 0, g_ref[before, cols], halo_g)
                vp = jnp.where(q > 0, v_ref[before, cols], halo_v)
                cg = bg + wg[2] * g + wg[1] * _shift_down(g, gp, 1, rw) + wg[0] * _shift_down(g, gp, 2, rw)
                cv = bv + wv[2] * v + wv[1] * _shift_down(v, vp, 1, rw) + wv[0] * _shift_down(v, vp, 2, rw)
                a_ref[rows, cols] = (cg * jax.nn.sigmoid(cg) * cv).astype(BF16)
                hc_ref[0, rows, cols] = cg
                hc_ref[1, rows, cols] = cv
                return 0

            lax.fori_loop(0, nq, chunk, 0, unroll=2)

    def main(off):
        return pl.BlockSpec((None, tr, FP), lambda j, i: (j + off, i, 0))

    def halo(off):
        return pl.BlockSpec((None, ACT_ROWS, FP), lambda j, i: (j + off, jnp.maximum(i * hb - 1, 0), 0))

    def wspec(off):
        return pl.BlockSpec((None, 3, FP), lambda j, i: (j + off, 0, 0))

    def bspec(off):
        return pl.BlockSpec((None, 1, FP), lambda j, i: (j + off, 0, 0))

    cb3 = cb.reshape(4, 1, FP)
    return _call(
        body, name=name, grid=(2, S // tr),
        in_specs=[main(0), halo(0), main(2), halo(2), wspec(0), wspec(2), bspec(0), bspec(2)],
        out_specs=[pl.BlockSpec((None, tr, FP), lambda j, i: (j, i, 0)),
                   pl.BlockSpec((None, 2, tr, FP), lambda j, i: (j, 0, i, 0))],
        out_shape=[jax.ShapeDtypeStruct((2, S, FP), BF16), jax.ShapeDtypeStruct((2, 2, S, FP), F32)],
        compiler_params=_cparams(("parallel", "parallel")),
    )(h, h, h, h, cw, cw, cb3, cb3)


def _act_bwd(h, hc, da, cw, *, name):
    _, S, FP = h.shape
    tr = _pick(S, 256)
    nq = tr // ACT_ROWS
    nr = S // tr
    half = ACT_ROWS // 2

    def fold(x):
        return x[:half] + x[half:]

    def body(g_ref, v_ref, hc_ref, da_ref, wg_ref, wv_ref,
             dh_ref, dwg_ref, dwv_ref, dbg_ref, dbv_ref, carry_g, carry_v):
        i = pl.program_id(1)
        bottom = i == 0
        for c0 in range(0, FP, ACT_COLS):
            cw_ = min(ACT_COLS, FP - c0)
            cols = pl.ds(c0, cw_)
            rw = lax.broadcasted_iota(jnp.int32, (ACT_ROWS, cw_), 0)
            wg = [wg_ref[pl.ds(k, 1), cols] for k in range(3)]
            wv = [wv_ref[pl.ds(k, 1), cols] for k in range(3)]
            after_g = jnp.where(bottom, 0.0, carry_g[:, cols])
            after_v = jnp.where(bottom, 0.0, carry_v[:, cols])

            def chunk(s, carry):
                ng, nv, acc = carry[0], carry[1], carry[2:]
                q = nq - 1 - s
                rows = pl.ds(pl.multiple_of(q * ACT_ROWS, ACT_ROWS), ACT_ROWS)
                g, v = g_ref[rows, cols], v_ref[rows, cols]
                cg, cv = hc_ref[0, rows, cols], hc_ref[1, rows, cols]
                sg = jax.nn.sigmoid(cg)
                d = da_ref[rows, cols]
                dcg = d * cv * sg * (1.0 + cg * (1.0 - sg))
                dcv = d * cg * sg
                ug1, ug2 = _shift_up(dcg, ng, 1, rw), _shift_up(dcg, ng, 2, rw)
                uv1, uv2 = _shift_up(dcv, nv, 1, rw), _shift_up(dcv, nv, 2, rw)
                dh_ref[0, rows, cols] = (wg[2] * dcg + wg[1] * ug1 + wg[0] * ug2).astype(BF16)
                dh_ref[1, rows, cols] = (wv[2] * dcv + wv[1] * uv1 + wv[0] * uv2).astype(BF16)
                terms = (ug2 * g, ug1 * g, dcg * g, dcg, uv2 * v, uv1 * v, dcv * v, dcv)
                return (dcg, dcv) + tuple(a + fold(t) for a, t in zip(acc, terms))

            zero = jnp.zeros((half, cw_), F32)
            out = lax.fori_loop(0, nq, chunk, (after_g, after_v) + (zero,) * 8, unroll=2)
            carry_g[:, cols] = out[0]
            carry_v[:, cols] = out[1]
            sums = [jnp.sum(a, axis=0, keepdims=True) for a in out[2:]]

            @pl.when(bottom)
            def _():
                for k in range(3):
                    dwg_ref[pl.ds(k, 1), cols] = sums[k]
                    dwv_ref[pl.ds(k, 1), cols] = sums[4 + k]
                dbg_ref[:, cols] = sums[3]
                dbv_ref[:, cols] = sums[7]

            @pl.when(jnp.logical_not(bottom))
            def _():
                for k in range(3):
                    dwg_ref[pl.ds(k, 1), cols] += sums[k]
                    dwv_ref[pl.ds(k, 1), cols] += sums[4 + k]
                dbg_ref[:, cols] += sums[3]
                dbv_ref[:, cols] += sums[7]

    def main(off):
        return pl.BlockSpec((None, tr, FP), lambda j, i: (j + off, nr - 1 - i, 0))

    def wspec(off):
        return pl.BlockSpec((None, 3, FP), lambda j, i: (j + off, 0, 0))

    bspec = pl.BlockSpec((None, 1, FP), lambda j, i: (j, 0, 0))
    pair = pl.BlockSpec((None, 2, tr, FP), lambda j, i: (j, 0, nr - 1 - i, 0))
    dh, dwg, dwv, dbg, dbv = _call(
        body, name=name, grid=(2, nr),
        in_specs=[main(0), main(2), pair, main(0), wspec(0), wspec(2)],
        out_specs=[pair, wspec(0), wspec(0), bspec, bspec],
        out_shape=[jax.ShapeDtypeStruct((2, 2, S, FP), BF16)]
        + [jax.ShapeDtypeStruct((2, 3, FP), F32)] * 2 + [jax.ShapeDtypeStruct((2, 1, FP), F32)] * 2,
        scratch_shapes=[pltpu.VMEM((ACT_ROWS, FP), F32), pltpu.VMEM((ACT_ROWS, FP), F32)],
        compiler_params=_cparams(("parallel", "arbitrary")),
    )(h, h, hc, da, cw, cw)
    return (dh.reshape(4, S, FP), jnp.concatenate([dwg, dwv], axis=0), jnp.concatenate([dbg, dbv], axis=0))


def _rope_tables(posf, *, name, after=()):
    S = posf.shape[0]
    half = ROPE_DIM // 2
    d = np.arange(LANE) % SWA_HEAD_DIM
    invf = np.where(d < ROPE_DIM, ROPE_THETA ** (-(d % half).astype(np.float64) / half), 0.0).astype(np.float32)
    m_rot = (d < ROPE_DIM).astype(np.float32)
    m_a = (d < half).astype(np.float32)
    m_b = ((d >= half) & (d < ROPE_DIM)).astype(np.float32)
    consts = jnp.asarray(np.stack([invf, m_rot, m_a, m_b] + [np.zeros(LANE, np.float32)] * 4))

    def body(p_ref, k_ref, c_ref, sa_ref, sb_ref):
        k = k_ref[...]
        ang = p_ref[...] * k[0:1]
        co, si = jnp.cos(ang), jnp.sin(ang)
        c_ref[...] = k[1:2] * co + (1.0 - k[1:2])
        sa_ref[...] = -k[2:3] * si
        sb_ref[...] = k[3:4] * si

    full = pl.BlockSpec((S, LANE), lambda: (0, 0))
    return _call(
        body, after=after, name=name,
        in_specs=[pl.BlockSpec((S, 1), lambda: (0, 0)), pl.BlockSpec((8, LANE), lambda: (0, 0))],
        out_specs=[full] * 3, out_shape=[jax.ShapeDtypeStruct((S, LANE), F32)] * 3,
    )(posf, consts)


def _rope(xv, tabs_refs, width, inverse):
    rep = width // LANE
    c, sa, sb = (jnp.tile(t[...], (1, rep)) for t in tabs_refs)
    if not inverse:
        return xv * c + pltpu.roll(xv, width - 8, 1) * sa + pltpu.roll(xv, 8, 1) * sb
    return xv * c + pltpu.roll(xv * sa, 8, 1) + pltpu.roll(xv * sb, width - 8, 1)


def _to_heads(x, tabs, *, col0, width, rotate, name, out_dtype):
    S = x.shape[0]
    tr = _pick(S, 256)
    nh = width // SWA_HEAD_DIM
    cb = col0 // width

    def body(x_ref, c_ref, sa_ref, sb_ref, o_ref):
        xv = x_ref[...].astype(F32)
        if rotate:
            xv = _rope(xv, (c_ref, sa_ref, sb_ref), width, False)
        for h in range(nh):
            o_ref[h] = xv[:, h * SWA_HEAD_DIM:(h + 1) * SWA_HEAD_DIM].astype(out_dtype)

    tab = pl.BlockSpec((tr, LANE), lambda i: (i, 0))
    return _call(
        body, name=name, grid=(S // tr,),
        in_specs=[pl.BlockSpec((tr, width), lambda i: (i, cb)), tab, tab, tab],
        out_specs=pl.BlockSpec((nh, tr, SWA_HEAD_DIM), lambda i: (0, i, 0)),
        out_shape=jax.ShapeDtypeStruct((nh, S, SWA_HEAD_DIM), out_dtype),
        compiler_params=_cparams(("parallel",)),
    )(x, *tabs)


def _from_heads(x3, tabs, *, rotate_back, name, out_dtype, skip_rows=0):
    nh = x3.shape[0]
    S = x3.shape[1] - skip_rows
    width = nh * SWA_HEAD_DIM
    tr = _pick(S, 256) if skip_rows == 0 else skip_rows
    off = skip_rows // tr

    def body(x_ref, c_ref, sa_ref, sb_ref, o_ref):
        xv = jnp.concatenate([x_ref[h].astype(F32) for h in range(nh)], axis=1)
        if rotate_back:
            xv = _rope(xv, (c_ref, sa_ref, sb_ref), width, True)
        o_ref[...] = xv.astype(out_dtype)

    tab = pl.BlockSpec((tr, LANE), lambda i: (i, 0))
    return _call(
        body, name=name, grid=(S // tr,),
        in_specs=[pl.BlockSpec((nh, tr, SWA_HEAD_DIM), lambda i: (0, i + off, 0)), tab, tab, tab],
        out_specs=pl.BlockSpec((tr, width), lambda i: (i, 0)),
        out_shape=jax.ShapeDtypeStruct((S, width), out_dtype),
        compiler_params=_cparams(("parallel",)),
    )(x3, *tabs)


def _swa_mask(n):
    rows = SWA_GROUPS * SWA_WINDOW
    qi = lax.broadcasted_iota(jnp.int32, (rows, 2 * SWA_WINDOW), 0) & (SWA_WINDOW - 1)
    kj = lax.broadcasted_iota(jnp.int32, (rows, 2 * SWA_WINDOW), 1)
    rel = SWA_WINDOW + qi - kj
    return (rel >= 0) & (rel < SWA_WINDOW) & ((n > 0) | (kj >= SWA_WINDOW))


def _swa_fwd(qT, kT, vT, sink_rows, *, name):
    S = qT.shape[1]
    W, G, Dh = SWA_WINDOW, SWA_GROUPS, SWA_HEAD_DIM
    nb = S // W
    scale = 1.0 / math.sqrt(Dh)

    def body(q_ref, kp_ref, kc_ref, vp_ref, vc_ref, s_ref, o_ref, l_ref):
        n = pl.program_id(1)
        q = q_ref[...].reshape(G * W, Dh)
        kk = jnp.concatenate([kp_ref[...], kc_ref[...]], axis=0)
        vv = jnp.concatenate([vp_ref[...], vc_ref[...]], axis=0)
        s = lax.dot_general(q, kk, (((1,), (1,)), ((), ())), preferred_element_type=F32) * scale
        s = jnp.where(_swa_mask(n), s, -1e30)
        sink = s_ref[...]
        m = jnp.maximum(jnp.max(s, axis=-1, keepdims=True), sink)
        e = jnp.exp(s - m)
        den = jnp.sum(e, axis=-1, keepdims=True) + jnp.exp(sink - m)
        p = e / den
        o_ref[...] = jnp.dot(p.astype(BF16), vv, preferred_element_type=F32).reshape(G, W, Dh)
        l_ref[...] = (m + jnp.log(den)).reshape(G, W, 1)

    qs = pl.BlockSpec((G, W, Dh), lambda g, n: (g, n, 0))
    prev = pl.BlockSpec((None, W, Dh), lambda g, n: (g, jnp.maximum(n - 1, 0), 0))
    cur = pl.BlockSpec((None, W, Dh), lambda g, n: (g, n, 0))
    return _call(
        body, name=name, grid=(SWA_KV_HEADS, nb),
        in_specs=[qs, prev, cur, prev, cur, pl.BlockSpec((None, G * W, 1), lambda g, n: (g, 0, 0))],
        out_specs=[qs, pl.BlockSpec((G, W, 1), lambda g, n: (g, n, 0))],
        out_shape=[jax.ShapeDtypeStruct((SWA_HEADS, S, Dh), F32), jax.ShapeDtypeStruct((SWA_HEADS, S, 1), F32)],
        compiler_params=_cparams(("parallel", "parallel")),
    )(qT, kT, kT, vT, vT, sink_rows)


def _swa_bwd(qT, kT, vT, sink_rows, oT, L, doT, *, name):
    S = qT.shape[1]
    W, G, Dh = SWA_WINDOW, SWA_GROUPS, SWA_HEAD_DIM
    nb = S // W
    scale = 1.0 / math.sqrt(Dh)

    def body(q_ref, kp_ref, kc_ref, vp_ref, vc_ref, s_ref, o_ref, l_ref, do_ref,
             dq_ref, dk_ref, dv_ref, ds_ref):
        n = pl.program_id(1)
        q = q_ref[...].reshape(G * W, Dh)
        kk = jnp.concatenate([kp_ref[...], kc_ref[...]], axis=0)
        vv = jnp.concatenate([vp_ref[...], vc_ref[...]], axis=0)
        s = lax.dot_general(q, kk, (((1,), (1,)), ((), ())), preferred_element_type=F32) * scale
        lrow = l_ref[...].reshape(G * W, 1)
        p = jnp.where(_swa_mask(n), jnp.exp(s - lrow), 0.0)
        do = do_ref[...].reshape(G * W, Dh)
        do_bf = do.astype(BF16)
        dp = lax.dot_general(do_bf, vv, (((1,), (1,)), ((), ())), preferred_element_type=F32)
        delta = jnp.sum(do * o_ref[...].reshape(G * W, Dh), axis=-1, keepdims=True)
        dsc = p * (dp - delta)
        ds_bf = dsc.astype(BF16)
        dq_ref[...] = (jnp.dot(ds_bf, kk, preferred_element_type=F32) * scale).astype(BF16).reshape(G, W, Dh)
        dkk = lax.dot_general(ds_bf, q, (((0,), (0,)), ((), ())), preferred_element_type=F32) * scale
        dvv = lax.dot_general(p.astype(BF16), do_bf, (((0,), (0,)), ((), ())), preferred_element_type=F32)
        dsk = -jnp.exp(s_ref[...] - lrow) * delta
        dsk = jnp.broadcast_to(jnp.sum(dsk.reshape(G, W, 1), axis=1), (G, LANE))

        @pl.when(n == 0)
        def _():
            dk_ref[...] = jnp.zeros_like(dk_ref)
            dv_ref[...] = jnp.zeros_like(dv_ref)
            ds_ref[...] = jnp.zeros_like(ds_ref)

        rows = pl.ds(pl.multiple_of(n * W, W), 2 * W)
        dk_ref[rows, :] += dkk
        dv_ref[rows, :] += dvv
        ds_ref[...] += dsk

    qs = pl.BlockSpec((G, W, Dh), lambda g, n: (g, n, 0))
    prev = pl.BlockSpec((None, W, Dh), lambda g, n: (g, jnp.maximum(n - 1, 0), 0))
    cur = pl.BlockSpec((None, W, Dh), lambda g, n: (g, n, 0))
    lsp = pl.BlockSpec((G, W, 1), lambda g, n: (g, n, 0))
    kvo = pl.BlockSpec((None, S + W, Dh), lambda g, n: (g, 0, 0))
    return _call(
        body, name=name, grid=(SWA_KV_HEADS, nb),
        in_specs=[qs, prev, cur, prev, cur, pl.BlockSpec((None, G * W, 1), lambda g, n: (g, 0, 0)), qs, lsp, qs],
        out_specs=[qs, kvo, kvo, pl.BlockSpec((None, G, LANE), lambda g, n: (g, 0, 0))],
        out_shape=[jax.ShapeDtypeStruct((SWA_HEADS, S, Dh), BF16),
                   jax.ShapeDtypeStruct((SWA_KV_HEADS, S + W, Dh), F32),
                   jax.ShapeDtypeStruct((SWA_KV_HEADS, S + W, Dh), F32),
                   jax.ShapeDtypeStruct((SWA_KV_HEADS, G, LANE), F32)],
        compiler_params=_cparams(("parallel", "arbitrary")),
    )(qT, kT, kT, vT, vT, sink_rows, oT, L, doT)


def _adamw(w, g, m, v, *, name, tr=128, by_cols=False):
    L, R, C = w.shape
    split = isinstance(g, (list, tuple))
    HR, HC = _half_shape(R, C, by_cols) if split else (R, C)
    tr, tc = _tile2d(HR, HC, tr)
    nr, nc = HR // tr, HC // tc
    c1 = 1.0 / (1.0 - ADAM_B1 ** ADAM_STEP)
    c2 = 1.0 / (1.0 - ADAM_B2 ** ADAM_STEP)
    ng = 2 * L if split else 1

    def body(c_ref, *refs):
        w_ref, g_refs, (m_ref, v_ref, go_ref, d_ref, mo_ref, vo_ref) = refs[0], refs[1:1 + ng], refs[1 + ng:]
        if split:
            mine = pl.program_id(1) == c_ref[0]
            g_ = jnp.where(mine, g_refs[0][...], g_refs[1][...])
            for l in range(1, L):
                g_ = jnp.where(pl.program_id(0) == l,
                               jnp.where(mine, g_refs[2 * l][...], g_refs[2 * l + 1][...]), g_)
        else:
            g_ = g_refs[0][...]
        mn = ADAM_B1 * m_ref[...] + (1.0 - ADAM_B1) * g_
        vn = ADAM_B2 * v_ref[...] + (1.0 - ADAM_B2) * (g_ * g_)
        go_ref[...] = g_
        mo_ref[...] = mn
        vo_ref[...] = vn
        d_ref[...] = -ADAM_LR * ((mn * c1) / (jnp.sqrt(vn * c2) + ADAM_EPS) + ADAM_WD * w_ref[...])

    def whole(l, hf, i, j, c):
        return (l, i, hf * nc + j) if by_cols else (l, hf * nr + i, j)

    def half(layer, own):
        def index(l, hf, i, j, c):
            used = (l == layer) & ((hf == c[0]) if own else (hf != c[0]))
            return jnp.where(used, i, 0), jnp.where(used, j, 0)
        return pl.BlockSpec((tr, tc), index)

    row = pl.BlockSpec((None, tr, tc), whole)
    gs = [h for pair in g for h in pair] if split else [g]
    g_specs = [half(l, own) for l in range(L) for own in (True, False)] if split else [row]
    core = lax.axis_index("c").astype(jnp.int32).reshape(1)
    return _call(
        body, name=name,
        grid_spec=pltpu.PrefetchScalarGridSpec(
            num_scalar_prefetch=1, grid=(L, 2 if split else 1, nr, nc),
            in_specs=[row] + g_specs + [row, row], out_specs=[row] * 4),
        out_shape=[jax.ShapeDtypeStruct((L, R, C), F32)] * 4,
        compiler_params=_cparams(("parallel",) * 4),
    )(core, w, *gs, m, v)


def _adamw_half(w, g, m, v, *, name, own, prev=None, tr=128, by_cols=False):
    L, R, C = w.shape
    HR, HC = _half_shape(R, C, by_cols)
    tr, tc = _tile2d(HR, HC, tr)
    nr, nc = HR // tr, HC // tc
    c1 = 1.0 / (1.0 - ADAM_B1 ** ADAM_STEP)
    c2 = 1.0 / (1.0 - ADAM_B2 ** ADAM_STEP)

    def body(c_ref, *refs):
        w_ref, g_refs, m_ref, v_ref = refs[0], refs[1:1 + L], refs[1 + L], refs[2 + L]
        go_ref, d_ref, mo_ref, vo_ref = refs[-4:]
        g_ = g_refs[0][...]
        for l in range(1, L):
            g_ = jnp.where(pl.program_id(0) == l, g_refs[l][...], g_)
        mn = ADAM_B1 * m_ref[...] + (1.0 - ADAM_B1) * g_
        vn = ADAM_B2 * v_ref[...] + (1.0 - ADAM_B2) * (g_ * g_)
        go_ref[...] = g_
        mo_ref[...] = mn
        vo_ref[...] = vn
        d_ref[...] = -ADAM_LR * ((mn * c1) / (jnp.sqrt(vn * c2) + ADAM_EPS) + ADAM_WD * w_ref[...])

    def whole(l, i, j, c):
        hf = c[0] if own else 1 - c[0]
        return (l, i, hf * nc + j) if by_cols else (l, hf * nr + i, j)

    def layer_half(layer):
        def index(l, i, j, c):
            return jnp.where(l == layer, i, 0), jnp.where(l == layer, j, 0)
        return pl.BlockSpec((tr, tc), index)

    row = pl.BlockSpec((None, tr, tc), whole)
    core = lax.axis_index("c").astype(jnp.int32).reshape(1)
    prev = list(prev) if prev is not None else []
    return _call(
        body, name=name,
        grid_spec=pltpu.PrefetchScalarGridSpec(
            num_scalar_prefetch=1, grid=(L, nr, nc),
            in_specs=[row] + [layer_half(l) for l in range(L)] + [row, row] + [ANY] * len(prev),
            out_specs=[row] * 4),
        out_shape=[jax.ShapeDtypeStruct((L, R, C), F32)] * 4,
        input_output_aliases={4 + L + k: k for k in range(len(prev))},
        compiler_params=_cparams(("parallel",) * 3),
    )(core, w, *g, m, v, *prev)


def _sum2_halves(g4, s4, by_cols, *, name):
    n, R, C = g4.shape
    HR, HC = _half_shape(R, C, by_cols)
    tr, tc = _tile2d(HR, HC, budget=1024 * 1024)
    nr, nc = HR // tr, HC // tc
    core = lax.axis_index("c").astype(jnp.int32).reshape(1)

    def body(c_ref, g_ref, s_ref, o_ref):
        o_ref[...] = (g_ref[...].astype(F32) + s_ref[...].astype(F32)).astype(BF16)

    def mine(k, i, j, c):
        return (k, i, c[0] * nc + j) if by_cols else (k, c[0] * nr + i, j)

    blk = pl.BlockSpec((None, tr, tc), lambda k, i, j, c: (k, i, j))
    return _call(
        body, name=name,
        grid_spec=pltpu.PrefetchScalarGridSpec(
            num_scalar_prefetch=1, grid=(n, nr, nc),
            in_specs=[pl.BlockSpec((None, tr, tc), mine), blk], out_specs=blk),
        out_shape=jax.ShapeDtypeStruct((n, HR, HC), BF16),
        compiler_params=_cparams(("parallel", "parallel", "parallel")),
    )(core, g4, s4)


def _rowsum(parts, *, name, out_dtype=F32):
    n, R, C = parts.shape
    tr, tc = _tile2d(R, C, budget=512 * 1024)

    def body(p_ref, o_ref):
        acc = p_ref[0].astype(F32)
        for i in range(1, n):
            acc = acc + p_ref[i].astype(F32)
        o_ref[...] = acc.astype(out_dtype)

    return _call(
        body, name=name, grid=(R // tr, C // tc),
        in_specs=[pl.BlockSpec((n, tr, tc), lambda i, j: (0, i, j))],
        out_specs=pl.BlockSpec((tr, tc), lambda i, j: (i, j)),
        out_shape=jax.ShapeDtypeStruct((R, C), out_dtype),
        compiler_params=_cparams(("parallel", "parallel")),
    )(parts)


def _where_am_i():
    x, y, c = lax.axis_index("x"), lax.axis_index("y"), lax.axis_index("c")
    chips = [(1 - x, y), (x, 1 - y), (1 - x, 1 - y)]
    return x, y, c, chips


def _half_idx(rows, cols, by_cols, which):
    if by_cols:
        hc = cols // 2
        return (slice(None), pl.ds(pl.multiple_of(which * hc, LANE), hc))
    hr = rows // 2
    return (pl.ds(pl.multiple_of(which * hr, 16), hr), slice(None))


def _half_shape(rows, cols, by_cols):
    return (rows, cols // 2) if by_cols else (rows // 2, cols)


HBM_SPEC = pl.BlockSpec(memory_space=pltpu.HBM)
SEM_SPEC = pl.BlockSpec(memory_space=pltpu.SEMAPHORE)
DATAFLOW = pltpu.SideEffectType.DATAFLOW_SIDE_EFFECTING


def _chip_exchange_refs(kind, shards_shape, by_cols, src, land, i, chip_k, c, me):
    if kind == 'gather':
        half = _half_idx(*shards_shape, by_cols, c)
        return src.at[half], land.at[(me,) + half], land.at[(chip_k,) + half]
    return src.at[chip_k], land.at[me], land.at[chip_k]


def _chip_exchange_start(kind, srcs, by_cols, *, name, after=()):
    n = len(srcs)
    land_shapes = [((N_CHIPS,) + s.shape) if kind == 'gather' else s.shape for s in srcs]

    def body(*refs):
        src_refs, land_refs = refs[:n], refs[n:2 * n]
        send, recv = refs[2 * n + len(after)], refs[2 * n + len(after) + 1]
        token = refs[-1]
        x, y, c, chips = _where_am_i()
        me = 2 * x + y
        for i in range(n):
            for k, (px, py) in enumerate(chips):
                s, d, _ = _chip_exchange_refs(kind, srcs[i].shape, by_cols[i], src_refs[i], land_refs[i], i,
                                              2 * px + py, c, me)
                pltpu.make_async_remote_copy(src_ref=s, dst_ref=d, send_sem=send.at[3 * i + k],
                                             recv_sem=recv.at[3 * i + k], device_id=(px, py, c),
                                             device_id_type=MESH).start()
        token[...] = jnp.zeros_like(token)

    lands = [pltpu.with_memory_space_constraint(lax.empty(sh, s.dtype), pltpu.HBM) for sh, s in zip(land_shapes, srcs)]
    outs = _call(
        body, name=name,
        out_shape=(pltpu.SemaphoreType.DMA((3 * n,)), pltpu.SemaphoreType.DMA((3 * n,)),
                   *[pltpu.HBM(s.shape, s.dtype) for s in srcs],
                   *[pltpu.HBM(sh, s.dtype) for sh, s in zip(land_shapes, srcs)],
                   jax.ShapeDtypeStruct((8, LANE), F32)),
        in_specs=[HBM_SPEC] * (2 * n) + [ANY] * len(after),
        out_specs=(SEM_SPEC, SEM_SPEC, *([HBM_SPEC] * (2 * n)), pl.BlockSpec(memory_space=pltpu.VMEM)),
        input_output_aliases={j: 2 + j for j in range(2 * n)},
        compiler_params=pltpu.CompilerParams(has_side_effects=DATAFLOW),
    )(*[pltpu.with_memory_space_constraint(s, pltpu.HBM) for s in srcs], *lands, *after)
    return outs[0], outs[1], list(outs[2:2 + n]), list(outs[2 + n:2 + 2 * n]), outs[-1]


def _chip_exchange_wait(kind, send, recv, srcs, lands, by_cols, after, *, name):
    n = len(srcs)

    def body(*refs):
        src_refs, land_refs = refs[:n], refs[n:2 * n]
        send_r, recv_r = refs[2 * n], refs[2 * n + 1]
        x, y, c, chips = _where_am_i()
        me = 2 * x + y
        for i in range(n):
            for k, (px, py) in enumerate(chips):
                s, _, d = _chip_exchange_refs(kind, srcs[i].shape, by_cols[i], src_refs[i], land_refs[i], i,
                                              2 * px + py, c, me)
                cp = pltpu.make_async_remote_copy(src_ref=s, dst_ref=d, send_sem=send_r.at[3 * i + k],
                                                  recv_sem=recv_r.at[3 * i + k], device_id=(px, py, c),
                                                  device_id_type=MESH)
                cp.wait_send()
                cp.wait_recv()

    outs = _call(
        body, name=name,
        out_shape=(*[pltpu.HBM(s.shape, s.dtype) for s in srcs], *[pltpu.HBM(l.shape, l.dtype) for l in lands]),
        in_specs=[HBM_SPEC] * (2 * n) + [SEM_SPEC, SEM_SPEC] + [ANY] * len(after),
        out_specs=tuple([HBM_SPEC] * (2 * n)),
        input_output_aliases={j: j for j in range(2 * n)},
        compiler_params=pltpu.CompilerParams(has_side_effects=DATAFLOW),
    )(*srcs, *lands, send, recv, *after)
    return list(outs[:n]), list(outs[n:])


def _sibling_halves_start(grads, by_cols, *, name, after=()):
    n = len(grads)
    land_shapes = [(N_CHIPS,) + _half_shape(*g.shape[1:], bc) for g, bc in zip(grads, by_cols)]

    def body(*refs):
        src_refs, land_refs = refs[:n], refs[n:2 * n]
        send, recv = refs[2 * n + len(after)], refs[2 * n + len(after) + 1]
        token = refs[-1]
        x, y, c, _ = _where_am_i()
        for i in range(n):
            src = src_refs[i].at[(slice(None),) + _half_idx(*grads[i].shape[1:], by_cols[i], 1 - c)]
            pltpu.make_async_remote_copy(src_ref=src, dst_ref=land_refs[i], send_sem=send.at[i], recv_sem=recv.at[i],
                                         device_id=(x, y, 1 - c), device_id_type=MESH).start()
        token[...] = jnp.zeros_like(token)

    lands = [pltpu.with_memory_space_constraint(lax.empty(sh, g.dtype), pltpu.HBM) for sh, g in zip(land_shapes, grads)]
    outs = _call(
        body, name=name,
        out_shape=(pltpu.SemaphoreType.DMA((n,)), pltpu.SemaphoreType.DMA((n,)),
                   *[pltpu.HBM(g.shape, g.dtype) for g in grads],
                   *[pltpu.HBM(sh, g.dtype) for sh, g in zip(land_shapes, grads)],
                   jax.ShapeDtypeStruct((8, LANE), F32)),
        in_specs=[HBM_SPEC] * (2 * n) + [ANY] * len(after),
        out_specs=(SEM_SPEC, SEM_SPEC, *([HBM_SPEC] * (2 * n)), pl.BlockSpec(memory_space=pltpu.VMEM)),
        input_output_aliases={j: 2 + j for j in range(2 * n)},
        compiler_params=pltpu.CompilerParams(has_side_effects=DATAFLOW),
    )(*[pltpu.with_memory_space_constraint(g, pltpu.HBM) for g in grads], *lands, *after)
    return outs[0], outs[1], list(outs[2:2 + n]), list(outs[2 + n:2 + 2 * n]), outs[-1]


def _sibling_halves_wait(send, recv, grads, lands, by_cols, after, *, name):
    n = len(grads)

    def body(*refs):
        src_refs, land_refs = refs[:n], refs[n:2 * n]
        send_r, recv_r = refs[2 * n], refs[2 * n + 1]
        x, y, c, _ = _where_am_i()
        for i in range(n):
            src = src_refs[i].at[(slice(None),) + _half_idx(*grads[i].shape[1:], by_cols[i], 1 - c)]
            cp = pltpu.make_async_remote_copy(src_ref=src, dst_ref=land_refs[i], send_sem=send_r.at[i],
                                              recv_sem=recv_r.at[i], device_id=(x, y, 1 - c), device_id_type=MESH)
            cp.wait_send()
            cp.wait_recv()

    outs = _call(
        body, name=name,
        out_shape=(*[pltpu.HBM(g.shape, g.dtype) for g in grads], *[pltpu.HBM(l.shape, l.dtype) for l in lands]),
        in_specs=[HBM_SPEC] * (2 * n) + [SEM_SPEC, SEM_SPEC] + [ANY] * len(after),
        out_specs=tuple([HBM_SPEC] * (2 * n)),
        input_output_aliases={j: j for j in range(2 * n)},
        compiler_params=pltpu.CompilerParams(has_side_effects=DATAFLOW),
    )(*grads, *lands, send, recv, *after)
    return list(outs[:n]), list(outs[n:])


def _sibling_swap_start(arrs, *, name, after=()):
    n = len(arrs)

    def body(*refs):
        src_refs, land_refs = refs[:n], refs[n:2 * n]
        send, recv = refs[2 * n + len(after)], refs[2 * n + len(after) + 1]
        token = refs[-1]
        x, y, c, _ = _where_am_i()
        for i in range(n):
            pltpu.make_async_remote_copy(src_ref=src_refs[i], dst_ref=land_refs[i], send_sem=send.at[i],
                                         recv_sem=recv.at[i], device_id=(x, y, 1 - c), device_id_type=MESH).start()
        token[...] = jnp.zeros_like(token)

    lands = [pltpu.with_memory_space_constraint(lax.empty(a.shape, a.dtype), pltpu.HBM) for a in arrs]
    outs = _call(
        body, name=name,
        out_shape=(pltpu.SemaphoreType.DMA((n,)), pltpu.SemaphoreType.DMA((n,)),
                   *[pltpu.HBM(a.shape, a.dtype) for a in arrs] * 2, jax.ShapeDtypeStruct((8, LANE), F32)),
        in_specs=[HBM_SPEC] * (2 * n) + [ANY] * len(after),
        out_specs=(SEM_SPEC, SEM_SPEC, *([HBM_SPEC] * (2 * n)), pl.BlockSpec(memory_space=pltpu.VMEM)),
        input_output_aliases={j: 2 + j for j in range(2 * n)},
        compiler_params=pltpu.CompilerParams(has_side_effects=DATAFLOW),
    )(*[pltpu.with_memory_space_constraint(a, pltpu.HBM) for a in arrs], *lands, *after)
    return outs[0], outs[1], list(outs[2:2 + n]), list(outs[2 + n:2 + 2 * n]), outs[-1]


def _sibling_swap_wait(send, recv, arrs, lands, after, *, name):
    n = len(arrs)

    def body(*refs):
        src_refs, land_refs = refs[:n], refs[n:2 * n]
        send_r, recv_r = refs[2 * n], refs[2 * n + 1]
        x, y, c, _ = _where_am_i()
        for i in range(n):
            cp = pltpu.make_async_remote_copy(src_ref=src_refs[i], dst_ref=land_refs[i], send_sem=send_r.at[i],
                                              recv_sem=recv_r.at[i], device_id=(x, y, 1 - c), device_id_type=MESH)
            cp.wait_send()
            cp.wait_recv()

    outs = _call(
        body, name=name,
        out_shape=tuple(pltpu.HBM(a.shape, a.dtype) for a in list(arrs) + list(lands)),
        in_specs=[HBM_SPEC] * (2 * n) + [SEM_SPEC, SEM_SPEC] + [ANY] * len(after),
        out_specs=tuple([HBM_SPEC] * (2 * n)),
        input_output_aliases={j: j for j in range(2 * n)},
        compiler_params=pltpu.CompilerParams(has_side_effects=DATAFLOW),
    )(*arrs, *lands, send, recv, *after)
    return list(outs[:n]), list(outs[n:])


def _sibling_pass_gathered(lands, shard_shapes, by_cols, *, name):
    n = len(lands)

    def body(*refs):
        outs = refs[n:2 * n]
        send, recv = refs[2 * n:]
        x, y, c, chips = _where_am_i()
        sibling = (x, y, 1 - c)
        cps = []
        for i in range(n):
            for k, (px, py) in enumerate(chips):
                blk = outs[i].at[(2 * px + py,) + _half_idx(*shard_shapes[i], by_cols[i], c)]
                d = pltpu.make_async_remote_copy(src_ref=blk, dst_ref=blk, send_sem=send.at[i, k],
                                                 recv_sem=recv.at[i, k], device_id=sibling, device_id_type=MESH)
                d.start()
                cps.append(d)
        for i in range(n):
            for k, (px, py) in enumerate(chips):
                blk = outs[i].at[(2 * px + py,) + _half_idx(*shard_shapes[i], by_cols[i], 1 - c)]
                pltpu.make_async_remote_copy(src_ref=blk, dst_ref=blk, send_sem=send.at[i, k], recv_sem=recv.at[i, k],
                                             device_id=sibling, device_id_type=MESH).wait_recv()
        for d in cps:
            d.wait_send()

    return _call(
        body, name=name, in_specs=[ANY] * n, out_specs=[ANY] * n,
        out_shape=[jax.ShapeDtypeStruct(l.shape, l.dtype) for l in lands],
        input_output_aliases={j: j for j in range(n)},
        scratch_shapes=[pltpu.SemaphoreType.DMA((n, 3)), pltpu.SemaphoreType.DMA((n, 3))],
    )(*lands)


def _own_slot(lands, owns):
    me = 2 * lax.axis_index("x") + lax.axis_index("y")
    return [lax.dynamic_update_slice_in_dim(g, s, me, axis=0) for g, s in zip(lands, owns)]


def _sibling_send_halves(grads, by_cols, *, name):
    n = len(grads)

    def body(*refs):
        ins, outs = refs[:n], refs[n:2 * n]
        send, recv = refs[2 * n:]
        x, y, c, _ = _where_am_i()
        sibling = (x, y, 1 - c)
        cps = []
        for i in range(n):
            src = ins[i].at[(slice(None),) + _half_idx(*grads[i].shape[1:], by_cols[i], 1 - c)]
            d = pltpu.make_async_remote_copy(src_ref=src, dst_ref=outs[i], send_sem=send.at[i],
                                             recv_sem=recv.at[i], device_id=sibling, device_id_type=MESH)
            d.start()
            cps.append(d)
        for d in cps:
            d.wait()

    return _call(
        body, name=name, in_specs=[ANY] * n, out_specs=[ANY] * n,
        out_shape=[jax.ShapeDtypeStruct((N_CHIPS,) + _half_shape(*g.shape[1:], bc), g.dtype)
                   for g, bc in zip(grads, by_cols)],
        scratch_shapes=[pltpu.SemaphoreType.DMA((n,)), pltpu.SemaphoreType.DMA((n,))],
    )(*grads)


def _all_reduce_small(v, *, name, after=()):
    R, C = v.shape
    H = R // 2

    def body(v_ref, o_ref, sib, slots, send, recv):
        x, y, c, chips = _where_am_i()
        me = 2 * x + y
        sibling = (x, y, 1 - c)
        mine = pl.ds(pl.multiple_of(c * H, 8), H)
        other = pl.ds(pl.multiple_of((1 - c) * H, 8), H)

        def copy(k, src, dst, to):
            return pltpu.make_async_remote_copy(src_ref=src, dst_ref=dst, send_sem=send.at[k], recv_sem=recv.at[k],
                                                device_id=to, device_id_type=MESH)

        d = copy(0, v_ref.at[other], sib, sibling)
        d.start()
        d.wait()
        slots[me] = v_ref[mine, :] + sib[...]
        cps = [copy(1 + k, slots.at[me], slots.at[me], (px, py, c)) for k, (px, py) in enumerate(chips)]
        for d in cps:
            d.start()
        for k, (px, py) in enumerate(chips):
            blk = slots.at[2 * px + py]
            copy(1 + k, blk, blk, (px, py, c)).wait_recv()
        for d in cps:
            d.wait_send()
        o_ref[mine, :] = (slots[0] + slots[1]) + (slots[2] + slots[3])
        d = copy(4, o_ref.at[mine], o_ref.at[mine], sibling)
        d.start()
        copy(4, o_ref.at[other], o_ref.at[other], sibling).wait_recv()
        d.wait_send()

    vm = pl.BlockSpec(memory_space=pltpu.VMEM)
    return _call(
        body, after=after, name=name, in_specs=[vm], out_specs=vm,
        out_shape=jax.ShapeDtypeStruct((R, C), F32),
        scratch_shapes=[pltpu.VMEM((H, C), F32), pltpu.VMEM((N_CHIPS, H, C), F32),
                        pltpu.SemaphoreType.DMA((5,)), pltpu.SemaphoreType.DMA((5,))],
        compiler_params=pltpu.CompilerParams(vmem_limit_bytes=VMEM_LIMIT),
    )(v)


def _cols_from_shards(g):
    return jnp.transpose(g, (1, 0, 2)).reshape(g.shape[1], -1)


def _shards_from_cols(w):
    R, C4 = w.shape
    return jnp.transpose(w.reshape(R, N_CHIPS, C4 // N_CHIPS), (1, 0, 2))


def _pack(arrs):
    flat = []
    for a in arrs:
        f = a.reshape(-1).astype(F32)
        flat.append(jnp.pad(f, (0, _rup(f.shape[0], LANE) - f.shape[0])))
    v = jnp.concatenate(flat)
    rows = _rup(v.shape[0] // LANE, 16)
    v = jnp.pad(v, (0, rows * LANE - v.shape[0]))
    return v.reshape(rows, LANE)


def _unpack(v, shapes):
    flat = v.reshape(-1)
    out, off = [], 0
    for s in shapes:
        n = int(np.prod(s))
        out.append(flat[off:off + n].reshape(s))
        off += _rup(n, LANE)
    return out


def _ffn_fwd(x, Wup, Wdn, cw, cb, tag):
    h = _mm(x, Wup, 'nt', bmode='bo', tm=512, tn=4096, name=f"ffn_up_{tag}")
    a, hc = _act_fwd(h, cw, cb, name=f"ffn_act_{tag}")
    f = _mm(a, Wdn, 'nn', bmode='abr', tm=512, tn=1024, tk=4096, name=f"ffn_down_{tag}")
    return f, (h, hc), a


def _ffn_bwd(df, x, saved, a, Wup, Wdn, cw, tag):
    h, hc = saved
    da = _mm(df, Wdn, 'nt', bmode='bo', tm=512, tn=4096, name=f"ffn_da_{tag}")
    dWdn = _mm(a, df, 'tn', bmode='ao', tm=4096, tn=512, name=f"ffn_dwdn_{tag}", out_dtype=BF16)
    dh, dcw, dcb = _act_bwd(h, hc, da, cw, name=f"ffn_actb_{tag}")

    def shard_of(k):
        return (k % 2) * 2 + k // 2

    dx = _mm(dh, Wup, 'nn', bmode='abr', tm=512, tn=1024, tk=4096, name=f"ffn_dx_{tag}", b_map=shard_of)
    dWup = _mm(dh, x, 'tn', bmode='ao', tm=4096, tn=512, name=f"ffn_dwup_{tag}", out_dtype=BF16,
               o_map=shard_of)
    return dx, dWup, dWdn, dcw, dcb


def kernel(x, positions, ev_w_in, ev_b_f, ev_lambda_re, ev_lambda_im, ev_log_step, ev_ssm_b_re, ev_ssm_b_im, ev_ssm_c_re, ev_ssm_c_im, ev_ssm_d, ev_w_glu, ev_w_out, od_w_in, od_sinks, od_w_out, ln_mix_g, ln_mix_b, ffn_w_up, ffn_conv_w, ffn_conv_b, ffn_w_down, ln_ffn_g, ln_ffn_b, loss_target, m_ev_w_in, m_ev_b_f, m_ev_lambda_re, m_ev_lambda_im, m_ev_log_step, m_ev_ssm_b_re, m_ev_ssm_b_im, m_ev_ssm_c_re, m_ev_ssm_c_im, m_ev_ssm_d, m_ev_w_glu, m_ev_w_out, m_od_w_in, m_od_sinks, m_od_w_out, m_ln_mix_g, m_ln_mix_b, m_ffn_w_up, m_ffn_conv_w, m_ffn_conv_b, m_ffn_w_down, m_ln_ffn_g, m_ln_ffn_b, v_ev_w_in, v_ev_b_f, v_ev_lambda_re, v_ev_lambda_im, v_ev_log_step, v_ev_ssm_b_re, v_ev_ssm_b_im, v_ev_ssm_c_re, v_ev_ssm_c_im, v_ev_ssm_d, v_ev_w_glu, v_ev_w_out, v_od_w_in, v_od_sinks, v_od_w_out, v_ln_mix_g, v_ln_mix_b, v_ffn_w_up, v_ffn_conv_w, v_ffn_conv_b, v_ffn_w_down, v_ln_ffn_g, v_ln_ffn_b):
    W = dict(ev_w_in=ev_w_in, ev_b_f=ev_b_f, ev_lambda_re=ev_lambda_re, ev_lambda_im=ev_lambda_im, ev_log_step=ev_log_step, ev_ssm_b_re=ev_ssm_b_re, ev_ssm_b_im=ev_ssm_b_im, ev_ssm_c_re=ev_ssm_c_re, ev_ssm_c_im=ev_ssm_c_im, ev_ssm_d=ev_ssm_d, ev_w_glu=ev_w_glu, ev_w_out=ev_w_out, od_w_in=od_w_in, od_sinks=od_sinks, od_w_out=od_w_out, ln_mix_g=ln_mix_g, ln_mix_b=ln_mix_b, ffn_w_up=ffn_w_up, ffn_conv_w=ffn_conv_w, ffn_conv_b=ffn_conv_b, ffn_w_down=ffn_w_down, ln_ffn_g=ln_ffn_g, ln_ffn_b=ln_ffn_b)
    Mo = dict(ev_w_in=m_ev_w_in, ev_b_f=m_ev_b_f, ev_lambda_re=m_ev_lambda_re, ev_lambda_im=m_ev_lambda_im, ev_log_step=m_ev_log_step, ev_ssm_b_re=m_ev_ssm_b_re, ev_ssm_b_im=m_ev_ssm_b_im, ev_ssm_c_re=m_ev_ssm_c_re, ev_ssm_c_im=m_ev_ssm_c_im, ev_ssm_d=m_ev_ssm_d, ev_w_glu=m_ev_w_glu, ev_w_out=m_ev_w_out, od_w_in=m_od_w_in, od_sinks=m_od_sinks, od_w_out=m_od_w_out, ln_mix_g=m_ln_mix_g, ln_mix_b=m_ln_mix_b, ffn_w_up=m_ffn_w_up, ffn_conv_w=m_ffn_conv_w, ffn_conv_b=m_ffn_conv_b, ffn_w_down=m_ffn_w_down, ln_ffn_g=m_ln_ffn_g, ln_ffn_b=m_ln_ffn_b)
    Vo = dict(ev_w_in=v_ev_w_in, ev_b_f=v_ev_b_f, ev_lambda_re=v_ev_lambda_re, ev_lambda_im=v_ev_lambda_im, ev_log_step=v_ev_log_step, ev_ssm_b_re=v_ev_ssm_b_re, ev_ssm_b_im=v_ev_ssm_b_im, ev_ssm_c_re=v_ev_ssm_c_re, ev_ssm_c_im=v_ev_ssm_c_im, ev_ssm_d=v_ev_ssm_d, ev_w_glu=v_ev_w_glu, ev_w_out=v_ev_w_out, od_w_in=v_od_w_in, od_sinks=v_od_sinks, od_w_out=v_od_w_out, ln_mix_g=v_ln_mix_g, ln_mix_b=v_ln_mix_b, ffn_w_up=v_ffn_w_up, ffn_conv_w=v_ffn_conv_w, ffn_conv_b=v_ffn_conv_b, ffn_w_down=v_ffn_w_down, ln_ffn_g=v_ln_ffn_g, ln_ffn_b=v_ln_ffn_b)
    names = list(W.keys())
    big = ['ev_w_in', 'ev_w_glu', 'ev_w_out', 'od_w_in', 'od_w_out', 'ffn_w_up', 'ffn_w_down']

    S, D = x.shape[1], x.shape[2]
    x0 = x.reshape(S, D)
    tgt = loss_target.reshape(S, D)
    G, Pn, Cg = SSM_GROUPS, SSM_STATE, SSM_GROUP
    Fs = ffn_w_up.shape[2]
    FP = Fs
    Rd = ffn_w_down.shape[1]
    EIN = N_CHIPS * ev_w_in.shape[2]

    cwl = ffn_conv_w.reshape(-1)
    cw_rows = _rup(_rup(cwl.shape[0], LANE) // LANE, 32)
    cw_pad = jnp.pad(cwl, (0, cw_rows * LANE - cwl.shape[0])).reshape(cw_rows, LANE)
    transposed = ('ev_w_in', 'ffn_w_up')

    def view(n, a):
        return jnp.transpose(a, (0, 2, 1)) if n in transposed else a

    Wv = {n: view(n, W[n]) for n in big}
    big_e = [(n, l) for n in big for l in range(W[n].shape[0])]
    split_cols = {e: (Wv[e[0]].shape[1] // 2) % 16 != 0 for e in big_e}
    shard16 = {e: Wv[e[0]][e[1]].astype(BF16) for e in big_e}
    grp_now = [e for e in big_e if e[0].startswith('ev_')]
    grp_ffn0 = [('ffn_w_up', 0), ('ffn_w_down', 0)]
    grp_l1 = [('od_w_in', 0), ('od_w_out', 0), ('ffn_w_up', 1), ('ffn_w_down', 1)]
    src_now = [shard16[e] for e in grp_now]
    src_ffn0 = [shard16[e] for e in grp_ffn0] + [cw_pad]
    src_l1 = [shard16[e] for e in grp_l1]
    cols_now = [split_cols[e] for e in grp_now]
    cols_ffn0 = [split_cols[e] for e in grp_ffn0] + [False]
    cols_l1 = [split_cols[e] for e in grp_l1]
    ag_in = _chip_exchange_start('gather', src_now[:1], cols_now[:1], name="ag_in_start")
    ag_mix = _chip_exchange_start('gather', src_now[1:], cols_now[1:], name="ag_mix_start", after=[ag_in[4]])
    ag_ffn0 = _chip_exchange_start('gather', src_ffn0, cols_ffn0, name="ag_ffn0_start", after=[ag_mix[4]])
    ag_l1 = _chip_exchange_start('gather', src_l1, cols_l1, name="ag_l1_start", after=[ag_ffn0[4]])
    started = [ag_l1[4]]

    def finish_gather(started, srcs, cols, after, tag):
        send, recv, thru, lands, _ = started
        thru, lands = _chip_exchange_wait('gather', send, recv, thru, lands, cols, after, name=f"ag_{tag}_wait")
        lands = _sibling_pass_gathered(lands, [s.shape for s in srcs], cols, name=f"ag_{tag}_pass")
        return _own_slot(lands, [s[None] for s in thru])

    lam_r, lam_i = ev_lambda_re[0], ev_lambda_im[0]
    lstep = ev_log_step[0].reshape(G, 1)
    a_re, a_im, g_re, g_im = _s5_disc_fwd(lam_r, lam_i, lstep, name="s5_disc", after=started)
    b_re2, b_im2 = ev_ssm_b_re[0].reshape(G * Pn, Cg), ev_ssm_b_im[0].reshape(G * Pn, Cg)
    g_re1, g_im1 = g_re.reshape(G * Pn, 1), g_im.reshape(G * Pn, 1)
    bb_re, bb_im = _s5_bb_fwd(g_re1, g_im1, b_re2, b_im2, name="s5_bb")
    bbt = jnp.stack([jnp.transpose(b.reshape(G, Pn, Cg), (0, 2, 1)).reshape(G * Cg, Pn) for b in (bb_re, bb_im)])
    BB = _diag_expand(bbt, Cg, Pn, name="s5_bb_dense")
    cct = jnp.stack([jnp.transpose(ev_ssm_c_re[0], (0, 2, 1)).reshape(G * Pn, Cg),
                     jnp.transpose(-ev_ssm_c_im[0], (0, 2, 1)).reshape(G * Pn, Cg)])
    CC = _diag_expand(cct, Pn, Cg, name="s5_cc_dense", after=started)
    a_cat = jnp.stack([a_re.reshape(1, G * Pn), a_im.reshape(1, G * Pn)])
    dskip = ev_ssm_d[0].reshape(1, SSM_WIDTH)
    tabs = _rope_tables(positions.reshape(S, 1).astype(F32), name="rope_tables", after=[BB, CC])

    gw = dict(zip(grp_now[:1], finish_gather(ag_in, src_now[:1], cols_now[:1], [tabs[2]], "in")))
    w_in_t = gw[('ev_w_in', 0)].reshape(EIN, D)
    qkv_w = 3 * FOX_WIDTH
    WmainT = jnp.concatenate([w_in_t[:qkv_w], w_in_t[qkv_w + FOX_HEADS:]], axis=0)
    WfT = jnp.pad(w_in_t[qkv_w:qkv_w + FOX_HEADS], ((0, LANE - FOX_HEADS), (0, 0)))
    cbs = [ffn_conv_b[l].reshape(N_CHIPS, Fs) for l in range(DEPTH)]

    P = _mm(x0, WmainT, 'nt', name="ev_proj")
    fl = _mm(x0, WfT, 'nt', name="ev_proj_f")
    bf_pad = jnp.pad(ev_b_f.reshape(1, FOX_HEADS), ((0, 0), (0, LANE - FOX_HEADS)))
    cgate, sgate = _gate_fwd(fl, bf_pad, name="fox_gate")
    ccol = jnp.transpose(cgate[:, :FOX_HEADS]).reshape(FOX_HEADS, S, 1)
    crow = jnp.transpose(cgate[:, :FOX_HEADS]).reshape(FOX_HEADS, 1, S)
    fox, lse = _fox_fwd(P, ccol, crow, name="fox_fwd")
    u_s5 = P[:, qkv_w:]
    UT, HT = _DIAG_TILE * Cg, _DIAG_TILE * Pn
    bu = _mm(u_s5, BB, 'nn', bmode='bo', tm=2048, tn=HT, tk=UT, diag='kn', name="s5_bu")
    hh = _s5_scan_fwd(bu, a_cat, name="s5_scan")
    yc = _mm(hh, CC, 'nn', bmode='abr', tm=2048, tn=UT, tk=HT, diag='kn', name="s5_y")
    y_s5, yg = _s5_out_fwd(yc, P, dskip, name="s5_out")
    gw.update(zip(grp_now[1:], finish_gather(ag_mix, src_now[1:], cols_now[1:], [yg], "mix")))
    Wglu = _cols_from_shards(gw[('ev_w_glu', 0)])
    Wout_ev = gw[('ev_w_out', 0)].reshape(D, D)
    z = _mm(yg, Wglu, 'nn', name="s5_glu_proj")
    cat = _glu_fwd(z, fox, name="s5_glu")
    mix0 = _mm(cat, Wout_ev, 'nn', name="ev_out")
    x1, xh1, rs1 = _add_ln_fwd(x0, mix0, ln_mix_g[0], ln_mix_b[0], name="ln_mix0")
    got = finish_gather(ag_ffn0, src_ffn0, cols_ffn0, [x1], "ffn0")
    gw.update(zip(grp_ffn0, got[:-1]))
    cw_all = got[-1].reshape(N_CHIPS, -1)[:, :cwl.shape[0]].reshape(N_CHIPS, DEPTH, 3, Fs)
    cws = [cw_all[:, l] for l in range(DEPTH)]
    Wup = {0: gw[('ffn_w_up', 0)]}
    Wdn = {0: gw[('ffn_w_down', 0)].reshape(2, Fs, D)}
    f0, hf0, af0 = _ffn_fwd(x1, Wup[0], Wdn[0], cws[0], cbs[0], "l0")
    x2, xh2, rs2 = _add_ln_fwd(x1, f0, ln_ffn_g[0], ln_ffn_b[0], name="ln_ffn0")

    gw.update(zip(grp_l1, finish_gather(ag_l1, src_l1, cols_l1, [x2], "l1")))
    Wodin = _cols_from_shards(gw[('od_w_in', 0)])
    Wodout = gw[('od_w_out', 0)].reshape(D, D)
    Wup[1] = gw[('ffn_w_up', 1)]
    Wdn[1] = gw[('ffn_w_down', 1)].reshape(2, Fs, D)
    QW, KW = SWA_HEADS * SWA_HEAD_DIM, SWA_KV_HEADS * SWA_HEAD_DIM
    P1 = _mm(x2, Wodin, 'nn', name="od_proj")
    qT = _to_heads(P1, tabs, col0=0, width=QW, rotate=True, name="rope_q", out_dtype=BF16)
    kT = _to_heads(P1, tabs, col0=QW, width=KW, rotate=True, name="rope_k", out_dtype=BF16)
    vT = _to_heads(P1, tabs, col0=QW + KW, width=KW, rotate=False, name="heads_v", out_dtype=BF16)
    sink_rows = jnp.broadcast_to(od_sinks[0].reshape(SWA_KV_HEADS, SWA_GROUPS, 1, 1),
                                 (SWA_KV_HEADS, SWA_GROUPS, SWA_WINDOW, 1)).reshape(SWA_KV_HEADS, -1, 1)
    oT, Lsw = _swa_fwd(qT, kT, vT, sink_rows, name="swa_fwd")
    o_sw = _from_heads(oT, tabs, rotate_back=False, name="heads_o", out_dtype=BF16)
    mix1 = _mm(o_sw, Wodout, 'nn', name="od_out")
    x3, xh3, rs3 = _add_ln_fwd(x2, mix1, ln_mix_g[1], ln_mix_b[1], name="ln_mix1")
    f1, hf1, af1 = _ffn_fwd(x3, Wup[1], Wdn[1], cws[1], cbs[1], "l1")
    _, xh4, rs4 = _add_ln_fwd(x3, f1, ln_ffn_g[1], ln_ffn_b[1], name="ln_ffn1")

    dz4, dg_ffn1, db_ffn1, loss_part = _loss_ln_bwd(tgt, xh4, rs4, ln_ffn_g[1], ln_ffn_b[1], name="loss_lnb_ffn1")
    dx3f, dWup1, dWdn1, dcw1, dcb1 = _ffn_bwd(dz4, x3, hf1, af1, Wup[1], Wdn[1], cws[1], "l1")
    sib_ffn1 = _sibling_halves_start([dWup1, dWdn1.reshape(N_CHIPS, Rd, D)], [False, False], name="rs_ffn1_sib_start")
    dz3, dg_mix1, db_mix1 = _ln_bwd(dz4, dx3f, xh3, rs3, ln_mix_g[1], name="lnb_mix1", after=[sib_ffn1[4]])
    do_sw = _mm(dz3, Wodout, 'nt', name="od_out_dx")
    dWodout = _mm(o_sw, dz3, 'tn', name="od_out_dw", out_dtype=BF16)
    doT = _to_heads(do_sw, tabs, col0=0, width=QW, rotate=False, name="heads_do", out_dtype=F32)
    dqT, dkT, dvT, dsink = _swa_bwd(qT, kT, vT, sink_rows, oT, Lsw, doT, name="swa_bwd")
    dq1 = _from_heads(dqT, tabs, rotate_back=True, name="rope_dq", out_dtype=BF16)
    dk1 = _from_heads(dkT, tabs, rotate_back=True, name="rope_dk", out_dtype=BF16, skip_rows=SWA_WINDOW)
    dv1 = _from_heads(dvT, tabs, rotate_back=False, name="heads_dv", out_dtype=BF16, skip_rows=SWA_WINDOW)
    dP1 = jnp.concatenate([dq1, dk1, dv1], axis=1)
    dx2m = _mm(dP1, Wodin, 'nt', name="od_proj_dx")
    dWodin = _mm(x2, dP1, 'tn', name="od_proj_dw", out_dtype=BF16)

    def rs_begin(entries, grads, tag):
        cols = [split_cols[e] for e in entries]
        sib = _sibling_send_halves(grads, cols, name=f"rs_{tag}_sibling")
        return [_sum2_halves(g4, s4, bc, name=f"rs_sum2_{n}{l}")
                for (n, l), g4, s4, bc in zip(entries, grads, sib, cols)]

    def rs_begin_started(entries, started, after, tag):
        send, rcv, thru, lands, _ = started
        thru, lands = _sibling_halves_wait(send, rcv, thru, lands, [False] * len(thru), after,
                                           name=f"rs_{tag}_sib_wait")
        return [_sum2_halves(g4, s4, False, name=f"rs_sum2_{n}{l}") for (n, l), g4, s4 in zip(entries, thru, lands)]

    def own_parts(parts):
        me = 2 * lax.axis_index("x") + lax.axis_index("y")
        return [lax.dynamic_slice_in_dim(p, me, 1, axis=0) for p in parts]

    part_l1 = (rs_begin(grp_l1[:2], [_shards_from_cols(dWodin), dWodout.reshape(N_CHIPS, D // N_CHIPS, D)], "od")
               + rs_begin_started(grp_l1[2:], sib_ffn1, [dWodin], "ffn1"))
    rs_l1 = _chip_exchange_start('scatter', part_l1, [False] * len(part_l1), name="rs_l1_start")

    dz2, dg_ffn0, db_ffn0 = _ln_bwd(dz3, dx2m, xh2, rs2, ln_ffn_g[0], name="lnb_ffn0", after=[rs_l1[4]])
    dx1f, dWup0, dWdn0, dcw0, dcb0 = _ffn_bwd(dz2, x1, hf0, af0, Wup[0], Wdn[0], cws[0], "l0")
    sib_ffn0 = _sibling_halves_start([dWup0, dWdn0.reshape(N_CHIPS, Rd, D)], [False, False], name="rs_ffn0_sib_start")
    dz1, dg_mix0, db_mix0 = _ln_bwd(dz2, dx1f, xh1, rs1, ln_mix_g[0], name="lnb_mix0", after=[sib_ffn0[4]])
    dcat = _mm(dz1, Wout_ev, 'nt', name="ev_out_dx")
    dWout_ev = _mm(cat, dz1, 'tn', name="ev_out_dw", out_dtype=BF16)
    part_ffn0 = rs_begin_started(grp_ffn0, sib_ffn0, [dWout_ev], "ffn0")
    rs_ffn0 = _chip_exchange_start('scatter', part_ffn0, [False] * len(part_ffn0), name="rs_ffn0_start")
    dz = _glu_bwd(z, dcat, name="s5_glu_bwd")
    dyg = _mm(dz, Wglu, 'nt', name="s5_glu_dx", after=[rs_ffn0[4]])
    dWglu = _mm(yg, dz, 'tn', name="s5_glu_dw", out_dtype=BF16)
    dy_s5, du_dir, dD = _s5_out_bwd(dyg, y_s5, P, dskip, name="s5_out_bwd")
    dhh = _mm(dy_s5, CC, 'nt', bmode='bo', tm=2048, tn=HT, tk=UT, diag='kn', name="s5_y_dx")
    dCC = _mm(hh, dy_s5, 'tn', bmode='ao', tm=HT, tn=UT, diag='mn', name="s5_y_dw")
    lam, da_s5 = _s5_scan_bwd(dhh, hh, a_cat, name="s5_scan_bwd")
    du = _mm(lam, BB, 'nt', bmode='abr', tm=2048, tn=UT, tk=HT, diag='kn', name="s5_bu_dx", plus=[(du_dir, 1.0)],
             out_dtype=BF16)
    dBB = _mm(u_s5, lam, 'tn', bmode='bo', tm=UT, tn=HT, diag='mn', name="s5_bu_dw")
    dq0, dk0, dv0, dccol, dcrow = _fox_bwd(P, ccol, crow, fox, lse, dcat, name="fox_bwd")
    dc = jnp.transpose((dccol.reshape(FOX_HEADS, S) - dcrow.reshape(FOX_HEADS, S)))
    dc = jnp.pad(dc, ((0, 0), (0, LANE - FOX_HEADS)))
    dfl, dbf = _gate_bwd(dc, sgate, name="fox_gate_bwd")
    dP = jnp.concatenate([dq0, dk0, dv0, du], axis=1)
    dx0b = _mm(dfl, WfT, 'nn', name="ev_proj_f_dx")
    grad_x = _mm(dP, WmainT, 'nn', name="ev_proj_dx", plus=[(dz1, ALPHA), (dx0b, 1.0)])
    dWmainT = _mm(dP, x0, 'tn', tm=1024, tn=1024, name="ev_proj_dw", out_dtype=BF16)
    dWfT = _mm(dfl, x0, 'tn', name="ev_proj_f_dw", out_dtype=BF16)

    dbbt = _diag_extract(dBB, Cg, Pn, name="s5_bb_diag")
    dcct = _diag_extract(dCC, Pn, Cg, name="s5_cc_diag")
    dbb_re = jnp.transpose(dbbt[0].reshape(G, Cg, Pn), (0, 2, 1)).reshape(G * Pn, Cg)
    dbb_im = jnp.transpose(dbbt[1].reshape(G, Cg, Pn), (0, 2, 1)).reshape(G * Pn, Cg)
    db_re, db_im, dg_re1, dg_im1 = _s5_bb_bwd(g_re1, g_im1, b_re2, b_im2, dbb_re, dbb_im, name="s5_bb_bwd")
    dlam_re, dlam_im, dlstep = _s5_disc_bwd(lam_r, lam_i, lstep, da_s5[0].reshape(G, Pn), da_s5[1].reshape(G, Pn),
                                            dg_re1.reshape(G, Pn), dg_im1.reshape(G, Pn), name="s5_disc_bwd")
    dc_re = jnp.transpose(dcct[0].reshape(G, Pn, Cg), (0, 2, 1))
    dc_im = -jnp.transpose(dcct[1].reshape(G, Pn, Cg), (0, 2, 1))

    def conv_w_full(d0, d1):
        return jnp.stack([jnp.reshape(jnp.transpose(d[:, :, :Fs], (1, 0, 2)), (3, N_CHIPS * Fs)) for d in (d0, d1)])

    def conv_b_full(d0, d1):
        return jnp.stack([jnp.reshape(d[:, 0, :Fs], (N_CHIPS * Fs,)) for d in (d0, d1)])

    small_local = dict(
        ev_b_f=dbf[:, :FOX_HEADS], ev_lambda_re=dlam_re, ev_lambda_im=dlam_im, ev_log_step=dlstep,
        ev_ssm_b_re=db_re, ev_ssm_b_im=db_im, ev_ssm_c_re=dc_re, ev_ssm_c_im=dc_im, ev_ssm_d=dD,
        od_sinks=dsink[:, :, 0],
        ln_mix_g=jnp.concatenate([dg_mix0, dg_mix1]), ln_mix_b=jnp.concatenate([db_mix0, db_mix1]),
        ffn_conv_w=conv_w_full(dcw0, dcw1), ffn_conv_b=conv_b_full(dcb0, dcb1),
        ln_ffn_g=jnp.concatenate([dg_ffn0, dg_ffn1]), ln_ffn_b=jnp.concatenate([db_ffn0, db_ffn1]))
    small = list(small_local.keys())
    out_g, out_d, out_m, out_v = {}, {}, {}, {}
    loss_out = []

    def small_update(after):
        red = _all_reduce_small(_pack([small_local[n] for n in small] + [loss_part]), name="ar_small", after=after)
        full_shapes = [W[n].shape if n != 'ffn_conv_w' else (DEPTH, 3, N_CHIPS * Fs) for n in small]
        pieces = _unpack(red, full_shapes + [()])
        loss_out.append(pieces[-1])
        gsmall = dict(zip(small, pieces[:-1]))
        chip = 2 * lax.axis_index("x") + lax.axis_index("y")
        gsmall['ffn_conv_w'] = lax.dynamic_slice_in_dim(gsmall['ffn_conv_w'], chip * Fs, Fs, axis=2)
        shapes = [W[n].shape for n in small]
        gs, ds_, ms, vs = _adamw(_pack([W[n] for n in small])[None], _pack([gsmall[n] for n in small])[None],
                                 _pack([Mo[n] for n in small])[None], _pack([Vo[n] for n in small])[None],
                                 name="adamw_small", tr=1 << 14)
        out_g.update(zip(small, _unpack(gs, shapes)))
        out_d.update(zip(small, _unpack(ds_, shapes)))
        out_m.update(zip(small, _unpack(ms, shapes)))
        out_v.update(zip(small, _unpack(vs, shapes)))
        return vs

    dw_in_t = jnp.concatenate([dWmainT[:qkv_w], dWfT[:FOX_HEADS], dWmainT[qkv_w:]], axis=0)
    part_now = rs_begin(grp_now, [dw_in_t.reshape(N_CHIPS, EIN // N_CHIPS, D), _shards_from_cols(dWglu),
                                  dWout_ev.reshape(N_CHIPS, D // N_CHIPS, D)], "l0")
    small_done = small_update([grad_x])
    rs_now = _chip_exchange_start('scatter', part_now, [False] * len(part_now), name="rs_l0_start",
                                  after=[small_done])

    def finish_scatter(started, parts, after, tag):
        send, rcv, thru, lands, _ = started
        thru, lands = _chip_exchange_wait('scatter', send, rcv, thru, lands, [False] * len(parts), after,
                                          name=f"rs_{tag}_wait")
        return _own_slot(lands, own_parts(thru))

    def update(entries, recv, tag):
        halves = [_rowsum(r, name=f"rs_sum4_{e[0]}{e[1]}") for e, r in zip(entries, recv)]
        send, rcv, thru, lands, tok = _sibling_swap_start(halves, name=f"rs_{tag}_join_start")
        own = dict(zip(entries, thru))
        params = list(dict.fromkeys(e[0] for e in entries))

        def half_update(n, grads, is_own, prev, after_name):
            return _adamw_half(Wv[n], [grads[(n, l)] for l in range(W[n].shape[0])], view(n, Mo[n]), view(n, Vo[n]),
                               name=f"adamw_{after_name}_{n}", own=is_own, prev=prev, by_cols=split_cols[(n, 0)])

        first = {n: half_update(n, own, True, None, "own") for n in params}
        _, others = _sibling_swap_wait(send, rcv, thru, lands, [first[n][3] for n in params] + [tok],
                                       name=f"rs_{tag}_join_wait")
        oth = dict(zip(entries, others))
        done = []
        for n in params:
            res = half_update(n, oth, False, first[n], "sib")
            out_g[n], out_d[n], out_m[n], out_v[n] = (view(n, t) for t in res)
            done.append(res[3])
        return done

    recv_rest = (finish_scatter(rs_l1, part_l1, [rs_now[4]], "l1")
                 + finish_scatter(rs_ffn0, part_ffn0, [rs_now[4]], "ffn0"))
    done = update(grp_l1 + grp_ffn0, recv_rest, "rest")
    update(grp_now, finish_scatter(rs_now, part_now, done, "l0"), "l0")
    loss = loss_out[0]

    return (loss, grad_x.reshape(1, S, D), *[out_g[n] for n in names], *[out_d[n] for n in names],
            *[out_m[n] for n in names], *[out_v[n] for n in names])
```

```python
import math

import numpy as np
import jax
import jax.numpy as jnp
from jax import lax
from jax.experimental import pallas as pl
from jax.experimental.pallas import tpu as pltpu

F32 = jnp.float32
BF16 = jnp.bfloat16
MESH = pl.DeviceIdType.MESH
ANY = pl.BlockSpec(memory_space=pl.ANY)

D_MODEL = 2048
FOX_HEADS = 8
FOX_HEAD_DIM = 128
FOX_WIDTH = 1024
SSM_WIDTH = 1024
SSM_GROUP = 16
SSM_GROUPS = 64
SSM_STATE = 64
SWA_HEADS = 32
SWA_KV_HEADS = 4
SWA_HEAD_DIM = 64
SWA_GROUPS = 8
SWA_WINDOW = 128
ROPE_DIM = 16
ROPE_THETA = 500000.0
LN_EPS = 1e-5
DEPTH = 2
ALPHA = (2.0 * DEPTH) ** 0.25
ADAM_LR = 0.001
ADAM_B1 = 0.9
ADAM_B2 = 0.999
ADAM_EPS = 1e-08
ADAM_WD = 0.01
ADAM_STEP = 10
N_CHIPS = 4

VMEM_LIMIT = 56 * 1024 * 1024
LANE = 128


def _call(body, after=(), **kw):
    if after:
        n = len(after)

        def shifted(*refs):
            return body(*refs[n:])

        call = _call(shifted, **dict(kw, in_specs=[ANY] * n + list(kw["in_specs"])))
        return lambda *args: call(*after, *args)
    return pl.pallas_call(body, **kw)


def _cparams(sem):
    return pltpu.CompilerParams(dimension_semantics=sem, vmem_limit_bytes=VMEM_LIMIT)


def _rup(n, m):
    return (n + m - 1) // m * m


def _pick(n, pref):
    if n <= pref:
        return n
    for step in (128, 16, 8):
        for t in range(pref - pref % step, 0, -step):
            if n % t == 0:
                return t
    return n


def _tile2d(rows, cols, pref_rows=256, budget=256 * 1024):
    tr = _pick(rows, pref_rows)
    if tr < 64:
        tr = rows
    if cols % LANE:
        return tr, cols
    return tr, _pick(cols, max(LANE, budget // tr // LANE * LANE))


def _mm(a, b, mode, *, name, tm=512, tn=1024, tk=2048, bmode=None, out_dtype=F32, after=(), b_map=None,
        o_map=None, diag=None, plus=()):
    a3 = a if a.ndim == 3 else a[None]
    b3 = b if b.ndim == 3 else b[None]
    if mode == 'tn':
        K, M = a3.shape[1:]
    else:
        M, K = a3.shape[1:]
    N = b3.shape[1] if mode == 'nt' else b3.shape[2]
    tm, tn, tk = _pick(M, tm), _pick(N, tn), _pick(K, tk)
    nb = max(a3.shape[0], b3.shape[0])
    nbo, nbr = (1, nb) if bmode == 'abr' else (nb, 1)
    nm, nk = M // tm, K // tk
    if diag == 'kn':
        assert K // tk == N // tn
        nk = 1
    if diag == 'mn':
        assert M // tm == N // tn
        nm = 1
    nred = nbr * nk
    a_b = bmode in ('ao', 'abr')
    b_b = bmode in ('bo', 'abr')
    o_b = bmode in ('bo', 'ao')

    def bsel(flag, bo, br, remap=None):
        if not flag:
            return 0
        return (bo + br) if remap is None else remap(bo + br)

    def mi(i, j):
        return j if diag == 'mn' else i

    def ki(j, k):
        return j if diag == 'kn' else k

    if mode == 'tn':
        a_spec = pl.BlockSpec((None, tk, tm), lambda bo, i, j, br, k: (bsel(a_b, bo, br), ki(j, k), mi(i, j)))
    else:
        a_spec = pl.BlockSpec((None, tm, tk), lambda bo, i, j, br, k: (bsel(a_b, bo, br), mi(i, j), ki(j, k)))
    if mode == 'nt':
        b_spec = pl.BlockSpec((None, tn, tk), lambda bo, i, j, br, k: (bsel(b_b, bo, br, b_map), j, ki(j, k)))
    else:
        b_spec = pl.BlockSpec((None, tk, tn), lambda bo, i, j, br, k: (bsel(b_b, bo, br, b_map), ki(j, k), j))
    o_spec = pl.BlockSpec((None, tm, tn), lambda bo, i, j, br, k: (bsel(o_b, bo, br, o_map), mi(i, j), j))
    dn = {'nn': (((1,), (0,)), ((), ())), 'nt': (((1,), (1,)), ((), ())), 'tn': (((0,), (0,)), ((), ()))}[mode]

    na = len(plus)

    def body(a_ref, b_ref, *rest):
        plus_refs = rest[:na]
        o_ref, scratch = rest[na + len(after)], rest[na + len(after) + 1:]
        r = lax.dot_general(a_ref[...].astype(BF16), b_ref[...].astype(BF16), dn, preferred_element_type=F32)

        def finish(total):
            for (_, scale), p_ref in zip(plus, plus_refs):
                total = total + scale * p_ref[...].astype(F32)
            o_ref[...] = total.astype(out_dtype)

        if nred == 1:
            finish(r)
        else:
            acc = scratch[0]
            step = pl.program_id(3) * nk + pl.program_id(4)

            @pl.when(step == 0)
            def _():
                acc[...] = r

            @pl.when(step > 0)
            def _():
                acc[...] += r

            @pl.when(step == nred - 1)
            def _():
                finish(acc[...])

    out = _call(
        body, name=name,
        grid=(nbo, nm, N // tn, nbr, nk),
        in_specs=[a_spec, b_spec] + [o_spec] * na + [ANY] * len(after), out_specs=o_spec,
        out_shape=jax.ShapeDtypeStruct((nbo if o_b else 1, M, N), out_dtype),
        scratch_shapes=[] if nred == 1 else [pltpu.VMEM((tm, tn), F32)],
        compiler_params=_cparams(("parallel", "parallel", "parallel", "arbitrary", "arbitrary")),
    )(a3, b3, *[p if p.ndim == 3 else p[None] for p, _ in plus], *after)
    return out if o_b else out[0]


def _add_ln_fwd(x, r, g, b, *, name):
    S, D = x.shape
    tr = _pick(S, 256)

    def body(x_ref, r_ref, g_ref, b_ref, o_ref, xh_ref, rs_ref):
        z = ALPHA * x_ref[...] + r_ref[...]
        mu = jnp.mean(z, axis=-1, keepdims=True)
        zc = z - mu
        var = jnp.mean(zc * zc, axis=-1, keepdims=True)
        rstd = lax.rsqrt(var + LN_EPS)
        xh = zc * rstd
        xh_ref[...] = xh
        rs_ref[...] = rstd
        o_ref[...] = xh * g_ref[...] + b_ref[...]

    row = pl.BlockSpec((tr, D), lambda i: (i, 0))
    vec = pl.BlockSpec((1, D), lambda i: (0, 0))
    return _call(
        body, name=name, grid=(S // tr,),
        in_specs=[row, row, vec, vec],
        out_specs=[row, row, pl.BlockSpec((tr, 1), lambda i: (i, 0))],
        out_shape=[jax.ShapeDtypeStruct((S, D), F32), jax.ShapeDtypeStruct((S, D), F32),
                   jax.ShapeDtypeStruct((S, 1), F32)],
        compiler_params=_cparams(("parallel",)),
    )(x, r, g.reshape(1, D), b.reshape(1, D))


def _ln_bwd(da, db, xhat, rstd, g, *, name, after=()):
    S, D = xhat.shape
    tr = _pick(S, 256)

    def body(*refs):
        da_ref, db_ref, xh_ref, rs_ref, g_ref, dz_ref, dg_ref, dbt_ref = refs[len(after):]
        dy = ALPHA * da_ref[...] + db_ref[...]
        xh = xh_ref[...]
        dxh = dy * g_ref[...]
        m1 = jnp.mean(dxh, axis=-1, keepdims=True)
        m2 = jnp.mean(dxh * xh, axis=-1, keepdims=True)
        dz_ref[...] = rs_ref[...] * (dxh - m1 - xh * m2)
        pg = jnp.sum(dy * xh, axis=0, keepdims=True)
        pb = jnp.sum(dy, axis=0, keepdims=True)

        @pl.when(pl.program_id(0) == 0)
        def _():
            dg_ref[...] = pg
            dbt_ref[...] = pb

        @pl.when(pl.program_id(0) > 0)
        def _():
            dg_ref[...] += pg
            dbt_ref[...] += pb

    row = pl.BlockSpec((tr, D), lambda i: (i, 0))
    vec = pl.BlockSpec((1, D), lambda i: (0, 0))
    ins = list(after) + [da, db, xhat, rstd, g.reshape(1, D)]
    in_specs = [ANY] * len(after) + [row, row, row, pl.BlockSpec((tr, 1), lambda i: (i, 0)), vec]
    return _call(
        body, name=name, grid=(S // tr,),
        in_specs=in_specs, out_specs=[row, vec, vec],
        out_shape=[jax.ShapeDtypeStruct((S, D), F32), jax.ShapeDtypeStruct((1, D), F32),
                   jax.ShapeDtypeStruct((1, D), F32)],
        compiler_params=_cparams(("arbitrary",)),
    )(*ins)


def _loss_ln_bwd(t, xhat, rstd, g, b, *, name):
    S, D = xhat.shape
    tr = _pick(S, 256)

    def body(t_ref, xh_ref, rs_ref, g_ref, b_ref, dz_ref, dg_ref, dbt_ref, l_ref):
        xh = xh_ref[...]
        e = xh * g_ref[...] + b_ref[...] - t_ref[...]
        dy = e * (1.0 / D)
        part = 0.5 * jnp.sum(jnp.sum(e * e, axis=-1, keepdims=True) * (1.0 / D), axis=0, keepdims=True)
        dxh = dy * g_ref[...]
        m1 = jnp.mean(dxh, axis=-1, keepdims=True)
        m2 = jnp.mean(dxh * xh, axis=-1, keepdims=True)
        dz_ref[...] = rs_ref[...] * (dxh - m1 - xh * m2)
        pg = jnp.sum(dy * xh, axis=0, keepdims=True)
        pb = jnp.sum(dy, axis=0, keepdims=True)

        @pl.when(pl.program_id(0) == 0)
        def _():
            dg_ref[...] = pg
            dbt_ref[...] = pb
            l_ref[...] = part

        @pl.when(pl.program_id(0) > 0)
        def _():
            dg_ref[...] += pg
            dbt_ref[...] += pb
            l_ref[...] += part

    row = pl.BlockSpec((tr, D), lambda i: (i, 0))
    vec = pl.BlockSpec((1, D), lambda i: (0, 0))
    return _call(
        body, name=name, grid=(S // tr,),
        in_specs=[row, row, pl.BlockSpec((tr, 1), lambda i: (i, 0)), vec, vec],
        out_specs=[row, vec, vec, pl.BlockSpec((1, 1), lambda i: (0, 0))],
        out_shape=[jax.ShapeDtypeStruct((S, D), F32), jax.ShapeDtypeStruct((1, D), F32),
                   jax.ShapeDtypeStruct((1, D), F32), jax.ShapeDtypeStruct((1, 1), F32)],
        compiler_params=_cparams(("arbitrary",)),
    )(t, xhat, rstd, g.reshape(1, D), b.reshape(1, D))


def _split3(x):
    h = x.astype(BF16)
    r = x - h.astype(F32)
    m = r.astype(BF16)
    l = (r - m.astype(F32)).astype(BF16)
    return h, m, l


def _tri_matmul(tri_bf, x):
    h, m, l = _split3(x)
    dn = (((1,), (0,)), ((), ()))
    return (lax.dot_general(tri_bf, l, dn, preferred_element_type=F32)
            + lax.dot_general(tri_bf, m, dn, preferred_element_type=F32)
            + lax.dot_general(tri_bf, h, dn, preferred_element_type=F32))


def _gate_fwd(fl, bf, *, name):
    S = fl.shape[0]
    tc = _pick(S, 256)
    nchunk = S // tc

    def body(fl_ref, bf_ref, c_ref, sg_ref):
        r = lax.broadcasted_iota(jnp.int32, (tc, tc), 0)
        cidx = lax.broadcasted_iota(jnp.int32, (tc, tc), 1)
        tri = (r >= cidx).astype(BF16)
        carry = jnp.zeros((1, LANE), F32)
        for ch in range(nchunk):
            x = fl_ref[pl.ds(ch * tc, tc), :] + bf_ref[...]
            lf = jnp.minimum(x, 0.0) - jnp.log(1.0 + jnp.exp(-jnp.abs(x)))
            sg_ref[pl.ds(ch * tc, tc), :] = jax.nn.sigmoid(-x)
            c_ref[pl.ds(ch * tc, tc), :] = _tri_matmul(tri, lf) + carry
            carry = carry + jnp.sum(lf, axis=0, keepdims=True)

    full = pl.BlockSpec((S, LANE), lambda: (0, 0))
    return _call(
        body, name=name, in_specs=[full, pl.BlockSpec((1, LANE), lambda: (0, 0))], out_specs=[full, full],
        out_shape=[jax.ShapeDtypeStruct((S, LANE), F32)] * 2,
        compiler_params=pltpu.CompilerParams(vmem_limit_bytes=VMEM_LIMIT),
    )(fl, bf)


def _gate_bwd(dc, sg, *, name):
    S = dc.shape[0]
    tc = _pick(S, 256)
    nchunk = S // tc

    def body(dc_ref, sg_ref, dfl_ref, db_ref):
        r = lax.broadcasted_iota(jnp.int32, (tc, tc), 0)
        cidx = lax.broadcasted_iota(jnp.int32, (tc, tc), 1)
        tri = (r <= cidx).astype(BF16)
        carry = jnp.zeros((1, LANE), F32)
        dbacc = jnp.zeros((1, LANE), F32)
        for ch in reversed(range(nchunk)):
            d = dc_ref[pl.ds(ch * tc, tc), :]
            dfl = (_tri_matmul(tri, d) + carry) * sg_ref[pl.ds(ch * tc, tc), :]
            dfl_ref[pl.ds(ch * tc, tc), :] = dfl
            dbacc = dbacc + jnp.sum(dfl, axis=0, keepdims=True)
            carry = carry + jnp.sum(d, axis=0, keepdims=True)
        db_ref[...] = dbacc

    full = pl.BlockSpec((S, LANE), lambda: (0, 0))
    return _call(
        body, name=name, in_specs=[full, full], out_specs=[full, pl.BlockSpec((1, LANE), lambda: (0, 0))],
        out_shape=[jax.ShapeDtypeStruct((S, LANE), F32), jax.ShapeDtypeStruct((1, LANE), F32)],
        compiler_params=pltpu.CompilerParams(vmem_limit_bytes=VMEM_LIMIT),
    )(dc, sg)


def _fox_scores(q_ref, k_ref, cc_ref, cr_ref, qi, tq, S):
    scale = 1.0 / math.sqrt(FOX_HEAD_DIM)
    s = lax.dot_general(q_ref[...].astype(BF16), k_ref[...].astype(BF16), (((1,), (1,)), ((), ())),
                        preferred_element_type=F32) * scale
    s = s + cc_ref[...] - cr_ref[...]
    row = lax.broadcasted_iota(jnp.int32, (tq, S), 0) + qi * tq
    col = lax.broadcasted_iota(jnp.int32, (tq, S), 1)
    return s, row >= col


def _fox_fwd(P, ccol, crow, *, name):
    S = P.shape[0]
    tq = _pick(S, 256)
    H = FOX_HEADS

    def body(q_ref, k_ref, v_ref, cc_ref, cr_ref, o_ref, l_ref):
        s, causal = _fox_scores(q_ref, k_ref, cc_ref, cr_ref, pl.program_id(1), tq, S)
        s = jnp.where(causal, s, -1e30)
        m = jnp.max(s, axis=-1, keepdims=True)
        e = jnp.exp(s - m)
        den = jnp.sum(e, axis=-1, keepdims=True)
        p = e / den
        o_ref[...] = jnp.dot(p.astype(BF16), v_ref[...].astype(BF16), preferred_element_type=F32)
        l_ref[...] = m + jnp.log(den)

    return _call(
        body, name=name, grid=(H, S // tq),
        in_specs=[pl.BlockSpec((tq, 128), lambda h, i: (i, h)),
                  pl.BlockSpec((S, 128), lambda h, i: (0, H + h)),
                  pl.BlockSpec((S, 128), lambda h, i: (0, 2 * H + h)),
                  pl.BlockSpec((None, tq, 1), lambda h, i: (h, i, 0)),
                  pl.BlockSpec((None, 1, S), lambda h, i: (h, 0, 0))],
        out_specs=[pl.BlockSpec((tq, 128), lambda h, i: (i, h)),
                   pl.BlockSpec((None, tq, 1), lambda h, i: (h, i, 0))],
        out_shape=[jax.ShapeDtypeStruct((S, FOX_WIDTH), F32), jax.ShapeDtypeStruct((H, S, 1), F32)],
        compiler_params=_cparams(("parallel", "parallel")),
    )(P, P, P, ccol, crow)


def _fox_bwd(P, ccol, crow, o, lse, dcat, *, name):
    S = P.shape[0]
    tq = _pick(S, 512)
    H = FOX_HEADS
    nq = S // tq
    scale = 1.0 / math.sqrt(FOX_HEAD_DIM)

    def body(q_ref, k_ref, v_ref, cc_ref, cr_ref, o_ref, l_ref, do_ref,
             dq_ref, dk_ref, dv_ref, dcc_ref, dcr_ref, dk_acc, dv_acc):
        qi = pl.program_id(1)
        s, causal = _fox_scores(q_ref, k_ref, cc_ref, cr_ref, qi, tq, S)
        p = jnp.where(causal, jnp.exp(s - l_ref[...]), 0.0)
        do = do_ref[...]
        do_bf = do.astype(BF16)
        dp = lax.dot_general(do_bf, v_ref[...].astype(BF16), (((1,), (1,)), ((), ())), preferred_element_type=F32)
        delta = jnp.sum(do * o_ref[...], axis=-1, keepdims=True)
        ds = p * (dp - delta)
        ds_bf = ds.astype(BF16)
        dq_ref[...] = (jnp.dot(ds_bf, k_ref[...].astype(BF16), preferred_element_type=F32) * scale).astype(BF16)
        dkp = lax.dot_general(ds_bf, q_ref[...].astype(BF16), (((0,), (0,)), ((), ())),
                              preferred_element_type=F32) * scale
        dvp = lax.dot_general(p.astype(BF16), do_bf, (((0,), (0,)), ((), ())), preferred_element_type=F32)
        dcc_ref[...] = jnp.sum(ds, axis=-1, keepdims=True)
        dcr = jnp.sum(ds, axis=0, keepdims=True)

        @pl.when(qi == 0)
        def _():
            dk_acc[...] = dkp
            dv_acc[...] = dvp
            dcr_ref[...] = dcr

        @pl.when(qi > 0)
        def _():
            dk_acc[...] += dkp
            dv_acc[...] += dvp
            dcr_ref[...] += dcr

        @pl.when(qi == nq - 1)
        def _():
            dk_ref[...] = dk_acc[...].astype(BF16)
            dv_ref[...] = dv_acc[...].astype(BF16)

    qblk = pl.BlockSpec((tq, 128), lambda h, i: (i, h))
    kvo = pl.BlockSpec((S, 128), lambda h, i: (0, h))
    col = pl.BlockSpec((None, tq, 1), lambda h, i: (h, i, 0))
    rowv = pl.BlockSpec((None, 1, S), lambda h, i: (h, 0, 0))
    return _call(
        body, name=name, grid=(H, nq),
        in_specs=[qblk,
                  pl.BlockSpec((S, 128), lambda h, i: (0, H + h)),
                  pl.BlockSpec((S, 128), lambda h, i: (0, 2 * H + h)),
                  col, rowv, qblk, col, qblk],
        out_specs=[qblk, kvo, kvo, col, rowv],
        out_shape=[jax.ShapeDtypeStruct((S, FOX_WIDTH), BF16)] * 3
        + [jax.ShapeDtypeStruct((H, S, 1), F32), jax.ShapeDtypeStruct((H, 1, S), F32)],
        scratch_shapes=[pltpu.VMEM((S, 128), F32), pltpu.VMEM((S, 128), F32)],
        compiler_params=_cparams(("parallel", "arbitrary")),
    )(P, P, P, ccol, crow, o, lse, dcat)


def _s5_disc_fwd(lr, li, ls, *, name, after=()):
    G, Pn = lr.shape

    def body(lr_ref, li_ref, ls_ref, ar_ref, ai_ref, gr_ref, gi_ref):
        lr_, li_ = lr_ref[...], li_ref[...]
        dt = jnp.exp(ls_ref[...])
        mag = jnp.exp(lr_ * dt)
        th = li_ * dt
        ar = mag * jnp.cos(th)
        ai = mag * jnp.sin(th)
        den = lr_ * lr_ + li_ * li_
        xr = ar - 1.0
        ar_ref[...] = ar
        ai_ref[...] = ai
        gr_ref[...] = (xr * lr_ + ai * li_) / den
        gi_ref[...] = (ai * lr_ - xr * li_) / den

    sq = pl.BlockSpec((G, Pn), lambda: (0, 0))
    return _call(
        body, after=after, name=name, in_specs=[sq, sq, pl.BlockSpec((G, 1), lambda: (0, 0))], out_specs=[sq] * 4,
        out_shape=[jax.ShapeDtypeStruct((G, Pn), F32)] * 4,
    )(lr, li, ls)


def _s5_disc_bwd(lr, li, ls, dar, dai, dgr, dgi, *, name):
    G, Pn = lr.shape

    def body(lr_ref, li_ref, ls_ref, dar_ref, dai_ref, dgr_ref, dgi_ref, dlr_ref, dli_ref, dls_ref):
        lr_, li_ = lr_ref[...], li_ref[...]
        dt = jnp.exp(ls_ref[...])
        mag = jnp.exp(lr_ * dt)
        th = li_ * dt
        ar = mag * jnp.cos(th)
        ai = mag * jnp.sin(th)
        den = lr_ * lr_ + li_ * li_
        xr = ar - 1.0
        xi = ai
        g_re = (xr * lr_ + xi * li_) / den
        g_im = (xi * lr_ - xr * li_) / den
        dgr_, dgi_ = dgr_ref[...], dgi_ref[...]
        dxr = (dgr_ * lr_ - dgi_ * li_) / den
        dxi = (dgr_ * li_ + dgi_ * lr_) / den
        dden = -(dgr_ * g_re + dgi_ * g_im) / den
        dlr = (dgr_ * xr + dgi_ * xi) / den + 2.0 * dden * lr_
        dli = (dgr_ * xi - dgi_ * xr) / den + 2.0 * dden * li_
        da_r = dar_ref[...] + dxr
        da_i = dai_ref[...] + dxi
        dmag_mag = da_r * ar + da_i * ai
        dth = da_i * ar - da_r * ai
        dlr_ref[...] = dlr + dmag_mag * dt
        dli_ref[...] = dli + dth * dt
        ddt = jnp.sum(dmag_mag * lr_ + dth * li_, axis=-1, keepdims=True)
        dls_ref[...] = ddt * dt

    sq = pl.BlockSpec((G, Pn), lambda: (0, 0))
    c1 = pl.BlockSpec((G, 1), lambda: (0, 0))
    return _call(
        body, name=name, in_specs=[sq, sq, c1, sq, sq, sq, sq], out_specs=[sq, sq, c1],
        out_shape=[jax.ShapeDtypeStruct((G, Pn), F32)] * 2 + [jax.ShapeDtypeStruct((G, 1), F32)],
    )(lr, li, ls, dar, dai, dgr, dgi)


def _s5_bb_fwd(gr, gi, br, bi, *, name):
    R, C = br.shape

    def body(gr_ref, gi_ref, br_ref, bi_ref, or_ref, oi_ref):
        g_r, g_i, b_r, b_i = gr_ref[...], gi_ref[...], br_ref[...], bi_ref[...]
        or_ref[...] = g_r * b_r - g_i * b_i
        oi_ref[...] = g_r * b_i + g_i * b_r

    w = pl.BlockSpec((R, C), lambda: (0, 0))
    c1 = pl.BlockSpec((R, 1), lambda: (0, 0))
    return _call(body, name=name, in_specs=[c1, c1, w, w], out_specs=[w, w],
                 out_shape=[jax.ShapeDtypeStruct((R, C), F32)] * 2)(gr, gi, br, bi)


def _s5_bb_bwd(gr, gi, br, bi, dbbr, dbbi, *, name):
    R, C = br.shape

    def body(gr_ref, gi_ref, br_ref, bi_ref, dr_ref, di_ref, dbr_ref, dbi_ref, dgr_ref, dgi_ref):
        g_r, g_i, b_r, b_i = gr_ref[...], gi_ref[...], br_ref[...], bi_ref[...]
        d_r, d_i = dr_ref[...], di_ref[...]
        dbr_ref[...] = g_r * d_r + g_i * d_i
        dbi_ref[...] = g_r * d_i - g_i * d_r
        dgr_ref[...] = jnp.sum(d_r * b_r + d_i * b_i, axis=-1, keepdims=True)
        dgi_ref[...] = jnp.sum(d_i * b_r - d_r * b_i, axis=-1, keepdims=True)

    w = pl.BlockSpec((R, C), lambda: (0, 0))
    c1 = pl.BlockSpec((R, 1), lambda: (0, 0))
    return _call(body, name=name, in_specs=[c1, c1, w, w, w, w], out_specs=[w, w, c1, c1],
                 out_shape=[jax.ShapeDtypeStruct((R, C), F32)] * 2 + [jax.ShapeDtypeStruct((R, 1), F32)] * 2,
                 )(gr, gi, br, bi, dbbr, dbbi)


_DIAG_TILE = 8


def _diag_mask(gr, gc):
    rows, cols = _DIAG_TILE * gr, _DIAG_TILE * gc
    r = lax.broadcasted_iota(jnp.int32, (rows, cols), 0) >> (gr.bit_length() - 1)
    c = lax.broadcasted_iota(jnp.int32, (rows, cols), 1) >> (gc.bit_length() - 1)
    return r == c


def _diag_expand(t2, gr, gc, *, name, after=()):
    _, R, _ = t2.shape
    G = R // gr
    nt = G // _DIAG_TILE
    rows, cols = _DIAG_TILE * gr, _DIAG_TILE * gc

    def body(t_ref, o_ref):
        src = lax.broadcasted_iota(jnp.int32, (gc, cols), 0)
        dst = lax.broadcasted_iota(jnp.int32, (gc, cols), 1) & (gc - 1)
        spread = (src == dst).astype(BF16)
        y = jnp.dot(t_ref[...].astype(BF16), spread, preferred_element_type=F32)
        o_ref[...] = jnp.where(_diag_mask(gr, gc), y, 0.0).astype(BF16)

    return _call(
        body, after=after, name=name, grid=(2, nt),
        in_specs=[pl.BlockSpec((None, rows, gc), lambda p, i: (p, i, 0))],
        out_specs=pl.BlockSpec((None, rows, cols), lambda p, i: (p, i, i)),
        out_shape=jax.ShapeDtypeStruct((2, R, G * gc), BF16),
        compiler_params=_cparams(("parallel",) * 2),
    )(t2)


def _diag_extract(xd, gr, gc, *, name):
    _, R, _ = xd.shape
    nt = R // gr // _DIAG_TILE
    rows, cols = _DIAG_TILE * gr, _DIAG_TILE * gc

    def body(x_ref, o_ref):
        src = lax.broadcasted_iota(jnp.int32, (cols, gc), 0) & (gc - 1)
        dst = lax.broadcasted_iota(jnp.int32, (cols, gc), 1)
        fold = (src == dst).astype(BF16)
        parts = _split3(jnp.where(_diag_mask(gr, gc), x_ref[...], 0.0))
        acc = jnp.dot(parts[2], fold, preferred_element_type=F32)
        acc = acc + jnp.dot(parts[1], fold, preferred_element_type=F32)
        o_ref[...] = acc + jnp.dot(parts[0], fold, preferred_element_type=F32)

    return _call(
        body, name=name, grid=(2, nt),
        in_specs=[pl.BlockSpec((None, rows, cols), lambda p, i: (p, i, i))],
        out_specs=pl.BlockSpec((None, rows, gc), lambda p, i: (p, i, 0)),
        out_shape=jax.ShapeDtypeStruct((2, R, gc), F32),
        compiler_params=_cparams(("parallel",) * 2),
    )(xd)


SCAN_BLOCK = 8


def _cpowers(ar, ai, sign):
    ai = sign * ai
    out = [(ar, ai)]
    for _ in range(SCAN_BLOCK - 1):
        pr, pi = out[-1]
        out.append((pr * ar - pi * ai, pr * ai + pi * ar))
    return out


def _row_table(pw, row, index_of_row):
    tr_ = jnp.broadcast_to(pw[index_of_row(0)][0], row.shape)
    ti_ = jnp.broadcast_to(pw[index_of_row(0)][1], row.shape)
    for r in range(1, SCAN_BLOCK):
        pr, pi = pw[index_of_row(r)]
        tr_ = jnp.where(row == r, pr, tr_)
        ti_ = jnp.where(row == r, pi, ti_)
    return tr_, ti_


def _s5_scan_fwd(bu, a, *, name):
    _, S, N = bu.shape
    tc = 512
    nt = N // tc

    def body(a_ref, b_ref, h_ref):
        pw = _cpowers(a_ref[0], a_ref[1], 1.0)
        row = lax.broadcasted_iota(jnp.int32, (SCAN_BLOCK, tc), 0)
        lead_r, lead_i = _row_table(pw, row, lambda r: r)
        mult = {sh: (jnp.where(row >= sh, pw[sh - 1][0], 0.0), jnp.where(row >= sh, pw[sh - 1][1], 0.0))
                for sh in (1, 2, 4)}

        def step(k, carry):
            cr, ci = carry
            rows = pl.ds(pl.multiple_of(k * SCAN_BLOCK, SCAN_BLOCK), SCAN_BLOCK)
            xr, xi = b_ref[0, rows, :], b_ref[1, rows, :]
            for sh in (1, 2, 4):
                sr, si = pltpu.roll(xr, sh, 0), pltpu.roll(xi, sh, 0)
                kr, ki = mult[sh]
                xr, xi = xr + kr * sr - ki * si, xi + kr * si + ki * sr
            h_ref[0, rows, :] = xr + lead_r * cr - lead_i * ci
            h_ref[1, rows, :] = xi + lead_r * ci + lead_i * cr
            last = row == SCAN_BLOCK - 1
            tr_ = jnp.sum(jnp.where(last, xr, 0.0), axis=0, keepdims=True)
            ti_ = jnp.sum(jnp.where(last, xi, 0.0), axis=0, keepdims=True)
            a8r, a8i = pw[SCAN_BLOCK - 1]
            return a8r * cr - a8i * ci + tr_, a8r * ci + a8i * cr + ti_

        z = jnp.zeros((1, tc), F32)
        lax.fori_loop(0, S // SCAN_BLOCK, step, (z, z), unroll=2)

    vec = pl.BlockSpec((2, 1, tc), lambda j: (0, 0, j))
    mat = pl.BlockSpec((2, S, tc), lambda j: (0, 0, j))
    return _call(
        body, name=name, grid=(nt,), in_specs=[vec, mat], out_specs=mat,
        out_shape=jax.ShapeDtypeStruct((2, S, N), F32),
        compiler_params=_cparams(("parallel",)),
    )(a, bu)


def _s5_scan_bwd(g, h, a, *, name):
    _, S, N = g.shape
    tc = 256
    nt = N // tc

    def body(a_ref, g_ref, h_ref, l_ref, da_ref):
        pw = _cpowers(a_ref[0], a_ref[1], -1.0)
        row = lax.broadcasted_iota(jnp.int32, (SCAN_BLOCK, tc), 0)
        tail_r, tail_i = _row_table(pw, row, lambda r: SCAN_BLOCK - 1 - r)
        nb = S // SCAN_BLOCK
        mult = {sh: (jnp.where(row < SCAN_BLOCK - sh, pw[sh - 1][0], 0.0),
                     jnp.where(row < SCAN_BLOCK - sh, pw[sh - 1][1], 0.0)) for sh in (1, 2, 4)}

        def step(i, carry):
            k = nb - 1 - i
            cr, ci, dar, dai = carry
            rows = pl.ds(pl.multiple_of(k * SCAN_BLOCK, SCAN_BLOCK), SCAN_BLOCK)
            xr, xi = g_ref[0, rows, :], g_ref[1, rows, :]
            for sh in (1, 2, 4):
                sr, si = pltpu.roll(xr, SCAN_BLOCK - sh, 0), pltpu.roll(xi, SCAN_BLOCK - sh, 0)
                kr, ki = mult[sh]
                xr, xi = xr + kr * sr - ki * si, xi + kr * si + ki * sr
            lr = xr + tail_r * cr - tail_i * ci
            li = xi + tail_r * ci + tail_i * cr
            l_ref[0, rows, :] = lr
            l_ref[1, rows, :] = li
            prev = pl.ds(pl.multiple_of(jnp.maximum(k - 1, 0) * SCAN_BLOCK, SCAN_BLOCK), SCAN_BLOCK)
            has_prev = jnp.where(k > 0, 1.0, 0.0).astype(F32)
            first = row == 0
            hpr = jnp.where(first, pltpu.roll(h_ref[0, prev, :], 1, 0) * has_prev, pltpu.roll(h_ref[0, rows, :], 1, 0))
            hpi = jnp.where(first, pltpu.roll(h_ref[1, prev, :], 1, 0) * has_prev, pltpu.roll(h_ref[1, rows, :], 1, 0))
            tr_ = jnp.sum(jnp.where(first, xr, 0.0), axis=0, keepdims=True)
            ti_ = jnp.sum(jnp.where(first, xi, 0.0), axis=0, keepdims=True)
            a8r, a8i = pw[SCAN_BLOCK - 1]
            return (a8r * cr - a8i * ci + tr_, a8r * ci + a8i * cr + ti_,
                    dar + lr * hpr + li * hpi, dai + li * hpr - lr * hpi)

        z = jnp.zeros((1, tc), F32)
        z8 = jnp.zeros((SCAN_BLOCK, tc), F32)
        _, _, dar, dai = lax.fori_loop(0, nb, step, (z, z, z8, z8), unroll=2)
        da_ref[0] = jnp.sum(dar, axis=0, keepdims=True)
        da_ref[1] = jnp.sum(dai, axis=0, keepdims=True)

    vec = pl.BlockSpec((2, 1, tc), lambda j: (0, 0, j))
    mat = pl.BlockSpec((2, S, tc), lambda j: (0, 0, j))
    return _call(
        body, name=name, grid=(nt,), in_specs=[vec, mat, mat], out_specs=[mat, vec],
        out_shape=[jax.ShapeDtypeStruct((2, S, N), F32), jax.ShapeDtypeStruct((2, 1, N), F32)],
        compiler_params=_cparams(("parallel",)),
    )(a, g, h)


_GELU_C = math.sqrt(2.0 / math.pi)


def _s5_out_fwd(yc, P, dskip, *, name):
    S, W = yc.shape
    tr = _pick(S, 256)
    ub = 3 * FOX_WIDTH // W

    def body(yc_ref, u_ref, d_ref, y_ref, yg_ref):
        y = yc_ref[...] + d_ref[...] * u_ref[...]
        y_ref[...] = y
        t = jnp.tanh(_GELU_C * (y + 0.044715 * y * y * y))
        yg_ref[...] = (0.5 * y * (1.0 + t)).astype(BF16)

    row = pl.BlockSpec((tr, W), lambda i: (i, 0))
    return _call(
        body, name=name, grid=(S // tr,),
        in_specs=[row, pl.BlockSpec((tr, W), lambda i: (i, ub)), pl.BlockSpec((1, W), lambda i: (0, 0))],
        out_specs=[row, row],
        out_shape=[jax.ShapeDtypeStruct((S, W), F32), jax.ShapeDtypeStruct((S, W), BF16)],
        compiler_params=_cparams(("parallel",)),
    )(yc, P, dskip)


def _s5_out_bwd(dyg, y, P, dskip, *, name):
    S, W = y.shape
    tr = _pick(S, 256)
    ub = 3 * FOX_WIDTH // W

    def body(dyg_ref, y_ref, u_ref, d_ref, dy_ref, du_ref, dd_ref):
        y_ = y_ref[...]
        inner = _GELU_C * (y_ + 0.044715 * y_ * y_ * y_)
        t = jnp.tanh(inner)
        dgelu = 0.5 * (1.0 + t) + 0.5 * y_ * (1.0 - t * t) * _GELU_C * (1.0 + 3.0 * 0.044715 * y_ * y_)
        dy = dyg_ref[...] * dgelu
        dy_ref[...] = dy.astype(BF16)
        du_ref[...] = d_ref[...] * dy
        part = jnp.sum(dy * u_ref[...], axis=0, keepdims=True)

        @pl.when(pl.program_id(0) == 0)
        def _():
            dd_ref[...] = part

        @pl.when(pl.program_id(0) > 0)
        def _():
            dd_ref[...] += part

    row = pl.BlockSpec((tr, W), lambda i: (i, 0))
    vec = pl.BlockSpec((1, W), lambda i: (0, 0))
    return _call(
        body, name=name, grid=(S // tr,),
        in_specs=[row, row, pl.BlockSpec((tr, W), lambda i: (i, ub)), vec],
        out_specs=[row, row, vec],
        out_shape=[jax.ShapeDtypeStruct((S, W), BF16), jax.ShapeDtypeStruct((S, W), F32),
                   jax.ShapeDtypeStruct((1, W), F32)],
        compiler_params=_cparams(("arbitrary",)),
    )(dyg, y, P, dskip)


def _glu_fwd(z, fox, *, name):
    S, W2 = z.shape
    W = W2 // 2
    tr = _pick(S, 256)

    def body(z1_ref, z2_ref, f_ref, o_ref):
        o_ref[:, :W] = f_ref[...].astype(BF16)
        o_ref[:, W:] = (z1_ref[...] * jax.nn.sigmoid(z2_ref[...])).astype(BF16)

    lo = pl.BlockSpec((tr, W), lambda i: (i, 0))
    return _call(
        body, name=name, grid=(S // tr,),
        in_specs=[lo, pl.BlockSpec((tr, W), lambda i: (i, 1)), lo],
        out_specs=pl.BlockSpec((tr, W2), lambda i: (i, 0)),
        out_shape=jax.ShapeDtypeStruct((S, W2), BF16),
        compiler_params=_cparams(("parallel",)),
    )(z, z, fox)


def _glu_bwd(z, dcat, *, name):
    S, W2 = z.shape
    W = W2 // 2
    tr = _pick(S, 256)

    def body(z1_ref, z2_ref, d_ref, dz_ref):
        sg = jax.nn.sigmoid(z2_ref[...])
        d = d_ref[...]
        dz_ref[:, :W] = (d * sg).astype(BF16)
        dz_ref[:, W:] = (d * z1_ref[...] * sg * (1.0 - sg)).astype(BF16)

    lo = pl.BlockSpec((tr, W), lambda i: (i, 0))
    hi = pl.BlockSpec((tr, W), lambda i: (i, 1))
    return _call(
        body, name=name, grid=(S // tr,), in_specs=[lo, hi, hi],
        out_specs=pl.BlockSpec((tr, W2), lambda i: (i, 0)),
        out_shape=jax.ShapeDtypeStruct((S, W2), BF16),
        compiler_params=_cparams(("parallel",)),
    )(z, z, dcat)


ACT_ROWS = 16
ACT_COLS = 256


def _shift_down(cur, prev, k, row):
    return jnp.where(row >= k, pltpu.roll(cur, k, 0), pltpu.roll(prev, k, 0))


def _shift_up(cur, nxt, k, row):
    n = cur.shape[0]
    return jnp.where(row < n - k, pltpu.roll(cur, n - k, 0), pltpu.roll(nxt, n - k, 0))


def _act_fwd(h, cw, cb, *, name):
    _, S, FP = h.shape
    tr = _pick(S, 256)
    hb = tr // ACT_ROWS
    nq = tr // ACT_ROWS

    def body(g_ref, gh_ref, v_ref, vh_ref, wg_ref, wv_ref, bg_ref, bv_ref, a_ref, hc_ref):
        first = pl.program_id(1) == 0
        for c0 in range(0, FP, ACT_COLS):
            cw_ = min(ACT_COLS, FP - c0)
            cols = pl.ds(c0, cw_)
            rw = lax.broadcasted_iota(jnp.int32, (ACT_ROWS, cw_), 0)
            wg = [wg_ref[pl.ds(k, 1), cols] for k in range(3)]
            wv = [wv_ref[pl.ds(k, 1), cols] for k in range(3)]
            bg, bv = bg_ref[:, cols], bv_ref[:, cols]
            halo_g = jnp.where(first, 0.0, gh_ref[:, cols])
            halo_v = jnp.where(first, 0.0, vh_ref[:, cols])

            def chunk(q, _):
                rows = pl.ds(pl.multiple_of(q * ACT_ROWS, ACT_ROWS), ACT_ROWS)
                before = pl.ds(pl.multiple_of(jnp.maximum(q - 1, 0) * ACT_ROWS, ACT_ROWS), ACT_ROWS)
                g, v = g_ref[rows, cols], v_ref[rows, cols]
                gp = jnp.where(q > 0, g_ref[before, cols], halo_g)
                vp = jnp.where(q > 0, v_ref[before, cols], halo_v)
                cg = bg + wg[2] * g + wg[1] * _shift_down(g, gp, 1, rw) + wg[0] * _shift_down(g, gp, 2, rw)
                cv = bv + wv[2] * v + wv[1] * _shift_down(v, vp, 1, rw) + wv[0] * _shift_down(v, vp, 2, rw)
                a_ref[rows, cols] = (cg * jax.nn.sigmoid(cg) * cv).astype(BF16)
                hc_ref[0, rows, cols] = cg
                hc_ref[1, rows, cols] = cv
                return 0

            lax.fori_loop(0, nq, chunk, 0, unroll=2)

    def main(off):
        return pl.BlockSpec((None, tr, FP), lambda j, i: (j + off, i, 0))

    def halo(off):
        return pl.BlockSpec((None, ACT_ROWS, FP), lambda j, i: (j + off, jnp.maximum(i * hb - 1, 0), 0))

    def wspec(off):
        return pl.BlockSpec((None, 3, FP), lambda j, i: (j + off, 0, 0))

    def bspec(off):
        return pl.BlockSpec((None, 1, FP), lambda j, i: (j + off, 0, 0))

    cb3 = cb.reshape(4, 1, FP)
    return _call(
        body, name=name, grid=(2, S // tr),
        in_specs=[main(0), halo(0), main(2), halo(2), wspec(0), wspec(2), bspec(0), bspec(2)],
        out_specs=[pl.BlockSpec((None, tr, FP), lambda j, i: (j, i, 0)),
                   pl.BlockSpec((None, 2, tr, FP), lambda j, i: (j, 0, i, 0))],
        out_shape=[jax.ShapeDtypeStruct((2, S, FP), BF16), jax.ShapeDtypeStruct((2, 2, S, FP), F32)],
        compiler_params=_cparams(("parallel", "parallel")),
    )(h, h, h, h, cw, cw, cb3, cb3)


def _act_bwd(h, hc, da, cw, *, name):
    _, S, FP = h.shape
    tr = _pick(S, 256)
    nq = tr // ACT_ROWS
    nr = S // tr
    half = ACT_ROWS // 2

    def fold(x):
        return x[:half] + x[half:]

    def body(g_ref, v_ref, hc_ref, da_ref, wg_ref, wv_ref,
             dh_ref, dwg_ref, dwv_ref, dbg_ref, dbv_ref, carry_g, carry_v):
        i = pl.program_id(1)
        bottom = i == 0
        for c0 in range(0, FP, ACT_COLS):
            cw_ = min(ACT_COLS, FP - c0)
            cols = pl.ds(c0, cw_)
            rw = lax.broadcasted_iota(jnp.int32, (ACT_ROWS, cw_), 0)
            wg = [wg_ref[pl.ds(k, 1), cols] for k in range(3)]
            wv = [wv_ref[pl.ds(k, 1), cols] for k in range(3)]
            after_g = jnp.where(bottom, 0.0, carry_g[:, cols])
            after_v = jnp.where(bottom, 0.0, carry_v[:, cols])

            def chunk(s, carry):
                ng, nv, acc = carry[0], carry[1], carry[2:]
                q = nq - 1 - s
                rows = pl.ds(pl.multiple_of(q * ACT_ROWS, ACT_ROWS), ACT_ROWS)
                g, v = g_ref[rows, cols], v_ref[rows, cols]
                cg, cv = hc_ref[0, rows, cols], hc_ref[1, rows, cols]
                sg = jax.nn.sigmoid(cg)
                d = da_ref[rows, cols]
                dcg = d * cv * sg * (1.0 + cg * (1.0 - sg))
                dcv = d * cg * sg
                ug1, ug2 = _shift_up(dcg, ng, 1, rw), _shift_up(dcg, ng, 2, rw)
                uv1, uv2 = _shift_up(dcv, nv, 1, rw), _shift_up(dcv, nv, 2, rw)
                dh_ref[0, rows, cols] = (wg[2] * dcg + wg[1] * ug1 + wg[0] * ug2).astype(BF16)
                dh_ref[1, rows, cols] = (wv[2] * dcv + wv[1] * uv1 + wv[0] * uv2).astype(BF16)
                terms = (ug2 * g, ug1 * g, dcg * g, dcg, uv2 * v, uv1 * v, dcv * v, dcv)
                return (dcg, dcv) + tuple(a + fold(t) for a, t in zip(acc, terms))

            zero = jnp.zeros((half, cw_), F32)
            out = lax.fori_loop(0, nq, chunk, (after_g, after_v) + (zero,) * 8, unroll=2)
            carry_g[:, cols] = out[0]
            carry_v[:, cols] = out[1]
            sums = [jnp.sum(a, axis=0, keepdims=True) for a in out[2:]]

            @pl.when(bottom)
            def _():
                for k in range(3):
                    dwg_ref[pl.ds(k, 1), cols] = sums[k]
                    dwv_ref[pl.ds(k, 1), cols] = sums[4 + k]
                dbg_ref[:, cols] = sums[3]
                dbv_ref[:, cols] = sums[7]

            @pl.when(jnp.logical_not(bottom))
            def _():
                for k in range(3):
                    dwg_ref[pl.ds(k, 1), cols] += sums[k]
                    dwv_ref[pl.ds(k, 1), cols] += sums[4 + k]
                dbg_ref[:, cols] += sums[3]
                dbv_ref[:, cols] += sums[7]

    def main(off):
        return pl.BlockSpec((None, tr, FP), lambda j, i: (j + off, nr - 1 - i, 0))

    def wspec(off):
        return pl.BlockSpec((None, 3, FP), lambda j, i: (j + off, 0, 0))

    bspec = pl.BlockSpec((None, 1, FP), lambda j, i: (j, 0, 0))
    pair = pl.BlockSpec((None, 2, tr, FP), lambda j, i: (j, 0, nr - 1 - i, 0))
    dh, dwg, dwv, dbg, dbv = _call(
        body, name=name, grid=(2, nr),
        in_specs=[main(0), main(2), pair, main(0), wspec(0), wspec(2)],
        out_specs=[pair, wspec(0), wspec(0), bspec, bspec],
        out_shape=[jax.ShapeDtypeStruct((2, 2, S, FP), BF16)]
        + [jax.ShapeDtypeStruct((2, 3, FP), F32)] * 2 + [jax.ShapeDtypeStruct((2, 1, FP), F32)] * 2,
        scratch_shapes=[pltpu.VMEM((ACT_ROWS, FP), F32), pltpu.VMEM((ACT_ROWS, FP), F32)],
        compiler_params=_cparams(("parallel", "arbitrary")),
    )(h, h, hc, da, cw, cw)
    return (dh.reshape(4, S, FP), jnp.concatenate([dwg, dwv], axis=0), jnp.concatenate([dbg, dbv], axis=0))


def _rope_tables(posf, *, name, after=()):
    S = posf.shape[0]
    half = ROPE_DIM // 2
    d = np.arange(LANE) % SWA_HEAD_DIM
    invf = np.where(d < ROPE_DIM, ROPE_THETA ** (-(d % half).astype(np.float64) / half), 0.0).astype(np.float32)
    m_rot = (d < ROPE_DIM).astype(np.float32)
    m_a = (d < half).astype(np.float32)
    m_b = ((d >= half) & (d < ROPE_DIM)).astype(np.float32)
    consts = jnp.asarray(np.stack([invf, m_rot, m_a, m_b] + [np.zeros(LANE, np.float32)] * 4))

    def body(p_ref, k_ref, c_ref, sa_ref, sb_ref):
        k = k_ref[...]
        ang = p_ref[...] * k[0:1]
        co, si = jnp.cos(ang), jnp.sin(ang)
        c_ref[...] = k[1:2] * co + (1.0 - k[1:2])
        sa_ref[...] = -k[2:3] * si
        sb_ref[...] = k[3:4] * si

    full = pl.BlockSpec((S, LANE), lambda: (0, 0))
    return _call(
        body, after=after, name=name,
        in_specs=[pl.BlockSpec((S, 1), lambda: (0, 0)), pl.BlockSpec((8, LANE), lambda: (0, 0))],
        out_specs=[full] * 3, out_shape=[jax.ShapeDtypeStruct((S, LANE), F32)] * 3,
    )(posf, consts)


def _rope(xv, tabs_refs, width, inverse):
    rep = width // LANE
    c, sa, sb = (jnp.tile(t[...], (1, rep)) for t in tabs_refs)
    if not inverse:
        return xv * c + pltpu.roll(xv, width - 8, 1) * sa + pltpu.roll(xv, 8, 1) * sb
    return xv * c + pltpu.roll(xv * sa, 8, 1) + pltpu.roll(xv * sb, width - 8, 1)


def _to_heads(x, tabs, *, col0, width, rotate, name, out_dtype):
    S = x.shape[0]
    tr = _pick(S, 256)
    nh = width // SWA_HEAD_DIM
    cb = col0 // width

    def body(x_ref, c_ref, sa_ref, sb_ref, o_ref):
        xv = x_ref[...].astype(F32)
        if rotate:
            xv = _rope(xv, (c_ref, sa_ref, sb_ref), width, False)
        for h in range(nh):
            o_ref[h] = xv[:, h * SWA_HEAD_DIM:(h + 1) * SWA_HEAD_DIM].astype(out_dtype)

    tab = pl.BlockSpec((tr, LANE), lambda i: (i, 0))
    return _call(
        body, name=name, grid=(S // tr,),
        in_specs=[pl.BlockSpec((tr, width), lambda i: (i, cb)), tab, tab, tab],
        out_specs=pl.BlockSpec((nh, tr, SWA_HEAD_DIM), lambda i: (0, i, 0)),
        out_shape=jax.ShapeDtypeStruct((nh, S, SWA_HEAD_DIM), out_dtype),
        compiler_params=_cparams(("parallel",)),
    )(x, *tabs)


def _from_heads(x3, tabs, *, rotate_back, name, out_dtype, skip_rows=0):
    nh = x3.shape[0]
    S = x3.shape[1] - skip_rows
    width = nh * SWA_HEAD_DIM
    tr = _pick(S, 256) if skip_rows == 0 else skip_rows
    off = skip_rows // tr

    def body(x_ref, c_ref, sa_ref, sb_ref, o_ref):
        xv = jnp.concatenate([x_ref[h].astype(F32) for h in range(nh)], axis=1)
        if rotate_back:
            xv = _rope(xv, (c_ref, sa_ref, sb_ref), width, True)
        o_ref[...] = xv.astype(out_dtype)

    tab = pl.BlockSpec((tr, LANE), lambda i: (i, 0))
    return _call(
        body, name=name, grid=(S // tr,),
        in_specs=[pl.BlockSpec((nh, tr, SWA_HEAD_DIM), lambda i: (0, i + off, 0)), tab, tab, tab],
        out_specs=pl.BlockSpec((tr, width), lambda i: (i, 0)),
        out_shape=jax.ShapeDtypeStruct((S, width), out_dtype),
        compiler_params=_cparams(("parallel",)),
    )(x3, *tabs)


def _swa_mask(n):
    rows = SWA_GROUPS * SWA_WINDOW
    qi = lax.broadcasted_iota(jnp.int32, (rows, 2 * SWA_WINDOW), 0) & (SWA_WINDOW - 1)
    kj = lax.broadcasted_iota(jnp.int32, (rows, 2 * SWA_WINDOW), 1)
    rel = SWA_WINDOW + qi - kj
    return (rel >= 0) & (rel < SWA_WINDOW) & ((n > 0) | (kj >= SWA_WINDOW))


def _swa_fwd(qT, kT, vT, sink_rows, *, name):
    S = qT.shape[1]
    W, G, Dh = SWA_WINDOW, SWA_GROUPS, SWA_HEAD_DIM
    nb = S // W
    scale = 1.0 / math.sqrt(Dh)

    def body(q_ref, kp_ref, kc_ref, vp_ref, vc_ref, s_ref, o_ref, l_ref):
        n = pl.program_id(1)
        q = q_ref[...].reshape(G * W, Dh)
        kk = jnp.concatenate([kp_ref[...], kc_ref[...]], axis=0)
        vv = jnp.concatenate([vp_ref[...], vc_ref[...]], axis=0)
        s = lax.dot_general(q, kk, (((1,), (1,)), ((), ())), preferred_element_type=F32) * scale
        s = jnp.where(_swa_mask(n), s, -1e30)
        sink = s_ref[...]
        m = jnp.maximum(jnp.max(s, axis=-1, keepdims=True), sink)
        e = jnp.exp(s - m)
        den = jnp.sum(e, axis=-1, keepdims=True) + jnp.exp(sink - m)
        p = e / den
        o_ref[...] = jnp.dot(p.astype(BF16), vv, preferred_element_type=F32).reshape(G, W, Dh)
        l_ref[...] = (m + jnp.log(den)).reshape(G, W, 1)

    qs = pl.BlockSpec((G, W, Dh), lambda g, n: (g, n, 0))
    prev = pl.BlockSpec((None, W, Dh), lambda g, n: (g, jnp.maximum(n - 1, 0), 0))
    cur = pl.BlockSpec((None, W, Dh), lambda g, n: (g, n, 0))
    return _call(
        body, name=name, grid=(SWA_KV_HEADS, nb),
        in_specs=[qs, prev, cur, prev, cur, pl.BlockSpec((None, G * W, 1), lambda g, n: (g, 0, 0))],
        out_specs=[qs, pl.BlockSpec((G, W, 1), lambda g, n: (g, n, 0))],
        out_shape=[jax.ShapeDtypeStruct((SWA_HEADS, S, Dh), F32), jax.ShapeDtypeStruct((SWA_HEADS, S, 1), F32)],
        compiler_params=_cparams(("parallel", "parallel")),
    )(qT, kT, kT, vT, vT, sink_rows)


def _swa_bwd(qT, kT, vT, sink_rows, oT, L, doT, *, name):
    S = qT.shape[1]
    W, G, Dh = SWA_WINDOW, SWA_GROUPS, SWA_HEAD_DIM
    nb = S // W
    scale = 1.0 / math.sqrt(Dh)

    def body(q_ref, kp_ref, kc_ref, vp_ref, vc_ref, s_ref, o_ref, l_ref, do_ref,
             dq_ref, dk_ref, dv_ref, ds_ref):
        n = pl.program_id(1)
        q = q_ref[...].reshape(G * W, Dh)
        kk = jnp.concatenate([kp_ref[...], kc_ref[...]], axis=0)
        vv = jnp.concatenate([vp_ref[...], vc_ref[...]], axis=0)
        s = lax.dot_general(q, kk, (((1,), (1,)), ((), ())), preferred_element_type=F32) * scale
        lrow = l_ref[...].reshape(G * W, 1)
        p = jnp.where(_swa_mask(n), jnp.exp(s - lrow), 0.0)
        do = do_ref[...].reshape(G * W, Dh)
        do_bf = do.astype(BF16)
        dp = lax.dot_general(do_bf, vv, (((1,), (1,)), ((), ())), preferred_element_type=F32)
        delta = jnp.sum(do * o_ref[...].reshape(G * W, Dh), axis=-1, keepdims=True)
        dsc = p * (dp - delta)
        ds_bf = dsc.astype(BF16)
        dq_ref[...] = (jnp.dot(ds_bf, kk, preferred_element_type=F32) * scale).astype(BF16).reshape(G, W, Dh)
        dkk = lax.dot_general(ds_bf, q, (((0,), (0,)), ((), ())), preferred_element_type=F32) * scale
        dvv = lax.dot_general(p.astype(BF16), do_bf, (((0,), (0,)), ((), ())), preferred_element_type=F32)
        dsk = -jnp.exp(s_ref[...] - lrow) * delta
        dsk = jnp.broadcast_to(jnp.sum(dsk.reshape(G, W, 1), axis=1), (G, LANE))

        @pl.when(n == 0)
        def _():
            dk_ref[...] = jnp.zeros_like(dk_ref)
            dv_ref[...] = jnp.zeros_like(dv_ref)
            ds_ref[...] = jnp.zeros_like(ds_ref)

        rows = pl.ds(pl.multiple_of(n * W, W), 2 * W)
        dk_ref[rows, :] += dkk
        dv_ref[rows, :] += dvv
        ds_ref[...] += dsk

    qs = pl.BlockSpec((G, W, Dh), lambda g, n: (g, n, 0))
    prev = pl.BlockSpec((None, W, Dh), lambda g, n: (g, jnp.maximum(n - 1, 0), 0))
    cur = pl.BlockSpec((None, W, Dh), lambda g, n: (g, n, 0))
    lsp = pl.BlockSpec((G, W, 1), lambda g, n: (g, n, 0))
    kvo = pl.BlockSpec((None, S + W, Dh), lambda g, n: (g, 0, 0))
    return _call(
        body, name=name, grid=(SWA_KV_HEADS, nb),
        in_specs=[qs, prev, cur, prev, cur, pl.BlockSpec((None, G * W, 1), lambda g, n: (g, 0, 0)), qs, lsp, qs],
        out_specs=[qs, kvo, kvo, pl.BlockSpec((None, G, LANE), lambda g, n: (g, 0, 0))],
        out_shape=[jax.ShapeDtypeStruct((SWA_HEADS, S, Dh), BF16),
                   jax.ShapeDtypeStruct((SWA_KV_HEADS, S + W, Dh), F32),
                   jax.ShapeDtypeStruct((SWA_KV_HEADS, S + W, Dh), F32),
                   jax.ShapeDtypeStruct((SWA_KV_HEADS, G, LANE), F32)],
        compiler_params=_cparams(("parallel", "arbitrary")),
    )(qT, kT, kT, vT, vT, sink_rows, oT, L, doT)


def _adamw(w, g, m, v, *, name, tr=128, by_cols=False):
    L, R, C = w.shape
    split = isinstance(g, (list, tuple))
    HR, HC = _half_shape(R, C, by_cols) if split else (R, C)
    tr, tc = _tile2d(HR, HC, tr)
    nr, nc = HR // tr, HC // tc
    c1 = 1.0 / (1.0 - ADAM_B1 ** ADAM_STEP)
    c2 = 1.0 / (1.0 - ADAM_B2 ** ADAM_STEP)
    ng = 2 * L if split else 1

    def body(c_ref, *refs):
        w_ref, g_refs, (m_ref, v_ref, go_ref, d_ref, mo_ref, vo_ref) = refs[0], refs[1:1 + ng], refs[1 + ng:]
        if split:
            mine = pl.program_id(1) == c_ref[0]
            g_ = jnp.where(mine, g_refs[0][...], g_refs[1][...])
            for l in range(1, L):
                g_ = jnp.where(pl.program_id(0) == l,
                               jnp.where(mine, g_refs[2 * l][...], g_refs[2 * l + 1][...]), g_)
        else:
            g_ = g_refs[0][...]
        mn = ADAM_B1 * m_ref[...] + (1.0 - ADAM_B1) * g_
        vn = ADAM_B2 * v_ref[...] + (1.0 - ADAM_B2) * (g_ * g_)
        go_ref[...] = g_
        mo_ref[...] = mn
        vo_ref[...] = vn
        d_ref[...] = -ADAM_LR * ((mn * c1) / (jnp.sqrt(vn * c2) + ADAM_EPS) + ADAM_WD * w_ref[...])

    def whole(l, hf, i, j, c):
        return (l, i, hf * nc + j) if by_cols else (l, hf * nr + i, j)

    def half(layer, own):
        def index(l, hf, i, j, c):
            used = (l == layer) & ((hf == c[0]) if own else (hf != c[0]))
            return jnp.where(used, i, 0), jnp.where(used, j, 0)
        return pl.BlockSpec((tr, tc), index)

    row = pl.BlockSpec((None, tr, tc), whole)
    gs = [h for pair in g for h in pair] if split else [g]
    g_specs = [half(l, own) for l in range(L) for own in (True, False)] if split else [row]
    core = lax.axis_index("c").astype(jnp.int32).reshape(1)
    return _call(
        body, name=name,
        grid_spec=pltpu.PrefetchScalarGridSpec(
            num_scalar_prefetch=1, grid=(L, 2 if split else 1, nr, nc),
            in_specs=[row] + g_specs + [row, row], out_specs=[row] * 4),
        out_shape=[jax.ShapeDtypeStruct((L, R, C), F32)] * 4,
        compiler_params=_cparams(("parallel",) * 4),
    )(core, w, *gs, m, v)


def _adamw_half(w, g, m, v, *, name, own, prev=None, tr=128, by_cols=False):
    L, R, C = w.shape
    HR, HC = _half_shape(R, C, by_cols)
    tr, tc = _tile2d(HR, HC, tr)
    nr, nc = HR // tr, HC // tc
    c1 = 1.0 / (1.0 - ADAM_B1 ** ADAM_STEP)
    c2 = 1.0 / (1.0 - ADAM_B2 ** ADAM_STEP)

    def body(c_ref, *refs):
        w_ref, g_refs, m_ref, v_ref = refs[0], refs[1:1 + L], refs[1 + L], refs[2 + L]
        go_ref, d_ref, mo_ref, vo_ref = refs[-4:]
        g_ = g_refs[0][...]
        for l in range(1, L):
            g_ = jnp.where(pl.program_id(0) == l, g_refs[l][...], g_)
        mn = ADAM_B1 * m_ref[...] + (1.0 - ADAM_B1) * g_
        vn = ADAM_B2 * v_ref[...] + (1.0 - ADAM_B2) * (g_ * g_)
        go_ref[...] = g_
        mo_ref[...] = mn
        vo_ref[...] = vn
        d_ref[...] = -ADAM_LR * ((mn * c1) / (jnp.sqrt(vn * c2) + ADAM_EPS) + ADAM_WD * w_ref[...])

    def whole(l, i, j, c):
        hf = c[0] if own else 1 - c[0]
        return (l, i, hf * nc + j) if by_cols else (l, hf * nr + i, j)

    def layer_half(layer):
        def index(l, i, j, c):
            return jnp.where(l == layer, i, 0), jnp.where(l == layer, j, 0)
        return pl.BlockSpec((tr, tc), index)

    row = pl.BlockSpec((None, tr, tc), whole)
    core = lax.axis_index("c").astype(jnp.int32).reshape(1)
    prev = list(prev) if prev is not None else []
    return _call(
        body, name=name,
        grid_spec=pltpu.PrefetchScalarGridSpec(
            num_scalar_prefetch=1, grid=(L, nr, nc),
            in_specs=[row] + [layer_half(l) for l in range(L)] + [row, row] + [ANY] * len(prev),
            out_specs=[row] * 4),
        out_shape=[jax.ShapeDtypeStruct((L, R, C), F32)] * 4,
        input_output_aliases={4 + L + k: k for k in range(len(prev))},
        compiler_params=_cparams(("parallel",) * 3),
    )(core, w, *g, m, v, *prev)


def _sum2_halves(g4, s4, by_cols, *, name):
    n, R, C = g4.shape
    HR, HC = _half_shape(R, C, by_cols)
    tr, tc = _tile2d(HR, HC, budget=1024 * 1024)
    nr, nc = HR // tr, HC // tc
    core = lax.axis_index("c").astype(jnp.int32).reshape(1)

    def body(c_ref, g_ref, s_ref, o_ref):
        o_ref[...] = (g_ref[...].astype(F32) + s_ref[...].astype(F32)).astype(BF16)

    def mine(k, i, j, c):
        return (k, i, c[0] * nc + j) if by_cols else (k, c[0] * nr + i, j)

    blk = pl.BlockSpec((None, tr, tc), lambda k, i, j, c: (k, i, j))
    return _call(
        body, name=name,
        grid_spec=pltpu.PrefetchScalarGridSpec(
            num_scalar_prefetch=1, grid=(n, nr, nc),
            in_specs=[pl.BlockSpec((None, tr, tc), mine), blk], out_specs=blk),
        out_shape=jax.ShapeDtypeStruct((n, HR, HC), BF16),
        compiler_params=_cparams(("parallel", "parallel", "parallel")),
    )(core, g4, s4)


def _rowsum(parts, *, name, out_dtype=F32):
    n, R, C = parts.shape
    tr, tc = _tile2d(R, C, budget=512 * 1024)

    def body(p_ref, o_ref):
        acc = p_ref[0].astype(F32)
        for i in range(1, n):
            acc = acc + p_ref[i].astype(F32)
        o_ref[...] = acc.astype(out_dtype)

    return _call(
        body, name=name, grid=(R // tr, C // tc),
        in_specs=[pl.BlockSpec((n, tr, tc), lambda i, j: (0, i, j))],
        out_specs=pl.BlockSpec((tr, tc), lambda i, j: (i, j)),
        out_shape=jax.ShapeDtypeStruct((R, C), out_dtype),
        compiler_params=_cparams(("parallel", "parallel")),
    )(parts)


def _where_am_i():
    x, y, c = lax.axis_index("x"), lax.axis_index("y"), lax.axis_index("c")
    chips = [(1 - x, y), (x, 1 - y), (1 - x, 1 - y)]
    return x, y, c, chips


def _half_idx(rows, cols, by_cols, which):
    if by_cols:
        hc = cols // 2
        return (slice(None), pl.ds(pl.multiple_of(which * hc, LANE), hc))
    hr = rows // 2
    return (pl.ds(pl.multiple_of(which * hr, 16), hr), slice(None))


def _half_shape(rows, cols, by_cols):
    return (rows, cols // 2) if by_cols else (rows // 2, cols)


HBM_SPEC = pl.BlockSpec(memory_space=pltpu.HBM)
SEM_SPEC = pl.BlockSpec(memory_space=pltpu.SEMAPHORE)
DATAFLOW = pltpu.SideEffectType.DATAFLOW_SIDE_EFFECTING


def _chip_exchange_refs(kind, shards_shape, by_cols, src, land, i, chip_k, c, me):
    if kind == 'gather':
        half = _half_idx(*shards_shape, by_cols, c)
        return src.at[half], land.at[(me,) + half], land.at[(chip_k,) + half]
    return src.at[chip_k], land.at[me], land.at[chip_k]


def _chip_exchange_start(kind, srcs, by_cols, *, name, after=()):
    n = len(srcs)
    land_shapes = [((N_CHIPS,) + s.shape) if kind == 'gather' else s.shape for s in srcs]

    def body(*refs):
        src_refs, land_refs = refs[:n], refs[n:2 * n]
        send, recv = refs[2 * n + len(after)], refs[2 * n + len(after) + 1]
        token = refs[-1]
        x, y, c, chips = _where_am_i()
        me = 2 * x + y
        for i in range(n):
            for k, (px, py) in enumerate(chips):
                s, d, _ = _chip_exchange_refs(kind, srcs[i].shape, by_cols[i], src_refs[i], land_refs[i], i,
                                              2 * px + py, c, me)
                pltpu.make_async_remote_copy(src_ref=s, dst_ref=d, send_sem=send.at[3 * i + k],
                                             recv_sem=recv.at[3 * i + k], device_id=(px, py, c),
                                             device_id_type=MESH).start()
        token[...] = jnp.zeros_like(token)

    lands = [pltpu.with_memory_space_constraint(lax.empty(sh, s.dtype), pltpu.HBM) for sh, s in zip(land_shapes, srcs)]
    outs = _call(
        body, name=name,
        out_shape=(pltpu.SemaphoreType.DMA((3 * n,)), pltpu.SemaphoreType.DMA((3 * n,)),
                   *[pltpu.HBM(s.shape, s.dtype) for s in srcs],
                   *[pltpu.HBM(sh, s.dtype) for sh, s in zip(land_shapes, srcs)],
                   jax.ShapeDtypeStruct((8, LANE), F32)),
        in_specs=[HBM_SPEC] * (2 * n) + [ANY] * len(after),
        out_specs=(SEM_SPEC, SEM_SPEC, *([HBM_SPEC] * (2 * n)), pl.BlockSpec(memory_space=pltpu.VMEM)),
        input_output_aliases={j: 2 + j for j in range(2 * n)},
        compiler_params=pltpu.CompilerParams(has_side_effects=DATAFLOW),
    )(*[pltpu.with_memory_space_constraint(s, pltpu.HBM) for s in srcs], *lands, *after)
    return outs[0], outs[1], list(outs[2:2 + n]), list(outs[2 + n:2 + 2 * n]), outs[-1]


def _chip_exchange_wait(kind, send, recv, srcs, lands, by_cols, after, *, name):
    n = len(srcs)

    def body(*refs):
        src_refs, land_refs = refs[:n], refs[n:2 * n]
        send_r, recv_r = refs[2 * n], refs[2 * n + 1]
        x, y, c, chips = _where_am_i()
        me = 2 * x + y
        for i in range(n):
            for k, (px, py) in enumerate(chips):
                s, _, d = _chip_exchange_refs(kind, srcs[i].shape, by_cols[i], src_refs[i], land_refs[i], i,
                                              2 * px + py, c, me)
                cp = pltpu.make_async_remote_copy(src_ref=s, dst_ref=d, send_sem=send_r.at[3 * i + k],
                                                  recv_sem=recv_r.at[3 * i + k], device_id=(px, py, c),
                                                  device_id_type=MESH)
                cp.wait_send()
                cp.wait_recv()

    outs = _call(
        body, name=name,
        out_shape=(*[pltpu.HBM(s.shape, s.dtype) for s in srcs], *[pltpu.HBM(l.shape, l.dtype) for l in lands]),
        in_specs=[HBM_SPEC] * (2 * n) + [SEM_SPEC, SEM_SPEC] + [ANY] * len(after),
        out_specs=tuple([HBM_SPEC] * (2 * n)),
        input_output_aliases={j: j for j in range(2 * n)},
        compiler_params=pltpu.CompilerParams(has_side_effects=DATAFLOW),
    )(*srcs, *lands, send, recv, *after)
    return list(outs[:n]), list(outs[n:])


def _sibling_halves_start(grads, by_cols, *, name, after=()):
    n = len(grads)
    land_shapes = [(N_CHIPS,) + _half_shape(*g.shape[1:], bc) for g, bc in zip(grads, by_cols)]

    def body(*refs):
        src_refs, land_refs = refs[:n], refs[n:2 * n]
        send, recv = refs[2 * n + len(after)], refs[2 * n + len(after) + 1]
        token = refs[-1]
        x, y, c, _ = _where_am_i()
        for i in range(n):
            src = src_refs[i].at[(slice(None),) + _half_idx(*grads[i].shape[1:], by_cols[i], 1 - c)]
            pltpu.make_async_remote_copy(src_ref=src, dst_ref=land_refs[i], send_sem=send.at[i], recv_sem=recv.at[i],
                                         device_id=(x, y, 1 - c), device_id_type=MESH).start()
        token[...] = jnp.zeros_like(token)

    lands = [pltpu.with_memory_space_constraint(lax.empty(sh, g.dtype), pltpu.HBM) for sh, g in zip(land_shapes, grads)]
    outs = _call(
        body, name=name,
        out_shape=(pltpu.SemaphoreType.DMA((n,)), pltpu.SemaphoreType.DMA((n,)),
                   *[pltpu.HBM(g.shape, g.dtype) for g in grads],
                   *[pltpu.HBM(sh, g.dtype) for sh, g in zip(land_shapes, grads)],
                   jax.ShapeDtypeStruct((8, LANE), F32)),
        in_specs=[HBM_SPEC] * (2 * n) + [ANY] * len(after),
        out_specs=(SEM_SPEC, SEM_SPEC, *([HBM_SPEC] * (2 * n)), pl.BlockSpec(memory_space=pltpu.VMEM)),
        input_output_aliases={j: 2 + j for j in range(2 * n)},
        compiler_params=pltpu.CompilerParams(has_side_effects=DATAFLOW),
    )(*[pltpu.with_memory_space_constraint(g, pltpu.HBM) for g in grads], *lands, *after)
    return outs[0], outs[1], list(outs[2:2 + n]), list(outs[2 + n:2 + 2 * n]), outs[-1]


def _sibling_halves_wait(send, recv, grads, lands, by_cols, after, *, name):
    n = len(grads)

    def body(*refs):
        src_refs, land_refs = refs[:n], refs[n:2 * n]
        send_r, recv_r = refs[2 * n], refs[2 * n + 1]
        x, y, c, _ = _where_am_i()
        for i in range(n):
            src = src_refs[i].at[(slice(None),) + _half_idx(*grads[i].shape[1:], by_cols[i], 1 - c)]
            cp = pltpu.make_async_remote_copy(src_ref=src, dst_ref=land_refs[i], send_sem=send_r.at[i],
                                              recv_sem=recv_r.at[i], device_id=(x, y, 1 - c), device_id_type=MESH)
            cp.wait_send()
            cp.wait_recv()

    outs = _call(
        body, name=name,
        out_shape=(*[pltpu.HBM(g.shape, g.dtype) for g in grads], *[pltpu.HBM(l.shape, l.dtype) for l in lands]),
        in_specs=[HBM_SPEC] * (2 * n) + [SEM_SPEC, SEM_SPEC] + [ANY] * len(after),
        out_specs=tuple([HBM_SPEC] * (2 * n)),
        input_output_aliases={j: j for j in range(2 * n)},
        compiler_params=pltpu.CompilerParams(has_side_effects=DATAFLOW),
    )(*grads, *lands, send, recv, *after)
    return list(outs[:n]), list(outs[n:])


def _sibling_swap_start(arrs, *, name, after=()):
    n = len(arrs)

    def body(*refs):
        src_refs, land_refs = refs[:n], refs[n:2 * n]
        send, recv = refs[2 * n + len(after)], refs[2 * n + len(after) + 1]
        token = refs[-1]
        x, y, c, _ = _where_am_i()
        for i in range(n):
            pltpu.make_async_remote_copy(src_ref=src_refs[i], dst_ref=land_refs[i], send_sem=send.at[i],
                                         recv_sem=recv.at[i], device_id=(x, y, 1 - c), device_id_type=MESH).start()
        token[...] = jnp.zeros_like(token)

    lands = [pltpu.with_memory_space_constraint(lax.empty(a.shape, a.dtype), pltpu.HBM) for a in arrs]
    outs = _call(
        body, name=name,
        out_shape=(pltpu.SemaphoreType.DMA((n,)), pltpu.SemaphoreType.DMA((n,)),
                   *[pltpu.HBM(a.shape, a.dtype) for a in arrs] * 2, jax.ShapeDtypeStruct((8, LANE), F32)),
        in_specs=[HBM_SPEC] * (2 * n) + [ANY] * len(after),
        out_specs=(SEM_SPEC, SEM_SPEC, *([HBM_SPEC] * (2 * n)), pl.BlockSpec(memory_space=pltpu.VMEM)),
        input_output_aliases={j: 2 + j for j in range(2 * n)},
        compiler_params=pltpu.CompilerParams(has_side_effects=DATAFLOW),
    )(*[pltpu.with_memory_space_constraint(a, pltpu.HBM) for a in arrs], *lands, *after)
    return outs[0], outs[1], list(outs[2:2 + n]), list(outs[2 + n:2 + 2 * n]), outs[-1]


def _sibling_swap_wait(send, recv, arrs, lands, after, *, name):
    n = len(arrs)

    def body(*refs):
        src_refs, land_refs = refs[:n], refs[n:2 * n]
        send_r, recv_r = refs[2 * n], refs[2 * n + 1]
        x, y, c, _ = _where_am_i()
        for i in range(n):
            cp = pltpu.make_async_remote_copy(src_ref=src_refs[i], dst_ref=land_refs[i], send_sem=send_r.at[i],
                                              recv_sem=recv_r.at[i], device_id=(x, y, 1 - c), device_id_type=MESH)
            cp.wait_send()
            cp.wait_recv()

    outs = _call(
        body, name=name,
        out_shape=tuple(pltpu.HBM(a.shape, a.dtype) for a in list(arrs) + list(lands)),
        in_specs=[HBM_SPEC] * (2 * n) + [SEM_SPEC, SEM_SPEC] + [ANY] * len(after),
        out_specs=tuple([HBM_SPEC] * (2 * n)),
        input_output_aliases={j: j for j in range(2 * n)},
        compiler_params=pltpu.CompilerParams(has_side_effects=DATAFLOW),
    )(*arrs, *lands, send, recv, *after)
    return list(outs[:n]), list(outs[n:])


def _sibling_pass_gathered(lands, shard_shapes, by_cols, *, name):
    n = len(lands)

    def body(*refs):
        outs = refs[n:2 * n]
        send, recv = refs[2 * n:]
        x, y, c, chips = _where_am_i()
        sibling = (x, y, 1 - c)
        cps = []
        for i in range(n):
            for k, (px, py) in enumerate(chips):
                blk = outs[i].at[(2 * px + py,) + _half_idx(*shard_shapes[i], by_cols[i], c)]
                d = pltpu.make_async_remote_copy(src_ref=blk, dst_ref=blk, send_sem=send.at[i, k],
                                                 recv_sem=recv.at[i, k], device_id=sibling, device_id_type=MESH)
                d.start()
                cps.append(d)
        for i in range(n):
            for k, (px, py) in enumerate(chips):
                blk = outs[i].at[(2 * px + py,) + _half_idx(*shard_shapes[i], by_cols[i], 1 - c)]
                pltpu.make_async_remote_copy(src_ref=blk, dst_ref=blk, send_sem=send.at[i, k], recv_sem=recv.at[i, k],
                                             device_id=sibling, device_id_type=MESH).wait_recv()
        for d in cps:
            d.wait_send()

    return _call(
        body, name=name, in_specs=[ANY] * n, out_specs=[ANY] * n,
        out_shape=[jax.ShapeDtypeStruct(l.shape, l.dtype) for l in lands],
        input_output_aliases={j: j for j in range(n)},
        scratch_shapes=[pltpu.SemaphoreType.DMA((n, 3)), pltpu.SemaphoreType.DMA((n, 3))],
    )(*lands)


def _own_slot(lands, owns):
    me = 2 * lax.axis_index("x") + lax.axis_index("y")
    return [lax.dynamic_update_slice_in_dim(g, s, me, axis=0) for g, s in zip(lands, owns)]


def _sibling_send_halves(grads, by_cols, *, name):
    n = len(grads)

    def body(*refs):
        ins, outs = refs[:n], refs[n:2 * n]
        send, recv = refs[2 * n:]
        x, y, c, _ = _where_am_i()
        sibling = (x, y, 1 - c)
        cps = []
        for i in range(n):
            src = ins[i].at[(slice(None),) + _half_idx(*grads[i].shape[1:], by_cols[i], 1 - c)]
            d = pltpu.make_async_remote_copy(src_ref=src, dst_ref=outs[i], send_sem=send.at[i],
                                             recv_sem=recv.at[i], device_id=sibling, device_id_type=MESH)
            d.start()
            cps.append(d)
        for d in cps:
            d.wait()

    return _call(
        body, name=name, in_specs=[ANY] * n, out_specs=[ANY] * n,
        out_shape=[jax.ShapeDtypeStruct((N_CHIPS,) + _half_shape(*g.shape[1:], bc), g.dtype)
                   for g, bc in zip(grads, by_cols)],
        scratch_shapes=[pltpu.SemaphoreType.DMA((n,)), pltpu.SemaphoreType.DMA((n,))],
    )(*grads)


def _all_reduce_small(v, *, name, after=()):
    R, C = v.shape
    H = R // 2

    def body(v_ref, o_ref, sib, slots, send, recv):
        x, y, c, chips = _where_am_i()
        me = 2 * x + y
        sibling = (x, y, 1 - c)
        mine = pl.ds(pl.multiple_of(c * H, 8), H)
        other = pl.ds(pl.multiple_of((1 - c) * H, 8), H)

        def copy(k, src, dst, to):
            return pltpu.make_async_remote_copy(src_ref=src, dst_ref=dst, send_sem=send.at[k], recv_sem=recv.at[k],
                                                device_id=to, device_id_type=MESH)

        d = copy(0, v_ref.at[other], sib, sibling)
        d.start()
        d.wait()
        slots[me] = v_ref[mine, :] + sib[...]
        cps = [copy(1 + k, slots.at[me], slots.at[me], (px, py, c)) for k, (px, py) in enumerate(chips)]
        for d in cps:
            d.start()
        for k, (px, py) in enumerate(chips):
            blk = slots.at[2 * px + py]
            copy(1 + k, blk, blk, (px, py, c)).wait_recv()
        for d in cps:
            d.wait_send()
        o_ref[mine, :] = (slots[0] + slots[1]) + (slots[2] + slots[3])
        d = copy(4, o_ref.at[mine], o_ref.at[mine], sibling)
        d.start()
        copy(4, o_ref.at[other], o_ref.at[other], sibling).wait_recv()
        d.wait_send()

    vm = pl.BlockSpec(memory_space=pltpu.VMEM)
    return _call(
        body, after=after, name=name, in_specs=[vm], out_specs=vm,
        out_shape=jax.ShapeDtypeStruct((R, C), F32),
        scratch_shapes=[pltpu.VMEM((H, C), F32), pltpu.VMEM((N_CHIPS, H, C), F32),
                        pltpu.SemaphoreType.DMA((5,)), pltpu.SemaphoreType.DMA((5,))],
        compiler_params=pltpu.CompilerParams(vmem_limit_bytes=VMEM_LIMIT),
    )(v)


def _cols_from_shards(g):
    return jnp.transpose(g, (1, 0, 2)).reshape(g.shape[1], -1)


def _shards_from_cols(w):
    R, C4 = w.shape
    return jnp.transpose(w.reshape(R, N_CHIPS, C4 // N_CHIPS), (1, 0, 2))


def _pack(arrs):
    flat = []
    for a in arrs:
        f = a.reshape(-1).astype(F32)
        flat.append(jnp.pad(f, (0, _rup(f.shape[0], LANE) - f.shape[0])))
    v = jnp.concatenate(flat)
    rows = _rup(v.shape[0] // LANE, 16)
    v = jnp.pad(v, (0, rows * LANE - v.shape[0]))
    return v.reshape(rows, LANE)


def _unpack(v, shapes):
    flat = v.reshape(-1)
    out, off = [], 0
    for s in shapes:
        n = int(np.prod(s))
        out.append(flat[off:off + n].reshape(s))
        off += _rup(n, LANE)
    return out


def _ffn_fwd(x, Wup, Wdn, cw, cb, tag):
    h = _mm(x, Wup, 'nt', bmode='bo', tm=512, tn=4096, name=f"ffn_up_{tag}")
    a, hc = _act_fwd(h, cw, cb, name=f"ffn_act_{tag}")
    f = _mm(a, Wdn, 'nn', bmode='abr', tm=512, tn=1024, tk=4096, name=f"ffn_down_{tag}")
    return f, (h, hc), a


def _ffn_bwd(df, x, saved, a, Wup, Wdn, cw, tag):
    h, hc = saved
    da = _mm(df, Wdn, 'nt', bmode='bo', tm=512, tn=4096, name=f"ffn_da_{tag}")
    dWdn = _mm(a, df, 'tn', bmode='ao', tm=4096, tn=512, name=f"ffn_dwdn_{tag}", out_dtype=BF16)
    dh, dcw, dcb = _act_bwd(h, hc, da, cw, name=f"ffn_actb_{tag}")

    def shard_of(k):
        return (k % 2) * 2 + k // 2

    dx = _mm(dh, Wup, 'nn', bmode='abr', tm=512, tn=1024, tk=4096, name=f"ffn_dx_{tag}", b_map=shard_of)
    dWup = _mm(dh, x, 'tn', bmode='ao', tm=4096, tn=512, name=f"ffn_dwup_{tag}", out_dtype=BF16,
               o_map=shard_of)
    return dx, dWup, dWdn, dcw, dcb


def kernel(x, positions, ev_w_in, ev_b_f, ev_lambda_re, ev_lambda_im, ev_log_step, ev_ssm_b_re, ev_ssm_b_im, ev_ssm_c_re, ev_ssm_c_im, ev_ssm_d, ev_w_glu, ev_w_out, od_w_in, od_sinks, od_w_out, ln_mix_g, ln_mix_b, ffn_w_up, ffn_conv_w, ffn_conv_b, ffn_w_down, ln_ffn_g, ln_ffn_b, loss_target, m_ev_w_in, m_ev_b_f, m_ev_lambda_re, m_ev_lambda_im, m_ev_log_step, m_ev_ssm_b_re, m_ev_ssm_b_im, m_ev_ssm_c_re, m_ev_ssm_c_im, m_ev_ssm_d, m_ev_w_glu, m_ev_w_out, m_od_w_in, m_od_sinks, m_od_w_out, m_ln_mix_g, m_ln_mix_b, m_ffn_w_up, m_ffn_conv_w, m_ffn_conv_b, m_ffn_w_down, m_ln_ffn_g, m_ln_ffn_b, v_ev_w_in, v_ev_b_f, v_ev_lambda_re, v_ev_lambda_im, v_ev_log_step, v_ev_ssm_b_re, v_ev_ssm_b_im, v_ev_ssm_c_re, v_ev_ssm_c_im, v_ev_ssm_d, v_ev_w_glu, v_ev_w_out, v_od_w_in, v_od_sinks, v_od_w_out, v_ln_mix_g, v_ln_mix_b, v_ffn_w_up, v_ffn_conv_w, v_ffn_conv_b, v_ffn_w_down, v_ln_ffn_g, v_ln_ffn_b):
    W = dict(ev_w_in=ev_w_in, ev_b_f=ev_b_f, ev_lambda_re=ev_lambda_re, ev_lambda_im=ev_lambda_im, ev_log_step=ev_log_step, ev_ssm_b_re=ev_ssm_b_re, ev_ssm_b_im=ev_ssm_b_im, ev_ssm_c_re=ev_ssm_c_re, ev_ssm_c_im=ev_ssm_c_im, ev_ssm_d=ev_ssm_d, ev_w_glu=ev_w_glu, ev_w_out=ev_w_out, od_w_in=od_w_in, od_sinks=od_sinks, od_w_out=od_w_out, ln_mix_g=ln_mix_g, ln_mix_b=ln_mix_b, ffn_w_up=ffn_w_up, ffn_conv_w=ffn_conv_w, ffn_conv_b=ffn_conv_b, ffn_w_down=ffn_w_down, ln_ffn_g=ln_ffn_g, ln_ffn_b=ln_ffn_b)
    Mo = dict(ev_w_in=m_ev_w_in, ev_b_f=m_ev_b_f, ev_lambda_re=m_ev_lambda_re, ev_lambda_im=m_ev_lambda_im, ev_log_step=m_ev_log_step, ev_ssm_b_re=m_ev_ssm_b_re, ev_ssm_b_im=m_ev_ssm_b_im, ev_ssm_c_re=m_ev_ssm_c_re, ev_ssm_c_im=m_ev_ssm_c_im, ev_ssm_d=m_ev_ssm_d, ev_w_glu=m_ev_w_glu, ev_w_out=m_ev_w_out, od_w_in=m_od_w_in, od_sinks=m_od_sinks, od_w_out=m_od_w_out, ln_mix_g=m_ln_mix_g, ln_mix_b=m_ln_mix_b, ffn_w_up=m_ffn_w_up, ffn_conv_w=m_ffn_conv_w, ffn_conv_b=m_ffn_conv_b, ffn_w_down=m_ffn_w_down, ln_ffn_g=m_ln_ffn_g, ln_ffn_b=m_ln_ffn_b)
    Vo = dict(ev_w_in=v_ev_w_in, ev_b_f=v_ev_b_f, ev_lambda_re=v_ev_lambda_re, ev_lambda_im=v_ev_lambda_im, ev_log_step=v_ev_log_step, ev_ssm_b_re=v_ev_ssm_b_re, ev_ssm_b_im=v_ev_ssm_b_im, ev_ssm_c_re=v_ev_ssm_c_re, ev_ssm_c_im=v_ev_ssm_c_im, ev_ssm_d=v_ev_ssm_d, ev_w_glu=v_ev_w_glu, ev_w_out=v_ev_w_out, od_w_in=v_od_w_in, od_sinks=v_od_sinks, od_w_out=v_od_w_out, ln_mix_g=v_ln_mix_g, ln_mix_b=v_ln_mix_b, ffn_w_up=v_ffn_w_up, ffn_conv_w=v_ffn_conv_w, ffn_conv_b=v_ffn_conv_b, ffn_w_down=v_ffn_w_down, ln_ffn_g=v_ln_ffn_g, ln_ffn_b=v_ln_ffn_b)
    names = list(W.keys())
    big = ['ev_w_in', 'ev_w_glu', 'ev_w_out', 'od_w_in', 'od_w_out', 'ffn_w_up', 'ffn_w_down']

    S, D = x.shape[1], x.shape[2]
    x0 = x.reshape(S, D)
    tgt = loss_target.reshape(S, D)
    G, Pn, Cg = SSM_GROUPS, SSM_STATE, SSM_GROUP
    Fs = ffn_w_up.shape[2]
    FP = Fs
    Rd = ffn_w_down.shape[1]
    EIN = N_CHIPS * ev_w_in.shape[2]

    cwl = ffn_conv_w.reshape(-1)
    cw_rows = _rup(_rup(cwl.shape[0], LANE) // LANE, 32)
    cw_pad = jnp.pad(cwl, (0, cw_rows * LANE - cwl.shape[0])).reshape(cw_rows, LANE)
    transposed = ('ev_w_in', 'ffn_w_up')

    def view(n, a):
        return jnp.transpose(a, (0, 2, 1)) if n in transposed else a

    Wv = {n: view(n, W[n]) for n in big}
    big_e = [(n, l) for n in big for l in range(W[n].shape[0])]
    split_cols = {e: (Wv[e[0]].shape[1] // 2) % 16 != 0 for e in big_e}
    shard16 = {e: Wv[e[0]][e[1]].astype(BF16) for e in big_e}
    grp_now = [e for e in big_e if e[0].startswith('ev_')]
    grp_ffn0 = [('ffn_w_up', 0), ('ffn_w_down', 0)]
    grp_l1 = [('od_w_in', 0), ('od_w_out', 0), ('ffn_w_up', 1), ('ffn_w_down', 1)]
    src_now = [shard16[e] for e in grp_now]
    src_ffn0 = [shard16[e] for e in grp_ffn0] + [cw_pad]
    src_l1 = [shard16[e] for e in grp_l1]
    cols_now = [split_cols[e] for e in grp_now]
    cols_ffn0 = [split_cols[e] for e in grp_ffn0] + [False]
    cols_l1 = [split_cols[e] for e in grp_l1]
    ag_in = _chip_exchange_start('gather', src_now[:1], cols_now[:1], name="ag_in_start")
    ag_mix = _chip_exchange_start('gather', src_now[1:], cols_now[1:], name="ag_mix_start", after=[ag_in[4]])
    ag_ffn0 = _chip_exchange_start('gather', src_ffn0, cols_ffn0, name="ag_ffn0_start", after=[ag_mix[4]])
    ag_l1 = _chip_exchange_start('gather', src_l1, cols_l1, name="ag_l1_start", after=[ag_ffn0[4]])
    started = [ag_l1[4]]

    def finish_gather(started, srcs, cols, after, tag):
        send, recv, thru, lands, _ = started
        thru, lands = _chip_exchange_wait('gather', send, recv, thru, lands, cols, after, name=f"ag_{tag}_wait")
        lands = _sibling_pass_gathered(lands, [s.shape for s in srcs], cols, name=f"ag_{tag}_pass")
        return _own_slot(lands, [s[None] for s in thru])

    lam_r, lam_i = ev_lambda_re[0], ev_lambda_im[0]
    lstep = ev_log_step[0].reshape(G, 1)
    a_re, a_im, g_re, g_im = _s5_disc_fwd(lam_r, lam_i, lstep, name="s5_disc", after=started)
    b_re2, b_im2 = ev_ssm_b_re[0].reshape(G * Pn, Cg), ev_ssm_b_im[0].reshape(G * Pn, Cg)
    g_re1, g_im1 = g_re.reshape(G * Pn, 1), g_im.reshape(G * Pn, 1)
    bb_re, bb_im = _s5_bb_fwd(g_re1, g_im1, b_re2, b_im2, name="s5_bb")
    bbt = jnp.stack([jnp.transpose(b.reshape(G, Pn, Cg), (0, 2, 1)).reshape(G * Cg, Pn) for b in (bb_re, bb_im)])
    BB = _diag_expand(bbt, Cg, Pn, name="s5_bb_dense")
    cct = jnp.stack([jnp.transpose(ev_ssm_c_re[0], (0, 2, 1)).reshape(G * Pn, Cg),
                     jnp.transpose(-ev_ssm_c_im[0], (0, 2, 1)).reshape(G * Pn, Cg)])
    CC = _diag_expand(cct, Pn, Cg, name="s5_cc_dense", after=started)
    a_cat = jnp.stack([a_re.reshape(1, G * Pn), a_im.reshape(1, G * Pn)])
    dskip = ev_ssm_d[0].reshape(1, SSM_WIDTH)
    tabs = _rope_tables(positions.reshape(S, 1).astype(F32), name="rope_tables", after=[BB, CC])

    gw = dict(zip(grp_now[:1], finish_gather(ag_in, src_now[:1], cols_now[:1], [tabs[2]], "in")))
    w_in_t = gw[('ev_w_in', 0)].reshape(EIN, D)
    qkv_w = 3 * FOX_WIDTH
    WmainT = jnp.concatenate([w_in_t[:qkv_w], w_in_t[qkv_w + FOX_HEADS:]], axis=0)
    WfT = jnp.pad(w_in_t[qkv_w:qkv_w + FOX_HEADS], ((0, LANE - FOX_HEADS), (0, 0)))
    cbs = [ffn_conv_b[l].reshape(N_CHIPS, Fs) for l in range(DEPTH)]

    P = _mm(x0, WmainT, 'nt', name="ev_proj")
    fl = _mm(x0, WfT, 'nt', name="ev_proj_f")
    bf_pad = jnp.pad(ev_b_f.reshape(1, FOX_HEADS), ((0, 0), (0, LANE - FOX_HEADS)))
    cgate, sgate = _gate_fwd(fl, bf_pad, name="fox_gate")
    ccol = jnp.transpose(cgate[:, :FOX_HEADS]).reshape(FOX_HEADS, S, 1)
    crow = jnp.transpose(cgate[:, :FOX_HEADS]).reshape(FOX_HEADS, 1, S)
    fox, lse = _fox_fwd(P, ccol, crow, name="fox_fwd")
    u_s5 = P[:, qkv_w:]
    UT, HT = _DIAG_TILE * Cg, _DIAG_TILE * Pn
    bu = _mm(u_s5, BB, 'nn', bmode='bo', tm=2048, tn=HT, tk=UT, diag='kn', name="s5_bu")
    hh = _s5_scan_fwd(bu, a_cat, name="s5_scan")
    yc = _mm(hh, CC, 'nn', bmode='abr', tm=2048, tn=UT, tk=HT, diag='kn', name="s5_y")
    y_s5, yg = _s5_out_fwd(yc, P, dskip, name="s5_out")
    gw.update(zip(grp_now[1:], finish_gather(ag_mix, src_now[1:], cols_now[1:], [yg], "mix")))
    Wglu = _cols_from_shards(gw[('ev_w_glu', 0)])
    Wout_ev = gw[('ev_w_out', 0)].reshape(D, D)
    z = _mm(yg, Wglu, 'nn', name="s5_glu_proj")
    cat = _glu_fwd(z, fox, name="s5_glu")
    mix0 = _mm(cat, Wout_ev, 'nn', name="ev_out")
    x1, xh1, rs1 = _add_ln_fwd(x0, mix0, ln_mix_g[0], ln_mix_b[0], name="ln_mix0")
    got = finish_gather(ag_ffn0, src_ffn0, cols_ffn0, [x1], "ffn0")
    gw.update(zip(grp_ffn0, got[:-1]))
    cw_all = got[-1].reshape(N_CHIPS, -1)[:, :cwl.shape[0]].reshape(N_CHIPS, DEPTH, 3, Fs)
    cws = [cw_all[:, l] for l in range(DEPTH)]
    Wup = {0: gw[('ffn_w_up', 0)]}
    Wdn = {0: gw[('ffn_w_down', 0)].reshape(2, Fs, D)}
    f0, hf0, af0 = _ffn_fwd(x1, Wup[0], Wdn[0], cws[0], cbs[0], "l0")
    x2, xh2, rs2 = _add_ln_fwd(x1, f0, ln_ffn_g[0], ln_ffn_b[0], name="ln_ffn0")

    gw.update(zip(grp_l1, finish_gather(ag_l1, src_l1, cols_l1, [x2], "l1")))
    Wodin = _cols_from_shards(gw[('od_w_in', 0)])
    Wodout = gw[('od_w_out', 0)].reshape(D, D)
    Wup[1] = gw[('ffn_w_up', 1)]
    Wdn[1] = gw[('ffn_w_down', 1)].reshape(2, Fs, D)
    QW, KW = SWA_HEADS * SWA_HEAD_DIM, SWA_KV_HEADS * SWA_HEAD_DIM
    P1 = _mm(x2, Wodin, 'nn', name="od_proj")
    qT = _to_heads(P1, tabs, col0=0, width=QW, rotate=True, name="rope_q", out_dtype=BF16)
    kT = _to_heads(P1, tabs, col0=QW, width=KW, rotate=True, name="rope_k", out_dtype=BF16)
    vT = _to_heads(P1, tabs, col0=QW + KW, width=KW, rotate=False, name="heads_v", out_dtype=BF16)
    sink_rows = jnp.broadcast_to(od_sinks[0].reshape(SWA_KV_HEADS, SWA_GROUPS, 1, 1),
                                 (SWA_KV_HEADS, SWA_GROUPS, SWA_WINDOW, 1)).reshape(SWA_KV_HEADS, -1, 1)
    oT, Lsw = _swa_fwd(qT, kT, vT, sink_rows, name="swa_fwd")
    o_sw = _from_heads(oT, tabs, rotate_back=False, name="heads_o", out_dtype=BF16)
    mix1 = _mm(o_sw, Wodout, 'nn', name="od_out")
    x3, xh3, rs3 = _add_ln_fwd(x2, mix1, ln_mix_g[1], ln_mix_b[1], name="ln_mix1")
    f1, hf1, af1 = _ffn_fwd(x3, Wup[1], Wdn[1], cws[1], cbs[1], "l1")
    _, xh4, rs4 = _add_ln_fwd(x3, f1, ln_ffn_g[1], ln_ffn_b[1], name="ln_ffn1")

    dz4, dg_ffn1, db_ffn1, loss_part = _loss_ln_bwd(tgt, xh4, rs4, ln_ffn_g[1], ln_ffn_b[1], name="loss_lnb_ffn1")
    dx3f, dWup1, dWdn1, dcw1, dcb1 = _ffn_bwd(dz4, x3, hf1, af1, Wup[1], Wdn[1], cws[1], "l1")
    sib_ffn1 = _sibling_halves_start([dWup1, dWdn1.reshape(N_CHIPS, Rd, D)], [False, False], name="rs_ffn1_sib_start")
    dz3, dg_mix1, db_mix1 = _ln_bwd(dz4, dx3f, xh3, rs3, ln_mix_g[1], name="lnb_mix1", after=[sib_ffn1[4]])
    do_sw = _mm(dz3, Wodout, 'nt', name="od_out_dx")
    dWodout = _mm(o_sw, dz3, 'tn', name="od_out_dw", out_dtype=BF16)
    doT = _to_heads(do_sw, tabs, col0=0, width=QW, rotate=False, name="heads_do", out_dtype=F32)
    dqT, dkT, dvT, dsink = _swa_bwd(qT, kT, vT, sink_rows, oT, Lsw, doT, name="swa_bwd")
    dq1 = _from_heads(dqT, tabs, rotate_back=True, name="rope_dq", out_dtype=BF16)
    dk1 = _from_heads(dkT, tabs, rotate_back=True, name="rope_dk", out_dtype=BF16, skip_rows=SWA_WINDOW)
    dv1 = _from_heads(dvT, tabs, rotate_back=False, name="heads_dv", out_dtype=BF16, skip_rows=SWA_WINDOW)
    dP1 = jnp.concatenate([dq1, dk1, dv1], axis=1)
    dx2m = _mm(dP1, Wodin, 'nt', name="od_proj_dx")
    dWodin = _mm(x2, dP1, 'tn', name="od_proj_dw", out_dtype=BF16)

    def rs_begin(entries, grads, tag):
        cols = [split_cols[e] for e in entries]
        sib = _sibling_send_halves(grads, cols, name=f"rs_{tag}_sibling")
        return [_sum2_halves(g4, s4, bc, name=f"rs_sum2_{n}{l}")
                for (n, l), g4, s4, bc in zip(entries, grads, sib, cols)]

    def rs_begin_started(entries, started, after, tag):
        send, rcv, thru, lands, _ = started
        thru, lands = _sibling_halves_wait(send, rcv, thru, lands, [False] * len(thru), after,
                                           name=f"rs_{tag}_sib_wait")
        return [_sum2_halves(g4, s4, False, name=f"rs_sum2_{n}{l}") for (n, l), g4, s4 in zip(entries, thru, lands)]

    def own_parts(parts):
        me = 2 * lax.axis_index("x") + lax.axis_index("y")
        return [lax.dynamic_slice_in_dim(p, me, 1, axis=0) for p in parts]

    part_l1 = (rs_begin(grp_l1[:2], [_shards_from_cols(dWodin), dWodout.reshape(N_CHIPS, D // N_CHIPS, D)], "od")
               + rs_begin_started(grp_l1[2:], sib_ffn1, [dWodin], "ffn1"))
    rs_l1 = _chip_exchange_start('scatter', part_l1, [False] * len(part_l1), name="rs_l1_start")

    dz2, dg_ffn0, db_ffn0 = _ln_bwd(dz3, dx2m, xh2, rs2, ln_ffn_g[0], name="lnb_ffn0", after=[rs_l1[4]])
    dx1f, dWup0, dWdn0, dcw0, dcb0 = _ffn_bwd(dz2, x1, hf0, af0, Wup[0], Wdn[0], cws[0], "l0")
    sib_ffn0 = _sibling_halves_start([dWup0, dWdn0.reshape(N_CHIPS, Rd, D)], [False, False], name="rs_ffn0_sib_start")
    dz1, dg_mix0, db_mix0 = _ln_bwd(dz2, dx1f, xh1, rs1, ln_mix_g[0], name="lnb_mix0", after=[sib_ffn0[4]])
    dcat = _mm(dz1, Wout_ev, 'nt', name="ev_out_dx")
    dWout_ev = _mm(cat, dz1, 'tn', name="ev_out_dw", out_dtype=BF16)
    part_ffn0 = rs_begin_started(grp_ffn0, sib_ffn0, [dWout_ev], "ffn0")
    rs_ffn0 = _chip_exchange_start('scatter', part_ffn0, [False] * len(part_ffn0), name="rs_ffn0_start")
    dz = _glu_bwd(z, dcat, name="s5_glu_bwd")
    dyg = _mm(dz, Wglu, 'nt', name="s5_glu_dx", after=[rs_ffn0[4]])
    dWglu = _mm(yg, dz, 'tn', name="s5_glu_dw", out_dtype=BF16)
    dy_s5, du_dir, dD = _s5_out_bwd(dyg, y_s5, P, dskip, name="s5_out_bwd")
    dhh = _mm(dy_s5, CC, 'nt', bmode='bo', tm=2048, tn=HT, tk=UT, diag='kn', name="s5_y_dx")
    dCC = _mm(hh, dy_s5, 'tn', bmode='ao', tm=HT, tn=UT, diag='mn', name="s5_y_dw")
    lam, da_s5 = _s5_scan_bwd(dhh, hh, a_cat, name="s5_scan_bwd")
    du = _mm(lam, BB, 'nt', bmode='abr', tm=2048, tn=UT, tk=HT, diag='kn', name="s5_bu_dx", plus=[(du_dir, 1.0)],
             out_dtype=BF16)
    dBB = _mm(u_s5, lam, 'tn', bmode='bo', tm=UT, tn=HT, diag='mn', name="s5_bu_dw")
    dq0, dk0, dv0, dccol, dcrow = _fox_bwd(P, ccol, crow, fox, lse, dcat, name="fox_bwd")
    dc = jnp.transpose((dccol.reshape(FOX_HEADS, S) - dcrow.reshape(FOX_HEADS, S)))
    dc = jnp.pad(dc, ((0, 0), (0, LANE - FOX_HEADS)))
    dfl, dbf = _gate_bwd(dc, sgate, name="fox_gate_bwd")
    dP = jnp.concatenate([dq0, dk0, dv0, du], axis=1)
    dx0b = _mm(dfl, WfT, 'nn', name="ev_proj_f_dx")
    grad_x = _mm(dP, WmainT, 'nn', name="ev_proj_dx", plus=[(dz1, ALPHA), (dx0b, 1.0)])
    dWmainT = _mm(dP, x0, 'tn', tm=1024, tn=1024, name="ev_proj_dw", out_dtype=BF16)
    dWfT = _mm(dfl, x0, 'tn', name="ev_proj_f_dw", out_dtype=BF16)

    dbbt = _diag_extract(dBB, Cg, Pn, name="s5_bb_diag")
    dcct = _diag_extract(dCC, Pn, Cg, name="s5_cc_diag")
    dbb_re = jnp.transpose(dbbt[0].reshape(G, Cg, Pn), (0, 2, 1)).reshape(G * Pn, Cg)
    dbb_im = jnp.transpose(dbbt[1].reshape(G, Cg, Pn), (0, 2, 1)).reshape(G * Pn, Cg)
    db_re, db_im, dg_re1, dg_im1 = _s5_bb_bwd(g_re1, g_im1, b_re2, b_im2, dbb_re, dbb_im, name="s5_bb_bwd")
    dlam_re, dlam_im, dlstep = _s5_disc_bwd(lam_r, lam_i, lstep, da_s5[0].reshape(G, Pn), da_s5[1].reshape(G, Pn),
                                            dg_re1.reshape(G, Pn), dg_im1.reshape(G, Pn), name="s5_disc_bwd")
    dc_re = jnp.transpose(dcct[0].reshape(G, Pn, Cg), (0, 2, 1))
    dc_im = -jnp.transpose(dcct[1].reshape(G, Pn, Cg), (0, 2, 1))

    def conv_w_full(d0, d1):
        return jnp.stack([jnp.reshape(jnp.transpose(d[:, :, :Fs], (1, 0, 2)), (3, N_CHIPS * Fs)) for d in (d0, d1)])

    def conv_b_full(d0, d1):
        return jnp.stack([jnp.reshape(d[:, 0, :Fs], (N_CHIPS * Fs,)) for d in (d0, d1)])

    small_local = dict(
        ev_b_f=dbf[:, :FOX_HEADS], ev_lambda_re=dlam_re, ev_lambda_im=dlam_im, ev_log_step=dlstep,
        ev_ssm_b_re=db_re, ev_ssm_b_im=db_im, ev_ssm_c_re=dc_re, ev_ssm_c_im=dc_im, ev_ssm_d=dD,
        od_sinks=dsink[:, :, 0],
        ln_mix_g=jnp.concatenate([dg_mix0, dg_mix1]), ln_mix_b=jnp.concatenate([db_mix0, db_mix1]),
        ffn_conv_w=conv_w_full(dcw0, dcw1), ffn_conv_b=conv_b_full(dcb0, dcb1),
        ln_ffn_g=jnp.concatenate([dg_ffn0, dg_ffn1]), ln_ffn_b=jnp.concatenate([db_ffn0, db_ffn1]))
    small = list(small_local.keys())
    out_g, out_d, out_m, out_v = {}, {}, {}, {}
    loss_out = []

    def small_update(after):
        red = _all_reduce_small(_pack([small_local[n] for n in small] + [loss_part]), name="ar_small", after=after)
        full_shapes = [W[n].shape if n != 'ffn_conv_w' else (DEPTH, 3, N_CHIPS * Fs) for n in small]
        pieces = _unpack(red, full_shapes + [()])
        loss_out.append(pieces[-1])
        gsmall = dict(zip(small, pieces[:-1]))
        chip = 2 * lax.axis_index("x") + lax.axis_index("y")
        gsmall['ffn_conv_w'] = lax.dynamic_slice_in_dim(gsmall['ffn_conv_w'], chip * Fs, Fs, axis=2)
        shapes = [W[n].shape for n in small]
        gs, ds_, ms, vs = _adamw(_pack([W[n] for n in small])[None], _pack([gsmall[n] for n in small])[None],
                                 _pack([Mo[n] for n in small])[None], _pack([Vo[n] for n in small])[None],
                                 name="adamw_small", tr=1 << 14)
        out_g.update(zip(small, _unpack(gs, shapes)))
        out_d.update(zip(small, _unpack(ds_, shapes)))
        out_m.update(zip(small, _unpack(ms, shapes)))
        out_v.update(zip(small, _unpack(vs, shapes)))
        return vs

    dw_in_t = jnp.concatenate([dWmainT[:qkv_w], dWfT[:FOX_HEADS], dWmainT[qkv_w:]], axis=0)
    part_now = rs_begin(grp_now, [dw_in_t.reshape(N_CHIPS, EIN // N_CHIPS, D), _shards_from_cols(dWglu),
                                  dWout_ev.reshape(N_CHIPS, D // N_CHIPS, D)], "l0")
    small_done = small_update([grad_x])
    rs_now = _chip_exchange_start('scatter', part_now, [False] * len(part_now), name="rs_l0_start",
                                  after=[small_done])

    def finish_scatter(started, parts, after, tag):
        send, rcv, thru, lands, _ = started
        thru, lands = _chip_exchange_wait('scatter', send, rcv, thru, lands, [False] * len(parts), after,
                                          name=f"rs_{tag}_wait")
        return _own_slot(lands, own_parts(thru))

    def update(entries, recv, tag):
        halves = [_rowsum(r, name=f"rs_sum4_{e[0]}{e[1]}") for e, r in zip(entries, recv)]
        send, rcv, thru, lands, tok = _sibling_swap_start(halves, name=f"rs_{tag}_join_start")
        own = dict(zip(entries, thru))
        params = list(dict.fromkeys(e[0] for e in entries))

        def half_update(n, grads, is_own, prev, after_name):
            return _adamw_half(Wv[n], [grads[(n, l)] for l in range(W[n].shape[0])], view(n, Mo[n]), view(n, Vo[n]),
                               name=f"adamw_{after_name}_{n}", own=is_own, prev=prev, by_cols=split_cols[(n, 0)])

        first = {n: half_update(n, own, True, None, "own") for n in params}
        _, others = _sibling_swap_wait(send, rcv, thru, lands, [first[n][3] for n in params] + [tok],
                                       name=f"rs_{tag}_join_wait")
        oth = dict(zip(entries, others))
        done = []
        for n in params:
            res = half_update(n, oth, False, first[n], "sib")
            out_g[n], out_d[n], out_m[n], out_v[n] = (view(n, t) for t in res)
            done.append(res[3])
        return done

    recv_rest = (finish_scatter(rs_l1, part_l1, [rs_now[4]], "l1")
                 + finish_scatter(rs_ffn0, part_ffn0, [rs_now[4]], "ffn0"))
    done = update(grp_l1 + grp_ffn0, recv_rest, "rest")
    update(grp_now, finish_scatter(rs_now, part_now, done, "l0"), "l0")
    loss = loss_out[0]

    return (loss, grad_x.reshape(1, S, D), *[out_g[n] for n in names], *[out_d[n] for n in names],
            *[out_m[n] for n in names], *[out_v[n] for n in names])
```

```python
import math

import numpy as np
import jax
import jax.numpy as jnp
from jax import lax
from jax.experimental import pallas as pl
from jax.experimental.pallas import tpu as pltpu

F32 = jnp.float32
BF16 = jnp.bfloat16
MESH = pl.DeviceIdType.MESH
ANY = pl.BlockSpec(memory_space=pl.ANY)

D_MODEL = 2048
FOX_HEADS = 8
FOX_HEAD_DIM = 128
FOX_WIDTH = 1024
SSM_WIDTH = 1024
SSM_GROUP = 16
SSM_GROUPS = 64
SSM_STATE = 64
SWA_HEADS = 32
SWA_KV_HEADS = 4
SWA_HEAD_DIM = 64
SWA_GROUPS = 8
SWA_WINDOW = 128
ROPE_DIM = 16
ROPE_THETA = 500000.0
LN_EPS = 1e-5
DEPTH = 2
ALPHA = (2.0 * DEPTH) ** 0.25
ADAM_LR = 0.001
ADAM_B1 = 0.9
ADAM_B2 = 0.999
ADAM_EPS = 1e-08
ADAM_WD = 0.01
ADAM_STEP = 10
N_CHIPS = 4

VMEM_LIMIT = 56 * 1024 * 1024
LANE = 128


def _call(body, after=(), **kw):
    if after:
        n = len(after)

        def shifted(*refs):
            return body(*refs[n:])

        call = _call(shifted, **dict(kw, in_specs=[ANY] * n + list(kw["in_specs"])))
        return lambda *args: call(*after, *args)
    return pl.pallas_call(body, **kw)


def _cparams(sem):
    return pltpu.CompilerParams(dimension_semantics=sem, vmem_limit_bytes=VMEM_LIMIT)


def _rup(n, m):
    return (n + m - 1) // m * m


def _pick(n, pref):
    if n <= pref:
        return n
    for step in (128, 16, 8):
        for t in range(pref - pref % step, 0, -step):
            if n % t == 0:
                return t
    return n


def _tile2d(rows, cols, pref_rows=256, budget=256 * 1024):
    tr = _pick(rows, pref_rows)
    if tr < 64:
        tr = rows
    if cols % LANE:
        return tr, cols
    return tr, _pick(cols, max(LANE, budget // tr // LANE * LANE))


def _mm(a, b, mode, *, name, tm=512, tn=1024, tk=2048, bmode=None, out_dtype=F32, after=(), b_map=None,
        o_map=None, diag=None, plus=()):
    a3 = a if a.ndim == 3 else a[None]
    b3 = b if b.ndim == 3 else b[None]
    if mode == 'tn':
        K, M = a3.shape[1:]
    else:
        M, K = a3.shape[1:]
    N = b3.shape[1] if mode == 'nt' else b3.shape[2]
    tm, tn, tk = _pick(M, tm), _pick(N, tn), _pick(K, tk)
    nb = max(a3.shape[0], b3.shape[0])
    nbo, nbr = (1, nb) if bmode == 'abr' else (nb, 1)
    nm, nk = M // tm, K // tk
    if diag == 'kn':
        assert K // tk == N // tn
        nk = 1
    if diag == 'mn':
        assert M // tm == N // tn
        nm = 1
    nred = nbr * nk
    a_b = bmode in ('ao', 'abr')
    b_b = bmode in ('bo', 'abr')
    o_b = bmode in ('bo', 'ao')

    def bsel(flag, bo, br, remap=None):
        if not flag:
            return 0
        return (bo + br) if remap is None else remap(bo + br)

    def mi(i, j):
        return j if diag == 'mn' else i

    def ki(j, k):
        return j if diag == 'kn' else k

    if mode == 'tn':
        a_spec = pl.BlockSpec((None, tk, tm), lambda bo, i, j, br, k: (bsel(a_b, bo, br), ki(j, k), mi(i, j)))
    else:
        a_spec = pl.BlockSpec((None, tm, tk), lambda bo, i, j, br, k: (bsel(a_b, bo, br), mi(i, j), ki(j, k)))
    if mode == 'nt':
        b_spec = pl.BlockSpec((None, tn, tk), lambda bo, i, j, br, k: (bsel(b_b, bo, br, b_map), j, ki(j, k)))
    else:
        b_spec = pl.BlockSpec((None, tk, tn), lambda bo, i, j, br, k: (bsel(b_b, bo, br, b_map), ki(j, k), j))
    o_spec = pl.BlockSpec((None, tm, tn), lambda bo, i, j, br, k: (bsel(o_b, bo, br, o_map), mi(i, j), j))
    dn = {'nn': (((1,), (0,)), ((), ())), 'nt': (((1,), (1,)), ((), ())), 'tn': (((0,), (0,)), ((), ()))}[mode]

    na = len(plus)

    def body(a_ref, b_ref, *rest):
        plus_refs = rest[:na]
        o_ref, scratch = rest[na + len(after)], rest[na + len(after) + 1:]
        r = lax.dot_general(a_ref[...].astype(BF16), b_ref[...].astype(BF16), dn, preferred_element_type=F32)

        def finish(total):
            for (_, scale), p_ref in zip(plus, plus_refs):
                total = total + scale * p_ref[...].astype(F32)
            o_ref[...] = total.astype(out_dtype)

        if nred == 1:
            finish(r)
        else:
            acc = scratch[0]
            step = pl.program_id(3) * nk + pl.program_id(4)

            @pl.when(step == 0)
            def _():
                acc[...] = r

            @pl.when(step > 0)
            def _():
                acc[...] += r

            @pl.when(step == nred - 1)
            def _():
                finish(acc[...])

    out = _call(
        body, name=name,
        grid=(nbo, nm, N // tn, nbr, nk),
        in_specs=[a_spec, b_spec] + [o_spec] * na + [ANY] * len(after), out_specs=o_spec,
        out_shape=jax.ShapeDtypeStruct((nbo if o_b else 1, M, N), out_dtype),
        scratch_shapes=[] if nred == 1 else [pltpu.VMEM((tm, tn), F32)],
        compiler_params=_cparams(("parallel", "parallel", "parallel", "arbitrary", "arbitrary")),
    )(a3, b3, *[p if p.ndim == 3 else p[None] for p, _ in plus], *after)
    return out if o_b else out[0]


def _add_ln_fwd(x, r, g, b, *, name):
    S, D = x.shape
    tr = _pick(S, 256)

    def body(x_ref, r_ref, g_ref, b_ref, o_ref, xh_ref, rs_ref):
        z = ALPHA * x_ref[...] + r_ref[...]
        mu = jnp.mean(z, axis=-1, keepdims=True)
        zc = z - mu
        var = jnp.mean(zc * zc, axis=-1, keepdims=True)
        rstd = lax.rsqrt(var + LN_EPS)
        xh = zc * rstd
        xh_ref[...] = xh
        rs_ref[...] = rstd
        o_ref[...] = xh * g_ref[...] + b_ref[...]

    row = pl.BlockSpec((tr, D), lambda i: (i, 0))
    vec = pl.BlockSpec((1, D), lambda i: (0, 0))
    return _call(
        body, name=name, grid=(S // tr,),
        in_specs=[row, row, vec, vec],
        out_specs=[row, row, pl.BlockSpec((tr, 1), lambda i: (i, 0))],
        out_shape=[jax.ShapeDtypeStruct((S, D), F32), jax.ShapeDtypeStruct((S, D), F32),
                   jax.ShapeDtypeStruct((S, 1), F32)],
        compiler_params=_cparams(("parallel",)),
    )(x, r, g.reshape(1, D), b.reshape(1, D))


def _ln_bwd(da, db, xhat, rstd, g, *, name, after=()):
    S, D = xhat.shape
    tr = _pick(S, 256)

    def body(*refs):
        da_ref, db_ref, xh_ref, rs_ref, g_ref, dz_ref, dg_ref, dbt_ref = refs[len(after):]
        dy = ALPHA * da_ref[...] + db_ref[...]
        xh = xh_ref[...]
        dxh = dy * g_ref[...]
        m1 = jnp.mean(dxh, axis=-1, keepdims=True)
        m2 = jnp.mean(dxh * xh, axis=-1, keepdims=True)
        dz_ref[...] = rs_ref[...] * (dxh - m1 - xh * m2)
        pg = jnp.sum(dy * xh, axis=0, keepdims=True)
        pb = jnp.sum(dy, axis=0, keepdims=True)

        @pl.when(pl.program_id(0) == 0)
        def _():
            dg_ref[...] = pg
            dbt_ref[...] = pb

        @pl.when(pl.program_id(0) > 0)
        def _():
            dg_ref[...] += pg
            dbt_ref[...] += pb

    row = pl.BlockSpec((tr, D), lambda i: (i, 0))
    vec = pl.BlockSpec((1, D), lambda i: (0, 0))
    ins = list(after) + [da, db, xhat, rstd, g.reshape(1, D)]
    in_specs = [ANY] * len(after) + [row, row, row, pl.BlockSpec((tr, 1), lambda i: (i, 0)), vec]
    return _call(
        body, name=name, grid=(S // tr,),
        in_specs=in_specs, out_specs=[row, vec, vec],
        out_shape=[jax.ShapeDtypeStruct((S, D), F32), jax.ShapeDtypeStruct((1, D), F32),
                   jax.ShapeDtypeStruct((1, D), F32)],
        compiler_params=_cparams(("arbitrary",)),
    )(*ins)


def _loss_ln_bwd(t, xhat, rstd, g, b, *, name):
    S, D = xhat.shape
    tr = _pick(S, 256)

    def body(t_ref, xh_ref, rs_ref, g_ref, b_ref, dz_ref, dg_ref, dbt_ref, l_ref):
        xh = xh_ref[...]
        e = xh * g_ref[...] + b_ref[...] - t_ref[...]
        dy = e * (1.0 / D)
        part = 0.5 * jnp.sum(jnp.sum(e * e, axis=-1, keepdims=True) * (1.0 / D), axis=0, keepdims=True)
        dxh = dy * g_ref[...]
        m1 = jnp.mean(dxh, axis=-1, keepdims=True)
        m2 = jnp.mean(dxh * xh, axis=-1, keepdims=True)
        dz_ref[...] = rs_ref[...] * (dxh - m1 - xh * m2)
        pg = jnp.sum(dy * xh, axis=0, keepdims=True)
        pb = jnp.sum(dy, axis=0, keepdims=True)

        @pl.when(pl.program_id(0) == 0)
        def _():
            dg_ref[...] = pg
            dbt_ref[...] = pb
            l_ref[...] = part

        @pl.when(pl.program_id(0) > 0)
        def _():
            dg_ref[...] += pg
            dbt_ref[...] += pb
            l_ref[...] += part

    row = pl.BlockSpec((tr, D), lambda i: (i, 0))
    vec = pl.BlockSpec((1, D), lambda i: (0, 0))
    return _call(
        body, name=name, grid=(S // tr,),
        in_specs=[row, row, pl.BlockSpec((tr, 1), lambda i: (i, 0)), vec, vec],
        out_specs=[row, vec, vec, pl.BlockSpec((1, 1), lambda i: (0, 0))],
        out_shape=[jax.ShapeDtypeStruct((S, D), F32), jax.ShapeDtypeStruct((1, D), F32),
                   jax.ShapeDtypeStruct((1, D), F32), jax.ShapeDtypeStruct((1, 1), F32)],
        compiler_params=_cparams(("arbitrary",)),
    )(t, xhat, rstd, g.reshape(1, D), b.reshape(1, D))


def _split3(x):
    h = x.astype(BF16)
    r = x - h.astype(F32)
    m = r.astype(BF16)
    l = (r - m.astype(F32)).astype(BF16)
    return h, m, l


def _tri_matmul(tri_bf, x):
    h, m, l = _split3(x)
    dn = (((1,), (0,)), ((), ()))
    return (lax.dot_general(tri_bf, l, dn, preferred_element_type=F32)
            + lax.dot_general(tri_bf, m, dn, preferred_element_type=F32)
            + lax.dot_general(tri_bf, h, dn, preferred_element_type=F32))


def _gate_fwd(fl, bf, *, name):
    S = fl.shape[0]
    tc = _pick(S, 256)
    nchunk = S // tc

    def body(fl_ref, bf_ref, c_ref, sg_ref):
        r = lax.broadcasted_iota(jnp.int32, (tc, tc), 0)
        cidx = lax.broadcasted_iota(jnp.int32, (tc, tc), 1)
        tri = (r >= cidx).astype(BF16)
        carry = jnp.zeros((1, LANE), F32)
        for ch in range(nchunk):
            x = fl_ref[pl.ds(ch * tc, tc), :] + bf_ref[...]
            lf = jnp.minimum(x, 0.0) - jnp.log(1.0 + jnp.exp(-jnp.abs(x)))
            sg_ref[pl.ds(ch * tc, tc), :] = jax.nn.sigmoid(-x)
            c_ref[pl.ds(ch * tc, tc), :] = _tri_matmul(tri, lf) + carry
            carry = carry + jnp.sum(lf, axis=0, keepdims=True)

    full = pl.BlockSpec((S, LANE), lambda: (0, 0))
    return _call(
        body, name=name, in_specs=[full, pl.BlockSpec((1, LANE), lambda: (0, 0))], out_specs=[full, full],
        out_shape=[jax.ShapeDtypeStruct((S, LANE), F32)] * 2,
        compiler_params=pltpu.CompilerParams(vmem_limit_bytes=VMEM_LIMIT),
    )(fl, bf)


def _gate_bwd(dc, sg, *, name):
    S = dc.shape[0]
    tc = _pick(S, 256)
    nchunk = S // tc

    def body(dc_ref, sg_ref, dfl_ref, db_ref):
        r = lax.broadcasted_iota(jnp.int32, (tc, tc), 0)
        cidx = lax.broadcasted_iota(jnp.int32, (tc, tc), 1)
        tri = (r <= cidx).astype(BF16)
        carry = jnp.zeros((1, LANE), F32)
        dbacc = jnp.zeros((1, LANE), F32)
        for ch in reversed(range(nchunk)):
            d = dc_ref[pl.ds(ch * tc, tc), :]
            dfl = (_tri_matmul(tri, d) + carry) * sg_ref[pl.ds(ch * tc, tc), :]
            dfl_ref[pl.ds(ch * tc, tc), :] = dfl
            dbacc = dbacc + jnp.sum(dfl, axis=0, keepdims=True)
            carry = carry + jnp.sum(d, axis=0, keepdims=True)
        db_ref[...] = dbacc

    full = pl.BlockSpec((S, LANE), lambda: (0, 0))
    return _call(
        body, name=name, in_specs=[full, full], out_specs=[full, pl.BlockSpec((1, LANE), lambda: (0, 0))],
        out_shape=[jax.ShapeDtypeStruct((S, LANE), F32), jax.ShapeDtypeStruct((1, LANE), F32)],
        compiler_params=pltpu.CompilerParams(vmem_limit_bytes=VMEM_LIMIT),
    )(dc, sg)


def _fox_scores(q_ref, k_ref, cc_ref, cr_ref, qi, tq, S):
    scale = 1.0 / math.sqrt(FOX_HEAD_DIM)
    s = lax.dot_general(q_ref[...].astype(BF16), k_ref[...].astype(BF16), (((1,), (1,)), ((), ())),
                        preferred_element_type=F32) * scale
    s = s + cc_ref[...] - cr_ref[...]
    row = lax.broadcasted_iota(jnp.int32, (tq, S), 0) + qi * tq
    col = lax.broadcasted_iota(jnp.int32, (tq, S), 1)
    return s, row >= col


def _fox_fwd(P, ccol, crow, *, name):
    S = P.shape[0]
    tq = _pick(S, 256)
    H = FOX_HEADS

    def body(q_ref, k_ref, v_ref, cc_ref, cr_ref, o_ref, l_ref):
        s, causal = _fox_scores(q_ref, k_ref, cc_ref, cr_ref, pl.program_id(1), tq, S)
        s = jnp.where(causal, s, -1e30)
        m = jnp.max(s, axis=-1, keepdims=True)
        e = jnp.exp(s - m)
        den = jnp.sum(e, axis=-1, keepdims=True)
        p = e * (1.0 / den)
        o_ref[...] = jnp.dot(p.astype(BF16), v_ref[...].astype(BF16), preferred_element_type=F32)
        l_ref[...] = m + jnp.log(den)

    return _call(
        body, name=name, grid=(H, S // tq),
        in_specs=[pl.BlockSpec((tq, 128), lambda h, i: (i, h)),
                  pl.BlockSpec((S, 128), lambda h, i: (0, H + h)),
                  pl.BlockSpec((S, 128), lambda h, i: (0, 2 * H + h)),
                  pl.BlockSpec((None, tq, 1), lambda h, i: (h, i, 0)),
                  pl.BlockSpec((None, 1, S), lambda h, i: (h, 0, 0))],
        out_specs=[pl.BlockSpec((tq, 128), lambda h, i: (i, h)),
                   pl.BlockSpec((None, tq, 1), lambda h, i: (h, i, 0))],
        out_shape=[jax.ShapeDtypeStruct((S, FOX_WIDTH), F32), jax.ShapeDtypeStruct((H, S, 1), F32)],
        compiler_params=_cparams(("parallel", "parallel")),
    )(P, P, P, ccol, crow)


def _fox_bwd(P, ccol, crow, o, lse, dcat, *, name):
    S = P.shape[0]
    tq = _pick(S, 512)
    H = FOX_HEADS
    nq = S // tq
    scale = 1.0 / math.sqrt(FOX_HEAD_DIM)

    def body(q_ref, k_ref, v_ref, cc_ref, cr_ref, o_ref, l_ref, do_ref,
             dq_ref, dk_ref, dv_ref, dcc_ref, dcr_ref, dk_acc, dv_acc):
        qi = pl.program_id(1)
        s, causal = _fox_scores(q_ref, k_ref, cc_ref, cr_ref, qi, tq, S)
        p = jnp.where(causal, jnp.exp(s - l_ref[...]), 0.0)
        do = do_ref[...]
        do_bf = do.astype(BF16)
        dp = lax.dot_general(do_bf, v_ref[...].astype(BF16), (((1,), (1,)), ((), ())), preferred_element_type=F32)
        delta = jnp.sum(do * o_ref[...], axis=-1, keepdims=True)
        ds = p * (dp - delta)
        ds_bf = ds.astype(BF16)
        dq_ref[...] = (jnp.dot(ds_bf, k_ref[...].astype(BF16), preferred_element_type=F32) * scale).astype(BF16)
        dkp = lax.dot_general(ds_bf, q_ref[...].astype(BF16), (((0,), (0,)), ((), ())),
                              preferred_element_type=F32) * scale
        dvp = lax.dot_general(p.astype(BF16), do_bf, (((0,), (0,)), ((), ())), preferred_element_type=F32)
        dcc_ref[...] = jnp.sum(ds, axis=-1, keepdims=True)
        dcr = jnp.sum(ds, axis=0, keepdims=True)

        @pl.when(qi == 0)
        def _():
            dk_acc[...] = dkp
            dv_acc[...] = dvp
            dcr_ref[...] = dcr

        @pl.when(qi > 0)
        def _():
            dk_acc[...] += dkp
            dv_acc[...] += dvp
            dcr_ref[...] += dcr

        @pl.when(qi == nq - 1)
        def _():
            dk_ref[...] = dk_acc[...].astype(BF16)
            dv_ref[...] = dv_acc[...].astype(BF16)

    qblk = pl.BlockSpec((tq, 128), lambda h, i: (i, h))
    kvo = pl.BlockSpec((S, 128), lambda h, i: (0, h))
    col = pl.BlockSpec((None, tq, 1), lambda h, i: (h, i, 0))
    rowv = pl.BlockSpec((None, 1, S), lambda h, i: (h, 0, 0))
    return _call(
        body, name=name, grid=(H, nq),
        in_specs=[qblk,
                  pl.BlockSpec((S, 128), lambda h, i: (0, H + h)),
                  pl.BlockSpec((S, 128), lambda h, i: (0, 2 * H + h)),
                  col, rowv, qblk, col, qblk],
        out_specs=[qblk, kvo, kvo, col, rowv],
        out_shape=[jax.ShapeDtypeStruct((S, FOX_WIDTH), BF16)] * 3
        + [jax.ShapeDtypeStruct((H, S, 1), F32), jax.ShapeDtypeStruct((H, 1, S), F32)],
        scratch_shapes=[pltpu.VMEM((S, 128), F32), pltpu.VMEM((S, 128), F32)],
        compiler_params=_cparams(("parallel", "arbitrary")),
    )(P, P, P, ccol, crow, o, lse, dcat)


def _s5_disc_fwd(lr, li, ls, *, name, after=()):
    G, Pn = lr.shape

    def body(lr_ref, li_ref, ls_ref, ar_ref, ai_ref, gr_ref, gi_ref):
        lr_, li_ = lr_ref[...], li_ref[...]
        dt = jnp.exp(ls_ref[...])
        mag = jnp.exp(lr_ * dt)
        th = li_ * dt
        ar = mag * jnp.cos(th)
        ai = mag * jnp.sin(th)
        den = lr_ * lr_ + li_ * li_
        xr = ar - 1.0
        ar_ref[...] = ar
        ai_ref[...] = ai
        gr_ref[...] = (xr * lr_ + ai * li_) / den
        gi_ref[...] = (ai * lr_ - xr * li_) / den

    sq = pl.BlockSpec((G, Pn), lambda: (0, 0))
    return _call(
        body, after=after, name=name, in_specs=[sq, sq, pl.BlockSpec((G, 1), lambda: (0, 0))], out_specs=[sq] * 4,
        out_shape=[jax.ShapeDtypeStruct((G, Pn), F32)] * 4,
    )(lr, li, ls)


def _s5_disc_bwd(lr, li, ls, dar, dai, dgr, dgi, *, name):
    G, Pn = lr.shape

    def body(lr_ref, li_ref, ls_ref, dar_ref, dai_ref, dgr_ref, dgi_ref, dlr_ref, dli_ref, dls_ref):
        lr_, li_ = lr_ref[...], li_ref[...]
        dt = jnp.exp(ls_ref[...])
        mag = jnp.exp(lr_ * dt)
        th = li_ * dt
        ar = mag * jnp.cos(th)
        ai = mag * jnp.sin(th)
        den = lr_ * lr_ + li_ * li_
        xr = ar - 1.0
        xi = ai
        g_re = (xr * lr_ + xi * li_) / den
        g_im = (xi * lr_ - xr * li_) / den
        dgr_, dgi_ = dgr_ref[...], dgi_ref[...]
        dxr = (dgr_ * lr_ - dgi_ * li_) / den
        dxi = (dgr_ * li_ + dgi_ * lr_) / den
        dden = -(dgr_ * g_re + dgi_ * g_im) / den
        dlr = (dgr_ * xr + dgi_ * xi) / den + 2.0 * dden * lr_
        dli = (dgr_ * xi - dgi_ * xr) / den + 2.0 * dden * li_
        da_r = dar_ref[...] + dxr
        da_i = dai_ref[...] + dxi
        dmag_mag = da_r * ar + da_i * ai
        dth = da_i * ar - da_r * ai
        dlr_ref[...] = dlr + dmag_mag * dt
        dli_ref[...] = dli + dth * dt
        ddt = jnp.sum(dmag_mag * lr_ + dth * li_, axis=-1, keepdims=True)
        dls_ref[...] = ddt * dt

    sq = pl.BlockSpec((G, Pn), lambda: (0, 0))
    c1 = pl.BlockSpec((G, 1), lambda: (0, 0))
    return _call(
        body, name=name, in_specs=[sq, sq, c1, sq, sq, sq, sq], out_specs=[sq, sq, c1],
        out_shape=[jax.ShapeDtypeStruct((G, Pn), F32)] * 2 + [jax.ShapeDtypeStruct((G, 1), F32)],
    )(lr, li, ls, dar, dai, dgr, dgi)


def _s5_bb_fwd(gr, gi, br, bi, *, name):
    R, C = br.shape

    def body(gr_ref, gi_ref, br_ref, bi_ref, or_ref, oi_ref):
        g_r, g_i, b_r, b_i = gr_ref[...], gi_ref[...], br_ref[...], bi_ref[...]
        or_ref[...] = g_r * b_r - g_i * b_i
        oi_ref[...] = g_r * b_i + g_i * b_r

    w = pl.BlockSpec((R, C), lambda: (0, 0))
    c1 = pl.BlockSpec((R, 1), lambda: (0, 0))
    return _call(body, name=name, in_specs=[c1, c1, w, w], out_specs=[w, w],
                 out_shape=[jax.ShapeDtypeStruct((R, C), F32)] * 2)(gr, gi, br, bi)


def _s5_bb_bwd(gr, gi, br, bi, dbbr, dbbi, *, name):
    R, C = br.shape

    def body(gr_ref, gi_ref, br_ref, bi_ref, dr_ref, di_ref, dbr_ref, dbi_ref, dgr_ref, dgi_ref):
        g_r, g_i, b_r, b_i = gr_ref[...], gi_ref[...], br_ref[...], bi_ref[...]
        d_r, d_i = dr_ref[...], di_ref[...]
        dbr_ref[...] = g_r * d_r + g_i * d_i
        dbi_ref[...] = g_r * d_i - g_i * d_r
        dgr_ref[...] = jnp.sum(d_r * b_r + d_i * b_i, axis=-1, keepdims=True)
        dgi_ref[...] = jnp.sum(d_i * b_r - d_r * b_i, axis=-1, keepdims=True)

    w = pl.BlockSpec((R, C), lambda: (0, 0))
    c1 = pl.BlockSpec((R, 1), lambda: (0, 0))
    return _call(body, name=name, in_specs=[c1, c1, w, w, w, w], out_specs=[w, w, c1, c1],
                 out_shape=[jax.ShapeDtypeStruct((R, C), F32)] * 2 + [jax.ShapeDtypeStruct((R, 1), F32)] * 2,
                 )(gr, gi, br, bi, dbbr, dbbi)


_DIAG_TILE = 8


def _diag_mask(gr, gc):
    rows, cols = _DIAG_TILE * gr, _DIAG_TILE * gc
    r = lax.broadcasted_iota(jnp.int32, (rows, cols), 0) >> (gr.bit_length() - 1)
    c = lax.broadcasted_iota(jnp.int32, (rows, cols), 1) >> (gc.bit_length() - 1)
    return r == c


def _diag_expand(t2, gr, gc, *, name, after=()):
    _, R, _ = t2.shape
    G = R // gr
    nt = G // _DIAG_TILE
    rows, cols = _DIAG_TILE * gr, _DIAG_TILE * gc

    def body(t_ref, o_ref):
        src = lax.broadcasted_iota(jnp.int32, (gc, cols), 0)
        dst = lax.broadcasted_iota(jnp.int32, (gc, cols), 1) & (gc - 1)
        spread = (src == dst).astype(BF16)
        y = jnp.dot(t_ref[...].astype(BF16), spread, preferred_element_type=F32)
        o_ref[...] = jnp.where(_diag_mask(gr, gc), y, 0.0).astype(BF16)

    return _call(
        body, after=after, name=name, grid=(2, nt),
        in_specs=[pl.BlockSpec((None, rows, gc), lambda p, i: (p, i, 0))],
        out_specs=pl.BlockSpec((None, rows, cols), lambda p, i: (p, i, i)),
        out_shape=jax.ShapeDtypeStruct((2, R, G * gc), BF16),
        compiler_params=_cparams(("parallel",) * 2),
    )(t2)


def _diag_extract(xd, gr, gc, *, name):
    _, R, _ = xd.shape
    nt = R // gr // _DIAG_TILE
    rows, cols = _DIAG_TILE * gr, _DIAG_TILE * gc

    def body(x_ref, o_ref):
        src = lax.broadcasted_iota(jnp.int32, (cols, gc), 0) & (gc - 1)
        dst = lax.broadcasted_iota(jnp.int32, (cols, gc), 1)
        fold = (src == dst).astype(BF16)
        parts = _split3(jnp.where(_diag_mask(gr, gc), x_ref[...], 0.0))
        acc = jnp.dot(parts[2], fold, preferred_element_type=F32)
        acc = acc + jnp.dot(parts[1], fold, preferred_element_type=F32)
        o_ref[...] = acc + jnp.dot(parts[0], fold, preferred_element_type=F32)

    return _call(
        body, name=name, grid=(2, nt),
        in_specs=[pl.BlockSpec((None, rows, cols), lambda p, i: (p, i, i))],
        out_specs=pl.BlockSpec((None, rows, gc), lambda p, i: (p, i, 0)),
        out_shape=jax.ShapeDtypeStruct((2, R, gc), F32),
        compiler_params=_cparams(("parallel",) * 2),
    )(xd)


SCAN_BLOCK = 8


def _cpowers(ar, ai, sign):
    ai = sign * ai
    out = [(ar, ai)]
    for _ in range(SCAN_BLOCK - 1):
        pr, pi = out[-1]
        out.append((pr * ar - pi * ai, pr * ai + pi * ar))
    return out


def _row_table(pw, row, index_of_row):
    tr_ = jnp.broadcast_to(pw[index_of_row(0)][0], row.shape)
    ti_ = jnp.broadcast_to(pw[index_of_row(0)][1], row.shape)
    for r in range(1, SCAN_BLOCK):
        pr, pi = pw[index_of_row(r)]
        tr_ = jnp.where(row == r, pr, tr_)
        ti_ = jnp.where(row == r, pi, ti_)
    return tr_, ti_


def _s5_scan_fwd(bu, a, *, name):
    _, S, N = bu.shape
    tc = 512
    nt = N // tc

    def body(a_ref, b_ref, h_ref):
        pw = _cpowers(a_ref[0], a_ref[1], 1.0)
        row = lax.broadcasted_iota(jnp.int32, (SCAN_BLOCK, tc), 0)
        lead_r, lead_i = _row_table(pw, row, lambda r: r)
        mult = {sh: (jnp.where(row >= sh, pw[sh - 1][0], 0.0), jnp.where(row >= sh, pw[sh - 1][1], 0.0))
                for sh in (1, 2, 4)}

        def step(k, carry):
            cr, ci = carry
            rows = pl.ds(pl.multiple_of(k * SCAN_BLOCK, SCAN_BLOCK), SCAN_BLOCK)
            xr, xi = b_ref[0, rows, :], b_ref[1, rows, :]
            for sh in (1, 2, 4):
                sr, si = pltpu.roll(xr, sh, 0), pltpu.roll(xi, sh, 0)
                kr, ki = mult[sh]
                xr, xi = xr + kr * sr - ki * si, xi + kr * si + ki * sr
            h_ref[0, rows, :] = xr + lead_r * cr - lead_i * ci
            h_ref[1, rows, :] = xi + lead_r * ci + lead_i * cr
            last = row == SCAN_BLOCK - 1
            tr_ = jnp.sum(jnp.where(last, xr, 0.0), axis=0, keepdims=True)
            ti_ = jnp.sum(jnp.where(last, xi, 0.0), axis=0, keepdims=True)
            a8r, a8i = pw[SCAN_BLOCK - 1]
            return a8r * cr - a8i * ci + tr_, a8r * ci + a8i * cr + ti_

        z = jnp.zeros((1, tc), F32)
        lax.fori_loop(0, S // SCAN_BLOCK, step, (z, z), unroll=2)

    vec = pl.BlockSpec((2, 1, tc), lambda j: (0, 0, j))
    mat = pl.BlockSpec((2, S, tc), lambda j: (0, 0, j))
    return _call(
        body, name=name, grid=(nt,), in_specs=[vec, mat], out_specs=mat,
        out_shape=jax.ShapeDtypeStruct((2, S, N), F32),
        compiler_params=_cparams(("parallel",)),
    )(a, bu)


def _s5_scan_bwd(g, h, a, *, name):
    _, S, N = g.shape
    tc = 256
    nt = N // tc

    def body(a_ref, g_ref, h_ref, l_ref, da_ref):
        pw = _cpowers(a_ref[0], a_ref[1], -1.0)
        row = lax.broadcasted_iota(jnp.int32, (SCAN_BLOCK, tc), 0)
        tail_r, tail_i = _row_table(pw, row, lambda r: SCAN_BLOCK - 1 - r)
        nb = S // SCAN_BLOCK
        mult = {sh: (jnp.where(row < SCAN_BLOCK - sh, pw[sh - 1][0], 0.0),
                     jnp.where(row < SCAN_BLOCK - sh, pw[sh - 1][1], 0.0)) for sh in (1, 2, 4)}

        def step(i, carry):
            k = nb - 1 - i
            cr, ci, dar, dai = carry
            rows = pl.ds(pl.multiple_of(k * SCAN_BLOCK, SCAN_BLOCK), SCAN_BLOCK)
            xr, xi = g_ref[0, rows, :], g_ref[1, rows, :]
            for sh in (1, 2, 4):
                sr, si = pltpu.roll(xr, SCAN_BLOCK - sh, 0), pltpu.roll(xi, SCAN_BLOCK - sh, 0)
                kr, ki = mult[sh]
                xr, xi = xr + kr * sr - ki * si, xi + kr * si + ki * sr
            lr = xr + tail_r * cr - tail_i * ci
            li = xi + tail_r * ci + tail_i * cr
            l_ref[0, rows, :] = lr
            l_ref[1, rows, :] = li
            prev = pl.ds(pl.multiple_of(jnp.maximum(k - 1, 0) * SCAN_BLOCK, SCAN_BLOCK), SCAN_BLOCK)
            has_prev = jnp.where(k > 0, 1.0, 0.0).astype(F32)
            first = row == 0
            hpr = jnp.where(first, pltpu.roll(h_ref[0, prev, :], 1, 0) * has_prev, pltpu.roll(h_ref[0, rows, :], 1, 0))
            hpi = jnp.where(first, pltpu.roll(h_ref[1, prev, :], 1, 0) * has_prev, pltpu.roll(h_ref[1, rows, :], 1, 0))
            tr_ = jnp.sum(jnp.where(first, xr, 0.0), axis=0, keepdims=True)
            ti_ = jnp.sum(jnp.where(first, xi, 0.0), axis=0, keepdims=True)
            a8r, a8i = pw[SCAN_BLOCK - 1]
            return (a8r * cr - a8i * ci + tr_, a8r * ci + a8i * cr + ti_,
                    dar + lr * hpr + li * hpi, dai + li * hpr - lr * hpi)

        z = jnp.zeros((1, tc), F32)
        z8 = jnp.zeros((SCAN_BLOCK, tc), F32)
        _, _, dar, dai = lax.fori_loop(0, nb, step, (z, z, z8, z8), unroll=2)
        da_ref[0] = jnp.sum(dar, axis=0, keepdims=True)
        da_ref[1] = jnp.sum(dai, axis=0, keepdims=True)

    vec = pl.BlockSpec((2, 1, tc), lambda j: (0, 0, j))
    mat = pl.BlockSpec((2, S, tc), lambda j: (0, 0, j))
    return _call(
        body, name=name, grid=(nt,), in_specs=[vec, mat, mat], out_specs=[mat, vec],
        out_shape=[jax.ShapeDtypeStruct((2, S, N), F32), jax.ShapeDtypeStruct((2, 1, N), F32)],
        compiler_params=_cparams(("parallel",)),
    )(a, g, h)


_GELU_C = math.sqrt(2.0 / math.pi)


def _s5_out_fwd(yc, P, dskip, *, name):
    S, W = yc.shape
    tr = _pick(S, 256)
    ub = 3 * FOX_WIDTH // W

    def body(yc_ref, u_ref, d_ref, y_ref, yg_ref):
        y = yc_ref[...] + d_ref[...] * u_ref[...]
        y_ref[...] = y
        t = jnp.tanh(_GELU_C * (y + 0.044715 * y * y * y))
        yg_ref[...] = (0.5 * y * (1.0 + t)).astype(BF16)

    row = pl.BlockSpec((tr, W), lambda i: (i, 0))
    return _call(
        body, name=name, grid=(S // tr,),
        in_specs=[row, pl.BlockSpec((tr, W), lambda i: (i, ub)), pl.BlockSpec((1, W), lambda i: (0, 0))],
        out_specs=[row, row],
        out_shape=[jax.ShapeDtypeStruct((S, W), F32), jax.ShapeDtypeStruct((S, W), BF16)],
        compiler_params=_cparams(("parallel",)),
    )(yc, P, dskip)


def _s5_out_bwd(dyg, y, P, dskip, *, name):
    S, W = y.shape
    tr = _pick(S, 256)
    ub = 3 * FOX_WIDTH // W

    def body(dyg_ref, y_ref, u_ref, d_ref, dy_ref, du_ref, dd_ref):
        y_ = y_ref[...]
        inner = _GELU_C * (y_ + 0.044715 * y_ * y_ * y_)
        t = jnp.tanh(inner)
        dgelu = 0.5 * (1.0 + t) + 0.5 * y_ * (1.0 - t * t) * _GELU_C * (1.0 + 3.0 * 0.044715 * y_ * y_)
        dy = dyg_ref[...] * dgelu
        dy_ref[...] = dy.astype(BF16)
        du_ref[...] = d_ref[...] * dy
        part = jnp.sum(dy * u_ref[...], axis=0, keepdims=True)

        @pl.when(pl.program_id(0) == 0)
        def _():
            dd_ref[...] = part

        @pl.when(pl.program_id(0) > 0)
        def _():
            dd_ref[...] += part

    row = pl.BlockSpec((tr, W), lambda i: (i, 0))
    vec = pl.BlockSpec((1, W), lambda i: (0, 0))
    return _call(
        body, name=name, grid=(S // tr,),
        in_specs=[row, row, pl.BlockSpec((tr, W), lambda i: (i, ub)), vec],
        out_specs=[row, row, vec],
        out_shape=[jax.ShapeDtypeStruct((S, W), BF16), jax.ShapeDtypeStruct((S, W), F32),
                   jax.ShapeDtypeStruct((1, W), F32)],
        compiler_params=_cparams(("arbitrary",)),
    )(dyg, y, P, dskip)


def _glu_fwd(z, fox, *, name):
    S, W2 = z.shape
    W = W2 // 2
    tr = _pick(S, 256)

    def body(z1_ref, z2_ref, f_ref, o_ref):
        o_ref[:, :W] = f_ref[...].astype(BF16)
        o_ref[:, W:] = (z1_ref[...] * jax.nn.sigmoid(z2_ref[...])).astype(BF16)

    lo = pl.BlockSpec((tr, W), lambda i: (i, 0))
    return _call(
        body, name=name, grid=(S // tr,),
        in_specs=[lo, pl.BlockSpec((tr, W), lambda i: (i, 1)), lo],
        out_specs=pl.BlockSpec((tr, W2), lambda i: (i, 0)),
        out_shape=jax.ShapeDtypeStruct((S, W2), BF16),
        compiler_params=_cparams(("parallel",)),
    )(z, z, fox)


def _glu_bwd(z, dcat, *, name):
    S, W2 = z.shape
    W = W2 // 2
    tr = _pick(S, 256)

    def body(z1_ref, z2_ref, d_ref, dz_ref):
        sg = jax.nn.sigmoid(z2_ref[...])
        d = d_ref[...]
        dz_ref[:, :W] = (d * sg).astype(BF16)
        dz_ref[:, W:] = (d * z1_ref[...] * sg * (1.0 - sg)).astype(BF16)

    lo = pl.BlockSpec((tr, W), lambda i: (i, 0))
    hi = pl.BlockSpec((tr, W), lambda i: (i, 1))
    return _call(
        body, name=name, grid=(S // tr,), in_specs=[lo, hi, hi],
        out_specs=pl.BlockSpec((tr, W2), lambda i: (i, 0)),
        out_shape=jax.ShapeDtypeStruct((S, W2), BF16),
        compiler_params=_cparams(("parallel",)),
    )(z, z, dcat)


ACT_ROWS = 16
ACT_COLS = 256


def _shift_down(cur, prev, k, row):
    return jnp.where(row >= k, pltpu.roll(cur, k, 0), pltpu.roll(prev, k, 0))


def _shift_up(cur, nxt, k, row):
    n = cur.shape[0]
    return jnp.where(row < n - k, pltpu.roll(cur, n - k, 0), pltpu.roll(nxt, n - k, 0))


def _act_fwd(h, cw, cb, *, name):
    _, S, FP = h.shape
    tr = _pick(S, 256)
    hb = tr // ACT_ROWS
    nq = tr // ACT_ROWS

    def body(g_ref, gh_ref, v_ref, vh_ref, wg_ref, wv_ref, bg_ref, bv_ref, a_ref, hc_ref):
        first = pl.program_id(1) == 0
        for c0 in range(0, FP, ACT_COLS):
            cw_ = min(ACT_COLS, FP - c0)
            cols = pl.ds(c0, cw_)
            rw = lax.broadcasted_iota(jnp.int32, (ACT_ROWS, cw_), 0)
            wg = [wg_ref[pl.ds(k, 1), cols] for k in range(3)]
            wv = [wv_ref[pl.ds(k, 1), cols] for k in range(3)]
            bg, bv = bg_ref[:, cols], bv_ref[:, cols]
            halo_g = jnp.where(first, 0.0, gh_ref[:, cols])
            halo_v = jnp.where(first, 0.0, vh_ref[:, cols])

            def chunk(q, _):
                rows = pl.ds(pl.multiple_of(q * ACT_ROWS, ACT_ROWS), ACT_ROWS)
                before = pl.ds(pl.multiple_of(jnp.maximum(q - 1, 0) * ACT_ROWS, ACT_ROWS), ACT_ROWS)
                g, v = g_ref[rows, cols], v_ref[rows, cols]
                gp = jnp.where(q > 0, g_ref[before, cols], halo_g)
                vp = jnp.where(q > 0, v_ref[before, cols], halo_v)
                cg = bg + wg[2] * g + wg[1] * _shift_down(g, gp, 1, rw) + wg[0] * _shift_down(g, gp, 2, rw)
                cv = bv + wv[2] * v + wv[1] * _shift_down(v, vp, 1, rw) + wv[0] * _shift_down(v, vp, 2, rw)
                a_ref[rows, cols] = (cg * jax.nn.sigmoid(cg) * cv).astype(BF16)
                hc_ref[0, rows, cols] = cg
                hc_ref[1, rows, cols] = cv
                return 0

            lax.fori_loop(0, nq, chunk, 0, unroll=2)

    def main(off):
        return pl.BlockSpec((None, tr, FP), lambda j, i: (j + off, i, 0))

    def halo(off):
        return pl.BlockSpec((None, ACT_ROWS, FP), lambda j, i: (j + off, jnp.maximum(i * hb - 1, 0), 0))

    def wspec(off):
        return pl.BlockSpec((None, 3, FP), lambda j, i: (j + off, 0, 0))

    def bspec(off):
        return pl.BlockSpec((None, 1, FP), lambda j, i: (j + off, 0, 0))

    cb3 = cb.reshape(4, 1, FP)
    return _call(
        body, name=name, grid=(2, S // tr),
        in_specs=[main(0), halo(0), main(2), halo(2), wspec(0), wspec(2), bspec(0), bspec(2)],
        out_specs=[pl.BlockSpec((None, tr, FP), lambda j, i: (j, i, 0)),
                   pl.BlockSpec((None, 2, tr, FP), lambda j, i: (j, 0, i, 0))],
        out_shape=[jax.ShapeDtypeStruct((2, S, FP), BF16), jax.ShapeDtypeStruct((2, 2, S, FP), F32)],
        compiler_params=_cparams(("parallel", "parallel")),
    )(h, h, h, h, cw, cw, cb3, cb3)


def _act_bwd(h, hc, da, cw, *, name):
    _, S, FP = h.shape
    tr = _pick(S, 256)
    nq = tr // ACT_ROWS
    nr = S // tr
    half = ACT_ROWS // 2

    def fold(x):
        return x[:half] + x[half:]

    def body(g_ref, v_ref, hc_ref, da_ref, wg_ref, wv_ref,
             dh_ref, dwg_ref, dwv_ref, dbg_ref, dbv_ref, carry_g, carry_v):
        i = pl.program_id(1)
        bottom = i == 0
        for c0 in range(0, FP, ACT_COLS):
            cw_ = min(ACT_COLS, FP - c0)
            cols = pl.ds(c0, cw_)
            rw = lax.broadcasted_iota(jnp.int32, (ACT_ROWS, cw_), 0)
            wg = [wg_ref[pl.ds(k, 1), cols] for k in range(3)]
            wv = [wv_ref[pl.ds(k, 1), cols] for k in range(3)]
            after_g = jnp.where(bottom, 0.0, carry_g[:, cols])
            after_v = jnp.where(bottom, 0.0, carry_v[:, cols])

            def chunk(s, carry):
                ng, nv, acc = carry[0], carry[1], carry[2:]
                q = nq - 1 - s
                rows = pl.ds(pl.multiple_of(q * ACT_ROWS, ACT_ROWS), ACT_ROWS)
                g, v = g_ref[rows, cols], v_ref[rows, cols]
                cg, cv = hc_ref[0, rows, cols], hc_ref[1, rows, cols]
                sg = jax.nn.sigmoid(cg)
                d = da_ref[rows, cols]
                dcg = d * cv * sg * (1.0 + cg * (1.0 - sg))
                dcv = d * cg * sg
                ug1, ug2 = _shift_up(dcg, ng, 1, rw), _shift_up(dcg, ng, 2, rw)
                uv1, uv2 = _shift_up(dcv, nv, 1, rw), _shift_up(dcv, nv, 2, rw)
                dh_ref[0, rows, cols] = (wg[2] * dcg + wg[1] * ug1 + wg[0] * ug2).astype(BF16)
                dh_ref[1, rows, cols] = (wv[2] * dcv + wv[1] * uv1 + wv[0] * uv2).astype(BF16)
                terms = (ug2 * g, ug1 * g, dcg * g, dcg, uv2 * v, uv1 * v, dcv * v, dcv)
                return (dcg, dcv) + tuple(a + fold(t) for a, t in zip(acc, terms))

            zero = jnp.zeros((half, cw_), F32)
            out = lax.fori_loop(0, nq, chunk, (after_g, after_v) + (zero,) * 8, unroll=2)
            carry_g[:, cols] = out[0]
            carry_v[:, cols] = out[1]
            sums = [jnp.sum(a, axis=0, keepdims=True) for a in out[2:]]

            @pl.when(bottom)
            def _():
                for k in range(3):
                    dwg_ref[pl.ds(k, 1), cols] = sums[k]
                    dwv_ref[pl.ds(k, 1), cols] = sums[4 + k]
                dbg_ref[:, cols] = sums[3]
                dbv_ref[:, cols] = sums[7]

            @pl.when(jnp.logical_not(bottom))
            def _():
                for k in range(3):
                    dwg_ref[pl.ds(k, 1), cols] += sums[k]
                    dwv_ref[pl.ds(k, 1), cols] += sums[4 + k]
                dbg_ref[:, cols] += sums[3]
                dbv_ref[:, cols] += sums[7]

    def main(off):
        return pl.BlockSpec((None, tr, FP), lambda j, i: (j + off, nr - 1 - i, 0))

    def wspec(off):
        return pl.BlockSpec((None, 3, FP), lambda j, i: (j + off, 0, 0))

    bspec = pl.BlockSpec((None, 1, FP), lambda j, i: (j, 0, 0))
    pair = pl.BlockSpec((None, 2, tr, FP), lambda j, i: (j, 0, nr - 1 - i, 0))
    dh, dwg, dwv, dbg, dbv = _call(
        body, name=name, grid=(2, nr),
        in_specs=[main(0), main(2), pair, main(0), wspec(0), wspec(2)],
        out_specs=[pair, wspec(0), wspec(0), bspec, bspec],
        out_shape=[jax.ShapeDtypeStruct((2, 2, S, FP), BF16)]
        + [jax.ShapeDtypeStruct((2, 3, FP), F32)] * 2 + [jax.ShapeDtypeStruct((2, 1, FP), F32)] * 2,
        scratch_shapes=[pltpu.VMEM((ACT_ROWS, FP), F32), pltpu.VMEM((ACT_ROWS, FP), F32)],
        compiler_params=_cparams(("parallel", "arbitrary")),
    )(h, h, hc, da, cw, cw)
    return (dh.reshape(4, S, FP), jnp.concatenate([dwg, dwv], axis=0), jnp.concatenate([dbg, dbv], axis=0))


def _rope_tables(posf, *, name, after=()):
    S = posf.shape[0]
    half = ROPE_DIM // 2
    d = np.arange(LANE) % SWA_HEAD_DIM
    invf = np.where(d < ROPE_DIM, ROPE_THETA ** (-(d % half).astype(np.float64) / half), 0.0).astype(np.float32)
    m_rot = (d < ROPE_DIM).astype(np.float32)
    m_a = (d < half).astype(np.float32)
    m_b = ((d >= half) & (d < ROPE_DIM)).astype(np.float32)
    consts = jnp.asarray(np.stack([invf, m_rot, m_a, m_b] + [np.zeros(LANE, np.float32)] * 4))

    def body(p_ref, k_ref, c_ref, sa_ref, sb_ref):
        k = k_ref[...]
        ang = p_ref[...] * k[0:1]
        co, si = jnp.cos(ang), jnp.sin(ang)
        c_ref[...] = k[1:2] * co + (1.0 - k[1:2])
        sa_ref[...] = -k[2:3] * si
        sb_ref[...] = k[3:4] * si

    full = pl.BlockSpec((S, LANE), lambda: (0, 0))
    return _call(
        body, after=after, name=name,
        in_specs=[pl.BlockSpec((S, 1), lambda: (0, 0)), pl.BlockSpec((8, LANE), lambda: (0, 0))],
        out_specs=[full] * 3, out_shape=[jax.ShapeDtypeStruct((S, LANE), F32)] * 3,
    )(posf, consts)


def _rope(xv, tabs_refs, width, inverse):
    rep = width // LANE
    c, sa, sb = (jnp.tile(t[...], (1, rep)) for t in tabs_refs)
    if not inverse:
        return xv * c + pltpu.roll(xv, width - 8, 1) * sa + pltpu.roll(xv, 8, 1) * sb
    return xv * c + pltpu.roll(xv * sa, 8, 1) + pltpu.roll(xv * sb, width - 8, 1)


def _to_heads(x, tabs, *, col0, width, rotate, name, out_dtype):
    S = x.shape[0]
    tr = _pick(S, 256)
    nh = width // SWA_HEAD_DIM
    cb = col0 // width

    def body(x_ref, c_ref, sa_ref, sb_ref, o_ref):
        xv = x_ref[...].astype(F32)
        if rotate:
            xv = _rope(xv, (c_ref, sa_ref, sb_ref), width, False)
        for h in range(nh):
            o_ref[h] = xv[:, h * SWA_HEAD_DIM:(h + 1) * SWA_HEAD_DIM].astype(out_dtype)

    tab = pl.BlockSpec((tr, LANE), lambda i: (i, 0))
    return _call(
        body, name=name, grid=(S // tr,),
        in_specs=[pl.BlockSpec((tr, width), lambda i: (i, cb)), tab, tab, tab],
        out_specs=pl.BlockSpec((nh, tr, SWA_HEAD_DIM), lambda i: (0, i, 0)),
        out_shape=jax.ShapeDtypeStruct((nh, S, SWA_HEAD_DIM), out_dtype),
        compiler_params=_cparams(("parallel",)),
    )(x, *tabs)


def _from_heads(x3, tabs, *, rotate_back, name, out_dtype, skip_rows=0):
    nh = x3.shape[0]
    S = x3.shape[1] - skip_rows
    width = nh * SWA_HEAD_DIM
    tr = _pick(S, 256) if skip_rows == 0 else skip_rows
    off = skip_rows // tr

    def body(x_ref, c_ref, sa_ref, sb_ref, o_ref):
        xv = jnp.concatenate([x_ref[h].astype(F32) for h in range(nh)], axis=1)
        if rotate_back:
            xv = _rope(xv, (c_ref, sa_ref, sb_ref), width, True)
        o_ref[...] = xv.astype(out_dtype)

    tab = pl.BlockSpec((tr, LANE), lambda i: (i, 0))
    return _call(
        body, name=name, grid=(S // tr,),
        in_specs=[pl.BlockSpec((nh, tr, SWA_HEAD_DIM), lambda i: (0, i + off, 0)), tab, tab, tab],
        out_specs=pl.BlockSpec((tr, width), lambda i: (i, 0)),
        out_shape=jax.ShapeDtypeStruct((S, width), out_dtype),
        compiler_params=_cparams(("parallel",)),
    )(x3, *tabs)


def _swa_mask(n):
    rows = SWA_GROUPS * SWA_WINDOW
    qi = lax.broadcasted_iota(jnp.int32, (rows, 2 * SWA_WINDOW), 0) & (SWA_WINDOW - 1)
    kj = lax.broadcasted_iota(jnp.int32, (rows, 2 * SWA_WINDOW), 1)
    rel = SWA_WINDOW + qi - kj
    return (rel >= 0) & (rel < SWA_WINDOW) & ((n > 0) | (kj >= SWA_WINDOW))


def _swa_fwd(qT, kT, vT, sink_rows, *, name):
    S = qT.shape[1]
    W, G, Dh = SWA_WINDOW, SWA_GROUPS, SWA_HEAD_DIM
    nb = S // W
    scale = 1.0 / math.sqrt(Dh)

    def body(q_ref, kp_ref, kc_ref, vp_ref, vc_ref, s_ref, o_ref, l_ref):
        n = pl.program_id(1)
        q = q_ref[...].reshape(G * W, Dh)
        kk = jnp.concatenate([kp_ref[...], kc_ref[...]], axis=0)
        vv = jnp.concatenate([vp_ref[...], vc_ref[...]], axis=0)
        s = lax.dot_general(q, kk, (((1,), (1,)), ((), ())), preferred_element_type=F32) * scale
        s = jnp.where(_swa_mask(n), s, -1e30)
        sink = s_ref[...]
        m = jnp.maximum(jnp.max(s, axis=-1, keepdims=True), sink)
        e = jnp.exp(s - m)
        den = jnp.sum(e, axis=-1, keepdims=True) + jnp.exp(sink - m)
        p = e * (1.0 / den)
        o_ref[...] = jnp.dot(p.astype(BF16), vv, preferred_element_type=F32).reshape(G, W, Dh)
        l_ref[...] = (m + jnp.log(den)).reshape(G, W, 1)

    qs = pl.BlockSpec((G, W, Dh), lambda g, n: (g, n, 0))
    prev = pl.BlockSpec((None, W, Dh), lambda g, n: (g, jnp.maximum(n - 1, 0), 0))
    cur = pl.BlockSpec((None, W, Dh), lambda g, n: (g, n, 0))
    return _call(
        body, name=name, grid=(SWA_KV_HEADS, nb),
        in_specs=[qs, prev, cur, prev, cur, pl.BlockSpec((None, G * W, 1), lambda g, n: (g, 0, 0))],
        out_specs=[qs, pl.BlockSpec((G, W, 1), lambda g, n: (g, n, 0))],
        out_shape=[jax.ShapeDtypeStruct((SWA_HEADS, S, Dh), F32), jax.ShapeDtypeStruct((SWA_HEADS, S, 1), F32)],
        compiler_params=_cparams(("parallel", "parallel")),
    )(qT, kT, kT, vT, vT, sink_rows)


def _swa_bwd(qT, kT, vT, sink_rows, oT, L, doT, *, name):
    S = qT.shape[1]
    W, G, Dh = SWA_WINDOW, SWA_GROUPS, SWA_HEAD_DIM
    nb = S // W
    scale = 1.0 / math.sqrt(Dh)

    def body(q_ref, kp_ref, kc_ref, vp_ref, vc_ref, s_ref, o_ref, l_ref, do_ref,
             dq_ref, dk_ref, dv_ref, ds_ref):
        n = pl.program_id(1)
        q = q_ref[...].reshape(G * W, Dh)
        kk = jnp.concatenate([kp_ref[...], kc_ref[...]], axis=0)
        vv = jnp.concatenate([vp_ref[...], vc_ref[...]], axis=0)
        s = lax.dot_general(q, kk, (((1,), (1,)), ((), ())), preferred_element_type=F32) * scale
        lrow = l_ref[...].reshape(G * W, 1)
        p = jnp.where(_swa_mask(n), jnp.exp(s - lrow), 0.0)
        do = do_ref[...].reshape(G * W, Dh)
        do_bf = do.astype(BF16)
        dp = lax.dot_general(do_bf, vv, (((1,), (1,)), ((), ())), preferred_element_type=F32)
        delta = jnp.sum(do * o_ref[...].reshape(G * W, Dh), axis=-1, keepdims=True)
        dsc = p * (dp - delta)
        ds_bf = dsc.astype(BF16)
        dq_ref[...] = (jnp.dot(ds_bf, kk, preferred_element_type=F32) * scale).astype(BF16).reshape(G, W, Dh)
        dkk = lax.dot_general(ds_bf, q, (((0,), (0,)), ((), ())), preferred_element_type=F32) * scale
        dvv = lax.dot_general(p.astype(BF16), do_bf, (((0,), (0,)), ((), ())), preferred_element_type=F32)
        dsk = -jnp.exp(s_ref[...] - lrow) * delta
        dsk = jnp.broadcast_to(jnp.sum(dsk.reshape(G, W, 1), axis=1), (G, LANE))

        @pl.when(n == 0)
        def _():
            dk_ref[...] = jnp.zeros_like(dk_ref)
            dv_ref[...] = jnp.zeros_like(dv_ref)
            ds_ref[...] = jnp.zeros_like(ds_ref)

        rows = pl.ds(pl.multiple_of(n * W, W), 2 * W)
        dk_ref[rows, :] += dkk
        dv_ref[rows, :] += dvv
        ds_ref[...] += dsk

    qs = pl.BlockSpec((G, W, Dh), lambda g, n: (g, n, 0))
    prev = pl.BlockSpec((None, W, Dh), lambda g, n: (g, jnp.maximum(n - 1, 0), 0))
    cur = pl.BlockSpec((None, W, Dh), lambda g, n: (g, n, 0))
    lsp = pl.BlockSpec((G, W, 1), lambda g, n: (g, n, 0))
    kvo = pl.BlockSpec((None, S + W, Dh), lambda g, n: (g, 0, 0))
    return _call(
        body, name=name, grid=(SWA_KV_HEADS, nb),
        in_specs=[qs, prev, cur, prev, cur, pl.BlockSpec((None, G * W, 1), lambda g, n: (g, 0, 0)), qs, lsp, qs],
        out_specs=[qs, kvo, kvo, pl.BlockSpec((None, G, LANE), lambda g, n: (g, 0, 0))],
        out_shape=[jax.ShapeDtypeStruct((SWA_HEADS, S, Dh), BF16),
                   jax.ShapeDtypeStruct((SWA_KV_HEADS, S + W, Dh), F32),
                   jax.ShapeDtypeStruct((SWA_KV_HEADS, S + W, Dh), F32),
                   jax.ShapeDtypeStruct((SWA_KV_HEADS, G, LANE), F32)],
        compiler_params=_cparams(("parallel", "arbitrary")),
    )(qT, kT, kT, vT, vT, sink_rows, oT, L, doT)


def _adamw(w, g, m, v, *, name, tr=128, by_cols=False):
    L, R, C = w.shape
    split = isinstance(g, (list, tuple))
    HR, HC = _half_shape(R, C, by_cols) if split else (R, C)
    tr, tc = _tile2d(HR, HC, tr)
    nr, nc = HR // tr, HC // tc
    c1 = 1.0 / (1.0 - ADAM_B1 ** ADAM_STEP)
    c2 = 1.0 / (1.0 - ADAM_B2 ** ADAM_STEP)
    ng = 2 * L if split else 1

    def body(c_ref, *refs):
        w_ref, g_refs, (m_ref, v_ref, go_ref, d_ref, mo_ref, vo_ref) = refs[0], refs[1:1 + ng], refs[1 + ng:]
        if split:
            mine = pl.program_id(1) == c_ref[0]
            g_ = jnp.where(mine, g_refs[0][...], g_refs[1][...])
            for l in range(1, L):
                g_ = jnp.where(pl.program_id(0) == l,
                               jnp.where(mine, g_refs[2 * l][...], g_refs[2 * l + 1][...]), g_)
        else:
            g_ = g_refs[0][...]
        mn = ADAM_B1 * m_ref[...] + (1.0 - ADAM_B1) * g_
        vn = ADAM_B2 * v_ref[...] + (1.0 - ADAM_B2) * (g_ * g_)
        go_ref[...] = g_
        mo_ref[...] = mn
        vo_ref[...] = vn
        d_ref[...] = -ADAM_LR * ((mn * c1) / (jnp.sqrt(vn * c2) + ADAM_EPS) + ADAM_WD * w_ref[...])

    def whole(l, hf, i, j, c):
        return (l, i, hf * nc + j) if by_cols else (l, hf * nr + i, j)

    def half(layer, own):
        def index(l, hf, i, j, c):
            used = (l == layer) & ((hf == c[0]) if own else (hf != c[0]))
            return jnp.where(used, i, 0), jnp.where(used, j, 0)
        return pl.BlockSpec((tr, tc), index)

    row = pl.BlockSpec((None, tr, tc), whole)
    gs = [h for pair in g for h in pair] if split else [g]
    g_specs = [half(l, own) for l in range(L) for own in (True, False)] if split else [row]
    core = lax.axis_index("c").astype(jnp.int32).reshape(1)
    return _call(
        body, name=name,
        grid_spec=pltpu.PrefetchScalarGridSpec(
            num_scalar_prefetch=1, grid=(L, 2 if split else 1, nr, nc),
            in_specs=[row] + g_specs + [row, row], out_specs=[row] * 4),
        out_shape=[jax.ShapeDtypeStruct((L, R, C), F32)] * 4,
        compiler_params=_cparams(("parallel",) * 4),
    )(core, w, *gs, m, v)


def _adamw_half(w, g, m, v, *, name, own, prev=None, tr=128, by_cols=False):
    L, R, C = w.shape
    HR, HC = _half_shape(R, C, by_cols)
    tr, tc = _tile2d(HR, HC, tr)
    nr, nc = HR // tr, HC // tc
    c1 = 1.0 / (1.0 - ADAM_B1 ** ADAM_STEP)
    c2 = 1.0 / (1.0 - ADAM_B2 ** ADAM_STEP)

    def body(c_ref, *refs):
        w_ref, g_refs, m_ref, v_ref = refs[0], refs[1:1 + L], refs[1 + L], refs[2 + L]
        go_ref, d_ref, mo_ref, vo_ref = refs[-4:]
        g_ = g_refs[0][...]
        for l in range(1, L):
            g_ = jnp.where(pl.program_id(0) == l, g_refs[l][...], g_)
        mn = ADAM_B1 * m_ref[...] + (1.0 - ADAM_B1) * g_
        vn = ADAM_B2 * v_ref[...] + (1.0 - ADAM_B2) * (g_ * g_)
        go_ref[...] = g_
        mo_ref[...] = mn
        vo_ref[...] = vn
        d_ref[...] = -ADAM_LR * ((mn * c1) / (jnp.sqrt(vn * c2) + ADAM_EPS) + ADAM_WD * w_ref[...])

    def whole(l, i, j, c):
        hf = c[0] if own else 1 - c[0]
        return (l, i, hf * nc + j) if by_cols else (l, hf * nr + i, j)

    def layer_half(layer):
        def index(l, i, j, c):
            return jnp.where(l == layer, i, 0), jnp.where(l == layer, j, 0)
        return pl.BlockSpec((tr, tc), index)

    row = pl.BlockSpec((None, tr, tc), whole)
    core = lax.axis_index("c").astype(jnp.int32).reshape(1)
    prev = list(prev) if prev is not None else []
    return _call(
        body, name=name,
        grid_spec=pltpu.PrefetchScalarGridSpec(
            num_scalar_prefetch=1, grid=(L, nr, nc),
            in_specs=[row] + [layer_half(l) for l in range(L)] + [row, row] + [ANY] * len(prev),
            out_specs=[row] * 4),
        out_shape=[jax.ShapeDtypeStruct((L, R, C), F32)] * 4,
        input_output_aliases={4 + L + k: k for k in range(len(prev))},
        compiler_params=_cparams(("parallel",) * 3),
    )(core, w, *g, m, v, *prev)


def _sum2_halves(g4, s4, by_cols, *, name):
    n, R, C = g4.shape
    HR, HC = _half_shape(R, C, by_cols)
    tr, tc = _tile2d(HR, HC, budget=1024 * 1024)
    nr, nc = HR // tr, HC // tc
    core = lax.axis_index("c").astype(jnp.int32).reshape(1)

    def body(c_ref, g_ref, s_ref, o_ref):
        o_ref[...] = (g_ref[...].astype(F32) + s_ref[...].astype(F32)).astype(BF16)

    def mine(k, i, j, c):
        return (k, i, c[0] * nc + j) if by_cols else (k, c[0] * nr + i, j)

    blk = pl.BlockSpec((None, tr, tc), lambda k, i, j, c: (k, i, j))
    return _call(
        body, name=name,
        grid_spec=pltpu.PrefetchScalarGridSpec(
            num_scalar_prefetch=1, grid=(n, nr, nc),
            in_specs=[pl.BlockSpec((None, tr, tc), mine), blk], out_specs=blk),
        out_shape=jax.ShapeDtypeStruct((n, HR, HC), BF16),
        compiler_params=_cparams(("parallel", "parallel", "parallel")),
    )(core, g4, s4)


def _rowsum(parts, *, name, out_dtype=F32):
    n, R, C = parts.shape
    tr, tc = _tile2d(R, C, budget=512 * 1024)

    def body(p_ref, o_ref):
        acc = p_ref[0].astype(F32)
        for i in range(1, n):
            acc = acc + p_ref[i].astype(F32)
        o_ref[...] = acc.astype(out_dtype)

    return _call(
        body, name=name, grid=(R // tr, C // tc),
        in_specs=[pl.BlockSpec((n, tr, tc), lambda i, j: (0, i, j))],
        out_specs=pl.BlockSpec((tr, tc), lambda i, j: (i, j)),
        out_shape=jax.ShapeDtypeStruct((R, C), out_dtype),
        compiler_params=_cparams(("parallel", "parallel")),
    )(parts)


def _where_am_i():
    x, y, c = lax.axis_index("x"), lax.axis_index("y"), lax.axis_index("c")
    chips = [(1 - x, y), (x, 1 - y), (1 - x, 1 - y)]
    return x, y, c, chips


def _half_idx(rows, cols, by_cols, which):
    if by_cols:
        hc = cols // 2
        return (slice(None), pl.ds(pl.multiple_of(which * hc, LANE), hc))
    hr = rows // 2
    return (pl.ds(pl.multiple_of(which * hr, 16), hr), slice(None))


def _half_shape(rows, cols, by_cols):
    return (rows, cols // 2) if by_cols else (rows // 2, cols)


HBM_SPEC = pl.BlockSpec(memory_space=pltpu.HBM)
SEM_SPEC = pl.BlockSpec(memory_space=pltpu.SEMAPHORE)
DATAFLOW = pltpu.SideEffectType.DATAFLOW_SIDE_EFFECTING


def _chip_exchange_refs(kind, shards_shape, by_cols, src, land, i, chip_k, c, me):
    if kind == 'gather':
        half = _half_idx(*shards_shape, by_cols, c)
        return src.at[half], land.at[(me,) + half], land.at[(chip_k,) + half]
    return src.at[chip_k], land.at[me], land.at[chip_k]


def _chip_exchange_start(kind, srcs, by_cols, *, name, after=()):
    n = len(srcs)
    land_shapes = [((N_CHIPS,) + s.shape) if kind == 'gather' else s.shape for s in srcs]

    def body(*refs):
        src_refs, land_refs = refs[:n], refs[n:2 * n]
        send, recv = refs[2 * n + len(after)], refs[2 * n + len(after) + 1]
        token = refs[-1]
        x, y, c, chips = _where_am_i()
        me = 2 * x + y
        for i in range(n):
            for k, (px, py) in enumerate(chips):
                s, d, _ = _chip_exchange_refs(kind, srcs[i].shape, by_cols[i], src_refs[i], land_refs[i], i,
                                              2 * px + py, c, me)
                pltpu.make_async_remote_copy(src_ref=s, dst_ref=d, send_sem=send.at[3 * i + k],
                                             recv_sem=recv.at[3 * i + k], device_id=(px, py, c),
                                             device_id_type=MESH).start()
        token[...] = jnp.zeros_like(token)

    lands = [pltpu.with_memory_space_constraint(lax.empty(sh, s.dtype), pltpu.HBM) for sh, s in zip(land_shapes, srcs)]
    outs = _call(
        body, name=name,
        out_shape=(pltpu.SemaphoreType.DMA((3 * n,)), pltpu.SemaphoreType.DMA((3 * n,)),
                   *[pltpu.HBM(s.shape, s.dtype) for s in srcs],
                   *[pltpu.HBM(sh, s.dtype) for sh, s in zip(land_shapes, srcs)],
                   jax.ShapeDtypeStruct((8, LANE), F32)),
        in_specs=[HBM_SPEC] * (2 * n) + [ANY] * len(after),
        out_specs=(SEM_SPEC, SEM_SPEC, *([HBM_SPEC] * (2 * n)), pl.BlockSpec(memory_space=pltpu.VMEM)),
        input_output_aliases={j: 2 + j for j in range(2 * n)},
        compiler_params=pltpu.CompilerParams(has_side_effects=DATAFLOW),
    )(*[pltpu.with_memory_space_constraint(s, pltpu.HBM) for s in srcs], *lands, *after)
    return outs[0], outs[1], list(outs[2:2 + n]), list(outs[2 + n:2 + 2 * n]), outs[-1]


def _chip_exchange_wait(kind, send, recv, srcs, lands, by_cols, after, *, name):
    n = len(srcs)

    def body(*refs):
        src_refs, land_refs = refs[:n], refs[n:2 * n]
        send_r, recv_r = refs[2 * n], refs[2 * n + 1]
        x, y, c, chips = _where_am_i()
        me = 2 * x + y
        for i in range(n):
            for k, (px, py) in enumerate(chips):
                s, _, d = _chip_exchange_refs(kind, srcs[i].shape, by_cols[i], src_refs[i], land_refs[i], i,
                                              2 * px + py, c, me)
                cp = pltpu.make_async_remote_copy(src_ref=s, dst_ref=d, send_sem=send_r.at[3 * i + k],
                                                  recv_sem=recv_r.at[3 * i + k], device_id=(px, py, c),
                                                  device_id_type=MESH)
                cp.wait_send()
                cp.wait_recv()

    outs = _call(
        body, name=name,
        out_shape=(*[pltpu.HBM(s.shape, s.dtype) for s in srcs], *[pltpu.HBM(l.shape, l.dtype) for l in lands]),
        in_specs=[HBM_SPEC] * (2 * n) + [SEM_SPEC, SEM_SPEC] + [ANY] * len(after),
        out_specs=tuple([HBM_SPEC] * (2 * n)),
        input_output_aliases={j: j for j in range(2 * n)},
        compiler_params=pltpu.CompilerParams(has_side_effects=DATAFLOW),
    )(*srcs, *lands, send, recv, *after)
    return list(outs[:n]), list(outs[n:])


def _sibling_halves_start(grads, by_cols, *, name, after=()):
    n = len(grads)
    land_shapes = [(N_CHIPS,) + _half_shape(*g.shape[1:], bc) for g, bc in zip(grads, by_cols)]

    def body(*refs):
        src_refs, land_refs = refs[:n], refs[n:2 * n]
        send, recv = refs[2 * n + len(after)], refs[2 * n + len(after) + 1]
        token = refs[-1]
        x, y, c, _ = _where_am_i()
        for i in range(n):
            src = src_refs[i].at[(slice(None),) + _half_idx(*grads[i].shape[1:], by_cols[i], 1 - c)]
            pltpu.make_async_remote_copy(src_ref=src, dst_ref=land_refs[i], send_sem=send.at[i], recv_sem=recv.at[i],
                                         device_id=(x, y, 1 - c), device_id_type=MESH).start()
        token[...] = jnp.zeros_like(token)

    lands = [pltpu.with_memory_space_constraint(lax.empty(sh, g.dtype), pltpu.HBM) for sh, g in zip(land_shapes, grads)]
    outs = _call(
        body, name=name,
        out_shape=(pltpu.SemaphoreType.DMA((n,)), pltpu.SemaphoreType.DMA((n,)),
                   *[pltpu.HBM(g.shape, g.dtype) for g in grads],
                   *[pltpu.HBM(sh, g.dtype) for sh, g in zip(land_shapes, grads)],
                   jax.ShapeDtypeStruct((8, LANE), F32)),
        in_specs=[HBM_SPEC] * (2 * n) + [ANY] * len(after),
        out_specs=(SEM_SPEC, SEM_SPEC, *([HBM_SPEC] * (2 * n)), pl.BlockSpec(memory_space=pltpu.VMEM)),
        input_output_aliases={j: 2 + j for j in range(2 * n)},
        compiler_params=pltpu.CompilerParams(has_side_effects=DATAFLOW),
    )(*[pltpu.with_memory_space_constraint(g, pltpu.HBM) for g in grads], *lands, *after)
    return outs[0], outs[1], list(outs[2:2 + n]), list(outs[2 + n:2 + 2 * n]), outs[-1]


def _sibling_halves_wait(send, recv, grads, lands, by_cols, after, *, name):
    n = len(grads)

    def body(*refs):
        src_refs, land_refs = refs[:n], refs[n:2 * n]
        send_r, recv_r = refs[2 * n], refs[2 * n + 1]
        x, y, c, _ = _where_am_i()
        for i in range(n):
            src = src_refs[i].at[(slice(None),) + _half_idx(*grads[i].shape[1:], by_cols[i], 1 - c)]
            cp = pltpu.make_async_remote_copy(src_ref=src, dst_ref=land_refs[i], send_sem=send_r.at[i],
                                              recv_sem=recv_r.at[i], device_id=(x, y, 1 - c), device_id_type=MESH)
            cp.wait_send()
            cp.wait_recv()

    outs = _call(
        body, name=name,
        out_shape=(*[pltpu.HBM(g.shape, g.dtype) for g in grads], *[pltpu.HBM(l.shape, l.dtype) for l in lands]),
        in_specs=[HBM_SPEC] * (2 * n) + [SEM_SPEC, SEM_SPEC] + [ANY] * len(after),
        out_specs=tuple([HBM_SPEC] * (2 * n)),
        input_output_aliases={j: j for j in range(2 * n)},
        compiler_params=pltpu.CompilerParams(has_side_effects=DATAFLOW),
    )(*grads, *lands, send, recv, *after)
    return list(outs[:n]), list(outs[n:])


def _sibling_swap_start(arrs, *, name, after=()):
    n = len(arrs)

    def body(*refs):
        src_refs, land_refs = refs[:n], refs[n:2 * n]
        send, recv = refs[2 * n + len(after)], refs[2 * n + len(after) + 1]
        token = refs[-1]
        x, y, c, _ = _where_am_i()
        for i in range(n):
            pltpu.make_async_remote_copy(src_ref=src_refs[i], dst_ref=land_refs[i], send_sem=send.at[i],
                                         recv_sem=recv.at[i], device_id=(x, y, 1 - c), device_id_type=MESH).start()
        token[...] = jnp.zeros_like(token)

    lands = [pltpu.with_memory_space_constraint(lax.empty(a.shape, a.dtype), pltpu.HBM) for a in arrs]
    outs = _call(
        body, name=name,
        out_shape=(pltpu.SemaphoreType.DMA((n,)), pltpu.SemaphoreType.DMA((n,)),
                   *[pltpu.HBM(a.shape, a.dtype) for a in arrs] * 2, jax.ShapeDtypeStruct((8, LANE), F32)),
        in_specs=[HBM_SPEC] * (2 * n) + [ANY] * len(after),
        out_specs=(SEM_SPEC, SEM_SPEC, *([HBM_SPEC] * (2 * n)), pl.BlockSpec(memory_space=pltpu.VMEM)),
        input_output_aliases={j: 2 + j for j in range(2 * n)},
        compiler_params=pltpu.CompilerParams(has_side_effects=DATAFLOW),
    )(*[pltpu.with_memory_space_constraint(a, pltpu.HBM) for a in arrs], *lands, *after)
    return outs[0], outs[1], list(outs[2:2 + n]), list(outs[2 + n:2 + 2 * n]), outs[-1]


def _sibling_swap_wait(send, recv, arrs, lands, after, *, name):
    n = len(arrs)

    def body(*refs):
        src_refs, land_refs = refs[:n], refs[n:2 * n]
        send_r, recv_r = refs[2 * n], refs[2 * n + 1]
        x, y, c, _ = _where_am_i()
        for i in range(n):
            cp = pltpu.make_async_remote_copy(src_ref=src_refs[i], dst_ref=land_refs[i], send_sem=send_r.at[i],
                                              recv_sem=recv_r.at[i], device_id=(x, y, 1 - c), device_id_type=MESH)
            cp.wait_send()
            cp.wait_recv()

    outs = _call(
        body, name=name,
        out_shape=tuple(pltpu.HBM(a.shape, a.dtype) for a in list(arrs) + list(lands)),
        in_specs=[HBM_SPEC] * (2 * n) + [SEM_SPEC, SEM_SPEC] + [ANY] * len(after),
        out_specs=tuple([HBM_SPEC] * (2 * n)),
        input_output_aliases={j: j for j in range(2 * n)},
        compiler_params=pltpu.CompilerParams(has_side_effects=DATAFLOW),
    )(*arrs, *lands, send, recv, *after)
    return list(outs[:n]), list(outs[n:])


def _sibling_pass_gathered(lands, shard_shapes, by_cols, *, name):
    n = len(lands)

    def body(*refs):
        outs = refs[n:2 * n]
        send, recv = refs[2 * n:]
        x, y, c, chips = _where_am_i()
        sibling = (x, y, 1 - c)
        cps = []
        for i in range(n):
            for k, (px, py) in enumerate(chips):
                blk = outs[i].at[(2 * px + py,) + _half_idx(*shard_shapes[i], by_cols[i], c)]
                d = pltpu.make_async_remote_copy(src_ref=blk, dst_ref=blk, send_sem=send.at[i, k],
                                                 recv_sem=recv.at[i, k], device_id=sibling, device_id_type=MESH)
                d.start()
                cps.append(d)
        for i in range(n):
            for k, (px, py) in enumerate(chips):
                blk = outs[i].at[(2 * px + py,) + _half_idx(*shard_shapes[i], by_cols[i], 1 - c)]
                pltpu.make_async_remote_copy(src_ref=blk, dst_ref=blk, send_sem=send.at[i, k], recv_sem=recv.at[i, k],
                                             device_id=sibling, device_id_type=MESH).wait_recv()
        for d in cps:
            d.wait_send()

    return _call(
        body, name=name, in_specs=[ANY] * n, out_specs=[ANY] * n,
        out_shape=[jax.ShapeDtypeStruct(l.shape, l.dtype) for l in lands],
        input_output_aliases={j: j for j in range(n)},
        scratch_shapes=[pltpu.SemaphoreType.DMA((n, 3)), pltpu.SemaphoreType.DMA((n, 3))],
    )(*lands)


def _own_slot(lands, owns):
    me = 2 * lax.axis_index("x") + lax.axis_index("y")
    return [lax.dynamic_update_slice_in_dim(g, s, me, axis=0) for g, s in zip(lands, owns)]


def _sibling_send_halves(grads, by_cols, *, name):
    n = len(grads)

    def body(*refs):
        ins, outs = refs[:n], refs[n:2 * n]
        send, recv = refs[2 * n:]
        x, y, c, _ = _where_am_i()
        sibling = (x, y, 1 - c)
        cps = []
        for i in range(n):
            src = ins[i].at[(slice(None),) + _half_idx(*grads[i].shape[1:], by_cols[i], 1 - c)]
            d = pltpu.make_async_remote_copy(src_ref=src, dst_ref=outs[i], send_sem=send.at[i],
                                             recv_sem=recv.at[i], device_id=sibling, device_id_type=MESH)
            d.start()
            cps.append(d)
        for d in cps:
            d.wait()

    return _call(
        body, name=name, in_specs=[ANY] * n, out_specs=[ANY] * n,
        out_shape=[jax.ShapeDtypeStruct((N_CHIPS,) + _half_shape(*g.shape[1:], bc), g.dtype)
                   for g, bc in zip(grads, by_cols)],
        scratch_shapes=[pltpu.SemaphoreType.DMA((n,)), pltpu.SemaphoreType.DMA((n,))],
    )(*grads)


def _all_reduce_small(v, *, name, after=()):
    R, C = v.shape
    H = R // 2

    def body(v_ref, o_ref, sib, slots, send, recv):
        x, y, c, chips = _where_am_i()
        me = 2 * x + y
        sibling = (x, y, 1 - c)
        mine = pl.ds(pl.multiple_of(c * H, 8), H)
        other = pl.ds(pl.multiple_of((1 - c) * H, 8), H)

        def copy(k, src, dst, to):
            return pltpu.make_async_remote_copy(src_ref=src, dst_ref=dst, send_sem=send.at[k], recv_sem=recv.at[k],
                                                device_id=to, device_id_type=MESH)

        d = copy(0, v_ref.at[other], sib, sibling)
        d.start()
        d.wait()
        slots[me] = v_ref[mine, :] + sib[...]
        cps = [copy(1 + k, slots.at[me], slots.at[me], (px, py, c)) for k, (px, py) in enumerate(chips)]
        for d in cps:
            d.start()
        for k, (px, py) in enumerate(chips):
            blk = slots.at[2 * px + py]
            copy(1 + k, blk, blk, (px, py, c)).wait_recv()
        for d in cps:
            d.wait_send()
        o_ref[mine, :] = (slots[0] + slots[1]) + (slots[2] + slots[3])
        d = copy(4, o_ref.at[mine], o_ref.at[mine], sibling)
        d.start()
        copy(4, o_ref.at[other], o_ref.at[other], sibling).wait_recv()
        d.wait_send()

    vm = pl.BlockSpec(memory_space=pltpu.VMEM)
    return _call(
        body, after=after, name=name, in_specs=[vm], out_specs=vm,
        out_shape=jax.ShapeDtypeStruct((R, C), F32),
        scratch_shapes=[pltpu.VMEM((H, C), F32), pltpu.VMEM((N_CHIPS, H, C), F32),
                        pltpu.SemaphoreType.DMA((5,)), pltpu.SemaphoreType.DMA((5,))],
        compiler_params=pltpu.CompilerParams(vmem_limit_bytes=VMEM_LIMIT),
    )(v)


def _cols_from_shards(g):
    return jnp.transpose(g, (1, 0, 2)).reshape(g.shape[1], -1)


def _shards_from_cols(w):
    R, C4 = w.shape
    return jnp.transpose(w.reshape(R, N_CHIPS, C4 // N_CHIPS), (1, 0, 2))


def _pack(arrs):
    flat = []
    for a in arrs:
        f = a.reshape(-1).astype(F32)
        flat.append(jnp.pad(f, (0, _rup(f.shape[0], LANE) - f.shape[0])))
    v = jnp.concatenate(flat)
    rows = _rup(v.shape[0] // LANE, 16)
    v = jnp.pad(v, (0, rows * LANE - v.shape[0]))
    return v.reshape(rows, LANE)


def _unpack(v, shapes):
    flat = v.reshape(-1)
    out, off = [], 0
    for s in shapes:
        n = int(np.prod(s))
        out.append(flat[off:off + n].reshape(s))
        off += _rup(n, LANE)
    return out


def _ffn_fwd(x, Wup, Wdn, cw, cb, tag):
    h = _mm(x, Wup, 'nt', bmode='bo', tm=512, tn=4096, name=f"ffn_up_{tag}")
    a, hc = _act_fwd(h, cw, cb, name=f"ffn_act_{tag}")
    f = _mm(a, Wdn, 'nn', bmode='abr', tm=512, tn=1024, tk=4096, name=f"ffn_down_{tag}")
    return f, (h, hc), a


def _ffn_bwd(df, x, saved, a, Wup, Wdn, cw, tag):
    h, hc = saved
    da = _mm(df, Wdn, 'nt', bmode='bo', tm=512, tn=4096, name=f"ffn_da_{tag}")
    dWdn = _mm(a, df, 'tn', bmode='ao', tm=4096, tn=512, name=f"ffn_dwdn_{tag}", out_dtype=BF16)
    dh, dcw, dcb = _act_bwd(h, hc, da, cw, name=f"ffn_actb_{tag}")

    def shard_of(k):
        return (k % 2) * 2 + k // 2

    dx = _mm(dh, Wup, 'nn', bmode='abr', tm=512, tn=1024, tk=4096, name=f"ffn_dx_{tag}", b_map=shard_of)
    dWup = _mm(dh, x, 'tn', bmode='ao', tm=4096, tn=512, name=f"ffn_dwup_{tag}", out_dtype=BF16,
               o_map=shard_of)
    return dx, dWup, dWdn, dcw, dcb


def kernel(x, positions, ev_w_in, ev_b_f, ev_lambda_re, ev_lambda_im, ev_log_step, ev_ssm_b_re, ev_ssm_b_im, ev_ssm_c_re, ev_ssm_c_im, ev_ssm_d, ev_w_glu, ev_w_out, od_w_in, od_sinks, od_w_out, ln_mix_g, ln_mix_b, ffn_w_up, ffn_conv_w, ffn_conv_b, ffn_w_down, ln_ffn_g, ln_ffn_b, loss_target, m_ev_w_in, m_ev_b_f, m_ev_lambda_re, m_ev_lambda_im, m_ev_log_step, m_ev_ssm_b_re, m_ev_ssm_b_im, m_ev_ssm_c_re, m_ev_ssm_c_im, m_ev_ssm_d, m_ev_w_glu, m_ev_w_out, m_od_w_in, m_od_sinks, m_od_w_out, m_ln_mix_g, m_ln_mix_b, m_ffn_w_up, m_ffn_conv_w, m_ffn_conv_b, m_ffn_w_down, m_ln_ffn_g, m_ln_ffn_b, v_ev_w_in, v_ev_b_f, v_ev_lambda_re, v_ev_lambda_im, v_ev_log_step, v_ev_ssm_b_re, v_ev_ssm_b_im, v_ev_ssm_c_re, v_ev_ssm_c_im, v_ev_ssm_d, v_ev_w_glu, v_ev_w_out, v_od_w_in, v_od_sinks, v_od_w_out, v_ln_mix_g, v_ln_mix_b, v_ffn_w_up, v_ffn_conv_w, v_ffn_conv_b, v_ffn_w_down, v_ln_ffn_g, v_ln_ffn_b):
    W = dict(ev_w_in=ev_w_in, ev_b_f=ev_b_f, ev_lambda_re=ev_lambda_re, ev_lambda_im=ev_lambda_im, ev_log_step=ev_log_step, ev_ssm_b_re=ev_ssm_b_re, ev_ssm_b_im=ev_ssm_b_im, ev_ssm_c_re=ev_ssm_c_re, ev_ssm_c_im=ev_ssm_c_im, ev_ssm_d=ev_ssm_d, ev_w_glu=ev_w_glu, ev_w_out=ev_w_out, od_w_in=od_w_in, od_sinks=od_sinks, od_w_out=od_w_out, ln_mix_g=ln_mix_g, ln_mix_b=ln_mix_b, ffn_w_up=ffn_w_up, ffn_conv_w=ffn_conv_w, ffn_conv_b=ffn_conv_b, ffn_w_down=ffn_w_down, ln_ffn_g=ln_ffn_g, ln_ffn_b=ln_ffn_b)
    Mo = dict(ev_w_in=m_ev_w_in, ev_b_f=m_ev_b_f, ev_lambda_re=m_ev_lambda_re, ev_lambda_im=m_ev_lambda_im, ev_log_step=m_ev_log_step, ev_ssm_b_re=m_ev_ssm_b_re, ev_ssm_b_im=m_ev_ssm_b_im, ev_ssm_c_re=m_ev_ssm_c_re, ev_ssm_c_im=m_ev_ssm_c_im, ev_ssm_d=m_ev_ssm_d, ev_w_glu=m_ev_w_glu, ev_w_out=m_ev_w_out, od_w_in=m_od_w_in, od_sinks=m_od_sinks, od_w_out=m_od_w_out, ln_mix_g=m_ln_mix_g, ln_mix_b=m_ln_mix_b, ffn_w_up=m_ffn_w_up, ffn_conv_w=m_ffn_conv_w, ffn_conv_b=m_ffn_conv_b, ffn_w_down=m_ffn_w_down, ln_ffn_g=m_ln_ffn_g, ln_ffn_b=m_ln_ffn_b)
    Vo = dict(ev_w_in=v_ev_w_in, ev_b_f=v_ev_b_f, ev_lambda_re=v_ev_lambda_re, ev_lambda_im=v_ev_lambda_im, ev_log_step=v_ev_log_step, ev_ssm_b_re=v_ev_ssm_b_re, ev_ssm_b_im=v_ev_ssm_b_im, ev_ssm_c_re=v_ev_ssm_c_re, ev_ssm_c_im=v_ev_ssm_c_im, ev_ssm_d=v_ev_ssm_d, ev_w_glu=v_ev_w_glu, ev_w_out=v_ev_w_out, od_w_in=v_od_w_in, od_sinks=v_od_sinks, od_w_out=v_od_w_out, ln_mix_g=v_ln_mix_g, ln_mix_b=v_ln_mix_b, ffn_w_up=v_ffn_w_up, ffn_conv_w=v_ffn_conv_w, ffn_conv_b=v_ffn_conv_b, ffn_w_down=v_ffn_w_down, ln_ffn_g=v_ln_ffn_g, ln_ffn_b=v_ln_ffn_b)
    names = list(W.keys())
    big = ['ev_w_in', 'ev_w_glu', 'ev_w_out', 'od_w_in', 'od_w_out', 'ffn_w_up', 'ffn_w_down']

    S, D = x.shape[1], x.shape[2]
    x0 = x.reshape(S, D)
    tgt = loss_target.reshape(S, D)
    G, Pn, Cg = SSM_GROUPS, SSM_STATE, SSM_GROUP
    Fs = ffn_w_up.shape[2]
    FP = Fs
    Rd = ffn_w_down.shape[1]
    EIN = N_CHIPS * ev_w_in.shape[2]

    cwl = ffn_conv_w.reshape(-1)
    cw_rows = _rup(_rup(cwl.shape[0], LANE) // LANE, 32)
    cw_pad = jnp.pad(cwl, (0, cw_rows * LANE - cwl.shape[0])).reshape(cw_rows, LANE)
    transposed = ('ev_w_in', 'ffn_w_up')

    def view(n, a):
        return jnp.transpose(a, (0, 2, 1)) if n in transposed else a

    Wv = {n: view(n, W[n]) for n in big}
    big_e = [(n, l) for n in big for l in range(W[n].shape[0])]
    split_cols = {e: (Wv[e[0]].shape[1] // 2) % 16 != 0 for e in big_e}
    shard16 = {e: Wv[e[0]][e[1]].astype(BF16) for e in big_e}
    grp_now = [e for e in big_e if e[0].startswith('ev_')]
    grp_ffn0 = [('ffn_w_up', 0), ('ffn_w_down', 0)]
    grp_l1 = [('od_w_in', 0), ('od_w_out', 0), ('ffn_w_up', 1), ('ffn_w_down', 1)]
    src_now = [shard16[e] for e in grp_now]
    src_ffn0 = [shard16[e] for e in grp_ffn0] + [cw_pad]
    src_l1 = [shard16[e] for e in grp_l1]
    cols_now = [split_cols[e] for e in grp_now]
    cols_ffn0 = [split_cols[e] for e in grp_ffn0] + [False]
    cols_l1 = [split_cols[e] for e in grp_l1]
    ag_in = _chip_exchange_start('gather', src_now[:1], cols_now[:1], name="ag_in_start")
    ag_mix = _chip_exchange_start('gather', src_now[1:], cols_now[1:], name="ag_mix_start", after=[ag_in[4]])
    ag_ffn0 = _chip_exchange_start('gather', src_ffn0, cols_ffn0, name="ag_ffn0_start", after=[ag_mix[4]])
    ag_l1 = _chip_exchange_start('gather', src_l1, cols_l1, name="ag_l1_start", after=[ag_ffn0[4]])
    started = [ag_l1[4]]

    def finish_gather(started, srcs, cols, after, tag):
        send, recv, thru, lands, _ = started
        thru, lands = _chip_exchange_wait('gather', send, recv, thru, lands, cols, after, name=f"ag_{tag}_wait")
        lands = _sibling_pass_gathered(lands, [s.shape for s in srcs], cols, name=f"ag_{tag}_pass")
        return _own_slot(lands, [s[None] for s in thru])

    lam_r, lam_i = ev_lambda_re[0], ev_lambda_im[0]
    lstep = ev_log_step[0].reshape(G, 1)
    a_re, a_im, g_re, g_im = _s5_disc_fwd(lam_r, lam_i, lstep, name="s5_disc", after=started)
    b_re2, b_im2 = ev_ssm_b_re[0].reshape(G * Pn, Cg), ev_ssm_b_im[0].reshape(G * Pn, Cg)
    g_re1, g_im1 = g_re.reshape(G * Pn, 1), g_im.reshape(G * Pn, 1)
    bb_re, bb_im = _s5_bb_fwd(g_re1, g_im1, b_re2, b_im2, name="s5_bb")
    bbt = jnp.stack([jnp.transpose(b.reshape(G, Pn, Cg), (0, 2, 1)).reshape(G * Cg, Pn) for b in (bb_re, bb_im)])
    BB = _diag_expand(bbt, Cg, Pn, name="s5_bb_dense")
    cct = jnp.stack([jnp.transpose(ev_ssm_c_re[0], (0, 2, 1)).reshape(G * Pn, Cg),
                     jnp.transpose(-ev_ssm_c_im[0], (0, 2, 1)).reshape(G * Pn, Cg)])
    CC = _diag_expand(cct, Pn, Cg, name="s5_cc_dense", after=started)
    a_cat = jnp.stack([a_re.reshape(1, G * Pn), a_im.reshape(1, G * Pn)])
    dskip = ev_ssm_d[0].reshape(1, SSM_WIDTH)
    tabs = _rope_tables(positions.reshape(S, 1).astype(F32), name="rope_tables", after=[BB, CC])

    gw = dict(zip(grp_now[:1], finish_gather(ag_in, src_now[:1], cols_now[:1], [tabs[2]], "in")))
    w_in_t = gw[('ev_w_in', 0)].reshape(EIN, D)
    qkv_w = 3 * FOX_WIDTH
    WmainT = jnp.concatenate([w_in_t[:qkv_w], w_in_t[qkv_w + FOX_HEADS:]], axis=0)
    WfT = jnp.pad(w_in_t[qkv_w:qkv_w + FOX_HEADS], ((0, LANE - FOX_HEADS), (0, 0)))
    cbs = [ffn_conv_b[l].reshape(N_CHIPS, Fs) for l in range(DEPTH)]

    P = _mm(x0, WmainT, 'nt', name="ev_proj")
    fl = _mm(x0, WfT, 'nt', name="ev_proj_f")
    bf_pad = jnp.pad(ev_b_f.reshape(1, FOX_HEADS), ((0, 0), (0, LANE - FOX_HEADS)))
    cgate, sgate = _gate_fwd(fl, bf_pad, name="fox_gate")
    ccol = jnp.transpose(cgate[:, :FOX_HEADS]).reshape(FOX_HEADS, S, 1)
    crow = jnp.transpose(cgate[:, :FOX_HEADS]).reshape(FOX_HEADS, 1, S)
    fox, lse = _fox_fwd(P, ccol, crow, name="fox_fwd")
    u_s5 = P[:, qkv_w:]
    UT, HT = _DIAG_TILE * Cg, _DIAG_TILE * Pn
    bu = _mm(u_s5, BB, 'nn', bmode='bo', tm=2048, tn=HT, tk=UT, diag='kn', name="s5_bu")
    hh = _s5_scan_fwd(bu, a_cat, name="s5_scan")
    yc = _mm(hh, CC, 'nn', bmode='abr', tm=2048, tn=UT, tk=HT, diag='kn', name="s5_y")
    y_s5, yg = _s5_out_fwd(yc, P, dskip, name="s5_out")
    gw.update(zip(grp_now[1:], finish_gather(ag_mix, src_now[1:], cols_now[1:], [yg], "mix")))
    Wglu = _cols_from_shards(gw[('ev_w_glu', 0)])
    Wout_ev = gw[('ev_w_out', 0)].reshape(D, D)
    z = _mm(yg, Wglu, 'nn', name="s5_glu_proj")
    cat = _glu_fwd(z, fox, name="s5_glu")
    mix0 = _mm(cat, Wout_ev, 'nn', name="ev_out")
    x1, xh1, rs1 = _add_ln_fwd(x0, mix0, ln_mix_g[0], ln_mix_b[0], name="ln_mix0")
    got = finish_gather(ag_ffn0, src_ffn0, cols_ffn0, [x1], "ffn0")
    gw.update(zip(grp_ffn0, got[:-1]))
    cw_all = got[-1].reshape(N_CHIPS, -1)[:, :cwl.shape[0]].reshape(N_CHIPS, DEPTH, 3, Fs)
    cws = [cw_all[:, l] for l in range(DEPTH)]
    Wup = {0: gw[('ffn_w_up', 0)]}
    Wdn = {0: gw[('ffn_w_down', 0)].reshape(2, Fs, D)}
    f0, hf0, af0 = _ffn_fwd(x1, Wup[0], Wdn[0], cws[0], cbs[0], "l0")
    x2, xh2, rs2 = _add_ln_fwd(x1, f0, ln_ffn_g[0], ln_ffn_b[0], name="ln_ffn0")

    gw.update(zip(grp_l1, finish_gather(ag_l1, src_l1, cols_l1, [x2], "l1")))
    Wodin = _cols_from_shards(gw[('od_w_in', 0)])
    Wodout = gw[('od_w_out', 0)].reshape(D, D)
    Wup[1] = gw[('ffn_w_up', 1)]
    Wdn[1] = gw[('ffn_w_down', 1)].reshape(2, Fs, D)
    QW, KW = SWA_HEADS * SWA_HEAD_DIM, SWA_KV_HEADS * SWA_HEAD_DIM
    P1 = _mm(x2, Wodin, 'nn', name="od_proj")
    qT = _to_heads(P1, tabs, col0=0, width=QW, rotate=True, name="rope_q", out_dtype=BF16)
    kT = _to_heads(P1, tabs, col0=QW, width=KW, rotate=True, name="rope_k", out_dtype=BF16)
    vT = _to_heads(P1, tabs, col0=QW + KW, width=KW, rotate=False, name="heads_v", out_dtype=BF16)
    sink_rows = jnp.broadcast_to(od_sinks[0].reshape(SWA_KV_HEADS, SWA_GROUPS, 1, 1),
                                 (SWA_KV_HEADS, SWA_GROUPS, SWA_WINDOW, 1)).reshape(SWA_KV_HEADS, -1, 1)
    oT, Lsw = _swa_fwd(qT, kT, vT, sink_rows, name="swa_fwd")
    o_sw = _from_heads(oT, tabs, rotate_back=False, name="heads_o", out_dtype=BF16)
    mix1 = _mm(o_sw, Wodout, 'nn', name="od_out")
    x3, xh3, rs3 = _add_ln_fwd(x2, mix1, ln_mix_g[1], ln_mix_b[1], name="ln_mix1")
    f1, hf1, af1 = _ffn_fwd(x3, Wup[1], Wdn[1], cws[1], cbs[1], "l1")
    _, xh4, rs4 = _add_ln_fwd(x3, f1, ln_ffn_g[1], ln_ffn_b[1], name="ln_ffn1")

    dz4, dg_ffn1, db_ffn1, loss_part = _loss_ln_bwd(tgt, xh4, rs4, ln_ffn_g[1], ln_ffn_b[1], name="loss_lnb_ffn1")
    dx3f, dWup1, dWdn1, dcw1, dcb1 = _ffn_bwd(dz4, x3, hf1, af1, Wup[1], Wdn[1], cws[1], "l1")
    sib_ffn1 = _sibling_halves_start([dWup1, dWdn1.reshape(N_CHIPS, Rd, D)], [False, False], name="rs_ffn1_sib_start")
    dz3, dg_mix1, db_mix1 = _ln_bwd(dz4, dx3f, xh3, rs3, ln_mix_g[1], name="lnb_mix1", after=[sib_ffn1[4]])
    do_sw = _mm(dz3, Wodout, 'nt', name="od_out_dx")
    dWodout = _mm(o_sw, dz3, 'tn', name="od_out_dw", out_dtype=BF16)
    doT = _to_heads(do_sw, tabs, col0=0, width=QW, rotate=False, name="heads_do", out_dtype=F32)
    dqT, dkT, dvT, dsink = _swa_bwd(qT, kT, vT, sink_rows, oT, Lsw, doT, name="swa_bwd")
    dq1 = _from_heads(dqT, tabs, rotate_back=True, name="rope_dq", out_dtype=BF16)
    dk1 = _from_heads(dkT, tabs, rotate_back=True, name="rope_dk", out_dtype=BF16, skip_rows=SWA_WINDOW)
    dv1 = _from_heads(dvT, tabs, rotate_back=False, name="heads_dv", out_dtype=BF16, skip_rows=SWA_WINDOW)
    dP1 = jnp.concatenate([dq1, dk1, dv1], axis=1)
    dx2m = _mm(dP1, Wodin, 'nt', name="od_proj_dx")
    dWodin = _mm(x2, dP1, 'tn', name="od_proj_dw", out_dtype=BF16)

    def rs_begin(entries, grads, tag):
        cols = [split_cols[e] for e in entries]
        sib = _sibling_send_halves(grads, cols, name=f"rs_{tag}_sibling")
        return [_sum2_halves(g4, s4, bc, name=f"rs_sum2_{n}{l}")
                for (n, l), g4, s4, bc in zip(entries, grads, sib, cols)]

    def rs_begin_started(entries, started, after, tag):
        send, rcv, thru, lands, _ = started
        thru, lands = _sibling_halves_wait(send, rcv, thru, lands, [False] * len(thru), after,
                                           name=f"rs_{tag}_sib_wait")
        return [_sum2_halves(g4, s4, False, name=f"rs_sum2_{n}{l}") for (n, l), g4, s4 in zip(entries, thru, lands)]

    def own_parts(parts):
        me = 2 * lax.axis_index("x") + lax.axis_index("y")
        return [lax.dynamic_slice_in_dim(p, me, 1, axis=0) for p in parts]

    part_l1 = (rs_begin(grp_l1[:2], [_shards_from_cols(dWodin), dWodout.reshape(N_CHIPS, D // N_CHIPS, D)], "od")
               + rs_begin_started(grp_l1[2:], sib_ffn1, [dWodin], "ffn1"))
    rs_l1 = _chip_exchange_start('scatter', part_l1, [False] * len(part_l1), name="rs_l1_start")

    dz2, dg_ffn0, db_ffn0 = _ln_bwd(dz3, dx2m, xh2, rs2, ln_ffn_g[0], name="lnb_ffn0", after=[rs_l1[4]])
    dx1f, dWup0, dWdn0, dcw0, dcb0 = _ffn_bwd(dz2, x1, hf0, af0, Wup[0], Wdn[0], cws[0], "l0")
    sib_ffn0 = _sibling_halves_start([dWup0, dWdn0.reshape(N_CHIPS, Rd, D)], [False, False], name="rs_ffn0_sib_start")
    dz1, dg_mix0, db_mix0 = _ln_bwd(dz2, dx1f, xh1, rs1, ln_mix_g[0], name="lnb_mix0", after=[sib_ffn0[4]])
    dcat = _mm(dz1, Wout_ev, 'nt', name="ev_out_dx")
    dWout_ev = _mm(cat, dz1, 'tn', name="ev_out_dw", out_dtype=BF16)
    part_ffn0 = rs_begin_started(grp_ffn0, sib_ffn0, [dWout_ev], "ffn0")
    rs_ffn0 = _chip_exchange_start('scatter', part_ffn0, [False] * len(part_ffn0), name="rs_ffn0_start")
    dz = _glu_bwd(z, dcat, name="s5_glu_bwd")
    dyg = _mm(dz, Wglu, 'nt', name="s5_glu_dx", after=[rs_ffn0[4]])
    dWglu = _mm(yg, dz, 'tn', name="s5_glu_dw", out_dtype=BF16)
    dy_s5, du_dir, dD = _s5_out_bwd(dyg, y_s5, P, dskip, name="s5_out_bwd")
    dhh = _mm(dy_s5, CC, 'nt', bmode='bo', tm=2048, tn=HT, tk=UT, diag='kn', name="s5_y_dx")
    dCC = _mm(hh, dy_s5, 'tn', bmode='ao', tm=HT, tn=UT, diag='mn', name="s5_y_dw")
    lam, da_s5 = _s5_scan_bwd(dhh, hh, a_cat, name="s5_scan_bwd")
    du = _mm(lam, BB, 'nt', bmode='abr', tm=2048, tn=UT, tk=HT, diag='kn', name="s5_bu_dx", plus=[(du_dir, 1.0)],
             out_dtype=BF16)
    dBB = _mm(u_s5, lam, 'tn', bmode='bo', tm=UT, tn=HT, diag='mn', name="s5_bu_dw")
    dq0, dk0, dv0, dccol, dcrow = _fox_bwd(P, ccol, crow, fox, lse, dcat, name="fox_bwd")
    dc = jnp.transpose((dccol.reshape(FOX_HEADS, S) - dcrow.reshape(FOX_HEADS, S)))
    dc = jnp.pad(dc, ((0, 0), (0, LANE - FOX_HEADS)))
    dfl, dbf = _gate_bwd(dc, sgate, name="fox_gate_bwd")
    dP = jnp.concatenate([dq0, dk0, dv0, du], axis=1)
    dx0b = _mm(dfl, WfT, 'nn', name="ev_proj_f_dx")
    grad_x = _mm(dP, WmainT, 'nn', name="ev_proj_dx", plus=[(dz1, ALPHA), (dx0b, 1.0)])
    dWmainT = _mm(dP, x0, 'tn', tm=1024, tn=1024, name="ev_proj_dw", out_dtype=BF16)
    dWfT = _mm(dfl, x0, 'tn', name="ev_proj_f_dw", out_dtype=BF16)

    dbbt = _diag_extract(dBB, Cg, Pn, name="s5_bb_diag")
    dcct = _diag_extract(dCC, Pn, Cg, name="s5_cc_diag")
    dbb_re = jnp.transpose(dbbt[0].reshape(G, Cg, Pn), (0, 2, 1)).reshape(G * Pn, Cg)
    dbb_im = jnp.transpose(dbbt[1].reshape(G, Cg, Pn), (0, 2, 1)).reshape(G * Pn, Cg)
    db_re, db_im, dg_re1, dg_im1 = _s5_bb_bwd(g_re1, g_im1, b_re2, b_im2, dbb_re, dbb_im, name="s5_bb_bwd")
    dlam_re, dlam_im, dlstep = _s5_disc_bwd(lam_r, lam_i, lstep, da_s5[0].reshape(G, Pn), da_s5[1].reshape(G, Pn),
                                            dg_re1.reshape(G, Pn), dg_im1.reshape(G, Pn), name="s5_disc_bwd")
    dc_re = jnp.transpose(dcct[0].reshape(G, Pn, Cg), (0, 2, 1))
    dc_im = -jnp.transpose(dcct[1].reshape(G, Pn, Cg), (0, 2, 1))

    def conv_w_full(d0, d1):
        return jnp.stack([jnp.reshape(jnp.transpose(d[:, :, :Fs], (1, 0, 2)), (3, N_CHIPS * Fs)) for d in (d0, d1)])

    def conv_b_full(d0, d1):
        return jnp.stack([jnp.reshape(d[:, 0, :Fs], (N_CHIPS * Fs,)) for d in (d0, d1)])

    small_local = dict(
        ev_b_f=dbf[:, :FOX_HEADS], ev_lambda_re=dlam_re, ev_lambda_im=dlam_im, ev_log_step=dlstep,
        ev_ssm_b_re=db_re, ev_ssm_b_im=db_im, ev_ssm_c_re=dc_re, ev_ssm_c_im=dc_im, ev_ssm_d=dD,
        od_sinks=dsink[:, :, 0],
        ln_mix_g=jnp.concatenate([dg_mix0, dg_mix1]), ln_mix_b=jnp.concatenate([db_mix0, db_mix1]),
        ffn_conv_w=conv_w_full(dcw0, dcw1), ffn_conv_b=conv_b_full(dcb0, dcb1),
        ln_ffn_g=jnp.concatenate([dg_ffn0, dg_ffn1]), ln_ffn_b=jnp.concatenate([db_ffn0, db_ffn1]))
    small = list(small_local.keys())
    out_g, out_d, out_m, out_v = {}, {}, {}, {}
    loss_out = []

    def small_update(after):
        red = _all_reduce_small(_pack([small_local[n] for n in small] + [loss_part]), name="ar_small", after=after)
        full_shapes = [W[n].shape if n != 'ffn_conv_w' else (DEPTH, 3, N_CHIPS * Fs) for n in small]
        pieces = _unpack(red, full_shapes + [()])
        loss_out.append(pieces[-1])
        gsmall = dict(zip(small, pieces[:-1]))
        chip = 2 * lax.axis_index("x") + lax.axis_index("y")
        gsmall['ffn_conv_w'] = lax.dynamic_slice_in_dim(gsmall['ffn_conv_w'], chip * Fs, Fs, axis=2)
        shapes = [W[n].shape for n in small]
        gs, ds_, ms, vs = _adamw(_pack([W[n] for n in small])[None], _pack([gsmall[n] for n in small])[None],
                                 _pack([Mo[n] for n in small])[None], _pack([Vo[n] for n in small])[None],
                                 name="adamw_small", tr=1 << 14)
        out_g.update(zip(small, _unpack(gs, shapes)))
        out_d.update(zip(small, _unpack(ds_, shapes)))
        out_m.update(zip(small, _unpack(ms, shapes)))
        out_v.update(zip(small, _unpack(vs, shapes)))
        return vs

    dw_in_t = jnp.concatenate([dWmainT[:qkv_w], dWfT[:FOX_HEADS], dWmainT[qkv_w:]], axis=0)
    part_now = rs_begin(grp_now, [dw_in_t.reshape(N_CHIPS, EIN // N_CHIPS, D), _shards_from_cols(dWglu),
                                  dWout_ev.reshape(N_CHIPS, D // N_CHIPS, D)], "l0")
    small_done = small_update([grad_x])
    rs_now = _chip_exchange_start('scatter', part_now, [False] * len(part_now), name="rs_l0_start",
                                  after=[small_done])

    def finish_scatter(started, parts, after, tag):
        send, rcv, thru, lands, _ = started
        thru, lands = _chip_exchange_wait('scatter', send, rcv, thru, lands, [False] * len(parts), after,
                                          name=f"rs_{tag}_wait")
        return _own_slot(lands, own_parts(thru))

    def update(entries, recv, tag):
        halves = [_rowsum(r, name=f"rs_sum4_{e[0]}{e[1]}") for e, r in zip(entries, recv)]
        send, rcv, thru, lands, tok = _sibling_swap_start(halves, name=f"rs_{tag}_join_start")
        own = dict(zip(entries, thru))
        params = list(dict.fromkeys(e[0] for e in entries))

        def half_update(n, grads, is_own, prev, after_name):
            return _adamw_half(Wv[n], [grads[(n, l)] for l in range(W[n].shape[0])], view(n, Mo[n]), view(n, Vo[n]),
                               name=f"adamw_{after_name}_{n}", own=is_own, prev=prev, by_cols=split_cols[(n, 0)])

        first = {n: half_update(n, own, True, None, "own") for n in params}
        _, others = _sibling_swap_wait(send, rcv, thru, lands, [first[n][3] for n in params] + [tok],
                                       name=f"rs_{tag}_join_wait")
        oth = dict(zip(entries, others))
        done = []
        for n in params:
            res = half_update(n, oth, False, first[n], "sib")
            out_g[n], out_d[n], out_m[n], out_v[n] = (view(n, t) for t in res)
            done.append(res[3])
        return done

    recv_rest = (finish_scatter(rs_l1, part_l1, [rs_now[4]], "l1")
                 + finish_scatter(rs_ffn0, part_ffn0, [rs_now[4]], "ffn0"))
    done = update(grp_l1 + grp_ffn0, recv_rest, "rest")
    update(grp_now, finish_scatter(rs_now, part_now, done, "l0"), "l0")
    loss = loss_out[0]

    return (loss, grad_x.reshape(1, S, D), *[out_g[n] for n in names], *[out_d[n] for n in names],
            *[out_m[n] for n in names], *[out_v[n] for n in names])
```

```python
import math

import numpy as np
import jax
import jax.numpy as jnp
from jax import lax
from jax.experimental import pallas as pl
from jax.experimental.pallas import tpu as pltpu

F32 = jnp.float32
BF16 = jnp.bfloat16
MESH = pl.DeviceIdType.MESH
ANY = pl.BlockSpec(memory_space=pl.ANY)

D_MODEL = 2048
FOX_HEADS = 8
FOX_HEAD_DIM = 128
FOX_WIDTH = 1024
SSM_WIDTH = 1024
SSM_GROUP = 16
SSM_GROUPS = 64
SSM_STATE = 64
SWA_HEADS = 32
SWA_KV_HEADS = 4
SWA_HEAD_DIM = 64
SWA_GROUPS = 8
SWA_WINDOW = 128
ROPE_DIM = 16
ROPE_THETA = 500000.0
LN_EPS = 1e-5
DEPTH = 2
ALPHA = (2.0 * DEPTH) ** 0.25
ADAM_LR = 0.001
ADAM_B1 = 0.9
ADAM_B2 = 0.999
ADAM_EPS = 1e-08
ADAM_WD = 0.01
ADAM_STEP = 10
N_CHIPS = 4

VMEM_LIMIT = 56 * 1024 * 1024
LANE = 128


def _call(body, after=(), **kw):
    if after:
        n = len(after)

        def shifted(*refs):
            return body(*refs[n:])

        call = _call(shifted, **dict(kw, in_specs=[ANY] * n + list(kw["in_specs"])))
        return lambda *args: call(*after, *args)
    return pl.pallas_call(body, **kw)


def _cparams(sem):
    return pltpu.CompilerParams(dimension_semantics=sem, vmem_limit_bytes=VMEM_LIMIT)


def _rup(n, m):
    return (n + m - 1) // m * m


def _pick(n, pref):
    if n <= pref:
        return n
    for step in (128, 16, 8):
        for t in range(pref - pref % step, 0, -step):
            if n % t == 0:
                return t
    return n


def _tile2d(rows, cols, pref_rows=256, budget=256 * 1024):
    tr = _pick(rows, pref_rows)
    if tr < 64:
        tr = rows
    if cols % LANE:
        return tr, cols
    return tr, _pick(cols, max(LANE, budget // tr // LANE * LANE))


def _mm(a, b, mode, *, name, tm=512, tn=1024, tk=2048, bmode=None, out_dtype=F32, after=(), b_map=None,
        o_map=None, diag=None, plus=()):
    a3 = a if a.ndim == 3 else a[None]
    b3 = b if b.ndim == 3 else b[None]
    if mode == 'tn':
        K, M = a3.shape[1:]
    else:
        M, K = a3.shape[1:]
    N = b3.shape[1] if mode == 'nt' else b3.shape[2]
    tm, tn, tk = _pick(M, tm), _pick(N, tn), _pick(K, tk)
    nb = max(a3.shape[0], b3.shape[0])
    nbo, nbr = (1, nb) if bmode == 'abr' else (nb, 1)
    nm, nk = M // tm, K // tk
    if diag == 'kn':
        assert K // tk == N // tn
        nk = 1
    if diag == 'mn':
        assert M // tm == N // tn
        nm = 1
    nred = nbr * nk
    a_b = bmode in ('ao', 'abr')
    b_b = bmode in ('bo', 'abr')
    o_b = bmode in ('bo', 'ao')

    def bsel(flag, bo, br, remap=None):
        if not flag:
            return 0
        return (bo + br) if remap is None else remap(bo + br)

    def mi(i, j):
        return j if diag == 'mn' else i

    def ki(j, k):
        return j if diag == 'kn' else k

    if mode == 'tn':
        a_spec = pl.BlockSpec((None, tk, tm), lambda bo, i, j, br, k: (bsel(a_b, bo, br), ki(j, k), mi(i, j)))
    else:
        a_spec = pl.BlockSpec((None, tm, tk), lambda bo, i, j, br, k: (bsel(a_b, bo, br), mi(i, j), ki(j, k)))
    if mode == 'nt':
        b_spec = pl.BlockSpec((None, tn, tk), lambda bo, i, j, br, k: (bsel(b_b, bo, br, b_map), j, ki(j, k)))
    else:
        b_spec = pl.BlockSpec((None, tk, tn), lambda bo, i, j, br, k: (bsel(b_b, bo, br, b_map), ki(j, k), j))
    o_spec = pl.BlockSpec((None, tm, tn), lambda bo, i, j, br, k: (bsel(o_b, bo, br, o_map), mi(i, j), j))
    dn = {'nn': (((1,), (0,)), ((), ())), 'nt': (((1,), (1,)), ((), ())), 'tn': (((0,), (0,)), ((), ()))}[mode]

    na = len(plus)

    def body(a_ref, b_ref, *rest):
        plus_refs = rest[:na]
        o_ref, scratch = rest[na + len(after)], rest[na + len(after) + 1:]
        r = lax.dot_general(a_ref[...].astype(BF16), b_ref[...].astype(BF16), dn, preferred_element_type=F32)

        def finish(total):
            for (_, scale), p_ref in zip(plus, plus_refs):
                total = total + scale * p_ref[...].astype(F32)
            o_ref[...] = total.astype(out_dtype)

        if nred == 1:
            finish(r)
        else:
            acc = scratch[0]
            step = pl.program_id(3) * nk + pl.program_id(4)

            @pl.when(step == 0)
            def _():
                acc[...] = r

            @pl.when(step > 0)
            def _():
                acc[...] += r

            @pl.when(step == nred - 1)
            def _():
                finish(acc[...])

    out = _call(
        body, name=name,
        grid=(nbo, nm, N // tn, nbr, nk),
        in_specs=[a_spec, b_spec] + [o_spec] * na + [ANY] * len(after), out_specs=o_spec,
        out_shape=jax.ShapeDtypeStruct((nbo if o_b else 1, M, N), out_dtype),
        scratch_shapes=[] if nred == 1 else [pltpu.VMEM((tm, tn), F32)],
        compiler_params=_cparams(("parallel", "parallel", "parallel", "arbitrary", "arbitrary")),
    )(a3, b3, *[p if p.ndim == 3 else p[None] for p, _ in plus], *after)
    return out if o_b else out[0]


def _add_ln_fwd(x, r, g, b, *, name):
    S, D = x.shape
    tr = _pick(S, 512)

    def body(x_ref, r_ref, g_ref, b_ref, o_ref, xh_ref, rs_ref):
        z = ALPHA * x_ref[...] + r_ref[...]
        mu = jnp.mean(z, axis=-1, keepdims=True)
        zc = z - mu
        var = jnp.mean(zc * zc, axis=-1, keepdims=True)
        rstd = lax.rsqrt(var + LN_EPS)
        xh = zc * rstd
        xh_ref[...] = xh
        rs_ref[...] = rstd
        o_ref[...] = xh * g_ref[...] + b_ref[...]

    row = pl.BlockSpec((tr, D), lambda i: (i, 0))
    vec = pl.BlockSpec((1, D), lambda i: (0, 0))
    return _call(
        body, name=name, grid=(S // tr,),
        in_specs=[row, row, vec, vec],
        out_specs=[row, row, pl.BlockSpec((tr, 1), lambda i: (i, 0))],
        out_shape=[jax.ShapeDtypeStruct((S, D), F32), jax.ShapeDtypeStruct((S, D), F32),
                   jax.ShapeDtypeStruct((S, 1), F32)],
        compiler_params=_cparams(("parallel",)),
    )(x, r, g.reshape(1, D), b.reshape(1, D))


def _ln_bwd(da, db, xhat, rstd, g, *, name, after=()):
    S, D = xhat.shape
    tr = _pick(S, 256)

    def body(*refs):
        da_ref, db_ref, xh_ref, rs_ref, g_ref, dz_ref, dg_ref, dbt_ref = refs[len(after):]
        dy = ALPHA * da_ref[...] + db_ref[...]
        xh = xh_ref[...]
        dxh = dy * g_ref[...]
        m1 = jnp.mean(dxh, axis=-1, keepdims=True)
        m2 = jnp.mean(dxh * xh, axis=-1, keepdims=True)
        dz_ref[...] = rs_ref[...] * (dxh - m1 - xh * m2)
        pg = jnp.sum(dy * xh, axis=0, keepdims=True)
        pb = jnp.sum(dy, axis=0, keepdims=True)

        @pl.when(pl.program_id(0) == 0)
        def _():
            dg_ref[...] = pg
            dbt_ref[...] = pb

        @pl.when(pl.program_id(0) > 0)
        def _():
            dg_ref[...] += pg
            dbt_ref[...] += pb

    row = pl.BlockSpec((tr, D), lambda i: (i, 0))
    vec = pl.BlockSpec((1, D), lambda i: (0, 0))
    ins = list(after) + [da, db, xhat, rstd, g.reshape(1, D)]
    in_specs = [ANY] * len(after) + [row, row, row, pl.BlockSpec((tr, 1), lambda i: (i, 0)), vec]
    return _call(
        body, name=name, grid=(S // tr,),
        in_specs=in_specs, out_specs=[row, vec, vec],
        out_shape=[jax.ShapeDtypeStruct((S, D), F32), jax.ShapeDtypeStruct((1, D), F32),
                   jax.ShapeDtypeStruct((1, D), F32)],
        compiler_params=_cparams(("arbitrary",)),
    )(*ins)


def _loss_ln_bwd(t, xhat, rstd, g, b, *, name):
    S, D = xhat.shape
    tr = _pick(S, 256)

    def body(t_ref, xh_ref, rs_ref, g_ref, b_ref, dz_ref, dg_ref, dbt_ref, l_ref):
        xh = xh_ref[...]
        e = xh * g_ref[...] + b_ref[...] - t_ref[...]
        dy = e * (1.0 / D)
        part = 0.5 * jnp.sum(jnp.sum(e * e, axis=-1, keepdims=True) * (1.0 / D), axis=0, keepdims=True)
        dxh = dy * g_ref[...]
        m1 = jnp.mean(dxh, axis=-1, keepdims=True)
        m2 = jnp.mean(dxh * xh, axis=-1, keepdims=True)
        dz_ref[...] = rs_ref[...] * (dxh - m1 - xh * m2)
        pg = jnp.sum(dy * xh, axis=0, keepdims=True)
        pb = jnp.sum(dy, axis=0, keepdims=True)

        @pl.when(pl.program_id(0) == 0)
        def _():
            dg_ref[...] = pg
            dbt_ref[...] = pb
            l_ref[...] = part

        @pl.when(pl.program_id(0) > 0)
        def _():
            dg_ref[...] += pg
            dbt_ref[...] += pb
            l_ref[...] += part

    row = pl.BlockSpec((tr, D), lambda i: (i, 0))
    vec = pl.BlockSpec((1, D), lambda i: (0, 0))
    return _call(
        body, name=name, grid=(S // tr,),
        in_specs=[row, row, pl.BlockSpec((tr, 1), lambda i: (i, 0)), vec, vec],
        out_specs=[row, vec, vec, pl.BlockSpec((1, 1), lambda i: (0, 0))],
        out_shape=[jax.ShapeDtypeStruct((S, D), F32), jax.ShapeDtypeStruct((1, D), F32),
                   jax.ShapeDtypeStruct((1, D), F32), jax.ShapeDtypeStruct((1, 1), F32)],
        compiler_params=_cparams(("arbitrary",)),
    )(t, xhat, rstd, g.reshape(1, D), b.reshape(1, D))


def _split3(x):
    h = x.astype(BF16)
    r = x - h.astype(F32)
    m = r.astype(BF16)
    l = (r - m.astype(F32)).astype(BF16)
    return h, m, l


def _tri_matmul(tri_bf, x):
    h, m, l = _split3(x)
    dn = (((1,), (0,)), ((), ()))
    return (lax.dot_general(tri_bf, l, dn, preferred_element_type=F32)
            + lax.dot_general(tri_bf, m, dn, preferred_element_type=F32)
            + lax.dot_general(tri_bf, h, dn, preferred_element_type=F32))


def _gate_fwd(fl, bf, *, name):
    S = fl.shape[0]
    tc = _pick(S, 256)
    nchunk = S // tc

    def body(fl_ref, bf_ref, c_ref, sg_ref):
        r = lax.broadcasted_iota(jnp.int32, (tc, tc), 0)
        cidx = lax.broadcasted_iota(jnp.int32, (tc, tc), 1)
        tri = (r >= cidx).astype(BF16)
        carry = jnp.zeros((1, LANE), F32)
        for ch in range(nchunk):
            x = fl_ref[pl.ds(ch * tc, tc), :] + bf_ref[...]
            lf = jnp.minimum(x, 0.0) - jnp.log(1.0 + jnp.exp(-jnp.abs(x)))
            sg_ref[pl.ds(ch * tc, tc), :] = jax.nn.sigmoid(-x)
            c_ref[pl.ds(ch * tc, tc), :] = _tri_matmul(tri, lf) + carry
            carry = carry + jnp.sum(lf, axis=0, keepdims=True)

    full = pl.BlockSpec((S, LANE), lambda: (0, 0))
    return _call(
        body, name=name, in_specs=[full, pl.BlockSpec((1, LANE), lambda: (0, 0))], out_specs=[full, full],
        out_shape=[jax.ShapeDtypeStruct((S, LANE), F32)] * 2,
        compiler_params=pltpu.CompilerParams(vmem_limit_bytes=VMEM_LIMIT),
    )(fl, bf)


def _gate_bwd(dc, sg, *, name):
    S = dc.shape[0]
    tc = _pick(S, 256)
    nchunk = S // tc

    def body(dc_ref, sg_ref, dfl_ref, db_ref):
        r = lax.broadcasted_iota(jnp.int32, (tc, tc), 0)
        cidx = lax.broadcasted_iota(jnp.int32, (tc, tc), 1)
        tri = (r <= cidx).astype(BF16)
        carry = jnp.zeros((1, LANE), F32)
        dbacc = jnp.zeros((1, LANE), F32)
        for ch in reversed(range(nchunk)):
            d = dc_ref[pl.ds(ch * tc, tc), :]
            dfl = (_tri_matmul(tri, d) + carry) * sg_ref[pl.ds(ch * tc, tc), :]
            dfl_ref[pl.ds(ch * tc, tc), :] = dfl
            dbacc = dbacc + jnp.sum(dfl, axis=0, keepdims=True)
            carry = carry + jnp.sum(d, axis=0, keepdims=True)
        db_ref[...] = dbacc

    full = pl.BlockSpec((S, LANE), lambda: (0, 0))
    return _call(
        body, name=name, in_specs=[full, full], out_specs=[full, pl.BlockSpec((1, LANE), lambda: (0, 0))],
        out_shape=[jax.ShapeDtypeStruct((S, LANE), F32), jax.ShapeDtypeStruct((1, LANE), F32)],
        compiler_params=pltpu.CompilerParams(vmem_limit_bytes=VMEM_LIMIT),
    )(dc, sg)


def _fox_scores(q_ref, k_ref, cc_ref, cr_ref, qi, tq, S):
    scale = 1.0 / math.sqrt(FOX_HEAD_DIM)
    s = lax.dot_general(q_ref[...].astype(BF16), k_ref[...].astype(BF16), (((1,), (1,)), ((), ())),
                        preferred_element_type=F32) * scale
    s = s + cc_ref[...] - cr_ref[...]
    row = lax.broadcasted_iota(jnp.int32, (tq, S), 0) + qi * tq
    col = lax.broadcasted_iota(jnp.int32, (tq, S), 1)
    return s, row >= col


def _fox_fwd(P, ccol, crow, *, name):
    S = P.shape[0]
    tq = _pick(S, 256)
    H = FOX_HEADS

    def body(q_ref, k_ref, v_ref, cc_ref, cr_ref, o_ref, l_ref):
        s, causal = _fox_scores(q_ref, k_ref, cc_ref, cr_ref, pl.program_id(1), tq, S)
        s = jnp.where(causal, s, -1e30)
        m = jnp.max(s, axis=-1, keepdims=True)
        e = jnp.exp(s - m)
        den = jnp.sum(e, axis=-1, keepdims=True)
        p = e * (1.0 / den)
        o_ref[...] = jnp.dot(p.astype(BF16), v_ref[...].astype(BF16), preferred_element_type=F32)
        l_ref[...] = m + jnp.log(den)

    return _call(
        body, name=name, grid=(H, S // tq),
        in_specs=[pl.BlockSpec((tq, 128), lambda h, i: (i, h)),
                  pl.BlockSpec((S, 128), lambda h, i: (0, H + h)),
                  pl.BlockSpec((S, 128), lambda h, i: (0, 2 * H + h)),
                  pl.BlockSpec((None, tq, 1), lambda h, i: (h, i, 0)),
                  pl.BlockSpec((None, 1, S), lambda h, i: (h, 0, 0))],
        out_specs=[pl.BlockSpec((tq, 128), lambda h, i: (i, h)),
                   pl.BlockSpec((None, tq, 1), lambda h, i: (h, i, 0))],
        out_shape=[jax.ShapeDtypeStruct((S, FOX_WIDTH), F32), jax.ShapeDtypeStruct((H, S, 1), F32)],
        compiler_params=_cparams(("parallel", "parallel")),
    )(P, P, P, ccol, crow)


def _fox_bwd(P, ccol, crow, o, lse, dcat, *, name):
    S = P.shape[0]
    tq = _pick(S, 512)
    H = FOX_HEADS
    nq = S // tq
    scale = 1.0 / math.sqrt(FOX_HEAD_DIM)

    def body(q_ref, k_ref, v_ref, cc_ref, cr_ref, o_ref, l_ref, do_ref,
             dq_ref, dk_ref, dv_ref, dcc_ref, dcr_ref, dk_acc, dv_acc):
        qi = pl.program_id(1)
        s, causal = _fox_scores(q_ref, k_ref, cc_ref, cr_ref, qi, tq, S)
        p = jnp.where(causal, jnp.exp(s - l_ref[...]), 0.0)
        do = do_ref[...]
        do_bf = do.astype(BF16)
        dp = lax.dot_general(do_bf, v_ref[...].astype(BF16), (((1,), (1,)), ((), ())), preferred_element_type=F32)
        delta = jnp.sum(do * o_ref[...], axis=-1, keepdims=True)
        ds = p * (dp - delta)
        ds_bf = ds.astype(BF16)
        dq_ref[...] = (jnp.dot(ds_bf, k_ref[...].astype(BF16), preferred_element_type=F32) * scale).astype(BF16)
        dkp = lax.dot_general(ds_bf, q_ref[...].astype(BF16), (((0,), (0,)), ((), ())),
                              preferred_element_type=F32) * scale
        dvp = lax.dot_general(p.astype(BF16), do_bf, (((0,), (0,)), ((), ())), preferred_element_type=F32)
        dcc_ref[...] = jnp.sum(ds, axis=-1, keepdims=True)
        dcr = jnp.sum(ds, axis=0, keepdims=True)

        @pl.when(qi == 0)
        def _():
            dk_acc[...] = dkp
            dv_acc[...] = dvp
            dcr_ref[...] = dcr

        @pl.when(qi > 0)
        def _():
            dk_acc[...] += dkp
            dv_acc[...] += dvp
            dcr_ref[...] += dcr

        @pl.when(qi == nq - 1)
        def _():
            dk_ref[...] = dk_acc[...].astype(BF16)
            dv_ref[...] = dv_acc[...].astype(BF16)

    qblk = pl.BlockSpec((tq, 128), lambda h, i: (i, h))
    kvo = pl.BlockSpec((S, 128), lambda h, i: (0, h))
    col = pl.BlockSpec((None, tq, 1), lambda h, i: (h, i, 0))
    rowv = pl.BlockSpec((None, 1, S), lambda h, i: (h, 0, 0))
    return _call(
        body, name=name, grid=(H, nq),
        in_specs=[qblk,
                  pl.BlockSpec((S, 128), lambda h, i: (0, H + h)),
                  pl.BlockSpec((S, 128), lambda h, i: (0, 2 * H + h)),
                  col, rowv, qblk, col, qblk],
        out_specs=[qblk, kvo, kvo, col, rowv],
        out_shape=[jax.ShapeDtypeStruct((S, FOX_WIDTH), BF16)] * 3
        + [jax.ShapeDtypeStruct((H, S, 1), F32), jax.ShapeDtypeStruct((H, 1, S), F32)],
        scratch_shapes=[pltpu.VMEM((S, 128), F32), pltpu.VMEM((S, 128), F32)],
        compiler_params=_cparams(("parallel", "arbitrary")),
    )(P, P, P, ccol, crow, o, lse, dcat)


def _s5_disc_fwd(lr, li, ls, *, name, after=()):
    G, Pn = lr.shape

    def body(lr_ref, li_ref, ls_ref, ar_ref, ai_ref, gr_ref, gi_ref):
        lr_, li_ = lr_ref[...], li_ref[...]
        dt = jnp.exp(ls_ref[...])
        mag = jnp.exp(lr_ * dt)
        th = li_ * dt
        ar = mag * jnp.cos(th)
        ai = mag * jnp.sin(th)
        den = lr_ * lr_ + li_ * li_
        xr = ar - 1.0
        ar_ref[...] = ar
        ai_ref[...] = ai
        gr_ref[...] = (xr * lr_ + ai * li_) / den
        gi_ref[...] = (ai * lr_ - xr * li_) / den

    sq = pl.BlockSpec((G, Pn), lambda: (0, 0))
    return _call(
        body, after=after, name=name, in_specs=[sq, sq, pl.BlockSpec((G, 1), lambda: (0, 0))], out_specs=[sq] * 4,
        out_shape=[jax.ShapeDtypeStruct((G, Pn), F32)] * 4,
    )(lr, li, ls)


def _s5_disc_bwd(lr, li, ls, dar, dai, dgr, dgi, *, name):
    G, Pn = lr.shape

    def body(lr_ref, li_ref, ls_ref, dar_ref, dai_ref, dgr_ref, dgi_ref, dlr_ref, dli_ref, dls_ref):
        lr_, li_ = lr_ref[...], li_ref[...]
        dt = jnp.exp(ls_ref[...])
        mag = jnp.exp(lr_ * dt)
        th = li_ * dt
        ar = mag * jnp.cos(th)
        ai = mag * jnp.sin(th)
        den = lr_ * lr_ + li_ * li_
        xr = ar - 1.0
        xi = ai
        g_re = (xr * lr_ + xi * li_) / den
        g_im = (xi * lr_ - xr * li_) / den
        dgr_, dgi_ = dgr_ref[...], dgi_ref[...]
        dxr = (dgr_ * lr_ - dgi_ * li_) / den
        dxi = (dgr_ * li_ + dgi_ * lr_) / den
        dden = -(dgr_ * g_re + dgi_ * g_im) / den
        dlr = (dgr_ * xr + dgi_ * xi) / den + 2.0 * dden * lr_
        dli = (dgr_ * xi - dgi_ * xr) / den + 2.0 * dden * li_
        da_r = dar_ref[...] + dxr
        da_i = dai_ref[...] + dxi
        dmag_mag = da_r * ar + da_i * ai
        dth = da_i * ar - da_r * ai
        dlr_ref[...] = dlr + dmag_mag * dt
        dli_ref[...] = dli + dth * dt
        ddt = jnp.sum(dmag_mag * lr_ + dth * li_, axis=-1, keepdims=True)
        dls_ref[...] = ddt * dt

    sq = pl.BlockSpec((G, Pn), lambda: (0, 0))
    c1 = pl.BlockSpec((G, 1), lambda: (0, 0))
    return _call(
        body, name=name, in_specs=[sq, sq, c1, sq, sq, sq, sq], out_specs=[sq, sq, c1],
        out_shape=[jax.ShapeDtypeStruct((G, Pn), F32)] * 2 + [jax.ShapeDtypeStruct((G, 1), F32)],
    )(lr, li, ls, dar, dai, dgr, dgi)


def _s5_bb_fwd(gr, gi, br, bi, *, name):
    R, C = br.shape

    def body(gr_ref, gi_ref, br_ref, bi_ref, or_ref, oi_ref):
        g_r, g_i, b_r, b_i = gr_ref[...], gi_ref[...], br_ref[...], bi_ref[...]
        or_ref[...] = g_r * b_r - g_i * b_i
        oi_ref[...] = g_r * b_i + g_i * b_r

    w = pl.BlockSpec((R, C), lambda: (0, 0))
    c1 = pl.BlockSpec((R, 1), lambda: (0, 0))
    return _call(body, name=name, in_specs=[c1, c1, w, w], out_specs=[w, w],
                 out_shape=[jax.ShapeDtypeStruct((R, C), F32)] * 2)(gr, gi, br, bi)


def _s5_bb_bwd(gr, gi, br, bi, dbbr, dbbi, *, name):
    R, C = br.shape

    def body(gr_ref, gi_ref, br_ref, bi_ref, dr_ref, di_ref, dbr_ref, dbi_ref, dgr_ref, dgi_ref):
        g_r, g_i, b_r, b_i = gr_ref[...], gi_ref[...], br_ref[...], bi_ref[...]
        d_r, d_i = dr_ref[...], di_ref[...]
        dbr_ref[...] = g_r * d_r + g_i * d_i
        dbi_ref[...] = g_r * d_i - g_i * d_r
        dgr_ref[...] = jnp.sum(d_r * b_r + d_i * b_i, axis=-1, keepdims=True)
        dgi_ref[...] = jnp.sum(d_i * b_r - d_r * b_i, axis=-1, keepdims=True)

    w = pl.BlockSpec((R, C), lambda: (0, 0))
    c1 = pl.BlockSpec((R, 1), lambda: (0, 0))
    return _call(body, name=name, in_specs=[c1, c1, w, w, w, w], out_specs=[w, w, c1, c1],
                 out_shape=[jax.ShapeDtypeStruct((R, C), F32)] * 2 + [jax.ShapeDtypeStruct((R, 1), F32)] * 2,
                 )(gr, gi, br, bi, dbbr, dbbi)


_DIAG_TILE = 8


def _diag_mask(gr, gc):
    rows, cols = _DIAG_TILE * gr, _DIAG_TILE * gc
    r = lax.broadcasted_iota(jnp.int32, (rows, cols), 0) >> (gr.bit_length() - 1)
    c = lax.broadcasted_iota(jnp.int32, (rows, cols), 1) >> (gc.bit_length() - 1)
    return r == c


def _diag_expand(t2, gr, gc, *, name, after=()):
    _, R, _ = t2.shape
    G = R // gr
    nt = G // _DIAG_TILE
    rows, cols = _DIAG_TILE * gr, _DIAG_TILE * gc

    def body(t_ref, o_ref):
        src = lax.broadcasted_iota(jnp.int32, (gc, cols), 0)
        dst = lax.broadcasted_iota(jnp.int32, (gc, cols), 1) & (gc - 1)
        spread = (src == dst).astype(BF16)
        y = jnp.dot(t_ref[...].astype(BF16), spread, preferred_element_type=F32)
        o_ref[...] = jnp.where(_diag_mask(gr, gc), y, 0.0).astype(BF16)

    return _call(
        body, after=after, name=name, grid=(2, nt),
        in_specs=[pl.BlockSpec((None, rows, gc), lambda p, i: (p, i, 0))],
        out_specs=pl.BlockSpec((None, rows, cols), lambda p, i: (p, i, i)),
        out_shape=jax.ShapeDtypeStruct((2, R, G * gc), BF16),
        compiler_params=_cparams(("parallel",) * 2),
    )(t2)


def _diag_extract(xd, gr, gc, *, name):
    _, R, _ = xd.shape
    nt = R // gr // _DIAG_TILE
    rows, cols = _DIAG_TILE * gr, _DIAG_TILE * gc

    def body(x_ref, o_ref):
        src = lax.broadcasted_iota(jnp.int32, (cols, gc), 0) & (gc - 1)
        dst = lax.broadcasted_iota(jnp.int32, (cols, gc), 1)
        fold = (src == dst).astype(BF16)
        parts = _split3(jnp.where(_diag_mask(gr, gc), x_ref[...], 0.0))
        acc = jnp.dot(parts[2], fold, preferred_element_type=F32)
        acc = acc + jnp.dot(parts[1], fold, preferred_element_type=F32)
        o_ref[...] = acc + jnp.dot(parts[0], fold, preferred_element_type=F32)

    return _call(
        body, name=name, grid=(2, nt),
        in_specs=[pl.BlockSpec((None, rows, cols), lambda p, i: (p, i, i))],
        out_specs=pl.BlockSpec((None, rows, gc), lambda p, i: (p, i, 0)),
        out_shape=jax.ShapeDtypeStruct((2, R, gc), F32),
        compiler_params=_cparams(("parallel",) * 2),
    )(xd)


SCAN_BLOCK = 8


def _cpowers(ar, ai, sign):
    ai = sign * ai
    out = [(ar, ai)]
    for _ in range(SCAN_BLOCK - 1):
        pr, pi = out[-1]
        out.append((pr * ar - pi * ai, pr * ai + pi * ar))
    return out


def _row_table(pw, row, index_of_row):
    tr_ = jnp.broadcast_to(pw[index_of_row(0)][0], row.shape)
    ti_ = jnp.broadcast_to(pw[index_of_row(0)][1], row.shape)
    for r in range(1, SCAN_BLOCK):
        pr, pi = pw[index_of_row(r)]
        tr_ = jnp.where(row == r, pr, tr_)
        ti_ = jnp.where(row == r, pi, ti_)
    return tr_, ti_


def _s5_scan_fwd(bu, a, *, name):
    _, S, N = bu.shape
    tc = 512
    nt = N // tc

    def body(a_ref, b_ref, h_ref):
        pw = _cpowers(a_ref[0], a_ref[1], 1.0)
        row = lax.broadcasted_iota(jnp.int32, (SCAN_BLOCK, tc), 0)
        lead_r, lead_i = _row_table(pw, row, lambda r: r)
        mult = {sh: (jnp.where(row >= sh, pw[sh - 1][0], 0.0), jnp.where(row >= sh, pw[sh - 1][1], 0.0))
                for sh in (1, 2, 4)}

        def step(k, carry):
            cr, ci = carry
            rows = pl.ds(pl.multiple_of(k * SCAN_BLOCK, SCAN_BLOCK), SCAN_BLOCK)
            xr, xi = b_ref[0, rows, :], b_ref[1, rows, :]
            for sh in (1, 2, 4):
                sr, si = pltpu.roll(xr, sh, 0), pltpu.roll(xi, sh, 0)
                kr, ki = mult[sh]
                xr, xi = xr + kr * sr - ki * si, xi + kr * si + ki * sr
            h_ref[0, rows, :] = xr + lead_r * cr - lead_i * ci
            h_ref[1, rows, :] = xi + lead_r * ci + lead_i * cr
            last = row == SCAN_BLOCK - 1
            tr_ = jnp.sum(jnp.where(last, xr, 0.0), axis=0, keepdims=True)
            ti_ = jnp.sum(jnp.where(last, xi, 0.0), axis=0, keepdims=True)
            a8r, a8i = pw[SCAN_BLOCK - 1]
            return a8r * cr - a8i * ci + tr_, a8r * ci + a8i * cr + ti_

        z = jnp.zeros((1, tc), F32)
        lax.fori_loop(0, S // SCAN_BLOCK, step, (z, z), unroll=2)

    vec = pl.BlockSpec((2, 1, tc), lambda j: (0, 0, j))
    mat = pl.BlockSpec((2, S, tc), lambda j: (0, 0, j))
    return _call(
        body, name=name, grid=(nt,), in_specs=[vec, mat], out_specs=mat,
        out_shape=jax.ShapeDtypeStruct((2, S, N), F32),
        compiler_params=_cparams(("parallel",)),
    )(a, bu)


def _s5_scan_bwd(g, h, a, *, name):
    _, S, N = g.shape
    tc = 256
    nt = N // tc

    def body(a_ref, g_ref, h_ref, l_ref, da_ref):
        pw = _cpowers(a_ref[0], a_ref[1], -1.0)
        row = lax.broadcasted_iota(jnp.int32, (SCAN_BLOCK, tc), 0)
        tail_r, tail_i = _row_table(pw, row, lambda r: SCAN_BLOCK - 1 - r)
        nb = S // SCAN_BLOCK
        mult = {sh: (jnp.where(row < SCAN_BLOCK - sh, pw[sh - 1][0], 0.0),
                     jnp.where(row < SCAN_BLOCK - sh, pw[sh - 1][1], 0.0)) for sh in (1, 2, 4)}

        def step(i, carry):
            k = nb - 1 - i
            cr, ci, dar, dai = carry
            rows = pl.ds(pl.multiple_of(k * SCAN_BLOCK, SCAN_BLOCK), SCAN_BLOCK)
            xr, xi = g_ref[0, rows, :], g_ref[1, rows, :]
            for sh in (1, 2, 4):
                sr, si = pltpu.roll(xr, SCAN_BLOCK - sh, 0), pltpu.roll(xi, SCAN_BLOCK - sh, 0)
                kr, ki = mult[sh]
                xr, xi = xr + kr * sr - ki * si, xi + kr * si + ki * sr
            lr = xr + tail_r * cr - tail_i * ci
            li = xi + tail_r * ci + tail_i * cr
            l_ref[0, rows, :] = lr
            l_ref[1, rows, :] = li
            prev = pl.ds(pl.multiple_of(jnp.maximum(k - 1, 0) * SCAN_BLOCK, SCAN_BLOCK), SCAN_BLOCK)
            has_prev = jnp.where(k > 0, 1.0, 0.0).astype(F32)
            first = row == 0
            hpr = jnp.where(first, pltpu.roll(h_ref[0, prev, :], 1, 0) * has_prev, pltpu.roll(h_ref[0, rows, :], 1, 0))
            hpi = jnp.where(first, pltpu.roll(h_ref[1, prev, :], 1, 0) * has_prev, pltpu.roll(h_ref[1, rows, :], 1, 0))
            tr_ = jnp.sum(jnp.where(first, xr, 0.0), axis=0, keepdims=True)
            ti_ = jnp.sum(jnp.where(first, xi, 0.0), axis=0, keepdims=True)
            a8r, a8i = pw[SCAN_BLOCK - 1]
            return (a8r * cr - a8i * ci + tr_, a8r * ci + a8i * cr + ti_,
                    dar + lr * hpr + li * hpi, dai + li * hpr - lr * hpi)

        z = jnp.zeros((1, tc), F32)
        z8 = jnp.zeros((SCAN_BLOCK, tc), F32)
        _, _, dar, dai = lax.fori_loop(0, nb, step, (z, z, z8, z8), unroll=2)
        da_ref[0] = jnp.sum(dar, axis=0, keepdims=True)
        da_ref[1] = jnp.sum(dai, axis=0, keepdims=True)

    vec = pl.BlockSpec((2, 1, tc), lambda j: (0, 0, j))
    mat = pl.BlockSpec((2, S, tc), lambda j: (0, 0, j))
    return _call(
        body, name=name, grid=(nt,), in_specs=[vec, mat, mat], out_specs=[mat, vec],
        out_shape=[jax.ShapeDtypeStruct((2, S, N), F32), jax.ShapeDtypeStruct((2, 1, N), F32)],
        compiler_params=_cparams(("parallel",)),
    )(a, g, h)


_GELU_C = math.sqrt(2.0 / math.pi)


def _s5_out_fwd(yc, P, dskip, *, name):
    S, W = yc.shape
    tr = _pick(S, 256)
    ub = 3 * FOX_WIDTH // W

    def body(yc_ref, u_ref, d_ref, y_ref, yg_ref):
        y = yc_ref[...] + d_ref[...] * u_ref[...]
        y_ref[...] = y
        t = jnp.tanh(_GELU_C * (y + 0.044715 * y * y * y))
        yg_ref[...] = (0.5 * y * (1.0 + t)).astype(BF16)

    row = pl.BlockSpec((tr, W), lambda i: (i, 0))
    return _call(
        body, name=name, grid=(S // tr,),
        in_specs=[row, pl.BlockSpec((tr, W), lambda i: (i, ub)), pl.BlockSpec((1, W), lambda i: (0, 0))],
        out_specs=[row, row],
        out_shape=[jax.ShapeDtypeStruct((S, W), F32), jax.ShapeDtypeStruct((S, W), BF16)],
        compiler_params=_cparams(("parallel",)),
    )(yc, P, dskip)


def _s5_out_bwd(dyg, y, P, dskip, *, name):
    S, W = y.shape
    tr = _pick(S, 256)
    ub = 3 * FOX_WIDTH // W

    def body(dyg_ref, y_ref, u_ref, d_ref, dy_ref, du_ref, dd_ref):
        y_ = y_ref[...]
        inner = _GELU_C * (y_ + 0.044715 * y_ * y_ * y_)
        t = jnp.tanh(inner)
        dgelu = 0.5 * (1.0 + t) + 0.5 * y_ * (1.0 - t * t) * _GELU_C * (1.0 + 3.0 * 0.044715 * y_ * y_)
        dy = dyg_ref[...] * dgelu
        dy_ref[...] = dy.astype(BF16)
        du_ref[...] = d_ref[...] * dy
        part = jnp.sum(dy * u_ref[...], axis=0, keepdims=True)

        @pl.when(pl.program_id(0) == 0)
        def _():
            dd_ref[...] = part

        @pl.when(pl.program_id(0) > 0)
        def _():
            dd_ref[...] += part

    row = pl.BlockSpec((tr, W), lambda i: (i, 0))
    vec = pl.BlockSpec((1, W), lambda i: (0, 0))
    return _call(
        body, name=name, grid=(S // tr,),
        in_specs=[row, row, pl.BlockSpec((tr, W), lambda i: (i, ub)), vec],
        out_specs=[row, row, vec],
        out_shape=[jax.ShapeDtypeStruct((S, W), BF16), jax.ShapeDtypeStruct((S, W), F32),
                   jax.ShapeDtypeStruct((1, W), F32)],
        compiler_params=_cparams(("arbitrary",)),
    )(dyg, y, P, dskip)


def _glu_fwd(z, fox, *, name):
    S, W2 = z.shape
    W = W2 // 2
    tr = _pick(S, 256)

    def body(z1_ref, z2_ref, f_ref, o_ref):
        o_ref[:, :W] = f_ref[...].astype(BF16)
        o_ref[:, W:] = (z1_ref[...] * jax.nn.sigmoid(z2_ref[...])).astype(BF16)

    lo = pl.BlockSpec((tr, W), lambda i: (i, 0))
    return _call(
        body, name=name, grid=(S // tr,),
        in_specs=[lo, pl.BlockSpec((tr, W), lambda i: (i, 1)), lo],
        out_specs=pl.BlockSpec((tr, W2), lambda i: (i, 0)),
        out_shape=jax.ShapeDtypeStruct((S, W2), BF16),
        compiler_params=_cparams(("parallel",)),
    )(z, z, fox)


def _glu_bwd(z, dcat, *, name):
    S, W2 = z.shape
    W = W2 // 2
    tr = _pick(S, 256)

    def body(z1_ref, z2_ref, d_ref, dz_ref):
        sg = jax.nn.sigmoid(z2_ref[...])
        d = d_ref[...]
        dz_ref[:, :W] = (d * sg).astype(BF16)
        dz_ref[:, W:] = (d * z1_ref[...] * sg * (1.0 - sg)).astype(BF16)

    lo = pl.BlockSpec((tr, W), lambda i: (i, 0))
    hi = pl.BlockSpec((tr, W), lambda i: (i, 1))
    return _call(
        body, name=name, grid=(S // tr,), in_specs=[lo, hi, hi],
        out_specs=pl.BlockSpec((tr, W2), lambda i: (i, 0)),
        out_shape=jax.ShapeDtypeStruct((S, W2), BF16),
        compiler_params=_cparams(("parallel",)),
    )(z, z, dcat)


ACT_ROWS = 16
ACT_COLS = 256


def _shift_down(cur, prev, k, row):
    return jnp.where(row >= k, pltpu.roll(cur, k, 0), pltpu.roll(prev, k, 0))


def _shift_up(cur, nxt, k, row):
    n = cur.shape[0]
    return jnp.where(row < n - k, pltpu.roll(cur, n - k, 0), pltpu.roll(nxt, n - k, 0))


def _act_fwd(h, cw, cb, *, name):
    _, S, FP = h.shape
    tr = _pick(S, 256)
    hb = tr // ACT_ROWS
    nq = tr // ACT_ROWS

    def body(g_ref, gh_ref, v_ref, vh_ref, wg_ref, wv_ref, bg_ref, bv_ref, a_ref, hc_ref):
        first = pl.program_id(1) == 0
        for c0 in range(0, FP, ACT_COLS):
            cw_ = min(ACT_COLS, FP - c0)
            cols = pl.ds(c0, cw_)
            rw = lax.broadcasted_iota(jnp.int32, (ACT_ROWS, cw_), 0)
            wg = [wg_ref[pl.ds(k, 1), cols] for k in range(3)]
            wv = [wv_ref[pl.ds(k, 1), cols] for k in range(3)]
            bg, bv = bg_ref[:, cols], bv_ref[:, cols]
            halo_g = jnp.where(first, 0.0, gh_ref[:, cols])
            halo_v = jnp.where(first, 0.0, vh_ref[:, cols])

            def chunk(q, _):
                rows = pl.ds(pl.multiple_of(q * ACT_ROWS, ACT_ROWS), ACT_ROWS)
                before = pl.ds(pl.multiple_of(jnp.maximum(q - 1, 0) * ACT_ROWS, ACT_ROWS), ACT_ROWS)
                g, v = g_ref[rows, cols], v_ref[rows, cols]
                gp = jnp.where(q > 0, g_ref[before, cols], halo_g)
                vp = jnp.where(q > 0, v_ref[before, cols], halo_v)
                cg = bg + wg[2] * g + wg[1] * _shift_down(g, gp, 1, rw) + wg[0] * _shift_down(g, gp, 2, rw)
                cv = bv + wv[2] * v + wv[1] * _shift_down(v, vp, 1, rw) + wv[0] * _shift_down(v, vp, 2, rw)
                a_ref[rows, cols] = (cg * jax.nn.sigmoid(cg) * cv).astype(BF16)
                hc_ref[0, rows, cols] = cg
                hc_ref[1, rows, cols] = cv
                return 0

            lax.fori_loop(0, nq, chunk, 0, unroll=2)

    def main(off):
        return pl.BlockSpec((None, tr, FP), lambda j, i: (j + off, i, 0))

    def halo(off):
        return pl.BlockSpec((None, ACT_ROWS, FP), lambda j, i: (j + off, jnp.maximum(i * hb - 1, 0), 0))

    def wspec(off):
        return pl.BlockSpec((None, 3, FP), lambda j, i: (j + off, 0, 0))

    def bspec(off):
        return pl.BlockSpec((None, 1, FP), lambda j, i: (j + off, 0, 0))

    cb3 = cb.reshape(4, 1, FP)
    return _call(
        body, name=name, grid=(2, S // tr),
        in_specs=[main(0), halo(0), main(2), halo(2), wspec(0), wspec(2), bspec(0), bspec(2)],
        out_specs=[pl.BlockSpec((None, tr, FP), lambda j, i: (j, i, 0)),
                   pl.BlockSpec((None, 2, tr, FP), lambda j, i: (j, 0, i, 0))],
        out_shape=[jax.ShapeDtypeStruct((2, S, FP), BF16), jax.ShapeDtypeStruct((2, 2, S, FP), F32)],
        compiler_params=_cparams(("parallel", "parallel")),
    )(h, h, h, h, cw, cw, cb3, cb3)


def _act_bwd(h, hc, da, cw, *, name):
    _, S, FP = h.shape
    tr = _pick(S, 256)
    nq = tr // ACT_ROWS
    nr = S // tr
    half = ACT_ROWS // 2

    def fold(x):
        return x[:half] + x[half:]

    def body(g_ref, v_ref, hc_ref, da_ref, wg_ref, wv_ref,
             dh_ref, dwg_ref, dwv_ref, dbg_ref, dbv_ref, carry_g, carry_v):
        i = pl.program_id(1)
        bottom = i == 0
        for c0 in range(0, FP, ACT_COLS):
            cw_ = min(ACT_COLS, FP - c0)
            cols = pl.ds(c0, cw_)
            rw = lax.broadcasted_iota(jnp.int32, (ACT_ROWS, cw_), 0)
            wg = [wg_ref[pl.ds(k, 1), cols] for k in range(3)]
            wv = [wv_ref[pl.ds(k, 1), cols] for k in range(3)]
            after_g = jnp.where(bottom, 0.0, carry_g[:, cols])
            after_v = jnp.where(bottom, 0.0, carry_v[:, cols])

            def chunk(s, carry):
                ng, nv, acc = carry[0], carry[1], carry[2:]
                q = nq - 1 - s
                rows = pl.ds(pl.multiple_of(q * ACT_ROWS, ACT_ROWS), ACT_ROWS)
                g, v = g_ref[rows, cols], v_ref[rows, cols]
                cg, cv = hc_ref[0, rows, cols], hc_ref[1, rows, cols]
                sg = jax.nn.sigmoid(cg)
                d = da_ref[rows, cols]
                dcg = d * cv * sg * (1.0 + cg * (1.0 - sg))
                dcv = d * cg * sg
                ug1, ug2 = _shift_up(dcg, ng, 1, rw), _shift_up(dcg, ng, 2, rw)
                uv1, uv2 = _shift_up(dcv, nv, 1, rw), _shift_up(dcv, nv, 2, rw)
                dh_ref[0, rows, cols] = (wg[2] * dcg + wg[1] * ug1 + wg[0] * ug2).astype(BF16)
                dh_ref[1, rows, cols] = (wv[2] * dcv + wv[1] * uv1 + wv[0] * uv2).astype(BF16)
                terms = (ug2 * g, ug1 * g, dcg * g, dcg, uv2 * v, uv1 * v, dcv * v, dcv)
                return (dcg, dcv) + tuple(a + fold(t) for a, t in zip(acc, terms))

            zero = jnp.zeros((half, cw_), F32)
            out = lax.fori_loop(0, nq, chunk, (after_g, after_v) + (zero,) * 8, unroll=2)
            carry_g[:, cols] = out[0]
            carry_v[:, cols] = out[1]
            sums = [jnp.sum(a, axis=0, keepdims=True) for a in out[2:]]

            @pl.when(bottom)
            def _():
                for k in range(3):
                    dwg_ref[pl.ds(k, 1), cols] = sums[k]
                    dwv_ref[pl.ds(k, 1), cols] = sums[4 + k]
                dbg_ref[:, cols] = sums[3]
                dbv_ref[:, cols] = sums[7]

            @pl.when(jnp.logical_not(bottom))
            def _():
                for k in range(3):
                    dwg_ref[pl.ds(k, 1), cols] += sums[k]
                    dwv_ref[pl.ds(k, 1), cols] += sums[4 + k]
                dbg_ref[:, cols] += sums[3]
                dbv_ref[:, cols] += sums[7]

    def main(off):
        return pl.BlockSpec((None, tr, FP), lambda j, i: (j + off, nr - 1 - i, 0))

    def wspec(off):
        return pl.BlockSpec((None, 3, FP), lambda j, i: (j + off, 0, 0))

    bspec = pl.BlockSpec((None, 1, FP), lambda j, i: (j, 0, 0))
    pair = pl.BlockSpec((None, 2, tr, FP), lambda j, i: (j, 0, nr - 1 - i, 0))
    dh, dwg, dwv, dbg, dbv = _call(
        body, name=name, grid=(2, nr),
        in_specs=[main(0), main(2), pair, main(0), wspec(0), wspec(2)],
        out_specs=[pair, wspec(0), wspec(0), bspec, bspec],
        out_shape=[jax.ShapeDtypeStruct((2, 2, S, FP), BF16)]
        + [jax.ShapeDtypeStruct((2, 3, FP), F32)] * 2 + [jax.ShapeDtypeStruct((2, 1, FP), F32)] * 2,
        scratch_shapes=[pltpu.VMEM((ACT_ROWS, FP), F32), pltpu.VMEM((ACT_ROWS, FP), F32)],
        compiler_params=_cparams(("parallel", "arbitrary")),
    )(h, h, hc, da, cw, cw)
    return (dh.reshape(4, S, FP), jnp.concatenate([dwg, dwv], axis=0), jnp.concatenate([dbg, dbv], axis=0))


def _rope_tables(posf, *, name, after=()):
    S = posf.shape[0]
    half = ROPE_DIM // 2
    d = np.arange(LANE) % SWA_HEAD_DIM
    invf = np.where(d < ROPE_DIM, ROPE_THETA ** (-(d % half).astype(np.float64) / half), 0.0).astype(np.float32)
    m_rot = (d < ROPE_DIM).astype(np.float32)
    m_a = (d < half).astype(np.float32)
    m_b = ((d >= half) & (d < ROPE_DIM)).astype(np.float32)
    consts = jnp.asarray(np.stack([invf, m_rot, m_a, m_b] + [np.zeros(LANE, np.float32)] * 4))

    def body(p_ref, k_ref, c_ref, sa_ref, sb_ref):
        k = k_ref[...]
        ang = p_ref[...] * k[0:1]
        co, si = jnp.cos(ang), jnp.sin(ang)
        c_ref[...] = k[1:2] * co + (1.0 - k[1:2])
        sa_ref[...] = -k[2:3] * si
        sb_ref[...] = k[3:4] * si

    full = pl.BlockSpec((S, LANE), lambda: (0, 0))
    return _call(
        body, after=after, name=name,
        in_specs=[pl.BlockSpec((S, 1), lambda: (0, 0)), pl.BlockSpec((8, LANE), lambda: (0, 0))],
        out_specs=[full] * 3, out_shape=[jax.ShapeDtypeStruct((S, LANE), F32)] * 3,
    )(posf, consts)


def _rope(xv, tabs_refs, width, inverse):
    rep = width // LANE
    c, sa, sb = (jnp.tile(t[...], (1, rep)) for t in tabs_refs)
    if not inverse:
        return xv * c + pltpu.roll(xv, width - 8, 1) * sa + pltpu.roll(xv, 8, 1) * sb
    return xv * c + pltpu.roll(xv * sa, 8, 1) + pltpu.roll(xv * sb, width - 8, 1)


def _to_heads(x, tabs, *, col0, width, rotate, name, out_dtype):
    S = x.shape[0]
    tr = _pick(S, 256)
    nh = width // SWA_HEAD_DIM
    cb = col0 // width

    def body(x_ref, c_ref, sa_ref, sb_ref, o_ref):
        xv = x_ref[...].astype(F32)
        if rotate:
            xv = _rope(xv, (c_ref, sa_ref, sb_ref), width, False)
        for h in range(nh):
            o_ref[h] = xv[:, h * SWA_HEAD_DIM:(h + 1) * SWA_HEAD_DIM].astype(out_dtype)

    tab = pl.BlockSpec((tr, LANE), lambda i: (i, 0))
    return _call(
        body, name=name, grid=(S // tr,),
        in_specs=[pl.BlockSpec((tr, width), lambda i: (i, cb)), tab, tab, tab],
        out_specs=pl.BlockSpec((nh, tr, SWA_HEAD_DIM), lambda i: (0, i, 0)),
        out_shape=jax.ShapeDtypeStruct((nh, S, SWA_HEAD_DIM), out_dtype),
        compiler_params=_cparams(("parallel",)),
    )(x, *tabs)


def _from_heads(x3, tabs, *, rotate_back, name, out_dtype, skip_rows=0):
    nh = x3.shape[0]
    S = x3.shape[1] - skip_rows
    width = nh * SWA_HEAD_DIM
    tr = _pick(S, 256) if skip_rows == 0 else skip_rows
    off = skip_rows // tr

    def body(x_ref, c_ref, sa_ref, sb_ref, o_ref):
        xv = jnp.concatenate([x_ref[h].astype(F32) for h in range(nh)], axis=1)
        if rotate_back:
            xv = _rope(xv, (c_ref, sa_ref, sb_ref), width, True)
        o_ref[...] = xv.astype(out_dtype)

    tab = pl.BlockSpec((tr, LANE), lambda i: (i, 0))
    return _call(
        body, name=name, grid=(S // tr,),
        in_specs=[pl.BlockSpec((nh, tr, SWA_HEAD_DIM), lambda i: (0, i + off, 0)), tab, tab, tab],
        out_specs=pl.BlockSpec((tr, width), lambda i: (i, 0)),
        out_shape=jax.ShapeDtypeStruct((S, width), out_dtype),
        compiler_params=_cparams(("parallel",)),
    )(x3, *tabs)


def _swa_mask(n):
    rows = SWA_GROUPS * SWA_WINDOW
    qi = lax.broadcasted_iota(jnp.int32, (rows, 2 * SWA_WINDOW), 0) & (SWA_WINDOW - 1)
    kj = lax.broadcasted_iota(jnp.int32, (rows, 2 * SWA_WINDOW), 1)
    rel = SWA_WINDOW + qi - kj
    return (rel >= 0) & (rel < SWA_WINDOW) & ((n > 0) | (kj >= SWA_WINDOW))


def _swa_fwd(qT, kT, vT, sink_rows, *, name):
    S = qT.shape[1]
    W, G, Dh = SWA_WINDOW, SWA_GROUPS, SWA_HEAD_DIM
    nb = S // W
    scale = 1.0 / math.sqrt(Dh)

    def body(q_ref, kp_ref, kc_ref, vp_ref, vc_ref, s_ref, o_ref, l_ref):
        n = pl.program_id(1)
        q = q_ref[...].reshape(G * W, Dh)
        kk = jnp.concatenate([kp_ref[...], kc_ref[...]], axis=0)
        vv = jnp.concatenate([vp_ref[...], vc_ref[...]], axis=0)
        s = lax.dot_general(q, kk, (((1,), (1,)), ((), ())), preferred_element_type=F32) * scale
        s = jnp.where(_swa_mask(n), s, -1e30)
        sink = s_ref[...]
        m = jnp.maximum(jnp.max(s, axis=-1, keepdims=True), sink)
        e = jnp.exp(s - m)
        den = jnp.sum(e, axis=-1, keepdims=True) + jnp.exp(sink - m)
        p = e * (1.0 / den)
        o_ref[...] = jnp.dot(p.astype(BF16), vv, preferred_element_type=F32).reshape(G, W, Dh)
        l_ref[...] = (m + jnp.log(den)).reshape(G, W, 1)

    qs = pl.BlockSpec((G, W, Dh), lambda g, n: (g, n, 0))
    prev = pl.BlockSpec((None, W, Dh), lambda g, n: (g, jnp.maximum(n - 1, 0), 0))
    cur = pl.BlockSpec((None, W, Dh), lambda g, n: (g, n, 0))
    return _call(
        body, name=name, grid=(SWA_KV_HEADS, nb),
        in_specs=[qs, prev, cur, prev, cur, pl.BlockSpec((None, G * W, 1), lambda g, n: (g, 0, 0))],
        out_specs=[qs, pl.BlockSpec((G, W, 1), lambda g, n: (g, n, 0))],
        out_shape=[jax.ShapeDtypeStruct((SWA_HEADS, S, Dh), F32), jax.ShapeDtypeStruct((SWA_HEADS, S, 1), F32)],
        compiler_params=_cparams(("parallel", "parallel")),
    )(qT, kT, kT, vT, vT, sink_rows)


def _swa_bwd(qT, kT, vT, sink_rows, oT, L, doT, *, name):
    S = qT.shape[1]
    W, G, Dh = SWA_WINDOW, SWA_GROUPS, SWA_HEAD_DIM
    nb = S // W
    scale = 1.0 / math.sqrt(Dh)

    def body(q_ref, kp_ref, kc_ref, vp_ref, vc_ref, s_ref, o_ref, l_ref, do_ref,
             dq_ref, dk_ref, dv_ref, ds_ref):
        n = pl.program_id(1)
        q = q_ref[...].reshape(G * W, Dh)
        kk = jnp.concatenate([kp_ref[...], kc_ref[...]], axis=0)
        vv = jnp.concatenate([vp_ref[...], vc_ref[...]], axis=0)
        s = lax.dot_general(q, kk, (((1,), (1,)), ((), ())), preferred_element_type=F32) * scale
        lrow = l_ref[...].reshape(G * W, 1)
        p = jnp.where(_swa_mask(n), jnp.exp(s - lrow), 0.0)
        do = do_ref[...].reshape(G * W, Dh)
        do_bf = do.astype(BF16)
        dp = lax.dot_general(do_bf, vv, (((1,), (1,)), ((), ())), preferred_element_type=F32)
        delta = jnp.sum(do * o_ref[...].reshape(G * W, Dh), axis=-1, keepdims=True)
        dsc = p * (dp - delta)
        ds_bf = dsc.astype(BF16)
        dq_ref[...] = (jnp.dot(ds_bf, kk, preferred_element_type=F32) * scale).astype(BF16).reshape(G, W, Dh)
        dkk = lax.dot_general(ds_bf, q, (((0,), (0,)), ((), ())), preferred_element_type=F32) * scale
        dvv = lax.dot_general(p.astype(BF16), do_bf, (((0,), (0,)), ((), ())), preferred_element_type=F32)
        dsk = -jnp.exp(s_ref[...] - lrow) * delta
        dsk = jnp.broadcast_to(jnp.sum(dsk.reshape(G, W, 1), axis=1), (G, LANE))

        @pl.when(n == 0)
        def _():
            dk_ref[...] = jnp.zeros_like(dk_ref)
            dv_ref[...] = jnp.zeros_like(dv_ref)
            ds_ref[...] = jnp.zeros_like(ds_ref)

        rows = pl.ds(pl.multiple_of(n * W, W), 2 * W)
        dk_ref[rows, :] += dkk
        dv_ref[rows, :] += dvv
        ds_ref[...] += dsk

    qs = pl.BlockSpec((G, W, Dh), lambda g, n: (g, n, 0))
    prev = pl.BlockSpec((None, W, Dh), lambda g, n: (g, jnp.maximum(n - 1, 0), 0))
    cur = pl.BlockSpec((None, W, Dh), lambda g, n: (g, n, 0))
    lsp = pl.BlockSpec((G, W, 1), lambda g, n: (g, n, 0))
    kvo = pl.BlockSpec((None, S + W, Dh), lambda g, n: (g, 0, 0))
    return _call(
        body, name=name, grid=(SWA_KV_HEADS, nb),
        in_specs=[qs, prev, cur, prev, cur, pl.BlockSpec((None, G * W, 1), lambda g, n: (g, 0, 0)), qs, lsp, qs],
        out_specs=[qs, kvo, kvo, pl.BlockSpec((None, G, LANE), lambda g, n: (g, 0, 0))],
        out_shape=[jax.ShapeDtypeStruct((SWA_HEADS, S, Dh), BF16),
                   jax.ShapeDtypeStruct((SWA_KV_HEADS, S + W, Dh), F32),
                   jax.ShapeDtypeStruct((SWA_KV_HEADS, S + W, Dh), F32),
                   jax.ShapeDtypeStruct((SWA_KV_HEADS, G, LANE), F32)],
        compiler_params=_cparams(("parallel", "arbitrary")),
    )(qT, kT, kT, vT, vT, sink_rows, oT, L, doT)


def _adamw(w, g, m, v, *, name, tr=128, by_cols=False):
    L, R, C = w.shape
    split = isinstance(g, (list, tuple))
    HR, HC = _half_shape(R, C, by_cols) if split else (R, C)
    tr, tc = _tile2d(HR, HC, tr)
    nr, nc = HR // tr, HC // tc
    c1 = 1.0 / (1.0 - ADAM_B1 ** ADAM_STEP)
    c2 = 1.0 / (1.0 - ADAM_B2 ** ADAM_STEP)
    ng = 2 * L if split else 1

    def body(c_ref, *refs):
        w_ref, g_refs, (m_ref, v_ref, go_ref, d_ref, mo_ref, vo_ref) = refs[0], refs[1:1 + ng], refs[1 + ng:]
        if split:
            mine = pl.program_id(1) == c_ref[0]
            g_ = jnp.where(mine, g_refs[0][...], g_refs[1][...])
            for l in range(1, L):
                g_ = jnp.where(pl.program_id(0) == l,
                               jnp.where(mine, g_refs[2 * l][...], g_refs[2 * l + 1][...]), g_)
        else:
            g_ = g_refs[0][...]
        mn = ADAM_B1 * m_ref[...] + (1.0 - ADAM_B1) * g_
        vn = ADAM_B2 * v_ref[...] + (1.0 - ADAM_B2) * (g_ * g_)
        go_ref[...] = g_
        mo_ref[...] = mn
        vo_ref[...] = vn
        d_ref[...] = -ADAM_LR * ((mn * c1) / (jnp.sqrt(vn * c2) + ADAM_EPS) + ADAM_WD * w_ref[...])

    def whole(l, hf, i, j, c):
        return (l, i, hf * nc + j) if by_cols else (l, hf * nr + i, j)

    def half(layer, own):
        def index(l, hf, i, j, c):
            used = (l == layer) & ((hf == c[0]) if own else (hf != c[0]))
            return jnp.where(used, i, 0), jnp.where(used, j, 0)
        return pl.BlockSpec((tr, tc), index)

    row = pl.BlockSpec((None, tr, tc), whole)
    gs = [h for pair in g for h in pair] if split else [g]
    g_specs = [half(l, own) for l in range(L) for own in (True, False)] if split else [row]
    core = lax.axis_index("c").astype(jnp.int32).reshape(1)
    return _call(
        body, name=name,
        grid_spec=pltpu.PrefetchScalarGridSpec(
            num_scalar_prefetch=1, grid=(L, 2 if split else 1, nr, nc),
            in_specs=[row] + g_specs + [row, row], out_specs=[row] * 4),
        out_shape=[jax.ShapeDtypeStruct((L, R, C), F32)] * 4,
        compiler_params=_cparams(("parallel",) * 4),
    )(core, w, *gs, m, v)


def _adamw_half(w, g, m, v, *, name, own, prev=None, tr=128, by_cols=False):
    L, R, C = w.shape
    HR, HC = _half_shape(R, C, by_cols)
    tr, tc = _tile2d(HR, HC, tr)
    nr, nc = HR // tr, HC // tc
    c1 = 1.0 / (1.0 - ADAM_B1 ** ADAM_STEP)
    c2 = 1.0 / (1.0 - ADAM_B2 ** ADAM_STEP)

    def body(c_ref, *refs):
        w_ref, g_refs, m_ref, v_ref = refs[0], refs[1:1 + L], refs[1 + L], refs[2 + L]
        go_ref, d_ref, mo_ref, vo_ref = refs[-4:]
        g_ = g_refs[0][...]
        for l in range(1, L):
            g_ = jnp.where(pl.program_id(0) == l, g_refs[l][...], g_)
        mn = ADAM_B1 * m_ref[...] + (1.0 - ADAM_B1) * g_
        vn = ADAM_B2 * v_ref[...] + (1.0 - ADAM_B2) * (g_ * g_)
        go_ref[...] = g_
        mo_ref[...] = mn
        vo_ref[...] = vn
        d_ref[...] = -ADAM_LR * ((mn * c1) / (jnp.sqrt(vn * c2) + ADAM_EPS) + ADAM_WD * w_ref[...])

    def whole(l, i, j, c):
        hf = c[0] if own else 1 - c[0]
        return (l, i, hf * nc + j) if by_cols else (l, hf * nr + i, j)

    def layer_half(layer):
        def index(l, i, j, c):
            return jnp.where(l == layer, i, 0), jnp.where(l == layer, j, 0)
        return pl.BlockSpec((tr, tc), index)

    row = pl.BlockSpec((None, tr, tc), whole)
    core = lax.axis_index("c").astype(jnp.int32).reshape(1)
    prev = list(prev) if prev is not None else []
    return _call(
        body, name=name,
        grid_spec=pltpu.PrefetchScalarGridSpec(
            num_scalar_prefetch=1, grid=(L, nr, nc),
            in_specs=[row] + [layer_half(l) for l in range(L)] + [row, row] + [ANY] * len(prev),
            out_specs=[row] * 4),
        out_shape=[jax.ShapeDtypeStruct((L, R, C), F32)] * 4,
        input_output_aliases={4 + L + k: k for k in range(len(prev))},
        compiler_params=_cparams(("parallel",) * 3),
    )(core, w, *g, m, v, *prev)


def _sum2_halves(g4, s4, by_cols, *, name):
    n, R, C = g4.shape
    HR, HC = _half_shape(R, C, by_cols)
    tr, tc = _tile2d(HR, HC, budget=1024 * 1024)
    nr, nc = HR // tr, HC // tc
    core = lax.axis_index("c").astype(jnp.int32).reshape(1)

    def body(c_ref, g_ref, s_ref, o_ref):
        o_ref[...] = (g_ref[...].astype(F32) + s_ref[...].astype(F32)).astype(BF16)

    def mine(k, i, j, c):
        return (k, i, c[0] * nc + j) if by_cols else (k, c[0] * nr + i, j)

    blk = pl.BlockSpec((None, tr, tc), lambda k, i, j, c: (k, i, j))
    return _call(
        body, name=name,
        grid_spec=pltpu.PrefetchScalarGridSpec(
            num_scalar_prefetch=1, grid=(n, nr, nc),
            in_specs=[pl.BlockSpec((None, tr, tc), mine), blk], out_specs=blk),
        out_shape=jax.ShapeDtypeStruct((n, HR, HC), BF16),
        compiler_params=_cparams(("parallel", "parallel", "parallel")),
    )(core, g4, s4)


def _rowsum(parts, *, name, out_dtype=F32):
    n, R, C = parts.shape
    tr, tc = _tile2d(R, C, budget=512 * 1024)

    def body(p_ref, o_ref):
        acc = p_ref[0].astype(F32)
        for i in range(1, n):
            acc = acc + p_ref[i].astype(F32)
        o_ref[...] = acc.astype(out_dtype)

    return _call(
        body, name=name, grid=(R // tr, C // tc),
        in_specs=[pl.BlockSpec((n, tr, tc), lambda i, j: (0, i, j))],
        out_specs=pl.BlockSpec((tr, tc), lambda i, j: (i, j)),
        out_shape=jax.ShapeDtypeStruct((R, C), out_dtype),
        compiler_params=_cparams(("parallel", "parallel")),
    )(parts)


def _where_am_i():
    x, y, c = lax.axis_index("x"), lax.axis_index("y"), lax.axis_index("c")
    chips = [(1 - x, y), (x, 1 - y), (1 - x, 1 - y)]
    return x, y, c, chips


def _half_idx(rows, cols, by_cols, which):
    if by_cols:
        hc = cols // 2
        return (slice(None), pl.ds(pl.multiple_of(which * hc, LANE), hc))
    hr = rows // 2
    return (pl.ds(pl.multiple_of(which * hr, 16), hr), slice(None))


def _half_shape(rows, cols, by_cols):
    return (rows, cols // 2) if by_cols else (rows // 2, cols)


HBM_SPEC = pl.BlockSpec(memory_space=pltpu.HBM)
SEM_SPEC = pl.BlockSpec(memory_space=pltpu.SEMAPHORE)
DATAFLOW = pltpu.SideEffectType.DATAFLOW_SIDE_EFFECTING


def _chip_exchange_refs(kind, shards_shape, by_cols, src, land, i, chip_k, c, me):
    if kind == 'gather':
        half = _half_idx(*shards_shape, by_cols, c)
        return src.at[half], land.at[(me,) + half], land.at[(chip_k,) + half]
    return src.at[chip_k], land.at[me], land.at[chip_k]


def _chip_exchange_start(kind, srcs, by_cols, *, name, after=()):
    n = len(srcs)
    land_shapes = [((N_CHIPS,) + s.shape) if kind == 'gather' else s.shape for s in srcs]

    def body(*refs):
        src_refs, land_refs = refs[:n], refs[n:2 * n]
        send, recv = refs[2 * n + len(after)], refs[2 * n + len(after) + 1]
        token = refs[-1]
        x, y, c, chips = _where_am_i()
        me = 2 * x + y
        for i in range(n):
            for k, (px, py) in enumerate(chips):
                s, d, _ = _chip_exchange_refs(kind, srcs[i].shape, by_cols[i], src_refs[i], land_refs[i], i,
                                              2 * px + py, c, me)
                pltpu.make_async_remote_copy(src_ref=s, dst_ref=d, send_sem=send.at[3 * i + k],
                                             recv_sem=recv.at[3 * i + k], device_id=(px, py, c),
                                             device_id_type=MESH).start()
        token[...] = jnp.zeros_like(token)

    lands = [pltpu.with_memory_space_constraint(lax.empty(sh, s.dtype), pltpu.HBM) for sh, s in zip(land_shapes, srcs)]
    outs = _call(
        body, name=name,
        out_shape=(pltpu.SemaphoreType.DMA((3 * n,)), pltpu.SemaphoreType.DMA((3 * n,)),
                   *[pltpu.HBM(s.shape, s.dtype) for s in srcs],
                   *[pltpu.HBM(sh, s.dtype) for sh, s in zip(land_shapes, srcs)],
                   jax.ShapeDtypeStruct((8, LANE), F32)),
        in_specs=[HBM_SPEC] * (2 * n) + [ANY] * len(after),
        out_specs=(SEM_SPEC, SEM_SPEC, *([HBM_SPEC] * (2 * n)), pl.BlockSpec(memory_space=pltpu.VMEM)),
        input_output_aliases={j: 2 + j for j in range(2 * n)},
        compiler_params=pltpu.CompilerParams(has_side_effects=DATAFLOW),
    )(*[pltpu.with_memory_space_constraint(s, pltpu.HBM) for s in srcs], *lands, *after)
    return outs[0], outs[1], list(outs[2:2 + n]), list(outs[2 + n:2 + 2 * n]), outs[-1]


def _chip_exchange_wait(kind, send, recv, srcs, lands, by_cols, after, *, name):
    n = len(srcs)

    def body(*refs):
        src_refs, land_refs = refs[:n], refs[n:2 * n]
        send_r, recv_r = refs[2 * n], refs[2 * n + 1]
        x, y, c, chips = _where_am_i()
        me = 2 * x + y
        for i in range(n):
            for k, (px, py) in enumerate(chips):
                s, _, d = _chip_exchange_refs(kind, srcs[i].shape, by_cols[i], src_refs[i], land_refs[i], i,
                                              2 * px + py, c, me)
                cp = pltpu.make_async_remote_copy(src_ref=s, dst_ref=d, send_sem=send_r.at[3 * i + k],
                                                  recv_sem=recv_r.at[3 * i + k], device_id=(px, py, c),
                                                  device_id_type=MESH)
                cp.wait_send()
                cp.wait_recv()

    outs = _call(
        body, name=name,
        out_shape=(*[pltpu.HBM(s.shape, s.dtype) for s in srcs], *[pltpu.HBM(l.shape, l.dtype) for l in lands]),
        in_specs=[HBM_SPEC] * (2 * n) + [SEM_SPEC, SEM_SPEC] + [ANY] * len(after),
        out_specs=tuple([HBM_SPEC] * (2 * n)),
        input_output_aliases={j: j for j in range(2 * n)},
        compiler_params=pltpu.CompilerParams(has_side_effects=DATAFLOW),
    )(*srcs, *lands, send, recv, *after)
    return list(outs[:n]), list(outs[n:])


def _sibling_halves_start(grads, by_cols, *, name, after=()):
    n = len(grads)
    land_shapes = [(N_CHIPS,) + _half_shape(*g.shape[1:], bc) for g, bc in zip(grads, by_cols)]

    def body(*refs):
        src_refs, land_refs = refs[:n], refs[n:2 * n]
        send, recv = refs[2 * n + len(after)], refs[2 * n + len(after) + 1]
        token = refs[-1]
        x, y, c, _ = _where_am_i()
        for i in range(n):
            src = src_refs[i].at[(slice(None),) + _half_idx(*grads[i].shape[1:], by_cols[i], 1 - c)]
            pltpu.make_async_remote_copy(src_ref=src, dst_ref=land_refs[i], send_sem=send.at[i], recv_sem=recv.at[i],
                                         device_id=(x, y, 1 - c), device_id_type=MESH).start()
        token[...] = jnp.zeros_like(token)

    lands = [pltpu.with_memory_space_constraint(lax.empty(sh, g.dtype), pltpu.HBM) for sh, g in zip(land_shapes, grads)]
    outs = _call(
        body, name=name,
        out_shape=(pltpu.SemaphoreType.DMA((n,)), pltpu.SemaphoreType.DMA((n,)),
                   *[pltpu.HBM(g.shape, g.dtype) for g in grads],
                   *[pltpu.HBM(sh, g.dtype) for sh, g in zip(land_shapes, grads)],
                   jax.ShapeDtypeStruct((8, LANE), F32)),
        in_specs=[HBM_SPEC] * (2 * n) + [ANY] * len(after),
        out_specs=(SEM_SPEC, SEM_SPEC, *([HBM_SPEC] * (2 * n)), pl.BlockSpec(memory_space=pltpu.VMEM)),
        input_output_aliases={j: 2 + j for j in range(2 * n)},
        compiler_params=pltpu.CompilerParams(has_side_effects=DATAFLOW),
    )(*[pltpu.with_memory_space_constraint(g, pltpu.HBM) for g in grads], *lands, *after)
    return outs[0], outs[1], list(outs[2:2 + n]), list(outs[2 + n:2 + 2 * n]), outs[-1]


def _sibling_halves_wait(send, recv, grads, lands, by_cols, after, *, name):
    n = len(grads)

    def body(*refs):
        src_refs, land_refs = refs[:n], refs[n:2 * n]
        send_r, recv_r = refs[2 * n], refs[2 * n + 1]
        x, y, c, _ = _where_am_i()
        for i in range(n):
            src = src_refs[i].at[(slice(None),) + _half_idx(*grads[i].shape[1:], by_cols[i], 1 - c)]
            cp = pltpu.make_async_remote_copy(src_ref=src, dst_ref=land_refs[i], send_sem=send_r.at[i],
                                              recv_sem=recv_r.at[i], device_id=(x, y, 1 - c), device_id_type=MESH)
            cp.wait_send()
            cp.wait_recv()

    outs = _call(
        body, name=name,
        out_shape=(*[pltpu.HBM(g.shape, g.dtype) for g in grads], *[pltpu.HBM(l.shape, l.dtype) for l in lands]),
        in_specs=[HBM_SPEC] * (2 * n) + [SEM_SPEC, SEM_SPEC] + [ANY] * len(after),
        out_specs=tuple([HBM_SPEC] * (2 * n)),
        input_output_aliases={j: j for j in range(2 * n)},
        compiler_params=pltpu.CompilerParams(has_side_effects=DATAFLOW),
    )(*grads, *lands, send, recv, *after)
    return list(outs[:n]), list(outs[n:])


def _sibling_swap_start(arrs, *, name, after=()):
    n = len(arrs)

    def body(*refs):
        src_refs, land_refs = refs[:n], refs[n:2 * n]
        send, recv = refs[2 * n + len(after)], refs[2 * n + len(after) + 1]
        token = refs[-1]
        x, y, c, _ = _where_am_i()
        for i in range(n):
            pltpu.make_async_remote_copy(src_ref=src_refs[i], dst_ref=land_refs[i], send_sem=send.at[i],
                                         recv_sem=recv.at[i], device_id=(x, y, 1 - c), device_id_type=MESH).start()
        token[...] = jnp.zeros_like(token)

    lands = [pltpu.with_memory_space_constraint(lax.empty(a.shape, a.dtype), pltpu.HBM) for a in arrs]
    outs = _call(
        body, name=name,
        out_shape=(pltpu.SemaphoreType.DMA((n,)), pltpu.SemaphoreType.DMA((n,)),
                   *[pltpu.HBM(a.shape, a.dtype) for a in arrs] * 2, jax.ShapeDtypeStruct((8, LANE), F32)),
        in_specs=[HBM_SPEC] * (2 * n) + [ANY] * len(after),
        out_specs=(SEM_SPEC, SEM_SPEC, *([HBM_SPEC] * (2 * n)), pl.BlockSpec(memory_space=pltpu.VMEM)),
        input_output_aliases={j: 2 + j for j in range(2 * n)},
        compiler_params=pltpu.CompilerParams(has_side_effects=DATAFLOW),
    )(*[pltpu.with_memory_space_constraint(a, pltpu.HBM) for a in arrs], *lands, *after)
    return outs[0], outs[1], list(outs[2:2 + n]), list(outs[2 + n:2 + 2 * n]), outs[-1]


def _sibling_swap_wait(send, recv, arrs, lands, after, *, name):
    n = len(arrs)

    def body(*refs):
        src_refs, land_refs = refs[:n], refs[n:2 * n]
        send_r, recv_r = refs[2 * n], refs[2 * n + 1]
        x, y, c, _ = _where_am_i()
        for i in range(n):
            cp = pltpu.make_async_remote_copy(src_ref=src_refs[i], dst_ref=land_refs[i], send_sem=send_r.at[i],
                                              recv_sem=recv_r.at[i], device_id=(x, y, 1 - c), device_id_type=MESH)
            cp.wait_send()
            cp.wait_recv()

    outs = _call(
        body, name=name,
        out_shape=tuple(pltpu.HBM(a.shape, a.dtype) for a in list(arrs) + list(lands)),
        in_specs=[HBM_SPEC] * (2 * n) + [SEM_SPEC, SEM_SPEC] + [ANY] * len(after),
        out_specs=tuple([HBM_SPEC] * (2 * n)),
        input_output_aliases={j: j for j in range(2 * n)},
        compiler_params=pltpu.CompilerParams(has_side_effects=DATAFLOW),
    )(*arrs, *lands, send, recv, *after)
    return list(outs[:n]), list(outs[n:])


def _sibling_pass_gathered(lands, shard_shapes, by_cols, *, name):
    n = len(lands)

    def body(*refs):
        outs = refs[n:2 * n]
        send, recv = refs[2 * n:]
        x, y, c, chips = _where_am_i()
        sibling = (x, y, 1 - c)
        cps = []
        for i in range(n):
            for k, (px, py) in enumerate(chips):
                blk = outs[i].at[(2 * px + py,) + _half_idx(*shard_shapes[i], by_cols[i], c)]
                d = pltpu.make_async_remote_copy(src_ref=blk, dst_ref=blk, send_sem=send.at[i, k],
                                                 recv_sem=recv.at[i, k], device_id=sibling, device_id_type=MESH)
                d.start()
                cps.append(d)
        for i in range(n):
            for k, (px, py) in enumerate(chips):
                blk = outs[i].at[(2 * px + py,) + _half_idx(*shard_shapes[i], by_cols[i], 1 - c)]
                pltpu.make_async_remote_copy(src_ref=blk, dst_ref=blk, send_sem=send.at[i, k], recv_sem=recv.at[i, k],
                                             device_id=sibling, device_id_type=MESH).wait_recv()
        for d in cps:
            d.wait_send()

    return _call(
        body, name=name, in_specs=[ANY] * n, out_specs=[ANY] * n,
        out_shape=[jax.ShapeDtypeStruct(l.shape, l.dtype) for l in lands],
        input_output_aliases={j: j for j in range(n)},
        scratch_shapes=[pltpu.SemaphoreType.DMA((n, 3)), pltpu.SemaphoreType.DMA((n, 3))],
    )(*lands)


def _own_slot(lands, owns):
    me = 2 * lax.axis_index("x") + lax.axis_index("y")
    return [lax.dynamic_update_slice_in_dim(g, s, me, axis=0) for g, s in zip(lands, owns)]


def _sibling_send_halves(grads, by_cols, *, name):
    n = len(grads)

    def body(*refs):
        ins, outs = refs[:n], refs[n:2 * n]
        send, recv = refs[2 * n:]
        x, y, c, _ = _where_am_i()
        sibling = (x, y, 1 - c)
        cps = []
        for i in range(n):
            src = ins[i].at[(slice(None),) + _half_idx(*grads[i].shape[1:], by_cols[i], 1 - c)]
            d = pltpu.make_async_remote_copy(src_ref=src, dst_ref=outs[i], send_sem=send.at[i],
                                             recv_sem=recv.at[i], device_id=sibling, device_id_type=MESH)
            d.start()
            cps.append(d)
        for d in cps:
            d.wait()

    return _call(
        body, name=name, in_specs=[ANY] * n, out_specs=[ANY] * n,
        out_shape=[jax.ShapeDtypeStruct((N_CHIPS,) + _half_shape(*g.shape[1:], bc), g.dtype)
                   for g, bc in zip(grads, by_cols)],
        scratch_shapes=[pltpu.SemaphoreType.DMA((n,)), pltpu.SemaphoreType.DMA((n,))],
    )(*grads)


def _all_reduce_small(v, *, name, after=()):
    R, C = v.shape
    H = R // 2

    def body(v_ref, o_ref, sib, slots, send, recv):
        x, y, c, chips = _where_am_i()
        me = 2 * x + y
        sibling = (x, y, 1 - c)
        mine = pl.ds(pl.multiple_of(c * H, 8), H)
        other = pl.ds(pl.multiple_of((1 - c) * H, 8), H)

        def copy(k, src, dst, to):
            return pltpu.make_async_remote_copy(src_ref=src, dst_ref=dst, send_sem=send.at[k], recv_sem=recv.at[k],
                                                device_id=to, device_id_type=MESH)

        d = copy(0, v_ref.at[other], sib, sibling)
        d.start()
        d.wait()
        slots[me] = v_ref[mine, :] + sib[...]
        cps = [copy(1 + k, slots.at[me], slots.at[me], (px, py, c)) for k, (px, py) in enumerate(chips)]
        for d in cps:
            d.start()
        for k, (px, py) in enumerate(chips):
            blk = slots.at[2 * px + py]
            copy(1 + k, blk, blk, (px, py, c)).wait_recv()
        for d in cps:
            d.wait_send()
        o_ref[mine, :] = (slots[0] + slots[1]) + (slots[2] + slots[3])
        d = copy(4, o_ref.at[mine], o_ref.at[mine], sibling)
        d.start()
        copy(4, o_ref.at[other], o_ref.at[other], sibling).wait_recv()
        d.wait_send()

    vm = pl.BlockSpec(memory_space=pltpu.VMEM)
    return _call(
        body, after=after, name=name, in_specs=[vm], out_specs=vm,
        out_shape=jax.ShapeDtypeStruct((R, C), F32),
        scratch_shapes=[pltpu.VMEM((H, C), F32), pltpu.VMEM((N_CHIPS, H, C), F32),
                        pltpu.SemaphoreType.DMA((5,)), pltpu.SemaphoreType.DMA((5,))],
        compiler_params=pltpu.CompilerParams(vmem_limit_bytes=VMEM_LIMIT),
    )(v)


def _cols_from_shards(g):
    return jnp.transpose(g, (1, 0, 2)).reshape(g.shape[1], -1)


def _shards_from_cols(w):
    R, C4 = w.shape
    return jnp.transpose(w.reshape(R, N_CHIPS, C4 // N_CHIPS), (1, 0, 2))


def _pack(arrs):
    flat = []
    for a in arrs:
        f = a.reshape(-1).astype(F32)
        flat.append(jnp.pad(f, (0, _rup(f.shape[0], LANE) - f.shape[0])))
    v = jnp.concatenate(flat)
    rows = _rup(v.shape[0] // LANE, 16)
    v = jnp.pad(v, (0, rows * LANE - v.shape[0]))
    return v.reshape(rows, LANE)


def _unpack(v, shapes):
    flat = v.reshape(-1)
    out, off = [], 0
    for s in shapes:
        n = int(np.prod(s))
        out.append(flat[off:off + n].reshape(s))
        off += _rup(n, LANE)
    return out


def _ffn_fwd(x, Wup, Wdn, cw, cb, tag):
    h = _mm(x, Wup, 'nt', bmode='bo', tm=512, tn=4096, name=f"ffn_up_{tag}")
    a, hc = _act_fwd(h, cw, cb, name=f"ffn_act_{tag}")
    f = _mm(a, Wdn, 'nn', bmode='abr', tm=512, tn=1024, tk=4096, name=f"ffn_down_{tag}")
    return f, (h, hc), a


def _ffn_bwd(df, x, saved, a, Wup, Wdn, cw, tag):
    h, hc = saved
    da = _mm(df, Wdn, 'nt', bmode='bo', tm=512, tn=4096, name=f"ffn_da_{tag}")
    dWdn = _mm(a, df, 'tn', bmode='ao', tm=4096, tn=512, name=f"ffn_dwdn_{tag}", out_dtype=BF16)
    dh, dcw, dcb = _act_bwd(h, hc, da, cw, name=f"ffn_actb_{tag}")

    def shard_of(k):
        return (k % 2) * 2 + k // 2

    dx = _mm(dh, Wup, 'nn', bmode='abr', tm=512, tn=1024, tk=4096, name=f"ffn_dx_{tag}", b_map=shard_of)
    dWup = _mm(dh, x, 'tn', bmode='ao', tm=4096, tn=512, name=f"ffn_dwup_{tag}", out_dtype=BF16,
               o_map=shard_of)
    return dx, dWup, dWdn, dcw, dcb


def kernel(x, positions, ev_w_in, ev_b_f, ev_lambda_re, ev_lambda_im, ev_log_step, ev_ssm_b_re, ev_ssm_b_im, ev_ssm_c_re, ev_ssm_c_im, ev_ssm_d, ev_w_glu, ev_w_out, od_w_in, od_sinks, od_w_out, ln_mix_g, ln_mix_b, ffn_w_up, ffn_conv_w, ffn_conv_b, ffn_w_down, ln_ffn_g, ln_ffn_b, loss_target, m_ev_w_in, m_ev_b_f, m_ev_lambda_re, m_ev_lambda_im, m_ev_log_step, m_ev_ssm_b_re, m_ev_ssm_b_im, m_ev_ssm_c_re, m_ev_ssm_c_im, m_ev_ssm_d, m_ev_w_glu, m_ev_w_out, m_od_w_in, m_od_sinks, m_od_w_out, m_ln_mix_g, m_ln_mix_b, m_ffn_w_up, m_ffn_conv_w, m_ffn_conv_b, m_ffn_w_down, m_ln_ffn_g, m_ln_ffn_b, v_ev_w_in, v_ev_b_f, v_ev_lambda_re, v_ev_lambda_im, v_ev_log_step, v_ev_ssm_b_re, v_ev_ssm_b_im, v_ev_ssm_c_re, v_ev_ssm_c_im, v_ev_ssm_d, v_ev_w_glu, v_ev_w_out, v_od_w_in, v_od_sinks, v_od_w_out, v_ln_mix_g, v_ln_mix_b, v_ffn_w_up, v_ffn_conv_w, v_ffn_conv_b, v_ffn_w_down, v_ln_ffn_g, v_ln_ffn_b):
    W = dict(ev_w_in=ev_w_in, ev_b_f=ev_b_f, ev_lambda_re=ev_lambda_re, ev_lambda_im=ev_lambda_im, ev_log_step=ev_log_step, ev_ssm_b_re=ev_ssm_b_re, ev_ssm_b_im=ev_ssm_b_im, ev_ssm_c_re=ev_ssm_c_re, ev_ssm_c_im=ev_ssm_c_im, ev_ssm_d=ev_ssm_d, ev_w_glu=ev_w_glu, ev_w_out=ev_w_out, od_w_in=od_w_in, od_sinks=od_sinks, od_w_out=od_w_out, ln_mix_g=ln_mix_g, ln_mix_b=ln_mix_b, ffn_w_up=ffn_w_up, ffn_conv_w=ffn_conv_w, ffn_conv_b=ffn_conv_b, ffn_w_down=ffn_w_down, ln_ffn_g=ln_ffn_g, ln_ffn_b=ln_ffn_b)
    Mo = dict(ev_w_in=m_ev_w_in, ev_b_f=m_ev_b_f, ev_lambda_re=m_ev_lambda_re, ev_lambda_im=m_ev_lambda_im, ev_log_step=m_ev_log_step, ev_ssm_b_re=m_ev_ssm_b_re, ev_ssm_b_im=m_ev_ssm_b_im, ev_ssm_c_re=m_ev_ssm_c_re, ev_ssm_c_im=m_ev_ssm_c_im, ev_ssm_d=m_ev_ssm_d, ev_w_glu=m_ev_w_glu, ev_w_out=m_ev_w_out, od_w_in=m_od_w_in, od_sinks=m_od_sinks, od_w_out=m_od_w_out, ln_mix_g=m_ln_mix_g, ln_mix_b=m_ln_mix_b, ffn_w_up=m_ffn_w_up, ffn_conv_w=m_ffn_conv_w, ffn_conv_b=m_ffn_conv_b, ffn_w_down=m_ffn_w_down, ln_ffn_g=m_ln_ffn_g, ln_ffn_b=m_ln_ffn_b)
    Vo = dict(ev_w_in=v_ev_w_in, ev_b_f=v_ev_b_f, ev_lambda_re=v_ev_lambda_re, ev_lambda_im=v_ev_lambda_im, ev_log_step=v_ev_log_step, ev_ssm_b_re=v_ev_ssm_b_re, ev_ssm_b_im=v_ev_ssm_b_im, ev_ssm_c_re=v_ev_ssm_c_re, ev_ssm_c_im=v_ev_ssm_c_im, ev_ssm_d=v_ev_ssm_d, ev_w_glu=v_ev_w_glu, ev_w_out=v_ev_w_out, od_w_in=v_od_w_in, od_sinks=v_od_sinks, od_w_out=v_od_w_out, ln_mix_g=v_ln_mix_g, ln_mix_b=v_ln_mix_b, ffn_w_up=v_ffn_w_up, ffn_conv_w=v_ffn_conv_w, ffn_conv_b=v_ffn_conv_b, ffn_w_down=v_ffn_w_down, ln_ffn_g=v_ln_ffn_g, ln_ffn_b=v_ln_ffn_b)
    names = list(W.keys())
    big = ['ev_w_in', 'ev_w_glu', 'ev_w_out', 'od_w_in', 'od_w_out', 'ffn_w_up', 'ffn_w_down']

    S, D = x.shape[1], x.shape[2]
    x0 = x.reshape(S, D)
    tgt = loss_target.reshape(S, D)
    G, Pn, Cg = SSM_GROUPS, SSM_STATE, SSM_GROUP
    Fs = ffn_w_up.shape[2]
    FP = Fs
    Rd = ffn_w_down.shape[1]
    EIN = N_CHIPS * ev_w_in.shape[2]

    cwl = ffn_conv_w.reshape(-1)
    cw_rows = _rup(_rup(cwl.shape[0], LANE) // LANE, 32)
    cw_pad = jnp.pad(cwl, (0, cw_rows * LANE - cwl.shape[0])).reshape(cw_rows, LANE)
    transposed = ('ev_w_in', 'ffn_w_up')

    def view(n, a):
        return jnp.transpose(a, (0, 2, 1)) if n in transposed else a

    Wv = {n: view(n, W[n]) for n in big}
    big_e = [(n, l) for n in big for l in range(W[n].shape[0])]
    split_cols = {e: (Wv[e[0]].shape[1] // 2) % 16 != 0 for e in big_e}
    shard16 = {e: Wv[e[0]][e[1]].astype(BF16) for e in big_e}
    grp_now = [e for e in big_e if e[0].startswith('ev_')]
    grp_ffn0 = [('ffn_w_up', 0), ('ffn_w_down', 0)]
    grp_l1 = [('od_w_in', 0), ('od_w_out', 0), ('ffn_w_up', 1), ('ffn_w_down', 1)]
    src_now = [shard16[e] for e in grp_now]
    src_ffn0 = [shard16[e] for e in grp_ffn0] + [cw_pad]
    src_l1 = [shard16[e] for e in grp_l1]
    cols_now = [split_cols[e] for e in grp_now]
    cols_ffn0 = [split_cols[e] for e in grp_ffn0] + [False]
    cols_l1 = [split_cols[e] for e in grp_l1]
    ag_in = _chip_exchange_start('gather', src_now[:1], cols_now[:1], name="ag_in_start")
    ag_mix = _chip_exchange_start('gather', src_now[1:], cols_now[1:], name="ag_mix_start", after=[ag_in[4]])
    ag_ffn0 = _chip_exchange_start('gather', src_ffn0, cols_ffn0, name="ag_ffn0_start", after=[ag_mix[4]])
    ag_l1 = _chip_exchange_start('gather', src_l1, cols_l1, name="ag_l1_start", after=[ag_ffn0[4]])
    started = [ag_l1[4]]

    def finish_gather(started, srcs, cols, after, tag):
        send, recv, thru, lands, _ = started
        thru, lands = _chip_exchange_wait('gather', send, recv, thru, lands, cols, after, name=f"ag_{tag}_wait")
        lands = _sibling_pass_gathered(lands, [s.shape for s in srcs], cols, name=f"ag_{tag}_pass")
        return _own_slot(lands, [s[None] for s in thru])

    lam_r, lam_i = ev_lambda_re[0], ev_lambda_im[0]
    lstep = ev_log_step[0].reshape(G, 1)
    a_re, a_im, g_re, g_im = _s5_disc_fwd(lam_r, lam_i, lstep, name="s5_disc", after=started)
    b_re2, b_im2 = ev_ssm_b_re[0].reshape(G * Pn, Cg), ev_ssm_b_im[0].reshape(G * Pn, Cg)
    g_re1, g_im1 = g_re.reshape(G * Pn, 1), g_im.reshape(G * Pn, 1)
    bb_re, bb_im = _s5_bb_fwd(g_re1, g_im1, b_re2, b_im2, name="s5_bb")
    bbt = jnp.stack([jnp.transpose(b.reshape(G, Pn, Cg), (0, 2, 1)).reshape(G * Cg, Pn) for b in (bb_re, bb_im)])
    BB = _diag_expand(bbt, Cg, Pn, name="s5_bb_dense")
    cct = jnp.stack([jnp.transpose(ev_ssm_c_re[0], (0, 2, 1)).reshape(G * Pn, Cg),
                     jnp.transpose(-ev_ssm_c_im[0], (0, 2, 1)).reshape(G * Pn, Cg)])
    CC = _diag_expand(cct, Pn, Cg, name="s5_cc_dense", after=started)
    a_cat = jnp.stack([a_re.reshape(1, G * Pn), a_im.reshape(1, G * Pn)])
    dskip = ev_ssm_d[0].reshape(1, SSM_WIDTH)
    tabs = _rope_tables(positions.reshape(S, 1).astype(F32), name="rope_tables", after=[BB, CC])

    gw = dict(zip(grp_now[:1], finish_gather(ag_in, src_now[:1], cols_now[:1], [tabs[2]], "in")))
    w_in_t = gw[('ev_w_in', 0)].reshape(EIN, D)
    qkv_w = 3 * FOX_WIDTH
    WmainT = jnp.concatenate([w_in_t[:qkv_w], w_in_t[qkv_w + FOX_HEADS:]], axis=0)
    WfT = jnp.pad(w_in_t[qkv_w:qkv_w + FOX_HEADS], ((0, LANE - FOX_HEADS), (0, 0)))
    cbs = [ffn_conv_b[l].reshape(N_CHIPS, Fs) for l in range(DEPTH)]

    P = _mm(x0, WmainT, 'nt', name="ev_proj")
    fl = _mm(x0, WfT, 'nt', name="ev_proj_f")
    bf_pad = jnp.pad(ev_b_f.reshape(1, FOX_HEADS), ((0, 0), (0, LANE - FOX_HEADS)))
    cgate, sgate = _gate_fwd(fl, bf_pad, name="fox_gate")
    ccol = jnp.transpose(cgate[:, :FOX_HEADS]).reshape(FOX_HEADS, S, 1)
    crow = jnp.transpose(cgate[:, :FOX_HEADS]).reshape(FOX_HEADS, 1, S)
    fox, lse = _fox_fwd(P, ccol, crow, name="fox_fwd")
    u_s5 = P[:, qkv_w:]
    UT, HT = _DIAG_TILE * Cg, _DIAG_TILE * Pn
    bu = _mm(u_s5, BB, 'nn', bmode='bo', tm=2048, tn=HT, tk=UT, diag='kn', name="s5_bu")
    hh = _s5_scan_fwd(bu, a_cat, name="s5_scan")
    yc = _mm(hh, CC, 'nn', bmode='abr', tm=2048, tn=UT, tk=HT, diag='kn', name="s5_y")
    y_s5, yg = _s5_out_fwd(yc, P, dskip, name="s5_out")
    gw.update(zip(grp_now[1:], finish_gather(ag_mix, src_now[1:], cols_now[1:], [yg], "mix")))
    Wglu = _cols_from_shards(gw[('ev_w_glu', 0)])
    Wout_ev = gw[('ev_w_out', 0)].reshape(D, D)
    z = _mm(yg, Wglu, 'nn', name="s5_glu_proj")
    cat = _glu_fwd(z, fox, name="s5_glu")
    mix0 = _mm(cat, Wout_ev, 'nn', name="ev_out")
    x1, xh1, rs1 = _add_ln_fwd(x0, mix0, ln_mix_g[0], ln_mix_b[0], name="ln_mix0")
    got = finish_gather(ag_ffn0, src_ffn0, cols_ffn0, [x1], "ffn0")
    gw.update(zip(grp_ffn0, got[:-1]))
    cw_all = got[-1].reshape(N_CHIPS, -1)[:, :cwl.shape[0]].reshape(N_CHIPS, DEPTH, 3, Fs)
    cws = [cw_all[:, l] for l in range(DEPTH)]
    Wup = {0: gw[('ffn_w_up', 0)]}
    Wdn = {0: gw[('ffn_w_down', 0)].reshape(2, Fs, D)}
    f0, hf0, af0 = _ffn_fwd(x1, Wup[0], Wdn[0], cws[0], cbs[0], "l0")
    x2, xh2, rs2 = _add_ln_fwd(x1, f0, ln_ffn_g[0], ln_ffn_b[0], name="ln_ffn0")

    gw.update(zip(grp_l1, finish_gather(ag_l1, src_l1, cols_l1, [x2], "l1")))
    Wodin = _cols_from_shards(gw[('od_w_in', 0)])
    Wodout = gw[('od_w_out', 0)].reshape(D, D)
    Wup[1] = gw[('ffn_w_up', 1)]
    Wdn[1] = gw[('ffn_w_down', 1)].reshape(2, Fs, D)
    QW, KW = SWA_HEADS * SWA_HEAD_DIM, SWA_KV_HEADS * SWA_HEAD_DIM
    P1 = _mm(x2, Wodin, 'nn', name="od_proj")
    qT = _to_heads(P1, tabs, col0=0, width=QW, rotate=True, name="rope_q", out_dtype=BF16)
    kT = _to_heads(P1, tabs, col0=QW, width=KW, rotate=True, name="rope_k", out_dtype=BF16)
    vT = _to_heads(P1, tabs, col0=QW + KW, width=KW, rotate=False, name="heads_v", out_dtype=BF16)
    sink_rows = jnp.broadcast_to(od_sinks[0].reshape(SWA_KV_HEADS, SWA_GROUPS, 1, 1),
                                 (SWA_KV_HEADS, SWA_GROUPS, SWA_WINDOW, 1)).reshape(SWA_KV_HEADS, -1, 1)
    oT, Lsw = _swa_fwd(qT, kT, vT, sink_rows, name="swa_fwd")
    o_sw = _from_heads(oT, tabs, rotate_back=False, name="heads_o", out_dtype=BF16)
    mix1 = _mm(o_sw, Wodout, 'nn', name="od_out")
    x3, xh3, rs3 = _add_ln_fwd(x2, mix1, ln_mix_g[1], ln_mix_b[1], name="ln_mix1")
    f1, hf1, af1 = _ffn_fwd(x3, Wup[1], Wdn[1], cws[1], cbs[1], "l1")
    _, xh4, rs4 = _add_ln_fwd(x3, f1, ln_ffn_g[1], ln_ffn_b[1], name="ln_ffn1")

    dz4, dg_ffn1, db_ffn1, loss_part = _loss_ln_bwd(tgt, xh4, rs4, ln_ffn_g[1], ln_ffn_b[1], name="loss_lnb_ffn1")
    dx3f, dWup1, dWdn1, dcw1, dcb1 = _ffn_bwd(dz4, x3, hf1, af1, Wup[1], Wdn[1], cws[1], "l1")
    sib_ffn1 = _sibling_halves_start([dWup1, dWdn1.reshape(N_CHIPS, Rd, D)], [False, False], name="rs_ffn1_sib_start")
    dz3, dg_mix1, db_mix1 = _ln_bwd(dz4, dx3f, xh3, rs3, ln_mix_g[1], name="lnb_mix1", after=[sib_ffn1[4]])
    do_sw = _mm(dz3, Wodout, 'nt', name="od_out_dx")
    dWodout = _mm(o_sw, dz3, 'tn', name="od_out_dw", out_dtype=BF16)
    doT = _to_heads(do_sw, tabs, col0=0, width=QW, rotate=False, name="heads_do", out_dtype=F32)
    dqT, dkT, dvT, dsink = _swa_bwd(qT, kT, vT, sink_rows, oT, Lsw, doT, name="swa_bwd")
    dq1 = _from_heads(dqT, tabs, rotate_back=True, name="rope_dq", out_dtype=BF16)
    dk1 = _from_heads(dkT, tabs, rotate_back=True, name="rope_dk", out_dtype=BF16, skip_rows=SWA_WINDOW)
    dv1 = _from_heads(dvT, tabs, rotate_back=False, name="heads_dv", out_dtype=BF16, skip_rows=SWA_WINDOW)
    dP1 = jnp.concatenate([dq1, dk1, dv1], axis=1)
    dx2m = _mm(dP1, Wodin, 'nt', name="od_proj_dx")
    dWodin = _mm(x2, dP1, 'tn', name="od_proj_dw", out_dtype=BF16)

    def rs_begin(entries, grads, tag):
        cols = [split_cols[e] for e in entries]
        sib = _sibling_send_halves(grads, cols, name=f"rs_{tag}_sibling")
        return [_sum2_halves(g4, s4, bc, name=f"rs_sum2_{n}{l}")
                for (n, l), g4, s4, bc in zip(entries, grads, sib, cols)]

    def rs_begin_started(entries, started, after, tag):
        send, rcv, thru, lands, _ = started
        thru, lands = _sibling_halves_wait(send, rcv, thru, lands, [False] * len(thru), after,
                                           name=f"rs_{tag}_sib_wait")
        return [_sum2_halves(g4, s4, False, name=f"rs_sum2_{n}{l}") for (n, l), g4, s4 in zip(entries, thru, lands)]

    def own_parts(parts):
        me = 2 * lax.axis_index("x") + lax.axis_index("y")
        return [lax.dynamic_slice_in_dim(p, me, 1, axis=0) for p in parts]

    part_l1 = (rs_begin(grp_l1[:2], [_shards_from_cols(dWodin), dWodout.reshape(N_CHIPS, D // N_CHIPS, D)], "od")
               + rs_begin_started(grp_l1[2:], sib_ffn1, [dWodin], "ffn1"))
    rs_l1 = _chip_exchange_start('scatter', part_l1, [False] * len(part_l1), name="rs_l1_start")

    dz2, dg_ffn0, db_ffn0 = _ln_bwd(dz3, dx2m, xh2, rs2, ln_ffn_g[0], name="lnb_ffn0", after=[rs_l1[4]])
    dx1f, dWup0, dWdn0, dcw0, dcb0 = _ffn_bwd(dz2, x1, hf0, af0, Wup[0], Wdn[0], cws[0], "l0")
    sib_ffn0 = _sibling_halves_start([dWup0, dWdn0.reshape(N_CHIPS, Rd, D)], [False, False], name="rs_ffn0_sib_start")
    dz1, dg_mix0, db_mix0 = _ln_bwd(dz2, dx1f, xh1, rs1, ln_mix_g[0], name="lnb_mix0", after=[sib_ffn0[4]])
    dcat = _mm(dz1, Wout_ev, 'nt', name="ev_out_dx")
    dWout_ev = _mm(cat, dz1, 'tn', name="ev_out_dw", out_dtype=BF16)
    part_ffn0 = rs_begin_started(grp_ffn0, sib_ffn0, [dWout_ev], "ffn0")
    rs_ffn0 = _chip_exchange_start('scatter', part_ffn0, [False] * len(part_ffn0), name="rs_ffn0_start")
    dz = _glu_bwd(z, dcat, name="s5_glu_bwd")
    dyg = _mm(dz, Wglu, 'nt', name="s5_glu_dx", after=[rs_ffn0[4]])
    dWglu = _mm(yg, dz, 'tn', name="s5_glu_dw", out_dtype=BF16)
    dy_s5, du_dir, dD = _s5_out_bwd(dyg, y_s5, P, dskip, name="s5_out_bwd")
    dhh = _mm(dy_s5, CC, 'nt', bmode='bo', tm=2048, tn=HT, tk=UT, diag='kn', name="s5_y_dx")
    dCC = _mm(hh, dy_s5, 'tn', bmode='ao', tm=HT, tn=UT, diag='mn', name="s5_y_dw")
    lam, da_s5 = _s5_scan_bwd(dhh, hh, a_cat, name="s5_scan_bwd")
    du = _mm(lam, BB, 'nt', bmode='abr', tm=2048, tn=UT, tk=HT, diag='kn', name="s5_bu_dx", plus=[(du_dir, 1.0)],
             out_dtype=BF16)
    dBB = _mm(u_s5, lam, 'tn', bmode='bo', tm=UT, tn=HT, diag='mn', name="s5_bu_dw")
    dq0, dk0, dv0, dccol, dcrow = _fox_bwd(P, ccol, crow, fox, lse, dcat, name="fox_bwd")
    dc = jnp.transpose((dccol.reshape(FOX_HEADS, S) - dcrow.reshape(FOX_HEADS, S)))
    dc = jnp.pad(dc, ((0, 0), (0, LANE - FOX_HEADS)))
    dfl, dbf = _gate_bwd(dc, sgate, name="fox_gate_bwd")
    dP = jnp.concatenate([dq0, dk0, dv0, du], axis=1)
    dx0b = _mm(dfl, WfT, 'nn', name="ev_proj_f_dx")
    grad_x = _mm(dP, WmainT, 'nn', name="ev_proj_dx", plus=[(dz1, ALPHA), (dx0b, 1.0)])
    dWmainT = _mm(dP, x0, 'tn', tm=1024, tn=1024, name="ev_proj_dw", out_dtype=BF16)
    dWfT = _mm(dfl, x0, 'tn', name="ev_proj_f_dw", out_dtype=BF16)

    dbbt = _diag_extract(dBB, Cg, Pn, name="s5_bb_diag")
    dcct = _diag_extract(dCC, Pn, Cg, name="s5_cc_diag")
    dbb_re = jnp.transpose(dbbt[0].reshape(G, Cg, Pn), (0, 2, 1)).reshape(G * Pn, Cg)
    dbb_im = jnp.transpose(dbbt[1].reshape(G, Cg, Pn), (0, 2, 1)).reshape(G * Pn, Cg)
    db_re, db_im, dg_re1, dg_im1 = _s5_bb_bwd(g_re1, g_im1, b_re2, b_im2, dbb_re, dbb_im, name="s5_bb_bwd")
    dlam_re, dlam_im, dlstep = _s5_disc_bwd(lam_r, lam_i, lstep, da_s5[0].reshape(G, Pn), da_s5[1].reshape(G, Pn),
                                            dg_re1.reshape(G, Pn), dg_im1.reshape(G, Pn), name="s5_disc_bwd")
    dc_re = jnp.transpose(dcct[0].reshape(G, Pn, Cg), (0, 2, 1))
    dc_im = -jnp.transpose(dcct[1].reshape(G, Pn, Cg), (0, 2, 1))

    def conv_w_full(d0, d1):
        return jnp.stack([jnp.reshape(jnp.transpose(d[:, :, :Fs], (1, 0, 2)), (3, N_CHIPS * Fs)) for d in (d0, d1)])

    def conv_b_full(d0, d1):
        return jnp.stack([jnp.reshape(d[:, 0, :Fs], (N_CHIPS * Fs,)) for d in (d0, d1)])

    small_local = dict(
        ev_b_f=dbf[:, :FOX_HEADS], ev_lambda_re=dlam_re, ev_lambda_im=dlam_im, ev_log_step=dlstep,
        ev_ssm_b_re=db_re, ev_ssm_b_im=db_im, ev_ssm_c_re=dc_re, ev_ssm_c_im=dc_im, ev_ssm_d=dD,
        od_sinks=dsink[:, :, 0],
        ln_mix_g=jnp.concatenate([dg_mix0, dg_mix1]), ln_mix_b=jnp.concatenate([db_mix0, db_mix1]),
        ffn_conv_w=conv_w_full(dcw0, dcw1), ffn_conv_b=conv_b_full(dcb0, dcb1),
        ln_ffn_g=jnp.concatenate([dg_ffn0, dg_ffn1]), ln_ffn_b=jnp.concatenate([db_ffn0, db_ffn1]))
    small = list(small_local.keys())
    out_g, out_d, out_m, out_v = {}, {}, {}, {}
    loss_out = []

    def small_update(after):
        red = _all_reduce_small(_pack([small_local[n] for n in small] + [loss_part]), name="ar_small", after=after)
        full_shapes = [W[n].shape if n != 'ffn_conv_w' else (DEPTH, 3, N_CHIPS * Fs) for n in small]
        pieces = _unpack(red, full_shapes + [()])
        loss_out.append(pieces[-1])
        gsmall = dict(zip(small, pieces[:-1]))
        chip = 2 * lax.axis_index("x") + lax.axis_index("y")
        gsmall['ffn_conv_w'] = lax.dynamic_slice_in_dim(gsmall['ffn_conv_w'], chip * Fs, Fs, axis=2)
        shapes = [W[n].shape for n in small]
        gs, ds_, ms, vs = _adamw(_pack([W[n] for n in small])[None], _pack([gsmall[n] for n in small])[None],
                                 _pack([Mo[n] for n in small])[None], _pack([Vo[n] for n in small])[None],
                                 name="adamw_small", tr=1 << 14)
        out_g.update(zip(small, _unpack(gs, shapes)))
        out_d.update(zip(small, _unpack(ds_, shapes)))
        out_m.update(zip(small, _unpack(ms, shapes)))
        out_v.update(zip(small, _unpack(vs, shapes)))
        return vs

    dw_in_t = jnp.concatenate([dWmainT[:qkv_w], dWfT[:FOX_HEADS], dWmainT[qkv_w:]], axis=0)
    part_now = rs_begin(grp_now, [dw_in_t.reshape(N_CHIPS, EIN // N_CHIPS, D), _shards_from_cols(dWglu),
                                  dWout_ev.reshape(N_CHIPS, D // N_CHIPS, D)], "l0")
    small_done = small_update([grad_x])
    rs_now = _chip_exchange_start('scatter', part_now, [False] * len(part_now), name="rs_l0_start",
                                  after=[small_done])

    def finish_scatter(started, parts, after, tag):
        send, rcv, thru, lands, _ = started
        thru, lands = _chip_exchange_wait('scatter', send, rcv, thru, lands, [False] * len(parts), after,
                                          name=f"rs_{tag}_wait")
        return _own_slot(lands, own_parts(thru))

    def update(entries, recv, tag):
        halves = [_rowsum(r, name=f"rs_sum4_{e[0]}{e[1]}") for e, r in zip(entries, recv)]
        send, rcv, thru, lands, tok = _sibling_swap_start(halves, name=f"rs_{tag}_join_start")
        own = dict(zip(entries, thru))
        params = list(dict.fromkeys(e[0] for e in entries))

        def half_update(n, grads, is_own, prev, after_name):
            return _adamw_half(Wv[n], [grads[(n, l)] for l in range(W[n].shape[0])], view(n, Mo[n]), view(n, Vo[n]),
                               name=f"adamw_{after_name}_{n}", own=is_own, prev=prev, by_cols=split_cols[(n, 0)])

        first = {n: half_update(n, own, True, None, "own") for n in params}
        _, others = _sibling_swap_wait(send, rcv, thru, lands, [first[n][3] for n in params] + [tok],
                                       name=f"rs_{tag}_join_wait")
        oth = dict(zip(entries, others))
        done = []
        for n in params:
            res = half_update(n, oth, False, first[n], "sib")
            out_g[n], out_d[n], out_m[n], out_v[n] = (view(n, t) for t in res)
            done.append(res[3])
        return done

    recv_rest = (finish_scatter(rs_l1, part_l1, [rs_now[4]], "l1")
                 + finish_scatter(rs_ffn0, part_ffn0, [rs_now[4]], "ffn0"))
    done = update(grp_l1 + grp_ffn0, recv_rest, "rest")
    update(grp_now, finish_scatter(rs_now, part_now, done, "l0"), "l0")
    loss = loss_out[0]

    return (loss, grad_x.reshape(1, S, D), *[out_g[n] for n in names], *[out_d[n] for n in names],
            *[out_m[n] for n in names], *[out_v[n] for n in names])
```
